```python
import jax, jax.numpy as jnp
from jax import lax
import numpy as np

D_MODEL = 2048
BATCH = 8
SEQ = 4096
DEPTH = 1

A_HEADS = 8
A_KEY_DIM = 128
A_VAL_DIM = 128
A_KEY_WIDTH = A_HEADS * A_KEY_DIM
A_WIDTH = A_HEADS * A_VAL_DIM
A_CHUNK = 32
B_GROUPS = 8
B_GROUP_DIM = 128
B_WIDTH = B_GROUPS * B_GROUP_DIM
B_CHUNK = 128
D_FF = 4 * D_MODEL
N_MOD = 6
EPS = 1e-6

IN_SIZES = (A_KEY_WIDTH, A_KEY_WIDTH, A_KEY_WIDTH, A_WIDTH, A_WIDTH, 2 * B_WIDTH, D_MODEL, D_MODEL)
IN_WIDTH = sum(IN_SIZES)
IN_SPLIT_POINTS = tuple(int(s) for s in np.cumsum(IN_SIZES)[:-1])

kernel_name = 'hybrid_hgrn2_sgu_block'


def rms_norm(x, g):
    xf = x.astype(jnp.float32)
    y = xf * lax.rsqrt(jnp.mean(xf * xf, axis=-1, keepdims=True) + EPS)
    return (y * g.astype(jnp.float32)).astype(x.dtype)


def layer_norm(x, g):
    xf = x.astype(jnp.float32)
    mu = jnp.mean(xf, axis=-1, keepdims=True)
    d = xf - mu
    y = d * lax.rsqrt(jnp.mean(d * d, axis=-1, keepdims=True) + EPS)
    return (y * g.astype(jnp.float32)).astype(x.dtype)


def to_heads(t, d):
    b, l, _ = t.shape
    return t.reshape(b, l, A_HEADS, d).transpose(0, 2, 1, 3)


def gated_state_scan(q, k, v, log_f):
    bn, h, l, dk = q.shape
    dv = v.shape[-1]
    n = l // A_CHUNK
    rs = lambda t: t.reshape(bn, h, n, A_CHUNK, t.shape[-1])
    q, k, v, log_f = rs(q), rs(k), rs(v), rs(log_f)
    b = jnp.cumsum(log_f, axis=3)
    b_last = b[:, :, :, -1:, :]
    q_dec = q * jnp.exp(b)
    k_dec = k * jnp.exp(-b)
    k_end = k * jnp.exp(b_last - b)
    mask = jnp.tril(jnp.ones((A_CHUNK, A_CHUNK), dtype=bool))
    att = jnp.einsum('bhnck,bhnsk->bhncs', q_dec, k_dec)
    att = jnp.where(mask, att, 0.0)
    o_intra = jnp.einsum('bhncs,bhnsv->bhncv', att, v)
    decay = jnp.exp(b_last[:, :, :, 0, :])

    def step(state, xs):
        qd, ke, vc, dc = xs
        o = jnp.einsum('bhck,bhkv->bhcv', qd, state)
        state = dc[..., None] * state + jnp.einsum('bhck,bhcv->bhkv', ke, vc)
        return state, o

    xs = (jnp.moveaxis(q_dec, 2, 0), jnp.moveaxis(k_end, 2, 0),
          jnp.moveaxis(v, 2, 0), jnp.moveaxis(decay, 2, 0))
    s0 = jnp.zeros((bn, h, dk, dv), jnp.float32)
    _, o_inter = lax.scan(step, s0, xs)
    o = o_intra + jnp.moveaxis(o_inter, 0, 2)
    return o.reshape(bn, h, l, dv)


def hgrn2_forget(logit, lb):
    lf = logit.astype(jnp.float32)
    log_f = jnp.log(lb + (1.0 - lb) * jax.nn.sigmoid(lf))
    k = (1.0 - lb) * jax.nn.sigmoid(-lf)
    return k, log_f


def hgrn2_bidir(q, f_fw, f_bw, i_v, o_gate, lb_fw, lb_bw, g_norm):
    bn, l, _ = q.shape
    k_fw, lf_fw = hgrn2_forget(f_fw, lb_fw)
    k_bw, lf_bw = hgrn2_forget(f_bw, lb_bw)
    qh = to_heads(q.astype(jnp.float32) * (A_KEY_DIM ** -0.5), A_KEY_DIM)
    vh = to_heads(i_v.astype(jnp.float32), A_VAL_DIM)
    o_fw = gated_state_scan(qh, to_heads(k_fw, A_KEY_DIM), vh, to_heads(lf_fw, A_KEY_DIM))
    fl = lambda t: jnp.flip(t, axis=2)
    o_bw = fl(gated_state_scan(fl(qh), fl(to_heads(k_bw, A_KEY_DIM)), fl(vh),
                               fl(to_heads(lf_bw, A_KEY_DIM))))
    o = rms_norm(o_fw + o_bw, g_norm)
    o = o.transpose(0, 2, 1, 3).reshape(bn, l, A_WIDTH).astype(o_gate.dtype)
    return o * jax.nn.silu(o_gate)


def chunked_sgu(z, g_v, w_s, b_s):
    bn, l, _ = z.shape
    z = jax.nn.gelu(z, approximate=False)
    u, v = jnp.split(z, 2, axis=-1)
    v = layer_norm(v, g_v)
    v = v.reshape(bn, l // B_CHUNK, B_CHUNK, B_GROUPS, B_GROUP_DIM)
    vm = jnp.einsum('gts,bnsgc->bntgc', w_s, v) + b_s.T[None, None, :, :, None]
    return u * vm.reshape(bn, l, B_WIDTH)


def _fwd_setup_inputs(seed: int = 0) -> dict:
    key = jax.random.key(seed)
    ks = jax.random.split(key, 20)
    nrm = lambda k, shape, s: jax.random.normal(k, shape, jnp.float32) * s
    gain = lambda k, shape: 1.0 + nrm(k, shape, 0.02)
    return {
        'x': nrm(ks[0], (BATCH, SEQ, D_MODEL), 1.0),
        'c': nrm(ks[1], (BATCH, D_MODEL), 1.0),
        'w_ada': nrm(ks[2], (DEPTH, D_MODEL, N_MOD * D_MODEL), D_MODEL ** -0.5),
        'b_ada': nrm(ks[3], (DEPTH, N_MOD * D_MODEL), 0.02),
        'g_pre_mix': gain(ks[4], (DEPTH, D_MODEL)),
        'g_post_mix': gain(ks[5], (DEPTH, D_MODEL)),
        'g_pre_ffn': gain(ks[6], (DEPTH, D_MODEL)),
        'g_post_ffn': gain(ks[7], (DEPTH, D_MODEL)),
        'w_in': nrm(ks[8], (DEPTH, D_MODEL, IN_WIDTH), D_MODEL ** -0.5),
        'lb_logits': nrm(ks[9], (2, DEPTH + 1, A_KEY_WIDTH), 0.1),
        'g_hgrn_norm': gain(ks[10], (DEPTH, A_VAL_DIM)),
        'w_a_out': nrm(ks[11], (DEPTH, A_WIDTH, D_MODEL), A_WIDTH ** -0.5),
        'g_sgu_norm': gain(ks[12], (DEPTH, B_WIDTH)),
        'w_spatial': nrm(ks[13], (DEPTH, B_GROUPS, B_CHUNK, B_CHUNK), B_CHUNK ** -0.5),
        'b_spatial': nrm(ks[14], (DEPTH, B_GROUPS, B_CHUNK), 0.02),
        'w_b_out': nrm(ks[15], (DEPTH, B_WIDTH, D_MODEL), B_WIDTH ** -0.5),
        'w_o': nrm(ks[16], (DEPTH, D_MODEL, D_MODEL), D_MODEL ** -0.5),
        'w_ff1': nrm(ks[17], (DEPTH, D_MODEL, D_FF), D_MODEL ** -0.5),
        'w_ff2': nrm(ks[18], (DEPTH, D_FF, D_MODEL), D_FF ** -0.5),
    }


def _fwd_reference(x, c, w_ada, b_ada, g_pre_mix, g_post_mix, g_pre_ffn, g_post_ffn, w_in,
              lb_logits, g_hgrn_norm, w_a_out, g_sgu_norm, w_spatial, b_spatial,
              w_b_out, w_o, w_ff1, w_ff2):
    lb_all = jnp.cumsum(jax.nn.softmax(lb_logits.astype(jnp.float32), axis=1), axis=1)
    h = x
    for l in range(DEPTH):
        mod = jax.nn.silu(c) @ w_ada[l] + b_ada[l]
        sh1, sc1, gt1, sh2, sc2, gt2 = [m[:, None, :] for m in jnp.split(mod, N_MOD, axis=-1)]
        a = rms_norm(h, g_pre_mix[l]) * (1 + sc1) + sh1
        proj = a @ w_in[l]
        q, f_fw, f_bw, i_v, o_gate, z, gate_a, gate_b = jnp.split(proj, IN_SPLIT_POINTS, axis=-1)
        y_a = hgrn2_bidir(q, f_fw, f_bw, i_v, o_gate, lb_all[0, l], lb_all[1, l],
                          g_hgrn_norm[l]) @ w_a_out[l]
        y_b = chunked_sgu(z, g_sgu_norm[l], w_spatial[l], b_spatial[l]) @ w_b_out[l]
        merged = jax.nn.sigmoid(gate_a) * y_a + jax.nn.sigmoid(gate_b) * y_b
        h = h + gt1 * rms_norm(merged @ w_o[l], g_post_mix[l])
        a = rms_norm(h, g_pre_ffn[l]) * (1 + sc2) + sh2
        ff = jnp.square(jax.nn.relu(a @ w_ff1[l])) @ w_ff2[l]
        h = h + gt2 * rms_norm(ff, g_post_ffn[l])
    return h


import jax as _jax
import jax.numpy as _jnp

TWIN_FORMAT = 'train_step'
FWD_PARAMS = ['x', 'c', 'w_ada', 'b_ada', 'g_pre_mix', 'g_post_mix', 'g_pre_ffn', 'g_post_ffn', 'w_in', 'lb_logits', 'g_hgrn_norm', 'w_a_out', 'g_sgu_norm', 'w_spatial', 'b_spatial', 'w_b_out', 'w_o', 'w_ff1', 'w_ff2']
TWIN_WEIGHTS = ['w_ada', 'b_ada', 'g_pre_mix', 'g_post_mix', 'g_pre_ffn', 'g_post_ffn', 'w_in', 'lb_logits', 'g_hgrn_norm', 'w_a_out', 'g_sgu_norm', 'w_spatial', 'b_spatial', 'w_b_out', 'w_o', 'w_ff1', 'w_ff2']
TWIN_DIFF_INPUT = 'x'
TWIN_INPUTS = ['x', 'c', 'w_ada', 'b_ada', 'g_pre_mix', 'g_post_mix', 'g_pre_ffn', 'g_post_ffn', 'w_in', 'lb_logits', 'g_hgrn_norm', 'w_a_out', 'g_sgu_norm', 'w_spatial', 'b_spatial', 'w_b_out', 'w_o', 'w_ff1', 'w_ff2', 'loss_target', 'm_w_ada', 'm_b_ada', 'm_g_pre_mix', 'm_g_post_mix', 'm_g_pre_ffn', 'm_g_post_ffn', 'm_w_in', 'm_lb_logits', 'm_g_hgrn_norm', 'm_w_a_out', 'm_g_sgu_norm', 'm_w_spatial', 'm_b_spatial', 'm_w_b_out', 'm_w_o', 'm_w_ff1', 'm_w_ff2', 'v_w_ada', 'v_b_ada', 'v_g_pre_mix', 'v_g_post_mix', 'v_g_pre_ffn', 'v_g_post_ffn', 'v_w_in', 'v_lb_logits', 'v_g_hgrn_norm', 'v_w_a_out', 'v_g_sgu_norm', 'v_w_spatial', 'v_b_spatial', 'v_w_b_out', 'v_w_o', 'v_w_ff1', 'v_w_ff2']
TWIN_OUTPUTS = ['loss', 'grad_x', 'grad_w_ada', 'grad_b_ada', 'grad_g_pre_mix', 'grad_g_post_mix', 'grad_g_pre_ffn', 'grad_g_post_ffn', 'grad_w_in', 'grad_lb_logits', 'grad_g_hgrn_norm', 'grad_w_a_out', 'grad_g_sgu_norm', 'grad_w_spatial', 'grad_b_spatial', 'grad_w_b_out', 'grad_w_o', 'grad_w_ff1', 'grad_w_ff2', 'delta_w_ada', 'delta_b_ada', 'delta_g_pre_mix', 'delta_g_post_mix', 'delta_g_pre_ffn', 'delta_g_post_ffn', 'delta_w_in', 'delta_lb_logits', 'delta_g_hgrn_norm', 'delta_w_a_out', 'delta_g_sgu_norm', 'delta_w_spatial', 'delta_b_spatial', 'delta_w_b_out', 'delta_w_o', 'delta_w_ff1', 'delta_w_ff2', 'new_m_w_ada', 'new_m_b_ada', 'new_m_g_pre_mix', 'new_m_g_post_mix', 'new_m_g_pre_ffn', 'new_m_g_post_ffn', 'new_m_w_in', 'new_m_lb_logits', 'new_m_g_hgrn_norm', 'new_m_w_a_out', 'new_m_g_sgu_norm', 'new_m_w_spatial', 'new_m_b_spatial', 'new_m_w_b_out', 'new_m_w_o', 'new_m_w_ff1', 'new_m_w_ff2', 'new_v_w_ada', 'new_v_b_ada', 'new_v_g_pre_mix', 'new_v_g_post_mix', 'new_v_g_pre_ffn', 'new_v_g_post_ffn', 'new_v_w_in', 'new_v_lb_logits', 'new_v_g_hgrn_norm', 'new_v_w_a_out', 'new_v_g_sgu_norm', 'new_v_w_spatial', 'new_v_b_spatial', 'new_v_w_b_out', 'new_v_w_o', 'new_v_w_ff1', 'new_v_w_ff2']
TWIN_LEAF_KINDS = {'loss': 'loss', 'grad_x': 'grad_x', 'grad_w_ada': 'grad_w', 'grad_b_ada': 'grad_w', 'grad_g_pre_mix': 'grad_w', 'grad_g_post_mix': 'grad_w', 'grad_g_pre_ffn': 'grad_w', 'grad_g_post_ffn': 'grad_w', 'grad_w_in': 'grad_w', 'grad_lb_logits': 'grad_w', 'grad_g_hgrn_norm': 'grad_w', 'grad_w_a_out': 'grad_w', 'grad_g_sgu_norm': 'grad_w', 'grad_w_spatial': 'grad_w', 'grad_b_spatial': 'grad_w', 'grad_w_b_out': 'grad_w', 'grad_w_o': 'grad_w', 'grad_w_ff1': 'grad_w', 'grad_w_ff2': 'grad_w', 'delta_w_ada': 'delta_w', 'delta_b_ada': 'delta_w', 'delta_g_pre_mix': 'delta_w', 'delta_g_post_mix': 'delta_w', 'delta_g_pre_ffn': 'delta_w', 'delta_g_post_ffn': 'delta_w', 'delta_w_in': 'delta_w', 'delta_lb_logits': 'delta_w', 'delta_g_hgrn_norm': 'delta_w', 'delta_w_a_out': 'delta_w', 'delta_g_sgu_norm': 'delta_w', 'delta_w_spatial': 'delta_w', 'delta_b_spatial': 'delta_w', 'delta_w_b_out': 'delta_w', 'delta_w_o': 'delta_w', 'delta_w_ff1': 'delta_w', 'delta_w_ff2': 'delta_w', 'new_m_w_ada': 'new_m', 'new_m_b_ada': 'new_m', 'new_m_g_pre_mix': 'new_m', 'new_m_g_post_mix': 'new_m', 'new_m_g_pre_ffn': 'new_m', 'new_m_g_post_ffn': 'new_m', 'new_m_w_in': 'new_m', 'new_m_lb_logits': 'new_m', 'new_m_g_hgrn_norm': 'new_m', 'new_m_w_a_out': 'new_m', 'new_m_g_sgu_norm': 'new_m', 'new_m_w_spatial': 'new_m', 'new_m_b_spatial': 'new_m', 'new_m_w_b_out': 'new_m', 'new_m_w_o': 'new_m', 'new_m_w_ff1': 'new_m', 'new_m_w_ff2': 'new_m', 'new_v_w_ada': 'new_v', 'new_v_b_ada': 'new_v', 'new_v_g_pre_mix': 'new_v', 'new_v_g_post_mix': 'new_v', 'new_v_g_pre_ffn': 'new_v', 'new_v_g_post_ffn': 'new_v', 'new_v_w_in': 'new_v', 'new_v_lb_logits': 'new_v', 'new_v_g_hgrn_norm': 'new_v', 'new_v_w_a_out': 'new_v', 'new_v_g_sgu_norm': 'new_v', 'new_v_w_spatial': 'new_v', 'new_v_b_spatial': 'new_v', 'new_v_w_b_out': 'new_v', 'new_v_w_o': 'new_v', 'new_v_w_ff1': 'new_v', 'new_v_w_ff2': 'new_v'}


def _forward(args):
    return _fwd_reference(*[args[k] for k in FWD_PARAMS])


def _output_shape():
    def fwd():
        inp = _fwd_setup_inputs(0)
        return _fwd_reference(*[inp[k] for k in FWD_PARAMS])
    out = _jax.eval_shape(fwd)
    return out.shape, out.dtype

N_MICROBATCH = 1
ADAM_LR = 0.001
ADAM_B1 = 0.9
ADAM_B2 = 0.999
ADAM_EPS = 1e-08
ADAM_WD = 0.01
ADAM_STEP = 10
PER_EXAMPLE_BATCH_AXIS = {'x': 0, 'c': 0, 'loss_target': 0}
SHARED_INPUTS = []
_WEIGHT_DTYPES = {'w_ada': _jnp.float32, 'b_ada': _jnp.float32, 'g_pre_mix': _jnp.float32, 'g_post_mix': _jnp.float32, 'g_pre_ffn': _jnp.float32, 'g_post_ffn': _jnp.float32, 'w_in': _jnp.float32, 'lb_logits': _jnp.float32, 'g_hgrn_norm': _jnp.float32, 'w_a_out': _jnp.float32, 'g_sgu_norm': _jnp.float32, 'w_spatial': _jnp.float32, 'b_spatial': _jnp.float32, 'w_b_out': _jnp.float32, 'w_o': _jnp.float32, 'w_ff1': _jnp.float32, 'w_ff2': _jnp.float32}
MOMENT_SCALE = {'w_ada': 1.272932e+00, 'b_ada': 2.697522e+00, 'g_pre_mix': 1.884415e-01, 'g_post_mix': 6.528989e+00, 'g_pre_ffn': 1.277342e-01, 'g_post_ffn': 6.756584e+00, 'w_in': 1.162541e-01, 'lb_logits': 3.209894e-02, 'g_hgrn_norm': 5.227088e-01, 'w_a_out': 1.469808e-01, 'g_sgu_norm': 1.393714e-01, 'w_spatial': 1.109137e-01, 'b_spatial': 1.192897e-01, 'w_b_out': 9.184868e-02, 'w_o': 1.786865e-01, 'w_ff1': 2.513304e-01, 'w_ff2': 7.867183e-01}


def _to_microbatches(a, axis):
    t = _jnp.moveaxis(a, axis, 0)
    t = t.reshape((N_MICROBATCH, t.shape[0] // N_MICROBATCH) + t.shape[1:])
    return _jnp.moveaxis(t, 1, axis + 1)


def setup_inputs(seed: int = 0) -> dict:
    inp = _fwd_setup_inputs(seed)
    key = _jax.random.fold_in(_jax.random.key(seed), 7919)
    shape, _ = _output_shape()
    out = dict(inp)
    out["loss_target"] = _jax.random.normal(_jax.random.fold_in(key, 0), shape, _jnp.float32)
    for i, name in enumerate(TWIN_WEIGHTS):
        w = inp[name].astype(_jnp.float32)
        if MOMENT_SCALE is None:
            s = _jnp.sqrt(_jnp.mean(_jnp.square(w)) + 1e-30)
        else:
            s = MOMENT_SCALE[name]
        km, kv = _jax.random.split(_jax.random.fold_in(key, i + 1))
        out[name] = w
        out["m_" + name] = s * _jax.random.normal(km, w.shape, _jnp.float32)
        out["v_" + name] = (s * s) * _jax.random.uniform(kv, w.shape, _jnp.float32, 0.5, 1.5)
    if N_MICROBATCH > 1:
        for name, axis in PER_EXAMPLE_BATCH_AXIS.items():
            out[name] = _to_microbatches(out[name], axis)
    return {'x': out['x'], 'c': out['c'], 'w_ada': out['w_ada'], 'b_ada': out['b_ada'], 'g_pre_mix': out['g_pre_mix'], 'g_post_mix': out['g_post_mix'], 'g_pre_ffn': out['g_pre_ffn'], 'g_post_ffn': out['g_post_ffn'], 'w_in': out['w_in'], 'lb_logits': out['lb_logits'], 'g_hgrn_norm': out['g_hgrn_norm'], 'w_a_out': out['w_a_out'], 'g_sgu_norm': out['g_sgu_norm'], 'w_spatial': out['w_spatial'], 'b_spatial': out['b_spatial'], 'w_b_out': out['w_b_out'], 'w_o': out['w_o'], 'w_ff1': out['w_ff1'], 'w_ff2': out['w_ff2'], 'loss_target': out['loss_target'], 'm_w_ada': out['m_w_ada'], 'm_b_ada': out['m_b_ada'], 'm_g_pre_mix': out['m_g_pre_mix'], 'm_g_post_mix': out['m_g_post_mix'], 'm_g_pre_ffn': out['m_g_pre_ffn'], 'm_g_post_ffn': out['m_g_post_ffn'], 'm_w_in': out['m_w_in'], 'm_lb_logits': out['m_lb_logits'], 'm_g_hgrn_norm': out['m_g_hgrn_norm'], 'm_w_a_out': out['m_w_a_out'], 'm_g_sgu_norm': out['m_g_sgu_norm'], 'm_w_spatial': out['m_w_spatial'], 'm_b_spatial': out['m_b_spatial'], 'm_w_b_out': out['m_w_b_out'], 'm_w_o': out['m_w_o'], 'm_w_ff1': out['m_w_ff1'], 'm_w_ff2': out['m_w_ff2'], 'v_w_ada': out['v_w_ada'], 'v_b_ada': out['v_b_ada'], 'v_g_pre_mix': out['v_g_pre_mix'], 'v_g_post_mix': out['v_g_post_mix'], 'v_g_pre_ffn': out['v_g_pre_ffn'], 'v_g_post_ffn': out['v_g_post_ffn'], 'v_w_in': out['v_w_in'], 'v_lb_logits': out['v_lb_logits'], 'v_g_hgrn_norm': out['v_g_hgrn_norm'], 'v_w_a_out': out['v_w_a_out'], 'v_g_sgu_norm': out['v_g_sgu_norm'], 'v_w_spatial': out['v_w_spatial'], 'v_b_spatial': out['v_b_spatial'], 'v_w_b_out': out['v_w_b_out'], 'v_w_o': out['v_w_o'], 'v_w_ff1': out['v_w_ff1'], 'v_w_ff2': out['v_w_ff2']}


def _loss(weights, diff, rest, loss_target):
    with _jax.named_scope("forward"):
        args = {**rest, TWIN_DIFF_INPUT: diff, **{k: w.astype(_WEIGHT_DTYPES[k]) for k, w in weights.items()}}
        y = _forward(args)
    with _jax.named_scope("loss_head"):
        err = _jnp.square(y.astype(_jnp.float32) - loss_target)
        return 0.5 * _jnp.sum(_jnp.mean(err, axis=-1)) if err.ndim else 0.5 * err


def _adamw(w, g, m, v):
    m = ADAM_B1 * m + (1.0 - ADAM_B1) * g
    v = ADAM_B2 * v + (1.0 - ADAM_B2) * _jnp.square(g)
    m_hat = m / (1.0 - ADAM_B1 ** ADAM_STEP)
    v_hat = v / (1.0 - ADAM_B2 ** ADAM_STEP)
    delta = -ADAM_LR * (m_hat / (_jnp.sqrt(v_hat) + ADAM_EPS) + ADAM_WD * w)
    return delta, m, v


def reference(x, c, w_ada, b_ada, g_pre_mix, g_post_mix, g_pre_ffn, g_post_ffn, w_in, lb_logits, g_hgrn_norm, w_a_out, g_sgu_norm, w_spatial, b_spatial, w_b_out, w_o, w_ff1, w_ff2, loss_target, m_w_ada, m_b_ada, m_g_pre_mix, m_g_post_mix, m_g_pre_ffn, m_g_post_ffn, m_w_in, m_lb_logits, m_g_hgrn_norm, m_w_a_out, m_g_sgu_norm, m_w_spatial, m_b_spatial, m_w_b_out, m_w_o, m_w_ff1, m_w_ff2, v_w_ada, v_b_ada, v_g_pre_mix, v_g_post_mix, v_g_pre_ffn, v_g_post_ffn, v_w_in, v_lb_logits, v_g_hgrn_norm, v_w_a_out, v_g_sgu_norm, v_w_spatial, v_b_spatial, v_w_b_out, v_w_o, v_w_ff1, v_w_ff2):
    given = dict(x=x, c=c, w_ada=w_ada, b_ada=b_ada, g_pre_mix=g_pre_mix, g_post_mix=g_post_mix, g_pre_ffn=g_pre_ffn, g_post_ffn=g_post_ffn, w_in=w_in, lb_logits=lb_logits, g_hgrn_norm=g_hgrn_norm, w_a_out=w_a_out, g_sgu_norm=g_sgu_norm, w_spatial=w_spatial, b_spatial=b_spatial, w_b_out=w_b_out, w_o=w_o, w_ff1=w_ff1, w_ff2=w_ff2, loss_target=loss_target, m_w_ada=m_w_ada, m_b_ada=m_b_ada, m_g_pre_mix=m_g_pre_mix, m_g_post_mix=m_g_post_mix, m_g_pre_ffn=m_g_pre_ffn, m_g_post_ffn=m_g_post_ffn, m_w_in=m_w_in, m_lb_logits=m_lb_logits, m_g_hgrn_norm=m_g_hgrn_norm, m_w_a_out=m_w_a_out, m_g_sgu_norm=m_g_sgu_norm, m_w_spatial=m_w_spatial, m_b_spatial=m_b_spatial, m_w_b_out=m_w_b_out, m_w_o=m_w_o, m_w_ff1=m_w_ff1, m_w_ff2=m_w_ff2, v_w_ada=v_w_ada, v_b_ada=v_b_ada, v_g_pre_mix=v_g_pre_mix, v_g_post_mix=v_g_post_mix, v_g_pre_ffn=v_g_pre_ffn, v_g_post_ffn=v_g_post_ffn, v_w_in=v_w_in, v_lb_logits=v_lb_logits, v_g_hgrn_norm=v_g_hgrn_norm, v_w_a_out=v_w_a_out, v_g_sgu_norm=v_g_sgu_norm, v_w_spatial=v_w_spatial, v_b_spatial=v_b_spatial, v_w_b_out=v_w_b_out, v_w_o=v_w_o, v_w_ff1=v_w_ff1, v_w_ff2=v_w_ff2)
    weights = {n: given[n] for n in TWIN_WEIGHTS}
    shared = {n: given[n] for n in SHARED_INPUTS}
    per_example = {n: given[n] for n in ['x', 'c']}
    grad_fn = _jax.value_and_grad(_loss, argnums=(0, 1))

    def one_microbatch(ex, loss_target):
        ex = dict(ex)
        diff = ex.pop(TWIN_DIFF_INPUT)
        return grad_fn(weights, diff, {**shared, **ex}, loss_target)

    if N_MICROBATCH == 1:
        loss, (grad_w, grad_x) = one_microbatch(per_example, given["loss_target"])
    else:
        def body(carry, xs):
            loss_sum, grad_sum = carry
            l_k, (gw_k, gx_k) = one_microbatch(xs[0], xs[1])
            with _jax.named_scope("update"):
                return (loss_sum + l_k, _jax.tree.map(_jnp.add, grad_sum, gw_k)), gx_k

        init = (_jnp.zeros((), _jnp.float32), _jax.tree.map(_jnp.zeros_like, weights))
        (loss, grad_w), grad_x = _jax.lax.scan(body, init, (per_example, given["loss_target"]))
    with _jax.named_scope("update"):
        delta_w, new_m, new_v = {}, {}, {}
        for n in TWIN_WEIGHTS:
            delta_w[n], new_m[n], new_v[n] = _adamw(weights[n], grad_w[n], given["m_" + n], given["v_" + n])
    return (loss, grad_x, *[grad_w[n] for n in TWIN_WEIGHTS], *[delta_w[n] for n in TWIN_WEIGHTS],
            *[new_m[n] for n in TWIN_WEIGHTS], *[new_v[n] for n in TWIN_WEIGHTS])
```

```python
import functools
import math

import jax
import jax.numpy as jnp
from jax import lax
from jax.experimental import pallas as pl
from jax.experimental.pallas import tpu as pltpu

F32, BF16 = jnp.float32, jnp.bfloat16
HI = lax.Precision.HIGHEST
MESH = pl.DeviceIdType.MESH
ANY = pl.BlockSpec(memory_space=pl.ANY)

EPS = 1e-6
D_MODEL = 2048
N_HEADS = 8
HEAD_DIM = 128
HGRN_CHUNK = 32
HGRN_BLOCK = 256
SGU_CHUNK = 128
SGU_GROUPS = 8
Q_SCALE = HEAD_DIM ** -0.5
COL_Q, COL_FFW, COL_FBW, COL_V, COL_OG, COL_U, COL_ZV, COL_GA, COL_GB = 0, 1, 2, 3, 4, 5, 6, 7, 9
N_PROJ = 11264
VMEM_BYTES_V7X = 64 * 1024 * 1024
VMEM_LIMIT = VMEM_BYTES_V7X - 8 * 1024 * 1024

ADAM_LR, ADAM_B1, ADAM_B2, ADAM_EPS, ADAM_WD, ADAM_STEP = 0.001, 0.9, 0.999, 1e-08, 0.01, 10
ADAM_C1 = 1.0 - ADAM_B1 ** ADAM_STEP
ADAM_C2 = 1.0 - ADAM_B2 ** ADAM_STEP


def _cp(*sem):
    return pltpu.CompilerParams(dimension_semantics=sem if sem else None, vmem_limit_bytes=VMEM_LIMIT)


def _vec(d):
    return pl.BlockSpec((1, d), lambda *_: (0, 0))


def _colsum(x):
    return jnp.sum(x, axis=0, keepdims=True)


def _nt(a, b):
    return lax.dot_general(a, b, (((1,), (1,)), ((), ())), preferred_element_type=F32)


def _tn(a, b):
    return lax.dot_general(a, b, (((0,), (0,)), ((), ())), preferred_element_type=F32)


def _nn(a, b):
    return jnp.dot(a, b, preferred_element_type=F32)


def _adamw(w, g, m, v):
    m2 = ADAM_B1 * m + (1.0 - ADAM_B1) * g
    v2 = ADAM_B2 * v + (1.0 - ADAM_B2) * (g * g)
    delta = -ADAM_LR * ((m2 / ADAM_C1) / (jnp.sqrt(v2 / ADAM_C2) + ADAM_EPS) + ADAM_WD * w)
    return delta, m2, v2


def matmul(a, b, *, mode, out_dtype, tm, tn, tk, name, split=None):
    if mode == "tn":
        (K, M), (_, N) = a.shape, b.shape
    elif mode == "nt":
        (M, K), (N, _) = a.shape, b.shape
    else:
        (M, K), (_, N) = a.shape, b.shape
    tm, tn, tk = min(tm, M), min(tn, N), min(tk, K)
    nk = K // tk
    a_spec = pl.BlockSpec((tk, tm), lambda i, j, k: (k, i)) if mode == "tn" else pl.BlockSpec((tm, tk), lambda i, j, k: (i, k))
    b_spec = pl.BlockSpec((tn, tk), lambda i, j, k: (j, k)) if mode == "nt" else pl.BlockSpec((tk, tn), lambda i, j, k: (k, j))
    dot = {"nn": _nn, "nt": _nt, "tn": _tn}[mode]
    if split is None:
        out_shape = jax.ShapeDtypeStruct((M, N), out_dtype)
        out_spec = pl.BlockSpec((tm, tn), lambda i, j, k: (i, j))
    else:
        nj, nh = split
        rows, cols = M // nh, N // nj
        tm, tn = min(tm, rows), min(tn, cols)
        bi, bj = rows // tm, cols // tn
        out_shape = jax.ShapeDtypeStruct((nj, nh, rows, cols), out_dtype)
        out_spec = pl.BlockSpec((None, None, tm, tn), lambda i, j, k: (j // bj, i // bi, i % bi, j % bj))

    def body(a_ref, b_ref, o_ref, acc_ref):
        k = pl.program_id(2)

        @pl.when(k == 0)
        def _():
            acc_ref[...] = jnp.zeros_like(acc_ref)

        acc_ref[...] += dot(a_ref[...], b_ref[...])

        @pl.when(k == nk - 1)
        def _():
            o_ref[...] = acc_ref[...].astype(o_ref.dtype)

    return pl.pallas_call(
        body, name=name, out_shape=out_shape, grid=(M // tm, N // tn, nk),
        in_specs=[a_spec, b_spec], out_specs=out_spec, scratch_shapes=[pltpu.VMEM((tm, tn), F32)],
        compiler_params=_cp("parallel", "parallel", "arbitrary"),
    )(a, b)


def cast_bf16(w, name):
    R, C = w.shape
    tr = min(R, 512)

    def body(w_ref, o_ref):
        o_ref[...] = w_ref[...].astype(BF16)

    return pl.pallas_call(
        body, name=name, out_shape=jax.ShapeDtypeStruct((R, C), BF16), grid=(R // tr,),
        in_specs=[pl.BlockSpec((tr, C), lambda i: (i, 0))], out_specs=pl.BlockSpec((tr, C), lambda i: (i, 0)),
        compiler_params=_cp("parallel"),
    )(w)


def mod_matmul(c_all, w_ada, b_ada):
    D, N = w_ada.shape
    tn = 1024

    def body(c_ref, w_ref, b_ref, o_ref):
        c = c_ref[...]
        sc = c * jax.nn.sigmoid(c)
        o_ref[...] = jnp.dot(sc, w_ref[...], precision=HI, preferred_element_type=F32) + b_ref[...]

    return pl.pallas_call(
        body, name="mod_matmul", out_shape=jax.ShapeDtypeStruct((8, N), F32), grid=(N // tn,),
        in_specs=[pl.BlockSpec((8, D), lambda j: (0, 0)), pl.BlockSpec((D, tn), lambda j: (0, j)),
                  pl.BlockSpec((1, tn), lambda j: (0, j))],
        out_specs=pl.BlockSpec((8, tn), lambda j: (0, j)), compiler_params=_cp("parallel"),
    )(c_all, w_ada, b_ada)


def prenorm_matmul(h, g, sc, sh, w, *, relu2, name):
    T, D = h.shape
    N = w.shape[1]
    tm, tn = min(512, T), 1024

    def body(h_ref, g_ref, sc_ref, sh_ref, w_ref, y_ref, a_ref, *hid_ref):
        @pl.when(pl.program_id(1) == 0)
        def _():
            x = h_ref[...]
            r = lax.rsqrt(jnp.mean(x * x, axis=-1, keepdims=True) + EPS)
            a_ref[...] = ((x * r) * g_ref[...] * (1.0 + sc_ref[...]) + sh_ref[...]).astype(BF16)

        y = _nn(a_ref[...], w_ref[...])
        y_ref[...] = y
        if relu2:
            p = jnp.maximum(y, 0.0)
            hid_ref[0][...] = (p * p).astype(BF16)

    out_shape = [jax.ShapeDtypeStruct((T, N), F32), jax.ShapeDtypeStruct((T, D), BF16)]
    out_specs = [pl.BlockSpec((tm, tn), lambda i, j: (i, j)), pl.BlockSpec((tm, D), lambda i, j: (i, 0))]
    if relu2:
        out_shape.append(jax.ShapeDtypeStruct((T, N), BF16))
        out_specs.append(pl.BlockSpec((tm, tn), lambda i, j: (i, j)))
    return pl.pallas_call(
        body, name=name, out_shape=out_shape, grid=(T // tm, N // tn),
        in_specs=[pl.BlockSpec((tm, D), lambda i, j: (i, 0)), _vec(D), _vec(D), _vec(D),
                  pl.BlockSpec((D, tn), lambda i, j: (0, j))],
        out_specs=out_specs, compiler_params=_cp("parallel", "arbitrary"),
    )(h, g, sc, sh, w)


def _hgrn_lower_bound(l_ref):
    l0, l1 = l_ref[0:1, :], l_ref[1:2, :]
    m = jnp.maximum(l0, l1)
    e0, e1 = jnp.exp(l0 - m), jnp.exp(l1 - m)
    return e0 / (e0 + e1)


def _hgrn_chunk_masks(d):
    r = lax.broadcasted_iota(jnp.int32, (HGRN_BLOCK, HGRN_BLOCK), 0)
    c = lax.broadcasted_iota(jnp.int32, (HGRN_BLOCK, HGRN_BLOCK), 1)
    same = (r // HGRN_CHUNK) == (c // HGRN_CHUNK)
    fwd = d == 0
    tri = same & (((c <= r) & fwd) | ((c >= r) & jnp.logical_not(fwd)))
    tri_t = same & (((c >= r) & fwd) | ((c <= r) & jnp.logical_not(fwd)))
    one = jnp.where(same, 1.0, 0.0).astype(F32)
    return tri, jnp.where(tri, 1.0, 0.0).astype(F32), jnp.where(tri_t, 1.0, 0.0).astype(F32), one


def _hgrn_gate(f, lb):
    s = jax.nn.sigmoid(f)
    sn = jax.nn.sigmoid(-f)
    fg = lb + (1.0 - lb) * s
    return s, sn, fg, jnp.log(fg), (1.0 - lb) * sn


def _hgrn_specs(T):
    col = lambda base: pl.BlockSpec((T, HEAD_DIM), lambda h, d: (0, base * N_HEADS + h))
    f_spec = pl.BlockSpec((T, HEAD_DIM), lambda h, d: (0, COL_FFW * N_HEADS + N_HEADS * d + h))
    l_spec = pl.BlockSpec((None, 2, HEAD_DIM), lambda h, d: (d, 0, h))
    return col, f_spec, l_spec


def hgrn_fwd(proj, lb_logits):
    T = proj.shape[0]
    NC, CPB = T // HGRN_CHUNK, HGRN_BLOCK // HGRN_CHUNK
    col, f_spec, l_spec = _hgrn_specs(T)

    def body(l_ref, q_ref, f_ref, v_ref, o_ref, st_ref, dec_ref, qd_ref, oi_ref):
        d = pl.program_id(1)
        lb = _hgrn_lower_bound(l_ref)
        mask, mtri, _, mone = _hgrn_chunk_masks(d)

        def block(i, carry):
            rows = pl.ds(pl.multiple_of(i * HGRN_BLOCK, HGRN_BLOCK), HGRN_BLOCK)
            _, _, _, lf, k = _hgrn_gate(f_ref[rows, :], lb)
            b = jnp.dot(mtri, lf, precision=HI, preferred_element_type=F32)
            bl = jnp.dot(mone, lf, precision=HI, preferred_element_type=F32)
            qd = (q_ref[rows, :] * Q_SCALE * jnp.exp(b)).astype(BF16)
            kd = (k * jnp.exp(-b)).astype(BF16)
            ke = (k * jnp.exp(bl - b)).astype(BF16)
            vb = v_ref[rows, :].astype(BF16)
            att = jnp.where(mask, _nt(qd, kd), 0.0).astype(BF16)
            oi_ref[rows, :] = _nn(att, vb)
            qd_ref[rows, :] = qd
            dec = jnp.exp(bl)
            for cc in range(CPB):
                sl = slice(cc * HGRN_CHUNK, (cc + 1) * HGRN_CHUNK)
                n = i * CPB + cc
                st_ref[n] = _tn(vb[sl], ke[sl])
                dec_ref[n] = dec[cc * HGRN_CHUNK:cc * HGRN_CHUNK + 8, :]
            return carry

        lax.fori_loop(0, T // HGRN_BLOCK, block, 0)

        def scan(t, s):
            n = jnp.where(d == 0, t, NC - 1 - t)
            u = st_ref[n]
            st_ref[n] = s
            return dec_ref[n][0:1, :] * s + u

        lax.fori_loop(0, NC, scan, jnp.zeros((HEAD_DIM, HEAD_DIM), F32))

        def inter(n, carry):
            rows = pl.ds(pl.multiple_of(n * HGRN_CHUNK, HGRN_CHUNK), HGRN_CHUNK)
            oi_ref[rows, :] += _nt(qd_ref[rows, :], st_ref[n].astype(BF16))
            return carry

        lax.fori_loop(0, NC, inter, 0)

        @pl.when(d == 0)
        def _():
            o_ref[...] = oi_ref[...]

        @pl.when(d == 1)
        def _():
            o_ref[...] += oi_ref[...]

    return pl.pallas_call(
        body, name="hgrn_fwd", out_shape=jax.ShapeDtypeStruct((T, N_HEADS * HEAD_DIM), F32), grid=(N_HEADS, 2),
        in_specs=[l_spec, col(COL_Q), f_spec, col(COL_V)],
        out_specs=pl.BlockSpec((T, HEAD_DIM), lambda h, d: (0, h)),
        scratch_shapes=[pltpu.VMEM((NC, HEAD_DIM, HEAD_DIM), F32), pltpu.VMEM((NC, 8, HEAD_DIM), F32),
                        pltpu.VMEM((T, HEAD_DIM), BF16), pltpu.VMEM((T, HEAD_DIM), F32)],
        compiler_params=_cp("parallel", "arbitrary"),
    )(lb_logits, proj, proj, proj)


def hgrn_post_fwd(o, proj, g_norm):
    T, W = o.shape
    tm = min(256, T)

    def body(o_ref, og_ref, g_ref, y_ref):
        g = g_ref[...]
        for h in range(N_HEADS):
            sl = slice(h * HEAD_DIM, (h + 1) * HEAD_DIM)
            x = o_ref[:, sl]
            r = lax.rsqrt(jnp.mean(x * x, axis=-1, keepdims=True) + EPS)
            og = og_ref[:, sl]
            y_ref[:, sl] = ((x * r) * g * (og * jax.nn.sigmoid(og))).astype(BF16)

    return pl.pallas_call(
        body, name="hgrn_post_fwd", out_shape=jax.ShapeDtypeStruct((T, W), BF16), grid=(T // tm,),
        in_specs=[pl.BlockSpec((tm, W), lambda i: (i, 0)), pl.BlockSpec((tm, W), lambda i: (i, COL_OG)), _vec(HEAD_DIM)],
        out_specs=pl.BlockSpec((tm, W), lambda i: (i, 0)), compiler_params=_cp("parallel"),
    )(o, proj, g_norm)


def _gelu(x):
    return 0.5 * x * (1.0 + lax.erf(x * (1.0 / math.sqrt(2.0))))


def _gelu_grad(x):
    return 0.5 * (1.0 + lax.erf(x * (1.0 / math.sqrt(2.0)))) + x * jnp.exp(-0.5 * x * x) * (1.0 / math.sqrt(2.0 * math.pi))


def _sgu_mix(u_ref, v_ref, g_ref, ws_ref, bst_ref):
    W = u_ref.shape[1]
    zu, zv = _gelu(u_ref[...]), _gelu(v_ref[...])
    dv = zv - jnp.mean(zv, axis=-1, keepdims=True)
    rstd = lax.rsqrt(jnp.mean(dv * dv, axis=-1, keepdims=True) + EPS)
    dhat = dv * rstd
    vn = (dhat * g_ref[...]).astype(BF16)
    gw = W // SGU_GROUPS
    vm = [_nn(ws_ref[g].astype(BF16), vn[:, g * gw:(g + 1) * gw]) + bst_ref[:, g:g + 1] for g in range(SGU_GROUPS)]
    return zu, rstd, dhat, vn, jnp.concatenate(vm, axis=1)


def sgu_fwd(proj, g_norm, w_spatial, b_spatial_t):
    T = proj.shape[0]
    W = 1024
    n_chunks = T // SGU_CHUNK

    def body(u_ref, v_ref, g_ref, ws_ref, bst_ref, y_ref):
        zu, _, _, _, vm = _sgu_mix(u_ref, v_ref, g_ref, ws_ref, bst_ref)
        y_ref[...] = (zu * vm).astype(BF16)

    blk = lambda cb: pl.BlockSpec((SGU_CHUNK, W), lambda i: (i, cb))
    return pl.pallas_call(
        body, name="sgu_fwd", out_shape=jax.ShapeDtypeStruct((T, W), BF16), grid=(n_chunks,),
        in_specs=[blk(COL_U), blk(COL_ZV), _vec(W), pl.BlockSpec((SGU_GROUPS, SGU_CHUNK, SGU_CHUNK), lambda i: (0, 0, 0)),
                  pl.BlockSpec((SGU_CHUNK, SGU_GROUPS), lambda i: (0, 0))],
        out_specs=blk(0), compiler_params=_cp("parallel"),
    )(proj, proj, g_norm, w_spatial, b_spatial_t)


def merge_matmul(ya_pre, sgu, w_a, w_b, proj):
    T, K = ya_pre.shape
    N = w_a.shape[1]
    tm, tn = min(512, T), 512
    gpb = 1024 // tn

    def body(a_ref, b_ref, wa_ref, wb_ref, ga_ref, gb_ref, ya_ref, yb_ref, m_ref):
        ya = _nn(a_ref[...], wa_ref[...])
        yb = _nn(b_ref[...], wb_ref[...])
        ya_ref[...] = ya
        yb_ref[...] = yb
        m_ref[...] = (jax.nn.sigmoid(ga_ref[...]) * ya + jax.nn.sigmoid(gb_ref[...]) * yb).astype(BF16)

    lhs = pl.BlockSpec((tm, K), lambda i, j: (i, 0))
    rhs = pl.BlockSpec((K, tn), lambda i, j: (0, j))
    out = pl.BlockSpec((tm, tn), lambda i, j: (i, j))
    return pl.pallas_call(
        body, name="merge_matmul", grid=(T // tm, N // tn),
        out_shape=[jax.ShapeDtypeStruct((T, N), F32), jax.ShapeDtypeStruct((T, N), F32), jax.ShapeDtypeStruct((T, N), BF16)],
        in_specs=[lhs, lhs, rhs, rhs, pl.BlockSpec((tm, tn), lambda i, j: (i, COL_GA * gpb + j)),
                  pl.BlockSpec((tm, tn), lambda i, j: (i, COL_GB * gpb + j))],
        out_specs=[out, out, out], compiler_params=_cp("parallel", "parallel"),
    )(ya_pre, sgu, w_a, w_b, proj, proj)


def out_proj(merged, w_o, h0, gt1, g_post):
    T, D = h0.shape
    tm = min(256, T)

    def body(m_ref, w_ref, h_ref, gt_ref, gp_ref, mo_ref, h1_ref):
        mo = _nn(m_ref[...], w_ref[...])
        mo_ref[...] = mo
        r = lax.rsqrt(jnp.mean(mo * mo, axis=-1, keepdims=True) + EPS)
        h1_ref[...] = h_ref[...] + gt_ref[...] * ((mo * r) * gp_ref[...])

    row = pl.BlockSpec((tm, D), lambda i: (i, 0))
    return pl.pallas_call(
        body, name="out_proj", grid=(T // tm,),
        out_shape=[jax.ShapeDtypeStruct((T, D), F32), jax.ShapeDtypeStruct((T, D), F32)],
        in_specs=[row, pl.BlockSpec((D, D), lambda i: (0, 0)), row, _vec(D), _vec(D)],
        out_specs=[row, row], compiler_params=_cp("parallel"),
    )(merged, w_o, h0, gt1, g_post)


def ff2_loss(hid, w_ff2, h1, tgt, gt2, g_post):
    T, K = hid.shape
    D = w_ff2.shape[1]
    tm, tk = min(256, T), 2048
    nk = K // tk

    def body(a_ref, w_ref, h_ref, t_ref, gt_ref, g_ref, dy_ref, dff_ref, loss_ref, dgt_ref, dg_ref, acc_ref):
        i, k = pl.program_id(0), pl.program_id(1)

        @pl.when(k == 0)
        def _():
            acc_ref[...] = jnp.zeros_like(acc_ref)

        @pl.when((k == 0) & (i == 0))
        def _():
            loss_ref[...] = jnp.zeros_like(loss_ref)
            dgt_ref[...] = jnp.zeros_like(dgt_ref)
            dg_ref[...] = jnp.zeros_like(dg_ref)

        acc_ref[...] += _nn(a_ref[...], w_ref[...])

        @pl.when(k == nk - 1)
        def _():
            ff = acc_ref[...]
            gt, g = gt_ref[...], g_ref[...]
            r = lax.rsqrt(jnp.mean(ff * ff, axis=-1, keepdims=True) + EPS)
            fhat = ff * r
            nf = fhat * g
            err = (h_ref[...] + gt * nf) - t_ref[...]
            loss_ref[...] += jnp.sum(err * err)
            dy = err * (1.0 / D)
            dy_ref[...] = dy
            dgt_ref[...] += _colsum(dy * nf)
            dnf = dy * gt
            dg_ref[...] += _colsum(dnf * fhat)
            u = dnf * g
            dff_ref[...] = (r * (u - fhat * jnp.mean(u * fhat, axis=-1, keepdims=True))).astype(BF16)

    row = pl.BlockSpec((tm, D), lambda i, k: (i, 0))
    vec = pl.BlockSpec((1, D), lambda i, k: (0, 0))
    return pl.pallas_call(
        body, name="ff2_loss", grid=(T // tm, nk),
        out_shape=[jax.ShapeDtypeStruct((T, D), F32), jax.ShapeDtypeStruct((T, D), BF16), jax.ShapeDtypeStruct((8, 128), F32),
                   jax.ShapeDtypeStruct((1, D), F32), jax.ShapeDtypeStruct((1, D), F32)],
        in_specs=[pl.BlockSpec((tm, tk), lambda i, k: (i, k)), pl.BlockSpec((tk, D), lambda i, k: (k, 0)), row, row, vec, vec],
        out_specs=[row, row, pl.BlockSpec((8, 128), lambda i, k: (0, 0)), vec, vec],
        scratch_shapes=[pltpu.VMEM((tm, D), F32)], compiler_params=_cp("arbitrary", "arbitrary"),
    )(hid, w_ff2, h1, tgt, gt2, g_post)


def ff2_bwd(dff, w_ff2, f1):
    T, D = dff.shape
    K = w_ff2.shape[0]
    tm, tn = min(512, T), 1024

    def body(a_ref, w_ref, f_ref, o_ref):
        o_ref[...] = (_nt(a_ref[...], w_ref[...]) * (2.0 * jnp.maximum(f_ref[...], 0.0))).astype(BF16)

    return pl.pallas_call(
        body, name="ff2_bwd", out_shape=jax.ShapeDtypeStruct((T, K), BF16), grid=(T // tm, K // tn),
        in_specs=[pl.BlockSpec((tm, D), lambda i, j: (i, 0)), pl.BlockSpec((tn, D), lambda i, j: (j, 0)),
                  pl.BlockSpec((tm, tn), lambda i, j: (i, j))],
        out_specs=pl.BlockSpec((tm, tn), lambda i, j: (i, j)), compiler_params=_cp("parallel", "parallel"),
    )(dff, w_ff2, f1)


def ffn_norm_bwd(dy, da2, h1, mo, g_pre2, sc2, gt1, g_post):
    T, D = dy.shape
    tm = min(256, T)

    def body(dy_ref, da_ref, h_ref, mo_ref, g2_ref, sc_ref, gt_ref, gp_ref, dh_ref, dmo_ref, s_sh, s_sc, s_g2, s_gt, s_gp):
        @pl.when(pl.program_id(0) == 0)
        def _():
            for s in (s_sh, s_sc, s_g2, s_gt, s_gp):
                s[...] = jnp.zeros_like(s)

        h1, da = h_ref[...], da_ref[...]
        g2, sc = g2_ref[...], sc_ref[...]
        r2 = lax.rsqrt(jnp.mean(h1 * h1, axis=-1, keepdims=True) + EPS)
        n2 = h1 * r2
        s_sh[...] += _colsum(da)
        s_sc[...] += _colsum(da * (n2 * g2))
        s_g2[...] += _colsum(da * (1.0 + sc) * n2)
        dn2 = da * g2 * (1.0 + sc)
        dh1 = dy_ref[...] + r2 * (dn2 - n2 * jnp.mean(dn2 * n2, axis=-1, keepdims=True))
        dh_ref[...] = dh1
        mo = mo_ref[...]
        gt, gp = gt_ref[...], gp_ref[...]
        r = lax.rsqrt(jnp.mean(mo * mo, axis=-1, keepdims=True) + EPS)
        mhat = mo * r
        s_gt[...] += _colsum(dh1 * (mhat * gp))
        dnm = dh1 * gt
        s_gp[...] += _colsum(dnm * mhat)
        u = dnm * gp
        dmo_ref[...] = (r * (u - mhat * jnp.mean(u * mhat, axis=-1, keepdims=True))).astype(BF16)

    row = pl.BlockSpec((tm, D), lambda i: (i, 0))
    vec_out = jax.ShapeDtypeStruct((1, D), F32)
    return pl.pallas_call(
        body, name="ffn_norm_bwd", grid=(T // tm,),
        out_shape=[jax.ShapeDtypeStruct((T, D), F32), jax.ShapeDtypeStruct((T, D), BF16)] + [vec_out] * 5,
        in_specs=[row, row, row, row] + [_vec(D)] * 4, out_specs=[row, row] + [_vec(D)] * 5,
        compiler_params=_cp("arbitrary"),
    )(dy, da2, h1, mo, g_pre2, sc2, gt1, g_post)


def out_proj_bwd(dmo, w_o, y_a, y_b, proj):
    T, D = dmo.shape
    tm, tn = min(512, T), 512
    gpb = 1024 // tn

    def body(a_ref, w_ref, ya_ref, yb_ref, ga_ref, gb_ref, dya_ref, dyb_ref, dga_ref, dgb_ref):
        dm = _nt(a_ref[...], w_ref[...])
        sa, sb = jax.nn.sigmoid(ga_ref[...]), jax.nn.sigmoid(gb_ref[...])
        dya_ref[...] = (dm * sa).astype(BF16)
        dyb_ref[...] = (dm * sb).astype(BF16)
        dga_ref[...] = (dm * ya_ref[...] * sa * (1.0 - sa)).astype(BF16)
        dgb_ref[...] = (dm * yb_ref[...] * sb * (1.0 - sb)).astype(BF16)

    out = pl.BlockSpec((tm, tn), lambda i, j: (i, j))
    return pl.pallas_call(
        body, name="out_proj_bwd", grid=(T // tm, D // tn), out_shape=[jax.ShapeDtypeStruct((T, D), BF16)] * 4,
        in_specs=[pl.BlockSpec((tm, D), lambda i, j: (i, 0)), pl.BlockSpec((tn, D), lambda i, j: (j, 0)), out, out,
                  pl.BlockSpec((tm, tn), lambda i, j: (i, COL_GA * gpb + j)), pl.BlockSpec((tm, tn), lambda i, j: (i, COL_GB * gpb + j))],
        out_specs=[out] * 4, compiler_params=_cp("parallel", "parallel"),
    )(dmo, w_o, y_a, y_b, proj, proj)


def sgu_bwd(proj, dsgu, g_norm, w_spatial, b_spatial_t):
    T = proj.shape[0]
    W = 1024
    gw = W // SGU_GROUPS

    def body(u_ref, v_ref, ds_ref, g_ref, ws_ref, bst_ref, dz_ref, dw_ref, db_ref, dg_ref):
        @pl.when(pl.program_id(0) == 0)
        def _():
            dw_ref[...] = jnp.zeros_like(dw_ref)
            db_ref[...] = jnp.zeros_like(db_ref)
            dg_ref[...] = jnp.zeros_like(dg_ref)

        zu, rstd, dhat, vn, vm = _sgu_mix(u_ref, v_ref, g_ref, ws_ref, bst_ref)
        ds = ds_ref[...]
        du = ds * vm
        dvm = ds * zu
        dvm_b = dvm.astype(BF16)
        ones = jnp.ones((8, gw), F32)
        dvn = []
        for g in range(SGU_GROUPS):
            sl = slice(g * gw, (g + 1) * gw)
            dw_ref[g] += _nt(dvm_b[:, sl], vn[:, sl])
            db_ref[g] += lax.dot_general(ones, dvm[:, sl], (((1,), (1,)), ((), ())), precision=HI, preferred_element_type=F32)
            dvn.append(_tn(ws_ref[g].astype(BF16), dvm_b[:, sl]))
        dvn = jnp.concatenate(dvn, axis=1)
        dg_ref[...] += _colsum(dvn * dhat)
        ddh = dvn * g_ref[...]
        dzv = rstd * (ddh - jnp.mean(ddh, axis=-1, keepdims=True) - dhat * jnp.mean(ddh * dhat, axis=-1, keepdims=True))
        dz_ref[:, 0:W] = (du * _gelu_grad(u_ref[...])).astype(BF16)
        dz_ref[:, W:2 * W] = (dzv * _gelu_grad(v_ref[...])).astype(BF16)

    blk = lambda cb: pl.BlockSpec((SGU_CHUNK, W), lambda i: (i, cb))
    full3 = lambda a, b, c: pl.BlockSpec((a, b, c), lambda i: (0, 0, 0))
    return pl.pallas_call(
        body, name="sgu_bwd", grid=(T // SGU_CHUNK,),
        out_shape=[jax.ShapeDtypeStruct((T, 2 * W), BF16), jax.ShapeDtypeStruct((SGU_GROUPS, SGU_CHUNK, SGU_CHUNK), F32),
                   jax.ShapeDtypeStruct((SGU_GROUPS, 8, SGU_CHUNK), F32), jax.ShapeDtypeStruct((1, W), F32)],
        in_specs=[blk(COL_U), blk(COL_ZV), blk(0), _vec(W), full3(SGU_GROUPS, SGU_CHUNK, SGU_CHUNK),
                  pl.BlockSpec((SGU_CHUNK, SGU_GROUPS), lambda i: (0, 0))],
        out_specs=[pl.BlockSpec((SGU_CHUNK, 2 * W), lambda i: (i, 0)), full3(SGU_GROUPS, SGU_CHUNK, SGU_CHUNK),
                   full3(SGU_GROUPS, 8, SGU_CHUNK), _vec(W)],
        compiler_params=_cp("arbitrary"),
    )(proj, proj, dsgu, g_norm, w_spatial, b_spatial_t)


def hgrn_post_bwd(dya, o, proj, g_norm):
    T, W = o.shape
    tm = min(256, T)

    def body(dy_ref, o_ref, og_ref, g_ref, do_ref, dog_ref, dg_ref):
        @pl.when(pl.program_id(0) == 0)
        def _():
            dg_ref[...] = jnp.zeros_like(dg_ref)

        g = g_ref[...]
        dg = jnp.zeros((1, HEAD_DIM), F32)
        for h in range(N_HEADS):
            sl = slice(h * HEAD_DIM, (h + 1) * HEAD_DIM)
            x, og, dy = o_ref[:, sl], og_ref[:, sl], dy_ref[:, sl]
            r = lax.rsqrt(jnp.mean(x * x, axis=-1, keepdims=True) + EPS)
            xhat = x * r
            s = jax.nn.sigmoid(og)
            don = dy * (og * s)
            dog_ref[:, sl] = (dy * (xhat * g) * (s * (1.0 + og * (1.0 - s)))).astype(BF16)
            dg += _colsum(don * xhat)
            u = don * g
            do_ref[:, sl] = r * (u - xhat * jnp.mean(u * xhat, axis=-1, keepdims=True))
        dg_ref[...] += dg

    row = pl.BlockSpec((tm, W), lambda i: (i, 0))
    return pl.pallas_call(
        body, name="hgrn_post_bwd", grid=(T // tm,),
        out_shape=[jax.ShapeDtypeStruct((T, W), F32), jax.ShapeDtypeStruct((T, W), BF16), jax.ShapeDtypeStruct((1, HEAD_DIM), F32)],
        in_specs=[row, row, pl.BlockSpec((tm, W), lambda i: (i, COL_OG)), _vec(HEAD_DIM)],
        out_specs=[row, row, _vec(HEAD_DIM)], compiler_params=_cp("arbitrary"),
    )(dya, o, proj, g_norm)


def hgrn_bwd(proj, do, lb_logits):
    T = proj.shape[0]
    NC, CPB = T // HGRN_CHUNK, HGRN_BLOCK // HGRN_CHUNK
    W = N_HEADS * HEAD_DIM
    col, f_spec, l_spec = _hgrn_specs(T)

    def body(l_ref, q_ref, f_ref, v_ref, do_ref, dq_ref, dv_ref, dlg_ref, dlb_ref, st_ref, dst_ref, dec_ref, ddec_ref, dqa_ref, dva_ref):
        d = pl.program_id(1)
        lb = _hgrn_lower_bound(l_ref)
        oml = 1.0 - lb
        mask, mtri, mtri_t, mone = _hgrn_chunk_masks(d)

        def values(rows):
            s, sn, fg, lf, k = _hgrn_gate(f_ref[rows, :], lb)
            b = jnp.dot(mtri, lf, precision=HI, preferred_element_type=F32)
            bl = jnp.dot(mone, lf, precision=HI, preferred_element_type=F32)
            eb, enb, ee = jnp.exp(b), jnp.exp(-b), jnp.exp(bl - b)
            qd = q_ref[rows, :] * Q_SCALE * eb
            return s, sn, fg, k, bl, eb, enb, ee, qd, k * enb, k * ee

        def block1(i, carry):
            rows = pl.ds(pl.multiple_of(i * HGRN_BLOCK, HGRN_BLOCK), HGRN_BLOCK)
            _, _, _, _, bl, _, _, _, qd, _, ke = values(rows)
            qd, ke = qd.astype(BF16), ke.astype(BF16)
            vb, dob = v_ref[rows, :].astype(BF16), do_ref[rows, :].astype(BF16)
            dec = jnp.exp(bl)
            for cc in range(CPB):
                sl = slice(cc * HGRN_CHUNK, (cc + 1) * HGRN_CHUNK)
                n = i * CPB + cc
                st_ref[n] = _tn(vb[sl], ke[sl])
                dst_ref[n] = _tn(dob[sl], qd[sl])
                dec_ref[n] = dec[cc * HGRN_CHUNK:cc * HGRN_CHUNK + 8, :]
            return carry

        lax.fori_loop(0, T // HGRN_BLOCK, block1, 0)

        def scan(t, s):
            n = jnp.where(d == 0, t, NC - 1 - t)
            u = st_ref[n]
            st_ref[n] = s
            return dec_ref[n][0:1, :] * s + u

        lax.fori_loop(0, NC, scan, jnp.zeros((HEAD_DIM, HEAD_DIM), F32))

        def rscan(t, ds):
            n = jnp.where(d == 0, NC - 1 - t, t)
            w = dst_ref[n]
            dst_ref[n] = ds
            ddec_ref[n] = jnp.broadcast_to(_colsum(ds * st_ref[n]), (8, HEAD_DIM))
            return dec_ref[n][0:1, :] * ds + w

        lax.fori_loop(0, NC, rscan, jnp.zeros((HEAD_DIM, HEAD_DIM), F32))

        def block3(i, dlb):
            rows = pl.ds(pl.multiple_of(i * HGRN_BLOCK, HGRN_BLOCK), HGRN_BLOCK)
            s, sn, fg, k, bl, eb, enb, ee, qd, kd, ke = values(rows)
            qdb, kdb, keb = qd.astype(BF16), kd.astype(BF16), ke.astype(BF16)
            vb, dob = v_ref[rows, :].astype(BF16), do_ref[rows, :].astype(BF16)
            att = jnp.where(mask, _nt(qdb, kdb), 0.0).astype(BF16)
            datt = jnp.where(mask, _nt(dob, vb), 0.0).astype(BF16)
            dv = _tn(att, dob)
            dqd = _nn(datt, kdb)
            dkd = _tn(datt, qdb)
            dv_i, dqd_i, dke, ddl = [], [], [], []
            for cc in range(CPB):
                sl = slice(cc * HGRN_CHUNK, (cc + 1) * HGRN_CHUNK)
                n = i * CPB + cc
                st_b, dst_b = st_ref[n].astype(BF16), dst_ref[n].astype(BF16)
                dv_i.append(_nt(keb[sl], dst_b))
                dqd_i.append(_nn(dob[sl], st_b))
                dke.append(_nn(vb[sl], dst_b))
                ddl.append(jnp.broadcast_to(ddec_ref[n][0:1, :] * dec_ref[n][0:1, :], (HGRN_CHUNK, HEAD_DIM)))
            dv = dv + jnp.concatenate(dv_i, axis=0)
            dqd = dqd + jnp.concatenate(dqd_i, axis=0)
            dke = jnp.concatenate(dke, axis=0)
            dq = dqd * eb * Q_SCALE
            dk = dkd * enb + dke * ee
            t_end = dke * ke
            db = dqd * qd - dkd * kd - t_end
            dlf = (jnp.dot(mtri_t, db, precision=HI, preferred_element_type=F32)
                   + jnp.dot(mone, t_end, precision=HI, preferred_element_type=F32) + jnp.concatenate(ddl, axis=0))
            e = dlf / fg - dk
            dlg_ref[rows, :] = (oml * e * s * sn).astype(BF16)

            @pl.when(d == 0)
            def _():
                dqa_ref[rows, :] = dq
                dva_ref[rows, :] = dv

            @pl.when(d == 1)
            def _():
                dq_ref[rows, :] = (dqa_ref[rows, :] + dq).astype(BF16)
                dv_ref[rows, :] = (dva_ref[rows, :] + dv).astype(BF16)

            return dlb + _colsum(e * sn)

        dlb_ref[...] = lax.fori_loop(0, T // HGRN_BLOCK, block3, jnp.zeros((1, HEAD_DIM), F32))

    head = pl.BlockSpec((T, HEAD_DIM), lambda h, d: (0, h))
    big = pltpu.VMEM((NC, HEAD_DIM, HEAD_DIM), F32)
    small = pltpu.VMEM((NC, 8, HEAD_DIM), F32)
    acc = pltpu.VMEM((T, HEAD_DIM), F32)
    return pl.pallas_call(
        body, name="hgrn_bwd", grid=(N_HEADS, 2),
        out_shape=[jax.ShapeDtypeStruct((T, W), BF16), jax.ShapeDtypeStruct((T, W), BF16), jax.ShapeDtypeStruct((T, 2 * W), BF16),
                   jax.ShapeDtypeStruct((2, 1, W), F32)],
        in_specs=[l_spec, col(COL_Q), f_spec, col(COL_V), head],
        out_specs=[head, head, pl.BlockSpec((T, HEAD_DIM), lambda h, d: (0, N_HEADS * d + h)),
                   pl.BlockSpec((None, 1, HEAD_DIM), lambda h, d: (d, 0, h))],
        scratch_shapes=[big, big, small, small, acc, acc], compiler_params=_cp("parallel", "arbitrary"),
    )(lb_logits, proj, proj, proj, do)


def mix_norm_bwd(da1, h0, dh1, g_pre, sc1):
    T, D = h0.shape
    tm = min(256, T)

    def body(da_ref, h_ref, dh_ref, g_ref, sc_ref, gx_ref, s_sh, s_sc, s_g):
        @pl.when(pl.program_id(0) == 0)
        def _():
            for s in (s_sh, s_sc, s_g):
                s[...] = jnp.zeros_like(s)

        h, da = h_ref[...], da_ref[...]
        g, sc = g_ref[...], sc_ref[...]
        r = lax.rsqrt(jnp.mean(h * h, axis=-1, keepdims=True) + EPS)
        n = h * r
        s_sh[...] += _colsum(da)
        s_sc[...] += _colsum(da * (n * g))
        s_g[...] += _colsum(da * (1.0 + sc) * n)
        dn = da * g * (1.0 + sc)
        gx_ref[...] = dh_ref[...] + r * (dn - n * jnp.mean(dn * n, axis=-1, keepdims=True))

    row = pl.BlockSpec((tm, D), lambda i: (i, 0))
    return pl.pallas_call(
        body, name="mix_norm_bwd", grid=(T // tm,),
        out_shape=[jax.ShapeDtypeStruct((T, D), F32)] + [jax.ShapeDtypeStruct((1, D), F32)] * 3,
        in_specs=[row, row, row, _vec(D), _vec(D)], out_specs=[row] + [_vec(D)] * 3, compiler_params=_cp("arbitrary"),
    )(da1, h0, dh1, g_pre, sc1)


def adamw(w, g, m, v, name):
    R, C = w.shape
    tr = R if R * C * 4 <= (1 << 21) else max(8, ((1 << 21) // (C * 4)) // 8 * 8)
    while R % tr:
        tr -= 8

    def body(w_ref, g_ref, m_ref, v_ref, d_ref, m2_ref, v2_ref):
        d_ref[...], m2_ref[...], v2_ref[...] = _adamw(w_ref[...], g_ref[...], m_ref[...], v_ref[...])

    row = pl.BlockSpec((tr, C), lambda i: (i, 0))
    return pl.pallas_call(
        body, name=name, grid=(R // tr,), out_shape=[jax.ShapeDtypeStruct((R, C), F32)] * 3,
        in_specs=[row] * 4, out_specs=[row] * 3, compiler_params=_cp("parallel"),
    )(w, g, m, v)


def wada_update(c_all, dmod, w, m, v):
    D, N = w.shape
    tm, tn = 512, 1024

    def body(c_ref, dm_ref, w_ref, m_ref, v_ref, g_ref, d_ref, m2_ref, v2_ref):
        c = c_ref[...]
        g = lax.dot_general(c * jax.nn.sigmoid(c), dm_ref[...], (((0,), (0,)), ((), ())), precision=HI, preferred_element_type=F32)
        g_ref[...] = g
        d_ref[...], m2_ref[...], v2_ref[...] = _adamw(w_ref[...], g, m_ref[...], v_ref[...])

    blk = pl.BlockSpec((tm, tn), lambda i, j: (i, j))
    return pl.pallas_call(
        body, name="wada_update", grid=(D // tm, N // tn), out_shape=[jax.ShapeDtypeStruct((D, N), F32)] * 4,
        in_specs=[pl.BlockSpec((8, tm), lambda i, j: (0, i)), pl.BlockSpec((8, tn), lambda i, j: (0, j)), blk, blk, blk],
        out_specs=[blk] * 4, compiler_params=_cp("parallel", "parallel"),
    )(c_all, dmod, w, m, v)


def sum_devices(gathered):
    n, R, C = gathered.shape

    def body(g_ref, o_ref):
        s = g_ref[0]
        for i in range(1, n):
            s = s + g_ref[i]
        o_ref[...] = s

    return pl.pallas_call(body, name="sum_devices", out_shape=jax.ShapeDtypeStruct((R, C), F32), compiler_params=_cp())(gathered)


def lb_logits_grad(dlb, lb_logits):
    def body(d_ref, l_ref, o_ref):
        for d in range(2):
            l0, l1 = l_ref[d, 0:1, :], l_ref[d, 1:2, :]
            m = jnp.maximum(l0, l1)
            e0, e1 = jnp.exp(l0 - m), jnp.exp(l1 - m)
            p0, p1 = e0 / (e0 + e1), e1 / (e0 + e1)
            g = d_ref[d:d + 1, :]
            o_ref[d, 0:1, :] = p0 * (g - p0 * g)
            o_ref[d, 1:2, :] = -p1 * (p0 * g)

    return pl.pallas_call(body, name="lb_logits_grad", out_shape=jax.ShapeDtypeStruct(lb_logits.shape, F32), compiler_params=_cp())(dlb, lb_logits)


def add_halves(g, landed, core):
    nj, _, r, cc = g.shape
    tr = min(256, r)

    def body(core_ref, g_ref, l_ref, o_ref):
        o_ref[...] = (g_ref[...].astype(F32) + l_ref[...].astype(F32)).astype(BF16)

    return pl.pallas_call(
        body, name="add_halves_%dx%d" % (r, cc), out_shape=jax.ShapeDtypeStruct((nj, r, cc), BF16),
        grid_spec=pltpu.PrefetchScalarGridSpec(
            num_scalar_prefetch=1, grid=(nj, r // tr),
            in_specs=[pl.BlockSpec((None, None, tr, cc), lambda j, i, core_ref: (j, core_ref[0], i, 0)),
                      pl.BlockSpec((None, None, tr, cc), lambda j, i, core_ref: (j, 0, i, 0))],
            out_specs=pl.BlockSpec((None, tr, cc), lambda j, i, core_ref: (j, i, 0))),
        compiler_params=_cp("parallel", "parallel"),
    )(core, g, landed)


def sum_chips(parts):
    nj, r, cc = parts.shape
    tr = min(256, r)

    def body(p_ref, o_ref):
        s = p_ref[0].astype(F32)
        for j in range(1, nj):
            s = s + p_ref[j].astype(F32)
        o_ref[...] = s

    return pl.pallas_call(
        body, name="sum_chips_%dx%d" % (r, cc), out_shape=jax.ShapeDtypeStruct((r, cc), F32), grid=(r // tr,),
        in_specs=[pl.BlockSpec((nj, tr, cc), lambda i: (0, i, 0))], out_specs=pl.BlockSpec((tr, cc), lambda i: (i, 0)),
        compiler_params=_cp("parallel"),
    )(parts)


def _place():
    mx, my, mc = lax.axis_index("x"), lax.axis_index("y"), lax.axis_index("c")
    chips = [(1 - mx, my), (mx, 1 - my), (1 - mx, 1 - my)]
    return mx, my, mc, chips


def all_gather_small(x, name):
    R, C = x.shape

    def body(x_ref, out_ref, send_sems, recv_sems, local_sem):
        mx, my, mc, _ = _place()
        me = 4 * mx + 2 * my + mc
        mine = pltpu.make_async_copy(x_ref, out_ref.at[me], local_sem)
        mine.start()

        def peer(k):
            px = 1 - mx if k & 4 else mx
            py = 1 - my if k & 2 else my
            pc = 1 - mc if k & 1 else mc
            return px, py, pc

        def copy(k, src, slot):
            return pltpu.make_async_remote_copy(src_ref=src, dst_ref=out_ref.at[slot], send_sem=send_sems.at[k - 1],
                                                recv_sem=recv_sems.at[k - 1], device_id=peer(k), device_id_type=MESH)

        sends = [copy(k, x_ref, me) for k in range(1, 8)]
        for cp in sends:
            cp.start()
        for k in range(1, 8):
            px, py, pc = peer(k)
            slot = 4 * px + 2 * py + pc
            copy(k, out_ref.at[slot], slot).wait_recv()
        for cp in sends:
            cp.wait_send()
        mine.wait()

    return pl.pallas_call(
        body, name=name, out_shape=jax.ShapeDtypeStruct((8, R, C), F32),
        in_specs=[pl.BlockSpec(memory_space=pltpu.VMEM)], out_specs=pl.BlockSpec(memory_space=pltpu.VMEM),
        scratch_shapes=[pltpu.SemaphoreType.DMA((7,)), pltpu.SemaphoreType.DMA((7,)), pltpu.SemaphoreType.DMA],
        compiler_params=_cp(),
    )(x)


def _region(ref, kind, j, half, r, cc):
    nr = r if half is None else r // 2
    off = 0 if half is None else half * nr
    if kind == "col":
        return ref.at[pl.ds(off, nr), pl.ds(pl.multiple_of(j * cc, 128), cc)]
    return ref.at[pl.ds(pl.multiple_of(j * r + off, 16), nr), :]


def gather_weights(shards, kinds):
    n = len(shards)
    dims = [s.shape for s in shards]
    full_shapes = [(r, 4 * cc) if kind == "col" else (4 * r, cc) for (r, cc), kind in zip(dims, kinds)]

    def body(*refs):
        s_refs, f_refs = refs[:n], refs[n:2 * n]
        send_sems, recv_sems, local_sems = refs[2 * n:]
        mx, my, mc, chips = _place()
        jme = 2 * mx + my
        sibling = (mx, my, 1 - mc)
        local = []
        for w in range(n):
            cp = pltpu.make_async_copy(s_refs[w], _region(f_refs[w], kinds[w], jme, None, *dims[w]), local_sems.at[w])
            cp.start()
            local.append(cp)

        def landed(w, k, half):
            px, py = chips[k]
            return _region(f_refs[w], kinds[w], 2 * px + py, half, *dims[w])

        def over_ici(w, k, src, dst):
            px, py = chips[k]
            return pltpu.make_async_remote_copy(src_ref=src, dst_ref=dst, send_sem=send_sems.at[6 * w + k], recv_sem=recv_sems.at[6 * w + k],
                                                device_id=(px, py, mc), device_id_type=MESH)

        def over_d2d(w, k, half):
            reg = landed(w, k, half)
            return pltpu.make_async_remote_copy(src_ref=reg, dst_ref=reg, send_sem=send_sems.at[6 * w + 3 + k],
                                                recv_sem=recv_sems.at[6 * w + 3 + k], device_id=sibling, device_id_type=MESH)

        sends = []
        for w in range(n):
            r = dims[w][0]
            mine = s_refs[w].at[pl.ds(pl.multiple_of(mc * (r // 2), 16), r // 2), :]
            for k in range(3):
                sends.append(over_ici(w, k, mine, _region(f_refs[w], kinds[w], jme, mc, *dims[w])))
        for cp in sends:
            cp.start()
        passed = []
        for w in range(n):
            for k in range(3):
                over_ici(w, k, landed(w, k, mc), landed(w, k, mc)).wait_recv()
                cp = over_d2d(w, k, mc)
                cp.start()
                passed.append(cp)
        for w in range(n):
            for k in range(3):
                over_d2d(w, k, 1 - mc).wait_recv()
        for cp in sends + passed:
            cp.wait_send()
        for cp in local:
            cp.wait()

    return pl.pallas_call(
        body, name="gather_weights", out_shape=[jax.ShapeDtypeStruct(s, BF16) for s in full_shapes],
        in_specs=[ANY] * n, out_specs=[ANY] * n,
        scratch_shapes=[pltpu.SemaphoreType.DMA((6 * n,)), pltpu.SemaphoreType.DMA((6 * n,)), pltpu.SemaphoreType.DMA((n,))],
        compiler_params=_cp(),
    )(*shards)


def exchange_halves(grads):
    n = len(grads)

    def body(*refs):
        g_refs, l_refs = refs[:n], refs[n:2 * n]
        send_sems, recv_sems = refs[2 * n:]
        mx, my, mc, _ = _place()
        cps = [pltpu.make_async_remote_copy(src_ref=g_refs[w].at[:, pl.ds(1 - mc, 1)], dst_ref=l_refs[w], send_sem=send_sems.at[w],
                                            recv_sem=recv_sems.at[w], device_id=(mx, my, 1 - mc), device_id_type=MESH) for w in range(n)]
        for cp in cps:
            cp.start()
        for cp in cps:
            cp.wait()

    return pl.pallas_call(
        body, name="exchange_halves", out_shape=[jax.ShapeDtypeStruct((g.shape[0], 1) + g.shape[2:], BF16) for g in grads],
        in_specs=[ANY] * n, out_specs=[ANY] * n,
        scratch_shapes=[pltpu.SemaphoreType.DMA((n,)), pltpu.SemaphoreType.DMA((n,))], compiler_params=_cp(),
    )(*grads)


def scatter_to_chips(parts):
    n = len(parts)

    def body(*refs):
        p_refs, l_refs = refs[:n], refs[n:2 * n]
        send_sems, recv_sems, local_sems = refs[2 * n:]
        mx, my, mc, chips = _place()
        jme = 2 * mx + my
        local, sends = [], []
        for w in range(n):
            cp = pltpu.make_async_copy(p_refs[w].at[jme], l_refs[w].at[jme], local_sems.at[w])
            cp.start()
            local.append(cp)
            for k, (px, py) in enumerate(chips):
                sends.append(pltpu.make_async_remote_copy(src_ref=p_refs[w].at[2 * px + py], dst_ref=l_refs[w].at[jme],
                                                          send_sem=send_sems.at[3 * w + k], recv_sem=recv_sems.at[3 * w + k],
                                                          device_id=(px, py, mc), device_id_type=MESH))
        for cp in sends:
            cp.start()
        for w in range(n):
            for k, (px, py) in enumerate(chips):
                slot = l_refs[w].at[2 * px + py]
                pltpu.make_async_remote_copy(src_ref=slot, dst_ref=slot, send_sem=send_sems.at[3 * w + k], recv_sem=recv_sems.at[3 * w + k],
                                             device_id=(px, py, mc), device_id_type=MESH).wait_recv()
        for cp in sends:
            cp.wait_send()
        for cp in local:
            cp.wait()

    return pl.pallas_call(
        body, name="scatter_to_chips", out_shape=[jax.ShapeDtypeStruct(p.shape, BF16) for p in parts],
        in_specs=[ANY] * n, out_specs=[ANY] * n,
        scratch_shapes=[pltpu.SemaphoreType.DMA((3 * n,)), pltpu.SemaphoreType.DMA((3 * n,)), pltpu.SemaphoreType.DMA((n,))],
        compiler_params=_cp(),
    )(*parts)


def share_with_sibling(sums):
    n = len(sums)

    def body(*refs):
        q_refs, o_refs = refs[:n], refs[n:2 * n]
        send_sems, recv_sems, local_sems = refs[2 * n:]
        mx, my, mc, _ = _place()
        local, sends = [], []
        for w in range(n):
            cp = pltpu.make_async_copy(q_refs[w], o_refs[w].at[mc], local_sems.at[w])
            cp.start()
            local.append(cp)
            sends.append(pltpu.make_async_remote_copy(src_ref=q_refs[w], dst_ref=o_refs[w].at[mc], send_sem=send_sems.at[w],
                                                      recv_sem=recv_sems.at[w], device_id=(mx, my, 1 - mc), device_id_type=MESH))
        for cp in sends:
            cp.start()
        for w in range(n):
            slot = o_refs[w].at[1 - mc]
            pltpu.make_async_remote_copy(src_ref=slot, dst_ref=slot, send_sem=send_sems.at[w], recv_sem=recv_sems.at[w],
                                         device_id=(mx, my, 1 - mc), device_id_type=MESH).wait_recv()
        for cp in sends:
            cp.wait_send()
        for cp in local:
            cp.wait()

    return pl.pallas_call(
        body, name="share_with_sibling", out_shape=[jax.ShapeDtypeStruct((2,) + q.shape, F32) for q in sums],
        in_specs=[ANY] * n, out_specs=[ANY] * n,
        scratch_shapes=[pltpu.SemaphoreType.DMA((n,)), pltpu.SemaphoreType.DMA((n,)), pltpu.SemaphoreType.DMA((n,))],
        compiler_params=_cp(),
    )(*sums)


def _pack(arrays):
    flat = jnp.concatenate([a.reshape(-1) for a in arrays])
    rows = -(-flat.shape[0] // 1024) * 8
    return jnp.pad(flat, (0, rows * 128 - flat.shape[0])).reshape(rows, 128)


def _unpack(packed, shapes):
    flat, out, off = packed.reshape(-1), [], 0
    for s in shapes:
        n = math.prod(s)
        out.append(flat[off:off + n].reshape(s))
        off += n
    return out


def kernel(x, c, w_ada, b_ada, g_pre_mix, g_post_mix, g_pre_ffn, g_post_ffn, w_in, lb_logits, g_hgrn_norm, w_a_out, g_sgu_norm, w_spatial, b_spatial, w_b_out, w_o, w_ff1, w_ff2, loss_target, m_w_ada, m_b_ada, m_g_pre_mix, m_g_post_mix, m_g_pre_ffn, m_g_post_ffn, m_w_in, m_lb_logits, m_g_hgrn_norm, m_w_a_out, m_g_sgu_norm, m_w_spatial, m_b_spatial, m_w_b_out, m_w_o, m_w_ff1, m_w_ff2, v_w_ada, v_b_ada, v_g_pre_mix, v_g_post_mix, v_g_pre_ffn, v_g_post_ffn, v_w_in, v_lb_logits, v_g_hgrn_norm, v_w_a_out, v_g_sgu_norm, v_w_spatial, v_b_spatial, v_w_b_out, v_w_o, v_w_ff1, v_w_ff2):
    mx, my, mc = lax.axis_index("x"), lax.axis_index("y"), lax.axis_index("c")
    chip, me = 2 * mx + my, 4 * mx + 2 * my + mc
    D = D_MODEL
    h0, tgt = x[0], loss_target[0]
    n_ada = w_ada.shape[2]
    n_lb = lb_logits.shape[2]

    got = all_gather_small(_pack([c, lb_logits]), "gather_inputs")
    c_all = got[:, :D // 128, :].reshape(8, D)
    lb_full = got[0::2, D // 128:D // 128 + 4 * n_lb // 128, :].reshape(4, 2, 2, n_lb).transpose(1, 2, 0, 3).reshape(2, 2, 4 * n_lb)
    b_ada_chip = lax.dynamic_slice(b_ada, (0, chip * n_ada), (1, n_ada))
    mod_cols = mod_matmul(c_all, w_ada[0], b_ada_chip)
    got = all_gather_small(mod_cols.reshape(-1, 128), "gather_mod").reshape(4, 2, 8, n_ada)
    mod = lax.dynamic_index_in_dim(got[:, 0], me, axis=1, keepdims=False).reshape(6, 1, D)
    sh1, sc1, gt1, sh2, sc2, gt2 = (mod[i] for i in range(6))

    big = [("w_in", w_in, "col"), ("w_a_out", w_a_out, "col"), ("w_b_out", w_b_out, "col"), ("w_o", w_o, "row"),
           ("w_ff1", w_ff1, "col"), ("w_ff2", w_ff2, "row")]
    kinds = [k for _, _, k in big]
    w_in_f, w_a_f, w_b_f, w_o_f, w_ff1_f, w_ff2_f = gather_weights([cast_bf16(w[0], "cast_" + nm) for nm, w, _ in big], kinds)

    bst = b_spatial[0].T
    proj, a1 = prenorm_matmul(h0, g_pre_mix, sc1, sh1, w_in_f, relu2=False, name="in_proj")
    o = hgrn_fwd(proj, lb_full)
    ya_pre = hgrn_post_fwd(o, proj, g_hgrn_norm)
    sgu = sgu_fwd(proj, g_sgu_norm, w_spatial[0], bst)
    y_a, y_b, merged = merge_matmul(ya_pre, sgu, w_a_f, w_b_f, proj)
    mo, h1 = out_proj(merged, w_o_f, h0, gt1, g_post_mix)
    f1, a2, hid = prenorm_matmul(h1, g_pre_ffn, sc2, sh2, w_ff1_f, relu2=True, name="ff1")
    dy, dff, loss_parts, d_gt2, d_g_post_ffn = ff2_loss(hid, w_ff2_f, h1, tgt, gt2, g_post_ffn)
    loss = lax.psum(0.5 * loss_parts[0, 0] / D, ("x", "y", "c"))

    df1 = ff2_bwd(dff, w_ff2_f, f1)
    gr_ff2 = matmul(hid, dff, mode="tn", out_dtype=BF16, tm=1024, tn=1024, tk=512, name="dw_ff2")
    da2 = matmul(df1, w_ff1_f, mode="nt", out_dtype=F32, tm=512, tn=1024, tk=2048, name="da2")
    gr_ff1 = matmul(a2, df1, mode="tn", out_dtype=BF16, tm=1024, tn=2048, tk=512, name="dw_ff1", split=(4, 2))
    dh1, dmo, d_sh2, d_sc2, d_g_pre_ffn, d_gt1, d_g_post_mix = ffn_norm_bwd(dy, da2, h1, mo, g_pre_ffn, sc2, gt1, g_post_mix)
    dya, dyb, dga, dgb = out_proj_bwd(dmo, w_o_f, y_a, y_b, proj)
    gr_o = matmul(merged, dmo, mode="tn", out_dtype=BF16, tm=1024, tn=1024, tk=512, name="dw_o")
    dsgu = matmul(dyb, w_b_f, mode="nt", out_dtype=F32, tm=512, tn=1024, tk=2048, name="dsgu")
    gr_b = matmul(sgu, dyb, mode="tn", out_dtype=BF16, tm=512, tn=512, tk=512, name="dw_b_out", split=(4, 2))
    dz, d_w_spatial, d_b_spatial, d_g_sgu = sgu_bwd(proj, dsgu, g_sgu_norm, w_spatial[0], bst)
    dya_pre = matmul(dya, w_a_f, mode="nt", out_dtype=F32, tm=512, tn=1024, tk=2048, name="dya_pre")
    gr_a = matmul(ya_pre, dya, mode="tn", out_dtype=BF16, tm=512, tn=512, tk=512, name="dw_a_out", split=(4, 2))
    do, dog, d_g_hgrn = hgrn_post_bwd(dya_pre, o, proj, g_hgrn_norm)
    dq, dv, dlg, d_lb = hgrn_bwd(proj, do, lb_full)
    dproj = jnp.concatenate([dq, dlg, dv, dog, dz, dga, dgb], axis=1)
    da1 = matmul(dproj, w_in_f, mode="nt", out_dtype=F32, tm=512, tn=1024, tk=2816, name="da1")
    gr_in = matmul(a1, dproj, mode="tn", out_dtype=BF16, tm=1024, tn=2816, tk=512, name="dw_in", split=(4, 2))
    grad_x, d_sh1, d_sc1, d_g_pre_mix = mix_norm_bwd(da1, h0, dh1, g_pre_mix, sc1)

    core = mc.reshape(1).astype(jnp.int32)
    grads = [gr_in, gr_a, gr_b, gr_o.reshape(4, 2, -1, D), gr_ff1, gr_ff2.reshape(4, 2, -1, D)]
    landed = exchange_halves(grads)
    parts = scatter_to_chips([add_halves(g, l, core) for g, l in zip(grads, landed)])
    reduced = share_with_sibling([sum_chips(p) for p in parts])
    out = {}
    for (nm, w, _), g, m, v in zip(big, reduced, (m_w_in, m_w_a_out, m_w_b_out, m_w_o, m_w_ff1, m_w_ff2),
                                   (v_w_in, v_w_a_out, v_w_b_out, v_w_o, v_w_ff1, v_w_ff2)):
        g = g.reshape(w.shape[1:])
        out[nm] = tuple(a[None] for a in (g,) + tuple(adamw(w[0], g, m[0], v[0], "adamw_" + nm)))

    mine = _pack([d_sh1, d_sc1, d_gt1, d_sh2, d_sc2, d_gt2, d_g_pre_mix, d_g_post_mix, d_g_pre_ffn, d_g_post_ffn, d_g_hgrn, d_g_sgu,
                  d_w_spatial, d_b_spatial[:, 0, :], d_lb])
    got = all_gather_small(mine, "gather_small_grads")
    total = sum_devices(got)
    g_b_ada, g_g1, g_g2, g_g3, g_g4, g_hg, g_sg, g_ws, g_bs, g_lb = _unpack(
        total, [(1, 6 * D), (1, D), (1, D), (1, D), (1, D), (1, HEAD_DIM), (1, 1024), w_spatial.shape, b_spatial.shape, (2, 1024)])
    g_lbl = lax.dynamic_slice(lb_logits_grad(g_lb, lb_full), (0, 0, chip * n_lb), (2, 2, n_lb))
    names = ["b_ada", "g_pre_mix", "g_post_mix", "g_pre_ffn", "g_post_ffn", "g_hgrn_norm", "g_sgu_norm", "w_spatial", "b_spatial", "lb_logits"]
    ws = [b_ada, g_pre_mix, g_post_mix, g_pre_ffn, g_post_ffn, g_hgrn_norm, g_sgu_norm, w_spatial, b_spatial, lb_logits]
    gs = [g_b_ada, g_g1, g_g2, g_g3, g_g4, g_hg, g_sg, g_ws, g_bs, g_lbl]
    ms = [m_b_ada, m_g_pre_mix, m_g_post_mix, m_g_pre_ffn, m_g_post_ffn, m_g_hgrn_norm, m_g_sgu_norm, m_w_spatial, m_b_spatial, m_lb_logits]
    vs = [v_b_ada, v_g_pre_mix, v_g_post_mix, v_g_pre_ffn, v_g_post_ffn, v_g_hgrn_norm, v_g_sgu_norm, v_w_spatial, v_b_spatial, v_lb_logits]
    shapes = [w.shape for w in ws]
    upd = adamw(_pack(ws), _pack(gs), _pack(ms), _pack(vs), "adamw_small")
    upd = [_unpack(u, shapes) for u in upd]
    for i, nm in enumerate(names):
        out[nm] = (gs[i], upd[0][i], upd[1][i], upd[2][i])

    dmod_all = got[:, :6 * D // 128, :].reshape(8, 6 * D)
    dmod_chip = lax.dynamic_slice(dmod_all, (0, chip * n_ada), (8, n_ada))
    out["w_ada"] = tuple(a[None] for a in wada_update(c_all, dmod_chip, w_ada[0], m_w_ada[0], v_w_ada[0]))

    order = ["w_ada", "b_ada", "g_pre_mix", "g_post_mix", "g_pre_ffn", "g_post_ffn", "w_in", "lb_logits", "g_hgrn_norm", "w_a_out",
             "g_sgu_norm", "w_spatial", "b_spatial", "w_b_out", "w_o", "w_ff1", "w_ff2"]
    return (loss, grad_x[None], *[out[nm][0] for nm in order], *[out[nm][1] for nm in order], *[out[nm][2] for nm in order],
            *[out[nm][3] for nm in order])
```

```python
import functools
import math

import jax
import jax.numpy as jnp
from jax import lax
from jax.experimental import pallas as pl
from jax.experimental.pallas import tpu as pltpu

F32, BF16 = jnp.float32, jnp.bfloat16
HI = lax.Precision.HIGHEST
MESH = pl.DeviceIdType.MESH
ANY = pl.BlockSpec(memory_space=pl.ANY)

EPS = 1e-6
D_MODEL = 2048
N_HEADS = 8
HEAD_DIM = 128
HGRN_CHUNK = 32
HGRN_BLOCK = 256
SGU_CHUNK = 128
SGU_GROUPS = 8
Q_SCALE = HEAD_DIM ** -0.5
COL_Q, COL_FFW, COL_FBW, COL_V, COL_OG, COL_U, COL_ZV, COL_GA, COL_GB = 0, 1, 2, 3, 4, 5, 6, 7, 9
N_PROJ = 11264
VMEM_BYTES_V7X = 64 * 1024 * 1024
VMEM_LIMIT = VMEM_BYTES_V7X - 8 * 1024 * 1024

ADAM_LR, ADAM_B1, ADAM_B2, ADAM_EPS, ADAM_WD, ADAM_STEP = 0.001, 0.9, 0.999, 1e-08, 0.01, 10
ADAM_C1 = 1.0 - ADAM_B1 ** ADAM_STEP
ADAM_C2 = 1.0 - ADAM_B2 ** ADAM_STEP


def _cp(*sem):
    return pltpu.CompilerParams(dimension_semantics=sem if sem else None, vmem_limit_bytes=VMEM_LIMIT)


def _vec(d):
    return pl.BlockSpec((1, d), lambda *_: (0, 0))


def _colsum(x):
    return jnp.sum(x, axis=0, keepdims=True)


def _nt(a, b):
    return lax.dot_general(a, b, (((1,), (1,)), ((), ())), preferred_element_type=F32)


def _tn(a, b):
    return lax.dot_general(a, b, (((0,), (0,)), ((), ())), preferred_element_type=F32)


def _nn(a, b):
    return jnp.dot(a, b, preferred_element_type=F32)


def _adamw(w, g, m, v):
    m2 = ADAM_B1 * m + (1.0 - ADAM_B1) * g
    v2 = ADAM_B2 * v + (1.0 - ADAM_B2) * (g * g)
    delta = -ADAM_LR * ((m2 / ADAM_C1) / (jnp.sqrt(v2 / ADAM_C2) + ADAM_EPS) + ADAM_WD * w)
    return delta, m2, v2


def matmul(a, b, *, mode, out_dtype, tm, tn, tk, name, split=None):
    if mode == "tn":
        (K, M), (_, N) = a.shape, b.shape
    elif mode == "nt":
        (M, K), (N, _) = a.shape, b.shape
    else:
        (M, K), (_, N) = a.shape, b.shape
    tm, tn, tk = min(tm, M), min(tn, N), min(tk, K)
    nk = K // tk
    a_spec = pl.BlockSpec((tk, tm), lambda i, j, k: (k, i)) if mode == "tn" else pl.BlockSpec((tm, tk), lambda i, j, k: (i, k))
    b_spec = pl.BlockSpec((tn, tk), lambda i, j, k: (j, k)) if mode == "nt" else pl.BlockSpec((tk, tn), lambda i, j, k: (k, j))
    dot = {"nn": _nn, "nt": _nt, "tn": _tn}[mode]
    if split is None:
        out_shape = jax.ShapeDtypeStruct((M, N), out_dtype)
        out_spec = pl.BlockSpec((tm, tn), lambda i, j, k: (i, j))
    else:
        nj, nh = split
        rows, cols = M // nh, N // nj
        tm, tn = min(tm, rows), min(tn, cols)
        bi, bj = rows // tm, cols // tn
        out_shape = jax.ShapeDtypeStruct((nj, nh, rows, cols), out_dtype)
        out_spec = pl.BlockSpec((None, None, tm, tn), lambda i, j, k: (j // bj, i // bi, i % bi, j % bj))

    def body(a_ref, b_ref, o_ref, acc_ref):
        k = pl.program_id(2)

        @pl.when(k == 0)
        def _():
            acc_ref[...] = jnp.zeros_like(acc_ref)

        acc_ref[...] += dot(a_ref[...], b_ref[...])

        @pl.when(k == nk - 1)
        def _():
            o_ref[...] = acc_ref[...].astype(o_ref.dtype)

    return pl.pallas_call(
        body, name=name, out_shape=out_shape, grid=(M // tm, N // tn, nk),
        in_specs=[a_spec, b_spec], out_specs=out_spec, scratch_shapes=[pltpu.VMEM((tm, tn), F32)],
        compiler_params=_cp("parallel", "parallel", "arbitrary"),
    )(a, b)


def cast_into_full(w, kind, chip, name):
    r, cc = w.shape
    tr = min(r, 512)
    nb = r // tr

    def body(chip_ref, w_ref, o_ref):
        o_ref[...] = w_ref[...].astype(BF16)

    if kind == "col":
        full, out_map = (r, 4 * cc), lambda i, chip_ref: (i, chip_ref[0])
    else:
        full, out_map = (4 * r, cc), lambda i, chip_ref: (chip_ref[0] * nb + i, 0)
    return pl.pallas_call(
        body, name=name, out_shape=jax.ShapeDtypeStruct(full, BF16),
        grid_spec=pltpu.PrefetchScalarGridSpec(
            num_scalar_prefetch=1, grid=(nb,), in_specs=[pl.BlockSpec((tr, cc), lambda i, chip_ref: (i, 0))],
            out_specs=pl.BlockSpec((tr, cc), out_map)),
        compiler_params=_cp("parallel"),
    )(chip, w)


def mod_matmul(c_all, w_ada, b_ada):
    D, N = w_ada.shape
    tn = 1024

    def body(c_ref, w_ref, b_ref, o_ref):
        c = c_ref[...]
        sc = c * jax.nn.sigmoid(c)
        o_ref[...] = jnp.dot(sc, w_ref[...], precision=HI, preferred_element_type=F32) + b_ref[...]

    return pl.pallas_call(
        body, name="mod_matmul", out_shape=jax.ShapeDtypeStruct((8, N), F32), grid=(N // tn,),
        in_specs=[pl.BlockSpec((8, D), lambda j: (0, 0)), pl.BlockSpec((D, tn), lambda j: (0, j)),
                  pl.BlockSpec((1, tn), lambda j: (0, j))],
        out_specs=pl.BlockSpec((8, tn), lambda j: (0, j)), compiler_params=_cp("parallel"),
    )(c_all, w_ada, b_ada)


def prenorm_matmul(h, g, sc, sh, w, *, relu2, name):
    T, D = h.shape
    N = w.shape[1]
    tm, tn = min(512, T), 1024

    def body(h_ref, g_ref, sc_ref, sh_ref, w_ref, y_ref, a_ref, *hid_ref):
        @pl.when(pl.program_id(1) == 0)
        def _():
            x = h_ref[...]
            r = lax.rsqrt(jnp.mean(x * x, axis=-1, keepdims=True) + EPS)
            a_ref[...] = ((x * r) * g_ref[...] * (1.0 + sc_ref[...]) + sh_ref[...]).astype(BF16)

        y = _nn(a_ref[...], w_ref[...])
        y_ref[...] = y
        if relu2:
            p = jnp.maximum(y, 0.0)
            hid_ref[0][...] = (p * p).astype(BF16)

    out_shape = [jax.ShapeDtypeStruct((T, N), F32), jax.ShapeDtypeStruct((T, D), BF16)]
    out_specs = [pl.BlockSpec((tm, tn), lambda i, j: (i, j)), pl.BlockSpec((tm, D), lambda i, j: (i, 0))]
    if relu2:
        out_shape.append(jax.ShapeDtypeStruct((T, N), BF16))
        out_specs.append(pl.BlockSpec((tm, tn), lambda i, j: (i, j)))
    return pl.pallas_call(
        body, name=name, out_shape=out_shape, grid=(T // tm, N // tn),
        in_specs=[pl.BlockSpec((tm, D), lambda i, j: (i, 0)), _vec(D), _vec(D), _vec(D),
                  pl.BlockSpec((D, tn), lambda i, j: (0, j))],
        out_specs=out_specs, compiler_params=_cp("parallel", "arbitrary"),
    )(h, g, sc, sh, w)


def _hgrn_lower_bound(l_ref):
    l0, l1 = l_ref[0:1, :], l_ref[1:2, :]
    m = jnp.maximum(l0, l1)
    e0, e1 = jnp.exp(l0 - m), jnp.exp(l1 - m)
    return e0 / (e0 + e1)


def _hgrn_chunk_masks(d):
    r = lax.broadcasted_iota(jnp.int32, (HGRN_BLOCK, HGRN_BLOCK), 0)
    c = lax.broadcasted_iota(jnp.int32, (HGRN_BLOCK, HGRN_BLOCK), 1)
    same = (r // HGRN_CHUNK) == (c // HGRN_CHUNK)
    fwd = d == 0
    tri = same & (((c <= r) & fwd) | ((c >= r) & jnp.logical_not(fwd)))
    tri_t = same & (((c >= r) & fwd) | ((c <= r) & jnp.logical_not(fwd)))
    one = jnp.where(same, 1.0, 0.0).astype(F32)
    return tri, jnp.where(tri, 1.0, 0.0).astype(F32), jnp.where(tri_t, 1.0, 0.0).astype(F32), one


def _hgrn_gate(f, lb):
    s = jax.nn.sigmoid(f)
    sn = jax.nn.sigmoid(-f)
    fg = lb + (1.0 - lb) * s
    return s, sn, fg, jnp.log(fg), (1.0 - lb) * sn


def _hgrn_specs(T):
    col = lambda base: pl.BlockSpec((T, HEAD_DIM), lambda h, d: (0, base * N_HEADS + h))
    f_spec = pl.BlockSpec((T, HEAD_DIM), lambda h, d: (0, COL_FFW * N_HEADS + N_HEADS * d + h))
    l_spec = pl.BlockSpec((None, 2, HEAD_DIM), lambda h, d: (d, 0, h))
    return col, f_spec, l_spec


def hgrn_fwd(proj, lb_logits):
    T = proj.shape[0]
    NC, CPB = T // HGRN_CHUNK, HGRN_BLOCK // HGRN_CHUNK
    col, f_spec, l_spec = _hgrn_specs(T)

    def body(l_ref, q_ref, f_ref, v_ref, o_ref, st_ref, dec_ref, qd_ref, oi_ref):
        d = pl.program_id(1)
        lb = _hgrn_lower_bound(l_ref)
        mask, mtri, _, mone = _hgrn_chunk_masks(d)

        def block(i, carry):
            rows = pl.ds(pl.multiple_of(i * HGRN_BLOCK, HGRN_BLOCK), HGRN_BLOCK)
            _, _, _, lf, k = _hgrn_gate(f_ref[rows, :], lb)
            b = jnp.dot(mtri, lf, precision=HI, preferred_element_type=F32)
            bl = jnp.dot(mone, lf, precision=HI, preferred_element_type=F32)
            qd = (q_ref[rows, :] * Q_SCALE * jnp.exp(b)).astype(BF16)
            kd = (k * jnp.exp(-b)).astype(BF16)
            ke = (k * jnp.exp(bl - b)).astype(BF16)
            vb = v_ref[rows, :].astype(BF16)
            att = jnp.where(mask, _nt(qd, kd), 0.0).astype(BF16)
            oi_ref[rows, :] = _nn(att, vb)
            qd_ref[rows, :] = qd
            dec = jnp.exp(bl)
            for cc in range(CPB):
                sl = slice(cc * HGRN_CHUNK, (cc + 1) * HGRN_CHUNK)
                n = i * CPB + cc
                st_ref[n] = _tn(vb[sl], ke[sl])
                dec_ref[n] = dec[cc * HGRN_CHUNK:cc * HGRN_CHUNK + 8, :]
            return carry

        lax.fori_loop(0, T // HGRN_BLOCK, block, 0)

        def scan(t, s):
            n = jnp.where(d == 0, t, NC - 1 - t)
            u = st_ref[n]
            st_ref[n] = s
            return dec_ref[n][0:1, :] * s + u

        lax.fori_loop(0, NC, scan, jnp.zeros((HEAD_DIM, HEAD_DIM), F32))

        def inter(n, carry):
            rows = pl.ds(pl.multiple_of(n * HGRN_CHUNK, HGRN_CHUNK), HGRN_CHUNK)
            oi_ref[rows, :] += _nt(qd_ref[rows, :], st_ref[n].astype(BF16))
            return carry

        lax.fori_loop(0, NC, inter, 0)

        @pl.when(d == 0)
        def _():
            o_ref[...] = oi_ref[...]

        @pl.when(d == 1)
        def _():
            o_ref[...] += oi_ref[...]

    return pl.pallas_call(
        body, name="hgrn_fwd", out_shape=jax.ShapeDtypeStruct((T, N_HEADS * HEAD_DIM), F32), grid=(N_HEADS, 2),
        in_specs=[l_spec, col(COL_Q), f_spec, col(COL_V)],
        out_specs=pl.BlockSpec((T, HEAD_DIM), lambda h, d: (0, h)),
        scratch_shapes=[pltpu.VMEM((NC, HEAD_DIM, HEAD_DIM), F32), pltpu.VMEM((NC, 8, HEAD_DIM), F32),
                        pltpu.VMEM((T, HEAD_DIM), BF16), pltpu.VMEM((T, HEAD_DIM), F32)],
        compiler_params=_cp("parallel", "arbitrary"),
    )(lb_logits, proj, proj, proj)


def hgrn_post_fwd(o, proj, g_norm):
    T, W = o.shape
    tm = min(256, T)

    def body(o_ref, og_ref, g_ref, y_ref):
        g = g_ref[...]
        for h in range(N_HEADS):
            sl = slice(h * HEAD_DIM, (h + 1) * HEAD_DIM)
            x = o_ref[:, sl]
            r = lax.rsqrt(jnp.mean(x * x, axis=-1, keepdims=True) + EPS)
            og = og_ref[:, sl]
            y_ref[:, sl] = ((x * r) * g * (og * jax.nn.sigmoid(og))).astype(BF16)

    return pl.pallas_call(
        body, name="hgrn_post_fwd", out_shape=jax.ShapeDtypeStruct((T, W), BF16), grid=(T // tm,),
        in_specs=[pl.BlockSpec((tm, W), lambda i: (i, 0)), pl.BlockSpec((tm, W), lambda i: (i, COL_OG)), _vec(HEAD_DIM)],
        out_specs=pl.BlockSpec((tm, W), lambda i: (i, 0)), compiler_params=_cp("parallel"),
    )(o, proj, g_norm)


def _gelu(x):
    return 0.5 * x * (1.0 + lax.erf(x * (1.0 / math.sqrt(2.0))))


def _gelu_grad(x):
    return 0.5 * (1.0 + lax.erf(x * (1.0 / math.sqrt(2.0)))) + x * jnp.exp(-0.5 * x * x) * (1.0 / math.sqrt(2.0 * math.pi))


def _sgu_mix(u_ref, v_ref, g_ref, ws_ref, bst_ref):
    W = u_ref.shape[1]
    zu, zv = _gelu(u_ref[...]), _gelu(v_ref[...])
    dv = zv - jnp.mean(zv, axis=-1, keepdims=True)
    rstd = lax.rsqrt(jnp.mean(dv * dv, axis=-1, keepdims=True) + EPS)
    dhat = dv * rstd
    vn = (dhat * g_ref[...]).astype(BF16)
    gw = W // SGU_GROUPS
    vm = [_nn(ws_ref[g].astype(BF16), vn[:, g * gw:(g + 1) * gw]) + bst_ref[:, g:g + 1] for g in range(SGU_GROUPS)]
    return zu, rstd, dhat, vn, jnp.concatenate(vm, axis=1)


def sgu_fwd(proj, g_norm, w_spatial, b_spatial_t):
    T = proj.shape[0]
    W = 1024
    n_chunks = T // SGU_CHUNK

    def body(u_ref, v_ref, g_ref, ws_ref, bst_ref, y_ref):
        zu, _, _, _, vm = _sgu_mix(u_ref, v_ref, g_ref, ws_ref, bst_ref)
        y_ref[...] = (zu * vm).astype(BF16)

    blk = lambda cb: pl.BlockSpec((SGU_CHUNK, W), lambda i: (i, cb))
    return pl.pallas_call(
        body, name="sgu_fwd", out_shape=jax.ShapeDtypeStruct((T, W), BF16), grid=(n_chunks,),
        in_specs=[blk(COL_U), blk(COL_ZV), _vec(W), pl.BlockSpec((SGU_GROUPS, SGU_CHUNK, SGU_CHUNK), lambda i: (0, 0, 0)),
                  pl.BlockSpec((SGU_CHUNK, SGU_GROUPS), lambda i: (0, 0))],
        out_specs=blk(0), compiler_params=_cp("parallel"),
    )(proj, proj, g_norm, w_spatial, b_spatial_t)


def merge_matmul(ya_pre, sgu, w_a, w_b, proj):
    T, K = ya_pre.shape
    N = w_a.shape[1]
    tm, tn = min(512, T), 512
    gpb = 1024 // tn

    def body(a_ref, b_ref, wa_ref, wb_ref, ga_ref, gb_ref, ya_ref, yb_ref, m_ref):
        ya = _nn(a_ref[...], wa_ref[...])
        yb = _nn(b_ref[...], wb_ref[...])
        ya_ref[...] = ya
        yb_ref[...] = yb
        m_ref[...] = (jax.nn.sigmoid(ga_ref[...]) * ya + jax.nn.sigmoid(gb_ref[...]) * yb).astype(BF16)

    lhs = pl.BlockSpec((tm, K), lambda i, j: (i, 0))
    rhs = pl.BlockSpec((K, tn), lambda i, j: (0, j))
    out = pl.BlockSpec((tm, tn), lambda i, j: (i, j))
    return pl.pallas_call(
        body, name="merge_matmul", grid=(T // tm, N // tn),
        out_shape=[jax.ShapeDtypeStruct((T, N), F32), jax.ShapeDtypeStruct((T, N), F32), jax.ShapeDtypeStruct((T, N), BF16)],
        in_specs=[lhs, lhs, rhs, rhs, pl.BlockSpec((tm, tn), lambda i, j: (i, COL_GA * gpb + j)),
                  pl.BlockSpec((tm, tn), lambda i, j: (i, COL_GB * gpb + j))],
        out_specs=[out, out, out], compiler_params=_cp("parallel", "parallel"),
    )(ya_pre, sgu, w_a, w_b, proj, proj)


def out_proj(merged, w_o, h0, gt1, g_post):
    T, D = h0.shape
    tm = min(256, T)

    def body(m_ref, w_ref, h_ref, gt_ref, gp_ref, mo_ref, h1_ref):
        mo = _nn(m_ref[...], w_ref[...])
        mo_ref[...] = mo
        r = lax.rsqrt(jnp.mean(mo * mo, axis=-1, keepdims=True) + EPS)
        h1_ref[...] = h_ref[...] + gt_ref[...] * ((mo * r) * gp_ref[...])

    row = pl.BlockSpec((tm, D), lambda i: (i, 0))
    return pl.pallas_call(
        body, name="out_proj", grid=(T // tm,),
        out_shape=[jax.ShapeDtypeStruct((T, D), F32), jax.ShapeDtypeStruct((T, D), F32)],
        in_specs=[row, pl.BlockSpec((D, D), lambda i: (0, 0)), row, _vec(D), _vec(D)],
        out_specs=[row, row], compiler_params=_cp("parallel"),
    )(merged, w_o, h0, gt1, g_post)


def ff2_loss(hid, w_ff2, h1, tgt, gt2, g_post):
    T, K = hid.shape
    D = w_ff2.shape[1]
    tm, tk = min(256, T), 2048
    nk = K // tk

    def body(a_ref, w_ref, h_ref, t_ref, gt_ref, g_ref, dy_ref, dff_ref, loss_ref, dgt_ref, dg_ref, acc_ref):
        i, k = pl.program_id(0), pl.program_id(1)

        @pl.when(k == 0)
        def _():
            acc_ref[...] = jnp.zeros_like(acc_ref)

        @pl.when((k == 0) & (i == 0))
        def _():
            loss_ref[...] = jnp.zeros_like(loss_ref)
            dgt_ref[...] = jnp.zeros_like(dgt_ref)
            dg_ref[...] = jnp.zeros_like(dg_ref)

        acc_ref[...] += _nn(a_ref[...], w_ref[...])

        @pl.when(k == nk - 1)
        def _():
            ff = acc_ref[...]
            gt, g = gt_ref[...], g_ref[...]
            r = lax.rsqrt(jnp.mean(ff * ff, axis=-1, keepdims=True) + EPS)
            fhat = ff * r
            nf = fhat * g
            err = (h_ref[...] + gt * nf) - t_ref[...]
            loss_ref[...] += jnp.sum(err * err)
            dy = err * (1.0 / D)
            dy_ref[...] = dy
            dgt_ref[...] += _colsum(dy * nf)
            dnf = dy * gt
            dg_ref[...] += _colsum(dnf * fhat)
            u = dnf * g
            dff_ref[...] = (r * (u - fhat * jnp.mean(u * fhat, axis=-1, keepdims=True))).astype(BF16)

    row = pl.BlockSpec((tm, D), lambda i, k: (i, 0))
    vec = pl.BlockSpec((1, D), lambda i, k: (0, 0))
    return pl.pallas_call(
        body, name="ff2_loss", grid=(T // tm, nk),
        out_shape=[jax.ShapeDtypeStruct((T, D), F32), jax.ShapeDtypeStruct((T, D), BF16), jax.ShapeDtypeStruct((8, 128), F32),
                   jax.ShapeDtypeStruct((1, D), F32), jax.ShapeDtypeStruct((1, D), F32)],
        in_specs=[pl.BlockSpec((tm, tk), lambda i, k: (i, k)), pl.BlockSpec((tk, D), lambda i, k: (k, 0)), row, row, vec, vec],
        out_specs=[row, row, pl.BlockSpec((8, 128), lambda i, k: (0, 0)), vec, vec],
        scratch_shapes=[pltpu.VMEM((tm, D), F32)], compiler_params=_cp("arbitrary", "arbitrary"),
    )(hid, w_ff2, h1, tgt, gt2, g_post)


def ff2_bwd(dff, w_ff2, f1):
    T, D = dff.shape
    K = w_ff2.shape[0]
    tm, tn = min(512, T), 1024

    def body(a_ref, w_ref, f_ref, o_ref):
        o_ref[...] = (_nt(a_ref[...], w_ref[...]) * (2.0 * jnp.maximum(f_ref[...], 0.0))).astype(BF16)

    return pl.pallas_call(
        body, name="ff2_bwd", out_shape=jax.ShapeDtypeStruct((T, K), BF16), grid=(T // tm, K // tn),
        in_specs=[pl.BlockSpec((tm, D), lambda i, j: (i, 0)), pl.BlockSpec((tn, D), lambda i, j: (j, 0)),
                  pl.BlockSpec((tm, tn), lambda i, j: (i, j))],
        out_specs=pl.BlockSpec((tm, tn), lambda i, j: (i, j)), compiler_params=_cp("parallel", "parallel"),
    )(dff, w_ff2, f1)


def ffn_norm_bwd(dy, da2, h1, mo, g_pre2, sc2, gt1, g_post):
    T, D = dy.shape
    tm = min(256, T)

    def body(dy_ref, da_ref, h_ref, mo_ref, g2_ref, sc_ref, gt_ref, gp_ref, dh_ref, dmo_ref, s_sh, s_sc, s_g2, s_gt, s_gp):
        @pl.when(pl.program_id(0) == 0)
        def _():
            for s in (s_sh, s_sc, s_g2, s_gt, s_gp):
                s[...] = jnp.zeros_like(s)

        h1, da = h_ref[...], da_ref[...]
        g2, sc = g2_ref[...], sc_ref[...]
        r2 = lax.rsqrt(jnp.mean(h1 * h1, axis=-1, keepdims=True) + EPS)
        n2 = h1 * r2
        s_sh[...] += _colsum(da)
        s_sc[...] += _colsum(da * (n2 * g2))
        s_g2[...] += _colsum(da * (1.0 + sc) * n2)
        dn2 = da * g2 * (1.0 + sc)
        dh1 = dy_ref[...] + r2 * (dn2 - n2 * jnp.mean(dn2 * n2, axis=-1, keepdims=True))
        dh_ref[...] = dh1
        mo = mo_ref[...]
        gt, gp = gt_ref[...], gp_ref[...]
        r = lax.rsqrt(jnp.mean(mo * mo, axis=-1, keepdims=True) + EPS)
        mhat = mo * r
        s_gt[...] += _colsum(dh1 * (mhat * gp))
        dnm = dh1 * gt
        s_gp[...] += _colsum(dnm * mhat)
        u = dnm * gp
        dmo_ref[...] = (r * (u - mhat * jnp.mean(u * mhat, axis=-1, keepdims=True))).astype(BF16)

    row = pl.BlockSpec((tm, D), lambda i: (i, 0))
    vec_out = jax.ShapeDtypeStruct((1, D), F32)
    return pl.pallas_call(
        body, name="ffn_norm_bwd", grid=(T // tm,),
        out_shape=[jax.ShapeDtypeStruct((T, D), F32), jax.ShapeDtypeStruct((T, D), BF16)] + [vec_out] * 5,
        in_specs=[row, row, row, row] + [_vec(D)] * 4, out_specs=[row, row] + [_vec(D)] * 5,
        compiler_params=_cp("arbitrary"),
    )(dy, da2, h1, mo, g_pre2, sc2, gt1, g_post)


def out_proj_bwd(dmo, w_o, y_a, y_b, proj):
    T, D = dmo.shape
    tm, tn = min(512, T), 512
    gpb = 1024 // tn

    def body(a_ref, w_ref, ya_ref, yb_ref, ga_ref, gb_ref, dya_ref, dyb_ref, dga_ref, dgb_ref):
        dm = _nt(a_ref[...], w_ref[...])
        sa, sb = jax.nn.sigmoid(ga_ref[...]), jax.nn.sigmoid(gb_ref[...])
        dya_ref[...] = (dm * sa).astype(BF16)
        dyb_ref[...] = (dm * sb).astype(BF16)
        dga_ref[...] = (dm * ya_ref[...] * sa * (1.0 - sa)).astype(BF16)
        dgb_ref[...] = (dm * yb_ref[...] * sb * (1.0 - sb)).astype(BF16)

    out = pl.BlockSpec((tm, tn), lambda i, j: (i, j))
    return pl.pallas_call(
        body, name="out_proj_bwd", grid=(T // tm, D // tn), out_shape=[jax.ShapeDtypeStruct((T, D), BF16)] * 4,
        in_specs=[pl.BlockSpec((tm, D), lambda i, j: (i, 0)), pl.BlockSpec((tn, D), lambda i, j: (j, 0)), out, out,
                  pl.BlockSpec((tm, tn), lambda i, j: (i, COL_GA * gpb + j)), pl.BlockSpec((tm, tn), lambda i, j: (i, COL_GB * gpb + j))],
        out_specs=[out] * 4, compiler_params=_cp("parallel", "parallel"),
    )(dmo, w_o, y_a, y_b, proj, proj)


def sgu_bwd(proj, dsgu, g_norm, w_spatial, b_spatial_t):
    T = proj.shape[0]
    W = 1024
    gw = W // SGU_GROUPS

    def body(u_ref, v_ref, ds_ref, g_ref, ws_ref, bst_ref, dz_ref, dw_ref, db_ref, dg_ref):
        @pl.when(pl.program_id(0) == 0)
        def _():
            dw_ref[...] = jnp.zeros_like(dw_ref)
            db_ref[...] = jnp.zeros_like(db_ref)
            dg_ref[...] = jnp.zeros_like(dg_ref)

        zu, rstd, dhat, vn, vm = _sgu_mix(u_ref, v_ref, g_ref, ws_ref, bst_ref)
        ds = ds_ref[...]
        du = ds * vm
        dvm = ds * zu
        dvm_b = dvm.astype(BF16)
        ones = jnp.ones((8, gw), F32)
        dvn = []
        for g in range(SGU_GROUPS):
            sl = slice(g * gw, (g + 1) * gw)
            dw_ref[g] += _nt(dvm_b[:, sl], vn[:, sl])
            db_ref[g] += lax.dot_general(ones, dvm[:, sl], (((1,), (1,)), ((), ())), precision=HI, preferred_element_type=F32)
            dvn.append(_tn(ws_ref[g].astype(BF16), dvm_b[:, sl]))
        dvn = jnp.concatenate(dvn, axis=1)
        dg_ref[...] += _colsum(dvn * dhat)
        ddh = dvn * g_ref[...]
        dzv = rstd * (ddh - jnp.mean(ddh, axis=-1, keepdims=True) - dhat * jnp.mean(ddh * dhat, axis=-1, keepdims=True))
        dz_ref[:, 0:W] = (du * _gelu_grad(u_ref[...])).astype(BF16)
        dz_ref[:, W:2 * W] = (dzv * _gelu_grad(v_ref[...])).astype(BF16)

    blk = lambda cb: pl.BlockSpec((SGU_CHUNK, W), lambda i: (i, cb))
    full3 = lambda a, b, c: pl.BlockSpec((a, b, c), lambda i: (0, 0, 0))
    return pl.pallas_call(
        body, name="sgu_bwd", grid=(T // SGU_CHUNK,),
        out_shape=[jax.ShapeDtypeStruct((T, 2 * W), BF16), jax.ShapeDtypeStruct((SGU_GROUPS, SGU_CHUNK, SGU_CHUNK), F32),
                   jax.ShapeDtypeStruct((SGU_GROUPS, 8, SGU_CHUNK), F32), jax.ShapeDtypeStruct((1, W), F32)],
        in_specs=[blk(COL_U), blk(COL_ZV), blk(0), _vec(W), full3(SGU_GROUPS, SGU_CHUNK, SGU_CHUNK),
                  pl.BlockSpec((SGU_CHUNK, SGU_GROUPS), lambda i: (0, 0))],
        out_specs=[pl.BlockSpec((SGU_CHUNK, 2 * W), lambda i: (i, 0)), full3(SGU_GROUPS, SGU_CHUNK, SGU_CHUNK),
                   full3(SGU_GROUPS, 8, SGU_CHUNK), _vec(W)],
        compiler_params=_cp("arbitrary"),
    )(proj, proj, dsgu, g_norm, w_spatial, b_spatial_t)


def hgrn_post_bwd(dya, o, proj, g_norm):
    T, W = o.shape
    tm = min(256, T)

    def body(dy_ref, o_ref, og_ref, g_ref, do_ref, dog_ref, dg_ref):
        @pl.when(pl.program_id(0) == 0)
        def _():
            dg_ref[...] = jnp.zeros_like(dg_ref)

        g = g_ref[...]
        dg = jnp.zeros((1, HEAD_DIM), F32)
        for h in range(N_HEADS):
            sl = slice(h * HEAD_DIM, (h + 1) * HEAD_DIM)
            x, og, dy = o_ref[:, sl], og_ref[:, sl], dy_ref[:, sl]
            r = lax.rsqrt(jnp.mean(x * x, axis=-1, keepdims=True) + EPS)
            xhat = x * r
            s = jax.nn.sigmoid(og)
            don = dy * (og * s)
            dog_ref[:, sl] = (dy * (xhat * g) * (s * (1.0 + og * (1.0 - s)))).astype(BF16)
            dg += _colsum(don * xhat)
            u = don * g
            do_ref[:, sl] = r * (u - xhat * jnp.mean(u * xhat, axis=-1, keepdims=True))
        dg_ref[...] += dg

    row = pl.BlockSpec((tm, W), lambda i: (i, 0))
    return pl.pallas_call(
        body, name="hgrn_post_bwd", grid=(T // tm,),
        out_shape=[jax.ShapeDtypeStruct((T, W), F32), jax.ShapeDtypeStruct((T, W), BF16), jax.ShapeDtypeStruct((1, HEAD_DIM), F32)],
        in_specs=[row, row, pl.BlockSpec((tm, W), lambda i: (i, COL_OG)), _vec(HEAD_DIM)],
        out_specs=[row, row, _vec(HEAD_DIM)], compiler_params=_cp("arbitrary"),
    )(dya, o, proj, g_norm)


def hgrn_bwd(proj, do, lb_logits):
    T = proj.shape[0]
    NC, CPB = T // HGRN_CHUNK, HGRN_BLOCK // HGRN_CHUNK
    W = N_HEADS * HEAD_DIM
    col, f_spec, l_spec = _hgrn_specs(T)

    def body(l_ref, q_ref, f_ref, v_ref, do_ref, dq_ref, dv_ref, dlg_ref, dlb_ref, st_ref, dst_ref, dec_ref, ddec_ref, dqa_ref, dva_ref):
        d = pl.program_id(1)
        lb = _hgrn_lower_bound(l_ref)
        oml = 1.0 - lb
        mask, mtri, mtri_t, mone = _hgrn_chunk_masks(d)

        def values(rows):
            s, sn, fg, lf, k = _hgrn_gate(f_ref[rows, :], lb)
            b = jnp.dot(mtri, lf, precision=HI, preferred_element_type=F32)
            bl = jnp.dot(mone, lf, precision=HI, preferred_element_type=F32)
            eb, enb, ee = jnp.exp(b), jnp.exp(-b), jnp.exp(bl - b)
            qd = q_ref[rows, :] * Q_SCALE * eb
            return s, sn, fg, k, bl, eb, enb, ee, qd, k * enb, k * ee

        def block1(i, carry):
            rows = pl.ds(pl.multiple_of(i * HGRN_BLOCK, HGRN_BLOCK), HGRN_BLOCK)
            _, _, _, _, bl, _, _, _, qd, _, ke = values(rows)
            qd, ke = qd.astype(BF16), ke.astype(BF16)
            vb, dob = v_ref[rows, :].astype(BF16), do_ref[rows, :].astype(BF16)
            dec = jnp.exp(bl)
            for cc in range(CPB):
                sl = slice(cc * HGRN_CHUNK, (cc + 1) * HGRN_CHUNK)
                n = i * CPB + cc
                st_ref[n] = _tn(vb[sl], ke[sl])
                dst_ref[n] = _tn(dob[sl], qd[sl])
                dec_ref[n] = dec[cc * HGRN_CHUNK:cc * HGRN_CHUNK + 8, :]
            return carry

        lax.fori_loop(0, T // HGRN_BLOCK, block1, 0)

        def scan(t, s):
            n = jnp.where(d == 0, t, NC - 1 - t)
            u = st_ref[n]
            st_ref[n] = s
            return dec_ref[n][0:1, :] * s + u

        lax.fori_loop(0, NC, scan, jnp.zeros((HEAD_DIM, HEAD_DIM), F32))

        def rscan(t, ds):
            n = jnp.where(d == 0, NC - 1 - t, t)
            w = dst_ref[n]
            dst_ref[n] = ds
            ddec_ref[n] = jnp.broadcast_to(_colsum(ds * st_ref[n]), (8, HEAD_DIM))
            return dec_ref[n][0:1, :] * ds + w

        lax.fori_loop(0, NC, rscan, jnp.zeros((HEAD_DIM, HEAD_DIM), F32))

        def block3(i, dlb):
            rows = pl.ds(pl.multiple_of(i * HGRN_BLOCK, HGRN_BLOCK), HGRN_BLOCK)
            s, sn, fg, k, bl, eb, enb, ee, qd, kd, ke = values(rows)
            qdb, kdb, keb = qd.astype(BF16), kd.astype(BF16), ke.astype(BF16)
            vb, dob = v_ref[rows, :].astype(BF16), do_ref[rows, :].astype(BF16)
            att = jnp.where(mask, _nt(qdb, kdb), 0.0).astype(BF16)
            datt = jnp.where(mask, _nt(dob, vb), 0.0).astype(BF16)
            dv = _tn(att, dob)
            dqd = _nn(datt, kdb)
            dkd = _tn(datt, qdb)
            dv_i, dqd_i, dke, ddl = [], [], [], []
            for cc in range(CPB):
                sl = slice(cc * HGRN_CHUNK, (cc + 1) * HGRN_CHUNK)
                n = i * CPB + cc
                st_b, dst_b = st_ref[n].astype(BF16), dst_ref[n].astype(BF16)
                dv_i.append(_nt(keb[sl], dst_b))
                dqd_i.append(_nn(dob[sl], st_b))
                dke.append(_nn(vb[sl], dst_b))
                ddl.append(jnp.broadcast_to(ddec_ref[n][0:1, :] * dec_ref[n][0:1, :], (HGRN_CHUNK, HEAD_DIM)))
            dv = dv + jnp.concatenate(dv_i, axis=0)
            dqd = dqd + jnp.concatenate(dqd_i, axis=0)
            dke = jnp.concatenate(dke, axis=0)
            dq = dqd * eb * Q_SCALE
            dk = dkd * enb + dke * ee
            t_end = dke * ke
            db = dqd * qd - dkd * kd - t_end
            dlf = (jnp.dot(mtri_t, db, precision=HI, preferred_element_type=F32)
                   + jnp.dot(mone, t_end, precision=HI, preferred_element_type=F32) + jnp.concatenate(ddl, axis=0))
            e = dlf / fg - dk
            dlg_ref[rows, :] = (oml * e * s * sn).astype(BF16)

            @pl.when(d == 0)
            def _():
                dqa_ref[rows, :] = dq
                dva_ref[rows, :] = dv

            @pl.when(d == 1)
            def _():
                dq_ref[rows, :] = (dqa_ref[rows, :] + dq).astype(BF16)
                dv_ref[rows, :] = (dva_ref[rows, :] + dv).astype(BF16)

            return dlb + _colsum(e * sn)

        dlb_ref[...] = lax.fori_loop(0, T // HGRN_BLOCK, block3, jnp.zeros((1, HEAD_DIM), F32))

    head = pl.BlockSpec((T, HEAD_DIM), lambda h, d: (0, h))
    big = pltpu.VMEM((NC, HEAD_DIM, HEAD_DIM), F32)
    small = pltpu.VMEM((NC, 8, HEAD_DIM), F32)
    acc = pltpu.VMEM((T, HEAD_DIM), F32)
    return pl.pallas_call(
        body, name="hgrn_bwd", grid=(N_HEADS, 2),
        out_shape=[jax.ShapeDtypeStruct((T, W), BF16), jax.ShapeDtypeStruct((T, W), BF16), jax.ShapeDtypeStruct((T, 2 * W), BF16),
                   jax.ShapeDtypeStruct((2, 1, W), F32)],
        in_specs=[l_spec, col(COL_Q), f_spec, col(COL_V), head],
        out_specs=[head, head, pl.BlockSpec((T, HEAD_DIM), lambda h, d: (0, N_HEADS * d + h)),
                   pl.BlockSpec((None, 1, HEAD_DIM), lambda h, d: (d, 0, h))],
        scratch_shapes=[big, big, small, small, acc, acc], compiler_params=_cp("parallel", "arbitrary"),
    )(lb_logits, proj, proj, proj, do)


def mix_norm_bwd(da1, h0, dh1, g_pre, sc1):
    T, D = h0.shape
    tm = min(256, T)

    def body(da_ref, h_ref, dh_ref, g_ref, sc_ref, gx_ref, s_sh, s_sc, s_g):
        @pl.when(pl.program_id(0) == 0)
        def _():
            for s in (s_sh, s_sc, s_g):
                s[...] = jnp.zeros_like(s)

        h, da = h_ref[...], da_ref[...]
        g, sc = g_ref[...], sc_ref[...]
        r = lax.rsqrt(jnp.mean(h * h, axis=-1, keepdims=True) + EPS)
        n = h * r
        s_sh[...] += _colsum(da)
        s_sc[...] += _colsum(da * (n * g))
        s_g[...] += _colsum(da * (1.0 + sc) * n)
        dn = da * g * (1.0 + sc)
        gx_ref[...] = dh_ref[...] + r * (dn - n * jnp.mean(dn * n, axis=-1, keepdims=True))

    row = pl.BlockSpec((tm, D), lambda i: (i, 0))
    return pl.pallas_call(
        body, name="mix_norm_bwd", grid=(T // tm,),
        out_shape=[jax.ShapeDtypeStruct((T, D), F32)] + [jax.ShapeDtypeStruct((1, D), F32)] * 3,
        in_specs=[row, row, row, _vec(D), _vec(D)], out_specs=[row] + [_vec(D)] * 3, compiler_params=_cp("arbitrary"),
    )(da1, h0, dh1, g_pre, sc1)


def adamw(w, g, m, v, name):
    R, C = w.shape
    tr = R if R * C * 4 <= (1 << 21) else max(8, ((1 << 21) // (C * 4)) // 8 * 8)
    while R % tr:
        tr -= 8

    def body(w_ref, g_ref, m_ref, v_ref, d_ref, m2_ref, v2_ref):
        d_ref[...], m2_ref[...], v2_ref[...] = _adamw(w_ref[...], g_ref[...], m_ref[...], v_ref[...])

    row = pl.BlockSpec((tr, C), lambda i: (i, 0))
    return pl.pallas_call(
        body, name=name, grid=(R // tr,), out_shape=[jax.ShapeDtypeStruct((R, C), F32)] * 3,
        in_specs=[row] * 4, out_specs=[row] * 3, compiler_params=_cp("parallel"),
    )(w, g, m, v)


def wada_update(c_all, dmod, w, m, v):
    D, N = w.shape
    tm, tn = 512, 1024

    def body(c_ref, dm_ref, w_ref, m_ref, v_ref, g_ref, d_ref, m2_ref, v2_ref):
        c = c_ref[...]
        g = lax.dot_general(c * jax.nn.sigmoid(c), dm_ref[...], (((0,), (0,)), ((), ())), precision=HI, preferred_element_type=F32)
        g_ref[...] = g
        d_ref[...], m2_ref[...], v2_ref[...] = _adamw(w_ref[...], g, m_ref[...], v_ref[...])

    blk = pl.BlockSpec((tm, tn), lambda i, j: (i, j))
    return pl.pallas_call(
        body, name="wada_update", grid=(D // tm, N // tn), out_shape=[jax.ShapeDtypeStruct((D, N), F32)] * 4,
        in_specs=[pl.BlockSpec((8, tm), lambda i, j: (0, i)), pl.BlockSpec((8, tn), lambda i, j: (0, j)), blk, blk, blk],
        out_specs=[blk] * 4, compiler_params=_cp("parallel", "parallel"),
    )(c_all, dmod, w, m, v)


def sum_devices(gathered):
    n, R, C = gathered.shape

    def body(g_ref, o_ref):
        s = g_ref[0]
        for i in range(1, n):
            s = s + g_ref[i]
        o_ref[...] = s

    return pl.pallas_call(body, name="sum_devices", out_shape=jax.ShapeDtypeStruct((R, C), F32), compiler_params=_cp())(gathered)


def lb_logits_grad(dlb, lb_logits):
    def body(d_ref, l_ref, o_ref):
        for d in range(2):
            l0, l1 = l_ref[d, 0:1, :], l_ref[d, 1:2, :]
            m = jnp.maximum(l0, l1)
            e0, e1 = jnp.exp(l0 - m), jnp.exp(l1 - m)
            p0, p1 = e0 / (e0 + e1), e1 / (e0 + e1)
            g = d_ref[d:d + 1, :]
            o_ref[d, 0:1, :] = p0 * (g - p0 * g)
            o_ref[d, 1:2, :] = -p1 * (p0 * g)

    return pl.pallas_call(body, name="lb_logits_grad", out_shape=jax.ShapeDtypeStruct(lb_logits.shape, F32), compiler_params=_cp())(dlb, lb_logits)


def add_halves(g, landed, core):
    nj, _, r, cc = g.shape
    tr = min(256, r)

    def body(core_ref, g_ref, l_ref, o_ref):
        o_ref[...] = (g_ref[...].astype(F32) + l_ref[...].astype(F32)).astype(BF16)

    return pl.pallas_call(
        body, name="add_halves_%dx%d" % (r, cc), out_shape=jax.ShapeDtypeStruct((nj, r, cc), BF16),
        grid_spec=pltpu.PrefetchScalarGridSpec(
            num_scalar_prefetch=1, grid=(nj, r // tr),
            in_specs=[pl.BlockSpec((None, None, tr, cc), lambda j, i, core_ref: (j, core_ref[0], i, 0)),
                      pl.BlockSpec((None, None, tr, cc), lambda j, i, core_ref: (j, 0, i, 0))],
            out_specs=pl.BlockSpec((None, tr, cc), lambda j, i, core_ref: (j, i, 0))),
        compiler_params=_cp("parallel", "parallel"),
    )(core, g, landed)


def sum_chips(parts, landed, chip):
    nj, r, cc = parts.shape
    tr = min(256, r)

    def body(chip_ref, p_ref, l_ref, o_ref):
        mine = p_ref[...].astype(F32)
        s = None
        for j in range(nj):
            t = jnp.where(chip_ref[0] == j, mine, l_ref[j].astype(F32))
            s = t if s is None else s + t
        o_ref[...] = s

    return pl.pallas_call(
        body, name="sum_chips_%dx%d" % (r, cc), out_shape=jax.ShapeDtypeStruct((r, cc), F32),
        grid_spec=pltpu.PrefetchScalarGridSpec(
            num_scalar_prefetch=1, grid=(r // tr,),
            in_specs=[pl.BlockSpec((None, tr, cc), lambda i, chip_ref: (chip_ref[0], i, 0)),
                      pl.BlockSpec((nj, tr, cc), lambda i, chip_ref: (0, i, 0))],
            out_specs=pl.BlockSpec((tr, cc), lambda i, chip_ref: (i, 0))),
        compiler_params=_cp("parallel"),
    )(chip, parts, landed)


def adamw_halves(w, own, other, m, v, core, name):
    r, cc = own.shape
    tr = min(128, r)
    nb = r // tr

    def body(core_ref, w_ref, a_ref, b_ref, m_ref, v_ref, g_ref, d_ref, m2_ref, v2_ref):
        g = jnp.where(pl.program_id(0) == core_ref[0], a_ref[...], b_ref[...])
        g_ref[...] = g
        d_ref[...], m2_ref[...], v2_ref[...] = _adamw(w_ref[...], g, m_ref[...], v_ref[...])

    full = pl.BlockSpec((tr, cc), lambda h, i, core_ref: (h * nb + i, 0))
    half = pl.BlockSpec((tr, cc), lambda h, i, core_ref: (i, 0))
    return pl.pallas_call(
        body, name=name, out_shape=[jax.ShapeDtypeStruct((2 * r, cc), F32)] * 4,
        grid_spec=pltpu.PrefetchScalarGridSpec(
            num_scalar_prefetch=1, grid=(2, nb), in_specs=[full, half, half, full, full], out_specs=[full] * 4),
        compiler_params=_cp("parallel", "parallel"),
    )(core, w, own, other, m, v)


def _place():
    mx, my, mc = lax.axis_index("x"), lax.axis_index("y"), lax.axis_index("c")
    chips = [(1 - mx, my), (mx, 1 - my), (1 - mx, 1 - my)]
    return mx, my, mc, chips


def all_gather_small(x, name):
    R, C = x.shape

    def body(x_ref, out_ref, send_sems, recv_sems, local_sem):
        mx, my, mc, _ = _place()
        me = 4 * mx + 2 * my + mc
        mine = pltpu.make_async_copy(x_ref, out_ref.at[me], local_sem)
        mine.start()

        def peer(k):
            px = 1 - mx if k & 4 else mx
            py = 1 - my if k & 2 else my
            pc = 1 - mc if k & 1 else mc
            return px, py, pc

        def copy(k, src, slot):
            return pltpu.make_async_remote_copy(src_ref=src, dst_ref=out_ref.at[slot], send_sem=send_sems.at[k - 1],
                                                recv_sem=recv_sems.at[k - 1], device_id=peer(k), device_id_type=MESH)

        sends = [copy(k, x_ref, me) for k in range(1, 8)]
        for cp in sends:
            cp.start()
        for k in range(1, 8):
            px, py, pc = peer(k)
            slot = 4 * px + 2 * py + pc
            copy(k, out_ref.at[slot], slot).wait_recv()
        for cp in sends:
            cp.wait_send()
        mine.wait()

    return pl.pallas_call(
        body, name=name, out_shape=jax.ShapeDtypeStruct((8, R, C), F32),
        in_specs=[pl.BlockSpec(memory_space=pltpu.VMEM)], out_specs=pl.BlockSpec(memory_space=pltpu.VMEM),
        scratch_shapes=[pltpu.SemaphoreType.DMA((7,)), pltpu.SemaphoreType.DMA((7,)), pltpu.SemaphoreType.DMA],
        compiler_params=_cp(),
    )(x)


def _region(ref, kind, j, half, r, cc):
    nr = r if half is None else r // 2
    off = 0 if half is None else half * nr
    if kind == "col":
        return ref.at[pl.ds(off, nr), pl.ds(pl.multiple_of(j * cc, 128), cc)]
    return ref.at[pl.ds(pl.multiple_of(j * r + off, 16), nr), :]


def gather_weights(fulls, kinds, dims):
    n = len(fulls)

    def body(*refs):
        f_refs = refs[n:2 * n]
        send_sems, recv_sems = refs[2 * n:]
        mx, my, mc, chips = _place()
        jme = 2 * mx + my
        sibling = (mx, my, 1 - mc)

        def landed(w, k, half):
            px, py = chips[k]
            return _region(f_refs[w], kinds[w], 2 * px + py, half, *dims[w])

        def over_ici(w, k, src, dst):
            px, py = chips[k]
            return pltpu.make_async_remote_copy(src_ref=src, dst_ref=dst, send_sem=send_sems.at[6 * w + k], recv_sem=recv_sems.at[6 * w + k],
                                                device_id=(px, py, mc), device_id_type=MESH)

        def over_d2d(w, k, half):
            reg = landed(w, k, half)
            return pltpu.make_async_remote_copy(src_ref=reg, dst_ref=reg, send_sem=send_sems.at[6 * w + 3 + k],
                                                recv_sem=recv_sems.at[6 * w + 3 + k], device_id=sibling, device_id_type=MESH)

        sends = []
        for w in range(n):
            mine = _region(f_refs[w], kinds[w], jme, mc, *dims[w])
            for k in range(3):
                sends.append(over_ici(w, k, mine, mine))
        for cp in sends:
            cp.start()
        passed = []
        for w in range(n):
            for k in range(3):
                over_ici(w, k, landed(w, k, mc), landed(w, k, mc)).wait_recv()
                cp = over_d2d(w, k, mc)
                cp.start()
                passed.append(cp)
        for w in range(n):
            for k in range(3):
                over_d2d(w, k, 1 - mc).wait_recv()
        for cp in sends + passed:
            cp.wait_send()

    return pl.pallas_call(
        body, name="gather_weights", out_shape=[jax.ShapeDtypeStruct(f.shape, BF16) for f in fulls],
        in_specs=[ANY] * n, out_specs=[ANY] * n, input_output_aliases={w: w for w in range(n)},
        scratch_shapes=[pltpu.SemaphoreType.DMA((6 * n,)), pltpu.SemaphoreType.DMA((6 * n,))],
        compiler_params=_cp(),
    )(*fulls)


def exchange_halves(grads):
    n = len(grads)

    def body(*refs):
        g_refs, l_refs = refs[:n], refs[n:2 * n]
        send_sems, recv_sems = refs[2 * n:]
        mx, my, mc, _ = _place()
        cps = [pltpu.make_async_remote_copy(src_ref=g_refs[w].at[:, pl.ds(1 - mc, 1)], dst_ref=l_refs[w], send_sem=send_sems.at[w],
                                            recv_sem=recv_sems.at[w], device_id=(mx, my, 1 - mc), device_id_type=MESH) for w in range(n)]
        for cp in cps:
            cp.start()
        for cp in cps:
            cp.wait()

    return pl.pallas_call(
        body, name="exchange_halves", out_shape=[jax.ShapeDtypeStruct((g.shape[0], 1) + g.shape[2:], BF16) for g in grads],
        in_specs=[ANY] * n, out_specs=[ANY] * n,
        scratch_shapes=[pltpu.SemaphoreType.DMA((n,)), pltpu.SemaphoreType.DMA((n,))], compiler_params=_cp(),
    )(*grads)


def scatter_to_chips(parts):
    n = len(parts)

    def body(*refs):
        p_refs, l_refs = refs[:n], refs[n:2 * n]
        send_sems, recv_sems = refs[2 * n:]
        mx, my, mc, chips = _place()
        jme = 2 * mx + my
        sends = []
        for w in range(n):
            for k, (px, py) in enumerate(chips):
                sends.append(pltpu.make_async_remote_copy(src_ref=p_refs[w].at[2 * px + py], dst_ref=l_refs[w].at[jme],
                                                          send_sem=send_sems.at[3 * w + k], recv_sem=recv_sems.at[3 * w + k],
                                                          device_id=(px, py, mc), device_id_type=MESH))
        for cp in sends:
            cp.start()
        for w in range(n):
            for k, (px, py) in enumerate(chips):
                slot = l_refs[w].at[2 * px + py]
                pltpu.make_async_remote_copy(src_ref=slot, dst_ref=slot, send_sem=send_sems.at[3 * w + k], recv_sem=recv_sems.at[3 * w + k],
                                             device_id=(px, py, mc), device_id_type=MESH).wait_recv()
        for cp in sends:
            cp.wait_send()

    return pl.pallas_call(
        body, name="scatter_to_chips", out_shape=[jax.ShapeDtypeStruct(p.shape, BF16) for p in parts],
        in_specs=[ANY] * n, out_specs=[ANY] * n,
        scratch_shapes=[pltpu.SemaphoreType.DMA((3 * n,)), pltpu.SemaphoreType.DMA((3 * n,))],
        compiler_params=_cp(),
    )(*parts)


def share_with_sibling(sums):
    n = len(sums)

    def body(*refs):
        q_refs, o_refs = refs[:n], refs[n:2 * n]
        send_sems, recv_sems = refs[2 * n:]
        mx, my, mc, _ = _place()
        cps = [pltpu.make_async_remote_copy(src_ref=q_refs[w], dst_ref=o_refs[w], send_sem=send_sems.at[w], recv_sem=recv_sems.at[w],
                                            device_id=(mx, my, 1 - mc), device_id_type=MESH) for w in range(n)]
        for cp in cps:
            cp.start()
        for cp in cps:
            cp.wait()

    return pl.pallas_call(
        body, name="share_with_sibling", out_shape=[jax.ShapeDtypeStruct(q.shape, F32) for q in sums],
        in_specs=[ANY] * n, out_specs=[ANY] * n,
        scratch_shapes=[pltpu.SemaphoreType.DMA((n,)), pltpu.SemaphoreType.DMA((n,))],
        compiler_params=_cp(),
    )(*sums)


def _pack(arrays):
    flat = jnp.concatenate([a.reshape(-1) for a in arrays])
    rows = -(-flat.shape[0] // 1024) * 8
    return jnp.pad(flat, (0, rows * 128 - flat.shape[0])).reshape(rows, 128)


def _unpack(packed, shapes):
    flat, out, off = packed.reshape(-1), [], 0
    for s in shapes:
        n = math.prod(s)
        out.append(flat[off:off + n].reshape(s))
        off += n
    return out


def kernel(x, c, w_ada, b_ada, g_pre_mix, g_post_mix, g_pre_ffn, g_post_ffn, w_in, lb_logits, g_hgrn_norm, w_a_out, g_sgu_norm, w_spatial, b_spatial, w_b_out, w_o, w_ff1, w_ff2, loss_target, m_w_ada, m_b_ada, m_g_pre_mix, m_g_post_mix, m_g_pre_ffn, m_g_post_ffn, m_w_in, m_lb_logits, m_g_hgrn_norm, m_w_a_out, m_g_sgu_norm, m_w_spatial, m_b_spatial, m_w_b_out, m_w_o, m_w_ff1, m_w_ff2, v_w_ada, v_b_ada, v_g_pre_mix, v_g_post_mix, v_g_pre_ffn, v_g_post_ffn, v_w_in, v_lb_logits, v_g_hgrn_norm, v_w_a_out, v_g_sgu_norm, v_w_spatial, v_b_spatial, v_w_b_out, v_w_o, v_w_ff1, v_w_ff2):
    mx, my, mc = lax.axis_index("x"), lax.axis_index("y"), lax.axis_index("c")
    chip, me = 2 * mx + my, 4 * mx + 2 * my + mc
    D = D_MODEL
    h0, tgt = x[0], loss_target[0]
    n_ada = w_ada.shape[2]
    n_lb = lb_logits.shape[2]

    got = all_gather_small(_pack([c, lb_logits]), "gather_inputs")
    c_all = got[:, :D // 128, :].reshape(8, D)
    lb_full = got[0::2, D // 128:D // 128 + 4 * n_lb // 128, :].reshape(4, 2, 2, n_lb).transpose(1, 2, 0, 3).reshape(2, 2, 4 * n_lb)
    b_ada_chip = lax.dynamic_slice(b_ada, (0, chip * n_ada), (1, n_ada))
    mod_cols = mod_matmul(c_all, w_ada[0], b_ada_chip)
    got = all_gather_small(mod_cols.reshape(-1, 128), "gather_mod").reshape(4, 2, 8, n_ada)
    mod = lax.dynamic_index_in_dim(got[:, 0], me, axis=1, keepdims=False).reshape(6, 1, D)
    sh1, sc1, gt1, sh2, sc2, gt2 = (mod[i] for i in range(6))

    big = [("w_in", w_in, "col"), ("w_a_out", w_a_out, "col"), ("w_b_out", w_b_out, "col"), ("w_o", w_o, "row"),
           ("w_ff1", w_ff1, "col"), ("w_ff2", w_ff2, "row")]
    kinds = [k for _, _, k in big]
    chip_idx, core = chip.reshape(1).astype(jnp.int32), mc.reshape(1).astype(jnp.int32)
    w_in_f, w_a_f, w_b_f, w_o_f, w_ff1_f, w_ff2_f = gather_weights(
        [cast_into_full(w[0], kind, chip_idx, "cast_" + nm) for nm, w, kind in big], kinds, [w.shape[1:] for _, w, _ in big])

    bst = b_spatial[0].T
    proj, a1 = prenorm_matmul(h0, g_pre_mix, sc1, sh1, w_in_f, relu2=False, name="in_proj")
    o = hgrn_fwd(proj, lb_full)
    ya_pre = hgrn_post_fwd(o, proj, g_hgrn_norm)
    sgu = sgu_fwd(proj, g_sgu_norm, w_spatial[0], bst)
    y_a, y_b, merged = merge_matmul(ya_pre, sgu, w_a_f, w_b_f, proj)
    mo, h1 = out_proj(merged, w_o_f, h0, gt1, g_post_mix)
    f1, a2, hid = prenorm_matmul(h1, g_pre_ffn, sc2, sh2, w_ff1_f, relu2=True, name="ff1")
    dy, dff, loss_parts, d_gt2, d_g_post_ffn = ff2_loss(hid, w_ff2_f, h1, tgt, gt2, g_post_ffn)
    loss = lax.psum(0.5 * loss_parts[0, 0] / D, ("x", "y", "c"))

    df1 = ff2_bwd(dff, w_ff2_f, f1)
    gr_ff2 = matmul(hid, dff, mode="tn", out_dtype=BF16, tm=1024, tn=1024, tk=512, name="dw_ff2")
    da2 = matmul(df1, w_ff1_f, mode="nt", out_dtype=F32, tm=512, tn=1024, tk=2048, name="da2")
    gr_ff1 = matmul(a2, df1, mode="tn", out_dtype=BF16, tm=1024, tn=2048, tk=512, name="dw_ff1", split=(4, 2))
    dh1, dmo, d_sh2, d_sc2, d_g_pre_ffn, d_gt1, d_g_post_mix = ffn_norm_bwd(dy, da2, h1, mo, g_pre_ffn, sc2, gt1, g_post_mix)
    dya, dyb, dga, dgb = out_proj_bwd(dmo, w_o_f, y_a, y_b, proj)
    gr_o = matmul(merged, dmo, mode="tn", out_dtype=BF16, tm=1024, tn=1024, tk=512, name="dw_o")
    dsgu = matmul(dyb, w_b_f, mode="nt", out_dtype=F32, tm=512, tn=1024, tk=2048, name="dsgu")
    gr_b = matmul(sgu, dyb, mode="tn", out_dtype=BF16, tm=512, tn=512, tk=512, name="dw_b_out", split=(4, 2))
    dz, d_w_spatial, d_b_spatial, d_g_sgu = sgu_bwd(proj, dsgu, g_sgu_norm, w_spatial[0], bst)
    dya_pre = matmul(dya, w_a_f, mode="nt", out_dtype=F32, tm=512, tn=1024, tk=2048, name="dya_pre")
    gr_a = matmul(ya_pre, dya, mode="tn", out_dtype=BF16, tm=512, tn=512, tk=512, name="dw_a_out", split=(4, 2))
    do, dog, d_g_hgrn = hgrn_post_bwd(dya_pre, o, proj, g_hgrn_norm)
    dq, dv, dlg, d_lb = hgrn_bwd(proj, do, lb_full)
    dproj = jnp.concatenate([dq, dlg, dv, dog, dz, dga, dgb], axis=1)
    da1 = matmul(dproj, w_in_f, mode="nt", out_dtype=F32, tm=512, tn=1024, tk=2816, name="da1")
    gr_in = matmul(a1, dproj, mode="tn", out_dtype=BF16, tm=1024, tn=2816, tk=512, name="dw_in", split=(4, 2))
    grad_x, d_sh1, d_sc1, d_g_pre_mix = mix_norm_bwd(da1, h0, dh1, g_pre_mix, sc1)

    grads = [gr_in, gr_a, gr_b, gr_o.reshape(4, 2, -1, D), gr_ff1, gr_ff2.reshape(4, 2, -1, D)]
    landed = exchange_halves(grads)
    parts = [add_halves(g, l, core) for g, l in zip(grads, landed)]
    own = [sum_chips(p, l, chip_idx) for p, l in zip(parts, scatter_to_chips(parts))]
    other = share_with_sibling(own)
    out = {}
    for (nm, w, _), a, b, m, v in zip(big, own, other, (m_w_in, m_w_a_out, m_w_b_out, m_w_o, m_w_ff1, m_w_ff2),
                                      (v_w_in, v_w_a_out, v_w_b_out, v_w_o, v_w_ff1, v_w_ff2)):
        out[nm] = tuple(t[None] for t in adamw_halves(w[0], a, b, m[0], v[0], core, "adamw_" + nm))

    mine = _pack([d_sh1, d_sc1, d_gt1, d_sh2, d_sc2, d_gt2, d_g_pre_mix, d_g_post_mix, d_g_pre_ffn, d_g_post_ffn, d_g_hgrn, d_g_sgu,
                  d_w_spatial, d_b_spatial[:, 0, :], d_lb])
    got = all_gather_small(mine, "gather_small_grads")
    total = sum_devices(got)
    g_b_ada, g_g1, g_g2, g_g3, g_g4, g_hg, g_sg, g_ws, g_bs, g_lb = _unpack(
        total, [(1, 6 * D), (1, D), (1, D), (1, D), (1, D), (1, HEAD_DIM), (1, 1024), w_spatial.shape, b_spatial.shape, (2, 1024)])
    g_lbl = lax.dynamic_slice(lb_logits_grad(g_lb, lb_full), (0, 0, chip * n_lb), (2, 2, n_lb))
    names = ["b_ada", "g_pre_mix", "g_post_mix", "g_pre_ffn", "g_post_ffn", "g_hgrn_norm", "g_sgu_norm", "w_spatial", "b_spatial", "lb_logits"]
    ws = [b_ada, g_pre_mix, g_post_mix, g_pre_ffn, g_post_ffn, g_hgrn_norm, g_sgu_norm, w_spatial, b_spatial, lb_logits]
    gs = [g_b_ada, g_g1, g_g2, g_g3, g_g4, g_hg, g_sg, g_ws, g_bs, g_lbl]
    ms = [m_b_ada, m_g_pre_mix, m_g_post_mix, m_g_pre_ffn, m_g_post_ffn, m_g_hgrn_norm, m_g_sgu_norm, m_w_spatial, m_b_spatial, m_lb_logits]
    vs = [v_b_ada, v_g_pre_mix, v_g_post_mix, v_g_pre_ffn, v_g_post_ffn, v_g_hgrn_norm, v_g_sgu_norm, v_w_spatial, v_b_spatial, v_lb_logits]
    shapes = [w.shape for w in ws]
    upd = adamw(_pack(ws), _pack(gs), _pack(ms), _pack(vs), "adamw_small")
    upd = [_unpack(u, shapes) for u in upd]
    for i, nm in enumerate(names):
        out[nm] = (gs[i], upd[0][i], upd[1][i], upd[2][i])

    dmod_all = got[:, :6 * D // 128, :].reshape(8, 6 * D)
    dmod_chip = lax.dynamic_slice(dmod_all, (0, chip * n_ada), (8, n_ada))
    out["w_ada"] = tuple(a[None] for a in wada_update(c_all, dmod_chip, w_ada[0], m_w_ada[0], v_w_ada[0]))

    order = ["w_ada", "b_ada", "g_pre_mix", "g_post_mix", "g_pre_ffn", "g_post_ffn", "w_in", "lb_logits", "g_hgrn_norm", "w_a_out",
             "g_sgu_norm", "w_spatial", "b_spatial", "w_b_out", "w_o", "w_ff1", "w_ff2"]
    return (loss, grad_x[None], *[out[nm][0] for nm in order], *[out[nm][1] for nm in order], *[out[nm][2] for nm in order],
            *[out[nm][3] for nm in order])
```

```python
import functools
import math

import jax
import jax.numpy as jnp
from jax import lax
from jax.experimental import pallas as pl
from jax.experimental.pallas import tpu as pltpu

F32, BF16 = jnp.float32, jnp.bfloat16
HI = lax.Precision.HIGHEST
MESH = pl.DeviceIdType.MESH
ANY = pl.BlockSpec(memory_space=pl.ANY)

EPS = 1e-6
D_MODEL = 2048
N_HEADS = 8
HEAD_DIM = 128
HGRN_CHUNK = 32
HGRN_BLOCK = 256
SGU_CHUNK = 128
SGU_GROUPS = 8
Q_SCALE = HEAD_DIM ** -0.5
COL_Q, COL_FFW, COL_FBW, COL_V, COL_OG, COL_U, COL_ZV, COL_GA, COL_GB = 0, 1, 2, 3, 4, 5, 6, 7, 9
N_PROJ = 11264
VMEM_BYTES_V7X = 64 * 1024 * 1024
VMEM_LIMIT = VMEM_BYTES_V7X - 8 * 1024 * 1024

ADAM_LR, ADAM_B1, ADAM_B2, ADAM_EPS, ADAM_WD, ADAM_STEP = 0.001, 0.9, 0.999, 1e-08, 0.01, 10
ADAM_C1 = 1.0 - ADAM_B1 ** ADAM_STEP
ADAM_C2 = 1.0 - ADAM_B2 ** ADAM_STEP


def _cp(*sem):
    return pltpu.CompilerParams(dimension_semantics=sem if sem else None, vmem_limit_bytes=VMEM_LIMIT)


def _vec(d):
    return pl.BlockSpec((1, d), lambda *_: (0, 0))


def _colsum(x):
    return jnp.sum(x, axis=0, keepdims=True)


def _nt(a, b):
    return lax.dot_general(a, b, (((1,), (1,)), ((), ())), preferred_element_type=F32)


def _tn(a, b):
    return lax.dot_general(a, b, (((0,), (0,)), ((), ())), preferred_element_type=F32)


def _nn(a, b):
    return jnp.dot(a, b, preferred_element_type=F32)


def _adamw(w, g, m, v):
    m2 = ADAM_B1 * m + (1.0 - ADAM_B1) * g
    v2 = ADAM_B2 * v + (1.0 - ADAM_B2) * (g * g)
    delta = -ADAM_LR * ((m2 / ADAM_C1) / (jnp.sqrt(v2 / ADAM_C2) + ADAM_EPS) + ADAM_WD * w)
    return delta, m2, v2


def matmul(a, b, *, mode, out_dtype, tm, tn, tk, name, split=None):
    if mode == "tn":
        (K, M), (_, N) = a.shape, b.shape
    elif mode == "nt":
        (M, K), (N, _) = a.shape, b.shape
    else:
        (M, K), (_, N) = a.shape, b.shape
    tm, tn, tk = min(tm, M), min(tn, N), min(tk, K)
    nk = K // tk
    a_spec = pl.BlockSpec((tk, tm), lambda i, j, k: (k, i)) if mode == "tn" else pl.BlockSpec((tm, tk), lambda i, j, k: (i, k))
    b_spec = pl.BlockSpec((tn, tk), lambda i, j, k: (j, k)) if mode == "nt" else pl.BlockSpec((tk, tn), lambda i, j, k: (k, j))
    dot = {"nn": _nn, "nt": _nt, "tn": _tn}[mode]
    if split is None:
        out_shape = jax.ShapeDtypeStruct((M, N), out_dtype)
        out_spec = pl.BlockSpec((tm, tn), lambda i, j, k: (i, j))
    else:
        nj, nh = split
        rows, cols = M // nh, N // nj
        tm, tn = min(tm, rows), min(tn, cols)
        bi, bj = rows // tm, cols // tn
        out_shape = jax.ShapeDtypeStruct((nj, nh, rows, cols), out_dtype)
        out_spec = pl.BlockSpec((None, None, tm, tn), lambda i, j, k: (j // bj, i // bi, i % bi, j % bj))

    def body(a_ref, b_ref, o_ref, acc_ref):
        k = pl.program_id(2)

        @pl.when(k == 0)
        def _():
            acc_ref[...] = jnp.zeros_like(acc_ref)

        acc_ref[...] += dot(a_ref[...], b_ref[...])

        @pl.when(k == nk - 1)
        def _():
            o_ref[...] = acc_ref[...].astype(o_ref.dtype)

    return pl.pallas_call(
        body, name=name, out_shape=out_shape, grid=(M // tm, N // tn, nk),
        in_specs=[a_spec, b_spec], out_specs=out_spec, scratch_shapes=[pltpu.VMEM((tm, tn), F32)],
        compiler_params=_cp("parallel", "parallel", "arbitrary"),
    )(a, b)


def cast_into_full(w, kind, chip, name):
    r, cc = w.shape
    tr = min(r, 512)
    nb = r // tr

    def body(chip_ref, w_ref, o_ref):
        o_ref[...] = w_ref[...].astype(BF16)

    if kind == "col":
        full, out_map = (r, 4 * cc), lambda i, chip_ref: (i, chip_ref[0])
    else:
        full, out_map = (4 * r, cc), lambda i, chip_ref: (chip_ref[0] * nb + i, 0)
    return pl.pallas_call(
        body, name=name, out_shape=jax.ShapeDtypeStruct(full, BF16),
        grid_spec=pltpu.PrefetchScalarGridSpec(
            num_scalar_prefetch=1, grid=(nb,), in_specs=[pl.BlockSpec((tr, cc), lambda i, chip_ref: (i, 0))],
            out_specs=pl.BlockSpec((tr, cc), out_map)),
        compiler_params=_cp("parallel"),
    )(chip, w)


def mod_matmul(c_all, w_ada, b_ada):
    D, N = w_ada.shape
    tn = 1024

    def body(c_ref, w_ref, b_ref, o_ref):
        c = c_ref[...]
        sc = c * jax.nn.sigmoid(c)
        o_ref[...] = jnp.dot(sc, w_ref[...], precision=HI, preferred_element_type=F32) + b_ref[...]

    return pl.pallas_call(
        body, name="mod_matmul", out_shape=jax.ShapeDtypeStruct((8, N), F32), grid=(N // tn,),
        in_specs=[pl.BlockSpec((8, D), lambda j: (0, 0)), pl.BlockSpec((D, tn), lambda j: (0, j)),
                  pl.BlockSpec((1, tn), lambda j: (0, j))],
        out_specs=pl.BlockSpec((8, tn), lambda j: (0, j)), compiler_params=_cp("parallel"),
    )(c_all, w_ada, b_ada)


def prenorm_matmul(h, g, sc, sh, w, *, relu2, name):
    T, D = h.shape
    N = w.shape[1]
    tm, tn = min(512, T), 1024

    def body(h_ref, g_ref, sc_ref, sh_ref, w_ref, y_ref, a_ref, *hid_ref):
        @pl.when(pl.program_id(1) == 0)
        def _():
            x = h_ref[...]
            r = lax.rsqrt(jnp.mean(x * x, axis=-1, keepdims=True) + EPS)
            a_ref[...] = ((x * r) * g_ref[...] * (1.0 + sc_ref[...]) + sh_ref[...]).astype(BF16)

        y = _nn(a_ref[...], w_ref[...])
        y_ref[...] = y
        if relu2:
            p = jnp.maximum(y, 0.0)
            hid_ref[0][...] = (p * p).astype(BF16)

    out_shape = [jax.ShapeDtypeStruct((T, N), F32), jax.ShapeDtypeStruct((T, D), BF16)]
    out_specs = [pl.BlockSpec((tm, tn), lambda i, j: (i, j)), pl.BlockSpec((tm, D), lambda i, j: (i, 0))]
    if relu2:
        out_shape.append(jax.ShapeDtypeStruct((T, N), BF16))
        out_specs.append(pl.BlockSpec((tm, tn), lambda i, j: (i, j)))
    return pl.pallas_call(
        body, name=name, out_shape=out_shape, grid=(T // tm, N // tn),
        in_specs=[pl.BlockSpec((tm, D), lambda i, j: (i, 0)), _vec(D), _vec(D), _vec(D),
                  pl.BlockSpec((D, tn), lambda i, j: (0, j))],
        out_specs=out_specs, compiler_params=_cp("parallel", "arbitrary"),
    )(h, g, sc, sh, w)


def _hgrn_lower_bound(l_ref):
    l0, l1 = l_ref[0:1, :], l_ref[1:2, :]
    m = jnp.maximum(l0, l1)
    e0, e1 = jnp.exp(l0 - m), jnp.exp(l1 - m)
    return e0 / (e0 + e1)


def _hgrn_chunk_masks(d):
    r = lax.broadcasted_iota(jnp.int32, (HGRN_BLOCK, HGRN_BLOCK), 0)
    c = lax.broadcasted_iota(jnp.int32, (HGRN_BLOCK, HGRN_BLOCK), 1)
    same = (r // HGRN_CHUNK) == (c // HGRN_CHUNK)
    fwd = d == 0
    tri = same & (((c <= r) & fwd) | ((c >= r) & jnp.logical_not(fwd)))
    tri_t = same & (((c >= r) & fwd) | ((c <= r) & jnp.logical_not(fwd)))
    return tri, jnp.where(tri, 1.0, 0.0).astype(BF16), jnp.where(tri_t, 1.0, 0.0).astype(BF16)


def _tri_sum(tri, x, terms):
    pieces, rest = [], x
    for t in range(terms):
        p = rest.astype(BF16)
        pieces.append(p)
        if t + 1 < terms:
            rest = rest - p.astype(F32)
    y = _nn(tri, jnp.concatenate(pieces, axis=1))
    w = x.shape[1]
    return sum(y[:, t * w:(t + 1) * w] for t in range(terms))


def _chunk_total(x):
    x3 = x.reshape(HGRN_BLOCK // HGRN_CHUNK, HGRN_CHUNK, x.shape[1])
    return jnp.broadcast_to(jnp.sum(x3, axis=1, keepdims=True), x3.shape).reshape(x.shape)


def _hgrn_gate(f, lb):
    s = jax.nn.sigmoid(f)
    sn = jax.nn.sigmoid(-f)
    fg = lb + (1.0 - lb) * s
    return s, sn, fg, jnp.log(fg), (1.0 - lb) * sn


def _hgrn_specs(T):
    col = lambda base: pl.BlockSpec((T, HEAD_DIM), lambda h, d: (0, base * N_HEADS + h))
    f_spec = pl.BlockSpec((T, HEAD_DIM), lambda h, d: (0, COL_FFW * N_HEADS + N_HEADS * d + h))
    l_spec = pl.BlockSpec((None, 2, HEAD_DIM), lambda h, d: (d, 0, h))
    return col, f_spec, l_spec


def hgrn_fwd(proj, lb_logits):
    T = proj.shape[0]
    NC, CPB = T // HGRN_CHUNK, HGRN_BLOCK // HGRN_CHUNK
    col, f_spec, l_spec = _hgrn_specs(T)

    def body(l_ref, q_ref, f_ref, v_ref, o_ref, st_ref, dec_ref, qd_ref):
        d = pl.program_id(1)
        lb = _hgrn_lower_bound(l_ref)
        mask, mtri, _ = _hgrn_chunk_masks(d)

        def block(i, carry):
            rows = pl.ds(pl.multiple_of(i * HGRN_BLOCK, HGRN_BLOCK), HGRN_BLOCK)
            _, _, _, lf, k = _hgrn_gate(f_ref[rows, :], lb)
            b = _tri_sum(mtri, lf, 3)
            bl = _chunk_total(lf)
            qd = (q_ref[rows, :] * Q_SCALE * jnp.exp(b)).astype(BF16)
            kd = (k * jnp.exp(-b)).astype(BF16)
            ke = (k * jnp.exp(bl - b)).astype(BF16)
            vb = v_ref[rows, :].astype(BF16)
            att = jnp.where(mask, _nt(qd, kd), 0.0).astype(BF16)
            o_ref[rows, :] = jnp.where(d == 0, 0.0, o_ref[rows, :]) + _nn(att, vb)
            qd_ref[rows, :] = qd
            dec = jnp.exp(bl)
            for cc in range(CPB):
                sl = slice(cc * HGRN_CHUNK, (cc + 1) * HGRN_CHUNK)
                n = i * CPB + cc
                st_ref[n] = _tn(vb[sl], ke[sl])
                dec_ref[n] = dec[cc * HGRN_CHUNK:cc * HGRN_CHUNK + 8, :]
            return carry

        lax.fori_loop(0, T // HGRN_BLOCK, block, 0)

        def scan(t, s):
            n = jnp.where(d == 0, t, NC - 1 - t)
            u = st_ref[n]
            st_ref[n] = s
            return dec_ref[n][0:1, :] * s + u

        lax.fori_loop(0, NC, scan, jnp.zeros((HEAD_DIM, HEAD_DIM), F32))

        def inter(i, carry):
            rows = pl.ds(pl.multiple_of(i * HGRN_BLOCK, HGRN_BLOCK), HGRN_BLOCK)
            qd = qd_ref[rows, :]
            o_ref[rows, :] += jnp.concatenate(
                [_nt(qd[cc * HGRN_CHUNK:(cc + 1) * HGRN_CHUNK], st_ref[i * CPB + cc].astype(BF16)) for cc in range(CPB)], axis=0)
            return carry

        lax.fori_loop(0, T // HGRN_BLOCK, inter, 0)

    return pl.pallas_call(
        body, name="hgrn_fwd", out_shape=jax.ShapeDtypeStruct((T, N_HEADS * HEAD_DIM), F32), grid=(N_HEADS, 2),
        in_specs=[l_spec, col(COL_Q), f_spec, col(COL_V)],
        out_specs=pl.BlockSpec((T, HEAD_DIM), lambda h, d: (0, h)),
        scratch_shapes=[pltpu.VMEM((NC, HEAD_DIM, HEAD_DIM), F32), pltpu.VMEM((NC, 8, HEAD_DIM), F32),
                        pltpu.VMEM((T, HEAD_DIM), BF16)],
        compiler_params=_cp("parallel", "arbitrary"),
    )(lb_logits, proj, proj, proj)


def hgrn_post_fwd(o, proj, g_norm):
    T, W = o.shape
    tm = min(256, T)

    def body(o_ref, og_ref, g_ref, y_ref):
        g = g_ref[...]
        for h in range(N_HEADS):
            sl = slice(h * HEAD_DIM, (h + 1) * HEAD_DIM)
            x = o_ref[:, sl]
            r = lax.rsqrt(jnp.mean(x * x, axis=-1, keepdims=True) + EPS)
            og = og_ref[:, sl]
            y_ref[:, sl] = ((x * r) * g * (og * jax.nn.sigmoid(og))).astype(BF16)

    return pl.pallas_call(
        body, name="hgrn_post_fwd", out_shape=jax.ShapeDtypeStruct((T, W), BF16), grid=(T // tm,),
        in_specs=[pl.BlockSpec((tm, W), lambda i: (i, 0)), pl.BlockSpec((tm, W), lambda i: (i, COL_OG)), _vec(HEAD_DIM)],
        out_specs=pl.BlockSpec((tm, W), lambda i: (i, 0)), compiler_params=_cp("parallel"),
    )(o, proj, g_norm)


def _gelu(x):
    return 0.5 * x * (1.0 + lax.erf(x * (1.0 / math.sqrt(2.0))))


def _gelu_grad(x):
    return 0.5 * (1.0 + lax.erf(x * (1.0 / math.sqrt(2.0)))) + x * jnp.exp(-0.5 * x * x) * (1.0 / math.sqrt(2.0 * math.pi))


def _sgu_mix(u_ref, v_ref, g_ref, ws_ref, bst_ref):
    W = u_ref.shape[1]
    zu, zv = _gelu(u_ref[...]), _gelu(v_ref[...])
    dv = zv - jnp.mean(zv, axis=-1, keepdims=True)
    rstd = lax.rsqrt(jnp.mean(dv * dv, axis=-1, keepdims=True) + EPS)
    dhat = dv * rstd
    vn = (dhat * g_ref[...]).astype(BF16)
    gw = W // SGU_GROUPS
    vm = [_nn(ws_ref[g].astype(BF16), vn[:, g * gw:(g + 1) * gw]) + bst_ref[:, g:g + 1] for g in range(SGU_GROUPS)]
    return zu, rstd, dhat, vn, jnp.concatenate(vm, axis=1)


def sgu_fwd(proj, g_norm, w_spatial, b_spatial_t):
    T = proj.shape[0]
    W = 1024
    n_chunks = T // SGU_CHUNK

    def body(u_ref, v_ref, g_ref, ws_ref, bst_ref, y_ref):
        zu, _, _, _, vm = _sgu_mix(u_ref, v_ref, g_ref, ws_ref, bst_ref)
        y_ref[...] = (zu * vm).astype(BF16)

    blk = lambda cb: pl.BlockSpec((SGU_CHUNK, W), lambda i: (i, cb))
    return pl.pallas_call(
        body, name="sgu_fwd", out_shape=jax.ShapeDtypeStruct((T, W), BF16), grid=(n_chunks,),
        in_specs=[blk(COL_U), blk(COL_ZV), _vec(W), pl.BlockSpec((SGU_GROUPS, SGU_CHUNK, SGU_CHUNK), lambda i: (0, 0, 0)),
                  pl.BlockSpec((SGU_CHUNK, SGU_GROUPS), lambda i: (0, 0))],
        out_specs=blk(0), compiler_params=_cp("parallel"),
    )(proj, proj, g_norm, w_spatial, b_spatial_t)


def merge_matmul(ya_pre, sgu, w_a, w_b, proj):
    T, K = ya_pre.shape
    N = w_a.shape[1]
    tm, tn = min(512, T), 512
    gpb = 1024 // tn

    def body(a_ref, b_ref, wa_ref, wb_ref, ga_ref, gb_ref, ya_ref, yb_ref, m_ref):
        ya = _nn(a_ref[...], wa_ref[...])
        yb = _nn(b_ref[...], wb_ref[...])
        ya_ref[...] = ya
        yb_ref[...] = yb
        m_ref[...] = (jax.nn.sigmoid(ga_ref[...]) * ya + jax.nn.sigmoid(gb_ref[...]) * yb).astype(BF16)

    lhs = pl.BlockSpec((tm, K), lambda i, j: (i, 0))
    rhs = pl.BlockSpec((K, tn), lambda i, j: (0, j))
    out = pl.BlockSpec((tm, tn), lambda i, j: (i, j))
    return pl.pallas_call(
        body, name="merge_matmul", grid=(T // tm, N // tn),
        out_shape=[jax.ShapeDtypeStruct((T, N), F32), jax.ShapeDtypeStruct((T, N), F32), jax.ShapeDtypeStruct((T, N), BF16)],
        in_specs=[lhs, lhs, rhs, rhs, pl.BlockSpec((tm, tn), lambda i, j: (i, COL_GA * gpb + j)),
                  pl.BlockSpec((tm, tn), lambda i, j: (i, COL_GB * gpb + j))],
        out_specs=[out, out, out], compiler_params=_cp("parallel", "parallel"),
    )(ya_pre, sgu, w_a, w_b, proj, proj)


def out_proj(merged, w_o, h0, gt1, g_post):
    T, D = h0.shape
    tm = min(256, T)

    def body(m_ref, w_ref, h_ref, gt_ref, gp_ref, mo_ref, h1_ref):
        mo = _nn(m_ref[...], w_ref[...])
        mo_ref[...] = mo
        r = lax.rsqrt(jnp.mean(mo * mo, axis=-1, keepdims=True) + EPS)
        h1_ref[...] = h_ref[...] + gt_ref[...] * ((mo * r) * gp_ref[...])

    row = pl.BlockSpec((tm, D), lambda i: (i, 0))
    return pl.pallas_call(
        body, name="out_proj", grid=(T // tm,),
        out_shape=[jax.ShapeDtypeStruct((T, D), F32), jax.ShapeDtypeStruct((T, D), F32)],
        in_specs=[row, pl.BlockSpec((D, D), lambda i: (0, 0)), row, _vec(D), _vec(D)],
        out_specs=[row, row], compiler_params=_cp("parallel"),
    )(merged, w_o, h0, gt1, g_post)


def ff2_loss(hid, w_ff2, h1, tgt, gt2, g_post):
    T, K = hid.shape
    D = w_ff2.shape[1]
    tm, tk = min(256, T), 2048
    nk = K // tk

    def body(a_ref, w_ref, h_ref, t_ref, gt_ref, g_ref, dy_ref, dff_ref, loss_ref, dgt_ref, dg_ref, acc_ref):
        i, k = pl.program_id(0), pl.program_id(1)

        @pl.when(k == 0)
        def _():
            acc_ref[...] = jnp.zeros_like(acc_ref)

        @pl.when((k == 0) & (i == 0))
        def _():
            loss_ref[...] = jnp.zeros_like(loss_ref)
            dgt_ref[...] = jnp.zeros_like(dgt_ref)
            dg_ref[...] = jnp.zeros_like(dg_ref)

        acc_ref[...] += _nn(a_ref[...], w_ref[...])

        @pl.when(k == nk - 1)
        def _():
            ff = acc_ref[...]
            gt, g = gt_ref[...], g_ref[...]
            r = lax.rsqrt(jnp.mean(ff * ff, axis=-1, keepdims=True) + EPS)
            fhat = ff * r
            nf = fhat * g
            err = (h_ref[...] + gt * nf) - t_ref[...]
            loss_ref[...] += jnp.sum(err * err)
            dy = err * (1.0 / D)
            dy_ref[...] = dy
            dgt_ref[...] += _colsum(dy * nf)
            dnf = dy * gt
            dg_ref[...] += _colsum(dnf * fhat)
            u = dnf * g
            dff_ref[...] = (r * (u - fhat * jnp.mean(u * fhat, axis=-1, keepdims=True))).astype(BF16)

    row = pl.BlockSpec((tm, D), lambda i, k: (i, 0))
    vec = pl.BlockSpec((1, D), lambda i, k: (0, 0))
    return pl.pallas_call(
        body, name="ff2_loss", grid=(T // tm, nk),
        out_shape=[jax.ShapeDtypeStruct((T, D), F32), jax.ShapeDtypeStruct((T, D), BF16), jax.ShapeDtypeStruct((8, 128), F32),
                   jax.ShapeDtypeStruct((1, D), F32), jax.ShapeDtypeStruct((1, D), F32)],
        in_specs=[pl.BlockSpec((tm, tk), lambda i, k: (i, k)), pl.BlockSpec((tk, D), lambda i, k: (k, 0)), row, row, vec, vec],
        out_specs=[row, row, pl.BlockSpec((8, 128), lambda i, k: (0, 0)), vec, vec],
        scratch_shapes=[pltpu.VMEM((tm, D), F32)], compiler_params=_cp("arbitrary", "arbitrary"),
    )(hid, w_ff2, h1, tgt, gt2, g_post)


def ff2_bwd(dff, w_ff2, f1):
    T, D = dff.shape
    K = w_ff2.shape[0]
    tm, tn = min(512, T), 1024

    def body(a_ref, w_ref, f_ref, o_ref):
        o_ref[...] = (_nt(a_ref[...], w_ref[...]) * (2.0 * jnp.maximum(f_ref[...], 0.0))).astype(BF16)

    return pl.pallas_call(
        body, name="ff2_bwd", out_shape=jax.ShapeDtypeStruct((T, K), BF16), grid=(T // tm, K // tn),
        in_specs=[pl.BlockSpec((tm, D), lambda i, j: (i, 0)), pl.BlockSpec((tn, D), lambda i, j: (j, 0)),
                  pl.BlockSpec((tm, tn), lambda i, j: (i, j))],
        out_specs=pl.BlockSpec((tm, tn), lambda i, j: (i, j)), compiler_params=_cp("parallel", "parallel"),
    )(dff, w_ff2, f1)


def ffn_norm_bwd(dy, da2, h1, mo, g_pre2, sc2, gt1, g_post):
    T, D = dy.shape
    tm = min(256, T)

    def body(dy_ref, da_ref, h_ref, mo_ref, g2_ref, sc_ref, gt_ref, gp_ref, dh_ref, dmo_ref, s_sh, s_sc, s_g2, s_gt, s_gp):
        @pl.when(pl.program_id(0) == 0)
        def _():
            for s in (s_sh, s_sc, s_g2, s_gt, s_gp):
                s[...] = jnp.zeros_like(s)

        h1, da = h_ref[...], da_ref[...]
        g2, sc = g2_ref[...], sc_ref[...]
        r2 = lax.rsqrt(jnp.mean(h1 * h1, axis=-1, keepdims=True) + EPS)
        n2 = h1 * r2
        s_sh[...] += _colsum(da)
        s_sc[...] += _colsum(da * (n2 * g2))
        s_g2[...] += _colsum(da * (1.0 + sc) * n2)
        dn2 = da * g2 * (1.0 + sc)
        dh1 = dy_ref[...] + r2 * (dn2 - n2 * jnp.mean(dn2 * n2, axis=-1, keepdims=True))
        dh_ref[...] = dh1
        mo = mo_ref[...]
        gt, gp = gt_ref[...], gp_ref[...]
        r = lax.rsqrt(jnp.mean(mo * mo, axis=-1, keepdims=True) + EPS)
        mhat = mo * r
        s_gt[...] += _colsum(dh1 * (mhat * gp))
        dnm = dh1 * gt
        s_gp[...] += _colsum(dnm * mhat)
        u = dnm * gp
        dmo_ref[...] = (r * (u - mhat * jnp.mean(u * mhat, axis=-1, keepdims=True))).astype(BF16)

    row = pl.BlockSpec((tm, D), lambda i: (i, 0))
    vec_out = jax.ShapeDtypeStruct((1, D), F32)
    return pl.pallas_call(
        body, name="ffn_norm_bwd", grid=(T // tm,),
        out_shape=[jax.ShapeDtypeStruct((T, D), F32), jax.ShapeDtypeStruct((T, D), BF16)] + [vec_out] * 5,
        in_specs=[row, row, row, row] + [_vec(D)] * 4, out_specs=[row, row] + [_vec(D)] * 5,
        compiler_params=_cp("arbitrary"),
    )(dy, da2, h1, mo, g_pre2, sc2, gt1, g_post)


def out_proj_bwd(dmo, w_o, y_a, y_b, proj):
    T, D = dmo.shape
    tm, tn = min(512, T), 512
    gpb = 1024 // tn

    def body(a_ref, w_ref, ya_ref, yb_ref, ga_ref, gb_ref, dya_ref, dyb_ref, dga_ref, dgb_ref):
        dm = _nt(a_ref[...], w_ref[...])
        sa, sb = jax.nn.sigmoid(ga_ref[...]), jax.nn.sigmoid(gb_ref[...])
        dya_ref[...] = (dm * sa).astype(BF16)
        dyb_ref[...] = (dm * sb).astype(BF16)
        dga_ref[...] = (dm * ya_ref[...] * sa * (1.0 - sa)).astype(BF16)
        dgb_ref[...] = (dm * yb_ref[...] * sb * (1.0 - sb)).astype(BF16)

    out = pl.BlockSpec((tm, tn), lambda i, j: (i, j))
    return pl.pallas_call(
        body, name="out_proj_bwd", grid=(T // tm, D // tn), out_shape=[jax.ShapeDtypeStruct((T, D), BF16)] * 4,
        in_specs=[pl.BlockSpec((tm, D), lambda i, j: (i, 0)), pl.BlockSpec((tn, D), lambda i, j: (j, 0)), out, out,
                  pl.BlockSpec((tm, tn), lambda i, j: (i, COL_GA * gpb + j)), pl.BlockSpec((tm, tn), lambda i, j: (i, COL_GB * gpb + j))],
        out_specs=[out] * 4, compiler_params=_cp("parallel", "parallel"),
    )(dmo, w_o, y_a, y_b, proj, proj)


def sgu_bwd(proj, dsgu, g_norm, w_spatial, b_spatial_t):
    T = proj.shape[0]
    W = 1024
    gw = W // SGU_GROUPS

    def body(u_ref, v_ref, ds_ref, g_ref, ws_ref, bst_ref, dz_ref, dw_ref, db_ref, dg_ref):
        @pl.when(pl.program_id(0) == 0)
        def _():
            dw_ref[...] = jnp.zeros_like(dw_ref)
            db_ref[...] = jnp.zeros_like(db_ref)
            dg_ref[...] = jnp.zeros_like(dg_ref)

        zu, rstd, dhat, vn, vm = _sgu_mix(u_ref, v_ref, g_ref, ws_ref, bst_ref)
        ds = ds_ref[...]
        du = ds * vm
        dvm = ds * zu
        dvm_b = dvm.astype(BF16)
        ones = jnp.ones((8, gw), F32)
        dvn = []
        for g in range(SGU_GROUPS):
            sl = slice(g * gw, (g + 1) * gw)
            dw_ref[g] += _nt(dvm_b[:, sl], vn[:, sl])
            db_ref[g] += lax.dot_general(ones, dvm[:, sl], (((1,), (1,)), ((), ())), precision=HI, preferred_element_type=F32)
            dvn.append(_tn(ws_ref[g].astype(BF16), dvm_b[:, sl]))
        dvn = jnp.concatenate(dvn, axis=1)
        dg_ref[...] += _colsum(dvn * dhat)
        ddh = dvn * g_ref[...]
        dzv = rstd * (ddh - jnp.mean(ddh, axis=-1, keepdims=True) - dhat * jnp.mean(ddh * dhat, axis=-1, keepdims=True))
        dz_ref[:, 0:W] = (du * _gelu_grad(u_ref[...])).astype(BF16)
        dz_ref[:, W:2 * W] = (dzv * _gelu_grad(v_ref[...])).astype(BF16)

    blk = lambda cb: pl.BlockSpec((SGU_CHUNK, W), lambda i: (i, cb))
    full3 = lambda a, b, c: pl.BlockSpec((a, b, c), lambda i: (0, 0, 0))
    return pl.pallas_call(
        body, name="sgu_bwd", grid=(T // SGU_CHUNK,),
        out_shape=[jax.ShapeDtypeStruct((T, 2 * W), BF16), jax.ShapeDtypeStruct((SGU_GROUPS, SGU_CHUNK, SGU_CHUNK), F32),
                   jax.ShapeDtypeStruct((SGU_GROUPS, 8, SGU_CHUNK), F32), jax.ShapeDtypeStruct((1, W), F32)],
        in_specs=[blk(COL_U), blk(COL_ZV), blk(0), _vec(W), full3(SGU_GROUPS, SGU_CHUNK, SGU_CHUNK),
                  pl.BlockSpec((SGU_CHUNK, SGU_GROUPS), lambda i: (0, 0))],
        out_specs=[pl.BlockSpec((SGU_CHUNK, 2 * W), lambda i: (i, 0)), full3(SGU_GROUPS, SGU_CHUNK, SGU_CHUNK),
                   full3(SGU_GROUPS, 8, SGU_CHUNK), _vec(W)],
        compiler_params=_cp("arbitrary"),
    )(proj, proj, dsgu, g_norm, w_spatial, b_spatial_t)


def hgrn_post_bwd(dya, o, proj, g_norm):
    T, W = o.shape
    tm = min(256, T)

    def body(dy_ref, o_ref, og_ref, g_ref, do_ref, dog_ref, dg_ref):
        @pl.when(pl.program_id(0) == 0)
        def _():
            dg_ref[...] = jnp.zeros_like(dg_ref)

        g = g_ref[...]
        dg = jnp.zeros((1, HEAD_DIM), F32)
        for h in range(N_HEADS):
            sl = slice(h * HEAD_DIM, (h + 1) * HEAD_DIM)
            x, og, dy = o_ref[:, sl], og_ref[:, sl], dy_ref[:, sl]
            r = lax.rsqrt(jnp.mean(x * x, axis=-1, keepdims=True) + EPS)
            xhat = x * r
            s = jax.nn.sigmoid(og)
            don = dy * (og * s)
            dog_ref[:, sl] = (dy * (xhat * g) * (s * (1.0 + og * (1.0 - s)))).astype(BF16)
            dg += _colsum(don * xhat)
            u = don * g
            do_ref[:, sl] = r * (u - xhat * jnp.mean(u * xhat, axis=-1, keepdims=True))
        dg_ref[...] += dg

    row = pl.BlockSpec((tm, W), lambda i: (i, 0))
    return pl.pallas_call(
        body, name="hgrn_post_bwd", grid=(T // tm,),
        out_shape=[jax.ShapeDtypeStruct((T, W), F32), jax.ShapeDtypeStruct((T, W), BF16), jax.ShapeDtypeStruct((1, HEAD_DIM), F32)],
        in_specs=[row, row, pl.BlockSpec((tm, W), lambda i: (i, COL_OG)), _vec(HEAD_DIM)],
        out_specs=[row, row, _vec(HEAD_DIM)], compiler_params=_cp("arbitrary"),
    )(dya, o, proj, g_norm)


def hgrn_bwd(proj, do, lb_logits):
    T = proj.shape[0]
    NC, CPB = T // HGRN_CHUNK, HGRN_BLOCK // HGRN_CHUNK
    W = N_HEADS * HEAD_DIM
    col, f_spec, l_spec = _hgrn_specs(T)

    def body(l_ref, q_ref, f_ref, v_ref, do_ref, dq_ref, dv_ref, dlg_ref, dlb_ref, st_ref, dst_ref, dec_ref, ddec_ref, dqa_ref, dva_ref):
        d = pl.program_id(1)
        lb = _hgrn_lower_bound(l_ref)
        oml = 1.0 - lb
        mask, mtri, mtri_t = _hgrn_chunk_masks(d)

        def values(rows):
            s, sn, fg, lf, k = _hgrn_gate(f_ref[rows, :], lb)
            b = _tri_sum(mtri, lf, 3)
            bl = _chunk_total(lf)
            eb, enb, ee = jnp.exp(b), jnp.exp(-b), jnp.exp(bl - b)
            qd = q_ref[rows, :] * Q_SCALE * eb
            return s, sn, fg, k, bl, eb, enb, ee, qd, k * enb, k * ee

        def block1(i, carry):
            rows = pl.ds(pl.multiple_of(i * HGRN_BLOCK, HGRN_BLOCK), HGRN_BLOCK)
            _, _, _, _, bl, _, _, _, qd, _, ke = values(rows)
            qd, ke = qd.astype(BF16), ke.astype(BF16)
            vb, dob = v_ref[rows, :].astype(BF16), do_ref[rows, :].astype(BF16)
            dec = jnp.exp(bl)
            for cc in range(CPB):
                sl = slice(cc * HGRN_CHUNK, (cc + 1) * HGRN_CHUNK)
                n = i * CPB + cc
                st_ref[n] = _tn(vb[sl], ke[sl])
                dst_ref[n] = _tn(dob[sl], qd[sl])
                dec_ref[n] = dec[cc * HGRN_CHUNK:cc * HGRN_CHUNK + 8, :]
            return carry

        lax.fori_loop(0, T // HGRN_BLOCK, block1, 0)

        def scan(t, s):
            n = jnp.where(d == 0, t, NC - 1 - t)
            u = st_ref[n]
            st_ref[n] = s
            return dec_ref[n][0:1, :] * s + u

        lax.fori_loop(0, NC, scan, jnp.zeros((HEAD_DIM, HEAD_DIM), F32))

        def rscan(t, ds):
            n = jnp.where(d == 0, NC - 1 - t, t)
            w = dst_ref[n]
            dst_ref[n] = ds
            ddec_ref[n] = jnp.broadcast_to(_colsum(ds * st_ref[n]), (8, HEAD_DIM))
            return dec_ref[n][0:1, :] * ds + w

        lax.fori_loop(0, NC, rscan, jnp.zeros((HEAD_DIM, HEAD_DIM), F32))

        def block3(i, dlb):
            rows = pl.ds(pl.multiple_of(i * HGRN_BLOCK, HGRN_BLOCK), HGRN_BLOCK)
            s, sn, fg, k, bl, eb, enb, ee, qd, kd, ke = values(rows)
            qdb, kdb, keb = qd.astype(BF16), kd.astype(BF16), ke.astype(BF16)
            vb, dob = v_ref[rows, :].astype(BF16), do_ref[rows, :].astype(BF16)
            att = jnp.where(mask, _nt(qdb, kdb), 0.0).astype(BF16)
            datt = jnp.where(mask, _nt(dob, vb), 0.0).astype(BF16)
            dv = _tn(att, dob)
            dqd = _nn(datt, kdb)
            dkd = _tn(datt, qdb)
            dv_i, dqd_i, dke, ddl = [], [], [], []
            for cc in range(CPB):
                sl = slice(cc * HGRN_CHUNK, (cc + 1) * HGRN_CHUNK)
                n = i * CPB + cc
                st_b, dst_b = st_ref[n].astype(BF16), dst_ref[n].astype(BF16)
                dv_i.append(_nt(keb[sl], dst_b))
                dqd_i.append(_nn(dob[sl], st_b))
                dke.append(_nn(vb[sl], dst_b))
                ddl.append(jnp.broadcast_to(ddec_ref[n][0:1, :] * dec_ref[n][0:1, :], (HGRN_CHUNK, HEAD_DIM)))
            dv = dv + jnp.concatenate(dv_i, axis=0)
            dqd = dqd + jnp.concatenate(dqd_i, axis=0)
            dke = jnp.concatenate(dke, axis=0)
            dq = dqd * eb * Q_SCALE
            dk = dkd * enb + dke * ee
            t_end = dke * ke
            db = dqd * qd - dkd * kd - t_end
            dlf = _tri_sum(mtri_t, db, 2) + _chunk_total(t_end) + jnp.concatenate(ddl, axis=0)
            e = dlf / fg - dk
            dlg_ref[rows, :] = (oml * e * s * sn).astype(BF16)

            @pl.when(d == 0)
            def _():
                dqa_ref[rows, :] = dq
                dva_ref[rows, :] = dv

            @pl.when(d == 1)
            def _():
                dq_ref[rows, :] = (dqa_ref[rows, :] + dq).astype(BF16)
                dv_ref[rows, :] = (dva_ref[rows, :] + dv).astype(BF16)

            return dlb + _colsum(e * sn)

        dlb_ref[...] = lax.fori_loop(0, T // HGRN_BLOCK, block3, jnp.zeros((1, HEAD_DIM), F32))

    head = pl.BlockSpec((T, HEAD_DIM), lambda h, d: (0, h))
    big = pltpu.VMEM((NC, HEAD_DIM, HEAD_DIM), F32)
    small = pltpu.VMEM((NC, 8, HEAD_DIM), F32)
    acc = pltpu.VMEM((T, HEAD_DIM), F32)
    return pl.pallas_call(
        body, name="hgrn_bwd", grid=(N_HEADS, 2),
        out_shape=[jax.ShapeDtypeStruct((T, W), BF16), jax.ShapeDtypeStruct((T, W), BF16), jax.ShapeDtypeStruct((T, 2 * W), BF16),
                   jax.ShapeDtypeStruct((2, 1, W), F32)],
        in_specs=[l_spec, col(COL_Q), f_spec, col(COL_V), head],
        out_specs=[head, head, pl.BlockSpec((T, HEAD_DIM), lambda h, d: (0, N_HEADS * d + h)),
                   pl.BlockSpec((None, 1, HEAD_DIM), lambda h, d: (d, 0, h))],
        scratch_shapes=[big, big, small, small, acc, acc], compiler_params=_cp("parallel", "arbitrary"),
    )(lb_logits, proj, proj, proj, do)


def mix_norm_bwd(da1, h0, dh1, g_pre, sc1):
    T, D = h0.shape
    tm = min(256, T)

    def body(da_ref, h_ref, dh_ref, g_ref, sc_ref, gx_ref, s_sh, s_sc, s_g):
        @pl.when(pl.program_id(0) == 0)
        def _():
            for s in (s_sh, s_sc, s_g):
                s[...] = jnp.zeros_like(s)

        h, da = h_ref[...], da_ref[...]
        g, sc = g_ref[...], sc_ref[...]
        r = lax.rsqrt(jnp.mean(h * h, axis=-1, keepdims=True) + EPS)
        n = h * r
        s_sh[...] += _colsum(da)
        s_sc[...] += _colsum(da * (n * g))
        s_g[...] += _colsum(da * (1.0 + sc) * n)
        dn = da * g * (1.0 + sc)
        gx_ref[...] = dh_ref[...] + r * (dn - n * jnp.mean(dn * n, axis=-1, keepdims=True))

    row = pl.BlockSpec((tm, D), lambda i: (i, 0))
    return pl.pallas_call(
        body, name="mix_norm_bwd", grid=(T // tm,),
        out_shape=[jax.ShapeDtypeStruct((T, D), F32)] + [jax.ShapeDtypeStruct((1, D), F32)] * 3,
        in_specs=[row, row, row, _vec(D), _vec(D)], out_specs=[row] + [_vec(D)] * 3, compiler_params=_cp("arbitrary"),
    )(da1, h0, dh1, g_pre, sc1)


def adamw(w, g, m, v, name):
    R, C = w.shape
    tr = R if R * C * 4 <= (1 << 21) else max(8, ((1 << 21) // (C * 4)) // 8 * 8)
    while R % tr:
        tr -= 8

    def body(w_ref, g_ref, m_ref, v_ref, d_ref, m2_ref, v2_ref):
        d_ref[...], m2_ref[...], v2_ref[...] = _adamw(w_ref[...], g_ref[...], m_ref[...], v_ref[...])

    row = pl.BlockSpec((tr, C), lambda i: (i, 0))
    return pl.pallas_call(
        body, name=name, grid=(R // tr,), out_shape=[jax.ShapeDtypeStruct((R, C), F32)] * 3,
        in_specs=[row] * 4, out_specs=[row] * 3, compiler_params=_cp("parallel"),
    )(w, g, m, v)


def wada_update(c_all, dmod, w, m, v):
    D, N = w.shape
    tm, tn = 512, 1024

    def body(c_ref, dm_ref, w_ref, m_ref, v_ref, g_ref, d_ref, m2_ref, v2_ref):
        c = c_ref[...]
        g = lax.dot_general(c * jax.nn.sigmoid(c), dm_ref[...], (((0,), (0,)), ((), ())), precision=HI, preferred_element_type=F32)
        g_ref[...] = g
        d_ref[...], m2_ref[...], v2_ref[...] = _adamw(w_ref[...], g, m_ref[...], v_ref[...])

    blk = pl.BlockSpec((tm, tn), lambda i, j: (i, j))
    return pl.pallas_call(
        body, name="wada_update", grid=(D // tm, N // tn), out_shape=[jax.ShapeDtypeStruct((D, N), F32)] * 4,
        in_specs=[pl.BlockSpec((8, tm), lambda i, j: (0, i)), pl.BlockSpec((8, tn), lambda i, j: (0, j)), blk, blk, blk],
        out_specs=[blk] * 4, compiler_params=_cp("parallel", "parallel"),
    )(c_all, dmod, w, m, v)


def sum_devices(gathered):
    n, R, C = gathered.shape

    def body(g_ref, o_ref):
        s = g_ref[0]
        for i in range(1, n):
            s = s + g_ref[i]
        o_ref[...] = s

    return pl.pallas_call(body, name="sum_devices", out_shape=jax.ShapeDtypeStruct((R, C), F32), compiler_params=_cp())(gathered)


def lb_logits_grad(dlb, lb_logits):
    def body(d_ref, l_ref, o_ref):
        for d in range(2):
            l0, l1 = l_ref[d, 0:1, :], l_ref[d, 1:2, :]
            m = jnp.maximum(l0, l1)
            e0, e1 = jnp.exp(l0 - m), jnp.exp(l1 - m)
            p0, p1 = e0 / (e0 + e1), e1 / (e0 + e1)
            g = d_ref[d:d + 1, :]
            o_ref[d, 0:1, :] = p0 * (g - p0 * g)
            o_ref[d, 1:2, :] = -p1 * (p0 * g)

    return pl.pallas_call(body, name="lb_logits_grad", out_shape=jax.ShapeDtypeStruct(lb_logits.shape, F32), compiler_params=_cp())(dlb, lb_logits)


def add_halves(g, landed, core):
    nj, _, r, cc = g.shape
    tr = min(256, r)

    def body(core_ref, g_ref, l_ref, o_ref):
        o_ref[...] = (g_ref[...].astype(F32) + l_ref[...].astype(F32)).astype(BF16)

    return pl.pallas_call(
        body, name="add_halves_%dx%d" % (r, cc), out_shape=jax.ShapeDtypeStruct((nj, r, cc), BF16),
        grid_spec=pltpu.PrefetchScalarGridSpec(
            num_scalar_prefetch=1, grid=(nj, r // tr),
            in_specs=[pl.BlockSpec((None, None, tr, cc), lambda j, i, core_ref: (j, core_ref[0], i, 0)),
                      pl.BlockSpec((None, None, tr, cc), lambda j, i, core_ref: (j, 0, i, 0))],
            out_specs=pl.BlockSpec((None, tr, cc), lambda j, i, core_ref: (j, i, 0))),
        compiler_params=_cp("parallel", "parallel"),
    )(core, g, landed)


def sum_chips(parts, landed, chip):
    nj, r, cc = parts.shape
    tr = min(256, r)

    def body(chip_ref, p_ref, l_ref, o_ref):
        mine = p_ref[...].astype(F32)
        s = None
        for j in range(nj):
            t = jnp.where(chip_ref[0] == j, mine, l_ref[j].astype(F32))
            s = t if s is None else s + t
        o_ref[...] = s

    return pl.pallas_call(
        body, name="sum_chips_%dx%d" % (r, cc), out_shape=jax.ShapeDtypeStruct((r, cc), F32),
        grid_spec=pltpu.PrefetchScalarGridSpec(
            num_scalar_prefetch=1, grid=(r // tr,),
            in_specs=[pl.BlockSpec((None, tr, cc), lambda i, chip_ref: (chip_ref[0], i, 0)),
                      pl.BlockSpec((nj, tr, cc), lambda i, chip_ref: (0, i, 0))],
            out_specs=pl.BlockSpec((tr, cc), lambda i, chip_ref: (i, 0))),
        compiler_params=_cp("parallel"),
    )(chip, parts, landed)


def adamw_halves(w, own, other, m, v, core, name):
    r, cc = own.shape
    tr = min(128, r)
    nb = r // tr

    def body(core_ref, w_ref, a_ref, b_ref, m_ref, v_ref, g_ref, d_ref, m2_ref, v2_ref):
        g = jnp.where(pl.program_id(0) == core_ref[0], a_ref[...], b_ref[...])
        g_ref[...] = g
        d_ref[...], m2_ref[...], v2_ref[...] = _adamw(w_ref[...], g, m_ref[...], v_ref[...])

    full = pl.BlockSpec((tr, cc), lambda h, i, core_ref: (h * nb + i, 0))
    half = pl.BlockSpec((tr, cc), lambda h, i, core_ref: (i, 0))
    return pl.pallas_call(
        body, name=name, out_shape=[jax.ShapeDtypeStruct((2 * r, cc), F32)] * 4,
        grid_spec=pltpu.PrefetchScalarGridSpec(
            num_scalar_prefetch=1, grid=(2, nb), in_specs=[full, half, half, full, full], out_specs=[full] * 4),
        compiler_params=_cp("parallel", "parallel"),
    )(core, w, own, other, m, v)


def _place():
    mx, my, mc = lax.axis_index("x"), lax.axis_index("y"), lax.axis_index("c")
    chips = [(1 - mx, my), (mx, 1 - my), (1 - mx, 1 - my)]
    return mx, my, mc, chips


def all_gather_small(x, name):
    R, C = x.shape

    def body(x_ref, out_ref, send_sems, recv_sems, local_sem):
        mx, my, mc, _ = _place()
        me = 4 * mx + 2 * my + mc
        mine = pltpu.make_async_copy(x_ref, out_ref.at[me], local_sem)
        mine.start()

        def peer(k):
            px = 1 - mx if k & 4 else mx
            py = 1 - my if k & 2 else my
            pc = 1 - mc if k & 1 else mc
            return px, py, pc

        def copy(k, src, slot):
            return pltpu.make_async_remote_copy(src_ref=src, dst_ref=out_ref.at[slot], send_sem=send_sems.at[k - 1],
                                                recv_sem=recv_sems.at[k - 1], device_id=peer(k), device_id_type=MESH)

        sends = [copy(k, x_ref, me) for k in range(1, 8)]
        for cp in sends:
            cp.start()
        for k in range(1, 8):
            px, py, pc = peer(k)
            slot = 4 * px + 2 * py + pc
            copy(k, out_ref.at[slot], slot).wait_recv()
        for cp in sends:
            cp.wait_send()
        mine.wait()

    return pl.pallas_call(
        body, name=name, out_shape=jax.ShapeDtypeStruct((8, R, C), F32),
        in_specs=[pl.BlockSpec(memory_space=pltpu.VMEM)], out_specs=pl.BlockSpec(memory_space=pltpu.VMEM),
        scratch_shapes=[pltpu.SemaphoreType.DMA((7,)), pltpu.SemaphoreType.DMA((7,)), pltpu.SemaphoreType.DMA],
        compiler_params=_cp(),
    )(x)


def _region(ref, kind, j, half, r, cc):
    nr = r if half is None else r // 2
    off = 0 if half is None else half * nr
    if kind == "col":
        return ref.at[pl.ds(off, nr), pl.ds(pl.multiple_of(j * cc, 128), cc)]
    return ref.at[pl.ds(pl.multiple_of(j * r + off, 16), nr), :]


def gather_weights(fulls, kinds, dims):
    n = len(fulls)

    def body(*refs):
        f_refs = refs[n:2 * n]
        send_sems, recv_sems = refs[2 * n:]
        mx, my, mc, chips = _place()
        jme = 2 * mx + my
        sibling = (mx, my, 1 - mc)

        def landed(w, k, half):
            px, py = chips[k]
            return _region(f_refs[w], kinds[w], 2 * px + py, half, *dims[w])

        def over_ici(w, k, src, dst):
            px, py = chips[k]
            return pltpu.make_async_remote_copy(src_ref=src, dst_ref=dst, send_sem=send_sems.at[6 * w + k], recv_sem=recv_sems.at[6 * w + k],
                                                device_id=(px, py, mc), device_id_type=MESH)

        def over_d2d(w, k, half):
            reg = landed(w, k, half)
            return pltpu.make_async_remote_copy(src_ref=reg, dst_ref=reg, send_sem=send_sems.at[6 * w + 3 + k],
                                                recv_sem=recv_sems.at[6 * w + 3 + k], device_id=sibling, device_id_type=MESH)

        sends = []
        for w in range(n):
            mine = _region(f_refs[w], kinds[w], jme, mc, *dims[w])
            for k in range(3):
                sends.append(over_ici(w, k, mine, mine))
        for cp in sends:
            cp.start()
        passed = []
        for w in range(n):
            for k in range(3):
                over_ici(w, k, landed(w, k, mc), landed(w, k, mc)).wait_recv()
                cp = over_d2d(w, k, mc)
                cp.start()
                passed.append(cp)
        for w in range(n):
            for k in range(3):
                over_d2d(w, k, 1 - mc).wait_recv()
        for cp in sends + passed:
            cp.wait_send()

    return pl.pallas_call(
        body, name="gather_weights", out_shape=[jax.ShapeDtypeStruct(f.shape, BF16) for f in fulls],
        in_specs=[ANY] * n, out_specs=[ANY] * n, input_output_aliases={w: w for w in range(n)},
        scratch_shapes=[pltpu.SemaphoreType.DMA((6 * n,)), pltpu.SemaphoreType.DMA((6 * n,))],
        compiler_params=_cp(),
    )(*fulls)


def exchange_halves(grads):
    n = len(grads)

    def body(*refs):
        g_refs, l_refs = refs[:n], refs[n:2 * n]
        send_sems, recv_sems = refs[2 * n:]
        mx, my, mc, _ = _place()
        cps = [pltpu.make_async_remote_copy(src_ref=g_refs[w].at[:, pl.ds(1 - mc, 1)], dst_ref=l_refs[w], send_sem=send_sems.at[w],
                                            recv_sem=recv_sems.at[w], device_id=(mx, my, 1 - mc), device_id_type=MESH) for w in range(n)]
        for cp in cps:
            cp.start()
        for cp in cps:
            cp.wait()

    return pl.pallas_call(
        body, name="exchange_halves", out_shape=[jax.ShapeDtypeStruct((g.shape[0], 1) + g.shape[2:], BF16) for g in grads],
        in_specs=[ANY] * n, out_specs=[ANY] * n,
        scratch_shapes=[pltpu.SemaphoreType.DMA((n,)), pltpu.SemaphoreType.DMA((n,))], compiler_params=_cp(),
    )(*grads)


def scatter_to_chips(parts):
    n = len(parts)

    def body(*refs):
        p_refs, l_refs = refs[:n], refs[n:2 * n]
        send_sems, recv_sems = refs[2 * n:]
        mx, my, mc, chips = _place()
        jme = 2 * mx + my
        sends = []
        for w in range(n):
            for k, (px, py) in enumerate(chips):
                sends.append(pltpu.make_async_remote_copy(src_ref=p_refs[w].at[2 * px + py], dst_ref=l_refs[w].at[jme],
                                                          send_sem=send_sems.at[3 * w + k], recv_sem=recv_sems.at[3 * w + k],
                                                          device_id=(px, py, mc), device_id_type=MESH))
        for cp in sends:
            cp.start()
        for w in range(n):
            for k, (px, py) in enumerate(chips):
                slot = l_refs[w].at[2 * px + py]
                pltpu.make_async_remote_copy(src_ref=slot, dst_ref=slot, send_sem=send_sems.at[3 * w + k], recv_sem=recv_sems.at[3 * w + k],
                                             device_id=(px, py, mc), device_id_type=MESH).wait_recv()
        for cp in sends:
            cp.wait_send()

    return pl.pallas_call(
        body, name="scatter_to_chips", out_shape=[jax.ShapeDtypeStruct(p.shape, BF16) for p in parts],
        in_specs=[ANY] * n, out_specs=[ANY] * n,
        scratch_shapes=[pltpu.SemaphoreType.DMA((3 * n,)), pltpu.SemaphoreType.DMA((3 * n,))],
        compiler_params=_cp(),
    )(*parts)


def share_with_sibling(sums):
    n = len(sums)

    def body(*refs):
        q_refs, o_refs = refs[:n], refs[n:2 * n]
        send_sems, recv_sems = refs[2 * n:]
        mx, my, mc, _ = _place()
        cps = [pltpu.make_async_remote_copy(src_ref=q_refs[w], dst_ref=o_refs[w], send_sem=send_sems.at[w], recv_sem=recv_sems.at[w],
                                            device_id=(mx, my, 1 - mc), device_id_type=MESH) for w in range(n)]
        for cp in cps:
            cp.start()
        for cp in cps:
            cp.wait()

    return pl.pallas_call(
        body, name="share_with_sibling", out_shape=[jax.ShapeDtypeStruct(q.shape, F32) for q in sums],
        in_specs=[ANY] * n, out_specs=[ANY] * n,
        scratch_shapes=[pltpu.SemaphoreType.DMA((n,)), pltpu.SemaphoreType.DMA((n,))],
        compiler_params=_cp(),
    )(*sums)


def _pack(arrays):
    flat = jnp.concatenate([a.reshape(-1) for a in arrays])
    rows = -(-flat.shape[0] // 1024) * 8
    return jnp.pad(flat, (0, rows * 128 - flat.shape[0])).reshape(rows, 128)


def _unpack(packed, shapes):
    flat, out, off = packed.reshape(-1), [], 0
    for s in shapes:
        n = math.prod(s)
        out.append(flat[off:off + n].reshape(s))
        off += n
    return out


def kernel(x, c, w_ada, b_ada, g_pre_mix, g_post_mix, g_pre_ffn, g_post_ffn, w_in, lb_logits, g_hgrn_norm, w_a_out, g_sgu_norm, w_spatial, b_spatial, w_b_out, w_o, w_ff1, w_ff2, loss_target, m_w_ada, m_b_ada, m_g_pre_mix, m_g_post_mix, m_g_pre_ffn, m_g_post_ffn, m_w_in, m_lb_logits, m_g_hgrn_norm, m_w_a_out, m_g_sgu_norm, m_w_spatial, m_b_spatial, m_w_b_out, m_w_o, m_w_ff1, m_w_ff2, v_w_ada, v_b_ada, v_g_pre_mix, v_g_post_mix, v_g_pre_ffn, v_g_post_ffn, v_w_in, v_lb_logits, v_g_hgrn_norm, v_w_a_out, v_g_sgu_norm, v_w_spatial, v_b_spatial, v_w_b_out, v_w_o, v_w_ff1, v_w_ff2):
    mx, my, mc = lax.axis_index("x"), lax.axis_index("y"), lax.axis_index("c")
    chip, me = 2 * mx + my, 4 * mx + 2 * my + mc
    D = D_MODEL
    h0, tgt = x[0], loss_target[0]
    n_ada = w_ada.shape[2]
    n_lb = lb_logits.shape[2]

    got = all_gather_small(_pack([c, lb_logits]), "gather_inputs")
    c_all = got[:, :D // 128, :].reshape(8, D)
    lb_full = got[0::2, D // 128:D // 128 + 4 * n_lb // 128, :].reshape(4, 2, 2, n_lb).transpose(1, 2, 0, 3).reshape(2, 2, 4 * n_lb)
    b_ada_chip = lax.dynamic_slice(b_ada, (0, chip * n_ada), (1, n_ada))
    mod_cols = mod_matmul(c_all, w_ada[0], b_ada_chip)
    got = all_gather_small(mod_cols.reshape(-1, 128), "gather_mod").reshape(4, 2, 8, n_ada)
    mod = lax.dynamic_index_in_dim(got[:, 0], me, axis=1, keepdims=False).reshape(6, 1, D)
    sh1, sc1, gt1, sh2, sc2, gt2 = (mod[i] for i in range(6))

    big = [("w_in", w_in, "col"), ("w_a_out", w_a_out, "col"), ("w_b_out", w_b_out, "col"), ("w_o", w_o, "row"),
           ("w_ff1", w_ff1, "col"), ("w_ff2", w_ff2, "row")]
    kinds = [k for _, _, k in big]
    chip_idx, core = chip.reshape(1).astype(jnp.int32), mc.reshape(1).astype(jnp.int32)
    w_in_f, w_a_f, w_b_f, w_o_f, w_ff1_f, w_ff2_f = gather_weights(
        [cast_into_full(w[0], kind, chip_idx, "cast_" + nm) for nm, w, kind in big], kinds, [w.shape[1:] for _, w, _ in big])

    bst = b_spatial[0].T
    proj, a1 = prenorm_matmul(h0, g_pre_mix, sc1, sh1, w_in_f, relu2=False, name="in_proj")
    o = hgrn_fwd(proj, lb_full)
    ya_pre = hgrn_post_fwd(o, proj, g_hgrn_norm)
    sgu = sgu_fwd(proj, g_sgu_norm, w_spatial[0], bst)
    y_a, y_b, merged = merge_matmul(ya_pre, sgu, w_a_f, w_b_f, proj)
    mo, h1 = out_proj(merged, w_o_f, h0, gt1, g_post_mix)
    f1, a2, hid = prenorm_matmul(h1, g_pre_ffn, sc2, sh2, w_ff1_f, relu2=True, name="ff1")
    dy, dff, loss_parts, d_gt2, d_g_post_ffn = ff2_loss(hid, w_ff2_f, h1, tgt, gt2, g_post_ffn)
    loss = lax.psum(0.5 * loss_parts[0, 0] / D, ("x", "y", "c"))

    df1 = ff2_bwd(dff, w_ff2_f, f1)
    gr_ff2 = matmul(hid, dff, mode="tn", out_dtype=BF16, tm=1024, tn=1024, tk=512, name="dw_ff2")
    da2 = matmul(df1, w_ff1_f, mode="nt", out_dtype=F32, tm=512, tn=1024, tk=2048, name="da2")
    gr_ff1 = matmul(a2, df1, mode="tn", out_dtype=BF16, tm=1024, tn=2048, tk=512, name="dw_ff1", split=(4, 2))
    dh1, dmo, d_sh2, d_sc2, d_g_pre_ffn, d_gt1, d_g_post_mix = ffn_norm_bwd(dy, da2, h1, mo, g_pre_ffn, sc2, gt1, g_post_mix)
    dya, dyb, dga, dgb = out_proj_bwd(dmo, w_o_f, y_a, y_b, proj)
    gr_o = matmul(merged, dmo, mode="tn", out_dtype=BF16, tm=1024, tn=1024, tk=512, name="dw_o")
    dsgu = matmul(dyb, w_b_f, mode="nt", out_dtype=F32, tm=512, tn=1024, tk=2048, name="dsgu")
    gr_b = matmul(sgu, dyb, mode="tn", out_dtype=BF16, tm=512, tn=512, tk=512, name="dw_b_out", split=(4, 2))
    dz, d_w_spatial, d_b_spatial, d_g_sgu = sgu_bwd(proj, dsgu, g_sgu_norm, w_spatial[0], bst)
    dya_pre = matmul(dya, w_a_f, mode="nt", out_dtype=F32, tm=512, tn=1024, tk=2048, name="dya_pre")
    gr_a = matmul(ya_pre, dya, mode="tn", out_dtype=BF16, tm=512, tn=512, tk=512, name="dw_a_out", split=(4, 2))
    do, dog, d_g_hgrn = hgrn_post_bwd(dya_pre, o, proj, g_hgrn_norm)
    dq, dv, dlg, d_lb = hgrn_bwd(proj, do, lb_full)
    dproj = jnp.concatenate([dq, dlg, dv, dog, dz, dga, dgb], axis=1)
    da1 = matmul(dproj, w_in_f, mode="nt", out_dtype=F32, tm=512, tn=1024, tk=2816, name="da1")
    gr_in = matmul(a1, dproj, mode="tn", out_dtype=BF16, tm=1024, tn=2816, tk=512, name="dw_in", split=(4, 2))
    grad_x, d_sh1, d_sc1, d_g_pre_mix = mix_norm_bwd(da1, h0, dh1, g_pre_mix, sc1)

    grads = [gr_in, gr_a, gr_b, gr_o.reshape(4, 2, -1, D), gr_ff1, gr_ff2.reshape(4, 2, -1, D)]
    landed = exchange_halves(grads)
    parts = [add_halves(g, l, core) for g, l in zip(grads, landed)]
    own = [sum_chips(p, l, chip_idx) for p, l in zip(parts, scatter_to_chips(parts))]
    other = share_with_sibling(own)
    out = {}
    for (nm, w, _), a, b, m, v in zip(big, own, other, (m_w_in, m_w_a_out, m_w_b_out, m_w_o, m_w_ff1, m_w_ff2),
                                      (v_w_in, v_w_a_out, v_w_b_out, v_w_o, v_w_ff1, v_w_ff2)):
        out[nm] = tuple(t[None] for t in adamw_halves(w[0], a, b, m[0], v[0], core, "adamw_" + nm))

    mine = _pack([d_sh1, d_sc1, d_gt1, d_sh2, d_sc2, d_gt2, d_g_pre_mix, d_g_post_mix, d_g_pre_ffn, d_g_post_ffn, d_g_hgrn, d_g_sgu,
                  d_w_spatial, d_b_spatial[:, 0, :], d_lb])
    got = all_gather_small(mine, "gather_small_grads")
    total = sum_devices(got)
    g_b_ada, g_g1, g_g2, g_g3, g_g4, g_hg, g_sg, g_ws, g_bs, g_lb = _unpack(
        total, [(1, 6 * D), (1, D), (1, D), (1, D), (1, D), (1, HEAD_DIM), (1, 1024), w_spatial.shape, b_spatial.shape, (2, 1024)])
    g_lbl = lax.dynamic_slice(lb_logits_grad(g_lb, lb_full), (0, 0, chip * n_lb), (2, 2, n_lb))
    names = ["b_ada", "g_pre_mix", "g_post_mix", "g_pre_ffn", "g_post_ffn", "g_hgrn_norm", "g_sgu_norm", "w_spatial", "b_spatial", "lb_logits"]
    ws = [b_ada, g_pre_mix, g_post_mix, g_pre_ffn, g_post_ffn, g_hgrn_norm, g_sgu_norm, w_spatial, b_spatial, lb_logits]
    gs = [g_b_ada, g_g1, g_g2, g_g3, g_g4, g_hg, g_sg, g_ws, g_bs, g_lbl]
    ms = [m_b_ada, m_g_pre_mix, m_g_post_mix, m_g_pre_ffn, m_g_post_ffn, m_g_hgrn_norm, m_g_sgu_norm, m_w_spatial, m_b_spatial, m_lb_logits]
    vs = [v_b_ada, v_g_pre_mix, v_g_post_mix, v_g_pre_ffn, v_g_post_ffn, v_g_hgrn_norm, v_g_sgu_norm, v_w_spatial, v_b_spatial, v_lb_logits]
    shapes = [w.shape for w in ws]
    upd = adamw(_pack(ws), _pack(gs), _pack(ms), _pack(vs), "adamw_small")
    upd = [_unpack(u, shapes) for u in upd]
    for i, nm in enumerate(names):
        out[nm] = (gs[i], upd[0][i], upd[1][i], upd[2][i])

    dmod_all = got[:, :6 * D // 128, :].reshape(8, 6 * D)
    dmod_chip = lax.dynamic_slice(dmod_all, (0, chip * n_ada), (8, n_ada))
    out["w_ada"] = tuple(a[None] for a in wada_update(c_all, dmod_chip, w_ada[0], m_w_ada[0], v_w_ada[0]))

    order = ["w_ada", "b_ada", "g_pre_mix", "g_post_mix", "g_pre_ffn", "g_post_ffn", "w_in", "lb_logits", "g_hgrn_norm", "w_a_out",
             "g_sgu_norm", "w_spatial", "b_spatial", "w_b_out", "w_o", "w_ff1", "w_ff2"]
    return (loss, grad_x[None], *[out[nm][0] for nm in order], *[out[nm][1] for nm in order], *[out[nm][2] for nm in order],
            *[out[nm][3] for nm in order])
```

```python
import functools
import math

import jax
import jax.numpy as jnp
from jax import lax
from jax.experimental import pallas as pl
from jax.experimental.pallas import tpu as pltpu

F32, BF16 = jnp.float32, jnp.bfloat16
HI = lax.Precision.HIGHEST
MESH = pl.DeviceIdType.MESH
ANY = pl.BlockSpec(memory_space=pl.ANY)

EPS = 1e-6
D_MODEL = 2048
N_HEADS = 8
HEAD_DIM = 128
HGRN_CHUNK = 32
HGRN_BLOCK = 256
SGU_CHUNK = 128
SGU_GROUPS = 8
Q_SCALE = HEAD_DIM ** -0.5
COL_Q, COL_FFW, COL_FBW, COL_V, COL_OG, COL_U, COL_ZV, COL_GA, COL_GB = 0, 1, 2, 3, 4, 5, 6, 7, 9
N_PROJ = 11264
VMEM_BYTES_V7X = 64 * 1024 * 1024
VMEM_LIMIT = VMEM_BYTES_V7X - 8 * 1024 * 1024

ADAM_LR, ADAM_B1, ADAM_B2, ADAM_EPS, ADAM_WD, ADAM_STEP = 0.001, 0.9, 0.999, 1e-08, 0.01, 10
ADAM_C1 = 1.0 - ADAM_B1 ** ADAM_STEP
ADAM_C2 = 1.0 - ADAM_B2 ** ADAM_STEP


def _cp(*sem):
    return pltpu.CompilerParams(dimension_semantics=sem if sem else None, vmem_limit_bytes=VMEM_LIMIT)


def _vec(d):
    return pl.BlockSpec((1, d), lambda *_: (0, 0))


def _colsum(x):
    return jnp.sum(x, axis=0, keepdims=True)


def _nt(a, b):
    return lax.dot_general(a, b, (((1,), (1,)), ((), ())), preferred_element_type=F32)


def _tn(a, b):
    return lax.dot_general(a, b, (((0,), (0,)), ((), ())), preferred_element_type=F32)


def _nn(a, b):
    return jnp.dot(a, b, preferred_element_type=F32)


def _adamw(w, g, m, v):
    m2 = ADAM_B1 * m + (1.0 - ADAM_B1) * g
    v2 = ADAM_B2 * v + (1.0 - ADAM_B2) * (g * g)
    delta = -ADAM_LR * ((m2 / ADAM_C1) / (jnp.sqrt(v2 / ADAM_C2) + ADAM_EPS) + ADAM_WD * w)
    return delta, m2, v2


class _Comm:
    def __init__(self, operands, out_shape, aliases, n_sems, start, finish):
        self.operands, self.out_shape, self.aliases, self.n_sems = list(operands), list(out_shape), dict(aliases), n_sems
        self.start, self.finish = start, finish


def _pallas(body, *, name, grid, in_specs, out_specs, out_shape, scratch, semantics, operands, comm=None):
    if comm is None:
        res = pl.pallas_call(body, name=name, grid=grid, in_specs=in_specs, out_specs=out_specs, out_shape=out_shape,
                             scratch_shapes=scratch, compiler_params=_cp(*semantics))(*operands)
        return res, []
    n_in, n_out, n_scr = len(in_specs), len(out_specs), len(scratch)
    nci, nco = len(comm.operands), len(comm.out_shape)

    def with_comm(*refs):
        ins, rest = refs[:n_in], refs[n_in:]
        cin, rest = rest[:nci], rest[nci:]
        outs, rest = rest[:n_out], rest[n_out:]
        cout, rest = rest[:nco], rest[nco:]
        scr, (send, recv) = rest[:n_scr], rest[n_scr:]
        ids = [pl.program_id(a) for a in range(len(grid))]
        first = functools.reduce(jnp.logical_and, [i == 0 for i in ids])
        last = functools.reduce(jnp.logical_and, [i == g - 1 for i, g in zip(ids, grid)])

        @pl.when(first)
        def _():
            comm.start(cin, cout, send, recv)

        body(*ins, *outs, *scr)

        @pl.when(last)
        def _():
            comm.finish(cin, cout, send, recv)

    res = pl.pallas_call(
        with_comm, name=name, grid=grid, in_specs=list(in_specs) + [ANY] * nci, out_specs=list(out_specs) + [ANY] * nco,
        out_shape=list(out_shape) + comm.out_shape, input_output_aliases={n_in + i: n_out + o for i, o in comm.aliases.items()},
        scratch_shapes=list(scratch) + [pltpu.SemaphoreType.DMA((comm.n_sems,)), pltpu.SemaphoreType.DMA((comm.n_sems,))],
        compiler_params=_cp(*["arbitrary"] * len(grid)),
    )(*operands, *comm.operands)
    return res[:n_out], res[n_out:]


def matmul(a, b, *, mode, out_dtype, tm, tn, tk, name, split=None, comm=None):
    if mode == "tn":
        (K, M), (_, N) = a.shape, b.shape
    elif mode == "nt":
        (M, K), (N, _) = a.shape, b.shape
    else:
        (M, K), (_, N) = a.shape, b.shape
    tm, tn, tk = min(tm, M), min(tn, N), min(tk, K)
    nk = K // tk
    a_spec = pl.BlockSpec((tk, tm), lambda i, j, k: (k, i)) if mode == "tn" else pl.BlockSpec((tm, tk), lambda i, j, k: (i, k))
    b_spec = pl.BlockSpec((tn, tk), lambda i, j, k: (j, k)) if mode == "nt" else pl.BlockSpec((tk, tn), lambda i, j, k: (k, j))
    dot = {"nn": _nn, "nt": _nt, "tn": _tn}[mode]
    if split is None:
        out_shape = jax.ShapeDtypeStruct((M, N), out_dtype)
        out_spec = pl.BlockSpec((tm, tn), lambda i, j, k: (i, j))
    else:
        nj, nh = split
        rows, cols = M // nh, N // nj
        tm, tn = min(tm, rows), min(tn, cols)
        bi, bj = rows // tm, cols // tn
        out_shape = jax.ShapeDtypeStruct((nj, nh, rows, cols), out_dtype)
        out_spec = pl.BlockSpec((None, None, tm, tn), lambda i, j, k: (j // bj, i // bi, i % bi, j % bj))

    def body(a_ref, b_ref, o_ref, acc_ref):
        k = pl.program_id(2)

        @pl.when(k == 0)
        def _():
            acc_ref[...] = jnp.zeros_like(acc_ref)

        acc_ref[...] += dot(a_ref[...], b_ref[...])

        @pl.when(k == nk - 1)
        def _():
            o_ref[...] = acc_ref[...].astype(o_ref.dtype)

    (out,), landed = _pallas(
        body, name=name, grid=(M // tm, N // tn, nk), in_specs=[a_spec, b_spec], out_specs=[out_spec], out_shape=[out_shape],
        scratch=[pltpu.VMEM((tm, tn), F32)], semantics=("parallel", "parallel", "arbitrary"), operands=(a, b), comm=comm)
    return out if comm is None else (out, landed)


def cast_into_full(w, kind, chip, name):
    r, cc = w.shape
    tr = min(r, 512)
    nb = r // tr

    def body(chip_ref, w_ref, o_ref):
        o_ref[...] = w_ref[...].astype(BF16)

    if kind == "col":
        full, out_map = (r, 4 * cc), lambda i, chip_ref: (i, chip_ref[0])
    else:
        full, out_map = (4 * r, cc), lambda i, chip_ref: (chip_ref[0] * nb + i, 0)
    return pl.pallas_call(
        body, name=name, out_shape=jax.ShapeDtypeStruct(full, BF16),
        grid_spec=pltpu.PrefetchScalarGridSpec(
            num_scalar_prefetch=1, grid=(nb,), in_specs=[pl.BlockSpec((tr, cc), lambda i, chip_ref: (i, 0))],
            out_specs=pl.BlockSpec((tr, cc), out_map)),
        compiler_params=_cp("parallel"),
    )(chip, w)


def mod_matmul(c_all, w_ada, b_ada):
    D, N = w_ada.shape
    tn = 1024

    def body(c_ref, w_ref, b_ref, o_ref):
        c = c_ref[...]
        sc = c * jax.nn.sigmoid(c)
        o_ref[...] = jnp.dot(sc, w_ref[...], precision=HI, preferred_element_type=F32) + b_ref[...]

    return pl.pallas_call(
        body, name="mod_matmul", out_shape=jax.ShapeDtypeStruct((8, N), F32), grid=(N // tn,),
        in_specs=[pl.BlockSpec((8, D), lambda j: (0, 0)), pl.BlockSpec((D, tn), lambda j: (0, j)),
                  pl.BlockSpec((1, tn), lambda j: (0, j))],
        out_specs=pl.BlockSpec((8, tn), lambda j: (0, j)), compiler_params=_cp("parallel"),
    )(c_all, w_ada, b_ada)


def prenorm_matmul(h, g, sc, sh, w, *, relu2, name, comm=None):
    T, D = h.shape
    N = w.shape[1]
    tm, tn = min(512, T), 1024

    def body(h_ref, g_ref, sc_ref, sh_ref, w_ref, y_ref, a_ref, *hid_ref):
        @pl.when(pl.program_id(1) == 0)
        def _():
            x = h_ref[...]
            r = lax.rsqrt(jnp.mean(x * x, axis=-1, keepdims=True) + EPS)
            a_ref[...] = ((x * r) * g_ref[...] * (1.0 + sc_ref[...]) + sh_ref[...]).astype(BF16)

        y = _nn(a_ref[...], w_ref[...])
        y_ref[...] = y
        if relu2:
            p = jnp.maximum(y, 0.0)
            hid_ref[0][...] = (p * p).astype(BF16)

    out_shape = [jax.ShapeDtypeStruct((T, N), F32), jax.ShapeDtypeStruct((T, D), BF16)]
    out_specs = [pl.BlockSpec((tm, tn), lambda i, j: (i, j)), pl.BlockSpec((tm, D), lambda i, j: (i, 0))]
    if relu2:
        out_shape.append(jax.ShapeDtypeStruct((T, N), BF16))
        out_specs.append(pl.BlockSpec((tm, tn), lambda i, j: (i, j)))
    outs, landed = _pallas(
        body, name=name, grid=(T // tm, N // tn),
        in_specs=[pl.BlockSpec((tm, D), lambda i, j: (i, 0)), _vec(D), _vec(D), _vec(D), pl.BlockSpec((D, tn), lambda i, j: (0, j))],
        out_specs=out_specs, out_shape=out_shape, scratch=[], semantics=("parallel", "arbitrary"), operands=(h, g, sc, sh, w), comm=comm)
    return outs if comm is None else (outs, landed)


def _hgrn_lower_bound(l_ref):
    l0, l1 = l_ref[0:1, :], l_ref[1:2, :]
    m = jnp.maximum(l0, l1)
    e0, e1 = jnp.exp(l0 - m), jnp.exp(l1 - m)
    return e0 / (e0 + e1)


def _hgrn_chunk_masks(d):
    r = lax.broadcasted_iota(jnp.int32, (HGRN_BLOCK, HGRN_BLOCK), 0)
    c = lax.broadcasted_iota(jnp.int32, (HGRN_BLOCK, HGRN_BLOCK), 1)
    same = (r // HGRN_CHUNK) == (c // HGRN_CHUNK)
    fwd = d == 0
    tri = same & (((c <= r) & fwd) | ((c >= r) & jnp.logical_not(fwd)))
    tri_t = same & (((c >= r) & fwd) | ((c <= r) & jnp.logical_not(fwd)))
    return tri, jnp.where(tri, 1.0, 0.0).astype(BF16), jnp.where(tri_t, 1.0, 0.0).astype(BF16)


def _tri_sum(tri, x, terms):
    pieces, rest = [], x
    for t in range(terms):
        p = rest.astype(BF16)
        pieces.append(p)
        if t + 1 < terms:
            rest = rest - p.astype(F32)
    y = _nn(tri, jnp.concatenate(pieces, axis=1))
    w = x.shape[1]
    return sum(y[:, t * w:(t + 1) * w] for t in range(terms))


def _chunk_total(x):
    x3 = x.reshape(HGRN_BLOCK // HGRN_CHUNK, HGRN_CHUNK, x.shape[1])
    return jnp.broadcast_to(jnp.sum(x3, axis=1, keepdims=True), x3.shape).reshape(x.shape)


def _hgrn_gate(f, lb):
    s = jax.nn.sigmoid(f)
    sn = jax.nn.sigmoid(-f)
    fg = lb + (1.0 - lb) * s
    return s, sn, fg, jnp.log(fg), (1.0 - lb) * sn


def _hgrn_specs(T):
    col = lambda base: pl.BlockSpec((T, HEAD_DIM), lambda h, d: (0, base * N_HEADS + h))
    f_spec = pl.BlockSpec((T, HEAD_DIM), lambda h, d: (0, COL_FFW * N_HEADS + N_HEADS * d + h))
    l_spec = pl.BlockSpec((None, 2, HEAD_DIM), lambda h, d: (d, 0, h))
    return col, f_spec, l_spec


def hgrn_fwd(proj, lb_logits, comm=None):
    T = proj.shape[0]
    NC, CPB = T // HGRN_CHUNK, HGRN_BLOCK // HGRN_CHUNK
    col, f_spec, l_spec = _hgrn_specs(T)

    def body(l_ref, q_ref, f_ref, v_ref, o_ref, st_ref, dec_ref, qd_ref):
        d = pl.program_id(1)
        lb = _hgrn_lower_bound(l_ref)
        mask, mtri, _ = _hgrn_chunk_masks(d)

        def block(i, carry):
            rows = pl.ds(pl.multiple_of(i * HGRN_BLOCK, HGRN_BLOCK), HGRN_BLOCK)
            _, _, _, lf, k = _hgrn_gate(f_ref[rows, :], lb)
            b = _tri_sum(mtri, lf, 3)
            bl = _chunk_total(lf)
            qd = (q_ref[rows, :] * Q_SCALE * jnp.exp(b)).astype(BF16)
            kd = (k * jnp.exp(-b)).astype(BF16)
            ke = (k * jnp.exp(bl - b)).astype(BF16)
            vb = v_ref[rows, :].astype(BF16)
            att = jnp.where(mask, _nt(qd, kd), 0.0).astype(BF16)
            o_ref[rows, :] = jnp.where(d == 0, 0.0, o_ref[rows, :]) + _nn(att, vb)
            qd_ref[rows, :] = qd
            dec = jnp.exp(bl)
            for cc in range(CPB):
                sl = slice(cc * HGRN_CHUNK, (cc + 1) * HGRN_CHUNK)
                n = i * CPB + cc
                st_ref[n] = _tn(vb[sl], ke[sl])
                dec_ref[n] = dec[cc * HGRN_CHUNK:cc * HGRN_CHUNK + 8, :]
            return carry

        lax.fori_loop(0, T // HGRN_BLOCK, block, 0)

        def scan(t, s):
            n = jnp.where(d == 0, t, NC - 1 - t)
            u = st_ref[n]
            st_ref[n] = s
            return dec_ref[n][0:1, :] * s + u

        lax.fori_loop(0, NC, scan, jnp.zeros((HEAD_DIM, HEAD_DIM), F32))

        def inter(i, carry):
            rows = pl.ds(pl.multiple_of(i * HGRN_BLOCK, HGRN_BLOCK), HGRN_BLOCK)
            qd = qd_ref[rows, :]
            o_ref[rows, :] += jnp.concatenate(
                [_nt(qd[cc * HGRN_CHUNK:(cc + 1) * HGRN_CHUNK], st_ref[i * CPB + cc].astype(BF16)) for cc in range(CPB)], axis=0)
            return carry

        lax.fori_loop(0, T // HGRN_BLOCK, inter, 0)

    (o,), landed = _pallas(
        body, name="hgrn_fwd", grid=(N_HEADS, 2), in_specs=[l_spec, col(COL_Q), f_spec, col(COL_V)],
        out_specs=[pl.BlockSpec((T, HEAD_DIM), lambda h, d: (0, h))], out_shape=[jax.ShapeDtypeStruct((T, N_HEADS * HEAD_DIM), F32)],
        scratch=[pltpu.VMEM((NC, HEAD_DIM, HEAD_DIM), F32), pltpu.VMEM((NC, 8, HEAD_DIM), F32), pltpu.VMEM((T, HEAD_DIM), BF16)],
        semantics=("parallel", "arbitrary"), operands=(lb_logits, proj, proj, proj), comm=comm)
    return o if comm is None else (o, landed)


def hgrn_post_fwd(o, proj, g_norm):
    T, W = o.shape
    tm = min(256, T)

    def body(o_ref, og_ref, g_ref, y_ref):
        g = g_ref[...]
        for h in range(N_HEADS):
            sl = slice(h * HEAD_DIM, (h + 1) * HEAD_DIM)
            x = o_ref[:, sl]
            r = lax.rsqrt(jnp.mean(x * x, axis=-1, keepdims=True) + EPS)
            og = og_ref[:, sl]
            y_ref[:, sl] = ((x * r) * g * (og * jax.nn.sigmoid(og))).astype(BF16)

    return pl.pallas_call(
        body, name="hgrn_post_fwd", out_shape=jax.ShapeDtypeStruct((T, W), BF16), grid=(T // tm,),
        in_specs=[pl.BlockSpec((tm, W), lambda i: (i, 0)), pl.BlockSpec((tm, W), lambda i: (i, COL_OG)), _vec(HEAD_DIM)],
        out_specs=pl.BlockSpec((tm, W), lambda i: (i, 0)), compiler_params=_cp("parallel"),
    )(o, proj, g_norm)


def _gelu(x):
    return 0.5 * x * (1.0 + lax.erf(x * (1.0 / math.sqrt(2.0))))


def _gelu_grad(x):
    return 0.5 * (1.0 + lax.erf(x * (1.0 / math.sqrt(2.0)))) + x * jnp.exp(-0.5 * x * x) * (1.0 / math.sqrt(2.0 * math.pi))


def _sgu_mix(u_ref, v_ref, g_ref, ws_ref, bst_ref):
    W = u_ref.shape[1]
    zu, zv = _gelu(u_ref[...]), _gelu(v_ref[...])
    dv = zv - jnp.mean(zv, axis=-1, keepdims=True)
    rstd = lax.rsqrt(jnp.mean(dv * dv, axis=-1, keepdims=True) + EPS)
    dhat = dv * rstd
    vn = (dhat * g_ref[...]).astype(BF16)
    gw = W // SGU_GROUPS
    vm = [_nn(ws_ref[g].astype(BF16), vn[:, g * gw:(g + 1) * gw]) + bst_ref[:, g:g + 1] for g in range(SGU_GROUPS)]
    return zu, rstd, dhat, vn, jnp.concatenate(vm, axis=1)


def sgu_fwd(proj, g_norm, w_spatial, b_spatial_t):
    T = proj.shape[0]
    W = 1024
    n_chunks = T // SGU_CHUNK

    def body(u_ref, v_ref, g_ref, ws_ref, bst_ref, y_ref):
        zu, _, _, _, vm = _sgu_mix(u_ref, v_ref, g_ref, ws_ref, bst_ref)
        y_ref[...] = (zu * vm).astype(BF16)

    blk = lambda cb: pl.BlockSpec((SGU_CHUNK, W), lambda i: (i, cb))
    return pl.pallas_call(
        body, name="sgu_fwd", out_shape=jax.ShapeDtypeStruct((T, W), BF16), grid=(n_chunks,),
        in_specs=[blk(COL_U), blk(COL_ZV), _vec(W), pl.BlockSpec((SGU_GROUPS, SGU_CHUNK, SGU_CHUNK), lambda i: (0, 0, 0)),
                  pl.BlockSpec((SGU_CHUNK, SGU_GROUPS), lambda i: (0, 0))],
        out_specs=blk(0), compiler_params=_cp("parallel"),
    )(proj, proj, g_norm, w_spatial, b_spatial_t)


def merge_matmul(ya_pre, sgu, w_a, w_b, proj):
    T, K = ya_pre.shape
    N = w_a.shape[1]
    tm, tn = min(512, T), 512
    gpb = 1024 // tn

    def body(a_ref, b_ref, wa_ref, wb_ref, ga_ref, gb_ref, ya_ref, yb_ref, m_ref):
        ya = _nn(a_ref[...], wa_ref[...])
        yb = _nn(b_ref[...], wb_ref[...])
        ya_ref[...] = ya
        yb_ref[...] = yb
        m_ref[...] = (jax.nn.sigmoid(ga_ref[...]) * ya + jax.nn.sigmoid(gb_ref[...]) * yb).astype(BF16)

    lhs = pl.BlockSpec((tm, K), lambda i, j: (i, 0))
    rhs = pl.BlockSpec((K, tn), lambda i, j: (0, j))
    out = pl.BlockSpec((tm, tn), lambda i, j: (i, j))
    return pl.pallas_call(
        body, name="merge_matmul", grid=(T // tm, N // tn),
        out_shape=[jax.ShapeDtypeStruct((T, N), F32), jax.ShapeDtypeStruct((T, N), F32), jax.ShapeDtypeStruct((T, N), BF16)],
        in_specs=[lhs, lhs, rhs, rhs, pl.BlockSpec((tm, tn), lambda i, j: (i, COL_GA * gpb + j)),
                  pl.BlockSpec((tm, tn), lambda i, j: (i, COL_GB * gpb + j))],
        out_specs=[out, out, out], compiler_params=_cp("parallel", "parallel"),
    )(ya_pre, sgu, w_a, w_b, proj, proj)


def out_proj(merged, w_o, h0, gt1, g_post):
    T, D = h0.shape
    tm = min(256, T)

    def body(m_ref, w_ref, h_ref, gt_ref, gp_ref, mo_ref, h1_ref):
        mo = _nn(m_ref[...], w_ref[...])
        mo_ref[...] = mo
        r = lax.rsqrt(jnp.mean(mo * mo, axis=-1, keepdims=True) + EPS)
        h1_ref[...] = h_ref[...] + gt_ref[...] * ((mo * r) * gp_ref[...])

    row = pl.BlockSpec((tm, D), lambda i: (i, 0))
    return pl.pallas_call(
        body, name="out_proj", grid=(T // tm,),
        out_shape=[jax.ShapeDtypeStruct((T, D), F32), jax.ShapeDtypeStruct((T, D), F32)],
        in_specs=[row, pl.BlockSpec((D, D), lambda i: (0, 0)), row, _vec(D), _vec(D)],
        out_specs=[row, row], compiler_params=_cp("parallel"),
    )(merged, w_o, h0, gt1, g_post)


def ff2_loss(hid, w_ff2, h1, tgt, gt2, g_post):
    T, K = hid.shape
    D = w_ff2.shape[1]
    tm, tk = min(256, T), 2048
    nk = K // tk

    def body(a_ref, w_ref, h_ref, t_ref, gt_ref, g_ref, dy_ref, dff_ref, loss_ref, dgt_ref, dg_ref, acc_ref):
        i, k = pl.program_id(0), pl.program_id(1)

        @pl.when(k == 0)
        def _():
            acc_ref[...] = jnp.zeros_like(acc_ref)

        @pl.when((k == 0) & (i == 0))
        def _():
            loss_ref[...] = jnp.zeros_like(loss_ref)
            dgt_ref[...] = jnp.zeros_like(dgt_ref)
            dg_ref[...] = jnp.zeros_like(dg_ref)

        acc_ref[...] += _nn(a_ref[...], w_ref[...])

        @pl.when(k == nk - 1)
        def _():
            ff = acc_ref[...]
            gt, g = gt_ref[...], g_ref[...]
            r = lax.rsqrt(jnp.mean(ff * ff, axis=-1, keepdims=True) + EPS)
            fhat = ff * r
            nf = fhat * g
            err = (h_ref[...] + gt * nf) - t_ref[...]
            loss_ref[...] += jnp.sum(err * err)
            dy = err * (1.0 / D)
            dy_ref[...] = dy
            dgt_ref[...] += _colsum(dy * nf)
            dnf = dy * gt
            dg_ref[...] += _colsum(dnf * fhat)
            u = dnf * g
            dff_ref[...] = (r * (u - fhat * jnp.mean(u * fhat, axis=-1, keepdims=True))).astype(BF16)

    row = pl.BlockSpec((tm, D), lambda i, k: (i, 0))
    vec = pl.BlockSpec((1, D), lambda i, k: (0, 0))
    return pl.pallas_call(
        body, name="ff2_loss", grid=(T // tm, nk),
        out_shape=[jax.ShapeDtypeStruct((T, D), F32), jax.ShapeDtypeStruct((T, D), BF16), jax.ShapeDtypeStruct((8, 128), F32),
                   jax.ShapeDtypeStruct((1, D), F32), jax.ShapeDtypeStruct((1, D), F32)],
        in_specs=[pl.BlockSpec((tm, tk), lambda i, k: (i, k)), pl.BlockSpec((tk, D), lambda i, k: (k, 0)), row, row, vec, vec],
        out_specs=[row, row, pl.BlockSpec((8, 128), lambda i, k: (0, 0)), vec, vec],
        scratch_shapes=[pltpu.VMEM((tm, D), F32)], compiler_params=_cp("arbitrary", "arbitrary"),
    )(hid, w_ff2, h1, tgt, gt2, g_post)


def ff2_bwd(dff, w_ff2, f1):
    T, D = dff.shape
    K = w_ff2.shape[0]
    tm, tn = min(512, T), 1024

    def body(a_ref, w_ref, f_ref, o_ref):
        o_ref[...] = (_nt(a_ref[...], w_ref[...]) * (2.0 * jnp.maximum(f_ref[...], 0.0))).astype(BF16)

    return pl.pallas_call(
        body, name="ff2_bwd", out_shape=jax.ShapeDtypeStruct((T, K), BF16), grid=(T // tm, K // tn),
        in_specs=[pl.BlockSpec((tm, D), lambda i, j: (i, 0)), pl.BlockSpec((tn, D), lambda i, j: (j, 0)),
                  pl.BlockSpec((tm, tn), lambda i, j: (i, j))],
        out_specs=pl.BlockSpec((tm, tn), lambda i, j: (i, j)), compiler_params=_cp("parallel", "parallel"),
    )(dff, w_ff2, f1)


def ffn_norm_bwd(dy, da2, h1, mo, g_pre2, sc2, gt1, g_post):
    T, D = dy.shape
    tm = min(256, T)

    def body(dy_ref, da_ref, h_ref, mo_ref, g2_ref, sc_ref, gt_ref, gp_ref, dh_ref, dmo_ref, s_sh, s_sc, s_g2, s_gt, s_gp):
        @pl.when(pl.program_id(0) == 0)
        def _():
            for s in (s_sh, s_sc, s_g2, s_gt, s_gp):
                s[...] = jnp.zeros_like(s)

        h1, da = h_ref[...], da_ref[...]
        g2, sc = g2_ref[...], sc_ref[...]
        r2 = lax.rsqrt(jnp.mean(h1 * h1, axis=-1, keepdims=True) + EPS)
        n2 = h1 * r2
        s_sh[...] += _colsum(da)
        s_sc[...] += _colsum(da * (n2 * g2))
        s_g2[...] += _colsum(da * (1.0 + sc) * n2)
        dn2 = da * g2 * (1.0 + sc)
        dh1 = dy_ref[...] + r2 * (dn2 - n2 * jnp.mean(dn2 * n2, axis=-1, keepdims=True))
        dh_ref[...] = dh1
        mo = mo_ref[...]
        gt, gp = gt_ref[...], gp_ref[...]
        r = lax.rsqrt(jnp.mean(mo * mo, axis=-1, keepdims=True) + EPS)
        mhat = mo * r
        s_gt[...] += _colsum(dh1 * (mhat * gp))
        dnm = dh1 * gt
        s_gp[...] += _colsum(dnm * mhat)
        u = dnm * gp
        dmo_ref[...] = (r * (u - mhat * jnp.mean(u * mhat, axis=-1, keepdims=True))).astype(BF16)

    row = pl.BlockSpec((tm, D), lambda i: (i, 0))
    vec_out = jax.ShapeDtypeStruct((1, D), F32)
    return pl.pallas_call(
        body, name="ffn_norm_bwd", grid=(T // tm,),
        out_shape=[jax.ShapeDtypeStruct((T, D), F32), jax.ShapeDtypeStruct((T, D), BF16)] + [vec_out] * 5,
        in_specs=[row, row, row, row] + [_vec(D)] * 4, out_specs=[row, row] + [_vec(D)] * 5,
        compiler_params=_cp("arbitrary"),
    )(dy, da2, h1, mo, g_pre2, sc2, gt1, g_post)


def out_proj_bwd(dmo, w_o, y_a, y_b, proj):
    T, D = dmo.shape
    tm, tn = min(512, T), 512
    gpb = 1024 // tn

    def body(a_ref, w_ref, ya_ref, yb_ref, ga_ref, gb_ref, dya_ref, dyb_ref, dga_ref, dgb_ref):
        dm = _nt(a_ref[...], w_ref[...])
        sa, sb = jax.nn.sigmoid(ga_ref[...]), jax.nn.sigmoid(gb_ref[...])
        dya_ref[...] = (dm * sa).astype(BF16)
        dyb_ref[...] = (dm * sb).astype(BF16)
        dga_ref[...] = (dm * ya_ref[...] * sa * (1.0 - sa)).astype(BF16)
        dgb_ref[...] = (dm * yb_ref[...] * sb * (1.0 - sb)).astype(BF16)

    out = pl.BlockSpec((tm, tn), lambda i, j: (i, j))
    return pl.pallas_call(
        body, name="out_proj_bwd", grid=(T // tm, D // tn), out_shape=[jax.ShapeDtypeStruct((T, D), BF16)] * 4,
        in_specs=[pl.BlockSpec((tm, D), lambda i, j: (i, 0)), pl.BlockSpec((tn, D), lambda i, j: (j, 0)), out, out,
                  pl.BlockSpec((tm, tn), lambda i, j: (i, COL_GA * gpb + j)), pl.BlockSpec((tm, tn), lambda i, j: (i, COL_GB * gpb + j))],
        out_specs=[out] * 4, compiler_params=_cp("parallel", "parallel"),
    )(dmo, w_o, y_a, y_b, proj, proj)


def sgu_bwd(proj, dsgu, g_norm, w_spatial, b_spatial_t):
    T = proj.shape[0]
    W = 1024
    gw = W // SGU_GROUPS

    def body(u_ref, v_ref, ds_ref, g_ref, ws_ref, bst_ref, dz_ref, dw_ref, db_ref, dg_ref):
        @pl.when(pl.program_id(0) == 0)
        def _():
            dw_ref[...] = jnp.zeros_like(dw_ref)
            db_ref[...] = jnp.zeros_like(db_ref)
            dg_ref[...] = jnp.zeros_like(dg_ref)

        zu, rstd, dhat, vn, vm = _sgu_mix(u_ref, v_ref, g_ref, ws_ref, bst_ref)
        ds = ds_ref[...]
        du = ds * vm
        dvm = ds * zu
        dvm_b = dvm.astype(BF16)
        ones = jnp.ones((8, gw), F32)
        dvn = []
        for g in range(SGU_GROUPS):
            sl = slice(g * gw, (g + 1) * gw)
            dw_ref[g] += _nt(dvm_b[:, sl], vn[:, sl])
            db_ref[g] += lax.dot_general(ones, dvm[:, sl], (((1,), (1,)), ((), ())), precision=HI, preferred_element_type=F32)
            dvn.append(_tn(ws_ref[g].astype(BF16), dvm_b[:, sl]))
        dvn = jnp.concatenate(dvn, axis=1)
        dg_ref[...] += _colsum(dvn * dhat)
        ddh = dvn * g_ref[...]
        dzv = rstd * (ddh - jnp.mean(ddh, axis=-1, keepdims=True) - dhat * jnp.mean(ddh * dhat, axis=-1, keepdims=True))
        dz_ref[:, 0:W] = (du * _gelu_grad(u_ref[...])).astype(BF16)
        dz_ref[:, W:2 * W] = (dzv * _gelu_grad(v_ref[...])).astype(BF16)

    blk = lambda cb: pl.BlockSpec((SGU_CHUNK, W), lambda i: (i, cb))
    full3 = lambda a, b, c: pl.BlockSpec((a, b, c), lambda i: (0, 0, 0))
    return pl.pallas_call(
        body, name="sgu_bwd", grid=(T // SGU_CHUNK,),
        out_shape=[jax.ShapeDtypeStruct((T, 2 * W), BF16), jax.ShapeDtypeStruct((SGU_GROUPS, SGU_CHUNK, SGU_CHUNK), F32),
                   jax.ShapeDtypeStruct((SGU_GROUPS, 8, SGU_CHUNK), F32), jax.ShapeDtypeStruct((1, W), F32)],
        in_specs=[blk(COL_U), blk(COL_ZV), blk(0), _vec(W), full3(SGU_GROUPS, SGU_CHUNK, SGU_CHUNK),
                  pl.BlockSpec((SGU_CHUNK, SGU_GROUPS), lambda i: (0, 0))],
        out_specs=[pl.BlockSpec((SGU_CHUNK, 2 * W), lambda i: (i, 0)), full3(SGU_GROUPS, SGU_CHUNK, SGU_CHUNK),
                   full3(SGU_GROUPS, 8, SGU_CHUNK), _vec(W)],
        compiler_params=_cp("arbitrary"),
    )(proj, proj, dsgu, g_norm, w_spatial, b_spatial_t)


def hgrn_post_bwd(dya, o, proj, g_norm):
    T, W = o.shape
    tm = min(256, T)

    def body(dy_ref, o_ref, og_ref, g_ref, do_ref, dog_ref, dg_ref):
        @pl.when(pl.program_id(0) == 0)
        def _():
            dg_ref[...] = jnp.zeros_like(dg_ref)

        g = g_ref[...]
        dg = jnp.zeros((1, HEAD_DIM), F32)
        for h in range(N_HEADS):
            sl = slice(h * HEAD_DIM, (h + 1) * HEAD_DIM)
            x, og, dy = o_ref[:, sl], og_ref[:, sl], dy_ref[:, sl]
            r = lax.rsqrt(jnp.mean(x * x, axis=-1, keepdims=True) + EPS)
            xhat = x * r
            s = jax.nn.sigmoid(og)
            don = dy * (og * s)
            dog_ref[:, sl] = (dy * (xhat * g) * (s * (1.0 + og * (1.0 - s)))).astype(BF16)
            dg += _colsum(don * xhat)
            u = don * g
            do_ref[:, sl] = r * (u - xhat * jnp.mean(u * xhat, axis=-1, keepdims=True))
        dg_ref[...] += dg

    row = pl.BlockSpec((tm, W), lambda i: (i, 0))
    return pl.pallas_call(
        body, name="hgrn_post_bwd", grid=(T // tm,),
        out_shape=[jax.ShapeDtypeStruct((T, W), F32), jax.ShapeDtypeStruct((T, W), BF16), jax.ShapeDtypeStruct((1, HEAD_DIM), F32)],
        in_specs=[row, row, pl.BlockSpec((tm, W), lambda i: (i, COL_OG)), _vec(HEAD_DIM)],
        out_specs=[row, row, _vec(HEAD_DIM)], compiler_params=_cp("arbitrary"),
    )(dya, o, proj, g_norm)


def hgrn_bwd(proj, do, lb_logits, comm=None):
    T = proj.shape[0]
    NC, CPB = T // HGRN_CHUNK, HGRN_BLOCK // HGRN_CHUNK
    W = N_HEADS * HEAD_DIM
    col, f_spec, l_spec = _hgrn_specs(T)

    def body(l_ref, q_ref, f_ref, v_ref, do_ref, dq_ref, dv_ref, dlg_ref, dlb_ref, st_ref, dst_ref, dec_ref, ddec_ref, dqa_ref, dva_ref):
        d = pl.program_id(1)
        lb = _hgrn_lower_bound(l_ref)
        oml = 1.0 - lb
        mask, mtri, mtri_t = _hgrn_chunk_masks(d)

        def values(rows):
            s, sn, fg, lf, k = _hgrn_gate(f_ref[rows, :], lb)
            b = _tri_sum(mtri, lf, 3)
            bl = _chunk_total(lf)
            eb, enb, ee = jnp.exp(b), jnp.exp(-b), jnp.exp(bl - b)
            qd = q_ref[rows, :] * Q_SCALE * eb
            return s, sn, fg, k, bl, eb, enb, ee, qd, k * enb, k * ee

        def block1(i, carry):
            rows = pl.ds(pl.multiple_of(i * HGRN_BLOCK, HGRN_BLOCK), HGRN_BLOCK)
            _, _, _, _, bl, _, _, _, qd, _, ke = values(rows)
            qd, ke = qd.astype(BF16), ke.astype(BF16)
            vb, dob = v_ref[rows, :].astype(BF16), do_ref[rows, :].astype(BF16)
            dec = jnp.exp(bl)
            for cc in range(CPB):
                sl = slice(cc * HGRN_CHUNK, (cc + 1) * HGRN_CHUNK)
                n = i * CPB + cc
                st_ref[n] = _tn(vb[sl], ke[sl])
                dst_ref[n] = _tn(dob[sl], qd[sl])
                dec_ref[n] = dec[cc * HGRN_CHUNK:cc * HGRN_CHUNK + 8, :]
            return carry

        lax.fori_loop(0, T // HGRN_BLOCK, block1, 0)

        def scan(t, s):
            n = jnp.where(d == 0, t, NC - 1 - t)
            u = st_ref[n]
            st_ref[n] = s
            return dec_ref[n][0:1, :] * s + u

        lax.fori_loop(0, NC, scan, jnp.zeros((HEAD_DIM, HEAD_DIM), F32))

        def rscan(t, ds):
            n = jnp.where(d == 0, NC - 1 - t, t)
            w = dst_ref[n]
            dst_ref[n] = ds
            ddec_ref[n] = jnp.broadcast_to(_colsum(ds * st_ref[n]), (8, HEAD_DIM))
            return dec_ref[n][0:1, :] * ds + w

        lax.fori_loop(0, NC, rscan, jnp.zeros((HEAD_DIM, HEAD_DIM), F32))

        def block3(i, dlb):
            rows = pl.ds(pl.multiple_of(i * HGRN_BLOCK, HGRN_BLOCK), HGRN_BLOCK)
            s, sn, fg, k, bl, eb, enb, ee, qd, kd, ke = values(rows)
            qdb, kdb, keb = qd.astype(BF16), kd.astype(BF16), ke.astype(BF16)
            vb, dob = v_ref[rows, :].astype(BF16), do_ref[rows, :].astype(BF16)
            att = jnp.where(mask, _nt(qdb, kdb), 0.0).astype(BF16)
            datt = jnp.where(mask, _nt(dob, vb), 0.0).astype(BF16)
            dv = _tn(att, dob)
            dqd = _nn(datt, kdb)
            dkd = _tn(datt, qdb)
            dv_i, dqd_i, dke, ddl = [], [], [], []
            for cc in range(CPB):
                sl = slice(cc * HGRN_CHUNK, (cc + 1) * HGRN_CHUNK)
                n = i * CPB + cc
                st_b, dst_b = st_ref[n].astype(BF16), dst_ref[n].astype(BF16)
                dv_i.append(_nt(keb[sl], dst_b))
                dqd_i.append(_nn(dob[sl], st_b))
                dke.append(_nn(vb[sl], dst_b))
                ddl.append(jnp.broadcast_to(ddec_ref[n][0:1, :] * dec_ref[n][0:1, :], (HGRN_CHUNK, HEAD_DIM)))
            dv = dv + jnp.concatenate(dv_i, axis=0)
            dqd = dqd + jnp.concatenate(dqd_i, axis=0)
            dke = jnp.concatenate(dke, axis=0)
            dq = dqd * eb * Q_SCALE
            dk = dkd * enb + dke * ee
            t_end = dke * ke
            db = dqd * qd - dkd * kd - t_end
            dlf = _tri_sum(mtri_t, db, 2) + _chunk_total(t_end) + jnp.concatenate(ddl, axis=0)
            e = dlf / fg - dk
            dlg_ref[rows, :] = (oml * e * s * sn).astype(BF16)

            @pl.when(d == 0)
            def _():
                dqa_ref[rows, :] = dq
                dva_ref[rows, :] = dv

            @pl.when(d == 1)
            def _():
                dq_ref[rows, :] = (dqa_ref[rows, :] + dq).astype(BF16)
                dv_ref[rows, :] = (dva_ref[rows, :] + dv).astype(BF16)

            return dlb + _colsum(e * sn)

        dlb_ref[...] = lax.fori_loop(0, T // HGRN_BLOCK, block3, jnp.zeros((1, HEAD_DIM), F32))

    head = pl.BlockSpec((T, HEAD_DIM), lambda h, d: (0, h))
    big = pltpu.VMEM((NC, HEAD_DIM, HEAD_DIM), F32)
    small = pltpu.VMEM((NC, 8, HEAD_DIM), F32)
    acc = pltpu.VMEM((T, HEAD_DIM), F32)
    outs, landed = _pallas(
        body, name="hgrn_bwd", grid=(N_HEADS, 2),
        out_shape=[jax.ShapeDtypeStruct((T, W), BF16), jax.ShapeDtypeStruct((T, W), BF16), jax.ShapeDtypeStruct((T, 2 * W), BF16),
                   jax.ShapeDtypeStruct((2, 1, W), F32)],
        in_specs=[l_spec, col(COL_Q), f_spec, col(COL_V), head],
        out_specs=[head, head, pl.BlockSpec((T, HEAD_DIM), lambda h, d: (0, N_HEADS * d + h)),
                   pl.BlockSpec((None, 1, HEAD_DIM), lambda h, d: (d, 0, h))],
        scratch=[big, big, small, small, acc, acc], semantics=("parallel", "arbitrary"), operands=(lb_logits, proj, proj, proj, do), comm=comm)
    return outs if comm is None else (outs, landed)


def mix_norm_bwd(da1, h0, dh1, g_pre, sc1):
    T, D = h0.shape
    tm = min(256, T)

    def body(da_ref, h_ref, dh_ref, g_ref, sc_ref, gx_ref, s_sh, s_sc, s_g):
        @pl.when(pl.program_id(0) == 0)
        def _():
            for s in (s_sh, s_sc, s_g):
                s[...] = jnp.zeros_like(s)

        h, da = h_ref[...], da_ref[...]
        g, sc = g_ref[...], sc_ref[...]
        r = lax.rsqrt(jnp.mean(h * h, axis=-1, keepdims=True) + EPS)
        n = h * r
        s_sh[...] += _colsum(da)
        s_sc[...] += _colsum(da * (n * g))
        s_g[...] += _colsum(da * (1.0 + sc) * n)
        dn = da * g * (1.0 + sc)
        gx_ref[...] = dh_ref[...] + r * (dn - n * jnp.mean(dn * n, axis=-1, keepdims=True))

    row = pl.BlockSpec((tm, D), lambda i: (i, 0))
    return pl.pallas_call(
        body, name="mix_norm_bwd", grid=(T // tm,),
        out_shape=[jax.ShapeDtypeStruct((T, D), F32)] + [jax.ShapeDtypeStruct((1, D), F32)] * 3,
        in_specs=[row, row, row, _vec(D), _vec(D)], out_specs=[row] + [_vec(D)] * 3, compiler_params=_cp("arbitrary"),
    )(da1, h0, dh1, g_pre, sc1)


def adamw(w, g, m, v, name):
    R, C = w.shape
    tr = R if R * C * 4 <= (1 << 21) else max(8, ((1 << 21) // (C * 4)) // 8 * 8)
    while R % tr:
        tr -= 8

    def body(w_ref, g_ref, m_ref, v_ref, d_ref, m2_ref, v2_ref):
        d_ref[...], m2_ref[...], v2_ref[...] = _adamw(w_ref[...], g_ref[...], m_ref[...], v_ref[...])

    row = pl.BlockSpec((tr, C), lambda i: (i, 0))
    return pl.pallas_call(
        body, name=name, grid=(R // tr,), out_shape=[jax.ShapeDtypeStruct((R, C), F32)] * 3,
        in_specs=[row] * 4, out_specs=[row] * 3, compiler_params=_cp("parallel"),
    )(w, g, m, v)


def wada_update(c_all, dmod, w, m, v):
    D, N = w.shape
    tm, tn = 512, 1024

    def body(c_ref, dm_ref, w_ref, m_ref, v_ref, g_ref, d_ref, m2_ref, v2_ref):
        c = c_ref[...]
        g = lax.dot_general(c * jax.nn.sigmoid(c), dm_ref[...], (((0,), (0,)), ((), ())), precision=HI, preferred_element_type=F32)
        g_ref[...] = g
        d_ref[...], m2_ref[...], v2_ref[...] = _adamw(w_ref[...], g, m_ref[...], v_ref[...])

    blk = pl.BlockSpec((tm, tn), lambda i, j: (i, j))
    return pl.pallas_call(
        body, name="wada_update", grid=(D // tm, N // tn), out_shape=[jax.ShapeDtypeStruct((D, N), F32)] * 4,
        in_specs=[pl.BlockSpec((8, tm), lambda i, j: (0, i)), pl.BlockSpec((8, tn), lambda i, j: (0, j)), blk, blk, blk],
        out_specs=[blk] * 4, compiler_params=_cp("parallel", "parallel"),
    )(c_all, dmod, w, m, v)


def sum_devices(gathered):
    n, R, C = gathered.shape

    def body(g_ref, o_ref):
        s = g_ref[0]
        for i in range(1, n):
            s = s + g_ref[i]
        o_ref[...] = s

    return pl.pallas_call(body, name="sum_devices", out_shape=jax.ShapeDtypeStruct((R, C), F32), compiler_params=_cp())(gathered)


def lb_logits_grad(dlb, lb_logits):
    def body(d_ref, l_ref, o_ref):
        for d in range(2):
            l0, l1 = l_ref[d, 0:1, :], l_ref[d, 1:2, :]
            m = jnp.maximum(l0, l1)
            e0, e1 = jnp.exp(l0 - m), jnp.exp(l1 - m)
            p0, p1 = e0 / (e0 + e1), e1 / (e0 + e1)
            g = d_ref[d:d + 1, :]
            o_ref[d, 0:1, :] = p0 * (g - p0 * g)
            o_ref[d, 1:2, :] = -p1 * (p0 * g)

    return pl.pallas_call(body, name="lb_logits_grad", out_shape=jax.ShapeDtypeStruct(lb_logits.shape, F32), compiler_params=_cp())(dlb, lb_logits)


def add_halves(g, landed, core):
    nj, _, r, cc = g.shape
    tr = min(256, r)

    def body(core_ref, g_ref, l_ref, o_ref):
        o_ref[...] = (g_ref[...].astype(F32) + l_ref[...].astype(F32)).astype(BF16)

    return pl.pallas_call(
        body, name="add_halves_%dx%d" % (r, cc), out_shape=jax.ShapeDtypeStruct((nj, r, cc), BF16),
        grid_spec=pltpu.PrefetchScalarGridSpec(
            num_scalar_prefetch=1, grid=(nj, r // tr),
            in_specs=[pl.BlockSpec((None, None, tr, cc), lambda j, i, core_ref: (j, core_ref[0], i, 0)),
                      pl.BlockSpec((None, None, tr, cc), lambda j, i, core_ref: (j, 0, i, 0))],
            out_specs=pl.BlockSpec((None, tr, cc), lambda j, i, core_ref: (j, i, 0))),
        compiler_params=_cp("parallel", "parallel"),
    )(core, g, landed)


def sum_chips(parts, landed, chip):
    nj, r, cc = parts.shape
    tr = min(256, r)

    def body(chip_ref, p_ref, l_ref, o_ref):
        mine = p_ref[...].astype(F32)
        s = None
        for j in range(nj):
            t = jnp.where(chip_ref[0] == j, mine, l_ref[j].astype(F32))
            s = t if s is None else s + t
        o_ref[...] = s

    return pl.pallas_call(
        body, name="sum_chips_%dx%d" % (r, cc), out_shape=jax.ShapeDtypeStruct((r, cc), F32),
        grid_spec=pltpu.PrefetchScalarGridSpec(
            num_scalar_prefetch=1, grid=(r // tr,),
            in_specs=[pl.BlockSpec((None, tr, cc), lambda i, chip_ref: (chip_ref[0], i, 0)),
                      pl.BlockSpec((nj, tr, cc), lambda i, chip_ref: (0, i, 0))],
            out_specs=pl.BlockSpec((tr, cc), lambda i, chip_ref: (i, 0))),
        compiler_params=_cp("parallel"),
    )(chip, parts, landed)


def adamw_halves(w, own, other, m, v, core, name):
    r, cc = own.shape
    tr = min(128, r)
    nb = r // tr

    def body(core_ref, w_ref, a_ref, b_ref, m_ref, v_ref, g_ref, d_ref, m2_ref, v2_ref):
        g = jnp.where(pl.program_id(0) == core_ref[0], a_ref[...], b_ref[...])
        g_ref[...] = g
        d_ref[...], m2_ref[...], v2_ref[...] = _adamw(w_ref[...], g, m_ref[...], v_ref[...])

    full = pl.BlockSpec((tr, cc), lambda h, i, core_ref: (h * nb + i, 0))
    half = pl.BlockSpec((tr, cc), lambda h, i, core_ref: (i, 0))
    return pl.pallas_call(
        body, name=name, out_shape=[jax.ShapeDtypeStruct((2 * r, cc), F32)] * 4,
        grid_spec=pltpu.PrefetchScalarGridSpec(
            num_scalar_prefetch=1, grid=(2, nb), in_specs=[full, half, half, full, full], out_specs=[full] * 4),
        compiler_params=_cp("parallel", "parallel"),
    )(core, w, own, other, m, v)


def _place():
    mx, my, mc = lax.axis_index("x"), lax.axis_index("y"), lax.axis_index("c")
    chips = [(1 - mx, my), (mx, 1 - my), (1 - mx, 1 - my)]
    return mx, my, mc, chips


def all_gather_small(x, name):
    R, C = x.shape

    def body(x_ref, out_ref, send_sems, recv_sems, local_sem):
        mx, my, mc, _ = _place()
        me = 4 * mx + 2 * my + mc
        mine = pltpu.make_async_copy(x_ref, out_ref.at[me], local_sem)
        mine.start()

        def peer(k):
            px = 1 - mx if k & 4 else mx
            py = 1 - my if k & 2 else my
            pc = 1 - mc if k & 1 else mc
            return px, py, pc

        def copy(k, src, slot):
            return pltpu.make_async_remote_copy(src_ref=src, dst_ref=out_ref.at[slot], send_sem=send_sems.at[k - 1],
                                                recv_sem=recv_sems.at[k - 1], device_id=peer(k), device_id_type=MESH)

        sends = [copy(k, x_ref, me) for k in range(1, 8)]
        for cp in sends:
            cp.start()
        for k in range(1, 8):
            px, py, pc = peer(k)
            slot = 4 * px + 2 * py + pc
            copy(k, out_ref.at[slot], slot).wait_recv()
        for cp in sends:
            cp.wait_send()
        mine.wait()

    return pl.pallas_call(
        body, name=name, out_shape=jax.ShapeDtypeStruct((8, R, C), F32),
        in_specs=[pl.BlockSpec(memory_space=pltpu.VMEM)], out_specs=pl.BlockSpec(memory_space=pltpu.VMEM),
        scratch_shapes=[pltpu.SemaphoreType.DMA((7,)), pltpu.SemaphoreType.DMA((7,)), pltpu.SemaphoreType.DMA],
        compiler_params=_cp(),
    )(x)


def _region(ref, kind, j, half, r, cc):
    nr = r if half is None else r // 2
    off = 0 if half is None else half * nr
    if kind == "col":
        return ref.at[pl.ds(off, nr), pl.ds(pl.multiple_of(j * cc, 128), cc)]
    return ref.at[pl.ds(pl.multiple_of(j * r + off, 16), nr), :]


def gather_weights(fulls, kinds, dims):
    comm = gather_comm(fulls, kinds, dims)
    n = len(fulls)

    def body(*refs):
        comm.start(refs[:n], refs[n:2 * n], *refs[2 * n:])
        comm.finish(refs[:n], refs[n:2 * n], *refs[2 * n:])

    return pl.pallas_call(
        body, name="gather_weights", out_shape=comm.out_shape, in_specs=[ANY] * n, out_specs=[ANY] * n, input_output_aliases=comm.aliases,
        scratch_shapes=[pltpu.SemaphoreType.DMA((comm.n_sems,)), pltpu.SemaphoreType.DMA((comm.n_sems,))], compiler_params=_cp(),
    )(*fulls)


def gather_comm(fulls, kinds, dims):
    n = len(fulls)

    def copies(f_refs, send_sems, recv_sems):
        mx, my, mc, chips = _place()
        jme = 2 * mx + my

        def landed(w, k, half):
            px, py = chips[k]
            return _region(f_refs[w], kinds[w], 2 * px + py, half, *dims[w])

        def over_ici(w, k, reg):
            px, py = chips[k]
            return pltpu.make_async_remote_copy(src_ref=reg, dst_ref=reg, send_sem=send_sems.at[6 * w + k], recv_sem=recv_sems.at[6 * w + k],
                                                device_id=(px, py, mc), device_id_type=MESH)

        def over_d2d(w, k, half):
            reg = landed(w, k, half)
            return pltpu.make_async_remote_copy(src_ref=reg, dst_ref=reg, send_sem=send_sems.at[6 * w + 3 + k],
                                                recv_sem=recv_sems.at[6 * w + 3 + k], device_id=(mx, my, 1 - mc), device_id_type=MESH)

        sends = [over_ici(w, k, _region(f_refs[w], kinds[w], jme, mc, *dims[w])) for w in range(n) for k in range(3)]
        return mc, landed, over_ici, over_d2d, sends

    def start(cin, f_refs, send_sems, recv_sems):
        for cp in copies(f_refs, send_sems, recv_sems)[4]:
            cp.start()

    def finish(cin, f_refs, send_sems, recv_sems):
        mc, landed, over_ici, over_d2d, sends = copies(f_refs, send_sems, recv_sems)
        passed = []
        for w in range(n):
            for k in range(3):
                over_ici(w, k, landed(w, k, mc)).wait_recv()
                cp = over_d2d(w, k, mc)
                cp.start()
                passed.append(cp)
        for w in range(n):
            for k in range(3):
                over_d2d(w, k, 1 - mc).wait_recv()
        for cp in sends + passed:
            cp.wait_send()

    return _Comm(fulls, [jax.ShapeDtypeStruct(f.shape, BF16) for f in fulls], {w: w for w in range(n)}, 6 * n, start, finish)


def exchange_halves(grads, name):
    n = len(grads)

    def body(*refs):
        g_refs, l_refs = refs[:n], refs[n:2 * n]
        send_sems, recv_sems = refs[2 * n:]
        mx, my, mc, _ = _place()
        cps = [pltpu.make_async_remote_copy(src_ref=g_refs[w].at[:, pl.ds(1 - mc, 1)], dst_ref=l_refs[w], send_sem=send_sems.at[w],
                                            recv_sem=recv_sems.at[w], device_id=(mx, my, 1 - mc), device_id_type=MESH) for w in range(n)]
        for cp in cps:
            cp.start()
        for cp in cps:
            cp.wait()

    return pl.pallas_call(
        body, name=name, out_shape=[jax.ShapeDtypeStruct((g.shape[0], 1) + g.shape[2:], BF16) for g in grads],
        in_specs=[ANY] * n, out_specs=[ANY] * n,
        scratch_shapes=[pltpu.SemaphoreType.DMA((n,)), pltpu.SemaphoreType.DMA((n,))], compiler_params=_cp(),
    )(*grads)


def scatter_comm(parts):
    n = len(parts)

    def sends(p_refs, l_refs, send_sems, recv_sems):
        mx, my, mc, chips = _place()
        return [pltpu.make_async_remote_copy(src_ref=p_refs[w].at[2 * px + py], dst_ref=l_refs[w].at[2 * mx + my],
                                             send_sem=send_sems.at[3 * w + k], recv_sem=recv_sems.at[3 * w + k],
                                             device_id=(px, py, mc), device_id_type=MESH) for w in range(n) for k, (px, py) in enumerate(chips)]

    def start(p_refs, l_refs, send_sems, recv_sems):
        for cp in sends(p_refs, l_refs, send_sems, recv_sems):
            cp.start()

    def finish(p_refs, l_refs, send_sems, recv_sems):
        mx, my, mc, chips = _place()
        for w in range(n):
            for k, (px, py) in enumerate(chips):
                slot = l_refs[w].at[2 * px + py]
                pltpu.make_async_remote_copy(src_ref=slot, dst_ref=slot, send_sem=send_sems.at[3 * w + k], recv_sem=recv_sems.at[3 * w + k],
                                             device_id=(px, py, mc), device_id_type=MESH).wait_recv()
        for cp in sends(p_refs, l_refs, send_sems, recv_sems):
            cp.wait_send()

    return _Comm(parts, [jax.ShapeDtypeStruct(p.shape, BF16) for p in parts], {}, 3 * n, start, finish)


def share_with_sibling(sums):
    n = len(sums)

    def body(*refs):
        q_refs, o_refs = refs[:n], refs[n:2 * n]
        send_sems, recv_sems = refs[2 * n:]
        mx, my, mc, _ = _place()
        cps = [pltpu.make_async_remote_copy(src_ref=q_refs[w], dst_ref=o_refs[w], send_sem=send_sems.at[w], recv_sem=recv_sems.at[w],
                                            device_id=(mx, my, 1 - mc), device_id_type=MESH) for w in range(n)]
        for cp in cps:
            cp.start()
        for cp in cps:
            cp.wait()

    return pl.pallas_call(
        body, name="share_with_sibling", out_shape=[jax.ShapeDtypeStruct(q.shape, F32) for q in sums],
        in_specs=[ANY] * n, out_specs=[ANY] * n,
        scratch_shapes=[pltpu.SemaphoreType.DMA((n,)), pltpu.SemaphoreType.DMA((n,))],
        compiler_params=_cp(),
    )(*sums)


def _pack(arrays):
    flat = jnp.concatenate([a.reshape(-1) for a in arrays])
    rows = -(-flat.shape[0] // 1024) * 8
    return jnp.pad(flat, (0, rows * 128 - flat.shape[0])).reshape(rows, 128)


def _unpack(packed, shapes):
    flat, out, off = packed.reshape(-1), [], 0
    for s in shapes:
        n = math.prod(s)
        out.append(flat[off:off + n].reshape(s))
        off += n
    return out


def kernel(x, c, w_ada, b_ada, g_pre_mix, g_post_mix, g_pre_ffn, g_post_ffn, w_in, lb_logits, g_hgrn_norm, w_a_out, g_sgu_norm, w_spatial, b_spatial, w_b_out, w_o, w_ff1, w_ff2, loss_target, m_w_ada, m_b_ada, m_g_pre_mix, m_g_post_mix, m_g_pre_ffn, m_g_post_ffn, m_w_in, m_lb_logits, m_g_hgrn_norm, m_w_a_out, m_g_sgu_norm, m_w_spatial, m_b_spatial, m_w_b_out, m_w_o, m_w_ff1, m_w_ff2, v_w_ada, v_b_ada, v_g_pre_mix, v_g_post_mix, v_g_pre_ffn, v_g_post_ffn, v_w_in, v_lb_logits, v_g_hgrn_norm, v_w_a_out, v_g_sgu_norm, v_w_spatial, v_b_spatial, v_w_b_out, v_w_o, v_w_ff1, v_w_ff2):
    mx, my, mc = lax.axis_index("x"), lax.axis_index("y"), lax.axis_index("c")
    chip, me = 2 * mx + my, 4 * mx + 2 * my + mc
    D = D_MODEL
    h0, tgt = x[0], loss_target[0]
    n_ada = w_ada.shape[2]
    n_lb = lb_logits.shape[2]

    got = all_gather_small(_pack([c, lb_logits]), "gather_inputs")
    c_all = got[:, :D // 128, :].reshape(8, D)
    lb_full = got[0::2, D // 128:D // 128 + 4 * n_lb // 128, :].reshape(4, 2, 2, n_lb).transpose(1, 2, 0, 3).reshape(2, 2, 4 * n_lb)
    b_ada_chip = lax.dynamic_slice(b_ada, (0, chip * n_ada), (1, n_ada))
    mod_cols = mod_matmul(c_all, w_ada[0], b_ada_chip)
    got = all_gather_small(mod_cols.reshape(-1, 128), "gather_mod").reshape(4, 2, 8, n_ada)
    mod = lax.dynamic_index_in_dim(got[:, 0], me, axis=1, keepdims=False).reshape(6, 1, D)
    sh1, sc1, gt1, sh2, sc2, gt2 = (mod[i] for i in range(6))

    big = [("w_in", w_in, "col"), ("w_a_out", w_a_out, "col"), ("w_b_out", w_b_out, "col"), ("w_o", w_o, "row"),
           ("w_ff1", w_ff1, "col"), ("w_ff2", w_ff2, "row")]
    kinds = [k for _, _, k in big]
    chip_idx, core = chip.reshape(1).astype(jnp.int32), mc.reshape(1).astype(jnp.int32)
    fulls = [cast_into_full(w[0], kind, chip_idx, "cast_" + nm) for nm, w, kind in big]
    dims = [w.shape[1:] for _, w, _ in big]
    later = lambda lo, hi: gather_comm(fulls[lo:hi], kinds[lo:hi], dims[lo:hi])
    halves_summed = lambda grads, name: [add_halves(g, l, core) for g, l in zip(grads, exchange_halves(grads, name))]
    (w_in_f,) = gather_weights(fulls[:1], kinds[:1], dims[:1])

    bst = b_spatial[0].T
    (proj, a1), (w_a_f, w_b_f, w_o_f) = prenorm_matmul(h0, g_pre_mix, sc1, sh1, w_in_f, relu2=False, name="in_proj", comm=later(1, 4))
    o, (w_ff1_f,) = hgrn_fwd(proj, lb_full, comm=later(4, 5))
    ya_pre = hgrn_post_fwd(o, proj, g_hgrn_norm)
    sgu = sgu_fwd(proj, g_sgu_norm, w_spatial[0], bst)
    y_a, y_b, merged = merge_matmul(ya_pre, sgu, w_a_f, w_b_f, proj)
    mo, h1 = out_proj(merged, w_o_f, h0, gt1, g_post_mix)
    (f1, a2, hid), (w_ff2_f,) = prenorm_matmul(h1, g_pre_ffn, sc2, sh2, w_ff1_f, relu2=True, name="ff1", comm=later(5, 6))
    dy, dff, loss_parts, d_gt2, d_g_post_ffn = ff2_loss(hid, w_ff2_f, h1, tgt, gt2, g_post_ffn)
    loss = lax.psum(0.5 * loss_parts[0, 0] / D, ("x", "y", "c"))

    df1 = ff2_bwd(dff, w_ff2_f, f1)
    gr_ff2 = matmul(hid, dff, mode="tn", out_dtype=BF16, tm=1024, tn=1024, tk=512, name="dw_ff2")
    da2 = matmul(df1, w_ff1_f, mode="nt", out_dtype=F32, tm=512, tn=1024, tk=2048, name="da2")
    gr_ff1 = matmul(a2, df1, mode="tn", out_dtype=BF16, tm=1024, tn=2048, tk=512, name="dw_ff1", split=(4, 2))
    parts_ff = halves_summed([gr_ff1, gr_ff2.reshape(4, 2, -1, D)], "exchange_ff")
    dh1, dmo, d_sh2, d_sc2, d_g_pre_ffn, d_gt1, d_g_post_mix = ffn_norm_bwd(dy, da2, h1, mo, g_pre_ffn, sc2, gt1, g_post_mix)
    dya, dyb, dga, dgb = out_proj_bwd(dmo, w_o_f, y_a, y_b, proj)
    gr_o = matmul(merged, dmo, mode="tn", out_dtype=BF16, tm=1024, tn=1024, tk=512, name="dw_o")
    dsgu = matmul(dyb, w_b_f, mode="nt", out_dtype=F32, tm=512, tn=1024, tk=2048, name="dsgu")
    gr_b = matmul(sgu, dyb, mode="tn", out_dtype=BF16, tm=512, tn=512, tk=512, name="dw_b_out", split=(4, 2))
    dz, d_w_spatial, d_b_spatial, d_g_sgu = sgu_bwd(proj, dsgu, g_sgu_norm, w_spatial[0], bst)
    dya_pre = matmul(dya, w_a_f, mode="nt", out_dtype=F32, tm=512, tn=1024, tk=2048, name="dya_pre")
    gr_a = matmul(ya_pre, dya, mode="tn", out_dtype=BF16, tm=512, tn=512, tk=512, name="dw_a_out", split=(4, 2))
    parts_mix = halves_summed([gr_a, gr_b, gr_o.reshape(4, 2, -1, D)], "exchange_mix")
    do, dog, d_g_hgrn = hgrn_post_bwd(dya_pre, o, proj, g_hgrn_norm)
    (dq, dv, dlg, d_lb), landed_ff = hgrn_bwd(proj, do, lb_full, comm=scatter_comm(parts_ff))
    dproj = jnp.concatenate([dq, dlg, dv, dog, dz, dga, dgb], axis=1)
    gr_in, landed_mix = matmul(a1, dproj, mode="tn", out_dtype=BF16, tm=1024, tn=2816, tk=512, name="dw_in", split=(4, 2),
                               comm=scatter_comm(parts_mix))
    parts_in = halves_summed([gr_in], "exchange_in")
    da1, landed_in = matmul(dproj, w_in_f, mode="nt", out_dtype=F32, tm=512, tn=1024, tk=2816, name="da1", comm=scatter_comm(parts_in))
    grad_x, d_sh1, d_sc1, d_g_pre_mix = mix_norm_bwd(da1, h0, dh1, g_pre_mix, sc1)

    parts = parts_in + parts_mix + parts_ff
    own = [sum_chips(p, l, chip_idx) for p, l in zip(parts, list(landed_in) + list(landed_mix) + list(landed_ff))]
    other = share_with_sibling(own)
    out = {}
    for (nm, w, _), a, b, m, v in zip(big, own, other, (m_w_in, m_w_a_out, m_w_b_out, m_w_o, m_w_ff1, m_w_ff2),
                                      (v_w_in, v_w_a_out, v_w_b_out, v_w_o, v_w_ff1, v_w_ff2)):
        out[nm] = tuple(t[None] for t in adamw_halves(w[0], a, b, m[0], v[0], core, "adamw_" + nm))

    mine = _pack([d_sh1, d_sc1, d_gt1, d_sh2, d_sc2, d_gt2, d_g_pre_mix, d_g_post_mix, d_g_pre_ffn, d_g_post_ffn, d_g_hgrn, d_g_sgu,
                  d_w_spatial, d_b_spatial[:, 0, :], d_lb])
    got = all_gather_small(mine, "gather_small_grads")
    total = sum_devices(got)
    g_b_ada, g_g1, g_g2, g_g3, g_g4, g_hg, g_sg, g_ws, g_bs, g_lb = _unpack(
        total, [(1, 6 * D), (1, D), (1, D), (1, D), (1, D), (1, HEAD_DIM), (1, 1024), w_spatial.shape, b_spatial.shape, (2, 1024)])
    g_lbl = lax.dynamic_slice(lb_logits_grad(g_lb, lb_full), (0, 0, chip * n_lb), (2, 2, n_lb))
    names = ["b_ada", "g_pre_mix", "g_post_mix", "g_pre_ffn", "g_post_ffn", "g_hgrn_norm", "g_sgu_norm", "w_spatial", "b_spatial", "lb_logits"]
    ws = [b_ada, g_pre_mix, g_post_mix, g_pre_ffn, g_post_ffn, g_hgrn_norm, g_sgu_norm, w_spatial, b_spatial, lb_logits]
    gs = [g_b_ada, g_g1, g_g2, g_g3, g_g4, g_hg, g_sg, g_ws, g_bs, g_lbl]
    ms = [m_b_ada, m_g_pre_mix, m_g_post_mix, m_g_pre_ffn, m_g_post_ffn, m_g_hgrn_norm, m_g_sgu_norm, m_w_spatial, m_b_spatial, m_lb_logits]
    vs = [v_b_ada, v_g_pre_mix, v_g_post_mix, v_g_pre_ffn, v_g_post_ffn, v_g_hgrn_norm, v_g_sgu_norm, v_w_spatial, v_b_spatial, v_lb_logits]
    shapes = [w.shape for w in ws]
    upd = adamw(_pack(ws), _pack(gs), _pack(ms), _pack(vs), "adamw_small")
    upd = [_unpack(u, shapes) for u in upd]
    for i, nm in enumerate(names):
        out[nm] = (gs[i], upd[0][i], upd[1][i], upd[2][i])

    dmod_all = got[:, :6 * D // 128, :].reshape(8, 6 * D)
    dmod_chip = lax.dynamic_slice(dmod_all, (0, chip * n_ada), (8, n_ada))
    out["w_ada"] = tuple(a[None] for a in wada_update(c_all, dmod_chip, w_ada[0], m_w_ada[0], v_w_ada[0]))

    order = ["w_ada", "b_ada", "g_pre_mix", "g_post_mix", "g_pre_ffn", "g_post_ffn", "w_in", "lb_logits", "g_hgrn_norm", "w_a_out",
             "g_sgu_norm", "w_spatial", "b_spatial", "w_b_out", "w_o", "w_ff1", "w_ff2"]
    return (loss, grad_x[None], *[out[nm][0] for nm in order], *[out[nm][1] for nm in order], *[out[nm][2] for nm in order],
            *[out[nm][3] for nm in order])
```

```python
import functools
import math

import jax
import jax.numpy as jnp
from jax import lax
from jax.experimental import pallas as pl
from jax.experimental.pallas import tpu as pltpu

F32, BF16 = jnp.float32, jnp.bfloat16
HI = lax.Precision.HIGHEST
MESH = pl.DeviceIdType.MESH
ANY = pl.BlockSpec(memory_space=pl.ANY)

EPS = 1e-6
D_MODEL = 2048
N_HEADS = 8
HEAD_DIM = 128
HGRN_CHUNK = 32
HGRN_BLOCK = 256
SGU_CHUNK = 128
SGU_GROUPS = 8
Q_SCALE = HEAD_DIM ** -0.5
COL_Q, COL_FFW, COL_FBW, COL_V, COL_OG, COL_U, COL_ZV, COL_GA, COL_GB = 0, 1, 2, 3, 4, 5, 6, 7, 9
N_PROJ = 11264
VMEM_BYTES_V7X = 64 * 1024 * 1024
VMEM_LIMIT = VMEM_BYTES_V7X - 8 * 1024 * 1024

ADAM_LR, ADAM_B1, ADAM_B2, ADAM_EPS, ADAM_WD, ADAM_STEP = 0.001, 0.9, 0.999, 1e-08, 0.01, 10
ADAM_C1 = 1.0 - ADAM_B1 ** ADAM_STEP
ADAM_C2 = 1.0 - ADAM_B2 ** ADAM_STEP


def _cp(*sem):
    return pltpu.CompilerParams(dimension_semantics=sem if sem else None, vmem_limit_bytes=VMEM_LIMIT)


def _vec(d):
    return pl.BlockSpec((1, d), lambda *_: (0, 0))


def _colsum(x):
    return jnp.sum(x, axis=0, keepdims=True)


def _nt(a, b):
    return lax.dot_general(a, b, (((1,), (1,)), ((), ())), preferred_element_type=F32)


def _tn(a, b):
    return lax.dot_general(a, b, (((0,), (0,)), ((), ())), preferred_element_type=F32)


def _nn(a, b):
    return jnp.dot(a, b, preferred_element_type=F32)


def _adamw(w, g, m, v):
    m2 = ADAM_B1 * m + (1.0 - ADAM_B1) * g
    v2 = ADAM_B2 * v + (1.0 - ADAM_B2) * (g * g)
    delta = -ADAM_LR * ((m2 / ADAM_C1) / (jnp.sqrt(v2 / ADAM_C2) + ADAM_EPS) + ADAM_WD * w)
    return delta, m2, v2


class _Comm:
    def __init__(self, operands, out_shape, aliases, n_sems, start, finish):
        self.operands, self.out_shape, self.aliases, self.n_sems = list(operands), list(out_shape), dict(aliases), n_sems
        self.start, self.finish = start, finish


def _pallas(body, *, name, grid, in_specs, out_specs, out_shape, scratch, semantics, operands, comm=None):
    if comm is None:
        res = pl.pallas_call(body, name=name, grid=grid, in_specs=in_specs, out_specs=out_specs, out_shape=out_shape,
                             scratch_shapes=scratch, compiler_params=_cp(*semantics))(*operands)
        return res, []
    n_in, n_out, n_scr = len(in_specs), len(out_specs), len(scratch)
    nci, nco = len(comm.operands), len(comm.out_shape)

    def with_comm(*refs):
        ins, rest = refs[:n_in], refs[n_in:]
        cin, rest = rest[:nci], rest[nci:]
        outs, rest = rest[:n_out], rest[n_out:]
        cout, rest = rest[:nco], rest[nco:]
        scr, (send, recv) = rest[:n_scr], rest[n_scr:]
        ids = [pl.program_id(a) for a in range(len(grid))]
        first = functools.reduce(jnp.logical_and, [i == 0 for i in ids])
        last = functools.reduce(jnp.logical_and, [i == g - 1 for i, g in zip(ids, grid)])

        @pl.when(first)
        def _():
            comm.start(cin, cout, send, recv)

        body(*ins, *outs, *scr)

        @pl.when(last)
        def _():
            comm.finish(cin, cout, send, recv)

    res = pl.pallas_call(
        with_comm, name=name, grid=grid, in_specs=list(in_specs) + [ANY] * nci, out_specs=list(out_specs) + [ANY] * nco,
        out_shape=list(out_shape) + comm.out_shape, input_output_aliases={n_in + i: n_out + o for i, o in comm.aliases.items()},
        scratch_shapes=list(scratch) + [pltpu.SemaphoreType.DMA((comm.n_sems,)), pltpu.SemaphoreType.DMA((comm.n_sems,))],
        compiler_params=_cp(*["arbitrary"] * len(grid)),
    )(*operands, *comm.operands)
    return res[:n_out], res[n_out:]


def matmul(a, b, *, mode, out_dtype, tm, tn, tk, name, split=None, comm=None):
    if mode == "tn":
        (K, M), (_, N) = a.shape, b.shape
    elif mode == "nt":
        (M, K), (N, _) = a.shape, b.shape
    else:
        (M, K), (_, N) = a.shape, b.shape
    tm, tn, tk = min(tm, M), min(tn, N), min(tk, K)
    nk = K // tk
    a_spec = pl.BlockSpec((tk, tm), lambda i, j, k: (k, i)) if mode == "tn" else pl.BlockSpec((tm, tk), lambda i, j, k: (i, k))
    b_spec = pl.BlockSpec((tn, tk), lambda i, j, k: (j, k)) if mode == "nt" else pl.BlockSpec((tk, tn), lambda i, j, k: (k, j))
    dot = {"nn": _nn, "nt": _nt, "tn": _tn}[mode]
    if split is None:
        out_shape = jax.ShapeDtypeStruct((M, N), out_dtype)
        out_spec = pl.BlockSpec((tm, tn), lambda i, j, k: (i, j))
    else:
        nj, nh = split
        rows, cols = M // nh, N // nj
        tm, tn = min(tm, rows), min(tn, cols)
        bi, bj = rows // tm, cols // tn
        out_shape = jax.ShapeDtypeStruct((nj, nh, rows, cols), out_dtype)
        out_spec = pl.BlockSpec((None, None, tm, tn), lambda i, j, k: (j // bj, i // bi, i % bi, j % bj))

    def body(a_ref, b_ref, o_ref, acc_ref):
        k = pl.program_id(2)

        @pl.when(k == 0)
        def _():
            acc_ref[...] = jnp.zeros_like(acc_ref)

        acc_ref[...] += dot(a_ref[...], b_ref[...])

        @pl.when(k == nk - 1)
        def _():
            o_ref[...] = acc_ref[...].astype(o_ref.dtype)

    (out,), landed = _pallas(
        body, name=name, grid=(M // tm, N // tn, nk), in_specs=[a_spec, b_spec], out_specs=[out_spec], out_shape=[out_shape],
        scratch=[pltpu.VMEM((tm, tn), F32)], semantics=("parallel", "parallel", "arbitrary"), operands=(a, b), comm=comm)
    return out if comm is None else (out, landed)


def cast_into_full(w, kind, chip, name):
    r, cc = w.shape
    tr = min(r, 512)
    nb = r // tr

    def body(chip_ref, w_ref, o_ref):
        o_ref[...] = w_ref[...].astype(BF16)

    if kind == "col":
        full, out_map = (r, 4 * cc), lambda i, chip_ref: (i, chip_ref[0])
    else:
        full, out_map = (4 * r, cc), lambda i, chip_ref: (chip_ref[0] * nb + i, 0)
    return pl.pallas_call(
        body, name=name, out_shape=jax.ShapeDtypeStruct(full, BF16),
        grid_spec=pltpu.PrefetchScalarGridSpec(
            num_scalar_prefetch=1, grid=(nb,), in_specs=[pl.BlockSpec((tr, cc), lambda i, chip_ref: (i, 0))],
            out_specs=pl.BlockSpec((tr, cc), out_map)),
        compiler_params=_cp("parallel"),
    )(chip, w)


def mod_matmul(c_all, w_ada, b_ada):
    D, N = w_ada.shape
    tn = 1024

    def body(c_ref, w_ref, b_ref, o_ref):
        c = c_ref[...]
        sc = c * jax.nn.sigmoid(c)
        o_ref[...] = jnp.dot(sc, w_ref[...], precision=HI, preferred_element_type=F32) + b_ref[...]

    return pl.pallas_call(
        body, name="mod_matmul", out_shape=jax.ShapeDtypeStruct((8, N), F32), grid=(N // tn,),
        in_specs=[pl.BlockSpec((8, D), lambda j: (0, 0)), pl.BlockSpec((D, tn), lambda j: (0, j)),
                  pl.BlockSpec((1, tn), lambda j: (0, j))],
        out_specs=pl.BlockSpec((8, tn), lambda j: (0, j)), compiler_params=_cp("parallel"),
    )(c_all, w_ada, b_ada)


def prenorm_matmul(h, g, sc, sh, w, *, relu2, name, comm=None):
    T, D = h.shape
    N = w.shape[1]
    tm, tn = min(512, T), 2048 if N % 2048 == 0 else 1024

    def body(h_ref, g_ref, sc_ref, sh_ref, w_ref, y_ref, a_ref, *hid_ref):
        @pl.when(pl.program_id(1) == 0)
        def _():
            x = h_ref[...]
            r = lax.rsqrt(jnp.mean(x * x, axis=-1, keepdims=True) + EPS)
            a_ref[...] = ((x * r) * g_ref[...] * (1.0 + sc_ref[...]) + sh_ref[...]).astype(BF16)

        y = _nn(a_ref[...], w_ref[...])
        y_ref[...] = y
        if relu2:
            p = jnp.maximum(y, 0.0)
            hid_ref[0][...] = (p * p).astype(BF16)

    out_shape = [jax.ShapeDtypeStruct((T, N), F32), jax.ShapeDtypeStruct((T, D), BF16)]
    out_specs = [pl.BlockSpec((tm, tn), lambda i, j: (i, j)), pl.BlockSpec((tm, D), lambda i, j: (i, 0))]
    if relu2:
        out_shape.append(jax.ShapeDtypeStruct((T, N), BF16))
        out_specs.append(pl.BlockSpec((tm, tn), lambda i, j: (i, j)))
    outs, landed = _pallas(
        body, name=name, grid=(T // tm, N // tn),
        in_specs=[pl.BlockSpec((tm, D), lambda i, j: (i, 0)), _vec(D), _vec(D), _vec(D), pl.BlockSpec((D, tn), lambda i, j: (0, j))],
        out_specs=out_specs, out_shape=out_shape, scratch=[], semantics=("parallel", "arbitrary"), operands=(h, g, sc, sh, w), comm=comm)
    return outs if comm is None else (outs, landed)


def _hgrn_lower_bound(l_ref):
    l0, l1 = l_ref[0:1, :], l_ref[1:2, :]
    m = jnp.maximum(l0, l1)
    e0, e1 = jnp.exp(l0 - m), jnp.exp(l1 - m)
    return e0 / (e0 + e1)


def _hgrn_chunk_masks(d):
    r = lax.broadcasted_iota(jnp.int32, (HGRN_BLOCK, HGRN_BLOCK), 0)
    c = lax.broadcasted_iota(jnp.int32, (HGRN_BLOCK, HGRN_BLOCK), 1)
    same = (r // HGRN_CHUNK) == (c // HGRN_CHUNK)
    fwd = d == 0
    tri = same & (((c <= r) & fwd) | ((c >= r) & jnp.logical_not(fwd)))
    tri_t = same & (((c >= r) & fwd) | ((c <= r) & jnp.logical_not(fwd)))
    return tri, jnp.where(tri, 1.0, 0.0).astype(BF16), jnp.where(tri_t, 1.0, 0.0).astype(BF16)


def _tri_sum(tri, x, terms):
    pieces, rest = [], x
    for t in range(terms):
        p = rest.astype(BF16)
        pieces.append(p)
        if t + 1 < terms:
            rest = rest - p.astype(F32)
    y = _nn(tri, jnp.concatenate(pieces, axis=1))
    w = x.shape[1]
    return sum(y[:, t * w:(t + 1) * w] for t in range(terms))


def _chunk_total(x):
    x3 = x.reshape(HGRN_BLOCK // HGRN_CHUNK, HGRN_CHUNK, x.shape[1])
    return jnp.broadcast_to(jnp.sum(x3, axis=1, keepdims=True), x3.shape).reshape(x.shape)


def _hgrn_gate(f, lb):
    s = jax.nn.sigmoid(f)
    sn = jax.nn.sigmoid(-f)
    fg = lb + (1.0 - lb) * s
    return s, sn, fg, jnp.log(fg), (1.0 - lb) * sn


def _hgrn_specs(T):
    col = lambda base: pl.BlockSpec((T, HEAD_DIM), lambda h, d: (0, base * N_HEADS + h))
    f_spec = pl.BlockSpec((T, HEAD_DIM), lambda h, d: (0, COL_FFW * N_HEADS + N_HEADS * d + h))
    l_spec = pl.BlockSpec((None, 2, HEAD_DIM), lambda h, d: (d, 0, h))
    return col, f_spec, l_spec


def hgrn_fwd(proj, lb_logits, comm=None):
    T = proj.shape[0]
    NC, CPB = T // HGRN_CHUNK, HGRN_BLOCK // HGRN_CHUNK
    col, f_spec, l_spec = _hgrn_specs(T)

    def body(l_ref, q_ref, f_ref, v_ref, o_ref, st_ref, dec_ref, qd_ref):
        d = pl.program_id(1)
        lb = _hgrn_lower_bound(l_ref)
        mask, mtri, _ = _hgrn_chunk_masks(d)

        def block(i, carry):
            rows = pl.ds(pl.multiple_of(i * HGRN_BLOCK, HGRN_BLOCK), HGRN_BLOCK)
            _, _, _, lf, k = _hgrn_gate(f_ref[rows, :], lb)
            b = _tri_sum(mtri, lf, 3)
            bl = _chunk_total(lf)
            qd = (q_ref[rows, :] * Q_SCALE * jnp.exp(b)).astype(BF16)
            kd = (k * jnp.exp(-b)).astype(BF16)
            ke = (k * jnp.exp(bl - b)).astype(BF16)
            vb = v_ref[rows, :].astype(BF16)
            att = jnp.where(mask, _nt(qd, kd), 0.0).astype(BF16)
            o_ref[rows, :] = jnp.where(d == 0, 0.0, o_ref[rows, :]) + _nn(att, vb)
            qd_ref[rows, :] = qd
            dec = jnp.exp(bl)
            for cc in range(CPB):
                sl = slice(cc * HGRN_CHUNK, (cc + 1) * HGRN_CHUNK)
                n = i * CPB + cc
                st_ref[n] = _tn(vb[sl], ke[sl])
                dec_ref[n] = dec[cc * HGRN_CHUNK:cc * HGRN_CHUNK + 8, :]
            return carry

        lax.fori_loop(0, T // HGRN_BLOCK, block, 0)

        def scan(t, s):
            n = jnp.where(d == 0, t, NC - 1 - t)
            u = st_ref[n]
            st_ref[n] = s
            return dec_ref[n][0:1, :] * s + u

        lax.fori_loop(0, NC, scan, jnp.zeros((HEAD_DIM, HEAD_DIM), F32))

        def inter(i, carry):
            rows = pl.ds(pl.multiple_of(i * HGRN_BLOCK, HGRN_BLOCK), HGRN_BLOCK)
            qd = qd_ref[rows, :]
            o_ref[rows, :] += jnp.concatenate(
                [_nt(qd[cc * HGRN_CHUNK:(cc + 1) * HGRN_CHUNK], st_ref[i * CPB + cc].astype(BF16)) for cc in range(CPB)], axis=0)
            return carry

        lax.fori_loop(0, T // HGRN_BLOCK, inter, 0)

    (o,), landed = _pallas(
        body, name="hgrn_fwd", grid=(N_HEADS, 2), in_specs=[l_spec, col(COL_Q), f_spec, col(COL_V)],
        out_specs=[pl.BlockSpec((T, HEAD_DIM), lambda h, d: (0, h))], out_shape=[jax.ShapeDtypeStruct((T, N_HEADS * HEAD_DIM), F32)],
        scratch=[pltpu.VMEM((NC, HEAD_DIM, HEAD_DIM), F32), pltpu.VMEM((NC, 8, HEAD_DIM), F32), pltpu.VMEM((T, HEAD_DIM), BF16)],
        semantics=("parallel", "arbitrary"), operands=(lb_logits, proj, proj, proj), comm=comm)
    return o if comm is None else (o, landed)


def hgrn_post_fwd(o, proj, g_norm):
    T, W = o.shape
    tm = min(256, T)

    def body(o_ref, og_ref, g_ref, y_ref):
        g = g_ref[...]
        for h in range(N_HEADS):
            sl = slice(h * HEAD_DIM, (h + 1) * HEAD_DIM)
            x = o_ref[:, sl]
            r = lax.rsqrt(jnp.mean(x * x, axis=-1, keepdims=True) + EPS)
            og = og_ref[:, sl]
            y_ref[:, sl] = ((x * r) * g * (og * jax.nn.sigmoid(og))).astype(BF16)

    return pl.pallas_call(
        body, name="hgrn_post_fwd", out_shape=jax.ShapeDtypeStruct((T, W), BF16), grid=(T // tm,),
        in_specs=[pl.BlockSpec((tm, W), lambda i: (i, 0)), pl.BlockSpec((tm, W), lambda i: (i, COL_OG)), _vec(HEAD_DIM)],
        out_specs=pl.BlockSpec((tm, W), lambda i: (i, 0)), compiler_params=_cp("parallel"),
    )(o, proj, g_norm)


def _gelu(x):
    return 0.5 * x * (1.0 + lax.erf(x * (1.0 / math.sqrt(2.0))))


def _gelu_grad(x):
    return 0.5 * (1.0 + lax.erf(x * (1.0 / math.sqrt(2.0)))) + x * jnp.exp(-0.5 * x * x) * (1.0 / math.sqrt(2.0 * math.pi))


def _sgu_mix(u_ref, v_ref, g_ref, ws_ref, bst_ref):
    W = u_ref.shape[1]
    zu, zv = _gelu(u_ref[...]), _gelu(v_ref[...])
    dv = zv - jnp.mean(zv, axis=-1, keepdims=True)
    rstd = lax.rsqrt(jnp.mean(dv * dv, axis=-1, keepdims=True) + EPS)
    dhat = dv * rstd
    vn = (dhat * g_ref[...]).astype(BF16)
    gw = W // SGU_GROUPS
    vm = [_nn(ws_ref[g].astype(BF16), vn[:, g * gw:(g + 1) * gw]) + bst_ref[:, g:g + 1] for g in range(SGU_GROUPS)]
    return zu, rstd, dhat, vn, jnp.concatenate(vm, axis=1)


def sgu_fwd(proj, g_norm, w_spatial, b_spatial_t):
    T = proj.shape[0]
    W = 1024
    n_chunks = T // SGU_CHUNK

    def body(u_ref, v_ref, g_ref, ws_ref, bst_ref, y_ref):
        zu, _, _, _, vm = _sgu_mix(u_ref, v_ref, g_ref, ws_ref, bst_ref)
        y_ref[...] = (zu * vm).astype(BF16)

    blk = lambda cb: pl.BlockSpec((SGU_CHUNK, W), lambda i: (i, cb))
    return pl.pallas_call(
        body, name="sgu_fwd", out_shape=jax.ShapeDtypeStruct((T, W), BF16), grid=(n_chunks,),
        in_specs=[blk(COL_U), blk(COL_ZV), _vec(W), pl.BlockSpec((SGU_GROUPS, SGU_CHUNK, SGU_CHUNK), lambda i: (0, 0, 0)),
                  pl.BlockSpec((SGU_CHUNK, SGU_GROUPS), lambda i: (0, 0))],
        out_specs=blk(0), compiler_params=_cp("parallel"),
    )(proj, proj, g_norm, w_spatial, b_spatial_t)


def merge_matmul(ya_pre, sgu, w_a, w_b, proj):
    T, K = ya_pre.shape
    N = w_a.shape[1]
    tm, tn = min(512, T), 512
    gpb = 1024 // tn

    def body(a_ref, b_ref, wa_ref, wb_ref, ga_ref, gb_ref, ya_ref, yb_ref, m_ref):
        ya = _nn(a_ref[...], wa_ref[...])
        yb = _nn(b_ref[...], wb_ref[...])
        ya_ref[...] = ya
        yb_ref[...] = yb
        m_ref[...] = (jax.nn.sigmoid(ga_ref[...]) * ya + jax.nn.sigmoid(gb_ref[...]) * yb).astype(BF16)

    lhs = pl.BlockSpec((tm, K), lambda i, j: (i, 0))
    rhs = pl.BlockSpec((K, tn), lambda i, j: (0, j))
    out = pl.BlockSpec((tm, tn), lambda i, j: (i, j))
    return pl.pallas_call(
        body, name="merge_matmul", grid=(T // tm, N // tn),
        out_shape=[jax.ShapeDtypeStruct((T, N), F32), jax.ShapeDtypeStruct((T, N), F32), jax.ShapeDtypeStruct((T, N), BF16)],
        in_specs=[lhs, lhs, rhs, rhs, pl.BlockSpec((tm, tn), lambda i, j: (i, COL_GA * gpb + j)),
                  pl.BlockSpec((tm, tn), lambda i, j: (i, COL_GB * gpb + j))],
        out_specs=[out, out, out], compiler_params=_cp("parallel", "parallel"),
    )(ya_pre, sgu, w_a, w_b, proj, proj)


def out_proj(merged, w_o, h0, gt1, g_post):
    T, D = h0.shape
    tm = min(256, T)

    def body(m_ref, w_ref, h_ref, gt_ref, gp_ref, mo_ref, h1_ref):
        mo = _nn(m_ref[...], w_ref[...])
        mo_ref[...] = mo
        r = lax.rsqrt(jnp.mean(mo * mo, axis=-1, keepdims=True) + EPS)
        h1_ref[...] = h_ref[...] + gt_ref[...] * ((mo * r) * gp_ref[...])

    row = pl.BlockSpec((tm, D), lambda i: (i, 0))
    return pl.pallas_call(
        body, name="out_proj", grid=(T // tm,),
        out_shape=[jax.ShapeDtypeStruct((T, D), F32), jax.ShapeDtypeStruct((T, D), F32)],
        in_specs=[row, pl.BlockSpec((D, D), lambda i: (0, 0)), row, _vec(D), _vec(D)],
        out_specs=[row, row], compiler_params=_cp("parallel"),
    )(merged, w_o, h0, gt1, g_post)


def ff2_loss(hid, w_ff2, h1, tgt, gt2, g_post):
    T, K = hid.shape
    D = w_ff2.shape[1]
    tm, tk = min(256, T), 2048
    nk = K // tk

    def body(a_ref, w_ref, h_ref, t_ref, gt_ref, g_ref, dy_ref, dff_ref, loss_ref, dgt_ref, dg_ref, acc_ref):
        i, k = pl.program_id(0), pl.program_id(1)

        @pl.when(k == 0)
        def _():
            acc_ref[...] = jnp.zeros_like(acc_ref)

        @pl.when((k == 0) & (i == 0))
        def _():
            loss_ref[...] = jnp.zeros_like(loss_ref)
            dgt_ref[...] = jnp.zeros_like(dgt_ref)
            dg_ref[...] = jnp.zeros_like(dg_ref)

        acc_ref[...] += _nn(a_ref[...], w_ref[...])

        @pl.when(k == nk - 1)
        def _():
            ff = acc_ref[...]
            gt, g = gt_ref[...], g_ref[...]
            r = lax.rsqrt(jnp.mean(ff * ff, axis=-1, keepdims=True) + EPS)
            fhat = ff * r
            nf = fhat * g
            err = (h_ref[...] + gt * nf) - t_ref[...]
            loss_ref[...] += jnp.sum(err * err)
            dy = err * (1.0 / D)
            dy_ref[...] = dy
            dgt_ref[...] += _colsum(dy * nf)
            dnf = dy * gt
            dg_ref[...] += _colsum(dnf * fhat)
            u = dnf * g
            dff_ref[...] = (r * (u - fhat * jnp.mean(u * fhat, axis=-1, keepdims=True))).astype(BF16)

    row = pl.BlockSpec((tm, D), lambda i, k: (i, 0))
    vec = pl.BlockSpec((1, D), lambda i, k: (0, 0))
    return pl.pallas_call(
        body, name="ff2_loss", grid=(T // tm, nk),
        out_shape=[jax.ShapeDtypeStruct((T, D), F32), jax.ShapeDtypeStruct((T, D), BF16), jax.ShapeDtypeStruct((8, 128), F32),
                   jax.ShapeDtypeStruct((1, D), F32), jax.ShapeDtypeStruct((1, D), F32)],
        in_specs=[pl.BlockSpec((tm, tk), lambda i, k: (i, k)), pl.BlockSpec((tk, D), lambda i, k: (k, 0)), row, row, vec, vec],
        out_specs=[row, row, pl.BlockSpec((8, 128), lambda i, k: (0, 0)), vec, vec],
        scratch_shapes=[pltpu.VMEM((tm, D), F32)], compiler_params=_cp("arbitrary", "arbitrary"),
    )(hid, w_ff2, h1, tgt, gt2, g_post)


def ff2_bwd(dff, w_ff2, f1):
    T, D = dff.shape
    K = w_ff2.shape[0]
    tm, tn = min(512, T), 2048

    def body(a_ref, w_ref, f_ref, o_ref):
        o_ref[...] = (_nt(a_ref[...], w_ref[...]) * (2.0 * jnp.maximum(f_ref[...], 0.0))).astype(BF16)

    return pl.pallas_call(
        body, name="ff2_bwd", out_shape=jax.ShapeDtypeStruct((T, K), BF16), grid=(T // tm, K // tn),
        in_specs=[pl.BlockSpec((tm, D), lambda i, j: (i, 0)), pl.BlockSpec((tn, D), lambda i, j: (j, 0)),
                  pl.BlockSpec((tm, tn), lambda i, j: (i, j))],
        out_specs=pl.BlockSpec((tm, tn), lambda i, j: (i, j)), compiler_params=_cp("parallel", "parallel"),
    )(dff, w_ff2, f1)


def ffn_norm_bwd(dy, da2, h1, mo, g_pre2, sc2, gt1, g_post):
    T, D = dy.shape
    tm = min(256, T)

    def body(dy_ref, da_ref, h_ref, mo_ref, g2_ref, sc_ref, gt_ref, gp_ref, dh_ref, dmo_ref, s_sh, s_sc, s_g2, s_gt, s_gp):
        @pl.when(pl.program_id(0) == 0)
        def _():
            for s in (s_sh, s_sc, s_g2, s_gt, s_gp):
                s[...] = jnp.zeros_like(s)

        h1, da = h_ref[...], da_ref[...]
        g2, sc = g2_ref[...], sc_ref[...]
        r2 = lax.rsqrt(jnp.mean(h1 * h1, axis=-1, keepdims=True) + EPS)
        n2 = h1 * r2
        s_sh[...] += _colsum(da)
        s_sc[...] += _colsum(da * (n2 * g2))
        s_g2[...] += _colsum(da * (1.0 + sc) * n2)
        dn2 = da * g2 * (1.0 + sc)
        dh1 = dy_ref[...] + r2 * (dn2 - n2 * jnp.mean(dn2 * n2, axis=-1, keepdims=True))
        dh_ref[...] = dh1
        mo = mo_ref[...]
        gt, gp = gt_ref[...], gp_ref[...]
        r = lax.rsqrt(jnp.mean(mo * mo, axis=-1, keepdims=True) + EPS)
        mhat = mo * r
        s_gt[...] += _colsum(dh1 * (mhat * gp))
        dnm = dh1 * gt
        s_gp[...] += _colsum(dnm * mhat)
        u = dnm * gp
        dmo_ref[...] = (r * (u - mhat * jnp.mean(u * mhat, axis=-1, keepdims=True))).astype(BF16)

    row = pl.BlockSpec((tm, D), lambda i: (i, 0))
    vec_out = jax.ShapeDtypeStruct((1, D), F32)
    return pl.pallas_call(
        body, name="ffn_norm_bwd", grid=(T // tm,),
        out_shape=[jax.ShapeDtypeStruct((T, D), F32), jax.ShapeDtypeStruct((T, D), BF16)] + [vec_out] * 5,
        in_specs=[row, row, row, row] + [_vec(D)] * 4, out_specs=[row, row] + [_vec(D)] * 5,
        compiler_params=_cp("arbitrary"),
    )(dy, da2, h1, mo, g_pre2, sc2, gt1, g_post)


def out_proj_bwd(dmo, w_o, y_a, y_b, proj):
    T, D = dmo.shape
    tm, tn = min(512, T), 512
    gpb = 1024 // tn

    def body(a_ref, w_ref, ya_ref, yb_ref, ga_ref, gb_ref, dya_ref, dyb_ref, dga_ref, dgb_ref):
        dm = _nt(a_ref[...], w_ref[...])
        sa, sb = jax.nn.sigmoid(ga_ref[...]), jax.nn.sigmoid(gb_ref[...])
        dya_ref[...] = (dm * sa).astype(BF16)
        dyb_ref[...] = (dm * sb).astype(BF16)
        dga_ref[...] = (dm * ya_ref[...] * sa * (1.0 - sa)).astype(BF16)
        dgb_ref[...] = (dm * yb_ref[...] * sb * (1.0 - sb)).astype(BF16)

    out = pl.BlockSpec((tm, tn), lambda i, j: (i, j))
    return pl.pallas_call(
        body, name="out_proj_bwd", grid=(T // tm, D // tn), out_shape=[jax.ShapeDtypeStruct((T, D), BF16)] * 4,
        in_specs=[pl.BlockSpec((tm, D), lambda i, j: (i, 0)), pl.BlockSpec((tn, D), lambda i, j: (j, 0)), out, out,
                  pl.BlockSpec((tm, tn), lambda i, j: (i, COL_GA * gpb + j)), pl.BlockSpec((tm, tn), lambda i, j: (i, COL_GB * gpb + j))],
        out_specs=[out] * 4, compiler_params=_cp("parallel", "parallel"),
    )(dmo, w_o, y_a, y_b, proj, proj)


def sgu_bwd(proj, dsgu, g_norm, w_spatial, b_spatial_t):
    T = proj.shape[0]
    W = 1024
    gw = W // SGU_GROUPS

    def body(u_ref, v_ref, ds_ref, g_ref, ws_ref, bst_ref, dz_ref, dw_ref, db_ref, dg_ref):
        @pl.when(pl.program_id(0) == 0)
        def _():
            dw_ref[...] = jnp.zeros_like(dw_ref)
            db_ref[...] = jnp.zeros_like(db_ref)
            dg_ref[...] = jnp.zeros_like(dg_ref)

        zu, rstd, dhat, vn, vm = _sgu_mix(u_ref, v_ref, g_ref, ws_ref, bst_ref)
        ds = ds_ref[...]
        du = ds * vm
        dvm = ds * zu
        dvm_b = dvm.astype(BF16)
        ones = jnp.ones((8, gw), F32)
        dvn = []
        for g in range(SGU_GROUPS):
            sl = slice(g * gw, (g + 1) * gw)
            dw_ref[g] += _nt(dvm_b[:, sl], vn[:, sl])
            db_ref[g] += lax.dot_general(ones, dvm[:, sl], (((1,), (1,)), ((), ())), precision=HI, preferred_element_type=F32)
            dvn.append(_tn(ws_ref[g].astype(BF16), dvm_b[:, sl]))
        dvn = jnp.concatenate(dvn, axis=1)
        dg_ref[...] += _colsum(dvn * dhat)
        ddh = dvn * g_ref[...]
        dzv = rstd * (ddh - jnp.mean(ddh, axis=-1, keepdims=True) - dhat * jnp.mean(ddh * dhat, axis=-1, keepdims=True))
        dz_ref[:, 0:W] = (du * _gelu_grad(u_ref[...])).astype(BF16)
        dz_ref[:, W:2 * W] = (dzv * _gelu_grad(v_ref[...])).astype(BF16)

    blk = lambda cb: pl.BlockSpec((SGU_CHUNK, W), lambda i: (i, cb))
    full3 = lambda a, b, c: pl.BlockSpec((a, b, c), lambda i: (0, 0, 0))
    return pl.pallas_call(
        body, name="sgu_bwd", grid=(T // SGU_CHUNK,),
        out_shape=[jax.ShapeDtypeStruct((T, 2 * W), BF16), jax.ShapeDtypeStruct((SGU_GROUPS, SGU_CHUNK, SGU_CHUNK), F32),
                   jax.ShapeDtypeStruct((SGU_GROUPS, 8, SGU_CHUNK), F32), jax.ShapeDtypeStruct((1, W), F32)],
        in_specs=[blk(COL_U), blk(COL_ZV), blk(0), _vec(W), full3(SGU_GROUPS, SGU_CHUNK, SGU_CHUNK),
                  pl.BlockSpec((SGU_CHUNK, SGU_GROUPS), lambda i: (0, 0))],
        out_specs=[pl.BlockSpec((SGU_CHUNK, 2 * W), lambda i: (i, 0)), full3(SGU_GROUPS, SGU_CHUNK, SGU_CHUNK),
                   full3(SGU_GROUPS, 8, SGU_CHUNK), _vec(W)],
        compiler_params=_cp("arbitrary"),
    )(proj, proj, dsgu, g_norm, w_spatial, b_spatial_t)


def hgrn_post_bwd(dya, o, proj, g_norm):
    T, W = o.shape
    tm = min(256, T)

    def body(dy_ref, o_ref, og_ref, g_ref, do_ref, dog_ref, dg_ref):
        @pl.when(pl.program_id(0) == 0)
        def _():
            dg_ref[...] = jnp.zeros_like(dg_ref)

        g = g_ref[...]
        dg = jnp.zeros((1, HEAD_DIM), F32)
        for h in range(N_HEADS):
            sl = slice(h * HEAD_DIM, (h + 1) * HEAD_DIM)
            x, og, dy = o_ref[:, sl], og_ref[:, sl], dy_ref[:, sl]
            r = lax.rsqrt(jnp.mean(x * x, axis=-1, keepdims=True) + EPS)
            xhat = x * r
            s = jax.nn.sigmoid(og)
            don = dy * (og * s)
            dog_ref[:, sl] = (dy * (xhat * g) * (s * (1.0 + og * (1.0 - s)))).astype(BF16)
            dg += _colsum(don * xhat)
            u = don * g
            do_ref[:, sl] = r * (u - xhat * jnp.mean(u * xhat, axis=-1, keepdims=True))
        dg_ref[...] += dg

    row = pl.BlockSpec((tm, W), lambda i: (i, 0))
    return pl.pallas_call(
        body, name="hgrn_post_bwd", grid=(T // tm,),
        out_shape=[jax.ShapeDtypeStruct((T, W), F32), jax.ShapeDtypeStruct((T, W), BF16), jax.ShapeDtypeStruct((1, HEAD_DIM), F32)],
        in_specs=[row, row, pl.BlockSpec((tm, W), lambda i: (i, COL_OG)), _vec(HEAD_DIM)],
        out_specs=[row, row, _vec(HEAD_DIM)], compiler_params=_cp("arbitrary"),
    )(dya, o, proj, g_norm)


def hgrn_bwd(proj, do, lb_logits, comm=None):
    T = proj.shape[0]
    NC, CPB = T // HGRN_CHUNK, HGRN_BLOCK // HGRN_CHUNK
    W = N_HEADS * HEAD_DIM
    col, f_spec, l_spec = _hgrn_specs(T)

    def body(l_ref, q_ref, f_ref, v_ref, do_ref, dq_ref, dv_ref, dlg_ref, dlb_ref, st_ref, dst_ref, dec_ref, ddec_ref, dqa_ref, dva_ref):
        d = pl.program_id(1)
        lb = _hgrn_lower_bound(l_ref)
        oml = 1.0 - lb
        mask, mtri, mtri_t = _hgrn_chunk_masks(d)

        def values(rows):
            s, sn, fg, lf, k = _hgrn_gate(f_ref[rows, :], lb)
            b = _tri_sum(mtri, lf, 3)
            bl = _chunk_total(lf)
            eb, enb, ee = jnp.exp(b), jnp.exp(-b), jnp.exp(bl - b)
            qd = q_ref[rows, :] * Q_SCALE * eb
            return s, sn, fg, k, bl, eb, enb, ee, qd, k * enb, k * ee

        def block1(i, carry):
            rows = pl.ds(pl.multiple_of(i * HGRN_BLOCK, HGRN_BLOCK), HGRN_BLOCK)
            _, _, _, _, bl, _, _, _, qd, _, ke = values(rows)
            qd, ke = qd.astype(BF16), ke.astype(BF16)
            vb, dob = v_ref[rows, :].astype(BF16), do_ref[rows, :].astype(BF16)
            dec = jnp.exp(bl)
            for cc in range(CPB):
                sl = slice(cc * HGRN_CHUNK, (cc + 1) * HGRN_CHUNK)
                n = i * CPB + cc
                st_ref[n] = _tn(vb[sl], ke[sl])
                dst_ref[n] = _tn(dob[sl], qd[sl])
                dec_ref[n] = dec[cc * HGRN_CHUNK:cc * HGRN_CHUNK + 8, :]
            return carry

        lax.fori_loop(0, T // HGRN_BLOCK, block1, 0)

        def scan(t, s):
            n = jnp.where(d == 0, t, NC - 1 - t)
            u = st_ref[n]
            st_ref[n] = s
            return dec_ref[n][0:1, :] * s + u

        lax.fori_loop(0, NC, scan, jnp.zeros((HEAD_DIM, HEAD_DIM), F32))

        def rscan(t, ds):
            n = jnp.where(d == 0, NC - 1 - t, t)
            w = dst_ref[n]
            dst_ref[n] = ds
            ddec_ref[n] = jnp.broadcast_to(_colsum(ds * st_ref[n]), (8, HEAD_DIM))
            return dec_ref[n][0:1, :] * ds + w

        lax.fori_loop(0, NC, rscan, jnp.zeros((HEAD_DIM, HEAD_DIM), F32))

        def block3(i, dlb):
            rows = pl.ds(pl.multiple_of(i * HGRN_BLOCK, HGRN_BLOCK), HGRN_BLOCK)
            s, sn, fg, k, bl, eb, enb, ee, qd, kd, ke = values(rows)
            qdb, kdb, keb = qd.astype(BF16), kd.astype(BF16), ke.astype(BF16)
            vb, dob = v_ref[rows, :].astype(BF16), do_ref[rows, :].astype(BF16)
            att = jnp.where(mask, _nt(qdb, kdb), 0.0).astype(BF16)
            datt = jnp.where(mask, _nt(dob, vb), 0.0).astype(BF16)
            dv = _tn(att, dob)
            dqd = _nn(datt, kdb)
            dkd = _tn(datt, qdb)
            dv_i, dqd_i, dke, ddl = [], [], [], []
            for cc in range(CPB):
                sl = slice(cc * HGRN_CHUNK, (cc + 1) * HGRN_CHUNK)
                n = i * CPB + cc
                st_b, dst_b = st_ref[n].astype(BF16), dst_ref[n].astype(BF16)
                dv_i.append(_nt(keb[sl], dst_b))
                dqd_i.append(_nn(dob[sl], st_b))
                dke.append(_nn(vb[sl], dst_b))
                ddl.append(jnp.broadcast_to(ddec_ref[n][0:1, :] * dec_ref[n][0:1, :], (HGRN_CHUNK, HEAD_DIM)))
            dv = dv + jnp.concatenate(dv_i, axis=0)
            dqd = dqd + jnp.concatenate(dqd_i, axis=0)
            dke = jnp.concatenate(dke, axis=0)
            dq = dqd * eb * Q_SCALE
            dk = dkd * enb + dke * ee
            t_end = dke * ke
            db = dqd * qd - dkd * kd - t_end
            dlf = _tri_sum(mtri_t, db, 2) + _chunk_total(t_end) + jnp.concatenate(ddl, axis=0)
            e = dlf / fg - dk
            dlg_ref[rows, :] = (oml * e * s * sn).astype(BF16)

            @pl.when(d == 0)
            def _():
                dqa_ref[rows, :] = dq
                dva_ref[rows, :] = dv

            @pl.when(d == 1)
            def _():
                dq_ref[rows, :] = (dqa_ref[rows, :] + dq).astype(BF16)
                dv_ref[rows, :] = (dva_ref[rows, :] + dv).astype(BF16)

            return dlb + _colsum(e * sn)

        dlb_ref[...] = lax.fori_loop(0, T // HGRN_BLOCK, block3, jnp.zeros((1, HEAD_DIM), F32))

    head = pl.BlockSpec((T, HEAD_DIM), lambda h, d: (0, h))
    big = pltpu.VMEM((NC, HEAD_DIM, HEAD_DIM), F32)
    small = pltpu.VMEM((NC, 8, HEAD_DIM), F32)
    acc = pltpu.VMEM((T, HEAD_DIM), F32)
    outs, landed = _pallas(
        body, name="hgrn_bwd", grid=(N_HEADS, 2),
        out_shape=[jax.ShapeDtypeStruct((T, W), BF16), jax.ShapeDtypeStruct((T, W), BF16), jax.ShapeDtypeStruct((T, 2 * W), BF16),
                   jax.ShapeDtypeStruct((2, 1, W), F32)],
        in_specs=[l_spec, col(COL_Q), f_spec, col(COL_V), head],
        out_specs=[head, head, pl.BlockSpec((T, HEAD_DIM), lambda h, d: (0, N_HEADS * d + h)),
                   pl.BlockSpec((None, 1, HEAD_DIM), lambda h, d: (d, 0, h))],
        scratch=[big, big, small, small, acc, acc], semantics=("parallel", "arbitrary"), operands=(lb_logits, proj, proj, proj, do), comm=comm)
    return outs if comm is None else (outs, landed)


def mix_norm_bwd(da1, h0, dh1, g_pre, sc1):
    T, D = h0.shape
    tm = min(256, T)

    def body(da_ref, h_ref, dh_ref, g_ref, sc_ref, gx_ref, s_sh, s_sc, s_g):
        @pl.when(pl.program_id(0) == 0)
        def _():
            for s in (s_sh, s_sc, s_g):
                s[...] = jnp.zeros_like(s)

        h, da = h_ref[...], da_ref[...]
        g, sc = g_ref[...], sc_ref[...]
        r = lax.rsqrt(jnp.mean(h * h, axis=-1, keepdims=True) + EPS)
        n = h * r
        s_sh[...] += _colsum(da)
        s_sc[...] += _colsum(da * (n * g))
        s_g[...] += _colsum(da * (1.0 + sc) * n)
        dn = da * g * (1.0 + sc)
        gx_ref[...] = dh_ref[...] + r * (dn - n * jnp.mean(dn * n, axis=-1, keepdims=True))

    row = pl.BlockSpec((tm, D), lambda i: (i, 0))
    return pl.pallas_call(
        body, name="mix_norm_bwd", grid=(T // tm,),
        out_shape=[jax.ShapeDtypeStruct((T, D), F32)] + [jax.ShapeDtypeStruct((1, D), F32)] * 3,
        in_specs=[row, row, row, _vec(D), _vec(D)], out_specs=[row] + [_vec(D)] * 3, compiler_params=_cp("arbitrary"),
    )(da1, h0, dh1, g_pre, sc1)


def adamw(w, g, m, v, name):
    R, C = w.shape
    tr = R if R * C * 4 <= (1 << 21) else max(8, ((1 << 21) // (C * 4)) // 8 * 8)
    while R % tr:
        tr -= 8

    def body(w_ref, g_ref, m_ref, v_ref, d_ref, m2_ref, v2_ref):
        d_ref[...], m2_ref[...], v2_ref[...] = _adamw(w_ref[...], g_ref[...], m_ref[...], v_ref[...])

    row = pl.BlockSpec((tr, C), lambda i: (i, 0))
    return pl.pallas_call(
        body, name=name, grid=(R // tr,), out_shape=[jax.ShapeDtypeStruct((R, C), F32)] * 3,
        in_specs=[row] * 4, out_specs=[row] * 3, compiler_params=_cp("parallel"),
    )(w, g, m, v)


def wada_update(c_all, dmod, w, m, v, comm):
    D, N = w.shape
    tm, tn = 512, 1024

    def body(c_ref, dm_ref, w_ref, m_ref, v_ref, g_ref, d_ref, m2_ref, v2_ref):
        c = c_ref[...]
        g = lax.dot_general(c * jax.nn.sigmoid(c), dm_ref[...], (((0,), (0,)), ((), ())), precision=HI, preferred_element_type=F32)
        g_ref[...] = g
        d_ref[...], m2_ref[...], v2_ref[...] = _adamw(w_ref[...], g, m_ref[...], v_ref[...])

    blk = pl.BlockSpec((tm, tn), lambda i, j: (i, j))
    return _pallas(
        body, name="wada_update", grid=(D // tm, N // tn), out_shape=[jax.ShapeDtypeStruct((D, N), F32)] * 4,
        in_specs=[pl.BlockSpec((8, tm), lambda i, j: (0, i)), pl.BlockSpec((8, tn), lambda i, j: (0, j)), blk, blk, blk],
        out_specs=[blk] * 4, scratch=[], semantics=("parallel", "parallel"), operands=(c_all, dmod, w, m, v), comm=comm)


def sum_devices(gathered, name):
    n, R, C = gathered.shape

    def body(g_ref, o_ref):
        s = g_ref[0]
        for i in range(1, n):
            s = s + g_ref[i]
        o_ref[...] = s

    return pl.pallas_call(body, name=name, out_shape=jax.ShapeDtypeStruct((R, C), F32), compiler_params=_cp())(gathered)


def lb_logits_grad(dlb, lb_logits):
    def body(d_ref, l_ref, o_ref):
        for d in range(2):
            l0, l1 = l_ref[d, 0:1, :], l_ref[d, 1:2, :]
            m = jnp.maximum(l0, l1)
            e0, e1 = jnp.exp(l0 - m), jnp.exp(l1 - m)
            p0, p1 = e0 / (e0 + e1), e1 / (e0 + e1)
            g = d_ref[d:d + 1, :]
            o_ref[d, 0:1, :] = p0 * (g - p0 * g)
            o_ref[d, 1:2, :] = -p1 * (p0 * g)

    return pl.pallas_call(body, name="lb_logits_grad", out_shape=jax.ShapeDtypeStruct(lb_logits.shape, F32), compiler_params=_cp())(dlb, lb_logits)


def add_halves(g, landed, core):
    nj, _, r, cc = g.shape
    tr = min(256, r)

    def body(core_ref, g_ref, l_ref, o_ref):
        o_ref[...] = (g_ref[...].astype(F32) + l_ref[...].astype(F32)).astype(BF16)

    return pl.pallas_call(
        body, name="add_halves_%dx%d" % (r, cc), out_shape=jax.ShapeDtypeStruct((nj, r, cc), BF16),
        grid_spec=pltpu.PrefetchScalarGridSpec(
            num_scalar_prefetch=1, grid=(nj, r // tr),
            in_specs=[pl.BlockSpec((None, None, tr, cc), lambda j, i, core_ref: (j, core_ref[0], i, 0)),
                      pl.BlockSpec((None, None, tr, cc), lambda j, i, core_ref: (j, 0, i, 0))],
            out_specs=pl.BlockSpec((None, tr, cc), lambda j, i, core_ref: (j, i, 0))),
        compiler_params=_cp("parallel", "parallel"),
    )(core, g, landed)


def sum_chips(parts, landed, chip):
    nj, r, cc = parts.shape
    tr = min(256, r)

    def body(chip_ref, p_ref, l_ref, o_ref):
        mine = p_ref[...].astype(F32)
        s = None
        for j in range(nj):
            t = jnp.where(chip_ref[0] == j, mine, l_ref[j].astype(F32))
            s = t if s is None else s + t
        o_ref[...] = s

    return pl.pallas_call(
        body, name="sum_chips_%dx%d" % (r, cc), out_shape=jax.ShapeDtypeStruct((r, cc), F32),
        grid_spec=pltpu.PrefetchScalarGridSpec(
            num_scalar_prefetch=1, grid=(r // tr,),
            in_specs=[pl.BlockSpec((None, tr, cc), lambda i, chip_ref: (chip_ref[0], i, 0)),
                      pl.BlockSpec((nj, tr, cc), lambda i, chip_ref: (0, i, 0))],
            out_specs=pl.BlockSpec((tr, cc), lambda i, chip_ref: (i, 0))),
        compiler_params=_cp("parallel"),
    )(chip, parts, landed)


def adamw_halves(w, own, other, m, v, core, name):
    r, cc = own.shape
    tr = min(128, r)
    nb = r // tr

    def body(core_ref, w_ref, a_ref, b_ref, m_ref, v_ref, g_ref, d_ref, m2_ref, v2_ref):
        g = jnp.where(pl.program_id(0) == core_ref[0], a_ref[...], b_ref[...])
        g_ref[...] = g
        d_ref[...], m2_ref[...], v2_ref[...] = _adamw(w_ref[...], g, m_ref[...], v_ref[...])

    full = pl.BlockSpec((tr, cc), lambda h, i, core_ref: (h * nb + i, 0))
    half = pl.BlockSpec((tr, cc), lambda h, i, core_ref: (i, 0))
    return pl.pallas_call(
        body, name=name, out_shape=[jax.ShapeDtypeStruct((2 * r, cc), F32)] * 4,
        grid_spec=pltpu.PrefetchScalarGridSpec(
            num_scalar_prefetch=1, grid=(2, nb), in_specs=[full, half, half, full, full], out_specs=[full] * 4),
        compiler_params=_cp("parallel", "parallel"),
    )(core, w, own, other, m, v)


def _place():
    mx, my, mc = lax.axis_index("x"), lax.axis_index("y"), lax.axis_index("c")
    chips = [(1 - mx, my), (mx, 1 - my), (1 - mx, 1 - my)]
    return mx, my, mc, chips


def all_gather_small(x, name):
    R, C = x.shape

    def body(x_ref, out_ref, send_sems, recv_sems, local_sem):
        mx, my, mc, _ = _place()
        me = 4 * mx + 2 * my + mc
        mine = pltpu.make_async_copy(x_ref, out_ref.at[me], local_sem)
        mine.start()

        def peer(k):
            px = 1 - mx if k & 4 else mx
            py = 1 - my if k & 2 else my
            pc = 1 - mc if k & 1 else mc
            return px, py, pc

        def copy(k, src, slot):
            return pltpu.make_async_remote_copy(src_ref=src, dst_ref=out_ref.at[slot], send_sem=send_sems.at[k - 1],
                                                recv_sem=recv_sems.at[k - 1], device_id=peer(k), device_id_type=MESH)

        sends = [copy(k, x_ref, me) for k in range(1, 8)]
        for cp in sends:
            cp.start()
        for k in range(1, 8):
            px, py, pc = peer(k)
            slot = 4 * px + 2 * py + pc
            copy(k, out_ref.at[slot], slot).wait_recv()
        for cp in sends:
            cp.wait_send()
        mine.wait()

    return pl.pallas_call(
        body, name=name, out_shape=jax.ShapeDtypeStruct((8, R, C), F32),
        in_specs=[pl.BlockSpec(memory_space=pltpu.VMEM)], out_specs=pl.BlockSpec(memory_space=pltpu.VMEM),
        scratch_shapes=[pltpu.SemaphoreType.DMA((7,)), pltpu.SemaphoreType.DMA((7,)), pltpu.SemaphoreType.DMA],
        compiler_params=_cp(),
    )(x)


def gather8_comm(x):
    def copies(x_ref, out_ref, send_sems, recv_sems):
        mx, my, mc, _ = _place()
        me = 4 * mx + 2 * my + mc

        def peer(k):
            return (1 - mx if k & 4 else mx, 1 - my if k & 2 else my, 1 - mc if k & 1 else mc)

        def copy(k, src, slot):
            return pltpu.make_async_remote_copy(src_ref=src, dst_ref=out_ref.at[slot], send_sem=send_sems.at[k - 1],
                                                recv_sem=recv_sems.at[k - 1], device_id=peer(k), device_id_type=MESH)

        sends = [copy(k, x_ref, me) for k in range(1, 8)]
        arrivals = []
        for k in range(1, 8):
            px, py, pc = peer(k)
            slot = 4 * px + 2 * py + pc
            arrivals.append(copy(k, out_ref.at[slot], slot))
        return sends, arrivals, pltpu.make_async_copy(x_ref, out_ref.at[me], send_sems.at[7])

    def start(cin, cout, send_sems, recv_sems):
        sends, _, mine = copies(cin[0], cout[0], send_sems, recv_sems)
        mine.start()
        for cp in sends:
            cp.start()

    def finish(cin, cout, send_sems, recv_sems):
        sends, arrivals, mine = copies(cin[0], cout[0], send_sems, recv_sems)
        for cp in arrivals:
            cp.wait_recv()
        for cp in sends:
            cp.wait_send()
        mine.wait()

    return _Comm([x], [jax.ShapeDtypeStruct((8,) + x.shape, F32)], {}, 8, start, finish)


def _join(a, b):
    na_in, na_out = len(a.operands), len(a.out_shape)

    def split(fn_a, fn_b):
        def both(cin, cout, send_sems, recv_sems):
            fn_a(cin[:na_in], cout[:na_out], send_sems.at[pl.ds(0, a.n_sems)], recv_sems.at[pl.ds(0, a.n_sems)])
            fn_b(cin[na_in:], cout[na_out:], send_sems.at[pl.ds(a.n_sems, b.n_sems)], recv_sems.at[pl.ds(a.n_sems, b.n_sems)])
        return both

    aliases = dict(a.aliases)
    aliases.update({na_in + i: na_out + o for i, o in b.aliases.items()})
    return _Comm(a.operands + b.operands, a.out_shape + b.out_shape, aliases, a.n_sems + b.n_sems, split(a.start, b.start), split(a.finish, b.finish))


def _region(ref, kind, j, half, r, cc):
    nr = r if half is None else r // 2
    off = 0 if half is None else half * nr
    if kind == "col":
        return ref.at[pl.ds(off, nr), pl.ds(pl.multiple_of(j * cc, 128), cc)]
    return ref.at[pl.ds(pl.multiple_of(j * r + off, 16), nr), :]


def gather_weights(fulls, kinds, dims):
    comm = gather_comm(fulls, kinds, dims)
    n = len(fulls)

    def body(*refs):
        comm.start(refs[:n], refs[n:2 * n], *refs[2 * n:])
        comm.finish(refs[:n], refs[n:2 * n], *refs[2 * n:])

    return pl.pallas_call(
        body, name="gather_weights", out_shape=comm.out_shape, in_specs=[ANY] * n, out_specs=[ANY] * n, input_output_aliases=comm.aliases,
        scratch_shapes=[pltpu.SemaphoreType.DMA((comm.n_sems,)), pltpu.SemaphoreType.DMA((comm.n_sems,))], compiler_params=_cp(),
    )(*fulls)


def gather_comm(fulls, kinds, dims):
    n = len(fulls)

    def copies(f_refs, send_sems, recv_sems):
        mx, my, mc, chips = _place()
        jme = 2 * mx + my

        def landed(w, k, half):
            px, py = chips[k]
            return _region(f_refs[w], kinds[w], 2 * px + py, half, *dims[w])

        def over_ici(w, k, reg):
            px, py = chips[k]
            return pltpu.make_async_remote_copy(src_ref=reg, dst_ref=reg, send_sem=send_sems.at[6 * w + k], recv_sem=recv_sems.at[6 * w + k],
                                                device_id=(px, py, mc), device_id_type=MESH)

        def over_d2d(w, k, half):
            reg = landed(w, k, half)
            return pltpu.make_async_remote_copy(src_ref=reg, dst_ref=reg, send_sem=send_sems.at[6 * w + 3 + k],
                                                recv_sem=recv_sems.at[6 * w + 3 + k], device_id=(mx, my, 1 - mc), device_id_type=MESH)

        sends = [over_ici(w, k, _region(f_refs[w], kinds[w], jme, mc, *dims[w])) for w in range(n) for k in range(3)]
        return mc, landed, over_ici, over_d2d, sends

    def start(cin, f_refs, send_sems, recv_sems):
        for cp in copies(f_refs, send_sems, recv_sems)[4]:
            cp.start()

    def finish(cin, f_refs, send_sems, recv_sems):
        mc, landed, over_ici, over_d2d, sends = copies(f_refs, send_sems, recv_sems)
        passed = []
        for w in range(n):
            for k in range(3):
                over_ici(w, k, landed(w, k, mc)).wait_recv()
                cp = over_d2d(w, k, mc)
                cp.start()
                passed.append(cp)
        for w in range(n):
            for k in range(3):
                over_d2d(w, k, 1 - mc).wait_recv()
        for cp in sends + passed:
            cp.wait_send()

    return _Comm(fulls, [jax.ShapeDtypeStruct(f.shape, BF16) for f in fulls], {w: w for w in range(n)}, 6 * n, start, finish)


def exchange_halves(grads, name):
    n = len(grads)

    def body(*refs):
        g_refs, l_refs = refs[:n], refs[n:2 * n]
        send_sems, recv_sems = refs[2 * n:]
        mx, my, mc, _ = _place()
        cps = [pltpu.make_async_remote_copy(src_ref=g_refs[w].at[:, pl.ds(1 - mc, 1)], dst_ref=l_refs[w], send_sem=send_sems.at[w],
                                            recv_sem=recv_sems.at[w], device_id=(mx, my, 1 - mc), device_id_type=MESH) for w in range(n)]
        for cp in cps:
            cp.start()
        for cp in cps:
            cp.wait()

    return pl.pallas_call(
        body, name=name, out_shape=[jax.ShapeDtypeStruct((g.shape[0], 1) + g.shape[2:], BF16) for g in grads],
        in_specs=[ANY] * n, out_specs=[ANY] * n,
        scratch_shapes=[pltpu.SemaphoreType.DMA((n,)), pltpu.SemaphoreType.DMA((n,))], compiler_params=_cp(),
    )(*grads)


def scatter_comm(parts):
    n = len(parts)

    def sends(p_refs, l_refs, send_sems, recv_sems):
        mx, my, mc, chips = _place()
        return [pltpu.make_async_remote_copy(src_ref=p_refs[w].at[2 * px + py], dst_ref=l_refs[w].at[2 * mx + my],
                                             send_sem=send_sems.at[3 * w + k], recv_sem=recv_sems.at[3 * w + k],
                                             device_id=(px, py, mc), device_id_type=MESH) for w in range(n) for k, (px, py) in enumerate(chips)]

    def start(p_refs, l_refs, send_sems, recv_sems):
        for cp in sends(p_refs, l_refs, send_sems, recv_sems):
            cp.start()

    def finish(p_refs, l_refs, send_sems, recv_sems):
        mx, my, mc, chips = _place()
        for w in range(n):
            for k, (px, py) in enumerate(chips):
                slot = l_refs[w].at[2 * px + py]
                pltpu.make_async_remote_copy(src_ref=slot, dst_ref=slot, send_sem=send_sems.at[3 * w + k], recv_sem=recv_sems.at[3 * w + k],
                                             device_id=(px, py, mc), device_id_type=MESH).wait_recv()
        for cp in sends(p_refs, l_refs, send_sems, recv_sems):
            cp.wait_send()

    return _Comm(parts, [jax.ShapeDtypeStruct(p.shape, BF16) for p in parts], {}, 3 * n, start, finish)


def share_comm(sums):
    n = len(sums)

    def copies(q_refs, o_refs, send_sems, recv_sems):
        mx, my, mc, _ = _place()
        return [pltpu.make_async_remote_copy(src_ref=q_refs[w], dst_ref=o_refs[w], send_sem=send_sems.at[w], recv_sem=recv_sems.at[w],
                                             device_id=(mx, my, 1 - mc), device_id_type=MESH) for w in range(n)]

    def start(*refs):
        for cp in copies(*refs):
            cp.start()

    def finish(*refs):
        for cp in copies(*refs):
            cp.wait()

    return _Comm(sums, [jax.ShapeDtypeStruct(q.shape, F32) for q in sums], {}, n, start, finish)


def _pack(arrays):
    flat = jnp.concatenate([a.reshape(-1) for a in arrays])
    rows = -(-flat.shape[0] // 1024) * 8
    return jnp.pad(flat, (0, rows * 128 - flat.shape[0])).reshape(rows, 128)


def _unpack(packed, shapes):
    flat, out, off = packed.reshape(-1), [], 0
    for s in shapes:
        n = math.prod(s)
        out.append(flat[off:off + n].reshape(s))
        off += n
    return out


def kernel(x, c, w_ada, b_ada, g_pre_mix, g_post_mix, g_pre_ffn, g_post_ffn, w_in, lb_logits, g_hgrn_norm, w_a_out, g_sgu_norm, w_spatial, b_spatial, w_b_out, w_o, w_ff1, w_ff2, loss_target, m_w_ada, m_b_ada, m_g_pre_mix, m_g_post_mix, m_g_pre_ffn, m_g_post_ffn, m_w_in, m_lb_logits, m_g_hgrn_norm, m_w_a_out, m_g_sgu_norm, m_w_spatial, m_b_spatial, m_w_b_out, m_w_o, m_w_ff1, m_w_ff2, v_w_ada, v_b_ada, v_g_pre_mix, v_g_post_mix, v_g_pre_ffn, v_g_post_ffn, v_w_in, v_lb_logits, v_g_hgrn_norm, v_w_a_out, v_g_sgu_norm, v_w_spatial, v_b_spatial, v_w_b_out, v_w_o, v_w_ff1, v_w_ff2):
    mx, my, mc = lax.axis_index("x"), lax.axis_index("y"), lax.axis_index("c")
    chip, me = 2 * mx + my, 4 * mx + 2 * my + mc
    D = D_MODEL
    h0, tgt = x[0], loss_target[0]
    n_ada = w_ada.shape[2]
    n_lb = lb_logits.shape[2]

    got = all_gather_small(_pack([c, lb_logits]), "gather_inputs")
    c_all = got[:, :D // 128, :].reshape(8, D)
    lb_full = got[0::2, D // 128:D // 128 + 4 * n_lb // 128, :].reshape(4, 2, 2, n_lb).transpose(1, 2, 0, 3).reshape(2, 2, 4 * n_lb)
    b_ada_chip = lax.dynamic_slice(b_ada, (0, chip * n_ada), (1, n_ada))
    mod_cols = mod_matmul(c_all, w_ada[0], b_ada_chip)
    got = all_gather_small(mod_cols.reshape(-1, 128), "gather_mod").reshape(4, 2, 8, n_ada)
    mod = lax.dynamic_index_in_dim(got[:, 0], me, axis=1, keepdims=False).reshape(6, 1, D)
    sh1, sc1, gt1, sh2, sc2, gt2 = (mod[i] for i in range(6))

    big = [("w_in", w_in, "col"), ("w_a_out", w_a_out, "col"), ("w_b_out", w_b_out, "col"), ("w_o", w_o, "row"),
           ("w_ff1", w_ff1, "col"), ("w_ff2", w_ff2, "row")]
    kinds = [k for _, _, k in big]
    chip_idx, core = chip.reshape(1).astype(jnp.int32), mc.reshape(1).astype(jnp.int32)
    fulls = [cast_into_full(w[0], kind, chip_idx, "cast_" + nm) for nm, w, kind in big]
    dims = [w.shape[1:] for _, w, _ in big]
    later = lambda lo, hi: gather_comm(fulls[lo:hi], kinds[lo:hi], dims[lo:hi])
    halves_summed = lambda grads, name: [add_halves(g, l, core) for g, l in zip(grads, exchange_halves(grads, name))]
    (w_in_f,) = gather_weights(fulls[:1], kinds[:1], dims[:1])

    bst = b_spatial[0].T
    (proj, a1), (w_a_f, w_b_f, w_o_f) = prenorm_matmul(h0, g_pre_mix, sc1, sh1, w_in_f, relu2=False, name="in_proj", comm=later(1, 4))
    o, (w_ff1_f,) = hgrn_fwd(proj, lb_full, comm=later(4, 5))
    ya_pre = hgrn_post_fwd(o, proj, g_hgrn_norm)
    sgu = sgu_fwd(proj, g_sgu_norm, w_spatial[0], bst)
    y_a, y_b, merged = merge_matmul(ya_pre, sgu, w_a_f, w_b_f, proj)
    mo, h1 = out_proj(merged, w_o_f, h0, gt1, g_post_mix)
    (f1, a2, hid), (w_ff2_f,) = prenorm_matmul(h1, g_pre_ffn, sc2, sh2, w_ff1_f, relu2=True, name="ff1", comm=later(5, 6))
    dy, dff, loss_parts, d_gt2, d_g_post_ffn = ff2_loss(hid, w_ff2_f, h1, tgt, gt2, g_post_ffn)
    loss = lax.psum(0.5 * loss_parts[0, 0] / D, ("x", "y", "c"))

    df1 = ff2_bwd(dff, w_ff2_f, f1)
    gr_ff2 = matmul(hid, dff, mode="tn", out_dtype=BF16, tm=1024, tn=1024, tk=2048, name="dw_ff2")
    da2 = matmul(df1, w_ff1_f, mode="nt", out_dtype=F32, tm=1024, tn=1024, tk=2048, name="da2")
    gr_ff1 = matmul(a2, df1, mode="tn", out_dtype=BF16, tm=1024, tn=2048, tk=1024, name="dw_ff1", split=(4, 2))
    parts_ff = halves_summed([gr_ff1, gr_ff2.reshape(4, 2, -1, D)], "exchange_ff")
    dh1, dmo, d_sh2, d_sc2, d_g_pre_ffn, d_gt1, d_g_post_mix = ffn_norm_bwd(dy, da2, h1, mo, g_pre_ffn, sc2, gt1, g_post_mix)
    dya, dyb, dga, dgb = out_proj_bwd(dmo, w_o_f, y_a, y_b, proj)
    gr_o = matmul(merged, dmo, mode="tn", out_dtype=BF16, tm=1024, tn=1024, tk=2048, name="dw_o")
    dsgu = matmul(dyb, w_b_f, mode="nt", out_dtype=F32, tm=512, tn=1024, tk=2048, name="dsgu")
    gr_b = matmul(sgu, dyb, mode="tn", out_dtype=BF16, tm=512, tn=512, tk=4096, name="dw_b_out", split=(4, 2))
    dz, d_w_spatial, d_b_spatial, d_g_sgu = sgu_bwd(proj, dsgu, g_sgu_norm, w_spatial[0], bst)
    dya_pre = matmul(dya, w_a_f, mode="nt", out_dtype=F32, tm=512, tn=1024, tk=2048, name="dya_pre")
    gr_a = matmul(ya_pre, dya, mode="tn", out_dtype=BF16, tm=512, tn=512, tk=4096, name="dw_a_out", split=(4, 2))
    parts_mix = halves_summed([gr_a, gr_b, gr_o.reshape(4, 2, -1, D)], "exchange_mix")
    do, dog, d_g_hgrn = hgrn_post_bwd(dya_pre, o, proj, g_hgrn_norm)
    (dq, dv, dlg, d_lb), landed_ff = hgrn_bwd(proj, do, lb_full, comm=scatter_comm(parts_ff))
    dproj = jnp.concatenate([dq, dlg, dv, dog, dz, dga, dgb], axis=1)
    early = _pack([d_g_sgu, d_w_spatial, d_b_spatial[:, 0, :]])
    gr_in, (*landed_mix, got_early) = matmul(a1, dproj, mode="tn", out_dtype=BF16, tm=1024, tn=2816, tk=1024, name="dw_in", split=(4, 2),
                                             comm=_join(scatter_comm(parts_mix), gather8_comm(early)))
    parts_in = halves_summed([gr_in], "exchange_in")
    da1, landed_in = matmul(dproj, w_in_f, mode="nt", out_dtype=F32, tm=1024, tn=1024, tk=2816, name="da1", comm=scatter_comm(parts_in))
    grad_x, d_sh1, d_sc1, d_g_pre_mix = mix_norm_bwd(da1, h0, dh1, g_pre_mix, sc1)

    parts = parts_in + parts_mix + parts_ff
    own = [sum_chips(p, l, chip_idx) for p, l in zip(parts, list(landed_in) + list(landed_mix) + list(landed_ff))]
    out = {}

    mine = _pack([d_sh1, d_sc1, d_gt1, d_sh2, d_sc2, d_gt2, d_g_pre_mix, d_g_post_mix, d_g_pre_ffn, d_g_post_ffn, d_g_hgrn, d_lb])
    got = all_gather_small(mine, "gather_small_grads")
    g_b_ada, g_g1, g_g2, g_g3, g_g4, g_hg, g_lb = _unpack(
        sum_devices(got, "sum_small_grads"), [(1, 6 * D), (1, D), (1, D), (1, D), (1, D), (1, HEAD_DIM), (2, 1024)])
    g_sg, g_ws, g_bs = _unpack(sum_devices(got_early, "sum_sgu_grads"), [(1, 1024), w_spatial.shape, b_spatial.shape])
    g_lbl = lax.dynamic_slice(lb_logits_grad(g_lb, lb_full), (0, 0, chip * n_lb), (2, 2, n_lb))
    names = ["b_ada", "g_pre_mix", "g_post_mix", "g_pre_ffn", "g_post_ffn", "g_hgrn_norm", "g_sgu_norm", "w_spatial", "b_spatial", "lb_logits"]
    ws = [b_ada, g_pre_mix, g_post_mix, g_pre_ffn, g_post_ffn, g_hgrn_norm, g_sgu_norm, w_spatial, b_spatial, lb_logits]
    gs = [g_b_ada, g_g1, g_g2, g_g3, g_g4, g_hg, g_sg, g_ws, g_bs, g_lbl]
    ms = [m_b_ada, m_g_pre_mix, m_g_post_mix, m_g_pre_ffn, m_g_post_ffn, m_g_hgrn_norm, m_g_sgu_norm, m_w_spatial, m_b_spatial, m_lb_logits]
    vs = [v_b_ada, v_g_pre_mix, v_g_post_mix, v_g_pre_ffn, v_g_post_ffn, v_g_hgrn_norm, v_g_sgu_norm, v_w_spatial, v_b_spatial, v_lb_logits]
    shapes = [w.shape for w in ws]
    upd = adamw(_pack(ws), _pack(gs), _pack(ms), _pack(vs), "adamw_small")
    upd = [_unpack(u, shapes) for u in upd]
    for i, nm in enumerate(names):
        out[nm] = (gs[i], upd[0][i], upd[1][i], upd[2][i])

    dmod_all = got[:, :6 * D // 128, :].reshape(8, 6 * D)
    dmod_chip = lax.dynamic_slice(dmod_all, (0, chip * n_ada), (8, n_ada))
    upd_ada, other = wada_update(c_all, dmod_chip, w_ada[0], m_w_ada[0], v_w_ada[0], share_comm(own))
    out["w_ada"] = tuple(a[None] for a in upd_ada)
    for (nm, w, _), a, b, m, v in zip(big, own, other, (m_w_in, m_w_a_out, m_w_b_out, m_w_o, m_w_ff1, m_w_ff2),
                                      (v_w_in, v_w_a_out, v_w_b_out, v_w_o, v_w_ff1, v_w_ff2)):
        out[nm] = tuple(t[None] for t in adamw_halves(w[0], a, b, m[0], v[0], core, "adamw_" + nm))

    order = ["w_ada", "b_ada", "g_pre_mix", "g_post_mix", "g_pre_ffn", "g_post_ffn", "w_in", "lb_logits", "g_hgrn_norm", "w_a_out",
             "g_sgu_norm", "w_spatial", "b_spatial", "w_b_out", "w_o", "w_ff1", "w_ff2"]
    return (loss, grad_x[None], *[out[nm][0] for nm in order], *[out[nm][1] for nm in order], *[out[nm][2] for nm in order],
            *[out[nm][3] for nm in order])
```

```python
import functools
import math

import jax
import jax.numpy as jnp
from jax import lax
from jax.experimental import pallas as pl
from jax.experimental.pallas import tpu as pltpu

F32, BF16 = jnp.float32, jnp.bfloat16
HI = lax.Precision.HIGHEST
MESH = pl.DeviceIdType.MESH
ANY = pl.BlockSpec(memory_space=pl.ANY)

EPS = 1e-6
D_MODEL = 2048
N_HEADS = 8
HEAD_DIM = 128
HGRN_CHUNK = 32
HGRN_BLOCK = 256
ROW_CHUNK = 16
SGU_CHUNK = 128
SGU_GROUPS = 8
Q_SCALE = HEAD_DIM ** -0.5
COL_Q, COL_FFW, COL_FBW, COL_V, COL_OG, COL_U, COL_ZV, COL_GA, COL_GB = 0, 1, 2, 3, 4, 5, 6, 7, 9
N_PROJ = 11264
VMEM_BYTES_V7X = 64 * 1024 * 1024
VMEM_LIMIT = VMEM_BYTES_V7X - 8 * 1024 * 1024

ADAM_LR, ADAM_B1, ADAM_B2, ADAM_EPS, ADAM_WD, ADAM_STEP = 0.001, 0.9, 0.999, 1e-08, 0.01, 10
ADAM_C1 = 1.0 - ADAM_B1 ** ADAM_STEP
ADAM_C2 = 1.0 - ADAM_B2 ** ADAM_STEP


def _cp(*sem):
    return pltpu.CompilerParams(dimension_semantics=sem if sem else None, vmem_limit_bytes=VMEM_LIMIT)


def _vec(d):
    return pl.BlockSpec((1, d), lambda *_: (0, 0))


def _colsum(x):
    return jnp.sum(x, axis=0, keepdims=True)


def _nt(a, b):
    return lax.dot_general(a, b, (((1,), (1,)), ((), ())), preferred_element_type=F32)


def _tn(a, b):
    return lax.dot_general(a, b, (((0,), (0,)), ((), ())), preferred_element_type=F32)


def _nn(a, b):
    return jnp.dot(a, b, preferred_element_type=F32)


def _adamw(w, g, m, v):
    m2 = ADAM_B1 * m + (1.0 - ADAM_B1) * g
    v2 = ADAM_B2 * v + (1.0 - ADAM_B2) * (g * g)
    delta = -ADAM_LR * ((m2 / ADAM_C1) / (jnp.sqrt(v2 / ADAM_C2) + ADAM_EPS) + ADAM_WD * w)
    return delta, m2, v2


class _Comm:
    def __init__(self, operands, out_shape, aliases, n_sems, start, finish):
        self.operands, self.out_shape, self.aliases, self.n_sems = list(operands), list(out_shape), dict(aliases), n_sems
        self.start, self.finish = start, finish


def _pallas(body, *, name, grid, in_specs, out_specs, out_shape, scratch, semantics, operands, comm=None):
    if comm is None:
        res = pl.pallas_call(body, name=name, grid=grid, in_specs=in_specs, out_specs=out_specs, out_shape=out_shape,
                             scratch_shapes=scratch, compiler_params=_cp(*semantics))(*operands)
        return res, []
    n_in, n_out, n_scr = len(in_specs), len(out_specs), len(scratch)
    nci, nco = len(comm.operands), len(comm.out_shape)

    def with_comm(*refs):
        ins, rest = refs[:n_in], refs[n_in:]
        cin, rest = rest[:nci], rest[nci:]
        outs, rest = rest[:n_out], rest[n_out:]
        cout, rest = rest[:nco], rest[nco:]
        scr, (send, recv) = rest[:n_scr], rest[n_scr:]
        ids = [pl.program_id(a) for a in range(len(grid))]
        first = functools.reduce(jnp.logical_and, [i == 0 for i in ids])
        last = functools.reduce(jnp.logical_and, [i == g - 1 for i, g in zip(ids, grid)])

        @pl.when(first)
        def _():
            comm.start(cin, cout, send, recv)

        body(*ins, *outs, *scr)

        @pl.when(last)
        def _():
            comm.finish(cin, cout, send, recv)

    res = pl.pallas_call(
        with_comm, name=name, grid=grid, in_specs=list(in_specs) + [ANY] * nci, out_specs=list(out_specs) + [ANY] * nco,
        out_shape=list(out_shape) + comm.out_shape, input_output_aliases={n_in + i: n_out + o for i, o in comm.aliases.items()},
        scratch_shapes=list(scratch) + [pltpu.SemaphoreType.DMA((comm.n_sems,)), pltpu.SemaphoreType.DMA((comm.n_sems,))],
        compiler_params=_cp(*["arbitrary"] * len(grid)),
    )(*operands, *comm.operands)
    return res[:n_out], res[n_out:]


def matmul(a, b, *, mode, out_dtype, tm, tn, tk, name, split=None, comm=None):
    if mode == "tn":
        (K, M), (_, N) = a.shape, b.shape
    elif mode == "nt":
        (M, K), (N, _) = a.shape, b.shape
    else:
        (M, K), (_, N) = a.shape, b.shape
    tm, tn, tk = min(tm, M), min(tn, N), min(tk, K)
    nk = K // tk
    a_spec = pl.BlockSpec((tk, tm), lambda i, j, k: (k, i)) if mode == "tn" else pl.BlockSpec((tm, tk), lambda i, j, k: (i, k))
    b_spec = pl.BlockSpec((tn, tk), lambda i, j, k: (j, k)) if mode == "nt" else pl.BlockSpec((tk, tn), lambda i, j, k: (k, j))
    dot = {"nn": _nn, "nt": _nt, "tn": _tn}[mode]
    if split is None:
        out_shape = jax.ShapeDtypeStruct((M, N), out_dtype)
        out_spec = pl.BlockSpec((tm, tn), lambda i, j, k: (i, j))
    else:
        nj, nh = split
        rows, cols = M // nh, N // nj
        tm, tn = min(tm, rows), min(tn, cols)
        bi, bj = rows // tm, cols // tn
        out_shape = jax.ShapeDtypeStruct((nj, nh, rows, cols), out_dtype)
        out_spec = pl.BlockSpec((None, None, tm, tn), lambda i, j, k: (j // bj, i // bi, i % bi, j % bj))

    def body(a_ref, b_ref, o_ref, acc_ref):
        k = pl.program_id(2)

        @pl.when(k == 0)
        def _():
            acc_ref[...] = jnp.zeros_like(acc_ref)

        acc_ref[...] += dot(a_ref[...], b_ref[...])

        @pl.when(k == nk - 1)
        def _():
            o_ref[...] = acc_ref[...].astype(o_ref.dtype)

    (out,), landed = _pallas(
        body, name=name, grid=(M // tm, N // tn, nk), in_specs=[a_spec, b_spec], out_specs=[out_spec], out_shape=[out_shape],
        scratch=[pltpu.VMEM((tm, tn), F32)], semantics=("parallel", "parallel", "arbitrary"), operands=(a, b), comm=comm)
    return out if comm is None else (out, landed)


def cast_into_full(w, kind, chip, name):
    r, cc = w.shape
    tr = min(r, 512)
    nb = r // tr

    def body(chip_ref, w_ref, o_ref):
        o_ref[...] = w_ref[...].astype(BF16)

    if kind == "col":
        full, out_map = (r, 4 * cc), lambda i, chip_ref: (i, chip_ref[0])
    else:
        full, out_map = (4 * r, cc), lambda i, chip_ref: (chip_ref[0] * nb + i, 0)
    return pl.pallas_call(
        body, name=name, out_shape=jax.ShapeDtypeStruct(full, BF16),
        grid_spec=pltpu.PrefetchScalarGridSpec(
            num_scalar_prefetch=1, grid=(nb,), in_specs=[pl.BlockSpec((tr, cc), lambda i, chip_ref: (i, 0))],
            out_specs=pl.BlockSpec((tr, cc), out_map)),
        compiler_params=_cp("parallel"),
    )(chip, w)


def mod_matmul(c_all, w_ada, b_ada):
    D, N = w_ada.shape
    tn = 1024

    def body(c_ref, w_ref, b_ref, o_ref):
        c = c_ref[...]
        sc = c * jax.nn.sigmoid(c)
        o_ref[...] = jnp.dot(sc, w_ref[...], precision=HI, preferred_element_type=F32) + b_ref[...]

    return pl.pallas_call(
        body, name="mod_matmul", out_shape=jax.ShapeDtypeStruct((8, N), F32), grid=(N // tn,),
        in_specs=[pl.BlockSpec((8, D), lambda j: (0, 0)), pl.BlockSpec((D, tn), lambda j: (0, j)),
                  pl.BlockSpec((1, tn), lambda j: (0, j))],
        out_specs=pl.BlockSpec((8, tn), lambda j: (0, j)), compiler_params=_cp("parallel"),
    )(c_all, w_ada, b_ada)


def prenorm_matmul(h, g, sc, sh, w, *, relu2, name, comm=None):
    T, D = h.shape
    N = w.shape[1]
    tm, tn = min(512, T), 2048 if N % 2048 == 0 else 1024

    def body(h_ref, g_ref, sc_ref, sh_ref, w_ref, y_ref, a_ref, *hid_ref):
        @pl.when(pl.program_id(1) == 0)
        def _():
            gain, scale, shift = g_ref[...], 1.0 + sc_ref[...], sh_ref[...]

            def rows(c, carry):
                r = pl.ds(pl.multiple_of(c * ROW_CHUNK, ROW_CHUNK), ROW_CHUNK)
                x = h_ref[r, :]
                a_ref[r, :] = ((x * lax.rsqrt(jnp.mean(x * x, axis=-1, keepdims=True) + EPS)) * gain * scale + shift).astype(BF16)
                return carry

            lax.fori_loop(0, tm // ROW_CHUNK, rows, 0)

        y = _nn(a_ref[...], w_ref[...])
        y_ref[...] = y
        if relu2:
            p = jnp.maximum(y, 0.0)
            hid_ref[0][...] = (p * p).astype(BF16)

    out_shape = [jax.ShapeDtypeStruct((T, N), F32), jax.ShapeDtypeStruct((T, D), BF16)]
    out_specs = [pl.BlockSpec((tm, tn), lambda i, j: (i, j)), pl.BlockSpec((tm, D), lambda i, j: (i, 0))]
    if relu2:
        out_shape.append(jax.ShapeDtypeStruct((T, N), BF16))
        out_specs.append(pl.BlockSpec((tm, tn), lambda i, j: (i, j)))
    outs, landed = _pallas(
        body, name=name, grid=(T // tm, N // tn),
        in_specs=[pl.BlockSpec((tm, D), lambda i, j: (i, 0)), _vec(D), _vec(D), _vec(D), pl.BlockSpec((D, tn), lambda i, j: (0, j))],
        out_specs=out_specs, out_shape=out_shape, scratch=[], semantics=("parallel", "arbitrary"), operands=(h, g, sc, sh, w), comm=comm)
    return outs if comm is None else (outs, landed)


def _hgrn_lower_bound(l_ref):
    l0, l1 = l_ref[0:1, :], l_ref[1:2, :]
    m = jnp.maximum(l0, l1)
    e0, e1 = jnp.exp(l0 - m), jnp.exp(l1 - m)
    return e0 / (e0 + e1)


def _hgrn_chunk_masks(d):
    r = lax.broadcasted_iota(jnp.int32, (HGRN_BLOCK, HGRN_BLOCK), 0)
    c = lax.broadcasted_iota(jnp.int32, (HGRN_BLOCK, HGRN_BLOCK), 1)
    same = (r // HGRN_CHUNK) == (c // HGRN_CHUNK)
    fwd = d == 0
    tri = same & (((c <= r) & fwd) | ((c >= r) & jnp.logical_not(fwd)))
    tri_t = same & (((c >= r) & fwd) | ((c <= r) & jnp.logical_not(fwd)))
    return tri, jnp.where(tri, 1.0, 0.0).astype(BF16), jnp.where(tri_t, 1.0, 0.0).astype(BF16)


def _tri_sum(tri, x, terms):
    pieces, rest = [], x
    for t in range(terms):
        p = rest.astype(BF16)
        pieces.append(p)
        if t + 1 < terms:
            rest = rest - p.astype(F32)
    y = _nn(tri, jnp.concatenate(pieces, axis=1))
    w = x.shape[1]
    return sum(y[:, t * w:(t + 1) * w] for t in range(terms))


def _chunk_total(x):
    x3 = x.reshape(HGRN_BLOCK // HGRN_CHUNK, HGRN_CHUNK, x.shape[1])
    return jnp.broadcast_to(jnp.sum(x3, axis=1, keepdims=True), x3.shape).reshape(x.shape)


def _block_loop(T, body, init):
    n = T // HGRN_BLOCK
    return lax.fori_loop(0, n, body, init, unroll=2 if n % 2 == 0 else 1)


def _hgrn_gate(f, lb):
    s = jax.nn.sigmoid(f)
    sn = jax.nn.sigmoid(-f)
    fg = lb + (1.0 - lb) * s
    return s, sn, fg, jnp.log(fg), (1.0 - lb) * sn


def _hgrn_specs(T):
    col = lambda base: pl.BlockSpec((T, HEAD_DIM), lambda h, d: (0, base * N_HEADS + h))
    f_spec = pl.BlockSpec((T, HEAD_DIM), lambda h, d: (0, COL_FFW * N_HEADS + N_HEADS * d + h))
    l_spec = pl.BlockSpec((None, 2, HEAD_DIM), lambda h, d: (d, 0, h))
    return col, f_spec, l_spec


def hgrn_fwd(proj, lb_logits, comm=None):
    T = proj.shape[0]
    NC, CPB = T // HGRN_CHUNK, HGRN_BLOCK // HGRN_CHUNK
    col, f_spec, l_spec = _hgrn_specs(T)

    def body(l_ref, q_ref, f_ref, v_ref, o_ref, st_ref, dec_ref, qd_ref):
        d = pl.program_id(1)
        lb = _hgrn_lower_bound(l_ref)
        mask, mtri, _ = _hgrn_chunk_masks(d)

        def block(i, carry):
            rows = pl.ds(pl.multiple_of(i * HGRN_BLOCK, HGRN_BLOCK), HGRN_BLOCK)
            _, _, _, lf, k = _hgrn_gate(f_ref[rows, :], lb)
            b = _tri_sum(mtri, lf, 3)
            bl = _chunk_total(lf)
            qd = (q_ref[rows, :] * Q_SCALE * jnp.exp(b)).astype(BF16)
            kd = (k * jnp.exp(-b)).astype(BF16)
            ke = (k * jnp.exp(bl - b)).astype(BF16)
            vb = v_ref[rows, :].astype(BF16)
            att = jnp.where(mask, _nt(qd, kd), 0.0).astype(BF16)
            o_ref[rows, :] = jnp.where(d == 0, 0.0, o_ref[rows, :]) + _nn(att, vb)
            qd_ref[rows, :] = qd
            dec = jnp.exp(bl)
            for cc in range(CPB):
                sl = slice(cc * HGRN_CHUNK, (cc + 1) * HGRN_CHUNK)
                n = i * CPB + cc
                st_ref[n] = _tn(vb[sl], ke[sl])
                dec_ref[n] = dec[cc * HGRN_CHUNK:cc * HGRN_CHUNK + 8, :]
            return carry

        _block_loop(T, block, 0)

        def scan(t, s):
            n = jnp.where(d == 0, t, NC - 1 - t)
            u = st_ref[n]
            st_ref[n] = s
            return dec_ref[n][0:1, :] * s + u

        lax.fori_loop(0, NC, scan, jnp.zeros((HEAD_DIM, HEAD_DIM), F32))

        def inter(i, carry):
            rows = pl.ds(pl.multiple_of(i * HGRN_BLOCK, HGRN_BLOCK), HGRN_BLOCK)
            qd = qd_ref[rows, :]
            o_ref[rows, :] += jnp.concatenate(
                [_nt(qd[cc * HGRN_CHUNK:(cc + 1) * HGRN_CHUNK], st_ref[i * CPB + cc].astype(BF16)) for cc in range(CPB)], axis=0)
            return carry

        _block_loop(T, inter, 0)

    (o,), landed = _pallas(
        body, name="hgrn_fwd", grid=(N_HEADS, 2), in_specs=[l_spec, col(COL_Q), f_spec, col(COL_V)],
        out_specs=[pl.BlockSpec((T, HEAD_DIM), lambda h, d: (0, h))], out_shape=[jax.ShapeDtypeStruct((T, N_HEADS * HEAD_DIM), F32)],
        scratch=[pltpu.VMEM((NC, HEAD_DIM, HEAD_DIM), F32), pltpu.VMEM((NC, 8, HEAD_DIM), F32), pltpu.VMEM((T, HEAD_DIM), BF16)],
        semantics=("parallel", "arbitrary"), operands=(lb_logits, proj, proj, proj), comm=comm)
    return o if comm is None else (o, landed)


def hgrn_post_fwd(o, proj, g_norm):
    T, W = o.shape
    tm = min(256, T)

    def body(o_ref, og_ref, g_ref, y_ref):
        g = g_ref[...]
        for h in range(N_HEADS):
            sl = slice(h * HEAD_DIM, (h + 1) * HEAD_DIM)
            x = o_ref[:, sl]
            r = lax.rsqrt(jnp.mean(x * x, axis=-1, keepdims=True) + EPS)
            og = og_ref[:, sl]
            y_ref[:, sl] = ((x * r) * g * (og * jax.nn.sigmoid(og))).astype(BF16)

    return pl.pallas_call(
        body, name="hgrn_post_fwd", out_shape=jax.ShapeDtypeStruct((T, W), BF16), grid=(T // tm,),
        in_specs=[pl.BlockSpec((tm, W), lambda i: (i, 0)), pl.BlockSpec((tm, W), lambda i: (i, COL_OG)), _vec(HEAD_DIM)],
        out_specs=pl.BlockSpec((tm, W), lambda i: (i, 0)), compiler_params=_cp("parallel"),
    )(o, proj, g_norm)


def _gelu(x):
    return 0.5 * x * (1.0 + lax.erf(x * (1.0 / math.sqrt(2.0))))


def _gelu_grad(x):
    return 0.5 * (1.0 + lax.erf(x * (1.0 / math.sqrt(2.0)))) + x * jnp.exp(-0.5 * x * x) * (1.0 / math.sqrt(2.0 * math.pi))


def _sgu_mix(u_ref, v_ref, g_ref, ws_ref, bst_ref):
    W = u_ref.shape[1]
    zu, zv = _gelu(u_ref[...]), _gelu(v_ref[...])
    dv = zv - jnp.mean(zv, axis=-1, keepdims=True)
    rstd = lax.rsqrt(jnp.mean(dv * dv, axis=-1, keepdims=True) + EPS)
    dhat = dv * rstd
    vn = (dhat * g_ref[...]).astype(BF16)
    gw = W // SGU_GROUPS
    vm = [_nn(ws_ref[g].astype(BF16), vn[:, g * gw:(g + 1) * gw]) + bst_ref[:, g:g + 1] for g in range(SGU_GROUPS)]
    return zu, rstd, dhat, vn, jnp.concatenate(vm, axis=1)


def sgu_fwd(proj, g_norm, w_spatial, b_spatial_t):
    T = proj.shape[0]
    W = 1024
    n_chunks = T // SGU_CHUNK

    def body(u_ref, v_ref, g_ref, ws_ref, bst_ref, y_ref):
        zu, _, _, _, vm = _sgu_mix(u_ref, v_ref, g_ref, ws_ref, bst_ref)
        y_ref[...] = (zu * vm).astype(BF16)

    blk = lambda cb: pl.BlockSpec((SGU_CHUNK, W), lambda i: (i, cb))
    return pl.pallas_call(
        body, name="sgu_fwd", out_shape=jax.ShapeDtypeStruct((T, W), BF16), grid=(n_chunks,),
        in_specs=[blk(COL_U), blk(COL_ZV), _vec(W), pl.BlockSpec((SGU_GROUPS, SGU_CHUNK, SGU_CHUNK), lambda i: (0, 0, 0)),
                  pl.BlockSpec((SGU_CHUNK, SGU_GROUPS), lambda i: (0, 0))],
        out_specs=blk(0), compiler_params=_cp("parallel"),
    )(proj, proj, g_norm, w_spatial, b_spatial_t)


def merge_matmul(ya_pre, sgu, w_a, w_b, proj):
    T, K = ya_pre.shape
    N = w_a.shape[1]
    tm, tn = min(512, T), 512
    gpb = 1024 // tn

    def body(a_ref, b_ref, wa_ref, wb_ref, ga_ref, gb_ref, ya_ref, yb_ref, m_ref):
        ya = _nn(a_ref[...], wa_ref[...])
        yb = _nn(b_ref[...], wb_ref[...])
        ya_ref[...] = ya
        yb_ref[...] = yb
        m_ref[...] = (jax.nn.sigmoid(ga_ref[...]) * ya + jax.nn.sigmoid(gb_ref[...]) * yb).astype(BF16)

    lhs = pl.BlockSpec((tm, K), lambda i, j: (i, 0))
    rhs = pl.BlockSpec((K, tn), lambda i, j: (0, j))
    out = pl.BlockSpec((tm, tn), lambda i, j: (i, j))
    return pl.pallas_call(
        body, name="merge_matmul", grid=(T // tm, N // tn),
        out_shape=[jax.ShapeDtypeStruct((T, N), F32), jax.ShapeDtypeStruct((T, N), F32), jax.ShapeDtypeStruct((T, N), BF16)],
        in_specs=[lhs, lhs, rhs, rhs, pl.BlockSpec((tm, tn), lambda i, j: (i, COL_GA * gpb + j)),
                  pl.BlockSpec((tm, tn), lambda i, j: (i, COL_GB * gpb + j))],
        out_specs=[out, out, out], compiler_params=_cp("parallel", "parallel"),
    )(ya_pre, sgu, w_a, w_b, proj, proj)


def out_proj(merged, w_o, h0, gt1, g_post):
    T, D = h0.shape
    tm = min(256, T)

    def body(m_ref, w_ref, h_ref, gt_ref, gp_ref, mo_ref, h1_ref):
        mo = _nn(m_ref[...], w_ref[...])
        mo_ref[...] = mo
        r = lax.rsqrt(jnp.mean(mo * mo, axis=-1, keepdims=True) + EPS)
        h1_ref[...] = h_ref[...] + gt_ref[...] * ((mo * r) * gp_ref[...])

    row = pl.BlockSpec((tm, D), lambda i: (i, 0))
    return pl.pallas_call(
        body, name="out_proj", grid=(T // tm,),
        out_shape=[jax.ShapeDtypeStruct((T, D), F32), jax.ShapeDtypeStruct((T, D), F32)],
        in_specs=[row, pl.BlockSpec((D, D), lambda i: (0, 0)), row, _vec(D), _vec(D)],
        out_specs=[row, row], compiler_params=_cp("parallel"),
    )(merged, w_o, h0, gt1, g_post)


def ff2_loss(hid, w_ff2, h1, tgt, gt2, g_post):
    T, K = hid.shape
    D = w_ff2.shape[1]
    tm, tk = min(256, T), 2048
    nk = K // tk

    def body(a_ref, w_ref, h_ref, t_ref, gt_ref, g_ref, dy_ref, dff_ref, loss_ref, dgt_ref, dg_ref, acc_ref, s_loss, s_gt, s_g):
        i, k = pl.program_id(0), pl.program_id(1)

        @pl.when(k == 0)
        def _():
            acc_ref[...] = jnp.zeros_like(acc_ref)

        @pl.when((k == 0) & (i == 0))
        def _():
            for s in (s_loss, s_gt, s_g):
                s[...] = jnp.zeros_like(s)

        acc_ref[...] += _nn(a_ref[...], w_ref[...])

        @pl.when(k == nk - 1)
        def _():
            gt, g = gt_ref[...], g_ref[...]
            fold = lambda x: x[0:8] + x[8:16]

            def rows(c, carry):
                rr = pl.ds(pl.multiple_of(c * ROW_CHUNK, ROW_CHUNK), ROW_CHUNK)
                ff = acc_ref[rr, :]
                r = lax.rsqrt(jnp.mean(ff * ff, axis=-1, keepdims=True) + EPS)
                fhat = ff * r
                nf = fhat * g
                err = (h_ref[rr, :] + gt * nf) - t_ref[rr, :]
                s_loss[...] += fold(err * err)
                dy = err * (1.0 / D)
                dy_ref[rr, :] = dy
                s_gt[...] += fold(dy * nf)
                dnf = dy * gt
                s_g[...] += fold(dnf * fhat)
                u = dnf * g
                dff_ref[rr, :] = (r * (u - fhat * jnp.mean(u * fhat, axis=-1, keepdims=True))).astype(BF16)
                return carry

            lax.fori_loop(0, tm // ROW_CHUNK, rows, 0)

        @pl.when((k == nk - 1) & (i == T // tm - 1))
        def _():
            loss_ref[...] = jnp.broadcast_to(jnp.sum(s_loss[...]), loss_ref.shape)
            dgt_ref[...] = _colsum(s_gt[...])
            dg_ref[...] = _colsum(s_g[...])

    row = pl.BlockSpec((tm, D), lambda i, k: (i, 0))
    vec = pl.BlockSpec((1, D), lambda i, k: (0, 0))
    part = pltpu.VMEM((8, D), F32)
    return pl.pallas_call(
        body, name="ff2_loss", grid=(T // tm, nk),
        out_shape=[jax.ShapeDtypeStruct((T, D), F32), jax.ShapeDtypeStruct((T, D), BF16), jax.ShapeDtypeStruct((8, 128), F32),
                   jax.ShapeDtypeStruct((1, D), F32), jax.ShapeDtypeStruct((1, D), F32)],
        in_specs=[pl.BlockSpec((tm, tk), lambda i, k: (i, k)), pl.BlockSpec((tk, D), lambda i, k: (k, 0)), row, row, vec, vec],
        out_specs=[row, row, pl.BlockSpec((8, 128), lambda i, k: (0, 0)), vec, vec],
        scratch_shapes=[pltpu.VMEM((tm, D), F32), part, part, part], compiler_params=_cp("arbitrary", "arbitrary"),
    )(hid, w_ff2, h1, tgt, gt2, g_post)


def ff2_bwd(dff, w_ff2, f1):
    T, D = dff.shape
    K = w_ff2.shape[0]
    tm, tn = min(512, T), 2048

    def body(a_ref, w_ref, f_ref, o_ref):
        o_ref[...] = (_nt(a_ref[...], w_ref[...]) * (2.0 * jnp.maximum(f_ref[...], 0.0))).astype(BF16)

    return pl.pallas_call(
        body, name="ff2_bwd", out_shape=jax.ShapeDtypeStruct((T, K), BF16), grid=(K // tn, T // tm),
        in_specs=[pl.BlockSpec((tm, D), lambda j, i: (i, 0)), pl.BlockSpec((tn, D), lambda j, i: (j, 0)),
                  pl.BlockSpec((tm, tn), lambda j, i: (i, j))],
        out_specs=pl.BlockSpec((tm, tn), lambda j, i: (i, j)), compiler_params=_cp("parallel", "parallel"),
    )(dff, w_ff2, f1)


def ffn_norm_bwd(dy, da2, h1, mo, g_pre2, sc2, gt1, g_post):
    T, D = dy.shape
    tm = min(256, T)

    def body(dy_ref, da_ref, h_ref, mo_ref, g2_ref, sc_ref, gt_ref, gp_ref, dh_ref, dmo_ref, s_sh, s_sc, s_g2, s_gt, s_gp):
        @pl.when(pl.program_id(0) == 0)
        def _():
            for s in (s_sh, s_sc, s_g2, s_gt, s_gp):
                s[...] = jnp.zeros_like(s)

        h1, da = h_ref[...], da_ref[...]
        g2, sc = g2_ref[...], sc_ref[...]
        r2 = lax.rsqrt(jnp.mean(h1 * h1, axis=-1, keepdims=True) + EPS)
        n2 = h1 * r2
        s_sh[...] += _colsum(da)
        s_sc[...] += _colsum(da * (n2 * g2))
        s_g2[...] += _colsum(da * (1.0 + sc) * n2)
        dn2 = da * g2 * (1.0 + sc)
        dh1 = dy_ref[...] + r2 * (dn2 - n2 * jnp.mean(dn2 * n2, axis=-1, keepdims=True))
        dh_ref[...] = dh1
        mo = mo_ref[...]
        gt, gp = gt_ref[...], gp_ref[...]
        r = lax.rsqrt(jnp.mean(mo * mo, axis=-1, keepdims=True) + EPS)
        mhat = mo * r
        s_gt[...] += _colsum(dh1 * (mhat * gp))
        dnm = dh1 * gt
        s_gp[...] += _colsum(dnm * mhat)
        u = dnm * gp
        dmo_ref[...] = (r * (u - mhat * jnp.mean(u * mhat, axis=-1, keepdims=True))).astype(BF16)

    row = pl.BlockSpec((tm, D), lambda i: (i, 0))
    vec_out = jax.ShapeDtypeStruct((1, D), F32)
    return pl.pallas_call(
        body, name="ffn_norm_bwd", grid=(T // tm,),
        out_shape=[jax.ShapeDtypeStruct((T, D), F32), jax.ShapeDtypeStruct((T, D), BF16)] + [vec_out] * 5,
        in_specs=[row, row, row, row] + [_vec(D)] * 4, out_specs=[row, row] + [_vec(D)] * 5,
        compiler_params=_cp("arbitrary"),
    )(dy, da2, h1, mo, g_pre2, sc2, gt1, g_post)


def out_proj_bwd(dmo, w_o, y_a, y_b, proj):
    T, D = dmo.shape
    tm, tn = min(512, T), 512
    gpb = 1024 // tn

    def body(a_ref, w_ref, ya_ref, yb_ref, ga_ref, gb_ref, dya_ref, dyb_ref, dga_ref, dgb_ref):
        dm = _nt(a_ref[...], w_ref[...])
        sa, sb = jax.nn.sigmoid(ga_ref[...]), jax.nn.sigmoid(gb_ref[...])
        dya_ref[...] = (dm * sa).astype(BF16)
        dyb_ref[...] = (dm * sb).astype(BF16)
        dga_ref[...] = (dm * ya_ref[...] * sa * (1.0 - sa)).astype(BF16)
        dgb_ref[...] = (dm * yb_ref[...] * sb * (1.0 - sb)).astype(BF16)

    out = pl.BlockSpec((tm, tn), lambda i, j: (i, j))
    return pl.pallas_call(
        body, name="out_proj_bwd", grid=(T // tm, D // tn), out_shape=[jax.ShapeDtypeStruct((T, D), BF16)] * 4,
        in_specs=[pl.BlockSpec((tm, D), lambda i, j: (i, 0)), pl.BlockSpec((tn, D), lambda i, j: (j, 0)), out, out,
                  pl.BlockSpec((tm, tn), lambda i, j: (i, COL_GA * gpb + j)), pl.BlockSpec((tm, tn), lambda i, j: (i, COL_GB * gpb + j))],
        out_specs=[out] * 4, compiler_params=_cp("parallel", "parallel"),
    )(dmo, w_o, y_a, y_b, proj, proj)


def sgu_bwd(proj, dsgu, g_norm, w_spatial, b_spatial_t):
    T = proj.shape[0]
    W = 1024
    gw = W // SGU_GROUPS

    def body(u_ref, v_ref, ds_ref, g_ref, ws_ref, bst_ref, dz_ref, dw_ref, db_ref, dg_ref):
        @pl.when(pl.program_id(0) == 0)
        def _():
            dw_ref[...] = jnp.zeros_like(dw_ref)
            db_ref[...] = jnp.zeros_like(db_ref)
            dg_ref[...] = jnp.zeros_like(dg_ref)

        zu, rstd, dhat, vn, vm = _sgu_mix(u_ref, v_ref, g_ref, ws_ref, bst_ref)
        ds = ds_ref[...]
        du = ds * vm
        dvm = ds * zu
        dvm_b = dvm.astype(BF16)
        ones = jnp.ones((8, gw), F32)
        dvn = []
        for g in range(SGU_GROUPS):
            sl = slice(g * gw, (g + 1) * gw)
            dw_ref[g] += _nt(dvm_b[:, sl], vn[:, sl])
            db_ref[g] += lax.dot_general(ones, dvm[:, sl], (((1,), (1,)), ((), ())), precision=HI, preferred_element_type=F32)
            dvn.append(_tn(ws_ref[g].astype(BF16), dvm_b[:, sl]))
        dvn = jnp.concatenate(dvn, axis=1)
        dg_ref[...] += _colsum(dvn * dhat)
        ddh = dvn * g_ref[...]
        dzv = rstd * (ddh - jnp.mean(ddh, axis=-1, keepdims=True) - dhat * jnp.mean(ddh * dhat, axis=-1, keepdims=True))
        dz_ref[:, 0:W] = (du * _gelu_grad(u_ref[...])).astype(BF16)
        dz_ref[:, W:2 * W] = (dzv * _gelu_grad(v_ref[...])).astype(BF16)

    blk = lambda cb: pl.BlockSpec((SGU_CHUNK, W), lambda i: (i, cb))
    full3 = lambda a, b, c: pl.BlockSpec((a, b, c), lambda i: (0, 0, 0))
    return pl.pallas_call(
        body, name="sgu_bwd", grid=(T // SGU_CHUNK,),
        out_shape=[jax.ShapeDtypeStruct((T, 2 * W), BF16), jax.ShapeDtypeStruct((SGU_GROUPS, SGU_CHUNK, SGU_CHUNK), F32),
                   jax.ShapeDtypeStruct((SGU_GROUPS, 8, SGU_CHUNK), F32), jax.ShapeDtypeStruct((1, W), F32)],
        in_specs=[blk(COL_U), blk(COL_ZV), blk(0), _vec(W), full3(SGU_GROUPS, SGU_CHUNK, SGU_CHUNK),
                  pl.BlockSpec((SGU_CHUNK, SGU_GROUPS), lambda i: (0, 0))],
        out_specs=[pl.BlockSpec((SGU_CHUNK, 2 * W), lambda i: (i, 0)), full3(SGU_GROUPS, SGU_CHUNK, SGU_CHUNK),
                   full3(SGU_GROUPS, 8, SGU_CHUNK), _vec(W)],
        compiler_params=_cp("arbitrary"),
    )(proj, proj, dsgu, g_norm, w_spatial, b_spatial_t)


def hgrn_post_bwd(dya, o, proj, g_norm):
    T, W = o.shape
    tm = min(256, T)

    def body(dy_ref, o_ref, og_ref, g_ref, do_ref, dog_ref, dg_ref):
        @pl.when(pl.program_id(0) == 0)
        def _():
            dg_ref[...] = jnp.zeros_like(dg_ref)

        g = g_ref[...]
        dg = jnp.zeros((1, HEAD_DIM), F32)
        for h in range(N_HEADS):
            sl = slice(h * HEAD_DIM, (h + 1) * HEAD_DIM)
            x, og, dy = o_ref[:, sl], og_ref[:, sl], dy_ref[:, sl]
            r = lax.rsqrt(jnp.mean(x * x, axis=-1, keepdims=True) + EPS)
            xhat = x * r
            s = jax.nn.sigmoid(og)
            don = dy * (og * s)
            dog_ref[:, sl] = (dy * (xhat * g) * (s * (1.0 + og * (1.0 - s)))).astype(BF16)
            dg += _colsum(don * xhat)
            u = don * g
            do_ref[:, sl] = r * (u - xhat * jnp.mean(u * xhat, axis=-1, keepdims=True))
        dg_ref[...] += dg

    row = pl.BlockSpec((tm, W), lambda i: (i, 0))
    return pl.pallas_call(
        body, name="hgrn_post_bwd", grid=(T // tm,),
        out_shape=[jax.ShapeDtypeStruct((T, W), F32), jax.ShapeDtypeStruct((T, W), BF16), jax.ShapeDtypeStruct((1, HEAD_DIM), F32)],
        in_specs=[row, row, pl.BlockSpec((tm, W), lambda i: (i, COL_OG)), _vec(HEAD_DIM)],
        out_specs=[row, row, _vec(HEAD_DIM)], compiler_params=_cp("arbitrary"),
    )(dya, o, proj, g_norm)


def hgrn_bwd(proj, do, lb_logits, comm=None):
    T = proj.shape[0]
    NC, CPB = T // HGRN_CHUNK, HGRN_BLOCK // HGRN_CHUNK
    W = N_HEADS * HEAD_DIM
    col, f_spec, l_spec = _hgrn_specs(T)

    def body(l_ref, q_ref, f_ref, v_ref, do_ref, dq_ref, dv_ref, dlg_ref, dlb_ref, st_ref, dst_ref, dec_ref, ddec_ref, dqa_ref, dva_ref):
        d = pl.program_id(1)
        lb = _hgrn_lower_bound(l_ref)
        oml = 1.0 - lb
        mask, mtri, mtri_t = _hgrn_chunk_masks(d)

        def values(rows):
            s, sn, fg, lf, k = _hgrn_gate(f_ref[rows, :], lb)
            b = _tri_sum(mtri, lf, 3)
            bl = _chunk_total(lf)
            eb, enb, ee = jnp.exp(b), jnp.exp(-b), jnp.exp(bl - b)
            qd = q_ref[rows, :] * Q_SCALE * eb
            return s, sn, fg, k, bl, eb, enb, ee, qd, k * enb, k * ee

        def block1(i, carry):
            rows = pl.ds(pl.multiple_of(i * HGRN_BLOCK, HGRN_BLOCK), HGRN_BLOCK)
            _, _, _, _, bl, _, _, _, qd, _, ke = values(rows)
            qd, ke = qd.astype(BF16), ke.astype(BF16)
            vb, dob = v_ref[rows, :].astype(BF16), do_ref[rows, :].astype(BF16)
            dec = jnp.exp(bl)
            for cc in range(CPB):
                sl = slice(cc * HGRN_CHUNK, (cc + 1) * HGRN_CHUNK)
                n = i * CPB + cc
                st_ref[n] = _tn(vb[sl], ke[sl])
                dst_ref[n] = _tn(dob[sl], qd[sl])
                dec_ref[n] = dec[cc * HGRN_CHUNK:cc * HGRN_CHUNK + 8, :]
            return carry

        _block_loop(T, block1, 0)

        def scan(t, s):
            n = jnp.where(d == 0, t, NC - 1 - t)
            u = st_ref[n]
            st_ref[n] = s
            return dec_ref[n][0:1, :] * s + u

        lax.fori_loop(0, NC, scan, jnp.zeros((HEAD_DIM, HEAD_DIM), F32))

        def rscan(t, ds):
            n = jnp.where(d == 0, NC - 1 - t, t)
            w = dst_ref[n]
            dst_ref[n] = ds
            ddec_ref[n] = jnp.broadcast_to(_colsum(ds * st_ref[n]), (8, HEAD_DIM))
            return dec_ref[n][0:1, :] * ds + w

        lax.fori_loop(0, NC, rscan, jnp.zeros((HEAD_DIM, HEAD_DIM), F32))

        def block3(i, dlb):
            rows = pl.ds(pl.multiple_of(i * HGRN_BLOCK, HGRN_BLOCK), HGRN_BLOCK)
            s, sn, fg, k, bl, eb, enb, ee, qd, kd, ke = values(rows)
            qdb, kdb, keb = qd.astype(BF16), kd.astype(BF16), ke.astype(BF16)
            vb, dob = v_ref[rows, :].astype(BF16), do_ref[rows, :].astype(BF16)
            att = jnp.where(mask, _nt(qdb, kdb), 0.0).astype(BF16)
            datt = jnp.where(mask, _nt(dob, vb), 0.0).astype(BF16)
            dv = _tn(att, dob)
            dqd = _nn(datt, kdb)
            dkd = _tn(datt, qdb)
            dv_i, dqd_i, dke, ddl = [], [], [], []
            for cc in range(CPB):
                sl = slice(cc * HGRN_CHUNK, (cc + 1) * HGRN_CHUNK)
                n = i * CPB + cc
                st_b, dst_b = st_ref[n].astype(BF16), dst_ref[n].astype(BF16)
                dv_i.append(_nt(keb[sl], dst_b))
                dqd_i.append(_nn(dob[sl], st_b))
                dke.append(_nn(vb[sl], dst_b))
                ddl.append(jnp.broadcast_to(ddec_ref[n][0:1, :] * dec_ref[n][0:1, :], (HGRN_CHUNK, HEAD_DIM)))
            dv = dv + jnp.concatenate(dv_i, axis=0)
            dqd = dqd + jnp.concatenate(dqd_i, axis=0)
            dke = jnp.concatenate(dke, axis=0)
            dq = dqd * eb * Q_SCALE
            dk = dkd * enb + dke * ee
            t_end = dke * ke
            db = dqd * qd - dkd * kd - t_end
            dlf = _tri_sum(mtri_t, db, 2) + _chunk_total(t_end) + jnp.concatenate(ddl, axis=0)
            e = dlf / fg - dk
            dlg_ref[rows, :] = (oml * e * s * sn).astype(BF16)

            @pl.when(d == 0)
            def _():
                dqa_ref[rows, :] = dq
                dva_ref[rows, :] = dv

            @pl.when(d == 1)
            def _():
                dq_ref[rows, :] = (dqa_ref[rows, :] + dq).astype(BF16)
                dv_ref[rows, :] = (dva_ref[rows, :] + dv).astype(BF16)

            return dlb + _colsum(e * sn)

        dlb_ref[...] = _block_loop(T, block3, jnp.zeros((1, HEAD_DIM), F32))

    head = pl.BlockSpec((T, HEAD_DIM), lambda h, d: (0, h))
    big = pltpu.VMEM((NC, HEAD_DIM, HEAD_DIM), F32)
    small = pltpu.VMEM((NC, 8, HEAD_DIM), F32)
    acc = pltpu.VMEM((T, HEAD_DIM), F32)
    outs, landed = _pallas(
        body, name="hgrn_bwd", grid=(N_HEADS, 2),
        out_shape=[jax.ShapeDtypeStruct((T, W), BF16), jax.ShapeDtypeStruct((T, W), BF16), jax.ShapeDtypeStruct((T, 2 * W), BF16),
                   jax.ShapeDtypeStruct((2, 1, W), F32)],
        in_specs=[l_spec, col(COL_Q), f_spec, col(COL_V), head],
        out_specs=[head, head, pl.BlockSpec((T, HEAD_DIM), lambda h, d: (0, N_HEADS * d + h)),
                   pl.BlockSpec((None, 1, HEAD_DIM), lambda h, d: (d, 0, h))],
        scratch=[big, big, small, small, acc, acc], semantics=("parallel", "arbitrary"), operands=(lb_logits, proj, proj, proj, do), comm=comm)
    return outs if comm is None else (outs, landed)


def mix_norm_bwd(da1, h0, dh1, g_pre, sc1):
    T, D = h0.shape
    tm = min(256, T)

    def body(da_ref, h_ref, dh_ref, g_ref, sc_ref, gx_ref, s_sh, s_sc, s_g):
        @pl.when(pl.program_id(0) == 0)
        def _():
            for s in (s_sh, s_sc, s_g):
                s[...] = jnp.zeros_like(s)

        h, da = h_ref[...], da_ref[...]
        g, sc = g_ref[...], sc_ref[...]
        r = lax.rsqrt(jnp.mean(h * h, axis=-1, keepdims=True) + EPS)
        n = h * r
        s_sh[...] += _colsum(da)
        s_sc[...] += _colsum(da * (n * g))
        s_g[...] += _colsum(da * (1.0 + sc) * n)
        dn = da * g * (1.0 + sc)
        gx_ref[...] = dh_ref[...] + r * (dn - n * jnp.mean(dn * n, axis=-1, keepdims=True))

    row = pl.BlockSpec((tm, D), lambda i: (i, 0))
    return pl.pallas_call(
        body, name="mix_norm_bwd", grid=(T // tm,),
        out_shape=[jax.ShapeDtypeStruct((T, D), F32)] + [jax.ShapeDtypeStruct((1, D), F32)] * 3,
        in_specs=[row, row, row, _vec(D), _vec(D)], out_specs=[row] + [_vec(D)] * 3, compiler_params=_cp("arbitrary"),
    )(da1, h0, dh1, g_pre, sc1)


def adamw(w, g, m, v, name):
    R, C = w.shape
    tr = R if R * C * 4 <= (1 << 21) else max(8, ((1 << 21) // (C * 4)) // 8 * 8)
    while R % tr:
        tr -= 8

    def body(w_ref, g_ref, m_ref, v_ref, d_ref, m2_ref, v2_ref):
        d_ref[...], m2_ref[...], v2_ref[...] = _adamw(w_ref[...], g_ref[...], m_ref[...], v_ref[...])

    row = pl.BlockSpec((tr, C), lambda i: (i, 0))
    return pl.pallas_call(
        body, name=name, grid=(R // tr,), out_shape=[jax.ShapeDtypeStruct((R, C), F32)] * 3,
        in_specs=[row] * 4, out_specs=[row] * 3, compiler_params=_cp("parallel"),
    )(w, g, m, v)


def wada_update(c_all, dmod, w, m, v):
    D, N = w.shape
    tm, tn = 512, 1024

    def body(c_ref, dm_ref, w_ref, m_ref, v_ref, g_ref, d_ref, m2_ref, v2_ref):
        c = c_ref[...]
        g = lax.dot_general(c * jax.nn.sigmoid(c), dm_ref[...], (((0,), (0,)), ((), ())), precision=HI, preferred_element_type=F32)
        g_ref[...] = g
        d_ref[...], m2_ref[...], v2_ref[...] = _adamw(w_ref[...], g, m_ref[...], v_ref[...])

    blk = pl.BlockSpec((tm, tn), lambda i, j: (i, j))
    return pl.pallas_call(
        body, name="wada_update", grid=(D // tm, N // tn), out_shape=[jax.ShapeDtypeStruct((D, N), F32)] * 4,
        in_specs=[pl.BlockSpec((8, tm), lambda i, j: (0, i)), pl.BlockSpec((8, tn), lambda i, j: (0, j)), blk, blk, blk],
        out_specs=[blk] * 4, compiler_params=_cp("parallel", "parallel"),
    )(c_all, dmod, w, m, v)


def sum_devices(gathered, name):
    n, R, C = gathered.shape

    def body(g_ref, o_ref):
        s = g_ref[0]
        for i in range(1, n):
            s = s + g_ref[i]
        o_ref[...] = s

    return pl.pallas_call(body, name=name, out_shape=jax.ShapeDtypeStruct((R, C), F32), compiler_params=_cp())(gathered)


def lb_logits_grad(dlb, lb_logits):
    def body(d_ref, l_ref, o_ref):
        for d in range(2):
            l0, l1 = l_ref[d, 0:1, :], l_ref[d, 1:2, :]
            m = jnp.maximum(l0, l1)
            e0, e1 = jnp.exp(l0 - m), jnp.exp(l1 - m)
            p0, p1 = e0 / (e0 + e1), e1 / (e0 + e1)
            g = d_ref[d:d + 1, :]
            o_ref[d, 0:1, :] = p0 * (g - p0 * g)
            o_ref[d, 1:2, :] = -p1 * (p0 * g)

    return pl.pallas_call(body, name="lb_logits_grad", out_shape=jax.ShapeDtypeStruct(lb_logits.shape, F32), compiler_params=_cp())(dlb, lb_logits)


def add_halves(g, landed, core):
    nj, _, r, cc = g.shape
    tr = min(256, r)

    def body(core_ref, g_ref, l_ref, o_ref):
        o_ref[...] = (g_ref[...].astype(F32) + l_ref[...].astype(F32)).astype(BF16)

    return pl.pallas_call(
        body, name="add_halves_%dx%d" % (r, cc), out_shape=jax.ShapeDtypeStruct((nj, r, cc), BF16),
        grid_spec=pltpu.PrefetchScalarGridSpec(
            num_scalar_prefetch=1, grid=(nj, r // tr),
            in_specs=[pl.BlockSpec((None, None, tr, cc), lambda j, i, core_ref: (j, core_ref[0], i, 0)),
                      pl.BlockSpec((None, None, tr, cc), lambda j, i, core_ref: (j, 0, i, 0))],
            out_specs=pl.BlockSpec((None, tr, cc), lambda j, i, core_ref: (j, i, 0))),
        compiler_params=_cp("parallel", "parallel"),
    )(core, g, landed)


def sum_chips(parts, landed, chip):
    nj, r, cc = parts.shape
    tr = min(256, r)

    def body(chip_ref, p_ref, l_ref, o_ref):
        mine = p_ref[...].astype(F32)
        s = None
        for j in range(nj):
            t = jnp.where(chip_ref[0] == j, mine, l_ref[j].astype(F32))
            s = t if s is None else s + t
        o_ref[...] = s

    return pl.pallas_call(
        body, name="sum_chips_%dx%d" % (r, cc), out_shape=jax.ShapeDtypeStruct((r, cc), F32),
        grid_spec=pltpu.PrefetchScalarGridSpec(
            num_scalar_prefetch=1, grid=(r // tr,),
            in_specs=[pl.BlockSpec((None, tr, cc), lambda i, chip_ref: (chip_ref[0], i, 0)),
                      pl.BlockSpec((nj, tr, cc), lambda i, chip_ref: (0, i, 0))],
            out_specs=pl.BlockSpec((tr, cc), lambda i, chip_ref: (i, 0))),
        compiler_params=_cp("parallel"),
    )(chip, parts, landed)


def adamw_halves(w, own, other, m, v, core, name):
    r, cc = own.shape
    tr = min(128, r)
    nb = r // tr

    def body(core_ref, w_ref, a_ref, b_ref, m_ref, v_ref, g_ref, d_ref, m2_ref, v2_ref):
        g = jnp.where(pl.program_id(0) == core_ref[0], a_ref[...], b_ref[...])
        g_ref[...] = g
        d_ref[...], m2_ref[...], v2_ref[...] = _adamw(w_ref[...], g, m_ref[...], v_ref[...])

    full = pl.BlockSpec((tr, cc), lambda h, i, core_ref: (h * nb + i, 0))
    half = pl.BlockSpec((tr, cc), lambda h, i, core_ref: (i, 0))
    return pl.pallas_call(
        body, name=name, out_shape=[jax.ShapeDtypeStruct((2 * r, cc), F32)] * 4,
        grid_spec=pltpu.PrefetchScalarGridSpec(
            num_scalar_prefetch=1, grid=(2, nb), in_specs=[full, half, half, full, full], out_specs=[full] * 4),
        compiler_params=_cp("parallel", "parallel"),
    )(core, w, own, other, m, v)


def _place():
    mx, my, mc = lax.axis_index("x"), lax.axis_index("y"), lax.axis_index("c")
    chips = [(1 - mx, my), (mx, 1 - my), (1 - mx, 1 - my)]
    return mx, my, mc, chips


def all_gather_small(x, name):
    R, C = x.shape

    def body(x_ref, out_ref, send_sems, recv_sems, local_sem):
        mx, my, mc, _ = _place()
        me = 4 * mx + 2 * my + mc
        mine = pltpu.make_async_copy(x_ref, out_ref.at[me], local_sem)
        mine.start()

        def peer(k):
            px = 1 - mx if k & 4 else mx
            py = 1 - my if k & 2 else my
            pc = 1 - mc if k & 1 else mc
            return px, py, pc

        def copy(k, src, slot):
            return pltpu.make_async_remote_copy(src_ref=src, dst_ref=out_ref.at[slot], send_sem=send_sems.at[k - 1],
                                                recv_sem=recv_sems.at[k - 1], device_id=peer(k), device_id_type=MESH)

        sends = [copy(k, x_ref, me) for k in range(1, 8)]
        for cp in sends:
            cp.start()
        for k in range(1, 8):
            px, py, pc = peer(k)
            slot = 4 * px + 2 * py + pc
            copy(k, out_ref.at[slot], slot).wait_recv()
        for cp in sends:
            cp.wait_send()
        mine.wait()

    return pl.pallas_call(
        body, name=name, out_shape=jax.ShapeDtypeStruct((8, R, C), F32),
        in_specs=[pl.BlockSpec(memory_space=pltpu.VMEM)], out_specs=pl.BlockSpec(memory_space=pltpu.VMEM),
        scratch_shapes=[pltpu.SemaphoreType.DMA((7,)), pltpu.SemaphoreType.DMA((7,)), pltpu.SemaphoreType.DMA],
        compiler_params=_cp(),
    )(x)


def gather8_comm(x):
    def copies(x_ref, out_ref, send_sems, recv_sems):
        mx, my, mc, _ = _place()
        me = 4 * mx + 2 * my + mc

        def peer(k):
            return (1 - mx if k & 4 else mx, 1 - my if k & 2 else my, 1 - mc if k & 1 else mc)

        def copy(k, src, slot):
            return pltpu.make_async_remote_copy(src_ref=src, dst_ref=out_ref.at[slot], send_sem=send_sems.at[k - 1],
                                                recv_sem=recv_sems.at[k - 1], device_id=peer(k), device_id_type=MESH)

        sends = [copy(k, x_ref, me) for k in range(1, 8)]
        arrivals = []
        for k in range(1, 8):
            px, py, pc = peer(k)
            slot = 4 * px + 2 * py + pc
            arrivals.append(copy(k, out_ref.at[slot], slot))
        return sends, arrivals, pltpu.make_async_copy(x_ref, out_ref.at[me], send_sems.at[7])

    def start(cin, cout, send_sems, recv_sems):
        sends, _, mine = copies(cin[0], cout[0], send_sems, recv_sems)
        mine.start()
        for cp in sends:
            cp.start()

    def finish(cin, cout, send_sems, recv_sems):
        sends, arrivals, mine = copies(cin[0], cout[0], send_sems, recv_sems)
        for cp in arrivals:
            cp.wait_recv()
        for cp in sends:
            cp.wait_send()
        mine.wait()

    return _Comm([x], [jax.ShapeDtypeStruct((8,) + x.shape, F32)], {}, 8, start, finish)


def _join(a, b):
    na_in, na_out = len(a.operands), len(a.out_shape)

    def split(fn_a, fn_b):
        def both(cin, cout, send_sems, recv_sems):
            fn_a(cin[:na_in], cout[:na_out], send_sems.at[pl.ds(0, a.n_sems)], recv_sems.at[pl.ds(0, a.n_sems)])
            fn_b(cin[na_in:], cout[na_out:], send_sems.at[pl.ds(a.n_sems, b.n_sems)], recv_sems.at[pl.ds(a.n_sems, b.n_sems)])
        return both

    aliases = dict(a.aliases)
    aliases.update({na_in + i: na_out + o for i, o in b.aliases.items()})
    return _Comm(a.operands + b.operands, a.out_shape + b.out_shape, aliases, a.n_sems + b.n_sems, split(a.start, b.start), split(a.finish, b.finish))


def _region(ref, kind, j, half, r, cc):
    nr = r if half is None else r // 2
    off = 0 if half is None else half * nr
    if kind == "col":
        return ref.at[pl.ds(off, nr), pl.ds(pl.multiple_of(j * cc, 128), cc)]
    return ref.at[pl.ds(pl.multiple_of(j * r + off, 16), nr), :]


def comm_call(comm, name):
    ni, no = len(comm.operands), len(comm.out_shape)

    def body(*refs):
        comm.start(refs[:ni], refs[ni:ni + no], *refs[ni + no:])
        comm.finish(refs[:ni], refs[ni:ni + no], *refs[ni + no:])

    return pl.pallas_call(
        body, name=name, out_shape=comm.out_shape, in_specs=[ANY] * ni, out_specs=[ANY] * no, input_output_aliases=comm.aliases,
        scratch_shapes=[pltpu.SemaphoreType.DMA((comm.n_sems,)), pltpu.SemaphoreType.DMA((comm.n_sems,))], compiler_params=_cp(),
    )(*comm.operands)


def gather_comm(fulls, kinds, dims):
    n = len(fulls)

    def copies(f_refs, send_sems, recv_sems):
        mx, my, mc, chips = _place()
        jme = 2 * mx + my

        def landed(w, k, half):
            px, py = chips[k]
            return _region(f_refs[w], kinds[w], 2 * px + py, half, *dims[w])

        def over_ici(w, k, reg):
            px, py = chips[k]
            return pltpu.make_async_remote_copy(src_ref=reg, dst_ref=reg, send_sem=send_sems.at[6 * w + k], recv_sem=recv_sems.at[6 * w + k],
                                                device_id=(px, py, mc), device_id_type=MESH)

        def over_d2d(w, k, half):
            reg = landed(w, k, half)
            return pltpu.make_async_remote_copy(src_ref=reg, dst_ref=reg, send_sem=send_sems.at[6 * w + 3 + k],
                                                recv_sem=recv_sems.at[6 * w + 3 + k], device_id=(mx, my, 1 - mc), device_id_type=MESH)

        sends = [over_ici(w, k, _region(f_refs[w], kinds[w], jme, mc, *dims[w])) for w in range(n) for k in range(3)]
        return mc, landed, over_ici, over_d2d, sends

    def start(cin, f_refs, send_sems, recv_sems):
        for cp in copies(f_refs, send_sems, recv_sems)[4]:
            cp.start()

    def finish(cin, f_refs, send_sems, recv_sems):
        mc, landed, over_ici, over_d2d, sends = copies(f_refs, send_sems, recv_sems)
        passed = []
        for w in range(n):
            for k in range(3):
                over_ici(w, k, landed(w, k, mc)).wait_recv()
                cp = over_d2d(w, k, mc)
                cp.start()
                passed.append(cp)
        for w in range(n):
            for k in range(3):
                over_d2d(w, k, 1 - mc).wait_recv()
        for cp in sends + passed:
            cp.wait_send()

    return _Comm(fulls, [jax.ShapeDtypeStruct(f.shape, BF16) for f in fulls], {w: w for w in range(n)}, 6 * n, start, finish)


def exchange_halves(grads, name):
    n = len(grads)

    def body(*refs):
        g_refs, l_refs = refs[:n], refs[n:2 * n]
        send_sems, recv_sems = refs[2 * n:]
        mx, my, mc, _ = _place()
        cps = [pltpu.make_async_remote_copy(src_ref=g_refs[w].at[:, pl.ds(1 - mc, 1)], dst_ref=l_refs[w], send_sem=send_sems.at[w],
                                            recv_sem=recv_sems.at[w], device_id=(mx, my, 1 - mc), device_id_type=MESH) for w in range(n)]
        for cp in cps:
            cp.start()
        for cp in cps:
            cp.wait()

    return pl.pallas_call(
        body, name=name, out_shape=[jax.ShapeDtypeStruct((g.shape[0], 1) + g.shape[2:], BF16) for g in grads],
        in_specs=[ANY] * n, out_specs=[ANY] * n,
        scratch_shapes=[pltpu.SemaphoreType.DMA((n,)), pltpu.SemaphoreType.DMA((n,))], compiler_params=_cp(),
    )(*grads)


def scatter_comm(parts):
    n = len(parts)

    def sends(p_refs, l_refs, send_sems, recv_sems):
        mx, my, mc, chips = _place()
        return [pltpu.make_async_remote_copy(src_ref=p_refs[w].at[2 * px + py], dst_ref=l_refs[w].at[2 * mx + my],
                                             send_sem=send_sems.at[3 * w + k], recv_sem=recv_sems.at[3 * w + k],
                                             device_id=(px, py, mc), device_id_type=MESH) for w in range(n) for k, (px, py) in enumerate(chips)]

    def start(p_refs, l_refs, send_sems, recv_sems):
        for cp in sends(p_refs, l_refs, send_sems, recv_sems):
            cp.start()

    def finish(p_refs, l_refs, send_sems, recv_sems):
        mx, my, mc, chips = _place()
        for w in range(n):
            for k, (px, py) in enumerate(chips):
                slot = l_refs[w].at[2 * px + py]
                pltpu.make_async_remote_copy(src_ref=slot, dst_ref=slot, send_sem=send_sems.at[3 * w + k], recv_sem=recv_sems.at[3 * w + k],
                                             device_id=(px, py, mc), device_id_type=MESH).wait_recv()
        for cp in sends(p_refs, l_refs, send_sems, recv_sems):
            cp.wait_send()

    return _Comm(parts, [jax.ShapeDtypeStruct(p.shape, BF16) for p in parts], {}, 3 * n, start, finish)


def share_comm(sums):
    n = len(sums)

    def copies(q_refs, o_refs, send_sems, recv_sems):
        mx, my, mc, _ = _place()
        return [pltpu.make_async_remote_copy(src_ref=q_refs[w], dst_ref=o_refs[w], send_sem=send_sems.at[w], recv_sem=recv_sems.at[w],
                                             device_id=(mx, my, 1 - mc), device_id_type=MESH) for w in range(n)]

    def start(*refs):
        for cp in copies(*refs):
            cp.start()

    def finish(*refs):
        for cp in copies(*refs):
            cp.wait()

    return _Comm(sums, [jax.ShapeDtypeStruct(q.shape, F32) for q in sums], {}, n, start, finish)


def _pack(arrays):
    flat = jnp.concatenate([a.reshape(-1) for a in arrays])
    rows = -(-flat.shape[0] // 1024) * 8
    return jnp.pad(flat, (0, rows * 128 - flat.shape[0])).reshape(rows, 128)


def _unpack(packed, shapes):
    flat, out, off = packed.reshape(-1), [], 0
    for s in shapes:
        n = math.prod(s)
        out.append(flat[off:off + n].reshape(s))
        off += n
    return out


def kernel(x, c, w_ada, b_ada, g_pre_mix, g_post_mix, g_pre_ffn, g_post_ffn, w_in, lb_logits, g_hgrn_norm, w_a_out, g_sgu_norm, w_spatial, b_spatial, w_b_out, w_o, w_ff1, w_ff2, loss_target, m_w_ada, m_b_ada, m_g_pre_mix, m_g_post_mix, m_g_pre_ffn, m_g_post_ffn, m_w_in, m_lb_logits, m_g_hgrn_norm, m_w_a_out, m_g_sgu_norm, m_w_spatial, m_b_spatial, m_w_b_out, m_w_o, m_w_ff1, m_w_ff2, v_w_ada, v_b_ada, v_g_pre_mix, v_g_post_mix, v_g_pre_ffn, v_g_post_ffn, v_w_in, v_lb_logits, v_g_hgrn_norm, v_w_a_out, v_g_sgu_norm, v_w_spatial, v_b_spatial, v_w_b_out, v_w_o, v_w_ff1, v_w_ff2):
    mx, my, mc = lax.axis_index("x"), lax.axis_index("y"), lax.axis_index("c")
    chip, me = 2 * mx + my, 4 * mx + 2 * my + mc
    D = D_MODEL
    h0, tgt = x[0], loss_target[0]
    n_ada = w_ada.shape[2]
    n_lb = lb_logits.shape[2]

    got = all_gather_small(_pack([c, lb_logits]), "gather_inputs")
    c_all = got[:, :D // 128, :].reshape(8, D)
    lb_full = got[0::2, D // 128:D // 128 + 4 * n_lb // 128, :].reshape(4, 2, 2, n_lb).transpose(1, 2, 0, 3).reshape(2, 2, 4 * n_lb)
    b_ada_chip = lax.dynamic_slice(b_ada, (0, chip * n_ada), (1, n_ada))
    mod_cols = mod_matmul(c_all, w_ada[0], b_ada_chip)
    got = all_gather_small(mod_cols.reshape(-1, 128), "gather_mod").reshape(4, 2, 8, n_ada)
    mod = lax.dynamic_index_in_dim(got[:, 0], me, axis=1, keepdims=False).reshape(6, 1, D)
    sh1, sc1, gt1, sh2, sc2, gt2 = (mod[i] for i in range(6))

    big = [("w_in", w_in, "col"), ("w_a_out", w_a_out, "col"), ("w_b_out", w_b_out, "col"), ("w_o", w_o, "row"),
           ("w_ff1", w_ff1, "col"), ("w_ff2", w_ff2, "row")]
    kinds = [k for _, _, k in big]
    chip_idx, core = chip.reshape(1).astype(jnp.int32), mc.reshape(1).astype(jnp.int32)
    fulls = [cast_into_full(w[0], kind, chip_idx, "cast_" + nm) for nm, w, kind in big]
    dims = [w.shape[1:] for _, w, _ in big]
    later = lambda lo, hi: gather_comm(fulls[lo:hi], kinds[lo:hi], dims[lo:hi])
    halves_summed = lambda grads, name: [add_halves(g, l, core) for g, l in zip(grads, exchange_halves(grads, name))]
    (w_in_f,) = comm_call(later(0, 1), "gather_w_in")

    bst = b_spatial[0].T
    (proj, a1), (w_a_f, w_b_f, w_o_f) = prenorm_matmul(h0, g_pre_mix, sc1, sh1, w_in_f, relu2=False, name="in_proj", comm=later(1, 4))
    o, (w_ff1_f,) = hgrn_fwd(proj, lb_full, comm=later(4, 5))
    ya_pre = hgrn_post_fwd(o, proj, g_hgrn_norm)
    sgu = sgu_fwd(proj, g_sgu_norm, w_spatial[0], bst)
    y_a, y_b, merged = merge_matmul(ya_pre, sgu, w_a_f, w_b_f, proj)
    mo, h1 = out_proj(merged, w_o_f, h0, gt1, g_post_mix)
    (f1, a2, hid), (w_ff2_f,) = prenorm_matmul(h1, g_pre_ffn, sc2, sh2, w_ff1_f, relu2=True, name="ff1", comm=later(5, 6))
    dy, dff, loss_parts, d_gt2, d_g_post_ffn = ff2_loss(hid, w_ff2_f, h1, tgt, gt2, g_post_ffn)
    loss = lax.psum(0.5 * loss_parts[0, 0] / D, ("x", "y", "c"))

    df1 = ff2_bwd(dff, w_ff2_f, f1)
    gr_ff2 = matmul(hid, dff, mode="tn", out_dtype=BF16, tm=1024, tn=1024, tk=2048, name="dw_ff2")
    da2 = matmul(df1, w_ff1_f, mode="nt", out_dtype=F32, tm=1024, tn=1024, tk=2048, name="da2")
    gr_ff1 = matmul(a2, df1, mode="tn", out_dtype=BF16, tm=1024, tn=2048, tk=1024, name="dw_ff1", split=(4, 2))
    parts_ff = halves_summed([gr_ff1, gr_ff2.reshape(4, 2, -1, D)], "exchange_ff")
    dh1, dmo, d_sh2, d_sc2, d_g_pre_ffn, d_gt1, d_g_post_mix = ffn_norm_bwd(dy, da2, h1, mo, g_pre_ffn, sc2, gt1, g_post_mix)
    dya, dyb, dga, dgb = out_proj_bwd(dmo, w_o_f, y_a, y_b, proj)
    gr_o = matmul(merged, dmo, mode="tn", out_dtype=BF16, tm=1024, tn=1024, tk=2048, name="dw_o")
    dsgu = matmul(dyb, w_b_f, mode="nt", out_dtype=F32, tm=512, tn=1024, tk=2048, name="dsgu")
    gr_b = matmul(sgu, dyb, mode="tn", out_dtype=BF16, tm=512, tn=512, tk=4096, name="dw_b_out", split=(4, 2))
    dz, d_w_spatial, d_b_spatial, d_g_sgu = sgu_bwd(proj, dsgu, g_sgu_norm, w_spatial[0], bst)
    dya_pre = matmul(dya, w_a_f, mode="nt", out_dtype=F32, tm=512, tn=1024, tk=2048, name="dya_pre")
    gr_a = matmul(ya_pre, dya, mode="tn", out_dtype=BF16, tm=512, tn=512, tk=4096, name="dw_a_out", split=(4, 2))
    parts_mix = halves_summed([gr_a, gr_b, gr_o.reshape(4, 2, -1, D)], "exchange_mix")
    do, dog, d_g_hgrn = hgrn_post_bwd(dya_pre, o, proj, g_hgrn_norm)
    chips_summed = lambda parts, landed: [sum_chips(p, l, chip_idx) for p, l in zip(parts, landed)]
    (dq, dv, dlg, d_lb), landed_ff = hgrn_bwd(proj, do, lb_full, comm=scatter_comm(parts_ff))
    own_ff = chips_summed(parts_ff, landed_ff)
    dproj = jnp.concatenate([dq, dlg, dv, dog, dz, dga, dgb], axis=1)
    early = _pack([d_g_sgu, d_w_spatial, d_b_spatial[:, 0, :]])
    gr_in, (*landed_mix, got_early) = matmul(a1, dproj, mode="tn", out_dtype=BF16, tm=1024, tn=2816, tk=1024, name="dw_in", split=(4, 2),
                                             comm=_join(scatter_comm(parts_mix), gather8_comm(early)))
    own_mix = chips_summed(parts_mix, landed_mix)
    parts_in = halves_summed([gr_in], "exchange_in")
    da1, (landed_in, *other_rest) = matmul(dproj, w_in_f, mode="nt", out_dtype=F32, tm=1024, tn=1024, tk=2816, name="da1",
                                           comm=_join(scatter_comm(parts_in), share_comm(own_mix + own_ff)))
    own_in = chips_summed(parts_in, [landed_in])
    other_in = comm_call(share_comm(own_in), "share_w_in")
    own, other = own_in + own_mix + own_ff, list(other_in) + other_rest
    grad_x, d_sh1, d_sc1, d_g_pre_mix = mix_norm_bwd(da1, h0, dh1, g_pre_mix, sc1)
    out = {}

    mine = _pack([d_sh1, d_sc1, d_gt1, d_sh2, d_sc2, d_gt2, d_g_pre_mix, d_g_post_mix, d_g_pre_ffn, d_g_post_ffn, d_g_hgrn, d_lb])
    got = all_gather_small(mine, "gather_small_grads")
    g_b_ada, g_g1, g_g2, g_g3, g_g4, g_hg, g_lb = _unpack(
        sum_devices(got, "sum_small_grads"), [(1, 6 * D), (1, D), (1, D), (1, D), (1, D), (1, HEAD_DIM), (2, 1024)])
    g_sg, g_ws, g_bs = _unpack(sum_devices(got_early, "sum_sgu_grads"), [(1, 1024), w_spatial.shape, b_spatial.shape])
    g_lbl = lax.dynamic_slice(lb_logits_grad(g_lb, lb_full), (0, 0, chip * n_lb), (2, 2, n_lb))
    names = ["b_ada", "g_pre_mix", "g_post_mix", "g_pre_ffn", "g_post_ffn", "g_hgrn_norm", "g_sgu_norm", "w_spatial", "b_spatial", "lb_logits"]
    ws = [b_ada, g_pre_mix, g_post_mix, g_pre_ffn, g_post_ffn, g_hgrn_norm, g_sgu_norm, w_spatial, b_spatial, lb_logits]
    gs = [g_b_ada, g_g1, g_g2, g_g3, g_g4, g_hg, g_sg, g_ws, g_bs, g_lbl]
    ms = [m_b_ada, m_g_pre_mix, m_g_post_mix, m_g_pre_ffn, m_g_post_ffn, m_g_hgrn_norm, m_g_sgu_norm, m_w_spatial, m_b_spatial, m_lb_logits]
    vs = [v_b_ada, v_g_pre_mix, v_g_post_mix, v_g_pre_ffn, v_g_post_ffn, v_g_hgrn_norm, v_g_sgu_norm, v_w_spatial, v_b_spatial, v_lb_logits]
    shapes = [w.shape for w in ws]
    upd = adamw(_pack(ws), _pack(gs), _pack(ms), _pack(vs), "adamw_small")
    upd = [_unpack(u, shapes) for u in upd]
    for i, nm in enumerate(names):
        out[nm] = (gs[i], upd[0][i], upd[1][i], upd[2][i])

    dmod_all = got[:, :6 * D // 128, :].reshape(8, 6 * D)
    dmod_chip = lax.dynamic_slice(dmod_all, (0, chip * n_ada), (8, n_ada))
    out["w_ada"] = tuple(a[None] for a in wada_update(c_all, dmod_chip, w_ada[0], m_w_ada[0], v_w_ada[0]))
    for (nm, w, _), a, b, m, v in zip(big, own, other, (m_w_in, m_w_a_out, m_w_b_out, m_w_o, m_w_ff1, m_w_ff2),
                                      (v_w_in, v_w_a_out, v_w_b_out, v_w_o, v_w_ff1, v_w_ff2)):
        out[nm] = tuple(t[None] for t in adamw_halves(w[0], a, b, m[0], v[0], core, "adamw_" + nm))

    order = ["w_ada", "b_ada", "g_pre_mix", "g_post_mix", "g_pre_ffn", "g_post_ffn", "w_in", "lb_logits", "g_hgrn_norm", "w_a_out",
             "g_sgu_norm", "w_spatial", "b_spatial", "w_b_out", "w_o", "w_ff1", "w_ff2"]
    return (loss, grad_x[None], *[out[nm][0] for nm in order], *[out[nm][1] for nm in order], *[out[nm][2] for nm in order],
            *[out[nm][3] for nm in order])
```

```python
import functools
import math

import jax
import jax.numpy as jnp
from jax import lax
from jax.experimental import pallas as pl
from jax.experimental.pallas import tpu as pltpu

F32, BF16 = jnp.float32, jnp.bfloat16
HI = lax.Precision.HIGHEST
MESH = pl.DeviceIdType.MESH
ANY = pl.BlockSpec(memory_space=pl.ANY)

EPS = 1e-6
D_MODEL = 2048
N_HEADS = 8
HEAD_DIM = 128
HGRN_CHUNK = 32
HGRN_BLOCK = 256
SGU_CHUNK = 128
SGU_GROUPS = 8
Q_SCALE = HEAD_DIM ** -0.5
COL_Q, COL_FFW, COL_FBW, COL_V, COL_OG, COL_U, COL_ZV, COL_GA, COL_GB = 0, 1, 2, 3, 4, 5, 6, 7, 9
N_PROJ = 11264
VMEM_BYTES_V7X = 64 * 1024 * 1024
VMEM_LIMIT = VMEM_BYTES_V7X - 8 * 1024 * 1024

ADAM_LR, ADAM_B1, ADAM_B2, ADAM_EPS, ADAM_WD, ADAM_STEP = 0.001, 0.9, 0.999, 1e-08, 0.01, 10
ADAM_C1 = 1.0 - ADAM_B1 ** ADAM_STEP
ADAM_C2 = 1.0 - ADAM_B2 ** ADAM_STEP


def _cp(*sem):
    return pltpu.CompilerParams(dimension_semantics=sem if sem else None, vmem_limit_bytes=VMEM_LIMIT)


def _vec(d):
    return pl.BlockSpec((1, d), lambda *_: (0, 0))


def _colsum(x):
    return jnp.sum(x, axis=0, keepdims=True)


def _nt(a, b):
    return lax.dot_general(a, b, (((1,), (1,)), ((), ())), preferred_element_type=F32)


def _tn(a, b):
    return lax.dot_general(a, b, (((0,), (0,)), ((), ())), preferred_element_type=F32)


def _nn(a, b):
    return jnp.dot(a, b, preferred_element_type=F32)


def _adamw(w, g, m, v):
    m2 = ADAM_B1 * m + (1.0 - ADAM_B1) * g
    v2 = ADAM_B2 * v + (1.0 - ADAM_B2) * (g * g)
    delta = -ADAM_LR * ((m2 / ADAM_C1) / (jnp.sqrt(v2 / ADAM_C2) + ADAM_EPS) + ADAM_WD * w)
    return delta, m2, v2


class _Comm:
    def __init__(self, operands, out_shape, aliases, n_sems, start, finish):
        self.operands, self.out_shape, self.aliases, self.n_sems = list(operands), list(out_shape), dict(aliases), n_sems
        self.start, self.finish = start, finish


def _pallas(body, *, name, grid, in_specs, out_specs, out_shape, scratch, semantics, operands, comm=None):
    if comm is None:
        res = pl.pallas_call(body, name=name, grid=grid, in_specs=in_specs, out_specs=out_specs, out_shape=out_shape,
                             scratch_shapes=scratch, compiler_params=_cp(*semantics))(*operands)
        return res, []
    n_in, n_out, n_scr = len(in_specs), len(out_specs), len(scratch)
    nci, nco = len(comm.operands), len(comm.out_shape)

    def with_comm(*refs):
        ins, rest = refs[:n_in], refs[n_in:]
        cin, rest = rest[:nci], rest[nci:]
        outs, rest = rest[:n_out], rest[n_out:]
        cout, rest = rest[:nco], rest[nco:]
        scr, (send, recv) = rest[:n_scr], rest[n_scr:]
        ids = [pl.program_id(a) for a in range(len(grid))]
        first = functools.reduce(jnp.logical_and, [i == 0 for i in ids])
        last = functools.reduce(jnp.logical_and, [i == g - 1 for i, g in zip(ids, grid)])

        @pl.when(first)
        def _():
            comm.start(cin, cout, send, recv)

        body(*ins, *outs, *scr)

        @pl.when(last)
        def _():
            comm.finish(cin, cout, send, recv)

    res = pl.pallas_call(
        with_comm, name=name, grid=grid, in_specs=list(in_specs) + [ANY] * nci, out_specs=list(out_specs) + [ANY] * nco,
        out_shape=list(out_shape) + comm.out_shape, input_output_aliases={n_in + i: n_out + o for i, o in comm.aliases.items()},
        scratch_shapes=list(scratch) + [pltpu.SemaphoreType.DMA((comm.n_sems,)), pltpu.SemaphoreType.DMA((comm.n_sems,))],
        compiler_params=_cp(*["arbitrary"] * len(grid)),
    )(*operands, *comm.operands)
    return res[:n_out], res[n_out:]


def matmul(a, b, *, mode, out_dtype, tm, tn, tk, name, split=None, comm=None):
    if mode == "tn":
        (K, M), (_, N) = a.shape, b.shape
    elif mode == "nt":
        (M, K), (N, _) = a.shape, b.shape
    else:
        (M, K), (_, N) = a.shape, b.shape
    tm, tn, tk = min(tm, M), min(tn, N), min(tk, K)
    nk = K // tk
    a_spec = pl.BlockSpec((tk, tm), lambda i, j, k: (k, i)) if mode == "tn" else pl.BlockSpec((tm, tk), lambda i, j, k: (i, k))
    b_spec = pl.BlockSpec((tn, tk), lambda i, j, k: (j, k)) if mode == "nt" else pl.BlockSpec((tk, tn), lambda i, j, k: (k, j))
    dot = {"nn": _nn, "nt": _nt, "tn": _tn}[mode]
    if split is None:
        out_shape = jax.ShapeDtypeStruct((M, N), out_dtype)
        out_spec = pl.BlockSpec((tm, tn), lambda i, j, k: (i, j))
    else:
        nj, nh = split
        rows, cols = M // nh, N // nj
        tm, tn = min(tm, rows), min(tn, cols)
        bi, bj = rows // tm, cols // tn
        out_shape = jax.ShapeDtypeStruct((nj, nh, rows, cols), out_dtype)
        out_spec = pl.BlockSpec((None, None, tm, tn), lambda i, j, k: (j // bj, i // bi, i % bi, j % bj))

    def body(a_ref, b_ref, o_ref, acc_ref):
        k = pl.program_id(2)

        @pl.when(k == 0)
        def _():
            acc_ref[...] = jnp.zeros_like(acc_ref)

        acc_ref[...] += dot(a_ref[...], b_ref[...])

        @pl.when(k == nk - 1)
        def _():
            o_ref[...] = acc_ref[...].astype(o_ref.dtype)

    (out,), landed = _pallas(
        body, name=name, grid=(M // tm, N // tn, nk), in_specs=[a_spec, b_spec], out_specs=[out_spec], out_shape=[out_shape],
        scratch=[pltpu.VMEM((tm, tn), F32)], semantics=("parallel", "parallel", "arbitrary"), operands=(a, b), comm=comm)
    return out if comm is None else (out, landed)


def cast_into_full(w, kind, chip, name):
    r, cc = w.shape
    tr = min(r, 512)
    nb = r // tr

    def body(chip_ref, w_ref, o_ref):
        o_ref[...] = w_ref[...].astype(BF16)

    if kind == "col":
        full, out_map = (r, 4 * cc), lambda i, chip_ref: (i, chip_ref[0])
    else:
        full, out_map = (4 * r, cc), lambda i, chip_ref: (chip_ref[0] * nb + i, 0)
    return pl.pallas_call(
        body, name=name, out_shape=jax.ShapeDtypeStruct(full, BF16),
        grid_spec=pltpu.PrefetchScalarGridSpec(
            num_scalar_prefetch=1, grid=(nb,), in_specs=[pl.BlockSpec((tr, cc), lambda i, chip_ref: (i, 0))],
            out_specs=pl.BlockSpec((tr, cc), out_map)),
        compiler_params=_cp("parallel"),
    )(chip, w)


def mod_matmul(c_all, w_ada, b_ada):
    D, N = w_ada.shape
    tn = 1024

    def body(c_ref, w_ref, b_ref, o_ref):
        c = c_ref[...]
        sc = c * jax.nn.sigmoid(c)
        o_ref[...] = jnp.dot(sc, w_ref[...], precision=HI, preferred_element_type=F32) + b_ref[...]

    return pl.pallas_call(
        body, name="mod_matmul", out_shape=jax.ShapeDtypeStruct((8, N), F32), grid=(N // tn,),
        in_specs=[pl.BlockSpec((8, D), lambda j: (0, 0)), pl.BlockSpec((D, tn), lambda j: (0, j)),
                  pl.BlockSpec((1, tn), lambda j: (0, j))],
        out_specs=pl.BlockSpec((8, tn), lambda j: (0, j)), compiler_params=_cp("parallel"),
    )(c_all, w_ada, b_ada)


def prenorm_matmul(h, g, sc, sh, w, *, relu2, name, comm=None):
    T, D = h.shape
    N = w.shape[1]
    tm, tn = min(512, T), 2048 if N % 2048 == 0 else 1024

    def body(h_ref, g_ref, sc_ref, sh_ref, w_ref, y_ref, a_ref, *hid_ref):
        @pl.when(pl.program_id(1) == 0)
        def _():
            x = h_ref[...]
            r = lax.rsqrt(jnp.mean(x * x, axis=-1, keepdims=True) + EPS)
            a_ref[...] = ((x * r) * g_ref[...] * (1.0 + sc_ref[...]) + sh_ref[...]).astype(BF16)

        y = _nn(a_ref[...], w_ref[...])
        y_ref[...] = y
        if relu2:
            p = jnp.maximum(y, 0.0)
            hid_ref[0][...] = (p * p).astype(BF16)

    out_shape = [jax.ShapeDtypeStruct((T, N), F32), jax.ShapeDtypeStruct((T, D), BF16)]
    out_specs = [pl.BlockSpec((tm, tn), lambda i, j: (i, j)), pl.BlockSpec((tm, D), lambda i, j: (i, 0))]
    if relu2:
        out_shape.append(jax.ShapeDtypeStruct((T, N), BF16))
        out_specs.append(pl.BlockSpec((tm, tn), lambda i, j: (i, j)))
    outs, landed = _pallas(
        body, name=name, grid=(T // tm, N // tn),
        in_specs=[pl.BlockSpec((tm, D), lambda i, j: (i, 0)), _vec(D), _vec(D), _vec(D), pl.BlockSpec((D, tn), lambda i, j: (0, j))],
        out_specs=out_specs, out_shape=out_shape, scratch=[], semantics=("parallel", "arbitrary"), operands=(h, g, sc, sh, w), comm=comm)
    return outs if comm is None else (outs, landed)


def _hgrn_lower_bound(l_ref):
    l0, l1 = l_ref[0:1, :], l_ref[1:2, :]
    m = jnp.maximum(l0, l1)
    e0, e1 = jnp.exp(l0 - m), jnp.exp(l1 - m)
    return e0 / (e0 + e1)


def _hgrn_chunk_mask(d):
    r = lax.broadcasted_iota(jnp.int32, (HGRN_BLOCK, HGRN_BLOCK), 0)
    c = lax.broadcasted_iota(jnp.int32, (HGRN_BLOCK, HGRN_BLOCK), 1)
    same = (r // HGRN_CHUNK) == (c // HGRN_CHUNK)
    fwd = d == 0
    return same & (((c <= r) & fwd) | ((c >= r) & jnp.logical_not(fwd)))


def _chunk_total(x):
    x3 = x.reshape(HGRN_BLOCK // HGRN_CHUNK, HGRN_CHUNK, x.shape[1])
    return jnp.broadcast_to(jnp.sum(x3, axis=1, keepdims=True), x3.shape).reshape(x.shape)


def _chunk_cumsum(x, suffix):
    pos = lax.broadcasted_iota(jnp.int32, x.shape, 0) % HGRN_CHUNK
    p, s = x, 1
    while s < HGRN_CHUNK:
        p = p + jnp.where(pos >= s, pltpu.roll(p, s, 0), 0.0)
        s *= 2
    return jnp.where(suffix, _chunk_total(x) - p + x, p)


def _block_loop(T, body, init):
    n = T // HGRN_BLOCK
    return lax.fori_loop(0, n, body, init, unroll=2 if n % 2 == 0 else 1)


def _hgrn_gate(f, lb):
    s = jax.nn.sigmoid(f)
    sn = jax.nn.sigmoid(-f)
    fg = lb + (1.0 - lb) * s
    return s, sn, fg, jnp.log(fg), (1.0 - lb) * sn


def _hgrn_specs(T):
    col = lambda base: pl.BlockSpec((T, HEAD_DIM), lambda h, d: (0, base * N_HEADS + h))
    f_spec = pl.BlockSpec((T, HEAD_DIM), lambda h, d: (0, COL_FFW * N_HEADS + N_HEADS * d + h))
    l_spec = pl.BlockSpec((None, 2, HEAD_DIM), lambda h, d: (d, 0, h))
    return col, f_spec, l_spec


def hgrn_fwd(proj, lb_logits, comm=None):
    T = proj.shape[0]
    NC, CPB = T // HGRN_CHUNK, HGRN_BLOCK // HGRN_CHUNK
    col, f_spec, l_spec = _hgrn_specs(T)

    def body(l_ref, q_ref, f_ref, v_ref, o_ref, st_ref, dec_ref, qd_ref):
        d = pl.program_id(1)
        lb = _hgrn_lower_bound(l_ref)
        mask = _hgrn_chunk_mask(d)

        def block(i, carry):
            rows = pl.ds(pl.multiple_of(i * HGRN_BLOCK, HGRN_BLOCK), HGRN_BLOCK)
            _, _, _, lf, k = _hgrn_gate(f_ref[rows, :], lb)
            b = _chunk_cumsum(lf, d == 1)
            bl = _chunk_total(lf)
            qd = (q_ref[rows, :] * Q_SCALE * jnp.exp(b)).astype(BF16)
            kd = (k * jnp.exp(-b)).astype(BF16)
            ke = (k * jnp.exp(bl - b)).astype(BF16)
            vb = v_ref[rows, :].astype(BF16)
            att = jnp.where(mask, _nt(qd, kd), 0.0).astype(BF16)
            o_ref[rows, :] = jnp.where(d == 0, 0.0, o_ref[rows, :]) + _nn(att, vb)
            qd_ref[rows, :] = qd
            dec = jnp.exp(bl)
            for cc in range(CPB):
                sl = slice(cc * HGRN_CHUNK, (cc + 1) * HGRN_CHUNK)
                n = i * CPB + cc
                st_ref[n] = _tn(vb[sl], ke[sl])
                dec_ref[n] = dec[cc * HGRN_CHUNK:cc * HGRN_CHUNK + 8, :]
            return carry

        _block_loop(T, block, 0)

        def scan(t, s):
            n = jnp.where(d == 0, t, NC - 1 - t)
            u = st_ref[n]
            st_ref[n] = s
            return dec_ref[n][0:1, :] * s + u

        lax.fori_loop(0, NC, scan, jnp.zeros((HEAD_DIM, HEAD_DIM), F32))

        def inter(i, carry):
            rows = pl.ds(pl.multiple_of(i * HGRN_BLOCK, HGRN_BLOCK), HGRN_BLOCK)
            qd = qd_ref[rows, :]
            o_ref[rows, :] += jnp.concatenate(
                [_nt(qd[cc * HGRN_CHUNK:(cc + 1) * HGRN_CHUNK], st_ref[i * CPB + cc].astype(BF16)) for cc in range(CPB)], axis=0)
            return carry

        _block_loop(T, inter, 0)

    (o,), landed = _pallas(
        body, name="hgrn_fwd", grid=(N_HEADS, 2), in_specs=[l_spec, col(COL_Q), f_spec, col(COL_V)],
        out_specs=[pl.BlockSpec((T, HEAD_DIM), lambda h, d: (0, h))], out_shape=[jax.ShapeDtypeStruct((T, N_HEADS * HEAD_DIM), F32)],
        scratch=[pltpu.VMEM((NC, HEAD_DIM, HEAD_DIM), F32), pltpu.VMEM((NC, 8, HEAD_DIM), F32), pltpu.VMEM((T, HEAD_DIM), BF16)],
        semantics=("parallel", "arbitrary"), operands=(lb_logits, proj, proj, proj), comm=comm)
    return o if comm is None else (o, landed)


def hgrn_post_fwd(o, proj, g_norm):
    T, W = o.shape
    tm = min(256, T)

    def body(o_ref, og_ref, g_ref, y_ref):
        g = g_ref[...]
        for h in range(N_HEADS):
            sl = slice(h * HEAD_DIM, (h + 1) * HEAD_DIM)
            x = o_ref[:, sl]
            r = lax.rsqrt(jnp.mean(x * x, axis=-1, keepdims=True) + EPS)
            og = og_ref[:, sl]
            y_ref[:, sl] = ((x * r) * g * (og * jax.nn.sigmoid(og))).astype(BF16)

    return pl.pallas_call(
        body, name="hgrn_post_fwd", out_shape=jax.ShapeDtypeStruct((T, W), BF16), grid=(T // tm,),
        in_specs=[pl.BlockSpec((tm, W), lambda i: (i, 0)), pl.BlockSpec((tm, W), lambda i: (i, COL_OG)), _vec(HEAD_DIM)],
        out_specs=pl.BlockSpec((tm, W), lambda i: (i, 0)), compiler_params=_cp("parallel"),
    )(o, proj, g_norm)


def _gelu(x):
    return 0.5 * x * (1.0 + lax.erf(x * (1.0 / math.sqrt(2.0))))


def _gelu_grad(x):
    return 0.5 * (1.0 + lax.erf(x * (1.0 / math.sqrt(2.0)))) + x * jnp.exp(-0.5 * x * x) * (1.0 / math.sqrt(2.0 * math.pi))


def _sgu_mix(u_ref, v_ref, g_ref, ws_ref, bst_ref):
    W = u_ref.shape[1]
    zu, zv = _gelu(u_ref[...]), _gelu(v_ref[...])
    dv = zv - jnp.mean(zv, axis=-1, keepdims=True)
    rstd = lax.rsqrt(jnp.mean(dv * dv, axis=-1, keepdims=True) + EPS)
    dhat = dv * rstd
    vn = (dhat * g_ref[...]).astype(BF16)
    gw = W // SGU_GROUPS
    vm = [_nn(ws_ref[g].astype(BF16), vn[:, g * gw:(g + 1) * gw]) + bst_ref[:, g:g + 1] for g in range(SGU_GROUPS)]
    return zu, rstd, dhat, vn, jnp.concatenate(vm, axis=1)


def sgu_fwd(proj, g_norm, w_spatial, b_spatial_t):
    T = proj.shape[0]
    W = 1024
    n_chunks = T // SGU_CHUNK

    def body(u_ref, v_ref, g_ref, ws_ref, bst_ref, y_ref):
        zu, _, _, _, vm = _sgu_mix(u_ref, v_ref, g_ref, ws_ref, bst_ref)
        y_ref[...] = (zu * vm).astype(BF16)

    blk = lambda cb: pl.BlockSpec((SGU_CHUNK, W), lambda i: (i, cb))
    return pl.pallas_call(
        body, name="sgu_fwd", out_shape=jax.ShapeDtypeStruct((T, W), BF16), grid=(n_chunks,),
        in_specs=[blk(COL_U), blk(COL_ZV), _vec(W), pl.BlockSpec((SGU_GROUPS, SGU_CHUNK, SGU_CHUNK), lambda i: (0, 0, 0)),
                  pl.BlockSpec((SGU_CHUNK, SGU_GROUPS), lambda i: (0, 0))],
        out_specs=blk(0), compiler_params=_cp("parallel"),
    )(proj, proj, g_norm, w_spatial, b_spatial_t)


def merge_matmul(ya_pre, sgu, w_a, w_b, proj):
    T, K = ya_pre.shape
    N = w_a.shape[1]
    tm, tn = min(512, T), 512
    gpb = 1024 // tn

    def body(a_ref, b_ref, wa_ref, wb_ref, ga_ref, gb_ref, ya_ref, yb_ref, m_ref):
        ya = _nn(a_ref[...], wa_ref[...])
        yb = _nn(b_ref[...], wb_ref[...])
        ya_ref[...] = ya
        yb_ref[...] = yb
        m_ref[...] = (jax.nn.sigmoid(ga_ref[...]) * ya + jax.nn.sigmoid(gb_ref[...]) * yb).astype(BF16)

    lhs = pl.BlockSpec((tm, K), lambda i, j: (i, 0))
    rhs = pl.BlockSpec((K, tn), lambda i, j: (0, j))
    out = pl.BlockSpec((tm, tn), lambda i, j: (i, j))
    return pl.pallas_call(
        body, name="merge_matmul", grid=(T // tm, N // tn),
        out_shape=[jax.ShapeDtypeStruct((T, N), F32), jax.ShapeDtypeStruct((T, N), F32), jax.ShapeDtypeStruct((T, N), BF16)],
        in_specs=[lhs, lhs, rhs, rhs, pl.BlockSpec((tm, tn), lambda i, j: (i, COL_GA * gpb + j)),
                  pl.BlockSpec((tm, tn), lambda i, j: (i, COL_GB * gpb + j))],
        out_specs=[out, out, out], compiler_params=_cp("parallel", "parallel"),
    )(ya_pre, sgu, w_a, w_b, proj, proj)


def out_proj(merged, w_o, h0, gt1, g_post):
    T, D = h0.shape
    tm = min(256, T)

    def body(m_ref, w_ref, h_ref, gt_ref, gp_ref, mo_ref, h1_ref):
        mo = _nn(m_ref[...], w_ref[...])
        mo_ref[...] = mo
        r = lax.rsqrt(jnp.mean(mo * mo, axis=-1, keepdims=True) + EPS)
        h1_ref[...] = h_ref[...] + gt_ref[...] * ((mo * r) * gp_ref[...])

    row = pl.BlockSpec((tm, D), lambda i: (i, 0))
    return pl.pallas_call(
        body, name="out_proj", grid=(T // tm,),
        out_shape=[jax.ShapeDtypeStruct((T, D), F32), jax.ShapeDtypeStruct((T, D), F32)],
        in_specs=[row, pl.BlockSpec((D, D), lambda i: (0, 0)), row, _vec(D), _vec(D)],
        out_specs=[row, row], compiler_params=_cp("parallel"),
    )(merged, w_o, h0, gt1, g_post)


def ff2_loss(hid, w_ff2, h1, tgt, gt2, g_post):
    T, K = hid.shape
    D = w_ff2.shape[1]
    tm, tk = min(256, T), 2048
    nk = K // tk

    def body(a_ref, w_ref, h_ref, t_ref, gt_ref, g_ref, dy_ref, dff_ref, loss_ref, dgt_ref, dg_ref, acc_ref):
        i, k = pl.program_id(0), pl.program_id(1)

        @pl.when(k == 0)
        def _():
            acc_ref[...] = jnp.zeros_like(acc_ref)

        @pl.when((k == 0) & (i == 0))
        def _():
            loss_ref[...] = jnp.zeros_like(loss_ref)
            dgt_ref[...] = jnp.zeros_like(dgt_ref)
            dg_ref[...] = jnp.zeros_like(dg_ref)

        acc_ref[...] += _nn(a_ref[...], w_ref[...])

        @pl.when(k == nk - 1)
        def _():
            ff = acc_ref[...]
            gt, g = gt_ref[...], g_ref[...]
            r = lax.rsqrt(jnp.mean(ff * ff, axis=-1, keepdims=True) + EPS)
            fhat = ff * r
            nf = fhat * g
            err = (h_ref[...] + gt * nf) - t_ref[...]
            loss_ref[...] += jnp.sum(err * err)
            dy = err * (1.0 / D)
            dy_ref[...] = dy
            dgt_ref[...] += _colsum(dy * nf)
            dnf = dy * gt
            dg_ref[...] += _colsum(dnf * fhat)
            u = dnf * g
            dff_ref[...] = (r * (u - fhat * jnp.mean(u * fhat, axis=-1, keepdims=True))).astype(BF16)

    row = pl.BlockSpec((tm, D), lambda i, k: (i, 0))
    vec = pl.BlockSpec((1, D), lambda i, k: (0, 0))
    return pl.pallas_call(
        body, name="ff2_loss", grid=(T // tm, nk),
        out_shape=[jax.ShapeDtypeStruct((T, D), F32), jax.ShapeDtypeStruct((T, D), BF16), jax.ShapeDtypeStruct((8, 128), F32),
                   jax.ShapeDtypeStruct((1, D), F32), jax.ShapeDtypeStruct((1, D), F32)],
        in_specs=[pl.BlockSpec((tm, tk), lambda i, k: (i, k)), pl.BlockSpec((tk, D), lambda i, k: (k, 0)), row, row, vec, vec],
        out_specs=[row, row, pl.BlockSpec((8, 128), lambda i, k: (0, 0)), vec, vec],
        scratch_shapes=[pltpu.VMEM((tm, D), F32)], compiler_params=_cp("arbitrary", "arbitrary"),
    )(hid, w_ff2, h1, tgt, gt2, g_post)


def ff2_bwd(dff, w_ff2, f1):
    T, D = dff.shape
    K = w_ff2.shape[0]
    tm, tn = min(512, T), 2048

    def body(a_ref, w_ref, f_ref, o_ref):
        o_ref[...] = (_nt(a_ref[...], w_ref[...]) * (2.0 * jnp.maximum(f_ref[...], 0.0))).astype(BF16)

    return pl.pallas_call(
        body, name="ff2_bwd", out_shape=jax.ShapeDtypeStruct((T, K), BF16), grid=(K // tn, T // tm),
        in_specs=[pl.BlockSpec((tm, D), lambda j, i: (i, 0)), pl.BlockSpec((tn, D), lambda j, i: (j, 0)),
                  pl.BlockSpec((tm, tn), lambda j, i: (i, j))],
        out_specs=pl.BlockSpec((tm, tn), lambda j, i: (i, j)), compiler_params=_cp("parallel", "parallel"),
    )(dff, w_ff2, f1)


def ffn_norm_bwd(dy, da2, h1, mo, g_pre2, sc2, gt1, g_post):
    T, D = dy.shape
    tm = min(256, T)

    def body(dy_ref, da_ref, h_ref, mo_ref, g2_ref, sc_ref, gt_ref, gp_ref, dh_ref, dmo_ref, s_sh, s_sc, s_g2, s_gt, s_gp):
        @pl.when(pl.program_id(0) == 0)
        def _():
            for s in (s_sh, s_sc, s_g2, s_gt, s_gp):
                s[...] = jnp.zeros_like(s)

        h1, da = h_ref[...], da_ref[...]
        g2, sc = g2_ref[...], sc_ref[...]
        r2 = lax.rsqrt(jnp.mean(h1 * h1, axis=-1, keepdims=True) + EPS)
        n2 = h1 * r2
        s_sh[...] += _colsum(da)
        s_sc[...] += _colsum(da * (n2 * g2))
        s_g2[...] += _colsum(da * (1.0 + sc) * n2)
        dn2 = da * g2 * (1.0 + sc)
        dh1 = dy_ref[...] + r2 * (dn2 - n2 * jnp.mean(dn2 * n2, axis=-1, keepdims=True))
        dh_ref[...] = dh1
        mo = mo_ref[...]
        gt, gp = gt_ref[...], gp_ref[...]
        r = lax.rsqrt(jnp.mean(mo * mo, axis=-1, keepdims=True) + EPS)
        mhat = mo * r
        s_gt[...] += _colsum(dh1 * (mhat * gp))
        dnm = dh1 * gt
        s_gp[...] += _colsum(dnm * mhat)
        u = dnm * gp
        dmo_ref[...] = (r * (u - mhat * jnp.mean(u * mhat, axis=-1, keepdims=True))).astype(BF16)

    row = pl.BlockSpec((tm, D), lambda i: (i, 0))
    vec_out = jax.ShapeDtypeStruct((1, D), F32)
    return pl.pallas_call(
        body, name="ffn_norm_bwd", grid=(T // tm,),
        out_shape=[jax.ShapeDtypeStruct((T, D), F32), jax.ShapeDtypeStruct((T, D), BF16)] + [vec_out] * 5,
        in_specs=[row, row, row, row] + [_vec(D)] * 4, out_specs=[row, row] + [_vec(D)] * 5,
        compiler_params=_cp("arbitrary"),
    )(dy, da2, h1, mo, g_pre2, sc2, gt1, g_post)


def out_proj_bwd(dmo, w_o, y_a, y_b, proj):
    T, D = dmo.shape
    tm, tn = min(512, T), 512
    gpb = 1024 // tn

    def body(a_ref, w_ref, ya_ref, yb_ref, ga_ref, gb_ref, dya_ref, dyb_ref, dga_ref, dgb_ref):
        dm = _nt(a_ref[...], w_ref[...])
        sa, sb = jax.nn.sigmoid(ga_ref[...]), jax.nn.sigmoid(gb_ref[...])
        dya_ref[...] = (dm * sa).astype(BF16)
        dyb_ref[...] = (dm * sb).astype(BF16)
        dga_ref[...] = (dm * ya_ref[...] * sa * (1.0 - sa)).astype(BF16)
        dgb_ref[...] = (dm * yb_ref[...] * sb * (1.0 - sb)).astype(BF16)

    out = pl.BlockSpec((tm, tn), lambda i, j: (i, j))
    return pl.pallas_call(
        body, name="out_proj_bwd", grid=(T // tm, D // tn), out_shape=[jax.ShapeDtypeStruct((T, D), BF16)] * 4,
        in_specs=[pl.BlockSpec((tm, D), lambda i, j: (i, 0)), pl.BlockSpec((tn, D), lambda i, j: (j, 0)), out, out,
                  pl.BlockSpec((tm, tn), lambda i, j: (i, COL_GA * gpb + j)), pl.BlockSpec((tm, tn), lambda i, j: (i, COL_GB * gpb + j))],
        out_specs=[out] * 4, compiler_params=_cp("parallel", "parallel"),
    )(dmo, w_o, y_a, y_b, proj, proj)


def sgu_bwd(proj, dsgu, g_norm, w_spatial, b_spatial_t):
    T = proj.shape[0]
    W = 1024
    gw = W // SGU_GROUPS

    def body(u_ref, v_ref, ds_ref, g_ref, ws_ref, bst_ref, dz_ref, dw_ref, db_ref, dg_ref):
        @pl.when(pl.program_id(0) == 0)
        def _():
            dw_ref[...] = jnp.zeros_like(dw_ref)
            db_ref[...] = jnp.zeros_like(db_ref)
            dg_ref[...] = jnp.zeros_like(dg_ref)

        zu, rstd, dhat, vn, vm = _sgu_mix(u_ref, v_ref, g_ref, ws_ref, bst_ref)
        ds = ds_ref[...]
        du = ds * vm
        dvm = ds * zu
        dvm_b = dvm.astype(BF16)
        ones = jnp.ones((8, gw), F32)
        dvn = []
        for g in range(SGU_GROUPS):
            sl = slice(g * gw, (g + 1) * gw)
            dw_ref[g] += _nt(dvm_b[:, sl], vn[:, sl])
            db_ref[g] += lax.dot_general(ones, dvm[:, sl], (((1,), (1,)), ((), ())), precision=HI, preferred_element_type=F32)
            dvn.append(_tn(ws_ref[g].astype(BF16), dvm_b[:, sl]))
        dvn = jnp.concatenate(dvn, axis=1)
        dg_ref[...] += _colsum(dvn * dhat)
        ddh = dvn * g_ref[...]
        dzv = rstd * (ddh - jnp.mean(ddh, axis=-1, keepdims=True) - dhat * jnp.mean(ddh * dhat, axis=-1, keepdims=True))
        dz_ref[:, 0:W] = (du * _gelu_grad(u_ref[...])).astype(BF16)
        dz_ref[:, W:2 * W] = (dzv * _gelu_grad(v_ref[...])).astype(BF16)

    blk = lambda cb: pl.BlockSpec((SGU_CHUNK, W), lambda i: (i, cb))
    full3 = lambda a, b, c: pl.BlockSpec((a, b, c), lambda i: (0, 0, 0))
    return pl.pallas_call(
        body, name="sgu_bwd", grid=(T // SGU_CHUNK,),
        out_shape=[jax.ShapeDtypeStruct((T, 2 * W), BF16), jax.ShapeDtypeStruct((SGU_GROUPS, SGU_CHUNK, SGU_CHUNK), F32),
                   jax.ShapeDtypeStruct((SGU_GROUPS, 8, SGU_CHUNK), F32), jax.ShapeDtypeStruct((1, W), F32)],
        in_specs=[blk(COL_U), blk(COL_ZV), blk(0), _vec(W), full3(SGU_GROUPS, SGU_CHUNK, SGU_CHUNK),
                  pl.BlockSpec((SGU_CHUNK, SGU_GROUPS), lambda i: (0, 0))],
        out_specs=[pl.BlockSpec((SGU_CHUNK, 2 * W), lambda i: (i, 0)), full3(SGU_GROUPS, SGU_CHUNK, SGU_CHUNK),
                   full3(SGU_GROUPS, 8, SGU_CHUNK), _vec(W)],
        compiler_params=_cp("arbitrary"),
    )(proj, proj, dsgu, g_norm, w_spatial, b_spatial_t)


def hgrn_post_bwd(dya, o, proj, g_norm):
    T, W = o.shape
    tm = min(256, T)

    def body(dy_ref, o_ref, og_ref, g_ref, do_ref, dog_ref, dg_ref):
        @pl.when(pl.program_id(0) == 0)
        def _():
            dg_ref[...] = jnp.zeros_like(dg_ref)

        g = g_ref[...]
        dg = jnp.zeros((1, HEAD_DIM), F32)
        for h in range(N_HEADS):
            sl = slice(h * HEAD_DIM, (h + 1) * HEAD_DIM)
            x, og, dy = o_ref[:, sl], og_ref[:, sl], dy_ref[:, sl]
            r = lax.rsqrt(jnp.mean(x * x, axis=-1, keepdims=True) + EPS)
            xhat = x * r
            s = jax.nn.sigmoid(og)
            don = dy * (og * s)
            dog_ref[:, sl] = (dy * (xhat * g) * (s * (1.0 + og * (1.0 - s)))).astype(BF16)
            dg += _colsum(don * xhat)
            u = don * g
            do_ref[:, sl] = r * (u - xhat * jnp.mean(u * xhat, axis=-1, keepdims=True))
        dg_ref[...] += dg

    row = pl.BlockSpec((tm, W), lambda i: (i, 0))
    return pl.pallas_call(
        body, name="hgrn_post_bwd", grid=(T // tm,),
        out_shape=[jax.ShapeDtypeStruct((T, W), F32), jax.ShapeDtypeStruct((T, W), BF16), jax.ShapeDtypeStruct((1, HEAD_DIM), F32)],
        in_specs=[row, row, pl.BlockSpec((tm, W), lambda i: (i, COL_OG)), _vec(HEAD_DIM)],
        out_specs=[row, row, _vec(HEAD_DIM)], compiler_params=_cp("arbitrary"),
    )(dya, o, proj, g_norm)


def hgrn_bwd(proj, do, lb_logits, comm=None):
    T = proj.shape[0]
    NC, CPB = T // HGRN_CHUNK, HGRN_BLOCK // HGRN_CHUNK
    W = N_HEADS * HEAD_DIM
    col, f_spec, l_spec = _hgrn_specs(T)

    def body(l_ref, q_ref, f_ref, v_ref, do_ref, dq_ref, dv_ref, dlg_ref, dlb_ref, st_ref, dst_ref, dec_ref, ddec_ref, dqa_ref, dva_ref):
        d = pl.program_id(1)
        lb = _hgrn_lower_bound(l_ref)
        oml = 1.0 - lb
        mask = _hgrn_chunk_mask(d)

        def values(rows):
            s, sn, fg, lf, k = _hgrn_gate(f_ref[rows, :], lb)
            b = _chunk_cumsum(lf, d == 1)
            bl = _chunk_total(lf)
            eb, enb, ee = jnp.exp(b), jnp.exp(-b), jnp.exp(bl - b)
            qd = q_ref[rows, :] * Q_SCALE * eb
            return s, sn, fg, k, bl, eb, enb, ee, qd, k * enb, k * ee

        def block1(i, carry):
            rows = pl.ds(pl.multiple_of(i * HGRN_BLOCK, HGRN_BLOCK), HGRN_BLOCK)
            _, _, _, _, bl, _, _, _, qd, _, ke = values(rows)
            qd, ke = qd.astype(BF16), ke.astype(BF16)
            vb, dob = v_ref[rows, :].astype(BF16), do_ref[rows, :].astype(BF16)
            dec = jnp.exp(bl)
            for cc in range(CPB):
                sl = slice(cc * HGRN_CHUNK, (cc + 1) * HGRN_CHUNK)
                n = i * CPB + cc
                st_ref[n] = _tn(vb[sl], ke[sl])
                dst_ref[n] = _tn(dob[sl], qd[sl])
                dec_ref[n] = dec[cc * HGRN_CHUNK:cc * HGRN_CHUNK + 8, :]
            return carry

        _block_loop(T, block1, 0)

        def scan(t, s):
            n = jnp.where(d == 0, t, NC - 1 - t)
            u = st_ref[n]
            st_ref[n] = s
            return dec_ref[n][0:1, :] * s + u

        lax.fori_loop(0, NC, scan, jnp.zeros((HEAD_DIM, HEAD_DIM), F32))

        def rscan(t, ds):
            n = jnp.where(d == 0, NC - 1 - t, t)
            w = dst_ref[n]
            dst_ref[n] = ds
            ddec_ref[n] = jnp.broadcast_to(_colsum(ds * st_ref[n]), (8, HEAD_DIM))
            return dec_ref[n][0:1, :] * ds + w

        lax.fori_loop(0, NC, rscan, jnp.zeros((HEAD_DIM, HEAD_DIM), F32))

        def block3(i, dlb):
            rows = pl.ds(pl.multiple_of(i * HGRN_BLOCK, HGRN_BLOCK), HGRN_BLOCK)
            s, sn, fg, k, bl, eb, enb, ee, qd, kd, ke = values(rows)
            qdb, kdb, keb = qd.astype(BF16), kd.astype(BF16), ke.astype(BF16)
            vb, dob = v_ref[rows, :].astype(BF16), do_ref[rows, :].astype(BF16)
            att = jnp.where(mask, _nt(qdb, kdb), 0.0).astype(BF16)
            datt = jnp.where(mask, _nt(dob, vb), 0.0).astype(BF16)
            dv = _tn(att, dob)
            dqd = _nn(datt, kdb)
            dkd = _tn(datt, qdb)
            dv_i, dqd_i, dke, ddl = [], [], [], []
            for cc in range(CPB):
                sl = slice(cc * HGRN_CHUNK, (cc + 1) * HGRN_CHUNK)
                n = i * CPB + cc
                st_b, dst_b = st_ref[n].astype(BF16), dst_ref[n].astype(BF16)
                dv_i.append(_nt(keb[sl], dst_b))
                dqd_i.append(_nn(dob[sl], st_b))
                dke.append(_nn(vb[sl], dst_b))
                ddl.append(jnp.broadcast_to(ddec_ref[n][0:1, :] * dec_ref[n][0:1, :], (HGRN_CHUNK, HEAD_DIM)))
            dv = dv + jnp.concatenate(dv_i, axis=0)
            dqd = dqd + jnp.concatenate(dqd_i, axis=0)
            dke = jnp.concatenate(dke, axis=0)
            dq = dqd * eb * Q_SCALE
            dk = dkd * enb + dke * ee
            t_end = dke * ke
            db = dqd * qd - dkd * kd - t_end
            dlf = _chunk_cumsum(db, d == 0) + _chunk_total(t_end) + jnp.concatenate(ddl, axis=0)
            e = dlf / fg - dk
            dlg_ref[rows, :] = (oml * e * s * sn).astype(BF16)

            dq = jnp.where(d == 0, 0.0, dqa_ref[rows, :]) + dq
            dv = jnp.where(d == 0, 0.0, dva_ref[rows, :]) + dv
            dqa_ref[rows, :] = dq
            dva_ref[rows, :] = dv
            dq_ref[rows, :] = dq.astype(BF16)
            dv_ref[rows, :] = dv.astype(BF16)

            return dlb + _colsum(e * sn)

        dlb_ref[...] = _block_loop(T, block3, jnp.zeros((1, HEAD_DIM), F32))

    head = pl.BlockSpec((T, HEAD_DIM), lambda h, d: (0, h))
    big = pltpu.VMEM((NC, HEAD_DIM, HEAD_DIM), F32)
    small = pltpu.VMEM((NC, 8, HEAD_DIM), F32)
    acc = pltpu.VMEM((T, HEAD_DIM), F32)
    outs, landed = _pallas(
        body, name="hgrn_bwd", grid=(N_HEADS, 2),
        out_shape=[jax.ShapeDtypeStruct((T, W), BF16), jax.ShapeDtypeStruct((T, W), BF16), jax.ShapeDtypeStruct((T, 2 * W), BF16),
                   jax.ShapeDtypeStruct((2, 1, W), F32)],
        in_specs=[l_spec, col(COL_Q), f_spec, col(COL_V), head],
        out_specs=[head, head, pl.BlockSpec((T, HEAD_DIM), lambda h, d: (0, N_HEADS * d + h)),
                   pl.BlockSpec((None, 1, HEAD_DIM), lambda h, d: (d, 0, h))],
        scratch=[big, big, small, small, acc, acc], semantics=("parallel", "arbitrary"), operands=(lb_logits, proj, proj, proj, do), comm=comm)
    return outs if comm is None else (outs, landed)


def mix_norm_bwd(da1, h0, dh1, g_pre, sc1):
    T, D = h0.shape
    tm = min(256, T)

    def body(da_ref, h_ref, dh_ref, g_ref, sc_ref, gx_ref, s_sh, s_sc, s_g):
        @pl.when(pl.program_id(0) == 0)
        def _():
            for s in (s_sh, s_sc, s_g):
                s[...] = jnp.zeros_like(s)

        h, da = h_ref[...], da_ref[...]
        g, sc = g_ref[...], sc_ref[...]
        r = lax.rsqrt(jnp.mean(h * h, axis=-1, keepdims=True) + EPS)
        n = h * r
        s_sh[...] += _colsum(da)
        s_sc[...] += _colsum(da * (n * g))
        s_g[...] += _colsum(da * (1.0 + sc) * n)
        dn = da * g * (1.0 + sc)
        gx_ref[...] = dh_ref[...] + r * (dn - n * jnp.mean(dn * n, axis=-1, keepdims=True))

    row = pl.BlockSpec((tm, D), lambda i: (i, 0))
    return pl.pallas_call(
        body, name="mix_norm_bwd", grid=(T // tm,),
        out_shape=[jax.ShapeDtypeStruct((T, D), F32)] + [jax.ShapeDtypeStruct((1, D), F32)] * 3,
        in_specs=[row, row, row, _vec(D), _vec(D)], out_specs=[row] + [_vec(D)] * 3, compiler_params=_cp("arbitrary"),
    )(da1, h0, dh1, g_pre, sc1)


def adamw(w, g, m, v, name):
    R, C = w.shape
    tr = R if R * C * 4 <= (1 << 21) else max(8, ((1 << 21) // (C * 4)) // 8 * 8)
    while R % tr:
        tr -= 8

    def body(w_ref, g_ref, m_ref, v_ref, d_ref, m2_ref, v2_ref):
        d_ref[...], m2_ref[...], v2_ref[...] = _adamw(w_ref[...], g_ref[...], m_ref[...], v_ref[...])

    row = pl.BlockSpec((tr, C), lambda i: (i, 0))
    return pl.pallas_call(
        body, name=name, grid=(R // tr,), out_shape=[jax.ShapeDtypeStruct((R, C), F32)] * 3,
        in_specs=[row] * 4, out_specs=[row] * 3, compiler_params=_cp("parallel"),
    )(w, g, m, v)


def wada_update(c_all, dmod, w, m, v):
    D, N = w.shape
    tm, tn = 512, 1024

    def body(c_ref, dm_ref, w_ref, m_ref, v_ref, g_ref, d_ref, m2_ref, v2_ref):
        c = c_ref[...]
        g = lax.dot_general(c * jax.nn.sigmoid(c), dm_ref[...], (((0,), (0,)), ((), ())), precision=HI, preferred_element_type=F32)
        g_ref[...] = g
        d_ref[...], m2_ref[...], v2_ref[...] = _adamw(w_ref[...], g, m_ref[...], v_ref[...])

    blk = pl.BlockSpec((tm, tn), lambda i, j: (i, j))
    return pl.pallas_call(
        body, name="wada_update", grid=(D // tm, N // tn), out_shape=[jax.ShapeDtypeStruct((D, N), F32)] * 4,
        in_specs=[pl.BlockSpec((8, tm), lambda i, j: (0, i)), pl.BlockSpec((8, tn), lambda i, j: (0, j)), blk, blk, blk],
        out_specs=[blk] * 4, compiler_params=_cp("parallel", "parallel"),
    )(c_all, dmod, w, m, v)


def sum_devices(gathered, name):
    n, R, C = gathered.shape

    def body(g_ref, o_ref):
        s = g_ref[0]
        for i in range(1, n):
            s = s + g_ref[i]
        o_ref[...] = s

    return pl.pallas_call(body, name=name, out_shape=jax.ShapeDtypeStruct((R, C), F32), compiler_params=_cp())(gathered)


def lb_logits_grad(dlb, lb_logits):
    def body(d_ref, l_ref, o_ref):
        for d in range(2):
            l0, l1 = l_ref[d, 0:1, :], l_ref[d, 1:2, :]
            m = jnp.maximum(l0, l1)
            e0, e1 = jnp.exp(l0 - m), jnp.exp(l1 - m)
            p0, p1 = e0 / (e0 + e1), e1 / (e0 + e1)
            g = d_ref[d:d + 1, :]
            o_ref[d, 0:1, :] = p0 * (g - p0 * g)
            o_ref[d, 1:2, :] = -p1 * (p0 * g)

    return pl.pallas_call(body, name="lb_logits_grad", out_shape=jax.ShapeDtypeStruct(lb_logits.shape, F32), compiler_params=_cp())(dlb, lb_logits)


def add_halves(g, landed, core):
    nj, _, r, cc = g.shape
    tr = min(256, r)

    def body(core_ref, g_ref, l_ref, o_ref):
        o_ref[...] = (g_ref[...].astype(F32) + l_ref[...].astype(F32)).astype(BF16)

    return pl.pallas_call(
        body, name="add_halves_%dx%d" % (r, cc), out_shape=jax.ShapeDtypeStruct((nj, r, cc), BF16),
        grid_spec=pltpu.PrefetchScalarGridSpec(
            num_scalar_prefetch=1, grid=(nj, r // tr),
            in_specs=[pl.BlockSpec((None, None, tr, cc), lambda j, i, core_ref: (j, core_ref[0], i, 0)),
                      pl.BlockSpec((None, None, tr, cc), lambda j, i, core_ref: (j, 0, i, 0))],
            out_specs=pl.BlockSpec((None, tr, cc), lambda j, i, core_ref: (j, i, 0))),
        compiler_params=_cp("parallel", "parallel"),
    )(core, g, landed)


def sum_chips(parts, landed, chip):
    nj, r, cc = parts.shape
    tr = min(256, r)

    def body(chip_ref, p_ref, l_ref, o_ref):
        mine = p_ref[...].astype(F32)
        s = None
        for j in range(nj):
            t = jnp.where(chip_ref[0] == j, mine, l_ref[j].astype(F32))
            s = t if s is None else s + t
        o_ref[...] = s

    return pl.pallas_call(
        body, name="sum_chips_%dx%d" % (r, cc), out_shape=jax.ShapeDtypeStruct((r, cc), F32),
        grid_spec=pltpu.PrefetchScalarGridSpec(
            num_scalar_prefetch=1, grid=(r // tr,),
            in_specs=[pl.BlockSpec((None, tr, cc), lambda i, chip_ref: (chip_ref[0], i, 0)),
                      pl.BlockSpec((nj, tr, cc), lambda i, chip_ref: (0, i, 0))],
            out_specs=pl.BlockSpec((tr, cc), lambda i, chip_ref: (i, 0))),
        compiler_params=_cp("parallel"),
    )(chip, parts, landed)


def adamw_halves(w, own, other, m, v, core, name):
    r, cc = own.shape
    tr = min(128, r)
    nb = r // tr

    def body(core_ref, w_ref, a_ref, b_ref, m_ref, v_ref, g_ref, d_ref, m2_ref, v2_ref):
        g = jnp.where(pl.program_id(0) == core_ref[0], a_ref[...], b_ref[...])
        g_ref[...] = g
        d_ref[...], m2_ref[...], v2_ref[...] = _adamw(w_ref[...], g, m_ref[...], v_ref[...])

    full = pl.BlockSpec((tr, cc), lambda h, i, core_ref: (h * nb + i, 0))
    half = pl.BlockSpec((tr, cc), lambda h, i, core_ref: (i, 0))
    return pl.pallas_call(
        body, name=name, out_shape=[jax.ShapeDtypeStruct((2 * r, cc), F32)] * 4,
        grid_spec=pltpu.PrefetchScalarGridSpec(
            num_scalar_prefetch=1, grid=(2, nb), in_specs=[full, half, half, full, full], out_specs=[full] * 4),
        compiler_params=_cp("parallel", "parallel"),
    )(core, w, own, other, m, v)


def _place():
    mx, my, mc = lax.axis_index("x"), lax.axis_index("y"), lax.axis_index("c")
    chips = [(1 - mx, my), (mx, 1 - my), (1 - mx, 1 - my)]
    return mx, my, mc, chips


def all_gather_small(x, name):
    R, C = x.shape

    def body(x_ref, out_ref, send_sems, recv_sems, local_sem):
        mx, my, mc, _ = _place()
        me = 4 * mx + 2 * my + mc
        mine = pltpu.make_async_copy(x_ref, out_ref.at[me], local_sem)
        mine.start()

        def peer(k):
            px = 1 - mx if k & 4 else mx
            py = 1 - my if k & 2 else my
            pc = 1 - mc if k & 1 else mc
            return px, py, pc

        def copy(k, src, slot):
            return pltpu.make_async_remote_copy(src_ref=src, dst_ref=out_ref.at[slot], send_sem=send_sems.at[k - 1],
                                                recv_sem=recv_sems.at[k - 1], device_id=peer(k), device_id_type=MESH)

        sends = [copy(k, x_ref, me) for k in range(1, 8)]
        for cp in sends:
            cp.start()
        for k in range(1, 8):
            px, py, pc = peer(k)
            slot = 4 * px + 2 * py + pc
            copy(k, out_ref.at[slot], slot).wait_recv()
        for cp in sends:
            cp.wait_send()
        mine.wait()

    return pl.pallas_call(
        body, name=name, out_shape=jax.ShapeDtypeStruct((8, R, C), F32),
        in_specs=[pl.BlockSpec(memory_space=pltpu.VMEM)], out_specs=pl.BlockSpec(memory_space=pltpu.VMEM),
        scratch_shapes=[pltpu.SemaphoreType.DMA((7,)), pltpu.SemaphoreType.DMA((7,)), pltpu.SemaphoreType.DMA],
        compiler_params=_cp(),
    )(x)


def gather8_comm(x):
    def copies(x_ref, out_ref, send_sems, recv_sems):
        mx, my, mc, _ = _place()
        me = 4 * mx + 2 * my + mc

        def peer(k):
            return (1 - mx if k & 4 else mx, 1 - my if k & 2 else my, 1 - mc if k & 1 else mc)

        def copy(k, src, slot):
            return pltpu.make_async_remote_copy(src_ref=src, dst_ref=out_ref.at[slot], send_sem=send_sems.at[k - 1],
                                                recv_sem=recv_sems.at[k - 1], device_id=peer(k), device_id_type=MESH)

        sends = [copy(k, x_ref, me) for k in range(1, 8)]
        arrivals = []
        for k in range(1, 8):
            px, py, pc = peer(k)
            slot = 4 * px + 2 * py + pc
            arrivals.append(copy(k, out_ref.at[slot], slot))
        return sends, arrivals, pltpu.make_async_copy(x_ref, out_ref.at[me], send_sems.at[7])

    def start(cin, cout, send_sems, recv_sems):
        sends, _, mine = copies(cin[0], cout[0], send_sems, recv_sems)
        mine.start()
        for cp in sends:
            cp.start()

    def finish(cin, cout, send_sems, recv_sems):
        sends, arrivals, mine = copies(cin[0], cout[0], send_sems, recv_sems)
        for cp in arrivals:
            cp.wait_recv()
        for cp in sends:
            cp.wait_send()
        mine.wait()

    return _Comm([x], [jax.ShapeDtypeStruct((8,) + x.shape, F32)], {}, 8, start, finish)


def _join(a, b):
    na_in, na_out = len(a.operands), len(a.out_shape)

    def split(fn_a, fn_b):
        def both(cin, cout, send_sems, recv_sems):
            fn_a(cin[:na_in], cout[:na_out], send_sems.at[pl.ds(0, a.n_sems)], recv_sems.at[pl.ds(0, a.n_sems)])
            fn_b(cin[na_in:], cout[na_out:], send_sems.at[pl.ds(a.n_sems, b.n_sems)], recv_sems.at[pl.ds(a.n_sems, b.n_sems)])
        return both

    aliases = dict(a.aliases)
    aliases.update({na_in + i: na_out + o for i, o in b.aliases.items()})
    return _Comm(a.operands + b.operands, a.out_shape + b.out_shape, aliases, a.n_sems + b.n_sems, split(a.start, b.start), split(a.finish, b.finish))


def _region(ref, kind, j, half, r, cc):
    nr = r if half is None else r // 2
    off = 0 if half is None else half * nr
    if kind == "col":
        return ref.at[pl.ds(off, nr), pl.ds(pl.multiple_of(j * cc, 128), cc)]
    return ref.at[pl.ds(pl.multiple_of(j * r + off, 16), nr), :]


def comm_call(comm, name):
    ni, no = len(comm.operands), len(comm.out_shape)

    def body(*refs):
        comm.start(refs[:ni], refs[ni:ni + no], *refs[ni + no:])
        comm.finish(refs[:ni], refs[ni:ni + no], *refs[ni + no:])

    return pl.pallas_call(
        body, name=name, out_shape=comm.out_shape, in_specs=[ANY] * ni, out_specs=[ANY] * no, input_output_aliases=comm.aliases,
        scratch_shapes=[pltpu.SemaphoreType.DMA((comm.n_sems,)), pltpu.SemaphoreType.DMA((comm.n_sems,))], compiler_params=_cp(),
    )(*comm.operands)


def gather_comm(fulls, kinds, dims):
    n = len(fulls)

    def copies(f_refs, send_sems, recv_sems):
        mx, my, mc, chips = _place()
        jme = 2 * mx + my

        def landed(w, k, half):
            px, py = chips[k]
            return _region(f_refs[w], kinds[w], 2 * px + py, half, *dims[w])

        def over_ici(w, k, reg):
            px, py = chips[k]
            return pltpu.make_async_remote_copy(src_ref=reg, dst_ref=reg, send_sem=send_sems.at[6 * w + k], recv_sem=recv_sems.at[6 * w + k],
                                                device_id=(px, py, mc), device_id_type=MESH)

        def over_d2d(w, k, half):
            reg = landed(w, k, half)
            return pltpu.make_async_remote_copy(src_ref=reg, dst_ref=reg, send_sem=send_sems.at[6 * w + 3 + k],
                                                recv_sem=recv_sems.at[6 * w + 3 + k], device_id=(mx, my, 1 - mc), device_id_type=MESH)

        sends = [over_ici(w, k, _region(f_refs[w], kinds[w], jme, mc, *dims[w])) for w in range(n) for k in range(3)]
        return mc, landed, over_ici, over_d2d, sends

    def start(cin, f_refs, send_sems, recv_sems):
        for cp in copies(f_refs, send_sems, recv_sems)[4]:
            cp.start()

    def finish(cin, f_refs, send_sems, recv_sems):
        mc, landed, over_ici, over_d2d, sends = copies(f_refs, send_sems, recv_sems)
        passed = []
        for w in range(n):
            for k in range(3):
                over_ici(w, k, landed(w, k, mc)).wait_recv()
                cp = over_d2d(w, k, mc)
                cp.start()
                passed.append(cp)
        for w in range(n):
            for k in range(3):
                over_d2d(w, k, 1 - mc).wait_recv()
        for cp in sends + passed:
            cp.wait_send()

    return _Comm(fulls, [jax.ShapeDtypeStruct(f.shape, BF16) for f in fulls], {w: w for w in range(n)}, 6 * n, start, finish)


def exchange_halves(grads, name):
    n = len(grads)

    def body(*refs):
        g_refs, l_refs = refs[:n], refs[n:2 * n]
        send_sems, recv_sems = refs[2 * n:]
        mx, my, mc, _ = _place()
        cps = [pltpu.make_async_remote_copy(src_ref=g_refs[w].at[:, pl.ds(1 - mc, 1)], dst_ref=l_refs[w], send_sem=send_sems.at[w],
                                            recv_sem=recv_sems.at[w], device_id=(mx, my, 1 - mc), device_id_type=MESH) for w in range(n)]
        for cp in cps:
            cp.start()
        for cp in cps:
            cp.wait()

    return pl.pallas_call(
        body, name=name, out_shape=[jax.ShapeDtypeStruct((g.shape[0], 1) + g.shape[2:], BF16) for g in grads],
        in_specs=[ANY] * n, out_specs=[ANY] * n,
        scratch_shapes=[pltpu.SemaphoreType.DMA((n,)), pltpu.SemaphoreType.DMA((n,))], compiler_params=_cp(),
    )(*grads)


def scatter_comm(parts):
    n = len(parts)

    def sends(p_refs, l_refs, send_sems, recv_sems):
        mx, my, mc, chips = _place()
        return [pltpu.make_async_remote_copy(src_ref=p_refs[w].at[2 * px + py], dst_ref=l_refs[w].at[2 * mx + my],
                                             send_sem=send_sems.at[3 * w + k], recv_sem=recv_sems.at[3 * w + k],
                                             device_id=(px, py, mc), device_id_type=MESH) for w in range(n) for k, (px, py) in enumerate(chips)]

    def start(p_refs, l_refs, send_sems, recv_sems):
        for cp in sends(p_refs, l_refs, send_sems, recv_sems):
            cp.start()

    def finish(p_refs, l_refs, send_sems, recv_sems):
        mx, my, mc, chips = _place()
        for w in range(n):
            for k, (px, py) in enumerate(chips):
                slot = l_refs[w].at[2 * px + py]
                pltpu.make_async_remote_copy(src_ref=slot, dst_ref=slot, send_sem=send_sems.at[3 * w + k], recv_sem=recv_sems.at[3 * w + k],
                                             device_id=(px, py, mc), device_id_type=MESH).wait_recv()
        for cp in sends(p_refs, l_refs, send_sems, recv_sems):
            cp.wait_send()

    return _Comm(parts, [jax.ShapeDtypeStruct(p.shape, BF16) for p in parts], {}, 3 * n, start, finish)


def share_comm(sums):
    n = len(sums)

    def copies(q_refs, o_refs, send_sems, recv_sems):
        mx, my, mc, _ = _place()
        return [pltpu.make_async_remote_copy(src_ref=q_refs[w], dst_ref=o_refs[w], send_sem=send_sems.at[w], recv_sem=recv_sems.at[w],
                                             device_id=(mx, my, 1 - mc), device_id_type=MESH) for w in range(n)]

    def start(*refs):
        for cp in copies(*refs):
            cp.start()

    def finish(*refs):
        for cp in copies(*refs):
            cp.wait()

    return _Comm(sums, [jax.ShapeDtypeStruct(q.shape, F32) for q in sums], {}, n, start, finish)


def _pack(arrays):
    flat = jnp.concatenate([a.reshape(-1) for a in arrays])
    rows = -(-flat.shape[0] // 1024) * 8
    return jnp.pad(flat, (0, rows * 128 - flat.shape[0])).reshape(rows, 128)


def _unpack(packed, shapes):
    flat, out, off = packed.reshape(-1), [], 0
    for s in shapes:
        n = math.prod(s)
        out.append(flat[off:off + n].reshape(s))
        off += n
    return out


def kernel(x, c, w_ada, b_ada, g_pre_mix, g_post_mix, g_pre_ffn, g_post_ffn, w_in, lb_logits, g_hgrn_norm, w_a_out, g_sgu_norm, w_spatial, b_spatial, w_b_out, w_o, w_ff1, w_ff2, loss_target, m_w_ada, m_b_ada, m_g_pre_mix, m_g_post_mix, m_g_pre_ffn, m_g_post_ffn, m_w_in, m_lb_logits, m_g_hgrn_norm, m_w_a_out, m_g_sgu_norm, m_w_spatial, m_b_spatial, m_w_b_out, m_w_o, m_w_ff1, m_w_ff2, v_w_ada, v_b_ada, v_g_pre_mix, v_g_post_mix, v_g_pre_ffn, v_g_post_ffn, v_w_in, v_lb_logits, v_g_hgrn_norm, v_w_a_out, v_g_sgu_norm, v_w_spatial, v_b_spatial, v_w_b_out, v_w_o, v_w_ff1, v_w_ff2):
    mx, my, mc = lax.axis_index("x"), lax.axis_index("y"), lax.axis_index("c")
    chip, me = 2 * mx + my, 4 * mx + 2 * my + mc
    D = D_MODEL
    h0, tgt = x[0], loss_target[0]
    n_ada = w_ada.shape[2]
    n_lb = lb_logits.shape[2]

    got = all_gather_small(_pack([c, lb_logits]), "gather_inputs")
    c_all = got[:, :D // 128, :].reshape(8, D)
    lb_full = got[0::2, D // 128:D // 128 + 4 * n_lb // 128, :].reshape(4, 2, 2, n_lb).transpose(1, 2, 0, 3).reshape(2, 2, 4 * n_lb)
    b_ada_chip = lax.dynamic_slice(b_ada, (0, chip * n_ada), (1, n_ada))
    mod_cols = mod_matmul(c_all, w_ada[0], b_ada_chip)
    got = all_gather_small(mod_cols.reshape(-1, 128), "gather_mod").reshape(4, 2, 8, n_ada)
    mod = lax.dynamic_index_in_dim(got[:, 0], me, axis=1, keepdims=False).reshape(6, 1, D)
    sh1, sc1, gt1, sh2, sc2, gt2 = (mod[i] for i in range(6))

    big = [("w_in", w_in, "col"), ("w_a_out", w_a_out, "col"), ("w_b_out", w_b_out, "col"), ("w_o", w_o, "row"),
           ("w_ff1", w_ff1, "col"), ("w_ff2", w_ff2, "row")]
    kinds = [k for _, _, k in big]
    chip_idx, core = chip.reshape(1).astype(jnp.int32), mc.reshape(1).astype(jnp.int32)
    fulls = [cast_into_full(w[0], kind, chip_idx, "cast_" + nm) for nm, w, kind in big]
    dims = [w.shape[1:] for _, w, _ in big]
    later = lambda lo, hi: gather_comm(fulls[lo:hi], kinds[lo:hi], dims[lo:hi])
    halves_summed = lambda grads, name: [add_halves(g, l, core) for g, l in zip(grads, exchange_halves(grads, name))]
    (w_in_f,) = comm_call(later(0, 1), "gather_w_in")

    bst = b_spatial[0].T
    (proj, a1), (w_a_f, w_b_f, w_o_f) = prenorm_matmul(h0, g_pre_mix, sc1, sh1, w_in_f, relu2=False, name="in_proj", comm=later(1, 4))
    o, (w_ff1_f,) = hgrn_fwd(proj, lb_full, comm=later(4, 5))
    ya_pre = hgrn_post_fwd(o, proj, g_hgrn_norm)
    sgu = sgu_fwd(proj, g_sgu_norm, w_spatial[0], bst)
    y_a, y_b, merged = merge_matmul(ya_pre, sgu, w_a_f, w_b_f, proj)
    mo, h1 = out_proj(merged, w_o_f, h0, gt1, g_post_mix)
    (f1, a2, hid), (w_ff2_f,) = prenorm_matmul(h1, g_pre_ffn, sc2, sh2, w_ff1_f, relu2=True, name="ff1", comm=later(5, 6))
    dy, dff, loss_parts, d_gt2, d_g_post_ffn = ff2_loss(hid, w_ff2_f, h1, tgt, gt2, g_post_ffn)
    loss = lax.psum(0.5 * loss_parts[0, 0] / D, ("x", "y", "c"))

    df1 = ff2_bwd(dff, w_ff2_f, f1)
    gr_ff2 = matmul(hid, dff, mode="tn", out_dtype=BF16, tm=1024, tn=1024, tk=2048, name="dw_ff2")
    da2 = matmul(df1, w_ff1_f, mode="nt", out_dtype=F32, tm=1024, tn=1024, tk=2048, name="da2")
    gr_ff1 = matmul(a2, df1, mode="tn", out_dtype=BF16, tm=1024, tn=2048, tk=1024, name="dw_ff1", split=(4, 2))
    parts_ff = halves_summed([gr_ff1, gr_ff2.reshape(4, 2, -1, D)], "exchange_ff")
    dh1, dmo, d_sh2, d_sc2, d_g_pre_ffn, d_gt1, d_g_post_mix = ffn_norm_bwd(dy, da2, h1, mo, g_pre_ffn, sc2, gt1, g_post_mix)
    dya, dyb, dga, dgb = out_proj_bwd(dmo, w_o_f, y_a, y_b, proj)
    gr_o = matmul(merged, dmo, mode="tn", out_dtype=BF16, tm=1024, tn=1024, tk=2048, name="dw_o")
    dsgu = matmul(dyb, w_b_f, mode="nt", out_dtype=F32, tm=512, tn=1024, tk=2048, name="dsgu")
    gr_b = matmul(sgu, dyb, mode="tn", out_dtype=BF16, tm=512, tn=512, tk=4096, name="dw_b_out", split=(4, 2))
    dz, d_w_spatial, d_b_spatial, d_g_sgu = sgu_bwd(proj, dsgu, g_sgu_norm, w_spatial[0], bst)
    dya_pre = matmul(dya, w_a_f, mode="nt", out_dtype=F32, tm=512, tn=1024, tk=2048, name="dya_pre")
    gr_a = matmul(ya_pre, dya, mode="tn", out_dtype=BF16, tm=512, tn=512, tk=4096, name="dw_a_out", split=(4, 2))
    parts_mix = halves_summed([gr_a, gr_b, gr_o.reshape(4, 2, -1, D)], "exchange_mix")
    do, dog, d_g_hgrn = hgrn_post_bwd(dya_pre, o, proj, g_hgrn_norm)
    chips_summed = lambda parts, landed: [sum_chips(p, l, chip_idx) for p, l in zip(parts, landed)]
    (dq, dv, dlg, d_lb), landed_ff = hgrn_bwd(proj, do, lb_full, comm=scatter_comm(parts_ff))
    own_ff = chips_summed(parts_ff, landed_ff)
    dproj = jnp.concatenate([dq, dlg, dv, dog, dz, dga, dgb], axis=1)
    early = _pack([d_g_sgu, d_w_spatial, d_b_spatial[:, 0, :]])
    gr_in, (*landed_mix, got_early) = matmul(a1, dproj, mode="tn", out_dtype=BF16, tm=1024, tn=2816, tk=1024, name="dw_in", split=(4, 2),
                                             comm=_join(scatter_comm(parts_mix), gather8_comm(early)))
    own_mix = chips_summed(parts_mix, landed_mix)
    parts_in = halves_summed([gr_in], "exchange_in")
    da1, (landed_in, *other_rest) = matmul(dproj, w_in_f, mode="nt", out_dtype=F32, tm=1024, tn=1024, tk=2816, name="da1",
                                           comm=_join(scatter_comm(parts_in), share_comm(own_mix + own_ff)))
    own_in = chips_summed(parts_in, [landed_in])
    other_in = comm_call(share_comm(own_in), "share_w_in")
    own, other = own_in + own_mix + own_ff, list(other_in) + other_rest
    grad_x, d_sh1, d_sc1, d_g_pre_mix = mix_norm_bwd(da1, h0, dh1, g_pre_mix, sc1)
    out = {}

    mine = _pack([d_sh1, d_sc1, d_gt1, d_sh2, d_sc2, d_gt2, d_g_pre_mix, d_g_post_mix, d_g_pre_ffn, d_g_post_ffn, d_g_hgrn, d_lb])
    got = all_gather_small(mine, "gather_small_grads")
    g_b_ada, g_g1, g_g2, g_g3, g_g4, g_hg, g_lb = _unpack(
        sum_devices(got, "sum_small_grads"), [(1, 6 * D), (1, D), (1, D), (1, D), (1, D), (1, HEAD_DIM), (2, 1024)])
    g_sg, g_ws, g_bs = _unpack(sum_devices(got_early, "sum_sgu_grads"), [(1, 1024), w_spatial.shape, b_spatial.shape])
    g_lbl = lax.dynamic_slice(lb_logits_grad(g_lb, lb_full), (0, 0, chip * n_lb), (2, 2, n_lb))
    names = ["b_ada", "g_pre_mix", "g_post_mix", "g_pre_ffn", "g_post_ffn", "g_hgrn_norm", "g_sgu_norm", "w_spatial", "b_spatial", "lb_logits"]
    ws = [b_ada, g_pre_mix, g_post_mix, g_pre_ffn, g_post_ffn, g_hgrn_norm, g_sgu_norm, w_spatial, b_spatial, lb_logits]
    gs = [g_b_ada, g_g1, g_g2, g_g3, g_g4, g_hg, g_sg, g_ws, g_bs, g_lbl]
    ms = [m_b_ada, m_g_pre_mix, m_g_post_mix, m_g_pre_ffn, m_g_post_ffn, m_g_hgrn_norm, m_g_sgu_norm, m_w_spatial, m_b_spatial, m_lb_logits]
    vs = [v_b_ada, v_g_pre_mix, v_g_post_mix, v_g_pre_ffn, v_g_post_ffn, v_g_hgrn_norm, v_g_sgu_norm, v_w_spatial, v_b_spatial, v_lb_logits]
    shapes = [w.shape for w in ws]
    upd = adamw(_pack(ws), _pack(gs), _pack(ms), _pack(vs), "adamw_small")
    upd = [_unpack(u, shapes) for u in upd]
    for i, nm in enumerate(names):
        out[nm] = (gs[i], upd[0][i], upd[1][i], upd[2][i])

    dmod_all = got[:, :6 * D // 128, :].reshape(8, 6 * D)
    dmod_chip = lax.dynamic_slice(dmod_all, (0, chip * n_ada), (8, n_ada))
    out["w_ada"] = tuple(a[None] for a in wada_update(c_all, dmod_chip, w_ada[0], m_w_ada[0], v_w_ada[0]))
    for (nm, w, _), a, b, m, v in zip(big, own, other, (m_w_in, m_w_a_out, m_w_b_out, m_w_o, m_w_ff1, m_w_ff2),
                                      (v_w_in, v_w_a_out, v_w_b_out, v_w_o, v_w_ff1, v_w_ff2)):
        out[nm] = tuple(t[None] for t in adamw_halves(w[0], a, b, m[0], v[0], core, "adamw_" + nm))

    order = ["w_ada", "b_ada", "g_pre_mix", "g_post_mix", "g_pre_ffn", "g_post_ffn", "w_in", "lb_logits", "g_hgrn_norm", "w_a_out",
             "g_sgu_norm", "w_spatial", "b_spatial", "w_b_out", "w_o", "w_ff1", "w_ff2"]
    return (loss, grad_x[None], *[out[nm][0] for nm in order], *[out[nm][1] for nm in order], *[out[nm][2] for nm in order],
            *[out[nm][3] for nm in order])
```

```python
import functools
import math

import jax
import jax.numpy as jnp
from jax import lax
from jax.experimental import pallas as pl
from jax.experimental.pallas import tpu as pltpu

F32, BF16 = jnp.float32, jnp.bfloat16
HI = lax.Precision.HIGHEST
MESH = pl.DeviceIdType.MESH
ANY = pl.BlockSpec(memory_space=pl.ANY)

EPS = 1e-6
D_MODEL = 2048
N_HEADS = 8
HEAD_DIM = 128
HGRN_CHUNK = 32
HGRN_BLOCK = 256
SGU_CHUNK = 128
SGU_GROUPS = 8
Q_SCALE = HEAD_DIM ** -0.5
COL_Q, COL_FFW, COL_FBW, COL_V, COL_OG, COL_U, COL_ZV, COL_GA, COL_GB = 0, 1, 2, 3, 4, 5, 6, 7, 9
N_PROJ = 11264
VMEM_BYTES_V7X = 64 * 1024 * 1024
VMEM_LIMIT = VMEM_BYTES_V7X - 8 * 1024 * 1024

ADAM_LR, ADAM_B1, ADAM_B2, ADAM_EPS, ADAM_WD, ADAM_STEP = 0.001, 0.9, 0.999, 1e-08, 0.01, 10
ADAM_C1 = 1.0 - ADAM_B1 ** ADAM_STEP
ADAM_C2 = 1.0 - ADAM_B2 ** ADAM_STEP


def _cp(*sem):
    return pltpu.CompilerParams(dimension_semantics=sem if sem else None, vmem_limit_bytes=VMEM_LIMIT)


def _vec(d):
    return pl.BlockSpec((1, d), lambda *_: (0, 0))


def _colsum(x):
    return jnp.sum(x, axis=0, keepdims=True)


def _nt(a, b):
    return lax.dot_general(a, b, (((1,), (1,)), ((), ())), preferred_element_type=F32)


def _tn(a, b):
    return lax.dot_general(a, b, (((0,), (0,)), ((), ())), preferred_element_type=F32)


def _nn(a, b):
    return jnp.dot(a, b, preferred_element_type=F32)


def _adamw(w, g, m, v):
    m2 = ADAM_B1 * m + (1.0 - ADAM_B1) * g
    v2 = ADAM_B2 * v + (1.0 - ADAM_B2) * (g * g)
    delta = -ADAM_LR * ((m2 / ADAM_C1) / (jnp.sqrt(v2 / ADAM_C2) + ADAM_EPS) + ADAM_WD * w)
    return delta, m2, v2


class _Comm:
    def __init__(self, operands, out_shape, aliases, n_sems, start, finish):
        self.operands, self.out_shape, self.aliases, self.n_sems = list(operands), list(out_shape), dict(aliases), n_sems
        self.start, self.finish = start, finish


def _pallas(body, *, name, grid, in_specs, out_specs, out_shape, scratch, semantics, operands, comm=None):
    if comm is None:
        res = pl.pallas_call(body, name=name, grid=grid, in_specs=in_specs, out_specs=out_specs, out_shape=out_shape,
                             scratch_shapes=scratch, compiler_params=_cp(*semantics))(*operands)
        return res, []
    n_in, n_out, n_scr = len(in_specs), len(out_specs), len(scratch)
    nci, nco = len(comm.operands), len(comm.out_shape)

    def with_comm(*refs):
        ins, rest = refs[:n_in], refs[n_in:]
        cin, rest = rest[:nci], rest[nci:]
        outs, rest = rest[:n_out], rest[n_out:]
        cout, rest = rest[:nco], rest[nco:]
        scr, (send, recv) = rest[:n_scr], rest[n_scr:]
        ids = [pl.program_id(a) for a in range(len(grid))]
        first = functools.reduce(jnp.logical_and, [i == 0 for i in ids])
        last = functools.reduce(jnp.logical_and, [i == g - 1 for i, g in zip(ids, grid)])

        @pl.when(first)
        def _():
            comm.start(cin, cout, send, recv)

        body(*ins, *outs, *scr)

        @pl.when(last)
        def _():
            comm.finish(cin, cout, send, recv)

    res = pl.pallas_call(
        with_comm, name=name, grid=grid, in_specs=list(in_specs) + [ANY] * nci, out_specs=list(out_specs) + [ANY] * nco,
        out_shape=list(out_shape) + comm.out_shape, input_output_aliases={n_in + i: n_out + o for i, o in comm.aliases.items()},
        scratch_shapes=list(scratch) + [pltpu.SemaphoreType.DMA((comm.n_sems,)), pltpu.SemaphoreType.DMA((comm.n_sems,))],
        compiler_params=_cp(*["arbitrary"] * len(grid)),
    )(*operands, *comm.operands)
    return res[:n_out], res[n_out:]


def matmul(a, b, *, mode, out_dtype, tm, tn, tk, name, split=None, comm=None):
    if mode == "tn":
        (K, M), (_, N) = a.shape, b.shape
    elif mode == "nt":
        (M, K), (N, _) = a.shape, b.shape
    else:
        (M, K), (_, N) = a.shape, b.shape
    tm, tn, tk = min(tm, M), min(tn, N), min(tk, K)
    nk = K // tk
    a_spec = pl.BlockSpec((tk, tm), lambda i, j, k: (k, i)) if mode == "tn" else pl.BlockSpec((tm, tk), lambda i, j, k: (i, k))
    b_spec = pl.BlockSpec((tn, tk), lambda i, j, k: (j, k)) if mode == "nt" else pl.BlockSpec((tk, tn), lambda i, j, k: (k, j))
    dot = {"nn": _nn, "nt": _nt, "tn": _tn}[mode]
    if split is None:
        out_shape = jax.ShapeDtypeStruct((M, N), out_dtype)
        out_spec = pl.BlockSpec((tm, tn), lambda i, j, k: (i, j))
    else:
        nj, nh = split
        rows, cols = M // nh, N // nj
        tm, tn = min(tm, rows), min(tn, cols)
        bi, bj = rows // tm, cols // tn
        out_shape = jax.ShapeDtypeStruct((nj, nh, rows, cols), out_dtype)
        out_spec = pl.BlockSpec((None, None, tm, tn), lambda i, j, k: (j // bj, i // bi, i % bi, j % bj))

    def body(a_ref, b_ref, o_ref, acc_ref):
        k = pl.program_id(2)

        @pl.when(k == 0)
        def _():
            acc_ref[...] = jnp.zeros_like(acc_ref)

        acc_ref[...] += dot(a_ref[...], b_ref[...])

        @pl.when(k == nk - 1)
        def _():
            o_ref[...] = acc_ref[...].astype(o_ref.dtype)

    (out,), landed = _pallas(
        body, name=name, grid=(M // tm, N // tn, nk), in_specs=[a_spec, b_spec], out_specs=[out_spec], out_shape=[out_shape],
        scratch=[pltpu.VMEM((tm, tn), F32)], semantics=("parallel", "parallel", "arbitrary"), operands=(a, b), comm=comm)
    return out if comm is None else (out, landed)


def cast_into_full(w, kind, chip, name):
    r, cc = w.shape
    tr = min(r, 512)
    nb = r // tr

    def body(chip_ref, w_ref, o_ref):
        o_ref[...] = w_ref[...].astype(BF16)

    if kind == "col":
        full, out_map = (r, 4 * cc), lambda i, chip_ref: (i, chip_ref[0])
    else:
        full, out_map = (4 * r, cc), lambda i, chip_ref: (chip_ref[0] * nb + i, 0)
    return pl.pallas_call(
        body, name=name, out_shape=jax.ShapeDtypeStruct(full, BF16),
        grid_spec=pltpu.PrefetchScalarGridSpec(
            num_scalar_prefetch=1, grid=(nb,), in_specs=[pl.BlockSpec((tr, cc), lambda i, chip_ref: (i, 0))],
            out_specs=pl.BlockSpec((tr, cc), out_map)),
        compiler_params=_cp("parallel"),
    )(chip, w)


def mod_matmul(c_all, w_ada, b_ada):
    D, N = w_ada.shape
    tn = 1024

    def body(c_ref, w_ref, b_ref, o_ref):
        c = c_ref[...]
        sc = c * jax.nn.sigmoid(c)
        o_ref[...] = jnp.dot(sc, w_ref[...], precision=HI, preferred_element_type=F32) + b_ref[...]

    return pl.pallas_call(
        body, name="mod_matmul", out_shape=jax.ShapeDtypeStruct((8, N), F32), grid=(N // tn,),
        in_specs=[pl.BlockSpec((8, D), lambda j: (0, 0)), pl.BlockSpec((D, tn), lambda j: (0, j)),
                  pl.BlockSpec((1, tn), lambda j: (0, j))],
        out_specs=pl.BlockSpec((8, tn), lambda j: (0, j)), compiler_params=_cp("parallel"),
    )(c_all, w_ada, b_ada)


def prenorm(h, g, sc, sh):
    T, D = h.shape
    tm = min(256, T)

    def body(h_ref, g_ref, sc_ref, sh_ref, a_ref):
        x = h_ref[...]
        r = lax.rsqrt(jnp.mean(x * x, axis=-1, keepdims=True) + EPS)
        a_ref[...] = ((x * r) * g_ref[...] * (1.0 + sc_ref[...]) + sh_ref[...]).astype(BF16)

    row = pl.BlockSpec((tm, D), lambda i: (i, 0))
    return pl.pallas_call(
        body, name="prenorm", out_shape=jax.ShapeDtypeStruct((T, D), BF16), grid=(T // tm,),
        in_specs=[row, _vec(D), _vec(D), _vec(D)], out_specs=row, compiler_params=_cp("parallel"),
    )(h, g, sc, sh)


def in_proj_gathered(a, w_full, chip, dims):
    T, D = a.shape
    rows, cc = dims
    tm, tn = min(512, T), cc // 2
    ni = T // tm
    half = rows // 2

    def body(chip_ref, a_ref, w_in_ref, y_ref, w_ref, wbuf, wsem, send_sems, recv_sems):
        q, j, i = pl.program_id(0), pl.program_id(1), pl.program_id(2)
        mx, my, mc, _ = _place()
        me = chip_ref[0]

        def tile(block, jj):
            src = w_ref.at[:, pl.ds(pl.multiple_of(block * cc + jj * tn, 128), tn)]
            return pltpu.make_async_copy(src, wbuf.at[jj], wsem.at[jj])

        def rows_half(block, hh):
            return w_ref.at[pl.ds(pl.multiple_of(hh * half, 16), half), pl.ds(pl.multiple_of(block * cc, 128), cc)]

        def over_ici(s, block):
            peer = (1 - mx if s & 2 else mx, 1 - my if s & 1 else my, mc)
            reg = rows_half(block, mc)
            return pltpu.make_async_remote_copy(src_ref=reg, dst_ref=reg, send_sem=send_sems.at[s - 1], recv_sem=recv_sems.at[s - 1],
                                                device_id=peer, device_id_type=MESH)

        def over_d2d(s, block, hh):
            reg = rows_half(block, hh)
            return pltpu.make_async_remote_copy(src_ref=reg, dst_ref=reg, send_sem=send_sems.at[2 + s], recv_sem=recv_sems.at[2 + s],
                                                device_id=(mx, my, 1 - mc), device_id_type=MESH)

        @pl.when((q == 0) & (j == 0) & (i == 0))
        def _():
            for s in (1, 2, 3):
                over_ici(s, me).start()
            tile(me, 0).start()

        @pl.when(i == 0)
        def _():
            tile(me ^ q, j).wait()

        @pl.when((i == 0) & (j == 0))
        def _():
            tile(me ^ q, 1).start()

        y_ref[...] = _nn(a_ref[...], wbuf[j])

        for s in (1, 2, 3):
            @pl.when((q == s - 1) & (j == 1) & (i == ni - 1))
            def _():
                block = me ^ s
                over_ici(s, block).wait_recv()
                over_d2d(s, block, mc).start()
                over_d2d(s, block, 1 - mc).wait_recv()
                tile(block, 0).start()

        @pl.when((q == 3) & (j == 1) & (i == ni - 1))
        def _():
            for s in (1, 2, 3):
                over_ici(s, me).wait_send()
                over_d2d(s, me ^ s, mc).wait_send()

    y, w_out = pl.pallas_call(
        body, name="in_proj", out_shape=[jax.ShapeDtypeStruct((T, 4 * cc), F32), jax.ShapeDtypeStruct(w_full.shape, BF16)],
        grid_spec=pltpu.PrefetchScalarGridSpec(
            num_scalar_prefetch=1, grid=(4, 2, ni),
            in_specs=[pl.BlockSpec((tm, D), lambda q, j, i, chip_ref: (i, 0)), ANY],
            out_specs=[pl.BlockSpec((tm, tn), lambda q, j, i, chip_ref: (i, (chip_ref[0] ^ q) * 2 + j)), ANY],
            scratch_shapes=[pltpu.VMEM((2, D, tn), BF16), pltpu.SemaphoreType.DMA((2,)),
                            pltpu.SemaphoreType.DMA((6,)), pltpu.SemaphoreType.DMA((6,))]),
        input_output_aliases={2: 1}, compiler_params=_cp("arbitrary", "arbitrary", "arbitrary"),
    )(chip, a, w_full)
    return y, w_out


def prenorm_matmul(h, g, sc, sh, w, *, relu2, name, comm=None):
    T, D = h.shape
    N = w.shape[1]
    tm, tn = min(512, T), 2048 if N % 2048 == 0 else 1024

    def body(h_ref, g_ref, sc_ref, sh_ref, w_ref, y_ref, a_ref, *hid_ref):
        @pl.when(pl.program_id(1) == 0)
        def _():
            x = h_ref[...]
            r = lax.rsqrt(jnp.mean(x * x, axis=-1, keepdims=True) + EPS)
            a_ref[...] = ((x * r) * g_ref[...] * (1.0 + sc_ref[...]) + sh_ref[...]).astype(BF16)

        y = _nn(a_ref[...], w_ref[...])
        y_ref[...] = y
        if relu2:
            p = jnp.maximum(y, 0.0)
            hid_ref[0][...] = (p * p).astype(BF16)

    out_shape = [jax.ShapeDtypeStruct((T, N), F32), jax.ShapeDtypeStruct((T, D), BF16)]
    out_specs = [pl.BlockSpec((tm, tn), lambda i, j: (i, j)), pl.BlockSpec((tm, D), lambda i, j: (i, 0))]
    if relu2:
        out_shape.append(jax.ShapeDtypeStruct((T, N), BF16))
        out_specs.append(pl.BlockSpec((tm, tn), lambda i, j: (i, j)))
    outs, landed = _pallas(
        body, name=name, grid=(T // tm, N // tn),
        in_specs=[pl.BlockSpec((tm, D), lambda i, j: (i, 0)), _vec(D), _vec(D), _vec(D), pl.BlockSpec((D, tn), lambda i, j: (0, j))],
        out_specs=out_specs, out_shape=out_shape, scratch=[], semantics=("parallel", "arbitrary"), operands=(h, g, sc, sh, w), comm=comm)
    return outs if comm is None else (outs, landed)


def _hgrn_lower_bound(l_ref):
    l0, l1 = l_ref[0:1, :], l_ref[1:2, :]
    m = jnp.maximum(l0, l1)
    e0, e1 = jnp.exp(l0 - m), jnp.exp(l1 - m)
    return e0 / (e0 + e1)


def _hgrn_chunk_mask(d):
    r = lax.broadcasted_iota(jnp.int32, (HGRN_BLOCK, HGRN_BLOCK), 0)
    c = lax.broadcasted_iota(jnp.int32, (HGRN_BLOCK, HGRN_BLOCK), 1)
    same = (r // HGRN_CHUNK) == (c // HGRN_CHUNK)
    fwd = d == 0
    return same & (((c <= r) & fwd) | ((c >= r) & jnp.logical_not(fwd)))


def _chunk_total(x):
    x3 = x.reshape(HGRN_BLOCK // HGRN_CHUNK, HGRN_CHUNK, x.shape[1])
    return jnp.broadcast_to(jnp.sum(x3, axis=1, keepdims=True), x3.shape).reshape(x.shape)


def _chunk_cumsum(x, suffix):
    pos = lax.broadcasted_iota(jnp.int32, x.shape, 0) % HGRN_CHUNK
    p, s = x, 1
    while s < HGRN_CHUNK:
        p = p + jnp.where(pos >= s, pltpu.roll(p, s, 0), 0.0)
        s *= 2
    return jnp.where(suffix, _chunk_total(x) - p + x, p)


def _block_loop(T, body, init):
    n = T // HGRN_BLOCK
    return lax.fori_loop(0, n, body, init, unroll=2 if n % 2 == 0 else 1)


def _hgrn_gate(f, lb):
    s = jax.nn.sigmoid(f)
    sn = jax.nn.sigmoid(-f)
    fg = lb + (1.0 - lb) * s
    return s, sn, fg, jnp.log(fg), (1.0 - lb) * sn


def _hgrn_specs(T):
    col = lambda base: pl.BlockSpec((T, HEAD_DIM), lambda h, d: (0, base * N_HEADS + h))
    f_spec = pl.BlockSpec((T, HEAD_DIM), lambda h, d: (0, COL_FFW * N_HEADS + N_HEADS * d + h))
    l_spec = pl.BlockSpec((None, 2, HEAD_DIM), lambda h, d: (d, 0, h))
    return col, f_spec, l_spec


def hgrn_fwd(proj, lb_logits, comm=None):
    T = proj.shape[0]
    NC, CPB = T // HGRN_CHUNK, HGRN_BLOCK // HGRN_CHUNK
    col, f_spec, l_spec = _hgrn_specs(T)

    def body(l_ref, q_ref, f_ref, v_ref, o_ref, st_ref, dec_ref, qd_ref):
        d = pl.program_id(1)
        lb = _hgrn_lower_bound(l_ref)
        mask = _hgrn_chunk_mask(d)

        def block(i, carry):
            rows = pl.ds(pl.multiple_of(i * HGRN_BLOCK, HGRN_BLOCK), HGRN_BLOCK)
            _, _, _, lf, k = _hgrn_gate(f_ref[rows, :], lb)
            b = _chunk_cumsum(lf, d == 1)
            bl = _chunk_total(lf)
            qd = (q_ref[rows, :] * Q_SCALE * jnp.exp(b)).astype(BF16)
            kd = (k * jnp.exp(-b)).astype(BF16)
            ke = (k * jnp.exp(bl - b)).astype(BF16)
            vb = v_ref[rows, :].astype(BF16)
            att = jnp.where(mask, _nt(qd, kd), 0.0).astype(BF16)
            o_ref[rows, :] = jnp.where(d == 0, 0.0, o_ref[rows, :]) + _nn(att, vb)
            qd_ref[rows, :] = qd
            dec = jnp.exp(bl)
            for cc in range(CPB):
                sl = slice(cc * HGRN_CHUNK, (cc + 1) * HGRN_CHUNK)
                n = i * CPB + cc
                st_ref[n] = _tn(vb[sl], ke[sl])
                dec_ref[n] = dec[cc * HGRN_CHUNK:cc * HGRN_CHUNK + 8, :]
            return carry

        _block_loop(T, block, 0)

        def scan(t, s):
            n = jnp.where(d == 0, t, NC - 1 - t)
            u = st_ref[n]
            st_ref[n] = s
            return dec_ref[n][0:1, :] * s + u

        lax.fori_loop(0, NC, scan, jnp.zeros((HEAD_DIM, HEAD_DIM), F32))

        def inter(i, carry):
            rows = pl.ds(pl.multiple_of(i * HGRN_BLOCK, HGRN_BLOCK), HGRN_BLOCK)
            qd = qd_ref[rows, :]
            o_ref[rows, :] += jnp.concatenate(
                [_nt(qd[cc * HGRN_CHUNK:(cc + 1) * HGRN_CHUNK], st_ref[i * CPB + cc].astype(BF16)) for cc in range(CPB)], axis=0)
            return carry

        _block_loop(T, inter, 0)

    (o,), landed = _pallas(
        body, name="hgrn_fwd", grid=(N_HEADS, 2), in_specs=[l_spec, col(COL_Q), f_spec, col(COL_V)],
        out_specs=[pl.BlockSpec((T, HEAD_DIM), lambda h, d: (0, h))], out_shape=[jax.ShapeDtypeStruct((T, N_HEADS * HEAD_DIM), F32)],
        scratch=[pltpu.VMEM((NC, HEAD_DIM, HEAD_DIM), F32), pltpu.VMEM((NC, 8, HEAD_DIM), F32), pltpu.VMEM((T, HEAD_DIM), BF16)],
        semantics=("parallel", "arbitrary"), operands=(lb_logits, proj, proj, proj), comm=comm)
    return o if comm is None else (o, landed)


def hgrn_post_fwd(o, proj, g_norm):
    T, W = o.shape
    tm = min(256, T)

    def body(o_ref, og_ref, g_ref, y_ref):
        g = g_ref[...]
        for h in range(N_HEADS):
            sl = slice(h * HEAD_DIM, (h + 1) * HEAD_DIM)
            x = o_ref[:, sl]
            r = lax.rsqrt(jnp.mean(x * x, axis=-1, keepdims=True) + EPS)
            og = og_ref[:, sl]
            y_ref[:, sl] = ((x * r) * g * (og * jax.nn.sigmoid(og))).astype(BF16)

    return pl.pallas_call(
        body, name="hgrn_post_fwd", out_shape=jax.ShapeDtypeStruct((T, W), BF16), grid=(T // tm,),
        in_specs=[pl.BlockSpec((tm, W), lambda i: (i, 0)), pl.BlockSpec((tm, W), lambda i: (i, COL_OG)), _vec(HEAD_DIM)],
        out_specs=pl.BlockSpec((tm, W), lambda i: (i, 0)), compiler_params=_cp("parallel"),
    )(o, proj, g_norm)


def _gelu(x):
    return 0.5 * x * (1.0 + lax.erf(x * (1.0 / math.sqrt(2.0))))


def _gelu_grad(x):
    return 0.5 * (1.0 + lax.erf(x * (1.0 / math.sqrt(2.0)))) + x * jnp.exp(-0.5 * x * x) * (1.0 / math.sqrt(2.0 * math.pi))


def _sgu_mix(u_ref, v_ref, g_ref, ws_ref, bst_ref):
    W = u_ref.shape[1]
    zu, zv = _gelu(u_ref[...]), _gelu(v_ref[...])
    dv = zv - jnp.mean(zv, axis=-1, keepdims=True)
    rstd = lax.rsqrt(jnp.mean(dv * dv, axis=-1, keepdims=True) + EPS)
    dhat = dv * rstd
    vn = (dhat * g_ref[...]).astype(BF16)
    gw = W // SGU_GROUPS
    vm = [_nn(ws_ref[g].astype(BF16), vn[:, g * gw:(g + 1) * gw]) + bst_ref[:, g:g + 1] for g in range(SGU_GROUPS)]
    return zu, rstd, dhat, vn, jnp.concatenate(vm, axis=1)


def sgu_fwd(proj, g_norm, w_spatial, b_spatial_t):
    T = proj.shape[0]
    W = 1024
    n_chunks = T // SGU_CHUNK

    def body(u_ref, v_ref, g_ref, ws_ref, bst_ref, y_ref):
        zu, _, _, _, vm = _sgu_mix(u_ref, v_ref, g_ref, ws_ref, bst_ref)
        y_ref[...] = (zu * vm).astype(BF16)

    blk = lambda cb: pl.BlockSpec((SGU_CHUNK, W), lambda i: (i, cb))
    return pl.pallas_call(
        body, name="sgu_fwd", out_shape=jax.ShapeDtypeStruct((T, W), BF16), grid=(n_chunks,),
        in_specs=[blk(COL_U), blk(COL_ZV), _vec(W), pl.BlockSpec((SGU_GROUPS, SGU_CHUNK, SGU_CHUNK), lambda i: (0, 0, 0)),
                  pl.BlockSpec((SGU_CHUNK, SGU_GROUPS), lambda i: (0, 0))],
        out_specs=blk(0), compiler_params=_cp("parallel"),
    )(proj, proj, g_norm, w_spatial, b_spatial_t)


def merge_matmul(ya_pre, sgu, w_a, w_b, proj):
    T, K = ya_pre.shape
    N = w_a.shape[1]
    tm, tn = min(512, T), 512
    gpb = 1024 // tn

    def body(a_ref, b_ref, wa_ref, wb_ref, ga_ref, gb_ref, ya_ref, yb_ref, m_ref):
        ya = _nn(a_ref[...], wa_ref[...])
        yb = _nn(b_ref[...], wb_ref[...])
        ya_ref[...] = ya
        yb_ref[...] = yb
        m_ref[...] = (jax.nn.sigmoid(ga_ref[...]) * ya + jax.nn.sigmoid(gb_ref[...]) * yb).astype(BF16)

    lhs = pl.BlockSpec((tm, K), lambda i, j: (i, 0))
    rhs = pl.BlockSpec((K, tn), lambda i, j: (0, j))
    out = pl.BlockSpec((tm, tn), lambda i, j: (i, j))
    return pl.pallas_call(
        body, name="merge_matmul", grid=(T // tm, N // tn),
        out_shape=[jax.ShapeDtypeStruct((T, N), F32), jax.ShapeDtypeStruct((T, N), F32), jax.ShapeDtypeStruct((T, N), BF16)],
        in_specs=[lhs, lhs, rhs, rhs, pl.BlockSpec((tm, tn), lambda i, j: (i, COL_GA * gpb + j)),
                  pl.BlockSpec((tm, tn), lambda i, j: (i, COL_GB * gpb + j))],
        out_specs=[out, out, out], compiler_params=_cp("parallel", "parallel"),
    )(ya_pre, sgu, w_a, w_b, proj, proj)


def out_proj(merged, w_o, h0, gt1, g_post):
    T, D = h0.shape
    tm = min(256, T)

    def body(m_ref, w_ref, h_ref, gt_ref, gp_ref, mo_ref, h1_ref):
        mo = _nn(m_ref[...], w_ref[...])
        mo_ref[...] = mo
        r = lax.rsqrt(jnp.mean(mo * mo, axis=-1, keepdims=True) + EPS)
        h1_ref[...] = h_ref[...] + gt_ref[...] * ((mo * r) * gp_ref[...])

    row = pl.BlockSpec((tm, D), lambda i: (i, 0))
    return pl.pallas_call(
        body, name="out_proj", grid=(T // tm,),
        out_shape=[jax.ShapeDtypeStruct((T, D), F32), jax.ShapeDtypeStruct((T, D), F32)],
        in_specs=[row, pl.BlockSpec((D, D), lambda i: (0, 0)), row, _vec(D), _vec(D)],
        out_specs=[row, row], compiler_params=_cp("parallel"),
    )(merged, w_o, h0, gt1, g_post)


def ff2_loss(hid, w_ff2, h1, tgt, gt2, g_post):
    T, K = hid.shape
    D = w_ff2.shape[1]
    tm, tk = min(256, T), 2048
    nk = K // tk

    def body(a_ref, w_ref, h_ref, t_ref, gt_ref, g_ref, dy_ref, dff_ref, loss_ref, dgt_ref, dg_ref, acc_ref):
        i, k = pl.program_id(0), pl.program_id(1)

        @pl.when(k == 0)
        def _():
            acc_ref[...] = jnp.zeros_like(acc_ref)

        @pl.when((k == 0) & (i == 0))
        def _():
            loss_ref[...] = jnp.zeros_like(loss_ref)
            dgt_ref[...] = jnp.zeros_like(dgt_ref)
            dg_ref[...] = jnp.zeros_like(dg_ref)

        acc_ref[...] += _nn(a_ref[...], w_ref[...])

        @pl.when(k == nk - 1)
        def _():
            ff = acc_ref[...]
            gt, g = gt_ref[...], g_ref[...]
            r = lax.rsqrt(jnp.mean(ff * ff, axis=-1, keepdims=True) + EPS)
            fhat = ff * r
            nf = fhat * g
            err = (h_ref[...] + gt * nf) - t_ref[...]
            loss_ref[...] += jnp.sum(err * err)
            dy = err * (1.0 / D)
            dy_ref[...] = dy
            dgt_ref[...] += _colsum(dy * nf)
            dnf = dy * gt
            dg_ref[...] += _colsum(dnf * fhat)
            u = dnf * g
            dff_ref[...] = (r * (u - fhat * jnp.mean(u * fhat, axis=-1, keepdims=True))).astype(BF16)

    row = pl.BlockSpec((tm, D), lambda i, k: (i, 0))
    vec = pl.BlockSpec((1, D), lambda i, k: (0, 0))
    return pl.pallas_call(
        body, name="ff2_loss", grid=(T // tm, nk),
        out_shape=[jax.ShapeDtypeStruct((T, D), F32), jax.ShapeDtypeStruct((T, D), BF16), jax.ShapeDtypeStruct((8, 128), F32),
                   jax.ShapeDtypeStruct((1, D), F32), jax.ShapeDtypeStruct((1, D), F32)],
        in_specs=[pl.BlockSpec((tm, tk), lambda i, k: (i, k)), pl.BlockSpec((tk, D), lambda i, k: (k, 0)), row, row, vec, vec],
        out_specs=[row, row, pl.BlockSpec((8, 128), lambda i, k: (0, 0)), vec, vec],
        scratch_shapes=[pltpu.VMEM((tm, D), F32)], compiler_params=_cp("arbitrary", "arbitrary"),
    )(hid, w_ff2, h1, tgt, gt2, g_post)


def ff2_bwd(dff, w_ff2, f1):
    T, D = dff.shape
    K = w_ff2.shape[0]
    tm, tn = min(512, T), 2048

    def body(a_ref, w_ref, f_ref, o_ref):
        o_ref[...] = (_nt(a_ref[...], w_ref[...]) * (2.0 * jnp.maximum(f_ref[...], 0.0))).astype(BF16)

    return pl.pallas_call(
        body, name="ff2_bwd", out_shape=jax.ShapeDtypeStruct((T, K), BF16), grid=(K // tn, T // tm),
        in_specs=[pl.BlockSpec((tm, D), lambda j, i: (i, 0)), pl.BlockSpec((tn, D), lambda j, i: (j, 0)),
                  pl.BlockSpec((tm, tn), lambda j, i: (i, j))],
        out_specs=pl.BlockSpec((tm, tn), lambda j, i: (i, j)), compiler_params=_cp("parallel", "parallel"),
    )(dff, w_ff2, f1)


def ffn_norm_bwd(dy, da2, h1, mo, g_pre2, sc2, gt1, g_post):
    T, D = dy.shape
    tm = min(256, T)

    def body(dy_ref, da_ref, h_ref, mo_ref, g2_ref, sc_ref, gt_ref, gp_ref, dh_ref, dmo_ref, s_sh, s_sc, s_g2, s_gt, s_gp):
        @pl.when(pl.program_id(0) == 0)
        def _():
            for s in (s_sh, s_sc, s_g2, s_gt, s_gp):
                s[...] = jnp.zeros_like(s)

        h1, da = h_ref[...], da_ref[...]
        g2, sc = g2_ref[...], sc_ref[...]
        r2 = lax.rsqrt(jnp.mean(h1 * h1, axis=-1, keepdims=True) + EPS)
        n2 = h1 * r2
        s_sh[...] += _colsum(da)
        s_sc[...] += _colsum(da * (n2 * g2))
        s_g2[...] += _colsum(da * (1.0 + sc) * n2)
        dn2 = da * g2 * (1.0 + sc)
        dh1 = dy_ref[...] + r2 * (dn2 - n2 * jnp.mean(dn2 * n2, axis=-1, keepdims=True))
        dh_ref[...] = dh1
        mo = mo_ref[...]
        gt, gp = gt_ref[...], gp_ref[...]
        r = lax.rsqrt(jnp.mean(mo * mo, axis=-1, keepdims=True) + EPS)
        mhat = mo * r
        s_gt[...] += _colsum(dh1 * (mhat * gp))
        dnm = dh1 * gt
        s_gp[...] += _colsum(dnm * mhat)
        u = dnm * gp
        dmo_ref[...] = (r * (u - mhat * jnp.mean(u * mhat, axis=-1, keepdims=True))).astype(BF16)

    row = pl.BlockSpec((tm, D), lambda i: (i, 0))
    vec_out = jax.ShapeDtypeStruct((1, D), F32)
    return pl.pallas_call(
        body, name="ffn_norm_bwd", grid=(T // tm,),
        out_shape=[jax.ShapeDtypeStruct((T, D), F32), jax.ShapeDtypeStruct((T, D), BF16)] + [vec_out] * 5,
        in_specs=[row, row, row, row] + [_vec(D)] * 4, out_specs=[row, row] + [_vec(D)] * 5,
        compiler_params=_cp("arbitrary"),
    )(dy, da2, h1, mo, g_pre2, sc2, gt1, g_post)


def out_proj_bwd(dmo, w_o, y_a, y_b, proj):
    T, D = dmo.shape
    tm, tn = min(512, T), 512
    gpb = 1024 // tn

    def body(a_ref, w_ref, ya_ref, yb_ref, ga_ref, gb_ref, dya_ref, dyb_ref, dga_ref, dgb_ref):
        dm = _nt(a_ref[...], w_ref[...])
        sa, sb = jax.nn.sigmoid(ga_ref[...]), jax.nn.sigmoid(gb_ref[...])
        dya_ref[...] = (dm * sa).astype(BF16)
        dyb_ref[...] = (dm * sb).astype(BF16)
        dga_ref[...] = (dm * ya_ref[...] * sa * (1.0 - sa)).astype(BF16)
        dgb_ref[...] = (dm * yb_ref[...] * sb * (1.0 - sb)).astype(BF16)

    out = pl.BlockSpec((tm, tn), lambda i, j: (i, j))
    return pl.pallas_call(
        body, name="out_proj_bwd", grid=(T // tm, D // tn), out_shape=[jax.ShapeDtypeStruct((T, D), BF16)] * 4,
        in_specs=[pl.BlockSpec((tm, D), lambda i, j: (i, 0)), pl.BlockSpec((tn, D), lambda i, j: (j, 0)), out, out,
                  pl.BlockSpec((tm, tn), lambda i, j: (i, COL_GA * gpb + j)), pl.BlockSpec((tm, tn), lambda i, j: (i, COL_GB * gpb + j))],
        out_specs=[out] * 4, compiler_params=_cp("parallel", "parallel"),
    )(dmo, w_o, y_a, y_b, proj, proj)


def sgu_bwd(proj, dsgu, g_norm, w_spatial, b_spatial_t):
    T = proj.shape[0]
    W = 1024
    gw = W // SGU_GROUPS

    def body(u_ref, v_ref, ds_ref, g_ref, ws_ref, bst_ref, dz_ref, dw_ref, db_ref, dg_ref):
        @pl.when(pl.program_id(0) == 0)
        def _():
            dw_ref[...] = jnp.zeros_like(dw_ref)
            db_ref[...] = jnp.zeros_like(db_ref)
            dg_ref[...] = jnp.zeros_like(dg_ref)

        zu, rstd, dhat, vn, vm = _sgu_mix(u_ref, v_ref, g_ref, ws_ref, bst_ref)
        ds = ds_ref[...]
        du = ds * vm
        dvm = ds * zu
        dvm_b = dvm.astype(BF16)
        ones = jnp.ones((8, gw), F32)
        dvn = []
        for g in range(SGU_GROUPS):
            sl = slice(g * gw, (g + 1) * gw)
            dw_ref[g] += _nt(dvm_b[:, sl], vn[:, sl])
            db_ref[g] += lax.dot_general(ones, dvm[:, sl], (((1,), (1,)), ((), ())), precision=HI, preferred_element_type=F32)
            dvn.append(_tn(ws_ref[g].astype(BF16), dvm_b[:, sl]))
        dvn = jnp.concatenate(dvn, axis=1)
        dg_ref[...] += _colsum(dvn * dhat)
        ddh = dvn * g_ref[...]
        dzv = rstd * (ddh - jnp.mean(ddh, axis=-1, keepdims=True) - dhat * jnp.mean(ddh * dhat, axis=-1, keepdims=True))
        dz_ref[:, 0:W] = (du * _gelu_grad(u_ref[...])).astype(BF16)
        dz_ref[:, W:2 * W] = (dzv * _gelu_grad(v_ref[...])).astype(BF16)

    blk = lambda cb: pl.BlockSpec((SGU_CHUNK, W), lambda i: (i, cb))
    full3 = lambda a, b, c: pl.BlockSpec((a, b, c), lambda i: (0, 0, 0))
    return pl.pallas_call(
        body, name="sgu_bwd", grid=(T // SGU_CHUNK,),
        out_shape=[jax.ShapeDtypeStruct((T, 2 * W), BF16), jax.ShapeDtypeStruct((SGU_GROUPS, SGU_CHUNK, SGU_CHUNK), F32),
                   jax.ShapeDtypeStruct((SGU_GROUPS, 8, SGU_CHUNK), F32), jax.ShapeDtypeStruct((1, W), F32)],
        in_specs=[blk(COL_U), blk(COL_ZV), blk(0), _vec(W), full3(SGU_GROUPS, SGU_CHUNK, SGU_CHUNK),
                  pl.BlockSpec((SGU_CHUNK, SGU_GROUPS), lambda i: (0, 0))],
        out_specs=[pl.BlockSpec((SGU_CHUNK, 2 * W), lambda i: (i, 0)), full3(SGU_GROUPS, SGU_CHUNK, SGU_CHUNK),
                   full3(SGU_GROUPS, 8, SGU_CHUNK), _vec(W)],
        compiler_params=_cp("arbitrary"),
    )(proj, proj, dsgu, g_norm, w_spatial, b_spatial_t)


def hgrn_post_bwd(dya, o, proj, g_norm):
    T, W = o.shape
    tm = min(256, T)

    def body(dy_ref, o_ref, og_ref, g_ref, do_ref, dog_ref, dg_ref):
        @pl.when(pl.program_id(0) == 0)
        def _():
            dg_ref[...] = jnp.zeros_like(dg_ref)

        g = g_ref[...]
        dg = jnp.zeros((1, HEAD_DIM), F32)
        for h in range(N_HEADS):
            sl = slice(h * HEAD_DIM, (h + 1) * HEAD_DIM)
            x, og, dy = o_ref[:, sl], og_ref[:, sl], dy_ref[:, sl]
            r = lax.rsqrt(jnp.mean(x * x, axis=-1, keepdims=True) + EPS)
            xhat = x * r
            s = jax.nn.sigmoid(og)
            don = dy * (og * s)
            dog_ref[:, sl] = (dy * (xhat * g) * (s * (1.0 + og * (1.0 - s)))).astype(BF16)
            dg += _colsum(don * xhat)
            u = don * g
            do_ref[:, sl] = r * (u - xhat * jnp.mean(u * xhat, axis=-1, keepdims=True))
        dg_ref[...] += dg

    row = pl.BlockSpec((tm, W), lambda i: (i, 0))
    return pl.pallas_call(
        body, name="hgrn_post_bwd", grid=(T // tm,),
        out_shape=[jax.ShapeDtypeStruct((T, W), F32), jax.ShapeDtypeStruct((T, W), BF16), jax.ShapeDtypeStruct((1, HEAD_DIM), F32)],
        in_specs=[row, row, pl.BlockSpec((tm, W), lambda i: (i, COL_OG)), _vec(HEAD_DIM)],
        out_specs=[row, row, _vec(HEAD_DIM)], compiler_params=_cp("arbitrary"),
    )(dya, o, proj, g_norm)


def hgrn_bwd(proj, do, lb_logits, comm=None):
    T = proj.shape[0]
    NC, CPB = T // HGRN_CHUNK, HGRN_BLOCK // HGRN_CHUNK
    W = N_HEADS * HEAD_DIM
    col, f_spec, l_spec = _hgrn_specs(T)

    def body(l_ref, q_ref, f_ref, v_ref, do_ref, dq_ref, dv_ref, dlg_ref, dlb_ref, st_ref, dst_ref, dec_ref, ddec_ref, dqa_ref, dva_ref):
        d = pl.program_id(1)
        lb = _hgrn_lower_bound(l_ref)
        oml = 1.0 - lb
        mask = _hgrn_chunk_mask(d)

        def values(rows):
            s, sn, fg, lf, k = _hgrn_gate(f_ref[rows, :], lb)
            b = _chunk_cumsum(lf, d == 1)
            bl = _chunk_total(lf)
            eb, enb, ee = jnp.exp(b), jnp.exp(-b), jnp.exp(bl - b)
            qd = q_ref[rows, :] * Q_SCALE * eb
            return s, sn, fg, k, bl, eb, enb, ee, qd, k * enb, k * ee

        def block1(i, carry):
            rows = pl.ds(pl.multiple_of(i * HGRN_BLOCK, HGRN_BLOCK), HGRN_BLOCK)
            _, _, _, _, bl, _, _, _, qd, _, ke = values(rows)
            qd, ke = qd.astype(BF16), ke.astype(BF16)
            vb, dob = v_ref[rows, :].astype(BF16), do_ref[rows, :].astype(BF16)
            dec = jnp.exp(bl)
            for cc in range(CPB):
                sl = slice(cc * HGRN_CHUNK, (cc + 1) * HGRN_CHUNK)
                n = i * CPB + cc
                st_ref[n] = _tn(vb[sl], ke[sl])
                dst_ref[n] = _tn(dob[sl], qd[sl])
                dec_ref[n] = dec[cc * HGRN_CHUNK:cc * HGRN_CHUNK + 8, :]
            return carry

        _block_loop(T, block1, 0)

        def scan(t, s):
            n = jnp.where(d == 0, t, NC - 1 - t)
            u = st_ref[n]
            st_ref[n] = s
            return dec_ref[n][0:1, :] * s + u

        lax.fori_loop(0, NC, scan, jnp.zeros((HEAD_DIM, HEAD_DIM), F32))

        def rscan(t, ds):
            n = jnp.where(d == 0, NC - 1 - t, t)
            w = dst_ref[n]
            dst_ref[n] = ds
            ddec_ref[n] = jnp.broadcast_to(_colsum(ds * st_ref[n]), (8, HEAD_DIM))
            return dec_ref[n][0:1, :] * ds + w

        lax.fori_loop(0, NC, rscan, jnp.zeros((HEAD_DIM, HEAD_DIM), F32))

        def block3(i, dlb):
            rows = pl.ds(pl.multiple_of(i * HGRN_BLOCK, HGRN_BLOCK), HGRN_BLOCK)
            s, sn, fg, k, bl, eb, enb, ee, qd, kd, ke = values(rows)
            qdb, kdb, keb = qd.astype(BF16), kd.astype(BF16), ke.astype(BF16)
            vb, dob = v_ref[rows, :].astype(BF16), do_ref[rows, :].astype(BF16)
            att = jnp.where(mask, _nt(qdb, kdb), 0.0).astype(BF16)
            datt = jnp.where(mask, _nt(dob, vb), 0.0).astype(BF16)
            dv = _tn(att, dob)
            dqd = _nn(datt, kdb)
            dkd = _tn(datt, qdb)
            dv_i, dqd_i, dke, ddl = [], [], [], []
            for cc in range(CPB):
                sl = slice(cc * HGRN_CHUNK, (cc + 1) * HGRN_CHUNK)
                n = i * CPB + cc
                st_b, dst_b = st_ref[n].astype(BF16), dst_ref[n].astype(BF16)
                dv_i.append(_nt(keb[sl], dst_b))
                dqd_i.append(_nn(dob[sl], st_b))
                dke.append(_nn(vb[sl], dst_b))
                ddl.append(jnp.broadcast_to(ddec_ref[n][0:1, :] * dec_ref[n][0:1, :], (HGRN_CHUNK, HEAD_DIM)))
            dv = dv + jnp.concatenate(dv_i, axis=0)
            dqd = dqd + jnp.concatenate(dqd_i, axis=0)
            dke = jnp.concatenate(dke, axis=0)
            dq = dqd * eb * Q_SCALE
            dk = dkd * enb + dke * ee
            t_end = dke * ke
            db = dqd * qd - dkd * kd - t_end
            dlf = _chunk_cumsum(db, d == 0) + _chunk_total(t_end) + jnp.concatenate(ddl, axis=0)
            e = dlf / fg - dk
            dlg_ref[rows, :] = (oml * e * s * sn).astype(BF16)

            dq = jnp.where(d == 0, 0.0, dqa_ref[rows, :]) + dq
            dv = jnp.where(d == 0, 0.0, dva_ref[rows, :]) + dv
            dqa_ref[rows, :] = dq
            dva_ref[rows, :] = dv
            dq_ref[rows, :] = dq.astype(BF16)
            dv_ref[rows, :] = dv.astype(BF16)

            return dlb + _colsum(e * sn)

        dlb_ref[...] = _block_loop(T, block3, jnp.zeros((1, HEAD_DIM), F32))

    head = pl.BlockSpec((T, HEAD_DIM), lambda h, d: (0, h))
    big = pltpu.VMEM((NC, HEAD_DIM, HEAD_DIM), F32)
    small = pltpu.VMEM((NC, 8, HEAD_DIM), F32)
    acc = pltpu.VMEM((T, HEAD_DIM), F32)
    outs, landed = _pallas(
        body, name="hgrn_bwd", grid=(N_HEADS, 2),
        out_shape=[jax.ShapeDtypeStruct((T, W), BF16), jax.ShapeDtypeStruct((T, W), BF16), jax.ShapeDtypeStruct((T, 2 * W), BF16),
                   jax.ShapeDtypeStruct((2, 1, W), F32)],
        in_specs=[l_spec, col(COL_Q), f_spec, col(COL_V), head],
        out_specs=[head, head, pl.BlockSpec((T, HEAD_DIM), lambda h, d: (0, N_HEADS * d + h)),
                   pl.BlockSpec((None, 1, HEAD_DIM), lambda h, d: (d, 0, h))],
        scratch=[big, big, small, small, acc, acc], semantics=("parallel", "arbitrary"), operands=(lb_logits, proj, proj, proj, do), comm=comm)
    return outs if comm is None else (outs, landed)


def mix_norm_bwd(da1, h0, dh1, g_pre, sc1):
    T, D = h0.shape
    tm = min(256, T)

    def body(da_ref, h_ref, dh_ref, g_ref, sc_ref, gx_ref, s_sh, s_sc, s_g):
        @pl.when(pl.program_id(0) == 0)
        def _():
            for s in (s_sh, s_sc, s_g):
                s[...] = jnp.zeros_like(s)

        h, da = h_ref[...], da_ref[...]
        g, sc = g_ref[...], sc_ref[...]
        r = lax.rsqrt(jnp.mean(h * h, axis=-1, keepdims=True) + EPS)
        n = h * r
        s_sh[...] += _colsum(da)
        s_sc[...] += _colsum(da * (n * g))
        s_g[...] += _colsum(da * (1.0 + sc) * n)
        dn = da * g * (1.0 + sc)
        gx_ref[...] = dh_ref[...] + r * (dn - n * jnp.mean(dn * n, axis=-1, keepdims=True))

    row = pl.BlockSpec((tm, D), lambda i: (i, 0))
    return pl.pallas_call(
        body, name="mix_norm_bwd", grid=(T // tm,),
        out_shape=[jax.ShapeDtypeStruct((T, D), F32)] + [jax.ShapeDtypeStruct((1, D), F32)] * 3,
        in_specs=[row, row, row, _vec(D), _vec(D)], out_specs=[row] + [_vec(D)] * 3, compiler_params=_cp("arbitrary"),
    )(da1, h0, dh1, g_pre, sc1)


def adamw(w, g, m, v, name):
    R, C = w.shape
    tr = R if R * C * 4 <= (1 << 21) else max(8, ((1 << 21) // (C * 4)) // 8 * 8)
    while R % tr:
        tr -= 8

    def body(w_ref, g_ref, m_ref, v_ref, d_ref, m2_ref, v2_ref):
        d_ref[...], m2_ref[...], v2_ref[...] = _adamw(w_ref[...], g_ref[...], m_ref[...], v_ref[...])

    row = pl.BlockSpec((tr, C), lambda i: (i, 0))
    return pl.pallas_call(
        body, name=name, grid=(R // tr,), out_shape=[jax.ShapeDtypeStruct((R, C), F32)] * 3,
        in_specs=[row] * 4, out_specs=[row] * 3, compiler_params=_cp("parallel"),
    )(w, g, m, v)


def wada_update(c_all, dmod, w, m, v):
    D, N = w.shape
    tm, tn = 512, 1024

    def body(c_ref, dm_ref, w_ref, m_ref, v_ref, g_ref, d_ref, m2_ref, v2_ref):
        c = c_ref[...]
        g = lax.dot_general(c * jax.nn.sigmoid(c), dm_ref[...], (((0,), (0,)), ((), ())), precision=HI, preferred_element_type=F32)
        g_ref[...] = g
        d_ref[...], m2_ref[...], v2_ref[...] = _adamw(w_ref[...], g, m_ref[...], v_ref[...])

    blk = pl.BlockSpec((tm, tn), lambda i, j: (i, j))
    return pl.pallas_call(
        body, name="wada_update", grid=(D // tm, N // tn), out_shape=[jax.ShapeDtypeStruct((D, N), F32)] * 4,
        in_specs=[pl.BlockSpec((8, tm), lambda i, j: (0, i)), pl.BlockSpec((8, tn), lambda i, j: (0, j)), blk, blk, blk],
        out_specs=[blk] * 4, compiler_params=_cp("parallel", "parallel"),
    )(c_all, dmod, w, m, v)


def sum_devices(gathered, name):
    n, R, C = gathered.shape

    def body(g_ref, o_ref):
        s = g_ref[0]
        for i in range(1, n):
            s = s + g_ref[i]
        o_ref[...] = s

    return pl.pallas_call(body, name=name, out_shape=jax.ShapeDtypeStruct((R, C), F32), compiler_params=_cp())(gathered)


def lb_logits_grad(dlb, lb_logits):
    def body(d_ref, l_ref, o_ref):
        for d in range(2):
            l0, l1 = l_ref[d, 0:1, :], l_ref[d, 1:2, :]
            m = jnp.maximum(l0, l1)
            e0, e1 = jnp.exp(l0 - m), jnp.exp(l1 - m)
            p0, p1 = e0 / (e0 + e1), e1 / (e0 + e1)
            g = d_ref[d:d + 1, :]
            o_ref[d, 0:1, :] = p0 * (g - p0 * g)
            o_ref[d, 1:2, :] = -p1 * (p0 * g)

    return pl.pallas_call(body, name="lb_logits_grad", out_shape=jax.ShapeDtypeStruct(lb_logits.shape, F32), compiler_params=_cp())(dlb, lb_logits)


def add_halves(g, landed, core):
    nj, _, r, cc = g.shape
    tr = min(256, r)

    def body(core_ref, g_ref, l_ref, o_ref):
        o_ref[...] = (g_ref[...].astype(F32) + l_ref[...].astype(F32)).astype(BF16)

    return pl.pallas_call(
        body, name="add_halves_%dx%d" % (r, cc), out_shape=jax.ShapeDtypeStruct((nj, r, cc), BF16),
        grid_spec=pltpu.PrefetchScalarGridSpec(
            num_scalar_prefetch=1, grid=(nj, r // tr),
            in_specs=[pl.BlockSpec((None, None, tr, cc), lambda j, i, core_ref: (j, core_ref[0], i, 0)),
                      pl.BlockSpec((None, None, tr, cc), lambda j, i, core_ref: (j, 0, i, 0))],
            out_specs=pl.BlockSpec((None, tr, cc), lambda j, i, core_ref: (j, i, 0))),
        compiler_params=_cp("parallel", "parallel"),
    )(core, g, landed)


def sum_chips(parts, landed, chip):
    nj, r, cc = parts.shape
    tr = min(256, r)

    def body(chip_ref, p_ref, l_ref, o_ref):
        mine = p_ref[...].astype(F32)
        s = None
        for j in range(nj):
            t = jnp.where(chip_ref[0] == j, mine, l_ref[j].astype(F32))
            s = t if s is None else s + t
        o_ref[...] = s

    return pl.pallas_call(
        body, name="sum_chips_%dx%d" % (r, cc), out_shape=jax.ShapeDtypeStruct((r, cc), F32),
        grid_spec=pltpu.PrefetchScalarGridSpec(
            num_scalar_prefetch=1, grid=(r // tr,),
            in_specs=[pl.BlockSpec((None, tr, cc), lambda i, chip_ref: (chip_ref[0], i, 0)),
                      pl.BlockSpec((nj, tr, cc), lambda i, chip_ref: (0, i, 0))],
            out_specs=pl.BlockSpec((tr, cc), lambda i, chip_ref: (i, 0))),
        compiler_params=_cp("parallel"),
    )(chip, parts, landed)


def adamw_halves(w, own, other, m, v, core, name):
    r, cc = own.shape
    tr = min(128, r)
    nb = r // tr

    def body(core_ref, w_ref, a_ref, b_ref, m_ref, v_ref, g_ref, d_ref, m2_ref, v2_ref):
        g = jnp.where(pl.program_id(0) == core_ref[0], a_ref[...], b_ref[...])
        g_ref[...] = g
        d_ref[...], m2_ref[...], v2_ref[...] = _adamw(w_ref[...], g, m_ref[...], v_ref[...])

    full = pl.BlockSpec((tr, cc), lambda h, i, core_ref: (h * nb + i, 0))
    half = pl.BlockSpec((tr, cc), lambda h, i, core_ref: (i, 0))
    return pl.pallas_call(
        body, name=name, out_shape=[jax.ShapeDtypeStruct((2 * r, cc), F32)] * 4,
        grid_spec=pltpu.PrefetchScalarGridSpec(
            num_scalar_prefetch=1, grid=(2, nb), in_specs=[full, half, half, full, full], out_specs=[full] * 4),
        compiler_params=_cp("parallel", "parallel"),
    )(core, w, own, other, m, v)


def _place():
    mx, my, mc = lax.axis_index("x"), lax.axis_index("y"), lax.axis_index("c")
    chips = [(1 - mx, my), (mx, 1 - my), (1 - mx, 1 - my)]
    return mx, my, mc, chips


def all_gather_small(x, name):
    R, C = x.shape

    def body(x_ref, out_ref, send_sems, recv_sems, local_sem):
        mx, my, mc, _ = _place()
        me = 4 * mx + 2 * my + mc
        mine = pltpu.make_async_copy(x_ref, out_ref.at[me], local_sem)
        mine.start()

        def peer(k):
            px = 1 - mx if k & 4 else mx
            py = 1 - my if k & 2 else my
            pc = 1 - mc if k & 1 else mc
            return px, py, pc

        def copy(k, src, slot):
            return pltpu.make_async_remote_copy(src_ref=src, dst_ref=out_ref.at[slot], send_sem=send_sems.at[k - 1],
                                                recv_sem=recv_sems.at[k - 1], device_id=peer(k), device_id_type=MESH)

        sends = [copy(k, x_ref, me) for k in range(1, 8)]
        for cp in sends:
            cp.start()
        for k in range(1, 8):
            px, py, pc = peer(k)
            slot = 4 * px + 2 * py + pc
            copy(k, out_ref.at[slot], slot).wait_recv()
        for cp in sends:
            cp.wait_send()
        mine.wait()

    return pl.pallas_call(
        body, name=name, out_shape=jax.ShapeDtypeStruct((8, R, C), F32),
        in_specs=[pl.BlockSpec(memory_space=pltpu.VMEM)], out_specs=pl.BlockSpec(memory_space=pltpu.VMEM),
        scratch_shapes=[pltpu.SemaphoreType.DMA((7,)), pltpu.SemaphoreType.DMA((7,)), pltpu.SemaphoreType.DMA],
        compiler_params=_cp(),
    )(x)


def gather8_comm(x):
    def copies(x_ref, out_ref, send_sems, recv_sems):
        mx, my, mc, _ = _place()
        me = 4 * mx + 2 * my + mc

        def peer(k):
            return (1 - mx if k & 4 else mx, 1 - my if k & 2 else my, 1 - mc if k & 1 else mc)

        def copy(k, src, slot):
            return pltpu.make_async_remote_copy(src_ref=src, dst_ref=out_ref.at[slot], send_sem=send_sems.at[k - 1],
                                                recv_sem=recv_sems.at[k - 1], device_id=peer(k), device_id_type=MESH)

        sends = [copy(k, x_ref, me) for k in range(1, 8)]
        arrivals = []
        for k in range(1, 8):
            px, py, pc = peer(k)
            slot = 4 * px + 2 * py + pc
            arrivals.append(copy(k, out_ref.at[slot], slot))
        return sends, arrivals, pltpu.make_async_copy(x_ref, out_ref.at[me], send_sems.at[7])

    def start(cin, cout, send_sems, recv_sems):
        sends, _, mine = copies(cin[0], cout[0], send_sems, recv_sems)
        mine.start()
        for cp in sends:
            cp.start()

    def finish(cin, cout, send_sems, recv_sems):
        sends, arrivals, mine = copies(cin[0], cout[0], send_sems, recv_sems)
        for cp in arrivals:
            cp.wait_recv()
        for cp in sends:
            cp.wait_send()
        mine.wait()

    return _Comm([x], [jax.ShapeDtypeStruct((8,) + x.shape, F32)], {}, 8, start, finish)


def _join(a, b):
    na_in, na_out = len(a.operands), len(a.out_shape)

    def split(fn_a, fn_b):
        def both(cin, cout, send_sems, recv_sems):
            fn_a(cin[:na_in], cout[:na_out], send_sems.at[pl.ds(0, a.n_sems)], recv_sems.at[pl.ds(0, a.n_sems)])
            fn_b(cin[na_in:], cout[na_out:], send_sems.at[pl.ds(a.n_sems, b.n_sems)], recv_sems.at[pl.ds(a.n_sems, b.n_sems)])
        return both

    aliases = dict(a.aliases)
    aliases.update({na_in + i: na_out + o for i, o in b.aliases.items()})
    return _Comm(a.operands + b.operands, a.out_shape + b.out_shape, aliases, a.n_sems + b.n_sems, split(a.start, b.start), split(a.finish, b.finish))


def _region(ref, kind, j, half, r, cc):
    nr = r if half is None else r // 2
    off = 0 if half is None else half * nr
    if kind == "col":
        return ref.at[pl.ds(off, nr), pl.ds(pl.multiple_of(j * cc, 128), cc)]
    return ref.at[pl.ds(pl.multiple_of(j * r + off, 16), nr), :]


def comm_call(comm, name):
    ni, no = len(comm.operands), len(comm.out_shape)

    def body(*refs):
        comm.start(refs[:ni], refs[ni:ni + no], *refs[ni + no:])
        comm.finish(refs[:ni], refs[ni:ni + no], *refs[ni + no:])

    return pl.pallas_call(
        body, name=name, out_shape=comm.out_shape, in_specs=[ANY] * ni, out_specs=[ANY] * no, input_output_aliases=comm.aliases,
        scratch_shapes=[pltpu.SemaphoreType.DMA((comm.n_sems,)), pltpu.SemaphoreType.DMA((comm.n_sems,))], compiler_params=_cp(),
    )(*comm.operands)


def gather_comm(fulls, kinds, dims):
    n = len(fulls)

    def copies(f_refs, send_sems, recv_sems):
        mx, my, mc, chips = _place()
        jme = 2 * mx + my

        def landed(w, k, half):
            px, py = chips[k]
            return _region(f_refs[w], kinds[w], 2 * px + py, half, *dims[w])

        def over_ici(w, k, reg):
            px, py = chips[k]
            return pltpu.make_async_remote_copy(src_ref=reg, dst_ref=reg, send_sem=send_sems.at[6 * w + k], recv_sem=recv_sems.at[6 * w + k],
                                                device_id=(px, py, mc), device_id_type=MESH)

        def over_d2d(w, k, half):
            reg = landed(w, k, half)
            return pltpu.make_async_remote_copy(src_ref=reg, dst_ref=reg, send_sem=send_sems.at[6 * w + 3 + k],
                                                recv_sem=recv_sems.at[6 * w + 3 + k], device_id=(mx, my, 1 - mc), device_id_type=MESH)

        sends = [over_ici(w, k, _region(f_refs[w], kinds[w], jme, mc, *dims[w])) for w in range(n) for k in range(3)]
        return mc, landed, over_ici, over_d2d, sends

    def start(cin, f_refs, send_sems, recv_sems):
        for cp in copies(f_refs, send_sems, recv_sems)[4]:
            cp.start()

    def finish(cin, f_refs, send_sems, recv_sems):
        mc, landed, over_ici, over_d2d, sends = copies(f_refs, send_sems, recv_sems)
        passed = []
        for w in range(n):
            for k in range(3):
                over_ici(w, k, landed(w, k, mc)).wait_recv()
                cp = over_d2d(w, k, mc)
                cp.start()
                passed.append(cp)
        for w in range(n):
            for k in range(3):
                over_d2d(w, k, 1 - mc).wait_recv()
        for cp in sends + passed:
            cp.wait_send()

    return _Comm(fulls, [jax.ShapeDtypeStruct(f.shape, BF16) for f in fulls], {w: w for w in range(n)}, 6 * n, start, finish)


def exchange_halves(grads, name):
    n = len(grads)

    def body(*refs):
        g_refs, l_refs = refs[:n], refs[n:2 * n]
        send_sems, recv_sems = refs[2 * n:]
        mx, my, mc, _ = _place()
        cps = [pltpu.make_async_remote_copy(src_ref=g_refs[w].at[:, pl.ds(1 - mc, 1)], dst_ref=l_refs[w], send_sem=send_sems.at[w],
                                            recv_sem=recv_sems.at[w], device_id=(mx, my, 1 - mc), device_id_type=MESH) for w in range(n)]
        for cp in cps:
            cp.start()
        for cp in cps:
            cp.wait()

    return pl.pallas_call(
        body, name=name, out_shape=[jax.ShapeDtypeStruct((g.shape[0], 1) + g.shape[2:], BF16) for g in grads],
        in_specs=[ANY] * n, out_specs=[ANY] * n,
        scratch_shapes=[pltpu.SemaphoreType.DMA((n,)), pltpu.SemaphoreType.DMA((n,))], compiler_params=_cp(),
    )(*grads)


def scatter_comm(parts):
    n = len(parts)

    def sends(p_refs, l_refs, send_sems, recv_sems):
        mx, my, mc, chips = _place()
        return [pltpu.make_async_remote_copy(src_ref=p_refs[w].at[2 * px + py], dst_ref=l_refs[w].at[2 * mx + my],
                                             send_sem=send_sems.at[3 * w + k], recv_sem=recv_sems.at[3 * w + k],
                                             device_id=(px, py, mc), device_id_type=MESH) for w in range(n) for k, (px, py) in enumerate(chips)]

    def start(p_refs, l_refs, send_sems, recv_sems):
        for cp in sends(p_refs, l_refs, send_sems, recv_sems):
            cp.start()

    def finish(p_refs, l_refs, send_sems, recv_sems):
        mx, my, mc, chips = _place()
        for w in range(n):
            for k, (px, py) in enumerate(chips):
                slot = l_refs[w].at[2 * px + py]
                pltpu.make_async_remote_copy(src_ref=slot, dst_ref=slot, send_sem=send_sems.at[3 * w + k], recv_sem=recv_sems.at[3 * w + k],
                                             device_id=(px, py, mc), device_id_type=MESH).wait_recv()
        for cp in sends(p_refs, l_refs, send_sems, recv_sems):
            cp.wait_send()

    return _Comm(parts, [jax.ShapeDtypeStruct(p.shape, BF16) for p in parts], {}, 3 * n, start, finish)


def share_comm(sums):
    n = len(sums)

    def copies(q_refs, o_refs, send_sems, recv_sems):
        mx, my, mc, _ = _place()
        return [pltpu.make_async_remote_copy(src_ref=q_refs[w], dst_ref=o_refs[w], send_sem=send_sems.at[w], recv_sem=recv_sems.at[w],
                                             device_id=(mx, my, 1 - mc), device_id_type=MESH) for w in range(n)]

    def start(*refs):
        for cp in copies(*refs):
            cp.start()

    def finish(*refs):
        for cp in copies(*refs):
            cp.wait()

    return _Comm(sums, [jax.ShapeDtypeStruct(q.shape, F32) for q in sums], {}, n, start, finish)


def _pack(arrays):
    flat = jnp.concatenate([a.reshape(-1) for a in arrays])
    rows = -(-flat.shape[0] // 1024) * 8
    return jnp.pad(flat, (0, rows * 128 - flat.shape[0])).reshape(rows, 128)


def _unpack(packed, shapes):
    flat, out, off = packed.reshape(-1), [], 0
    for s in shapes:
        n = math.prod(s)
        out.append(flat[off:off + n].reshape(s))
        off += n
    return out


def kernel(x, c, w_ada, b_ada, g_pre_mix, g_post_mix, g_pre_ffn, g_post_ffn, w_in, lb_logits, g_hgrn_norm, w_a_out, g_sgu_norm, w_spatial, b_spatial, w_b_out, w_o, w_ff1, w_ff2, loss_target, m_w_ada, m_b_ada, m_g_pre_mix, m_g_post_mix, m_g_pre_ffn, m_g_post_ffn, m_w_in, m_lb_logits, m_g_hgrn_norm, m_w_a_out, m_g_sgu_norm, m_w_spatial, m_b_spatial, m_w_b_out, m_w_o, m_w_ff1, m_w_ff2, v_w_ada, v_b_ada, v_g_pre_mix, v_g_post_mix, v_g_pre_ffn, v_g_post_ffn, v_w_in, v_lb_logits, v_g_hgrn_norm, v_w_a_out, v_g_sgu_norm, v_w_spatial, v_b_spatial, v_w_b_out, v_w_o, v_w_ff1, v_w_ff2):
    mx, my, mc = lax.axis_index("x"), lax.axis_index("y"), lax.axis_index("c")
    chip, me = 2 * mx + my, 4 * mx + 2 * my + mc
    D = D_MODEL
    h0, tgt = x[0], loss_target[0]
    n_ada = w_ada.shape[2]
    n_lb = lb_logits.shape[2]

    got = all_gather_small(_pack([c, lb_logits]), "gather_inputs")
    c_all = got[:, :D // 128, :].reshape(8, D)
    lb_full = got[0::2, D // 128:D // 128 + 4 * n_lb // 128, :].reshape(4, 2, 2, n_lb).transpose(1, 2, 0, 3).reshape(2, 2, 4 * n_lb)
    b_ada_chip = lax.dynamic_slice(b_ada, (0, chip * n_ada), (1, n_ada))
    mod_cols = mod_matmul(c_all, w_ada[0], b_ada_chip)
    got = all_gather_small(mod_cols.reshape(-1, 128), "gather_mod").reshape(4, 2, 8, n_ada)
    mod = lax.dynamic_index_in_dim(got[:, 0], me, axis=1, keepdims=False).reshape(6, 1, D)
    sh1, sc1, gt1, sh2, sc2, gt2 = (mod[i] for i in range(6))

    big = [("w_in", w_in, "col"), ("w_a_out", w_a_out, "col"), ("w_b_out", w_b_out, "col"), ("w_o", w_o, "row"),
           ("w_ff1", w_ff1, "col"), ("w_ff2", w_ff2, "row")]
    kinds = [k for _, _, k in big]
    chip_idx, core = chip.reshape(1).astype(jnp.int32), mc.reshape(1).astype(jnp.int32)
    fulls = [cast_into_full(w[0], kind, chip_idx, "cast_" + nm) for nm, w, kind in big]
    dims = [w.shape[1:] for _, w, _ in big]
    later = lambda lo, hi: gather_comm(fulls[lo:hi], kinds[lo:hi], dims[lo:hi])
    halves_summed = lambda grads, name: [add_halves(g, l, core) for g, l in zip(grads, exchange_halves(grads, name))]

    bst = b_spatial[0].T
    a1 = prenorm(h0, g_pre_mix, sc1, sh1)
    proj, w_in_f = in_proj_gathered(a1, fulls[0], chip_idx, dims[0])
    o, (w_a_f, w_b_f, w_o_f, w_ff1_f) = hgrn_fwd(proj, lb_full, comm=later(1, 5))
    ya_pre = hgrn_post_fwd(o, proj, g_hgrn_norm)
    sgu = sgu_fwd(proj, g_sgu_norm, w_spatial[0], bst)
    y_a, y_b, merged = merge_matmul(ya_pre, sgu, w_a_f, w_b_f, proj)
    mo, h1 = out_proj(merged, w_o_f, h0, gt1, g_post_mix)
    (f1, a2, hid), (w_ff2_f,) = prenorm_matmul(h1, g_pre_ffn, sc2, sh2, w_ff1_f, relu2=True, name="ff1", comm=later(5, 6))
    dy, dff, loss_parts, d_gt2, d_g_post_ffn = ff2_loss(hid, w_ff2_f, h1, tgt, gt2, g_post_ffn)
    loss = lax.psum(0.5 * loss_parts[0, 0] / D, ("x", "y", "c"))

    df1 = ff2_bwd(dff, w_ff2_f, f1)
    gr_ff2 = matmul(hid, dff, mode="tn", out_dtype=BF16, tm=1024, tn=1024, tk=2048, name="dw_ff2")
    da2 = matmul(df1, w_ff1_f, mode="nt", out_dtype=F32, tm=1024, tn=1024, tk=2048, name="da2")
    gr_ff1 = matmul(a2, df1, mode="tn", out_dtype=BF16, tm=1024, tn=2048, tk=1024, name="dw_ff1", split=(4, 2))
    parts_ff = halves_summed([gr_ff1, gr_ff2.reshape(4, 2, -1, D)], "exchange_ff")
    dh1, dmo, d_sh2, d_sc2, d_g_pre_ffn, d_gt1, d_g_post_mix = ffn_norm_bwd(dy, da2, h1, mo, g_pre_ffn, sc2, gt1, g_post_mix)
    dya, dyb, dga, dgb = out_proj_bwd(dmo, w_o_f, y_a, y_b, proj)
    gr_o = matmul(merged, dmo, mode="tn", out_dtype=BF16, tm=1024, tn=1024, tk=2048, name="dw_o")
    dsgu = matmul(dyb, w_b_f, mode="nt", out_dtype=F32, tm=512, tn=1024, tk=2048, name="dsgu")
    gr_b = matmul(sgu, dyb, mode="tn", out_dtype=BF16, tm=512, tn=512, tk=4096, name="dw_b_out", split=(4, 2))
    dz, d_w_spatial, d_b_spatial, d_g_sgu = sgu_bwd(proj, dsgu, g_sgu_norm, w_spatial[0], bst)
    dya_pre = matmul(dya, w_a_f, mode="nt", out_dtype=F32, tm=512, tn=1024, tk=2048, name="dya_pre")
    gr_a = matmul(ya_pre, dya, mode="tn", out_dtype=BF16, tm=512, tn=512, tk=4096, name="dw_a_out", split=(4, 2))
    parts_mix = halves_summed([gr_a, gr_b, gr_o.reshape(4, 2, -1, D)], "exchange_mix")
    do, dog, d_g_hgrn = hgrn_post_bwd(dya_pre, o, proj, g_hgrn_norm)
    chips_summed = lambda parts, landed: [sum_chips(p, l, chip_idx) for p, l in zip(parts, landed)]
    (dq, dv, dlg, d_lb), landed_ff = hgrn_bwd(proj, do, lb_full, comm=scatter_comm(parts_ff))
    own_ff = chips_summed(parts_ff, landed_ff)
    dproj = jnp.concatenate([dq, dlg, dv, dog, dz, dga, dgb], axis=1)
    early = _pack([d_g_sgu, d_w_spatial, d_b_spatial[:, 0, :]])
    gr_in, (*landed_mix, got_early) = matmul(a1, dproj, mode="tn", out_dtype=BF16, tm=1024, tn=2816, tk=1024, name="dw_in", split=(4, 2),
                                             comm=_join(scatter_comm(parts_mix), gather8_comm(early)))
    own_mix = chips_summed(parts_mix, landed_mix)
    parts_in = halves_summed([gr_in], "exchange_in")
    da1, (landed_in, *other_rest) = matmul(dproj, w_in_f, mode="nt", out_dtype=F32, tm=1024, tn=1024, tk=2816, name="da1",
                                           comm=_join(scatter_comm(parts_in), share_comm(own_mix + own_ff)))
    own_in = chips_summed(parts_in, [landed_in])
    other_in = comm_call(share_comm(own_in), "share_w_in")
    own, other = own_in + own_mix + own_ff, list(other_in) + other_rest
    grad_x, d_sh1, d_sc1, d_g_pre_mix = mix_norm_bwd(da1, h0, dh1, g_pre_mix, sc1)
    out = {}

    mine = _pack([d_sh1, d_sc1, d_gt1, d_sh2, d_sc2, d_gt2, d_g_pre_mix, d_g_post_mix, d_g_pre_ffn, d_g_post_ffn, d_g_hgrn, d_lb])
    got = all_gather_small(mine, "gather_small_grads")
    g_b_ada, g_g1, g_g2, g_g3, g_g4, g_hg, g_lb = _unpack(
        sum_devices(got, "sum_small_grads"), [(1, 6 * D), (1, D), (1, D), (1, D), (1, D), (1, HEAD_DIM), (2, 1024)])
    g_sg, g_ws, g_bs = _unpack(sum_devices(got_early, "sum_sgu_grads"), [(1, 1024), w_spatial.shape, b_spatial.shape])
    g_lbl = lax.dynamic_slice(lb_logits_grad(g_lb, lb_full), (0, 0, chip * n_lb), (2, 2, n_lb))
    names = ["b_ada", "g_pre_mix", "g_post_mix", "g_pre_ffn", "g_post_ffn", "g_hgrn_norm", "g_sgu_norm", "w_spatial", "b_spatial", "lb_logits"]
    ws = [b_ada, g_pre_mix, g_post_mix, g_pre_ffn, g_post_ffn, g_hgrn_norm, g_sgu_norm, w_spatial, b_spatial, lb_logits]
    gs = [g_b_ada, g_g1, g_g2, g_g3, g_g4, g_hg, g_sg, g_ws, g_bs, g_lbl]
    ms = [m_b_ada, m_g_pre_mix, m_g_post_mix, m_g_pre_ffn, m_g_post_ffn, m_g_hgrn_norm, m_g_sgu_norm, m_w_spatial, m_b_spatial, m_lb_logits]
    vs = [v_b_ada, v_g_pre_mix, v_g_post_mix, v_g_pre_ffn, v_g_post_ffn, v_g_hgrn_norm, v_g_sgu_norm, v_w_spatial, v_b_spatial, v_lb_logits]
    shapes = [w.shape for w in ws]
    upd = adamw(_pack(ws), _pack(gs), _pack(ms), _pack(vs), "adamw_small")
    upd = [_unpack(u, shapes) for u in upd]
    for i, nm in enumerate(names):
        out[nm] = (gs[i], upd[0][i], upd[1][i], upd[2][i])

    dmod_all = got[:, :6 * D // 128, :].reshape(8, 6 * D)
    dmod_chip = lax.dynamic_slice(dmod_all, (0, chip * n_ada), (8, n_ada))
    out["w_ada"] = tuple(a[None] for a in wada_update(c_all, dmod_chip, w_ada[0], m_w_ada[0], v_w_ada[0]))
    for (nm, w, _), a, b, m, v in zip(big, own, other, (m_w_in, m_w_a_out, m_w_b_out, m_w_o, m_w_ff1, m_w_ff2),
                                      (v_w_in, v_w_a_out, v_w_b_out, v_w_o, v_w_ff1, v_w_ff2)):
        out[nm] = tuple(t[None] for t in adamw_halves(w[0], a, b, m[0], v[0], core, "adamw_" + nm))

    order = ["w_ada", "b_ada", "g_pre_mix", "g_post_mix", "g_pre_ffn", "g_post_ffn", "w_in", "lb_logits", "g_hgrn_norm", "w_a_out",
             "g_sgu_norm", "w_spatial", "b_spatial", "w_b_out", "w_o", "w_ff1", "w_ff2"]
    return (loss, grad_x[None], *[out[nm][0] for nm in order], *[out[nm][1] for nm in order], *[out[nm][2] for nm in order],
            *[out[nm][3] for nm in order])
```

```python
import functools
import math

import jax
import jax.numpy as jnp
from jax import lax
from jax.experimental import pallas as pl
from jax.experimental.pallas import tpu as pltpu

F32, BF16 = jnp.float32, jnp.bfloat16
HI = lax.Precision.HIGHEST
MESH = pl.DeviceIdType.MESH
ANY = pl.BlockSpec(memory_space=pl.ANY)

EPS = 1e-6
D_MODEL = 2048
N_HEADS = 8
HEAD_DIM = 128
HGRN_CHUNK = 32
HGRN_BLOCK = 256
SGU_CHUNK = 128
SGU_GROUPS = 8
Q_SCALE = HEAD_DIM ** -0.5
COL_Q, COL_FFW, COL_FBW, COL_V, COL_OG, COL_U, COL_ZV, COL_GA, COL_GB = 0, 1, 2, 3, 4, 5, 6, 7, 9
N_PROJ = 11264
VMEM_BYTES_V7X = 64 * 1024 * 1024
VMEM_LIMIT = VMEM_BYTES_V7X - 8 * 1024 * 1024

ADAM_LR, ADAM_B1, ADAM_B2, ADAM_EPS, ADAM_WD, ADAM_STEP = 0.001, 0.9, 0.999, 1e-08, 0.01, 10
ADAM_C1 = 1.0 - ADAM_B1 ** ADAM_STEP
ADAM_C2 = 1.0 - ADAM_B2 ** ADAM_STEP


def _cp(*sem):
    return pltpu.CompilerParams(dimension_semantics=sem if sem else None, vmem_limit_bytes=VMEM_LIMIT)


def _vec(d):
    return pl.BlockSpec((1, d), lambda *_: (0, 0))


def _colsum(x):
    return jnp.sum(x, axis=0, keepdims=True)


def _nt(a, b):
    return lax.dot_general(a, b, (((1,), (1,)), ((), ())), preferred_element_type=F32)


def _tn(a, b):
    return lax.dot_general(a, b, (((0,), (0,)), ((), ())), preferred_element_type=F32)


def _nn(a, b):
    return jnp.dot(a, b, preferred_element_type=F32)


def _adamw(w, g, m, v):
    m2 = ADAM_B1 * m + (1.0 - ADAM_B1) * g
    v2 = ADAM_B2 * v + (1.0 - ADAM_B2) * (g * g)
    delta = -ADAM_LR * ((m2 / ADAM_C1) / (jnp.sqrt(v2 / ADAM_C2) + ADAM_EPS) + ADAM_WD * w)
    return delta, m2, v2


class _Comm:
    def __init__(self, operands, out_shape, aliases, n_sems, start, finish):
        self.operands, self.out_shape, self.aliases, self.n_sems = list(operands), list(out_shape), dict(aliases), n_sems
        self.start, self.finish = start, finish


def _pallas(body, *, name, grid, in_specs, out_specs, out_shape, scratch, semantics, operands, comm=None):
    if comm is None:
        res = pl.pallas_call(body, name=name, grid=grid, in_specs=in_specs, out_specs=out_specs, out_shape=out_shape,
                             scratch_shapes=scratch, compiler_params=_cp(*semantics))(*operands)
        return res, []
    n_in, n_out, n_scr = len(in_specs), len(out_specs), len(scratch)
    nci, nco = len(comm.operands), len(comm.out_shape)

    def with_comm(*refs):
        ins, rest = refs[:n_in], refs[n_in:]
        cin, rest = rest[:nci], rest[nci:]
        outs, rest = rest[:n_out], rest[n_out:]
        cout, rest = rest[:nco], rest[nco:]
        scr, (send, recv) = rest[:n_scr], rest[n_scr:]
        ids = [pl.program_id(a) for a in range(len(grid))]
        first = functools.reduce(jnp.logical_and, [i == 0 for i in ids])
        last = functools.reduce(jnp.logical_and, [i == g - 1 for i, g in zip(ids, grid)])

        @pl.when(first)
        def _():
            comm.start(cin, cout, send, recv)

        body(*ins, *outs, *scr)

        @pl.when(last)
        def _():
            comm.finish(cin, cout, send, recv)

    res = pl.pallas_call(
        with_comm, name=name, grid=grid, in_specs=list(in_specs) + [ANY] * nci, out_specs=list(out_specs) + [ANY] * nco,
        out_shape=list(out_shape) + comm.out_shape, input_output_aliases={n_in + i: n_out + o for i, o in comm.aliases.items()},
        scratch_shapes=list(scratch) + [pltpu.SemaphoreType.DMA((comm.n_sems,)), pltpu.SemaphoreType.DMA((comm.n_sems,))],
        compiler_params=_cp(*["arbitrary"] * len(grid)),
    )(*operands, *comm.operands)
    return res[:n_out], res[n_out:]


def matmul(a, b, *, mode, out_dtype, tm, tn, tk, name, split=None, comm=None):
    if mode == "tn":
        (K, M), (_, N) = a.shape, b.shape
    elif mode == "nt":
        (M, K), (N, _) = a.shape, b.shape
    else:
        (M, K), (_, N) = a.shape, b.shape
    tm, tn, tk = min(tm, M), min(tn, N), min(tk, K)
    nk = K // tk
    a_spec = pl.BlockSpec((tk, tm), lambda i, j, k: (k, i)) if mode == "tn" else pl.BlockSpec((tm, tk), lambda i, j, k: (i, k))
    b_spec = pl.BlockSpec((tn, tk), lambda i, j, k: (j, k)) if mode == "nt" else pl.BlockSpec((tk, tn), lambda i, j, k: (k, j))
    dot = {"nn": _nn, "nt": _nt, "tn": _tn}[mode]
    if split is None:
        out_shape = jax.ShapeDtypeStruct((M, N), out_dtype)
        out_spec = pl.BlockSpec((tm, tn), lambda i, j, k: (i, j))
    else:
        nj, nh = split
        rows, cols = M // nh, N // nj
        tm, tn = min(tm, rows), min(tn, cols)
        bi, bj = rows // tm, cols // tn
        out_shape = jax.ShapeDtypeStruct((nj, nh, rows, cols), out_dtype)
        out_spec = pl.BlockSpec((None, None, tm, tn), lambda i, j, k: (j // bj, i // bi, i % bi, j % bj))

    def body(a_ref, b_ref, o_ref, acc_ref):
        k = pl.program_id(2)

        @pl.when(k == 0)
        def _():
            acc_ref[...] = jnp.zeros_like(acc_ref)

        acc_ref[...] += dot(a_ref[...], b_ref[...])

        @pl.when(k == nk - 1)
        def _():
            o_ref[...] = acc_ref[...].astype(o_ref.dtype)

    (out,), landed = _pallas(
        body, name=name, grid=(M // tm, N // tn, nk), in_specs=[a_spec, b_spec], out_specs=[out_spec], out_shape=[out_shape],
        scratch=[pltpu.VMEM((tm, tn), F32)], semantics=("parallel", "parallel", "arbitrary"), operands=(a, b), comm=comm)
    return out if comm is None else (out, landed)


def cast_into_full(w, kind, chip, name):
    r, cc = w.shape
    tr = min(r, 512)
    nb = r // tr

    def body(chip_ref, w_ref, o_ref):
        o_ref[...] = w_ref[...].astype(BF16)

    if kind == "col":
        full, out_map = (r, 4 * cc), lambda i, chip_ref: (i, chip_ref[0])
    else:
        full, out_map = (4 * r, cc), lambda i, chip_ref: (chip_ref[0] * nb + i, 0)
    return pl.pallas_call(
        body, name=name, out_shape=jax.ShapeDtypeStruct(full, BF16),
        grid_spec=pltpu.PrefetchScalarGridSpec(
            num_scalar_prefetch=1, grid=(nb,), in_specs=[pl.BlockSpec((tr, cc), lambda i, chip_ref: (i, 0))],
            out_specs=pl.BlockSpec((tr, cc), out_map)),
        compiler_params=_cp("parallel"),
    )(chip, w)


def mod_matmul(c_all, w_ada, b_ada):
    D, N = w_ada.shape
    tn = 1024

    def body(c_ref, w_ref, b_ref, o_ref):
        c = c_ref[...]
        sc = c * jax.nn.sigmoid(c)
        o_ref[...] = jnp.dot(sc, w_ref[...], precision=HI, preferred_element_type=F32) + b_ref[...]

    return pl.pallas_call(
        body, name="mod_matmul", out_shape=jax.ShapeDtypeStruct((8, N), F32), grid=(N // tn,),
        in_specs=[pl.BlockSpec((8, D), lambda j: (0, 0)), pl.BlockSpec((D, tn), lambda j: (0, j)),
                  pl.BlockSpec((1, tn), lambda j: (0, j))],
        out_specs=pl.BlockSpec((8, tn), lambda j: (0, j)), compiler_params=_cp("parallel"),
    )(c_all, w_ada, b_ada)


def prenorm(h, g, sc, sh):
    T, D = h.shape
    tm = min(256, T)

    def body(h_ref, g_ref, sc_ref, sh_ref, a_ref):
        x = h_ref[...]
        r = lax.rsqrt(jnp.mean(x * x, axis=-1, keepdims=True) + EPS)
        a_ref[...] = ((x * r) * g_ref[...] * (1.0 + sc_ref[...]) + sh_ref[...]).astype(BF16)

    row = pl.BlockSpec((tm, D), lambda i: (i, 0))
    return pl.pallas_call(
        body, name="prenorm", out_shape=jax.ShapeDtypeStruct((T, D), BF16), grid=(T // tm,),
        in_specs=[row, _vec(D), _vec(D), _vec(D)], out_specs=row, compiler_params=_cp("parallel"),
    )(h, g, sc, sh)


def in_proj_gathered(a, w_full, chip, dims, tail):
    T, D = a.shape
    rows, cc = dims
    tm, tn = min(512, T), cc // 2
    ni = T // tm
    half = rows // 2

    nt = len(tail.operands)

    def body(chip_ref, a_ref, w_in_ref, *rest):
        tail_in, (y_ref, w_ref), rest = rest[:nt], rest[nt:nt + 2], rest[nt + 2:]
        tail_out, (wbuf, wsem, send_sems, recv_sems, tail_send, tail_recv) = rest[:nt], rest[nt:]
        q, j, i = pl.program_id(0), pl.program_id(1), pl.program_id(2)
        mx, my, mc, _ = _place()
        me = chip_ref[0]

        def tile(block, jj):
            src = w_ref.at[:, pl.ds(pl.multiple_of(block * cc + jj * tn, 128), tn)]
            return pltpu.make_async_copy(src, wbuf.at[jj], wsem.at[jj])

        def rows_half(block, hh):
            return w_ref.at[pl.ds(pl.multiple_of(hh * half, 16), half), pl.ds(pl.multiple_of(block * cc, 128), cc)]

        def over_ici(s, block):
            peer = (1 - mx if s & 2 else mx, 1 - my if s & 1 else my, mc)
            reg = rows_half(block, mc)
            return pltpu.make_async_remote_copy(src_ref=reg, dst_ref=reg, send_sem=send_sems.at[s - 1], recv_sem=recv_sems.at[s - 1],
                                                device_id=peer, device_id_type=MESH)

        def over_d2d(s, block, hh):
            reg = rows_half(block, hh)
            return pltpu.make_async_remote_copy(src_ref=reg, dst_ref=reg, send_sem=send_sems.at[2 + s], recv_sem=recv_sems.at[2 + s],
                                                device_id=(mx, my, 1 - mc), device_id_type=MESH)

        @pl.when((q == 0) & (j == 0) & (i == 0))
        def _():
            for s in (1, 2, 3):
                over_ici(s, me).start()
            tile(me, 0).start()

        @pl.when(i == 0)
        def _():
            tile(me ^ q, j).wait()

        @pl.when((i == 0) & (j == 0))
        def _():
            tile(me ^ q, 1).start()

        y_ref[...] = _nn(a_ref[...], wbuf[j])

        for s in (1, 2, 3):
            @pl.when((q == s - 1) & (j == 1) & (i == ni - 1))
            def _():
                block = me ^ s
                over_ici(s, block).wait_recv()
                over_d2d(s, block, mc).start()
                over_d2d(s, block, 1 - mc).wait_recv()
                tile(block, 0).start()

        @pl.when((q == 3) & (j == 0) & (i == 0))
        def _():
            tail.start(tail_in, tail_out, tail_send, tail_recv)

        @pl.when((q == 3) & (j == 1) & (i == ni - 1))
        def _():
            for s in (1, 2, 3):
                over_ici(s, me).wait_send()
                over_d2d(s, me ^ s, mc).wait_send()
            tail.finish(tail_in, tail_out, tail_send, tail_recv)

    dma = pltpu.SemaphoreType.DMA
    y, w_out, *tail_res = pl.pallas_call(
        body, name="in_proj", out_shape=[jax.ShapeDtypeStruct((T, 4 * cc), F32), jax.ShapeDtypeStruct(w_full.shape, BF16)] + tail.out_shape,
        grid_spec=pltpu.PrefetchScalarGridSpec(
            num_scalar_prefetch=1, grid=(4, 2, ni),
            in_specs=[pl.BlockSpec((tm, D), lambda q, j, i, chip_ref: (i, 0)), ANY] + [ANY] * nt,
            out_specs=[pl.BlockSpec((tm, tn), lambda q, j, i, chip_ref: (i, (chip_ref[0] ^ q) * 2 + j)), ANY] + [ANY] * nt,
            scratch_shapes=[pltpu.VMEM((2, D, tn), BF16), dma((2,)), dma((6,)), dma((6,)), dma((tail.n_sems,)), dma((tail.n_sems,))]),
        input_output_aliases={2: 1, **{3 + i: 2 + o for i, o in tail.aliases.items()}},
        compiler_params=_cp("arbitrary", "arbitrary", "arbitrary"),
    )(chip, a, w_full, *tail.operands)
    return y, w_out, tail_res


def prenorm_matmul(h, g, sc, sh, w, *, relu2, name, comm=None):
    T, D = h.shape
    N = w.shape[1]
    tm, tn = min(512, T), 2048 if N % 2048 == 0 else 1024

    def body(h_ref, g_ref, sc_ref, sh_ref, w_ref, y_ref, a_ref, *hid_ref):
        @pl.when(pl.program_id(1) == 0)
        def _():
            x = h_ref[...]
            r = lax.rsqrt(jnp.mean(x * x, axis=-1, keepdims=True) + EPS)
            a_ref[...] = ((x * r) * g_ref[...] * (1.0 + sc_ref[...]) + sh_ref[...]).astype(BF16)

        y = _nn(a_ref[...], w_ref[...])
        y_ref[...] = y
        if relu2:
            p = jnp.maximum(y, 0.0)
            hid_ref[0][...] = (p * p).astype(BF16)

    out_shape = [jax.ShapeDtypeStruct((T, N), F32), jax.ShapeDtypeStruct((T, D), BF16)]
    out_specs = [pl.BlockSpec((tm, tn), lambda i, j: (i, j)), pl.BlockSpec((tm, D), lambda i, j: (i, 0))]
    if relu2:
        out_shape.append(jax.ShapeDtypeStruct((T, N), BF16))
        out_specs.append(pl.BlockSpec((tm, tn), lambda i, j: (i, j)))
    outs, landed = _pallas(
        body, name=name, grid=(T // tm, N // tn),
        in_specs=[pl.BlockSpec((tm, D), lambda i, j: (i, 0)), _vec(D), _vec(D), _vec(D), pl.BlockSpec((D, tn), lambda i, j: (0, j))],
        out_specs=out_specs, out_shape=out_shape, scratch=[], semantics=("parallel", "arbitrary"), operands=(h, g, sc, sh, w), comm=comm)
    return outs if comm is None else (outs, landed)


def _hgrn_lower_bound(l_ref):
    l0, l1 = l_ref[0:1, :], l_ref[1:2, :]
    m = jnp.maximum(l0, l1)
    e0, e1 = jnp.exp(l0 - m), jnp.exp(l1 - m)
    return e0 / (e0 + e1)


def _hgrn_chunk_mask(d):
    r = lax.broadcasted_iota(jnp.int32, (HGRN_BLOCK, HGRN_BLOCK), 0)
    c = lax.broadcasted_iota(jnp.int32, (HGRN_BLOCK, HGRN_BLOCK), 1)
    same = (r // HGRN_CHUNK) == (c // HGRN_CHUNK)
    fwd = d == 0
    return same & (((c <= r) & fwd) | ((c >= r) & jnp.logical_not(fwd)))


def _chunk_total(x):
    x3 = x.reshape(HGRN_BLOCK // HGRN_CHUNK, HGRN_CHUNK, x.shape[1])
    return jnp.broadcast_to(jnp.sum(x3, axis=1, keepdims=True), x3.shape).reshape(x.shape)


def _chunk_cumsum(x, suffix):
    pos = lax.broadcasted_iota(jnp.int32, x.shape, 0) % HGRN_CHUNK
    p, s = x, 1
    while s < HGRN_CHUNK:
        p = p + jnp.where(pos >= s, pltpu.roll(p, s, 0), 0.0)
        s *= 2
    return jnp.where(suffix, _chunk_total(x) - p + x, p)


def _block_loop(T, body, init):
    n = T // HGRN_BLOCK
    return lax.fori_loop(0, n, body, init, unroll=2 if n % 2 == 0 else 1)


def _hgrn_gate(f, lb):
    s = jax.nn.sigmoid(f)
    sn = jax.nn.sigmoid(-f)
    fg = lb + (1.0 - lb) * s
    return s, sn, fg, jnp.log(fg), (1.0 - lb) * sn


def _hgrn_specs(T):
    col = lambda base: pl.BlockSpec((T, HEAD_DIM), lambda h, d: (0, base * N_HEADS + h))
    f_spec = pl.BlockSpec((T, HEAD_DIM), lambda h, d: (0, COL_FFW * N_HEADS + N_HEADS * d + h))
    l_spec = pl.BlockSpec((None, 2, HEAD_DIM), lambda h, d: (d, 0, h))
    return col, f_spec, l_spec


def hgrn_fwd(proj, lb_logits, comm=None):
    T = proj.shape[0]
    NC, CPB = T // HGRN_CHUNK, HGRN_BLOCK // HGRN_CHUNK
    col, f_spec, l_spec = _hgrn_specs(T)

    def body(l_ref, q_ref, f_ref, v_ref, o_ref, st_ref, dec_ref, qd_ref):
        d = pl.program_id(1)
        lb = _hgrn_lower_bound(l_ref)
        mask = _hgrn_chunk_mask(d)

        def block(i, carry):
            rows = pl.ds(pl.multiple_of(i * HGRN_BLOCK, HGRN_BLOCK), HGRN_BLOCK)
            _, _, _, lf, k = _hgrn_gate(f_ref[rows, :], lb)
            b = _chunk_cumsum(lf, d == 1)
            bl = _chunk_total(lf)
            qd = (q_ref[rows, :] * Q_SCALE * jnp.exp(b)).astype(BF16)
            kd = (k * jnp.exp(-b)).astype(BF16)
            ke = (k * jnp.exp(bl - b)).astype(BF16)
            vb = v_ref[rows, :].astype(BF16)
            att = jnp.where(mask, _nt(qd, kd), 0.0).astype(BF16)
            o_ref[rows, :] = jnp.where(d == 0, 0.0, o_ref[rows, :]) + _nn(att, vb)
            qd_ref[rows, :] = qd
            dec = jnp.exp(bl)
            for cc in range(CPB):
                sl = slice(cc * HGRN_CHUNK, (cc + 1) * HGRN_CHUNK)
                n = i * CPB + cc
                st_ref[n] = _tn(vb[sl], ke[sl])
                dec_ref[n] = dec[cc * HGRN_CHUNK:cc * HGRN_CHUNK + 8, :]
            return carry

        _block_loop(T, block, 0)

        def scan(t, s):
            n = jnp.where(d == 0, t, NC - 1 - t)
            u = st_ref[n]
            st_ref[n] = s
            return dec_ref[n][0:1, :] * s + u

        lax.fori_loop(0, NC, scan, jnp.zeros((HEAD_DIM, HEAD_DIM), F32))

        def inter(i, carry):
            rows = pl.ds(pl.multiple_of(i * HGRN_BLOCK, HGRN_BLOCK), HGRN_BLOCK)
            qd = qd_ref[rows, :]
            o_ref[rows, :] += jnp.concatenate(
                [_nt(qd[cc * HGRN_CHUNK:(cc + 1) * HGRN_CHUNK], st_ref[i * CPB + cc].astype(BF16)) for cc in range(CPB)], axis=0)
            return carry

        _block_loop(T, inter, 0)

    (o,), landed = _pallas(
        body, name="hgrn_fwd", grid=(N_HEADS, 2), in_specs=[l_spec, col(COL_Q), f_spec, col(COL_V)],
        out_specs=[pl.BlockSpec((T, HEAD_DIM), lambda h, d: (0, h))], out_shape=[jax.ShapeDtypeStruct((T, N_HEADS * HEAD_DIM), F32)],
        scratch=[pltpu.VMEM((NC, HEAD_DIM, HEAD_DIM), F32), pltpu.VMEM((NC, 8, HEAD_DIM), F32), pltpu.VMEM((T, HEAD_DIM), BF16)],
        semantics=("parallel", "arbitrary"), operands=(lb_logits, proj, proj, proj), comm=comm)
    return o if comm is None else (o, landed)


def hgrn_post_fwd(o, proj, g_norm):
    T, W = o.shape
    tm = min(256, T)

    def body(o_ref, og_ref, g_ref, y_ref):
        g = g_ref[...]
        for h in range(N_HEADS):
            sl = slice(h * HEAD_DIM, (h + 1) * HEAD_DIM)
            x = o_ref[:, sl]
            r = lax.rsqrt(jnp.mean(x * x, axis=-1, keepdims=True) + EPS)
            og = og_ref[:, sl]
            y_ref[:, sl] = ((x * r) * g * (og * jax.nn.sigmoid(og))).astype(BF16)

    return pl.pallas_call(
        body, name="hgrn_post_fwd", out_shape=jax.ShapeDtypeStruct((T, W), BF16), grid=(T // tm,),
        in_specs=[pl.BlockSpec((tm, W), lambda i: (i, 0)), pl.BlockSpec((tm, W), lambda i: (i, COL_OG)), _vec(HEAD_DIM)],
        out_specs=pl.BlockSpec((tm, W), lambda i: (i, 0)), compiler_params=_cp("parallel"),
    )(o, proj, g_norm)


def _gelu(x):
    return 0.5 * x * (1.0 + lax.erf(x * (1.0 / math.sqrt(2.0))))


def _gelu_grad(x):
    return 0.5 * (1.0 + lax.erf(x * (1.0 / math.sqrt(2.0)))) + x * jnp.exp(-0.5 * x * x) * (1.0 / math.sqrt(2.0 * math.pi))


def _sgu_mix(u_ref, v_ref, g_ref, ws_ref, bst_ref):
    W = u_ref.shape[1]
    zu, zv = _gelu(u_ref[...]), _gelu(v_ref[...])
    dv = zv - jnp.mean(zv, axis=-1, keepdims=True)
    rstd = lax.rsqrt(jnp.mean(dv * dv, axis=-1, keepdims=True) + EPS)
    dhat = dv * rstd
    vn = (dhat * g_ref[...]).astype(BF16)
    gw = W // SGU_GROUPS
    vm = [_nn(ws_ref[g].astype(BF16), vn[:, g * gw:(g + 1) * gw]) + bst_ref[:, g:g + 1] for g in range(SGU_GROUPS)]
    return zu, rstd, dhat, vn, jnp.concatenate(vm, axis=1)


def sgu_fwd(proj, g_norm, w_spatial, b_spatial_t):
    T = proj.shape[0]
    W = 1024
    n_chunks = T // SGU_CHUNK

    def body(u_ref, v_ref, g_ref, ws_ref, bst_ref, y_ref):
        zu, _, _, _, vm = _sgu_mix(u_ref, v_ref, g_ref, ws_ref, bst_ref)
        y_ref[...] = (zu * vm).astype(BF16)

    blk = lambda cb: pl.BlockSpec((SGU_CHUNK, W), lambda i: (i, cb))
    return pl.pallas_call(
        body, name="sgu_fwd", out_shape=jax.ShapeDtypeStruct((T, W), BF16), grid=(n_chunks,),
        in_specs=[blk(COL_U), blk(COL_ZV), _vec(W), pl.BlockSpec((SGU_GROUPS, SGU_CHUNK, SGU_CHUNK), lambda i: (0, 0, 0)),
                  pl.BlockSpec((SGU_CHUNK, SGU_GROUPS), lambda i: (0, 0))],
        out_specs=blk(0), compiler_params=_cp("parallel"),
    )(proj, proj, g_norm, w_spatial, b_spatial_t)


def merge_matmul(ya_pre, sgu, w_a, w_b, proj):
    T, K = ya_pre.shape
    N = w_a.shape[1]
    tm, tn = min(512, T), 512
    gpb = 1024 // tn

    def body(a_ref, b_ref, wa_ref, wb_ref, ga_ref, gb_ref, ya_ref, yb_ref, m_ref):
        ya = _nn(a_ref[...], wa_ref[...])
        yb = _nn(b_ref[...], wb_ref[...])
        ya_ref[...] = ya
        yb_ref[...] = yb
        m_ref[...] = (jax.nn.sigmoid(ga_ref[...]) * ya + jax.nn.sigmoid(gb_ref[...]) * yb).astype(BF16)

    lhs = pl.BlockSpec((tm, K), lambda i, j: (i, 0))
    rhs = pl.BlockSpec((K, tn), lambda i, j: (0, j))
    out = pl.BlockSpec((tm, tn), lambda i, j: (i, j))
    return pl.pallas_call(
        body, name="merge_matmul", grid=(T // tm, N // tn),
        out_shape=[jax.ShapeDtypeStruct((T, N), F32), jax.ShapeDtypeStruct((T, N), F32), jax.ShapeDtypeStruct((T, N), BF16)],
        in_specs=[lhs, lhs, rhs, rhs, pl.BlockSpec((tm, tn), lambda i, j: (i, COL_GA * gpb + j)),
                  pl.BlockSpec((tm, tn), lambda i, j: (i, COL_GB * gpb + j))],
        out_specs=[out, out, out], compiler_params=_cp("parallel", "parallel"),
    )(ya_pre, sgu, w_a, w_b, proj, proj)


def out_proj(merged, w_o, h0, gt1, g_post):
    T, D = h0.shape
    tm = min(256, T)

    def body(m_ref, w_ref, h_ref, gt_ref, gp_ref, mo_ref, h1_ref):
        mo = _nn(m_ref[...], w_ref[...])
        mo_ref[...] = mo
        r = lax.rsqrt(jnp.mean(mo * mo, axis=-1, keepdims=True) + EPS)
        h1_ref[...] = h_ref[...] + gt_ref[...] * ((mo * r) * gp_ref[...])

    row = pl.BlockSpec((tm, D), lambda i: (i, 0))
    return pl.pallas_call(
        body, name="out_proj", grid=(T // tm,),
        out_shape=[jax.ShapeDtypeStruct((T, D), F32), jax.ShapeDtypeStruct((T, D), F32)],
        in_specs=[row, pl.BlockSpec((D, D), lambda i: (0, 0)), row, _vec(D), _vec(D)],
        out_specs=[row, row], compiler_params=_cp("parallel"),
    )(merged, w_o, h0, gt1, g_post)


def ff2_loss(hid, w_ff2, h1, tgt, gt2, g_post):
    T, K = hid.shape
    D = w_ff2.shape[1]
    tm, tk = min(256, T), 2048
    nk = K // tk

    def body(a_ref, w_ref, h_ref, t_ref, gt_ref, g_ref, dy_ref, dff_ref, loss_ref, dgt_ref, dg_ref, acc_ref):
        i, k = pl.program_id(0), pl.program_id(1)

        @pl.when(k == 0)
        def _():
            acc_ref[...] = jnp.zeros_like(acc_ref)

        @pl.when((k == 0) & (i == 0))
        def _():
            loss_ref[...] = jnp.zeros_like(loss_ref)
            dgt_ref[...] = jnp.zeros_like(dgt_ref)
            dg_ref[...] = jnp.zeros_like(dg_ref)

        acc_ref[...] += _nn(a_ref[...], w_ref[...])

        @pl.when(k == nk - 1)
        def _():
            ff = acc_ref[...]
            gt, g = gt_ref[...], g_ref[...]
            r = lax.rsqrt(jnp.mean(ff * ff, axis=-1, keepdims=True) + EPS)
            fhat = ff * r
            nf = fhat * g
            err = (h_ref[...] + gt * nf) - t_ref[...]
            loss_ref[...] += jnp.sum(err * err)
            dy = err * (1.0 / D)
            dy_ref[...] = dy
            dgt_ref[...] += _colsum(dy * nf)
            dnf = dy * gt
            dg_ref[...] += _colsum(dnf * fhat)
            u = dnf * g
            dff_ref[...] = (r * (u - fhat * jnp.mean(u * fhat, axis=-1, keepdims=True))).astype(BF16)

    row = pl.BlockSpec((tm, D), lambda i, k: (i, 0))
    vec = pl.BlockSpec((1, D), lambda i, k: (0, 0))
    return pl.pallas_call(
        body, name="ff2_loss", grid=(T // tm, nk),
        out_shape=[jax.ShapeDtypeStruct((T, D), F32), jax.ShapeDtypeStruct((T, D), BF16), jax.ShapeDtypeStruct((8, 128), F32),
                   jax.ShapeDtypeStruct((1, D), F32), jax.ShapeDtypeStruct((1, D), F32)],
        in_specs=[pl.BlockSpec((tm, tk), lambda i, k: (i, k)), pl.BlockSpec((tk, D), lambda i, k: (k, 0)), row, row, vec, vec],
        out_specs=[row, row, pl.BlockSpec((8, 128), lambda i, k: (0, 0)), vec, vec],
        scratch_shapes=[pltpu.VMEM((tm, D), F32)], compiler_params=_cp("arbitrary", "arbitrary"),
    )(hid, w_ff2, h1, tgt, gt2, g_post)


def ff2_bwd(dff, w_ff2, f1):
    T, D = dff.shape
    K = w_ff2.shape[0]
    tm, tn = min(512, T), 2048

    def body(a_ref, w_ref, f_ref, o_ref):
        o_ref[...] = (_nt(a_ref[...], w_ref[...]) * (2.0 * jnp.maximum(f_ref[...], 0.0))).astype(BF16)

    return pl.pallas_call(
        body, name="ff2_bwd", out_shape=jax.ShapeDtypeStruct((T, K), BF16), grid=(K // tn, T // tm),
        in_specs=[pl.BlockSpec((tm, D), lambda j, i: (i, 0)), pl.BlockSpec((tn, D), lambda j, i: (j, 0)),
                  pl.BlockSpec((tm, tn), lambda j, i: (i, j))],
        out_specs=pl.BlockSpec((tm, tn), lambda j, i: (i, j)), compiler_params=_cp("parallel", "parallel"),
    )(dff, w_ff2, f1)


def ffn_norm_bwd(dy, da2, h1, mo, g_pre2, sc2, gt1, g_post):
    T, D = dy.shape
    tm = min(256, T)

    def body(dy_ref, da_ref, h_ref, mo_ref, g2_ref, sc_ref, gt_ref, gp_ref, dh_ref, dmo_ref, s_sh, s_sc, s_g2, s_gt, s_gp):
        @pl.when(pl.program_id(0) == 0)
        def _():
            for s in (s_sh, s_sc, s_g2, s_gt, s_gp):
                s[...] = jnp.zeros_like(s)

        h1, da = h_ref[...], da_ref[...]
        g2, sc = g2_ref[...], sc_ref[...]
        r2 = lax.rsqrt(jnp.mean(h1 * h1, axis=-1, keepdims=True) + EPS)
        n2 = h1 * r2
        s_sh[...] += _colsum(da)
        s_sc[...] += _colsum(da * (n2 * g2))
        s_g2[...] += _colsum(da * (1.0 + sc) * n2)
        dn2 = da * g2 * (1.0 + sc)
        dh1 = dy_ref[...] + r2 * (dn2 - n2 * jnp.mean(dn2 * n2, axis=-1, keepdims=True))
        dh_ref[...] = dh1
        mo = mo_ref[...]
        gt, gp = gt_ref[...], gp_ref[...]
        r = lax.rsqrt(jnp.mean(mo * mo, axis=-1, keepdims=True) + EPS)
        mhat = mo * r
        s_gt[...] += _colsum(dh1 * (mhat * gp))
        dnm = dh1 * gt
        s_gp[...] += _colsum(dnm * mhat)
        u = dnm * gp
        dmo_ref[...] = (r * (u - mhat * jnp.mean(u * mhat, axis=-1, keepdims=True))).astype(BF16)

    row = pl.BlockSpec((tm, D), lambda i: (i, 0))
    vec_out = jax.ShapeDtypeStruct((1, D), F32)
    return pl.pallas_call(
        body, name="ffn_norm_bwd", grid=(T // tm,),
        out_shape=[jax.ShapeDtypeStruct((T, D), F32), jax.ShapeDtypeStruct((T, D), BF16)] + [vec_out] * 5,
        in_specs=[row, row, row, row] + [_vec(D)] * 4, out_specs=[row, row] + [_vec(D)] * 5,
        compiler_params=_cp("arbitrary"),
    )(dy, da2, h1, mo, g_pre2, sc2, gt1, g_post)


def out_proj_bwd(dmo, w_o, y_a, y_b, proj):
    T, D = dmo.shape
    tm, tn = min(512, T), 512
    gpb = 1024 // tn

    def body(a_ref, w_ref, ya_ref, yb_ref, ga_ref, gb_ref, dya_ref, dyb_ref, dga_ref, dgb_ref):
        dm = _nt(a_ref[...], w_ref[...])
        sa, sb = jax.nn.sigmoid(ga_ref[...]), jax.nn.sigmoid(gb_ref[...])
        dya_ref[...] = (dm * sa).astype(BF16)
        dyb_ref[...] = (dm * sb).astype(BF16)
        dga_ref[...] = (dm * ya_ref[...] * sa * (1.0 - sa)).astype(BF16)
        dgb_ref[...] = (dm * yb_ref[...] * sb * (1.0 - sb)).astype(BF16)

    out = pl.BlockSpec((tm, tn), lambda i, j: (i, j))
    return pl.pallas_call(
        body, name="out_proj_bwd", grid=(T // tm, D // tn), out_shape=[jax.ShapeDtypeStruct((T, D), BF16)] * 4,
        in_specs=[pl.BlockSpec((tm, D), lambda i, j: (i, 0)), pl.BlockSpec((tn, D), lambda i, j: (j, 0)), out, out,
                  pl.BlockSpec((tm, tn), lambda i, j: (i, COL_GA * gpb + j)), pl.BlockSpec((tm, tn), lambda i, j: (i, COL_GB * gpb + j))],
        out_specs=[out] * 4, compiler_params=_cp("parallel", "parallel"),
    )(dmo, w_o, y_a, y_b, proj, proj)


def sgu_bwd(proj, dsgu, g_norm, w_spatial, b_spatial_t):
    T = proj.shape[0]
    W = 1024
    gw = W // SGU_GROUPS

    def body(u_ref, v_ref, ds_ref, g_ref, ws_ref, bst_ref, dz_ref, dw_ref, db_ref, dg_ref):
        @pl.when(pl.program_id(0) == 0)
        def _():
            dw_ref[...] = jnp.zeros_like(dw_ref)
            db_ref[...] = jnp.zeros_like(db_ref)
            dg_ref[...] = jnp.zeros_like(dg_ref)

        zu, rstd, dhat, vn, vm = _sgu_mix(u_ref, v_ref, g_ref, ws_ref, bst_ref)
        ds = ds_ref[...]
        du = ds * vm
        dvm = ds * zu
        dvm_b = dvm.astype(BF16)
        ones = jnp.ones((8, gw), F32)
        dvn = []
        for g in range(SGU_GROUPS):
            sl = slice(g * gw, (g + 1) * gw)
            dw_ref[g] += _nt(dvm_b[:, sl], vn[:, sl])
            db_ref[g] += lax.dot_general(ones, dvm[:, sl], (((1,), (1,)), ((), ())), precision=HI, preferred_element_type=F32)
            dvn.append(_tn(ws_ref[g].astype(BF16), dvm_b[:, sl]))
        dvn = jnp.concatenate(dvn, axis=1)
        dg_ref[...] += _colsum(dvn * dhat)
        ddh = dvn * g_ref[...]
        dzv = rstd * (ddh - jnp.mean(ddh, axis=-1, keepdims=True) - dhat * jnp.mean(ddh * dhat, axis=-1, keepdims=True))
        dz_ref[:, 0:W] = (du * _gelu_grad(u_ref[...])).astype(BF16)
        dz_ref[:, W:2 * W] = (dzv * _gelu_grad(v_ref[...])).astype(BF16)

    blk = lambda cb: pl.BlockSpec((SGU_CHUNK, W), lambda i: (i, cb))
    full3 = lambda a, b, c: pl.BlockSpec((a, b, c), lambda i: (0, 0, 0))
    return pl.pallas_call(
        body, name="sgu_bwd", grid=(T // SGU_CHUNK,),
        out_shape=[jax.ShapeDtypeStruct((T, 2 * W), BF16), jax.ShapeDtypeStruct((SGU_GROUPS, SGU_CHUNK, SGU_CHUNK), F32),
                   jax.ShapeDtypeStruct((SGU_GROUPS, 8, SGU_CHUNK), F32), jax.ShapeDtypeStruct((1, W), F32)],
        in_specs=[blk(COL_U), blk(COL_ZV), blk(0), _vec(W), full3(SGU_GROUPS, SGU_CHUNK, SGU_CHUNK),
                  pl.BlockSpec((SGU_CHUNK, SGU_GROUPS), lambda i: (0, 0))],
        out_specs=[pl.BlockSpec((SGU_CHUNK, 2 * W), lambda i: (i, 0)), full3(SGU_GROUPS, SGU_CHUNK, SGU_CHUNK),
                   full3(SGU_GROUPS, 8, SGU_CHUNK), _vec(W)],
        compiler_params=_cp("arbitrary"),
    )(proj, proj, dsgu, g_norm, w_spatial, b_spatial_t)


def hgrn_post_bwd(dya, o, proj, g_norm):
    T, W = o.shape
    tm = min(256, T)

    def body(dy_ref, o_ref, og_ref, g_ref, do_ref, dog_ref, dg_ref):
        @pl.when(pl.program_id(0) == 0)
        def _():
            dg_ref[...] = jnp.zeros_like(dg_ref)

        g = g_ref[...]
        dg = jnp.zeros((1, HEAD_DIM), F32)
        for h in range(N_HEADS):
            sl = slice(h * HEAD_DIM, (h + 1) * HEAD_DIM)
            x, og, dy = o_ref[:, sl], og_ref[:, sl], dy_ref[:, sl]
            r = lax.rsqrt(jnp.mean(x * x, axis=-1, keepdims=True) + EPS)
            xhat = x * r
            s = jax.nn.sigmoid(og)
            don = dy * (og * s)
            dog_ref[:, sl] = (dy * (xhat * g) * (s * (1.0 + og * (1.0 - s)))).astype(BF16)
            dg += _colsum(don * xhat)
            u = don * g
            do_ref[:, sl] = r * (u - xhat * jnp.mean(u * xhat, axis=-1, keepdims=True))
        dg_ref[...] += dg

    row = pl.BlockSpec((tm, W), lambda i: (i, 0))
    return pl.pallas_call(
        body, name="hgrn_post_bwd", grid=(T // tm,),
        out_shape=[jax.ShapeDtypeStruct((T, W), F32), jax.ShapeDtypeStruct((T, W), BF16), jax.ShapeDtypeStruct((1, HEAD_DIM), F32)],
        in_specs=[row, row, pl.BlockSpec((tm, W), lambda i: (i, COL_OG)), _vec(HEAD_DIM)],
        out_specs=[row, row, _vec(HEAD_DIM)], compiler_params=_cp("arbitrary"),
    )(dya, o, proj, g_norm)


def hgrn_bwd(proj, do, lb_logits, comm=None):
    T = proj.shape[0]
    NC, CPB = T // HGRN_CHUNK, HGRN_BLOCK // HGRN_CHUNK
    W = N_HEADS * HEAD_DIM
    col, f_spec, l_spec = _hgrn_specs(T)

    def body(l_ref, q_ref, f_ref, v_ref, do_ref, dq_ref, dv_ref, dlg_ref, dlb_ref, st_ref, dst_ref, dec_ref, ddec_ref, dqa_ref, dva_ref):
        d = pl.program_id(1)
        lb = _hgrn_lower_bound(l_ref)
        oml = 1.0 - lb
        mask = _hgrn_chunk_mask(d)

        def values(rows):
            s, sn, fg, lf, k = _hgrn_gate(f_ref[rows, :], lb)
            b = _chunk_cumsum(lf, d == 1)
            bl = _chunk_total(lf)
            eb, enb, ee = jnp.exp(b), jnp.exp(-b), jnp.exp(bl - b)
            qd = q_ref[rows, :] * Q_SCALE * eb
            return s, sn, fg, k, bl, eb, enb, ee, qd, k * enb, k * ee

        def block1(i, carry):
            rows = pl.ds(pl.multiple_of(i * HGRN_BLOCK, HGRN_BLOCK), HGRN_BLOCK)
            _, _, _, _, bl, _, _, _, qd, _, ke = values(rows)
            qd, ke = qd.astype(BF16), ke.astype(BF16)
            vb, dob = v_ref[rows, :].astype(BF16), do_ref[rows, :].astype(BF16)
            dec = jnp.exp(bl)
            for cc in range(CPB):
                sl = slice(cc * HGRN_CHUNK, (cc + 1) * HGRN_CHUNK)
                n = i * CPB + cc
                st_ref[n] = _tn(vb[sl], ke[sl])
                dst_ref[n] = _tn(dob[sl], qd[sl])
                dec_ref[n] = dec[cc * HGRN_CHUNK:cc * HGRN_CHUNK + 8, :]
            return carry

        _block_loop(T, block1, 0)

        def scan(t, s):
            n = jnp.where(d == 0, t, NC - 1 - t)
            u = st_ref[n]
            st_ref[n] = s
            return dec_ref[n][0:1, :] * s + u

        lax.fori_loop(0, NC, scan, jnp.zeros((HEAD_DIM, HEAD_DIM), F32))

        def rscan(t, ds):
            n = jnp.where(d == 0, NC - 1 - t, t)
            w = dst_ref[n]
            dst_ref[n] = ds
            ddec_ref[n] = jnp.broadcast_to(_colsum(ds * st_ref[n]), (8, HEAD_DIM))
            return dec_ref[n][0:1, :] * ds + w

        lax.fori_loop(0, NC, rscan, jnp.zeros((HEAD_DIM, HEAD_DIM), F32))

        def block3(i, dlb):
            rows = pl.ds(pl.multiple_of(i * HGRN_BLOCK, HGRN_BLOCK), HGRN_BLOCK)
            s, sn, fg, k, bl, eb, enb, ee, qd, kd, ke = values(rows)
            qdb, kdb, keb = qd.astype(BF16), kd.astype(BF16), ke.astype(BF16)
            vb, dob = v_ref[rows, :].astype(BF16), do_ref[rows, :].astype(BF16)
            att = jnp.where(mask, _nt(qdb, kdb), 0.0).astype(BF16)
            datt = jnp.where(mask, _nt(dob, vb), 0.0).astype(BF16)
            dv = _tn(att, dob)
            dqd = _nn(datt, kdb)
            dkd = _tn(datt, qdb)
            dv_i, dqd_i, dke, ddl = [], [], [], []
            for cc in range(CPB):
                sl = slice(cc * HGRN_CHUNK, (cc + 1) * HGRN_CHUNK)
                n = i * CPB + cc
                st_b, dst_b = st_ref[n].astype(BF16), dst_ref[n].astype(BF16)
                dv_i.append(_nt(keb[sl], dst_b))
                dqd_i.append(_nn(dob[sl], st_b))
                dke.append(_nn(vb[sl], dst_b))
                ddl.append(jnp.broadcast_to(ddec_ref[n][0:1, :] * dec_ref[n][0:1, :], (HGRN_CHUNK, HEAD_DIM)))
            dv = dv + jnp.concatenate(dv_i, axis=0)
            dqd = dqd + jnp.concatenate(dqd_i, axis=0)
            dke = jnp.concatenate(dke, axis=0)
            dq = dqd * eb * Q_SCALE
            dk = dkd * enb + dke * ee
            t_end = dke * ke
            db = dqd * qd - dkd * kd - t_end
            dlf = _chunk_cumsum(db, d == 0) + _chunk_total(t_end) + jnp.concatenate(ddl, axis=0)
            e = dlf / fg - dk
            dlg_ref[rows, :] = (oml * e * s * sn).astype(BF16)

            dq = jnp.where(d == 0, 0.0, dqa_ref[rows, :]) + dq
            dv = jnp.where(d == 0, 0.0, dva_ref[rows, :]) + dv
            dqa_ref[rows, :] = dq
            dva_ref[rows, :] = dv
            dq_ref[rows, :] = dq.astype(BF16)
            dv_ref[rows, :] = dv.astype(BF16)

            return dlb + _colsum(e * sn)

        dlb_ref[...] = _block_loop(T, block3, jnp.zeros((1, HEAD_DIM), F32))

    head = pl.BlockSpec((T, HEAD_DIM), lambda h, d: (0, h))
    big = pltpu.VMEM((NC, HEAD_DIM, HEAD_DIM), F32)
    small = pltpu.VMEM((NC, 8, HEAD_DIM), F32)
    acc = pltpu.VMEM((T, HEAD_DIM), F32)
    outs, landed = _pallas(
        body, name="hgrn_bwd", grid=(N_HEADS, 2),
        out_shape=[jax.ShapeDtypeStruct((T, W), BF16), jax.ShapeDtypeStruct((T, W), BF16), jax.ShapeDtypeStruct((T, 2 * W), BF16),
                   jax.ShapeDtypeStruct((2, 1, W), F32)],
        in_specs=[l_spec, col(COL_Q), f_spec, col(COL_V), head],
        out_specs=[head, head, pl.BlockSpec((T, HEAD_DIM), lambda h, d: (0, N_HEADS * d + h)),
                   pl.BlockSpec((None, 1, HEAD_DIM), lambda h, d: (d, 0, h))],
        scratch=[big, big, small, small, acc, acc], semantics=("parallel", "arbitrary"), operands=(lb_logits, proj, proj, proj, do), comm=comm)
    return outs if comm is None else (outs, landed)


def mix_norm_bwd(da1, h0, dh1, g_pre, sc1):
    T, D = h0.shape
    tm = min(256, T)

    def body(da_ref, h_ref, dh_ref, g_ref, sc_ref, gx_ref, s_sh, s_sc, s_g):
        @pl.when(pl.program_id(0) == 0)
        def _():
            for s in (s_sh, s_sc, s_g):
                s[...] = jnp.zeros_like(s)

        h, da = h_ref[...], da_ref[...]
        g, sc = g_ref[...], sc_ref[...]
        r = lax.rsqrt(jnp.mean(h * h, axis=-1, keepdims=True) + EPS)
        n = h * r
        s_sh[...] += _colsum(da)
        s_sc[...] += _colsum(da * (n * g))
        s_g[...] += _colsum(da * (1.0 + sc) * n)
        dn = da * g * (1.0 + sc)
        gx_ref[...] = dh_ref[...] + r * (dn - n * jnp.mean(dn * n, axis=-1, keepdims=True))

    row = pl.BlockSpec((tm, D), lambda i: (i, 0))
    return pl.pallas_call(
        body, name="mix_norm_bwd", grid=(T // tm,),
        out_shape=[jax.ShapeDtypeStruct((T, D), F32)] + [jax.ShapeDtypeStruct((1, D), F32)] * 3,
        in_specs=[row, row, row, _vec(D), _vec(D)], out_specs=[row] + [_vec(D)] * 3, compiler_params=_cp("arbitrary"),
    )(da1, h0, dh1, g_pre, sc1)


def adamw(w, g, m, v, name):
    R, C = w.shape
    tr = R if R * C * 4 <= (1 << 21) else max(8, ((1 << 21) // (C * 4)) // 8 * 8)
    while R % tr:
        tr -= 8

    def body(w_ref, g_ref, m_ref, v_ref, d_ref, m2_ref, v2_ref):
        d_ref[...], m2_ref[...], v2_ref[...] = _adamw(w_ref[...], g_ref[...], m_ref[...], v_ref[...])

    row = pl.BlockSpec((tr, C), lambda i: (i, 0))
    return pl.pallas_call(
        body, name=name, grid=(R // tr,), out_shape=[jax.ShapeDtypeStruct((R, C), F32)] * 3,
        in_specs=[row] * 4, out_specs=[row] * 3, compiler_params=_cp("parallel"),
    )(w, g, m, v)


def wada_update(c_all, dmod, w, m, v):
    D, N = w.shape
    tm, tn = 512, 1024

    def body(c_ref, dm_ref, w_ref, m_ref, v_ref, g_ref, d_ref, m2_ref, v2_ref):
        c = c_ref[...]
        g = lax.dot_general(c * jax.nn.sigmoid(c), dm_ref[...], (((0,), (0,)), ((), ())), precision=HI, preferred_element_type=F32)
        g_ref[...] = g
        d_ref[...], m2_ref[...], v2_ref[...] = _adamw(w_ref[...], g, m_ref[...], v_ref[...])

    blk = pl.BlockSpec((tm, tn), lambda i, j: (i, j))
    return pl.pallas_call(
        body, name="wada_update", grid=(D // tm, N // tn), out_shape=[jax.ShapeDtypeStruct((D, N), F32)] * 4,
        in_specs=[pl.BlockSpec((8, tm), lambda i, j: (0, i)), pl.BlockSpec((8, tn), lambda i, j: (0, j)), blk, blk, blk],
        out_specs=[blk] * 4, compiler_params=_cp("parallel", "parallel"),
    )(c_all, dmod, w, m, v)


def sum_devices(gathered, name):
    n, R, C = gathered.shape

    def body(g_ref, o_ref):
        s = g_ref[0]
        for i in range(1, n):
            s = s + g_ref[i]
        o_ref[...] = s

    return pl.pallas_call(body, name=name, out_shape=jax.ShapeDtypeStruct((R, C), F32), compiler_params=_cp())(gathered)


def lb_logits_grad(dlb, lb_logits):
    def body(d_ref, l_ref, o_ref):
        for d in range(2):
            l0, l1 = l_ref[d, 0:1, :], l_ref[d, 1:2, :]
            m = jnp.maximum(l0, l1)
            e0, e1 = jnp.exp(l0 - m), jnp.exp(l1 - m)
            p0, p1 = e0 / (e0 + e1), e1 / (e0 + e1)
            g = d_ref[d:d + 1, :]
            o_ref[d, 0:1, :] = p0 * (g - p0 * g)
            o_ref[d, 1:2, :] = -p1 * (p0 * g)

    return pl.pallas_call(body, name="lb_logits_grad", out_shape=jax.ShapeDtypeStruct(lb_logits.shape, F32), compiler_params=_cp())(dlb, lb_logits)


def add_halves(g, landed, core):
    nj, _, r, cc = g.shape
    tr = min(256, r)

    def body(core_ref, g_ref, l_ref, o_ref):
        o_ref[...] = (g_ref[...].astype(F32) + l_ref[...].astype(F32)).astype(BF16)

    return pl.pallas_call(
        body, name="add_halves_%dx%d" % (r, cc), out_shape=jax.ShapeDtypeStruct((nj, r, cc), BF16),
        grid_spec=pltpu.PrefetchScalarGridSpec(
            num_scalar_prefetch=1, grid=(nj, r // tr),
            in_specs=[pl.BlockSpec((None, None, tr, cc), lambda j, i, core_ref: (j, core_ref[0], i, 0)),
                      pl.BlockSpec((None, None, tr, cc), lambda j, i, core_ref: (j, 0, i, 0))],
            out_specs=pl.BlockSpec((None, tr, cc), lambda j, i, core_ref: (j, i, 0))),
        compiler_params=_cp("parallel", "parallel"),
    )(core, g, landed)


def sum_chips(parts, landed, chip):
    nj, r, cc = parts.shape
    tr = min(256, r)

    def body(chip_ref, p_ref, l_ref, o_ref):
        mine = p_ref[...].astype(F32)
        s = None
        for j in range(nj):
            t = jnp.where(chip_ref[0] == j, mine, l_ref[j].astype(F32))
            s = t if s is None else s + t
        o_ref[...] = s

    return pl.pallas_call(
        body, name="sum_chips_%dx%d" % (r, cc), out_shape=jax.ShapeDtypeStruct((r, cc), F32),
        grid_spec=pltpu.PrefetchScalarGridSpec(
            num_scalar_prefetch=1, grid=(r // tr,),
            in_specs=[pl.BlockSpec((None, tr, cc), lambda i, chip_ref: (chip_ref[0], i, 0)),
                      pl.BlockSpec((nj, tr, cc), lambda i, chip_ref: (0, i, 0))],
            out_specs=pl.BlockSpec((tr, cc), lambda i, chip_ref: (i, 0))),
        compiler_params=_cp("parallel"),
    )(chip, parts, landed)


def adamw_halves(w, own, other, m, v, core, name):
    r, cc = own.shape
    tr = min(128, r)
    nb = r // tr

    def body(core_ref, w_ref, a_ref, b_ref, m_ref, v_ref, g_ref, d_ref, m2_ref, v2_ref):
        g = jnp.where(pl.program_id(0) == core_ref[0], a_ref[...], b_ref[...])
        g_ref[...] = g
        d_ref[...], m2_ref[...], v2_ref[...] = _adamw(w_ref[...], g, m_ref[...], v_ref[...])

    full = pl.BlockSpec((tr, cc), lambda h, i, core_ref: (h * nb + i, 0))
    half = pl.BlockSpec((tr, cc), lambda h, i, core_ref: (i, 0))
    return pl.pallas_call(
        body, name=name, out_shape=[jax.ShapeDtypeStruct((2 * r, cc), F32)] * 4,
        grid_spec=pltpu.PrefetchScalarGridSpec(
            num_scalar_prefetch=1, grid=(2, nb), in_specs=[full, half, half, full, full], out_specs=[full] * 4),
        compiler_params=_cp("parallel", "parallel"),
    )(core, w, own, other, m, v)


def _place():
    mx, my, mc = lax.axis_index("x"), lax.axis_index("y"), lax.axis_index("c")
    chips = [(1 - mx, my), (mx, 1 - my), (1 - mx, 1 - my)]
    return mx, my, mc, chips


def all_gather_small(x, name):
    R, C = x.shape

    def body(x_ref, out_ref, send_sems, recv_sems, local_sem):
        mx, my, mc, _ = _place()
        me = 4 * mx + 2 * my + mc
        mine = pltpu.make_async_copy(x_ref, out_ref.at[me], local_sem)
        mine.start()

        def peer(k):
            px = 1 - mx if k & 4 else mx
            py = 1 - my if k & 2 else my
            pc = 1 - mc if k & 1 else mc
            return px, py, pc

        def copy(k, src, slot):
            return pltpu.make_async_remote_copy(src_ref=src, dst_ref=out_ref.at[slot], send_sem=send_sems.at[k - 1],
                                                recv_sem=recv_sems.at[k - 1], device_id=peer(k), device_id_type=MESH)

        sends = [copy(k, x_ref, me) for k in range(1, 8)]
        for cp in sends:
            cp.start()
        for k in range(1, 8):
            px, py, pc = peer(k)
            slot = 4 * px + 2 * py + pc
            copy(k, out_ref.at[slot], slot).wait_recv()
        for cp in sends:
            cp.wait_send()
        mine.wait()

    return pl.pallas_call(
        body, name=name, out_shape=jax.ShapeDtypeStruct((8, R, C), F32),
        in_specs=[pl.BlockSpec(memory_space=pltpu.VMEM)], out_specs=pl.BlockSpec(memory_space=pltpu.VMEM),
        scratch_shapes=[pltpu.SemaphoreType.DMA((7,)), pltpu.SemaphoreType.DMA((7,)), pltpu.SemaphoreType.DMA],
        compiler_params=_cp(),
    )(x)


def gather8_comm(x):
    def copies(x_ref, out_ref, send_sems, recv_sems):
        mx, my, mc, _ = _place()
        me = 4 * mx + 2 * my + mc

        def peer(k):
            return (1 - mx if k & 4 else mx, 1 - my if k & 2 else my, 1 - mc if k & 1 else mc)

        def copy(k, src, slot):
            return pltpu.make_async_remote_copy(src_ref=src, dst_ref=out_ref.at[slot], send_sem=send_sems.at[k - 1],
                                                recv_sem=recv_sems.at[k - 1], device_id=peer(k), device_id_type=MESH)

        sends = [copy(k, x_ref, me) for k in range(1, 8)]
        arrivals = []
        for k in range(1, 8):
            px, py, pc = peer(k)
            slot = 4 * px + 2 * py + pc
            arrivals.append(copy(k, out_ref.at[slot], slot))
        return sends, arrivals, pltpu.make_async_copy(x_ref, out_ref.at[me], send_sems.at[7])

    def start(cin, cout, send_sems, recv_sems):
        sends, _, mine = copies(cin[0], cout[0], send_sems, recv_sems)
        mine.start()
        for cp in sends:
            cp.start()

    def finish(cin, cout, send_sems, recv_sems):
        sends, arrivals, mine = copies(cin[0], cout[0], send_sems, recv_sems)
        for cp in arrivals:
            cp.wait_recv()
        for cp in sends:
            cp.wait_send()
        mine.wait()

    return _Comm([x], [jax.ShapeDtypeStruct((8,) + x.shape, F32)], {}, 8, start, finish)


def _join(a, b):
    na_in, na_out = len(a.operands), len(a.out_shape)

    def split(fn_a, fn_b):
        def both(cin, cout, send_sems, recv_sems):
            fn_a(cin[:na_in], cout[:na_out], send_sems.at[pl.ds(0, a.n_sems)], recv_sems.at[pl.ds(0, a.n_sems)])
            fn_b(cin[na_in:], cout[na_out:], send_sems.at[pl.ds(a.n_sems, b.n_sems)], recv_sems.at[pl.ds(a.n_sems, b.n_sems)])
        return both

    aliases = dict(a.aliases)
    aliases.update({na_in + i: na_out + o for i, o in b.aliases.items()})
    return _Comm(a.operands + b.operands, a.out_shape + b.out_shape, aliases, a.n_sems + b.n_sems, split(a.start, b.start), split(a.finish, b.finish))


def _region(ref, kind, j, half, r, cc):
    nr = r if half is None else r // 2
    off = 0 if half is None else half * nr
    if kind == "col":
        return ref.at[pl.ds(off, nr), pl.ds(pl.multiple_of(j * cc, 128), cc)]
    return ref.at[pl.ds(pl.multiple_of(j * r + off, 16), nr), :]


def comm_call(comm, name):
    ni, no = len(comm.operands), len(comm.out_shape)

    def body(*refs):
        comm.start(refs[:ni], refs[ni:ni + no], *refs[ni + no:])
        comm.finish(refs[:ni], refs[ni:ni + no], *refs[ni + no:])

    return pl.pallas_call(
        body, name=name, out_shape=comm.out_shape, in_specs=[ANY] * ni, out_specs=[ANY] * no, input_output_aliases=comm.aliases,
        scratch_shapes=[pltpu.SemaphoreType.DMA((comm.n_sems,)), pltpu.SemaphoreType.DMA((comm.n_sems,))], compiler_params=_cp(),
    )(*comm.operands)


def gather_comm(fulls, kinds, dims):
    n = len(fulls)

    def copies(f_refs, send_sems, recv_sems):
        mx, my, mc, chips = _place()
        jme = 2 * mx + my

        def landed(w, k, half):
            px, py = chips[k]
            return _region(f_refs[w], kinds[w], 2 * px + py, half, *dims[w])

        def over_ici(w, k, reg):
            px, py = chips[k]
            return pltpu.make_async_remote_copy(src_ref=reg, dst_ref=reg, send_sem=send_sems.at[6 * w + k], recv_sem=recv_sems.at[6 * w + k],
                                                device_id=(px, py, mc), device_id_type=MESH)

        def over_d2d(w, k, half):
            reg = landed(w, k, half)
            return pltpu.make_async_remote_copy(src_ref=reg, dst_ref=reg, send_sem=send_sems.at[6 * w + 3 + k],
                                                recv_sem=recv_sems.at[6 * w + 3 + k], device_id=(mx, my, 1 - mc), device_id_type=MESH)

        sends = [over_ici(w, k, _region(f_refs[w], kinds[w], jme, mc, *dims[w])) for w in range(n) for k in range(3)]
        return mc, landed, over_ici, over_d2d, sends

    def start(cin, f_refs, send_sems, recv_sems):
        for cp in copies(f_refs, send_sems, recv_sems)[4]:
            cp.start()

    def finish(cin, f_refs, send_sems, recv_sems):
        mc, landed, over_ici, over_d2d, sends = copies(f_refs, send_sems, recv_sems)
        passed = []
        for w in range(n):
            for k in range(3):
                over_ici(w, k, landed(w, k, mc)).wait_recv()
                cp = over_d2d(w, k, mc)
                cp.start()
                passed.append(cp)
        for w in range(n):
            for k in range(3):
                over_d2d(w, k, 1 - mc).wait_recv()
        for cp in sends + passed:
            cp.wait_send()

    return _Comm(fulls, [jax.ShapeDtypeStruct(f.shape, BF16) for f in fulls], {w: w for w in range(n)}, 6 * n, start, finish)


def exchange_halves(grads, name):
    n = len(grads)

    def body(*refs):
        g_refs, l_refs = refs[:n], refs[n:2 * n]
        send_sems, recv_sems = refs[2 * n:]
        mx, my, mc, _ = _place()
        cps = [pltpu.make_async_remote_copy(src_ref=g_refs[w].at[:, pl.ds(1 - mc, 1)], dst_ref=l_refs[w], send_sem=send_sems.at[w],
                                            recv_sem=recv_sems.at[w], device_id=(mx, my, 1 - mc), device_id_type=MESH) for w in range(n)]
        for cp in cps:
            cp.start()
        for cp in cps:
            cp.wait()

    return pl.pallas_call(
        body, name=name, out_shape=[jax.ShapeDtypeStruct((g.shape[0], 1) + g.shape[2:], BF16) for g in grads],
        in_specs=[ANY] * n, out_specs=[ANY] * n,
        scratch_shapes=[pltpu.SemaphoreType.DMA((n,)), pltpu.SemaphoreType.DMA((n,))], compiler_params=_cp(),
    )(*grads)


def scatter_comm(parts):
    n = len(parts)

    def sends(p_refs, l_refs, send_sems, recv_sems):
        mx, my, mc, chips = _place()
        return [pltpu.make_async_remote_copy(src_ref=p_refs[w].at[2 * px + py], dst_ref=l_refs[w].at[2 * mx + my],
                                             send_sem=send_sems.at[3 * w + k], recv_sem=recv_sems.at[3 * w + k],
                                             device_id=(px, py, mc), device_id_type=MESH) for w in range(n) for k, (px, py) in enumerate(chips)]

    def start(p_refs, l_refs, send_sems, recv_sems):
        for cp in sends(p_refs, l_refs, send_sems, recv_sems):
            cp.start()

    def finish(p_refs, l_refs, send_sems, recv_sems):
        mx, my, mc, chips = _place()
        for w in range(n):
            for k, (px, py) in enumerate(chips):
                slot = l_refs[w].at[2 * px + py]
                pltpu.make_async_remote_copy(src_ref=slot, dst_ref=slot, send_sem=send_sems.at[3 * w + k], recv_sem=recv_sems.at[3 * w + k],
                                             device_id=(px, py, mc), device_id_type=MESH).wait_recv()
        for cp in sends(p_refs, l_refs, send_sems, recv_sems):
            cp.wait_send()

    return _Comm(parts, [jax.ShapeDtypeStruct(p.shape, BF16) for p in parts], {}, 3 * n, start, finish)


def share_comm(sums):
    n = len(sums)

    def copies(q_refs, o_refs, send_sems, recv_sems):
        mx, my, mc, _ = _place()
        return [pltpu.make_async_remote_copy(src_ref=q_refs[w], dst_ref=o_refs[w], send_sem=send_sems.at[w], recv_sem=recv_sems.at[w],
                                             device_id=(mx, my, 1 - mc), device_id_type=MESH) for w in range(n)]

    def start(*refs):
        for cp in copies(*refs):
            cp.start()

    def finish(*refs):
        for cp in copies(*refs):
            cp.wait()

    return _Comm(sums, [jax.ShapeDtypeStruct(q.shape, F32) for q in sums], {}, n, start, finish)


def _pack(arrays):
    flat = jnp.concatenate([a.reshape(-1) for a in arrays])
    rows = -(-flat.shape[0] // 1024) * 8
    return jnp.pad(flat, (0, rows * 128 - flat.shape[0])).reshape(rows, 128)


def _unpack(packed, shapes):
    flat, out, off = packed.reshape(-1), [], 0
    for s in shapes:
        n = math.prod(s)
        out.append(flat[off:off + n].reshape(s))
        off += n
    return out


def kernel(x, c, w_ada, b_ada, g_pre_mix, g_post_mix, g_pre_ffn, g_post_ffn, w_in, lb_logits, g_hgrn_norm, w_a_out, g_sgu_norm, w_spatial, b_spatial, w_b_out, w_o, w_ff1, w_ff2, loss_target, m_w_ada, m_b_ada, m_g_pre_mix, m_g_post_mix, m_g_pre_ffn, m_g_post_ffn, m_w_in, m_lb_logits, m_g_hgrn_norm, m_w_a_out, m_g_sgu_norm, m_w_spatial, m_b_spatial, m_w_b_out, m_w_o, m_w_ff1, m_w_ff2, v_w_ada, v_b_ada, v_g_pre_mix, v_g_post_mix, v_g_pre_ffn, v_g_post_ffn, v_w_in, v_lb_logits, v_g_hgrn_norm, v_w_a_out, v_g_sgu_norm, v_w_spatial, v_b_spatial, v_w_b_out, v_w_o, v_w_ff1, v_w_ff2):
    mx, my, mc = lax.axis_index("x"), lax.axis_index("y"), lax.axis_index("c")
    chip, me = 2 * mx + my, 4 * mx + 2 * my + mc
    D = D_MODEL
    h0, tgt = x[0], loss_target[0]
    n_ada = w_ada.shape[2]
    n_lb = lb_logits.shape[2]

    got = all_gather_small(_pack([c, lb_logits]), "gather_inputs")
    c_all = got[:, :D // 128, :].reshape(8, D)
    lb_full = got[0::2, D // 128:D // 128 + 4 * n_lb // 128, :].reshape(4, 2, 2, n_lb).transpose(1, 2, 0, 3).reshape(2, 2, 4 * n_lb)
    b_ada_chip = lax.dynamic_slice(b_ada, (0, chip * n_ada), (1, n_ada))
    mod_cols = mod_matmul(c_all, w_ada[0], b_ada_chip)
    got = all_gather_small(mod_cols.reshape(-1, 128), "gather_mod").reshape(4, 2, 8, n_ada)
    mod = lax.dynamic_index_in_dim(got[:, 0], me, axis=1, keepdims=False).reshape(6, 1, D)
    sh1, sc1, gt1, sh2, sc2, gt2 = (mod[i] for i in range(6))

    big = [("w_in", w_in, "col"), ("w_a_out", w_a_out, "col"), ("w_b_out", w_b_out, "col"), ("w_o", w_o, "row"),
           ("w_ff1", w_ff1, "col"), ("w_ff2", w_ff2, "row")]
    kinds = [k for _, _, k in big]
    chip_idx, core = chip.reshape(1).astype(jnp.int32), mc.reshape(1).astype(jnp.int32)
    fulls = [cast_into_full(w[0], kind, chip_idx, "cast_" + nm) for nm, w, kind in big]
    dims = [w.shape[1:] for _, w, _ in big]
    later = lambda lo, hi: gather_comm(fulls[lo:hi], kinds[lo:hi], dims[lo:hi])
    halves_summed = lambda grads, name: [add_halves(g, l, core) for g, l in zip(grads, exchange_halves(grads, name))]

    bst = b_spatial[0].T
    a1 = prenorm(h0, g_pre_mix, sc1, sh1)
    proj, w_in_f, (w_a_f, w_b_f, w_o_f) = in_proj_gathered(a1, fulls[0], chip_idx, dims[0], later(1, 4))
    o, (w_ff1_f,) = hgrn_fwd(proj, lb_full, comm=later(4, 5))
    ya_pre = hgrn_post_fwd(o, proj, g_hgrn_norm)
    sgu = sgu_fwd(proj, g_sgu_norm, w_spatial[0], bst)
    y_a, y_b, merged = merge_matmul(ya_pre, sgu, w_a_f, w_b_f, proj)
    mo, h1 = out_proj(merged, w_o_f, h0, gt1, g_post_mix)
    (f1, a2, hid), (w_ff2_f,) = prenorm_matmul(h1, g_pre_ffn, sc2, sh2, w_ff1_f, relu2=True, name="ff1", comm=later(5, 6))
    dy, dff, loss_parts, d_gt2, d_g_post_ffn = ff2_loss(hid, w_ff2_f, h1, tgt, gt2, g_post_ffn)
    loss = lax.psum(0.5 * loss_parts[0, 0] / D, ("x", "y", "c"))

    df1 = ff2_bwd(dff, w_ff2_f, f1)
    gr_ff2 = matmul(hid, dff, mode="tn", out_dtype=BF16, tm=1024, tn=1024, tk=2048, name="dw_ff2")
    da2 = matmul(df1, w_ff1_f, mode="nt", out_dtype=F32, tm=1024, tn=1024, tk=2048, name="da2")
    gr_ff1 = matmul(a2, df1, mode="tn", out_dtype=BF16, tm=1024, tn=2048, tk=1024, name="dw_ff1", split=(4, 2))
    parts_ff = halves_summed([gr_ff1, gr_ff2.reshape(4, 2, -1, D)], "exchange_ff")
    dh1, dmo, d_sh2, d_sc2, d_g_pre_ffn, d_gt1, d_g_post_mix = ffn_norm_bwd(dy, da2, h1, mo, g_pre_ffn, sc2, gt1, g_post_mix)
    dya, dyb, dga, dgb = out_proj_bwd(dmo, w_o_f, y_a, y_b, proj)
    gr_o = matmul(merged, dmo, mode="tn", out_dtype=BF16, tm=1024, tn=1024, tk=2048, name="dw_o")
    dsgu = matmul(dyb, w_b_f, mode="nt", out_dtype=F32, tm=512, tn=1024, tk=2048, name="dsgu")
    gr_b = matmul(sgu, dyb, mode="tn", out_dtype=BF16, tm=512, tn=512, tk=4096, name="dw_b_out", split=(4, 2))
    dz, d_w_spatial, d_b_spatial, d_g_sgu = sgu_bwd(proj, dsgu, g_sgu_norm, w_spatial[0], bst)
    dya_pre = matmul(dya, w_a_f, mode="nt", out_dtype=F32, tm=512, tn=1024, tk=2048, name="dya_pre")
    gr_a = matmul(ya_pre, dya, mode="tn", out_dtype=BF16, tm=512, tn=512, tk=4096, name="dw_a_out", split=(4, 2))
    parts_mix = halves_summed([gr_a, gr_b, gr_o.reshape(4, 2, -1, D)], "exchange_mix")
    do, dog, d_g_hgrn = hgrn_post_bwd(dya_pre, o, proj, g_hgrn_norm)
    chips_summed = lambda parts, landed: [sum_chips(p, l, chip_idx) for p, l in zip(parts, landed)]
    (dq, dv, dlg, d_lb), landed_ff = hgrn_bwd(proj, do, lb_full, comm=scatter_comm(parts_ff))
    own_ff = chips_summed(parts_ff, landed_ff)
    dproj = jnp.concatenate([dq, dlg, dv, dog, dz, dga, dgb], axis=1)
    early = _pack([d_g_sgu, d_w_spatial, d_b_spatial[:, 0, :]])
    gr_in, (*landed_mix, got_early) = matmul(a1, dproj, mode="tn", out_dtype=BF16, tm=1024, tn=2816, tk=1024, name="dw_in", split=(4, 2),
                                             comm=_join(scatter_comm(parts_mix), gather8_comm(early)))
    own_mix = chips_summed(parts_mix, landed_mix)
    parts_in = halves_summed([gr_in], "exchange_in")
    da1, (landed_in, *other_rest) = matmul(dproj, w_in_f, mode="nt", out_dtype=F32, tm=1024, tn=1024, tk=2816, name="da1",
                                           comm=_join(scatter_comm(parts_in), share_comm(own_mix + own_ff)))
    own_in = chips_summed(parts_in, [landed_in])
    other_in = comm_call(share_comm(own_in), "share_w_in")
    own, other = own_in + own_mix + own_ff, list(other_in) + other_rest
    grad_x, d_sh1, d_sc1, d_g_pre_mix = mix_norm_bwd(da1, h0, dh1, g_pre_mix, sc1)
    out = {}

    mine = _pack([d_sh1, d_sc1, d_gt1, d_sh2, d_sc2, d_gt2, d_g_pre_mix, d_g_post_mix, d_g_pre_ffn, d_g_post_ffn, d_g_hgrn, d_lb])
    got = all_gather_small(mine, "gather_small_grads")
    g_b_ada, g_g1, g_g2, g_g3, g_g4, g_hg, g_lb = _unpack(
        sum_devices(got, "sum_small_grads"), [(1, 6 * D), (1, D), (1, D), (1, D), (1, D), (1, HEAD_DIM), (2, 1024)])
    g_sg, g_ws, g_bs = _unpack(sum_devices(got_early, "sum_sgu_grads"), [(1, 1024), w_spatial.shape, b_spatial.shape])
    g_lbl = lax.dynamic_slice(lb_logits_grad(g_lb, lb_full), (0, 0, chip * n_lb), (2, 2, n_lb))
    names = ["b_ada", "g_pre_mix", "g_post_mix", "g_pre_ffn", "g_post_ffn", "g_hgrn_norm", "g_sgu_norm", "w_spatial", "b_spatial", "lb_logits"]
    ws = [b_ada, g_pre_mix, g_post_mix, g_pre_ffn, g_post_ffn, g_hgrn_norm, g_sgu_norm, w_spatial, b_spatial, lb_logits]
    gs = [g_b_ada, g_g1, g_g2, g_g3, g_g4, g_hg, g_sg, g_ws, g_bs, g_lbl]
    ms = [m_b_ada, m_g_pre_mix, m_g_post_mix, m_g_pre_ffn, m_g_post_ffn, m_g_hgrn_norm, m_g_sgu_norm, m_w_spatial, m_b_spatial, m_lb_logits]
    vs = [v_b_ada, v_g_pre_mix, v_g_post_mix, v_g_pre_ffn, v_g_post_ffn, v_g_hgrn_norm, v_g_sgu_norm, v_w_spatial, v_b_spatial, v_lb_logits]
    shapes = [w.shape for w in ws]
    upd = adamw(_pack(ws), _pack(gs), _pack(ms), _pack(vs), "adamw_small")
    upd = [_unpack(u, shapes) for u in upd]
    for i, nm in enumerate(names):
        out[nm] = (gs[i], upd[0][i], upd[1][i], upd[2][i])

    dmod_all = got[:, :6 * D // 128, :].reshape(8, 6 * D)
    dmod_chip = lax.dynamic_slice(dmod_all, (0, chip * n_ada), (8, n_ada))
    out["w_ada"] = tuple(a[None] for a in wada_update(c_all, dmod_chip, w_ada[0], m_w_ada[0], v_w_ada[0]))
    for (nm, w, _), a, b, m, v in zip(big, own, other, (m_w_in, m_w_a_out, m_w_b_out, m_w_o, m_w_ff1, m_w_ff2),
                                      (v_w_in, v_w_a_out, v_w_b_out, v_w_o, v_w_ff1, v_w_ff2)):
        out[nm] = tuple(t[None] for t in adamw_halves(w[0], a, b, m[0], v[0], core, "adamw_" + nm))

    order = ["w_ada", "b_ada", "g_pre_mix", "g_post_mix", "g_pre_ffn", "g_post_ffn", "w_in", "lb_logits", "g_hgrn_norm", "w_a_out",
             "g_sgu_norm", "w_spatial", "b_spatial", "w_b_out", "w_o", "w_ff1", "w_ff2"]
    return (loss, grad_x[None], *[out[nm][0] for nm in order], *[out[nm][1] for nm in order], *[out[nm][2] for nm in order],
            *[out[nm][3] for nm in order])
```

```python
import functools
import math

import jax
import jax.numpy as jnp
from jax import lax
from jax.experimental import pallas as pl
from jax.experimental.pallas import tpu as pltpu

F32, BF16 = jnp.float32, jnp.bfloat16
HI = lax.Precision.HIGHEST
MESH = pl.DeviceIdType.MESH
ANY = pl.BlockSpec(memory_space=pl.ANY)

EPS = 1e-6
D_MODEL = 2048
N_HEADS = 8
HEAD_DIM = 128
HGRN_CHUNK = 32
HGRN_BLOCK = 256
SGU_CHUNK = 128
SGU_GROUPS = 8
Q_SCALE = HEAD_DIM ** -0.5
COL_Q, COL_FFW, COL_FBW, COL_V, COL_OG, COL_U, COL_ZV, COL_GA, COL_GB = 0, 1, 2, 3, 4, 5, 6, 7, 9
N_PROJ = 11264
VMEM_BYTES_V7X = 64 * 1024 * 1024
VMEM_LIMIT = VMEM_BYTES_V7X - 8 * 1024 * 1024

ADAM_LR, ADAM_B1, ADAM_B2, ADAM_EPS, ADAM_WD, ADAM_STEP = 0.001, 0.9, 0.999, 1e-08, 0.01, 10
ADAM_C1 = 1.0 - ADAM_B1 ** ADAM_STEP
ADAM_C2 = 1.0 - ADAM_B2 ** ADAM_STEP


def _cp(*sem):
    return pltpu.CompilerParams(dimension_semantics=sem if sem else None, vmem_limit_bytes=VMEM_LIMIT)


def _vec(d):
    return pl.BlockSpec((1, d), lambda *_: (0, 0))


def _colsum(x):
    return jnp.sum(x, axis=0, keepdims=True)


def _nt(a, b):
    return lax.dot_general(a, b, (((1,), (1,)), ((), ())), preferred_element_type=F32)


def _tn(a, b):
    return lax.dot_general(a, b, (((0,), (0,)), ((), ())), preferred_element_type=F32)


def _nn(a, b):
    return jnp.dot(a, b, preferred_element_type=F32)


def _adamw(w, g, m, v):
    m2 = ADAM_B1 * m + (1.0 - ADAM_B1) * g
    v2 = ADAM_B2 * v + (1.0 - ADAM_B2) * (g * g)
    delta = -ADAM_LR * ((m2 / ADAM_C1) / (jnp.sqrt(v2 / ADAM_C2) + ADAM_EPS) + ADAM_WD * w)
    return delta, m2, v2


class _Comm:
    def __init__(self, operands, out_shape, aliases, n_sems, start, finish):
        self.operands, self.out_shape, self.aliases, self.n_sems = list(operands), list(out_shape), dict(aliases), n_sems
        self.start, self.finish = start, finish


def _pallas(body, *, name, grid, in_specs, out_specs, out_shape, scratch, semantics, operands, comm=None):
    if comm is None:
        res = pl.pallas_call(body, name=name, grid=grid, in_specs=in_specs, out_specs=out_specs, out_shape=out_shape,
                             scratch_shapes=scratch, compiler_params=_cp(*semantics))(*operands)
        return res, []
    n_in, n_out, n_scr = len(in_specs), len(out_specs), len(scratch)
    nci, nco = len(comm.operands), len(comm.out_shape)

    def with_comm(*refs):
        ins, rest = refs[:n_in], refs[n_in:]
        cin, rest = rest[:nci], rest[nci:]
        outs, rest = rest[:n_out], rest[n_out:]
        cout, rest = rest[:nco], rest[nco:]
        scr, (send, recv) = rest[:n_scr], rest[n_scr:]
        ids = [pl.program_id(a) for a in range(len(grid))]
        first = functools.reduce(jnp.logical_and, [i == 0 for i in ids])
        last = functools.reduce(jnp.logical_and, [i == g - 1 for i, g in zip(ids, grid)])

        @pl.when(first)
        def _():
            comm.start(cin, cout, send, recv)

        body(*ins, *outs, *scr)

        @pl.when(last)
        def _():
            comm.finish(cin, cout, send, recv)

    res = pl.pallas_call(
        with_comm, name=name, grid=grid, in_specs=list(in_specs) + [ANY] * nci, out_specs=list(out_specs) + [ANY] * nco,
        out_shape=list(out_shape) + comm.out_shape, input_output_aliases={n_in + i: n_out + o for i, o in comm.aliases.items()},
        scratch_shapes=list(scratch) + [pltpu.SemaphoreType.DMA((comm.n_sems,)), pltpu.SemaphoreType.DMA((comm.n_sems,))],
        compiler_params=_cp(*["arbitrary"] * len(grid)),
    )(*operands, *comm.operands)
    return res[:n_out], res[n_out:]


def matmul(a, b, *, mode, out_dtype, tm, tn, tk, name, split=None, comm=None, b_stacked=False):
    if mode == "tn":
        (K, M), (_, N) = a.shape, b.shape
    elif mode == "nt":
        (M, K), (N, _) = a.shape, b.shape
        N = N // 4 if b_stacked else N
    else:
        (M, K), (_, N) = a.shape, b.shape
    tm, tn, tk = min(tm, M), min(tn, N), min(tk, K)
    if b_stacked:
        tk = K // 4
    nk = K // tk
    a_spec = pl.BlockSpec((tk, tm), lambda i, j, k: (k, i)) if mode == "tn" else pl.BlockSpec((tm, tk), lambda i, j, k: (i, k))
    b_spec = pl.BlockSpec((tn, tk), lambda i, j, k: (j, k)) if mode == "nt" else pl.BlockSpec((tk, tn), lambda i, j, k: (k, j))
    if b_stacked:
        b_spec = pl.BlockSpec((tn, tk), lambda i, j, k: (k * (N // tn) + j, 0))
    dot = {"nn": _nn, "nt": _nt, "tn": _tn}[mode]
    if split is None:
        out_shape = jax.ShapeDtypeStruct((M, N), out_dtype)
        out_spec = pl.BlockSpec((tm, tn), lambda i, j, k: (i, j))
    else:
        nj, nh = split
        rows, cols = M // nh, N // nj
        tm, tn = min(tm, rows), min(tn, cols)
        bi, bj = rows // tm, cols // tn
        out_shape = jax.ShapeDtypeStruct((nj, nh, rows, cols), out_dtype)
        out_spec = pl.BlockSpec((None, None, tm, tn), lambda i, j, k: (j // bj, i // bi, i % bi, j % bj))

    def body(a_ref, b_ref, o_ref, acc_ref):
        k = pl.program_id(2)

        @pl.when(k == 0)
        def _():
            acc_ref[...] = jnp.zeros_like(acc_ref)

        acc_ref[...] += dot(a_ref[...], b_ref[...])

        @pl.when(k == nk - 1)
        def _():
            o_ref[...] = acc_ref[...].astype(o_ref.dtype)

    (out,), landed = _pallas(
        body, name=name, grid=(M // tm, N // tn, nk), in_specs=[a_spec, b_spec], out_specs=[out_spec], out_shape=[out_shape],
        scratch=[pltpu.VMEM((tm, tn), F32)], semantics=("parallel", "parallel", "arbitrary"), operands=(a, b), comm=comm)
    return out if comm is None else (out, landed)


def cast_into_full(w, kind, chip, name):
    r, cc = w.shape
    tr = min(r, 512)
    nb = r // tr

    def body(chip_ref, w_ref, o_ref):
        o_ref[...] = w_ref[...].astype(BF16)

    if kind == "col":
        full, out_map = (r, 4 * cc), lambda i, chip_ref: (i, chip_ref[0])
    else:
        full, out_map = (4 * r, cc), lambda i, chip_ref: (chip_ref[0] * nb + i, 0)
    return pl.pallas_call(
        body, name=name, out_shape=jax.ShapeDtypeStruct(full, BF16),
        grid_spec=pltpu.PrefetchScalarGridSpec(
            num_scalar_prefetch=1, grid=(nb,), in_specs=[pl.BlockSpec((tr, cc), lambda i, chip_ref: (i, 0))],
            out_specs=pl.BlockSpec((tr, cc), out_map)),
        compiler_params=_cp("parallel"),
    )(chip, w)


def mod_matmul(c_all, w_ada, b_ada):
    D, N = w_ada.shape
    tn = 1024

    def body(c_ref, w_ref, b_ref, o_ref):
        c = c_ref[...]
        sc = c * jax.nn.sigmoid(c)
        o_ref[...] = jnp.dot(sc, w_ref[...], precision=HI, preferred_element_type=F32) + b_ref[...]

    return pl.pallas_call(
        body, name="mod_matmul", out_shape=jax.ShapeDtypeStruct((8, N), F32), grid=(N // tn,),
        in_specs=[pl.BlockSpec((8, D), lambda j: (0, 0)), pl.BlockSpec((D, tn), lambda j: (0, j)),
                  pl.BlockSpec((1, tn), lambda j: (0, j))],
        out_specs=pl.BlockSpec((8, tn), lambda j: (0, j)), compiler_params=_cp("parallel"),
    )(c_all, w_ada, b_ada)


def prenorm(h, g, sc, sh):
    T, D = h.shape
    tm = min(256, T)

    def body(h_ref, g_ref, sc_ref, sh_ref, a_ref):
        x = h_ref[...]
        r = lax.rsqrt(jnp.mean(x * x, axis=-1, keepdims=True) + EPS)
        a_ref[...] = ((x * r) * g_ref[...] * (1.0 + sc_ref[...]) + sh_ref[...]).astype(BF16)

    row = pl.BlockSpec((tm, D), lambda i: (i, 0))
    return pl.pallas_call(
        body, name="prenorm", out_shape=jax.ShapeDtypeStruct((T, D), BF16), grid=(T // tm,),
        in_specs=[row, _vec(D), _vec(D), _vec(D)], out_specs=row, compiler_params=_cp("parallel"),
    )(h, g, sc, sh)


def in_proj_gathered(a, w_full, chip, dims, tail):
    T, D = a.shape
    rows, cc = dims
    tm, tn = min(512, T), cc // 2
    ni = T // tm
    half = rows // 2

    nt = len(tail.operands)

    def body(chip_ref, a_ref, w_in_ref, *rest):
        tail_in, (y_ref, w_ref), rest = rest[:nt], rest[nt:nt + 2], rest[nt + 2:]
        tail_out, (wbuf, wsem, send_sems, recv_sems, tail_send, tail_recv) = rest[:nt], rest[nt:]
        q, j, i = pl.program_id(0), pl.program_id(1), pl.program_id(2)
        mx, my, mc, _ = _place()
        me = chip_ref[0]

        def tile(block, jj):
            src = w_ref.at[:, pl.ds(pl.multiple_of(block * cc + jj * tn, 128), tn)]
            return pltpu.make_async_copy(src, wbuf.at[jj], wsem.at[jj])

        def rows_half(block, hh):
            return w_ref.at[pl.ds(pl.multiple_of(hh * half, 16), half), pl.ds(pl.multiple_of(block * cc, 128), cc)]

        def over_ici(s, block):
            peer = (1 - mx if s & 2 else mx, 1 - my if s & 1 else my, mc)
            reg = rows_half(block, mc)
            return pltpu.make_async_remote_copy(src_ref=reg, dst_ref=reg, send_sem=send_sems.at[s - 1], recv_sem=recv_sems.at[s - 1],
                                                device_id=peer, device_id_type=MESH)

        def over_d2d(s, block, hh):
            reg = rows_half(block, hh)
            return pltpu.make_async_remote_copy(src_ref=reg, dst_ref=reg, send_sem=send_sems.at[2 + s], recv_sem=recv_sems.at[2 + s],
                                                device_id=(mx, my, 1 - mc), device_id_type=MESH)

        @pl.when((q == 0) & (j == 0) & (i == 0))
        def _():
            for s in (1, 2, 3):
                over_ici(s, me).start()
            tile(me, 0).start()

        @pl.when(i == 0)
        def _():
            tile(me ^ q, j).wait()

        @pl.when((i == 0) & (j == 0))
        def _():
            tile(me ^ q, 1).start()

        y_ref[...] = _nn(a_ref[...], wbuf[j])

        for s in (1, 2, 3):
            @pl.when((q == s - 1) & (j == 1) & (i == ni - 1))
            def _():
                block = me ^ s
                over_ici(s, block).wait_recv()
                over_d2d(s, block, mc).start()
                over_d2d(s, block, 1 - mc).wait_recv()
                tile(block, 0).start()

        @pl.when((q == 3) & (j == 0) & (i == 0))
        def _():
            tail.start(tail_in, tail_out, tail_send, tail_recv)

        @pl.when((q == 3) & (j == 1) & (i == ni - 1))
        def _():
            for s in (1, 2, 3):
                over_ici(s, me).wait_send()
                over_d2d(s, me ^ s, mc).wait_send()
            tail.finish(tail_in, tail_out, tail_send, tail_recv)

    dma = pltpu.SemaphoreType.DMA
    y, w_out, *tail_res = pl.pallas_call(
        body, name="in_proj", out_shape=[jax.ShapeDtypeStruct((T, 4 * cc), F32), jax.ShapeDtypeStruct(w_full.shape, BF16)] + tail.out_shape,
        grid_spec=pltpu.PrefetchScalarGridSpec(
            num_scalar_prefetch=1, grid=(4, 2, ni),
            in_specs=[pl.BlockSpec((tm, D), lambda q, j, i, chip_ref: (i, 0)), ANY] + [ANY] * nt,
            out_specs=[pl.BlockSpec((tm, tn), lambda q, j, i, chip_ref: (i, (chip_ref[0] ^ q) * 2 + j)), ANY] + [ANY] * nt,
            scratch_shapes=[pltpu.VMEM((2, D, tn), BF16), dma((2,)), dma((6,)), dma((6,)), dma((tail.n_sems,)), dma((tail.n_sems,))]),
        input_output_aliases={2: 1, **{3 + i: 2 + o for i, o in tail.aliases.items()}},
        compiler_params=_cp("arbitrary", "arbitrary", "arbitrary"),
    )(chip, a, w_full, *tail.operands)
    return y, w_out, tail_res


def prenorm_matmul(h, g, sc, sh, w, *, relu2, name, comm=None):
    T, D = h.shape
    N = w.shape[1]
    tm, tn = min(512, T), 2048 if N % 2048 == 0 else 1024

    def body(h_ref, g_ref, sc_ref, sh_ref, w_ref, y_ref, a_ref, *hid_ref):
        @pl.when(pl.program_id(1) == 0)
        def _():
            x = h_ref[...]
            r = lax.rsqrt(jnp.mean(x * x, axis=-1, keepdims=True) + EPS)
            a_ref[...] = ((x * r) * g_ref[...] * (1.0 + sc_ref[...]) + sh_ref[...]).astype(BF16)

        y = _nn(a_ref[...], w_ref[...])
        y_ref[...] = y
        if relu2:
            p = jnp.maximum(y, 0.0)
            hid_ref[0][...] = (p * p).astype(BF16)

    out_shape = [jax.ShapeDtypeStruct((T, N), F32), jax.ShapeDtypeStruct((T, D), BF16)]
    out_specs = [pl.BlockSpec((tm, tn), lambda i, j: (i, j)), pl.BlockSpec((tm, D), lambda i, j: (i, 0))]
    if relu2:
        out_shape.append(jax.ShapeDtypeStruct((T, N), BF16))
        out_specs.append(pl.BlockSpec((tm, tn), lambda i, j: (i, j)))
    outs, landed = _pallas(
        body, name=name, grid=(T // tm, N // tn),
        in_specs=[pl.BlockSpec((tm, D), lambda i, j: (i, 0)), _vec(D), _vec(D), _vec(D), pl.BlockSpec((D, tn), lambda i, j: (0, j))],
        out_specs=out_specs, out_shape=out_shape, scratch=[], semantics=("parallel", "arbitrary"), operands=(h, g, sc, sh, w), comm=comm)
    return outs if comm is None else (outs, landed)


def _hgrn_lower_bound(l_ref):
    l0, l1 = l_ref[0:1, :], l_ref[1:2, :]
    m = jnp.maximum(l0, l1)
    e0, e1 = jnp.exp(l0 - m), jnp.exp(l1 - m)
    return e0 / (e0 + e1)


def _hgrn_chunk_mask(d):
    r = lax.broadcasted_iota(jnp.int32, (HGRN_BLOCK, HGRN_BLOCK), 0)
    c = lax.broadcasted_iota(jnp.int32, (HGRN_BLOCK, HGRN_BLOCK), 1)
    same = (r // HGRN_CHUNK) == (c // HGRN_CHUNK)
    fwd = d == 0
    return same & (((c <= r) & fwd) | ((c >= r) & jnp.logical_not(fwd)))


def _chunk_total(x):
    x3 = x.reshape(HGRN_BLOCK // HGRN_CHUNK, HGRN_CHUNK, x.shape[1])
    return jnp.broadcast_to(jnp.sum(x3, axis=1, keepdims=True), x3.shape).reshape(x.shape)


def _chunk_cumsum(x, suffix):
    pos = lax.broadcasted_iota(jnp.int32, x.shape, 0) % HGRN_CHUNK
    p, s = x, 1
    while s < HGRN_CHUNK:
        p = p + jnp.where(pos >= s, pltpu.roll(p, s, 0), 0.0)
        s *= 2
    return jnp.where(suffix, _chunk_total(x) - p + x, p)


def _block_loop(T, body, init):
    n = T // HGRN_BLOCK
    return lax.fori_loop(0, n, body, init, unroll=2 if n % 2 == 0 else 1)


def _hgrn_gate(f, lb):
    s = jax.nn.sigmoid(f)
    sn = jax.nn.sigmoid(-f)
    fg = lb + (1.0 - lb) * s
    return s, sn, fg, jnp.log(fg), (1.0 - lb) * sn


def _hgrn_specs(T):
    col = lambda base: pl.BlockSpec((T, HEAD_DIM), lambda h, d: (0, base * N_HEADS + h))
    f_spec = pl.BlockSpec((T, HEAD_DIM), lambda h, d: (0, COL_FFW * N_HEADS + N_HEADS * d + h))
    l_spec = pl.BlockSpec((None, 2, HEAD_DIM), lambda h, d: (d, 0, h))
    return col, f_spec, l_spec


def hgrn_fwd(proj, lb_logits, comm=None):
    T = proj.shape[0]
    NC, CPB = T // HGRN_CHUNK, HGRN_BLOCK // HGRN_CHUNK
    col, f_spec, l_spec = _hgrn_specs(T)

    def body(l_ref, q_ref, f_ref, v_ref, o_ref, st_ref, dec_ref, qd_ref):
        d = pl.program_id(1)
        lb = _hgrn_lower_bound(l_ref)
        mask = _hgrn_chunk_mask(d)

        def block(i, carry):
            rows = pl.ds(pl.multiple_of(i * HGRN_BLOCK, HGRN_BLOCK), HGRN_BLOCK)
            _, _, _, lf, k = _hgrn_gate(f_ref[rows, :], lb)
            b = _chunk_cumsum(lf, d == 1)
            bl = _chunk_total(lf)
            qd = (q_ref[rows, :] * Q_SCALE * jnp.exp(b)).astype(BF16)
            kd = (k * jnp.exp(-b)).astype(BF16)
            ke = (k * jnp.exp(bl - b)).astype(BF16)
            vb = v_ref[rows, :].astype(BF16)
            att = jnp.where(mask, _nt(qd, kd), 0.0).astype(BF16)
            o_ref[rows, :] = jnp.where(d == 0, 0.0, o_ref[rows, :]) + _nn(att, vb)
            qd_ref[rows, :] = qd
            dec = jnp.exp(bl)
            for cc in range(CPB):
                sl = slice(cc * HGRN_CHUNK, (cc + 1) * HGRN_CHUNK)
                n = i * CPB + cc
                st_ref[n] = _tn(vb[sl], ke[sl])
                dec_ref[n] = dec[cc * HGRN_CHUNK:cc * HGRN_CHUNK + 8, :]
            return carry

        _block_loop(T, block, 0)

        def scan(t, s):
            n = jnp.where(d == 0, t, NC - 1 - t)
            u = st_ref[n]
            st_ref[n] = s
            return dec_ref[n][0:1, :] * s + u

        lax.fori_loop(0, NC, scan, jnp.zeros((HEAD_DIM, HEAD_DIM), F32))

        def inter(i, carry):
            rows = pl.ds(pl.multiple_of(i * HGRN_BLOCK, HGRN_BLOCK), HGRN_BLOCK)
            qd = qd_ref[rows, :]
            o_ref[rows, :] += jnp.concatenate(
                [_nt(qd[cc * HGRN_CHUNK:(cc + 1) * HGRN_CHUNK], st_ref[i * CPB + cc].astype(BF16)) for cc in range(CPB)], axis=0)
            return carry

        _block_loop(T, inter, 0)

    (o,), landed = _pallas(
        body, name="hgrn_fwd", grid=(N_HEADS, 2), in_specs=[l_spec, col(COL_Q), f_spec, col(COL_V)],
        out_specs=[pl.BlockSpec((T, HEAD_DIM), lambda h, d: (0, h))], out_shape=[jax.ShapeDtypeStruct((T, N_HEADS * HEAD_DIM), F32)],
        scratch=[pltpu.VMEM((NC, HEAD_DIM, HEAD_DIM), F32), pltpu.VMEM((NC, 8, HEAD_DIM), F32), pltpu.VMEM((T, HEAD_DIM), BF16)],
        semantics=("parallel", "arbitrary"), operands=(lb_logits, proj, proj, proj), comm=comm)
    return o if comm is None else (o, landed)


def hgrn_post_fwd(o, proj, g_norm):
    T, W = o.shape
    tm = min(256, T)

    def body(o_ref, og_ref, g_ref, y_ref):
        g = g_ref[...]
        for h in range(N_HEADS):
            sl = slice(h * HEAD_DIM, (h + 1) * HEAD_DIM)
            x = o_ref[:, sl]
            r = lax.rsqrt(jnp.mean(x * x, axis=-1, keepdims=True) + EPS)
            og = og_ref[:, sl]
            y_ref[:, sl] = ((x * r) * g * (og * jax.nn.sigmoid(og))).astype(BF16)

    return pl.pallas_call(
        body, name="hgrn_post_fwd", out_shape=jax.ShapeDtypeStruct((T, W), BF16), grid=(T // tm,),
        in_specs=[pl.BlockSpec((tm, W), lambda i: (i, 0)), pl.BlockSpec((tm, W), lambda i: (i, COL_OG)), _vec(HEAD_DIM)],
        out_specs=pl.BlockSpec((tm, W), lambda i: (i, 0)), compiler_params=_cp("parallel"),
    )(o, proj, g_norm)


def _gelu(x):
    return 0.5 * x * (1.0 + lax.erf(x * (1.0 / math.sqrt(2.0))))


def _gelu_grad(x):
    return 0.5 * (1.0 + lax.erf(x * (1.0 / math.sqrt(2.0)))) + x * jnp.exp(-0.5 * x * x) * (1.0 / math.sqrt(2.0 * math.pi))


def _sgu_mix(u_ref, v_ref, g_ref, ws_ref, bst_ref):
    W = u_ref.shape[1]
    zu, zv = _gelu(u_ref[...]), _gelu(v_ref[...])
    dv = zv - jnp.mean(zv, axis=-1, keepdims=True)
    rstd = lax.rsqrt(jnp.mean(dv * dv, axis=-1, keepdims=True) + EPS)
    dhat = dv * rstd
    vn = (dhat * g_ref[...]).astype(BF16)
    gw = W // SGU_GROUPS
    vm = [_nn(ws_ref[g].astype(BF16), vn[:, g * gw:(g + 1) * gw]) + bst_ref[:, g:g + 1] for g in range(SGU_GROUPS)]
    return zu, rstd, dhat, vn, jnp.concatenate(vm, axis=1)


def sgu_fwd(proj, g_norm, w_spatial, b_spatial_t):
    T = proj.shape[0]
    W = 1024
    n_chunks = T // SGU_CHUNK

    def body(u_ref, v_ref, g_ref, ws_ref, bst_ref, y_ref):
        zu, _, _, _, vm = _sgu_mix(u_ref, v_ref, g_ref, ws_ref, bst_ref)
        y_ref[...] = (zu * vm).astype(BF16)

    blk = lambda cb: pl.BlockSpec((SGU_CHUNK, W), lambda i: (i, cb))
    return pl.pallas_call(
        body, name="sgu_fwd", out_shape=jax.ShapeDtypeStruct((T, W), BF16), grid=(n_chunks,),
        in_specs=[blk(COL_U), blk(COL_ZV), _vec(W), pl.BlockSpec((SGU_GROUPS, SGU_CHUNK, SGU_CHUNK), lambda i: (0, 0, 0)),
                  pl.BlockSpec((SGU_CHUNK, SGU_GROUPS), lambda i: (0, 0))],
        out_specs=blk(0), compiler_params=_cp("parallel"),
    )(proj, proj, g_norm, w_spatial, b_spatial_t)


def merge_matmul(ya_pre, sgu, w_a, w_b, proj):
    T, K = ya_pre.shape
    tn = w_a.shape[1]
    N = 4 * tn
    tm = min(512, T)
    gpb = 1024 // tn

    def body(a_ref, b_ref, wa_ref, wb_ref, ga_ref, gb_ref, ya_ref, yb_ref, m_ref):
        ya = _nn(a_ref[...], wa_ref[...])
        yb = _nn(b_ref[...], wb_ref[...])
        ya_ref[...] = ya
        yb_ref[...] = yb
        m_ref[...] = (jax.nn.sigmoid(ga_ref[...]) * ya + jax.nn.sigmoid(gb_ref[...]) * yb).astype(BF16)

    lhs = pl.BlockSpec((tm, K), lambda i, j: (i, 0))
    rhs = pl.BlockSpec((K, tn), lambda i, j: (j, 0))
    out = pl.BlockSpec((tm, tn), lambda i, j: (i, j))
    return pl.pallas_call(
        body, name="merge_matmul", grid=(T // tm, N // tn),
        out_shape=[jax.ShapeDtypeStruct((T, N), F32), jax.ShapeDtypeStruct((T, N), F32), jax.ShapeDtypeStruct((T, N), BF16)],
        in_specs=[lhs, lhs, rhs, rhs, pl.BlockSpec((tm, tn), lambda i, j: (i, COL_GA * gpb + j)),
                  pl.BlockSpec((tm, tn), lambda i, j: (i, COL_GB * gpb + j))],
        out_specs=[out, out, out], compiler_params=_cp("parallel", "parallel"),
    )(ya_pre, sgu, w_a, w_b, proj, proj)


def out_proj(merged, w_o, h0, gt1, g_post):
    T, D = h0.shape
    tm = min(256, T)

    def body(m_ref, w_ref, h_ref, gt_ref, gp_ref, mo_ref, h1_ref):
        mo = _nn(m_ref[...], w_ref[...])
        mo_ref[...] = mo
        r = lax.rsqrt(jnp.mean(mo * mo, axis=-1, keepdims=True) + EPS)
        h1_ref[...] = h_ref[...] + gt_ref[...] * ((mo * r) * gp_ref[...])

    row = pl.BlockSpec((tm, D), lambda i: (i, 0))
    return pl.pallas_call(
        body, name="out_proj", grid=(T // tm,),
        out_shape=[jax.ShapeDtypeStruct((T, D), F32), jax.ShapeDtypeStruct((T, D), F32)],
        in_specs=[row, pl.BlockSpec((D, D), lambda i: (0, 0)), row, _vec(D), _vec(D)],
        out_specs=[row, row], compiler_params=_cp("parallel"),
    )(merged, w_o, h0, gt1, g_post)


def ff2_loss(hid, w_ff2, h1, tgt, gt2, g_post):
    T, K = hid.shape
    D = w_ff2.shape[1]
    tm, tk = min(256, T), 2048
    nk = K // tk

    def body(a_ref, w_ref, h_ref, t_ref, gt_ref, g_ref, dy_ref, dff_ref, loss_ref, dgt_ref, dg_ref, acc_ref):
        i, k = pl.program_id(0), pl.program_id(1)

        @pl.when(k == 0)
        def _():
            acc_ref[...] = jnp.zeros_like(acc_ref)

        @pl.when((k == 0) & (i == 0))
        def _():
            loss_ref[...] = jnp.zeros_like(loss_ref)
            dgt_ref[...] = jnp.zeros_like(dgt_ref)
            dg_ref[...] = jnp.zeros_like(dg_ref)

        acc_ref[...] += _nn(a_ref[...], w_ref[...])

        @pl.when(k == nk - 1)
        def _():
            ff = acc_ref[...]
            gt, g = gt_ref[...], g_ref[...]
            r = lax.rsqrt(jnp.mean(ff * ff, axis=-1, keepdims=True) + EPS)
            fhat = ff * r
            nf = fhat * g
            err = (h_ref[...] + gt * nf) - t_ref[...]
            loss_ref[...] += jnp.sum(err * err)
            dy = err * (1.0 / D)
            dy_ref[...] = dy
            dgt_ref[...] += _colsum(dy * nf)
            dnf = dy * gt
            dg_ref[...] += _colsum(dnf * fhat)
            u = dnf * g
            dff_ref[...] = (r * (u - fhat * jnp.mean(u * fhat, axis=-1, keepdims=True))).astype(BF16)

    row = pl.BlockSpec((tm, D), lambda i, k: (i, 0))
    vec = pl.BlockSpec((1, D), lambda i, k: (0, 0))
    return pl.pallas_call(
        body, name="ff2_loss", grid=(T // tm, nk),
        out_shape=[jax.ShapeDtypeStruct((T, D), F32), jax.ShapeDtypeStruct((T, D), BF16), jax.ShapeDtypeStruct((8, 128), F32),
                   jax.ShapeDtypeStruct((1, D), F32), jax.ShapeDtypeStruct((1, D), F32)],
        in_specs=[pl.BlockSpec((tm, tk), lambda i, k: (i, k)), pl.BlockSpec((tk, D), lambda i, k: (k, 0)), row, row, vec, vec],
        out_specs=[row, row, pl.BlockSpec((8, 128), lambda i, k: (0, 0)), vec, vec],
        scratch_shapes=[pltpu.VMEM((tm, D), F32)], compiler_params=_cp("arbitrary", "arbitrary"),
    )(hid, w_ff2, h1, tgt, gt2, g_post)


def ff2_bwd(dff, w_ff2, f1):
    T, D = dff.shape
    K = w_ff2.shape[0]
    tm, tn = min(512, T), 2048

    def body(a_ref, w_ref, f_ref, o_ref):
        o_ref[...] = (_nt(a_ref[...], w_ref[...]) * (2.0 * jnp.maximum(f_ref[...], 0.0))).astype(BF16)

    return pl.pallas_call(
        body, name="ff2_bwd", out_shape=jax.ShapeDtypeStruct((T, K), BF16), grid=(K // tn, T // tm),
        in_specs=[pl.BlockSpec((tm, D), lambda j, i: (i, 0)), pl.BlockSpec((tn, D), lambda j, i: (j, 0)),
                  pl.BlockSpec((tm, tn), lambda j, i: (i, j))],
        out_specs=pl.BlockSpec((tm, tn), lambda j, i: (i, j)), compiler_params=_cp("parallel", "parallel"),
    )(dff, w_ff2, f1)


def ffn_norm_bwd(dy, da2, h1, mo, g_pre2, sc2, gt1, g_post):
    T, D = dy.shape
    tm = min(256, T)

    def body(dy_ref, da_ref, h_ref, mo_ref, g2_ref, sc_ref, gt_ref, gp_ref, dh_ref, dmo_ref, s_sh, s_sc, s_g2, s_gt, s_gp):
        @pl.when(pl.program_id(0) == 0)
        def _():
            for s in (s_sh, s_sc, s_g2, s_gt, s_gp):
                s[...] = jnp.zeros_like(s)

        h1, da = h_ref[...], da_ref[...]
        g2, sc = g2_ref[...], sc_ref[...]
        r2 = lax.rsqrt(jnp.mean(h1 * h1, axis=-1, keepdims=True) + EPS)
        n2 = h1 * r2
        s_sh[...] += _colsum(da)
        s_sc[...] += _colsum(da * (n2 * g2))
        s_g2[...] += _colsum(da * (1.0 + sc) * n2)
        dn2 = da * g2 * (1.0 + sc)
        dh1 = dy_ref[...] + r2 * (dn2 - n2 * jnp.mean(dn2 * n2, axis=-1, keepdims=True))
        dh_ref[...] = dh1
        mo = mo_ref[...]
        gt, gp = gt_ref[...], gp_ref[...]
        r = lax.rsqrt(jnp.mean(mo * mo, axis=-1, keepdims=True) + EPS)
        mhat = mo * r
        s_gt[...] += _colsum(dh1 * (mhat * gp))
        dnm = dh1 * gt
        s_gp[...] += _colsum(dnm * mhat)
        u = dnm * gp
        dmo_ref[...] = (r * (u - mhat * jnp.mean(u * mhat, axis=-1, keepdims=True))).astype(BF16)

    row = pl.BlockSpec((tm, D), lambda i: (i, 0))
    vec_out = jax.ShapeDtypeStruct((1, D), F32)
    return pl.pallas_call(
        body, name="ffn_norm_bwd", grid=(T // tm,),
        out_shape=[jax.ShapeDtypeStruct((T, D), F32), jax.ShapeDtypeStruct((T, D), BF16)] + [vec_out] * 5,
        in_specs=[row, row, row, row] + [_vec(D)] * 4, out_specs=[row, row] + [_vec(D)] * 5,
        compiler_params=_cp("arbitrary"),
    )(dy, da2, h1, mo, g_pre2, sc2, gt1, g_post)


def out_proj_bwd(dmo, w_o, y_a, y_b, proj):
    T, D = dmo.shape
    tm, tn = min(512, T), 512
    gpb = 1024 // tn

    def body(a_ref, w_ref, ya_ref, yb_ref, ga_ref, gb_ref, dya_ref, dyb_ref, dga_ref, dgb_ref):
        dm = _nt(a_ref[...], w_ref[...])
        sa, sb = jax.nn.sigmoid(ga_ref[...]), jax.nn.sigmoid(gb_ref[...])
        dya_ref[...] = (dm * sa).astype(BF16)
        dyb_ref[...] = (dm * sb).astype(BF16)
        dga_ref[...] = (dm * ya_ref[...] * sa * (1.0 - sa)).astype(BF16)
        dgb_ref[...] = (dm * yb_ref[...] * sb * (1.0 - sb)).astype(BF16)

    out = pl.BlockSpec((tm, tn), lambda i, j: (i, j))
    return pl.pallas_call(
        body, name="out_proj_bwd", grid=(T // tm, D // tn), out_shape=[jax.ShapeDtypeStruct((T, D), BF16)] * 4,
        in_specs=[pl.BlockSpec((tm, D), lambda i, j: (i, 0)), pl.BlockSpec((tn, D), lambda i, j: (j, 0)), out, out,
                  pl.BlockSpec((tm, tn), lambda i, j: (i, COL_GA * gpb + j)), pl.BlockSpec((tm, tn), lambda i, j: (i, COL_GB * gpb + j))],
        out_specs=[out] * 4, compiler_params=_cp("parallel", "parallel"),
    )(dmo, w_o, y_a, y_b, proj, proj)


def sgu_bwd(proj, dsgu, g_norm, w_spatial, b_spatial_t):
    T = proj.shape[0]
    W = 1024
    gw = W // SGU_GROUPS

    def body(u_ref, v_ref, ds_ref, g_ref, ws_ref, bst_ref, dz_ref, dw_ref, db_ref, dg_ref):
        @pl.when(pl.program_id(0) == 0)
        def _():
            dw_ref[...] = jnp.zeros_like(dw_ref)
            db_ref[...] = jnp.zeros_like(db_ref)
            dg_ref[...] = jnp.zeros_like(dg_ref)

        zu, rstd, dhat, vn, vm = _sgu_mix(u_ref, v_ref, g_ref, ws_ref, bst_ref)
        ds = ds_ref[...]
        du = ds * vm
        dvm = ds * zu
        dvm_b = dvm.astype(BF16)
        ones = jnp.ones((8, gw), F32)
        dvn = []
        for g in range(SGU_GROUPS):
            sl = slice(g * gw, (g + 1) * gw)
            dw_ref[g] += _nt(dvm_b[:, sl], vn[:, sl])
            db_ref[g] += lax.dot_general(ones, dvm[:, sl], (((1,), (1,)), ((), ())), precision=HI, preferred_element_type=F32)
            dvn.append(_tn(ws_ref[g].astype(BF16), dvm_b[:, sl]))
        dvn = jnp.concatenate(dvn, axis=1)
        dg_ref[...] += _colsum(dvn * dhat)
        ddh = dvn * g_ref[...]
        dzv = rstd * (ddh - jnp.mean(ddh, axis=-1, keepdims=True) - dhat * jnp.mean(ddh * dhat, axis=-1, keepdims=True))
        dz_ref[:, 0:W] = (du * _gelu_grad(u_ref[...])).astype(BF16)
        dz_ref[:, W:2 * W] = (dzv * _gelu_grad(v_ref[...])).astype(BF16)

    blk = lambda cb: pl.BlockSpec((SGU_CHUNK, W), lambda i: (i, cb))
    full3 = lambda a, b, c: pl.BlockSpec((a, b, c), lambda i: (0, 0, 0))
    return pl.pallas_call(
        body, name="sgu_bwd", grid=(T // SGU_CHUNK,),
        out_shape=[jax.ShapeDtypeStruct((T, 2 * W), BF16), jax.ShapeDtypeStruct((SGU_GROUPS, SGU_CHUNK, SGU_CHUNK), F32),
                   jax.ShapeDtypeStruct((SGU_GROUPS, 8, SGU_CHUNK), F32), jax.ShapeDtypeStruct((1, W), F32)],
        in_specs=[blk(COL_U), blk(COL_ZV), blk(0), _vec(W), full3(SGU_GROUPS, SGU_CHUNK, SGU_CHUNK),
                  pl.BlockSpec((SGU_CHUNK, SGU_GROUPS), lambda i: (0, 0))],
        out_specs=[pl.BlockSpec((SGU_CHUNK, 2 * W), lambda i: (i, 0)), full3(SGU_GROUPS, SGU_CHUNK, SGU_CHUNK),
                   full3(SGU_GROUPS, 8, SGU_CHUNK), _vec(W)],
        compiler_params=_cp("arbitrary"),
    )(proj, proj, dsgu, g_norm, w_spatial, b_spatial_t)


def hgrn_post_bwd(dya, o, proj, g_norm):
    T, W = o.shape
    tm = min(256, T)

    def body(dy_ref, o_ref, og_ref, g_ref, do_ref, dog_ref, dg_ref):
        @pl.when(pl.program_id(0) == 0)
        def _():
            dg_ref[...] = jnp.zeros_like(dg_ref)

        g = g_ref[...]
        dg = jnp.zeros((1, HEAD_DIM), F32)
        for h in range(N_HEADS):
            sl = slice(h * HEAD_DIM, (h + 1) * HEAD_DIM)
            x, og, dy = o_ref[:, sl], og_ref[:, sl], dy_ref[:, sl]
            r = lax.rsqrt(jnp.mean(x * x, axis=-1, keepdims=True) + EPS)
            xhat = x * r
            s = jax.nn.sigmoid(og)
            don = dy * (og * s)
            dog_ref[:, sl] = (dy * (xhat * g) * (s * (1.0 + og * (1.0 - s)))).astype(BF16)
            dg += _colsum(don * xhat)
            u = don * g
            do_ref[:, sl] = r * (u - xhat * jnp.mean(u * xhat, axis=-1, keepdims=True))
        dg_ref[...] += dg

    row = pl.BlockSpec((tm, W), lambda i: (i, 0))
    return pl.pallas_call(
        body, name="hgrn_post_bwd", grid=(T // tm,),
        out_shape=[jax.ShapeDtypeStruct((T, W), F32), jax.ShapeDtypeStruct((T, W), BF16), jax.ShapeDtypeStruct((1, HEAD_DIM), F32)],
        in_specs=[row, row, pl.BlockSpec((tm, W), lambda i: (i, COL_OG)), _vec(HEAD_DIM)],
        out_specs=[row, row, _vec(HEAD_DIM)], compiler_params=_cp("arbitrary"),
    )(dya, o, proj, g_norm)


def hgrn_bwd(proj, do, lb_logits, comm=None):
    T = proj.shape[0]
    NC, CPB = T // HGRN_CHUNK, HGRN_BLOCK // HGRN_CHUNK
    W = N_HEADS * HEAD_DIM
    col, f_spec, l_spec = _hgrn_specs(T)

    def body(l_ref, q_ref, f_ref, v_ref, do_ref, dq_ref, dv_ref, dlg_ref, dlb_ref, st_ref, dst_ref, dec_ref, ddec_ref, dqa_ref, dva_ref):
        d = pl.program_id(1)
        lb = _hgrn_lower_bound(l_ref)
        oml = 1.0 - lb
        mask = _hgrn_chunk_mask(d)

        def values(rows):
            s, sn, fg, lf, k = _hgrn_gate(f_ref[rows, :], lb)
            b = _chunk_cumsum(lf, d == 1)
            bl = _chunk_total(lf)
            eb, enb, ee = jnp.exp(b), jnp.exp(-b), jnp.exp(bl - b)
            qd = q_ref[rows, :] * Q_SCALE * eb
            return s, sn, fg, k, bl, eb, enb, ee, qd, k * enb, k * ee

        def block1(i, carry):
            rows = pl.ds(pl.multiple_of(i * HGRN_BLOCK, HGRN_BLOCK), HGRN_BLOCK)
            _, _, _, _, bl, _, _, _, qd, _, ke = values(rows)
            qd, ke = qd.astype(BF16), ke.astype(BF16)
            vb, dob = v_ref[rows, :].astype(BF16), do_ref[rows, :].astype(BF16)
            dec = jnp.exp(bl)
            for cc in range(CPB):
                sl = slice(cc * HGRN_CHUNK, (cc + 1) * HGRN_CHUNK)
                n = i * CPB + cc
                st_ref[n] = _tn(vb[sl], ke[sl])
                dst_ref[n] = _tn(dob[sl], qd[sl])
                dec_ref[n] = dec[cc * HGRN_CHUNK:cc * HGRN_CHUNK + 8, :]
            return carry

        _block_loop(T, block1, 0)

        def scan(t, s):
            n = jnp.where(d == 0, t, NC - 1 - t)
            u = st_ref[n]
            st_ref[n] = s
            return dec_ref[n][0:1, :] * s + u

        lax.fori_loop(0, NC, scan, jnp.zeros((HEAD_DIM, HEAD_DIM), F32))

        def rscan(t, ds):
            n = jnp.where(d == 0, NC - 1 - t, t)
            w = dst_ref[n]
            dst_ref[n] = ds
            ddec_ref[n] = jnp.broadcast_to(_colsum(ds * st_ref[n]), (8, HEAD_DIM))
            return dec_ref[n][0:1, :] * ds + w

        lax.fori_loop(0, NC, rscan, jnp.zeros((HEAD_DIM, HEAD_DIM), F32))

        def block3(i, dlb):
            rows = pl.ds(pl.multiple_of(i * HGRN_BLOCK, HGRN_BLOCK), HGRN_BLOCK)
            s, sn, fg, k, bl, eb, enb, ee, qd, kd, ke = values(rows)
            qdb, kdb, keb = qd.astype(BF16), kd.astype(BF16), ke.astype(BF16)
            vb, dob = v_ref[rows, :].astype(BF16), do_ref[rows, :].astype(BF16)
            att = jnp.where(mask, _nt(qdb, kdb), 0.0).astype(BF16)
            datt = jnp.where(mask, _nt(dob, vb), 0.0).astype(BF16)
            dv = _tn(att, dob)
            dqd = _nn(datt, kdb)
            dkd = _tn(datt, qdb)
            dv_i, dqd_i, dke, ddl = [], [], [], []
            for cc in range(CPB):
                sl = slice(cc * HGRN_CHUNK, (cc + 1) * HGRN_CHUNK)
                n = i * CPB + cc
                st_b, dst_b = st_ref[n].astype(BF16), dst_ref[n].astype(BF16)
                dv_i.append(_nt(keb[sl], dst_b))
                dqd_i.append(_nn(dob[sl], st_b))
                dke.append(_nn(vb[sl], dst_b))
                ddl.append(jnp.broadcast_to(ddec_ref[n][0:1, :] * dec_ref[n][0:1, :], (HGRN_CHUNK, HEAD_DIM)))
            dv = dv + jnp.concatenate(dv_i, axis=0)
            dqd = dqd + jnp.concatenate(dqd_i, axis=0)
            dke = jnp.concatenate(dke, axis=0)
            dq = dqd * eb * Q_SCALE
            dk = dkd * enb + dke * ee
            t_end = dke * ke
            db = dqd * qd - dkd * kd - t_end
            dlf = _chunk_cumsum(db, d == 0) + _chunk_total(t_end) + jnp.concatenate(ddl, axis=0)
            e = dlf / fg - dk
            dlg_ref[rows, :] = (oml * e * s * sn).astype(BF16)

            dq = jnp.where(d == 0, 0.0, dqa_ref[rows, :]) + dq
            dv = jnp.where(d == 0, 0.0, dva_ref[rows, :]) + dv
            dqa_ref[rows, :] = dq
            dva_ref[rows, :] = dv
            dq_ref[rows, :] = dq.astype(BF16)
            dv_ref[rows, :] = dv.astype(BF16)

            return dlb + _colsum(e * sn)

        dlb_ref[...] = _block_loop(T, block3, jnp.zeros((1, HEAD_DIM), F32))

    head = pl.BlockSpec((T, HEAD_DIM), lambda h, d: (0, h))
    big = pltpu.VMEM((NC, HEAD_DIM, HEAD_DIM), F32)
    small = pltpu.VMEM((NC, 8, HEAD_DIM), F32)
    acc = pltpu.VMEM((T, HEAD_DIM), F32)
    outs, landed = _pallas(
        body, name="hgrn_bwd", grid=(N_HEADS, 2),
        out_shape=[jax.ShapeDtypeStruct((T, W), BF16), jax.ShapeDtypeStruct((T, W), BF16), jax.ShapeDtypeStruct((T, 2 * W), BF16),
                   jax.ShapeDtypeStruct((2, 1, W), F32)],
        in_specs=[l_spec, col(COL_Q), f_spec, col(COL_V), head],
        out_specs=[head, head, pl.BlockSpec((T, HEAD_DIM), lambda h, d: (0, N_HEADS * d + h)),
                   pl.BlockSpec((None, 1, HEAD_DIM), lambda h, d: (d, 0, h))],
        scratch=[big, big, small, small, acc, acc], semantics=("parallel", "arbitrary"), operands=(lb_logits, proj, proj, proj, do), comm=comm)
    return outs if comm is None else (outs, landed)


def mix_norm_bwd(da1, h0, dh1, g_pre, sc1):
    T, D = h0.shape
    tm = min(256, T)

    def body(da_ref, h_ref, dh_ref, g_ref, sc_ref, gx_ref, s_sh, s_sc, s_g):
        @pl.when(pl.program_id(0) == 0)
        def _():
            for s in (s_sh, s_sc, s_g):
                s[...] = jnp.zeros_like(s)

        h, da = h_ref[...], da_ref[...]
        g, sc = g_ref[...], sc_ref[...]
        r = lax.rsqrt(jnp.mean(h * h, axis=-1, keepdims=True) + EPS)
        n = h * r
        s_sh[...] += _colsum(da)
        s_sc[...] += _colsum(da * (n * g))
        s_g[...] += _colsum(da * (1.0 + sc) * n)
        dn = da * g * (1.0 + sc)
        gx_ref[...] = dh_ref[...] + r * (dn - n * jnp.mean(dn * n, axis=-1, keepdims=True))

    row = pl.BlockSpec((tm, D), lambda i: (i, 0))
    return pl.pallas_call(
        body, name="mix_norm_bwd", grid=(T // tm,),
        out_shape=[jax.ShapeDtypeStruct((T, D), F32)] + [jax.ShapeDtypeStruct((1, D), F32)] * 3,
        in_specs=[row, row, row, _vec(D), _vec(D)], out_specs=[row] + [_vec(D)] * 3, compiler_params=_cp("arbitrary"),
    )(da1, h0, dh1, g_pre, sc1)


def adamw(w, g, m, v, name):
    R, C = w.shape
    tr = R if R * C * 4 <= (1 << 21) else max(8, ((1 << 21) // (C * 4)) // 8 * 8)
    while R % tr:
        tr -= 8

    def body(w_ref, g_ref, m_ref, v_ref, d_ref, m2_ref, v2_ref):
        d_ref[...], m2_ref[...], v2_ref[...] = _adamw(w_ref[...], g_ref[...], m_ref[...], v_ref[...])

    row = pl.BlockSpec((tr, C), lambda i: (i, 0))
    return pl.pallas_call(
        body, name=name, grid=(R // tr,), out_shape=[jax.ShapeDtypeStruct((R, C), F32)] * 3,
        in_specs=[row] * 4, out_specs=[row] * 3, compiler_params=_cp("parallel"),
    )(w, g, m, v)


def wada_update(c_all, dmod, w, m, v):
    D, N = w.shape
    tm, tn = 512, 1024

    def body(c_ref, dm_ref, w_ref, m_ref, v_ref, g_ref, d_ref, m2_ref, v2_ref):
        c = c_ref[...]
        g = lax.dot_general(c * jax.nn.sigmoid(c), dm_ref[...], (((0,), (0,)), ((), ())), precision=HI, preferred_element_type=F32)
        g_ref[...] = g
        d_ref[...], m2_ref[...], v2_ref[...] = _adamw(w_ref[...], g, m_ref[...], v_ref[...])

    blk = pl.BlockSpec((tm, tn), lambda i, j: (i, j))
    return pl.pallas_call(
        body, name="wada_update", grid=(D // tm, N // tn), out_shape=[jax.ShapeDtypeStruct((D, N), F32)] * 4,
        in_specs=[pl.BlockSpec((8, tm), lambda i, j: (0, i)), pl.BlockSpec((8, tn), lambda i, j: (0, j)), blk, blk, blk],
        out_specs=[blk] * 4, compiler_params=_cp("parallel", "parallel"),
    )(c_all, dmod, w, m, v)


def sum_devices(gathered, name):
    n, R, C = gathered.shape

    def body(g_ref, o_ref):
        s = g_ref[0]
        for i in range(1, n):
            s = s + g_ref[i]
        o_ref[...] = s

    return pl.pallas_call(body, name=name, out_shape=jax.ShapeDtypeStruct((R, C), F32), compiler_params=_cp())(gathered)


def lb_logits_grad(dlb, lb_logits):
    def body(d_ref, l_ref, o_ref):
        for d in range(2):
            l0, l1 = l_ref[d, 0:1, :], l_ref[d, 1:2, :]
            m = jnp.maximum(l0, l1)
            e0, e1 = jnp.exp(l0 - m), jnp.exp(l1 - m)
            p0, p1 = e0 / (e0 + e1), e1 / (e0 + e1)
            g = d_ref[d:d + 1, :]
            o_ref[d, 0:1, :] = p0 * (g - p0 * g)
            o_ref[d, 1:2, :] = -p1 * (p0 * g)

    return pl.pallas_call(body, name="lb_logits_grad", out_shape=jax.ShapeDtypeStruct(lb_logits.shape, F32), compiler_params=_cp())(dlb, lb_logits)


def add_halves(g, landed, core):
    nj, _, r, cc = g.shape
    tr = min(256, r)

    def body(core_ref, g_ref, l_ref, o_ref):
        o_ref[...] = (g_ref[...].astype(F32) + l_ref[...].astype(F32)).astype(BF16)

    return pl.pallas_call(
        body, name="add_halves_%dx%d" % (r, cc), out_shape=jax.ShapeDtypeStruct((nj, r, cc), BF16),
        grid_spec=pltpu.PrefetchScalarGridSpec(
            num_scalar_prefetch=1, grid=(nj, r // tr),
            in_specs=[pl.BlockSpec((None, None, tr, cc), lambda j, i, core_ref: (j, core_ref[0], i, 0)),
                      pl.BlockSpec((None, None, tr, cc), lambda j, i, core_ref: (j, 0, i, 0))],
            out_specs=pl.BlockSpec((None, tr, cc), lambda j, i, core_ref: (j, i, 0))),
        compiler_params=_cp("parallel", "parallel"),
    )(core, g, landed)


def sum_chips(parts, landed, chip):
    nj, r, cc = parts.shape
    tr = min(256, r)

    def body(chip_ref, p_ref, l_ref, o_ref):
        mine = p_ref[...].astype(F32)
        s = None
        for j in range(nj):
            t = jnp.where(chip_ref[0] == j, mine, l_ref[j].astype(F32))
            s = t if s is None else s + t
        o_ref[...] = s

    return pl.pallas_call(
        body, name="sum_chips_%dx%d" % (r, cc), out_shape=jax.ShapeDtypeStruct((r, cc), F32),
        grid_spec=pltpu.PrefetchScalarGridSpec(
            num_scalar_prefetch=1, grid=(r // tr,),
            in_specs=[pl.BlockSpec((None, tr, cc), lambda i, chip_ref: (chip_ref[0], i, 0)),
                      pl.BlockSpec((nj, tr, cc), lambda i, chip_ref: (0, i, 0))],
            out_specs=pl.BlockSpec((tr, cc), lambda i, chip_ref: (i, 0))),
        compiler_params=_cp("parallel"),
    )(chip, parts, landed)


def adamw_halves(w, own, other, m, v, core, name):
    r, cc = own.shape
    tr = min(128, r)
    nb = r // tr

    def body(core_ref, w_ref, a_ref, b_ref, m_ref, v_ref, g_ref, d_ref, m2_ref, v2_ref):
        g = jnp.where(pl.program_id(0) == core_ref[0], a_ref[...], b_ref[...])
        g_ref[...] = g
        d_ref[...], m2_ref[...], v2_ref[...] = _adamw(w_ref[...], g, m_ref[...], v_ref[...])

    full = pl.BlockSpec((tr, cc), lambda h, i, core_ref: (h * nb + i, 0))
    half = pl.BlockSpec((tr, cc), lambda h, i, core_ref: (i, 0))
    return pl.pallas_call(
        body, name=name, out_shape=[jax.ShapeDtypeStruct((2 * r, cc), F32)] * 4,
        grid_spec=pltpu.PrefetchScalarGridSpec(
            num_scalar_prefetch=1, grid=(2, nb), in_specs=[full, half, half, full, full], out_specs=[full] * 4),
        compiler_params=_cp("parallel", "parallel"),
    )(core, w, own, other, m, v)


def _place():
    mx, my, mc = lax.axis_index("x"), lax.axis_index("y"), lax.axis_index("c")
    chips = [(1 - mx, my), (mx, 1 - my), (1 - mx, 1 - my)]
    return mx, my, mc, chips


def all_gather_small(x, name):
    R, C = x.shape

    def body(x_ref, out_ref, send_sems, recv_sems, local_sem):
        mx, my, mc, _ = _place()
        me = 4 * mx + 2 * my + mc
        mine = pltpu.make_async_copy(x_ref, out_ref.at[me], local_sem)
        mine.start()

        def peer(k):
            px = 1 - mx if k & 4 else mx
            py = 1 - my if k & 2 else my
            pc = 1 - mc if k & 1 else mc
            return px, py, pc

        def copy(k, src, slot):
            return pltpu.make_async_remote_copy(src_ref=src, dst_ref=out_ref.at[slot], send_sem=send_sems.at[k - 1],
                                                recv_sem=recv_sems.at[k - 1], device_id=peer(k), device_id_type=MESH)

        sends = [copy(k, x_ref, me) for k in range(1, 8)]
        for cp in sends:
            cp.start()
        for k in range(1, 8):
            px, py, pc = peer(k)
            slot = 4 * px + 2 * py + pc
            copy(k, out_ref.at[slot], slot).wait_recv()
        for cp in sends:
            cp.wait_send()
        mine.wait()

    return pl.pallas_call(
        body, name=name, out_shape=jax.ShapeDtypeStruct((8, R, C), F32),
        in_specs=[pl.BlockSpec(memory_space=pltpu.VMEM)], out_specs=pl.BlockSpec(memory_space=pltpu.VMEM),
        scratch_shapes=[pltpu.SemaphoreType.DMA((7,)), pltpu.SemaphoreType.DMA((7,)), pltpu.SemaphoreType.DMA],
        compiler_params=_cp(),
    )(x)


def gather8_comm(x):
    def copies(x_ref, out_ref, send_sems, recv_sems):
        mx, my, mc, _ = _place()
        me = 4 * mx + 2 * my + mc

        def peer(k):
            return (1 - mx if k & 4 else mx, 1 - my if k & 2 else my, 1 - mc if k & 1 else mc)

        def copy(k, src, slot):
            return pltpu.make_async_remote_copy(src_ref=src, dst_ref=out_ref.at[slot], send_sem=send_sems.at[k - 1],
                                                recv_sem=recv_sems.at[k - 1], device_id=peer(k), device_id_type=MESH)

        sends = [copy(k, x_ref, me) for k in range(1, 8)]
        arrivals = []
        for k in range(1, 8):
            px, py, pc = peer(k)
            slot = 4 * px + 2 * py + pc
            arrivals.append(copy(k, out_ref.at[slot], slot))
        return sends, arrivals, pltpu.make_async_copy(x_ref, out_ref.at[me], send_sems.at[7])

    def start(cin, cout, send_sems, recv_sems):
        sends, _, mine = copies(cin[0], cout[0], send_sems, recv_sems)
        mine.start()
        for cp in sends:
            cp.start()

    def finish(cin, cout, send_sems, recv_sems):
        sends, arrivals, mine = copies(cin[0], cout[0], send_sems, recv_sems)
        for cp in arrivals:
            cp.wait_recv()
        for cp in sends:
            cp.wait_send()
        mine.wait()

    return _Comm([x], [jax.ShapeDtypeStruct((8,) + x.shape, F32)], {}, 8, start, finish)


def _join(a, b):
    na_in, na_out = len(a.operands), len(a.out_shape)

    def split(fn_a, fn_b):
        def both(cin, cout, send_sems, recv_sems):
            fn_a(cin[:na_in], cout[:na_out], send_sems.at[pl.ds(0, a.n_sems)], recv_sems.at[pl.ds(0, a.n_sems)])
            fn_b(cin[na_in:], cout[na_out:], send_sems.at[pl.ds(a.n_sems, b.n_sems)], recv_sems.at[pl.ds(a.n_sems, b.n_sems)])
        return both

    aliases = dict(a.aliases)
    aliases.update({na_in + i: na_out + o for i, o in b.aliases.items()})
    return _Comm(a.operands + b.operands, a.out_shape + b.out_shape, aliases, a.n_sems + b.n_sems, split(a.start, b.start), split(a.finish, b.finish))


def _region(ref, kind, j, half, r, cc):
    nr = r if half is None else r // 2
    off = 0 if half is None else half * nr
    if kind == "col":
        return ref.at[pl.ds(off, nr), pl.ds(pl.multiple_of(j * cc, 128), cc)]
    return ref.at[pl.ds(pl.multiple_of(j * r + off, 16), nr), :]


def comm_call(comm, name):
    ni, no = len(comm.operands), len(comm.out_shape)

    def body(*refs):
        comm.start(refs[:ni], refs[ni:ni + no], *refs[ni + no:])
        comm.finish(refs[:ni], refs[ni:ni + no], *refs[ni + no:])

    return pl.pallas_call(
        body, name=name, out_shape=comm.out_shape, in_specs=[ANY] * ni, out_specs=[ANY] * no, input_output_aliases=comm.aliases,
        scratch_shapes=[pltpu.SemaphoreType.DMA((comm.n_sems,)), pltpu.SemaphoreType.DMA((comm.n_sems,))], compiler_params=_cp(),
    )(*comm.operands)


def gather_comm(fulls, kinds, dims):
    n = len(fulls)

    def copies(f_refs, send_sems, recv_sems):
        mx, my, mc, chips = _place()
        jme = 2 * mx + my

        def landed(w, k, half):
            px, py = chips[k]
            return _region(f_refs[w], kinds[w], 2 * px + py, half, *dims[w])

        def over_ici(w, k, reg):
            px, py = chips[k]
            return pltpu.make_async_remote_copy(src_ref=reg, dst_ref=reg, send_sem=send_sems.at[6 * w + k], recv_sem=recv_sems.at[6 * w + k],
                                                device_id=(px, py, mc), device_id_type=MESH)

        def over_d2d(w, k, half):
            reg = landed(w, k, half)
            return pltpu.make_async_remote_copy(src_ref=reg, dst_ref=reg, send_sem=send_sems.at[6 * w + 3 + k],
                                                recv_sem=recv_sems.at[6 * w + 3 + k], device_id=(mx, my, 1 - mc), device_id_type=MESH)

        sends = [over_ici(w, k, _region(f_refs[w], kinds[w], jme, mc, *dims[w])) for w in range(n) for k in range(3)]
        return mc, landed, over_ici, over_d2d, sends

    def start(cin, f_refs, send_sems, recv_sems):
        for cp in copies(f_refs, send_sems, recv_sems)[4]:
            cp.start()

    def finish(cin, f_refs, send_sems, recv_sems):
        mc, landed, over_ici, over_d2d, sends = copies(f_refs, send_sems, recv_sems)
        passed = []
        for w in range(n):
            for k in range(3):
                over_ici(w, k, landed(w, k, mc)).wait_recv()
                cp = over_d2d(w, k, mc)
                cp.start()
                passed.append(cp)
        for w in range(n):
            for k in range(3):
                over_d2d(w, k, 1 - mc).wait_recv()
        for cp in sends + passed:
            cp.wait_send()

    return _Comm(fulls, [jax.ShapeDtypeStruct(f.shape, BF16) for f in fulls], {w: w for w in range(n)}, 6 * n, start, finish)


def exchange_halves(grads, name):
    n = len(grads)

    def body(*refs):
        g_refs, l_refs = refs[:n], refs[n:2 * n]
        send_sems, recv_sems = refs[2 * n:]
        mx, my, mc, _ = _place()
        cps = [pltpu.make_async_remote_copy(src_ref=g_refs[w].at[:, pl.ds(1 - mc, 1)], dst_ref=l_refs[w], send_sem=send_sems.at[w],
                                            recv_sem=recv_sems.at[w], device_id=(mx, my, 1 - mc), device_id_type=MESH) for w in range(n)]
        for cp in cps:
            cp.start()
        for cp in cps:
            cp.wait()

    return pl.pallas_call(
        body, name=name, out_shape=[jax.ShapeDtypeStruct((g.shape[0], 1) + g.shape[2:], BF16) for g in grads],
        in_specs=[ANY] * n, out_specs=[ANY] * n,
        scratch_shapes=[pltpu.SemaphoreType.DMA((n,)), pltpu.SemaphoreType.DMA((n,))], compiler_params=_cp(),
    )(*grads)


def scatter_comm(parts):
    n = len(parts)

    def sends(p_refs, l_refs, send_sems, recv_sems):
        mx, my, mc, chips = _place()
        return [pltpu.make_async_remote_copy(src_ref=p_refs[w].at[2 * px + py], dst_ref=l_refs[w].at[2 * mx + my],
                                             send_sem=send_sems.at[3 * w + k], recv_sem=recv_sems.at[3 * w + k],
                                             device_id=(px, py, mc), device_id_type=MESH) for w in range(n) for k, (px, py) in enumerate(chips)]

    def start(p_refs, l_refs, send_sems, recv_sems):
        for cp in sends(p_refs, l_refs, send_sems, recv_sems):
            cp.start()

    def finish(p_refs, l_refs, send_sems, recv_sems):
        mx, my, mc, chips = _place()
        for w in range(n):
            for k, (px, py) in enumerate(chips):
                slot = l_refs[w].at[2 * px + py]
                pltpu.make_async_remote_copy(src_ref=slot, dst_ref=slot, send_sem=send_sems.at[3 * w + k], recv_sem=recv_sems.at[3 * w + k],
                                             device_id=(px, py, mc), device_id_type=MESH).wait_recv()
        for cp in sends(p_refs, l_refs, send_sems, recv_sems):
            cp.wait_send()

    return _Comm(parts, [jax.ShapeDtypeStruct(p.shape, BF16) for p in parts], {}, 3 * n, start, finish)


def share_comm(sums):
    n = len(sums)

    def copies(q_refs, o_refs, send_sems, recv_sems):
        mx, my, mc, _ = _place()
        return [pltpu.make_async_remote_copy(src_ref=q_refs[w], dst_ref=o_refs[w], send_sem=send_sems.at[w], recv_sem=recv_sems.at[w],
                                             device_id=(mx, my, 1 - mc), device_id_type=MESH) for w in range(n)]

    def start(*refs):
        for cp in copies(*refs):
            cp.start()

    def finish(*refs):
        for cp in copies(*refs):
            cp.wait()

    return _Comm(sums, [jax.ShapeDtypeStruct(q.shape, F32) for q in sums], {}, n, start, finish)


def _pack(arrays):
    flat = jnp.concatenate([a.reshape(-1) for a in arrays])
    rows = -(-flat.shape[0] // 1024) * 8
    return jnp.pad(flat, (0, rows * 128 - flat.shape[0])).reshape(rows, 128)


def _unpack(packed, shapes):
    flat, out, off = packed.reshape(-1), [], 0
    for s in shapes:
        n = math.prod(s)
        out.append(flat[off:off + n].reshape(s))
        off += n
    return out


def kernel(x, c, w_ada, b_ada, g_pre_mix, g_post_mix, g_pre_ffn, g_post_ffn, w_in, lb_logits, g_hgrn_norm, w_a_out, g_sgu_norm, w_spatial, b_spatial, w_b_out, w_o, w_ff1, w_ff2, loss_target, m_w_ada, m_b_ada, m_g_pre_mix, m_g_post_mix, m_g_pre_ffn, m_g_post_ffn, m_w_in, m_lb_logits, m_g_hgrn_norm, m_w_a_out, m_g_sgu_norm, m_w_spatial, m_b_spatial, m_w_b_out, m_w_o, m_w_ff1, m_w_ff2, v_w_ada, v_b_ada, v_g_pre_mix, v_g_post_mix, v_g_pre_ffn, v_g_post_ffn, v_w_in, v_lb_logits, v_g_hgrn_norm, v_w_a_out, v_g_sgu_norm, v_w_spatial, v_b_spatial, v_w_b_out, v_w_o, v_w_ff1, v_w_ff2):
    mx, my, mc = lax.axis_index("x"), lax.axis_index("y"), lax.axis_index("c")
    chip, me = 2 * mx + my, 4 * mx + 2 * my + mc
    D = D_MODEL
    h0, tgt = x[0], loss_target[0]
    n_ada = w_ada.shape[2]
    n_lb = lb_logits.shape[2]

    got = all_gather_small(_pack([c, lb_logits]), "gather_inputs")
    c_all = got[:, :D // 128, :].reshape(8, D)
    lb_full = got[0::2, D // 128:D // 128 + 4 * n_lb // 128, :].reshape(4, 2, 2, n_lb).transpose(1, 2, 0, 3).reshape(2, 2, 4 * n_lb)
    b_ada_chip = lax.dynamic_slice(b_ada, (0, chip * n_ada), (1, n_ada))
    mod_cols = mod_matmul(c_all, w_ada[0], b_ada_chip)
    got = all_gather_small(mod_cols.reshape(-1, 128), "gather_mod").reshape(4, 2, 8, n_ada)
    mod = lax.dynamic_index_in_dim(got[:, 0], me, axis=1, keepdims=False).reshape(6, 1, D)
    sh1, sc1, gt1, sh2, sc2, gt2 = (mod[i] for i in range(6))

    big = [("w_in", w_in, "col"), ("w_a_out", w_a_out, "row"), ("w_b_out", w_b_out, "row"), ("w_o", w_o, "row"),
           ("w_ff1", w_ff1, "col"), ("w_ff2", w_ff2, "row")]
    kinds = [k for _, _, k in big]
    chip_idx, core = chip.reshape(1).astype(jnp.int32), mc.reshape(1).astype(jnp.int32)
    fulls = [cast_into_full(w[0], kind, chip_idx, "cast_" + nm) for nm, w, kind in big]
    dims = [w.shape[1:] for _, w, _ in big]
    later = lambda lo, hi: gather_comm(fulls[lo:hi], kinds[lo:hi], dims[lo:hi])
    halves_summed = lambda grads, name: [add_halves(g, l, core) for g, l in zip(grads, exchange_halves(grads, name))]

    bst = b_spatial[0].T
    a1 = prenorm(h0, g_pre_mix, sc1, sh1)
    proj, w_in_f, (w_a_f, w_b_f, w_o_f) = in_proj_gathered(a1, fulls[0], chip_idx, dims[0], later(1, 4))
    o, (w_ff1_f,) = hgrn_fwd(proj, lb_full, comm=later(4, 5))
    ya_pre = hgrn_post_fwd(o, proj, g_hgrn_norm)
    sgu = sgu_fwd(proj, g_sgu_norm, w_spatial[0], bst)
    y_a, y_b, merged = merge_matmul(ya_pre, sgu, w_a_f, w_b_f, proj)
    mo, h1 = out_proj(merged, w_o_f, h0, gt1, g_post_mix)
    (f1, a2, hid), (w_ff2_f,) = prenorm_matmul(h1, g_pre_ffn, sc2, sh2, w_ff1_f, relu2=True, name="ff1", comm=later(5, 6))
    dy, dff, loss_parts, d_gt2, d_g_post_ffn = ff2_loss(hid, w_ff2_f, h1, tgt, gt2, g_post_ffn)
    loss = lax.psum(0.5 * loss_parts[0, 0] / D, ("x", "y", "c"))

    df1 = ff2_bwd(dff, w_ff2_f, f1)
    gr_ff2 = matmul(hid, dff, mode="tn", out_dtype=BF16, tm=1024, tn=1024, tk=2048, name="dw_ff2")
    da2 = matmul(df1, w_ff1_f, mode="nt", out_dtype=F32, tm=1024, tn=1024, tk=2048, name="da2")
    gr_ff1 = matmul(a2, df1, mode="tn", out_dtype=BF16, tm=1024, tn=2048, tk=1024, name="dw_ff1", split=(4, 2))
    parts_ff = halves_summed([gr_ff1, gr_ff2.reshape(4, 2, -1, D)], "exchange_ff")
    dh1, dmo, d_sh2, d_sc2, d_g_pre_ffn, d_gt1, d_g_post_mix = ffn_norm_bwd(dy, da2, h1, mo, g_pre_ffn, sc2, gt1, g_post_mix)
    dya, dyb, dga, dgb = out_proj_bwd(dmo, w_o_f, y_a, y_b, proj)
    gr_o = matmul(merged, dmo, mode="tn", out_dtype=BF16, tm=1024, tn=1024, tk=2048, name="dw_o")
    dsgu = matmul(dyb, w_b_f, mode="nt", out_dtype=F32, tm=512, tn=1024, tk=512, name="dsgu", b_stacked=True)
    gr_b = matmul(sgu, dyb, mode="tn", out_dtype=BF16, tm=512, tn=512, tk=4096, name="dw_b_out", split=(4, 2))
    dz, d_w_spatial, d_b_spatial, d_g_sgu = sgu_bwd(proj, dsgu, g_sgu_norm, w_spatial[0], bst)
    dya_pre = matmul(dya, w_a_f, mode="nt", out_dtype=F32, tm=512, tn=1024, tk=512, name="dya_pre", b_stacked=True)
    gr_a = matmul(ya_pre, dya, mode="tn", out_dtype=BF16, tm=512, tn=512, tk=4096, name="dw_a_out", split=(4, 2))
    parts_mix = halves_summed([gr_a, gr_b, gr_o.reshape(4, 2, -1, D)], "exchange_mix")
    do, dog, d_g_hgrn = hgrn_post_bwd(dya_pre, o, proj, g_hgrn_norm)
    chips_summed = lambda parts, landed: [sum_chips(p, l, chip_idx) for p, l in zip(parts, landed)]
    (dq, dv, dlg, d_lb), landed_ff = hgrn_bwd(proj, do, lb_full, comm=scatter_comm(parts_ff))
    own_ff = chips_summed(parts_ff, landed_ff)
    dproj = jnp.concatenate([dq, dlg, dv, dog, dz, dga, dgb], axis=1)
    early = _pack([d_g_sgu, d_w_spatial, d_b_spatial[:, 0, :]])
    gr_in, (*landed_mix, got_early) = matmul(a1, dproj, mode="tn", out_dtype=BF16, tm=1024, tn=2816, tk=1024, name="dw_in", split=(4, 2),
                                             comm=_join(scatter_comm(parts_mix), gather8_comm(early)))
    own_mix = chips_summed(parts_mix, landed_mix)
    parts_in = halves_summed([gr_in], "exchange_in")
    da1, (landed_in, *other_rest) = matmul(dproj, w_in_f, mode="nt", out_dtype=F32, tm=1024, tn=1024, tk=2816, name="da1",
                                           comm=_join(scatter_comm(parts_in), share_comm(own_mix + own_ff)))
    own_in = chips_summed(parts_in, [landed_in])
    other_in = comm_call(share_comm(own_in), "share_w_in")
    own, other = own_in + own_mix + own_ff, list(other_in) + other_rest
    grad_x, d_sh1, d_sc1, d_g_pre_mix = mix_norm_bwd(da1, h0, dh1, g_pre_mix, sc1)
    out = {}

    mine = _pack([d_sh1, d_sc1, d_gt1, d_sh2, d_sc2, d_gt2, d_g_pre_mix, d_g_post_mix, d_g_pre_ffn, d_g_post_ffn, d_g_hgrn, d_lb])
    got = all_gather_small(mine, "gather_small_grads")
    g_b_ada, g_g1, g_g2, g_g3, g_g4, g_hg, g_lb = _unpack(
        sum_devices(got, "sum_small_grads"), [(1, 6 * D), (1, D), (1, D), (1, D), (1, D), (1, HEAD_DIM), (2, 1024)])
    g_sg, g_ws, g_bs = _unpack(sum_devices(got_early, "sum_sgu_grads"), [(1, 1024), w_spatial.shape, b_spatial.shape])
    g_lbl = lax.dynamic_slice(lb_logits_grad(g_lb, lb_full), (0, 0, chip * n_lb), (2, 2, n_lb))
    names = ["b_ada", "g_pre_mix", "g_post_mix", "g_pre_ffn", "g_post_ffn", "g_hgrn_norm", "g_sgu_norm", "w_spatial", "b_spatial", "lb_logits"]
    ws = [b_ada, g_pre_mix, g_post_mix, g_pre_ffn, g_post_ffn, g_hgrn_norm, g_sgu_norm, w_spatial, b_spatial, lb_logits]
    gs = [g_b_ada, g_g1, g_g2, g_g3, g_g4, g_hg, g_sg, g_ws, g_bs, g_lbl]
    ms = [m_b_ada, m_g_pre_mix, m_g_post_mix, m_g_pre_ffn, m_g_post_ffn, m_g_hgrn_norm, m_g_sgu_norm, m_w_spatial, m_b_spatial, m_lb_logits]
    vs = [v_b_ada, v_g_pre_mix, v_g_post_mix, v_g_pre_ffn, v_g_post_ffn, v_g_hgrn_norm, v_g_sgu_norm, v_w_spatial, v_b_spatial, v_lb_logits]
    shapes = [w.shape for w in ws]
    upd = adamw(_pack(ws), _pack(gs), _pack(ms), _pack(vs), "adamw_small")
    upd = [_unpack(u, shapes) for u in upd]
    for i, nm in enumerate(names):
        out[nm] = (gs[i], upd[0][i], upd[1][i], upd[2][i])

    dmod_all = got[:, :6 * D // 128, :].reshape(8, 6 * D)
    dmod_chip = lax.dynamic_slice(dmod_all, (0, chip * n_ada), (8, n_ada))
    out["w_ada"] = tuple(a[None] for a in wada_update(c_all, dmod_chip, w_ada[0], m_w_ada[0], v_w_ada[0]))
    for (nm, w, _), a, b, m, v in zip(big, own, other, (m_w_in, m_w_a_out, m_w_b_out, m_w_o, m_w_ff1, m_w_ff2),
                                      (v_w_in, v_w_a_out, v_w_b_out, v_w_o, v_w_ff1, v_w_ff2)):
        out[nm] = tuple(t[None] for t in adamw_halves(w[0], a, b, m[0], v[0], core, "adamw_" + nm))

    order = ["w_ada", "b_ada", "g_pre_mix", "g_post_mix", "g_pre_ffn", "g_post_ffn", "w_in", "lb_logits", "g_hgrn_norm", "w_a_out",
             "g_sgu_norm", "w_spatial", "b_spatial", "w_b_out", "w_o", "w_ff1", "w_ff2"]
    return (loss, grad_x[None], *[out[nm][0] for nm in order], *[out[nm][1] for nm in order], *[out[nm][2] for nm in order],
            *[out[nm][3] for nm in order])
```

```python
import functools
import math

import jax
import jax.numpy as jnp
from jax import lax
from jax.experimental import pallas as pl
from jax.experimental.pallas import tpu as pltpu

F32, BF16 = jnp.float32, jnp.bfloat16
HI = lax.Precision.HIGHEST
MESH = pl.DeviceIdType.MESH
ANY = pl.BlockSpec(memory_space=pl.ANY)

EPS = 1e-6
D_MODEL = 2048
N_HEADS = 8
HEAD_DIM = 128
HGRN_CHUNK = 32
HGRN_BLOCK = 256
SGU_CHUNK = 128
SGU_GROUPS = 8
Q_SCALE = HEAD_DIM ** -0.5
COL_Q, COL_FFW, COL_FBW, COL_V, COL_OG, COL_U, COL_ZV, COL_GA, COL_GB = 0, 1, 2, 3, 4, 5, 6, 7, 9
N_PROJ = 11264
VMEM_BYTES_V7X = 64 * 1024 * 1024
VMEM_LIMIT = VMEM_BYTES_V7X - 8 * 1024 * 1024

ADAM_LR, ADAM_B1, ADAM_B2, ADAM_EPS, ADAM_WD, ADAM_STEP = 0.001, 0.9, 0.999, 1e-08, 0.01, 10
ADAM_C1 = 1.0 - ADAM_B1 ** ADAM_STEP
ADAM_C2 = 1.0 - ADAM_B2 ** ADAM_STEP


def _cp(*sem):
    return pltpu.CompilerParams(dimension_semantics=sem if sem else None, vmem_limit_bytes=VMEM_LIMIT)


def _vec(d):
    return pl.BlockSpec((1, d), lambda *_: (0, 0))


def _colsum(x):
    return jnp.sum(x, axis=0, keepdims=True)


def _nt(a, b):
    return lax.dot_general(a, b, (((1,), (1,)), ((), ())), preferred_element_type=F32)


def _tn(a, b):
    return lax.dot_general(a, b, (((0,), (0,)), ((), ())), preferred_element_type=F32)


def _nn(a, b):
    return jnp.dot(a, b, preferred_element_type=F32)


def _adamw(w, g, m, v):
    m2 = ADAM_B1 * m + (1.0 - ADAM_B1) * g
    v2 = ADAM_B2 * v + (1.0 - ADAM_B2) * (g * g)
    delta = -ADAM_LR * ((m2 / ADAM_C1) / (jnp.sqrt(v2 / ADAM_C2) + ADAM_EPS) + ADAM_WD * w)
    return delta, m2, v2


class _Comm:
    def __init__(self, operands, out_shape, aliases, n_sems, start, finish):
        self.operands, self.out_shape, self.aliases, self.n_sems = list(operands), list(out_shape), dict(aliases), n_sems
        self.start, self.finish = start, finish


def _pallas(body, *, name, grid, in_specs, out_specs, out_shape, scratch, semantics, operands, comm=None):
    if comm is None:
        res = pl.pallas_call(body, name=name, grid=grid, in_specs=in_specs, out_specs=out_specs, out_shape=out_shape,
                             scratch_shapes=scratch, compiler_params=_cp(*semantics))(*operands)
        return res, []
    n_in, n_out, n_scr = len(in_specs), len(out_specs), len(scratch)
    nci, nco = len(comm.operands), len(comm.out_shape)

    def with_comm(*refs):
        ins, rest = refs[:n_in], refs[n_in:]
        cin, rest = rest[:nci], rest[nci:]
        outs, rest = rest[:n_out], rest[n_out:]
        cout, rest = rest[:nco], rest[nco:]
        scr, (send, recv) = rest[:n_scr], rest[n_scr:]
        ids = [pl.program_id(a) for a in range(len(grid))]
        first = functools.reduce(jnp.logical_and, [i == 0 for i in ids])
        last = functools.reduce(jnp.logical_and, [i == g - 1 for i, g in zip(ids, grid)])

        @pl.when(first)
        def _():
            comm.start(cin, cout, send, recv)

        body(*ins, *outs, *scr)

        @pl.when(last)
        def _():
            comm.finish(cin, cout, send, recv)

    res = pl.pallas_call(
        with_comm, name=name, grid=grid, in_specs=list(in_specs) + [ANY] * nci, out_specs=list(out_specs) + [ANY] * nco,
        out_shape=list(out_shape) + comm.out_shape, input_output_aliases={n_in + i: n_out + o for i, o in comm.aliases.items()},
        scratch_shapes=list(scratch) + [pltpu.SemaphoreType.DMA((comm.n_sems,)), pltpu.SemaphoreType.DMA((comm.n_sems,))],
        compiler_params=_cp(*["arbitrary"] * len(grid)),
    )(*operands, *comm.operands)
    return res[:n_out], res[n_out:]


def matmul(a, b, *, mode, out_dtype, tm, tn, tk, name, split=None, comm=None, b_stacked=False):
    if mode == "tn":
        (K, M), (_, N) = a.shape, b.shape
    elif mode == "nt":
        (M, K), (N, _) = a.shape, b.shape
        N = N // 4 if b_stacked else N
    else:
        (M, K), (_, N) = a.shape, b.shape
    tm, tn, tk = min(tm, M), min(tn, N), min(tk, K)
    if b_stacked:
        tk = K // 4
    nk = K // tk
    a_spec = pl.BlockSpec((tk, tm), lambda i, j, k: (k, i)) if mode == "tn" else pl.BlockSpec((tm, tk), lambda i, j, k: (i, k))
    b_spec = pl.BlockSpec((tn, tk), lambda i, j, k: (j, k)) if mode == "nt" else pl.BlockSpec((tk, tn), lambda i, j, k: (k, j))
    if b_stacked:
        b_spec = pl.BlockSpec((tn, tk), lambda i, j, k: (k * (N // tn) + j, 0))
    dot = {"nn": _nn, "nt": _nt, "tn": _tn}[mode]
    if split is None:
        out_shape = jax.ShapeDtypeStruct((M, N), out_dtype)
        out_spec = pl.BlockSpec((tm, tn), lambda i, j, k: (i, j))
    else:
        nj, nh = split
        rows, cols = M // nh, N // nj
        tm, tn = min(tm, rows), min(tn, cols)
        bi, bj = rows // tm, cols // tn
        out_shape = jax.ShapeDtypeStruct((nj, nh, rows, cols), out_dtype)
        out_spec = pl.BlockSpec((None, None, tm, tn), lambda i, j, k: (j // bj, i // bi, i % bi, j % bj))

    def body(a_ref, b_ref, o_ref, acc_ref):
        k = pl.program_id(2)

        @pl.when(k == 0)
        def _():
            acc_ref[...] = jnp.zeros_like(acc_ref)

        acc_ref[...] += dot(a_ref[...], b_ref[...])

        @pl.when(k == nk - 1)
        def _():
            o_ref[...] = acc_ref[...].astype(o_ref.dtype)

    (out,), landed = _pallas(
        body, name=name, grid=(M // tm, N // tn, nk), in_specs=[a_spec, b_spec], out_specs=[out_spec], out_shape=[out_shape],
        scratch=[pltpu.VMEM((tm, tn), F32)], semantics=("parallel", "parallel", "arbitrary"), operands=(a, b), comm=comm)
    return out if comm is None else (out, landed)


def cast_into_full(w, kind, chip, name):
    r, cc = w.shape
    tr = min(r, 512)
    nb = r // tr

    def body(chip_ref, w_ref, o_ref):
        o_ref[...] = w_ref[...].astype(BF16)

    if kind == "col":
        full, out_map = (r, 4 * cc), lambda i, chip_ref: (i, chip_ref[0])
    else:
        full, out_map = (4 * r, cc), lambda i, chip_ref: (chip_ref[0] * nb + i, 0)
    return pl.pallas_call(
        body, name=name, out_shape=jax.ShapeDtypeStruct(full, BF16),
        grid_spec=pltpu.PrefetchScalarGridSpec(
            num_scalar_prefetch=1, grid=(nb,), in_specs=[pl.BlockSpec((tr, cc), lambda i, chip_ref: (i, 0))],
            out_specs=pl.BlockSpec((tr, cc), out_map)),
        compiler_params=_cp("parallel"),
    )(chip, w)


def mod_matmul(c_all, w_ada, b_ada):
    D, N = w_ada.shape
    tn = 1024

    def body(c_ref, w_ref, b_ref, o_ref):
        c = c_ref[...]
        sc = c * jax.nn.sigmoid(c)
        o_ref[...] = jnp.dot(sc, w_ref[...], precision=HI, preferred_element_type=F32) + b_ref[...]

    return pl.pallas_call(
        body, name="mod_matmul", out_shape=jax.ShapeDtypeStruct((8, N), F32), grid=(N // tn,),
        in_specs=[pl.BlockSpec((8, D), lambda j: (0, 0)), pl.BlockSpec((D, tn), lambda j: (0, j)),
                  pl.BlockSpec((1, tn), lambda j: (0, j))],
        out_specs=pl.BlockSpec((8, tn), lambda j: (0, j)), compiler_params=_cp("parallel"),
    )(c_all, w_ada, b_ada)


def prenorm(h, g, sc, sh):
    T, D = h.shape
    tm = min(256, T)

    def body(h_ref, g_ref, sc_ref, sh_ref, a_ref):
        x = h_ref[...]
        r = lax.rsqrt(jnp.mean(x * x, axis=-1, keepdims=True) + EPS)
        a_ref[...] = ((x * r) * g_ref[...] * (1.0 + sc_ref[...]) + sh_ref[...]).astype(BF16)

    row = pl.BlockSpec((tm, D), lambda i: (i, 0))
    return pl.pallas_call(
        body, name="prenorm", out_shape=jax.ShapeDtypeStruct((T, D), BF16), grid=(T // tm,),
        in_specs=[row, _vec(D), _vec(D), _vec(D)], out_specs=row, compiler_params=_cp("parallel"),
    )(h, g, sc, sh)


def in_proj_gathered(a, w_full, chip, dims, tail):
    T, D = a.shape
    rows, cc = dims
    tm, tn = min(512, T), cc // 2
    ni = T // tm
    half = rows // 2

    nt = len(tail.operands)

    def body(chip_ref, a_ref, w_in_ref, *rest):
        tail_in, (y_ref, w_ref), rest = rest[:nt], rest[nt:nt + 2], rest[nt + 2:]
        tail_out, (wbuf, wsem, send_sems, recv_sems, tail_send, tail_recv) = rest[:nt], rest[nt:]
        q, j, i = pl.program_id(0), pl.program_id(1), pl.program_id(2)
        mx, my, mc, _ = _place()
        me = chip_ref[0]

        def tile(block, jj):
            src = w_ref.at[:, pl.ds(pl.multiple_of(block * cc + jj * tn, 128), tn)]
            return pltpu.make_async_copy(src, wbuf.at[jj], wsem.at[jj])

        def rows_half(block, hh):
            return w_ref.at[pl.ds(pl.multiple_of(hh * half, 16), half), pl.ds(pl.multiple_of(block * cc, 128), cc)]

        def over_ici(s, block):
            peer = (1 - mx if s & 2 else mx, 1 - my if s & 1 else my, mc)
            reg = rows_half(block, mc)
            return pltpu.make_async_remote_copy(src_ref=reg, dst_ref=reg, send_sem=send_sems.at[s - 1], recv_sem=recv_sems.at[s - 1],
                                                device_id=peer, device_id_type=MESH)

        def over_d2d(s, block, hh):
            reg = rows_half(block, hh)
            return pltpu.make_async_remote_copy(src_ref=reg, dst_ref=reg, send_sem=send_sems.at[2 + s], recv_sem=recv_sems.at[2 + s],
                                                device_id=(mx, my, 1 - mc), device_id_type=MESH)

        @pl.when((q == 0) & (j == 0) & (i == 0))
        def _():
            for s in (1, 2, 3):
                over_ici(s, me).start()
            tile(me, 0).start()
            tail.start(tail_in, tail_out, tail_send, tail_recv)

        @pl.when(i == 0)
        def _():
            tile(me ^ q, j).wait()

        @pl.when((i == 0) & (j == 0))
        def _():
            tile(me ^ q, 1).start()

        y_ref[...] = _nn(a_ref[...], wbuf[j])

        for s in (1, 2, 3):
            @pl.when((q == s - 1) & (j == 1) & (i == ni - 1))
            def _():
                block = me ^ s
                over_ici(s, block).wait_recv()
                over_d2d(s, block, mc).start()
                over_d2d(s, block, 1 - mc).wait_recv()
                tile(block, 0).start()


        @pl.when((q == 3) & (j == 1) & (i == ni - 1))
        def _():
            for s in (1, 2, 3):
                over_ici(s, me).wait_send()
                over_d2d(s, me ^ s, mc).wait_send()
            tail.finish(tail_in, tail_out, tail_send, tail_recv)

    dma = pltpu.SemaphoreType.DMA
    y, w_out, *tail_res = pl.pallas_call(
        body, name="in_proj", out_shape=[jax.ShapeDtypeStruct((T, 4 * cc), F32), jax.ShapeDtypeStruct(w_full.shape, BF16)] + tail.out_shape,
        grid_spec=pltpu.PrefetchScalarGridSpec(
            num_scalar_prefetch=1, grid=(4, 2, ni),
            in_specs=[pl.BlockSpec((tm, D), lambda q, j, i, chip_ref: (i, 0)), ANY] + [ANY] * nt,
            out_specs=[pl.BlockSpec((tm, tn), lambda q, j, i, chip_ref: (i, (chip_ref[0] ^ q) * 2 + j)), ANY] + [ANY] * nt,
            scratch_shapes=[pltpu.VMEM((2, D, tn), BF16), dma((2,)), dma((6,)), dma((6,)), dma((tail.n_sems,)), dma((tail.n_sems,))]),
        input_output_aliases={2: 1, **{3 + i: 2 + o for i, o in tail.aliases.items()}},
        compiler_params=_cp("arbitrary", "arbitrary", "arbitrary"),
    )(chip, a, w_full, *tail.operands)
    return y, w_out, tail_res


def prenorm_matmul(h, g, sc, sh, w, *, relu2, name, comm=None):
    T, D = h.shape
    N = w.shape[1]
    tm, tn = min(512, T), 2048 if N % 2048 == 0 else 1024

    def body(h_ref, g_ref, sc_ref, sh_ref, w_ref, y_ref, a_ref, *hid_ref):
        @pl.when(pl.program_id(1) == 0)
        def _():
            x = h_ref[...]
            r = lax.rsqrt(jnp.mean(x * x, axis=-1, keepdims=True) + EPS)
            a_ref[...] = ((x * r) * g_ref[...] * (1.0 + sc_ref[...]) + sh_ref[...]).astype(BF16)

        y = _nn(a_ref[...], w_ref[...])
        y_ref[...] = y
        if relu2:
            p = jnp.maximum(y, 0.0)
            hid_ref[0][...] = (p * p).astype(BF16)

    out_shape = [jax.ShapeDtypeStruct((T, N), F32), jax.ShapeDtypeStruct((T, D), BF16)]
    out_specs = [pl.BlockSpec((tm, tn), lambda i, j: (i, j)), pl.BlockSpec((tm, D), lambda i, j: (i, 0))]
    if relu2:
        out_shape.append(jax.ShapeDtypeStruct((T, N), BF16))
        out_specs.append(pl.BlockSpec((tm, tn), lambda i, j: (i, j)))
    outs, landed = _pallas(
        body, name=name, grid=(T // tm, N // tn),
        in_specs=[pl.BlockSpec((tm, D), lambda i, j: (i, 0)), _vec(D), _vec(D), _vec(D), pl.BlockSpec((D, tn), lambda i, j: (0, j))],
        out_specs=out_specs, out_shape=out_shape, scratch=[], semantics=("parallel", "arbitrary"), operands=(h, g, sc, sh, w), comm=comm)
    return outs if comm is None else (outs, landed)


def _hgrn_lower_bound(l_ref):
    l0, l1 = l_ref[0:1, :], l_ref[1:2, :]
    m = jnp.maximum(l0, l1)
    e0, e1 = jnp.exp(l0 - m), jnp.exp(l1 - m)
    return e0 / (e0 + e1)


def _hgrn_chunk_mask(d):
    r = lax.broadcasted_iota(jnp.int32, (HGRN_BLOCK, HGRN_BLOCK), 0)
    c = lax.broadcasted_iota(jnp.int32, (HGRN_BLOCK, HGRN_BLOCK), 1)
    same = (r // HGRN_CHUNK) == (c // HGRN_CHUNK)
    fwd = d == 0
    return same & (((c <= r) & fwd) | ((c >= r) & jnp.logical_not(fwd)))


def _chunk_total(x):
    x3 = x.reshape(HGRN_BLOCK // HGRN_CHUNK, HGRN_CHUNK, x.shape[1])
    return jnp.broadcast_to(jnp.sum(x3, axis=1, keepdims=True), x3.shape).reshape(x.shape)


def _chunk_cumsum(x, suffix):
    pos = lax.broadcasted_iota(jnp.int32, x.shape, 0) % HGRN_CHUNK
    p, s = x, 1
    while s < HGRN_CHUNK:
        p = p + jnp.where(pos >= s, pltpu.roll(p, s, 0), 0.0)
        s *= 2
    return jnp.where(suffix, _chunk_total(x) - p + x, p)


def _block_loop(T, body, init):
    n = T // HGRN_BLOCK
    return lax.fori_loop(0, n, body, init, unroll=2 if n % 2 == 0 else 1)


def _hgrn_gate(f, lb):
    s = jax.nn.sigmoid(f)
    sn = jax.nn.sigmoid(-f)
    fg = lb + (1.0 - lb) * s
    return s, sn, fg, jnp.log(fg), (1.0 - lb) * sn


def _hgrn_specs(T):
    col = lambda base: pl.BlockSpec((T, HEAD_DIM), lambda h, d: (0, base * N_HEADS + h))
    f_spec = pl.BlockSpec((T, HEAD_DIM), lambda h, d: (0, COL_FFW * N_HEADS + N_HEADS * d + h))
    l_spec = pl.BlockSpec((None, 2, HEAD_DIM), lambda h, d: (d, 0, h))
    return col, f_spec, l_spec


def hgrn_fwd(proj, lb_logits, comm=None):
    T = proj.shape[0]
    NC, CPB = T // HGRN_CHUNK, HGRN_BLOCK // HGRN_CHUNK
    col, f_spec, l_spec = _hgrn_specs(T)

    def body(l_ref, q_ref, f_ref, v_ref, o_ref, st_ref, dec_ref, qd_ref):
        d = pl.program_id(1)
        lb = _hgrn_lower_bound(l_ref)
        mask = _hgrn_chunk_mask(d)

        def block(i, carry):
            rows = pl.ds(pl.multiple_of(i * HGRN_BLOCK, HGRN_BLOCK), HGRN_BLOCK)
            _, _, _, lf, k = _hgrn_gate(f_ref[rows, :], lb)
            b = _chunk_cumsum(lf, d == 1)
            bl = _chunk_total(lf)
            qd = (q_ref[rows, :] * Q_SCALE * jnp.exp(b)).astype(BF16)
            kd = (k * jnp.exp(-b)).astype(BF16)
            ke = (k * jnp.exp(bl - b)).astype(BF16)
            vb = v_ref[rows, :].astype(BF16)
            att = jnp.where(mask, _nt(qd, kd), 0.0).astype(BF16)
            o_ref[rows, :] = jnp.where(d == 0, 0.0, o_ref[rows, :]) + _nn(att, vb)
            qd_ref[rows, :] = qd
            dec = jnp.exp(bl)
            for cc in range(CPB):
                sl = slice(cc * HGRN_CHUNK, (cc + 1) * HGRN_CHUNK)
                n = i * CPB + cc
                st_ref[n] = _tn(vb[sl], ke[sl])
                dec_ref[n] = dec[cc * HGRN_CHUNK:cc * HGRN_CHUNK + 8, :]
            return carry

        _block_loop(T, block, 0)

        def scan(t, s):
            n = jnp.where(d == 0, t, NC - 1 - t)
            u = st_ref[n]
            st_ref[n] = s
            return dec_ref[n][0:1, :] * s + u

        lax.fori_loop(0, NC, scan, jnp.zeros((HEAD_DIM, HEAD_DIM), F32))

        def inter(i, carry):
            rows = pl.ds(pl.multiple_of(i * HGRN_BLOCK, HGRN_BLOCK), HGRN_BLOCK)
            qd = qd_ref[rows, :]
            o_ref[rows, :] += jnp.concatenate(
                [_nt(qd[cc * HGRN_CHUNK:(cc + 1) * HGRN_CHUNK], st_ref[i * CPB + cc].astype(BF16)) for cc in range(CPB)], axis=0)
            return carry

        _block_loop(T, inter, 0)

    (o,), landed = _pallas(
        body, name="hgrn_fwd", grid=(N_HEADS, 2), in_specs=[l_spec, col(COL_Q), f_spec, col(COL_V)],
        out_specs=[pl.BlockSpec((T, HEAD_DIM), lambda h, d: (0, h))], out_shape=[jax.ShapeDtypeStruct((T, N_HEADS * HEAD_DIM), F32)],
        scratch=[pltpu.VMEM((NC, HEAD_DIM, HEAD_DIM), F32), pltpu.VMEM((NC, 8, HEAD_DIM), F32), pltpu.VMEM((T, HEAD_DIM), BF16)],
        semantics=("parallel", "arbitrary"), operands=(lb_logits, proj, proj, proj), comm=comm)
    return o if comm is None else (o, landed)


def hgrn_post_fwd(o, proj, g_norm):
    T, W = o.shape
    tm = min(256, T)

    def body(o_ref, og_ref, g_ref, y_ref):
        g = g_ref[...]
        for h in range(N_HEADS):
            sl = slice(h * HEAD_DIM, (h + 1) * HEAD_DIM)
            x = o_ref[:, sl]
            r = lax.rsqrt(jnp.mean(x * x, axis=-1, keepdims=True) + EPS)
            og = og_ref[:, sl]
            y_ref[:, sl] = ((x * r) * g * (og * jax.nn.sigmoid(og))).astype(BF16)

    return pl.pallas_call(
        body, name="hgrn_post_fwd", out_shape=jax.ShapeDtypeStruct((T, W), BF16), grid=(T // tm,),
        in_specs=[pl.BlockSpec((tm, W), lambda i: (i, 0)), pl.BlockSpec((tm, W), lambda i: (i, COL_OG)), _vec(HEAD_DIM)],
        out_specs=pl.BlockSpec((tm, W), lambda i: (i, 0)), compiler_params=_cp("parallel"),
    )(o, proj, g_norm)


def _gelu(x):
    return 0.5 * x * (1.0 + lax.erf(x * (1.0 / math.sqrt(2.0))))


def _gelu_grad(x):
    return 0.5 * (1.0 + lax.erf(x * (1.0 / math.sqrt(2.0)))) + x * jnp.exp(-0.5 * x * x) * (1.0 / math.sqrt(2.0 * math.pi))


def _sgu_mix(u_ref, v_ref, g_ref, ws_ref, bst_ref):
    W = u_ref.shape[1]
    zu, zv = _gelu(u_ref[...]), _gelu(v_ref[...])
    dv = zv - jnp.mean(zv, axis=-1, keepdims=True)
    rstd = lax.rsqrt(jnp.mean(dv * dv, axis=-1, keepdims=True) + EPS)
    dhat = dv * rstd
    vn = (dhat * g_ref[...]).astype(BF16)
    gw = W // SGU_GROUPS
    vm = [_nn(ws_ref[g].astype(BF16), vn[:, g * gw:(g + 1) * gw]) + bst_ref[:, g:g + 1] for g in range(SGU_GROUPS)]
    return zu, rstd, dhat, vn, jnp.concatenate(vm, axis=1)


def sgu_fwd(proj, g_norm, w_spatial, b_spatial_t):
    T = proj.shape[0]
    W = 1024
    n_chunks = T // SGU_CHUNK

    def body(u_ref, v_ref, g_ref, ws_ref, bst_ref, y_ref):
        zu, _, _, _, vm = _sgu_mix(u_ref, v_ref, g_ref, ws_ref, bst_ref)
        y_ref[...] = (zu * vm).astype(BF16)

    blk = lambda cb: pl.BlockSpec((SGU_CHUNK, W), lambda i: (i, cb))
    return pl.pallas_call(
        body, name="sgu_fwd", out_shape=jax.ShapeDtypeStruct((T, W), BF16), grid=(n_chunks,),
        in_specs=[blk(COL_U), blk(COL_ZV), _vec(W), pl.BlockSpec((SGU_GROUPS, SGU_CHUNK, SGU_CHUNK), lambda i: (0, 0, 0)),
                  pl.BlockSpec((SGU_CHUNK, SGU_GROUPS), lambda i: (0, 0))],
        out_specs=blk(0), compiler_params=_cp("parallel"),
    )(proj, proj, g_norm, w_spatial, b_spatial_t)


def merge_matmul(ya_pre, sgu, w_a, w_b, proj):
    T, K = ya_pre.shape
    N = w_a.shape[1]
    tm, tn = min(512, T), 512
    gpb = 1024 // tn

    def body(a_ref, b_ref, wa_ref, wb_ref, ga_ref, gb_ref, ya_ref, yb_ref, m_ref):
        ya = _nn(a_ref[...], wa_ref[...])
        yb = _nn(b_ref[...], wb_ref[...])
        ya_ref[...] = ya
        yb_ref[...] = yb
        m_ref[...] = (jax.nn.sigmoid(ga_ref[...]) * ya + jax.nn.sigmoid(gb_ref[...]) * yb).astype(BF16)

    lhs = pl.BlockSpec((tm, K), lambda i, j: (i, 0))
    rhs = pl.BlockSpec((K, tn), lambda i, j: (0, j))
    out = pl.BlockSpec((tm, tn), lambda i, j: (i, j))
    return pl.pallas_call(
        body, name="merge_matmul", grid=(T // tm, N // tn),
        out_shape=[jax.ShapeDtypeStruct((T, N), F32), jax.ShapeDtypeStruct((T, N), F32), jax.ShapeDtypeStruct((T, N), BF16)],
        in_specs=[lhs, lhs, rhs, rhs, pl.BlockSpec((tm, tn), lambda i, j: (i, COL_GA * gpb + j)),
                  pl.BlockSpec((tm, tn), lambda i, j: (i, COL_GB * gpb + j))],
        out_specs=[out, out, out], compiler_params=_cp("parallel", "parallel"),
    )(ya_pre, sgu, w_a, w_b, proj, proj)


def out_proj(merged, w_o, h0, gt1, g_post):
    T, D = h0.shape
    tm = min(256, T)

    def body(m_ref, w_ref, h_ref, gt_ref, gp_ref, mo_ref, h1_ref):
        mo = _nn(m_ref[...], w_ref[...])
        mo_ref[...] = mo
        r = lax.rsqrt(jnp.mean(mo * mo, axis=-1, keepdims=True) + EPS)
        h1_ref[...] = h_ref[...] + gt_ref[...] * ((mo * r) * gp_ref[...])

    row = pl.BlockSpec((tm, D), lambda i: (i, 0))
    return pl.pallas_call(
        body, name="out_proj", grid=(T // tm,),
        out_shape=[jax.ShapeDtypeStruct((T, D), F32), jax.ShapeDtypeStruct((T, D), F32)],
        in_specs=[row, pl.BlockSpec((D, D), lambda i: (0, 0)), row, _vec(D), _vec(D)],
        out_specs=[row, row], compiler_params=_cp("parallel"),
    )(merged, w_o, h0, gt1, g_post)


def ff2_loss(hid, w_ff2, h1, tgt, gt2, g_post):
    T, K = hid.shape
    D = w_ff2.shape[1]
    tm, tk = min(256, T), 2048
    nk = K // tk

    def body(a_ref, w_ref, h_ref, t_ref, gt_ref, g_ref, dy_ref, dff_ref, loss_ref, dgt_ref, dg_ref, acc_ref):
        i, k = pl.program_id(0), pl.program_id(1)

        @pl.when(k == 0)
        def _():
            acc_ref[...] = jnp.zeros_like(acc_ref)

        @pl.when((k == 0) & (i == 0))
        def _():
            loss_ref[...] = jnp.zeros_like(loss_ref)
            dgt_ref[...] = jnp.zeros_like(dgt_ref)
            dg_ref[...] = jnp.zeros_like(dg_ref)

        acc_ref[...] += _nn(a_ref[...], w_ref[...])

        @pl.when(k == nk - 1)
        def _():
            ff = acc_ref[...]
            gt, g = gt_ref[...], g_ref[...]
            r = lax.rsqrt(jnp.mean(ff * ff, axis=-1, keepdims=True) + EPS)
            fhat = ff * r
            nf = fhat * g
            err = (h_ref[...] + gt * nf) - t_ref[...]
            loss_ref[...] += jnp.sum(err * err)
            dy = err * (1.0 / D)
            dy_ref[...] = dy
            dgt_ref[...] += _colsum(dy * nf)
            dnf = dy * gt
            dg_ref[...] += _colsum(dnf * fhat)
            u = dnf * g
            dff_ref[...] = (r * (u - fhat * jnp.mean(u * fhat, axis=-1, keepdims=True))).astype(BF16)

    row = pl.BlockSpec((tm, D), lambda i, k: (i, 0))
    vec = pl.BlockSpec((1, D), lambda i, k: (0, 0))
    return pl.pallas_call(
        body, name="ff2_loss", grid=(T // tm, nk),
        out_shape=[jax.ShapeDtypeStruct((T, D), F32), jax.ShapeDtypeStruct((T, D), BF16), jax.ShapeDtypeStruct((8, 128), F32),
                   jax.ShapeDtypeStruct((1, D), F32), jax.ShapeDtypeStruct((1, D), F32)],
        in_specs=[pl.BlockSpec((tm, tk), lambda i, k: (i, k)), pl.BlockSpec((tk, D), lambda i, k: (k, 0)), row, row, vec, vec],
        out_specs=[row, row, pl.BlockSpec((8, 128), lambda i, k: (0, 0)), vec, vec],
        scratch_shapes=[pltpu.VMEM((tm, D), F32)], compiler_params=_cp("arbitrary", "arbitrary"),
    )(hid, w_ff2, h1, tgt, gt2, g_post)


def ff2_bwd(dff, w_ff2, f1):
    T, D = dff.shape
    K = w_ff2.shape[0]
    tm, tn = min(512, T), 2048

    def body(a_ref, w_ref, f_ref, o_ref):
        o_ref[...] = (_nt(a_ref[...], w_ref[...]) * (2.0 * jnp.maximum(f_ref[...], 0.0))).astype(BF16)

    return pl.pallas_call(
        body, name="ff2_bwd", out_shape=jax.ShapeDtypeStruct((T, K), BF16), grid=(K // tn, T // tm),
        in_specs=[pl.BlockSpec((tm, D), lambda j, i: (i, 0)), pl.BlockSpec((tn, D), lambda j, i: (j, 0)),
                  pl.BlockSpec((tm, tn), lambda j, i: (i, j))],
        out_specs=pl.BlockSpec((tm, tn), lambda j, i: (i, j)), compiler_params=_cp("parallel", "parallel"),
    )(dff, w_ff2, f1)


def ffn_norm_bwd(dy, da2, h1, mo, g_pre2, sc2, gt1, g_post):
    T, D = dy.shape
    tm = min(256, T)

    def body(dy_ref, da_ref, h_ref, mo_ref, g2_ref, sc_ref, gt_ref, gp_ref, dh_ref, dmo_ref, s_sh, s_sc, s_g2, s_gt, s_gp):
        @pl.when(pl.program_id(0) == 0)
        def _():
            for s in (s_sh, s_sc, s_g2, s_gt, s_gp):
                s[...] = jnp.zeros_like(s)

        h1, da = h_ref[...], da_ref[...]
        g2, sc = g2_ref[...], sc_ref[...]
        r2 = lax.rsqrt(jnp.mean(h1 * h1, axis=-1, keepdims=True) + EPS)
        n2 = h1 * r2
        s_sh[...] += _colsum(da)
        s_sc[...] += _colsum(da * (n2 * g2))
        s_g2[...] += _colsum(da * (1.0 + sc) * n2)
        dn2 = da * g2 * (1.0 + sc)
        dh1 = dy_ref[...] + r2 * (dn2 - n2 * jnp.mean(dn2 * n2, axis=-1, keepdims=True))
        dh_ref[...] = dh1
        mo = mo_ref[...]
        gt, gp = gt_ref[...], gp_ref[...]
        r = lax.rsqrt(jnp.mean(mo * mo, axis=-1, keepdims=True) + EPS)
        mhat = mo * r
        s_gt[...] += _colsum(dh1 * (mhat * gp))
        dnm = dh1 * gt
        s_gp[...] += _colsum(dnm * mhat)
        u = dnm * gp
        dmo_ref[...] = (r * (u - mhat * jnp.mean(u * mhat, axis=-1, keepdims=True))).astype(BF16)

    row = pl.BlockSpec((tm, D), lambda i: (i, 0))
    vec_out = jax.ShapeDtypeStruct((1, D), F32)
    return pl.pallas_call(
        body, name="ffn_norm_bwd", grid=(T // tm,),
        out_shape=[jax.ShapeDtypeStruct((T, D), F32), jax.ShapeDtypeStruct((T, D), BF16)] + [vec_out] * 5,
        in_specs=[row, row, row, row] + [_vec(D)] * 4, out_specs=[row, row] + [_vec(D)] * 5,
        compiler_params=_cp("arbitrary"),
    )(dy, da2, h1, mo, g_pre2, sc2, gt1, g_post)


def out_proj_bwd(dmo, w_o, y_a, y_b, proj):
    T, D = dmo.shape
    tm, tn = min(512, T), 512
    gpb = 1024 // tn

    def body(a_ref, w_ref, ya_ref, yb_ref, ga_ref, gb_ref, dya_ref, dyb_ref, dga_ref, dgb_ref):
        dm = _nt(a_ref[...], w_ref[...])
        sa, sb = jax.nn.sigmoid(ga_ref[...]), jax.nn.sigmoid(gb_ref[...])
        dya_ref[...] = (dm * sa).astype(BF16)
        dyb_ref[...] = (dm * sb).astype(BF16)
        dga_ref[...] = (dm * ya_ref[...] * sa * (1.0 - sa)).astype(BF16)
        dgb_ref[...] = (dm * yb_ref[...] * sb * (1.0 - sb)).astype(BF16)

    out = pl.BlockSpec((tm, tn), lambda i, j: (i, j))
    return pl.pallas_call(
        body, name="out_proj_bwd", grid=(T // tm, D // tn), out_shape=[jax.ShapeDtypeStruct((T, D), BF16)] * 4,
        in_specs=[pl.BlockSpec((tm, D), lambda i, j: (i, 0)), pl.BlockSpec((tn, D), lambda i, j: (j, 0)), out, out,
                  pl.BlockSpec((tm, tn), lambda i, j: (i, COL_GA * gpb + j)), pl.BlockSpec((tm, tn), lambda i, j: (i, COL_GB * gpb + j))],
        out_specs=[out] * 4, compiler_params=_cp("parallel", "parallel"),
    )(dmo, w_o, y_a, y_b, proj, proj)


def sgu_bwd(proj, dsgu, g_norm, w_spatial, b_spatial_t):
    T = proj.shape[0]
    W = 1024
    gw = W // SGU_GROUPS

    def body(u_ref, v_ref, ds_ref, g_ref, ws_ref, bst_ref, dz_ref, dw_ref, db_ref, dg_ref):
        @pl.when(pl.program_id(0) == 0)
        def _():
            dw_ref[...] = jnp.zeros_like(dw_ref)
            db_ref[...] = jnp.zeros_like(db_ref)
            dg_ref[...] = jnp.zeros_like(dg_ref)

        zu, rstd, dhat, vn, vm = _sgu_mix(u_ref, v_ref, g_ref, ws_ref, bst_ref)
        ds = ds_ref[...]
        du = ds * vm
        dvm = ds * zu
        dvm_b = dvm.astype(BF16)
        ones = jnp.ones((8, gw), F32)
        dvn = []
        for g in range(SGU_GROUPS):
            sl = slice(g * gw, (g + 1) * gw)
            dw_ref[g] += _nt(dvm_b[:, sl], vn[:, sl])
            db_ref[g] += lax.dot_general(ones, dvm[:, sl], (((1,), (1,)), ((), ())), precision=HI, preferred_element_type=F32)
            dvn.append(_tn(ws_ref[g].astype(BF16), dvm_b[:, sl]))
        dvn = jnp.concatenate(dvn, axis=1)
        dg_ref[...] += _colsum(dvn * dhat)
        ddh = dvn * g_ref[...]
        dzv = rstd * (ddh - jnp.mean(ddh, axis=-1, keepdims=True) - dhat * jnp.mean(ddh * dhat, axis=-1, keepdims=True))
        dz_ref[:, 0:W] = (du * _gelu_grad(u_ref[...])).astype(BF16)
        dz_ref[:, W:2 * W] = (dzv * _gelu_grad(v_ref[...])).astype(BF16)

    blk = lambda cb: pl.BlockSpec((SGU_CHUNK, W), lambda i: (i, cb))
    full3 = lambda a, b, c: pl.BlockSpec((a, b, c), lambda i: (0, 0, 0))
    return pl.pallas_call(
        body, name="sgu_bwd", grid=(T // SGU_CHUNK,),
        out_shape=[jax.ShapeDtypeStruct((T, 2 * W), BF16), jax.ShapeDtypeStruct((SGU_GROUPS, SGU_CHUNK, SGU_CHUNK), F32),
                   jax.ShapeDtypeStruct((SGU_GROUPS, 8, SGU_CHUNK), F32), jax.ShapeDtypeStruct((1, W), F32)],
        in_specs=[blk(COL_U), blk(COL_ZV), blk(0), _vec(W), full3(SGU_GROUPS, SGU_CHUNK, SGU_CHUNK),
                  pl.BlockSpec((SGU_CHUNK, SGU_GROUPS), lambda i: (0, 0))],
        out_specs=[pl.BlockSpec((SGU_CHUNK, 2 * W), lambda i: (i, 0)), full3(SGU_GROUPS, SGU_CHUNK, SGU_CHUNK),
                   full3(SGU_GROUPS, 8, SGU_CHUNK), _vec(W)],
        compiler_params=_cp("arbitrary"),
    )(proj, proj, dsgu, g_norm, w_spatial, b_spatial_t)


def hgrn_post_bwd(dya, o, proj, g_norm):
    T, W = o.shape
    tm = min(256, T)

    def body(dy_ref, o_ref, og_ref, g_ref, do_ref, dog_ref, dg_ref):
        @pl.when(pl.program_id(0) == 0)
        def _():
            dg_ref[...] = jnp.zeros_like(dg_ref)

        g = g_ref[...]
        dg = jnp.zeros((1, HEAD_DIM), F32)
        for h in range(N_HEADS):
            sl = slice(h * HEAD_DIM, (h + 1) * HEAD_DIM)
            x, og, dy = o_ref[:, sl], og_ref[:, sl], dy_ref[:, sl]
            r = lax.rsqrt(jnp.mean(x * x, axis=-1, keepdims=True) + EPS)
            xhat = x * r
            s = jax.nn.sigmoid(og)
            don = dy * (og * s)
            dog_ref[:, sl] = (dy * (xhat * g) * (s * (1.0 + og * (1.0 - s)))).astype(BF16)
            dg += _colsum(don * xhat)
            u = don * g
            do_ref[:, sl] = r * (u - xhat * jnp.mean(u * xhat, axis=-1, keepdims=True))
        dg_ref[...] += dg

    row = pl.BlockSpec((tm, W), lambda i: (i, 0))
    return pl.pallas_call(
        body, name="hgrn_post_bwd", grid=(T // tm,),
        out_shape=[jax.ShapeDtypeStruct((T, W), F32), jax.ShapeDtypeStruct((T, W), BF16), jax.ShapeDtypeStruct((1, HEAD_DIM), F32)],
        in_specs=[row, row, pl.BlockSpec((tm, W), lambda i: (i, COL_OG)), _vec(HEAD_DIM)],
        out_specs=[row, row, _vec(HEAD_DIM)], compiler_params=_cp("arbitrary"),
    )(dya, o, proj, g_norm)


def hgrn_bwd(proj, do, lb_logits, comm=None):
    T = proj.shape[0]
    NC, CPB = T // HGRN_CHUNK, HGRN_BLOCK // HGRN_CHUNK
    W = N_HEADS * HEAD_DIM
    col, f_spec, l_spec = _hgrn_specs(T)

    def body(l_ref, q_ref, f_ref, v_ref, do_ref, dq_ref, dv_ref, dlg_ref, dlb_ref, st_ref, dst_ref, dec_ref, ddec_ref, dqa_ref, dva_ref):
        d = pl.program_id(1)
        lb = _hgrn_lower_bound(l_ref)
        oml = 1.0 - lb
        mask = _hgrn_chunk_mask(d)

        def values(rows):
            s, sn, fg, lf, k = _hgrn_gate(f_ref[rows, :], lb)
            b = _chunk_cumsum(lf, d == 1)
            bl = _chunk_total(lf)
            eb, enb, ee = jnp.exp(b), jnp.exp(-b), jnp.exp(bl - b)
            qd = q_ref[rows, :] * Q_SCALE * eb
            return s, sn, fg, k, bl, eb, enb, ee, qd, k * enb, k * ee

        def block1(i, carry):
            rows = pl.ds(pl.multiple_of(i * HGRN_BLOCK, HGRN_BLOCK), HGRN_BLOCK)
            _, _, _, _, bl, _, _, _, qd, _, ke = values(rows)
            qd, ke = qd.astype(BF16), ke.astype(BF16)
            vb, dob = v_ref[rows, :].astype(BF16), do_ref[rows, :].astype(BF16)
            dec = jnp.exp(bl)
            for cc in range(CPB):
                sl = slice(cc * HGRN_CHUNK, (cc + 1) * HGRN_CHUNK)
                n = i * CPB + cc
                st_ref[n] = _tn(vb[sl], ke[sl])
                dst_ref[n] = _tn(dob[sl], qd[sl])
                dec_ref[n] = dec[cc * HGRN_CHUNK:cc * HGRN_CHUNK + 8, :]
            return carry

        _block_loop(T, block1, 0)

        def scan(t, s):
            n = jnp.where(d == 0, t, NC - 1 - t)
            u = st_ref[n]
            st_ref[n] = s
            return dec_ref[n][0:1, :] * s + u

        lax.fori_loop(0, NC, scan, jnp.zeros((HEAD_DIM, HEAD_DIM), F32))

        def rscan(t, ds):
            n = jnp.where(d == 0, NC - 1 - t, t)
            w = dst_ref[n]
            dst_ref[n] = ds
            ddec_ref[n] = jnp.broadcast_to(_colsum(ds * st_ref[n]), (8, HEAD_DIM))
            return dec_ref[n][0:1, :] * ds + w

        lax.fori_loop(0, NC, rscan, jnp.zeros((HEAD_DIM, HEAD_DIM), F32))

        def block3(i, dlb):
            rows = pl.ds(pl.multiple_of(i * HGRN_BLOCK, HGRN_BLOCK), HGRN_BLOCK)
            s, sn, fg, k, bl, eb, enb, ee, qd, kd, ke = values(rows)
            qdb, kdb, keb = qd.astype(BF16), kd.astype(BF16), ke.astype(BF16)
            vb, dob = v_ref[rows, :].astype(BF16), do_ref[rows, :].astype(BF16)
            att = jnp.where(mask, _nt(qdb, kdb), 0.0).astype(BF16)
            datt = jnp.where(mask, _nt(dob, vb), 0.0).astype(BF16)
            dv = _tn(att, dob)
            dqd = _nn(datt, kdb)
            dkd = _tn(datt, qdb)
            dv_i, dqd_i, dke, ddl = [], [], [], []
            for cc in range(CPB):
                sl = slice(cc * HGRN_CHUNK, (cc + 1) * HGRN_CHUNK)
                n = i * CPB + cc
                st_b, dst_b = st_ref[n].astype(BF16), dst_ref[n].astype(BF16)
                dv_i.append(_nt(keb[sl], dst_b))
                dqd_i.append(_nn(dob[sl], st_b))
                dke.append(_nn(vb[sl], dst_b))
                ddl.append(jnp.broadcast_to(ddec_ref[n][0:1, :] * dec_ref[n][0:1, :], (HGRN_CHUNK, HEAD_DIM)))
            dv = dv + jnp.concatenate(dv_i, axis=0)
            dqd = dqd + jnp.concatenate(dqd_i, axis=0)
            dke = jnp.concatenate(dke, axis=0)
            dq = dqd * eb * Q_SCALE
            dk = dkd * enb + dke * ee
            t_end = dke * ke
            db = dqd * qd - dkd * kd - t_end
            dlf = _chunk_cumsum(db, d == 0) + _chunk_total(t_end) + jnp.concatenate(ddl, axis=0)
            e = dlf / fg - dk
            dlg_ref[rows, :] = (oml * e * s * sn).astype(BF16)

            dq = jnp.where(d == 0, 0.0, dqa_ref[rows, :]) + dq
            dv = jnp.where(d == 0, 0.0, dva_ref[rows, :]) + dv
            dqa_ref[rows, :] = dq
            dva_ref[rows, :] = dv
            dq_ref[rows, :] = dq.astype(BF16)
            dv_ref[rows, :] = dv.astype(BF16)

            return dlb + _colsum(e * sn)

        dlb_ref[...] = _block_loop(T, block3, jnp.zeros((1, HEAD_DIM), F32))

    head = pl.BlockSpec((T, HEAD_DIM), lambda h, d: (0, h))
    big = pltpu.VMEM((NC, HEAD_DIM, HEAD_DIM), F32)
    small = pltpu.VMEM((NC, 8, HEAD_DIM), F32)
    acc = pltpu.VMEM((T, HEAD_DIM), F32)
    outs, landed = _pallas(
        body, name="hgrn_bwd", grid=(N_HEADS, 2),
        out_shape=[jax.ShapeDtypeStruct((T, W), BF16), jax.ShapeDtypeStruct((T, W), BF16), jax.ShapeDtypeStruct((T, 2 * W), BF16),
                   jax.ShapeDtypeStruct((2, 1, W), F32)],
        in_specs=[l_spec, col(COL_Q), f_spec, col(COL_V), head],
        out_specs=[head, head, pl.BlockSpec((T, HEAD_DIM), lambda h, d: (0, N_HEADS * d + h)),
                   pl.BlockSpec((None, 1, HEAD_DIM), lambda h, d: (d, 0, h))],
        scratch=[big, big, small, small, acc, acc], semantics=("parallel", "arbitrary"), operands=(lb_logits, proj, proj, proj, do), comm=comm)
    return outs if comm is None else (outs, landed)


def mix_norm_bwd(da1, h0, dh1, g_pre, sc1):
    T, D = h0.shape
    tm = min(256, T)

    def body(da_ref, h_ref, dh_ref, g_ref, sc_ref, gx_ref, s_sh, s_sc, s_g):
        @pl.when(pl.program_id(0) == 0)
        def _():
            for s in (s_sh, s_sc, s_g):
                s[...] = jnp.zeros_like(s)

        h, da = h_ref[...], da_ref[...]
        g, sc = g_ref[...], sc_ref[...]
        r = lax.rsqrt(jnp.mean(h * h, axis=-1, keepdims=True) + EPS)
        n = h * r
        s_sh[...] += _colsum(da)
        s_sc[...] += _colsum(da * (n * g))
        s_g[...] += _colsum(da * (1.0 + sc) * n)
        dn = da * g * (1.0 + sc)
        gx_ref[...] = dh_ref[...] + r * (dn - n * jnp.mean(dn * n, axis=-1, keepdims=True))

    row = pl.BlockSpec((tm, D), lambda i: (i, 0))
    return pl.pallas_call(
        body, name="mix_norm_bwd", grid=(T // tm,),
        out_shape=[jax.ShapeDtypeStruct((T, D), F32)] + [jax.ShapeDtypeStruct((1, D), F32)] * 3,
        in_specs=[row, row, row, _vec(D), _vec(D)], out_specs=[row] + [_vec(D)] * 3, compiler_params=_cp("arbitrary"),
    )(da1, h0, dh1, g_pre, sc1)


def adamw(w, g, m, v, name):
    R, C = w.shape
    tr = R if R * C * 4 <= (1 << 21) else max(8, ((1 << 21) // (C * 4)) // 8 * 8)
    while R % tr:
        tr -= 8

    def body(w_ref, g_ref, m_ref, v_ref, d_ref, m2_ref, v2_ref):
        d_ref[...], m2_ref[...], v2_ref[...] = _adamw(w_ref[...], g_ref[...], m_ref[...], v_ref[...])

    row = pl.BlockSpec((tr, C), lambda i: (i, 0))
    return pl.pallas_call(
        body, name=name, grid=(R // tr,), out_shape=[jax.ShapeDtypeStruct((R, C), F32)] * 3,
        in_specs=[row] * 4, out_specs=[row] * 3, compiler_params=_cp("parallel"),
    )(w, g, m, v)


def wada_update(c_all, dmod, w, m, v):
    D, N = w.shape
    tm, tn = 512, 1024

    def body(c_ref, dm_ref, w_ref, m_ref, v_ref, g_ref, d_ref, m2_ref, v2_ref):
        c = c_ref[...]
        g = lax.dot_general(c * jax.nn.sigmoid(c), dm_ref[...], (((0,), (0,)), ((), ())), precision=HI, preferred_element_type=F32)
        g_ref[...] = g
        d_ref[...], m2_ref[...], v2_ref[...] = _adamw(w_ref[...], g, m_ref[...], v_ref[...])

    blk = pl.BlockSpec((tm, tn), lambda i, j: (i, j))
    return pl.pallas_call(
        body, name="wada_update", grid=(D // tm, N // tn), out_shape=[jax.ShapeDtypeStruct((D, N), F32)] * 4,
        in_specs=[pl.BlockSpec((8, tm), lambda i, j: (0, i)), pl.BlockSpec((8, tn), lambda i, j: (0, j)), blk, blk, blk],
        out_specs=[blk] * 4, compiler_params=_cp("parallel", "parallel"),
    )(c_all, dmod, w, m, v)


def sum_devices(gathered, name):
    n, R, C = gathered.shape

    def body(g_ref, o_ref):
        s = g_ref[0]
        for i in range(1, n):
            s = s + g_ref[i]
        o_ref[...] = s

    return pl.pallas_call(body, name=name, out_shape=jax.ShapeDtypeStruct((R, C), F32), compiler_params=_cp())(gathered)


def lb_logits_grad(dlb, lb_logits):
    def body(d_ref, l_ref, o_ref):
        for d in range(2):
            l0, l1 = l_ref[d, 0:1, :], l_ref[d, 1:2, :]
            m = jnp.maximum(l0, l1)
            e0, e1 = jnp.exp(l0 - m), jnp.exp(l1 - m)
            p0, p1 = e0 / (e0 + e1), e1 / (e0 + e1)
            g = d_ref[d:d + 1, :]
            o_ref[d, 0:1, :] = p0 * (g - p0 * g)
            o_ref[d, 1:2, :] = -p1 * (p0 * g)

    return pl.pallas_call(body, name="lb_logits_grad", out_shape=jax.ShapeDtypeStruct(lb_logits.shape, F32), compiler_params=_cp())(dlb, lb_logits)


def add_halves(g, landed, core):
    nj, _, r, cc = g.shape
    tr = min(256, r)

    def body(core_ref, g_ref, l_ref, o_ref):
        o_ref[...] = (g_ref[...].astype(F32) + l_ref[...].astype(F32)).astype(BF16)

    return pl.pallas_call(
        body, name="add_halves_%dx%d" % (r, cc), out_shape=jax.ShapeDtypeStruct((nj, r, cc), BF16),
        grid_spec=pltpu.PrefetchScalarGridSpec(
            num_scalar_prefetch=1, grid=(nj, r // tr),
            in_specs=[pl.BlockSpec((None, None, tr, cc), lambda j, i, core_ref: (j, core_ref[0], i, 0)),
                      pl.BlockSpec((None, None, tr, cc), lambda j, i, core_ref: (j, 0, i, 0))],
            out_specs=pl.BlockSpec((None, tr, cc), lambda j, i, core_ref: (j, i, 0))),
        compiler_params=_cp("parallel", "parallel"),
    )(core, g, landed)


def sum_chips(parts, landed, chip):
    nj, r, cc = parts.shape
    tr = min(256, r)

    def body(chip_ref, p_ref, l_ref, o_ref):
        mine = p_ref[...].astype(F32)
        s = None
        for j in range(nj):
            t = jnp.where(chip_ref[0] == j, mine, l_ref[j].astype(F32))
            s = t if s is None else s + t
        o_ref[...] = s

    return pl.pallas_call(
        body, name="sum_chips_%dx%d" % (r, cc), out_shape=jax.ShapeDtypeStruct((r, cc), F32),
        grid_spec=pltpu.PrefetchScalarGridSpec(
            num_scalar_prefetch=1, grid=(r // tr,),
            in_specs=[pl.BlockSpec((None, tr, cc), lambda i, chip_ref: (chip_ref[0], i, 0)),
                      pl.BlockSpec((nj, tr, cc), lambda i, chip_ref: (0, i, 0))],
            out_specs=pl.BlockSpec((tr, cc), lambda i, chip_ref: (i, 0))),
        compiler_params=_cp("parallel"),
    )(chip, parts, landed)


def adamw_halves(w, own, other, m, v, core, name):
    r, cc = own.shape
    tr = min(128, r)
    nb = r // tr

    def body(core_ref, w_ref, a_ref, b_ref, m_ref, v_ref, g_ref, d_ref, m2_ref, v2_ref):
        g = jnp.where(pl.program_id(0) == core_ref[0], a_ref[...], b_ref[...])
        g_ref[...] = g
        d_ref[...], m2_ref[...], v2_ref[...] = _adamw(w_ref[...], g, m_ref[...], v_ref[...])

    full = pl.BlockSpec((tr, cc), lambda h, i, core_ref: (h * nb + i, 0))
    half = pl.BlockSpec((tr, cc), lambda h, i, core_ref: (i, 0))
    return pl.pallas_call(
        body, name=name, out_shape=[jax.ShapeDtypeStruct((2 * r, cc), F32)] * 4,
        grid_spec=pltpu.PrefetchScalarGridSpec(
            num_scalar_prefetch=1, grid=(2, nb), in_specs=[full, half, half, full, full], out_specs=[full] * 4),
        compiler_params=_cp("parallel", "parallel"),
    )(core, w, own, other, m, v)


def _place():
    mx, my, mc = lax.axis_index("x"), lax.axis_index("y"), lax.axis_index("c")
    chips = [(1 - mx, my), (mx, 1 - my), (1 - mx, 1 - my)]
    return mx, my, mc, chips


def all_gather_small(x, name):
    R, C = x.shape

    def body(x_ref, out_ref, send_sems, recv_sems, local_sem):
        mx, my, mc, _ = _place()
        me = 4 * mx + 2 * my + mc
        mine = pltpu.make_async_copy(x_ref, out_ref.at[me], local_sem)
        mine.start()

        def peer(k):
            px = 1 - mx if k & 4 else mx
            py = 1 - my if k & 2 else my
            pc = 1 - mc if k & 1 else mc
            return px, py, pc

        def copy(k, src, slot):
            return pltpu.make_async_remote_copy(src_ref=src, dst_ref=out_ref.at[slot], send_sem=send_sems.at[k - 1],
                                                recv_sem=recv_sems.at[k - 1], device_id=peer(k), device_id_type=MESH)

        sends = [copy(k, x_ref, me) for k in range(1, 8)]
        for cp in sends:
            cp.start()
        for k in range(1, 8):
            px, py, pc = peer(k)
            slot = 4 * px + 2 * py + pc
            copy(k, out_ref.at[slot], slot).wait_recv()
        for cp in sends:
            cp.wait_send()
        mine.wait()

    return pl.pallas_call(
        body, name=name, out_shape=jax.ShapeDtypeStruct((8, R, C), F32),
        in_specs=[pl.BlockSpec(memory_space=pltpu.VMEM)], out_specs=pl.BlockSpec(memory_space=pltpu.VMEM),
        scratch_shapes=[pltpu.SemaphoreType.DMA((7,)), pltpu.SemaphoreType.DMA((7,)), pltpu.SemaphoreType.DMA],
        compiler_params=_cp(),
    )(x)


def gather8_comm(x):
    def copies(x_ref, out_ref, send_sems, recv_sems):
        mx, my, mc, _ = _place()
        me = 4 * mx + 2 * my + mc

        def peer(k):
            return (1 - mx if k & 4 else mx, 1 - my if k & 2 else my, 1 - mc if k & 1 else mc)

        def copy(k, src, slot):
            return pltpu.make_async_remote_copy(src_ref=src, dst_ref=out_ref.at[slot], send_sem=send_sems.at[k - 1],
                                                recv_sem=recv_sems.at[k - 1], device_id=peer(k), device_id_type=MESH)

        sends = [copy(k, x_ref, me) for k in range(1, 8)]
        arrivals = []
        for k in range(1, 8):
            px, py, pc = peer(k)
            slot = 4 * px + 2 * py + pc
            arrivals.append(copy(k, out_ref.at[slot], slot))
        return sends, arrivals, pltpu.make_async_copy(x_ref, out_ref.at[me], send_sems.at[7])

    def start(cin, cout, send_sems, recv_sems):
        sends, _, mine = copies(cin[0], cout[0], send_sems, recv_sems)
        mine.start()
        for cp in sends:
            cp.start()

    def finish(cin, cout, send_sems, recv_sems):
        sends, arrivals, mine = copies(cin[0], cout[0], send_sems, recv_sems)
        for cp in arrivals:
            cp.wait_recv()
        for cp in sends:
            cp.wait_send()
        mine.wait()

    return _Comm([x], [jax.ShapeDtypeStruct((8,) + x.shape, F32)], {}, 8, start, finish)


def _join(a, b):
    na_in, na_out = len(a.operands), len(a.out_shape)

    def split(fn_a, fn_b):
        def both(cin, cout, send_sems, recv_sems):
            fn_a(cin[:na_in], cout[:na_out], send_sems.at[pl.ds(0, a.n_sems)], recv_sems.at[pl.ds(0, a.n_sems)])
            fn_b(cin[na_in:], cout[na_out:], send_sems.at[pl.ds(a.n_sems, b.n_sems)], recv_sems.at[pl.ds(a.n_sems, b.n_sems)])
        return both

    aliases = dict(a.aliases)
    aliases.update({na_in + i: na_out + o for i, o in b.aliases.items()})
    return _Comm(a.operands + b.operands, a.out_shape + b.out_shape, aliases, a.n_sems + b.n_sems, split(a.start, b.start), split(a.finish, b.finish))


def _region(ref, kind, j, half, r, cc):
    nr = r if half is None else r // 2
    off = 0 if half is None else half * nr
    if kind == "col":
        return ref.at[pl.ds(off, nr), pl.ds(pl.multiple_of(j * cc, 128), cc)]
    return ref.at[pl.ds(pl.multiple_of(j * r + off, 16), nr), :]


def comm_call(comm, name):
    ni, no = len(comm.operands), len(comm.out_shape)

    def body(*refs):
        comm.start(refs[:ni], refs[ni:ni + no], *refs[ni + no:])
        comm.finish(refs[:ni], refs[ni:ni + no], *refs[ni + no:])

    return pl.pallas_call(
        body, name=name, out_shape=comm.out_shape, in_specs=[ANY] * ni, out_specs=[ANY] * no, input_output_aliases=comm.aliases,
        scratch_shapes=[pltpu.SemaphoreType.DMA((comm.n_sems,)), pltpu.SemaphoreType.DMA((comm.n_sems,))], compiler_params=_cp(),
    )(*comm.operands)


def gather_comm(fulls, kinds, dims):
    n = len(fulls)

    def copies(f_refs, send_sems, recv_sems):
        mx, my, mc, chips = _place()
        jme = 2 * mx + my

        def landed(w, k, half):
            px, py = chips[k]
            return _region(f_refs[w], kinds[w], 2 * px + py, half, *dims[w])

        def over_ici(w, k, reg):
            px, py = chips[k]
            return pltpu.make_async_remote_copy(src_ref=reg, dst_ref=reg, send_sem=send_sems.at[6 * w + k], recv_sem=recv_sems.at[6 * w + k],
                                                device_id=(px, py, mc), device_id_type=MESH)

        def over_d2d(w, k, half):
            reg = landed(w, k, half)
            return pltpu.make_async_remote_copy(src_ref=reg, dst_ref=reg, send_sem=send_sems.at[6 * w + 3 + k],
                                                recv_sem=recv_sems.at[6 * w + 3 + k], device_id=(mx, my, 1 - mc), device_id_type=MESH)

        sends = [over_ici(w, k, _region(f_refs[w], kinds[w], jme, mc, *dims[w])) for w in range(n) for k in range(3)]
        return mc, landed, over_ici, over_d2d, sends

    def start(cin, f_refs, send_sems, recv_sems):
        for cp in copies(f_refs, send_sems, recv_sems)[4]:
            cp.start()

    def finish(cin, f_refs, send_sems, recv_sems):
        mc, landed, over_ici, over_d2d, sends = copies(f_refs, send_sems, recv_sems)
        passed = []
        for w in range(n):
            for k in range(3):
                over_ici(w, k, landed(w, k, mc)).wait_recv()
                cp = over_d2d(w, k, mc)
                cp.start()
                passed.append(cp)
        for w in range(n):
            for k in range(3):
                over_d2d(w, k, 1 - mc).wait_recv()
        for cp in sends + passed:
            cp.wait_send()

    return _Comm(fulls, [jax.ShapeDtypeStruct(f.shape, BF16) for f in fulls], {w: w for w in range(n)}, 6 * n, start, finish)


def exchange_halves(grads, name):
    n = len(grads)

    def body(*refs):
        g_refs, l_refs = refs[:n], refs[n:2 * n]
        send_sems, recv_sems = refs[2 * n:]
        mx, my, mc, _ = _place()
        cps = [pltpu.make_async_remote_copy(src_ref=g_refs[w].at[:, pl.ds(1 - mc, 1)], dst_ref=l_refs[w], send_sem=send_sems.at[w],
                                            recv_sem=recv_sems.at[w], device_id=(mx, my, 1 - mc), device_id_type=MESH) for w in range(n)]
        for cp in cps:
            cp.start()
        for cp in cps:
            cp.wait()

    return pl.pallas_call(
        body, name=name, out_shape=[jax.ShapeDtypeStruct((g.shape[0], 1) + g.shape[2:], BF16) for g in grads],
        in_specs=[ANY] * n, out_specs=[ANY] * n,
        scratch_shapes=[pltpu.SemaphoreType.DMA((n,)), pltpu.SemaphoreType.DMA((n,))], compiler_params=_cp(),
    )(*grads)


def scatter_comm(parts):
    n = len(parts)

    def sends(p_refs, l_refs, send_sems, recv_sems):
        mx, my, mc, chips = _place()
        return [pltpu.make_async_remote_copy(src_ref=p_refs[w].at[2 * px + py], dst_ref=l_refs[w].at[2 * mx + my],
                                             send_sem=send_sems.at[3 * w + k], recv_sem=recv_sems.at[3 * w + k],
                                             device_id=(px, py, mc), device_id_type=MESH) for w in range(n) for k, (px, py) in enumerate(chips)]

    def start(p_refs, l_refs, send_sems, recv_sems):
        for cp in sends(p_refs, l_refs, send_sems, recv_sems):
            cp.start()

    def finish(p_refs, l_refs, send_sems, recv_sems):
        mx, my, mc, chips = _place()
        for w in range(n):
            for k, (px, py) in enumerate(chips):
                slot = l_refs[w].at[2 * px + py]
                pltpu.make_async_remote_copy(src_ref=slot, dst_ref=slot, send_sem=send_sems.at[3 * w + k], recv_sem=recv_sems.at[3 * w + k],
                                             device_id=(px, py, mc), device_id_type=MESH).wait_recv()
        for cp in sends(p_refs, l_refs, send_sems, recv_sems):
            cp.wait_send()

    return _Comm(parts, [jax.ShapeDtypeStruct(p.shape, BF16) for p in parts], {}, 3 * n, start, finish)


def share_comm(sums):
    n = len(sums)

    def copies(q_refs, o_refs, send_sems, recv_sems):
        mx, my, mc, _ = _place()
        return [pltpu.make_async_remote_copy(src_ref=q_refs[w], dst_ref=o_refs[w], send_sem=send_sems.at[w], recv_sem=recv_sems.at[w],
                                             device_id=(mx, my, 1 - mc), device_id_type=MESH) for w in range(n)]

    def start(*refs):
        for cp in copies(*refs):
            cp.start()

    def finish(*refs):
        for cp in copies(*refs):
            cp.wait()

    return _Comm(sums, [jax.ShapeDtypeStruct(q.shape, F32) for q in sums], {}, n, start, finish)


def _pack(arrays):
    flat = jnp.concatenate([a.reshape(-1) for a in arrays])
    rows = -(-flat.shape[0] // 1024) * 8
    return jnp.pad(flat, (0, rows * 128 - flat.shape[0])).reshape(rows, 128)


def _unpack(packed, shapes):
    flat, out, off = packed.reshape(-1), [], 0
    for s in shapes:
        n = math.prod(s)
        out.append(flat[off:off + n].reshape(s))
        off += n
    return out


def kernel(x, c, w_ada, b_ada, g_pre_mix, g_post_mix, g_pre_ffn, g_post_ffn, w_in, lb_logits, g_hgrn_norm, w_a_out, g_sgu_norm, w_spatial, b_spatial, w_b_out, w_o, w_ff1, w_ff2, loss_target, m_w_ada, m_b_ada, m_g_pre_mix, m_g_post_mix, m_g_pre_ffn, m_g_post_ffn, m_w_in, m_lb_logits, m_g_hgrn_norm, m_w_a_out, m_g_sgu_norm, m_w_spatial, m_b_spatial, m_w_b_out, m_w_o, m_w_ff1, m_w_ff2, v_w_ada, v_b_ada, v_g_pre_mix, v_g_post_mix, v_g_pre_ffn, v_g_post_ffn, v_w_in, v_lb_logits, v_g_hgrn_norm, v_w_a_out, v_g_sgu_norm, v_w_spatial, v_b_spatial, v_w_b_out, v_w_o, v_w_ff1, v_w_ff2):
    mx, my, mc = lax.axis_index("x"), lax.axis_index("y"), lax.axis_index("c")
    chip, me = 2 * mx + my, 4 * mx + 2 * my + mc
    D = D_MODEL
    h0, tgt = x[0], loss_target[0]
    n_ada = w_ada.shape[2]
    n_lb = lb_logits.shape[2]

    got = all_gather_small(_pack([c, lb_logits]), "gather_inputs")
    c_all = got[:, :D // 128, :].reshape(8, D)
    lb_full = got[0::2, D // 128:D // 128 + 4 * n_lb // 128, :].reshape(4, 2, 2, n_lb).transpose(1, 2, 0, 3).reshape(2, 2, 4 * n_lb)
    b_ada_chip = lax.dynamic_slice(b_ada, (0, chip * n_ada), (1, n_ada))
    mod_cols = mod_matmul(c_all, w_ada[0], b_ada_chip)
    got = all_gather_small(mod_cols.reshape(-1, 128), "gather_mod").reshape(4, 2, 8, n_ada)
    mod = lax.dynamic_index_in_dim(got[:, 0], me, axis=1, keepdims=False).reshape(6, 1, D)
    sh1, sc1, gt1, sh2, sc2, gt2 = (mod[i] for i in range(6))

    big = [("w_in", w_in, "col"), ("w_a_out", w_a_out, "col"), ("w_b_out", w_b_out, "col"), ("w_o", w_o, "row"),
           ("w_ff1", w_ff1, "col"), ("w_ff2", w_ff2, "row")]
    kinds = [k for _, _, k in big]
    chip_idx, core = chip.reshape(1).astype(jnp.int32), mc.reshape(1).astype(jnp.int32)
    fulls = [cast_into_full(w[0], kind, chip_idx, "cast_" + nm) for nm, w, kind in big]
    dims = [w.shape[1:] for _, w, _ in big]
    later = lambda lo, hi: gather_comm(fulls[lo:hi], kinds[lo:hi], dims[lo:hi])
    halves_summed = lambda grads, name: [add_halves(g, l, core) for g, l in zip(grads, exchange_halves(grads, name))]

    bst = b_spatial[0].T
    a1 = prenorm(h0, g_pre_mix, sc1, sh1)
    proj, w_in_f, (w_a_f, w_b_f, w_o_f) = in_proj_gathered(a1, fulls[0], chip_idx, dims[0], later(1, 4))
    o, (w_ff1_f,) = hgrn_fwd(proj, lb_full, comm=later(4, 5))
    ya_pre = hgrn_post_fwd(o, proj, g_hgrn_norm)
    sgu = sgu_fwd(proj, g_sgu_norm, w_spatial[0], bst)
    y_a, y_b, merged = merge_matmul(ya_pre, sgu, w_a_f, w_b_f, proj)
    mo, h1 = out_proj(merged, w_o_f, h0, gt1, g_post_mix)
    (f1, a2, hid), (w_ff2_f,) = prenorm_matmul(h1, g_pre_ffn, sc2, sh2, w_ff1_f, relu2=True, name="ff1", comm=later(5, 6))
    dy, dff, loss_parts, d_gt2, d_g_post_ffn = ff2_loss(hid, w_ff2_f, h1, tgt, gt2, g_post_ffn)
    loss = lax.psum(0.5 * loss_parts[0, 0] / D, ("x", "y", "c"))

    df1 = ff2_bwd(dff, w_ff2_f, f1)
    gr_ff2 = matmul(hid, dff, mode="tn", out_dtype=BF16, tm=1024, tn=1024, tk=2048, name="dw_ff2")
    da2 = matmul(df1, w_ff1_f, mode="nt", out_dtype=F32, tm=1024, tn=1024, tk=2048, name="da2")
    gr_ff1 = matmul(a2, df1, mode="tn", out_dtype=BF16, tm=1024, tn=2048, tk=1024, name="dw_ff1", split=(4, 2))
    parts_ff = halves_summed([gr_ff1, gr_ff2.reshape(4, 2, -1, D)], "exchange_ff")
    dh1, dmo, d_sh2, d_sc2, d_g_pre_ffn, d_gt1, d_g_post_mix = ffn_norm_bwd(dy, da2, h1, mo, g_pre_ffn, sc2, gt1, g_post_mix)
    dya, dyb, dga, dgb = out_proj_bwd(dmo, w_o_f, y_a, y_b, proj)
    gr_o = matmul(merged, dmo, mode="tn", out_dtype=BF16, tm=1024, tn=1024, tk=2048, name="dw_o")
    dsgu = matmul(dyb, w_b_f, mode="nt", out_dtype=F32, tm=512, tn=1024, tk=2048, name="dsgu")
    gr_b = matmul(sgu, dyb, mode="tn", out_dtype=BF16, tm=512, tn=512, tk=4096, name="dw_b_out", split=(4, 2))
    dz, d_w_spatial, d_b_spatial, d_g_sgu = sgu_bwd(proj, dsgu, g_sgu_norm, w_spatial[0], bst)
    dya_pre = matmul(dya, w_a_f, mode="nt", out_dtype=F32, tm=512, tn=1024, tk=2048, name="dya_pre")
    gr_a = matmul(ya_pre, dya, mode="tn", out_dtype=BF16, tm=512, tn=512, tk=4096, name="dw_a_out", split=(4, 2))
    parts_mix = halves_summed([gr_a, gr_b, gr_o.reshape(4, 2, -1, D)], "exchange_mix")
    do, dog, d_g_hgrn = hgrn_post_bwd(dya_pre, o, proj, g_hgrn_norm)
    chips_summed = lambda parts, landed: [sum_chips(p, l, chip_idx) for p, l in zip(parts, landed)]
    (dq, dv, dlg, d_lb), landed_ff = hgrn_bwd(proj, do, lb_full, comm=scatter_comm(parts_ff))
    own_ff = chips_summed(parts_ff, landed_ff)
    dproj = jnp.concatenate([dq, dlg, dv, dog, dz, dga, dgb], axis=1)
    early = _pack([d_g_sgu, d_w_spatial, d_b_spatial[:, 0, :]])
    gr_in, (*landed_mix, got_early) = matmul(a1, dproj, mode="tn", out_dtype=BF16, tm=1024, tn=2816, tk=1024, name="dw_in", split=(4, 2),
                                             comm=_join(scatter_comm(parts_mix), gather8_comm(early)))
    own_mix = chips_summed(parts_mix, landed_mix)
    parts_in = halves_summed([gr_in], "exchange_in")
    da1, (landed_in, *other_rest) = matmul(dproj, w_in_f, mode="nt", out_dtype=F32, tm=1024, tn=1024, tk=2816, name="da1",
                                           comm=_join(scatter_comm(parts_in), share_comm(own_mix + own_ff)))
    own_in = chips_summed(parts_in, [landed_in])
    other_in = comm_call(share_comm(own_in), "share_w_in")
    own, other = own_in + own_mix + own_ff, list(other_in) + other_rest
    grad_x, d_sh1, d_sc1, d_g_pre_mix = mix_norm_bwd(da1, h0, dh1, g_pre_mix, sc1)
    out = {}

    mine = _pack([d_sh1, d_sc1, d_gt1, d_sh2, d_sc2, d_gt2, d_g_pre_mix, d_g_post_mix, d_g_pre_ffn, d_g_post_ffn, d_g_hgrn, d_lb])
    got = all_gather_small(mine, "gather_small_grads")
    g_b_ada, g_g1, g_g2, g_g3, g_g4, g_hg, g_lb = _unpack(
        sum_devices(got, "sum_small_grads"), [(1, 6 * D), (1, D), (1, D), (1, D), (1, D), (1, HEAD_DIM), (2, 1024)])
    g_sg, g_ws, g_bs = _unpack(sum_devices(got_early, "sum_sgu_grads"), [(1, 1024), w_spatial.shape, b_spatial.shape])
    g_lbl = lax.dynamic_slice(lb_logits_grad(g_lb, lb_full), (0, 0, chip * n_lb), (2, 2, n_lb))
    names = ["b_ada", "g_pre_mix", "g_post_mix", "g_pre_ffn", "g_post_ffn", "g_hgrn_norm", "g_sgu_norm", "w_spatial", "b_spatial", "lb_logits"]
    ws = [b_ada, g_pre_mix, g_post_mix, g_pre_ffn, g_post_ffn, g_hgrn_norm, g_sgu_norm, w_spatial, b_spatial, lb_logits]
    gs = [g_b_ada, g_g1, g_g2, g_g3, g_g4, g_hg, g_sg, g_ws, g_bs, g_lbl]
    ms = [m_b_ada, m_g_pre_mix, m_g_post_mix, m_g_pre_ffn, m_g_post_ffn, m_g_hgrn_norm, m_g_sgu_norm, m_w_spatial, m_b_spatial, m_lb_logits]
    vs = [v_b_ada, v_g_pre_mix, v_g_post_mix, v_g_pre_ffn, v_g_post_ffn, v_g_hgrn_norm, v_g_sgu_norm, v_w_spatial, v_b_spatial, v_lb_logits]
    shapes = [w.shape for w in ws]
    upd = adamw(_pack(ws), _pack(gs), _pack(ms), _pack(vs), "adamw_small")
    upd = [_unpack(u, shapes) for u in upd]
    for i, nm in enumerate(names):
        out[nm] = (gs[i], upd[0][i], upd[1][i], upd[2][i])

    dmod_all = got[:, :6 * D // 128, :].reshape(8, 6 * D)
    dmod_chip = lax.dynamic_slice(dmod_all, (0, chip * n_ada), (8, n_ada))
    out["w_ada"] = tuple(a[None] for a in wada_update(c_all, dmod_chip, w_ada[0], m_w_ada[0], v_w_ada[0]))
    for (nm, w, _), a, b, m, v in zip(big, own, other, (m_w_in, m_w_a_out, m_w_b_out, m_w_o, m_w_ff1, m_w_ff2),
                                      (v_w_in, v_w_a_out, v_w_b_out, v_w_o, v_w_ff1, v_w_ff2)):
        out[nm] = tuple(t[None] for t in adamw_halves(w[0], a, b, m[0], v[0], core, "adamw_" + nm))

    order = ["w_ada", "b_ada", "g_pre_mix", "g_post_mix", "g_pre_ffn", "g_post_ffn", "w_in", "lb_logits", "g_hgrn_norm", "w_a_out",
             "g_sgu_norm", "w_spatial", "b_spatial", "w_b_out", "w_o", "w_ff1", "w_ff2"]
    return (loss, grad_x[None], *[out[nm][0] for nm in order], *[out[nm][1] for nm in order], *[out[nm][2] for nm in order],
            *[out[nm][3] for nm in order])
```

```python
import functools
import math

import jax
import jax.numpy as jnp
from jax import lax
from jax.experimental import pallas as pl
from jax.experimental.pallas import tpu as pltpu

F32, BF16 = jnp.float32, jnp.bfloat16
HI = lax.Precision.HIGHEST
MESH = pl.DeviceIdType.MESH
ANY = pl.BlockSpec(memory_space=pl.ANY)

EPS = 1e-6
D_MODEL = 2048
N_HEADS = 8
HEAD_DIM = 128
HGRN_CHUNK = 32
HGRN_BLOCK = 256
SGU_CHUNK = 128
SGU_GROUPS = 8
Q_SCALE = HEAD_DIM ** -0.5
COL_Q, COL_FFW, COL_FBW, COL_V, COL_OG, COL_U, COL_ZV, COL_GA, COL_GB = 0, 1, 2, 3, 4, 5, 6, 7, 9
N_PROJ = 11264
VMEM_BYTES_V7X = 64 * 1024 * 1024
VMEM_LIMIT = VMEM_BYTES_V7X - 8 * 1024 * 1024

ADAM_LR, ADAM_B1, ADAM_B2, ADAM_EPS, ADAM_WD, ADAM_STEP = 0.001, 0.9, 0.999, 1e-08, 0.01, 10
ADAM_C1 = 1.0 - ADAM_B1 ** ADAM_STEP
ADAM_C2 = 1.0 - ADAM_B2 ** ADAM_STEP


def _cp(*sem):
    return pltpu.CompilerParams(dimension_semantics=sem if sem else None, vmem_limit_bytes=VMEM_LIMIT)


def _vec(d):
    return pl.BlockSpec((1, d), lambda *_: (0, 0))


def _colsum(x):
    return jnp.sum(x, axis=0, keepdims=True)


def _nt(a, b):
    return lax.dot_general(a, b, (((1,), (1,)), ((), ())), preferred_element_type=F32)


def _tn(a, b):
    return lax.dot_general(a, b, (((0,), (0,)), ((), ())), preferred_element_type=F32)


def _nn(a, b):
    return jnp.dot(a, b, preferred_element_type=F32)


def _adamw(w, g, m, v):
    m2 = ADAM_B1 * m + (1.0 - ADAM_B1) * g
    v2 = ADAM_B2 * v + (1.0 - ADAM_B2) * (g * g)
    delta = -ADAM_LR * ((m2 / ADAM_C1) / (jnp.sqrt(v2 / ADAM_C2) + ADAM_EPS) + ADAM_WD * w)
    return delta, m2, v2


class _Comm:
    def __init__(self, operands, out_shape, aliases, n_sems, start, finish):
        self.operands, self.out_shape, self.aliases, self.n_sems = list(operands), list(out_shape), dict(aliases), n_sems
        self.start, self.finish = start, finish


def _pallas(body, *, name, grid, in_specs, out_specs, out_shape, scratch, semantics, operands, comm=None):
    if comm is None:
        res = pl.pallas_call(body, name=name, grid=grid, in_specs=in_specs, out_specs=out_specs, out_shape=out_shape,
                             scratch_shapes=scratch, compiler_params=_cp(*semantics))(*operands)
        return res, []
    n_in, n_out, n_scr = len(in_specs), len(out_specs), len(scratch)
    nci, nco = len(comm.operands), len(comm.out_shape)

    def with_comm(*refs):
        ins, rest = refs[:n_in], refs[n_in:]
        cin, rest = rest[:nci], rest[nci:]
        outs, rest = rest[:n_out], rest[n_out:]
        cout, rest = rest[:nco], rest[nco:]
        scr, (send, recv) = rest[:n_scr], rest[n_scr:]
        ids = [pl.program_id(a) for a in range(len(grid))]
        first = functools.reduce(jnp.logical_and, [i == 0 for i in ids])
        last = functools.reduce(jnp.logical_and, [i == g - 1 for i, g in zip(ids, grid)])

        @pl.when(first)
        def _():
            comm.start(cin, cout, send, recv)

        body(*ins, *outs, *scr)

        @pl.when(last)
        def _():
            comm.finish(cin, cout, send, recv)

    res = pl.pallas_call(
        with_comm, name=name, grid=grid, in_specs=list(in_specs) + [ANY] * nci, out_specs=list(out_specs) + [ANY] * nco,
        out_shape=list(out_shape) + comm.out_shape, input_output_aliases={n_in + i: n_out + o for i, o in comm.aliases.items()},
        scratch_shapes=list(scratch) + [pltpu.SemaphoreType.DMA((comm.n_sems,)), pltpu.SemaphoreType.DMA((comm.n_sems,))],
        compiler_params=_cp(*["arbitrary"] * len(grid)),
    )(*operands, *comm.operands)
    return res[:n_out], res[n_out:]


def matmul(a, b, *, mode, out_dtype, tm, tn, tk, name, split=None, comm=None, b_stacked=False):
    if mode == "tn":
        (K, M), (_, N) = a.shape, b.shape
    elif mode == "nt":
        (M, K), (N, _) = a.shape, b.shape
        N = N // 4 if b_stacked else N
    else:
        (M, K), (_, N) = a.shape, b.shape
    tm, tn, tk = min(tm, M), min(tn, N), min(tk, K)
    if b_stacked:
        tk = K // 4
    nk = K // tk
    a_spec = pl.BlockSpec((tk, tm), lambda i, j, k: (k, i)) if mode == "tn" else pl.BlockSpec((tm, tk), lambda i, j, k: (i, k))
    b_spec = pl.BlockSpec((tn, tk), lambda i, j, k: (j, k)) if mode == "nt" else pl.BlockSpec((tk, tn), lambda i, j, k: (k, j))
    if b_stacked:
        b_spec = pl.BlockSpec((tn, tk), lambda i, j, k: (k * (N // tn) + j, 0))
    dot = {"nn": _nn, "nt": _nt, "tn": _tn}[mode]
    if split is None:
        out_shape = jax.ShapeDtypeStruct((M, N), out_dtype)
        out_spec = pl.BlockSpec((tm, tn), lambda i, j, k: (i, j))
    else:
        nj, nh = split
        rows, cols = M // nh, N // nj
        tm, tn = min(tm, rows), min(tn, cols)
        bi, bj = rows // tm, cols // tn
        out_shape = jax.ShapeDtypeStruct((nj, nh, rows, cols), out_dtype)
        out_spec = pl.BlockSpec((None, None, tm, tn), lambda i, j, k: (j // bj, i // bi, i % bi, j % bj))

    def body(a_ref, b_ref, o_ref, acc_ref):
        k = pl.program_id(2)

        @pl.when(k == 0)
        def _():
            acc_ref[...] = jnp.zeros_like(acc_ref)

        acc_ref[...] += dot(a_ref[...], b_ref[...])

        @pl.when(k == nk - 1)
        def _():
            o_ref[...] = acc_ref[...].astype(o_ref.dtype)

    (out,), landed = _pallas(
        body, name=name, grid=(M // tm, N // tn, nk), in_specs=[a_spec, b_spec], out_specs=[out_spec], out_shape=[out_shape],
        scratch=[pltpu.VMEM((tm, tn), F32)], semantics=("parallel", "parallel", "arbitrary"), operands=(a, b), comm=comm)
    return out if comm is None else (out, landed)


def cast_into_full(w, kind, chip, name):
    r, cc = w.shape
    tr = min(r, 512)
    nb = r // tr

    def body(chip_ref, w_ref, o_ref):
        o_ref[...] = w_ref[...].astype(BF16)

    if kind == "col":
        full, out_map = (r, 4 * cc), lambda i, chip_ref: (i, chip_ref[0])
    else:
        full, out_map = (4 * r, cc), lambda i, chip_ref: (chip_ref[0] * nb + i, 0)
    return pl.pallas_call(
        body, name=name, out_shape=jax.ShapeDtypeStruct(full, BF16),
        grid_spec=pltpu.PrefetchScalarGridSpec(
            num_scalar_prefetch=1, grid=(nb,), in_specs=[pl.BlockSpec((tr, cc), lambda i, chip_ref: (i, 0))],
            out_specs=pl.BlockSpec((tr, cc), out_map)),
        compiler_params=_cp("parallel"),
    )(chip, w)


def mod_matmul(c_all, w_ada, b_ada):
    D, N = w_ada.shape
    tn = 1024

    def body(c_ref, w_ref, b_ref, o_ref):
        c = c_ref[...]
        sc = c * jax.nn.sigmoid(c)
        o_ref[...] = jnp.dot(sc, w_ref[...], precision=HI, preferred_element_type=F32) + b_ref[...]

    return pl.pallas_call(
        body, name="mod_matmul", out_shape=jax.ShapeDtypeStruct((8, N), F32), grid=(N // tn,),
        in_specs=[pl.BlockSpec((8, D), lambda j: (0, 0)), pl.BlockSpec((D, tn), lambda j: (0, j)),
                  pl.BlockSpec((1, tn), lambda j: (0, j))],
        out_specs=pl.BlockSpec((8, tn), lambda j: (0, j)), compiler_params=_cp("parallel"),
    )(c_all, w_ada, b_ada)


def prenorm(h, g, sc, sh):
    T, D = h.shape
    tm = min(256, T)

    def body(h_ref, g_ref, sc_ref, sh_ref, a_ref):
        x = h_ref[...]
        r = lax.rsqrt(jnp.mean(x * x, axis=-1, keepdims=True) + EPS)
        a_ref[...] = ((x * r) * g_ref[...] * (1.0 + sc_ref[...]) + sh_ref[...]).astype(BF16)

    row = pl.BlockSpec((tm, D), lambda i: (i, 0))
    return pl.pallas_call(
        body, name="prenorm", out_shape=jax.ShapeDtypeStruct((T, D), BF16), grid=(T // tm,),
        in_specs=[row, _vec(D), _vec(D), _vec(D)], out_specs=row, compiler_params=_cp("parallel"),
    )(h, g, sc, sh)


def in_proj_gathered(a, w_full, chip, dims, tail):
    T, D = a.shape
    rows, cc = dims
    tm, tn = min(512, T), cc // 2
    ni = T // tm
    half = rows // 2

    nt = len(tail.operands)

    def body(chip_ref, a_ref, w_in_ref, *rest):
        tail_in, (y_ref, w_ref), rest = rest[:nt], rest[nt:nt + 2], rest[nt + 2:]
        tail_out, (wbuf, wsem, send_sems, recv_sems, tail_send, tail_recv) = rest[:nt], rest[nt:]
        q, j, i = pl.program_id(0), pl.program_id(1), pl.program_id(2)
        mx, my, mc, _ = _place()
        me = chip_ref[0]

        def tile(block, jj):
            src = w_ref.at[:, pl.ds(pl.multiple_of(block * cc + jj * tn, 128), tn)]
            return pltpu.make_async_copy(src, wbuf.at[jj], wsem.at[jj])

        def rows_half(block, hh):
            return w_ref.at[pl.ds(pl.multiple_of(hh * half, 16), half), pl.ds(pl.multiple_of(block * cc, 128), cc)]

        def over_ici(s, block):
            peer = (1 - mx if s & 2 else mx, 1 - my if s & 1 else my, mc)
            reg = rows_half(block, mc)
            return pltpu.make_async_remote_copy(src_ref=reg, dst_ref=reg, send_sem=send_sems.at[s - 1], recv_sem=recv_sems.at[s - 1],
                                                device_id=peer, device_id_type=MESH)

        def over_d2d(s, block, hh):
            reg = rows_half(block, hh)
            return pltpu.make_async_remote_copy(src_ref=reg, dst_ref=reg, send_sem=send_sems.at[2 + s], recv_sem=recv_sems.at[2 + s],
                                                device_id=(mx, my, 1 - mc), device_id_type=MESH)

        @pl.when((q == 0) & (j == 0) & (i == 0))
        def _():
            for s in (1, 2, 3):
                over_ici(s, me).start()
            tile(me, 0).start()
            tail.start(tail_in, tail_out, tail_send, tail_recv)

        @pl.when(i == 0)
        def _():
            tile(me ^ q, j).wait()

        @pl.when((i == 0) & (j == 0))
        def _():
            tile(me ^ q, 1).start()

        y_ref[...] = _nn(a_ref[...], wbuf[j])

        for s in (1, 2, 3):
            @pl.when((q == s - 1) & (j == 1) & (i == ni - 1))
            def _():
                block = me ^ s
                over_ici(s, block).wait_recv()
                over_d2d(s, block, mc).start()
                over_d2d(s, block, 1 - mc).wait_recv()
                tile(block, 0).start()


        @pl.when((q == 3) & (j == 1) & (i == ni - 1))
        def _():
            for s in (1, 2, 3):
                over_ici(s, me).wait_send()
                over_d2d(s, me ^ s, mc).wait_send()
            tail.finish(tail_in, tail_out, tail_send, tail_recv)

    dma = pltpu.SemaphoreType.DMA
    y, w_out, *tail_res = pl.pallas_call(
        body, name="in_proj", out_shape=[jax.ShapeDtypeStruct((T, 4 * cc), F32), jax.ShapeDtypeStruct(w_full.shape, BF16)] + tail.out_shape,
        grid_spec=pltpu.PrefetchScalarGridSpec(
            num_scalar_prefetch=1, grid=(4, 2, ni),
            in_specs=[pl.BlockSpec((tm, D), lambda q, j, i, chip_ref: (i, 0)), ANY] + [ANY] * nt,
            out_specs=[pl.BlockSpec((tm, tn), lambda q, j, i, chip_ref: (i, (chip_ref[0] ^ q) * 2 + j)), ANY] + [ANY] * nt,
            scratch_shapes=[pltpu.VMEM((2, D, tn), BF16), dma((2,)), dma((6,)), dma((6,)), dma((tail.n_sems,)), dma((tail.n_sems,))]),
        input_output_aliases={2: 1, **{3 + i: 2 + o for i, o in tail.aliases.items()}},
        compiler_params=_cp("arbitrary", "arbitrary", "arbitrary"),
    )(chip, a, w_full, *tail.operands)
    return y, w_out, tail_res


def prenorm_matmul(h, g, sc, sh, w, *, relu2, name, comm=None):
    T, D = h.shape
    N = w.shape[1]
    tm, tn = min(512, T), 2048 if N % 2048 == 0 else 1024

    def body(h_ref, g_ref, sc_ref, sh_ref, w_ref, y_ref, a_ref, *hid_ref):
        @pl.when(pl.program_id(1) == 0)
        def _():
            x = h_ref[...]
            r = lax.rsqrt(jnp.mean(x * x, axis=-1, keepdims=True) + EPS)
            a_ref[...] = ((x * r) * g_ref[...] * (1.0 + sc_ref[...]) + sh_ref[...]).astype(BF16)

        y = _nn(a_ref[...], w_ref[...])
        y_ref[...] = y.astype(y_ref.dtype)
        if relu2:
            p = jnp.maximum(y, 0.0)
            hid_ref[0][...] = (p * p).astype(BF16)

    out_shape = [jax.ShapeDtypeStruct((T, N), BF16 if relu2 else F32), jax.ShapeDtypeStruct((T, D), BF16)]
    out_specs = [pl.BlockSpec((tm, tn), lambda i, j: (i, j)), pl.BlockSpec((tm, D), lambda i, j: (i, 0))]
    if relu2:
        out_shape.append(jax.ShapeDtypeStruct((T, N), BF16))
        out_specs.append(pl.BlockSpec((tm, tn), lambda i, j: (i, j)))
    outs, landed = _pallas(
        body, name=name, grid=(T // tm, N // tn),
        in_specs=[pl.BlockSpec((tm, D), lambda i, j: (i, 0)), _vec(D), _vec(D), _vec(D), pl.BlockSpec((D, tn), lambda i, j: (0, j))],
        out_specs=out_specs, out_shape=out_shape, scratch=[], semantics=("parallel", "arbitrary"), operands=(h, g, sc, sh, w), comm=comm)
    return outs if comm is None else (outs, landed)


def _hgrn_lower_bound(l_ref):
    l0, l1 = l_ref[0:1, :], l_ref[1:2, :]
    m = jnp.maximum(l0, l1)
    e0, e1 = jnp.exp(l0 - m), jnp.exp(l1 - m)
    return e0 / (e0 + e1)


def _hgrn_chunk_mask(d):
    r = lax.broadcasted_iota(jnp.int32, (HGRN_BLOCK, HGRN_BLOCK), 0)
    c = lax.broadcasted_iota(jnp.int32, (HGRN_BLOCK, HGRN_BLOCK), 1)
    same = (r // HGRN_CHUNK) == (c // HGRN_CHUNK)
    fwd = d == 0
    return same & (((c <= r) & fwd) | ((c >= r) & jnp.logical_not(fwd)))


def _chunk_total(x):
    x3 = x.reshape(HGRN_BLOCK // HGRN_CHUNK, HGRN_CHUNK, x.shape[1])
    return jnp.broadcast_to(jnp.sum(x3, axis=1, keepdims=True), x3.shape).reshape(x.shape)


def _chunk_cumsum(x, suffix):
    pos = lax.broadcasted_iota(jnp.int32, x.shape, 0) % HGRN_CHUNK
    p, s = x, 1
    while s < HGRN_CHUNK:
        p = p + jnp.where(pos >= s, pltpu.roll(p, s, 0), 0.0)
        s *= 2
    return jnp.where(suffix, _chunk_total(x) - p + x, p)


def _block_loop(T, body, init):
    n = T // HGRN_BLOCK
    return lax.fori_loop(0, n, body, init, unroll=2 if n % 2 == 0 else 1)


def _hgrn_gate(f, lb):
    s = jax.nn.sigmoid(f)
    sn = jax.nn.sigmoid(-f)
    fg = lb + (1.0 - lb) * s
    return s, sn, fg, jnp.log(fg), (1.0 - lb) * sn


def _hgrn_specs(T):
    col = lambda base: pl.BlockSpec((T, HEAD_DIM), lambda h, d: (0, base * N_HEADS + h))
    f_spec = pl.BlockSpec((T, HEAD_DIM), lambda h, d: (0, COL_FFW * N_HEADS + N_HEADS * d + h))
    l_spec = pl.BlockSpec((None, 2, HEAD_DIM), lambda h, d: (d, 0, h))
    return col, f_spec, l_spec


def hgrn_fwd(proj, lb_logits, comm=None):
    T = proj.shape[0]
    NC, CPB = T // HGRN_CHUNK, HGRN_BLOCK // HGRN_CHUNK
    col, f_spec, l_spec = _hgrn_specs(T)

    def body(l_ref, q_ref, f_ref, v_ref, o_ref, st_ref, dec_ref, qd_ref):
        d = pl.program_id(1)
        lb = _hgrn_lower_bound(l_ref)
        mask = _hgrn_chunk_mask(d)

        def block(i, carry):
            rows = pl.ds(pl.multiple_of(i * HGRN_BLOCK, HGRN_BLOCK), HGRN_BLOCK)
            _, _, _, lf, k = _hgrn_gate(f_ref[rows, :], lb)
            b = _chunk_cumsum(lf, d == 1)
            bl = _chunk_total(lf)
            qd = (q_ref[rows, :] * Q_SCALE * jnp.exp(b)).astype(BF16)
            kd = (k * jnp.exp(-b)).astype(BF16)
            ke = (k * jnp.exp(bl - b)).astype(BF16)
            vb = v_ref[rows, :].astype(BF16)
            att = jnp.where(mask, _nt(qd, kd), 0.0).astype(BF16)
            o_ref[rows, :] = jnp.where(d == 0, 0.0, o_ref[rows, :]) + _nn(att, vb)
            qd_ref[rows, :] = qd
            dec = jnp.exp(bl)
            for cc in range(CPB):
                sl = slice(cc * HGRN_CHUNK, (cc + 1) * HGRN_CHUNK)
                n = i * CPB + cc
                st_ref[n] = _tn(vb[sl], ke[sl])
                dec_ref[n] = dec[cc * HGRN_CHUNK:cc * HGRN_CHUNK + 8, :]
            return carry

        _block_loop(T, block, 0)

        def scan(t, s):
            n = jnp.where(d == 0, t, NC - 1 - t)
            u = st_ref[n]
            st_ref[n] = s
            return dec_ref[n][0:1, :] * s + u

        lax.fori_loop(0, NC, scan, jnp.zeros((HEAD_DIM, HEAD_DIM), F32))

        def inter(i, carry):
            rows = pl.ds(pl.multiple_of(i * HGRN_BLOCK, HGRN_BLOCK), HGRN_BLOCK)
            qd = qd_ref[rows, :]
            o_ref[rows, :] += jnp.concatenate(
                [_nt(qd[cc * HGRN_CHUNK:(cc + 1) * HGRN_CHUNK], st_ref[i * CPB + cc].astype(BF16)) for cc in range(CPB)], axis=0)
            return carry

        _block_loop(T, inter, 0)

    (o,), landed = _pallas(
        body, name="hgrn_fwd", grid=(N_HEADS, 2), in_specs=[l_spec, col(COL_Q), f_spec, col(COL_V)],
        out_specs=[pl.BlockSpec((T, HEAD_DIM), lambda h, d: (0, h))], out_shape=[jax.ShapeDtypeStruct((T, N_HEADS * HEAD_DIM), F32)],
        scratch=[pltpu.VMEM((NC, HEAD_DIM, HEAD_DIM), F32), pltpu.VMEM((NC, 8, HEAD_DIM), F32), pltpu.VMEM((T, HEAD_DIM), BF16)],
        semantics=("parallel", "arbitrary"), operands=(lb_logits, proj, proj, proj), comm=comm)
    return o if comm is None else (o, landed)


def hgrn_post_fwd(o, proj, g_norm):
    T, W = o.shape
    tm = min(256, T)

    def body(o_ref, og_ref, g_ref, y_ref):
        g = g_ref[...]
        for h in range(N_HEADS):
            sl = slice(h * HEAD_DIM, (h + 1) * HEAD_DIM)
            x = o_ref[:, sl]
            r = lax.rsqrt(jnp.mean(x * x, axis=-1, keepdims=True) + EPS)
            og = og_ref[:, sl]
            y_ref[:, sl] = ((x * r) * g * (og * jax.nn.sigmoid(og))).astype(BF16)

    return pl.pallas_call(
        body, name="hgrn_post_fwd", out_shape=jax.ShapeDtypeStruct((T, W), BF16), grid=(T // tm,),
        in_specs=[pl.BlockSpec((tm, W), lambda i: (i, 0)), pl.BlockSpec((tm, W), lambda i: (i, COL_OG)), _vec(HEAD_DIM)],
        out_specs=pl.BlockSpec((tm, W), lambda i: (i, 0)), compiler_params=_cp("parallel"),
    )(o, proj, g_norm)


def _gelu(x):
    return 0.5 * x * (1.0 + lax.erf(x * (1.0 / math.sqrt(2.0))))


def _gelu_grad(x):
    return 0.5 * (1.0 + lax.erf(x * (1.0 / math.sqrt(2.0)))) + x * jnp.exp(-0.5 * x * x) * (1.0 / math.sqrt(2.0 * math.pi))


def _sgu_mix(u_ref, v_ref, g_ref, ws_ref, bst_ref):
    W = u_ref.shape[1]
    zu, zv = _gelu(u_ref[...]), _gelu(v_ref[...])
    dv = zv - jnp.mean(zv, axis=-1, keepdims=True)
    rstd = lax.rsqrt(jnp.mean(dv * dv, axis=-1, keepdims=True) + EPS)
    dhat = dv * rstd
    vn = (dhat * g_ref[...]).astype(BF16)
    gw = W // SGU_GROUPS
    vm = [_nn(ws_ref[g].astype(BF16), vn[:, g * gw:(g + 1) * gw]) + bst_ref[:, g:g + 1] for g in range(SGU_GROUPS)]
    return zu, rstd, dhat, vn, jnp.concatenate(vm, axis=1)


def sgu_fwd(proj, g_norm, w_spatial, b_spatial_t):
    T = proj.shape[0]
    W = 1024
    n_chunks = T // SGU_CHUNK

    def body(u_ref, v_ref, g_ref, ws_ref, bst_ref, y_ref):
        zu, _, _, _, vm = _sgu_mix(u_ref, v_ref, g_ref, ws_ref, bst_ref)
        y_ref[...] = (zu * vm).astype(BF16)

    blk = lambda cb: pl.BlockSpec((SGU_CHUNK, W), lambda i: (i, cb))
    return pl.pallas_call(
        body, name="sgu_fwd", out_shape=jax.ShapeDtypeStruct((T, W), BF16), grid=(n_chunks,),
        in_specs=[blk(COL_U), blk(COL_ZV), _vec(W), pl.BlockSpec((SGU_GROUPS, SGU_CHUNK, SGU_CHUNK), lambda i: (0, 0, 0)),
                  pl.BlockSpec((SGU_CHUNK, SGU_GROUPS), lambda i: (0, 0))],
        out_specs=blk(0), compiler_params=_cp("parallel"),
    )(proj, proj, g_norm, w_spatial, b_spatial_t)


def merge_matmul(ya_pre, sgu, w_a, w_b, proj):
    T, K = ya_pre.shape
    N = w_a.shape[1]
    tm, tn = min(512, T), 512
    gpb = 1024 // tn

    def body(a_ref, b_ref, wa_ref, wb_ref, ga_ref, gb_ref, ya_ref, yb_ref, m_ref):
        ya = _nn(a_ref[...], wa_ref[...])
        yb = _nn(b_ref[...], wb_ref[...])
        ya_ref[...] = ya.astype(BF16)
        yb_ref[...] = yb.astype(BF16)
        m_ref[...] = (jax.nn.sigmoid(ga_ref[...]) * ya + jax.nn.sigmoid(gb_ref[...]) * yb).astype(BF16)

    lhs = pl.BlockSpec((tm, K), lambda i, j: (i, 0))
    rhs = pl.BlockSpec((K, tn), lambda i, j: (0, j))
    out = pl.BlockSpec((tm, tn), lambda i, j: (i, j))
    return pl.pallas_call(
        body, name="merge_matmul", grid=(T // tm, N // tn),
        out_shape=[jax.ShapeDtypeStruct((T, N), BF16)] * 3,
        in_specs=[lhs, lhs, rhs, rhs, pl.BlockSpec((tm, tn), lambda i, j: (i, COL_GA * gpb + j)),
                  pl.BlockSpec((tm, tn), lambda i, j: (i, COL_GB * gpb + j))],
        out_specs=[out, out, out], compiler_params=_cp("parallel", "parallel"),
    )(ya_pre, sgu, w_a, w_b, proj, proj)


def out_proj(merged, w_o, h0, gt1, g_post):
    T, D = h0.shape
    tm = min(256, T)

    def body(m_ref, w_ref, h_ref, gt_ref, gp_ref, mo_ref, h1_ref):
        mo = _nn(m_ref[...], w_ref[...])
        mo_ref[...] = mo
        r = lax.rsqrt(jnp.mean(mo * mo, axis=-1, keepdims=True) + EPS)
        h1_ref[...] = h_ref[...] + gt_ref[...] * ((mo * r) * gp_ref[...])

    row = pl.BlockSpec((tm, D), lambda i: (i, 0))
    return pl.pallas_call(
        body, name="out_proj", grid=(T // tm,),
        out_shape=[jax.ShapeDtypeStruct((T, D), F32), jax.ShapeDtypeStruct((T, D), F32)],
        in_specs=[row, pl.BlockSpec((D, D), lambda i: (0, 0)), row, _vec(D), _vec(D)],
        out_specs=[row, row], compiler_params=_cp("parallel"),
    )(merged, w_o, h0, gt1, g_post)


def ff2_loss(hid, w_ff2, h1, tgt, gt2, g_post):
    T, K = hid.shape
    D = w_ff2.shape[1]
    tm, tk = min(256, T), 2048
    nk = K // tk

    def body(a_ref, w_ref, h_ref, t_ref, gt_ref, g_ref, dy_ref, dff_ref, loss_ref, dgt_ref, dg_ref, acc_ref):
        i, k = pl.program_id(0), pl.program_id(1)

        @pl.when(k == 0)
        def _():
            acc_ref[...] = jnp.zeros_like(acc_ref)

        @pl.when((k == 0) & (i == 0))
        def _():
            loss_ref[...] = jnp.zeros_like(loss_ref)
            dgt_ref[...] = jnp.zeros_like(dgt_ref)
            dg_ref[...] = jnp.zeros_like(dg_ref)

        acc_ref[...] += _nn(a_ref[...], w_ref[...])

        @pl.when(k == nk - 1)
        def _():
            ff = acc_ref[...]
            gt, g = gt_ref[...], g_ref[...]
            r = lax.rsqrt(jnp.mean(ff * ff, axis=-1, keepdims=True) + EPS)
            fhat = ff * r
            nf = fhat * g
            err = (h_ref[...] + gt * nf) - t_ref[...]
            loss_ref[...] += jnp.sum(err * err)
            dy = err * (1.0 / D)
            dy_ref[...] = dy
            dgt_ref[...] += _colsum(dy * nf)
            dnf = dy * gt
            dg_ref[...] += _colsum(dnf * fhat)
            u = dnf * g
            dff_ref[...] = (r * (u - fhat * jnp.mean(u * fhat, axis=-1, keepdims=True))).astype(BF16)

    row = pl.BlockSpec((tm, D), lambda i, k: (i, 0))
    vec = pl.BlockSpec((1, D), lambda i, k: (0, 0))
    return pl.pallas_call(
        body, name="ff2_loss", grid=(T // tm, nk),
        out_shape=[jax.ShapeDtypeStruct((T, D), F32), jax.ShapeDtypeStruct((T, D), BF16), jax.ShapeDtypeStruct((8, 128), F32),
                   jax.ShapeDtypeStruct((1, D), F32), jax.ShapeDtypeStruct((1, D), F32)],
        in_specs=[pl.BlockSpec((tm, tk), lambda i, k: (i, k)), pl.BlockSpec((tk, D), lambda i, k: (k, 0)), row, row, vec, vec],
        out_specs=[row, row, pl.BlockSpec((8, 128), lambda i, k: (0, 0)), vec, vec],
        scratch_shapes=[pltpu.VMEM((tm, D), F32)], compiler_params=_cp("arbitrary", "arbitrary"),
    )(hid, w_ff2, h1, tgt, gt2, g_post)


def ff2_bwd(dff, w_ff2, f1):
    T, D = dff.shape
    K = w_ff2.shape[0]
    tm, tn = min(512, T), 2048

    def body(a_ref, w_ref, f_ref, o_ref):
        o_ref[...] = (_nt(a_ref[...], w_ref[...]) * (2.0 * jnp.maximum(f_ref[...].astype(F32), 0.0))).astype(BF16)

    return pl.pallas_call(
        body, name="ff2_bwd", out_shape=jax.ShapeDtypeStruct((T, K), BF16), grid=(K // tn, T // tm),
        in_specs=[pl.BlockSpec((tm, D), lambda j, i: (i, 0)), pl.BlockSpec((tn, D), lambda j, i: (j, 0)),
                  pl.BlockSpec((tm, tn), lambda j, i: (i, j))],
        out_specs=pl.BlockSpec((tm, tn), lambda j, i: (i, j)), compiler_params=_cp("parallel", "parallel"),
    )(dff, w_ff2, f1)


def ffn_norm_bwd(dy, da2, h1, mo, g_pre2, sc2, gt1, g_post, comm):
    T, D = dy.shape
    tm = min(256, T)

    def body(dy_ref, da_ref, h_ref, mo_ref, g2_ref, sc_ref, gt_ref, gp_ref, dh_ref, dmo_ref, s_sh, s_sc, s_g2, s_gt, s_gp):
        @pl.when(pl.program_id(0) == 0)
        def _():
            for s in (s_sh, s_sc, s_g2, s_gt, s_gp):
                s[...] = jnp.zeros_like(s)

        h1, da = h_ref[...], da_ref[...]
        g2, sc = g2_ref[...], sc_ref[...]
        r2 = lax.rsqrt(jnp.mean(h1 * h1, axis=-1, keepdims=True) + EPS)
        n2 = h1 * r2
        s_sh[...] += _colsum(da)
        s_sc[...] += _colsum(da * (n2 * g2))
        s_g2[...] += _colsum(da * (1.0 + sc) * n2)
        dn2 = da * g2 * (1.0 + sc)
        dh1 = dy_ref[...] + r2 * (dn2 - n2 * jnp.mean(dn2 * n2, axis=-1, keepdims=True))
        dh_ref[...] = dh1
        mo = mo_ref[...]
        gt, gp = gt_ref[...], gp_ref[...]
        r = lax.rsqrt(jnp.mean(mo * mo, axis=-1, keepdims=True) + EPS)
        mhat = mo * r
        s_gt[...] += _colsum(dh1 * (mhat * gp))
        dnm = dh1 * gt
        s_gp[...] += _colsum(dnm * mhat)
        u = dnm * gp
        dmo_ref[...] = (r * (u - mhat * jnp.mean(u * mhat, axis=-1, keepdims=True))).astype(BF16)

    row = pl.BlockSpec((tm, D), lambda i: (i, 0))
    vec_out = jax.ShapeDtypeStruct((1, D), F32)
    return _pallas(
        body, name="ffn_norm_bwd", grid=(T // tm,),
        out_shape=[jax.ShapeDtypeStruct((T, D), F32), jax.ShapeDtypeStruct((T, D), BF16)] + [vec_out] * 5,
        in_specs=[row, row, row, row] + [_vec(D)] * 4, out_specs=[row, row] + [_vec(D)] * 5,
        scratch=[], semantics=("arbitrary",), operands=(dy, da2, h1, mo, g_pre2, sc2, gt1, g_post), comm=comm)


def out_proj_bwd(dmo, w_o, y_a, y_b, proj):
    T, D = dmo.shape
    tm, tn = min(512, T), 512
    gpb = 1024 // tn

    def body(a_ref, w_ref, ya_ref, yb_ref, ga_ref, gb_ref, dya_ref, dyb_ref, dga_ref, dgb_ref):
        dm = _nt(a_ref[...], w_ref[...])
        sa, sb = jax.nn.sigmoid(ga_ref[...]), jax.nn.sigmoid(gb_ref[...])
        dya_ref[...] = (dm * sa).astype(BF16)
        dyb_ref[...] = (dm * sb).astype(BF16)
        dga_ref[...] = (dm * ya_ref[...].astype(F32) * sa * (1.0 - sa)).astype(BF16)
        dgb_ref[...] = (dm * yb_ref[...].astype(F32) * sb * (1.0 - sb)).astype(BF16)

    out = pl.BlockSpec((tm, tn), lambda i, j: (i, j))
    return pl.pallas_call(
        body, name="out_proj_bwd", grid=(T // tm, D // tn), out_shape=[jax.ShapeDtypeStruct((T, D), BF16)] * 4,
        in_specs=[pl.BlockSpec((tm, D), lambda i, j: (i, 0)), pl.BlockSpec((tn, D), lambda i, j: (j, 0)), out, out,
                  pl.BlockSpec((tm, tn), lambda i, j: (i, COL_GA * gpb + j)), pl.BlockSpec((tm, tn), lambda i, j: (i, COL_GB * gpb + j))],
        out_specs=[out] * 4, compiler_params=_cp("parallel", "parallel"),
    )(dmo, w_o, y_a, y_b, proj, proj)


def sgu_bwd(proj, dsgu, g_norm, w_spatial, b_spatial_t):
    T = proj.shape[0]
    W = 1024
    gw = W // SGU_GROUPS

    def body(u_ref, v_ref, ds_ref, g_ref, ws_ref, bst_ref, dz_ref, dw_ref, db_ref, dg_ref):
        @pl.when(pl.program_id(0) == 0)
        def _():
            dw_ref[...] = jnp.zeros_like(dw_ref)
            db_ref[...] = jnp.zeros_like(db_ref)
            dg_ref[...] = jnp.zeros_like(dg_ref)

        zu, rstd, dhat, vn, vm = _sgu_mix(u_ref, v_ref, g_ref, ws_ref, bst_ref)
        ds = ds_ref[...]
        du = ds * vm
        dvm = ds * zu
        dvm_b = dvm.astype(BF16)
        ones = jnp.ones((8, gw), F32)
        dvn = []
        for g in range(SGU_GROUPS):
            sl = slice(g * gw, (g + 1) * gw)
            dw_ref[g] += _nt(dvm_b[:, sl], vn[:, sl])
            db_ref[g] += lax.dot_general(ones, dvm[:, sl], (((1,), (1,)), ((), ())), precision=HI, preferred_element_type=F32)
            dvn.append(_tn(ws_ref[g].astype(BF16), dvm_b[:, sl]))
        dvn = jnp.concatenate(dvn, axis=1)
        dg_ref[...] += _colsum(dvn * dhat)
        ddh = dvn * g_ref[...]
        dzv = rstd * (ddh - jnp.mean(ddh, axis=-1, keepdims=True) - dhat * jnp.mean(ddh * dhat, axis=-1, keepdims=True))
        dz_ref[:, 0:W] = (du * _gelu_grad(u_ref[...])).astype(BF16)
        dz_ref[:, W:2 * W] = (dzv * _gelu_grad(v_ref[...])).astype(BF16)

    blk = lambda cb: pl.BlockSpec((SGU_CHUNK, W), lambda i: (i, cb))
    full3 = lambda a, b, c: pl.BlockSpec((a, b, c), lambda i: (0, 0, 0))
    return pl.pallas_call(
        body, name="sgu_bwd", grid=(T // SGU_CHUNK,),
        out_shape=[jax.ShapeDtypeStruct((T, 2 * W), BF16), jax.ShapeDtypeStruct((SGU_GROUPS, SGU_CHUNK, SGU_CHUNK), F32),
                   jax.ShapeDtypeStruct((SGU_GROUPS, 8, SGU_CHUNK), F32), jax.ShapeDtypeStruct((1, W), F32)],
        in_specs=[blk(COL_U), blk(COL_ZV), blk(0), _vec(W), full3(SGU_GROUPS, SGU_CHUNK, SGU_CHUNK),
                  pl.BlockSpec((SGU_CHUNK, SGU_GROUPS), lambda i: (0, 0))],
        out_specs=[pl.BlockSpec((SGU_CHUNK, 2 * W), lambda i: (i, 0)), full3(SGU_GROUPS, SGU_CHUNK, SGU_CHUNK),
                   full3(SGU_GROUPS, 8, SGU_CHUNK), _vec(W)],
        compiler_params=_cp("arbitrary"),
    )(proj, proj, dsgu, g_norm, w_spatial, b_spatial_t)


def hgrn_post_bwd(dya, o, proj, g_norm):
    T, W = o.shape
    tm = min(256, T)

    def body(dy_ref, o_ref, og_ref, g_ref, do_ref, dog_ref, dg_ref):
        @pl.when(pl.program_id(0) == 0)
        def _():
            dg_ref[...] = jnp.zeros_like(dg_ref)

        g = g_ref[...]
        dg = jnp.zeros((1, HEAD_DIM), F32)
        for h in range(N_HEADS):
            sl = slice(h * HEAD_DIM, (h + 1) * HEAD_DIM)
            x, og, dy = o_ref[:, sl], og_ref[:, sl], dy_ref[:, sl]
            r = lax.rsqrt(jnp.mean(x * x, axis=-1, keepdims=True) + EPS)
            xhat = x * r
            s = jax.nn.sigmoid(og)
            don = dy * (og * s)
            dog_ref[:, sl] = (dy * (xhat * g) * (s * (1.0 + og * (1.0 - s)))).astype(BF16)
            dg += _colsum(don * xhat)
            u = don * g
            do_ref[:, sl] = r * (u - xhat * jnp.mean(u * xhat, axis=-1, keepdims=True))
        dg_ref[...] += dg

    row = pl.BlockSpec((tm, W), lambda i: (i, 0))
    return pl.pallas_call(
        body, name="hgrn_post_bwd", grid=(T // tm,),
        out_shape=[jax.ShapeDtypeStruct((T, W), F32), jax.ShapeDtypeStruct((T, W), BF16), jax.ShapeDtypeStruct((1, HEAD_DIM), F32)],
        in_specs=[row, row, pl.BlockSpec((tm, W), lambda i: (i, COL_OG)), _vec(HEAD_DIM)],
        out_specs=[row, row, _vec(HEAD_DIM)], compiler_params=_cp("arbitrary"),
    )(dya, o, proj, g_norm)


def hgrn_bwd(proj, do, lb_logits, comm=None):
    T = proj.shape[0]
    NC, CPB = T // HGRN_CHUNK, HGRN_BLOCK // HGRN_CHUNK
    W = N_HEADS * HEAD_DIM
    col, f_spec, l_spec = _hgrn_specs(T)

    def body(l_ref, q_ref, f_ref, v_ref, do_ref, dq_ref, dv_ref, dlg_ref, dlb_ref, st_ref, dst_ref, dec_ref, ddec_ref, dqa_ref, dva_ref):
        d = pl.program_id(1)
        lb = _hgrn_lower_bound(l_ref)
        oml = 1.0 - lb
        mask = _hgrn_chunk_mask(d)

        def values(rows):
            s, sn, fg, lf, k = _hgrn_gate(f_ref[rows, :], lb)
            b = _chunk_cumsum(lf, d == 1)
            bl = _chunk_total(lf)
            eb, enb, ee = jnp.exp(b), jnp.exp(-b), jnp.exp(bl - b)
            qd = q_ref[rows, :] * Q_SCALE * eb
            return s, sn, fg, k, bl, eb, enb, ee, qd, k * enb, k * ee

        def block1(i, carry):
            rows = pl.ds(pl.multiple_of(i * HGRN_BLOCK, HGRN_BLOCK), HGRN_BLOCK)
            _, _, _, _, bl, _, _, _, qd, _, ke = values(rows)
            qd, ke = qd.astype(BF16), ke.astype(BF16)
            vb, dob = v_ref[rows, :].astype(BF16), do_ref[rows, :].astype(BF16)
            dec = jnp.exp(bl)
            for cc in range(CPB):
                sl = slice(cc * HGRN_CHUNK, (cc + 1) * HGRN_CHUNK)
                n = i * CPB + cc
                st_ref[n] = _tn(vb[sl], ke[sl])
                dst_ref[n] = _tn(dob[sl], qd[sl])
                dec_ref[n] = dec[cc * HGRN_CHUNK:cc * HGRN_CHUNK + 8, :]
            return carry

        _block_loop(T, block1, 0)

        def scan(t, s):
            n = jnp.where(d == 0, t, NC - 1 - t)
            u = st_ref[n]
            st_ref[n] = s
            return dec_ref[n][0:1, :] * s + u

        lax.fori_loop(0, NC, scan, jnp.zeros((HEAD_DIM, HEAD_DIM), F32))

        def rscan(t, ds):
            n = jnp.where(d == 0, NC - 1 - t, t)
            w = dst_ref[n]
            dst_ref[n] = ds
            ddec_ref[n] = jnp.broadcast_to(_colsum(ds * st_ref[n]), (8, HEAD_DIM))
            return dec_ref[n][0:1, :] * ds + w

        lax.fori_loop(0, NC, rscan, jnp.zeros((HEAD_DIM, HEAD_DIM), F32))

        def block3(i, dlb):
            rows = pl.ds(pl.multiple_of(i * HGRN_BLOCK, HGRN_BLOCK), HGRN_BLOCK)
            s, sn, fg, k, bl, eb, enb, ee, qd, kd, ke = values(rows)
            qdb, kdb, keb = qd.astype(BF16), kd.astype(BF16), ke.astype(BF16)
            vb, dob = v_ref[rows, :].astype(BF16), do_ref[rows, :].astype(BF16)
            att = jnp.where(mask, _nt(qdb, kdb), 0.0).astype(BF16)
            datt = jnp.where(mask, _nt(dob, vb), 0.0).astype(BF16)
            dv = _tn(att, dob)
            dqd = _nn(datt, kdb)
            dkd = _tn(datt, qdb)
            dv_i, dqd_i, dke, ddl = [], [], [], []
            for cc in range(CPB):
                sl = slice(cc * HGRN_CHUNK, (cc + 1) * HGRN_CHUNK)
                n = i * CPB + cc
                st_b, dst_b = st_ref[n].astype(BF16), dst_ref[n].astype(BF16)
                dv_i.append(_nt(keb[sl], dst_b))
                dqd_i.append(_nn(dob[sl], st_b))
                dke.append(_nn(vb[sl], dst_b))
                ddl.append(jnp.broadcast_to(ddec_ref[n][0:1, :] * dec_ref[n][0:1, :], (HGRN_CHUNK, HEAD_DIM)))
            dv = dv + jnp.concatenate(dv_i, axis=0)
            dqd = dqd + jnp.concatenate(dqd_i, axis=0)
            dke = jnp.concatenate(dke, axis=0)
            dq = dqd * eb * Q_SCALE
            dk = dkd * enb + dke * ee
            t_end = dke * ke
            db = dqd * qd - dkd * kd - t_end
            dlf = _chunk_cumsum(db, d == 0) + _chunk_total(t_end) + jnp.concatenate(ddl, axis=0)
            e = dlf / fg - dk
            dlg_ref[rows, :] = (oml * e * s * sn).astype(BF16)

            dq = jnp.where(d == 0, 0.0, dqa_ref[rows, :]) + dq
            dv = jnp.where(d == 0, 0.0, dva_ref[rows, :]) + dv
            dqa_ref[rows, :] = dq
            dva_ref[rows, :] = dv
            dq_ref[rows, :] = dq.astype(BF16)
            dv_ref[rows, :] = dv.astype(BF16)

            return dlb + _colsum(e * sn)

        dlb_ref[...] = _block_loop(T, block3, jnp.zeros((1, HEAD_DIM), F32))

    head = pl.BlockSpec((T, HEAD_DIM), lambda h, d: (0, h))
    big = pltpu.VMEM((NC, HEAD_DIM, HEAD_DIM), F32)
    small = pltpu.VMEM((NC, 8, HEAD_DIM), F32)
    acc = pltpu.VMEM((T, HEAD_DIM), F32)
    outs, landed = _pallas(
        body, name="hgrn_bwd", grid=(N_HEADS, 2),
        out_shape=[jax.ShapeDtypeStruct((T, W), BF16), jax.ShapeDtypeStruct((T, W), BF16), jax.ShapeDtypeStruct((T, 2 * W), BF16),
                   jax.ShapeDtypeStruct((2, 1, W), F32)],
        in_specs=[l_spec, col(COL_Q), f_spec, col(COL_V), head],
        out_specs=[head, head, pl.BlockSpec((T, HEAD_DIM), lambda h, d: (0, N_HEADS * d + h)),
                   pl.BlockSpec((None, 1, HEAD_DIM), lambda h, d: (d, 0, h))],
        scratch=[big, big, small, small, acc, acc], semantics=("parallel", "arbitrary"), operands=(lb_logits, proj, proj, proj, do), comm=comm)
    return outs if comm is None else (outs, landed)


def mix_norm_bwd(da1, h0, dh1, g_pre, sc1):
    T, D = h0.shape
    tm = min(256, T)

    def body(da_ref, h_ref, dh_ref, g_ref, sc_ref, gx_ref, s_sh, s_sc, s_g):
        @pl.when(pl.program_id(0) == 0)
        def _():
            for s in (s_sh, s_sc, s_g):
                s[...] = jnp.zeros_like(s)

        h, da = h_ref[...], da_ref[...]
        g, sc = g_ref[...], sc_ref[...]
        r = lax.rsqrt(jnp.mean(h * h, axis=-1, keepdims=True) + EPS)
        n = h * r
        s_sh[...] += _colsum(da)
        s_sc[...] += _colsum(da * (n * g))
        s_g[...] += _colsum(da * (1.0 + sc) * n)
        dn = da * g * (1.0 + sc)
        gx_ref[...] = dh_ref[...] + r * (dn - n * jnp.mean(dn * n, axis=-1, keepdims=True))

    row = pl.BlockSpec((tm, D), lambda i: (i, 0))
    return pl.pallas_call(
        body, name="mix_norm_bwd", grid=(T // tm,),
        out_shape=[jax.ShapeDtypeStruct((T, D), F32)] + [jax.ShapeDtypeStruct((1, D), F32)] * 3,
        in_specs=[row, row, row, _vec(D), _vec(D)], out_specs=[row] + [_vec(D)] * 3, compiler_params=_cp("arbitrary"),
    )(da1, h0, dh1, g_pre, sc1)


def adamw(w, g, m, v, name):
    R, C = w.shape
    tr = R if R * C * 4 <= (1 << 21) else max(8, ((1 << 21) // (C * 4)) // 8 * 8)
    while R % tr:
        tr -= 8

    def body(w_ref, g_ref, m_ref, v_ref, d_ref, m2_ref, v2_ref):
        d_ref[...], m2_ref[...], v2_ref[...] = _adamw(w_ref[...], g_ref[...], m_ref[...], v_ref[...])

    row = pl.BlockSpec((tr, C), lambda i: (i, 0))
    return pl.pallas_call(
        body, name=name, grid=(R // tr,), out_shape=[jax.ShapeDtypeStruct((R, C), F32)] * 3,
        in_specs=[row] * 4, out_specs=[row] * 3, compiler_params=_cp("parallel"),
    )(w, g, m, v)


def wada_update(c_all, dmod, w, m, v):
    D, N = w.shape
    tm, tn = 512, 1024

    def body(c_ref, dm_ref, w_ref, m_ref, v_ref, g_ref, d_ref, m2_ref, v2_ref):
        c = c_ref[...]
        g = lax.dot_general(c * jax.nn.sigmoid(c), dm_ref[...], (((0,), (0,)), ((), ())), precision=HI, preferred_element_type=F32)
        g_ref[...] = g
        d_ref[...], m2_ref[...], v2_ref[...] = _adamw(w_ref[...], g, m_ref[...], v_ref[...])

    blk = pl.BlockSpec((tm, tn), lambda i, j: (i, j))
    return pl.pallas_call(
        body, name="wada_update", grid=(D // tm, N // tn), out_shape=[jax.ShapeDtypeStruct((D, N), F32)] * 4,
        in_specs=[pl.BlockSpec((8, tm), lambda i, j: (0, i)), pl.BlockSpec((8, tn), lambda i, j: (0, j)), blk, blk, blk],
        out_specs=[blk] * 4, compiler_params=_cp("parallel", "parallel"),
    )(c_all, dmod, w, m, v)


def sum_devices(gathered, name):
    n, R, C = gathered.shape

    def body(g_ref, o_ref):
        s = g_ref[0]
        for i in range(1, n):
            s = s + g_ref[i]
        o_ref[...] = s

    return pl.pallas_call(body, name=name, out_shape=jax.ShapeDtypeStruct((R, C), F32), compiler_params=_cp())(gathered)


def lb_logits_grad(dlb, lb_logits):
    def body(d_ref, l_ref, o_ref):
        for d in range(2):
            l0, l1 = l_ref[d, 0:1, :], l_ref[d, 1:2, :]
            m = jnp.maximum(l0, l1)
            e0, e1 = jnp.exp(l0 - m), jnp.exp(l1 - m)
            p0, p1 = e0 / (e0 + e1), e1 / (e0 + e1)
            g = d_ref[d:d + 1, :]
            o_ref[d, 0:1, :] = p0 * (g - p0 * g)
            o_ref[d, 1:2, :] = -p1 * (p0 * g)

    return pl.pallas_call(body, name="lb_logits_grad", out_shape=jax.ShapeDtypeStruct(lb_logits.shape, F32), compiler_params=_cp())(dlb, lb_logits)


def add_halves(g, landed, core):
    nj, _, r, cc = g.shape
    tr = min(256, r)

    def body(core_ref, g_ref, l_ref, o_ref):
        o_ref[...] = (g_ref[...].astype(F32) + l_ref[...].astype(F32)).astype(BF16)

    return pl.pallas_call(
        body, name="add_halves_%dx%d" % (r, cc), out_shape=jax.ShapeDtypeStruct((nj, r, cc), BF16),
        grid_spec=pltpu.PrefetchScalarGridSpec(
            num_scalar_prefetch=1, grid=(nj, r // tr),
            in_specs=[pl.BlockSpec((None, None, tr, cc), lambda j, i, core_ref: (j, core_ref[0], i, 0)),
                      pl.BlockSpec((None, None, tr, cc), lambda j, i, core_ref: (j, 0, i, 0))],
            out_specs=pl.BlockSpec((None, tr, cc), lambda j, i, core_ref: (j, i, 0))),
        compiler_params=_cp("parallel", "parallel"),
    )(core, g, landed)


def sum_chips(parts, landed, chip):
    nj, r, cc = parts.shape
    tr = min(256, r)

    def body(chip_ref, p_ref, l_ref, o_ref):
        mine = p_ref[...].astype(F32)
        s = None
        for j in range(nj):
            t = jnp.where(chip_ref[0] == j, mine, l_ref[j].astype(F32))
            s = t if s is None else s + t
        o_ref[...] = s

    return pl.pallas_call(
        body, name="sum_chips_%dx%d" % (r, cc), out_shape=jax.ShapeDtypeStruct((r, cc), F32),
        grid_spec=pltpu.PrefetchScalarGridSpec(
            num_scalar_prefetch=1, grid=(r // tr,),
            in_specs=[pl.BlockSpec((None, tr, cc), lambda i, chip_ref: (chip_ref[0], i, 0)),
                      pl.BlockSpec((nj, tr, cc), lambda i, chip_ref: (0, i, 0))],
            out_specs=pl.BlockSpec((tr, cc), lambda i, chip_ref: (i, 0))),
        compiler_params=_cp("parallel"),
    )(chip, parts, landed)


def adamw_halves(w, own, other, m, v, core, name):
    r, cc = own.shape
    tr = min(128, r)
    nb = r // tr

    def body(core_ref, w_ref, a_ref, b_ref, m_ref, v_ref, g_ref, d_ref, m2_ref, v2_ref):
        g = jnp.where(pl.program_id(0) == core_ref[0], a_ref[...], b_ref[...])
        g_ref[...] = g
        d_ref[...], m2_ref[...], v2_ref[...] = _adamw(w_ref[...], g, m_ref[...], v_ref[...])

    full = pl.BlockSpec((tr, cc), lambda h, i, core_ref: (h * nb + i, 0))
    half = pl.BlockSpec((tr, cc), lambda h, i, core_ref: (i, 0))
    return pl.pallas_call(
        body, name=name, out_shape=[jax.ShapeDtypeStruct((2 * r, cc), F32)] * 4,
        grid_spec=pltpu.PrefetchScalarGridSpec(
            num_scalar_prefetch=1, grid=(2, nb), in_specs=[full, half, half, full, full], out_specs=[full] * 4),
        compiler_params=_cp("parallel", "parallel"),
    )(core, w, own, other, m, v)


def _place():
    mx, my, mc = lax.axis_index("x"), lax.axis_index("y"), lax.axis_index("c")
    chips = [(1 - mx, my), (mx, 1 - my), (1 - mx, 1 - my)]
    return mx, my, mc, chips


def all_gather_small(x, name):
    R, C = x.shape

    def body(x_ref, out_ref, send_sems, recv_sems, local_sem):
        mx, my, mc, _ = _place()
        me = 4 * mx + 2 * my + mc
        mine = pltpu.make_async_copy(x_ref, out_ref.at[me], local_sem)
        mine.start()

        def peer(k):
            px = 1 - mx if k & 4 else mx
            py = 1 - my if k & 2 else my
            pc = 1 - mc if k & 1 else mc
            return px, py, pc

        def copy(k, src, slot):
            return pltpu.make_async_remote_copy(src_ref=src, dst_ref=out_ref.at[slot], send_sem=send_sems.at[k - 1],
                                                recv_sem=recv_sems.at[k - 1], device_id=peer(k), device_id_type=MESH)

        sends = [copy(k, x_ref, me) for k in range(1, 8)]
        for cp in sends:
            cp.start()
        for k in range(1, 8):
            px, py, pc = peer(k)
            slot = 4 * px + 2 * py + pc
            copy(k, out_ref.at[slot], slot).wait_recv()
        for cp in sends:
            cp.wait_send()
        mine.wait()

    return pl.pallas_call(
        body, name=name, out_shape=jax.ShapeDtypeStruct((8, R, C), F32),
        in_specs=[pl.BlockSpec(memory_space=pltpu.VMEM)], out_specs=pl.BlockSpec(memory_space=pltpu.VMEM),
        scratch_shapes=[pltpu.SemaphoreType.DMA((7,)), pltpu.SemaphoreType.DMA((7,)), pltpu.SemaphoreType.DMA],
        compiler_params=_cp(),
    )(x)


def gather8_comm(x):
    def copies(x_ref, out_ref, send_sems, recv_sems):
        mx, my, mc, _ = _place()
        me = 4 * mx + 2 * my + mc

        def peer(k):
            return (1 - mx if k & 4 else mx, 1 - my if k & 2 else my, 1 - mc if k & 1 else mc)

        def copy(k, src, slot):
            return pltpu.make_async_remote_copy(src_ref=src, dst_ref=out_ref.at[slot], send_sem=send_sems.at[k - 1],
                                                recv_sem=recv_sems.at[k - 1], device_id=peer(k), device_id_type=MESH)

        sends = [copy(k, x_ref, me) for k in range(1, 8)]
        arrivals = []
        for k in range(1, 8):
            px, py, pc = peer(k)
            slot = 4 * px + 2 * py + pc
            arrivals.append(copy(k, out_ref.at[slot], slot))
        return sends, arrivals, pltpu.make_async_copy(x_ref, out_ref.at[me], send_sems.at[7])

    def start(cin, cout, send_sems, recv_sems):
        sends, _, mine = copies(cin[0], cout[0], send_sems, recv_sems)
        mine.start()
        for cp in sends:
            cp.start()

    def finish(cin, cout, send_sems, recv_sems):
        sends, arrivals, mine = copies(cin[0], cout[0], send_sems, recv_sems)
        for cp in arrivals:
            cp.wait_recv()
        for cp in sends:
            cp.wait_send()
        mine.wait()

    return _Comm([x], [jax.ShapeDtypeStruct((8,) + x.shape, F32)], {}, 8, start, finish)


def _join(a, b):
    na_in, na_out = len(a.operands), len(a.out_shape)

    def split(fn_a, fn_b):
        def both(cin, cout, send_sems, recv_sems):
            fn_a(cin[:na_in], cout[:na_out], send_sems.at[pl.ds(0, a.n_sems)], recv_sems.at[pl.ds(0, a.n_sems)])
            fn_b(cin[na_in:], cout[na_out:], send_sems.at[pl.ds(a.n_sems, b.n_sems)], recv_sems.at[pl.ds(a.n_sems, b.n_sems)])
        return both

    aliases = dict(a.aliases)
    aliases.update({na_in + i: na_out + o for i, o in b.aliases.items()})
    return _Comm(a.operands + b.operands, a.out_shape + b.out_shape, aliases, a.n_sems + b.n_sems, split(a.start, b.start), split(a.finish, b.finish))


def _region(ref, kind, j, half, r, cc):
    nr = r if half is None else r // 2
    off = 0 if half is None else half * nr
    if kind == "col":
        return ref.at[pl.ds(off, nr), pl.ds(pl.multiple_of(j * cc, 128), cc)]
    return ref.at[pl.ds(pl.multiple_of(j * r + off, 16), nr), :]


def comm_call(comm, name):
    ni, no = len(comm.operands), len(comm.out_shape)

    def body(*refs):
        comm.start(refs[:ni], refs[ni:ni + no], *refs[ni + no:])
        comm.finish(refs[:ni], refs[ni:ni + no], *refs[ni + no:])

    return pl.pallas_call(
        body, name=name, out_shape=comm.out_shape, in_specs=[ANY] * ni, out_specs=[ANY] * no, input_output_aliases=comm.aliases,
        scratch_shapes=[pltpu.SemaphoreType.DMA((comm.n_sems,)), pltpu.SemaphoreType.DMA((comm.n_sems,))], compiler_params=_cp(),
    )(*comm.operands)


def gather_comm(fulls, kinds, dims):
    n = len(fulls)

    def copies(f_refs, send_sems, recv_sems):
        mx, my, mc, chips = _place()
        jme = 2 * mx + my

        def landed(w, k, half):
            px, py = chips[k]
            return _region(f_refs[w], kinds[w], 2 * px + py, half, *dims[w])

        def over_ici(w, k, reg):
            px, py = chips[k]
            return pltpu.make_async_remote_copy(src_ref=reg, dst_ref=reg, send_sem=send_sems.at[6 * w + k], recv_sem=recv_sems.at[6 * w + k],
                                                device_id=(px, py, mc), device_id_type=MESH)

        def over_d2d(w, k, half):
            reg = landed(w, k, half)
            return pltpu.make_async_remote_copy(src_ref=reg, dst_ref=reg, send_sem=send_sems.at[6 * w + 3 + k],
                                                recv_sem=recv_sems.at[6 * w + 3 + k], device_id=(mx, my, 1 - mc), device_id_type=MESH)

        sends = [over_ici(w, k, _region(f_refs[w], kinds[w], jme, mc, *dims[w])) for w in range(n) for k in range(3)]
        return mc, landed, over_ici, over_d2d, sends

    def start(cin, f_refs, send_sems, recv_sems):
        for cp in copies(f_refs, send_sems, recv_sems)[4]:
            cp.start()

    def finish(cin, f_refs, send_sems, recv_sems):
        mc, landed, over_ici, over_d2d, sends = copies(f_refs, send_sems, recv_sems)
        passed = []
        for w in range(n):
            for k in range(3):
                over_ici(w, k, landed(w, k, mc)).wait_recv()
                cp = over_d2d(w, k, mc)
                cp.start()
                passed.append(cp)
        for w in range(n):
            for k in range(3):
                over_d2d(w, k, 1 - mc).wait_recv()
        for cp in sends + passed:
            cp.wait_send()

    return _Comm(fulls, [jax.ShapeDtypeStruct(f.shape, BF16) for f in fulls], {w: w for w in range(n)}, 6 * n, start, finish)


def exchange_comm(grads):
    n = len(grads)

    def copies(g_refs, l_refs, send_sems, recv_sems):
        mx, my, mc, _ = _place()
        return [pltpu.make_async_remote_copy(src_ref=g_refs[w].at[:, pl.ds(1 - mc, 1)], dst_ref=l_refs[w], send_sem=send_sems.at[w],
                                             recv_sem=recv_sems.at[w], device_id=(mx, my, 1 - mc), device_id_type=MESH) for w in range(n)]

    def start(*refs):
        for cp in copies(*refs):
            cp.start()

    def finish(*refs):
        for cp in copies(*refs):
            cp.wait()

    return _Comm(grads, [jax.ShapeDtypeStruct((g.shape[0], 1) + g.shape[2:], BF16) for g in grads], {}, n, start, finish)


def exchange_halves(grads, name):
    return comm_call(exchange_comm(grads), name)


def scatter_comm(parts):
    n = len(parts)

    def sends(p_refs, l_refs, send_sems, recv_sems):
        mx, my, mc, chips = _place()
        return [pltpu.make_async_remote_copy(src_ref=p_refs[w].at[2 * px + py], dst_ref=l_refs[w].at[2 * mx + my],
                                             send_sem=send_sems.at[3 * w + k], recv_sem=recv_sems.at[3 * w + k],
                                             device_id=(px, py, mc), device_id_type=MESH) for w in range(n) for k, (px, py) in enumerate(chips)]

    def start(p_refs, l_refs, send_sems, recv_sems):
        for cp in sends(p_refs, l_refs, send_sems, recv_sems):
            cp.start()

    def finish(p_refs, l_refs, send_sems, recv_sems):
        mx, my, mc, chips = _place()
        for w in range(n):
            for k, (px, py) in enumerate(chips):
                slot = l_refs[w].at[2 * px + py]
                pltpu.make_async_remote_copy(src_ref=slot, dst_ref=slot, send_sem=send_sems.at[3 * w + k], recv_sem=recv_sems.at[3 * w + k],
                                             device_id=(px, py, mc), device_id_type=MESH).wait_recv()
        for cp in sends(p_refs, l_refs, send_sems, recv_sems):
            cp.wait_send()

    return _Comm(parts, [jax.ShapeDtypeStruct(p.shape, BF16) for p in parts], {}, 3 * n, start, finish)


def share_comm(sums):
    n = len(sums)

    def copies(q_refs, o_refs, send_sems, recv_sems):
        mx, my, mc, _ = _place()
        return [pltpu.make_async_remote_copy(src_ref=q_refs[w], dst_ref=o_refs[w], send_sem=send_sems.at[w], recv_sem=recv_sems.at[w],
                                             device_id=(mx, my, 1 - mc), device_id_type=MESH) for w in range(n)]

    def start(*refs):
        for cp in copies(*refs):
            cp.start()

    def finish(*refs):
        for cp in copies(*refs):
            cp.wait()

    return _Comm(sums, [jax.ShapeDtypeStruct(q.shape, F32) for q in sums], {}, n, start, finish)


def _pack(arrays):
    flat = jnp.concatenate([a.reshape(-1) for a in arrays])
    rows = -(-flat.shape[0] // 1024) * 8
    return jnp.pad(flat, (0, rows * 128 - flat.shape[0])).reshape(rows, 128)


def _unpack(packed, shapes):
    flat, out, off = packed.reshape(-1), [], 0
    for s in shapes:
        n = math.prod(s)
        out.append(flat[off:off + n].reshape(s))
        off += n
    return out


def kernel(x, c, w_ada, b_ada, g_pre_mix, g_post_mix, g_pre_ffn, g_post_ffn, w_in, lb_logits, g_hgrn_norm, w_a_out, g_sgu_norm, w_spatial, b_spatial, w_b_out, w_o, w_ff1, w_ff2, loss_target, m_w_ada, m_b_ada, m_g_pre_mix, m_g_post_mix, m_g_pre_ffn, m_g_post_ffn, m_w_in, m_lb_logits, m_g_hgrn_norm, m_w_a_out, m_g_sgu_norm, m_w_spatial, m_b_spatial, m_w_b_out, m_w_o, m_w_ff1, m_w_ff2, v_w_ada, v_b_ada, v_g_pre_mix, v_g_post_mix, v_g_pre_ffn, v_g_post_ffn, v_w_in, v_lb_logits, v_g_hgrn_norm, v_w_a_out, v_g_sgu_norm, v_w_spatial, v_b_spatial, v_w_b_out, v_w_o, v_w_ff1, v_w_ff2):
    mx, my, mc = lax.axis_index("x"), lax.axis_index("y"), lax.axis_index("c")
    chip, me = 2 * mx + my, 4 * mx + 2 * my + mc
    D = D_MODEL
    h0, tgt = x[0], loss_target[0]
    n_ada = w_ada.shape[2]
    n_lb = lb_logits.shape[2]

    got = all_gather_small(_pack([c, lb_logits]), "gather_inputs")
    c_all = got[:, :D // 128, :].reshape(8, D)
    lb_full = got[0::2, D // 128:D // 128 + 4 * n_lb // 128, :].reshape(4, 2, 2, n_lb).transpose(1, 2, 0, 3).reshape(2, 2, 4 * n_lb)
    b_ada_chip = lax.dynamic_slice(b_ada, (0, chip * n_ada), (1, n_ada))
    mod_cols = mod_matmul(c_all, w_ada[0], b_ada_chip)
    got = all_gather_small(mod_cols.reshape(-1, 128), "gather_mod").reshape(4, 2, 8, n_ada)
    mod = lax.dynamic_index_in_dim(got[:, 0], me, axis=1, keepdims=False).reshape(6, 1, D)
    sh1, sc1, gt1, sh2, sc2, gt2 = (mod[i] for i in range(6))

    big = [("w_in", w_in, "col"), ("w_a_out", w_a_out, "col"), ("w_b_out", w_b_out, "col"), ("w_o", w_o, "row"),
           ("w_ff1", w_ff1, "col"), ("w_ff2", w_ff2, "row")]
    kinds = [k for _, _, k in big]
    chip_idx, core = chip.reshape(1).astype(jnp.int32), mc.reshape(1).astype(jnp.int32)
    fulls = [cast_into_full(w[0], kind, chip_idx, "cast_" + nm) for nm, w, kind in big]
    dims = [w.shape[1:] for _, w, _ in big]
    later = lambda lo, hi: gather_comm(fulls[lo:hi], kinds[lo:hi], dims[lo:hi])
    halves_summed = lambda grads, name: [add_halves(g, l, core) for g, l in zip(grads, exchange_halves(grads, name))]

    bst = b_spatial[0].T
    a1 = prenorm(h0, g_pre_mix, sc1, sh1)
    proj, w_in_f, (w_a_f, w_b_f, w_o_f) = in_proj_gathered(a1, fulls[0], chip_idx, dims[0], later(1, 4))
    o, (w_ff1_f,) = hgrn_fwd(proj, lb_full, comm=later(4, 5))
    ya_pre = hgrn_post_fwd(o, proj, g_hgrn_norm)
    sgu = sgu_fwd(proj, g_sgu_norm, w_spatial[0], bst)
    y_a, y_b, merged = merge_matmul(ya_pre, sgu, w_a_f, w_b_f, proj)
    mo, h1 = out_proj(merged, w_o_f, h0, gt1, g_post_mix)
    (f1, a2, hid), (w_ff2_f,) = prenorm_matmul(h1, g_pre_ffn, sc2, sh2, w_ff1_f, relu2=True, name="ff1", comm=later(5, 6))
    dy, dff, loss_parts, d_gt2, d_g_post_ffn = ff2_loss(hid, w_ff2_f, h1, tgt, gt2, g_post_ffn)
    loss = lax.psum(0.5 * loss_parts[0, 0] / D, ("x", "y", "c"))

    df1 = ff2_bwd(dff, w_ff2_f, f1)
    gr_ff2 = matmul(hid, dff, mode="tn", out_dtype=BF16, tm=1024, tn=1024, tk=2048, name="dw_ff2")
    gr_ff2 = gr_ff2.reshape(4, 2, -1, D)
    da2, (landed_ff2,) = matmul(df1, w_ff1_f, mode="nt", out_dtype=F32, tm=1024, tn=1024, tk=2048, name="da2", comm=exchange_comm([gr_ff2]))
    gr_ff1 = matmul(a2, df1, mode="tn", out_dtype=BF16, tm=1024, tn=2048, tk=1024, name="dw_ff1", split=(4, 2))
    (dh1, dmo, d_sh2, d_sc2, d_g_pre_ffn, d_gt1, d_g_post_mix), (landed_ff1,) = ffn_norm_bwd(
        dy, da2, h1, mo, g_pre_ffn, sc2, gt1, g_post_mix, exchange_comm([gr_ff1]))
    parts_ff = [add_halves(gr_ff1, landed_ff1, core), add_halves(gr_ff2, landed_ff2, core)]
    dya, dyb, dga, dgb = out_proj_bwd(dmo, w_o_f, y_a, y_b, proj)
    gr_o = matmul(merged, dmo, mode="tn", out_dtype=BF16, tm=1024, tn=1024, tk=2048, name="dw_o")
    dsgu = matmul(dyb, w_b_f, mode="nt", out_dtype=F32, tm=512, tn=1024, tk=2048, name="dsgu")
    gr_b = matmul(sgu, dyb, mode="tn", out_dtype=BF16, tm=512, tn=512, tk=4096, name="dw_b_out", split=(4, 2))
    dz, d_w_spatial, d_b_spatial, d_g_sgu = sgu_bwd(proj, dsgu, g_sgu_norm, w_spatial[0], bst)
    dya_pre = matmul(dya, w_a_f, mode="nt", out_dtype=F32, tm=512, tn=1024, tk=2048, name="dya_pre")
    gr_a = matmul(ya_pre, dya, mode="tn", out_dtype=BF16, tm=512, tn=512, tk=4096, name="dw_a_out", split=(4, 2))
    parts_mix = halves_summed([gr_a, gr_b, gr_o.reshape(4, 2, -1, D)], "exchange_mix")
    do, dog, d_g_hgrn = hgrn_post_bwd(dya_pre, o, proj, g_hgrn_norm)
    chips_summed = lambda parts, landed: [sum_chips(p, l, chip_idx) for p, l in zip(parts, landed)]
    (dq, dv, dlg, d_lb), landed_ff = hgrn_bwd(proj, do, lb_full, comm=scatter_comm(parts_ff))
    own_ff = chips_summed(parts_ff, landed_ff)
    dproj = jnp.concatenate([dq, dlg, dv, dog, dz, dga, dgb], axis=1)
    early = _pack([d_g_sgu, d_w_spatial, d_b_spatial[:, 0, :]])
    gr_in, (*landed_mix, got_early) = matmul(a1, dproj, mode="tn", out_dtype=BF16, tm=1024, tn=2816, tk=1024, name="dw_in", split=(4, 2),
                                             comm=_join(scatter_comm(parts_mix), gather8_comm(early)))
    own_mix = chips_summed(parts_mix, landed_mix)
    parts_in = halves_summed([gr_in], "exchange_in")
    da1, (landed_in, *other_rest) = matmul(dproj, w_in_f, mode="nt", out_dtype=F32, tm=1024, tn=1024, tk=2816, name="da1",
                                           comm=_join(scatter_comm(parts_in), share_comm(own_mix + own_ff)))
    own_in = chips_summed(parts_in, [landed_in])
    other_in = comm_call(share_comm(own_in), "share_w_in")
    own, other = own_in + own_mix + own_ff, list(other_in) + other_rest
    grad_x, d_sh1, d_sc1, d_g_pre_mix = mix_norm_bwd(da1, h0, dh1, g_pre_mix, sc1)
    out = {}

    mine = _pack([d_sh1, d_sc1, d_gt1, d_sh2, d_sc2, d_gt2, d_g_pre_mix, d_g_post_mix, d_g_pre_ffn, d_g_post_ffn, d_g_hgrn, d_lb])
    got = all_gather_small(mine, "gather_small_grads")
    g_b_ada, g_g1, g_g2, g_g3, g_g4, g_hg, g_lb = _unpack(
        sum_devices(got, "sum_small_grads"), [(1, 6 * D), (1, D), (1, D), (1, D), (1, D), (1, HEAD_DIM), (2, 1024)])
    g_sg, g_ws, g_bs = _unpack(sum_devices(got_early, "sum_sgu_grads"), [(1, 1024), w_spatial.shape, b_spatial.shape])
    g_lbl = lax.dynamic_slice(lb_logits_grad(g_lb, lb_full), (0, 0, chip * n_lb), (2, 2, n_lb))
    names = ["b_ada", "g_pre_mix", "g_post_mix", "g_pre_ffn", "g_post_ffn", "g_hgrn_norm", "g_sgu_norm", "w_spatial", "b_spatial", "lb_logits"]
    ws = [b_ada, g_pre_mix, g_post_mix, g_pre_ffn, g_post_ffn, g_hgrn_norm, g_sgu_norm, w_spatial, b_spatial, lb_logits]
    gs = [g_b_ada, g_g1, g_g2, g_g3, g_g4, g_hg, g_sg, g_ws, g_bs, g_lbl]
    ms = [m_b_ada, m_g_pre_mix, m_g_post_mix, m_g_pre_ffn, m_g_post_ffn, m_g_hgrn_norm, m_g_sgu_norm, m_w_spatial, m_b_spatial, m_lb_logits]
    vs = [v_b_ada, v_g_pre_mix, v_g_post_mix, v_g_pre_ffn, v_g_post_ffn, v_g_hgrn_norm, v_g_sgu_norm, v_w_spatial, v_b_spatial, v_lb_logits]
    shapes = [w.shape for w in ws]
    upd = adamw(_pack(ws), _pack(gs), _pack(ms), _pack(vs), "adamw_small")
    upd = [_unpack(u, shapes) for u in upd]
    for i, nm in enumerate(names):
        out[nm] = (gs[i], upd[0][i], upd[1][i], upd[2][i])

    dmod_all = got[:, :6 * D // 128, :].reshape(8, 6 * D)
    dmod_chip = lax.dynamic_slice(dmod_all, (0, chip * n_ada), (8, n_ada))
    out["w_ada"] = tuple(a[None] for a in wada_update(c_all, dmod_chip, w_ada[0], m_w_ada[0], v_w_ada[0]))
    for (nm, w, _), a, b, m, v in zip(big, own, other, (m_w_in, m_w_a_out, m_w_b_out, m_w_o, m_w_ff1, m_w_ff2),
                                      (v_w_in, v_w_a_out, v_w_b_out, v_w_o, v_w_ff1, v_w_ff2)):
        out[nm] = tuple(t[None] for t in adamw_halves(w[0], a, b, m[0], v[0], core, "adamw_" + nm))

    order = ["w_ada", "b_ada", "g_pre_mix", "g_post_mix", "g_pre_ffn", "g_post_ffn", "w_in", "lb_logits", "g_hgrn_norm", "w_a_out",
             "g_sgu_norm", "w_spatial", "b_spatial", "w_b_out", "w_o", "w_ff1", "w_ff2"]
    return (loss, grad_x[None], *[out[nm][0] for nm in order], *[out[nm][1] for nm in order], *[out[nm][2] for nm in order],
            *[out[nm][3] for nm in order])
```

```python
import functools
import math

import jax
import jax.numpy as jnp
from jax import lax
from jax.experimental import pallas as pl
from jax.experimental.pallas import tpu as pltpu

F32, BF16 = jnp.float32, jnp.bfloat16
HI = lax.Precision.HIGHEST
MESH = pl.DeviceIdType.MESH
ANY = pl.BlockSpec(memory_space=pl.ANY)

EPS = 1e-6
D_MODEL = 2048
N_HEADS = 8
HEAD_DIM = 128
HGRN_CHUNK = 32
HGRN_BLOCK = 256
SGU_CHUNK = 128
SGU_GROUPS = 8
Q_SCALE = HEAD_DIM ** -0.5
COL_Q, COL_FFW, COL_FBW, COL_V, COL_OG, COL_U, COL_ZV, COL_GA, COL_GB = 0, 1, 2, 3, 4, 5, 6, 7, 9
N_PROJ = 11264
VMEM_BYTES_V7X = 64 * 1024 * 1024
VMEM_LIMIT = VMEM_BYTES_V7X - 8 * 1024 * 1024

ADAM_LR, ADAM_B1, ADAM_B2, ADAM_EPS, ADAM_WD, ADAM_STEP = 0.001, 0.9, 0.999, 1e-08, 0.01, 10
ADAM_C1 = 1.0 - ADAM_B1 ** ADAM_STEP
ADAM_C2 = 1.0 - ADAM_B2 ** ADAM_STEP


def _cp(*sem):
    return pltpu.CompilerParams(dimension_semantics=sem if sem else None, vmem_limit_bytes=VMEM_LIMIT)


def _vec(d):
    return pl.BlockSpec((1, d), lambda *_: (0, 0))


def _colsum(x):
    return jnp.sum(x, axis=0, keepdims=True)


def _nt(a, b):
    return lax.dot_general(a, b, (((1,), (1,)), ((), ())), preferred_element_type=F32)


def _tn(a, b):
    return lax.dot_general(a, b, (((0,), (0,)), ((), ())), preferred_element_type=F32)


def _nn(a, b):
    return jnp.dot(a, b, preferred_element_type=F32)


def _adamw(w, g, m, v):
    m2 = ADAM_B1 * m + (1.0 - ADAM_B1) * g
    v2 = ADAM_B2 * v + (1.0 - ADAM_B2) * (g * g)
    delta = -ADAM_LR * ((m2 / ADAM_C1) / (jnp.sqrt(v2 / ADAM_C2) + ADAM_EPS) + ADAM_WD * w)
    return delta, m2, v2


class _Comm:
    def __init__(self, operands, out_shape, aliases, n_sems, start, finish):
        self.operands, self.out_shape, self.aliases, self.n_sems = list(operands), list(out_shape), dict(aliases), n_sems
        self.start, self.finish = start, finish


def _pallas(body, *, name, grid, in_specs, out_specs, out_shape, scratch, semantics, operands, comm=None):
    if comm is None:
        res = pl.pallas_call(body, name=name, grid=grid, in_specs=in_specs, out_specs=out_specs, out_shape=out_shape,
                             scratch_shapes=scratch, compiler_params=_cp(*semantics))(*operands)
        return res, []
    n_in, n_out, n_scr = len(in_specs), len(out_specs), len(scratch)
    nci, nco = len(comm.operands), len(comm.out_shape)

    def with_comm(*refs):
        ins, rest = refs[:n_in], refs[n_in:]
        cin, rest = rest[:nci], rest[nci:]
        outs, rest = rest[:n_out], rest[n_out:]
        cout, rest = rest[:nco], rest[nco:]
        scr, (send, recv) = rest[:n_scr], rest[n_scr:]
        ids = [pl.program_id(a) for a in range(len(grid))]
        first = functools.reduce(jnp.logical_and, [i == 0 for i in ids])
        last = functools.reduce(jnp.logical_and, [i == g - 1 for i, g in zip(ids, grid)])

        @pl.when(first)
        def _():
            comm.start(cin, cout, send, recv)

        body(*ins, *outs, *scr)

        @pl.when(last)
        def _():
            comm.finish(cin, cout, send, recv)

    res = pl.pallas_call(
        with_comm, name=name, grid=grid, in_specs=list(in_specs) + [ANY] * nci, out_specs=list(out_specs) + [ANY] * nco,
        out_shape=list(out_shape) + comm.out_shape, input_output_aliases={n_in + i: n_out + o for i, o in comm.aliases.items()},
        scratch_shapes=list(scratch) + [pltpu.SemaphoreType.DMA((comm.n_sems,)), pltpu.SemaphoreType.DMA((comm.n_sems,))],
        compiler_params=_cp(*["arbitrary"] * len(grid)),
    )(*operands, *comm.operands)
    return res[:n_out], res[n_out:]


def matmul(a, b, *, mode, out_dtype, tm, tn, tk, name, split=None, comm=None, b_stacked=False):
    if mode == "tn":
        (K, M), (_, N) = a.shape, b.shape
    elif mode == "nt":
        (M, K), (N, _) = a.shape, b.shape
        N = N // 4 if b_stacked else N
    else:
        (M, K), (_, N) = a.shape, b.shape
    tm, tn, tk = min(tm, M), min(tn, N), min(tk, K)
    if b_stacked:
        tk = K // 4
    nk = K // tk
    a_spec = pl.BlockSpec((tk, tm), lambda i, j, k: (k, i)) if mode == "tn" else pl.BlockSpec((tm, tk), lambda i, j, k: (i, k))
    b_spec = pl.BlockSpec((tn, tk), lambda i, j, k: (j, k)) if mode == "nt" else pl.BlockSpec((tk, tn), lambda i, j, k: (k, j))
    if b_stacked:
        b_spec = pl.BlockSpec((tn, tk), lambda i, j, k: (k * (N // tn) + j, 0))
    dot = {"nn": _nn, "nt": _nt, "tn": _tn}[mode]
    if split is None:
        out_shape = jax.ShapeDtypeStruct((M, N), out_dtype)
        out_spec = pl.BlockSpec((tm, tn), lambda i, j, k: (i, j))
    else:
        nj, nh = split
        rows, cols = M // nh, N // nj
        tm, tn = min(tm, rows), min(tn, cols)
        bi, bj = rows // tm, cols // tn
        out_shape = jax.ShapeDtypeStruct((nj, nh, rows, cols), out_dtype)
        out_spec = pl.BlockSpec((None, None, tm, tn), lambda i, j, k: (j // bj, i // bi, i % bi, j % bj))

    def body(a_ref, b_ref, o_ref, acc_ref):
        k = pl.program_id(2)

        @pl.when(k == 0)
        def _():
            acc_ref[...] = jnp.zeros_like(acc_ref)

        acc_ref[...] += dot(a_ref[...], b_ref[...])

        @pl.when(k == nk - 1)
        def _():
            o_ref[...] = acc_ref[...].astype(o_ref.dtype)

    (out,), landed = _pallas(
        body, name=name, grid=(M // tm, N // tn, nk), in_specs=[a_spec, b_spec], out_specs=[out_spec], out_shape=[out_shape],
        scratch=[pltpu.VMEM((tm, tn), F32)], semantics=("parallel", "parallel", "arbitrary"), operands=(a, b), comm=comm)
    return out if comm is None else (out, landed)


def cast_into_full(w, kind, chip, name):
    r, cc = w.shape
    tr = min(r, 512)
    nb = r // tr

    def body(chip_ref, w_ref, o_ref):
        o_ref[...] = w_ref[...].astype(BF16)

    if kind == "col":
        full, out_map = (r, 4 * cc), lambda i, chip_ref: (i, chip_ref[0])
    else:
        full, out_map = (4 * r, cc), lambda i, chip_ref: (chip_ref[0] * nb + i, 0)
    return pl.pallas_call(
        body, name=name, out_shape=jax.ShapeDtypeStruct(full, BF16),
        grid_spec=pltpu.PrefetchScalarGridSpec(
            num_scalar_prefetch=1, grid=(nb,), in_specs=[pl.BlockSpec((tr, cc), lambda i, chip_ref: (i, 0))],
            out_specs=pl.BlockSpec((tr, cc), out_map)),
        compiler_params=_cp("parallel"),
    )(chip, w)


def mod_matmul(c_all, w_ada, b_ada):
    D, N = w_ada.shape
    tn = 1024

    def body(c_ref, w_ref, b_ref, o_ref):
        c = c_ref[...]
        sc = c * jax.nn.sigmoid(c)
        o_ref[...] = jnp.dot(sc, w_ref[...], precision=HI, preferred_element_type=F32) + b_ref[...]

    return pl.pallas_call(
        body, name="mod_matmul", out_shape=jax.ShapeDtypeStruct((8, N), F32), grid=(N // tn,),
        in_specs=[pl.BlockSpec((8, D), lambda j: (0, 0)), pl.BlockSpec((D, tn), lambda j: (0, j)),
                  pl.BlockSpec((1, tn), lambda j: (0, j))],
        out_specs=pl.BlockSpec((8, tn), lambda j: (0, j)), compiler_params=_cp("parallel"),
    )(c_all, w_ada, b_ada)


def prenorm(h, g, sc, sh):
    T, D = h.shape
    tm = min(256, T)

    def body(h_ref, g_ref, sc_ref, sh_ref, a_ref):
        x = h_ref[...]
        r = lax.rsqrt(jnp.mean(x * x, axis=-1, keepdims=True) + EPS)
        a_ref[...] = ((x * r) * g_ref[...] * (1.0 + sc_ref[...]) + sh_ref[...]).astype(BF16)

    row = pl.BlockSpec((tm, D), lambda i: (i, 0))
    return pl.pallas_call(
        body, name="prenorm", out_shape=jax.ShapeDtypeStruct((T, D), BF16), grid=(T // tm,),
        in_specs=[row, _vec(D), _vec(D), _vec(D)], out_specs=row, compiler_params=_cp("parallel"),
    )(h, g, sc, sh)


def in_proj_gathered(a, w_full, chip, dims, tail):
    T, D = a.shape
    rows, cc = dims
    tm, tn = min(512, T), cc // 2
    ni = T // tm
    half = rows // 2

    nt = len(tail.operands)

    def body(chip_ref, a_ref, w_in_ref, *rest):
        tail_in, (y_ref, w_ref), rest = rest[:nt], rest[nt:nt + 2], rest[nt + 2:]
        tail_out, (wbuf, wsem, send_sems, recv_sems, tail_send, tail_recv) = rest[:nt], rest[nt:]
        q, j, i = pl.program_id(0), pl.program_id(1), pl.program_id(2)
        mx, my, mc, _ = _place()
        me = chip_ref[0]

        def tile(block, jj):
            src = w_ref.at[:, pl.ds(pl.multiple_of(block * cc + jj * tn, 128), tn)]
            return pltpu.make_async_copy(src, wbuf.at[jj], wsem.at[jj])

        def rows_half(block, hh):
            return w_ref.at[pl.ds(pl.multiple_of(hh * half, 16), half), pl.ds(pl.multiple_of(block * cc, 128), cc)]

        def over_ici(s, block):
            peer = (1 - mx if s & 2 else mx, 1 - my if s & 1 else my, mc)
            reg = rows_half(block, mc)
            return pltpu.make_async_remote_copy(src_ref=reg, dst_ref=reg, send_sem=send_sems.at[s - 1], recv_sem=recv_sems.at[s - 1],
                                                device_id=peer, device_id_type=MESH)

        def over_d2d(s, block, hh):
            reg = rows_half(block, hh)
            return pltpu.make_async_remote_copy(src_ref=reg, dst_ref=reg, send_sem=send_sems.at[2 + s], recv_sem=recv_sems.at[2 + s],
                                                device_id=(mx, my, 1 - mc), device_id_type=MESH)

        def passed_on():
            reg = rows_half(me ^ 1, mc)
            return pltpu.make_async_remote_copy(src_ref=reg, dst_ref=reg, send_sem=send_sems.at[2], recv_sem=recv_sems.at[2],
                                                device_id=(1 - mx, my, mc), device_id_type=MESH)

        @pl.when((q == 0) & (j == 0) & (i == 0))
        def _():
            for s in (1, 2):
                over_ici(s, me).start()
            tile(me, 0).start()

        @pl.when(i == 0)
        def _():
            tile(me ^ q, j).wait()

        @pl.when((i == 0) & (j == 0))
        def _():
            tile(me ^ q, 1).start()

        y_ref[...] = _nn(a_ref[...], wbuf[j])

        for s in (1, 2, 3):
            @pl.when((q == s - 1) & (j == 1) & (i == ni - 1))
            def _():
                block = me ^ s
                over_ici(s, block).wait_recv()
                if s == 1:
                    passed_on().start()
                    tail.start(tail_in, tail_out, tail_send, tail_recv)
                over_d2d(s, block, mc).start()
                over_d2d(s, block, 1 - mc).wait_recv()
                tile(block, 0).start()

        @pl.when((q == 3) & (j == 1) & (i == ni - 1))
        def _():
            for s in (1, 2):
                over_ici(s, me).wait_send()
            passed_on().wait_send()
            for s in (1, 2, 3):
                over_d2d(s, me ^ s, mc).wait_send()
            tail.finish(tail_in, tail_out, tail_send, tail_recv)

    dma = pltpu.SemaphoreType.DMA
    y, w_out, *tail_res = pl.pallas_call(
        body, name="in_proj", out_shape=[jax.ShapeDtypeStruct((T, 4 * cc), F32), jax.ShapeDtypeStruct(w_full.shape, BF16)] + tail.out_shape,
        grid_spec=pltpu.PrefetchScalarGridSpec(
            num_scalar_prefetch=1, grid=(4, 2, ni),
            in_specs=[pl.BlockSpec((tm, D), lambda q, j, i, chip_ref: (i, 0)), ANY] + [ANY] * nt,
            out_specs=[pl.BlockSpec((tm, tn), lambda q, j, i, chip_ref: (i, (chip_ref[0] ^ q) * 2 + j)), ANY] + [ANY] * nt,
            scratch_shapes=[pltpu.VMEM((2, D, tn), BF16), dma((2,)), dma((6,)), dma((6,)), dma((tail.n_sems,)), dma((tail.n_sems,))]),
        input_output_aliases={2: 1, **{3 + i: 2 + o for i, o in tail.aliases.items()}},
        compiler_params=_cp("arbitrary", "arbitrary", "arbitrary"),
    )(chip, a, w_full, *tail.operands)
    return y, w_out, tail_res


def prenorm_matmul(h, g, sc, sh, w, *, relu2, name, comm=None):
    T, D = h.shape
    N = w.shape[1]
    tm, tn = min(512, T), 2048 if N % 2048 == 0 else 1024

    def body(h_ref, g_ref, sc_ref, sh_ref, w_ref, y_ref, a_ref, *hid_ref):
        @pl.when(pl.program_id(1) == 0)
        def _():
            x = h_ref[...]
            r = lax.rsqrt(jnp.mean(x * x, axis=-1, keepdims=True) + EPS)
            a_ref[...] = ((x * r) * g_ref[...] * (1.0 + sc_ref[...]) + sh_ref[...]).astype(BF16)

        y = _nn(a_ref[...], w_ref[...])
        y_ref[...] = y.astype(y_ref.dtype)
        if relu2:
            p = jnp.maximum(y, 0.0)
            hid_ref[0][...] = (p * p).astype(BF16)

    out_shape = [jax.ShapeDtypeStruct((T, N), BF16 if relu2 else F32), jax.ShapeDtypeStruct((T, D), BF16)]
    out_specs = [pl.BlockSpec((tm, tn), lambda i, j: (i, j)), pl.BlockSpec((tm, D), lambda i, j: (i, 0))]
    if relu2:
        out_shape.append(jax.ShapeDtypeStruct((T, N), BF16))
        out_specs.append(pl.BlockSpec((tm, tn), lambda i, j: (i, j)))
    outs, landed = _pallas(
        body, name=name, grid=(T // tm, N // tn),
        in_specs=[pl.BlockSpec((tm, D), lambda i, j: (i, 0)), _vec(D), _vec(D), _vec(D), pl.BlockSpec((D, tn), lambda i, j: (0, j))],
        out_specs=out_specs, out_shape=out_shape, scratch=[], semantics=("parallel", "arbitrary"), operands=(h, g, sc, sh, w), comm=comm)
    return outs if comm is None else (outs, landed)


def _hgrn_lower_bound(l_ref):
    l0, l1 = l_ref[0:1, :], l_ref[1:2, :]
    m = jnp.maximum(l0, l1)
    e0, e1 = jnp.exp(l0 - m), jnp.exp(l1 - m)
    return e0 / (e0 + e1)


def _hgrn_chunk_mask(d):
    r = lax.broadcasted_iota(jnp.int32, (HGRN_BLOCK, HGRN_BLOCK), 0)
    c = lax.broadcasted_iota(jnp.int32, (HGRN_BLOCK, HGRN_BLOCK), 1)
    same = (r // HGRN_CHUNK) == (c // HGRN_CHUNK)
    fwd = d == 0
    return same & (((c <= r) & fwd) | ((c >= r) & jnp.logical_not(fwd)))


def _chunk_total(x):
    x3 = x.reshape(HGRN_BLOCK // HGRN_CHUNK, HGRN_CHUNK, x.shape[1])
    return jnp.broadcast_to(jnp.sum(x3, axis=1, keepdims=True), x3.shape).reshape(x.shape)


def _chunk_cumsum(x, suffix):
    pos = lax.broadcasted_iota(jnp.int32, x.shape, 0) % HGRN_CHUNK
    p, s = x, 1
    while s < HGRN_CHUNK:
        p = p + jnp.where(pos >= s, pltpu.roll(p, s, 0), 0.0)
        s *= 2
    return jnp.where(suffix, _chunk_total(x) - p + x, p)


def _block_loop(T, body, init):
    n = T // HGRN_BLOCK
    return lax.fori_loop(0, n, body, init, unroll=2 if n % 2 == 0 else 1)


def _hgrn_gate(f, lb):
    s = jax.nn.sigmoid(f)
    sn = jax.nn.sigmoid(-f)
    fg = lb + (1.0 - lb) * s
    return s, sn, fg, jnp.log(fg), (1.0 - lb) * sn


def _hgrn_specs(T):
    col = lambda base: pl.BlockSpec((T, HEAD_DIM), lambda h, d: (0, base * N_HEADS + h))
    f_spec = pl.BlockSpec((T, HEAD_DIM), lambda h, d: (0, COL_FFW * N_HEADS + N_HEADS * d + h))
    l_spec = pl.BlockSpec((None, 2, HEAD_DIM), lambda h, d: (d, 0, h))
    return col, f_spec, l_spec


def hgrn_fwd(proj, lb_logits, comm=None):
    T = proj.shape[0]
    NC, CPB = T // HGRN_CHUNK, HGRN_BLOCK // HGRN_CHUNK
    col, f_spec, l_spec = _hgrn_specs(T)

    def body(l_ref, q_ref, f_ref, v_ref, o_ref, st_ref, dec_ref, qd_ref):
        d = pl.program_id(1)
        lb = _hgrn_lower_bound(l_ref)
        mask = _hgrn_chunk_mask(d)

        def block(i, carry):
            rows = pl.ds(pl.multiple_of(i * HGRN_BLOCK, HGRN_BLOCK), HGRN_BLOCK)
            _, _, _, lf, k = _hgrn_gate(f_ref[rows, :], lb)
            b = _chunk_cumsum(lf, d == 1)
            bl = _chunk_total(lf)
            qd = (q_ref[rows, :] * Q_SCALE * jnp.exp(b)).astype(BF16)
            kd = (k * jnp.exp(-b)).astype(BF16)
            ke = (k * jnp.exp(bl - b)).astype(BF16)
            vb = v_ref[rows, :].astype(BF16)
            att = jnp.where(mask, _nt(qd, kd), 0.0).astype(BF16)
            o_ref[rows, :] = jnp.where(d == 0, 0.0, o_ref[rows, :]) + _nn(att, vb)
            qd_ref[rows, :] = qd
            dec = jnp.exp(bl)
            for cc in range(CPB):
                sl = slice(cc * HGRN_CHUNK, (cc + 1) * HGRN_CHUNK)
                n = i * CPB + cc
                st_ref[n] = _tn(vb[sl], ke[sl])
                dec_ref[n] = dec[cc * HGRN_CHUNK:cc * HGRN_CHUNK + 8, :]
            return carry

        _block_loop(T, block, 0)

        def scan(t, s):
            n = jnp.where(d == 0, t, NC - 1 - t)
            u = st_ref[n]
            st_ref[n] = s
            return dec_ref[n][0:1, :] * s + u

        lax.fori_loop(0, NC, scan, jnp.zeros((HEAD_DIM, HEAD_DIM), F32))

        def inter(i, carry):
            rows = pl.ds(pl.multiple_of(i * HGRN_BLOCK, HGRN_BLOCK), HGRN_BLOCK)
            qd = qd_ref[rows, :]
            o_ref[rows, :] += jnp.concatenate(
                [_nt(qd[cc * HGRN_CHUNK:(cc + 1) * HGRN_CHUNK], st_ref[i * CPB + cc].astype(BF16)) for cc in range(CPB)], axis=0)
            return carry

        _block_loop(T, inter, 0)

    (o,), landed = _pallas(
        body, name="hgrn_fwd", grid=(N_HEADS, 2), in_specs=[l_spec, col(COL_Q), f_spec, col(COL_V)],
        out_specs=[pl.BlockSpec((T, HEAD_DIM), lambda h, d: (0, h))], out_shape=[jax.ShapeDtypeStruct((T, N_HEADS * HEAD_DIM), F32)],
        scratch=[pltpu.VMEM((NC, HEAD_DIM, HEAD_DIM), F32), pltpu.VMEM((NC, 8, HEAD_DIM), F32), pltpu.VMEM((T, HEAD_DIM), BF16)],
        semantics=("parallel", "arbitrary"), operands=(lb_logits, proj, proj, proj), comm=comm)
    return o if comm is None else (o, landed)


def hgrn_post_fwd(o, proj, g_norm):
    T, W = o.shape
    tm = min(256, T)

    def body(o_ref, og_ref, g_ref, y_ref):
        g = g_ref[...]
        for h in range(N_HEADS):
            sl = slice(h * HEAD_DIM, (h + 1) * HEAD_DIM)
            x = o_ref[:, sl]
            r = lax.rsqrt(jnp.mean(x * x, axis=-1, keepdims=True) + EPS)
            og = og_ref[:, sl]
            y_ref[:, sl] = ((x * r) * g * (og * jax.nn.sigmoid(og))).astype(BF16)

    return pl.pallas_call(
        body, name="hgrn_post_fwd", out_shape=jax.ShapeDtypeStruct((T, W), BF16), grid=(T // tm,),
        in_specs=[pl.BlockSpec((tm, W), lambda i: (i, 0)), pl.BlockSpec((tm, W), lambda i: (i, COL_OG)), _vec(HEAD_DIM)],
        out_specs=pl.BlockSpec((tm, W), lambda i: (i, 0)), compiler_params=_cp("parallel"),
    )(o, proj, g_norm)


def _gelu(x):
    return 0.5 * x * (1.0 + lax.erf(x * (1.0 / math.sqrt(2.0))))


def _gelu_grad(x):
    return 0.5 * (1.0 + lax.erf(x * (1.0 / math.sqrt(2.0)))) + x * jnp.exp(-0.5 * x * x) * (1.0 / math.sqrt(2.0 * math.pi))


def _sgu_mix(u_ref, v_ref, g_ref, ws_ref, bst_ref):
    W = u_ref.shape[1]
    zu, zv = _gelu(u_ref[...]), _gelu(v_ref[...])
    dv = zv - jnp.mean(zv, axis=-1, keepdims=True)
    rstd = lax.rsqrt(jnp.mean(dv * dv, axis=-1, keepdims=True) + EPS)
    dhat = dv * rstd
    vn = (dhat * g_ref[...]).astype(BF16)
    gw = W // SGU_GROUPS
    vm = [_nn(ws_ref[g].astype(BF16), vn[:, g * gw:(g + 1) * gw]) + bst_ref[:, g:g + 1] for g in range(SGU_GROUPS)]
    return zu, rstd, dhat, vn, jnp.concatenate(vm, axis=1)


def sgu_fwd(proj, g_norm, w_spatial, b_spatial_t):
    T = proj.shape[0]
    W = 1024
    n_chunks = T // SGU_CHUNK

    def body(u_ref, v_ref, g_ref, ws_ref, bst_ref, y_ref):
        zu, _, _, _, vm = _sgu_mix(u_ref, v_ref, g_ref, ws_ref, bst_ref)
        y_ref[...] = (zu * vm).astype(BF16)

    blk = lambda cb: pl.BlockSpec((SGU_CHUNK, W), lambda i: (i, cb))
    return pl.pallas_call(
        body, name="sgu_fwd", out_shape=jax.ShapeDtypeStruct((T, W), BF16), grid=(n_chunks,),
        in_specs=[blk(COL_U), blk(COL_ZV), _vec(W), pl.BlockSpec((SGU_GROUPS, SGU_CHUNK, SGU_CHUNK), lambda i: (0, 0, 0)),
                  pl.BlockSpec((SGU_CHUNK, SGU_GROUPS), lambda i: (0, 0))],
        out_specs=blk(0), compiler_params=_cp("parallel"),
    )(proj, proj, g_norm, w_spatial, b_spatial_t)


def merge_matmul(ya_pre, sgu, w_a, w_b, proj):
    T, K = ya_pre.shape
    N = w_a.shape[1]
    tm, tn = min(512, T), 512
    gpb = 1024 // tn

    def body(a_ref, b_ref, wa_ref, wb_ref, ga_ref, gb_ref, ya_ref, yb_ref, m_ref):
        ya = _nn(a_ref[...], wa_ref[...])
        yb = _nn(b_ref[...], wb_ref[...])
        ya_ref[...] = ya.astype(BF16)
        yb_ref[...] = yb.astype(BF16)
        m_ref[...] = (jax.nn.sigmoid(ga_ref[...]) * ya + jax.nn.sigmoid(gb_ref[...]) * yb).astype(BF16)

    lhs = pl.BlockSpec((tm, K), lambda i, j: (i, 0))
    rhs = pl.BlockSpec((K, tn), lambda i, j: (0, j))
    out = pl.BlockSpec((tm, tn), lambda i, j: (i, j))
    return pl.pallas_call(
        body, name="merge_matmul", grid=(T // tm, N // tn),
        out_shape=[jax.ShapeDtypeStruct((T, N), BF16)] * 3,
        in_specs=[lhs, lhs, rhs, rhs, pl.BlockSpec((tm, tn), lambda i, j: (i, COL_GA * gpb + j)),
                  pl.BlockSpec((tm, tn), lambda i, j: (i, COL_GB * gpb + j))],
        out_specs=[out, out, out], compiler_params=_cp("parallel", "parallel"),
    )(ya_pre, sgu, w_a, w_b, proj, proj)


def out_proj(merged, w_o, h0, gt1, g_post):
    T, D = h0.shape
    tm = min(256, T)

    def body(m_ref, w_ref, h_ref, gt_ref, gp_ref, mo_ref, h1_ref):
        mo = _nn(m_ref[...], w_ref[...])
        mo_ref[...] = mo
        r = lax.rsqrt(jnp.mean(mo * mo, axis=-1, keepdims=True) + EPS)
        h1_ref[...] = h_ref[...] + gt_ref[...] * ((mo * r) * gp_ref[...])

    row = pl.BlockSpec((tm, D), lambda i: (i, 0))
    return pl.pallas_call(
        body, name="out_proj", grid=(T // tm,),
        out_shape=[jax.ShapeDtypeStruct((T, D), F32), jax.ShapeDtypeStruct((T, D), F32)],
        in_specs=[row, pl.BlockSpec((D, D), lambda i: (0, 0)), row, _vec(D), _vec(D)],
        out_specs=[row, row], compiler_params=_cp("parallel"),
    )(merged, w_o, h0, gt1, g_post)


def ff2_loss(hid, w_ff2, h1, tgt, gt2, g_post):
    T, K = hid.shape
    D = w_ff2.shape[1]
    tm, tk = min(256, T), 2048
    nk = K // tk

    def body(a_ref, w_ref, h_ref, t_ref, gt_ref, g_ref, dy_ref, dff_ref, loss_ref, dgt_ref, dg_ref, acc_ref):
        i, k = pl.program_id(0), pl.program_id(1)

        @pl.when(k == 0)
        def _():
            acc_ref[...] = jnp.zeros_like(acc_ref)

        @pl.when((k == 0) & (i == 0))
        def _():
            loss_ref[...] = jnp.zeros_like(loss_ref)
            dgt_ref[...] = jnp.zeros_like(dgt_ref)
            dg_ref[...] = jnp.zeros_like(dg_ref)

        acc_ref[...] += _nn(a_ref[...], w_ref[...])

        @pl.when(k == nk - 1)
        def _():
            ff = acc_ref[...]
            gt, g = gt_ref[...], g_ref[...]
            r = lax.rsqrt(jnp.mean(ff * ff, axis=-1, keepdims=True) + EPS)
            fhat = ff * r
            nf = fhat * g
            err = (h_ref[...] + gt * nf) - t_ref[...]
            loss_ref[...] += jnp.sum(err * err)
            dy = err * (1.0 / D)
            dy_ref[...] = dy
            dgt_ref[...] += _colsum(dy * nf)
            dnf = dy * gt
            dg_ref[...] += _colsum(dnf * fhat)
            u = dnf * g
            dff_ref[...] = (r * (u - fhat * jnp.mean(u * fhat, axis=-1, keepdims=True))).astype(BF16)

    row = pl.BlockSpec((tm, D), lambda i, k: (i, 0))
    vec = pl.BlockSpec((1, D), lambda i, k: (0, 0))
    return pl.pallas_call(
        body, name="ff2_loss", grid=(T // tm, nk),
        out_shape=[jax.ShapeDtypeStruct((T, D), F32), jax.ShapeDtypeStruct((T, D), BF16), jax.ShapeDtypeStruct((8, 128), F32),
                   jax.ShapeDtypeStruct((1, D), F32), jax.ShapeDtypeStruct((1, D), F32)],
        in_specs=[pl.BlockSpec((tm, tk), lambda i, k: (i, k)), pl.BlockSpec((tk, D), lambda i, k: (k, 0)), row, row, vec, vec],
        out_specs=[row, row, pl.BlockSpec((8, 128), lambda i, k: (0, 0)), vec, vec],
        scratch_shapes=[pltpu.VMEM((tm, D), F32)], compiler_params=_cp("arbitrary", "arbitrary"),
    )(hid, w_ff2, h1, tgt, gt2, g_post)


def ff2_bwd(dff, w_ff2, f1):
    T, D = dff.shape
    K = w_ff2.shape[0]
    tm, tn = min(512, T), 2048

    def body(a_ref, w_ref, f_ref, o_ref):
        o_ref[...] = (_nt(a_ref[...], w_ref[...]) * (2.0 * jnp.maximum(f_ref[...].astype(F32), 0.0))).astype(BF16)

    return pl.pallas_call(
        body, name="ff2_bwd", out_shape=jax.ShapeDtypeStruct((T, K), BF16), grid=(K // tn, T // tm),
        in_specs=[pl.BlockSpec((tm, D), lambda j, i: (i, 0)), pl.BlockSpec((tn, D), lambda j, i: (j, 0)),
                  pl.BlockSpec((tm, tn), lambda j, i: (i, j))],
        out_specs=pl.BlockSpec((tm, tn), lambda j, i: (i, j)), compiler_params=_cp("parallel", "parallel"),
    )(dff, w_ff2, f1)


def ffn_norm_bwd(dy, da2, h1, mo, g_pre2, sc2, gt1, g_post, comm):
    T, D = dy.shape
    tm = min(256, T)

    def body(dy_ref, da_ref, h_ref, mo_ref, g2_ref, sc_ref, gt_ref, gp_ref, dh_ref, dmo_ref, s_sh, s_sc, s_g2, s_gt, s_gp):
        @pl.when(pl.program_id(0) == 0)
        def _():
            for s in (s_sh, s_sc, s_g2, s_gt, s_gp):
                s[...] = jnp.zeros_like(s)

        h1, da = h_ref[...], da_ref[...]
        g2, sc = g2_ref[...], sc_ref[...]
        r2 = lax.rsqrt(jnp.mean(h1 * h1, axis=-1, keepdims=True) + EPS)
        n2 = h1 * r2
        s_sh[...] += _colsum(da)
        s_sc[...] += _colsum(da * (n2 * g2))
        s_g2[...] += _colsum(da * (1.0 + sc) * n2)
        dn2 = da * g2 * (1.0 + sc)
        dh1 = dy_ref[...] + r2 * (dn2 - n2 * jnp.mean(dn2 * n2, axis=-1, keepdims=True))
        dh_ref[...] = dh1
        mo = mo_ref[...]
        gt, gp = gt_ref[...], gp_ref[...]
        r = lax.rsqrt(jnp.mean(mo * mo, axis=-1, keepdims=True) + EPS)
        mhat = mo * r
        s_gt[...] += _colsum(dh1 * (mhat * gp))
        dnm = dh1 * gt
        s_gp[...] += _colsum(dnm * mhat)
        u = dnm * gp
        dmo_ref[...] = (r * (u - mhat * jnp.mean(u * mhat, axis=-1, keepdims=True))).astype(BF16)

    row = pl.BlockSpec((tm, D), lambda i: (i, 0))
    vec_out = jax.ShapeDtypeStruct((1, D), F32)
    return _pallas(
        body, name="ffn_norm_bwd", grid=(T // tm,),
        out_shape=[jax.ShapeDtypeStruct((T, D), F32), jax.ShapeDtypeStruct((T, D), BF16)] + [vec_out] * 5,
        in_specs=[row, row, row, row] + [_vec(D)] * 4, out_specs=[row, row] + [_vec(D)] * 5,
        scratch=[], semantics=("arbitrary",), operands=(dy, da2, h1, mo, g_pre2, sc2, gt1, g_post), comm=comm)


def out_proj_bwd(dmo, w_o, y_a, y_b, proj):
    T, D = dmo.shape
    tm, tn = min(512, T), 512
    gpb = 1024 // tn

    def body(a_ref, w_ref, ya_ref, yb_ref, ga_ref, gb_ref, dya_ref, dyb_ref, dga_ref, dgb_ref):
        dm = _nt(a_ref[...], w_ref[...])
        sa, sb = jax.nn.sigmoid(ga_ref[...]), jax.nn.sigmoid(gb_ref[...])
        dya_ref[...] = (dm * sa).astype(BF16)
        dyb_ref[...] = (dm * sb).astype(BF16)
        dga_ref[...] = (dm * ya_ref[...].astype(F32) * sa * (1.0 - sa)).astype(BF16)
        dgb_ref[...] = (dm * yb_ref[...].astype(F32) * sb * (1.0 - sb)).astype(BF16)

    out = pl.BlockSpec((tm, tn), lambda i, j: (i, j))
    return pl.pallas_call(
        body, name="out_proj_bwd", grid=(T // tm, D // tn), out_shape=[jax.ShapeDtypeStruct((T, D), BF16)] * 4,
        in_specs=[pl.BlockSpec((tm, D), lambda i, j: (i, 0)), pl.BlockSpec((tn, D), lambda i, j: (j, 0)), out, out,
                  pl.BlockSpec((tm, tn), lambda i, j: (i, COL_GA * gpb + j)), pl.BlockSpec((tm, tn), lambda i, j: (i, COL_GB * gpb + j))],
        out_specs=[out] * 4, compiler_params=_cp("parallel", "parallel"),
    )(dmo, w_o, y_a, y_b, proj, proj)


def sgu_bwd(proj, dsgu, g_norm, w_spatial, b_spatial_t):
    T = proj.shape[0]
    W = 1024
    gw = W // SGU_GROUPS

    def body(u_ref, v_ref, ds_ref, g_ref, ws_ref, bst_ref, dz_ref, dw_ref, db_ref, dg_ref):
        @pl.when(pl.program_id(0) == 0)
        def _():
            dw_ref[...] = jnp.zeros_like(dw_ref)
            db_ref[...] = jnp.zeros_like(db_ref)
            dg_ref[...] = jnp.zeros_like(dg_ref)

        zu, rstd, dhat, vn, vm = _sgu_mix(u_ref, v_ref, g_ref, ws_ref, bst_ref)
        ds = ds_ref[...]
        du = ds * vm
        dvm = ds * zu
        dvm_b = dvm.astype(BF16)
        ones = jnp.ones((8, gw), F32)
        dvn = []
        for g in range(SGU_GROUPS):
            sl = slice(g * gw, (g + 1) * gw)
            dw_ref[g] += _nt(dvm_b[:, sl], vn[:, sl])
            db_ref[g] += lax.dot_general(ones, dvm[:, sl], (((1,), (1,)), ((), ())), precision=HI, preferred_element_type=F32)
            dvn.append(_tn(ws_ref[g].astype(BF16), dvm_b[:, sl]))
        dvn = jnp.concatenate(dvn, axis=1)
        dg_ref[...] += _colsum(dvn * dhat)
        ddh = dvn * g_ref[...]
        dzv = rstd * (ddh - jnp.mean(ddh, axis=-1, keepdims=True) - dhat * jnp.mean(ddh * dhat, axis=-1, keepdims=True))
        dz_ref[:, 0:W] = (du * _gelu_grad(u_ref[...])).astype(BF16)
        dz_ref[:, W:2 * W] = (dzv * _gelu_grad(v_ref[...])).astype(BF16)

    blk = lambda cb: pl.BlockSpec((SGU_CHUNK, W), lambda i: (i, cb))
    full3 = lambda a, b, c: pl.BlockSpec((a, b, c), lambda i: (0, 0, 0))
    return pl.pallas_call(
        body, name="sgu_bwd", grid=(T // SGU_CHUNK,),
        out_shape=[jax.ShapeDtypeStruct((T, 2 * W), BF16), jax.ShapeDtypeStruct((SGU_GROUPS, SGU_CHUNK, SGU_CHUNK), F32),
                   jax.ShapeDtypeStruct((SGU_GROUPS, 8, SGU_CHUNK), F32), jax.ShapeDtypeStruct((1, W), F32)],
        in_specs=[blk(COL_U), blk(COL_ZV), blk(0), _vec(W), full3(SGU_GROUPS, SGU_CHUNK, SGU_CHUNK),
                  pl.BlockSpec((SGU_CHUNK, SGU_GROUPS), lambda i: (0, 0))],
        out_specs=[pl.BlockSpec((SGU_CHUNK, 2 * W), lambda i: (i, 0)), full3(SGU_GROUPS, SGU_CHUNK, SGU_CHUNK),
                   full3(SGU_GROUPS, 8, SGU_CHUNK), _vec(W)],
        compiler_params=_cp("arbitrary"),
    )(proj, proj, dsgu, g_norm, w_spatial, b_spatial_t)


def hgrn_post_bwd(dya, o, proj, g_norm):
    T, W = o.shape
    tm = min(256, T)

    def body(dy_ref, o_ref, og_ref, g_ref, do_ref, dog_ref, dg_ref):
        @pl.when(pl.program_id(0) == 0)
        def _():
            dg_ref[...] = jnp.zeros_like(dg_ref)

        g = g_ref[...]
        dg = jnp.zeros((1, HEAD_DIM), F32)
        for h in range(N_HEADS):
            sl = slice(h * HEAD_DIM, (h + 1) * HEAD_DIM)
            x, og, dy = o_ref[:, sl], og_ref[:, sl], dy_ref[:, sl]
            r = lax.rsqrt(jnp.mean(x * x, axis=-1, keepdims=True) + EPS)
            xhat = x * r
            s = jax.nn.sigmoid(og)
            don = dy * (og * s)
            dog_ref[:, sl] = (dy * (xhat * g) * (s * (1.0 + og * (1.0 - s)))).astype(BF16)
            dg += _colsum(don * xhat)
            u = don * g
            do_ref[:, sl] = r * (u - xhat * jnp.mean(u * xhat, axis=-1, keepdims=True))
        dg_ref[...] += dg

    row = pl.BlockSpec((tm, W), lambda i: (i, 0))
    return pl.pallas_call(
        body, name="hgrn_post_bwd", grid=(T // tm,),
        out_shape=[jax.ShapeDtypeStruct((T, W), F32), jax.ShapeDtypeStruct((T, W), BF16), jax.ShapeDtypeStruct((1, HEAD_DIM), F32)],
        in_specs=[row, row, pl.BlockSpec((tm, W), lambda i: (i, COL_OG)), _vec(HEAD_DIM)],
        out_specs=[row, row, _vec(HEAD_DIM)], compiler_params=_cp("arbitrary"),
    )(dya, o, proj, g_norm)


def hgrn_bwd(proj, do, lb_logits, comm=None):
    T = proj.shape[0]
    NC, CPB = T // HGRN_CHUNK, HGRN_BLOCK // HGRN_CHUNK
    W = N_HEADS * HEAD_DIM
    col, f_spec, l_spec = _hgrn_specs(T)

    def body(l_ref, q_ref, f_ref, v_ref, do_ref, dq_ref, dv_ref, dlg_ref, dlb_ref, st_ref, dst_ref, dec_ref, ddec_ref, dqa_ref, dva_ref):
        d = pl.program_id(1)
        lb = _hgrn_lower_bound(l_ref)
        oml = 1.0 - lb
        mask = _hgrn_chunk_mask(d)

        def values(rows):
            s, sn, fg, lf, k = _hgrn_gate(f_ref[rows, :], lb)
            b = _chunk_cumsum(lf, d == 1)
            bl = _chunk_total(lf)
            eb, enb, ee = jnp.exp(b), jnp.exp(-b), jnp.exp(bl - b)
            qd = q_ref[rows, :] * Q_SCALE * eb
            return s, sn, fg, k, bl, eb, enb, ee, qd, k * enb, k * ee

        def block1(i, carry):
            rows = pl.ds(pl.multiple_of(i * HGRN_BLOCK, HGRN_BLOCK), HGRN_BLOCK)
            _, _, _, _, bl, _, _, _, qd, _, ke = values(rows)
            qd, ke = qd.astype(BF16), ke.astype(BF16)
            vb, dob = v_ref[rows, :].astype(BF16), do_ref[rows, :].astype(BF16)
            dec = jnp.exp(bl)
            for cc in range(CPB):
                sl = slice(cc * HGRN_CHUNK, (cc + 1) * HGRN_CHUNK)
                n = i * CPB + cc
                st_ref[n] = _tn(vb[sl], ke[sl])
                dst_ref[n] = _tn(dob[sl], qd[sl])
                dec_ref[n] = dec[cc * HGRN_CHUNK:cc * HGRN_CHUNK + 8, :]
            return carry

        _block_loop(T, block1, 0)

        def scan(t, s):
            n = jnp.where(d == 0, t, NC - 1 - t)
            u = st_ref[n]
            st_ref[n] = s
            return dec_ref[n][0:1, :] * s + u

        lax.fori_loop(0, NC, scan, jnp.zeros((HEAD_DIM, HEAD_DIM), F32))

        def rscan(t, ds):
            n = jnp.where(d == 0, NC - 1 - t, t)
            w = dst_ref[n]
            dst_ref[n] = ds
            ddec_ref[n] = jnp.broadcast_to(_colsum(ds * st_ref[n]), (8, HEAD_DIM))
            return dec_ref[n][0:1, :] * ds + w

        lax.fori_loop(0, NC, rscan, jnp.zeros((HEAD_DIM, HEAD_DIM), F32))

        def block3(i, dlb):
            rows = pl.ds(pl.multiple_of(i * HGRN_BLOCK, HGRN_BLOCK), HGRN_BLOCK)
            s, sn, fg, k, bl, eb, enb, ee, qd, kd, ke = values(rows)
            qdb, kdb, keb = qd.astype(BF16), kd.astype(BF16), ke.astype(BF16)
            vb, dob = v_ref[rows, :].astype(BF16), do_ref[rows, :].astype(BF16)
            att = jnp.where(mask, _nt(qdb, kdb), 0.0).astype(BF16)
            datt = jnp.where(mask, _nt(dob, vb), 0.0).astype(BF16)
            dv = _tn(att, dob)
            dqd = _nn(datt, kdb)
            dkd = _tn(datt, qdb)
            dv_i, dqd_i, dke, ddl = [], [], [], []
            for cc in range(CPB):
                sl = slice(cc * HGRN_CHUNK, (cc + 1) * HGRN_CHUNK)
                n = i * CPB + cc
                st_b, dst_b = st_ref[n].astype(BF16), dst_ref[n].astype(BF16)
                dv_i.append(_nt(keb[sl], dst_b))
                dqd_i.append(_nn(dob[sl], st_b))
                dke.append(_nn(vb[sl], dst_b))
                ddl.append(jnp.broadcast_to(ddec_ref[n][0:1, :] * dec_ref[n][0:1, :], (HGRN_CHUNK, HEAD_DIM)))
            dv = dv + jnp.concatenate(dv_i, axis=0)
            dqd = dqd + jnp.concatenate(dqd_i, axis=0)
            dke = jnp.concatenate(dke, axis=0)
            dq = dqd * eb * Q_SCALE
            dk = dkd * enb + dke * ee
            t_end = dke * ke
            db = dqd * qd - dkd * kd - t_end
            dlf = _chunk_cumsum(db, d == 0) + _chunk_total(t_end) + jnp.concatenate(ddl, axis=0)
            e = dlf / fg - dk
            dlg_ref[rows, :] = (oml * e * s * sn).astype(BF16)

            dq = jnp.where(d == 0, 0.0, dqa_ref[rows, :]) + dq
            dv = jnp.where(d == 0, 0.0, dva_ref[rows, :]) + dv
            dqa_ref[rows, :] = dq
            dva_ref[rows, :] = dv
            dq_ref[rows, :] = dq.astype(BF16)
            dv_ref[rows, :] = dv.astype(BF16)

            return dlb + _colsum(e * sn)

        dlb_ref[...] = _block_loop(T, block3, jnp.zeros((1, HEAD_DIM), F32))

    head = pl.BlockSpec((T, HEAD_DIM), lambda h, d: (0, h))
    big = pltpu.VMEM((NC, HEAD_DIM, HEAD_DIM), F32)
    small = pltpu.VMEM((NC, 8, HEAD_DIM), F32)
    acc = pltpu.VMEM((T, HEAD_DIM), F32)
    outs, landed = _pallas(
        body, name="hgrn_bwd", grid=(N_HEADS, 2),
        out_shape=[jax.ShapeDtypeStruct((T, W), BF16), jax.ShapeDtypeStruct((T, W), BF16), jax.ShapeDtypeStruct((T, 2 * W), BF16),
                   jax.ShapeDtypeStruct((2, 1, W), F32)],
        in_specs=[l_spec, col(COL_Q), f_spec, col(COL_V), head],
        out_specs=[head, head, pl.BlockSpec((T, HEAD_DIM), lambda h, d: (0, N_HEADS * d + h)),
                   pl.BlockSpec((None, 1, HEAD_DIM), lambda h, d: (d, 0, h))],
        scratch=[big, big, small, small, acc, acc], semantics=("parallel", "arbitrary"), operands=(lb_logits, proj, proj, proj, do), comm=comm)
    return outs if comm is None else (outs, landed)


def mix_norm_bwd(da1, h0, dh1, g_pre, sc1):
    T, D = h0.shape
    tm = min(256, T)

    def body(da_ref, h_ref, dh_ref, g_ref, sc_ref, gx_ref, s_sh, s_sc, s_g):
        @pl.when(pl.program_id(0) == 0)
        def _():
            for s in (s_sh, s_sc, s_g):
                s[...] = jnp.zeros_like(s)

        h, da = h_ref[...], da_ref[...]
        g, sc = g_ref[...], sc_ref[...]
        r = lax.rsqrt(jnp.mean(h * h, axis=-1, keepdims=True) + EPS)
        n = h * r
        s_sh[...] += _colsum(da)
        s_sc[...] += _colsum(da * (n * g))
        s_g[...] += _colsum(da * (1.0 + sc) * n)
        dn = da * g * (1.0 + sc)
        gx_ref[...] = dh_ref[...] + r * (dn - n * jnp.mean(dn * n, axis=-1, keepdims=True))

    row = pl.BlockSpec((tm, D), lambda i: (i, 0))
    return pl.pallas_call(
        body, name="mix_norm_bwd", grid=(T // tm,),
        out_shape=[jax.ShapeDtypeStruct((T, D), F32)] + [jax.ShapeDtypeStruct((1, D), F32)] * 3,
        in_specs=[row, row, row, _vec(D), _vec(D)], out_specs=[row] + [_vec(D)] * 3, compiler_params=_cp("arbitrary"),
    )(da1, h0, dh1, g_pre, sc1)


def adamw(w, g, m, v, name):
    R, C = w.shape
    tr = R if R * C * 4 <= (1 << 21) else max(8, ((1 << 21) // (C * 4)) // 8 * 8)
    while R % tr:
        tr -= 8

    def body(w_ref, g_ref, m_ref, v_ref, d_ref, m2_ref, v2_ref):
        d_ref[...], m2_ref[...], v2_ref[...] = _adamw(w_ref[...], g_ref[...], m_ref[...], v_ref[...])

    row = pl.BlockSpec((tr, C), lambda i: (i, 0))
    return pl.pallas_call(
        body, name=name, grid=(R // tr,), out_shape=[jax.ShapeDtypeStruct((R, C), F32)] * 3,
        in_specs=[row] * 4, out_specs=[row] * 3, compiler_params=_cp("parallel"),
    )(w, g, m, v)


def wada_update(c_all, dmod, w, m, v):
    D, N = w.shape
    tm, tn = 512, 1024

    def body(c_ref, dm_ref, w_ref, m_ref, v_ref, g_ref, d_ref, m2_ref, v2_ref):
        c = c_ref[...]
        g = lax.dot_general(c * jax.nn.sigmoid(c), dm_ref[...], (((0,), (0,)), ((), ())), precision=HI, preferred_element_type=F32)
        g_ref[...] = g
        d_ref[...], m2_ref[...], v2_ref[...] = _adamw(w_ref[...], g, m_ref[...], v_ref[...])

    blk = pl.BlockSpec((tm, tn), lambda i, j: (i, j))
    return pl.pallas_call(
        body, name="wada_update", grid=(D // tm, N // tn), out_shape=[jax.ShapeDtypeStruct((D, N), F32)] * 4,
        in_specs=[pl.BlockSpec((8, tm), lambda i, j: (0, i)), pl.BlockSpec((8, tn), lambda i, j: (0, j)), blk, blk, blk],
        out_specs=[blk] * 4, compiler_params=_cp("parallel", "parallel"),
    )(c_all, dmod, w, m, v)


def sum_devices(gathered, name):
    n, R, C = gathered.shape

    def body(g_ref, o_ref):
        s = g_ref[0]
        for i in range(1, n):
            s = s + g_ref[i]
        o_ref[...] = s

    return pl.pallas_call(body, name=name, out_shape=jax.ShapeDtypeStruct((R, C), F32), compiler_params=_cp())(gathered)


def lb_logits_grad(dlb, lb_logits):
    def body(d_ref, l_ref, o_ref):
        for d in range(2):
            l0, l1 = l_ref[d, 0:1, :], l_ref[d, 1:2, :]
            m = jnp.maximum(l0, l1)
            e0, e1 = jnp.exp(l0 - m), jnp.exp(l1 - m)
            p0, p1 = e0 / (e0 + e1), e1 / (e0 + e1)
            g = d_ref[d:d + 1, :]
            o_ref[d, 0:1, :] = p0 * (g - p0 * g)
            o_ref[d, 1:2, :] = -p1 * (p0 * g)

    return pl.pallas_call(body, name="lb_logits_grad", out_shape=jax.ShapeDtypeStruct(lb_logits.shape, F32), compiler_params=_cp())(dlb, lb_logits)


def add_halves(g, landed, core):
    nj, _, r, cc = g.shape
    tr = min(256, r)

    def body(core_ref, g_ref, l_ref, o_ref):
        o_ref[...] = (g_ref[...].astype(F32) + l_ref[...].astype(F32)).astype(BF16)

    return pl.pallas_call(
        body, name="add_halves_%dx%d" % (r, cc), out_shape=jax.ShapeDtypeStruct((nj, r, cc), BF16),
        grid_spec=pltpu.PrefetchScalarGridSpec(
            num_scalar_prefetch=1, grid=(nj, r // tr),
            in_specs=[pl.BlockSpec((None, None, tr, cc), lambda j, i, core_ref: (j, core_ref[0], i, 0)),
                      pl.BlockSpec((None, None, tr, cc), lambda j, i, core_ref: (j, 0, i, 0))],
            out_specs=pl.BlockSpec((None, tr, cc), lambda j, i, core_ref: (j, i, 0))),
        compiler_params=_cp("parallel", "parallel"),
    )(core, g, landed)


def sum_chips(parts, landed, chip):
    nj, r, cc = parts.shape
    tr = min(256, r)

    def body(chip_ref, p_ref, l_ref, o_ref):
        mine = p_ref[...].astype(F32)
        s = None
        for j in range(nj):
            t = jnp.where(chip_ref[0] == j, mine, l_ref[j].astype(F32))
            s = t if s is None else s + t
        o_ref[...] = s

    return pl.pallas_call(
        body, name="sum_chips_%dx%d" % (r, cc), out_shape=jax.ShapeDtypeStruct((r, cc), F32),
        grid_spec=pltpu.PrefetchScalarGridSpec(
            num_scalar_prefetch=1, grid=(r // tr,),
            in_specs=[pl.BlockSpec((None, tr, cc), lambda i, chip_ref: (chip_ref[0], i, 0)),
                      pl.BlockSpec((nj, tr, cc), lambda i, chip_ref: (0, i, 0))],
            out_specs=pl.BlockSpec((tr, cc), lambda i, chip_ref: (i, 0))),
        compiler_params=_cp("parallel"),
    )(chip, parts, landed)


def adamw_halves(w, own, other, m, v, core, name):
    r, cc = own.shape
    tr = min(128, r)
    nb = r // tr

    def body(core_ref, w_ref, a_ref, b_ref, m_ref, v_ref, g_ref, d_ref, m2_ref, v2_ref):
        g = jnp.where(pl.program_id(0) == core_ref[0], a_ref[...], b_ref[...])
        g_ref[...] = g
        d_ref[...], m2_ref[...], v2_ref[...] = _adamw(w_ref[...], g, m_ref[...], v_ref[...])

    full = pl.BlockSpec((tr, cc), lambda h, i, core_ref: (h * nb + i, 0))
    half = pl.BlockSpec((tr, cc), lambda h, i, core_ref: (i, 0))
    return pl.pallas_call(
        body, name=name, out_shape=[jax.ShapeDtypeStruct((2 * r, cc), F32)] * 4,
        grid_spec=pltpu.PrefetchScalarGridSpec(
            num_scalar_prefetch=1, grid=(2, nb), in_specs=[full, half, half, full, full], out_specs=[full] * 4),
        compiler_params=_cp("parallel", "parallel"),
    )(core, w, own, other, m, v)


def _place():
    mx, my, mc = lax.axis_index("x"), lax.axis_index("y"), lax.axis_index("c")
    chips = [(1 - mx, my), (mx, 1 - my), (1 - mx, 1 - my)]
    return mx, my, mc, chips


def all_gather_small(x, name):
    R, C = x.shape

    def body(x_ref, out_ref, send_sems, recv_sems, local_sem):
        mx, my, mc, _ = _place()
        me = 4 * mx + 2 * my + mc
        mine = pltpu.make_async_copy(x_ref, out_ref.at[me], local_sem)
        mine.start()

        def peer(k):
            px = 1 - mx if k & 4 else mx
            py = 1 - my if k & 2 else my
            pc = 1 - mc if k & 1 else mc
            return px, py, pc

        def copy(k, src, slot):
            return pltpu.make_async_remote_copy(src_ref=src, dst_ref=out_ref.at[slot], send_sem=send_sems.at[k - 1],
                                                recv_sem=recv_sems.at[k - 1], device_id=peer(k), device_id_type=MESH)

        sends = [copy(k, x_ref, me) for k in range(1, 8)]
        for cp in sends:
            cp.start()
        for k in range(1, 8):
            px, py, pc = peer(k)
            slot = 4 * px + 2 * py + pc
            copy(k, out_ref.at[slot], slot).wait_recv()
        for cp in sends:
            cp.wait_send()
        mine.wait()

    return pl.pallas_call(
        body, name=name, out_shape=jax.ShapeDtypeStruct((8, R, C), F32),
        in_specs=[pl.BlockSpec(memory_space=pltpu.VMEM)], out_specs=pl.BlockSpec(memory_space=pltpu.VMEM),
        scratch_shapes=[pltpu.SemaphoreType.DMA((7,)), pltpu.SemaphoreType.DMA((7,)), pltpu.SemaphoreType.DMA],
        compiler_params=_cp(),
    )(x)


def gather8_comm(x):
    def copies(x_ref, out_ref, send_sems, recv_sems):
        mx, my, mc, _ = _place()
        me = 4 * mx + 2 * my + mc

        def peer(k):
            return (1 - mx if k & 4 else mx, 1 - my if k & 2 else my, 1 - mc if k & 1 else mc)

        def copy(k, src, slot):
            return pltpu.make_async_remote_copy(src_ref=src, dst_ref=out_ref.at[slot], send_sem=send_sems.at[k - 1],
                                                recv_sem=recv_sems.at[k - 1], device_id=peer(k), device_id_type=MESH)

        sends = [copy(k, x_ref, me) for k in range(1, 8)]
        arrivals = []
        for k in range(1, 8):
            px, py, pc = peer(k)
            slot = 4 * px + 2 * py + pc
            arrivals.append(copy(k, out_ref.at[slot], slot))
        return sends, arrivals, pltpu.make_async_copy(x_ref, out_ref.at[me], send_sems.at[7])

    def start(cin, cout, send_sems, recv_sems):
        sends, _, mine = copies(cin[0], cout[0], send_sems, recv_sems)
        mine.start()
        for cp in sends:
            cp.start()

    def finish(cin, cout, send_sems, recv_sems):
        sends, arrivals, mine = copies(cin[0], cout[0], send_sems, recv_sems)
        for cp in arrivals:
            cp.wait_recv()
        for cp in sends:
            cp.wait_send()
        mine.wait()

    return _Comm([x], [jax.ShapeDtypeStruct((8,) + x.shape, F32)], {}, 8, start, finish)


def _join(a, b):
    na_in, na_out = len(a.operands), len(a.out_shape)

    def split(fn_a, fn_b):
        def both(cin, cout, send_sems, recv_sems):
            fn_a(cin[:na_in], cout[:na_out], send_sems.at[pl.ds(0, a.n_sems)], recv_sems.at[pl.ds(0, a.n_sems)])
            fn_b(cin[na_in:], cout[na_out:], send_sems.at[pl.ds(a.n_sems, b.n_sems)], recv_sems.at[pl.ds(a.n_sems, b.n_sems)])
        return both

    aliases = dict(a.aliases)
    aliases.update({na_in + i: na_out + o for i, o in b.aliases.items()})
    return _Comm(a.operands + b.operands, a.out_shape + b.out_shape, aliases, a.n_sems + b.n_sems, split(a.start, b.start), split(a.finish, b.finish))


def _region(ref, kind, j, half, r, cc):
    nr = r if half is None else r // 2
    off = 0 if half is None else half * nr
    if kind == "col":
        return ref.at[pl.ds(off, nr), pl.ds(pl.multiple_of(j * cc, 128), cc)]
    return ref.at[pl.ds(pl.multiple_of(j * r + off, 16), nr), :]


def comm_call(comm, name):
    ni, no = len(comm.operands), len(comm.out_shape)

    def body(*refs):
        comm.start(refs[:ni], refs[ni:ni + no], *refs[ni + no:])
        comm.finish(refs[:ni], refs[ni:ni + no], *refs[ni + no:])

    return pl.pallas_call(
        body, name=name, out_shape=comm.out_shape, in_specs=[ANY] * ni, out_specs=[ANY] * no, input_output_aliases=comm.aliases,
        scratch_shapes=[pltpu.SemaphoreType.DMA((comm.n_sems,)), pltpu.SemaphoreType.DMA((comm.n_sems,))], compiler_params=_cp(),
    )(*comm.operands)


def gather_comm(fulls, kinds, dims):
    n = len(fulls)

    def copies(f_refs, send_sems, recv_sems):
        mx, my, mc, chips = _place()
        jme = 2 * mx + my

        def landed(w, k, half):
            px, py = chips[k]
            return _region(f_refs[w], kinds[w], 2 * px + py, half, *dims[w])

        def over_ici(w, k, reg):
            px, py = chips[k]
            return pltpu.make_async_remote_copy(src_ref=reg, dst_ref=reg, send_sem=send_sems.at[6 * w + k], recv_sem=recv_sems.at[6 * w + k],
                                                device_id=(px, py, mc), device_id_type=MESH)

        def over_d2d(w, k, half):
            reg = landed(w, k, half)
            return pltpu.make_async_remote_copy(src_ref=reg, dst_ref=reg, send_sem=send_sems.at[6 * w + 3 + k],
                                                recv_sem=recv_sems.at[6 * w + 3 + k], device_id=(mx, my, 1 - mc), device_id_type=MESH)

        sends = [over_ici(w, k, _region(f_refs[w], kinds[w], jme, mc, *dims[w])) for w in range(n) for k in range(3)]
        return mc, landed, over_ici, over_d2d, sends

    def start(cin, f_refs, send_sems, recv_sems):
        for cp in copies(f_refs, send_sems, recv_sems)[4]:
            cp.start()

    def finish(cin, f_refs, send_sems, recv_sems):
        mc, landed, over_ici, over_d2d, sends = copies(f_refs, send_sems, recv_sems)
        passed = []
        for w in range(n):
            for k in range(3):
                over_ici(w, k, landed(w, k, mc)).wait_recv()
                cp = over_d2d(w, k, mc)
                cp.start()
                passed.append(cp)
        for w in range(n):
            for k in range(3):
                over_d2d(w, k, 1 - mc).wait_recv()
        for cp in sends + passed:
            cp.wait_send()

    return _Comm(fulls, [jax.ShapeDtypeStruct(f.shape, BF16) for f in fulls], {w: w for w in range(n)}, 6 * n, start, finish)


def exchange_comm(grads):
    n = len(grads)

    def copies(g_refs, l_refs, send_sems, recv_sems):
        mx, my, mc, _ = _place()
        return [pltpu.make_async_remote_copy(src_ref=g_refs[w].at[:, pl.ds(1 - mc, 1)], dst_ref=l_refs[w], send_sem=send_sems.at[w],
                                             recv_sem=recv_sems.at[w], device_id=(mx, my, 1 - mc), device_id_type=MESH) for w in range(n)]

    def start(*refs):
        for cp in copies(*refs):
            cp.start()

    def finish(*refs):
        for cp in copies(*refs):
            cp.wait()

    return _Comm(grads, [jax.ShapeDtypeStruct((g.shape[0], 1) + g.shape[2:], BF16) for g in grads], {}, n, start, finish)


def exchange_halves(grads, name):
    return comm_call(exchange_comm(grads), name)


def scatter_comm(parts):
    n = len(parts)

    def sends(p_refs, l_refs, send_sems, recv_sems):
        mx, my, mc, chips = _place()
        return [pltpu.make_async_remote_copy(src_ref=p_refs[w].at[2 * px + py], dst_ref=l_refs[w].at[2 * mx + my],
                                             send_sem=send_sems.at[3 * w + k], recv_sem=recv_sems.at[3 * w + k],
                                             device_id=(px, py, mc), device_id_type=MESH) for w in range(n) for k, (px, py) in enumerate(chips)]

    def start(p_refs, l_refs, send_sems, recv_sems):
        for cp in sends(p_refs, l_refs, send_sems, recv_sems):
            cp.start()

    def finish(p_refs, l_refs, send_sems, recv_sems):
        mx, my, mc, chips = _place()
        for w in range(n):
            for k, (px, py) in enumerate(chips):
                slot = l_refs[w].at[2 * px + py]
                pltpu.make_async_remote_copy(src_ref=slot, dst_ref=slot, send_sem=send_sems.at[3 * w + k], recv_sem=recv_sems.at[3 * w + k],
                                             device_id=(px, py, mc), device_id_type=MESH).wait_recv()
        for cp in sends(p_refs, l_refs, send_sems, recv_sems):
            cp.wait_send()

    return _Comm(parts, [jax.ShapeDtypeStruct(p.shape, BF16) for p in parts], {}, 3 * n, start, finish)


def share_comm(sums):
    n = len(sums)

    def copies(q_refs, o_refs, send_sems, recv_sems):
        mx, my, mc, _ = _place()
        return [pltpu.make_async_remote_copy(src_ref=q_refs[w], dst_ref=o_refs[w], send_sem=send_sems.at[w], recv_sem=recv_sems.at[w],
                                             device_id=(mx, my, 1 - mc), device_id_type=MESH) for w in range(n)]

    def start(*refs):
        for cp in copies(*refs):
            cp.start()

    def finish(*refs):
        for cp in copies(*refs):
            cp.wait()

    return _Comm(sums, [jax.ShapeDtypeStruct(q.shape, F32) for q in sums], {}, n, start, finish)


def _pack(arrays):
    flat = jnp.concatenate([a.reshape(-1) for a in arrays])
    rows = -(-flat.shape[0] // 1024) * 8
    return jnp.pad(flat, (0, rows * 128 - flat.shape[0])).reshape(rows, 128)


def _unpack(packed, shapes):
    flat, out, off = packed.reshape(-1), [], 0
    for s in shapes:
        n = math.prod(s)
        out.append(flat[off:off + n].reshape(s))
        off += n
    return out


def kernel(x, c, w_ada, b_ada, g_pre_mix, g_post_mix, g_pre_ffn, g_post_ffn, w_in, lb_logits, g_hgrn_norm, w_a_out, g_sgu_norm, w_spatial, b_spatial, w_b_out, w_o, w_ff1, w_ff2, loss_target, m_w_ada, m_b_ada, m_g_pre_mix, m_g_post_mix, m_g_pre_ffn, m_g_post_ffn, m_w_in, m_lb_logits, m_g_hgrn_norm, m_w_a_out, m_g_sgu_norm, m_w_spatial, m_b_spatial, m_w_b_out, m_w_o, m_w_ff1, m_w_ff2, v_w_ada, v_b_ada, v_g_pre_mix, v_g_post_mix, v_g_pre_ffn, v_g_post_ffn, v_w_in, v_lb_logits, v_g_hgrn_norm, v_w_a_out, v_g_sgu_norm, v_w_spatial, v_b_spatial, v_w_b_out, v_w_o, v_w_ff1, v_w_ff2):
    mx, my, mc = lax.axis_index("x"), lax.axis_index("y"), lax.axis_index("c")
    chip, me = 2 * mx + my, 4 * mx + 2 * my + mc
    D = D_MODEL
    h0, tgt = x[0], loss_target[0]
    n_ada = w_ada.shape[2]
    n_lb = lb_logits.shape[2]

    got = all_gather_small(_pack([c, lb_logits]), "gather_inputs")
    c_all = got[:, :D // 128, :].reshape(8, D)
    lb_full = got[0::2, D // 128:D // 128 + 4 * n_lb // 128, :].reshape(4, 2, 2, n_lb).transpose(1, 2, 0, 3).reshape(2, 2, 4 * n_lb)
    b_ada_chip = lax.dynamic_slice(b_ada, (0, chip * n_ada), (1, n_ada))
    mod_cols = mod_matmul(c_all, w_ada[0], b_ada_chip)
    got = all_gather_small(mod_cols.reshape(-1, 128), "gather_mod").reshape(4, 2, 8, n_ada)
    mod = lax.dynamic_index_in_dim(got[:, 0], me, axis=1, keepdims=False).reshape(6, 1, D)
    sh1, sc1, gt1, sh2, sc2, gt2 = (mod[i] for i in range(6))

    big = [("w_in", w_in, "col"), ("w_a_out", w_a_out, "col"), ("w_b_out", w_b_out, "col"), ("w_o", w_o, "row"),
           ("w_ff1", w_ff1, "col"), ("w_ff2", w_ff2, "row")]
    kinds = [k for _, _, k in big]
    chip_idx, core = chip.reshape(1).astype(jnp.int32), mc.reshape(1).astype(jnp.int32)
    fulls = [cast_into_full(w[0], kind, chip_idx, "cast_" + nm) for nm, w, kind in big]
    dims = [w.shape[1:] for _, w, _ in big]
    later = lambda lo, hi: gather_comm(fulls[lo:hi], kinds[lo:hi], dims[lo:hi])
    halves_summed = lambda grads, name: [add_halves(g, l, core) for g, l in zip(grads, exchange_halves(grads, name))]

    bst = b_spatial[0].T
    a1 = prenorm(h0, g_pre_mix, sc1, sh1)
    proj, w_in_f, (w_a_f, w_b_f, w_o_f) = in_proj_gathered(a1, fulls[0], chip_idx, dims[0], later(1, 4))
    o, (w_ff1_f,) = hgrn_fwd(proj, lb_full, comm=later(4, 5))
    ya_pre = hgrn_post_fwd(o, proj, g_hgrn_norm)
    sgu = sgu_fwd(proj, g_sgu_norm, w_spatial[0], bst)
    y_a, y_b, merged = merge_matmul(ya_pre, sgu, w_a_f, w_b_f, proj)
    mo, h1 = out_proj(merged, w_o_f, h0, gt1, g_post_mix)
    (f1, a2, hid), (w_ff2_f,) = prenorm_matmul(h1, g_pre_ffn, sc2, sh2, w_ff1_f, relu2=True, name="ff1", comm=later(5, 6))
    dy, dff, loss_parts, d_gt2, d_g_post_ffn = ff2_loss(hid, w_ff2_f, h1, tgt, gt2, g_post_ffn)
    loss = lax.psum(0.5 * loss_parts[0, 0] / D, ("x", "y", "c"))

    df1 = ff2_bwd(dff, w_ff2_f, f1)
    gr_ff2 = matmul(hid, dff, mode="tn", out_dtype=BF16, tm=1024, tn=1024, tk=2048, name="dw_ff2")
    gr_ff2 = gr_ff2.reshape(4, 2, -1, D)
    da2, (landed_ff2,) = matmul(df1, w_ff1_f, mode="nt", out_dtype=F32, tm=1024, tn=1024, tk=2048, name="da2", comm=exchange_comm([gr_ff2]))
    gr_ff1 = matmul(a2, df1, mode="tn", out_dtype=BF16, tm=1024, tn=2048, tk=1024, name="dw_ff1", split=(4, 2))
    (dh1, dmo, d_sh2, d_sc2, d_g_pre_ffn, d_gt1, d_g_post_mix), (landed_ff1,) = ffn_norm_bwd(
        dy, da2, h1, mo, g_pre_ffn, sc2, gt1, g_post_mix, exchange_comm([gr_ff1]))
    parts_ff = [add_halves(gr_ff1, landed_ff1, core), add_halves(gr_ff2, landed_ff2, core)]
    dya, dyb, dga, dgb = out_proj_bwd(dmo, w_o_f, y_a, y_b, proj)
    gr_o = matmul(merged, dmo, mode="tn", out_dtype=BF16, tm=1024, tn=1024, tk=2048, name="dw_o")
    dsgu = matmul(dyb, w_b_f, mode="nt", out_dtype=F32, tm=512, tn=1024, tk=2048, name="dsgu")
    gr_b = matmul(sgu, dyb, mode="tn", out_dtype=BF16, tm=512, tn=512, tk=4096, name="dw_b_out", split=(4, 2))
    dz, d_w_spatial, d_b_spatial, d_g_sgu = sgu_bwd(proj, dsgu, g_sgu_norm, w_spatial[0], bst)
    dya_pre = matmul(dya, w_a_f, mode="nt", out_dtype=F32, tm=512, tn=1024, tk=2048, name="dya_pre")
    gr_a = matmul(ya_pre, dya, mode="tn", out_dtype=BF16, tm=512, tn=512, tk=4096, name="dw_a_out", split=(4, 2))
    parts_mix = halves_summed([gr_a, gr_b, gr_o.reshape(4, 2, -1, D)], "exchange_mix")
    do, dog, d_g_hgrn = hgrn_post_bwd(dya_pre, o, proj, g_hgrn_norm)
    chips_summed = lambda parts, landed: [sum_chips(p, l, chip_idx) for p, l in zip(parts, landed)]
    (dq, dv, dlg, d_lb), landed_ff = hgrn_bwd(proj, do, lb_full, comm=scatter_comm(parts_ff))
    own_ff = chips_summed(parts_ff, landed_ff)
    dproj = jnp.concatenate([dq, dlg, dv, dog, dz, dga, dgb], axis=1)
    early = _pack([d_g_sgu, d_w_spatial, d_b_spatial[:, 0, :]])
    gr_in, (*landed_mix, got_early) = matmul(a1, dproj, mode="tn", out_dtype=BF16, tm=1024, tn=2816, tk=1024, name="dw_in", split=(4, 2),
                                             comm=_join(scatter_comm(parts_mix), gather8_comm(early)))
    own_mix = chips_summed(parts_mix, landed_mix)
    parts_in = halves_summed([gr_in], "exchange_in")
    da1, (landed_in, *other_rest) = matmul(dproj, w_in_f, mode="nt", out_dtype=F32, tm=1024, tn=1024, tk=2816, name="da1",
                                           comm=_join(scatter_comm(parts_in), share_comm(own_mix + own_ff)))
    own_in = chips_summed(parts_in, [landed_in])
    other_in = comm_call(share_comm(own_in), "share_w_in")
    own, other = own_in + own_mix + own_ff, list(other_in) + other_rest
    grad_x, d_sh1, d_sc1, d_g_pre_mix = mix_norm_bwd(da1, h0, dh1, g_pre_mix, sc1)
    out = {}

    mine = _pack([d_sh1, d_sc1, d_gt1, d_sh2, d_sc2, d_gt2, d_g_pre_mix, d_g_post_mix, d_g_pre_ffn, d_g_post_ffn, d_g_hgrn, d_lb])
    got = all_gather_small(mine, "gather_small_grads")
    g_b_ada, g_g1, g_g2, g_g3, g_g4, g_hg, g_lb = _unpack(
        sum_devices(got, "sum_small_grads"), [(1, 6 * D), (1, D), (1, D), (1, D), (1, D), (1, HEAD_DIM), (2, 1024)])
    g_sg, g_ws, g_bs = _unpack(sum_devices(got_early, "sum_sgu_grads"), [(1, 1024), w_spatial.shape, b_spatial.shape])
    g_lbl = lax.dynamic_slice(lb_logits_grad(g_lb, lb_full), (0, 0, chip * n_lb), (2, 2, n_lb))
    names = ["b_ada", "g_pre_mix", "g_post_mix", "g_pre_ffn", "g_post_ffn", "g_hgrn_norm", "g_sgu_norm", "w_spatial", "b_spatial", "lb_logits"]
    ws = [b_ada, g_pre_mix, g_post_mix, g_pre_ffn, g_post_ffn, g_hgrn_norm, g_sgu_norm, w_spatial, b_spatial, lb_logits]
    gs = [g_b_ada, g_g1, g_g2, g_g3, g_g4, g_hg, g_sg, g_ws, g_bs, g_lbl]
    ms = [m_b_ada, m_g_pre_mix, m_g_post_mix, m_g_pre_ffn, m_g_post_ffn, m_g_hgrn_norm, m_g_sgu_norm, m_w_spatial, m_b_spatial, m_lb_logits]
    vs = [v_b_ada, v_g_pre_mix, v_g_post_mix, v_g_pre_ffn, v_g_post_ffn, v_g_hgrn_norm, v_g_sgu_norm, v_w_spatial, v_b_spatial, v_lb_logits]
    shapes = [w.shape for w in ws]
    upd = adamw(_pack(ws), _pack(gs), _pack(ms), _pack(vs), "adamw_small")
    upd = [_unpack(u, shapes) for u in upd]
    for i, nm in enumerate(names):
        out[nm] = (gs[i], upd[0][i], upd[1][i], upd[2][i])

    dmod_all = got[:, :6 * D // 128, :].reshape(8, 6 * D)
    dmod_chip = lax.dynamic_slice(dmod_all, (0, chip * n_ada), (8, n_ada))
    out["w_ada"] = tuple(a[None] for a in wada_update(c_all, dmod_chip, w_ada[0], m_w_ada[0], v_w_ada[0]))
    for (nm, w, _), a, b, m, v in zip(big, own, other, (m_w_in, m_w_a_out, m_w_b_out, m_w_o, m_w_ff1, m_w_ff2),
                                      (v_w_in, v_w_a_out, v_w_b_out, v_w_o, v_w_ff1, v_w_ff2)):
        out[nm] = tuple(t[None] for t in adamw_halves(w[0], a, b, m[0], v[0], core, "adamw_" + nm))

    order = ["w_ada", "b_ada", "g_pre_mix", "g_post_mix", "g_pre_ffn", "g_post_ffn", "w_in", "lb_logits", "g_hgrn_norm", "w_a_out",
             "g_sgu_norm", "w_spatial", "b_spatial", "w_b_out", "w_o", "w_ff1", "w_ff2"]
    return (loss, grad_x[None], *[out[nm][0] for nm in order], *[out[nm][1] for nm in order], *[out[nm][2] for nm in order],
            *[out[nm][3] for nm in order])
```

```python
import functools
import math

import jax
import jax.numpy as jnp
from jax import lax
from jax.experimental import pallas as pl
from jax.experimental.pallas import tpu as pltpu

F32, BF16 = jnp.float32, jnp.bfloat16
HI = lax.Precision.HIGHEST
MESH = pl.DeviceIdType.MESH
ANY = pl.BlockSpec(memory_space=pl.ANY)

EPS = 1e-6
D_MODEL = 2048
N_HEADS = 8
HEAD_DIM = 128
HGRN_CHUNK = 32
HGRN_BLOCK = 256
SGU_CHUNK = 128
SGU_GROUPS = 8
Q_SCALE = HEAD_DIM ** -0.5
COL_Q, COL_FFW, COL_FBW, COL_V, COL_OG, COL_U, COL_ZV, COL_GA, COL_GB = 0, 1, 2, 3, 4, 5, 6, 7, 9
N_PROJ = 11264
VMEM_BYTES_V7X = 64 * 1024 * 1024
VMEM_LIMIT = VMEM_BYTES_V7X - 8 * 1024 * 1024

ADAM_LR, ADAM_B1, ADAM_B2, ADAM_EPS, ADAM_WD, ADAM_STEP = 0.001, 0.9, 0.999, 1e-08, 0.01, 10
ADAM_C1 = 1.0 - ADAM_B1 ** ADAM_STEP
ADAM_C2 = 1.0 - ADAM_B2 ** ADAM_STEP


def _cp(*sem):
    return pltpu.CompilerParams(dimension_semantics=sem if sem else None, vmem_limit_bytes=VMEM_LIMIT)


def _vec(d):
    return pl.BlockSpec((1, d), lambda *_: (0, 0))


def _colsum(x):
    return jnp.sum(x, axis=0, keepdims=True)


def _nt(a, b):
    return lax.dot_general(a, b, (((1,), (1,)), ((), ())), preferred_element_type=F32)


def _tn(a, b):
    return lax.dot_general(a, b, (((0,), (0,)), ((), ())), preferred_element_type=F32)


def _nn(a, b):
    return jnp.dot(a, b, preferred_element_type=F32)


def _adamw(w, g, m, v):
    m2 = ADAM_B1 * m + (1.0 - ADAM_B1) * g
    v2 = ADAM_B2 * v + (1.0 - ADAM_B2) * (g * g)
    delta = -ADAM_LR * ((m2 / ADAM_C1) / (jnp.sqrt(v2 / ADAM_C2) + ADAM_EPS) + ADAM_WD * w)
    return delta, m2, v2


class _Comm:
    def __init__(self, operands, out_shape, aliases, n_sems, start, finish):
        self.operands, self.out_shape, self.aliases, self.n_sems = list(operands), list(out_shape), dict(aliases), n_sems
        self.start, self.finish = start, finish


def _pallas(body, *, name, grid, in_specs, out_specs, out_shape, scratch, semantics, operands, comm=None):
    if comm is None:
        res = pl.pallas_call(body, name=name, grid=grid, in_specs=in_specs, out_specs=out_specs, out_shape=out_shape,
                             scratch_shapes=scratch, compiler_params=_cp(*semantics))(*operands)
        return res, []
    n_in, n_out, n_scr = len(in_specs), len(out_specs), len(scratch)
    nci, nco = len(comm.operands), len(comm.out_shape)

    def with_comm(*refs):
        ins, rest = refs[:n_in], refs[n_in:]
        cin, rest = rest[:nci], rest[nci:]
        outs, rest = rest[:n_out], rest[n_out:]
        cout, rest = rest[:nco], rest[nco:]
        scr, (send, recv) = rest[:n_scr], rest[n_scr:]
        ids = [pl.program_id(a) for a in range(len(grid))]
        first = functools.reduce(jnp.logical_and, [i == 0 for i in ids])
        last = functools.reduce(jnp.logical_and, [i == g - 1 for i, g in zip(ids, grid)])

        @pl.when(first)
        def _():
            comm.start(cin, cout, send, recv)

        body(*ins, *outs, *scr)

        @pl.when(last)
        def _():
            comm.finish(cin, cout, send, recv)

    res = pl.pallas_call(
        with_comm, name=name, grid=grid, in_specs=list(in_specs) + [ANY] * nci, out_specs=list(out_specs) + [ANY] * nco,
        out_shape=list(out_shape) + comm.out_shape, input_output_aliases={n_in + i: n_out + o for i, o in comm.aliases.items()},
        scratch_shapes=list(scratch) + [pltpu.SemaphoreType.DMA((comm.n_sems,)), pltpu.SemaphoreType.DMA((comm.n_sems,))],
        compiler_params=_cp(*["arbitrary"] * len(grid)),
    )(*operands, *comm.operands)
    return res[:n_out], res[n_out:]


def matmul(a, b, *, mode, out_dtype, tm, tn, tk, name, split=None, comm=None, relu2=False):
    if mode == "tn":
        (K, M), (_, N) = a.shape, b.shape
    elif mode == "nt":
        (M, K), (N, _) = a.shape, b.shape
    else:
        (M, K), (_, N) = a.shape, b.shape
    tm, tn, tk = min(tm, M), min(tn, N), min(tk, K)
    nk = K // tk
    a_spec = pl.BlockSpec((tk, tm), lambda i, j, k: (k, i)) if mode == "tn" else pl.BlockSpec((tm, tk), lambda i, j, k: (i, k))
    b_spec = pl.BlockSpec((tn, tk), lambda i, j, k: (j, k)) if mode == "nt" else pl.BlockSpec((tk, tn), lambda i, j, k: (k, j))
    dot = {"nn": _nn, "nt": _nt, "tn": _tn}[mode]
    if split is None:
        out_shape = jax.ShapeDtypeStruct((M, N), out_dtype)
        out_spec = pl.BlockSpec((tm, tn), lambda i, j, k: (i, j))
    else:
        nj, nh = split
        rows, cols = M // nh, N // nj
        tm, tn = min(tm, rows), min(tn, cols)
        bi, bj = rows // tm, cols // tn
        out_shape = jax.ShapeDtypeStruct((nj, nh, rows, cols), out_dtype)
        out_spec = pl.BlockSpec((None, None, tm, tn), lambda i, j, k: (j // bj, i // bi, i % bi, j % bj))

    def finish(y, o_ref, sq_ref):
        o_ref[...] = y.astype(o_ref.dtype)
        if relu2:
            p = jnp.maximum(y, 0.0)
            sq_ref[0][...] = (p * p).astype(BF16)

    if nk == 1:
        def body(a_ref, b_ref, o_ref, *sq_ref):
            finish(dot(a_ref[...], b_ref[...]), o_ref, sq_ref)
        scratch = []
    else:
        def body(a_ref, b_ref, o_ref, *rest):
            acc_ref, k = rest[-1], pl.program_id(2)

            @pl.when(k == 0)
            def _():
                acc_ref[...] = jnp.zeros_like(acc_ref)

            acc_ref[...] += dot(a_ref[...], b_ref[...])

            @pl.when(k == nk - 1)
            def _():
                finish(acc_ref[...], o_ref, rest[:-1])
        scratch = [pltpu.VMEM((tm, tn), F32)]

    out_specs, out_shapes = [out_spec], [out_shape]
    if relu2:
        out_specs, out_shapes = out_specs + [out_spec], out_shapes + [jax.ShapeDtypeStruct(out_shape.shape, BF16)]
    outs, landed = _pallas(
        body, name=name, grid=(M // tm, N // tn, nk), in_specs=[a_spec, b_spec], out_specs=out_specs, out_shape=out_shapes,
        scratch=scratch, semantics=("parallel", "parallel", "arbitrary"), operands=(a, b), comm=comm)
    out = tuple(outs) if relu2 else outs[0]
    return out if comm is None else (out, landed)


def cast_into_full(w, kind, chip, name):
    r, cc = w.shape
    tr = min(r, 512)
    nb = r // tr

    def body(chip_ref, w_ref, o_ref):
        o_ref[...] = w_ref[...].astype(BF16)

    if kind == "col":
        full, out_map = (r, 4 * cc), lambda i, chip_ref: (i, chip_ref[0])
    else:
        full, out_map = (4 * r, cc), lambda i, chip_ref: (chip_ref[0] * nb + i, 0)
    return pl.pallas_call(
        body, name=name, out_shape=jax.ShapeDtypeStruct(full, BF16),
        grid_spec=pltpu.PrefetchScalarGridSpec(
            num_scalar_prefetch=1, grid=(nb,), in_specs=[pl.BlockSpec((tr, cc), lambda i, chip_ref: (i, 0))],
            out_specs=pl.BlockSpec((tr, cc), out_map)),
        compiler_params=_cp("parallel"),
    )(chip, w)


def mod_matmul(c_all, w_ada, b_ada):
    D, N = w_ada.shape
    tn = 1024

    def body(c_ref, w_ref, b_ref, o_ref):
        c = c_ref[...]
        sc = c * jax.nn.sigmoid(c)
        o_ref[...] = jnp.dot(sc, w_ref[...], precision=HI, preferred_element_type=F32) + b_ref[...]

    return pl.pallas_call(
        body, name="mod_matmul", out_shape=jax.ShapeDtypeStruct((8, N), F32), grid=(N // tn,),
        in_specs=[pl.BlockSpec((8, D), lambda j: (0, 0)), pl.BlockSpec((D, tn), lambda j: (0, j)),
                  pl.BlockSpec((1, tn), lambda j: (0, j))],
        out_specs=pl.BlockSpec((8, tn), lambda j: (0, j)), compiler_params=_cp("parallel"),
    )(c_all, w_ada, b_ada)


def prenorm(h, g, sc, sh):
    T, D = h.shape
    tm = min(256, T)

    def body(h_ref, g_ref, sc_ref, sh_ref, a_ref):
        x = h_ref[...]
        r = lax.rsqrt(jnp.mean(x * x, axis=-1, keepdims=True) + EPS)
        a_ref[...] = ((x * r) * g_ref[...] * (1.0 + sc_ref[...]) + sh_ref[...]).astype(BF16)

    row = pl.BlockSpec((tm, D), lambda i: (i, 0))
    return pl.pallas_call(
        body, name="prenorm", out_shape=jax.ShapeDtypeStruct((T, D), BF16), grid=(T // tm,),
        in_specs=[row, _vec(D), _vec(D), _vec(D)], out_specs=row, compiler_params=_cp("parallel"),
    )(h, g, sc, sh)


def in_proj_gathered(a, w_full, chip, dims, tail):
    T, D = a.shape
    rows, cc = dims
    tm, tn = min(512, T), cc // 2
    ni = T // tm
    half = rows // 2

    nt = len(tail.operands)

    def body(chip_ref, a_ref, w_in_ref, *rest):
        tail_in, (y_ref, w_ref), rest = rest[:nt], rest[nt:nt + 2], rest[nt + 2:]
        tail_out, (wbuf, wsem, send_sems, recv_sems, tail_send, tail_recv) = rest[:nt], rest[nt:]
        q, j, i = pl.program_id(0), pl.program_id(1), pl.program_id(2)
        mx, my, mc, _ = _place()
        me = chip_ref[0]

        def tile(block, jj):
            src = w_ref.at[:, pl.ds(pl.multiple_of(block * cc + jj * tn, 128), tn)]
            return pltpu.make_async_copy(src, wbuf.at[jj], wsem.at[jj])

        def rows_half(block, hh):
            return w_ref.at[pl.ds(pl.multiple_of(hh * half, 16), half), pl.ds(pl.multiple_of(block * cc, 128), cc)]

        def over_ici(s, block):
            peer = (1 - mx if s & 2 else mx, 1 - my if s & 1 else my, mc)
            reg = rows_half(block, mc)
            return pltpu.make_async_remote_copy(src_ref=reg, dst_ref=reg, send_sem=send_sems.at[s - 1], recv_sem=recv_sems.at[s - 1],
                                                device_id=peer, device_id_type=MESH)

        def over_d2d(s, block, hh):
            reg = rows_half(block, hh)
            return pltpu.make_async_remote_copy(src_ref=reg, dst_ref=reg, send_sem=send_sems.at[2 + s], recv_sem=recv_sems.at[2 + s],
                                                device_id=(mx, my, 1 - mc), device_id_type=MESH)

        def passed_on():
            reg = rows_half(me ^ 1, mc)
            return pltpu.make_async_remote_copy(src_ref=reg, dst_ref=reg, send_sem=send_sems.at[2], recv_sem=recv_sems.at[2],
                                                device_id=(1 - mx, my, mc), device_id_type=MESH)

        @pl.when((q == 0) & (j == 0) & (i == 0))
        def _():
            for s in (1, 2):
                over_ici(s, me).start()
            tile(me, 0).start()

        @pl.when(i == 0)
        def _():
            tile(me ^ q, j).wait()

        @pl.when((i == 0) & (j == 0))
        def _():
            tile(me ^ q, 1).start()

        y_ref[...] = _nn(a_ref[...], wbuf[j])

        for s in (1, 2, 3):
            @pl.when((q == s - 1) & (j == 1) & (i == ni - 1))
            def _():
                block = me ^ s
                over_ici(s, block).wait_recv()
                if s == 1:
                    passed_on().start()
                    tail.start(tail_in, tail_out, tail_send, tail_recv)
                over_d2d(s, block, mc).start()
                over_d2d(s, block, 1 - mc).wait_recv()
                tile(block, 0).start()

        @pl.when((q == 3) & (j == 1) & (i == ni - 1))
        def _():
            for s in (1, 2):
                over_ici(s, me).wait_send()
            passed_on().wait_send()
            for s in (1, 2, 3):
                over_d2d(s, me ^ s, mc).wait_send()
            tail.finish(tail_in, tail_out, tail_send, tail_recv)

    dma = pltpu.SemaphoreType.DMA
    y, w_out, *tail_res = pl.pallas_call(
        body, name="in_proj", out_shape=[jax.ShapeDtypeStruct((T, 4 * cc), F32), jax.ShapeDtypeStruct(w_full.shape, BF16)] + tail.out_shape,
        grid_spec=pltpu.PrefetchScalarGridSpec(
            num_scalar_prefetch=1, grid=(4, 2, ni),
            in_specs=[pl.BlockSpec((tm, D), lambda q, j, i, chip_ref: (i, 0)), ANY] + [ANY] * nt,
            out_specs=[pl.BlockSpec((tm, tn), lambda q, j, i, chip_ref: (i, (chip_ref[0] ^ q) * 2 + j)), ANY] + [ANY] * nt,
            scratch_shapes=[pltpu.VMEM((2, D, tn), BF16), dma((2,)), dma((6,)), dma((6,)), dma((tail.n_sems,)), dma((tail.n_sems,))]),
        input_output_aliases={2: 1, **{3 + i: 2 + o for i, o in tail.aliases.items()}},
        compiler_params=_cp("arbitrary", "arbitrary", "arbitrary"),
    )(chip, a, w_full, *tail.operands)
    return y, w_out, tail_res


def _hgrn_lower_bound(l_ref):
    l0, l1 = l_ref[0:1, :], l_ref[1:2, :]
    m = jnp.maximum(l0, l1)
    e0, e1 = jnp.exp(l0 - m), jnp.exp(l1 - m)
    return e0 / (e0 + e1)


def _hgrn_chunk_mask(d):
    r = lax.broadcasted_iota(jnp.int32, (HGRN_BLOCK, HGRN_BLOCK), 0)
    c = lax.broadcasted_iota(jnp.int32, (HGRN_BLOCK, HGRN_BLOCK), 1)
    same = (r // HGRN_CHUNK) == (c // HGRN_CHUNK)
    fwd = d == 0
    return same & (((c <= r) & fwd) | ((c >= r) & jnp.logical_not(fwd)))


def _chunk_total(x):
    x3 = x.reshape(HGRN_BLOCK // HGRN_CHUNK, HGRN_CHUNK, x.shape[1])
    return jnp.broadcast_to(jnp.sum(x3, axis=1, keepdims=True), x3.shape).reshape(x.shape)


def _chunk_cumsum(x, suffix):
    pos = lax.broadcasted_iota(jnp.int32, x.shape, 0) % HGRN_CHUNK
    p, s = x, 1
    while s < HGRN_CHUNK:
        p = p + jnp.where(pos >= s, pltpu.roll(p, s, 0), 0.0)
        s *= 2
    return jnp.where(suffix, _chunk_total(x) - p + x, p)


def _block_loop(T, body, init):
    n = T // HGRN_BLOCK
    return lax.fori_loop(0, n, body, init, unroll=2 if n % 2 == 0 else 1)


def _hgrn_gate(f, lb):
    s = jax.nn.sigmoid(f)
    sn = jax.nn.sigmoid(-f)
    fg = lb + (1.0 - lb) * s
    return s, sn, fg, jnp.log(fg), (1.0 - lb) * sn


def _hgrn_specs(T):
    col = lambda base: pl.BlockSpec((T, HEAD_DIM), lambda h, d: (0, base * N_HEADS + h))
    f_spec = pl.BlockSpec((T, HEAD_DIM), lambda h, d: (0, COL_FFW * N_HEADS + N_HEADS * d + h))
    l_spec = pl.BlockSpec((None, 2, HEAD_DIM), lambda h, d: (d, 0, h))
    return col, f_spec, l_spec


def hgrn_fwd(proj, lb_logits, comm=None):
    T = proj.shape[0]
    NC, CPB = T // HGRN_CHUNK, HGRN_BLOCK // HGRN_CHUNK
    col, f_spec, l_spec = _hgrn_specs(T)

    def body(l_ref, q_ref, f_ref, v_ref, o_ref, st_ref, dec_ref, qd_ref):
        d = pl.program_id(1)
        lb = _hgrn_lower_bound(l_ref)
        mask = _hgrn_chunk_mask(d)

        def block(i, carry):
            rows = pl.ds(pl.multiple_of(i * HGRN_BLOCK, HGRN_BLOCK), HGRN_BLOCK)
            _, _, _, lf, k = _hgrn_gate(f_ref[rows, :], lb)
            b = _chunk_cumsum(lf, d == 1)
            bl = _chunk_total(lf)
            qd = (q_ref[rows, :] * Q_SCALE * jnp.exp(b)).astype(BF16)
            kd = (k * jnp.exp(-b)).astype(BF16)
            ke = (k * jnp.exp(bl - b)).astype(BF16)
            vb = v_ref[rows, :].astype(BF16)
            att = jnp.where(mask, _nt(qd, kd), 0.0).astype(BF16)
            o_ref[rows, :] = jnp.where(d == 0, 0.0, o_ref[rows, :]) + _nn(att, vb)
            qd_ref[rows, :] = qd
            dec = jnp.exp(bl)
            for cc in range(CPB):
                sl = slice(cc * HGRN_CHUNK, (cc + 1) * HGRN_CHUNK)
                n = i * CPB + cc
                st_ref[n] = _tn(vb[sl], ke[sl])
                dec_ref[n] = dec[cc * HGRN_CHUNK:cc * HGRN_CHUNK + 8, :]
            return carry

        _block_loop(T, block, 0)

        def scan(t, s):
            n = jnp.where(d == 0, t, NC - 1 - t)
            u = st_ref[n]
            st_ref[n] = s
            return dec_ref[n][0:1, :] * s + u

        lax.fori_loop(0, NC, scan, jnp.zeros((HEAD_DIM, HEAD_DIM), F32))

        def inter(i, carry):
            rows = pl.ds(pl.multiple_of(i * HGRN_BLOCK, HGRN_BLOCK), HGRN_BLOCK)
            qd = qd_ref[rows, :]
            o_ref[rows, :] += jnp.concatenate(
                [_nt(qd[cc * HGRN_CHUNK:(cc + 1) * HGRN_CHUNK], st_ref[i * CPB + cc].astype(BF16)) for cc in range(CPB)], axis=0)
            return carry

        _block_loop(T, inter, 0)

    (o,), landed = _pallas(
        body, name="hgrn_fwd", grid=(N_HEADS, 2), in_specs=[l_spec, col(COL_Q), f_spec, col(COL_V)],
        out_specs=[pl.BlockSpec((T, HEAD_DIM), lambda h, d: (0, h))], out_shape=[jax.ShapeDtypeStruct((T, N_HEADS * HEAD_DIM), F32)],
        scratch=[pltpu.VMEM((NC, HEAD_DIM, HEAD_DIM), F32), pltpu.VMEM((NC, 8, HEAD_DIM), F32), pltpu.VMEM((T, HEAD_DIM), BF16)],
        semantics=("parallel", "arbitrary"), operands=(lb_logits, proj, proj, proj), comm=comm)
    return o if comm is None else (o, landed)


def hgrn_post_fwd(o, proj, g_norm):
    T, W = o.shape
    tm = min(256, T)

    def body(o_ref, og_ref, g_ref, y_ref):
        g = g_ref[...]
        for h in range(N_HEADS):
            sl = slice(h * HEAD_DIM, (h + 1) * HEAD_DIM)
            x = o_ref[:, sl]
            r = lax.rsqrt(jnp.mean(x * x, axis=-1, keepdims=True) + EPS)
            og = og_ref[:, sl]
            y_ref[:, sl] = ((x * r) * g * (og * jax.nn.sigmoid(og))).astype(BF16)

    return pl.pallas_call(
        body, name="hgrn_post_fwd", out_shape=jax.ShapeDtypeStruct((T, W), BF16), grid=(T // tm,),
        in_specs=[pl.BlockSpec((tm, W), lambda i: (i, 0)), pl.BlockSpec((tm, W), lambda i: (i, COL_OG)), _vec(HEAD_DIM)],
        out_specs=pl.BlockSpec((tm, W), lambda i: (i, 0)), compiler_params=_cp("parallel"),
    )(o, proj, g_norm)


def _gelu(x):
    return 0.5 * x * (1.0 + lax.erf(x * (1.0 / math.sqrt(2.0))))


def _gelu_grad(x):
    return 0.5 * (1.0 + lax.erf(x * (1.0 / math.sqrt(2.0)))) + x * jnp.exp(-0.5 * x * x) * (1.0 / math.sqrt(2.0 * math.pi))


def _sgu_mix(u_ref, v_ref, g_ref, ws_ref, bst_ref):
    W = u_ref.shape[1]
    zu, zv = _gelu(u_ref[...]), _gelu(v_ref[...])
    dv = zv - jnp.mean(zv, axis=-1, keepdims=True)
    rstd = lax.rsqrt(jnp.mean(dv * dv, axis=-1, keepdims=True) + EPS)
    dhat = dv * rstd
    vn = (dhat * g_ref[...]).astype(BF16)
    gw = W // SGU_GROUPS
    vm = [_nn(ws_ref[g].astype(BF16), vn[:, g * gw:(g + 1) * gw]) + bst_ref[:, g:g + 1] for g in range(SGU_GROUPS)]
    return zu, rstd, dhat, vn, jnp.concatenate(vm, axis=1)


def sgu_fwd(proj, g_norm, w_spatial, b_spatial_t):
    T = proj.shape[0]
    W = 1024
    n_chunks = T // SGU_CHUNK

    def body(u_ref, v_ref, g_ref, ws_ref, bst_ref, y_ref):
        zu, _, _, _, vm = _sgu_mix(u_ref, v_ref, g_ref, ws_ref, bst_ref)
        y_ref[...] = (zu * vm).astype(BF16)

    blk = lambda cb: pl.BlockSpec((SGU_CHUNK, W), lambda i: (i, cb))
    return pl.pallas_call(
        body, name="sgu_fwd", out_shape=jax.ShapeDtypeStruct((T, W), BF16), grid=(n_chunks,),
        in_specs=[blk(COL_U), blk(COL_ZV), _vec(W), pl.BlockSpec((SGU_GROUPS, SGU_CHUNK, SGU_CHUNK), lambda i: (0, 0, 0)),
                  pl.BlockSpec((SGU_CHUNK, SGU_GROUPS), lambda i: (0, 0))],
        out_specs=blk(0), compiler_params=_cp("parallel"),
    )(proj, proj, g_norm, w_spatial, b_spatial_t)


def merge_matmul(ya_pre, sgu, w_a, w_b, proj):
    T, K = ya_pre.shape
    N = w_a.shape[1]
    tm, tn = min(512, T), 512
    gpb = 1024 // tn

    def body(a_ref, b_ref, wa_ref, wb_ref, ga_ref, gb_ref, ya_ref, yb_ref, m_ref):
        ya = _nn(a_ref[...], wa_ref[...])
        yb = _nn(b_ref[...], wb_ref[...])
        ya_ref[...] = ya.astype(BF16)
        yb_ref[...] = yb.astype(BF16)
        m_ref[...] = (jax.nn.sigmoid(ga_ref[...]) * ya + jax.nn.sigmoid(gb_ref[...]) * yb).astype(BF16)

    lhs = pl.BlockSpec((tm, K), lambda i, j: (i, 0))
    rhs = pl.BlockSpec((K, tn), lambda i, j: (0, j))
    out = pl.BlockSpec((tm, tn), lambda i, j: (i, j))
    return pl.pallas_call(
        body, name="merge_matmul", grid=(T // tm, N // tn),
        out_shape=[jax.ShapeDtypeStruct((T, N), BF16)] * 3,
        in_specs=[lhs, lhs, rhs, rhs, pl.BlockSpec((tm, tn), lambda i, j: (i, COL_GA * gpb + j)),
                  pl.BlockSpec((tm, tn), lambda i, j: (i, COL_GB * gpb + j))],
        out_specs=[out, out, out], compiler_params=_cp("parallel", "parallel"),
    )(ya_pre, sgu, w_a, w_b, proj, proj)


def out_proj(merged, w_o, h0, gt1, g_post, g_pre2, sc2, sh2):
    T, D = h0.shape
    tm = min(256, T)

    def body(m_ref, w_ref, h_ref, gt_ref, gp_ref, g2_ref, sc_ref, sh_ref, mo_ref, h1_ref, a2_ref):
        mo = _nn(m_ref[...], w_ref[...])
        mo_ref[...] = mo
        r = lax.rsqrt(jnp.mean(mo * mo, axis=-1, keepdims=True) + EPS)
        h1 = h_ref[...] + gt_ref[...] * ((mo * r) * gp_ref[...])
        h1_ref[...] = h1
        r2 = lax.rsqrt(jnp.mean(h1 * h1, axis=-1, keepdims=True) + EPS)
        a2_ref[...] = ((h1 * r2) * g2_ref[...] * (1.0 + sc_ref[...]) + sh_ref[...]).astype(BF16)

    row = pl.BlockSpec((tm, D), lambda i: (i, 0))
    return pl.pallas_call(
        body, name="out_proj", grid=(T // tm,),
        out_shape=[jax.ShapeDtypeStruct((T, D), F32), jax.ShapeDtypeStruct((T, D), F32), jax.ShapeDtypeStruct((T, D), BF16)],
        in_specs=[row, pl.BlockSpec((D, D), lambda i: (0, 0)), row] + [_vec(D)] * 5,
        out_specs=[row, row, row], compiler_params=_cp("parallel"),
    )(merged, w_o, h0, gt1, g_post, g_pre2, sc2, sh2)


def loss_bwd(ff, h1, tgt, gt2, g_post):
    T, D = ff.shape
    tm = min(256, T)

    def body(f_ref, h_ref, t_ref, gt_ref, g_ref, dy_ref, dff_ref, loss_ref, dgt_ref, dg_ref):
        @pl.when(pl.program_id(0) == 0)
        def _():
            loss_ref[...] = jnp.zeros_like(loss_ref)
            dgt_ref[...] = jnp.zeros_like(dgt_ref)
            dg_ref[...] = jnp.zeros_like(dg_ref)

        ff = f_ref[...]
        gt, g = gt_ref[...], g_ref[...]
        r = lax.rsqrt(jnp.mean(ff * ff, axis=-1, keepdims=True) + EPS)
        fhat = ff * r
        nf = fhat * g
        err = (h_ref[...] + gt * nf) - t_ref[...]
        loss_ref[...] += jnp.sum(err * err)
        dy = err * (1.0 / D)
        dy_ref[...] = dy
        dgt_ref[...] += _colsum(dy * nf)
        dnf = dy * gt
        dg_ref[...] += _colsum(dnf * fhat)
        u = dnf * g
        dff_ref[...] = (r * (u - fhat * jnp.mean(u * fhat, axis=-1, keepdims=True))).astype(BF16)

    row = pl.BlockSpec((tm, D), lambda i: (i, 0))
    return pl.pallas_call(
        body, name="loss_bwd", grid=(T // tm,),
        out_shape=[jax.ShapeDtypeStruct((T, D), F32), jax.ShapeDtypeStruct((T, D), BF16), jax.ShapeDtypeStruct((8, 128), F32),
                   jax.ShapeDtypeStruct((1, D), F32), jax.ShapeDtypeStruct((1, D), F32)],
        in_specs=[row, row, row, _vec(D), _vec(D)],
        out_specs=[row, row, pl.BlockSpec((8, 128), lambda i: (0, 0)), _vec(D), _vec(D)],
        compiler_params=_cp("arbitrary"),
    )(ff, h1, tgt, gt2, g_post)


def ff2_bwd(dff, w_ff2, f1):
    T, D = dff.shape
    K = w_ff2.shape[0]
    tm, tn = min(512, T), 2048

    def body(a_ref, w_ref, f_ref, o_ref):
        o_ref[...] = (_nt(a_ref[...], w_ref[...]) * (2.0 * jnp.maximum(f_ref[...].astype(F32), 0.0))).astype(BF16)

    return pl.pallas_call(
        body, name="ff2_bwd", out_shape=jax.ShapeDtypeStruct((T, K), BF16), grid=(K // tn, T // tm),
        in_specs=[pl.BlockSpec((tm, D), lambda j, i: (i, 0)), pl.BlockSpec((tn, D), lambda j, i: (j, 0)),
                  pl.BlockSpec((tm, tn), lambda j, i: (i, j))],
        out_specs=pl.BlockSpec((tm, tn), lambda j, i: (i, j)), compiler_params=_cp("parallel", "parallel"),
    )(dff, w_ff2, f1)


def ffn_norm_bwd(dy, da2, h1, mo, g_pre2, sc2, gt1, g_post, comm):
    T, D = dy.shape
    tm = min(256, T)

    def body(dy_ref, da_ref, h_ref, mo_ref, g2_ref, sc_ref, gt_ref, gp_ref, dh_ref, dmo_ref, s_sh, s_sc, s_g2, s_gt, s_gp):
        @pl.when(pl.program_id(0) == 0)
        def _():
            for s in (s_sh, s_sc, s_g2, s_gt, s_gp):
                s[...] = jnp.zeros_like(s)

        h1, da = h_ref[...], da_ref[...]
        g2, sc = g2_ref[...], sc_ref[...]
        r2 = lax.rsqrt(jnp.mean(h1 * h1, axis=-1, keepdims=True) + EPS)
        n2 = h1 * r2
        s_sh[...] += _colsum(da)
        s_sc[...] += _colsum(da * (n2 * g2))
        s_g2[...] += _colsum(da * (1.0 + sc) * n2)
        dn2 = da * g2 * (1.0 + sc)
        dh1 = dy_ref[...] + r2 * (dn2 - n2 * jnp.mean(dn2 * n2, axis=-1, keepdims=True))
        dh_ref[...] = dh1
        mo = mo_ref[...]
        gt, gp = gt_ref[...], gp_ref[...]
        r = lax.rsqrt(jnp.mean(mo * mo, axis=-1, keepdims=True) + EPS)
        mhat = mo * r
        s_gt[...] += _colsum(dh1 * (mhat * gp))
        dnm = dh1 * gt
        s_gp[...] += _colsum(dnm * mhat)
        u = dnm * gp
        dmo_ref[...] = (r * (u - mhat * jnp.mean(u * mhat, axis=-1, keepdims=True))).astype(BF16)

    row = pl.BlockSpec((tm, D), lambda i: (i, 0))
    vec_out = jax.ShapeDtypeStruct((1, D), F32)
    return _pallas(
        body, name="ffn_norm_bwd", grid=(T // tm,),
        out_shape=[jax.ShapeDtypeStruct((T, D), F32), jax.ShapeDtypeStruct((T, D), BF16)] + [vec_out] * 5,
        in_specs=[row, row, row, row] + [_vec(D)] * 4, out_specs=[row, row] + [_vec(D)] * 5,
        scratch=[], semantics=("arbitrary",), operands=(dy, da2, h1, mo, g_pre2, sc2, gt1, g_post), comm=comm)


def out_proj_bwd(dmo, w_o, y_a, y_b, proj):
    T, D = dmo.shape
    tm, tn = min(512, T), 512
    gpb = 1024 // tn

    def body(a_ref, w_ref, ya_ref, yb_ref, ga_ref, gb_ref, dya_ref, dyb_ref, dga_ref, dgb_ref):
        dm = _nt(a_ref[...], w_ref[...])
        sa, sb = jax.nn.sigmoid(ga_ref[...]), jax.nn.sigmoid(gb_ref[...])
        dya_ref[...] = (dm * sa).astype(BF16)
        dyb_ref[...] = (dm * sb).astype(BF16)
        dga_ref[...] = (dm * ya_ref[...].astype(F32) * sa * (1.0 - sa)).astype(BF16)
        dgb_ref[...] = (dm * yb_ref[...].astype(F32) * sb * (1.0 - sb)).astype(BF16)

    out = pl.BlockSpec((tm, tn), lambda i, j: (i, j))
    return pl.pallas_call(
        body, name="out_proj_bwd", grid=(T // tm, D // tn), out_shape=[jax.ShapeDtypeStruct((T, D), BF16)] * 4,
        in_specs=[pl.BlockSpec((tm, D), lambda i, j: (i, 0)), pl.BlockSpec((tn, D), lambda i, j: (j, 0)), out, out,
                  pl.BlockSpec((tm, tn), lambda i, j: (i, COL_GA * gpb + j)), pl.BlockSpec((tm, tn), lambda i, j: (i, COL_GB * gpb + j))],
        out_specs=[out] * 4, compiler_params=_cp("parallel", "parallel"),
    )(dmo, w_o, y_a, y_b, proj, proj)


def sgu_bwd(proj, dsgu, g_norm, w_spatial, b_spatial_t):
    T = proj.shape[0]
    W = 1024
    gw = W // SGU_GROUPS

    def body(u_ref, v_ref, ds_ref, g_ref, ws_ref, bst_ref, dz_ref, dw_ref, db_ref, dg_ref):
        @pl.when(pl.program_id(0) == 0)
        def _():
            dw_ref[...] = jnp.zeros_like(dw_ref)
            db_ref[...] = jnp.zeros_like(db_ref)
            dg_ref[...] = jnp.zeros_like(dg_ref)

        zu, rstd, dhat, vn, vm = _sgu_mix(u_ref, v_ref, g_ref, ws_ref, bst_ref)
        ds = ds_ref[...]
        du = ds * vm
        dvm = ds * zu
        dvm_b = dvm.astype(BF16)
        ones = jnp.ones((8, gw), F32)
        dvn = []
        for g in range(SGU_GROUPS):
            sl = slice(g * gw, (g + 1) * gw)
            dw_ref[g] += _nt(dvm_b[:, sl], vn[:, sl])
            db_ref[g] += lax.dot_general(ones, dvm[:, sl], (((1,), (1,)), ((), ())), precision=HI, preferred_element_type=F32)
            dvn.append(_tn(ws_ref[g].astype(BF16), dvm_b[:, sl]))
        dvn = jnp.concatenate(dvn, axis=1)
        dg_ref[...] += _colsum(dvn * dhat)
        ddh = dvn * g_ref[...]
        dzv = rstd * (ddh - jnp.mean(ddh, axis=-1, keepdims=True) - dhat * jnp.mean(ddh * dhat, axis=-1, keepdims=True))
        dz_ref[:, 0:W] = (du * _gelu_grad(u_ref[...])).astype(BF16)
        dz_ref[:, W:2 * W] = (dzv * _gelu_grad(v_ref[...])).astype(BF16)

    blk = lambda cb: pl.BlockSpec((SGU_CHUNK, W), lambda i: (i, cb))
    full3 = lambda a, b, c: pl.BlockSpec((a, b, c), lambda i: (0, 0, 0))
    return pl.pallas_call(
        body, name="sgu_bwd", grid=(T // SGU_CHUNK,),
        out_shape=[jax.ShapeDtypeStruct((T, 2 * W), BF16), jax.ShapeDtypeStruct((SGU_GROUPS, SGU_CHUNK, SGU_CHUNK), F32),
                   jax.ShapeDtypeStruct((SGU_GROUPS, 8, SGU_CHUNK), F32), jax.ShapeDtypeStruct((1, W), F32)],
        in_specs=[blk(COL_U), blk(COL_ZV), blk(0), _vec(W), full3(SGU_GROUPS, SGU_CHUNK, SGU_CHUNK),
                  pl.BlockSpec((SGU_CHUNK, SGU_GROUPS), lambda i: (0, 0))],
        out_specs=[pl.BlockSpec((SGU_CHUNK, 2 * W), lambda i: (i, 0)), full3(SGU_GROUPS, SGU_CHUNK, SGU_CHUNK),
                   full3(SGU_GROUPS, 8, SGU_CHUNK), _vec(W)],
        compiler_params=_cp("arbitrary"),
    )(proj, proj, dsgu, g_norm, w_spatial, b_spatial_t)


def hgrn_post_bwd(dya, o, proj, g_norm):
    T, W = o.shape
    tm = min(256, T)

    def body(dy_ref, o_ref, og_ref, g_ref, do_ref, dog_ref, dg_ref):
        @pl.when(pl.program_id(0) == 0)
        def _():
            dg_ref[...] = jnp.zeros_like(dg_ref)

        g = g_ref[...]
        dg = jnp.zeros((1, HEAD_DIM), F32)
        for h in range(N_HEADS):
            sl = slice(h * HEAD_DIM, (h + 1) * HEAD_DIM)
            x, og, dy = o_ref[:, sl], og_ref[:, sl], dy_ref[:, sl]
            r = lax.rsqrt(jnp.mean(x * x, axis=-1, keepdims=True) + EPS)
            xhat = x * r
            s = jax.nn.sigmoid(og)
            don = dy * (og * s)
            dog_ref[:, sl] = (dy * (xhat * g) * (s * (1.0 + og * (1.0 - s)))).astype(BF16)
            dg += _colsum(don * xhat)
            u = don * g
            do_ref[:, sl] = r * (u - xhat * jnp.mean(u * xhat, axis=-1, keepdims=True))
        dg_ref[...] += dg

    row = pl.BlockSpec((tm, W), lambda i: (i, 0))
    return pl.pallas_call(
        body, name="hgrn_post_bwd", grid=(T // tm,),
        out_shape=[jax.ShapeDtypeStruct((T, W), F32), jax.ShapeDtypeStruct((T, W), BF16), jax.ShapeDtypeStruct((1, HEAD_DIM), F32)],
        in_specs=[row, row, pl.BlockSpec((tm, W), lambda i: (i, COL_OG)), _vec(HEAD_DIM)],
        out_specs=[row, row, _vec(HEAD_DIM)], compiler_params=_cp("arbitrary"),
    )(dya, o, proj, g_norm)


def hgrn_bwd(proj, do, lb_logits, comm=None):
    T = proj.shape[0]
    NC, CPB = T // HGRN_CHUNK, HGRN_BLOCK // HGRN_CHUNK
    W = N_HEADS * HEAD_DIM
    col, f_spec, l_spec = _hgrn_specs(T)

    def body(l_ref, q_ref, f_ref, v_ref, do_ref, dq_ref, dv_ref, dlg_ref, dlb_ref, st_ref, dst_ref, dec_ref, ddec_ref, dqa_ref, dva_ref):
        d = pl.program_id(1)
        lb = _hgrn_lower_bound(l_ref)
        oml = 1.0 - lb
        mask = _hgrn_chunk_mask(d)

        def values(rows):
            s, sn, fg, lf, k = _hgrn_gate(f_ref[rows, :], lb)
            b = _chunk_cumsum(lf, d == 1)
            bl = _chunk_total(lf)
            eb, enb, ee = jnp.exp(b), jnp.exp(-b), jnp.exp(bl - b)
            qd = q_ref[rows, :] * Q_SCALE * eb
            return s, sn, fg, k, bl, eb, enb, ee, qd, k * enb, k * ee

        def block1(i, carry):
            rows = pl.ds(pl.multiple_of(i * HGRN_BLOCK, HGRN_BLOCK), HGRN_BLOCK)
            _, _, _, _, bl, _, _, _, qd, _, ke = values(rows)
            qd, ke = qd.astype(BF16), ke.astype(BF16)
            vb, dob = v_ref[rows, :].astype(BF16), do_ref[rows, :].astype(BF16)
            dec = jnp.exp(bl)
            for cc in range(CPB):
                sl = slice(cc * HGRN_CHUNK, (cc + 1) * HGRN_CHUNK)
                n = i * CPB + cc
                st_ref[n] = _tn(vb[sl], ke[sl])
                dst_ref[n] = _tn(dob[sl], qd[sl])
                dec_ref[n] = dec[cc * HGRN_CHUNK:cc * HGRN_CHUNK + 8, :]
            return carry

        _block_loop(T, block1, 0)

        def scan(t, s):
            n = jnp.where(d == 0, t, NC - 1 - t)
            u = st_ref[n]
            st_ref[n] = s
            return dec_ref[n][0:1, :] * s + u

        lax.fori_loop(0, NC, scan, jnp.zeros((HEAD_DIM, HEAD_DIM), F32))

        def rscan(t, ds):
            n = jnp.where(d == 0, NC - 1 - t, t)
            w = dst_ref[n]
            dst_ref[n] = ds
            ddec_ref[n] = jnp.broadcast_to(_colsum(ds * st_ref[n]), (8, HEAD_DIM))
            return dec_ref[n][0:1, :] * ds + w

        lax.fori_loop(0, NC, rscan, jnp.zeros((HEAD_DIM, HEAD_DIM), F32))

        def block3(i, dlb):
            rows = pl.ds(pl.multiple_of(i * HGRN_BLOCK, HGRN_BLOCK), HGRN_BLOCK)
            s, sn, fg, k, bl, eb, enb, ee, qd, kd, ke = values(rows)
            qdb, kdb, keb = qd.astype(BF16), kd.astype(BF16), ke.astype(BF16)
            vb, dob = v_ref[rows, :].astype(BF16), do_ref[rows, :].astype(BF16)
            att = jnp.where(mask, _nt(qdb, kdb), 0.0).astype(BF16)
            datt = jnp.where(mask, _nt(dob, vb), 0.0).astype(BF16)
            dv = _tn(att, dob)
            dqd = _nn(datt, kdb)
            dkd = _tn(datt, qdb)
            dv_i, dqd_i, dke, ddl = [], [], [], []
            for cc in range(CPB):
                sl = slice(cc * HGRN_CHUNK, (cc + 1) * HGRN_CHUNK)
                n = i * CPB + cc
                st_b, dst_b = st_ref[n].astype(BF16), dst_ref[n].astype(BF16)
                dv_i.append(_nt(keb[sl], dst_b))
                dqd_i.append(_nn(dob[sl], st_b))
                dke.append(_nn(vb[sl], dst_b))
                ddl.append(jnp.broadcast_to(ddec_ref[n][0:1, :] * dec_ref[n][0:1, :], (HGRN_CHUNK, HEAD_DIM)))
            dv = dv + jnp.concatenate(dv_i, axis=0)
            dqd = dqd + jnp.concatenate(dqd_i, axis=0)
            dke = jnp.concatenate(dke, axis=0)
            dq = dqd * eb * Q_SCALE
            dk = dkd * enb + dke * ee
            t_end = dke * ke
            db = dqd * qd - dkd * kd - t_end
            dlf = _chunk_cumsum(db, d == 0) + _chunk_total(t_end) + jnp.concatenate(ddl, axis=0)
            e = dlf / fg - dk
            dlg_ref[rows, :] = (oml * e * s * sn).astype(BF16)

            dq = jnp.where(d == 0, 0.0, dqa_ref[rows, :]) + dq
            dv = jnp.where(d == 0, 0.0, dva_ref[rows, :]) + dv
            dqa_ref[rows, :] = dq
            dva_ref[rows, :] = dv
            dq_ref[rows, :] = dq.astype(BF16)
            dv_ref[rows, :] = dv.astype(BF16)

            return dlb + _colsum(e * sn)

        dlb_ref[...] = _block_loop(T, block3, jnp.zeros((1, HEAD_DIM), F32))

    head = pl.BlockSpec((T, HEAD_DIM), lambda h, d: (0, h))
    big = pltpu.VMEM((NC, HEAD_DIM, HEAD_DIM), F32)
    small = pltpu.VMEM((NC, 8, HEAD_DIM), F32)
    acc = pltpu.VMEM((T, HEAD_DIM), F32)
    outs, landed = _pallas(
        body, name="hgrn_bwd", grid=(N_HEADS, 2),
        out_shape=[jax.ShapeDtypeStruct((T, W), BF16), jax.ShapeDtypeStruct((T, W), BF16), jax.ShapeDtypeStruct((T, 2 * W), BF16),
                   jax.ShapeDtypeStruct((2, 1, W), F32)],
        in_specs=[l_spec, col(COL_Q), f_spec, col(COL_V), head],
        out_specs=[head, head, pl.BlockSpec((T, HEAD_DIM), lambda h, d: (0, N_HEADS * d + h)),
                   pl.BlockSpec((None, 1, HEAD_DIM), lambda h, d: (d, 0, h))],
        scratch=[big, big, small, small, acc, acc], semantics=("parallel", "arbitrary"), operands=(lb_logits, proj, proj, proj, do), comm=comm)
    return outs if comm is None else (outs, landed)


def mix_norm_bwd(da1, h0, dh1, g_pre, sc1):
    T, D = h0.shape
    tm = min(256, T)

    def body(da_ref, h_ref, dh_ref, g_ref, sc_ref, gx_ref, s_sh, s_sc, s_g):
        @pl.when(pl.program_id(0) == 0)
        def _():
            for s in (s_sh, s_sc, s_g):
                s[...] = jnp.zeros_like(s)

        h, da = h_ref[...], da_ref[...]
        g, sc = g_ref[...], sc_ref[...]
        r = lax.rsqrt(jnp.mean(h * h, axis=-1, keepdims=True) + EPS)
        n = h * r
        s_sh[...] += _colsum(da)
        s_sc[...] += _colsum(da * (n * g))
        s_g[...] += _colsum(da * (1.0 + sc) * n)
        dn = da * g * (1.0 + sc)
        gx_ref[...] = dh_ref[...] + r * (dn - n * jnp.mean(dn * n, axis=-1, keepdims=True))

    row = pl.BlockSpec((tm, D), lambda i: (i, 0))
    return pl.pallas_call(
        body, name="mix_norm_bwd", grid=(T // tm,),
        out_shape=[jax.ShapeDtypeStruct((T, D), F32)] + [jax.ShapeDtypeStruct((1, D), F32)] * 3,
        in_specs=[row, row, row, _vec(D), _vec(D)], out_specs=[row] + [_vec(D)] * 3, compiler_params=_cp("arbitrary"),
    )(da1, h0, dh1, g_pre, sc1)


def adamw(w, g, m, v, name):
    R, C = w.shape
    tr = R if R * C * 4 <= (1 << 21) else max(8, ((1 << 21) // (C * 4)) // 8 * 8)
    while R % tr:
        tr -= 8

    def body(w_ref, g_ref, m_ref, v_ref, d_ref, m2_ref, v2_ref):
        d_ref[...], m2_ref[...], v2_ref[...] = _adamw(w_ref[...], g_ref[...], m_ref[...], v_ref[...])

    row = pl.BlockSpec((tr, C), lambda i: (i, 0))
    return pl.pallas_call(
        body, name=name, grid=(R // tr,), out_shape=[jax.ShapeDtypeStruct((R, C), F32)] * 3,
        in_specs=[row] * 4, out_specs=[row] * 3, compiler_params=_cp("parallel"),
    )(w, g, m, v)


def wada_update(c_all, dmod, w, m, v):
    D, N = w.shape
    tm, tn = 512, 1024

    def body(c_ref, dm_ref, w_ref, m_ref, v_ref, g_ref, d_ref, m2_ref, v2_ref):
        c = c_ref[...]
        g = lax.dot_general(c * jax.nn.sigmoid(c), dm_ref[...], (((0,), (0,)), ((), ())), precision=HI, preferred_element_type=F32)
        g_ref[...] = g
        d_ref[...], m2_ref[...], v2_ref[...] = _adamw(w_ref[...], g, m_ref[...], v_ref[...])

    blk = pl.BlockSpec((tm, tn), lambda i, j: (i, j))
    return pl.pallas_call(
        body, name="wada_update", grid=(D // tm, N // tn), out_shape=[jax.ShapeDtypeStruct((D, N), F32)] * 4,
        in_specs=[pl.BlockSpec((8, tm), lambda i, j: (0, i)), pl.BlockSpec((8, tn), lambda i, j: (0, j)), blk, blk, blk],
        out_specs=[blk] * 4, compiler_params=_cp("parallel", "parallel"),
    )(c_all, dmod, w, m, v)


def sum_devices(gathered, name):
    n, R, C = gathered.shape

    def body(g_ref, o_ref):
        s = g_ref[0]
        for i in range(1, n):
            s = s + g_ref[i]
        o_ref[...] = s

    return pl.pallas_call(body, name=name, out_shape=jax.ShapeDtypeStruct((R, C), F32), compiler_params=_cp())(gathered)


def lb_logits_grad(dlb, lb_logits):
    def body(d_ref, l_ref, o_ref):
        for d in range(2):
            l0, l1 = l_ref[d, 0:1, :], l_ref[d, 1:2, :]
            m = jnp.maximum(l0, l1)
            e0, e1 = jnp.exp(l0 - m), jnp.exp(l1 - m)
            p0, p1 = e0 / (e0 + e1), e1 / (e0 + e1)
            g = d_ref[d:d + 1, :]
            o_ref[d, 0:1, :] = p0 * (g - p0 * g)
            o_ref[d, 1:2, :] = -p1 * (p0 * g)

    return pl.pallas_call(body, name="lb_logits_grad", out_shape=jax.ShapeDtypeStruct(lb_logits.shape, F32), compiler_params=_cp())(dlb, lb_logits)


def add_halves(g, landed, core):
    nj, _, r, cc = g.shape
    tr = min(256, r)

    def body(core_ref, g_ref, l_ref, o_ref):
        o_ref[...] = (g_ref[...].astype(F32) + l_ref[...].astype(F32)).astype(BF16)

    return pl.pallas_call(
        body, name="add_halves_%dx%d" % (r, cc), out_shape=jax.ShapeDtypeStruct((nj, r, cc), BF16),
        grid_spec=pltpu.PrefetchScalarGridSpec(
            num_scalar_prefetch=1, grid=(nj, r // tr),
            in_specs=[pl.BlockSpec((None, None, tr, cc), lambda j, i, core_ref: (j, core_ref[0], i, 0)),
                      pl.BlockSpec((None, None, tr, cc), lambda j, i, core_ref: (j, 0, i, 0))],
            out_specs=pl.BlockSpec((None, tr, cc), lambda j, i, core_ref: (j, i, 0))),
        compiler_params=_cp("parallel", "parallel"),
    )(core, g, landed)


def sum_chips(parts, landed, chip):
    nj, r, cc = parts.shape
    tr = min(256, r)

    def body(chip_ref, p_ref, l_ref, o_ref):
        mine = p_ref[...].astype(F32)
        s = None
        for j in range(nj):
            t = jnp.where(chip_ref[0] == j, mine, l_ref[j].astype(F32))
            s = t if s is None else s + t
        o_ref[...] = s

    return pl.pallas_call(
        body, name="sum_chips_%dx%d" % (r, cc), out_shape=jax.ShapeDtypeStruct((r, cc), F32),
        grid_spec=pltpu.PrefetchScalarGridSpec(
            num_scalar_prefetch=1, grid=(r // tr,),
            in_specs=[pl.BlockSpec((None, tr, cc), lambda i, chip_ref: (chip_ref[0], i, 0)),
                      pl.BlockSpec((nj, tr, cc), lambda i, chip_ref: (0, i, 0))],
            out_specs=pl.BlockSpec((tr, cc), lambda i, chip_ref: (i, 0))),
        compiler_params=_cp("parallel"),
    )(chip, parts, landed)


def adamw_halves(w, own, other, m, v, core, name):
    r, cc = own.shape
    tr = min(128, r)
    nb = r // tr

    def body(core_ref, w_ref, a_ref, b_ref, m_ref, v_ref, g_ref, d_ref, m2_ref, v2_ref):
        g = jnp.where(pl.program_id(0) == core_ref[0], a_ref[...], b_ref[...])
        g_ref[...] = g
        d_ref[...], m2_ref[...], v2_ref[...] = _adamw(w_ref[...], g, m_ref[...], v_ref[...])

    full = pl.BlockSpec((tr, cc), lambda h, i, core_ref: (h * nb + i, 0))
    half = pl.BlockSpec((tr, cc), lambda h, i, core_ref: (i, 0))
    return pl.pallas_call(
        body, name=name, out_shape=[jax.ShapeDtypeStruct((2 * r, cc), F32)] * 4,
        grid_spec=pltpu.PrefetchScalarGridSpec(
            num_scalar_prefetch=1, grid=(2, nb), in_specs=[full, half, half, full, full], out_specs=[full] * 4),
        compiler_params=_cp("parallel", "parallel"),
    )(core, w, own, other, m, v)


def _place():
    mx, my, mc = lax.axis_index("x"), lax.axis_index("y"), lax.axis_index("c")
    chips = [(1 - mx, my), (mx, 1 - my), (1 - mx, 1 - my)]
    return mx, my, mc, chips


def all_gather_small(x, name):
    R, C = x.shape

    def body(x_ref, out_ref, send_sems, recv_sems, local_sem):
        mx, my, mc, _ = _place()
        me = 4 * mx + 2 * my + mc
        mine = pltpu.make_async_copy(x_ref, out_ref.at[me], local_sem)
        mine.start()

        def peer(k):
            px = 1 - mx if k & 4 else mx
            py = 1 - my if k & 2 else my
            pc = 1 - mc if k & 1 else mc
            return px, py, pc

        def copy(k, src, slot):
            return pltpu.make_async_remote_copy(src_ref=src, dst_ref=out_ref.at[slot], send_sem=send_sems.at[k - 1],
                                                recv_sem=recv_sems.at[k - 1], device_id=peer(k), device_id_type=MESH)

        sends = [copy(k, x_ref, me) for k in range(1, 8)]
        for cp in sends:
            cp.start()
        for k in range(1, 8):
            px, py, pc = peer(k)
            slot = 4 * px + 2 * py + pc
            copy(k, out_ref.at[slot], slot).wait_recv()
        for cp in sends:
            cp.wait_send()
        mine.wait()

    return pl.pallas_call(
        body, name=name, out_shape=jax.ShapeDtypeStruct((8, R, C), F32),
        in_specs=[pl.BlockSpec(memory_space=pltpu.VMEM)], out_specs=pl.BlockSpec(memory_space=pltpu.VMEM),
        scratch_shapes=[pltpu.SemaphoreType.DMA((7,)), pltpu.SemaphoreType.DMA((7,)), pltpu.SemaphoreType.DMA],
        compiler_params=_cp(),
    )(x)


def gather8_comm(x):
    def copies(x_ref, out_ref, send_sems, recv_sems):
        mx, my, mc, _ = _place()
        me = 4 * mx + 2 * my + mc

        def peer(k):
            return (1 - mx if k & 4 else mx, 1 - my if k & 2 else my, 1 - mc if k & 1 else mc)

        def copy(k, src, slot):
            return pltpu.make_async_remote_copy(src_ref=src, dst_ref=out_ref.at[slot], send_sem=send_sems.at[k - 1],
                                                recv_sem=recv_sems.at[k - 1], device_id=peer(k), device_id_type=MESH)

        sends = [copy(k, x_ref, me) for k in range(1, 8)]
        arrivals = []
        for k in range(1, 8):
            px, py, pc = peer(k)
            slot = 4 * px + 2 * py + pc
            arrivals.append(copy(k, out_ref.at[slot], slot))
        return sends, arrivals, pltpu.make_async_copy(x_ref, out_ref.at[me], send_sems.at[7])

    def start(cin, cout, send_sems, recv_sems):
        sends, _, mine = copies(cin[0], cout[0], send_sems, recv_sems)
        mine.start()
        for cp in sends:
            cp.start()

    def finish(cin, cout, send_sems, recv_sems):
        sends, arrivals, mine = copies(cin[0], cout[0], send_sems, recv_sems)
        for cp in arrivals:
            cp.wait_recv()
        for cp in sends:
            cp.wait_send()
        mine.wait()

    return _Comm([x], [jax.ShapeDtypeStruct((8,) + x.shape, F32)], {}, 8, start, finish)


def _join(a, b):
    na_in, na_out = len(a.operands), len(a.out_shape)

    def split(fn_a, fn_b):
        def both(cin, cout, send_sems, recv_sems):
            fn_a(cin[:na_in], cout[:na_out], send_sems.at[pl.ds(0, a.n_sems)], recv_sems.at[pl.ds(0, a.n_sems)])
            fn_b(cin[na_in:], cout[na_out:], send_sems.at[pl.ds(a.n_sems, b.n_sems)], recv_sems.at[pl.ds(a.n_sems, b.n_sems)])
        return both

    aliases = dict(a.aliases)
    aliases.update({na_in + i: na_out + o for i, o in b.aliases.items()})
    return _Comm(a.operands + b.operands, a.out_shape + b.out_shape, aliases, a.n_sems + b.n_sems, split(a.start, b.start), split(a.finish, b.finish))


def _region(ref, kind, j, half, r, cc):
    nr = r if half is None else r // 2
    off = 0 if half is None else half * nr
    if kind == "col":
        return ref.at[pl.ds(off, nr), pl.ds(pl.multiple_of(j * cc, 128), cc)]
    return ref.at[pl.ds(pl.multiple_of(j * r + off, 16), nr), :]


def comm_call(comm, name):
    ni, no = len(comm.operands), len(comm.out_shape)

    def body(*refs):
        comm.start(refs[:ni], refs[ni:ni + no], *refs[ni + no:])
        comm.finish(refs[:ni], refs[ni:ni + no], *refs[ni + no:])

    return pl.pallas_call(
        body, name=name, out_shape=comm.out_shape, in_specs=[ANY] * ni, out_specs=[ANY] * no, input_output_aliases=comm.aliases,
        scratch_shapes=[pltpu.SemaphoreType.DMA((comm.n_sems,)), pltpu.SemaphoreType.DMA((comm.n_sems,))], compiler_params=_cp(),
    )(*comm.operands)


def gather_comm(fulls, kinds, dims):
    n = len(fulls)

    def copies(f_refs, send_sems, recv_sems):
        mx, my, mc, chips = _place()
        jme = 2 * mx + my

        def landed(w, k, half):
            px, py = chips[k]
            return _region(f_refs[w], kinds[w], 2 * px + py, half, *dims[w])

        def over_ici(w, k, reg):
            px, py = chips[k]
            return pltpu.make_async_remote_copy(src_ref=reg, dst_ref=reg, send_sem=send_sems.at[6 * w + k], recv_sem=recv_sems.at[6 * w + k],
                                                device_id=(px, py, mc), device_id_type=MESH)

        def over_d2d(w, k, half):
            reg = landed(w, k, half)
            return pltpu.make_async_remote_copy(src_ref=reg, dst_ref=reg, send_sem=send_sems.at[6 * w + 3 + k],
                                                recv_sem=recv_sems.at[6 * w + 3 + k], device_id=(mx, my, 1 - mc), device_id_type=MESH)

        sends = [over_ici(w, k, _region(f_refs[w], kinds[w], jme, mc, *dims[w])) for w in range(n) for k in range(3)]
        return mc, landed, over_ici, over_d2d, sends

    def start(cin, f_refs, send_sems, recv_sems):
        for cp in copies(f_refs, send_sems, recv_sems)[4]:
            cp.start()

    def finish(cin, f_refs, send_sems, recv_sems):
        mc, landed, over_ici, over_d2d, sends = copies(f_refs, send_sems, recv_sems)
        passed = []
        for w in range(n):
            for k in range(3):
                over_ici(w, k, landed(w, k, mc)).wait_recv()
                cp = over_d2d(w, k, mc)
                cp.start()
                passed.append(cp)
        for w in range(n):
            for k in range(3):
                over_d2d(w, k, 1 - mc).wait_recv()
        for cp in sends + passed:
            cp.wait_send()

    return _Comm(fulls, [jax.ShapeDtypeStruct(f.shape, BF16) for f in fulls], {w: w for w in range(n)}, 6 * n, start, finish)


def exchange_comm(grads):
    n = len(grads)

    def copies(g_refs, l_refs, send_sems, recv_sems):
        mx, my, mc, _ = _place()
        return [pltpu.make_async_remote_copy(src_ref=g_refs[w].at[:, pl.ds(1 - mc, 1)], dst_ref=l_refs[w], send_sem=send_sems.at[w],
                                             recv_sem=recv_sems.at[w], device_id=(mx, my, 1 - mc), device_id_type=MESH) for w in range(n)]

    def start(*refs):
        for cp in copies(*refs):
            cp.start()

    def finish(*refs):
        for cp in copies(*refs):
            cp.wait()

    return _Comm(grads, [jax.ShapeDtypeStruct((g.shape[0], 1) + g.shape[2:], BF16) for g in grads], {}, n, start, finish)


def exchange_halves(grads, name):
    return comm_call(exchange_comm(grads), name)


def scatter_comm(parts):
    n = len(parts)

    def sends(p_refs, l_refs, send_sems, recv_sems):
        mx, my, mc, chips = _place()
        return [pltpu.make_async_remote_copy(src_ref=p_refs[w].at[2 * px + py], dst_ref=l_refs[w].at[2 * mx + my],
                                             send_sem=send_sems.at[3 * w + k], recv_sem=recv_sems.at[3 * w + k],
                                             device_id=(px, py, mc), device_id_type=MESH) for w in range(n) for k, (px, py) in enumerate(chips)]

    def start(p_refs, l_refs, send_sems, recv_sems):
        for cp in sends(p_refs, l_refs, send_sems, recv_sems):
            cp.start()

    def finish(p_refs, l_refs, send_sems, recv_sems):
        mx, my, mc, chips = _place()
        for w in range(n):
            for k, (px, py) in enumerate(chips):
                slot = l_refs[w].at[2 * px + py]
                pltpu.make_async_remote_copy(src_ref=slot, dst_ref=slot, send_sem=send_sems.at[3 * w + k], recv_sem=recv_sems.at[3 * w + k],
                                             device_id=(px, py, mc), device_id_type=MESH).wait_recv()
        for cp in sends(p_refs, l_refs, send_sems, recv_sems):
            cp.wait_send()

    return _Comm(parts, [jax.ShapeDtypeStruct(p.shape, BF16) for p in parts], {}, 3 * n, start, finish)


def share_comm(sums):
    n = len(sums)

    def copies(q_refs, o_refs, send_sems, recv_sems):
        mx, my, mc, _ = _place()
        return [pltpu.make_async_remote_copy(src_ref=q_refs[w], dst_ref=o_refs[w], send_sem=send_sems.at[w], recv_sem=recv_sems.at[w],
                                             device_id=(mx, my, 1 - mc), device_id_type=MESH) for w in range(n)]

    def start(*refs):
        for cp in copies(*refs):
            cp.start()

    def finish(*refs):
        for cp in copies(*refs):
            cp.wait()

    return _Comm(sums, [jax.ShapeDtypeStruct(q.shape, F32) for q in sums], {}, n, start, finish)


def _pack(arrays):
    flat = jnp.concatenate([a.reshape(-1) for a in arrays])
    rows = -(-flat.shape[0] // 1024) * 8
    return jnp.pad(flat, (0, rows * 128 - flat.shape[0])).reshape(rows, 128)


def _unpack(packed, shapes):
    flat, out, off = packed.reshape(-1), [], 0
    for s in shapes:
        n = math.prod(s)
        out.append(flat[off:off + n].reshape(s))
        off += n
    return out


def kernel(x, c, w_ada, b_ada, g_pre_mix, g_post_mix, g_pre_ffn, g_post_ffn, w_in, lb_logits, g_hgrn_norm, w_a_out, g_sgu_norm, w_spatial, b_spatial, w_b_out, w_o, w_ff1, w_ff2, loss_target, m_w_ada, m_b_ada, m_g_pre_mix, m_g_post_mix, m_g_pre_ffn, m_g_post_ffn, m_w_in, m_lb_logits, m_g_hgrn_norm, m_w_a_out, m_g_sgu_norm, m_w_spatial, m_b_spatial, m_w_b_out, m_w_o, m_w_ff1, m_w_ff2, v_w_ada, v_b_ada, v_g_pre_mix, v_g_post_mix, v_g_pre_ffn, v_g_post_ffn, v_w_in, v_lb_logits, v_g_hgrn_norm, v_w_a_out, v_g_sgu_norm, v_w_spatial, v_b_spatial, v_w_b_out, v_w_o, v_w_ff1, v_w_ff2):
    mx, my, mc = lax.axis_index("x"), lax.axis_index("y"), lax.axis_index("c")
    chip, me = 2 * mx + my, 4 * mx + 2 * my + mc
    D = D_MODEL
    h0, tgt = x[0], loss_target[0]
    n_ada = w_ada.shape[2]
    n_lb = lb_logits.shape[2]

    got = all_gather_small(_pack([c, lb_logits]), "gather_inputs")
    c_all = got[:, :D // 128, :].reshape(8, D)
    lb_full = got[0::2, D // 128:D // 128 + 4 * n_lb // 128, :].reshape(4, 2, 2, n_lb).transpose(1, 2, 0, 3).reshape(2, 2, 4 * n_lb)
    b_ada_chip = lax.dynamic_slice(b_ada, (0, chip * n_ada), (1, n_ada))
    mod_cols = mod_matmul(c_all, w_ada[0], b_ada_chip)
    got = all_gather_small(mod_cols.reshape(-1, 128), "gather_mod").reshape(4, 2, 8, n_ada)
    mod = lax.dynamic_index_in_dim(got[:, 0], me, axis=1, keepdims=False).reshape(6, 1, D)
    sh1, sc1, gt1, sh2, sc2, gt2 = (mod[i] for i in range(6))

    big = [("w_in", w_in, "col"), ("w_a_out", w_a_out, "col"), ("w_b_out", w_b_out, "col"), ("w_o", w_o, "row"),
           ("w_ff1", w_ff1, "col"), ("w_ff2", w_ff2, "row")]
    kinds = [k for _, _, k in big]
    chip_idx, core = chip.reshape(1).astype(jnp.int32), mc.reshape(1).astype(jnp.int32)
    fulls = [cast_into_full(w[0], kind, chip_idx, "cast_" + nm) for nm, w, kind in big]
    dims = [w.shape[1:] for _, w, _ in big]
    later = lambda lo, hi: gather_comm(fulls[lo:hi], kinds[lo:hi], dims[lo:hi])
    halves_summed = lambda grads, name: [add_halves(g, l, core) for g, l in zip(grads, exchange_halves(grads, name))]

    bst = b_spatial[0].T
    a1 = prenorm(h0, g_pre_mix, sc1, sh1)
    proj, w_in_f, (w_a_f, w_b_f, w_o_f) = in_proj_gathered(a1, fulls[0], chip_idx, dims[0], later(1, 4))
    o, (w_ff1_f,) = hgrn_fwd(proj, lb_full, comm=later(4, 5))
    ya_pre = hgrn_post_fwd(o, proj, g_hgrn_norm)
    sgu = sgu_fwd(proj, g_sgu_norm, w_spatial[0], bst)
    y_a, y_b, merged = merge_matmul(ya_pre, sgu, w_a_f, w_b_f, proj)
    mo, h1, a2 = out_proj(merged, w_o_f, h0, gt1, g_post_mix, g_pre_ffn, sc2, sh2)
    (f1, hid), (w_ff2_f,) = matmul(a2, w_ff1_f, mode="nn", out_dtype=BF16, tm=1024, tn=1024, tk=2048, name="ff1", relu2=True,
                                   comm=later(5, 6))
    ff = matmul(hid, w_ff2_f, mode="nn", out_dtype=F32, tm=1024, tn=1024, tk=2048, name="ff2")
    dy, dff, loss_parts, d_gt2, d_g_post_ffn = loss_bwd(ff, h1, tgt, gt2, g_post_ffn)
    loss = lax.psum(0.5 * loss_parts[0, 0] / D, ("x", "y", "c"))

    df1 = ff2_bwd(dff, w_ff2_f, f1)
    gr_ff2 = matmul(hid, dff, mode="tn", out_dtype=BF16, tm=1024, tn=1024, tk=2048, name="dw_ff2")
    gr_ff2 = gr_ff2.reshape(4, 2, -1, D)
    da2, (landed_ff2,) = matmul(df1, w_ff1_f, mode="nt", out_dtype=F32, tm=1024, tn=1024, tk=2048, name="da2", comm=exchange_comm([gr_ff2]))
    gr_ff1 = matmul(a2, df1, mode="tn", out_dtype=BF16, tm=1024, tn=2048, tk=1024, name="dw_ff1", split=(4, 2))
    (dh1, dmo, d_sh2, d_sc2, d_g_pre_ffn, d_gt1, d_g_post_mix), (landed_ff1,) = ffn_norm_bwd(
        dy, da2, h1, mo, g_pre_ffn, sc2, gt1, g_post_mix, exchange_comm([gr_ff1]))
    parts_ff = [add_halves(gr_ff1, landed_ff1, core), add_halves(gr_ff2, landed_ff2, core)]
    dya, dyb, dga, dgb = out_proj_bwd(dmo, w_o_f, y_a, y_b, proj)
    gr_o = matmul(merged, dmo, mode="tn", out_dtype=BF16, tm=1024, tn=1024, tk=2048, name="dw_o")
    dsgu = matmul(dyb, w_b_f, mode="nt", out_dtype=F32, tm=512, tn=1024, tk=2048, name="dsgu")
    gr_b = matmul(sgu, dyb, mode="tn", out_dtype=BF16, tm=512, tn=512, tk=4096, name="dw_b_out", split=(4, 2))
    dz, d_w_spatial, d_b_spatial, d_g_sgu = sgu_bwd(proj, dsgu, g_sgu_norm, w_spatial[0], bst)
    dya_pre = matmul(dya, w_a_f, mode="nt", out_dtype=F32, tm=512, tn=1024, tk=2048, name="dya_pre")
    gr_a = matmul(ya_pre, dya, mode="tn", out_dtype=BF16, tm=512, tn=512, tk=4096, name="dw_a_out", split=(4, 2))
    parts_mix = halves_summed([gr_a, gr_b, gr_o.reshape(4, 2, -1, D)], "exchange_mix")
    do, dog, d_g_hgrn = hgrn_post_bwd(dya_pre, o, proj, g_hgrn_norm)
    chips_summed = lambda parts, landed: [sum_chips(p, l, chip_idx) for p, l in zip(parts, landed)]
    (dq, dv, dlg, d_lb), landed_ff = hgrn_bwd(proj, do, lb_full, comm=scatter_comm(parts_ff))
    own_ff = chips_summed(parts_ff, landed_ff)
    dproj = jnp.concatenate([dq, dlg, dv, dog, dz, dga, dgb], axis=1)
    early = _pack([d_g_sgu, d_w_spatial, d_b_spatial[:, 0, :]])
    gr_in, (*landed_mix, got_early) = matmul(a1, dproj, mode="tn", out_dtype=BF16, tm=1024, tn=2816, tk=1024, name="dw_in", split=(4, 2),
                                             comm=_join(scatter_comm(parts_mix), gather8_comm(early)))
    own_mix = chips_summed(parts_mix, landed_mix)
    parts_in = halves_summed([gr_in], "exchange_in")
    da1, (landed_in, *other_rest) = matmul(dproj, w_in_f, mode="nt", out_dtype=F32, tm=1024, tn=1024, tk=2816, name="da1",
                                           comm=_join(scatter_comm(parts_in), share_comm(own_mix + own_ff)))
    own_in = chips_summed(parts_in, [landed_in])
    other_in = comm_call(share_comm(own_in), "share_w_in")
    own, other = own_in + own_mix + own_ff, list(other_in) + other_rest
    grad_x, d_sh1, d_sc1, d_g_pre_mix = mix_norm_bwd(da1, h0, dh1, g_pre_mix, sc1)
    out = {}

    mine = _pack([d_sh1, d_sc1, d_gt1, d_sh2, d_sc2, d_gt2, d_g_pre_mix, d_g_post_mix, d_g_pre_ffn, d_g_post_ffn, d_g_hgrn, d_lb])
    got = all_gather_small(mine, "gather_small_grads")
    g_b_ada, g_g1, g_g2, g_g3, g_g4, g_hg, g_lb = _unpack(
        sum_devices(got, "sum_small_grads"), [(1, 6 * D), (1, D), (1, D), (1, D), (1, D), (1, HEAD_DIM), (2, 1024)])
    g_sg, g_ws, g_bs = _unpack(sum_devices(got_early, "sum_sgu_grads"), [(1, 1024), w_spatial.shape, b_spatial.shape])
    g_lbl = lax.dynamic_slice(lb_logits_grad(g_lb, lb_full), (0, 0, chip * n_lb), (2, 2, n_lb))
    names = ["b_ada", "g_pre_mix", "g_post_mix", "g_pre_ffn", "g_post_ffn", "g_hgrn_norm", "g_sgu_norm", "w_spatial", "b_spatial", "lb_logits"]
    ws = [b_ada, g_pre_mix, g_post_mix, g_pre_ffn, g_post_ffn, g_hgrn_norm, g_sgu_norm, w_spatial, b_spatial, lb_logits]
    gs = [g_b_ada, g_g1, g_g2, g_g3, g_g4, g_hg, g_sg, g_ws, g_bs, g_lbl]
    ms = [m_b_ada, m_g_pre_mix, m_g_post_mix, m_g_pre_ffn, m_g_post_ffn, m_g_hgrn_norm, m_g_sgu_norm, m_w_spatial, m_b_spatial, m_lb_logits]
    vs = [v_b_ada, v_g_pre_mix, v_g_post_mix, v_g_pre_ffn, v_g_post_ffn, v_g_hgrn_norm, v_g_sgu_norm, v_w_spatial, v_b_spatial, v_lb_logits]
    shapes = [w.shape for w in ws]
    upd = adamw(_pack(ws), _pack(gs), _pack(ms), _pack(vs), "adamw_small")
    upd = [_unpack(u, shapes) for u in upd]
    for i, nm in enumerate(names):
        out[nm] = (gs[i], upd[0][i], upd[1][i], upd[2][i])

    dmod_all = got[:, :6 * D // 128, :].reshape(8, 6 * D)
    dmod_chip = lax.dynamic_slice(dmod_all, (0, chip * n_ada), (8, n_ada))
    out["w_ada"] = tuple(a[None] for a in wada_update(c_all, dmod_chip, w_ada[0], m_w_ada[0], v_w_ada[0]))
    for (nm, w, _), a, b, m, v in zip(big, own, other, (m_w_in, m_w_a_out, m_w_b_out, m_w_o, m_w_ff1, m_w_ff2),
                                      (v_w_in, v_w_a_out, v_w_b_out, v_w_o, v_w_ff1, v_w_ff2)):
        out[nm] = tuple(t[None] for t in adamw_halves(w[0], a, b, m[0], v[0], core, "adamw_" + nm))

    order = ["w_ada", "b_ada", "g_pre_mix", "g_post_mix", "g_pre_ffn", "g_post_ffn", "w_in", "lb_logits", "g_hgrn_norm", "w_a_out",
             "g_sgu_norm", "w_spatial", "b_spatial", "w_b_out", "w_o", "w_ff1", "w_ff2"]
    return (loss, grad_x[None], *[out[nm][0] for nm in order], *[out[nm][1] for nm in order], *[out[nm][2] for nm in order],
            *[out[nm][3] for nm in order])
```

```python
import functools
import math

import jax
import jax.numpy as jnp
from jax import lax
from jax.experimental import pallas as pl
from jax.experimental.pallas import tpu as pltpu

F32, BF16 = jnp.float32, jnp.bfloat16
HI = lax.Precision.HIGHEST
MESH = pl.DeviceIdType.MESH
ANY = pl.BlockSpec(memory_space=pl.ANY)

EPS = 1e-6
D_MODEL = 2048
N_HEADS = 8
HEAD_DIM = 128
HGRN_CHUNK = 32
HGRN_BLOCK = 256
SGU_CHUNK = 128
SGU_GROUPS = 8
Q_SCALE = HEAD_DIM ** -0.5
COL_Q, COL_FFW, COL_FBW, COL_V, COL_OG, COL_U, COL_ZV, COL_GA, COL_GB = 0, 1, 2, 3, 4, 5, 6, 7, 9
N_PROJ = 11264
VMEM_BYTES_V7X = 64 * 1024 * 1024
VMEM_LIMIT = VMEM_BYTES_V7X - 8 * 1024 * 1024

ADAM_LR, ADAM_B1, ADAM_B2, ADAM_EPS, ADAM_WD, ADAM_STEP = 0.001, 0.9, 0.999, 1e-08, 0.01, 10
ADAM_C1 = 1.0 - ADAM_B1 ** ADAM_STEP
ADAM_C2 = 1.0 - ADAM_B2 ** ADAM_STEP


def _cp(*sem):
    return pltpu.CompilerParams(dimension_semantics=sem if sem else None, vmem_limit_bytes=VMEM_LIMIT)


def _vec(d):
    return pl.BlockSpec((1, d), lambda *_: (0, 0))


def _colsum(x):
    return jnp.sum(x, axis=0, keepdims=True)


def _nt(a, b):
    return lax.dot_general(a, b, (((1,), (1,)), ((), ())), preferred_element_type=F32)


def _tn(a, b):
    return lax.dot_general(a, b, (((0,), (0,)), ((), ())), preferred_element_type=F32)


def _nn(a, b):
    return jnp.dot(a, b, preferred_element_type=F32)


def _adamw(w, g, m, v):
    m2 = ADAM_B1 * m + (1.0 - ADAM_B1) * g
    v2 = ADAM_B2 * v + (1.0 - ADAM_B2) * (g * g)
    delta = -ADAM_LR * ((m2 / ADAM_C1) / (jnp.sqrt(v2 / ADAM_C2) + ADAM_EPS) + ADAM_WD * w)
    return delta, m2, v2


class _Comm:
    def __init__(self, operands, out_shape, aliases, n_sems, start, finish):
        self.operands, self.out_shape, self.aliases, self.n_sems = list(operands), list(out_shape), dict(aliases), n_sems
        self.start, self.finish = start, finish


def _pallas(body, *, name, grid, in_specs, out_specs, out_shape, scratch, semantics, operands, comm=None):
    if comm is None:
        res = pl.pallas_call(body, name=name, grid=grid, in_specs=in_specs, out_specs=out_specs, out_shape=out_shape,
                             scratch_shapes=scratch, compiler_params=_cp(*semantics))(*operands)
        return res, []
    n_in, n_out, n_scr = len(in_specs), len(out_specs), len(scratch)
    nci, nco = len(comm.operands), len(comm.out_shape)

    def with_comm(*refs):
        ins, rest = refs[:n_in], refs[n_in:]
        cin, rest = rest[:nci], rest[nci:]
        outs, rest = rest[:n_out], rest[n_out:]
        cout, rest = rest[:nco], rest[nco:]
        scr, (send, recv) = rest[:n_scr], rest[n_scr:]
        ids = [pl.program_id(a) for a in range(len(grid))]
        first = functools.reduce(jnp.logical_and, [i == 0 for i in ids])
        last = functools.reduce(jnp.logical_and, [i == g - 1 for i, g in zip(ids, grid)])

        @pl.when(first)
        def _():
            comm.start(cin, cout, send, recv)

        body(*ins, *outs, *scr)

        @pl.when(last)
        def _():
            comm.finish(cin, cout, send, recv)

    res = pl.pallas_call(
        with_comm, name=name, grid=grid, in_specs=list(in_specs) + [ANY] * nci, out_specs=list(out_specs) + [ANY] * nco,
        out_shape=list(out_shape) + comm.out_shape, input_output_aliases={n_in + i: n_out + o for i, o in comm.aliases.items()},
        scratch_shapes=list(scratch) + [pltpu.SemaphoreType.DMA((comm.n_sems,)), pltpu.SemaphoreType.DMA((comm.n_sems,))],
        compiler_params=_cp(*["arbitrary"] * len(grid)),
    )(*operands, *comm.operands)
    return res[:n_out], res[n_out:]


def matmul(a, b, *, mode, out_dtype, tm, tn, tk, name, split=None, comm=None, relu2=False):
    if mode == "tn":
        (K, M), (_, N) = a.shape, b.shape
    elif mode == "nt":
        (M, K), (N, _) = a.shape, b.shape
    else:
        (M, K), (_, N) = a.shape, b.shape
    tm, tn, tk = min(tm, M), min(tn, N), min(tk, K)
    nk = K // tk
    a_spec = pl.BlockSpec((tk, tm), lambda i, j, k: (k, i)) if mode == "tn" else pl.BlockSpec((tm, tk), lambda i, j, k: (i, k))
    b_spec = pl.BlockSpec((tn, tk), lambda i, j, k: (j, k)) if mode == "nt" else pl.BlockSpec((tk, tn), lambda i, j, k: (k, j))
    dot = {"nn": _nn, "nt": _nt, "tn": _tn}[mode]
    if split is None:
        out_shape = jax.ShapeDtypeStruct((M, N), out_dtype)
        out_spec = pl.BlockSpec((tm, tn), lambda i, j, k: (i, j))
    else:
        nj, nh = split
        rows, cols = M // nh, N // nj
        tm, tn = min(tm, rows), min(tn, cols)
        bi, bj = rows // tm, cols // tn
        out_shape = jax.ShapeDtypeStruct((nj, nh, rows, cols), out_dtype)
        out_spec = pl.BlockSpec((None, None, tm, tn), lambda i, j, k: (j // bj, i // bi, i % bi, j % bj))

    def finish(y, o_ref, sq_ref):
        o_ref[...] = y.astype(o_ref.dtype)
        if relu2:
            p = jnp.maximum(y, 0.0)
            sq_ref[0][...] = (p * p).astype(BF16)

    if nk == 1:
        def body(a_ref, b_ref, o_ref, *sq_ref):
            finish(dot(a_ref[...], b_ref[...]), o_ref, sq_ref)
        scratch = []
    else:
        def body(a_ref, b_ref, o_ref, *rest):
            acc_ref, k = rest[-1], pl.program_id(2)

            @pl.when(k == 0)
            def _():
                acc_ref[...] = jnp.zeros_like(acc_ref)

            acc_ref[...] += dot(a_ref[...], b_ref[...])

            @pl.when(k == nk - 1)
            def _():
                finish(acc_ref[...], o_ref, rest[:-1])
        scratch = [pltpu.VMEM((tm, tn), F32)]

    out_specs, out_shapes = [out_spec], [out_shape]
    if relu2:
        out_specs, out_shapes = out_specs + [out_spec], out_shapes + [jax.ShapeDtypeStruct(out_shape.shape, BF16)]
    outs, landed = _pallas(
        body, name=name, grid=(M // tm, N // tn, nk), in_specs=[a_spec, b_spec], out_specs=out_specs, out_shape=out_shapes,
        scratch=scratch, semantics=("parallel", "parallel", "arbitrary"), operands=(a, b), comm=comm)
    out = tuple(outs) if relu2 else outs[0]
    return out if comm is None else (out, landed)


def cast_into_full(w, kind, chip, name):
    r, cc = w.shape
    tr = min(r, 512)
    nb = r // tr

    def body(chip_ref, w_ref, o_ref):
        o_ref[...] = w_ref[...].astype(BF16)

    if kind == "col":
        full, out_map = (r, 4 * cc), lambda i, chip_ref: (i, chip_ref[0])
    else:
        full, out_map = (4 * r, cc), lambda i, chip_ref: (chip_ref[0] * nb + i, 0)
    return pl.pallas_call(
        body, name=name, out_shape=jax.ShapeDtypeStruct(full, BF16),
        grid_spec=pltpu.PrefetchScalarGridSpec(
            num_scalar_prefetch=1, grid=(nb,), in_specs=[pl.BlockSpec((tr, cc), lambda i, chip_ref: (i, 0))],
            out_specs=pl.BlockSpec((tr, cc), out_map)),
        compiler_params=_cp("parallel"),
    )(chip, w)


def mod_matmul(c_all, w_ada, b_ada):
    D, N = w_ada.shape
    tn = 1024

    def body(c_ref, w_ref, b_ref, o_ref):
        c = c_ref[...]
        sc = c * jax.nn.sigmoid(c)
        o_ref[...] = jnp.dot(sc, w_ref[...], precision=HI, preferred_element_type=F32) + b_ref[...]

    return pl.pallas_call(
        body, name="mod_matmul", out_shape=jax.ShapeDtypeStruct((8, N), F32), grid=(N // tn,),
        in_specs=[pl.BlockSpec((8, D), lambda j: (0, 0)), pl.BlockSpec((D, tn), lambda j: (0, j)),
                  pl.BlockSpec((1, tn), lambda j: (0, j))],
        out_specs=pl.BlockSpec((8, tn), lambda j: (0, j)), compiler_params=_cp("parallel"),
    )(c_all, w_ada, b_ada)


def prenorm(h, g, sc, sh):
    T, D = h.shape
    tm = min(256, T)

    def body(h_ref, g_ref, sc_ref, sh_ref, a_ref):
        x = h_ref[...]
        r = lax.rsqrt(jnp.mean(x * x, axis=-1, keepdims=True) + EPS)
        a_ref[...] = ((x * r) * g_ref[...] * (1.0 + sc_ref[...]) + sh_ref[...]).astype(BF16)

    row = pl.BlockSpec((tm, D), lambda i: (i, 0))
    return pl.pallas_call(
        body, name="prenorm", out_shape=jax.ShapeDtypeStruct((T, D), BF16), grid=(T // tm,),
        in_specs=[row, _vec(D), _vec(D), _vec(D)], out_specs=row, compiler_params=_cp("parallel"),
    )(h, g, sc, sh)


def in_proj_gathered(a, w_full, chip, dims, tail):
    T, D = a.shape
    rows, cc = dims
    tm, tn = min(512, T), cc // 2
    ni = T // tm
    half = rows // 2

    nt = len(tail.operands)

    def body(chip_ref, a_ref, w_in_ref, *rest):
        tail_in, (y_ref, w_ref), rest = rest[:nt], rest[nt:nt + 2], rest[nt + 2:]
        tail_out, (wbuf, wsem, send_sems, recv_sems, tail_send, tail_recv) = rest[:nt], rest[nt:]
        q, j, i = pl.program_id(0), pl.program_id(1), pl.program_id(2)
        mx, my, mc, _ = _place()
        me = chip_ref[0]

        def tile(block, jj):
            src = w_ref.at[:, pl.ds(pl.multiple_of(block * cc + jj * tn, 128), tn)]
            return pltpu.make_async_copy(src, wbuf.at[jj], wsem.at[jj])

        def rows_half(block, hh):
            return w_ref.at[pl.ds(pl.multiple_of(hh * half, 16), half), pl.ds(pl.multiple_of(block * cc, 128), cc)]

        def over_ici(s, block):
            peer = (1 - mx if s & 2 else mx, 1 - my if s & 1 else my, mc)
            reg = rows_half(block, mc)
            return pltpu.make_async_remote_copy(src_ref=reg, dst_ref=reg, send_sem=send_sems.at[s - 1], recv_sem=recv_sems.at[s - 1],
                                                device_id=peer, device_id_type=MESH)

        def over_d2d(s, block, hh):
            reg = rows_half(block, hh)
            return pltpu.make_async_remote_copy(src_ref=reg, dst_ref=reg, send_sem=send_sems.at[2 + s], recv_sem=recv_sems.at[2 + s],
                                                device_id=(mx, my, 1 - mc), device_id_type=MESH)

        def passed_on(s, block):
            k = s - 1
            reg = w_ref.at[pl.ds(pl.multiple_of(mc * half + k * (half // 2), 16), half // 2), pl.ds(pl.multiple_of(block * cc, 128), cc)]
            peer = (1 - mx, my, mc) if s == 1 else (mx, 1 - my, mc)
            return pltpu.make_async_remote_copy(src_ref=reg, dst_ref=reg, send_sem=send_sems.at[6 + k], recv_sem=recv_sems.at[6 + k],
                                                device_id=peer, device_id_type=MESH)

        @pl.when((q == 0) & (j == 0) & (i == 0))
        def _():
            for s in (1, 2):
                over_ici(s, me).start()
            tile(me, 0).start()

        @pl.when(i == 0)
        def _():
            tile(me ^ q, j).wait()

        @pl.when((i == 0) & (j == 0))
        def _():
            tile(me ^ q, 1).start()

        y_ref[...] = _nn(a_ref[...], wbuf[j])

        @pl.when((q == 0) & (j == 1) & (i == ni - 1))
        def _():
            for s in (1, 2):
                over_ici(s, me ^ s).wait_recv()
                passed_on(s, me ^ s).start()
                over_d2d(s, me ^ s, mc).start()
            tail.start(tail_in, tail_out, tail_send, tail_recv)
            over_d2d(1, me ^ 1, 1 - mc).wait_recv()
            tile(me ^ 1, 0).start()

        @pl.when((q == 1) & (j == 1) & (i == ni - 1))
        def _():
            over_d2d(2, me ^ 2, 1 - mc).wait_recv()
            tile(me ^ 2, 0).start()

        @pl.when((q == 2) & (j == 1) & (i == ni - 1))
        def _():
            for s in (1, 2):
                passed_on(s, me ^ 3).wait_recv()
            over_d2d(3, me ^ 3, mc).start()
            over_d2d(3, me ^ 3, 1 - mc).wait_recv()
            tile(me ^ 3, 0).start()

        @pl.when((q == 3) & (j == 1) & (i == ni - 1))
        def _():
            for s in (1, 2):
                over_ici(s, me).wait_send()
                passed_on(s, me ^ s).wait_send()
            for s in (1, 2, 3):
                over_d2d(s, me ^ s, mc).wait_send()
            tail.finish(tail_in, tail_out, tail_send, tail_recv)

    dma = pltpu.SemaphoreType.DMA
    y, w_out, *tail_res = pl.pallas_call(
        body, name="in_proj", out_shape=[jax.ShapeDtypeStruct((T, 4 * cc), F32), jax.ShapeDtypeStruct(w_full.shape, BF16)] + tail.out_shape,
        grid_spec=pltpu.PrefetchScalarGridSpec(
            num_scalar_prefetch=1, grid=(4, 2, ni),
            in_specs=[pl.BlockSpec((tm, D), lambda q, j, i, chip_ref: (i, 0)), ANY] + [ANY] * nt,
            out_specs=[pl.BlockSpec((tm, tn), lambda q, j, i, chip_ref: (i, (chip_ref[0] ^ q) * 2 + j)), ANY] + [ANY] * nt,
            scratch_shapes=[pltpu.VMEM((2, D, tn), BF16), dma((2,)), dma((8,)), dma((8,)), dma((tail.n_sems,)), dma((tail.n_sems,))]),
        input_output_aliases={2: 1, **{3 + i: 2 + o for i, o in tail.aliases.items()}},
        compiler_params=_cp("arbitrary", "arbitrary", "arbitrary"),
    )(chip, a, w_full, *tail.operands)
    return y, w_out, tail_res


def _hgrn_lower_bound(l_ref):
    l0, l1 = l_ref[0:1, :], l_ref[1:2, :]
    m = jnp.maximum(l0, l1)
    e0, e1 = jnp.exp(l0 - m), jnp.exp(l1 - m)
    return e0 / (e0 + e1)


def _hgrn_chunk_mask(d):
    r = lax.broadcasted_iota(jnp.int32, (HGRN_BLOCK, HGRN_BLOCK), 0)
    c = lax.broadcasted_iota(jnp.int32, (HGRN_BLOCK, HGRN_BLOCK), 1)
    same = (r // HGRN_CHUNK) == (c // HGRN_CHUNK)
    fwd = d == 0
    return same & (((c <= r) & fwd) | ((c >= r) & jnp.logical_not(fwd)))


def _chunk_total(x):
    x3 = x.reshape(HGRN_BLOCK // HGRN_CHUNK, HGRN_CHUNK, x.shape[1])
    return jnp.broadcast_to(jnp.sum(x3, axis=1, keepdims=True), x3.shape).reshape(x.shape)


def _chunk_cumsum(x, suffix):
    pos = lax.broadcasted_iota(jnp.int32, x.shape, 0) % HGRN_CHUNK
    p, s = x, 1
    while s < HGRN_CHUNK:
        p = p + jnp.where(pos >= s, pltpu.roll(p, s, 0), 0.0)
        s *= 2
    return jnp.where(suffix, _chunk_total(x) - p + x, p)


def _block_loop(T, body, init):
    n = T // HGRN_BLOCK
    return lax.fori_loop(0, n, body, init, unroll=2 if n % 2 == 0 else 1)


def _hgrn_gate(f, lb):
    s = jax.nn.sigmoid(f)
    sn = jax.nn.sigmoid(-f)
    fg = lb + (1.0 - lb) * s
    return s, sn, fg, jnp.log(fg), (1.0 - lb) * sn


def _hgrn_specs(T):
    col = lambda base: pl.BlockSpec((T, HEAD_DIM), lambda h, d: (0, base * N_HEADS + h))
    f_spec = pl.BlockSpec((T, HEAD_DIM), lambda h, d: (0, COL_FFW * N_HEADS + N_HEADS * d + h))
    l_spec = pl.BlockSpec((None, 2, HEAD_DIM), lambda h, d: (d, 0, h))
    return col, f_spec, l_spec


def hgrn_fwd(proj, lb_logits, comm=None):
    T = proj.shape[0]
    NC, CPB = T // HGRN_CHUNK, HGRN_BLOCK // HGRN_CHUNK
    col, f_spec, l_spec = _hgrn_specs(T)

    def body(l_ref, q_ref, f_ref, v_ref, o_ref, st_ref, dec_ref, qd_ref):
        d = pl.program_id(1)
        lb = _hgrn_lower_bound(l_ref)
        mask = _hgrn_chunk_mask(d)

        def block(i, carry):
            rows = pl.ds(pl.multiple_of(i * HGRN_BLOCK, HGRN_BLOCK), HGRN_BLOCK)
            _, _, _, lf, k = _hgrn_gate(f_ref[rows, :], lb)
            b = _chunk_cumsum(lf, d == 1)
            bl = _chunk_total(lf)
            qd = (q_ref[rows, :] * Q_SCALE * jnp.exp(b)).astype(BF16)
            kd = (k * jnp.exp(-b)).astype(BF16)
            ke = (k * jnp.exp(bl - b)).astype(BF16)
            vb = v_ref[rows, :].astype(BF16)
            att = jnp.where(mask, _nt(qd, kd), 0.0).astype(BF16)
            o_ref[rows, :] = jnp.where(d == 0, 0.0, o_ref[rows, :]) + _nn(att, vb)
            qd_ref[rows, :] = qd
            dec = jnp.exp(bl)
            for cc in range(CPB):
                sl = slice(cc * HGRN_CHUNK, (cc + 1) * HGRN_CHUNK)
                n = i * CPB + cc
                st_ref[n] = _tn(vb[sl], ke[sl])
                dec_ref[n] = dec[cc * HGRN_CHUNK:cc * HGRN_CHUNK + 8, :]
            return carry

        _block_loop(T, block, 0)

        def scan(t, s):
            n = jnp.where(d == 0, t, NC - 1 - t)
            u = st_ref[n]
            st_ref[n] = s
            return dec_ref[n][0:1, :] * s + u

        lax.fori_loop(0, NC, scan, jnp.zeros((HEAD_DIM, HEAD_DIM), F32))

        def inter(i, carry):
            rows = pl.ds(pl.multiple_of(i * HGRN_BLOCK, HGRN_BLOCK), HGRN_BLOCK)
            qd = qd_ref[rows, :]
            o_ref[rows, :] += jnp.concatenate(
                [_nt(qd[cc * HGRN_CHUNK:(cc + 1) * HGRN_CHUNK], st_ref[i * CPB + cc].astype(BF16)) for cc in range(CPB)], axis=0)
            return carry

        _block_loop(T, inter, 0)

    (o,), landed = _pallas(
        body, name="hgrn_fwd", grid=(N_HEADS, 2), in_specs=[l_spec, col(COL_Q), f_spec, col(COL_V)],
        out_specs=[pl.BlockSpec((T, HEAD_DIM), lambda h, d: (0, h))], out_shape=[jax.ShapeDtypeStruct((T, N_HEADS * HEAD_DIM), F32)],
        scratch=[pltpu.VMEM((NC, HEAD_DIM, HEAD_DIM), F32), pltpu.VMEM((NC, 8, HEAD_DIM), F32), pltpu.VMEM((T, HEAD_DIM), BF16)],
        semantics=("parallel", "arbitrary"), operands=(lb_logits, proj, proj, proj), comm=comm)
    return o if comm is None else (o, landed)


def hgrn_post_fwd(o, proj, g_norm):
    T, W = o.shape
    tm = min(256, T)

    def body(o_ref, og_ref, g_ref, y_ref):
        g = g_ref[...]
        for h in range(N_HEADS):
            sl = slice(h * HEAD_DIM, (h + 1) * HEAD_DIM)
            x = o_ref[:, sl]
            r = lax.rsqrt(jnp.mean(x * x, axis=-1, keepdims=True) + EPS)
            og = og_ref[:, sl]
            y_ref[:, sl] = ((x * r) * g * (og * jax.nn.sigmoid(og))).astype(BF16)

    return pl.pallas_call(
        body, name="hgrn_post_fwd", out_shape=jax.ShapeDtypeStruct((T, W), BF16), grid=(T // tm,),
        in_specs=[pl.BlockSpec((tm, W), lambda i: (i, 0)), pl.BlockSpec((tm, W), lambda i: (i, COL_OG)), _vec(HEAD_DIM)],
        out_specs=pl.BlockSpec((tm, W), lambda i: (i, 0)), compiler_params=_cp("parallel"),
    )(o, proj, g_norm)


def _gelu(x):
    return 0.5 * x * (1.0 + lax.erf(x * (1.0 / math.sqrt(2.0))))


def _gelu_grad(x):
    return 0.5 * (1.0 + lax.erf(x * (1.0 / math.sqrt(2.0)))) + x * jnp.exp(-0.5 * x * x) * (1.0 / math.sqrt(2.0 * math.pi))


def _sgu_mix(u_ref, v_ref, g_ref, ws_ref, bst_ref):
    W = u_ref.shape[1]
    zu, zv = _gelu(u_ref[...]), _gelu(v_ref[...])
    dv = zv - jnp.mean(zv, axis=-1, keepdims=True)
    rstd = lax.rsqrt(jnp.mean(dv * dv, axis=-1, keepdims=True) + EPS)
    dhat = dv * rstd
    vn = (dhat * g_ref[...]).astype(BF16)
    gw = W // SGU_GROUPS
    vm = [_nn(ws_ref[g].astype(BF16), vn[:, g * gw:(g + 1) * gw]) + bst_ref[:, g:g + 1] for g in range(SGU_GROUPS)]
    return zu, rstd, dhat, vn, jnp.concatenate(vm, axis=1)


def sgu_fwd(proj, g_norm, w_spatial, b_spatial_t):
    T = proj.shape[0]
    W = 1024
    n_chunks = T // SGU_CHUNK

    def body(u_ref, v_ref, g_ref, ws_ref, bst_ref, y_ref):
        zu, _, _, _, vm = _sgu_mix(u_ref, v_ref, g_ref, ws_ref, bst_ref)
        y_ref[...] = (zu * vm).astype(BF16)

    blk = lambda cb: pl.BlockSpec((SGU_CHUNK, W), lambda i: (i, cb))
    return pl.pallas_call(
        body, name="sgu_fwd", out_shape=jax.ShapeDtypeStruct((T, W), BF16), grid=(n_chunks,),
        in_specs=[blk(COL_U), blk(COL_ZV), _vec(W), pl.BlockSpec((SGU_GROUPS, SGU_CHUNK, SGU_CHUNK), lambda i: (0, 0, 0)),
                  pl.BlockSpec((SGU_CHUNK, SGU_GROUPS), lambda i: (0, 0))],
        out_specs=blk(0), compiler_params=_cp("parallel"),
    )(proj, proj, g_norm, w_spatial, b_spatial_t)


def merge_matmul(ya_pre, sgu, w_a, w_b, proj):
    T, K = ya_pre.shape
    N = w_a.shape[1]
    tm, tn = min(512, T), 512
    gpb = 1024 // tn

    def body(a_ref, b_ref, wa_ref, wb_ref, ga_ref, gb_ref, ya_ref, yb_ref, m_ref):
        ya = _nn(a_ref[...], wa_ref[...])
        yb = _nn(b_ref[...], wb_ref[...])
        ya_ref[...] = ya.astype(BF16)
        yb_ref[...] = yb.astype(BF16)
        m_ref[...] = (jax.nn.sigmoid(ga_ref[...]) * ya + jax.nn.sigmoid(gb_ref[...]) * yb).astype(BF16)

    lhs = pl.BlockSpec((tm, K), lambda i, j: (i, 0))
    rhs = pl.BlockSpec((K, tn), lambda i, j: (0, j))
    out = pl.BlockSpec((tm, tn), lambda i, j: (i, j))
    return pl.pallas_call(
        body, name="merge_matmul", grid=(T // tm, N // tn),
        out_shape=[jax.ShapeDtypeStruct((T, N), BF16)] * 3,
        in_specs=[lhs, lhs, rhs, rhs, pl.BlockSpec((tm, tn), lambda i, j: (i, COL_GA * gpb + j)),
                  pl.BlockSpec((tm, tn), lambda i, j: (i, COL_GB * gpb + j))],
        out_specs=[out, out, out], compiler_params=_cp("parallel", "parallel"),
    )(ya_pre, sgu, w_a, w_b, proj, proj)


def out_proj(merged, w_o, h0, gt1, g_post, g_pre2, sc2, sh2):
    T, D = h0.shape
    tm = min(256, T)

    def body(m_ref, w_ref, h_ref, gt_ref, gp_ref, g2_ref, sc_ref, sh_ref, mo_ref, h1_ref, a2_ref):
        mo = _nn(m_ref[...], w_ref[...])
        mo_ref[...] = mo
        r = lax.rsqrt(jnp.mean(mo * mo, axis=-1, keepdims=True) + EPS)
        h1 = h_ref[...] + gt_ref[...] * ((mo * r) * gp_ref[...])
        h1_ref[...] = h1
        r2 = lax.rsqrt(jnp.mean(h1 * h1, axis=-1, keepdims=True) + EPS)
        a2_ref[...] = ((h1 * r2) * g2_ref[...] * (1.0 + sc_ref[...]) + sh_ref[...]).astype(BF16)

    row = pl.BlockSpec((tm, D), lambda i: (i, 0))
    return pl.pallas_call(
        body, name="out_proj", grid=(T // tm,),
        out_shape=[jax.ShapeDtypeStruct((T, D), F32), jax.ShapeDtypeStruct((T, D), F32), jax.ShapeDtypeStruct((T, D), BF16)],
        in_specs=[row, pl.BlockSpec((D, D), lambda i: (0, 0)), row] + [_vec(D)] * 5,
        out_specs=[row, row, row], compiler_params=_cp("parallel"),
    )(merged, w_o, h0, gt1, g_post, g_pre2, sc2, sh2)


def loss_bwd(ff, h1, tgt, gt2, g_post):
    T, D = ff.shape
    tm = min(256, T)

    def body(f_ref, h_ref, t_ref, gt_ref, g_ref, dy_ref, dff_ref, loss_ref, dgt_ref, dg_ref):
        @pl.when(pl.program_id(0) == 0)
        def _():
            loss_ref[...] = jnp.zeros_like(loss_ref)
            dgt_ref[...] = jnp.zeros_like(dgt_ref)
            dg_ref[...] = jnp.zeros_like(dg_ref)

        ff = f_ref[...]
        gt, g = gt_ref[...], g_ref[...]
        r = lax.rsqrt(jnp.mean(ff * ff, axis=-1, keepdims=True) + EPS)
        fhat = ff * r
        nf = fhat * g
        err = (h_ref[...] + gt * nf) - t_ref[...]
        loss_ref[...] += jnp.sum(err * err)
        dy = err * (1.0 / D)
        dy_ref[...] = dy
        dgt_ref[...] += _colsum(dy * nf)
        dnf = dy * gt
        dg_ref[...] += _colsum(dnf * fhat)
        u = dnf * g
        dff_ref[...] = (r * (u - fhat * jnp.mean(u * fhat, axis=-1, keepdims=True))).astype(BF16)

    row = pl.BlockSpec((tm, D), lambda i: (i, 0))
    return pl.pallas_call(
        body, name="loss_bwd", grid=(T // tm,),
        out_shape=[jax.ShapeDtypeStruct((T, D), F32), jax.ShapeDtypeStruct((T, D), BF16), jax.ShapeDtypeStruct((8, 128), F32),
                   jax.ShapeDtypeStruct((1, D), F32), jax.ShapeDtypeStruct((1, D), F32)],
        in_specs=[row, row, row, _vec(D), _vec(D)],
        out_specs=[row, row, pl.BlockSpec((8, 128), lambda i: (0, 0)), _vec(D), _vec(D)],
        compiler_params=_cp("arbitrary"),
    )(ff, h1, tgt, gt2, g_post)


def ff2_bwd(dff, w_ff2, f1):
    T, D = dff.shape
    K = w_ff2.shape[0]
    tm, tn = min(512, T), 2048

    def body(a_ref, w_ref, f_ref, o_ref):
        o_ref[...] = (_nt(a_ref[...], w_ref[...]) * (2.0 * jnp.maximum(f_ref[...].astype(F32), 0.0))).astype(BF16)

    return pl.pallas_call(
        body, name="ff2_bwd", out_shape=jax.ShapeDtypeStruct((T, K), BF16), grid=(K // tn, T // tm),
        in_specs=[pl.BlockSpec((tm, D), lambda j, i: (i, 0)), pl.BlockSpec((tn, D), lambda j, i: (j, 0)),
                  pl.BlockSpec((tm, tn), lambda j, i: (i, j))],
        out_specs=pl.BlockSpec((tm, tn), lambda j, i: (i, j)), compiler_params=_cp("parallel", "parallel"),
    )(dff, w_ff2, f1)


def ffn_norm_bwd(dy, da2, h1, mo, g_pre2, sc2, gt1, g_post, comm):
    T, D = dy.shape
    tm = min(256, T)

    def body(dy_ref, da_ref, h_ref, mo_ref, g2_ref, sc_ref, gt_ref, gp_ref, dh_ref, dmo_ref, s_sh, s_sc, s_g2, s_gt, s_gp):
        @pl.when(pl.program_id(0) == 0)
        def _():
            for s in (s_sh, s_sc, s_g2, s_gt, s_gp):
                s[...] = jnp.zeros_like(s)

        h1, da = h_ref[...], da_ref[...]
        g2, sc = g2_ref[...], sc_ref[...]
        r2 = lax.rsqrt(jnp.mean(h1 * h1, axis=-1, keepdims=True) + EPS)
        n2 = h1 * r2
        s_sh[...] += _colsum(da)
        s_sc[...] += _colsum(da * (n2 * g2))
        s_g2[...] += _colsum(da * (1.0 + sc) * n2)
        dn2 = da * g2 * (1.0 + sc)
        dh1 = dy_ref[...] + r2 * (dn2 - n2 * jnp.mean(dn2 * n2, axis=-1, keepdims=True))
        dh_ref[...] = dh1
        mo = mo_ref[...]
        gt, gp = gt_ref[...], gp_ref[...]
        r = lax.rsqrt(jnp.mean(mo * mo, axis=-1, keepdims=True) + EPS)
        mhat = mo * r
        s_gt[...] += _colsum(dh1 * (mhat * gp))
        dnm = dh1 * gt
        s_gp[...] += _colsum(dnm * mhat)
        u = dnm * gp
        dmo_ref[...] = (r * (u - mhat * jnp.mean(u * mhat, axis=-1, keepdims=True))).astype(BF16)

    row = pl.BlockSpec((tm, D), lambda i: (i, 0))
    vec_out = jax.ShapeDtypeStruct((1, D), F32)
    return _pallas(
        body, name="ffn_norm_bwd", grid=(T // tm,),
        out_shape=[jax.ShapeDtypeStruct((T, D), F32), jax.ShapeDtypeStruct((T, D), BF16)] + [vec_out] * 5,
        in_specs=[row, row, row, row] + [_vec(D)] * 4, out_specs=[row, row] + [_vec(D)] * 5,
        scratch=[], semantics=("arbitrary",), operands=(dy, da2, h1, mo, g_pre2, sc2, gt1, g_post), comm=comm)


def out_proj_bwd(dmo, w_o, y_a, y_b, proj):
    T, D = dmo.shape
    tm, tn = min(512, T), 512
    gpb = 1024 // tn

    def body(a_ref, w_ref, ya_ref, yb_ref, ga_ref, gb_ref, dya_ref, dyb_ref, dga_ref, dgb_ref):
        dm = _nt(a_ref[...], w_ref[...])
        sa, sb = jax.nn.sigmoid(ga_ref[...]), jax.nn.sigmoid(gb_ref[...])
        dya_ref[...] = (dm * sa).astype(BF16)
        dyb_ref[...] = (dm * sb).astype(BF16)
        dga_ref[...] = (dm * ya_ref[...].astype(F32) * sa * (1.0 - sa)).astype(BF16)
        dgb_ref[...] = (dm * yb_ref[...].astype(F32) * sb * (1.0 - sb)).astype(BF16)

    out = pl.BlockSpec((tm, tn), lambda i, j: (i, j))
    return pl.pallas_call(
        body, name="out_proj_bwd", grid=(T // tm, D // tn), out_shape=[jax.ShapeDtypeStruct((T, D), BF16)] * 4,
        in_specs=[pl.BlockSpec((tm, D), lambda i, j: (i, 0)), pl.BlockSpec((tn, D), lambda i, j: (j, 0)), out, out,
                  pl.BlockSpec((tm, tn), lambda i, j: (i, COL_GA * gpb + j)), pl.BlockSpec((tm, tn), lambda i, j: (i, COL_GB * gpb + j))],
        out_specs=[out] * 4, compiler_params=_cp("parallel", "parallel"),
    )(dmo, w_o, y_a, y_b, proj, proj)


def sgu_bwd(proj, dsgu, g_norm, w_spatial, b_spatial_t):
    T = proj.shape[0]
    W = 1024
    gw = W // SGU_GROUPS

    def body(u_ref, v_ref, ds_ref, g_ref, ws_ref, bst_ref, dz_ref, dw_ref, db_ref, dg_ref):
        @pl.when(pl.program_id(0) == 0)
        def _():
            dw_ref[...] = jnp.zeros_like(dw_ref)
            db_ref[...] = jnp.zeros_like(db_ref)
            dg_ref[...] = jnp.zeros_like(dg_ref)

        zu, rstd, dhat, vn, vm = _sgu_mix(u_ref, v_ref, g_ref, ws_ref, bst_ref)
        ds = ds_ref[...]
        du = ds * vm
        dvm = ds * zu
        dvm_b = dvm.astype(BF16)
        ones = jnp.ones((8, gw), F32)
        dvn = []
        for g in range(SGU_GROUPS):
            sl = slice(g * gw, (g + 1) * gw)
            dw_ref[g] += _nt(dvm_b[:, sl], vn[:, sl])
            db_ref[g] += lax.dot_general(ones, dvm[:, sl], (((1,), (1,)), ((), ())), precision=HI, preferred_element_type=F32)
            dvn.append(_tn(ws_ref[g].astype(BF16), dvm_b[:, sl]))
        dvn = jnp.concatenate(dvn, axis=1)
        dg_ref[...] += _colsum(dvn * dhat)
        ddh = dvn * g_ref[...]
        dzv = rstd * (ddh - jnp.mean(ddh, axis=-1, keepdims=True) - dhat * jnp.mean(ddh * dhat, axis=-1, keepdims=True))
        dz_ref[:, 0:W] = (du * _gelu_grad(u_ref[...])).astype(BF16)
        dz_ref[:, W:2 * W] = (dzv * _gelu_grad(v_ref[...])).astype(BF16)

    blk = lambda cb: pl.BlockSpec((SGU_CHUNK, W), lambda i: (i, cb))
    full3 = lambda a, b, c: pl.BlockSpec((a, b, c), lambda i: (0, 0, 0))
    return pl.pallas_call(
        body, name="sgu_bwd", grid=(T // SGU_CHUNK,),
        out_shape=[jax.ShapeDtypeStruct((T, 2 * W), BF16), jax.ShapeDtypeStruct((SGU_GROUPS, SGU_CHUNK, SGU_CHUNK), F32),
                   jax.ShapeDtypeStruct((SGU_GROUPS, 8, SGU_CHUNK), F32), jax.ShapeDtypeStruct((1, W), F32)],
        in_specs=[blk(COL_U), blk(COL_ZV), blk(0), _vec(W), full3(SGU_GROUPS, SGU_CHUNK, SGU_CHUNK),
                  pl.BlockSpec((SGU_CHUNK, SGU_GROUPS), lambda i: (0, 0))],
        out_specs=[pl.BlockSpec((SGU_CHUNK, 2 * W), lambda i: (i, 0)), full3(SGU_GROUPS, SGU_CHUNK, SGU_CHUNK),
                   full3(SGU_GROUPS, 8, SGU_CHUNK), _vec(W)],
        compiler_params=_cp("arbitrary"),
    )(proj, proj, dsgu, g_norm, w_spatial, b_spatial_t)


def hgrn_post_bwd(dya, o, proj, g_norm):
    T, W = o.shape
    tm = min(256, T)

    def body(dy_ref, o_ref, og_ref, g_ref, do_ref, dog_ref, dg_ref):
        @pl.when(pl.program_id(0) == 0)
        def _():
            dg_ref[...] = jnp.zeros_like(dg_ref)

        g = g_ref[...]
        dg = jnp.zeros((1, HEAD_DIM), F32)
        for h in range(N_HEADS):
            sl = slice(h * HEAD_DIM, (h + 1) * HEAD_DIM)
            x, og, dy = o_ref[:, sl], og_ref[:, sl], dy_ref[:, sl]
            r = lax.rsqrt(jnp.mean(x * x, axis=-1, keepdims=True) + EPS)
            xhat = x * r
            s = jax.nn.sigmoid(og)
            don = dy * (og * s)
            dog_ref[:, sl] = (dy * (xhat * g) * (s * (1.0 + og * (1.0 - s)))).astype(BF16)
            dg += _colsum(don * xhat)
            u = don * g
            do_ref[:, sl] = r * (u - xhat * jnp.mean(u * xhat, axis=-1, keepdims=True))
        dg_ref[...] += dg

    row = pl.BlockSpec((tm, W), lambda i: (i, 0))
    return pl.pallas_call(
        body, name="hgrn_post_bwd", grid=(T // tm,),
        out_shape=[jax.ShapeDtypeStruct((T, W), F32), jax.ShapeDtypeStruct((T, W), BF16), jax.ShapeDtypeStruct((1, HEAD_DIM), F32)],
        in_specs=[row, row, pl.BlockSpec((tm, W), lambda i: (i, COL_OG)), _vec(HEAD_DIM)],
        out_specs=[row, row, _vec(HEAD_DIM)], compiler_params=_cp("arbitrary"),
    )(dya, o, proj, g_norm)


def hgrn_bwd(proj, do, lb_logits, comm=None):
    T = proj.shape[0]
    NC, CPB = T // HGRN_CHUNK, HGRN_BLOCK // HGRN_CHUNK
    W = N_HEADS * HEAD_DIM
    col, f_spec, l_spec = _hgrn_specs(T)

    def body(l_ref, q_ref, f_ref, v_ref, do_ref, dq_ref, dv_ref, dlg_ref, dlb_ref, st_ref, dst_ref, dec_ref, ddec_ref, dqa_ref, dva_ref):
        d = pl.program_id(1)
        lb = _hgrn_lower_bound(l_ref)
        oml = 1.0 - lb
        mask = _hgrn_chunk_mask(d)

        def values(rows):
            s, sn, fg, lf, k = _hgrn_gate(f_ref[rows, :], lb)
            b = _chunk_cumsum(lf, d == 1)
            bl = _chunk_total(lf)
            eb, enb, ee = jnp.exp(b), jnp.exp(-b), jnp.exp(bl - b)
            qd = q_ref[rows, :] * Q_SCALE * eb
            return s, sn, fg, k, bl, eb, enb, ee, qd, k * enb, k * ee

        def block1(i, carry):
            rows = pl.ds(pl.multiple_of(i * HGRN_BLOCK, HGRN_BLOCK), HGRN_BLOCK)
            _, _, _, _, bl, _, _, _, qd, _, ke = values(rows)
            qd, ke = qd.astype(BF16), ke.astype(BF16)
            vb, dob = v_ref[rows, :].astype(BF16), do_ref[rows, :].astype(BF16)
            dec = jnp.exp(bl)
            for cc in range(CPB):
                sl = slice(cc * HGRN_CHUNK, (cc + 1) * HGRN_CHUNK)
                n = i * CPB + cc
                st_ref[n] = _tn(vb[sl], ke[sl])
                dst_ref[n] = _tn(dob[sl], qd[sl])
                dec_ref[n] = dec[cc * HGRN_CHUNK:cc * HGRN_CHUNK + 8, :]
            return carry

        _block_loop(T, block1, 0)

        def scan(t, s):
            n = jnp.where(d == 0, t, NC - 1 - t)
            u = st_ref[n]
            st_ref[n] = s
            return dec_ref[n][0:1, :] * s + u

        lax.fori_loop(0, NC, scan, jnp.zeros((HEAD_DIM, HEAD_DIM), F32))

        def rscan(t, ds):
            n = jnp.where(d == 0, NC - 1 - t, t)
            w = dst_ref[n]
            dst_ref[n] = ds
            ddec_ref[n] = jnp.broadcast_to(_colsum(ds * st_ref[n]), (8, HEAD_DIM))
            return dec_ref[n][0:1, :] * ds + w

        lax.fori_loop(0, NC, rscan, jnp.zeros((HEAD_DIM, HEAD_DIM), F32))

        def block3(i, dlb):
            rows = pl.ds(pl.multiple_of(i * HGRN_BLOCK, HGRN_BLOCK), HGRN_BLOCK)
            s, sn, fg, k, bl, eb, enb, ee, qd, kd, ke = values(rows)
            qdb, kdb, keb = qd.astype(BF16), kd.astype(BF16), ke.astype(BF16)
            vb, dob = v_ref[rows, :].astype(BF16), do_ref[rows, :].astype(BF16)
            att = jnp.where(mask, _nt(qdb, kdb), 0.0).astype(BF16)
            datt = jnp.where(mask, _nt(dob, vb), 0.0).astype(BF16)
            dv = _tn(att, dob)
            dqd = _nn(datt, kdb)
            dkd = _tn(datt, qdb)
            dv_i, dqd_i, dke, ddl = [], [], [], []
            for cc in range(CPB):
                sl = slice(cc * HGRN_CHUNK, (cc + 1) * HGRN_CHUNK)
                n = i * CPB + cc
                st_b, dst_b = st_ref[n].astype(BF16), dst_ref[n].astype(BF16)
                dv_i.append(_nt(keb[sl], dst_b))
                dqd_i.append(_nn(dob[sl], st_b))
                dke.append(_nn(vb[sl], dst_b))
                ddl.append(jnp.broadcast_to(ddec_ref[n][0:1, :] * dec_ref[n][0:1, :], (HGRN_CHUNK, HEAD_DIM)))
            dv = dv + jnp.concatenate(dv_i, axis=0)
            dqd = dqd + jnp.concatenate(dqd_i, axis=0)
            dke = jnp.concatenate(dke, axis=0)
            dq = dqd * eb * Q_SCALE
            dk = dkd * enb + dke * ee
            t_end = dke * ke
            db = dqd * qd - dkd * kd - t_end
            dlf = _chunk_cumsum(db, d == 0) + _chunk_total(t_end) + jnp.concatenate(ddl, axis=0)
            e = dlf / fg - dk
            dlg_ref[rows, :] = (oml * e * s * sn).astype(BF16)

            dq = jnp.where(d == 0, 0.0, dqa_ref[rows, :]) + dq
            dv = jnp.where(d == 0, 0.0, dva_ref[rows, :]) + dv
            dqa_ref[rows, :] = dq
            dva_ref[rows, :] = dv
            dq_ref[rows, :] = dq.astype(BF16)
            dv_ref[rows, :] = dv.astype(BF16)

            return dlb + _colsum(e * sn)

        dlb_ref[...] = _block_loop(T, block3, jnp.zeros((1, HEAD_DIM), F32))

    head = pl.BlockSpec((T, HEAD_DIM), lambda h, d: (0, h))
    big = pltpu.VMEM((NC, HEAD_DIM, HEAD_DIM), F32)
    small = pltpu.VMEM((NC, 8, HEAD_DIM), F32)
    acc = pltpu.VMEM((T, HEAD_DIM), F32)
    outs, landed = _pallas(
        body, name="hgrn_bwd", grid=(N_HEADS, 2),
        out_shape=[jax.ShapeDtypeStruct((T, W), BF16), jax.ShapeDtypeStruct((T, W), BF16), jax.ShapeDtypeStruct((T, 2 * W), BF16),
                   jax.ShapeDtypeStruct((2, 1, W), F32)],
        in_specs=[l_spec, col(COL_Q), f_spec, col(COL_V), head],
        out_specs=[head, head, pl.BlockSpec((T, HEAD_DIM), lambda h, d: (0, N_HEADS * d + h)),
                   pl.BlockSpec((None, 1, HEAD_DIM), lambda h, d: (d, 0, h))],
        scratch=[big, big, small, small, acc, acc], semantics=("parallel", "arbitrary"), operands=(lb_logits, proj, proj, proj, do), comm=comm)
    return outs if comm is None else (outs, landed)


def mix_norm_bwd(da1, h0, dh1, g_pre, sc1):
    T, D = h0.shape
    tm = min(256, T)

    def body(da_ref, h_ref, dh_ref, g_ref, sc_ref, gx_ref, s_sh, s_sc, s_g):
        @pl.when(pl.program_id(0) == 0)
        def _():
            for s in (s_sh, s_sc, s_g):
                s[...] = jnp.zeros_like(s)

        h, da = h_ref[...], da_ref[...]
        g, sc = g_ref[...], sc_ref[...]
        r = lax.rsqrt(jnp.mean(h * h, axis=-1, keepdims=True) + EPS)
        n = h * r
        s_sh[...] += _colsum(da)
        s_sc[...] += _colsum(da * (n * g))
        s_g[...] += _colsum(da * (1.0 + sc) * n)
        dn = da * g * (1.0 + sc)
        gx_ref[...] = dh_ref[...] + r * (dn - n * jnp.mean(dn * n, axis=-1, keepdims=True))

    row = pl.BlockSpec((tm, D), lambda i: (i, 0))
    return pl.pallas_call(
        body, name="mix_norm_bwd", grid=(T // tm,),
        out_shape=[jax.ShapeDtypeStruct((T, D), F32)] + [jax.ShapeDtypeStruct((1, D), F32)] * 3,
        in_specs=[row, row, row, _vec(D), _vec(D)], out_specs=[row] + [_vec(D)] * 3, compiler_params=_cp("arbitrary"),
    )(da1, h0, dh1, g_pre, sc1)


def adamw(w, g, m, v, name):
    R, C = w.shape
    tr = R if R * C * 4 <= (1 << 21) else max(8, ((1 << 21) // (C * 4)) // 8 * 8)
    while R % tr:
        tr -= 8

    def body(w_ref, g_ref, m_ref, v_ref, d_ref, m2_ref, v2_ref):
        d_ref[...], m2_ref[...], v2_ref[...] = _adamw(w_ref[...], g_ref[...], m_ref[...], v_ref[...])

    row = pl.BlockSpec((tr, C), lambda i: (i, 0))
    return pl.pallas_call(
        body, name=name, grid=(R // tr,), out_shape=[jax.ShapeDtypeStruct((R, C), F32)] * 3,
        in_specs=[row] * 4, out_specs=[row] * 3, compiler_params=_cp("parallel"),
    )(w, g, m, v)


def wada_update(c_all, dmod, w, m, v):
    D, N = w.shape
    tm, tn = 512, 1024

    def body(c_ref, dm_ref, w_ref, m_ref, v_ref, g_ref, d_ref, m2_ref, v2_ref):
        c = c_ref[...]
        g = lax.dot_general(c * jax.nn.sigmoid(c), dm_ref[...], (((0,), (0,)), ((), ())), precision=HI, preferred_element_type=F32)
        g_ref[...] = g
        d_ref[...], m2_ref[...], v2_ref[...] = _adamw(w_ref[...], g, m_ref[...], v_ref[...])

    blk = pl.BlockSpec((tm, tn), lambda i, j: (i, j))
    return pl.pallas_call(
        body, name="wada_update", grid=(D // tm, N // tn), out_shape=[jax.ShapeDtypeStruct((D, N), F32)] * 4,
        in_specs=[pl.BlockSpec((8, tm), lambda i, j: (0, i)), pl.BlockSpec((8, tn), lambda i, j: (0, j)), blk, blk, blk],
        out_specs=[blk] * 4, compiler_params=_cp("parallel", "parallel"),
    )(c_all, dmod, w, m, v)


def sum_devices(gathered, name):
    n, R, C = gathered.shape

    def body(g_ref, o_ref):
        s = g_ref[0]
        for i in range(1, n):
            s = s + g_ref[i]
        o_ref[...] = s

    return pl.pallas_call(body, name=name, out_shape=jax.ShapeDtypeStruct((R, C), F32), compiler_params=_cp())(gathered)


def lb_logits_grad(dlb, lb_logits):
    def body(d_ref, l_ref, o_ref):
        for d in range(2):
            l0, l1 = l_ref[d, 0:1, :], l_ref[d, 1:2, :]
            m = jnp.maximum(l0, l1)
            e0, e1 = jnp.exp(l0 - m), jnp.exp(l1 - m)
            p0, p1 = e0 / (e0 + e1), e1 / (e0 + e1)
            g = d_ref[d:d + 1, :]
            o_ref[d, 0:1, :] = p0 * (g - p0 * g)
            o_ref[d, 1:2, :] = -p1 * (p0 * g)

    return pl.pallas_call(body, name="lb_logits_grad", out_shape=jax.ShapeDtypeStruct(lb_logits.shape, F32), compiler_params=_cp())(dlb, lb_logits)


def add_halves(g, landed, core):
    nj, _, r, cc = g.shape
    tr = min(256, r)

    def body(core_ref, g_ref, l_ref, o_ref):
        o_ref[...] = (g_ref[...].astype(F32) + l_ref[...].astype(F32)).astype(BF16)

    return pl.pallas_call(
        body, name="add_halves_%dx%d" % (r, cc), out_shape=jax.ShapeDtypeStruct((nj, r, cc), BF16),
        grid_spec=pltpu.PrefetchScalarGridSpec(
            num_scalar_prefetch=1, grid=(nj, r // tr),
            in_specs=[pl.BlockSpec((None, None, tr, cc), lambda j, i, core_ref: (j, core_ref[0], i, 0)),
                      pl.BlockSpec((None, None, tr, cc), lambda j, i, core_ref: (j, 0, i, 0))],
            out_specs=pl.BlockSpec((None, tr, cc), lambda j, i, core_ref: (j, i, 0))),
        compiler_params=_cp("parallel", "parallel"),
    )(core, g, landed)


def sum_chips(parts, landed, chip):
    nj, r, cc = parts.shape
    tr = min(256, r)

    def body(chip_ref, p_ref, l_ref, o_ref):
        mine = p_ref[...].astype(F32)
        s = None
        for j in range(nj):
            t = jnp.where(chip_ref[0] == j, mine, l_ref[j].astype(F32))
            s = t if s is None else s + t
        o_ref[...] = s

    return pl.pallas_call(
        body, name="sum_chips_%dx%d" % (r, cc), out_shape=jax.ShapeDtypeStruct((r, cc), F32),
        grid_spec=pltpu.PrefetchScalarGridSpec(
            num_scalar_prefetch=1, grid=(r // tr,),
            in_specs=[pl.BlockSpec((None, tr, cc), lambda i, chip_ref: (chip_ref[0], i, 0)),
                      pl.BlockSpec((nj, tr, cc), lambda i, chip_ref: (0, i, 0))],
            out_specs=pl.BlockSpec((tr, cc), lambda i, chip_ref: (i, 0))),
        compiler_params=_cp("parallel"),
    )(chip, parts, landed)


def adamw_halves(w, own, other, m, v, core, name):
    r, cc = own.shape
    tr = min(128, r)
    nb = r // tr

    def body(core_ref, w_ref, a_ref, b_ref, m_ref, v_ref, g_ref, d_ref, m2_ref, v2_ref):
        g = jnp.where(pl.program_id(0) == core_ref[0], a_ref[...], b_ref[...])
        g_ref[...] = g
        d_ref[...], m2_ref[...], v2_ref[...] = _adamw(w_ref[...], g, m_ref[...], v_ref[...])

    full = pl.BlockSpec((tr, cc), lambda h, i, core_ref: (h * nb + i, 0))
    mine = pl.BlockSpec((tr, cc), lambda h, i, core_ref: (jnp.where(h == core_ref[0], i, 0), 0))
    theirs = pl.BlockSpec((tr, cc), lambda h, i, core_ref: (jnp.where(h == core_ref[0], 0, i), 0))
    return pl.pallas_call(
        body, name=name, out_shape=[jax.ShapeDtypeStruct((2 * r, cc), F32)] * 4,
        grid_spec=pltpu.PrefetchScalarGridSpec(
            num_scalar_prefetch=1, grid=(2, nb), in_specs=[full, mine, theirs, full, full], out_specs=[full] * 4),
        compiler_params=_cp("arbitrary", "arbitrary"),
    )(core, w, own, other, m, v)


def _place():
    mx, my, mc = lax.axis_index("x"), lax.axis_index("y"), lax.axis_index("c")
    chips = [(1 - mx, my), (mx, 1 - my), (1 - mx, 1 - my)]
    return mx, my, mc, chips


def all_gather_small(x, name):
    R, C = x.shape

    def body(x_ref, out_ref, send_sems, recv_sems, local_sem):
        mx, my, mc, _ = _place()
        me = 4 * mx + 2 * my + mc
        mine = pltpu.make_async_copy(x_ref, out_ref.at[me], local_sem)
        mine.start()

        def peer(k):
            px = 1 - mx if k & 4 else mx
            py = 1 - my if k & 2 else my
            pc = 1 - mc if k & 1 else mc
            return px, py, pc

        def copy(k, src, slot):
            return pltpu.make_async_remote_copy(src_ref=src, dst_ref=out_ref.at[slot], send_sem=send_sems.at[k - 1],
                                                recv_sem=recv_sems.at[k - 1], device_id=peer(k), device_id_type=MESH)

        sends = [copy(k, x_ref, me) for k in range(1, 8)]
        for cp in sends:
            cp.start()
        for k in range(1, 8):
            px, py, pc = peer(k)
            slot = 4 * px + 2 * py + pc
            copy(k, out_ref.at[slot], slot).wait_recv()
        for cp in sends:
            cp.wait_send()
        mine.wait()

    return pl.pallas_call(
        body, name=name, out_shape=jax.ShapeDtypeStruct((8, R, C), F32),
        in_specs=[pl.BlockSpec(memory_space=pltpu.VMEM)], out_specs=pl.BlockSpec(memory_space=pltpu.VMEM),
        scratch_shapes=[pltpu.SemaphoreType.DMA((7,)), pltpu.SemaphoreType.DMA((7,)), pltpu.SemaphoreType.DMA],
        compiler_params=_cp(),
    )(x)


def gather8_comm(x):
    def copies(x_ref, out_ref, send_sems, recv_sems):
        mx, my, mc, _ = _place()
        me = 4 * mx + 2 * my + mc

        def peer(k):
            return (1 - mx if k & 4 else mx, 1 - my if k & 2 else my, 1 - mc if k & 1 else mc)

        def copy(k, src, slot):
            return pltpu.make_async_remote_copy(src_ref=src, dst_ref=out_ref.at[slot], send_sem=send_sems.at[k - 1],
                                                recv_sem=recv_sems.at[k - 1], device_id=peer(k), device_id_type=MESH)

        sends = [copy(k, x_ref, me) for k in range(1, 8)]
        arrivals = []
        for k in range(1, 8):
            px, py, pc = peer(k)
            slot = 4 * px + 2 * py + pc
            arrivals.append(copy(k, out_ref.at[slot], slot))
        return sends, arrivals, pltpu.make_async_copy(x_ref, out_ref.at[me], send_sems.at[7])

    def start(cin, cout, send_sems, recv_sems):
        sends, _, mine = copies(cin[0], cout[0], send_sems, recv_sems)
        mine.start()
        for cp in sends:
            cp.start()

    def finish(cin, cout, send_sems, recv_sems):
        sends, arrivals, mine = copies(cin[0], cout[0], send_sems, recv_sems)
        for cp in arrivals:
            cp.wait_recv()
        for cp in sends:
            cp.wait_send()
        mine.wait()

    return _Comm([x], [jax.ShapeDtypeStruct((8,) + x.shape, F32)], {}, 8, start, finish)


def _join(a, b):
    na_in, na_out = len(a.operands), len(a.out_shape)

    def split(fn_a, fn_b):
        def both(cin, cout, send_sems, recv_sems):
            fn_a(cin[:na_in], cout[:na_out], send_sems.at[pl.ds(0, a.n_sems)], recv_sems.at[pl.ds(0, a.n_sems)])
            fn_b(cin[na_in:], cout[na_out:], send_sems.at[pl.ds(a.n_sems, b.n_sems)], recv_sems.at[pl.ds(a.n_sems, b.n_sems)])
        return both

    aliases = dict(a.aliases)
    aliases.update({na_in + i: na_out + o for i, o in b.aliases.items()})
    return _Comm(a.operands + b.operands, a.out_shape + b.out_shape, aliases, a.n_sems + b.n_sems, split(a.start, b.start), split(a.finish, b.finish))


def _region(ref, kind, j, half, r, cc):
    nr = r if half is None else r // 2
    off = 0 if half is None else half * nr
    if kind == "col":
        return ref.at[pl.ds(off, nr), pl.ds(pl.multiple_of(j * cc, 128), cc)]
    return ref.at[pl.ds(pl.multiple_of(j * r + off, 16), nr), :]


def comm_call(comm, name):
    ni, no = len(comm.operands), len(comm.out_shape)

    def body(*refs):
        comm.start(refs[:ni], refs[ni:ni + no], *refs[ni + no:])
        comm.finish(refs[:ni], refs[ni:ni + no], *refs[ni + no:])

    return pl.pallas_call(
        body, name=name, out_shape=comm.out_shape, in_specs=[ANY] * ni, out_specs=[ANY] * no, input_output_aliases=comm.aliases,
        scratch_shapes=[pltpu.SemaphoreType.DMA((comm.n_sems,)), pltpu.SemaphoreType.DMA((comm.n_sems,))], compiler_params=_cp(),
    )(*comm.operands)


def gather_comm(fulls, kinds, dims):
    n = len(fulls)

    def copies(f_refs, send_sems, recv_sems):
        mx, my, mc, chips = _place()
        jme = 2 * mx + my

        def landed(w, k, half):
            px, py = chips[k]
            return _region(f_refs[w], kinds[w], 2 * px + py, half, *dims[w])

        def over_ici(w, k, reg):
            px, py = chips[k]
            return pltpu.make_async_remote_copy(src_ref=reg, dst_ref=reg, send_sem=send_sems.at[6 * w + k], recv_sem=recv_sems.at[6 * w + k],
                                                device_id=(px, py, mc), device_id_type=MESH)

        def over_d2d(w, k, half):
            reg = landed(w, k, half)
            return pltpu.make_async_remote_copy(src_ref=reg, dst_ref=reg, send_sem=send_sems.at[6 * w + 3 + k],
                                                recv_sem=recv_sems.at[6 * w + 3 + k], device_id=(mx, my, 1 - mc), device_id_type=MESH)

        sends = [over_ici(w, k, _region(f_refs[w], kinds[w], jme, mc, *dims[w])) for w in range(n) for k in range(3)]
        return mc, landed, over_ici, over_d2d, sends

    def start(cin, f_refs, send_sems, recv_sems):
        for cp in copies(f_refs, send_sems, recv_sems)[4]:
            cp.start()

    def finish(cin, f_refs, send_sems, recv_sems):
        mc, landed, over_ici, over_d2d, sends = copies(f_refs, send_sems, recv_sems)
        passed = []
        for w in range(n):
            for k in range(3):
                over_ici(w, k, landed(w, k, mc)).wait_recv()
                cp = over_d2d(w, k, mc)
                cp.start()
                passed.append(cp)
        for w in range(n):
            for k in range(3):
                over_d2d(w, k, 1 - mc).wait_recv()
        for cp in sends + passed:
            cp.wait_send()

    return _Comm(fulls, [jax.ShapeDtypeStruct(f.shape, BF16) for f in fulls], {w: w for w in range(n)}, 6 * n, start, finish)


def exchange_comm(grads):
    n = len(grads)

    def copies(g_refs, l_refs, send_sems, recv_sems):
        mx, my, mc, _ = _place()
        return [pltpu.make_async_remote_copy(src_ref=g_refs[w].at[:, pl.ds(1 - mc, 1)], dst_ref=l_refs[w], send_sem=send_sems.at[w],
                                             recv_sem=recv_sems.at[w], device_id=(mx, my, 1 - mc), device_id_type=MESH) for w in range(n)]

    def start(*refs):
        for cp in copies(*refs):
            cp.start()

    def finish(*refs):
        for cp in copies(*refs):
            cp.wait()

    return _Comm(grads, [jax.ShapeDtypeStruct((g.shape[0], 1) + g.shape[2:], BF16) for g in grads], {}, n, start, finish)


def exchange_halves(grads, name):
    return comm_call(exchange_comm(grads), name)


def scatter_comm(parts):
    n = len(parts)

    def sends(p_refs, l_refs, send_sems, recv_sems):
        mx, my, mc, chips = _place()
        return [pltpu.make_async_remote_copy(src_ref=p_refs[w].at[2 * px + py], dst_ref=l_refs[w].at[2 * mx + my],
                                             send_sem=send_sems.at[3 * w + k], recv_sem=recv_sems.at[3 * w + k],
                                             device_id=(px, py, mc), device_id_type=MESH) for w in range(n) for k, (px, py) in enumerate(chips)]

    def start(p_refs, l_refs, send_sems, recv_sems):
        for cp in sends(p_refs, l_refs, send_sems, recv_sems):
            cp.start()

    def finish(p_refs, l_refs, send_sems, recv_sems):
        mx, my, mc, chips = _place()
        for w in range(n):
            for k, (px, py) in enumerate(chips):
                slot = l_refs[w].at[2 * px + py]
                pltpu.make_async_remote_copy(src_ref=slot, dst_ref=slot, send_sem=send_sems.at[3 * w + k], recv_sem=recv_sems.at[3 * w + k],
                                             device_id=(px, py, mc), device_id_type=MESH).wait_recv()
        for cp in sends(p_refs, l_refs, send_sems, recv_sems):
            cp.wait_send()

    return _Comm(parts, [jax.ShapeDtypeStruct(p.shape, BF16) for p in parts], {}, 3 * n, start, finish)


def share_comm(sums):
    n = len(sums)

    def copies(q_refs, o_refs, send_sems, recv_sems):
        mx, my, mc, _ = _place()
        return [pltpu.make_async_remote_copy(src_ref=q_refs[w], dst_ref=o_refs[w], send_sem=send_sems.at[w], recv_sem=recv_sems.at[w],
                                             device_id=(mx, my, 1 - mc), device_id_type=MESH) for w in range(n)]

    def start(*refs):
        for cp in copies(*refs):
            cp.start()

    def finish(*refs):
        for cp in copies(*refs):
            cp.wait()

    return _Comm(sums, [jax.ShapeDtypeStruct(q.shape, F32) for q in sums], {}, n, start, finish)


def _pack(arrays):
    flat = jnp.concatenate([a.reshape(-1) for a in arrays])
    rows = -(-flat.shape[0] // 1024) * 8
    return jnp.pad(flat, (0, rows * 128 - flat.shape[0])).reshape(rows, 128)


def _unpack(packed, shapes):
    flat, out, off = packed.reshape(-1), [], 0
    for s in shapes:
        n = math.prod(s)
        out.append(flat[off:off + n].reshape(s))
        off += n
    return out


def kernel(x, c, w_ada, b_ada, g_pre_mix, g_post_mix, g_pre_ffn, g_post_ffn, w_in, lb_logits, g_hgrn_norm, w_a_out, g_sgu_norm, w_spatial, b_spatial, w_b_out, w_o, w_ff1, w_ff2, loss_target, m_w_ada, m_b_ada, m_g_pre_mix, m_g_post_mix, m_g_pre_ffn, m_g_post_ffn, m_w_in, m_lb_logits, m_g_hgrn_norm, m_w_a_out, m_g_sgu_norm, m_w_spatial, m_b_spatial, m_w_b_out, m_w_o, m_w_ff1, m_w_ff2, v_w_ada, v_b_ada, v_g_pre_mix, v_g_post_mix, v_g_pre_ffn, v_g_post_ffn, v_w_in, v_lb_logits, v_g_hgrn_norm, v_w_a_out, v_g_sgu_norm, v_w_spatial, v_b_spatial, v_w_b_out, v_w_o, v_w_ff1, v_w_ff2):
    mx, my, mc = lax.axis_index("x"), lax.axis_index("y"), lax.axis_index("c")
    chip, me = 2 * mx + my, 4 * mx + 2 * my + mc
    D = D_MODEL
    h0, tgt = x[0], loss_target[0]
    n_ada = w_ada.shape[2]
    n_lb = lb_logits.shape[2]

    got = all_gather_small(_pack([c, lb_logits]), "gather_inputs")
    c_all = got[:, :D // 128, :].reshape(8, D)
    lb_full = got[0::2, D // 128:D // 128 + 4 * n_lb // 128, :].reshape(4, 2, 2, n_lb).transpose(1, 2, 0, 3).reshape(2, 2, 4 * n_lb)
    b_ada_chip = lax.dynamic_slice(b_ada, (0, chip * n_ada), (1, n_ada))
    mod_cols = mod_matmul(c_all, w_ada[0], b_ada_chip)
    got = all_gather_small(mod_cols.reshape(-1, 128), "gather_mod").reshape(4, 2, 8, n_ada)
    mod = lax.dynamic_index_in_dim(got[:, 0], me, axis=1, keepdims=False).reshape(6, 1, D)
    sh1, sc1, gt1, sh2, sc2, gt2 = (mod[i] for i in range(6))

    big = [("w_in", w_in, "col"), ("w_a_out", w_a_out, "col"), ("w_b_out", w_b_out, "col"), ("w_o", w_o, "row"),
           ("w_ff1", w_ff1, "col"), ("w_ff2", w_ff2, "row")]
    kinds = [k for _, _, k in big]
    chip_idx, core = chip.reshape(1).astype(jnp.int32), mc.reshape(1).astype(jnp.int32)
    fulls = [cast_into_full(w[0], kind, chip_idx, "cast_" + nm) for nm, w, kind in big]
    dims = [w.shape[1:] for _, w, _ in big]
    later = lambda lo, hi: gather_comm(fulls[lo:hi], kinds[lo:hi], dims[lo:hi])
    halves_summed = lambda grads, name: [add_halves(g, l, core) for g, l in zip(grads, exchange_halves(grads, name))]

    bst = b_spatial[0].T
    a1 = prenorm(h0, g_pre_mix, sc1, sh1)
    proj, w_in_f, (w_a_f, w_b_f, w_o_f) = in_proj_gathered(a1, fulls[0], chip_idx, dims[0], later(1, 4))
    o, (w_ff1_f,) = hgrn_fwd(proj, lb_full, comm=later(4, 5))
    ya_pre = hgrn_post_fwd(o, proj, g_hgrn_norm)
    sgu = sgu_fwd(proj, g_sgu_norm, w_spatial[0], bst)
    y_a, y_b, merged = merge_matmul(ya_pre, sgu, w_a_f, w_b_f, proj)
    mo, h1, a2 = out_proj(merged, w_o_f, h0, gt1, g_post_mix, g_pre_ffn, sc2, sh2)
    (f1, hid), (w_ff2_f,) = matmul(a2, w_ff1_f, mode="nn", out_dtype=BF16, tm=1024, tn=1024, tk=2048, name="ff1", relu2=True,
                                   comm=later(5, 6))
    ff = matmul(hid, w_ff2_f, mode="nn", out_dtype=F32, tm=1024, tn=1024, tk=2048, name="ff2")
    dy, dff, loss_parts, d_gt2, d_g_post_ffn = loss_bwd(ff, h1, tgt, gt2, g_post_ffn)
    loss = lax.psum(0.5 * loss_parts[0, 0] / D, ("x", "y", "c"))

    df1 = ff2_bwd(dff, w_ff2_f, f1)
    gr_ff2 = matmul(hid, dff, mode="tn", out_dtype=BF16, tm=1024, tn=1024, tk=2048, name="dw_ff2")
    gr_ff2 = gr_ff2.reshape(4, 2, -1, D)
    da2, (landed_ff2,) = matmul(df1, w_ff1_f, mode="nt", out_dtype=F32, tm=1024, tn=1024, tk=2048, name="da2", comm=exchange_comm([gr_ff2]))
    gr_ff1 = matmul(a2, df1, mode="tn", out_dtype=BF16, tm=1024, tn=2048, tk=1024, name="dw_ff1", split=(4, 2))
    (dh1, dmo, d_sh2, d_sc2, d_g_pre_ffn, d_gt1, d_g_post_mix), (landed_ff1,) = ffn_norm_bwd(
        dy, da2, h1, mo, g_pre_ffn, sc2, gt1, g_post_mix, exchange_comm([gr_ff1]))
    parts_ff = [add_halves(gr_ff1, landed_ff1, core), add_halves(gr_ff2, landed_ff2, core)]
    dya, dyb, dga, dgb = out_proj_bwd(dmo, w_o_f, y_a, y_b, proj)
    gr_o = matmul(merged, dmo, mode="tn", out_dtype=BF16, tm=1024, tn=1024, tk=2048, name="dw_o")
    dsgu = matmul(dyb, w_b_f, mode="nt", out_dtype=F32, tm=512, tn=1024, tk=2048, name="dsgu")
    gr_b = matmul(sgu, dyb, mode="tn", out_dtype=BF16, tm=512, tn=512, tk=4096, name="dw_b_out", split=(4, 2))
    dz, d_w_spatial, d_b_spatial, d_g_sgu = sgu_bwd(proj, dsgu, g_sgu_norm, w_spatial[0], bst)
    dya_pre = matmul(dya, w_a_f, mode="nt", out_dtype=F32, tm=512, tn=1024, tk=2048, name="dya_pre")
    gr_a = matmul(ya_pre, dya, mode="tn", out_dtype=BF16, tm=512, tn=512, tk=4096, name="dw_a_out", split=(4, 2))
    parts_mix = halves_summed([gr_a, gr_b, gr_o.reshape(4, 2, -1, D)], "exchange_mix")
    do, dog, d_g_hgrn = hgrn_post_bwd(dya_pre, o, proj, g_hgrn_norm)
    chips_summed = lambda parts, landed: [sum_chips(p, l, chip_idx) for p, l in zip(parts, landed)]
    (dq, dv, dlg, d_lb), landed_ff = hgrn_bwd(proj, do, lb_full, comm=scatter_comm(parts_ff))
    own_ff = chips_summed(parts_ff, landed_ff)
    dproj = jnp.concatenate([dq, dlg, dv, dog, dz, dga, dgb], axis=1)
    early = _pack([d_g_sgu, d_w_spatial, d_b_spatial[:, 0, :]])
    gr_in, (*landed_mix, got_early) = matmul(a1, dproj, mode="tn", out_dtype=BF16, tm=1024, tn=2816, tk=1024, name="dw_in", split=(4, 2),
                                             comm=_join(scatter_comm(parts_mix), gather8_comm(early)))
    own_mix = chips_summed(parts_mix, landed_mix)
    parts_in = halves_summed([gr_in], "exchange_in")
    da1, (landed_in, *other_rest) = matmul(dproj, w_in_f, mode="nt", out_dtype=F32, tm=1024, tn=1024, tk=2816, name="da1",
                                           comm=_join(scatter_comm(parts_in), share_comm(own_mix + own_ff)))
    own_in = chips_summed(parts_in, [landed_in])
    other_in = comm_call(share_comm(own_in), "share_w_in")
    own, other = own_in + own_mix + own_ff, list(other_in) + other_rest
    grad_x, d_sh1, d_sc1, d_g_pre_mix = mix_norm_bwd(da1, h0, dh1, g_pre_mix, sc1)
    out = {}

    mine = _pack([d_sh1, d_sc1, d_gt1, d_sh2, d_sc2, d_gt2, d_g_pre_mix, d_g_post_mix, d_g_pre_ffn, d_g_post_ffn, d_g_hgrn, d_lb])
    got = all_gather_small(mine, "gather_small_grads")
    g_b_ada, g_g1, g_g2, g_g3, g_g4, g_hg, g_lb = _unpack(
        sum_devices(got, "sum_small_grads"), [(1, 6 * D), (1, D), (1, D), (1, D), (1, D), (1, HEAD_DIM), (2, 1024)])
    g_sg, g_ws, g_bs = _unpack(sum_devices(got_early, "sum_sgu_grads"), [(1, 1024), w_spatial.shape, b_spatial.shape])
    g_lbl = lax.dynamic_slice(lb_logits_grad(g_lb, lb_full), (0, 0, chip * n_lb), (2, 2, n_lb))
    names = ["b_ada", "g_pre_mix", "g_post_mix", "g_pre_ffn", "g_post_ffn", "g_hgrn_norm", "g_sgu_norm", "w_spatial", "b_spatial", "lb_logits"]
    ws = [b_ada, g_pre_mix, g_post_mix, g_pre_ffn, g_post_ffn, g_hgrn_norm, g_sgu_norm, w_spatial, b_spatial, lb_logits]
    gs = [g_b_ada, g_g1, g_g2, g_g3, g_g4, g_hg, g_sg, g_ws, g_bs, g_lbl]
    ms = [m_b_ada, m_g_pre_mix, m_g_post_mix, m_g_pre_ffn, m_g_post_ffn, m_g_hgrn_norm, m_g_sgu_norm, m_w_spatial, m_b_spatial, m_lb_logits]
    vs = [v_b_ada, v_g_pre_mix, v_g_post_mix, v_g_pre_ffn, v_g_post_ffn, v_g_hgrn_norm, v_g_sgu_norm, v_w_spatial, v_b_spatial, v_lb_logits]
    shapes = [w.shape for w in ws]
    upd = adamw(_pack(ws), _pack(gs), _pack(ms), _pack(vs), "adamw_small")
    upd = [_unpack(u, shapes) for u in upd]
    for i, nm in enumerate(names):
        out[nm] = (gs[i], upd[0][i], upd[1][i], upd[2][i])

    dmod_all = got[:, :6 * D // 128, :].reshape(8, 6 * D)
    dmod_chip = lax.dynamic_slice(dmod_all, (0, chip * n_ada), (8, n_ada))
    out["w_ada"] = tuple(a[None] for a in wada_update(c_all, dmod_chip, w_ada[0], m_w_ada[0], v_w_ada[0]))
    for (nm, w, _), a, b, m, v in zip(big, own, other, (m_w_in, m_w_a_out, m_w_b_out, m_w_o, m_w_ff1, m_w_ff2),
                                      (v_w_in, v_w_a_out, v_w_b_out, v_w_o, v_w_ff1, v_w_ff2)):
        out[nm] = tuple(t[None] for t in adamw_halves(w[0], a, b, m[0], v[0], core, "adamw_" + nm))

    order = ["w_ada", "b_ada", "g_pre_mix", "g_post_mix", "g_pre_ffn", "g_post_ffn", "w_in", "lb_logits", "g_hgrn_norm", "w_a_out",
             "g_sgu_norm", "w_spatial", "b_spatial", "w_b_out", "w_o", "w_ff1", "w_ff2"]
    return (loss, grad_x[None], *[out[nm][0] for nm in order], *[out[nm][1] for nm in order], *[out[nm][2] for nm in order],
            *[out[nm][3] for nm in order])
```

```python
import functools
import math

import jax
import jax.numpy as jnp
from jax import lax
from jax.experimental import pallas as pl
from jax.experimental.pallas import tpu as pltpu

F32, BF16 = jnp.float32, jnp.bfloat16
HI = lax.Precision.HIGHEST
MESH = pl.DeviceIdType.MESH
ANY = pl.BlockSpec(memory_space=pl.ANY)

EPS = 1e-6
D_MODEL = 2048
N_HEADS = 8
HEAD_DIM = 128
HGRN_CHUNK = 32
HGRN_BLOCK = 256
SGU_CHUNK = 128
SGU_GROUPS = 8
Q_SCALE = HEAD_DIM ** -0.5
COL_Q, COL_FFW, COL_FBW, COL_V, COL_OG, COL_U, COL_ZV, COL_GA, COL_GB = 0, 1, 2, 3, 4, 5, 6, 7, 9
N_PROJ = 11264
VMEM_BYTES_V7X = 64 * 1024 * 1024
VMEM_LIMIT = VMEM_BYTES_V7X - 8 * 1024 * 1024

ADAM_LR, ADAM_B1, ADAM_B2, ADAM_EPS, ADAM_WD, ADAM_STEP = 0.001, 0.9, 0.999, 1e-08, 0.01, 10
ADAM_C1 = 1.0 - ADAM_B1 ** ADAM_STEP
ADAM_C2 = 1.0 - ADAM_B2 ** ADAM_STEP


def _cp(*sem):
    return pltpu.CompilerParams(dimension_semantics=sem if sem else None, vmem_limit_bytes=VMEM_LIMIT)


def _vec(d):
    return pl.BlockSpec((1, d), lambda *_: (0, 0))


def _colsum(x):
    return jnp.sum(x, axis=0, keepdims=True)


def _nt(a, b):
    return lax.dot_general(a, b, (((1,), (1,)), ((), ())), preferred_element_type=F32)


def _tn(a, b):
    return lax.dot_general(a, b, (((0,), (0,)), ((), ())), preferred_element_type=F32)


def _nn(a, b):
    return jnp.dot(a, b, preferred_element_type=F32)


def _adamw(w, g, m, v):
    m2 = ADAM_B1 * m + (1.0 - ADAM_B1) * g
    v2 = ADAM_B2 * v + (1.0 - ADAM_B2) * (g * g)
    delta = -ADAM_LR * ((m2 / ADAM_C1) / (jnp.sqrt(v2 / ADAM_C2) + ADAM_EPS) + ADAM_WD * w)
    return delta, m2, v2


class _Comm:
    def __init__(self, operands, out_shape, aliases, n_sems, start, finish):
        self.operands, self.out_shape, self.aliases, self.n_sems = list(operands), list(out_shape), dict(aliases), n_sems
        self.start, self.finish = start, finish


def _pallas(body, *, name, grid, in_specs, out_specs, out_shape, scratch, semantics, operands, comm=None):
    if comm is None:
        res = pl.pallas_call(body, name=name, grid=grid, in_specs=in_specs, out_specs=out_specs, out_shape=out_shape,
                             scratch_shapes=scratch, compiler_params=_cp(*semantics))(*operands)
        return res, []
    n_in, n_out, n_scr = len(in_specs), len(out_specs), len(scratch)
    nci, nco = len(comm.operands), len(comm.out_shape)

    def with_comm(*refs):
        ins, rest = refs[:n_in], refs[n_in:]
        cin, rest = rest[:nci], rest[nci:]
        outs, rest = rest[:n_out], rest[n_out:]
        cout, rest = rest[:nco], rest[nco:]
        scr, (send, recv) = rest[:n_scr], rest[n_scr:]
        ids = [pl.program_id(a) for a in range(len(grid))]
        first = functools.reduce(jnp.logical_and, [i == 0 for i in ids])
        last = functools.reduce(jnp.logical_and, [i == g - 1 for i, g in zip(ids, grid)])

        @pl.when(first)
        def _():
            comm.start(cin, cout, send, recv)

        body(*ins, *outs, *scr)

        @pl.when(last)
        def _():
            comm.finish(cin, cout, send, recv)

    res = pl.pallas_call(
        with_comm, name=name, grid=grid, in_specs=list(in_specs) + [ANY] * nci, out_specs=list(out_specs) + [ANY] * nco,
        out_shape=list(out_shape) + comm.out_shape, input_output_aliases={n_in + i: n_out + o for i, o in comm.aliases.items()},
        scratch_shapes=list(scratch) + [pltpu.SemaphoreType.DMA((comm.n_sems,)), pltpu.SemaphoreType.DMA((comm.n_sems,))],
        compiler_params=_cp(*["arbitrary"] * len(grid)),
    )(*operands, *comm.operands)
    return res[:n_out], res[n_out:]


def matmul(a, b, *, mode, out_dtype, tm, tn, tk, name, split=None, comm=None, relu2=False):
    if mode == "tn":
        (K, M), (_, N) = a.shape, b.shape
    elif mode == "nt":
        (M, K), (N, _) = a.shape, b.shape
    else:
        (M, K), (_, N) = a.shape, b.shape
    tm, tn, tk = min(tm, M), min(tn, N), min(tk, K)
    nk = K // tk
    a_spec = pl.BlockSpec((tk, tm), lambda i, j, k: (k, i)) if mode == "tn" else pl.BlockSpec((tm, tk), lambda i, j, k: (i, k))
    b_spec = pl.BlockSpec((tn, tk), lambda i, j, k: (j, k)) if mode == "nt" else pl.BlockSpec((tk, tn), lambda i, j, k: (k, j))
    dot = {"nn": _nn, "nt": _nt, "tn": _tn}[mode]
    if split is None:
        out_shape = jax.ShapeDtypeStruct((M, N), out_dtype)
        out_spec = pl.BlockSpec((tm, tn), lambda i, j, k: (i, j))
    else:
        nj, nh = split
        rows, cols = M // nh, N // nj
        tm, tn = min(tm, rows), min(tn, cols)
        bi, bj = rows // tm, cols // tn
        out_shape = jax.ShapeDtypeStruct((nj, nh, rows, cols), out_dtype)
        out_spec = pl.BlockSpec((None, None, tm, tn), lambda i, j, k: (j // bj, i // bi, i % bi, j % bj))

    def finish(y, o_ref, sq_ref):
        o_ref[...] = y.astype(o_ref.dtype)
        if relu2:
            p = jnp.maximum(y, 0.0)
            sq_ref[0][...] = (p * p).astype(BF16)

    if nk == 1:
        def body(a_ref, b_ref, o_ref, *sq_ref):
            finish(dot(a_ref[...], b_ref[...]), o_ref, sq_ref)
        scratch = []
    else:
        def body(a_ref, b_ref, o_ref, *rest):
            acc_ref, k = rest[-1], pl.program_id(2)

            @pl.when(k == 0)
            def _():
                acc_ref[...] = jnp.zeros_like(acc_ref)

            acc_ref[...] += dot(a_ref[...], b_ref[...])

            @pl.when(k == nk - 1)
            def _():
                finish(acc_ref[...], o_ref, rest[:-1])
        scratch = [pltpu.VMEM((tm, tn), F32)]

    out_specs, out_shapes = [out_spec], [out_shape]
    if relu2:
        out_specs, out_shapes = out_specs + [out_spec], out_shapes + [jax.ShapeDtypeStruct(out_shape.shape, BF16)]
    outs, landed = _pallas(
        body, name=name, grid=(M // tm, N // tn, nk), in_specs=[a_spec, b_spec], out_specs=out_specs, out_shape=out_shapes,
        scratch=scratch, semantics=("parallel", "parallel", "arbitrary"), operands=(a, b), comm=comm)
    out = tuple(outs) if relu2 else outs[0]
    return out if comm is None else (out, landed)


def cast_into_full(w, kind, chip, name):
    r, cc = w.shape
    tr = min(r, 512)
    nb = r // tr

    def body(chip_ref, w_ref, o_ref):
        o_ref[...] = w_ref[...].astype(BF16)

    if kind == "col":
        full, out_map = (r, 4 * cc), lambda i, chip_ref: (i, chip_ref[0])
    else:
        full, out_map = (4 * r, cc), lambda i, chip_ref: (chip_ref[0] * nb + i, 0)
    return pl.pallas_call(
        body, name=name, out_shape=jax.ShapeDtypeStruct(full, BF16),
        grid_spec=pltpu.PrefetchScalarGridSpec(
            num_scalar_prefetch=1, grid=(nb,), in_specs=[pl.BlockSpec((tr, cc), lambda i, chip_ref: (i, 0))],
            out_specs=pl.BlockSpec((tr, cc), out_map)),
        compiler_params=_cp("parallel"),
    )(chip, w)


def mod_matmul(c_all, w_ada, b_ada):
    D, N = w_ada.shape
    tn = 1024

    def body(c_ref, w_ref, b_ref, o_ref):
        c = c_ref[...]
        sc = c * jax.nn.sigmoid(c)
        o_ref[...] = jnp.dot(sc, w_ref[...], precision=HI, preferred_element_type=F32) + b_ref[...]

    return pl.pallas_call(
        body, name="mod_matmul", out_shape=jax.ShapeDtypeStruct((8, N), F32), grid=(N // tn,),
        in_specs=[pl.BlockSpec((8, D), lambda j: (0, 0)), pl.BlockSpec((D, tn), lambda j: (0, j)),
                  pl.BlockSpec((1, tn), lambda j: (0, j))],
        out_specs=pl.BlockSpec((8, tn), lambda j: (0, j)), compiler_params=_cp("parallel"),
    )(c_all, w_ada, b_ada)


def prenorm(h, g, sc, sh):
    T, D = h.shape
    tm = min(256, T)

    def body(h_ref, g_ref, sc_ref, sh_ref, a_ref):
        x = h_ref[...]
        r = lax.rsqrt(jnp.mean(x * x, axis=-1, keepdims=True) + EPS)
        a_ref[...] = ((x * r) * g_ref[...] * (1.0 + sc_ref[...]) + sh_ref[...]).astype(BF16)

    row = pl.BlockSpec((tm, D), lambda i: (i, 0))
    return pl.pallas_call(
        body, name="prenorm", out_shape=jax.ShapeDtypeStruct((T, D), BF16), grid=(T // tm,),
        in_specs=[row, _vec(D), _vec(D), _vec(D)], out_specs=row, compiler_params=_cp("parallel"),
    )(h, g, sc, sh)


def in_proj_gathered(a, w_full, chip, dims, tail):
    T, D = a.shape
    rows, cc = dims
    tm, tn = min(512, T), cc // 2
    ni = T // tm
    half = rows // 2

    nt = len(tail.operands)

    def body(chip_ref, a_ref, w_in_ref, *rest):
        tail_in, (y_ref, w_ref), rest = rest[:nt], rest[nt:nt + 2], rest[nt + 2:]
        tail_out, (wbuf, wsem, send_sems, recv_sems, tail_send, tail_recv) = rest[:nt], rest[nt:]
        q, j, i = pl.program_id(0), pl.program_id(1), pl.program_id(2)
        mx, my, mc, _ = _place()
        me = chip_ref[0]

        def tile(block, jj):
            src = w_ref.at[:, pl.ds(pl.multiple_of(block * cc + jj * tn, 128), tn)]
            return pltpu.make_async_copy(src, wbuf.at[jj], wsem.at[jj])

        def rows_half(block, hh):
            return w_ref.at[pl.ds(pl.multiple_of(hh * half, 16), half), pl.ds(pl.multiple_of(block * cc, 128), cc)]

        def over_ici(s, block):
            peer = (1 - mx if s & 2 else mx, 1 - my if s & 1 else my, mc)
            reg = rows_half(block, mc)
            return pltpu.make_async_remote_copy(src_ref=reg, dst_ref=reg, send_sem=send_sems.at[s - 1], recv_sem=recv_sems.at[s - 1],
                                                device_id=peer, device_id_type=MESH)

        def over_d2d(s, block, hh):
            reg = rows_half(block, hh)
            return pltpu.make_async_remote_copy(src_ref=reg, dst_ref=reg, send_sem=send_sems.at[2 + s], recv_sem=recv_sems.at[2 + s],
                                                device_id=(mx, my, 1 - mc), device_id_type=MESH)

        def passed_on(s, block):
            k = s - 1
            reg = w_ref.at[pl.ds(pl.multiple_of(mc * half + k * (half // 2), 16), half // 2), pl.ds(pl.multiple_of(block * cc, 128), cc)]
            peer = (1 - mx, my, mc) if s == 1 else (mx, 1 - my, mc)
            return pltpu.make_async_remote_copy(src_ref=reg, dst_ref=reg, send_sem=send_sems.at[6 + k], recv_sem=recv_sems.at[6 + k],
                                                device_id=peer, device_id_type=MESH)

        @pl.when((q == 0) & (j == 0) & (i == 0))
        def _():
            for s in (1, 2):
                over_ici(s, me).start()
            tile(me, 0).start()

        @pl.when(i == 0)
        def _():
            tile(me ^ q, j).wait()

        @pl.when((i == 0) & (j == 0))
        def _():
            tile(me ^ q, 1).start()

        y_ref[...] = _nn(a_ref[...], wbuf[j])

        @pl.when((q == 0) & (j == 1) & (i == ni - 1))
        def _():
            for s in (1, 2):
                over_ici(s, me ^ s).wait_recv()
                passed_on(s, me ^ s).start()
                over_d2d(s, me ^ s, mc).start()
            tail.start(tail_in, tail_out, tail_send, tail_recv)
            over_d2d(1, me ^ 1, 1 - mc).wait_recv()
            tile(me ^ 1, 0).start()

        @pl.when((q == 1) & (j == 1) & (i == ni - 1))
        def _():
            over_d2d(2, me ^ 2, 1 - mc).wait_recv()
            tile(me ^ 2, 0).start()

        @pl.when((q == 2) & (j == 1) & (i == ni - 1))
        def _():
            for s in (1, 2):
                passed_on(s, me ^ 3).wait_recv()
            over_d2d(3, me ^ 3, mc).start()
            over_d2d(3, me ^ 3, 1 - mc).wait_recv()
            tile(me ^ 3, 0).start()

        @pl.when((q == 3) & (j == 1) & (i == ni - 1))
        def _():
            for s in (1, 2):
                over_ici(s, me).wait_send()
                passed_on(s, me ^ s).wait_send()
            for s in (1, 2, 3):
                over_d2d(s, me ^ s, mc).wait_send()
            tail.finish(tail_in, tail_out, tail_send, tail_recv)

    dma = pltpu.SemaphoreType.DMA
    y, w_out, *tail_res = pl.pallas_call(
        body, name="in_proj", out_shape=[jax.ShapeDtypeStruct((T, 4 * cc), F32), jax.ShapeDtypeStruct(w_full.shape, BF16)] + tail.out_shape,
        grid_spec=pltpu.PrefetchScalarGridSpec(
            num_scalar_prefetch=1, grid=(4, 2, ni),
            in_specs=[pl.BlockSpec((tm, D), lambda q, j, i, chip_ref: (i, 0)), ANY] + [ANY] * nt,
            out_specs=[pl.BlockSpec((tm, tn), lambda q, j, i, chip_ref: (i, (chip_ref[0] ^ q) * 2 + j)), ANY] + [ANY] * nt,
            scratch_shapes=[pltpu.VMEM((2, D, tn), BF16), dma((2,)), dma((8,)), dma((8,)), dma((tail.n_sems,)), dma((tail.n_sems,))]),
        input_output_aliases={2: 1, **{3 + i: 2 + o for i, o in tail.aliases.items()}},
        compiler_params=_cp("arbitrary", "arbitrary", "arbitrary"),
    )(chip, a, w_full, *tail.operands)
    return y, w_out, tail_res


def _hgrn_lower_bound(l_ref):
    l0, l1 = l_ref[0:1, :], l_ref[1:2, :]
    m = jnp.maximum(l0, l1)
    e0, e1 = jnp.exp(l0 - m), jnp.exp(l1 - m)
    return e0 / (e0 + e1)


def _hgrn_chunk_mask(d):
    r = lax.broadcasted_iota(jnp.int32, (HGRN_BLOCK, HGRN_BLOCK), 0)
    c = lax.broadcasted_iota(jnp.int32, (HGRN_BLOCK, HGRN_BLOCK), 1)
    same = (r // HGRN_CHUNK) == (c // HGRN_CHUNK)
    fwd = d == 0
    return same & (((c <= r) & fwd) | ((c >= r) & jnp.logical_not(fwd)))


def _chunk_total(x):
    x3 = x.reshape(HGRN_BLOCK // HGRN_CHUNK, HGRN_CHUNK, x.shape[1])
    return jnp.broadcast_to(jnp.sum(x3, axis=1, keepdims=True), x3.shape).reshape(x.shape)


def _chunk_cumsum(x, suffix):
    pos = lax.broadcasted_iota(jnp.int32, x.shape, 0) % HGRN_CHUNK
    p, s = x, 1
    while s < HGRN_CHUNK:
        p = p + jnp.where(pos >= s, pltpu.roll(p, s, 0), 0.0)
        s *= 2
    return jnp.where(suffix, _chunk_total(x) - p + x, p)


def _block_loop(T, body, init):
    n = T // HGRN_BLOCK
    return lax.fori_loop(0, n, body, init, unroll=2 if n % 2 == 0 else 1)


def _hgrn_gate(f, lb):
    s = jax.nn.sigmoid(f)
    sn = jax.nn.sigmoid(-f)
    fg = lb + (1.0 - lb) * s
    return s, sn, fg, jnp.log(fg), (1.0 - lb) * sn


def _hgrn_specs(T):
    col = lambda base: pl.BlockSpec((T, HEAD_DIM), lambda h, d: (0, base * N_HEADS + h))
    f_spec = pl.BlockSpec((T, HEAD_DIM), lambda h, d: (0, COL_FFW * N_HEADS + N_HEADS * d + h))
    l_spec = pl.BlockSpec((None, 2, HEAD_DIM), lambda h, d: (d, 0, h))
    return col, f_spec, l_spec


def hgrn_fwd(proj, lb_logits, comm=None):
    T = proj.shape[0]
    NC, CPB = T // HGRN_CHUNK, HGRN_BLOCK // HGRN_CHUNK
    col, f_spec, l_spec = _hgrn_specs(T)

    def body(l_ref, q_ref, f_ref, v_ref, o_ref, st_ref, dec_ref, qd_ref):
        d = pl.program_id(1)
        lb = _hgrn_lower_bound(l_ref)
        mask = _hgrn_chunk_mask(d)

        def block(i, carry):
            rows = pl.ds(pl.multiple_of(i * HGRN_BLOCK, HGRN_BLOCK), HGRN_BLOCK)
            _, _, _, lf, k = _hgrn_gate(f_ref[rows, :], lb)
            b = _chunk_cumsum(lf, d == 1)
            bl = _chunk_total(lf)
            qd = (q_ref[rows, :] * Q_SCALE * jnp.exp(b)).astype(BF16)
            kd = (k * jnp.exp(-b)).astype(BF16)
            ke = (k * jnp.exp(bl - b)).astype(BF16)
            vb = v_ref[rows, :].astype(BF16)
            att = jnp.where(mask, _nt(qd, kd), 0.0).astype(BF16)
            o_ref[rows, :] = jnp.where(d == 0, 0.0, o_ref[rows, :]) + _nn(att, vb)
            qd_ref[rows, :] = qd
            dec = jnp.exp(bl)
            for cc in range(CPB):
                sl = slice(cc * HGRN_CHUNK, (cc + 1) * HGRN_CHUNK)
                n = i * CPB + cc
                st_ref[n] = _tn(vb[sl], ke[sl])
                dec_ref[n] = dec[cc * HGRN_CHUNK:cc * HGRN_CHUNK + 8, :]
            return carry

        _block_loop(T, block, 0)

        def scan(t, s):
            n = jnp.where(d == 0, t, NC - 1 - t)
            u = st_ref[n]
            st_ref[n] = s
            return dec_ref[n][0:1, :] * s + u

        lax.fori_loop(0, NC, scan, jnp.zeros((HEAD_DIM, HEAD_DIM), F32))

        def inter(i, carry):
            rows = pl.ds(pl.multiple_of(i * HGRN_BLOCK, HGRN_BLOCK), HGRN_BLOCK)
            qd = qd_ref[rows, :]
            o_ref[rows, :] += jnp.concatenate(
                [_nt(qd[cc * HGRN_CHUNK:(cc + 1) * HGRN_CHUNK], st_ref[i * CPB + cc].astype(BF16)) for cc in range(CPB)], axis=0)
            return carry

        _block_loop(T, inter, 0)

    (o,), landed = _pallas(
        body, name="hgrn_fwd", grid=(N_HEADS, 2), in_specs=[l_spec, col(COL_Q), f_spec, col(COL_V)],
        out_specs=[pl.BlockSpec((T, HEAD_DIM), lambda h, d: (0, h))], out_shape=[jax.ShapeDtypeStruct((T, N_HEADS * HEAD_DIM), F32)],
        scratch=[pltpu.VMEM((NC, HEAD_DIM, HEAD_DIM), F32), pltpu.VMEM((NC, 8, HEAD_DIM), F32), pltpu.VMEM((T, HEAD_DIM), BF16)],
        semantics=("parallel", "arbitrary"), operands=(lb_logits, proj, proj, proj), comm=comm)
    return o if comm is None else (o, landed)


def hgrn_post_fwd(o, proj, g_norm):
    T, W = o.shape
    tm = min(256, T)

    def body(o_ref, og_ref, g_ref, y_ref):
        g = g_ref[...]
        for h in range(N_HEADS):
            sl = slice(h * HEAD_DIM, (h + 1) * HEAD_DIM)
            x = o_ref[:, sl]
            r = lax.rsqrt(jnp.mean(x * x, axis=-1, keepdims=True) + EPS)
            og = og_ref[:, sl]
            y_ref[:, sl] = ((x * r) * g * (og * jax.nn.sigmoid(og))).astype(BF16)

    return pl.pallas_call(
        body, name="hgrn_post_fwd", out_shape=jax.ShapeDtypeStruct((T, W), BF16), grid=(T // tm,),
        in_specs=[pl.BlockSpec((tm, W), lambda i: (i, 0)), pl.BlockSpec((tm, W), lambda i: (i, COL_OG)), _vec(HEAD_DIM)],
        out_specs=pl.BlockSpec((tm, W), lambda i: (i, 0)), compiler_params=_cp("parallel"),
    )(o, proj, g_norm)


def _gelu(x):
    return 0.5 * x * (1.0 + lax.erf(x * (1.0 / math.sqrt(2.0))))


def _gelu_grad(x):
    return 0.5 * (1.0 + lax.erf(x * (1.0 / math.sqrt(2.0)))) + x * jnp.exp(-0.5 * x * x) * (1.0 / math.sqrt(2.0 * math.pi))


def _sgu_mix(u_ref, v_ref, g_ref, ws_ref, bst_ref):
    W = u_ref.shape[1]
    zu, zv = _gelu(u_ref[...]), _gelu(v_ref[...])
    dv = zv - jnp.mean(zv, axis=-1, keepdims=True)
    rstd = lax.rsqrt(jnp.mean(dv * dv, axis=-1, keepdims=True) + EPS)
    dhat = dv * rstd
    vn = (dhat * g_ref[...]).astype(BF16)
    gw = W // SGU_GROUPS
    vm = [_nn(ws_ref[g].astype(BF16), vn[:, g * gw:(g + 1) * gw]) + bst_ref[:, g:g + 1] for g in range(SGU_GROUPS)]
    return zu, rstd, dhat, vn, jnp.concatenate(vm, axis=1)


def sgu_fwd(proj, g_norm, w_spatial, b_spatial_t):
    T = proj.shape[0]
    W = 1024
    n_chunks = T // SGU_CHUNK

    def body(u_ref, v_ref, g_ref, ws_ref, bst_ref, y_ref):
        zu, _, _, _, vm = _sgu_mix(u_ref, v_ref, g_ref, ws_ref, bst_ref)
        y_ref[...] = (zu * vm).astype(BF16)

    blk = lambda cb: pl.BlockSpec((SGU_CHUNK, W), lambda i: (i, cb))
    return pl.pallas_call(
        body, name="sgu_fwd", out_shape=jax.ShapeDtypeStruct((T, W), BF16), grid=(n_chunks,),
        in_specs=[blk(COL_U), blk(COL_ZV), _vec(W), pl.BlockSpec((SGU_GROUPS, SGU_CHUNK, SGU_CHUNK), lambda i: (0, 0, 0)),
                  pl.BlockSpec((SGU_CHUNK, SGU_GROUPS), lambda i: (0, 0))],
        out_specs=blk(0), compiler_params=_cp("parallel"),
    )(proj, proj, g_norm, w_spatial, b_spatial_t)


def merge_matmul(ya_pre, sgu, w_a, w_b, proj):
    T, K = ya_pre.shape
    N = w_a.shape[1]
    tm, tn = min(512, T), 512
    gpb = 1024 // tn

    def body(a_ref, b_ref, wa_ref, wb_ref, ga_ref, gb_ref, ya_ref, yb_ref, m_ref):
        ya = _nn(a_ref[...], wa_ref[...])
        yb = _nn(b_ref[...], wb_ref[...])
        ya_ref[...] = ya.astype(BF16)
        yb_ref[...] = yb.astype(BF16)
        m_ref[...] = (jax.nn.sigmoid(ga_ref[...]) * ya + jax.nn.sigmoid(gb_ref[...]) * yb).astype(BF16)

    lhs = pl.BlockSpec((tm, K), lambda i, j: (i, 0))
    rhs = pl.BlockSpec((K, tn), lambda i, j: (0, j))
    out = pl.BlockSpec((tm, tn), lambda i, j: (i, j))
    return pl.pallas_call(
        body, name="merge_matmul", grid=(T // tm, N // tn),
        out_shape=[jax.ShapeDtypeStruct((T, N), BF16)] * 3,
        in_specs=[lhs, lhs, rhs, rhs, pl.BlockSpec((tm, tn), lambda i, j: (i, COL_GA * gpb + j)),
                  pl.BlockSpec((tm, tn), lambda i, j: (i, COL_GB * gpb + j))],
        out_specs=[out, out, out], compiler_params=_cp("parallel", "parallel"),
    )(ya_pre, sgu, w_a, w_b, proj, proj)


def out_proj(merged, w_o, h0, gt1, g_post, g_pre2, sc2, sh2):
    T, D = h0.shape
    tm = min(256, T)

    def body(m_ref, w_ref, h_ref, gt_ref, gp_ref, g2_ref, sc_ref, sh_ref, mo_ref, h1_ref, a2_ref):
        mo = _nn(m_ref[...], w_ref[...])
        mo_ref[...] = mo
        r = lax.rsqrt(jnp.mean(mo * mo, axis=-1, keepdims=True) + EPS)
        h1 = h_ref[...] + gt_ref[...] * ((mo * r) * gp_ref[...])
        h1_ref[...] = h1
        r2 = lax.rsqrt(jnp.mean(h1 * h1, axis=-1, keepdims=True) + EPS)
        a2_ref[...] = ((h1 * r2) * g2_ref[...] * (1.0 + sc_ref[...]) + sh_ref[...]).astype(BF16)

    row = pl.BlockSpec((tm, D), lambda i: (i, 0))
    return pl.pallas_call(
        body, name="out_proj", grid=(T // tm,),
        out_shape=[jax.ShapeDtypeStruct((T, D), F32), jax.ShapeDtypeStruct((T, D), F32), jax.ShapeDtypeStruct((T, D), BF16)],
        in_specs=[row, pl.BlockSpec((D, D), lambda i: (0, 0)), row] + [_vec(D)] * 5,
        out_specs=[row, row, row], compiler_params=_cp("parallel"),
    )(merged, w_o, h0, gt1, g_post, g_pre2, sc2, sh2)


def loss_bwd(ff, h1, tgt, gt2, g_post):
    T, D = ff.shape
    tm = min(256, T)

    def body(f_ref, h_ref, t_ref, gt_ref, g_ref, dy_ref, dff_ref, loss_ref, dgt_ref, dg_ref):
        @pl.when(pl.program_id(0) == 0)
        def _():
            loss_ref[...] = jnp.zeros_like(loss_ref)
            dgt_ref[...] = jnp.zeros_like(dgt_ref)
            dg_ref[...] = jnp.zeros_like(dg_ref)

        ff = f_ref[...]
        gt, g = gt_ref[...], g_ref[...]
        r = lax.rsqrt(jnp.mean(ff * ff, axis=-1, keepdims=True) + EPS)
        fhat = ff * r
        nf = fhat * g
        err = (h_ref[...] + gt * nf) - t_ref[...]
        loss_ref[...] += jnp.sum(err * err)
        dy = err * (1.0 / D)
        dy_ref[...] = dy
        dgt_ref[...] += _colsum(dy * nf)
        dnf = dy * gt
        dg_ref[...] += _colsum(dnf * fhat)
        u = dnf * g
        dff_ref[...] = (r * (u - fhat * jnp.mean(u * fhat, axis=-1, keepdims=True))).astype(BF16)

    row = pl.BlockSpec((tm, D), lambda i: (i, 0))
    return pl.pallas_call(
        body, name="loss_bwd", grid=(T // tm,),
        out_shape=[jax.ShapeDtypeStruct((T, D), F32), jax.ShapeDtypeStruct((T, D), BF16), jax.ShapeDtypeStruct((8, 128), F32),
                   jax.ShapeDtypeStruct((1, D), F32), jax.ShapeDtypeStruct((1, D), F32)],
        in_specs=[row, row, row, _vec(D), _vec(D)],
        out_specs=[row, row, pl.BlockSpec((8, 128), lambda i: (0, 0)), _vec(D), _vec(D)],
        compiler_params=_cp("arbitrary"),
    )(ff, h1, tgt, gt2, g_post)


def ff2_bwd(dff, w_ff2, f1):
    T, D = dff.shape
    K = w_ff2.shape[0]
    tm, tn = min(512, T), 2048

    def body(a_ref, w_ref, f_ref, o_ref):
        o_ref[...] = (_nt(a_ref[...], w_ref[...]) * (2.0 * jnp.maximum(f_ref[...].astype(F32), 0.0))).astype(BF16)

    return pl.pallas_call(
        body, name="ff2_bwd", out_shape=jax.ShapeDtypeStruct((T, K), BF16), grid=(K // tn, T // tm),
        in_specs=[pl.BlockSpec((tm, D), lambda j, i: (i, 0)), pl.BlockSpec((tn, D), lambda j, i: (j, 0)),
                  pl.BlockSpec((tm, tn), lambda j, i: (i, j))],
        out_specs=pl.BlockSpec((tm, tn), lambda j, i: (i, j)), compiler_params=_cp("parallel", "parallel"),
    )(dff, w_ff2, f1)


def ffn_norm_bwd(dy, da2, h1, mo, g_pre2, sc2, gt1, g_post, comm):
    T, D = dy.shape
    tm = min(256, T)

    def body(dy_ref, da_ref, h_ref, mo_ref, g2_ref, sc_ref, gt_ref, gp_ref, dh_ref, dmo_ref, s_sh, s_sc, s_g2, s_gt, s_gp):
        @pl.when(pl.program_id(0) == 0)
        def _():
            for s in (s_sh, s_sc, s_g2, s_gt, s_gp):
                s[...] = jnp.zeros_like(s)

        h1, da = h_ref[...], da_ref[...]
        g2, sc = g2_ref[...], sc_ref[...]
        r2 = lax.rsqrt(jnp.mean(h1 * h1, axis=-1, keepdims=True) + EPS)
        n2 = h1 * r2
        s_sh[...] += _colsum(da)
        s_sc[...] += _colsum(da * (n2 * g2))
        s_g2[...] += _colsum(da * (1.0 + sc) * n2)
        dn2 = da * g2 * (1.0 + sc)
        dh1 = dy_ref[...] + r2 * (dn2 - n2 * jnp.mean(dn2 * n2, axis=-1, keepdims=True))
        dh_ref[...] = dh1
        mo = mo_ref[...]
        gt, gp = gt_ref[...], gp_ref[...]
        r = lax.rsqrt(jnp.mean(mo * mo, axis=-1, keepdims=True) + EPS)
        mhat = mo * r
        s_gt[...] += _colsum(dh1 * (mhat * gp))
        dnm = dh1 * gt
        s_gp[...] += _colsum(dnm * mhat)
        u = dnm * gp
        dmo_ref[...] = (r * (u - mhat * jnp.mean(u * mhat, axis=-1, keepdims=True))).astype(BF16)

    row = pl.BlockSpec((tm, D), lambda i: (i, 0))
    vec_out = jax.ShapeDtypeStruct((1, D), F32)
    return _pallas(
        body, name="ffn_norm_bwd", grid=(T // tm,),
        out_shape=[jax.ShapeDtypeStruct((T, D), F32), jax.ShapeDtypeStruct((T, D), BF16)] + [vec_out] * 5,
        in_specs=[row, row, row, row] + [_vec(D)] * 4, out_specs=[row, row] + [_vec(D)] * 5,
        scratch=[], semantics=("arbitrary",), operands=(dy, da2, h1, mo, g_pre2, sc2, gt1, g_post), comm=comm)


def out_proj_bwd(dmo, w_o, y_a, y_b, proj):
    T, D = dmo.shape
    tm, tn = min(512, T), 512
    gpb = 1024 // tn

    def body(a_ref, w_ref, ya_ref, yb_ref, ga_ref, gb_ref, dya_ref, dyb_ref, dga_ref, dgb_ref):
        dm = _nt(a_ref[...], w_ref[...])
        sa, sb = jax.nn.sigmoid(ga_ref[...]), jax.nn.sigmoid(gb_ref[...])
        dya_ref[...] = (dm * sa).astype(BF16)
        dyb_ref[...] = (dm * sb).astype(BF16)
        dga_ref[...] = (dm * ya_ref[...].astype(F32) * sa * (1.0 - sa)).astype(BF16)
        dgb_ref[...] = (dm * yb_ref[...].astype(F32) * sb * (1.0 - sb)).astype(BF16)

    out = pl.BlockSpec((tm, tn), lambda i, j: (i, j))
    return pl.pallas_call(
        body, name="out_proj_bwd", grid=(T // tm, D // tn), out_shape=[jax.ShapeDtypeStruct((T, D), BF16)] * 4,
        in_specs=[pl.BlockSpec((tm, D), lambda i, j: (i, 0)), pl.BlockSpec((tn, D), lambda i, j: (j, 0)), out, out,
                  pl.BlockSpec((tm, tn), lambda i, j: (i, COL_GA * gpb + j)), pl.BlockSpec((tm, tn), lambda i, j: (i, COL_GB * gpb + j))],
        out_specs=[out] * 4, compiler_params=_cp("parallel", "parallel"),
    )(dmo, w_o, y_a, y_b, proj, proj)


def sgu_bwd(proj, dsgu, g_norm, w_spatial, b_spatial_t):
    T = proj.shape[0]
    W = 1024
    gw = W // SGU_GROUPS

    def body(u_ref, v_ref, ds_ref, g_ref, ws_ref, bst_ref, dz_ref, dw_ref, db_ref, dg_ref):
        @pl.when(pl.program_id(0) == 0)
        def _():
            dw_ref[...] = jnp.zeros_like(dw_ref)
            db_ref[...] = jnp.zeros_like(db_ref)
            dg_ref[...] = jnp.zeros_like(dg_ref)

        zu, rstd, dhat, vn, vm = _sgu_mix(u_ref, v_ref, g_ref, ws_ref, bst_ref)
        ds = ds_ref[...]
        du = ds * vm
        dvm = ds * zu
        dvm_b = dvm.astype(BF16)
        ones = jnp.ones((8, gw), F32)
        dvn = []
        for g in range(SGU_GROUPS):
            sl = slice(g * gw, (g + 1) * gw)
            dw_ref[g] += _nt(dvm_b[:, sl], vn[:, sl])
            db_ref[g] += lax.dot_general(ones, dvm[:, sl], (((1,), (1,)), ((), ())), precision=HI, preferred_element_type=F32)
            dvn.append(_tn(ws_ref[g].astype(BF16), dvm_b[:, sl]))
        dvn = jnp.concatenate(dvn, axis=1)
        dg_ref[...] += _colsum(dvn * dhat)
        ddh = dvn * g_ref[...]
        dzv = rstd * (ddh - jnp.mean(ddh, axis=-1, keepdims=True) - dhat * jnp.mean(ddh * dhat, axis=-1, keepdims=True))
        dz_ref[:, 0:W] = (du * _gelu_grad(u_ref[...])).astype(BF16)
        dz_ref[:, W:2 * W] = (dzv * _gelu_grad(v_ref[...])).astype(BF16)

    blk = lambda cb: pl.BlockSpec((SGU_CHUNK, W), lambda i: (i, cb))
    full3 = lambda a, b, c: pl.BlockSpec((a, b, c), lambda i: (0, 0, 0))
    return pl.pallas_call(
        body, name="sgu_bwd", grid=(T // SGU_CHUNK,),
        out_shape=[jax.ShapeDtypeStruct((T, 2 * W), BF16), jax.ShapeDtypeStruct((SGU_GROUPS, SGU_CHUNK, SGU_CHUNK), F32),
                   jax.ShapeDtypeStruct((SGU_GROUPS, 8, SGU_CHUNK), F32), jax.ShapeDtypeStruct((1, W), F32)],
        in_specs=[blk(COL_U), blk(COL_ZV), blk(0), _vec(W), full3(SGU_GROUPS, SGU_CHUNK, SGU_CHUNK),
                  pl.BlockSpec((SGU_CHUNK, SGU_GROUPS), lambda i: (0, 0))],
        out_specs=[pl.BlockSpec((SGU_CHUNK, 2 * W), lambda i: (i, 0)), full3(SGU_GROUPS, SGU_CHUNK, SGU_CHUNK),
                   full3(SGU_GROUPS, 8, SGU_CHUNK), _vec(W)],
        compiler_params=_cp("arbitrary"),
    )(proj, proj, dsgu, g_norm, w_spatial, b_spatial_t)


def hgrn_post_bwd(dya, o, proj, g_norm, comm):
    T, W = o.shape
    tm = min(256, T)

    def body(dy_ref, o_ref, og_ref, g_ref, do_ref, dog_ref, dg_ref):
        @pl.when(pl.program_id(0) == 0)
        def _():
            dg_ref[...] = jnp.zeros_like(dg_ref)

        g = g_ref[...]
        dg = jnp.zeros((1, HEAD_DIM), F32)
        for h in range(N_HEADS):
            sl = slice(h * HEAD_DIM, (h + 1) * HEAD_DIM)
            x, og, dy = o_ref[:, sl], og_ref[:, sl], dy_ref[:, sl]
            r = lax.rsqrt(jnp.mean(x * x, axis=-1, keepdims=True) + EPS)
            xhat = x * r
            s = jax.nn.sigmoid(og)
            don = dy * (og * s)
            dog_ref[:, sl] = (dy * (xhat * g) * (s * (1.0 + og * (1.0 - s)))).astype(BF16)
            dg += _colsum(don * xhat)
            u = don * g
            do_ref[:, sl] = r * (u - xhat * jnp.mean(u * xhat, axis=-1, keepdims=True))
        dg_ref[...] += dg

    row = pl.BlockSpec((tm, W), lambda i: (i, 0))
    return _pallas(
        body, name="hgrn_post_bwd", grid=(T // tm,),
        out_shape=[jax.ShapeDtypeStruct((T, W), F32), jax.ShapeDtypeStruct((T, W), BF16), jax.ShapeDtypeStruct((1, HEAD_DIM), F32)],
        in_specs=[row, row, pl.BlockSpec((tm, W), lambda i: (i, COL_OG)), _vec(HEAD_DIM)],
        out_specs=[row, row, _vec(HEAD_DIM)], scratch=[], semantics=("arbitrary",), operands=(dya, o, proj, g_norm), comm=comm)


def hgrn_bwd(proj, do, lb_logits, comm=None):
    T = proj.shape[0]
    NC, CPB = T // HGRN_CHUNK, HGRN_BLOCK // HGRN_CHUNK
    W = N_HEADS * HEAD_DIM
    col, f_spec, l_spec = _hgrn_specs(T)

    def body(l_ref, q_ref, f_ref, v_ref, do_ref, dq_ref, dv_ref, dlg_ref, dlb_ref, st_ref, dst_ref, dec_ref, ddec_ref, dqa_ref, dva_ref):
        d = pl.program_id(1)
        lb = _hgrn_lower_bound(l_ref)
        oml = 1.0 - lb
        mask = _hgrn_chunk_mask(d)

        def values(rows):
            s, sn, fg, lf, k = _hgrn_gate(f_ref[rows, :], lb)
            b = _chunk_cumsum(lf, d == 1)
            bl = _chunk_total(lf)
            eb, enb, ee = jnp.exp(b), jnp.exp(-b), jnp.exp(bl - b)
            qd = q_ref[rows, :] * Q_SCALE * eb
            return s, sn, fg, k, bl, eb, enb, ee, qd, k * enb, k * ee

        def block1(i, carry):
            rows = pl.ds(pl.multiple_of(i * HGRN_BLOCK, HGRN_BLOCK), HGRN_BLOCK)
            _, _, _, _, bl, _, _, _, qd, _, ke = values(rows)
            qd, ke = qd.astype(BF16), ke.astype(BF16)
            vb, dob = v_ref[rows, :].astype(BF16), do_ref[rows, :].astype(BF16)
            dec = jnp.exp(bl)
            for cc in range(CPB):
                sl = slice(cc * HGRN_CHUNK, (cc + 1) * HGRN_CHUNK)
                n = i * CPB + cc
                st_ref[n] = _tn(vb[sl], ke[sl])
                dst_ref[n] = _tn(dob[sl], qd[sl])
                dec_ref[n] = dec[cc * HGRN_CHUNK:cc * HGRN_CHUNK + 8, :]
            return carry

        _block_loop(T, block1, 0)

        def scan(t, s):
            n = jnp.where(d == 0, t, NC - 1 - t)
            u = st_ref[n]
            st_ref[n] = s
            return dec_ref[n][0:1, :] * s + u

        lax.fori_loop(0, NC, scan, jnp.zeros((HEAD_DIM, HEAD_DIM), F32))

        def rscan(t, ds):
            n = jnp.where(d == 0, NC - 1 - t, t)
            w = dst_ref[n]
            dst_ref[n] = ds
            ddec_ref[n] = jnp.broadcast_to(_colsum(ds * st_ref[n]), (8, HEAD_DIM))
            return dec_ref[n][0:1, :] * ds + w

        lax.fori_loop(0, NC, rscan, jnp.zeros((HEAD_DIM, HEAD_DIM), F32))

        def block3(i, dlb):
            rows = pl.ds(pl.multiple_of(i * HGRN_BLOCK, HGRN_BLOCK), HGRN_BLOCK)
            s, sn, fg, k, bl, eb, enb, ee, qd, kd, ke = values(rows)
            qdb, kdb, keb = qd.astype(BF16), kd.astype(BF16), ke.astype(BF16)
            vb, dob = v_ref[rows, :].astype(BF16), do_ref[rows, :].astype(BF16)
            att = jnp.where(mask, _nt(qdb, kdb), 0.0).astype(BF16)
            datt = jnp.where(mask, _nt(dob, vb), 0.0).astype(BF16)
            dv = _tn(att, dob)
            dqd = _nn(datt, kdb)
            dkd = _tn(datt, qdb)
            dv_i, dqd_i, dke, ddl = [], [], [], []
            for cc in range(CPB):
                sl = slice(cc * HGRN_CHUNK, (cc + 1) * HGRN_CHUNK)
                n = i * CPB + cc
                st_b, dst_b = st_ref[n].astype(BF16), dst_ref[n].astype(BF16)
                dv_i.append(_nt(keb[sl], dst_b))
                dqd_i.append(_nn(dob[sl], st_b))
                dke.append(_nn(vb[sl], dst_b))
                ddl.append(jnp.broadcast_to(ddec_ref[n][0:1, :] * dec_ref[n][0:1, :], (HGRN_CHUNK, HEAD_DIM)))
            dv = dv + jnp.concatenate(dv_i, axis=0)
            dqd = dqd + jnp.concatenate(dqd_i, axis=0)
            dke = jnp.concatenate(dke, axis=0)
            dq = dqd * eb * Q_SCALE
            dk = dkd * enb + dke * ee
            t_end = dke * ke
            db = dqd * qd - dkd * kd - t_end
            dlf = _chunk_cumsum(db, d == 0) + _chunk_total(t_end) + jnp.concatenate(ddl, axis=0)
            e = dlf / fg - dk
            dlg_ref[rows, :] = (oml * e * s * sn).astype(BF16)

            dq = jnp.where(d == 0, 0.0, dqa_ref[rows, :]) + dq
            dv = jnp.where(d == 0, 0.0, dva_ref[rows, :]) + dv
            dqa_ref[rows, :] = dq
            dva_ref[rows, :] = dv
            dq_ref[rows, :] = dq.astype(BF16)
            dv_ref[rows, :] = dv.astype(BF16)

            return dlb + _colsum(e * sn)

        dlb_ref[...] = _block_loop(T, block3, jnp.zeros((1, HEAD_DIM), F32))

    head = pl.BlockSpec((T, HEAD_DIM), lambda h, d: (0, h))
    big = pltpu.VMEM((NC, HEAD_DIM, HEAD_DIM), F32)
    small = pltpu.VMEM((NC, 8, HEAD_DIM), F32)
    acc = pltpu.VMEM((T, HEAD_DIM), F32)
    outs, landed = _pallas(
        body, name="hgrn_bwd", grid=(N_HEADS, 2),
        out_shape=[jax.ShapeDtypeStruct((T, W), BF16), jax.ShapeDtypeStruct((T, W), BF16), jax.ShapeDtypeStruct((T, 2 * W), BF16),
                   jax.ShapeDtypeStruct((2, 1, W), F32)],
        in_specs=[l_spec, col(COL_Q), f_spec, col(COL_V), head],
        out_specs=[head, head, pl.BlockSpec((T, HEAD_DIM), lambda h, d: (0, N_HEADS * d + h)),
                   pl.BlockSpec((None, 1, HEAD_DIM), lambda h, d: (d, 0, h))],
        scratch=[big, big, small, small, acc, acc], semantics=("parallel", "arbitrary"), operands=(lb_logits, proj, proj, proj, do), comm=comm)
    return outs if comm is None else (outs, landed)


def mix_norm_bwd(da1, h0, dh1, g_pre, sc1):
    T, D = h0.shape
    tm = min(256, T)

    def body(da_ref, h_ref, dh_ref, g_ref, sc_ref, gx_ref, s_sh, s_sc, s_g):
        @pl.when(pl.program_id(0) == 0)
        def _():
            for s in (s_sh, s_sc, s_g):
                s[...] = jnp.zeros_like(s)

        h, da = h_ref[...], da_ref[...]
        g, sc = g_ref[...], sc_ref[...]
        r = lax.rsqrt(jnp.mean(h * h, axis=-1, keepdims=True) + EPS)
        n = h * r
        s_sh[...] += _colsum(da)
        s_sc[...] += _colsum(da * (n * g))
        s_g[...] += _colsum(da * (1.0 + sc) * n)
        dn = da * g * (1.0 + sc)
        gx_ref[...] = dh_ref[...] + r * (dn - n * jnp.mean(dn * n, axis=-1, keepdims=True))

    row = pl.BlockSpec((tm, D), lambda i: (i, 0))
    return pl.pallas_call(
        body, name="mix_norm_bwd", grid=(T // tm,),
        out_shape=[jax.ShapeDtypeStruct((T, D), F32)] + [jax.ShapeDtypeStruct((1, D), F32)] * 3,
        in_specs=[row, row, row, _vec(D), _vec(D)], out_specs=[row] + [_vec(D)] * 3, compiler_params=_cp("arbitrary"),
    )(da1, h0, dh1, g_pre, sc1)


def adamw(w, g, m, v, name):
    R, C = w.shape
    tr = R if R * C * 4 <= (1 << 21) else max(8, ((1 << 21) // (C * 4)) // 8 * 8)
    while R % tr:
        tr -= 8

    def body(w_ref, g_ref, m_ref, v_ref, d_ref, m2_ref, v2_ref):
        d_ref[...], m2_ref[...], v2_ref[...] = _adamw(w_ref[...], g_ref[...], m_ref[...], v_ref[...])

    row = pl.BlockSpec((tr, C), lambda i: (i, 0))
    return pl.pallas_call(
        body, name=name, grid=(R // tr,), out_shape=[jax.ShapeDtypeStruct((R, C), F32)] * 3,
        in_specs=[row] * 4, out_specs=[row] * 3, compiler_params=_cp("parallel"),
    )(w, g, m, v)


def wada_update(c_all, dmod, w, m, v):
    D, N = w.shape
    tm, tn = 512, 1024

    def body(c_ref, dm_ref, w_ref, m_ref, v_ref, g_ref, d_ref, m2_ref, v2_ref):
        c = c_ref[...]
        g = lax.dot_general(c * jax.nn.sigmoid(c), dm_ref[...], (((0,), (0,)), ((), ())), precision=HI, preferred_element_type=F32)
        g_ref[...] = g
        d_ref[...], m2_ref[...], v2_ref[...] = _adamw(w_ref[...], g, m_ref[...], v_ref[...])

    blk = pl.BlockSpec((tm, tn), lambda i, j: (i, j))
    return pl.pallas_call(
        body, name="wada_update", grid=(D // tm, N // tn), out_shape=[jax.ShapeDtypeStruct((D, N), F32)] * 4,
        in_specs=[pl.BlockSpec((8, tm), lambda i, j: (0, i)), pl.BlockSpec((8, tn), lambda i, j: (0, j)), blk, blk, blk],
        out_specs=[blk] * 4, compiler_params=_cp("parallel", "parallel"),
    )(c_all, dmod, w, m, v)


def sum_devices(gathered, name):
    n, R, C = gathered.shape

    def body(g_ref, o_ref):
        s = g_ref[0]
        for i in range(1, n):
            s = s + g_ref[i]
        o_ref[...] = s

    return pl.pallas_call(body, name=name, out_shape=jax.ShapeDtypeStruct((R, C), F32), compiler_params=_cp())(gathered)


def lb_logits_grad(dlb, lb_logits):
    def body(d_ref, l_ref, o_ref):
        for d in range(2):
            l0, l1 = l_ref[d, 0:1, :], l_ref[d, 1:2, :]
            m = jnp.maximum(l0, l1)
            e0, e1 = jnp.exp(l0 - m), jnp.exp(l1 - m)
            p0, p1 = e0 / (e0 + e1), e1 / (e0 + e1)
            g = d_ref[d:d + 1, :]
            o_ref[d, 0:1, :] = p0 * (g - p0 * g)
            o_ref[d, 1:2, :] = -p1 * (p0 * g)

    return pl.pallas_call(body, name="lb_logits_grad", out_shape=jax.ShapeDtypeStruct(lb_logits.shape, F32), compiler_params=_cp())(dlb, lb_logits)


def add_halves(g, landed, core):
    nj, _, r, cc = g.shape
    tr = min(256, r)

    def body(core_ref, g_ref, l_ref, o_ref):
        o_ref[...] = (g_ref[...].astype(F32) + l_ref[...].astype(F32)).astype(BF16)

    return pl.pallas_call(
        body, name="add_halves_%dx%d" % (r, cc), out_shape=jax.ShapeDtypeStruct((nj, r, cc), BF16),
        grid_spec=pltpu.PrefetchScalarGridSpec(
            num_scalar_prefetch=1, grid=(nj, r // tr),
            in_specs=[pl.BlockSpec((None, None, tr, cc), lambda j, i, core_ref: (j, core_ref[0], i, 0)),
                      pl.BlockSpec((None, None, tr, cc), lambda j, i, core_ref: (j, 0, i, 0))],
            out_specs=pl.BlockSpec((None, tr, cc), lambda j, i, core_ref: (j, i, 0))),
        compiler_params=_cp("parallel", "parallel"),
    )(core, g, landed)


def sum_chips(parts, landed, chip):
    nj, r, cc = parts.shape
    tr = min(256, r)

    def body(chip_ref, p_ref, l_ref, o_ref):
        mine = p_ref[...].astype(F32)
        s = None
        for j in range(nj):
            t = jnp.where(chip_ref[0] == j, mine, l_ref[j].astype(F32))
            s = t if s is None else s + t
        o_ref[...] = s

    return pl.pallas_call(
        body, name="sum_chips_%dx%d" % (r, cc), out_shape=jax.ShapeDtypeStruct((r, cc), F32),
        grid_spec=pltpu.PrefetchScalarGridSpec(
            num_scalar_prefetch=1, grid=(r // tr,),
            in_specs=[pl.BlockSpec((None, tr, cc), lambda i, chip_ref: (chip_ref[0], i, 0)),
                      pl.BlockSpec((nj, tr, cc), lambda i, chip_ref: (0, i, 0))],
            out_specs=pl.BlockSpec((tr, cc), lambda i, chip_ref: (i, 0))),
        compiler_params=_cp("parallel"),
    )(chip, parts, landed)


def adamw_halves(w, own, other, m, v, core, name):
    r, cc = own.shape
    tr = min(128, r)
    nb = r // tr

    def body(core_ref, w_ref, a_ref, b_ref, m_ref, v_ref, g_ref, d_ref, m2_ref, v2_ref):
        g = jnp.where(pl.program_id(0) == core_ref[0], a_ref[...], b_ref[...])
        g_ref[...] = g
        d_ref[...], m2_ref[...], v2_ref[...] = _adamw(w_ref[...], g, m_ref[...], v_ref[...])

    full = pl.BlockSpec((tr, cc), lambda h, i, core_ref: (h * nb + i, 0))
    mine = pl.BlockSpec((tr, cc), lambda h, i, core_ref: (jnp.where(h == core_ref[0], i, 0), 0))
    theirs = pl.BlockSpec((tr, cc), lambda h, i, core_ref: (jnp.where(h == core_ref[0], 0, i), 0))
    return pl.pallas_call(
        body, name=name, out_shape=[jax.ShapeDtypeStruct((2 * r, cc), F32)] * 4,
        grid_spec=pltpu.PrefetchScalarGridSpec(
            num_scalar_prefetch=1, grid=(2, nb), in_specs=[full, mine, theirs, full, full], out_specs=[full] * 4),
        compiler_params=_cp("arbitrary", "arbitrary"),
    )(core, w, own, other, m, v)


def _place():
    mx, my, mc = lax.axis_index("x"), lax.axis_index("y"), lax.axis_index("c")
    chips = [(1 - mx, my), (mx, 1 - my), (1 - mx, 1 - my)]
    return mx, my, mc, chips


def all_gather_small(x, name):
    R, C = x.shape

    def body(x_ref, out_ref, send_sems, recv_sems, local_sem):
        mx, my, mc, _ = _place()
        me = 4 * mx + 2 * my + mc
        mine = pltpu.make_async_copy(x_ref, out_ref.at[me], local_sem)
        mine.start()

        def peer(k):
            px = 1 - mx if k & 4 else mx
            py = 1 - my if k & 2 else my
            pc = 1 - mc if k & 1 else mc
            return px, py, pc

        def copy(k, src, slot):
            return pltpu.make_async_remote_copy(src_ref=src, dst_ref=out_ref.at[slot], send_sem=send_sems.at[k - 1],
                                                recv_sem=recv_sems.at[k - 1], device_id=peer(k), device_id_type=MESH)

        sends = [copy(k, x_ref, me) for k in range(1, 8)]
        for cp in sends:
            cp.start()
        for k in range(1, 8):
            px, py, pc = peer(k)
            slot = 4 * px + 2 * py + pc
            copy(k, out_ref.at[slot], slot).wait_recv()
        for cp in sends:
            cp.wait_send()
        mine.wait()

    return pl.pallas_call(
        body, name=name, out_shape=jax.ShapeDtypeStruct((8, R, C), F32),
        in_specs=[pl.BlockSpec(memory_space=pltpu.VMEM)], out_specs=pl.BlockSpec(memory_space=pltpu.VMEM),
        scratch_shapes=[pltpu.SemaphoreType.DMA((7,)), pltpu.SemaphoreType.DMA((7,)), pltpu.SemaphoreType.DMA],
        compiler_params=_cp(),
    )(x)


def gather8_comm(x):
    def copies(x_ref, out_ref, send_sems, recv_sems):
        mx, my, mc, _ = _place()
        me = 4 * mx + 2 * my + mc

        def peer(k):
            return (1 - mx if k & 4 else mx, 1 - my if k & 2 else my, 1 - mc if k & 1 else mc)

        def copy(k, src, slot):
            return pltpu.make_async_remote_copy(src_ref=src, dst_ref=out_ref.at[slot], send_sem=send_sems.at[k - 1],
                                                recv_sem=recv_sems.at[k - 1], device_id=peer(k), device_id_type=MESH)

        sends = [copy(k, x_ref, me) for k in range(1, 8)]
        arrivals = []
        for k in range(1, 8):
            px, py, pc = peer(k)
            slot = 4 * px + 2 * py + pc
            arrivals.append(copy(k, out_ref.at[slot], slot))
        return sends, arrivals, pltpu.make_async_copy(x_ref, out_ref.at[me], send_sems.at[7])

    def start(cin, cout, send_sems, recv_sems):
        sends, _, mine = copies(cin[0], cout[0], send_sems, recv_sems)
        mine.start()
        for cp in sends:
            cp.start()

    def finish(cin, cout, send_sems, recv_sems):
        sends, arrivals, mine = copies(cin[0], cout[0], send_sems, recv_sems)
        for cp in arrivals:
            cp.wait_recv()
        for cp in sends:
            cp.wait_send()
        mine.wait()

    return _Comm([x], [jax.ShapeDtypeStruct((8,) + x.shape, F32)], {}, 8, start, finish)


def _join(a, b):
    na_in, na_out = len(a.operands), len(a.out_shape)

    def split(fn_a, fn_b):
        def both(cin, cout, send_sems, recv_sems):
            fn_a(cin[:na_in], cout[:na_out], send_sems.at[pl.ds(0, a.n_sems)], recv_sems.at[pl.ds(0, a.n_sems)])
            fn_b(cin[na_in:], cout[na_out:], send_sems.at[pl.ds(a.n_sems, b.n_sems)], recv_sems.at[pl.ds(a.n_sems, b.n_sems)])
        return both

    aliases = dict(a.aliases)
    aliases.update({na_in + i: na_out + o for i, o in b.aliases.items()})
    return _Comm(a.operands + b.operands, a.out_shape + b.out_shape, aliases, a.n_sems + b.n_sems, split(a.start, b.start), split(a.finish, b.finish))


def _region(ref, kind, j, half, r, cc):
    nr = r if half is None else r // 2
    off = 0 if half is None else half * nr
    if kind == "col":
        return ref.at[pl.ds(off, nr), pl.ds(pl.multiple_of(j * cc, 128), cc)]
    return ref.at[pl.ds(pl.multiple_of(j * r + off, 16), nr), :]


def comm_call(comm, name):
    ni, no = len(comm.operands), len(comm.out_shape)

    def body(*refs):
        comm.start(refs[:ni], refs[ni:ni + no], *refs[ni + no:])
        comm.finish(refs[:ni], refs[ni:ni + no], *refs[ni + no:])

    return pl.pallas_call(
        body, name=name, out_shape=comm.out_shape, in_specs=[ANY] * ni, out_specs=[ANY] * no, input_output_aliases=comm.aliases,
        scratch_shapes=[pltpu.SemaphoreType.DMA((comm.n_sems,)), pltpu.SemaphoreType.DMA((comm.n_sems,))], compiler_params=_cp(),
    )(*comm.operands)


def gather_comm(fulls, kinds, dims):
    n = len(fulls)

    def copies(f_refs, send_sems, recv_sems):
        mx, my, mc, chips = _place()
        jme = 2 * mx + my

        def landed(w, k, half):
            px, py = chips[k]
            return _region(f_refs[w], kinds[w], 2 * px + py, half, *dims[w])

        def over_ici(w, k, reg):
            px, py = chips[k]
            return pltpu.make_async_remote_copy(src_ref=reg, dst_ref=reg, send_sem=send_sems.at[6 * w + k], recv_sem=recv_sems.at[6 * w + k],
                                                device_id=(px, py, mc), device_id_type=MESH)

        def over_d2d(w, k, half):
            reg = landed(w, k, half)
            return pltpu.make_async_remote_copy(src_ref=reg, dst_ref=reg, send_sem=send_sems.at[6 * w + 3 + k],
                                                recv_sem=recv_sems.at[6 * w + 3 + k], device_id=(mx, my, 1 - mc), device_id_type=MESH)

        sends = [over_ici(w, k, _region(f_refs[w], kinds[w], jme, mc, *dims[w])) for w in range(n) for k in range(3)]
        return mc, landed, over_ici, over_d2d, sends

    def start(cin, f_refs, send_sems, recv_sems):
        for cp in copies(f_refs, send_sems, recv_sems)[4]:
            cp.start()

    def finish(cin, f_refs, send_sems, recv_sems):
        mc, landed, over_ici, over_d2d, sends = copies(f_refs, send_sems, recv_sems)
        passed = []
        for w in range(n):
            for k in range(3):
                over_ici(w, k, landed(w, k, mc)).wait_recv()
                cp = over_d2d(w, k, mc)
                cp.start()
                passed.append(cp)
        for w in range(n):
            for k in range(3):
                over_d2d(w, k, 1 - mc).wait_recv()
        for cp in sends + passed:
            cp.wait_send()

    return _Comm(fulls, [jax.ShapeDtypeStruct(f.shape, BF16) for f in fulls], {w: w for w in range(n)}, 6 * n, start, finish)


def exchange_comm(grads):
    n = len(grads)

    def copies(g_refs, l_refs, send_sems, recv_sems):
        mx, my, mc, _ = _place()
        return [pltpu.make_async_remote_copy(src_ref=g_refs[w].at[:, pl.ds(1 - mc, 1)], dst_ref=l_refs[w], send_sem=send_sems.at[w],
                                             recv_sem=recv_sems.at[w], device_id=(mx, my, 1 - mc), device_id_type=MESH) for w in range(n)]

    def start(*refs):
        for cp in copies(*refs):
            cp.start()

    def finish(*refs):
        for cp in copies(*refs):
            cp.wait()

    return _Comm(grads, [jax.ShapeDtypeStruct((g.shape[0], 1) + g.shape[2:], BF16) for g in grads], {}, n, start, finish)


def exchange_halves(grads, name):
    return comm_call(exchange_comm(grads), name)


def scatter_comm(parts):
    n = len(parts)

    def sends(p_refs, l_refs, send_sems, recv_sems):
        mx, my, mc, chips = _place()
        return [pltpu.make_async_remote_copy(src_ref=p_refs[w].at[2 * px + py], dst_ref=l_refs[w].at[2 * mx + my],
                                             send_sem=send_sems.at[3 * w + k], recv_sem=recv_sems.at[3 * w + k],
                                             device_id=(px, py, mc), device_id_type=MESH) for w in range(n) for k, (px, py) in enumerate(chips)]

    def start(p_refs, l_refs, send_sems, recv_sems):
        for cp in sends(p_refs, l_refs, send_sems, recv_sems):
            cp.start()

    def finish(p_refs, l_refs, send_sems, recv_sems):
        mx, my, mc, chips = _place()
        for w in range(n):
            for k, (px, py) in enumerate(chips):
                slot = l_refs[w].at[2 * px + py]
                pltpu.make_async_remote_copy(src_ref=slot, dst_ref=slot, send_sem=send_sems.at[3 * w + k], recv_sem=recv_sems.at[3 * w + k],
                                             device_id=(px, py, mc), device_id_type=MESH).wait_recv()
        for cp in sends(p_refs, l_refs, send_sems, recv_sems):
            cp.wait_send()

    return _Comm(parts, [jax.ShapeDtypeStruct(p.shape, BF16) for p in parts], {}, 3 * n, start, finish)


def share_comm(sums):
    n = len(sums)

    def copies(q_refs, o_refs, send_sems, recv_sems):
        mx, my, mc, _ = _place()
        return [pltpu.make_async_remote_copy(src_ref=q_refs[w], dst_ref=o_refs[w], send_sem=send_sems.at[w], recv_sem=recv_sems.at[w],
                                             device_id=(mx, my, 1 - mc), device_id_type=MESH) for w in range(n)]

    def start(*refs):
        for cp in copies(*refs):
            cp.start()

    def finish(*refs):
        for cp in copies(*refs):
            cp.wait()

    return _Comm(sums, [jax.ShapeDtypeStruct(q.shape, F32) for q in sums], {}, n, start, finish)


def _pack(arrays):
    flat = jnp.concatenate([a.reshape(-1) for a in arrays])
    rows = -(-flat.shape[0] // 1024) * 8
    return jnp.pad(flat, (0, rows * 128 - flat.shape[0])).reshape(rows, 128)


def _unpack(packed, shapes):
    flat, out, off = packed.reshape(-1), [], 0
    for s in shapes:
        n = math.prod(s)
        out.append(flat[off:off + n].reshape(s))
        off += n
    return out


def kernel(x, c, w_ada, b_ada, g_pre_mix, g_post_mix, g_pre_ffn, g_post_ffn, w_in, lb_logits, g_hgrn_norm, w_a_out, g_sgu_norm, w_spatial, b_spatial, w_b_out, w_o, w_ff1, w_ff2, loss_target, m_w_ada, m_b_ada, m_g_pre_mix, m_g_post_mix, m_g_pre_ffn, m_g_post_ffn, m_w_in, m_lb_logits, m_g_hgrn_norm, m_w_a_out, m_g_sgu_norm, m_w_spatial, m_b_spatial, m_w_b_out, m_w_o, m_w_ff1, m_w_ff2, v_w_ada, v_b_ada, v_g_pre_mix, v_g_post_mix, v_g_pre_ffn, v_g_post_ffn, v_w_in, v_lb_logits, v_g_hgrn_norm, v_w_a_out, v_g_sgu_norm, v_w_spatial, v_b_spatial, v_w_b_out, v_w_o, v_w_ff1, v_w_ff2):
    mx, my, mc = lax.axis_index("x"), lax.axis_index("y"), lax.axis_index("c")
    chip, me = 2 * mx + my, 4 * mx + 2 * my + mc
    D = D_MODEL
    h0, tgt = x[0], loss_target[0]
    n_ada = w_ada.shape[2]
    n_lb = lb_logits.shape[2]

    got = all_gather_small(_pack([c, lb_logits]), "gather_inputs")
    c_all = got[:, :D // 128, :].reshape(8, D)
    lb_full = got[0::2, D // 128:D // 128 + 4 * n_lb // 128, :].reshape(4, 2, 2, n_lb).transpose(1, 2, 0, 3).reshape(2, 2, 4 * n_lb)
    b_ada_chip = lax.dynamic_slice(b_ada, (0, chip * n_ada), (1, n_ada))
    mod_cols = mod_matmul(c_all, w_ada[0], b_ada_chip)
    got = all_gather_small(mod_cols.reshape(-1, 128), "gather_mod").reshape(4, 2, 8, n_ada)
    mod = lax.dynamic_index_in_dim(got[:, 0], me, axis=1, keepdims=False).reshape(6, 1, D)
    sh1, sc1, gt1, sh2, sc2, gt2 = (mod[i] for i in range(6))

    big = [("w_in", w_in, "col"), ("w_a_out", w_a_out, "col"), ("w_b_out", w_b_out, "col"), ("w_o", w_o, "row"),
           ("w_ff1", w_ff1, "col"), ("w_ff2", w_ff2, "row")]
    kinds = [k for _, _, k in big]
    chip_idx, core = chip.reshape(1).astype(jnp.int32), mc.reshape(1).astype(jnp.int32)
    fulls = [cast_into_full(w[0], kind, chip_idx, "cast_" + nm) for nm, w, kind in big]
    dims = [w.shape[1:] for _, w, _ in big]
    later = lambda lo, hi: gather_comm(fulls[lo:hi], kinds[lo:hi], dims[lo:hi])
    halves_summed = lambda grads, name: [add_halves(g, l, core) for g, l in zip(grads, exchange_halves(grads, name))]

    bst = b_spatial[0].T
    a1 = prenorm(h0, g_pre_mix, sc1, sh1)
    proj, w_in_f, (w_a_f, w_b_f, w_o_f) = in_proj_gathered(a1, fulls[0], chip_idx, dims[0], later(1, 4))
    o, (w_ff1_f,) = hgrn_fwd(proj, lb_full, comm=later(4, 5))
    ya_pre = hgrn_post_fwd(o, proj, g_hgrn_norm)
    sgu = sgu_fwd(proj, g_sgu_norm, w_spatial[0], bst)
    y_a, y_b, merged = merge_matmul(ya_pre, sgu, w_a_f, w_b_f, proj)
    mo, h1, a2 = out_proj(merged, w_o_f, h0, gt1, g_post_mix, g_pre_ffn, sc2, sh2)
    (f1, hid), (w_ff2_f,) = matmul(a2, w_ff1_f, mode="nn", out_dtype=BF16, tm=1024, tn=1024, tk=2048, name="ff1", relu2=True,
                                   comm=later(5, 6))
    ff = matmul(hid, w_ff2_f, mode="nn", out_dtype=F32, tm=1024, tn=1024, tk=2048, name="ff2")
    dy, dff, loss_parts, d_gt2, d_g_post_ffn = loss_bwd(ff, h1, tgt, gt2, g_post_ffn)
    loss = lax.psum(0.5 * loss_parts[0, 0] / D, ("x", "y", "c"))

    df1 = ff2_bwd(dff, w_ff2_f, f1)
    gr_ff2 = matmul(hid, dff, mode="tn", out_dtype=BF16, tm=1024, tn=1024, tk=2048, name="dw_ff2")
    gr_ff2 = gr_ff2.reshape(4, 2, -1, D)
    da2, (landed_ff2,) = matmul(df1, w_ff1_f, mode="nt", out_dtype=F32, tm=1024, tn=1024, tk=2048, name="da2", comm=exchange_comm([gr_ff2]))
    gr_ff1 = matmul(a2, df1, mode="tn", out_dtype=BF16, tm=1024, tn=2048, tk=1024, name="dw_ff1", split=(4, 2))
    (dh1, dmo, d_sh2, d_sc2, d_g_pre_ffn, d_gt1, d_g_post_mix), (landed_ff1,) = ffn_norm_bwd(
        dy, da2, h1, mo, g_pre_ffn, sc2, gt1, g_post_mix, exchange_comm([gr_ff1]))
    parts_ff = [add_halves(gr_ff1, landed_ff1, core), add_halves(gr_ff2, landed_ff2, core)]
    dya, dyb, dga, dgb = out_proj_bwd(dmo, w_o_f, y_a, y_b, proj)
    gr_o = matmul(merged, dmo, mode="tn", out_dtype=BF16, tm=1024, tn=1024, tk=2048, name="dw_o")
    dsgu = matmul(dyb, w_b_f, mode="nt", out_dtype=F32, tm=512, tn=1024, tk=2048, name="dsgu")
    gr_b = matmul(sgu, dyb, mode="tn", out_dtype=BF16, tm=512, tn=512, tk=4096, name="dw_b_out", split=(4, 2))
    dz, d_w_spatial, d_b_spatial, d_g_sgu = sgu_bwd(proj, dsgu, g_sgu_norm, w_spatial[0], bst)
    dya_pre = matmul(dya, w_a_f, mode="nt", out_dtype=F32, tm=512, tn=1024, tk=2048, name="dya_pre")
    gr_a = matmul(ya_pre, dya, mode="tn", out_dtype=BF16, tm=512, tn=512, tk=4096, name="dw_a_out", split=(4, 2))
    gr_mix = [gr_a, gr_b, gr_o.reshape(4, 2, -1, D)]
    (do, dog, d_g_hgrn), landed_halves = hgrn_post_bwd(dya_pre, o, proj, g_hgrn_norm, exchange_comm(gr_mix))
    parts_mix = [add_halves(g, l, core) for g, l in zip(gr_mix, landed_halves)]
    chips_summed = lambda parts, landed: [sum_chips(p, l, chip_idx) for p, l in zip(parts, landed)]
    (dq, dv, dlg, d_lb), landed_ff = hgrn_bwd(proj, do, lb_full, comm=scatter_comm(parts_ff))
    own_ff = chips_summed(parts_ff, landed_ff)
    dproj = jnp.concatenate([dq, dlg, dv, dog, dz, dga, dgb], axis=1)
    early = _pack([d_g_sgu, d_w_spatial, d_b_spatial[:, 0, :]])
    gr_in, (*landed_mix, got_early) = matmul(a1, dproj, mode="tn", out_dtype=BF16, tm=1024, tn=2816, tk=1024, name="dw_in", split=(4, 2),
                                             comm=_join(scatter_comm(parts_mix), gather8_comm(early)))
    own_mix = chips_summed(parts_mix, landed_mix)
    parts_in = halves_summed([gr_in], "exchange_in")
    da1, (landed_in, *other_rest) = matmul(dproj, w_in_f, mode="nt", out_dtype=F32, tm=1024, tn=1024, tk=2816, name="da1",
                                           comm=_join(scatter_comm(parts_in), share_comm(own_mix + own_ff)))
    own_in = chips_summed(parts_in, [landed_in])
    other_in = comm_call(share_comm(own_in), "share_w_in")
    own, other = own_in + own_mix + own_ff, list(other_in) + other_rest
    grad_x, d_sh1, d_sc1, d_g_pre_mix = mix_norm_bwd(da1, h0, dh1, g_pre_mix, sc1)
    out = {}

    mine = _pack([d_sh1, d_sc1, d_gt1, d_sh2, d_sc2, d_gt2, d_g_pre_mix, d_g_post_mix, d_g_pre_ffn, d_g_post_ffn, d_g_hgrn, d_lb])
    got = all_gather_small(mine, "gather_small_grads")
    g_b_ada, g_g1, g_g2, g_g3, g_g4, g_hg, g_lb = _unpack(
        sum_devices(got, "sum_small_grads"), [(1, 6 * D), (1, D), (1, D), (1, D), (1, D), (1, HEAD_DIM), (2, 1024)])
    g_sg, g_ws, g_bs = _unpack(sum_devices(got_early, "sum_sgu_grads"), [(1, 1024), w_spatial.shape, b_spatial.shape])
    g_lbl = lax.dynamic_slice(lb_logits_grad(g_lb, lb_full), (0, 0, chip * n_lb), (2, 2, n_lb))
    names = ["b_ada", "g_pre_mix", "g_post_mix", "g_pre_ffn", "g_post_ffn", "g_hgrn_norm", "g_sgu_norm", "w_spatial", "b_spatial", "lb_logits"]
    ws = [b_ada, g_pre_mix, g_post_mix, g_pre_ffn, g_post_ffn, g_hgrn_norm, g_sgu_norm, w_spatial, b_spatial, lb_logits]
    gs = [g_b_ada, g_g1, g_g2, g_g3, g_g4, g_hg, g_sg, g_ws, g_bs, g_lbl]
    ms = [m_b_ada, m_g_pre_mix, m_g_post_mix, m_g_pre_ffn, m_g_post_ffn, m_g_hgrn_norm, m_g_sgu_norm, m_w_spatial, m_b_spatial, m_lb_logits]
    vs = [v_b_ada, v_g_pre_mix, v_g_post_mix, v_g_pre_ffn, v_g_post_ffn, v_g_hgrn_norm, v_g_sgu_norm, v_w_spatial, v_b_spatial, v_lb_logits]
    shapes = [w.shape for w in ws]
    upd = adamw(_pack(ws), _pack(gs), _pack(ms), _pack(vs), "adamw_small")
    upd = [_unpack(u, shapes) for u in upd]
    for i, nm in enumerate(names):
        out[nm] = (gs[i], upd[0][i], upd[1][i], upd[2][i])

    dmod_all = got[:, :6 * D // 128, :].reshape(8, 6 * D)
    dmod_chip = lax.dynamic_slice(dmod_all, (0, chip * n_ada), (8, n_ada))
    out["w_ada"] = tuple(a[None] for a in wada_update(c_all, dmod_chip, w_ada[0], m_w_ada[0], v_w_ada[0]))
    for (nm, w, _), a, b, m, v in zip(big, own, other, (m_w_in, m_w_a_out, m_w_b_out, m_w_o, m_w_ff1, m_w_ff2),
                                      (v_w_in, v_w_a_out, v_w_b_out, v_w_o, v_w_ff1, v_w_ff2)):
        out[nm] = tuple(t[None] for t in adamw_halves(w[0], a, b, m[0], v[0], core, "adamw_" + nm))

    order = ["w_ada", "b_ada", "g_pre_mix", "g_post_mix", "g_pre_ffn", "g_post_ffn", "w_in", "lb_logits", "g_hgrn_norm", "w_a_out",
             "g_sgu_norm", "w_spatial", "b_spatial", "w_b_out", "w_o", "w_ff1", "w_ff2"]
    return (loss, grad_x[None], *[out[nm][0] for nm in order], *[out[nm][1] for nm in order], *[out[nm][2] for nm in order],
            *[out[nm][3] for nm in order])
```

```python
import functools
import math

import jax
import jax.numpy as jnp
from jax import lax
from jax.experimental import pallas as pl
from jax.experimental.pallas import tpu as pltpu

F32, BF16 = jnp.float32, jnp.bfloat16
HI = lax.Precision.HIGHEST
MESH = pl.DeviceIdType.MESH
ANY = pl.BlockSpec(memory_space=pl.ANY)

EPS = 1e-6
D_MODEL = 2048
N_HEADS = 8
HEAD_DIM = 128
HGRN_CHUNK = 32
HGRN_BLOCK = 256
HGRN_BLOCK_FWD = 512
SGU_CHUNK = 128
SGU_GROUPS = 8
Q_SCALE = HEAD_DIM ** -0.5
COL_Q, COL_FFW, COL_FBW, COL_V, COL_OG, COL_U, COL_ZV, COL_GA, COL_GB = 0, 1, 2, 3, 4, 5, 6, 7, 9
N_PROJ = 11264
VMEM_BYTES_V7X = 64 * 1024 * 1024
VMEM_LIMIT = VMEM_BYTES_V7X - 8 * 1024 * 1024

ADAM_LR, ADAM_B1, ADAM_B2, ADAM_EPS, ADAM_WD, ADAM_STEP = 0.001, 0.9, 0.999, 1e-08, 0.01, 10
ADAM_C1 = 1.0 - ADAM_B1 ** ADAM_STEP
ADAM_C2 = 1.0 - ADAM_B2 ** ADAM_STEP


def _cp(*sem):
    return pltpu.CompilerParams(dimension_semantics=sem if sem else None, vmem_limit_bytes=VMEM_LIMIT)


def _vec(d):
    return pl.BlockSpec((1, d), lambda *_: (0, 0))


def _colsum(x):
    return jnp.sum(x, axis=0, keepdims=True)


def _nt(a, b):
    return lax.dot_general(a, b, (((1,), (1,)), ((), ())), preferred_element_type=F32)


def _tn(a, b):
    return lax.dot_general(a, b, (((0,), (0,)), ((), ())), preferred_element_type=F32)


def _nn(a, b):
    return jnp.dot(a, b, preferred_element_type=F32)


def _adamw(w, g, m, v):
    m2 = ADAM_B1 * m + (1.0 - ADAM_B1) * g
    v2 = ADAM_B2 * v + (1.0 - ADAM_B2) * (g * g)
    delta = -ADAM_LR * ((m2 / ADAM_C1) / (jnp.sqrt(v2 / ADAM_C2) + ADAM_EPS) + ADAM_WD * w)
    return delta, m2, v2


class _Comm:
    def __init__(self, operands, out_shape, aliases, n_sems, start, finish):
        self.operands, self.out_shape, self.aliases, self.n_sems = list(operands), list(out_shape), dict(aliases), n_sems
        self.start, self.finish = start, finish


def _pallas(body, *, name, grid, in_specs, out_specs, out_shape, scratch, semantics, operands, comm=None):
    if comm is None:
        res = pl.pallas_call(body, name=name, grid=grid, in_specs=in_specs, out_specs=out_specs, out_shape=out_shape,
                             scratch_shapes=scratch, compiler_params=_cp(*semantics))(*operands)
        return res, []
    n_in, n_out, n_scr = len(in_specs), len(out_specs), len(scratch)
    nci, nco = len(comm.operands), len(comm.out_shape)

    def with_comm(*refs):
        ins, rest = refs[:n_in], refs[n_in:]
        cin, rest = rest[:nci], rest[nci:]
        outs, rest = rest[:n_out], rest[n_out:]
        cout, rest = rest[:nco], rest[nco:]
        scr, (send, recv) = rest[:n_scr], rest[n_scr:]
        ids = [pl.program_id(a) for a in range(len(grid))]
        first = functools.reduce(jnp.logical_and, [i == 0 for i in ids])
        last = functools.reduce(jnp.logical_and, [i == g - 1 for i, g in zip(ids, grid)])

        @pl.when(first)
        def _():
            comm.start(cin, cout, send, recv)

        body(*ins, *outs, *scr)

        @pl.when(last)
        def _():
            comm.finish(cin, cout, send, recv)

    res = pl.pallas_call(
        with_comm, name=name, grid=grid, in_specs=list(in_specs) + [ANY] * nci, out_specs=list(out_specs) + [ANY] * nco,
        out_shape=list(out_shape) + comm.out_shape, input_output_aliases={n_in + i: n_out + o for i, o in comm.aliases.items()},
        scratch_shapes=list(scratch) + [pltpu.SemaphoreType.DMA((comm.n_sems,)), pltpu.SemaphoreType.DMA((comm.n_sems,))],
        compiler_params=_cp(*["arbitrary"] * len(grid)),
    )(*operands, *comm.operands)
    return res[:n_out], res[n_out:]


def matmul(a, b, *, mode, out_dtype, tm, tn, tk, name, split=None, comm=None, relu2=False):
    if mode == "tn":
        (K, M), (_, N) = a.shape, b.shape
    elif mode == "nt":
        (M, K), (N, _) = a.shape, b.shape
    else:
        (M, K), (_, N) = a.shape, b.shape
    tm, tn, tk = min(tm, M), min(tn, N), min(tk, K)
    nk = K // tk
    a_spec = pl.BlockSpec((tk, tm), lambda i, j, k: (k, i)) if mode == "tn" else pl.BlockSpec((tm, tk), lambda i, j, k: (i, k))
    b_spec = pl.BlockSpec((tn, tk), lambda i, j, k: (j, k)) if mode == "nt" else pl.BlockSpec((tk, tn), lambda i, j, k: (k, j))
    dot = {"nn": _nn, "nt": _nt, "tn": _tn}[mode]
    if split is None:
        out_shape = jax.ShapeDtypeStruct((M, N), out_dtype)
        out_spec = pl.BlockSpec((tm, tn), lambda i, j, k: (i, j))
    else:
        nj, nh = split
        rows, cols = M // nh, N // nj
        tm, tn = min(tm, rows), min(tn, cols)
        bi, bj = rows // tm, cols // tn
        out_shape = jax.ShapeDtypeStruct((nj, nh, rows, cols), out_dtype)
        out_spec = pl.BlockSpec((None, None, tm, tn), lambda i, j, k: (j // bj, i // bi, i % bi, j % bj))

    def finish(y, o_ref, sq_ref):
        o_ref[...] = y.astype(o_ref.dtype)
        if relu2:
            p = jnp.maximum(y, 0.0)
            sq_ref[0][...] = (p * p).astype(BF16)

    if nk == 1:
        def body(a_ref, b_ref, o_ref, *sq_ref):
            finish(dot(a_ref[...], b_ref[...]), o_ref, sq_ref)
        scratch = []
    else:
        def body(a_ref, b_ref, o_ref, *rest):
            acc_ref, k = rest[-1], pl.program_id(2)

            @pl.when(k == 0)
            def _():
                acc_ref[...] = jnp.zeros_like(acc_ref)

            acc_ref[...] += dot(a_ref[...], b_ref[...])

            @pl.when(k == nk - 1)
            def _():
                finish(acc_ref[...], o_ref, rest[:-1])
        scratch = [pltpu.VMEM((tm, tn), F32)]

    out_specs, out_shapes = [out_spec], [out_shape]
    if relu2:
        out_specs, out_shapes = out_specs + [out_spec], out_shapes + [jax.ShapeDtypeStruct(out_shape.shape, BF16)]
    outs, landed = _pallas(
        body, name=name, grid=(M // tm, N // tn, nk), in_specs=[a_spec, b_spec], out_specs=out_specs, out_shape=out_shapes,
        scratch=scratch, semantics=("parallel", "parallel", "arbitrary"), operands=(a, b), comm=comm)
    out = tuple(outs) if relu2 else outs[0]
    return out if comm is None else (out, landed)


def cast_into_full(w, kind, chip, name):
    r, cc = w.shape
    tr = min(r, 512)
    nb = r // tr

    def body(chip_ref, w_ref, o_ref):
        o_ref[...] = w_ref[...].astype(BF16)

    if kind == "col":
        full, out_map = (r, 4 * cc), lambda i, chip_ref: (i, chip_ref[0])
    else:
        full, out_map = (4 * r, cc), lambda i, chip_ref: (chip_ref[0] * nb + i, 0)
    return pl.pallas_call(
        body, name=name, out_shape=jax.ShapeDtypeStruct(full, BF16),
        grid_spec=pltpu.PrefetchScalarGridSpec(
            num_scalar_prefetch=1, grid=(nb,), in_specs=[pl.BlockSpec((tr, cc), lambda i, chip_ref: (i, 0))],
            out_specs=pl.BlockSpec((tr, cc), out_map)),
        compiler_params=_cp("parallel"),
    )(chip, w)


def mod_matmul(c_all, w_ada, b_ada):
    D, N = w_ada.shape
    tn = 1024

    def body(c_ref, w_ref, b_ref, o_ref):
        c = c_ref[...]
        sc = c * jax.nn.sigmoid(c)
        o_ref[...] = jnp.dot(sc, w_ref[...], precision=HI, preferred_element_type=F32) + b_ref[...]

    return pl.pallas_call(
        body, name="mod_matmul", out_shape=jax.ShapeDtypeStruct((8, N), F32), grid=(N // tn,),
        in_specs=[pl.BlockSpec((8, D), lambda j: (0, 0)), pl.BlockSpec((D, tn), lambda j: (0, j)),
                  pl.BlockSpec((1, tn), lambda j: (0, j))],
        out_specs=pl.BlockSpec((8, tn), lambda j: (0, j)), compiler_params=_cp("parallel"),
    )(c_all, w_ada, b_ada)


def prenorm(h, g, sc, sh):
    T, D = h.shape
    tm = min(256, T)

    def body(h_ref, g_ref, sc_ref, sh_ref, a_ref):
        x = h_ref[...]
        r = lax.rsqrt(jnp.mean(x * x, axis=-1, keepdims=True) + EPS)
        a_ref[...] = ((x * r) * g_ref[...] * (1.0 + sc_ref[...]) + sh_ref[...]).astype(BF16)

    row = pl.BlockSpec((tm, D), lambda i: (i, 0))
    return pl.pallas_call(
        body, name="prenorm", out_shape=jax.ShapeDtypeStruct((T, D), BF16), grid=(T // tm,),
        in_specs=[row, _vec(D), _vec(D), _vec(D)], out_specs=row, compiler_params=_cp("parallel"),
    )(h, g, sc, sh)


def in_proj_gathered(a, w_full, chip, dims, tail):
    T, D = a.shape
    rows, cc = dims
    tm, tn = min(512, T), cc // 2
    ni = T // tm
    half = rows // 2

    nt = len(tail.operands)

    def body(chip_ref, a_ref, w_in_ref, *rest):
        tail_in, (y_ref, w_ref), rest = rest[:nt], rest[nt:nt + 2], rest[nt + 2:]
        tail_out, (wbuf, wsem, send_sems, recv_sems, tail_send, tail_recv) = rest[:nt], rest[nt:]
        q, j, i = pl.program_id(0), pl.program_id(1), pl.program_id(2)
        mx, my, mc, _ = _place()
        me = chip_ref[0]

        def tile(block, jj):
            src = w_ref.at[:, pl.ds(pl.multiple_of(block * cc + jj * tn, 128), tn)]
            return pltpu.make_async_copy(src, wbuf.at[jj], wsem.at[jj])

        def rows_half(block, hh):
            return w_ref.at[pl.ds(pl.multiple_of(hh * half, 16), half), pl.ds(pl.multiple_of(block * cc, 128), cc)]

        def over_ici(s, block):
            peer = (1 - mx if s & 2 else mx, 1 - my if s & 1 else my, mc)
            reg = rows_half(block, mc)
            return pltpu.make_async_remote_copy(src_ref=reg, dst_ref=reg, send_sem=send_sems.at[s - 1], recv_sem=recv_sems.at[s - 1],
                                                device_id=peer, device_id_type=MESH)

        def over_d2d(s, block, hh):
            reg = rows_half(block, hh)
            return pltpu.make_async_remote_copy(src_ref=reg, dst_ref=reg, send_sem=send_sems.at[2 + s], recv_sem=recv_sems.at[2 + s],
                                                device_id=(mx, my, 1 - mc), device_id_type=MESH)

        def passed_on(s, block):
            k = s - 1
            reg = w_ref.at[pl.ds(pl.multiple_of(mc * half + k * (half // 2), 16), half // 2), pl.ds(pl.multiple_of(block * cc, 128), cc)]
            peer = (1 - mx, my, mc) if s == 1 else (mx, 1 - my, mc)
            return pltpu.make_async_remote_copy(src_ref=reg, dst_ref=reg, send_sem=send_sems.at[6 + k], recv_sem=recv_sems.at[6 + k],
                                                device_id=peer, device_id_type=MESH)

        @pl.when((q == 0) & (j == 0) & (i == 0))
        def _():
            for s in (1, 2):
                over_ici(s, me).start()
            tile(me, 0).start()

        @pl.when(i == 0)
        def _():
            tile(me ^ q, j).wait()

        @pl.when((i == 0) & (j == 0))
        def _():
            tile(me ^ q, 1).start()

        y_ref[...] = _nn(a_ref[...], wbuf[j])

        @pl.when((q == 0) & (j == 1) & (i == ni - 1))
        def _():
            for s in (1, 2):
                over_ici(s, me ^ s).wait_recv()
                passed_on(s, me ^ s).start()
                over_d2d(s, me ^ s, mc).start()
            tail.start(tail_in, tail_out, tail_send, tail_recv)
            over_d2d(1, me ^ 1, 1 - mc).wait_recv()
            tile(me ^ 1, 0).start()

        @pl.when((q == 1) & (j == 1) & (i == ni - 1))
        def _():
            over_d2d(2, me ^ 2, 1 - mc).wait_recv()
            tile(me ^ 2, 0).start()

        @pl.when((q == 2) & (j == 1) & (i == ni - 1))
        def _():
            for s in (1, 2):
                passed_on(s, me ^ 3).wait_recv()
            over_d2d(3, me ^ 3, mc).start()
            over_d2d(3, me ^ 3, 1 - mc).wait_recv()
            tile(me ^ 3, 0).start()

        @pl.when((q == 3) & (j == 1) & (i == ni - 1))
        def _():
            for s in (1, 2):
                over_ici(s, me).wait_send()
                passed_on(s, me ^ s).wait_send()
            for s in (1, 2, 3):
                over_d2d(s, me ^ s, mc).wait_send()
            tail.finish(tail_in, tail_out, tail_send, tail_recv)

    dma = pltpu.SemaphoreType.DMA
    y, w_out, *tail_res = pl.pallas_call(
        body, name="in_proj", out_shape=[jax.ShapeDtypeStruct((T, 4 * cc), F32), jax.ShapeDtypeStruct(w_full.shape, BF16)] + tail.out_shape,
        grid_spec=pltpu.PrefetchScalarGridSpec(
            num_scalar_prefetch=1, grid=(4, 2, ni),
            in_specs=[pl.BlockSpec((tm, D), lambda q, j, i, chip_ref: (i, 0)), ANY] + [ANY] * nt,
            out_specs=[pl.BlockSpec((tm, tn), lambda q, j, i, chip_ref: (i, (chip_ref[0] ^ q) * 2 + j)), ANY] + [ANY] * nt,
            scratch_shapes=[pltpu.VMEM((2, D, tn), BF16), dma((2,)), dma((8,)), dma((8,)), dma((tail.n_sems,)), dma((tail.n_sems,))]),
        input_output_aliases={2: 1, **{3 + i: 2 + o for i, o in tail.aliases.items()}},
        compiler_params=_cp("arbitrary", "arbitrary", "arbitrary"),
    )(chip, a, w_full, *tail.operands)
    return y, w_out, tail_res


def _hgrn_lower_bound(l_ref):
    l0, l1 = l_ref[0:1, :], l_ref[1:2, :]
    m = jnp.maximum(l0, l1)
    e0, e1 = jnp.exp(l0 - m), jnp.exp(l1 - m)
    return e0 / (e0 + e1)


def _hgrn_chunk_mask(d, blk):
    r = lax.broadcasted_iota(jnp.int32, (blk, blk), 0)
    c = lax.broadcasted_iota(jnp.int32, (blk, blk), 1)
    same = (r // HGRN_CHUNK) == (c // HGRN_CHUNK)
    fwd = d == 0
    return same & (((c <= r) & fwd) | ((c >= r) & jnp.logical_not(fwd)))


def _chunk_total(x):
    x3 = x.reshape(x.shape[0] // HGRN_CHUNK, HGRN_CHUNK, x.shape[1])
    return jnp.broadcast_to(jnp.sum(x3, axis=1, keepdims=True), x3.shape).reshape(x.shape)


def _chunk_cumsum(x, suffix):
    pos = lax.broadcasted_iota(jnp.int32, x.shape, 0) % HGRN_CHUNK
    p, s = x, 1
    while s < HGRN_CHUNK:
        p = p + jnp.where(pos >= s, pltpu.roll(p, s, 0), 0.0)
        s *= 2
    return jnp.where(suffix, _chunk_total(x) - p + x, p)


def _block_loop(T, blk, body, init):
    n = T // blk
    return lax.fori_loop(0, n, body, init, unroll=2 if n % 2 == 0 else 1)


def _hgrn_gate(f, lb):
    s = jax.nn.sigmoid(f)
    sn = jax.nn.sigmoid(-f)
    fg = lb + (1.0 - lb) * s
    return s, sn, fg, jnp.log(fg), (1.0 - lb) * sn


def _hgrn_specs(T):
    col = lambda base: pl.BlockSpec((T, HEAD_DIM), lambda h, d: (0, base * N_HEADS + h))
    f_spec = pl.BlockSpec((T, HEAD_DIM), lambda h, d: (0, COL_FFW * N_HEADS + N_HEADS * d + h))
    l_spec = pl.BlockSpec((None, 2, HEAD_DIM), lambda h, d: (d, 0, h))
    return col, f_spec, l_spec


def hgrn_fwd(proj, lb_logits, comm=None):
    T = proj.shape[0]
    blk = min(HGRN_BLOCK_FWD, T)
    NC, CPB = T // HGRN_CHUNK, blk // HGRN_CHUNK
    col, f_spec, l_spec = _hgrn_specs(T)

    def body(l_ref, q_ref, f_ref, v_ref, o_ref, st_ref, dec_ref, qd_ref):
        d = pl.program_id(1)
        lb = _hgrn_lower_bound(l_ref)
        mask = _hgrn_chunk_mask(d, blk)

        def block(i, carry):
            rows = pl.ds(pl.multiple_of(i * blk, blk), blk)
            _, _, _, lf, k = _hgrn_gate(f_ref[rows, :], lb)
            b = _chunk_cumsum(lf, d == 1)
            bl = _chunk_total(lf)
            qd = (q_ref[rows, :] * Q_SCALE * jnp.exp(b)).astype(BF16)
            kd = (k * jnp.exp(-b)).astype(BF16)
            ke = (k * jnp.exp(bl - b)).astype(BF16)
            vb = v_ref[rows, :].astype(BF16)
            att = jnp.where(mask, _nt(qd, kd), 0.0).astype(BF16)
            o_ref[rows, :] = jnp.where(d == 0, 0.0, o_ref[rows, :]) + _nn(att, vb)
            qd_ref[rows, :] = qd
            dec = jnp.exp(bl)
            for cc in range(CPB):
                sl = slice(cc * HGRN_CHUNK, (cc + 1) * HGRN_CHUNK)
                n = i * CPB + cc
                st_ref[n] = _tn(vb[sl], ke[sl])
                dec_ref[n] = dec[cc * HGRN_CHUNK:cc * HGRN_CHUNK + 8, :]
            return carry

        _block_loop(T, blk, block, 0)

        def scan(t, s):
            n = jnp.where(d == 0, t, NC - 1 - t)
            u = st_ref[n]
            st_ref[n] = s
            return dec_ref[n][0:1, :] * s + u

        lax.fori_loop(0, NC, scan, jnp.zeros((HEAD_DIM, HEAD_DIM), F32))

        def inter(i, carry):
            rows = pl.ds(pl.multiple_of(i * blk, blk), blk)
            qd = qd_ref[rows, :]
            o_ref[rows, :] += jnp.concatenate(
                [_nt(qd[cc * HGRN_CHUNK:(cc + 1) * HGRN_CHUNK], st_ref[i * CPB + cc].astype(BF16)) for cc in range(CPB)], axis=0)
            return carry

        _block_loop(T, blk, inter, 0)

    (o,), landed = _pallas(
        body, name="hgrn_fwd", grid=(N_HEADS, 2), in_specs=[l_spec, col(COL_Q), f_spec, col(COL_V)],
        out_specs=[pl.BlockSpec((T, HEAD_DIM), lambda h, d: (0, h))], out_shape=[jax.ShapeDtypeStruct((T, N_HEADS * HEAD_DIM), F32)],
        scratch=[pltpu.VMEM((NC, HEAD_DIM, HEAD_DIM), F32), pltpu.VMEM((NC, 8, HEAD_DIM), F32), pltpu.VMEM((T, HEAD_DIM), BF16)],
        semantics=("parallel", "arbitrary"), operands=(lb_logits, proj, proj, proj), comm=comm)
    return o if comm is None else (o, landed)


def hgrn_post_fwd(o, proj, g_norm):
    T, W = o.shape
    tm = min(256, T)

    def body(o_ref, og_ref, g_ref, y_ref):
        g = g_ref[...]
        for h in range(N_HEADS):
            sl = slice(h * HEAD_DIM, (h + 1) * HEAD_DIM)
            x = o_ref[:, sl]
            r = lax.rsqrt(jnp.mean(x * x, axis=-1, keepdims=True) + EPS)
            og = og_ref[:, sl]
            y_ref[:, sl] = ((x * r) * g * (og * jax.nn.sigmoid(og))).astype(BF16)

    return pl.pallas_call(
        body, name="hgrn_post_fwd", out_shape=jax.ShapeDtypeStruct((T, W), BF16), grid=(T // tm,),
        in_specs=[pl.BlockSpec((tm, W), lambda i: (i, 0)), pl.BlockSpec((tm, W), lambda i: (i, COL_OG)), _vec(HEAD_DIM)],
        out_specs=pl.BlockSpec((tm, W), lambda i: (i, 0)), compiler_params=_cp("parallel"),
    )(o, proj, g_norm)


def _gelu(x):
    return 0.5 * x * (1.0 + lax.erf(x * (1.0 / math.sqrt(2.0))))


def _gelu_grad(x):
    return 0.5 * (1.0 + lax.erf(x * (1.0 / math.sqrt(2.0)))) + x * jnp.exp(-0.5 * x * x) * (1.0 / math.sqrt(2.0 * math.pi))


def _sgu_mix(u_ref, v_ref, g_ref, ws_ref, bst_ref):
    W = u_ref.shape[1]
    zu, zv = _gelu(u_ref[...]), _gelu(v_ref[...])
    dv = zv - jnp.mean(zv, axis=-1, keepdims=True)
    rstd = lax.rsqrt(jnp.mean(dv * dv, axis=-1, keepdims=True) + EPS)
    dhat = dv * rstd
    vn = (dhat * g_ref[...]).astype(BF16)
    gw = W // SGU_GROUPS
    vm = [_nn(ws_ref[g].astype(BF16), vn[:, g * gw:(g + 1) * gw]) + bst_ref[:, g:g + 1] for g in range(SGU_GROUPS)]
    return zu, rstd, dhat, vn, jnp.concatenate(vm, axis=1)


def sgu_fwd(proj, g_norm, w_spatial, b_spatial_t):
    T = proj.shape[0]
    W = 1024
    n_chunks = T // SGU_CHUNK

    def body(u_ref, v_ref, g_ref, ws_ref, bst_ref, y_ref):
        zu, _, _, _, vm = _sgu_mix(u_ref, v_ref, g_ref, ws_ref, bst_ref)
        y_ref[...] = (zu * vm).astype(BF16)

    blk = lambda cb: pl.BlockSpec((SGU_CHUNK, W), lambda i: (i, cb))
    return pl.pallas_call(
        body, name="sgu_fwd", out_shape=jax.ShapeDtypeStruct((T, W), BF16), grid=(n_chunks,),
        in_specs=[blk(COL_U), blk(COL_ZV), _vec(W), pl.BlockSpec((SGU_GROUPS, SGU_CHUNK, SGU_CHUNK), lambda i: (0, 0, 0)),
                  pl.BlockSpec((SGU_CHUNK, SGU_GROUPS), lambda i: (0, 0))],
        out_specs=blk(0), compiler_params=_cp("parallel"),
    )(proj, proj, g_norm, w_spatial, b_spatial_t)


def merge_matmul(ya_pre, sgu, w_a, w_b, proj):
    T, K = ya_pre.shape
    N = w_a.shape[1]
    tm, tn = min(512, T), 512
    gpb = 1024 // tn

    def body(a_ref, b_ref, wa_ref, wb_ref, ga_ref, gb_ref, ya_ref, yb_ref, m_ref):
        ya = _nn(a_ref[...], wa_ref[...])
        yb = _nn(b_ref[...], wb_ref[...])
        ya_ref[...] = ya.astype(BF16)
        yb_ref[...] = yb.astype(BF16)
        m_ref[...] = (jax.nn.sigmoid(ga_ref[...]) * ya + jax.nn.sigmoid(gb_ref[...]) * yb).astype(BF16)

    lhs = pl.BlockSpec((tm, K), lambda i, j: (i, 0))
    rhs = pl.BlockSpec((K, tn), lambda i, j: (0, j))
    out = pl.BlockSpec((tm, tn), lambda i, j: (i, j))
    return pl.pallas_call(
        body, name="merge_matmul", grid=(T // tm, N // tn),
        out_shape=[jax.ShapeDtypeStruct((T, N), BF16)] * 3,
        in_specs=[lhs, lhs, rhs, rhs, pl.BlockSpec((tm, tn), lambda i, j: (i, COL_GA * gpb + j)),
                  pl.BlockSpec((tm, tn), lambda i, j: (i, COL_GB * gpb + j))],
        out_specs=[out, out, out], compiler_params=_cp("parallel", "parallel"),
    )(ya_pre, sgu, w_a, w_b, proj, proj)


def out_proj(merged, w_o, h0, gt1, g_post, g_pre2, sc2, sh2):
    T, D = h0.shape
    tm = min(256, T)

    def body(m_ref, w_ref, h_ref, gt_ref, gp_ref, g2_ref, sc_ref, sh_ref, mo_ref, h1_ref, a2_ref):
        mo = _nn(m_ref[...], w_ref[...])
        mo_ref[...] = mo
        r = lax.rsqrt(jnp.mean(mo * mo, axis=-1, keepdims=True) + EPS)
        h1 = h_ref[...] + gt_ref[...] * ((mo * r) * gp_ref[...])
        h1_ref[...] = h1
        r2 = lax.rsqrt(jnp.mean(h1 * h1, axis=-1, keepdims=True) + EPS)
        a2_ref[...] = ((h1 * r2) * g2_ref[...] * (1.0 + sc_ref[...]) + sh_ref[...]).astype(BF16)

    row = pl.BlockSpec((tm, D), lambda i: (i, 0))
    return pl.pallas_call(
        body, name="out_proj", grid=(T // tm,),
        out_shape=[jax.ShapeDtypeStruct((T, D), F32), jax.ShapeDtypeStruct((T, D), F32), jax.ShapeDtypeStruct((T, D), BF16)],
        in_specs=[row, pl.BlockSpec((D, D), lambda i: (0, 0)), row] + [_vec(D)] * 5,
        out_specs=[row, row, row], compiler_params=_cp("parallel"),
    )(merged, w_o, h0, gt1, g_post, g_pre2, sc2, sh2)


def loss_bwd(ff, h1, tgt, gt2, g_post):
    T, D = ff.shape
    tm = min(256, T)

    def body(f_ref, h_ref, t_ref, gt_ref, g_ref, dy_ref, dff_ref, loss_ref, dgt_ref, dg_ref):
        @pl.when(pl.program_id(0) == 0)
        def _():
            loss_ref[...] = jnp.zeros_like(loss_ref)
            dgt_ref[...] = jnp.zeros_like(dgt_ref)
            dg_ref[...] = jnp.zeros_like(dg_ref)

        ff = f_ref[...]
        gt, g = gt_ref[...], g_ref[...]
        r = lax.rsqrt(jnp.mean(ff * ff, axis=-1, keepdims=True) + EPS)
        fhat = ff * r
        nf = fhat * g
        err = (h_ref[...] + gt * nf) - t_ref[...]
        loss_ref[...] += jnp.sum(err * err)
        dy = err * (1.0 / D)
        dy_ref[...] = dy
        dgt_ref[...] += _colsum(dy * nf)
        dnf = dy * gt
        dg_ref[...] += _colsum(dnf * fhat)
        u = dnf * g
        dff_ref[...] = (r * (u - fhat * jnp.mean(u * fhat, axis=-1, keepdims=True))).astype(BF16)

    row = pl.BlockSpec((tm, D), lambda i: (i, 0))
    return pl.pallas_call(
        body, name="loss_bwd", grid=(T // tm,),
        out_shape=[jax.ShapeDtypeStruct((T, D), F32), jax.ShapeDtypeStruct((T, D), BF16), jax.ShapeDtypeStruct((8, 128), F32),
                   jax.ShapeDtypeStruct((1, D), F32), jax.ShapeDtypeStruct((1, D), F32)],
        in_specs=[row, row, row, _vec(D), _vec(D)],
        out_specs=[row, row, pl.BlockSpec((8, 128), lambda i: (0, 0)), _vec(D), _vec(D)],
        compiler_params=_cp("arbitrary"),
    )(ff, h1, tgt, gt2, g_post)


def ff2_bwd(dff, w_ff2, f1):
    T, D = dff.shape
    K = w_ff2.shape[0]
    tm, tn = min(512, T), 2048

    def body(a_ref, w_ref, f_ref, o_ref):
        o_ref[...] = (_nt(a_ref[...], w_ref[...]) * (2.0 * jnp.maximum(f_ref[...].astype(F32), 0.0))).astype(BF16)

    return pl.pallas_call(
        body, name="ff2_bwd", out_shape=jax.ShapeDtypeStruct((T, K), BF16), grid=(K // tn, T // tm),
        in_specs=[pl.BlockSpec((tm, D), lambda j, i: (i, 0)), pl.BlockSpec((tn, D), lambda j, i: (j, 0)),
                  pl.BlockSpec((tm, tn), lambda j, i: (i, j))],
        out_specs=pl.BlockSpec((tm, tn), lambda j, i: (i, j)), compiler_params=_cp("parallel", "parallel"),
    )(dff, w_ff2, f1)


def ffn_norm_bwd(dy, da2, h1, mo, g_pre2, sc2, gt1, g_post, comm):
    T, D = dy.shape
    tm = min(256, T)

    def body(dy_ref, da_ref, h_ref, mo_ref, g2_ref, sc_ref, gt_ref, gp_ref, dh_ref, dmo_ref, s_sh, s_sc, s_g2, s_gt, s_gp):
        @pl.when(pl.program_id(0) == 0)
        def _():
            for s in (s_sh, s_sc, s_g2, s_gt, s_gp):
                s[...] = jnp.zeros_like(s)

        h1, da = h_ref[...], da_ref[...]
        g2, sc = g2_ref[...], sc_ref[...]
        r2 = lax.rsqrt(jnp.mean(h1 * h1, axis=-1, keepdims=True) + EPS)
        n2 = h1 * r2
        s_sh[...] += _colsum(da)
        s_sc[...] += _colsum(da * (n2 * g2))
        s_g2[...] += _colsum(da * (1.0 + sc) * n2)
        dn2 = da * g2 * (1.0 + sc)
        dh1 = dy_ref[...] + r2 * (dn2 - n2 * jnp.mean(dn2 * n2, axis=-1, keepdims=True))
        dh_ref[...] = dh1
        mo = mo_ref[...]
        gt, gp = gt_ref[...], gp_ref[...]
        r = lax.rsqrt(jnp.mean(mo * mo, axis=-1, keepdims=True) + EPS)
        mhat = mo * r
        s_gt[...] += _colsum(dh1 * (mhat * gp))
        dnm = dh1 * gt
        s_gp[...] += _colsum(dnm * mhat)
        u = dnm * gp
        dmo_ref[...] = (r * (u - mhat * jnp.mean(u * mhat, axis=-1, keepdims=True))).astype(BF16)

    row = pl.BlockSpec((tm, D), lambda i: (i, 0))
    vec_out = jax.ShapeDtypeStruct((1, D), F32)
    return _pallas(
        body, name="ffn_norm_bwd", grid=(T // tm,),
        out_shape=[jax.ShapeDtypeStruct((T, D), F32), jax.ShapeDtypeStruct((T, D), BF16)] + [vec_out] * 5,
        in_specs=[row, row, row, row] + [_vec(D)] * 4, out_specs=[row, row] + [_vec(D)] * 5,
        scratch=[], semantics=("arbitrary",), operands=(dy, da2, h1, mo, g_pre2, sc2, gt1, g_post), comm=comm)


def out_proj_bwd(dmo, w_o, y_a, y_b, proj):
    T, D = dmo.shape
    tm, tn = min(512, T), 512
    gpb = 1024 // tn

    def body(a_ref, w_ref, ya_ref, yb_ref, ga_ref, gb_ref, dya_ref, dyb_ref, dga_ref, dgb_ref):
        dm = _nt(a_ref[...], w_ref[...])
        sa, sb = jax.nn.sigmoid(ga_ref[...]), jax.nn.sigmoid(gb_ref[...])
        dya_ref[...] = (dm * sa).astype(BF16)
        dyb_ref[...] = (dm * sb).astype(BF16)
        dga_ref[...] = (dm * ya_ref[...].astype(F32) * sa * (1.0 - sa)).astype(BF16)
        dgb_ref[...] = (dm * yb_ref[...].astype(F32) * sb * (1.0 - sb)).astype(BF16)

    out = pl.BlockSpec((tm, tn), lambda i, j: (i, j))
    return pl.pallas_call(
        body, name="out_proj_bwd", grid=(T // tm, D // tn), out_shape=[jax.ShapeDtypeStruct((T, D), BF16)] * 4,
        in_specs=[pl.BlockSpec((tm, D), lambda i, j: (i, 0)), pl.BlockSpec((tn, D), lambda i, j: (j, 0)), out, out,
                  pl.BlockSpec((tm, tn), lambda i, j: (i, COL_GA * gpb + j)), pl.BlockSpec((tm, tn), lambda i, j: (i, COL_GB * gpb + j))],
        out_specs=[out] * 4, compiler_params=_cp("parallel", "parallel"),
    )(dmo, w_o, y_a, y_b, proj, proj)


def sgu_bwd(proj, dsgu, g_norm, w_spatial, b_spatial_t):
    T = proj.shape[0]
    W = 1024
    gw = W // SGU_GROUPS

    def body(u_ref, v_ref, ds_ref, g_ref, ws_ref, bst_ref, dz_ref, dw_ref, db_ref, dg_ref):
        @pl.when(pl.program_id(0) == 0)
        def _():
            dw_ref[...] = jnp.zeros_like(dw_ref)
            db_ref[...] = jnp.zeros_like(db_ref)
            dg_ref[...] = jnp.zeros_like(dg_ref)

        zu, rstd, dhat, vn, vm = _sgu_mix(u_ref, v_ref, g_ref, ws_ref, bst_ref)
        ds = ds_ref[...]
        du = ds * vm
        dvm = ds * zu
        dvm_b = dvm.astype(BF16)
        ones = jnp.ones((8, gw), F32)
        dvn = []
        for g in range(SGU_GROUPS):
            sl = slice(g * gw, (g + 1) * gw)
            dw_ref[g] += _nt(dvm_b[:, sl], vn[:, sl])
            db_ref[g] += lax.dot_general(ones, dvm[:, sl], (((1,), (1,)), ((), ())), precision=HI, preferred_element_type=F32)
            dvn.append(_tn(ws_ref[g].astype(BF16), dvm_b[:, sl]))
        dvn = jnp.concatenate(dvn, axis=1)
        dg_ref[...] += _colsum(dvn * dhat)
        ddh = dvn * g_ref[...]
        dzv = rstd * (ddh - jnp.mean(ddh, axis=-1, keepdims=True) - dhat * jnp.mean(ddh * dhat, axis=-1, keepdims=True))
        dz_ref[:, 0:W] = (du * _gelu_grad(u_ref[...])).astype(BF16)
        dz_ref[:, W:2 * W] = (dzv * _gelu_grad(v_ref[...])).astype(BF16)

    blk = lambda cb: pl.BlockSpec((SGU_CHUNK, W), lambda i: (i, cb))
    full3 = lambda a, b, c: pl.BlockSpec((a, b, c), lambda i: (0, 0, 0))
    return pl.pallas_call(
        body, name="sgu_bwd", grid=(T // SGU_CHUNK,),
        out_shape=[jax.ShapeDtypeStruct((T, 2 * W), BF16), jax.ShapeDtypeStruct((SGU_GROUPS, SGU_CHUNK, SGU_CHUNK), F32),
                   jax.ShapeDtypeStruct((SGU_GROUPS, 8, SGU_CHUNK), F32), jax.ShapeDtypeStruct((1, W), F32)],
        in_specs=[blk(COL_U), blk(COL_ZV), blk(0), _vec(W), full3(SGU_GROUPS, SGU_CHUNK, SGU_CHUNK),
                  pl.BlockSpec((SGU_CHUNK, SGU_GROUPS), lambda i: (0, 0))],
        out_specs=[pl.BlockSpec((SGU_CHUNK, 2 * W), lambda i: (i, 0)), full3(SGU_GROUPS, SGU_CHUNK, SGU_CHUNK),
                   full3(SGU_GROUPS, 8, SGU_CHUNK), _vec(W)],
        compiler_params=_cp("arbitrary"),
    )(proj, proj, dsgu, g_norm, w_spatial, b_spatial_t)


def hgrn_post_bwd(dya, o, proj, g_norm, comm):
    T, W = o.shape
    tm = min(256, T)

    def body(dy_ref, o_ref, og_ref, g_ref, do_ref, dog_ref, dg_ref):
        @pl.when(pl.program_id(0) == 0)
        def _():
            dg_ref[...] = jnp.zeros_like(dg_ref)

        g = g_ref[...]
        dg = jnp.zeros((1, HEAD_DIM), F32)
        for h in range(N_HEADS):
            sl = slice(h * HEAD_DIM, (h + 1) * HEAD_DIM)
            x, og, dy = o_ref[:, sl], og_ref[:, sl], dy_ref[:, sl]
            r = lax.rsqrt(jnp.mean(x * x, axis=-1, keepdims=True) + EPS)
            xhat = x * r
            s = jax.nn.sigmoid(og)
            don = dy * (og * s)
            dog_ref[:, sl] = (dy * (xhat * g) * (s * (1.0 + og * (1.0 - s)))).astype(BF16)
            dg += _colsum(don * xhat)
            u = don * g
            do_ref[:, sl] = r * (u - xhat * jnp.mean(u * xhat, axis=-1, keepdims=True))
        dg_ref[...] += dg

    row = pl.BlockSpec((tm, W), lambda i: (i, 0))
    return _pallas(
        body, name="hgrn_post_bwd", grid=(T // tm,),
        out_shape=[jax.ShapeDtypeStruct((T, W), F32), jax.ShapeDtypeStruct((T, W), BF16), jax.ShapeDtypeStruct((1, HEAD_DIM), F32)],
        in_specs=[row, row, pl.BlockSpec((tm, W), lambda i: (i, COL_OG)), _vec(HEAD_DIM)],
        out_specs=[row, row, _vec(HEAD_DIM)], scratch=[], semantics=("arbitrary",), operands=(dya, o, proj, g_norm), comm=comm)


def hgrn_bwd(proj, do, lb_logits, comm=None):
    T = proj.shape[0]
    NC, CPB = T // HGRN_CHUNK, HGRN_BLOCK // HGRN_CHUNK
    W = N_HEADS * HEAD_DIM
    col, f_spec, l_spec = _hgrn_specs(T)

    def body(l_ref, q_ref, f_ref, v_ref, do_ref, dq_ref, dv_ref, dlg_ref, dlb_ref, st_ref, dst_ref, dec_ref, ddec_ref, dqa_ref, dva_ref):
        d = pl.program_id(1)
        lb = _hgrn_lower_bound(l_ref)
        oml = 1.0 - lb
        mask = _hgrn_chunk_mask(d, HGRN_BLOCK)

        def values(rows):
            s, sn, fg, lf, k = _hgrn_gate(f_ref[rows, :], lb)
            b = _chunk_cumsum(lf, d == 1)
            bl = _chunk_total(lf)
            eb, enb, ee = jnp.exp(b), jnp.exp(-b), jnp.exp(bl - b)
            qd = q_ref[rows, :] * Q_SCALE * eb
            return s, sn, fg, k, bl, eb, enb, ee, qd, k * enb, k * ee

        def block1(i, carry):
            rows = pl.ds(pl.multiple_of(i * HGRN_BLOCK, HGRN_BLOCK), HGRN_BLOCK)
            _, _, _, _, bl, _, _, _, qd, _, ke = values(rows)
            qd, ke = qd.astype(BF16), ke.astype(BF16)
            vb, dob = v_ref[rows, :].astype(BF16), do_ref[rows, :].astype(BF16)
            dec = jnp.exp(bl)
            for cc in range(CPB):
                sl = slice(cc * HGRN_CHUNK, (cc + 1) * HGRN_CHUNK)
                n = i * CPB + cc
                st_ref[n] = _tn(vb[sl], ke[sl])
                dst_ref[n] = _tn(dob[sl], qd[sl])
                dec_ref[n] = dec[cc * HGRN_CHUNK:cc * HGRN_CHUNK + 8, :]
            return carry

        _block_loop(T, HGRN_BLOCK, block1, 0)

        def scan(t, s):
            n = jnp.where(d == 0, t, NC - 1 - t)
            u = st_ref[n]
            st_ref[n] = s
            return dec_ref[n][0:1, :] * s + u

        lax.fori_loop(0, NC, scan, jnp.zeros((HEAD_DIM, HEAD_DIM), F32))

        def rscan(t, ds):
            n = jnp.where(d == 0, NC - 1 - t, t)
            w = dst_ref[n]
            dst_ref[n] = ds
            ddec_ref[n] = jnp.broadcast_to(_colsum(ds * st_ref[n]), (8, HEAD_DIM))
            return dec_ref[n][0:1, :] * ds + w

        lax.fori_loop(0, NC, rscan, jnp.zeros((HEAD_DIM, HEAD_DIM), F32))

        def block3(i, dlb):
            rows = pl.ds(pl.multiple_of(i * HGRN_BLOCK, HGRN_BLOCK), HGRN_BLOCK)
            s, sn, fg, k, bl, eb, enb, ee, qd, kd, ke = values(rows)
            qdb, kdb, keb = qd.astype(BF16), kd.astype(BF16), ke.astype(BF16)
            vb, dob = v_ref[rows, :].astype(BF16), do_ref[rows, :].astype(BF16)
            att = jnp.where(mask, _nt(qdb, kdb), 0.0).astype(BF16)
            datt = jnp.where(mask, _nt(dob, vb), 0.0).astype(BF16)
            dv = _tn(att, dob)
            dqd = _nn(datt, kdb)
            dkd = _tn(datt, qdb)
            dv_i, dqd_i, dke, ddl = [], [], [], []
            for cc in range(CPB):
                sl = slice(cc * HGRN_CHUNK, (cc + 1) * HGRN_CHUNK)
                n = i * CPB + cc
                st_b, dst_b = st_ref[n].astype(BF16), dst_ref[n].astype(BF16)
                dv_i.append(_nt(keb[sl], dst_b))
                dqd_i.append(_nn(dob[sl], st_b))
                dke.append(_nn(vb[sl], dst_b))
                ddl.append(jnp.broadcast_to(ddec_ref[n][0:1, :] * dec_ref[n][0:1, :], (HGRN_CHUNK, HEAD_DIM)))
            dv = dv + jnp.concatenate(dv_i, axis=0)
            dqd = dqd + jnp.concatenate(dqd_i, axis=0)
            dke = jnp.concatenate(dke, axis=0)
            dq = dqd * eb * Q_SCALE
            dk = dkd * enb + dke * ee
            t_end = dke * ke
            db = dqd * qd - dkd * kd - t_end
            dlf = _chunk_cumsum(db, d == 0) + _chunk_total(t_end) + jnp.concatenate(ddl, axis=0)
            e = dlf / fg - dk
            dlg_ref[rows, :] = (oml * e * s * sn).astype(BF16)

            dq = jnp.where(d == 0, 0.0, dqa_ref[rows, :]) + dq
            dv = jnp.where(d == 0, 0.0, dva_ref[rows, :]) + dv
            dqa_ref[rows, :] = dq
            dva_ref[rows, :] = dv
            dq_ref[rows, :] = dq.astype(BF16)
            dv_ref[rows, :] = dv.astype(BF16)

            return dlb + _colsum(e * sn)

        dlb_ref[...] = _block_loop(T, HGRN_BLOCK, block3, jnp.zeros((1, HEAD_DIM), F32))

    head = pl.BlockSpec((T, HEAD_DIM), lambda h, d: (0, h))
    big = pltpu.VMEM((NC, HEAD_DIM, HEAD_DIM), F32)
    small = pltpu.VMEM((NC, 8, HEAD_DIM), F32)
    acc = pltpu.VMEM((T, HEAD_DIM), F32)
    outs, landed = _pallas(
        body, name="hgrn_bwd", grid=(N_HEADS, 2),
        out_shape=[jax.ShapeDtypeStruct((T, W), BF16), jax.ShapeDtypeStruct((T, W), BF16), jax.ShapeDtypeStruct((T, 2 * W), BF16),
                   jax.ShapeDtypeStruct((2, 1, W), F32)],
        in_specs=[l_spec, col(COL_Q), f_spec, col(COL_V), head],
        out_specs=[head, head, pl.BlockSpec((T, HEAD_DIM), lambda h, d: (0, N_HEADS * d + h)),
                   pl.BlockSpec((None, 1, HEAD_DIM), lambda h, d: (d, 0, h))],
        scratch=[big, big, small, small, acc, acc], semantics=("parallel", "arbitrary"), operands=(lb_logits, proj, proj, proj, do), comm=comm)
    return outs if comm is None else (outs, landed)


def mix_norm_bwd(da1, h0, dh1, g_pre, sc1):
    T, D = h0.shape
    tm = min(256, T)

    def body(da_ref, h_ref, dh_ref, g_ref, sc_ref, gx_ref, s_sh, s_sc, s_g):
        @pl.when(pl.program_id(0) == 0)
        def _():
            for s in (s_sh, s_sc, s_g):
                s[...] = jnp.zeros_like(s)

        h, da = h_ref[...], da_ref[...]
        g, sc = g_ref[...], sc_ref[...]
        r = lax.rsqrt(jnp.mean(h * h, axis=-1, keepdims=True) + EPS)
        n = h * r
        s_sh[...] += _colsum(da)
        s_sc[...] += _colsum(da * (n * g))
        s_g[...] += _colsum(da * (1.0 + sc) * n)
        dn = da * g * (1.0 + sc)
        gx_ref[...] = dh_ref[...] + r * (dn - n * jnp.mean(dn * n, axis=-1, keepdims=True))

    row = pl.BlockSpec((tm, D), lambda i: (i, 0))
    return pl.pallas_call(
        body, name="mix_norm_bwd", grid=(T // tm,),
        out_shape=[jax.ShapeDtypeStruct((T, D), F32)] + [jax.ShapeDtypeStruct((1, D), F32)] * 3,
        in_specs=[row, row, row, _vec(D), _vec(D)], out_specs=[row] + [_vec(D)] * 3, compiler_params=_cp("arbitrary"),
    )(da1, h0, dh1, g_pre, sc1)


def adamw(w, g, m, v, name):
    R, C = w.shape
    tr = R if R * C * 4 <= (1 << 21) else max(8, ((1 << 21) // (C * 4)) // 8 * 8)
    while R % tr:
        tr -= 8

    def body(w_ref, g_ref, m_ref, v_ref, d_ref, m2_ref, v2_ref):
        d_ref[...], m2_ref[...], v2_ref[...] = _adamw(w_ref[...], g_ref[...], m_ref[...], v_ref[...])

    row = pl.BlockSpec((tr, C), lambda i: (i, 0))
    return pl.pallas_call(
        body, name=name, grid=(R // tr,), out_shape=[jax.ShapeDtypeStruct((R, C), F32)] * 3,
        in_specs=[row] * 4, out_specs=[row] * 3, compiler_params=_cp("parallel"),
    )(w, g, m, v)


def wada_update(c_all, dmod, w, m, v):
    D, N = w.shape
    tm, tn = 512, 1024

    def body(c_ref, dm_ref, w_ref, m_ref, v_ref, g_ref, d_ref, m2_ref, v2_ref):
        c = c_ref[...]
        g = lax.dot_general(c * jax.nn.sigmoid(c), dm_ref[...], (((0,), (0,)), ((), ())), precision=HI, preferred_element_type=F32)
        g_ref[...] = g
        d_ref[...], m2_ref[...], v2_ref[...] = _adamw(w_ref[...], g, m_ref[...], v_ref[...])

    blk = pl.BlockSpec((tm, tn), lambda i, j: (i, j))
    return pl.pallas_call(
        body, name="wada_update", grid=(D // tm, N // tn), out_shape=[jax.ShapeDtypeStruct((D, N), F32)] * 4,
        in_specs=[pl.BlockSpec((8, tm), lambda i, j: (0, i)), pl.BlockSpec((8, tn), lambda i, j: (0, j)), blk, blk, blk],
        out_specs=[blk] * 4, compiler_params=_cp("parallel", "parallel"),
    )(c_all, dmod, w, m, v)


def sum_devices(gathered, name):
    n, R, C = gathered.shape

    def body(g_ref, o_ref):
        s = g_ref[0]
        for i in range(1, n):
            s = s + g_ref[i]
        o_ref[...] = s

    return pl.pallas_call(body, name=name, out_shape=jax.ShapeDtypeStruct((R, C), F32), compiler_params=_cp())(gathered)


def lb_logits_grad(dlb, lb_logits):
    def body(d_ref, l_ref, o_ref):
        for d in range(2):
            l0, l1 = l_ref[d, 0:1, :], l_ref[d, 1:2, :]
            m = jnp.maximum(l0, l1)
            e0, e1 = jnp.exp(l0 - m), jnp.exp(l1 - m)
            p0, p1 = e0 / (e0 + e1), e1 / (e0 + e1)
            g = d_ref[d:d + 1, :]
            o_ref[d, 0:1, :] = p0 * (g - p0 * g)
            o_ref[d, 1:2, :] = -p1 * (p0 * g)

    return pl.pallas_call(body, name="lb_logits_grad", out_shape=jax.ShapeDtypeStruct(lb_logits.shape, F32), compiler_params=_cp())(dlb, lb_logits)


def add_halves(g, landed, core):
    nj, _, r, cc = g.shape
    tr = min(256, r)

    def body(core_ref, g_ref, l_ref, o_ref):
        o_ref[...] = (g_ref[...].astype(F32) + l_ref[...].astype(F32)).astype(BF16)

    return pl.pallas_call(
        body, name="add_halves_%dx%d" % (r, cc), out_shape=jax.ShapeDtypeStruct((nj, r, cc), BF16),
        grid_spec=pltpu.PrefetchScalarGridSpec(
            num_scalar_prefetch=1, grid=(nj, r // tr),
            in_specs=[pl.BlockSpec((None, None, tr, cc), lambda j, i, core_ref: (j, core_ref[0], i, 0)),
                      pl.BlockSpec((None, None, tr, cc), lambda j, i, core_ref: (j, 0, i, 0))],
            out_specs=pl.BlockSpec((None, tr, cc), lambda j, i, core_ref: (j, i, 0))),
        compiler_params=_cp("parallel", "parallel"),
    )(core, g, landed)


def sum_chips(parts, landed, chip):
    nj, r, cc = parts.shape
    tr = min(256, r)

    def body(chip_ref, p_ref, l_ref, o_ref):
        mine = p_ref[...].astype(F32)
        s = None
        for j in range(nj):
            t = jnp.where(chip_ref[0] == j, mine, l_ref[j].astype(F32))
            s = t if s is None else s + t
        o_ref[...] = s

    return pl.pallas_call(
        body, name="sum_chips_%dx%d" % (r, cc), out_shape=jax.ShapeDtypeStruct((r, cc), F32),
        grid_spec=pltpu.PrefetchScalarGridSpec(
            num_scalar_prefetch=1, grid=(r // tr,),
            in_specs=[pl.BlockSpec((None, tr, cc), lambda i, chip_ref: (chip_ref[0], i, 0)),
                      pl.BlockSpec((nj, tr, cc), lambda i, chip_ref: (0, i, 0))],
            out_specs=pl.BlockSpec((tr, cc), lambda i, chip_ref: (i, 0))),
        compiler_params=_cp("parallel"),
    )(chip, parts, landed)


def adamw_halves(w, own, other, m, v, core, name):
    r, cc = own.shape
    tr = min(128, r)
    nb = r // tr

    def body(core_ref, w_ref, a_ref, b_ref, m_ref, v_ref, g_ref, d_ref, m2_ref, v2_ref):
        g = jnp.where(pl.program_id(0) == core_ref[0], a_ref[...], b_ref[...])
        g_ref[...] = g
        d_ref[...], m2_ref[...], v2_ref[...] = _adamw(w_ref[...], g, m_ref[...], v_ref[...])

    full = pl.BlockSpec((tr, cc), lambda h, i, core_ref: (h * nb + i, 0))
    mine = pl.BlockSpec((tr, cc), lambda h, i, core_ref: (jnp.where(h == core_ref[0], i, 0), 0))
    theirs = pl.BlockSpec((tr, cc), lambda h, i, core_ref: (jnp.where(h == core_ref[0], 0, i), 0))
    return pl.pallas_call(
        body, name=name, out_shape=[jax.ShapeDtypeStruct((2 * r, cc), F32)] * 4,
        grid_spec=pltpu.PrefetchScalarGridSpec(
            num_scalar_prefetch=1, grid=(2, nb), in_specs=[full, mine, theirs, full, full], out_specs=[full] * 4),
        compiler_params=_cp("arbitrary", "arbitrary"),
    )(core, w, own, other, m, v)


def _place():
    mx, my, mc = lax.axis_index("x"), lax.axis_index("y"), lax.axis_index("c")
    chips = [(1 - mx, my), (mx, 1 - my), (1 - mx, 1 - my)]
    return mx, my, mc, chips


def all_gather_small(x, name):
    R, C = x.shape

    def body(x_ref, out_ref, send_sems, recv_sems, local_sem):
        mx, my, mc, _ = _place()
        me = 4 * mx + 2 * my + mc
        mine = pltpu.make_async_copy(x_ref, out_ref.at[me], local_sem)
        mine.start()

        def peer(k):
            px = 1 - mx if k & 4 else mx
            py = 1 - my if k & 2 else my
            pc = 1 - mc if k & 1 else mc
            return px, py, pc

        def copy(k, src, slot):
            return pltpu.make_async_remote_copy(src_ref=src, dst_ref=out_ref.at[slot], send_sem=send_sems.at[k - 1],
                                                recv_sem=recv_sems.at[k - 1], device_id=peer(k), device_id_type=MESH)

        sends = [copy(k, x_ref, me) for k in range(1, 8)]
        for cp in sends:
            cp.start()
        for k in range(1, 8):
            px, py, pc = peer(k)
            slot = 4 * px + 2 * py + pc
            copy(k, out_ref.at[slot], slot).wait_recv()
        for cp in sends:
            cp.wait_send()
        mine.wait()

    return pl.pallas_call(
        body, name=name, out_shape=jax.ShapeDtypeStruct((8, R, C), F32),
        in_specs=[pl.BlockSpec(memory_space=pltpu.VMEM)], out_specs=pl.BlockSpec(memory_space=pltpu.VMEM),
        scratch_shapes=[pltpu.SemaphoreType.DMA((7,)), pltpu.SemaphoreType.DMA((7,)), pltpu.SemaphoreType.DMA],
        compiler_params=_cp(),
    )(x)


def gather8_comm(x):
    def copies(x_ref, out_ref, send_sems, recv_sems):
        mx, my, mc, _ = _place()
        me = 4 * mx + 2 * my + mc

        def peer(k):
            return (1 - mx if k & 4 else mx, 1 - my if k & 2 else my, 1 - mc if k & 1 else mc)

        def copy(k, src, slot):
            return pltpu.make_async_remote_copy(src_ref=src, dst_ref=out_ref.at[slot], send_sem=send_sems.at[k - 1],
                                                recv_sem=recv_sems.at[k - 1], device_id=peer(k), device_id_type=MESH)

        sends = [copy(k, x_ref, me) for k in range(1, 8)]
        arrivals = []
        for k in range(1, 8):
            px, py, pc = peer(k)
            slot = 4 * px + 2 * py + pc
            arrivals.append(copy(k, out_ref.at[slot], slot))
        return sends, arrivals, pltpu.make_async_copy(x_ref, out_ref.at[me], send_sems.at[7])

    def start(cin, cout, send_sems, recv_sems):
        sends, _, mine = copies(cin[0], cout[0], send_sems, recv_sems)
        mine.start()
        for cp in sends:
            cp.start()

    def finish(cin, cout, send_sems, recv_sems):
        sends, arrivals, mine = copies(cin[0], cout[0], send_sems, recv_sems)
        for cp in arrivals:
            cp.wait_recv()
        for cp in sends:
            cp.wait_send()
        mine.wait()

    return _Comm([x], [jax.ShapeDtypeStruct((8,) + x.shape, F32)], {}, 8, start, finish)


def _join(a, b):
    na_in, na_out = len(a.operands), len(a.out_shape)

    def split(fn_a, fn_b):
        def both(cin, cout, send_sems, recv_sems):
            fn_a(cin[:na_in], cout[:na_out], send_sems.at[pl.ds(0, a.n_sems)], recv_sems.at[pl.ds(0, a.n_sems)])
            fn_b(cin[na_in:], cout[na_out:], send_sems.at[pl.ds(a.n_sems, b.n_sems)], recv_sems.at[pl.ds(a.n_sems, b.n_sems)])
        return both

    aliases = dict(a.aliases)
    aliases.update({na_in + i: na_out + o for i, o in b.aliases.items()})
    return _Comm(a.operands + b.operands, a.out_shape + b.out_shape, aliases, a.n_sems + b.n_sems, split(a.start, b.start), split(a.finish, b.finish))


def _region(ref, kind, j, half, r, cc):
    nr = r if half is None else r // 2
    off = 0 if half is None else half * nr
    if kind == "col":
        return ref.at[pl.ds(off, nr), pl.ds(pl.multiple_of(j * cc, 128), cc)]
    return ref.at[pl.ds(pl.multiple_of(j * r + off, 16), nr), :]


def comm_call(comm, name):
    ni, no = len(comm.operands), len(comm.out_shape)

    def body(*refs):
        comm.start(refs[:ni], refs[ni:ni + no], *refs[ni + no:])
        comm.finish(refs[:ni], refs[ni:ni + no], *refs[ni + no:])

    return pl.pallas_call(
        body, name=name, out_shape=comm.out_shape, in_specs=[ANY] * ni, out_specs=[ANY] * no, input_output_aliases=comm.aliases,
        scratch_shapes=[pltpu.SemaphoreType.DMA((comm.n_sems,)), pltpu.SemaphoreType.DMA((comm.n_sems,))], compiler_params=_cp(),
    )(*comm.operands)


def gather_comm(fulls, kinds, dims):
    n = len(fulls)

    def copies(f_refs, send_sems, recv_sems):
        mx, my, mc, chips = _place()
        jme = 2 * mx + my

        def landed(w, k, half):
            px, py = chips[k]
            return _region(f_refs[w], kinds[w], 2 * px + py, half, *dims[w])

        def over_ici(w, k, reg):
            px, py = chips[k]
            return pltpu.make_async_remote_copy(src_ref=reg, dst_ref=reg, send_sem=send_sems.at[6 * w + k], recv_sem=recv_sems.at[6 * w + k],
                                                device_id=(px, py, mc), device_id_type=MESH)

        def over_d2d(w, k, half):
            reg = landed(w, k, half)
            return pltpu.make_async_remote_copy(src_ref=reg, dst_ref=reg, send_sem=send_sems.at[6 * w + 3 + k],
                                                recv_sem=recv_sems.at[6 * w + 3 + k], device_id=(mx, my, 1 - mc), device_id_type=MESH)

        sends = [over_ici(w, k, _region(f_refs[w], kinds[w], jme, mc, *dims[w])) for w in range(n) for k in range(3)]
        return mc, landed, over_ici, over_d2d, sends

    def start(cin, f_refs, send_sems, recv_sems):
        for cp in copies(f_refs, send_sems, recv_sems)[4]:
            cp.start()

    def finish(cin, f_refs, send_sems, recv_sems):
        mc, landed, over_ici, over_d2d, sends = copies(f_refs, send_sems, recv_sems)
        passed = []
        for w in range(n):
            for k in range(3):
                over_ici(w, k, landed(w, k, mc)).wait_recv()
                cp = over_d2d(w, k, mc)
                cp.start()
                passed.append(cp)
        for w in range(n):
            for k in range(3):
                over_d2d(w, k, 1 - mc).wait_recv()
        for cp in sends + passed:
            cp.wait_send()

    return _Comm(fulls, [jax.ShapeDtypeStruct(f.shape, BF16) for f in fulls], {w: w for w in range(n)}, 6 * n, start, finish)


def exchange_comm(grads):
    n = len(grads)

    def copies(g_refs, l_refs, send_sems, recv_sems):
        mx, my, mc, _ = _place()
        return [pltpu.make_async_remote_copy(src_ref=g_refs[w].at[:, pl.ds(1 - mc, 1)], dst_ref=l_refs[w], send_sem=send_sems.at[w],
                                             recv_sem=recv_sems.at[w], device_id=(mx, my, 1 - mc), device_id_type=MESH) for w in range(n)]

    def start(*refs):
        for cp in copies(*refs):
            cp.start()

    def finish(*refs):
        for cp in copies(*refs):
            cp.wait()

    return _Comm(grads, [jax.ShapeDtypeStruct((g.shape[0], 1) + g.shape[2:], BF16) for g in grads], {}, n, start, finish)


def exchange_halves(grads, name):
    return comm_call(exchange_comm(grads), name)


def scatter_comm(parts):
    n = len(parts)

    def sends(p_refs, l_refs, send_sems, recv_sems):
        mx, my, mc, chips = _place()
        return [pltpu.make_async_remote_copy(src_ref=p_refs[w].at[2 * px + py], dst_ref=l_refs[w].at[2 * mx + my],
                                             send_sem=send_sems.at[3 * w + k], recv_sem=recv_sems.at[3 * w + k],
                                             device_id=(px, py, mc), device_id_type=MESH) for w in range(n) for k, (px, py) in enumerate(chips)]

    def start(p_refs, l_refs, send_sems, recv_sems):
        for cp in sends(p_refs, l_refs, send_sems, recv_sems):
            cp.start()

    def finish(p_refs, l_refs, send_sems, recv_sems):
        mx, my, mc, chips = _place()
        for w in range(n):
            for k, (px, py) in enumerate(chips):
                slot = l_refs[w].at[2 * px + py]
                pltpu.make_async_remote_copy(src_ref=slot, dst_ref=slot, send_sem=send_sems.at[3 * w + k], recv_sem=recv_sems.at[3 * w + k],
                                             device_id=(px, py, mc), device_id_type=MESH).wait_recv()
        for cp in sends(p_refs, l_refs, send_sems, recv_sems):
            cp.wait_send()

    return _Comm(parts, [jax.ShapeDtypeStruct(p.shape, BF16) for p in parts], {}, 3 * n, start, finish)


def share_comm(sums):
    n = len(sums)

    def copies(q_refs, o_refs, send_sems, recv_sems):
        mx, my, mc, _ = _place()
        return [pltpu.make_async_remote_copy(src_ref=q_refs[w], dst_ref=o_refs[w], send_sem=send_sems.at[w], recv_sem=recv_sems.at[w],
                                             device_id=(mx, my, 1 - mc), device_id_type=MESH) for w in range(n)]

    def start(*refs):
        for cp in copies(*refs):
            cp.start()

    def finish(*refs):
        for cp in copies(*refs):
            cp.wait()

    return _Comm(sums, [jax.ShapeDtypeStruct(q.shape, F32) for q in sums], {}, n, start, finish)


def _pack(arrays):
    flat = jnp.concatenate([a.reshape(-1) for a in arrays])
    rows = -(-flat.shape[0] // 1024) * 8
    return jnp.pad(flat, (0, rows * 128 - flat.shape[0])).reshape(rows, 128)


def _unpack(packed, shapes):
    flat, out, off = packed.reshape(-1), [], 0
    for s in shapes:
        n = math.prod(s)
        out.append(flat[off:off + n].reshape(s))
        off += n
    return out


def kernel(x, c, w_ada, b_ada, g_pre_mix, g_post_mix, g_pre_ffn, g_post_ffn, w_in, lb_logits, g_hgrn_norm, w_a_out, g_sgu_norm, w_spatial, b_spatial, w_b_out, w_o, w_ff1, w_ff2, loss_target, m_w_ada, m_b_ada, m_g_pre_mix, m_g_post_mix, m_g_pre_ffn, m_g_post_ffn, m_w_in, m_lb_logits, m_g_hgrn_norm, m_w_a_out, m_g_sgu_norm, m_w_spatial, m_b_spatial, m_w_b_out, m_w_o, m_w_ff1, m_w_ff2, v_w_ada, v_b_ada, v_g_pre_mix, v_g_post_mix, v_g_pre_ffn, v_g_post_ffn, v_w_in, v_lb_logits, v_g_hgrn_norm, v_w_a_out, v_g_sgu_norm, v_w_spatial, v_b_spatial, v_w_b_out, v_w_o, v_w_ff1, v_w_ff2):
    mx, my, mc = lax.axis_index("x"), lax.axis_index("y"), lax.axis_index("c")
    chip, me = 2 * mx + my, 4 * mx + 2 * my + mc
    D = D_MODEL
    h0, tgt = x[0], loss_target[0]
    n_ada = w_ada.shape[2]
    n_lb = lb_logits.shape[2]

    got = all_gather_small(_pack([c, lb_logits]), "gather_inputs")
    c_all = got[:, :D // 128, :].reshape(8, D)
    lb_full = got[0::2, D // 128:D // 128 + 4 * n_lb // 128, :].reshape(4, 2, 2, n_lb).transpose(1, 2, 0, 3).reshape(2, 2, 4 * n_lb)
    b_ada_chip = lax.dynamic_slice(b_ada, (0, chip * n_ada), (1, n_ada))
    mod_cols = mod_matmul(c_all, w_ada[0], b_ada_chip)
    got = all_gather_small(mod_cols.reshape(-1, 128), "gather_mod").reshape(4, 2, 8, n_ada)
    mod = lax.dynamic_index_in_dim(got[:, 0], me, axis=1, keepdims=False).reshape(6, 1, D)
    sh1, sc1, gt1, sh2, sc2, gt2 = (mod[i] for i in range(6))

    big = [("w_in", w_in, "col"), ("w_a_out", w_a_out, "col"), ("w_b_out", w_b_out, "col"), ("w_o", w_o, "row"),
           ("w_ff1", w_ff1, "col"), ("w_ff2", w_ff2, "row")]
    kinds = [k for _, _, k in big]
    chip_idx, core = chip.reshape(1).astype(jnp.int32), mc.reshape(1).astype(jnp.int32)
    fulls = [cast_into_full(w[0], kind, chip_idx, "cast_" + nm) for nm, w, kind in big]
    dims = [w.shape[1:] for _, w, _ in big]
    later = lambda lo, hi: gather_comm(fulls[lo:hi], kinds[lo:hi], dims[lo:hi])
    halves_summed = lambda grads, name: [add_halves(g, l, core) for g, l in zip(grads, exchange_halves(grads, name))]

    bst = b_spatial[0].T
    a1 = prenorm(h0, g_pre_mix, sc1, sh1)
    proj, w_in_f, (w_a_f, w_b_f, w_o_f) = in_proj_gathered(a1, fulls[0], chip_idx, dims[0], later(1, 4))
    o, (w_ff1_f,) = hgrn_fwd(proj, lb_full, comm=later(4, 5))
    ya_pre = hgrn_post_fwd(o, proj, g_hgrn_norm)
    sgu = sgu_fwd(proj, g_sgu_norm, w_spatial[0], bst)
    y_a, y_b, merged = merge_matmul(ya_pre, sgu, w_a_f, w_b_f, proj)
    mo, h1, a2 = out_proj(merged, w_o_f, h0, gt1, g_post_mix, g_pre_ffn, sc2, sh2)
    (f1, hid), (w_ff2_f,) = matmul(a2, w_ff1_f, mode="nn", out_dtype=BF16, tm=1024, tn=1024, tk=2048, name="ff1", relu2=True,
                                   comm=later(5, 6))
    ff = matmul(hid, w_ff2_f, mode="nn", out_dtype=F32, tm=1024, tn=1024, tk=2048, name="ff2")
    dy, dff, loss_parts, d_gt2, d_g_post_ffn = loss_bwd(ff, h1, tgt, gt2, g_post_ffn)
    loss = lax.psum(0.5 * loss_parts[0, 0] / D, ("x", "y", "c"))

    df1 = ff2_bwd(dff, w_ff2_f, f1)
    gr_ff2 = matmul(hid, dff, mode="tn", out_dtype=BF16, tm=1024, tn=1024, tk=2048, name="dw_ff2")
    gr_ff2 = gr_ff2.reshape(4, 2, -1, D)
    da2, (landed_ff2,) = matmul(df1, w_ff1_f, mode="nt", out_dtype=F32, tm=1024, tn=1024, tk=2048, name="da2", comm=exchange_comm([gr_ff2]))
    gr_ff1 = matmul(a2, df1, mode="tn", out_dtype=BF16, tm=1024, tn=2048, tk=1024, name="dw_ff1", split=(4, 2))
    (dh1, dmo, d_sh2, d_sc2, d_g_pre_ffn, d_gt1, d_g_post_mix), (landed_ff1,) = ffn_norm_bwd(
        dy, da2, h1, mo, g_pre_ffn, sc2, gt1, g_post_mix, exchange_comm([gr_ff1]))
    parts_ff = [add_halves(gr_ff1, landed_ff1, core), add_halves(gr_ff2, landed_ff2, core)]
    dya, dyb, dga, dgb = out_proj_bwd(dmo, w_o_f, y_a, y_b, proj)
    gr_o = matmul(merged, dmo, mode="tn", out_dtype=BF16, tm=1024, tn=1024, tk=2048, name="dw_o")
    dsgu = matmul(dyb, w_b_f, mode="nt", out_dtype=F32, tm=512, tn=1024, tk=2048, name="dsgu")
    gr_b = matmul(sgu, dyb, mode="tn", out_dtype=BF16, tm=512, tn=512, tk=4096, name="dw_b_out", split=(4, 2))
    dz, d_w_spatial, d_b_spatial, d_g_sgu = sgu_bwd(proj, dsgu, g_sgu_norm, w_spatial[0], bst)
    dya_pre = matmul(dya, w_a_f, mode="nt", out_dtype=F32, tm=512, tn=1024, tk=2048, name="dya_pre")
    gr_a = matmul(ya_pre, dya, mode="tn", out_dtype=BF16, tm=512, tn=512, tk=4096, name="dw_a_out", split=(4, 2))
    gr_mix = [gr_a, gr_b, gr_o.reshape(4, 2, -1, D)]
    (do, dog, d_g_hgrn), landed_halves = hgrn_post_bwd(dya_pre, o, proj, g_hgrn_norm, exchange_comm(gr_mix))
    parts_mix = [add_halves(g, l, core) for g, l in zip(gr_mix, landed_halves)]
    chips_summed = lambda parts, landed: [sum_chips(p, l, chip_idx) for p, l in zip(parts, landed)]
    (dq, dv, dlg, d_lb), landed_ff = hgrn_bwd(proj, do, lb_full, comm=scatter_comm(parts_ff))
    own_ff = chips_summed(parts_ff, landed_ff)
    dproj = jnp.concatenate([dq, dlg, dv, dog, dz, dga, dgb], axis=1)
    early = _pack([d_g_sgu, d_w_spatial, d_b_spatial[:, 0, :]])
    gr_in, (*landed_mix, got_early) = matmul(a1, dproj, mode="tn", out_dtype=BF16, tm=1024, tn=2816, tk=1024, name="dw_in", split=(4, 2),
                                             comm=_join(scatter_comm(parts_mix), gather8_comm(early)))
    own_mix = chips_summed(parts_mix, landed_mix)
    parts_in = halves_summed([gr_in], "exchange_in")
    da1, (landed_in, *other_rest) = matmul(dproj, w_in_f, mode="nt", out_dtype=F32, tm=1024, tn=1024, tk=2816, name="da1",
                                           comm=_join(scatter_comm(parts_in), share_comm(own_mix + own_ff)))
    own_in = chips_summed(parts_in, [landed_in])
    other_in = comm_call(share_comm(own_in), "share_w_in")
    own, other = own_in + own_mix + own_ff, list(other_in) + other_rest
    grad_x, d_sh1, d_sc1, d_g_pre_mix = mix_norm_bwd(da1, h0, dh1, g_pre_mix, sc1)
    out = {}

    mine = _pack([d_sh1, d_sc1, d_gt1, d_sh2, d_sc2, d_gt2, d_g_pre_mix, d_g_post_mix, d_g_pre_ffn, d_g_post_ffn, d_g_hgrn, d_lb])
    got = all_gather_small(mine, "gather_small_grads")
    g_b_ada, g_g1, g_g2, g_g3, g_g4, g_hg, g_lb = _unpack(
        sum_devices(got, "sum_small_grads"), [(1, 6 * D), (1, D), (1, D), (1, D), (1, D), (1, HEAD_DIM), (2, 1024)])
    g_sg, g_ws, g_bs = _unpack(sum_devices(got_early, "sum_sgu_grads"), [(1, 1024), w_spatial.shape, b_spatial.shape])
    g_lbl = lax.dynamic_slice(lb_logits_grad(g_lb, lb_full), (0, 0, chip * n_lb), (2, 2, n_lb))
    names = ["b_ada", "g_pre_mix", "g_post_mix", "g_pre_ffn", "g_post_ffn", "g_hgrn_norm", "g_sgu_norm", "w_spatial", "b_spatial", "lb_logits"]
    ws = [b_ada, g_pre_mix, g_post_mix, g_pre_ffn, g_post_ffn, g_hgrn_norm, g_sgu_norm, w_spatial, b_spatial, lb_logits]
    gs = [g_b_ada, g_g1, g_g2, g_g3, g_g4, g_hg, g_sg, g_ws, g_bs, g_lbl]
    ms = [m_b_ada, m_g_pre_mix, m_g_post_mix, m_g_pre_ffn, m_g_post_ffn, m_g_hgrn_norm, m_g_sgu_norm, m_w_spatial, m_b_spatial, m_lb_logits]
    vs = [v_b_ada, v_g_pre_mix, v_g_post_mix, v_g_pre_ffn, v_g_post_ffn, v_g_hgrn_norm, v_g_sgu_norm, v_w_spatial, v_b_spatial, v_lb_logits]
    shapes = [w.shape for w in ws]
    upd = adamw(_pack(ws), _pack(gs), _pack(ms), _pack(vs), "adamw_small")
    upd = [_unpack(u, shapes) for u in upd]
    for i, nm in enumerate(names):
        out[nm] = (gs[i], upd[0][i], upd[1][i], upd[2][i])

    dmod_all = got[:, :6 * D // 128, :].reshape(8, 6 * D)
    dmod_chip = lax.dynamic_slice(dmod_all, (0, chip * n_ada), (8, n_ada))
    out["w_ada"] = tuple(a[None] for a in wada_update(c_all, dmod_chip, w_ada[0], m_w_ada[0], v_w_ada[0]))
    for (nm, w, _), a, b, m, v in zip(big, own, other, (m_w_in, m_w_a_out, m_w_b_out, m_w_o, m_w_ff1, m_w_ff2),
                                      (v_w_in, v_w_a_out, v_w_b_out, v_w_o, v_w_ff1, v_w_ff2)):
        out[nm] = tuple(t[None] for t in adamw_halves(w[0], a, b, m[0], v[0], core, "adamw_" + nm))

    order = ["w_ada", "b_ada", "g_pre_mix", "g_post_mix", "g_pre_ffn", "g_post_ffn", "w_in", "lb_logits", "g_hgrn_norm", "w_a_out",
             "g_sgu_norm", "w_spatial", "b_spatial", "w_b_out", "w_o", "w_ff1", "w_ff2"]
    return (loss, grad_x[None], *[out[nm][0] for nm in order], *[out[nm][1] for nm in order], *[out[nm][2] for nm in order],
            *[out[nm][3] for nm in order])
```

```python
import functools
import math

import jax
import jax.numpy as jnp
from jax import lax
from jax.experimental import pallas as pl
from jax.experimental.pallas import tpu as pltpu

F32, BF16 = jnp.float32, jnp.bfloat16
HI = lax.Precision.HIGHEST
MESH = pl.DeviceIdType.MESH
ANY = pl.BlockSpec(memory_space=pl.ANY)

EPS = 1e-6
D_MODEL = 2048
N_HEADS = 8
HEAD_DIM = 128
HGRN_CHUNK = 32
HGRN_BLOCK = 256
HGRN_BLOCK_FWD = 512
SGU_CHUNK = 128
SGU_GROUPS = 8
Q_SCALE = HEAD_DIM ** -0.5
COL_Q, COL_FFW, COL_FBW, COL_V, COL_OG, COL_U, COL_ZV, COL_GA, COL_GB = 0, 1, 2, 3, 4, 5, 6, 7, 9
N_PROJ = 11264
VMEM_BYTES_V7X = 64 * 1024 * 1024
VMEM_LIMIT = VMEM_BYTES_V7X - 8 * 1024 * 1024

ADAM_LR, ADAM_B1, ADAM_B2, ADAM_EPS, ADAM_WD, ADAM_STEP = 0.001, 0.9, 0.999, 1e-08, 0.01, 10
ADAM_C1 = 1.0 - ADAM_B1 ** ADAM_STEP
ADAM_C2 = 1.0 - ADAM_B2 ** ADAM_STEP


def _cp(*sem):
    return pltpu.CompilerParams(dimension_semantics=sem if sem else None, vmem_limit_bytes=VMEM_LIMIT)


def _vec(d):
    return pl.BlockSpec((1, d), lambda *_: (0, 0))


def _colsum(x):
    return jnp.sum(x, axis=0, keepdims=True)


def _nt(a, b):
    return lax.dot_general(a, b, (((1,), (1,)), ((), ())), preferred_element_type=F32)


def _tn(a, b):
    return lax.dot_general(a, b, (((0,), (0,)), ((), ())), preferred_element_type=F32)


def _nn(a, b):
    return jnp.dot(a, b, preferred_element_type=F32)


def _adamw(w, g, m, v):
    m2 = ADAM_B1 * m + (1.0 - ADAM_B1) * g
    v2 = ADAM_B2 * v + (1.0 - ADAM_B2) * (g * g)
    delta = -ADAM_LR * ((m2 / ADAM_C1) / (jnp.sqrt(v2 / ADAM_C2) + ADAM_EPS) + ADAM_WD * w)
    return delta, m2, v2


class _Comm:
    def __init__(self, operands, out_shape, aliases, n_sems, start, finish):
        self.operands, self.out_shape, self.aliases, self.n_sems = list(operands), list(out_shape), dict(aliases), n_sems
        self.start, self.finish = start, finish


def _pallas(body, *, name, grid, in_specs, out_specs, out_shape, scratch, semantics, operands, comm=None):
    if comm is None:
        res = pl.pallas_call(body, name=name, grid=grid, in_specs=in_specs, out_specs=out_specs, out_shape=out_shape,
                             scratch_shapes=scratch, compiler_params=_cp(*semantics))(*operands)
        return res, []
    n_in, n_out, n_scr = len(in_specs), len(out_specs), len(scratch)
    nci, nco = len(comm.operands), len(comm.out_shape)

    def with_comm(*refs):
        ins, rest = refs[:n_in], refs[n_in:]
        cin, rest = rest[:nci], rest[nci:]
        outs, rest = rest[:n_out], rest[n_out:]
        cout, rest = rest[:nco], rest[nco:]
        scr, (send, recv) = rest[:n_scr], rest[n_scr:]
        ids = [pl.program_id(a) for a in range(len(grid))]
        first = functools.reduce(jnp.logical_and, [i == 0 for i in ids])
        last = functools.reduce(jnp.logical_and, [i == g - 1 for i, g in zip(ids, grid)])

        @pl.when(first)
        def _():
            comm.start(cin, cout, send, recv)

        body(*ins, *outs, *scr)

        @pl.when(last)
        def _():
            comm.finish(cin, cout, send, recv)

    res = pl.pallas_call(
        with_comm, name=name, grid=grid, in_specs=list(in_specs) + [ANY] * nci, out_specs=list(out_specs) + [ANY] * nco,
        out_shape=list(out_shape) + comm.out_shape, input_output_aliases={n_in + i: n_out + o for i, o in comm.aliases.items()},
        scratch_shapes=list(scratch) + [pltpu.SemaphoreType.DMA((comm.n_sems,)), pltpu.SemaphoreType.DMA((comm.n_sems,))],
        compiler_params=_cp(*["arbitrary"] * len(grid)),
    )(*operands, *comm.operands)
    return res[:n_out], res[n_out:]


def matmul(a, b, *, mode, out_dtype, tm, tn, tk, name, split=None, comm=None, relu2=False):
    if mode == "tn":
        (K, M), (_, N) = a.shape, b.shape
    elif mode == "nt":
        (M, K), (N, _) = a.shape, b.shape
    else:
        (M, K), (_, N) = a.shape, b.shape
    tm, tn, tk = min(tm, M), min(tn, N), min(tk, K)
    nk = K // tk
    a_spec = pl.BlockSpec((tk, tm), lambda i, j, k: (k, i)) if mode == "tn" else pl.BlockSpec((tm, tk), lambda i, j, k: (i, k))
    b_spec = pl.BlockSpec((tn, tk), lambda i, j, k: (j, k)) if mode == "nt" else pl.BlockSpec((tk, tn), lambda i, j, k: (k, j))
    dot = {"nn": _nn, "nt": _nt, "tn": _tn}[mode]
    if split is None:
        out_shape = jax.ShapeDtypeStruct((M, N), out_dtype)
        out_spec = pl.BlockSpec((tm, tn), lambda i, j, k: (i, j))
    else:
        nj, nh = split
        rows, cols = M // nh, N // nj
        tm, tn = min(tm, rows), min(tn, cols)
        bi, bj = rows // tm, cols // tn
        out_shape = jax.ShapeDtypeStruct((nj, nh, rows, cols), out_dtype)
        out_spec = pl.BlockSpec((None, None, tm, tn), lambda i, j, k: (j // bj, i // bi, i % bi, j % bj))

    def finish(y, o_ref, sq_ref):
        o_ref[...] = y.astype(o_ref.dtype)
        if relu2:
            p = jnp.maximum(y, 0.0)
            sq_ref[0][...] = (p * p).astype(BF16)

    if nk == 1:
        def body(a_ref, b_ref, o_ref, *sq_ref):
            finish(dot(a_ref[...], b_ref[...]), o_ref, sq_ref)
        scratch = []
    else:
        def body(a_ref, b_ref, o_ref, *rest):
            acc_ref, k = rest[-1], pl.program_id(2)

            @pl.when(k == 0)
            def _():
                acc_ref[...] = jnp.zeros_like(acc_ref)

            acc_ref[...] += dot(a_ref[...], b_ref[...])

            @pl.when(k == nk - 1)
            def _():
                finish(acc_ref[...], o_ref, rest[:-1])
        scratch = [pltpu.VMEM((tm, tn), F32)]

    out_specs, out_shapes = [out_spec], [out_shape]
    if relu2:
        out_specs, out_shapes = out_specs + [out_spec], out_shapes + [jax.ShapeDtypeStruct(out_shape.shape, BF16)]
    outs, landed = _pallas(
        body, name=name, grid=(M // tm, N // tn, nk), in_specs=[a_spec, b_spec], out_specs=out_specs, out_shape=out_shapes,
        scratch=scratch, semantics=("parallel", "parallel", "arbitrary"), operands=(a, b), comm=comm)
    out = tuple(outs) if relu2 else outs[0]
    return out if comm is None else (out, landed)


def cast_into_full(w, kind, chip, name):
    r, cc = w.shape
    tr = min(r, 512)
    nb = r // tr

    def body(chip_ref, w_ref, o_ref):
        o_ref[...] = w_ref[...].astype(BF16)

    if kind == "col":
        full, out_map = (r, 4 * cc), lambda i, chip_ref: (i, chip_ref[0])
    else:
        full, out_map = (4 * r, cc), lambda i, chip_ref: (chip_ref[0] * nb + i, 0)
    return pl.pallas_call(
        body, name=name, out_shape=jax.ShapeDtypeStruct(full, BF16),
        grid_spec=pltpu.PrefetchScalarGridSpec(
            num_scalar_prefetch=1, grid=(nb,), in_specs=[pl.BlockSpec((tr, cc), lambda i, chip_ref: (i, 0))],
            out_specs=pl.BlockSpec((tr, cc), out_map)),
        compiler_params=_cp("parallel"),
    )(chip, w)


def mod_matmul(c_all, w_ada, b_ada):
    D, N = w_ada.shape
    tn = 1024

    def body(c_ref, w_ref, b_ref, o_ref):
        c = c_ref[...]
        sc = c * jax.nn.sigmoid(c)
        o_ref[...] = jnp.dot(sc, w_ref[...], precision=HI, preferred_element_type=F32) + b_ref[...]

    return pl.pallas_call(
        body, name="mod_matmul", out_shape=jax.ShapeDtypeStruct((8, N), F32), grid=(N // tn,),
        in_specs=[pl.BlockSpec((8, D), lambda j: (0, 0)), pl.BlockSpec((D, tn), lambda j: (0, j)),
                  pl.BlockSpec((1, tn), lambda j: (0, j))],
        out_specs=pl.BlockSpec((8, tn), lambda j: (0, j)), compiler_params=_cp("parallel"),
    )(c_all, w_ada, b_ada)


def prenorm(h, g, sc, sh):
    T, D = h.shape
    tm = min(256, T)

    def body(h_ref, g_ref, sc_ref, sh_ref, a_ref):
        x = h_ref[...]
        r = lax.rsqrt(jnp.mean(x * x, axis=-1, keepdims=True) + EPS)
        a_ref[...] = ((x * r) * g_ref[...] * (1.0 + sc_ref[...]) + sh_ref[...]).astype(BF16)

    row = pl.BlockSpec((tm, D), lambda i: (i, 0))
    return pl.pallas_call(
        body, name="prenorm", out_shape=jax.ShapeDtypeStruct((T, D), BF16), grid=(T // tm,),
        in_specs=[row, _vec(D), _vec(D), _vec(D)], out_specs=row, compiler_params=_cp("parallel"),
    )(h, g, sc, sh)


def in_proj_gathered(a, w_full, chip, dims, tail):
    T, D = a.shape
    rows, cc = dims
    tm, tn = min(512, T), cc // 2
    ni = T // tm
    half = rows // 2

    nt = len(tail.operands)

    def body(chip_ref, a_ref, w_in_ref, *rest):
        tail_in, (y_ref, w_ref), rest = rest[:nt], rest[nt:nt + 2], rest[nt + 2:]
        tail_out, (wbuf, wsem, send_sems, recv_sems, tail_send, tail_recv) = rest[:nt], rest[nt:]
        q, j, i = pl.program_id(0), pl.program_id(1), pl.program_id(2)
        mx, my, mc, _ = _place()
        me = chip_ref[0]

        def tile(block, jj):
            src = w_ref.at[:, pl.ds(pl.multiple_of(block * cc + jj * tn, 128), tn)]
            return pltpu.make_async_copy(src, wbuf.at[jj], wsem.at[jj])

        def rows_half(block, hh):
            return w_ref.at[pl.ds(pl.multiple_of(hh * half, 16), half), pl.ds(pl.multiple_of(block * cc, 128), cc)]

        def over_ici(s, block):
            peer = (1 - mx if s & 2 else mx, 1 - my if s & 1 else my, mc)
            reg = rows_half(block, mc)
            return pltpu.make_async_remote_copy(src_ref=reg, dst_ref=reg, send_sem=send_sems.at[s - 1], recv_sem=recv_sems.at[s - 1],
                                                device_id=peer, device_id_type=MESH)

        def over_d2d(s, block, hh):
            reg = rows_half(block, hh)
            return pltpu.make_async_remote_copy(src_ref=reg, dst_ref=reg, send_sem=send_sems.at[2 + s], recv_sem=recv_sems.at[2 + s],
                                                device_id=(mx, my, 1 - mc), device_id_type=MESH)

        def passed_on(s, block):
            k = s - 1
            reg = w_ref.at[pl.ds(pl.multiple_of(mc * half + k * (half // 2), 16), half // 2), pl.ds(pl.multiple_of(block * cc, 128), cc)]
            peer = (1 - mx, my, mc) if s == 1 else (mx, 1 - my, mc)
            return pltpu.make_async_remote_copy(src_ref=reg, dst_ref=reg, send_sem=send_sems.at[6 + k], recv_sem=recv_sems.at[6 + k],
                                                device_id=peer, device_id_type=MESH)

        @pl.when((q == 0) & (j == 0) & (i == 0))
        def _():
            for s in (1, 2):
                over_ici(s, me).start()
            tile(me, 0).start()

        @pl.when(i == 0)
        def _():
            tile(me ^ q, j).wait()

        @pl.when((i == 0) & (j == 0))
        def _():
            tile(me ^ q, 1).start()

        y_ref[...] = _nn(a_ref[...], wbuf[j])

        @pl.when((q == 0) & (j == 1) & (i == ni - 1))
        def _():
            for s in (1, 2):
                over_ici(s, me ^ s).wait_recv()
                passed_on(s, me ^ s).start()
                over_d2d(s, me ^ s, mc).start()
            tail.start(tail_in, tail_out, tail_send, tail_recv)
            over_d2d(1, me ^ 1, 1 - mc).wait_recv()
            tile(me ^ 1, 0).start()

        @pl.when((q == 1) & (j == 1) & (i == ni - 1))
        def _():
            over_d2d(2, me ^ 2, 1 - mc).wait_recv()
            tile(me ^ 2, 0).start()

        @pl.when((q == 2) & (j == 1) & (i == ni - 1))
        def _():
            for s in (1, 2):
                passed_on(s, me ^ 3).wait_recv()
            over_d2d(3, me ^ 3, mc).start()
            over_d2d(3, me ^ 3, 1 - mc).wait_recv()
            tile(me ^ 3, 0).start()

        @pl.when((q == 3) & (j == 1) & (i == ni - 1))
        def _():
            for s in (1, 2):
                over_ici(s, me).wait_send()
                passed_on(s, me ^ s).wait_send()
            for s in (1, 2, 3):
                over_d2d(s, me ^ s, mc).wait_send()
            tail.finish(tail_in, tail_out, tail_send, tail_recv)

    dma = pltpu.SemaphoreType.DMA
    y, w_out, *tail_res = pl.pallas_call(
        body, name="in_proj", out_shape=[jax.ShapeDtypeStruct((T, 4 * cc), F32), jax.ShapeDtypeStruct(w_full.shape, BF16)] + tail.out_shape,
        grid_spec=pltpu.PrefetchScalarGridSpec(
            num_scalar_prefetch=1, grid=(4, 2, ni),
            in_specs=[pl.BlockSpec((tm, D), lambda q, j, i, chip_ref: (i, 0)), ANY] + [ANY] * nt,
            out_specs=[pl.BlockSpec((tm, tn), lambda q, j, i, chip_ref: (i, (chip_ref[0] ^ q) * 2 + j)), ANY] + [ANY] * nt,
            scratch_shapes=[pltpu.VMEM((2, D, tn), BF16), dma((2,)), dma((8,)), dma((8,)), dma((tail.n_sems,)), dma((tail.n_sems,))]),
        input_output_aliases={2: 1, **{3 + i: 2 + o for i, o in tail.aliases.items()}},
        compiler_params=_cp("arbitrary", "arbitrary", "arbitrary"),
    )(chip, a, w_full, *tail.operands)
    return y, w_out, tail_res


def _hgrn_lower_bound(l_ref):
    l0, l1 = l_ref[0:1, :], l_ref[1:2, :]
    m = jnp.maximum(l0, l1)
    e0, e1 = jnp.exp(l0 - m), jnp.exp(l1 - m)
    return e0 / (e0 + e1)


def _hgrn_chunk_mask(d, blk):
    r = lax.broadcasted_iota(jnp.int32, (blk, blk), 0)
    c = lax.broadcasted_iota(jnp.int32, (blk, blk), 1)
    same = (r // HGRN_CHUNK) == (c // HGRN_CHUNK)
    fwd = d == 0
    return same & (((c <= r) & fwd) | ((c >= r) & jnp.logical_not(fwd)))


def _chunk_total(x):
    x3 = x.reshape(x.shape[0] // HGRN_CHUNK, HGRN_CHUNK, x.shape[1])
    return jnp.broadcast_to(jnp.sum(x3, axis=1, keepdims=True), x3.shape).reshape(x.shape)


def _chunk_cumsum(x, suffix):
    pos = lax.broadcasted_iota(jnp.int32, x.shape, 0) % HGRN_CHUNK
    p, s = x, 1
    while s < HGRN_CHUNK:
        p = p + jnp.where(pos >= s, pltpu.roll(p, s, 0), 0.0)
        s *= 2
    return jnp.where(suffix, _chunk_total(x) - p + x, p)


def _block_loop(T, blk, body, init):
    n = T // blk
    return lax.fori_loop(0, n, body, init, unroll=2 if n % 2 == 0 else 1)


def _hgrn_gate(f, lb):
    s = jax.nn.sigmoid(f)
    sn = jax.nn.sigmoid(-f)
    fg = lb + (1.0 - lb) * s
    return s, sn, fg, jnp.log(fg), (1.0 - lb) * sn


def _hgrn_specs(T):
    col = lambda base: pl.BlockSpec((T, HEAD_DIM), lambda h, d: (0, base * N_HEADS + h))
    f_spec = pl.BlockSpec((T, HEAD_DIM), lambda h, d: (0, COL_FFW * N_HEADS + N_HEADS * d + h))
    l_spec = pl.BlockSpec((None, 2, HEAD_DIM), lambda h, d: (d, 0, h))
    return col, f_spec, l_spec


def hgrn_fwd(proj, lb_logits, comm=None):
    T = proj.shape[0]
    blk = min(HGRN_BLOCK_FWD, T)
    NC, CPB = T // HGRN_CHUNK, blk // HGRN_CHUNK
    col, f_spec, l_spec = _hgrn_specs(T)

    def body(l_ref, q_ref, f_ref, v_ref, o_ref, st_ref, dec_ref, qd_ref):
        d = pl.program_id(1)
        lb = _hgrn_lower_bound(l_ref)
        mask = _hgrn_chunk_mask(d, blk)

        def block(i, carry):
            rows = pl.ds(pl.multiple_of(i * blk, blk), blk)
            _, _, _, lf, k = _hgrn_gate(f_ref[rows, :], lb)
            b = _chunk_cumsum(lf, d == 1)
            bl = _chunk_total(lf)
            qd = (q_ref[rows, :] * Q_SCALE * jnp.exp(b)).astype(BF16)
            kd = (k * jnp.exp(-b)).astype(BF16)
            ke = (k * jnp.exp(bl - b)).astype(BF16)
            vb = v_ref[rows, :].astype(BF16)
            att = jnp.where(mask, _nt(qd, kd), 0.0).astype(BF16)
            o_ref[rows, :] = jnp.where(d == 0, 0.0, o_ref[rows, :]) + _nn(att, vb)
            qd_ref[rows, :] = qd
            dec = jnp.exp(bl)
            for cc in range(CPB):
                sl = slice(cc * HGRN_CHUNK, (cc + 1) * HGRN_CHUNK)
                n = i * CPB + cc
                st_ref[n] = _tn(vb[sl], ke[sl])
                dec_ref[n] = dec[cc * HGRN_CHUNK:cc * HGRN_CHUNK + 8, :]
            return carry

        _block_loop(T, blk, block, 0)

        def scan(t, s):
            n = jnp.where(d == 0, t, NC - 1 - t)
            u = st_ref[n]
            st_ref[n] = s
            return dec_ref[n][0:1, :] * s + u

        lax.fori_loop(0, NC, scan, jnp.zeros((HEAD_DIM, HEAD_DIM), F32))

        def inter(i, carry):
            rows = pl.ds(pl.multiple_of(i * blk, blk), blk)
            qd = qd_ref[rows, :]
            o_ref[rows, :] += jnp.concatenate(
                [_nt(qd[cc * HGRN_CHUNK:(cc + 1) * HGRN_CHUNK], st_ref[i * CPB + cc].astype(BF16)) for cc in range(CPB)], axis=0)
            return carry

        _block_loop(T, blk, inter, 0)

    (o,), landed = _pallas(
        body, name="hgrn_fwd", grid=(N_HEADS, 2), in_specs=[l_spec, col(COL_Q), f_spec, col(COL_V)],
        out_specs=[pl.BlockSpec((T, HEAD_DIM), lambda h, d: (0, h))], out_shape=[jax.ShapeDtypeStruct((T, N_HEADS * HEAD_DIM), F32)],
        scratch=[pltpu.VMEM((NC, HEAD_DIM, HEAD_DIM), F32), pltpu.VMEM((NC, 8, HEAD_DIM), F32), pltpu.VMEM((T, HEAD_DIM), BF16)],
        semantics=("parallel", "arbitrary"), operands=(lb_logits, proj, proj, proj), comm=comm)
    return o if comm is None else (o, landed)


def hgrn_post_fwd(o, proj, g_norm):
    T, W = o.shape
    tm = min(256, T)

    def body(o_ref, og_ref, g_ref, y_ref):
        g = g_ref[...]
        for h in range(N_HEADS):
            sl = slice(h * HEAD_DIM, (h + 1) * HEAD_DIM)
            x = o_ref[:, sl]
            r = lax.rsqrt(jnp.mean(x * x, axis=-1, keepdims=True) + EPS)
            og = og_ref[:, sl]
            y_ref[:, sl] = ((x * r) * g * (og * jax.nn.sigmoid(og))).astype(BF16)

    return pl.pallas_call(
        body, name="hgrn_post_fwd", out_shape=jax.ShapeDtypeStruct((T, W), BF16), grid=(T // tm,),
        in_specs=[pl.BlockSpec((tm, W), lambda i: (i, 0)), pl.BlockSpec((tm, W), lambda i: (i, COL_OG)), _vec(HEAD_DIM)],
        out_specs=pl.BlockSpec((tm, W), lambda i: (i, 0)), compiler_params=_cp("parallel"),
    )(o, proj, g_norm)


def _gelu(x):
    return 0.5 * x * (1.0 + lax.erf(x * (1.0 / math.sqrt(2.0))))


def _gelu_grad(x):
    return 0.5 * (1.0 + lax.erf(x * (1.0 / math.sqrt(2.0)))) + x * jnp.exp(-0.5 * x * x) * (1.0 / math.sqrt(2.0 * math.pi))


def _sgu_mix(u_ref, v_ref, g_ref, ws_ref, bst_ref):
    W = u_ref.shape[1]
    zu, zv = _gelu(u_ref[...]), _gelu(v_ref[...])
    dv = zv - jnp.mean(zv, axis=-1, keepdims=True)
    rstd = lax.rsqrt(jnp.mean(dv * dv, axis=-1, keepdims=True) + EPS)
    dhat = dv * rstd
    vn = (dhat * g_ref[...]).astype(BF16)
    gw = W // SGU_GROUPS
    vm = [_nn(ws_ref[g].astype(BF16), vn[:, g * gw:(g + 1) * gw]) + bst_ref[:, g:g + 1] for g in range(SGU_GROUPS)]
    return zu, rstd, dhat, vn, jnp.concatenate(vm, axis=1)


def sgu_fwd(proj, g_norm, w_spatial, b_spatial_t):
    T = proj.shape[0]
    W = 1024
    n_chunks = T // SGU_CHUNK

    def body(u_ref, v_ref, g_ref, ws_ref, bst_ref, y_ref):
        zu, _, _, _, vm = _sgu_mix(u_ref, v_ref, g_ref, ws_ref, bst_ref)
        y_ref[...] = (zu * vm).astype(BF16)

    blk = lambda cb: pl.BlockSpec((SGU_CHUNK, W), lambda i: (i, cb))
    return pl.pallas_call(
        body, name="sgu_fwd", out_shape=jax.ShapeDtypeStruct((T, W), BF16), grid=(n_chunks,),
        in_specs=[blk(COL_U), blk(COL_ZV), _vec(W), pl.BlockSpec((SGU_GROUPS, SGU_CHUNK, SGU_CHUNK), lambda i: (0, 0, 0)),
                  pl.BlockSpec((SGU_CHUNK, SGU_GROUPS), lambda i: (0, 0))],
        out_specs=blk(0), compiler_params=_cp("parallel"),
    )(proj, proj, g_norm, w_spatial, b_spatial_t)


def merge_matmul(ya_pre, sgu, w_a, w_b, proj):
    T, K = ya_pre.shape
    N = w_a.shape[1]
    tm, tn = min(512, T), 512
    gpb = 1024 // tn

    def body(a_ref, b_ref, wa_ref, wb_ref, ga_ref, gb_ref, ya_ref, yb_ref, m_ref):
        ya = _nn(a_ref[...], wa_ref[...])
        yb = _nn(b_ref[...], wb_ref[...])
        ya_ref[...] = ya.astype(BF16)
        yb_ref[...] = yb.astype(BF16)
        m_ref[...] = (jax.nn.sigmoid(ga_ref[...]) * ya + jax.nn.sigmoid(gb_ref[...]) * yb).astype(BF16)

    lhs = pl.BlockSpec((tm, K), lambda i, j: (i, 0))
    rhs = pl.BlockSpec((K, tn), lambda i, j: (0, j))
    out = pl.BlockSpec((tm, tn), lambda i, j: (i, j))
    return pl.pallas_call(
        body, name="merge_matmul", grid=(T // tm, N // tn),
        out_shape=[jax.ShapeDtypeStruct((T, N), BF16)] * 3,
        in_specs=[lhs, lhs, rhs, rhs, pl.BlockSpec((tm, tn), lambda i, j: (i, COL_GA * gpb + j)),
                  pl.BlockSpec((tm, tn), lambda i, j: (i, COL_GB * gpb + j))],
        out_specs=[out, out, out], compiler_params=_cp("parallel", "parallel"),
    )(ya_pre, sgu, w_a, w_b, proj, proj)


def out_proj(merged, w_o, h0, gt1, g_post, g_pre2, sc2, sh2):
    T, D = h0.shape
    tm = min(256, T)

    def body(m_ref, w_ref, h_ref, gt_ref, gp_ref, g2_ref, sc_ref, sh_ref, mo_ref, h1_ref, a2_ref):
        mo = _nn(m_ref[...], w_ref[...])
        mo_ref[...] = mo
        r = lax.rsqrt(jnp.mean(mo * mo, axis=-1, keepdims=True) + EPS)
        h1 = h_ref[...] + gt_ref[...] * ((mo * r) * gp_ref[...])
        h1_ref[...] = h1
        r2 = lax.rsqrt(jnp.mean(h1 * h1, axis=-1, keepdims=True) + EPS)
        a2_ref[...] = ((h1 * r2) * g2_ref[...] * (1.0 + sc_ref[...]) + sh_ref[...]).astype(BF16)

    row = pl.BlockSpec((tm, D), lambda i: (i, 0))
    return pl.pallas_call(
        body, name="out_proj", grid=(T // tm,),
        out_shape=[jax.ShapeDtypeStruct((T, D), F32), jax.ShapeDtypeStruct((T, D), F32), jax.ShapeDtypeStruct((T, D), BF16)],
        in_specs=[row, pl.BlockSpec((D, D), lambda i: (0, 0)), row] + [_vec(D)] * 5,
        out_specs=[row, row, row], compiler_params=_cp("parallel"),
    )(merged, w_o, h0, gt1, g_post, g_pre2, sc2, sh2)


def loss_bwd(ff, h1, tgt, gt2, g_post):
    T, D = ff.shape
    tm = min(256, T)

    def body(f_ref, h_ref, t_ref, gt_ref, g_ref, dy_ref, dff_ref, loss_ref, dgt_ref, dg_ref):
        @pl.when(pl.program_id(0) == 0)
        def _():
            loss_ref[...] = jnp.zeros_like(loss_ref)
            dgt_ref[...] = jnp.zeros_like(dgt_ref)
            dg_ref[...] = jnp.zeros_like(dg_ref)

        ff = f_ref[...]
        gt, g = gt_ref[...], g_ref[...]
        r = lax.rsqrt(jnp.mean(ff * ff, axis=-1, keepdims=True) + EPS)
        fhat = ff * r
        nf = fhat * g
        err = (h_ref[...] + gt * nf) - t_ref[...]
        loss_ref[...] += jnp.sum(err * err)
        dy = err * (1.0 / D)
        dy_ref[...] = dy
        dgt_ref[...] += _colsum(dy * nf)
        dnf = dy * gt
        dg_ref[...] += _colsum(dnf * fhat)
        u = dnf * g
        dff_ref[...] = (r * (u - fhat * jnp.mean(u * fhat, axis=-1, keepdims=True))).astype(BF16)

    row = pl.BlockSpec((tm, D), lambda i: (i, 0))
    return pl.pallas_call(
        body, name="loss_bwd", grid=(T // tm,),
        out_shape=[jax.ShapeDtypeStruct((T, D), F32), jax.ShapeDtypeStruct((T, D), BF16), jax.ShapeDtypeStruct((8, 128), F32),
                   jax.ShapeDtypeStruct((1, D), F32), jax.ShapeDtypeStruct((1, D), F32)],
        in_specs=[row, row, row, _vec(D), _vec(D)],
        out_specs=[row, row, pl.BlockSpec((8, 128), lambda i: (0, 0)), _vec(D), _vec(D)],
        compiler_params=_cp("arbitrary"),
    )(ff, h1, tgt, gt2, g_post)


def ff2_bwd(dff, w_ff2, f1):
    T, D = dff.shape
    K = w_ff2.shape[0]
    tm, tn = min(512, T), 2048

    def body(a_ref, w_ref, f_ref, o_ref):
        o_ref[...] = (_nt(a_ref[...], w_ref[...]) * (2.0 * jnp.maximum(f_ref[...].astype(F32), 0.0))).astype(BF16)

    return pl.pallas_call(
        body, name="ff2_bwd", out_shape=jax.ShapeDtypeStruct((T, K), BF16), grid=(K // tn, T // tm),
        in_specs=[pl.BlockSpec((tm, D), lambda j, i: (i, 0)), pl.BlockSpec((tn, D), lambda j, i: (j, 0)),
                  pl.BlockSpec((tm, tn), lambda j, i: (i, j))],
        out_specs=pl.BlockSpec((tm, tn), lambda j, i: (i, j)), compiler_params=_cp("parallel", "parallel"),
    )(dff, w_ff2, f1)


def ffn_norm_bwd(dy, da2, h1, mo, g_pre2, sc2, gt1, g_post, comm):
    T, D = dy.shape
    tm = min(256, T)

    def body(dy_ref, da_ref, h_ref, mo_ref, g2_ref, sc_ref, gt_ref, gp_ref, dh_ref, dmo_ref, s_sh, s_sc, s_g2, s_gt, s_gp):
        @pl.when(pl.program_id(0) == 0)
        def _():
            for s in (s_sh, s_sc, s_g2, s_gt, s_gp):
                s[...] = jnp.zeros_like(s)

        h1, da = h_ref[...], da_ref[...]
        g2, sc = g2_ref[...], sc_ref[...]
        r2 = lax.rsqrt(jnp.mean(h1 * h1, axis=-1, keepdims=True) + EPS)
        n2 = h1 * r2
        s_sh[...] += _colsum(da)
        s_sc[...] += _colsum(da * (n2 * g2))
        s_g2[...] += _colsum(da * (1.0 + sc) * n2)
        dn2 = da * g2 * (1.0 + sc)
        dh1 = dy_ref[...] + r2 * (dn2 - n2 * jnp.mean(dn2 * n2, axis=-1, keepdims=True))
        dh_ref[...] = dh1
        mo = mo_ref[...]
        gt, gp = gt_ref[...], gp_ref[...]
        r = lax.rsqrt(jnp.mean(mo * mo, axis=-1, keepdims=True) + EPS)
        mhat = mo * r
        s_gt[...] += _colsum(dh1 * (mhat * gp))
        dnm = dh1 * gt
        s_gp[...] += _colsum(dnm * mhat)
        u = dnm * gp
        dmo_ref[...] = (r * (u - mhat * jnp.mean(u * mhat, axis=-1, keepdims=True))).astype(BF16)

    row = pl.BlockSpec((tm, D), lambda i: (i, 0))
    vec_out = jax.ShapeDtypeStruct((1, D), F32)
    return _pallas(
        body, name="ffn_norm_bwd", grid=(T // tm,),
        out_shape=[jax.ShapeDtypeStruct((T, D), F32), jax.ShapeDtypeStruct((T, D), BF16)] + [vec_out] * 5,
        in_specs=[row, row, row, row] + [_vec(D)] * 4, out_specs=[row, row] + [_vec(D)] * 5,
        scratch=[], semantics=("arbitrary",), operands=(dy, da2, h1, mo, g_pre2, sc2, gt1, g_post), comm=comm)


def out_proj_bwd(dmo, w_o, y_a, y_b, proj):
    T, D = dmo.shape
    tm, tn = min(512, T), 512
    gpb = 1024 // tn

    def body(a_ref, w_ref, ya_ref, yb_ref, ga_ref, gb_ref, dya_ref, dyb_ref, dga_ref, dgb_ref):
        dm = _nt(a_ref[...], w_ref[...])
        sa, sb = jax.nn.sigmoid(ga_ref[...]), jax.nn.sigmoid(gb_ref[...])
        dya_ref[...] = (dm * sa).astype(BF16)
        dyb_ref[...] = (dm * sb).astype(BF16)
        dga_ref[...] = (dm * ya_ref[...].astype(F32) * sa * (1.0 - sa)).astype(BF16)
        dgb_ref[...] = (dm * yb_ref[...].astype(F32) * sb * (1.0 - sb)).astype(BF16)

    out = pl.BlockSpec((tm, tn), lambda i, j: (i, j))
    return pl.pallas_call(
        body, name="out_proj_bwd", grid=(T // tm, D // tn), out_shape=[jax.ShapeDtypeStruct((T, D), BF16)] * 4,
        in_specs=[pl.BlockSpec((tm, D), lambda i, j: (i, 0)), pl.BlockSpec((tn, D), lambda i, j: (j, 0)), out, out,
                  pl.BlockSpec((tm, tn), lambda i, j: (i, COL_GA * gpb + j)), pl.BlockSpec((tm, tn), lambda i, j: (i, COL_GB * gpb + j))],
        out_specs=[out] * 4, compiler_params=_cp("parallel", "parallel"),
    )(dmo, w_o, y_a, y_b, proj, proj)


def sgu_bwd(proj, dsgu, g_norm, w_spatial, b_spatial_t):
    T = proj.shape[0]
    W = 1024
    gw = W // SGU_GROUPS

    def body(u_ref, v_ref, ds_ref, g_ref, ws_ref, bst_ref, dz_ref, dw_ref, db_ref, dg_ref):
        @pl.when(pl.program_id(0) == 0)
        def _():
            dw_ref[...] = jnp.zeros_like(dw_ref)
            db_ref[...] = jnp.zeros_like(db_ref)
            dg_ref[...] = jnp.zeros_like(dg_ref)

        zu, rstd, dhat, vn, vm = _sgu_mix(u_ref, v_ref, g_ref, ws_ref, bst_ref)
        ds = ds_ref[...]
        du = ds * vm
        dvm = ds * zu
        dvm_b = dvm.astype(BF16)
        ones = jnp.ones((8, gw), F32)
        dvn = []
        for g in range(SGU_GROUPS):
            sl = slice(g * gw, (g + 1) * gw)
            dw_ref[g] += _nt(dvm_b[:, sl], vn[:, sl])
            db_ref[g] += lax.dot_general(ones, dvm[:, sl], (((1,), (1,)), ((), ())), precision=HI, preferred_element_type=F32)
            dvn.append(_tn(ws_ref[g].astype(BF16), dvm_b[:, sl]))
        dvn = jnp.concatenate(dvn, axis=1)
        dg_ref[...] += _colsum(dvn * dhat)
        ddh = dvn * g_ref[...]
        dzv = rstd * (ddh - jnp.mean(ddh, axis=-1, keepdims=True) - dhat * jnp.mean(ddh * dhat, axis=-1, keepdims=True))
        dz_ref[:, 0:W] = (du * _gelu_grad(u_ref[...])).astype(BF16)
        dz_ref[:, W:2 * W] = (dzv * _gelu_grad(v_ref[...])).astype(BF16)

    blk = lambda cb: pl.BlockSpec((SGU_CHUNK, W), lambda i: (i, cb))
    full3 = lambda a, b, c: pl.BlockSpec((a, b, c), lambda i: (0, 0, 0))
    return pl.pallas_call(
        body, name="sgu_bwd", grid=(T // SGU_CHUNK,),
        out_shape=[jax.ShapeDtypeStruct((T, 2 * W), BF16), jax.ShapeDtypeStruct((SGU_GROUPS, SGU_CHUNK, SGU_CHUNK), F32),
                   jax.ShapeDtypeStruct((SGU_GROUPS, 8, SGU_CHUNK), F32), jax.ShapeDtypeStruct((1, W), F32)],
        in_specs=[blk(COL_U), blk(COL_ZV), blk(0), _vec(W), full3(SGU_GROUPS, SGU_CHUNK, SGU_CHUNK),
                  pl.BlockSpec((SGU_CHUNK, SGU_GROUPS), lambda i: (0, 0))],
        out_specs=[pl.BlockSpec((SGU_CHUNK, 2 * W), lambda i: (i, 0)), full3(SGU_GROUPS, SGU_CHUNK, SGU_CHUNK),
                   full3(SGU_GROUPS, 8, SGU_CHUNK), _vec(W)],
        compiler_params=_cp("arbitrary"),
    )(proj, proj, dsgu, g_norm, w_spatial, b_spatial_t)


def hgrn_post_bwd(dya, o, proj, g_norm, comm):
    T, W = o.shape
    tm = min(256, T)

    def body(dy_ref, o_ref, og_ref, g_ref, do_ref, dog_ref, dg_ref):
        @pl.when(pl.program_id(0) == 0)
        def _():
            dg_ref[...] = jnp.zeros_like(dg_ref)

        g = g_ref[...]
        dg = jnp.zeros((1, HEAD_DIM), F32)
        for h in range(N_HEADS):
            sl = slice(h * HEAD_DIM, (h + 1) * HEAD_DIM)
            x, og, dy = o_ref[:, sl], og_ref[:, sl], dy_ref[:, sl]
            r = lax.rsqrt(jnp.mean(x * x, axis=-1, keepdims=True) + EPS)
            xhat = x * r
            s = jax.nn.sigmoid(og)
            don = dy * (og * s)
            dog_ref[:, sl] = (dy * (xhat * g) * (s * (1.0 + og * (1.0 - s)))).astype(BF16)
            dg += _colsum(don * xhat)
            u = don * g
            do_ref[:, sl] = r * (u - xhat * jnp.mean(u * xhat, axis=-1, keepdims=True))
        dg_ref[...] += dg

    row = pl.BlockSpec((tm, W), lambda i: (i, 0))
    return _pallas(
        body, name="hgrn_post_bwd", grid=(T // tm,),
        out_shape=[jax.ShapeDtypeStruct((T, W), F32), jax.ShapeDtypeStruct((T, W), BF16), jax.ShapeDtypeStruct((1, HEAD_DIM), F32)],
        in_specs=[row, row, pl.BlockSpec((tm, W), lambda i: (i, COL_OG)), _vec(HEAD_DIM)],
        out_specs=[row, row, _vec(HEAD_DIM)], scratch=[], semantics=("arbitrary",), operands=(dya, o, proj, g_norm), comm=comm)


def hgrn_bwd(proj, do, lb_logits, comm=None):
    T = proj.shape[0]
    NC, CPB = T // HGRN_CHUNK, HGRN_BLOCK // HGRN_CHUNK
    blk1 = min(HGRN_BLOCK_FWD, T)
    W = N_HEADS * HEAD_DIM
    col, f_spec, l_spec = _hgrn_specs(T)

    def body(l_ref, q_ref, f_ref, v_ref, do_ref, dq_ref, dv_ref, dlg_ref, dlb_ref, st_ref, dst_ref, dec_ref, ddec_ref, dqa_ref, dva_ref):
        d = pl.program_id(1)
        lb = _hgrn_lower_bound(l_ref)
        oml = 1.0 - lb
        mask = _hgrn_chunk_mask(d, HGRN_BLOCK)

        def values(rows):
            s, sn, fg, lf, k = _hgrn_gate(f_ref[rows, :], lb)
            b = _chunk_cumsum(lf, d == 1)
            bl = _chunk_total(lf)
            eb, enb, ee = jnp.exp(b), jnp.exp(-b), jnp.exp(bl - b)
            qd = q_ref[rows, :] * Q_SCALE * eb
            return s, sn, fg, k, bl, eb, enb, ee, qd, k * enb, k * ee

        def block1(i, carry):
            rows = pl.ds(pl.multiple_of(i * blk1, blk1), blk1)
            _, _, _, _, bl, _, _, _, qd, _, ke = values(rows)
            qd, ke = qd.astype(BF16), ke.astype(BF16)
            vb, dob = v_ref[rows, :].astype(BF16), do_ref[rows, :].astype(BF16)
            dec = jnp.exp(bl)
            for cc in range(blk1 // HGRN_CHUNK):
                sl = slice(cc * HGRN_CHUNK, (cc + 1) * HGRN_CHUNK)
                n = i * (blk1 // HGRN_CHUNK) + cc
                st_ref[n] = _tn(vb[sl], ke[sl])
                dst_ref[n] = _tn(dob[sl], qd[sl])
                dec_ref[n] = dec[cc * HGRN_CHUNK:cc * HGRN_CHUNK + 8, :]
            return carry

        _block_loop(T, blk1, block1, 0)

        def scan(t, s):
            n = jnp.where(d == 0, t, NC - 1 - t)
            u = st_ref[n]
            st_ref[n] = s
            return dec_ref[n][0:1, :] * s + u

        lax.fori_loop(0, NC, scan, jnp.zeros((HEAD_DIM, HEAD_DIM), F32))

        def rscan(t, ds):
            n = jnp.where(d == 0, NC - 1 - t, t)
            w = dst_ref[n]
            dst_ref[n] = ds
            ddec_ref[n] = jnp.broadcast_to(_colsum(ds * st_ref[n]), (8, HEAD_DIM))
            return dec_ref[n][0:1, :] * ds + w

        lax.fori_loop(0, NC, rscan, jnp.zeros((HEAD_DIM, HEAD_DIM), F32))

        def block3(i, dlb):
            rows = pl.ds(pl.multiple_of(i * HGRN_BLOCK, HGRN_BLOCK), HGRN_BLOCK)
            s, sn, fg, k, bl, eb, enb, ee, qd, kd, ke = values(rows)
            qdb, kdb, keb = qd.astype(BF16), kd.astype(BF16), ke.astype(BF16)
            vb, dob = v_ref[rows, :].astype(BF16), do_ref[rows, :].astype(BF16)
            att = jnp.where(mask, _nt(qdb, kdb), 0.0).astype(BF16)
            datt = jnp.where(mask, _nt(dob, vb), 0.0).astype(BF16)
            dv = _tn(att, dob)
            dqd = _nn(datt, kdb)
            dkd = _tn(datt, qdb)
            dv_i, dqd_i, dke, ddl = [], [], [], []
            for cc in range(CPB):
                sl = slice(cc * HGRN_CHUNK, (cc + 1) * HGRN_CHUNK)
                n = i * CPB + cc
                st_b, dst_b = st_ref[n].astype(BF16), dst_ref[n].astype(BF16)
                dv_i.append(_nt(keb[sl], dst_b))
                dqd_i.append(_nn(dob[sl], st_b))
                dke.append(_nn(vb[sl], dst_b))
                ddl.append(jnp.broadcast_to(ddec_ref[n][0:1, :] * dec_ref[n][0:1, :], (HGRN_CHUNK, HEAD_DIM)))
            dv = dv + jnp.concatenate(dv_i, axis=0)
            dqd = dqd + jnp.concatenate(dqd_i, axis=0)
            dke = jnp.concatenate(dke, axis=0)
            dq = dqd * eb * Q_SCALE
            dk = dkd * enb + dke * ee
            t_end = dke * ke
            db = dqd * qd - dkd * kd - t_end
            dlf = _chunk_cumsum(db, d == 0) + _chunk_total(t_end) + jnp.concatenate(ddl, axis=0)
            e = dlf / fg - dk
            dlg_ref[rows, :] = (oml * e * s * sn).astype(BF16)

            dq = jnp.where(d == 0, 0.0, dqa_ref[rows, :]) + dq
            dv = jnp.where(d == 0, 0.0, dva_ref[rows, :]) + dv
            dqa_ref[rows, :] = dq
            dva_ref[rows, :] = dv
            dq_ref[rows, :] = dq.astype(BF16)
            dv_ref[rows, :] = dv.astype(BF16)

            return dlb + _colsum(e * sn)

        dlb_ref[...] = _block_loop(T, HGRN_BLOCK, block3, jnp.zeros((1, HEAD_DIM), F32))

    head = pl.BlockSpec((T, HEAD_DIM), lambda h, d: (0, h))
    big = pltpu.VMEM((NC, HEAD_DIM, HEAD_DIM), F32)
    small = pltpu.VMEM((NC, 8, HEAD_DIM), F32)
    acc = pltpu.VMEM((T, HEAD_DIM), F32)
    outs, landed = _pallas(
        body, name="hgrn_bwd", grid=(N_HEADS, 2),
        out_shape=[jax.ShapeDtypeStruct((T, W), BF16), jax.ShapeDtypeStruct((T, W), BF16), jax.ShapeDtypeStruct((T, 2 * W), BF16),
                   jax.ShapeDtypeStruct((2, 1, W), F32)],
        in_specs=[l_spec, col(COL_Q), f_spec, col(COL_V), head],
        out_specs=[head, head, pl.BlockSpec((T, HEAD_DIM), lambda h, d: (0, N_HEADS * d + h)),
                   pl.BlockSpec((None, 1, HEAD_DIM), lambda h, d: (d, 0, h))],
        scratch=[big, big, small, small, acc, acc], semantics=("parallel", "arbitrary"), operands=(lb_logits, proj, proj, proj, do), comm=comm)
    return outs if comm is None else (outs, landed)


def mix_norm_bwd(da1, h0, dh1, g_pre, sc1):
    T, D = h0.shape
    tm = min(256, T)

    def body(da_ref, h_ref, dh_ref, g_ref, sc_ref, gx_ref, s_sh, s_sc, s_g):
        @pl.when(pl.program_id(0) == 0)
        def _():
            for s in (s_sh, s_sc, s_g):
                s[...] = jnp.zeros_like(s)

        h, da = h_ref[...], da_ref[...]
        g, sc = g_ref[...], sc_ref[...]
        r = lax.rsqrt(jnp.mean(h * h, axis=-1, keepdims=True) + EPS)
        n = h * r
        s_sh[...] += _colsum(da)
        s_sc[...] += _colsum(da * (n * g))
        s_g[...] += _colsum(da * (1.0 + sc) * n)
        dn = da * g * (1.0 + sc)
        gx_ref[...] = dh_ref[...] + r * (dn - n * jnp.mean(dn * n, axis=-1, keepdims=True))

    row = pl.BlockSpec((tm, D), lambda i: (i, 0))
    return pl.pallas_call(
        body, name="mix_norm_bwd", grid=(T // tm,),
        out_shape=[jax.ShapeDtypeStruct((T, D), F32)] + [jax.ShapeDtypeStruct((1, D), F32)] * 3,
        in_specs=[row, row, row, _vec(D), _vec(D)], out_specs=[row] + [_vec(D)] * 3, compiler_params=_cp("arbitrary"),
    )(da1, h0, dh1, g_pre, sc1)


def adamw(w, g, m, v, name):
    R, C = w.shape
    tr = R if R * C * 4 <= (1 << 21) else max(8, ((1 << 21) // (C * 4)) // 8 * 8)
    while R % tr:
        tr -= 8

    def body(w_ref, g_ref, m_ref, v_ref, d_ref, m2_ref, v2_ref):
        d_ref[...], m2_ref[...], v2_ref[...] = _adamw(w_ref[...], g_ref[...], m_ref[...], v_ref[...])

    row = pl.BlockSpec((tr, C), lambda i: (i, 0))
    return pl.pallas_call(
        body, name=name, grid=(R // tr,), out_shape=[jax.ShapeDtypeStruct((R, C), F32)] * 3,
        in_specs=[row] * 4, out_specs=[row] * 3, compiler_params=_cp("parallel"),
    )(w, g, m, v)


def wada_update(c_all, dmod, w, m, v):
    D, N = w.shape
    tm, tn = 512, 1024

    def body(c_ref, dm_ref, w_ref, m_ref, v_ref, g_ref, d_ref, m2_ref, v2_ref):
        c = c_ref[...]
        g = lax.dot_general(c * jax.nn.sigmoid(c), dm_ref[...], (((0,), (0,)), ((), ())), precision=HI, preferred_element_type=F32)
        g_ref[...] = g
        d_ref[...], m2_ref[...], v2_ref[...] = _adamw(w_ref[...], g, m_ref[...], v_ref[...])

    blk = pl.BlockSpec((tm, tn), lambda i, j: (i, j))
    return pl.pallas_call(
        body, name="wada_update", grid=(D // tm, N // tn), out_shape=[jax.ShapeDtypeStruct((D, N), F32)] * 4,
        in_specs=[pl.BlockSpec((8, tm), lambda i, j: (0, i)), pl.BlockSpec((8, tn), lambda i, j: (0, j)), blk, blk, blk],
        out_specs=[blk] * 4, compiler_params=_cp("parallel", "parallel"),
    )(c_all, dmod, w, m, v)


def sum_devices(gathered, name):
    n, R, C = gathered.shape

    def body(g_ref, o_ref):
        s = g_ref[0]
        for i in range(1, n):
            s = s + g_ref[i]
        o_ref[...] = s

    return pl.pallas_call(body, name=name, out_shape=jax.ShapeDtypeStruct((R, C), F32), compiler_params=_cp())(gathered)


def lb_logits_grad(dlb, lb_logits):
    def body(d_ref, l_ref, o_ref):
        for d in range(2):
            l0, l1 = l_ref[d, 0:1, :], l_ref[d, 1:2, :]
            m = jnp.maximum(l0, l1)
            e0, e1 = jnp.exp(l0 - m), jnp.exp(l1 - m)
            p0, p1 = e0 / (e0 + e1), e1 / (e0 + e1)
            g = d_ref[d:d + 1, :]
            o_ref[d, 0:1, :] = p0 * (g - p0 * g)
            o_ref[d, 1:2, :] = -p1 * (p0 * g)

    return pl.pallas_call(body, name="lb_logits_grad", out_shape=jax.ShapeDtypeStruct(lb_logits.shape, F32), compiler_params=_cp())(dlb, lb_logits)


def add_halves(g, landed, core):
    nj, _, r, cc = g.shape
    tr = min(256, r)

    def body(core_ref, g_ref, l_ref, o_ref):
        o_ref[...] = (g_ref[...].astype(F32) + l_ref[...].astype(F32)).astype(BF16)

    return pl.pallas_call(
        body, name="add_halves_%dx%d" % (r, cc), out_shape=jax.ShapeDtypeStruct((nj, r, cc), BF16),
        grid_spec=pltpu.PrefetchScalarGridSpec(
            num_scalar_prefetch=1, grid=(nj, r // tr),
            in_specs=[pl.BlockSpec((None, None, tr, cc), lambda j, i, core_ref: (j, core_ref[0], i, 0)),
                      pl.BlockSpec((None, None, tr, cc), lambda j, i, core_ref: (j, 0, i, 0))],
            out_specs=pl.BlockSpec((None, tr, cc), lambda j, i, core_ref: (j, i, 0))),
        compiler_params=_cp("parallel", "parallel"),
    )(core, g, landed)


def sum_chips(parts, landed, chip):
    nj, r, cc = parts.shape
    tr = min(256, r)

    def body(chip_ref, p_ref, l_ref, o_ref):
        mine = p_ref[...].astype(F32)
        s = None
        for j in range(nj):
            t = jnp.where(chip_ref[0] == j, mine, l_ref[j].astype(F32))
            s = t if s is None else s + t
        o_ref[...] = s

    return pl.pallas_call(
        body, name="sum_chips_%dx%d" % (r, cc), out_shape=jax.ShapeDtypeStruct((r, cc), F32),
        grid_spec=pltpu.PrefetchScalarGridSpec(
            num_scalar_prefetch=1, grid=(r // tr,),
            in_specs=[pl.BlockSpec((None, tr, cc), lambda i, chip_ref: (chip_ref[0], i, 0)),
                      pl.BlockSpec((nj, tr, cc), lambda i, chip_ref: (0, i, 0))],
            out_specs=pl.BlockSpec((tr, cc), lambda i, chip_ref: (i, 0))),
        compiler_params=_cp("parallel"),
    )(chip, parts, landed)


def adamw_halves(w, own, other, m, v, core, name):
    r, cc = own.shape
    tr = min(128, r)
    nb = r // tr

    def body(core_ref, w_ref, a_ref, b_ref, m_ref, v_ref, g_ref, d_ref, m2_ref, v2_ref):
        g = jnp.where(pl.program_id(0) == core_ref[0], a_ref[...], b_ref[...])
        g_ref[...] = g
        d_ref[...], m2_ref[...], v2_ref[...] = _adamw(w_ref[...], g, m_ref[...], v_ref[...])

    full = pl.BlockSpec((tr, cc), lambda h, i, core_ref: (h * nb + i, 0))
    mine = pl.BlockSpec((tr, cc), lambda h, i, core_ref: (jnp.where(h == core_ref[0], i, 0), 0))
    theirs = pl.BlockSpec((tr, cc), lambda h, i, core_ref: (jnp.where(h == core_ref[0], 0, i), 0))
    return pl.pallas_call(
        body, name=name, out_shape=[jax.ShapeDtypeStruct((2 * r, cc), F32)] * 4,
        grid_spec=pltpu.PrefetchScalarGridSpec(
            num_scalar_prefetch=1, grid=(2, nb), in_specs=[full, mine, theirs, full, full], out_specs=[full] * 4),
        compiler_params=_cp("arbitrary", "arbitrary"),
    )(core, w, own, other, m, v)


def _place():
    mx, my, mc = lax.axis_index("x"), lax.axis_index("y"), lax.axis_index("c")
    chips = [(1 - mx, my), (mx, 1 - my), (1 - mx, 1 - my)]
    return mx, my, mc, chips


def all_gather_small(x, name):
    R, C = x.shape

    def body(x_ref, out_ref, send_sems, recv_sems, local_sem):
        mx, my, mc, _ = _place()
        me = 4 * mx + 2 * my + mc
        mine = pltpu.make_async_copy(x_ref, out_ref.at[me], local_sem)
        mine.start()

        def peer(k):
            px = 1 - mx if k & 4 else mx
            py = 1 - my if k & 2 else my
            pc = 1 - mc if k & 1 else mc
            return px, py, pc

        def copy(k, src, slot):
            return pltpu.make_async_remote_copy(src_ref=src, dst_ref=out_ref.at[slot], send_sem=send_sems.at[k - 1],
                                                recv_sem=recv_sems.at[k - 1], device_id=peer(k), device_id_type=MESH)

        sends = [copy(k, x_ref, me) for k in range(1, 8)]
        for cp in sends:
            cp.start()
        for k in range(1, 8):
            px, py, pc = peer(k)
            slot = 4 * px + 2 * py + pc
            copy(k, out_ref.at[slot], slot).wait_recv()
        for cp in sends:
            cp.wait_send()
        mine.wait()

    return pl.pallas_call(
        body, name=name, out_shape=jax.ShapeDtypeStruct((8, R, C), F32),
        in_specs=[pl.BlockSpec(memory_space=pltpu.VMEM)], out_specs=pl.BlockSpec(memory_space=pltpu.VMEM),
        scratch_shapes=[pltpu.SemaphoreType.DMA((7,)), pltpu.SemaphoreType.DMA((7,)), pltpu.SemaphoreType.DMA],
        compiler_params=_cp(),
    )(x)


def gather8_comm(x):
    def copies(x_ref, out_ref, send_sems, recv_sems):
        mx, my, mc, _ = _place()
        me = 4 * mx + 2 * my + mc

        def peer(k):
            return (1 - mx if k & 4 else mx, 1 - my if k & 2 else my, 1 - mc if k & 1 else mc)

        def copy(k, src, slot):
            return pltpu.make_async_remote_copy(src_ref=src, dst_ref=out_ref.at[slot], send_sem=send_sems.at[k - 1],
                                                recv_sem=recv_sems.at[k - 1], device_id=peer(k), device_id_type=MESH)

        sends = [copy(k, x_ref, me) for k in range(1, 8)]
        arrivals = []
        for k in range(1, 8):
            px, py, pc = peer(k)
            slot = 4 * px + 2 * py + pc
            arrivals.append(copy(k, out_ref.at[slot], slot))
        return sends, arrivals, pltpu.make_async_copy(x_ref, out_ref.at[me], send_sems.at[7])

    def start(cin, cout, send_sems, recv_sems):
        sends, _, mine = copies(cin[0], cout[0], send_sems, recv_sems)
        mine.start()
        for cp in sends:
            cp.start()

    def finish(cin, cout, send_sems, recv_sems):
        sends, arrivals, mine = copies(cin[0], cout[0], send_sems, recv_sems)
        for cp in arrivals:
            cp.wait_recv()
        for cp in sends:
            cp.wait_send()
        mine.wait()

    return _Comm([x], [jax.ShapeDtypeStruct((8,) + x.shape, F32)], {}, 8, start, finish)


def _join(a, b):
    na_in, na_out = len(a.operands), len(a.out_shape)

    def split(fn_a, fn_b):
        def both(cin, cout, send_sems, recv_sems):
            fn_a(cin[:na_in], cout[:na_out], send_sems.at[pl.ds(0, a.n_sems)], recv_sems.at[pl.ds(0, a.n_sems)])
            fn_b(cin[na_in:], cout[na_out:], send_sems.at[pl.ds(a.n_sems, b.n_sems)], recv_sems.at[pl.ds(a.n_sems, b.n_sems)])
        return both

    aliases = dict(a.aliases)
    aliases.update({na_in + i: na_out + o for i, o in b.aliases.items()})
    return _Comm(a.operands + b.operands, a.out_shape + b.out_shape, aliases, a.n_sems + b.n_sems, split(a.start, b.start), split(a.finish, b.finish))


def _region(ref, kind, j, half, r, cc):
    nr = r if half is None else r // 2
    off = 0 if half is None else half * nr
    if kind == "col":
        return ref.at[pl.ds(off, nr), pl.ds(pl.multiple_of(j * cc, 128), cc)]
    return ref.at[pl.ds(pl.multiple_of(j * r + off, 16), nr), :]


def comm_call(comm, name):
    ni, no = len(comm.operands), len(comm.out_shape)

    def body(*refs):
        comm.start(refs[:ni], refs[ni:ni + no], *refs[ni + no:])
        comm.finish(refs[:ni], refs[ni:ni + no], *refs[ni + no:])

    return pl.pallas_call(
        body, name=name, out_shape=comm.out_shape, in_specs=[ANY] * ni, out_specs=[ANY] * no, input_output_aliases=comm.aliases,
        scratch_shapes=[pltpu.SemaphoreType.DMA((comm.n_sems,)), pltpu.SemaphoreType.DMA((comm.n_sems,))], compiler_params=_cp(),
    )(*comm.operands)


def gather_comm(fulls, kinds, dims):
    n = len(fulls)

    def copies(f_refs, send_sems, recv_sems):
        mx, my, mc, chips = _place()
        jme = 2 * mx + my

        def landed(w, k, half):
            px, py = chips[k]
            return _region(f_refs[w], kinds[w], 2 * px + py, half, *dims[w])

        def over_ici(w, k, reg):
            px, py = chips[k]
            return pltpu.make_async_remote_copy(src_ref=reg, dst_ref=reg, send_sem=send_sems.at[6 * w + k], recv_sem=recv_sems.at[6 * w + k],
                                                device_id=(px, py, mc), device_id_type=MESH)

        def over_d2d(w, k, half):
            reg = landed(w, k, half)
            return pltpu.make_async_remote_copy(src_ref=reg, dst_ref=reg, send_sem=send_sems.at[6 * w + 3 + k],
                                                recv_sem=recv_sems.at[6 * w + 3 + k], device_id=(mx, my, 1 - mc), device_id_type=MESH)

        sends = [over_ici(w, k, _region(f_refs[w], kinds[w], jme, mc, *dims[w])) for w in range(n) for k in range(3)]
        return mc, landed, over_ici, over_d2d, sends

    def start(cin, f_refs, send_sems, recv_sems):
        for cp in copies(f_refs, send_sems, recv_sems)[4]:
            cp.start()

    def finish(cin, f_refs, send_sems, recv_sems):
        mc, landed, over_ici, over_d2d, sends = copies(f_refs, send_sems, recv_sems)
        passed = []
        for w in range(n):
            for k in range(3):
                over_ici(w, k, landed(w, k, mc)).wait_recv()
                cp = over_d2d(w, k, mc)
                cp.start()
                passed.append(cp)
        for w in range(n):
            for k in range(3):
                over_d2d(w, k, 1 - mc).wait_recv()
        for cp in sends + passed:
            cp.wait_send()

    return _Comm(fulls, [jax.ShapeDtypeStruct(f.shape, BF16) for f in fulls], {w: w for w in range(n)}, 6 * n, start, finish)


def exchange_comm(grads):
    n = len(grads)

    def copies(g_refs, l_refs, send_sems, recv_sems):
        mx, my, mc, _ = _place()
        return [pltpu.make_async_remote_copy(src_ref=g_refs[w].at[:, pl.ds(1 - mc, 1)], dst_ref=l_refs[w], send_sem=send_sems.at[w],
                                             recv_sem=recv_sems.at[w], device_id=(mx, my, 1 - mc), device_id_type=MESH) for w in range(n)]

    def start(*refs):
        for cp in copies(*refs):
            cp.start()

    def finish(*refs):
        for cp in copies(*refs):
            cp.wait()

    return _Comm(grads, [jax.ShapeDtypeStruct((g.shape[0], 1) + g.shape[2:], BF16) for g in grads], {}, n, start, finish)


def exchange_halves(grads, name):
    return comm_call(exchange_comm(grads), name)


def scatter_comm(parts):
    n = len(parts)

    def sends(p_refs, l_refs, send_sems, recv_sems):
        mx, my, mc, chips = _place()
        return [pltpu.make_async_remote_copy(src_ref=p_refs[w].at[2 * px + py], dst_ref=l_refs[w].at[2 * mx + my],
                                             send_sem=send_sems.at[3 * w + k], recv_sem=recv_sems.at[3 * w + k],
                                             device_id=(px, py, mc), device_id_type=MESH) for w in range(n) for k, (px, py) in enumerate(chips)]

    def start(p_refs, l_refs, send_sems, recv_sems):
        for cp in sends(p_refs, l_refs, send_sems, recv_sems):
            cp.start()

    def finish(p_refs, l_refs, send_sems, recv_sems):
        mx, my, mc, chips = _place()
        for w in range(n):
            for k, (px, py) in enumerate(chips):
                slot = l_refs[w].at[2 * px + py]
                pltpu.make_async_remote_copy(src_ref=slot, dst_ref=slot, send_sem=send_sems.at[3 * w + k], recv_sem=recv_sems.at[3 * w + k],
                                             device_id=(px, py, mc), device_id_type=MESH).wait_recv()
        for cp in sends(p_refs, l_refs, send_sems, recv_sems):
            cp.wait_send()

    return _Comm(parts, [jax.ShapeDtypeStruct(p.shape, BF16) for p in parts], {}, 3 * n, start, finish)


def share_comm(sums):
    n = len(sums)

    def copies(q_refs, o_refs, send_sems, recv_sems):
        mx, my, mc, _ = _place()
        return [pltpu.make_async_remote_copy(src_ref=q_refs[w], dst_ref=o_refs[w], send_sem=send_sems.at[w], recv_sem=recv_sems.at[w],
                                             device_id=(mx, my, 1 - mc), device_id_type=MESH) for w in range(n)]

    def start(*refs):
        for cp in copies(*refs):
            cp.start()

    def finish(*refs):
        for cp in copies(*refs):
            cp.wait()

    return _Comm(sums, [jax.ShapeDtypeStruct(q.shape, F32) for q in sums], {}, n, start, finish)


def _pack(arrays):
    flat = jnp.concatenate([a.reshape(-1) for a in arrays])
    rows = -(-flat.shape[0] // 1024) * 8
    return jnp.pad(flat, (0, rows * 128 - flat.shape[0])).reshape(rows, 128)


def _unpack(packed, shapes):
    flat, out, off = packed.reshape(-1), [], 0
    for s in shapes:
        n = math.prod(s)
        out.append(flat[off:off + n].reshape(s))
        off += n
    return out


def kernel(x, c, w_ada, b_ada, g_pre_mix, g_post_mix, g_pre_ffn, g_post_ffn, w_in, lb_logits, g_hgrn_norm, w_a_out, g_sgu_norm, w_spatial, b_spatial, w_b_out, w_o, w_ff1, w_ff2, loss_target, m_w_ada, m_b_ada, m_g_pre_mix, m_g_post_mix, m_g_pre_ffn, m_g_post_ffn, m_w_in, m_lb_logits, m_g_hgrn_norm, m_w_a_out, m_g_sgu_norm, m_w_spatial, m_b_spatial, m_w_b_out, m_w_o, m_w_ff1, m_w_ff2, v_w_ada, v_b_ada, v_g_pre_mix, v_g_post_mix, v_g_pre_ffn, v_g_post_ffn, v_w_in, v_lb_logits, v_g_hgrn_norm, v_w_a_out, v_g_sgu_norm, v_w_spatial, v_b_spatial, v_w_b_out, v_w_o, v_w_ff1, v_w_ff2):
    mx, my, mc = lax.axis_index("x"), lax.axis_index("y"), lax.axis_index("c")
    chip, me = 2 * mx + my, 4 * mx + 2 * my + mc
    D = D_MODEL
    h0, tgt = x[0], loss_target[0]
    n_ada = w_ada.shape[2]
    n_lb = lb_logits.shape[2]

    got = all_gather_small(_pack([c, lb_logits]), "gather_inputs")
    c_all = got[:, :D // 128, :].reshape(8, D)
    lb_full = got[0::2, D // 128:D // 128 + 4 * n_lb // 128, :].reshape(4, 2, 2, n_lb).transpose(1, 2, 0, 3).reshape(2, 2, 4 * n_lb)
    b_ada_chip = lax.dynamic_slice(b_ada, (0, chip * n_ada), (1, n_ada))
    mod_cols = mod_matmul(c_all, w_ada[0], b_ada_chip)
    got = all_gather_small(mod_cols.reshape(-1, 128), "gather_mod").reshape(4, 2, 8, n_ada)
    mod = lax.dynamic_index_in_dim(got[:, 0], me, axis=1, keepdims=False).reshape(6, 1, D)
    sh1, sc1, gt1, sh2, sc2, gt2 = (mod[i] for i in range(6))

    big = [("w_in", w_in, "col"), ("w_a_out", w_a_out, "col"), ("w_b_out", w_b_out, "col"), ("w_o", w_o, "row"),
           ("w_ff1", w_ff1, "col"), ("w_ff2", w_ff2, "row")]
    kinds = [k for _, _, k in big]
    chip_idx, core = chip.reshape(1).astype(jnp.int32), mc.reshape(1).astype(jnp.int32)
    fulls = [cast_into_full(w[0], kind, chip_idx, "cast_" + nm) for nm, w, kind in big]
    dims = [w.shape[1:] for _, w, _ in big]
    later = lambda lo, hi: gather_comm(fulls[lo:hi], kinds[lo:hi], dims[lo:hi])
    halves_summed = lambda grads, name: [add_halves(g, l, core) for g, l in zip(grads, exchange_halves(grads, name))]

    bst = b_spatial[0].T
    a1 = prenorm(h0, g_pre_mix, sc1, sh1)
    proj, w_in_f, (w_a_f, w_b_f, w_o_f) = in_proj_gathered(a1, fulls[0], chip_idx, dims[0], later(1, 4))
    o, (w_ff1_f,) = hgrn_fwd(proj, lb_full, comm=later(4, 5))
    ya_pre = hgrn_post_fwd(o, proj, g_hgrn_norm)
    sgu = sgu_fwd(proj, g_sgu_norm, w_spatial[0], bst)
    y_a, y_b, merged = merge_matmul(ya_pre, sgu, w_a_f, w_b_f, proj)
    mo, h1, a2 = out_proj(merged, w_o_f, h0, gt1, g_post_mix, g_pre_ffn, sc2, sh2)
    (f1, hid), (w_ff2_f,) = matmul(a2, w_ff1_f, mode="nn", out_dtype=BF16, tm=1024, tn=1024, tk=2048, name="ff1", relu2=True,
                                   comm=later(5, 6))
    ff = matmul(hid, w_ff2_f, mode="nn", out_dtype=F32, tm=1024, tn=1024, tk=2048, name="ff2")
    dy, dff, loss_parts, d_gt2, d_g_post_ffn = loss_bwd(ff, h1, tgt, gt2, g_post_ffn)

    df1 = ff2_bwd(dff, w_ff2_f, f1)
    gr_ff2 = matmul(hid, dff, mode="tn", out_dtype=BF16, tm=1024, tn=1024, tk=2048, name="dw_ff2")
    gr_ff2 = gr_ff2.reshape(4, 2, -1, D)
    da2, (landed_ff2,) = matmul(df1, w_ff1_f, mode="nt", out_dtype=F32, tm=1024, tn=1024, tk=2048, name="da2", comm=exchange_comm([gr_ff2]))
    gr_ff1 = matmul(a2, df1, mode="tn", out_dtype=BF16, tm=1024, tn=2048, tk=1024, name="dw_ff1", split=(4, 2))
    (dh1, dmo, d_sh2, d_sc2, d_g_pre_ffn, d_gt1, d_g_post_mix), (landed_ff1,) = ffn_norm_bwd(
        dy, da2, h1, mo, g_pre_ffn, sc2, gt1, g_post_mix, exchange_comm([gr_ff1]))
    parts_ff = [add_halves(gr_ff1, landed_ff1, core), add_halves(gr_ff2, landed_ff2, core)]
    dya, dyb, dga, dgb = out_proj_bwd(dmo, w_o_f, y_a, y_b, proj)
    gr_o = matmul(merged, dmo, mode="tn", out_dtype=BF16, tm=1024, tn=1024, tk=2048, name="dw_o")
    dsgu = matmul(dyb, w_b_f, mode="nt", out_dtype=F32, tm=512, tn=1024, tk=2048, name="dsgu")
    gr_b = matmul(sgu, dyb, mode="tn", out_dtype=BF16, tm=512, tn=512, tk=4096, name="dw_b_out", split=(4, 2))
    dz, d_w_spatial, d_b_spatial, d_g_sgu = sgu_bwd(proj, dsgu, g_sgu_norm, w_spatial[0], bst)
    dya_pre = matmul(dya, w_a_f, mode="nt", out_dtype=F32, tm=512, tn=1024, tk=2048, name="dya_pre")
    gr_a = matmul(ya_pre, dya, mode="tn", out_dtype=BF16, tm=512, tn=512, tk=4096, name="dw_a_out", split=(4, 2))
    gr_mix = [gr_a, gr_b, gr_o.reshape(4, 2, -1, D)]
    (do, dog, d_g_hgrn), landed_halves = hgrn_post_bwd(dya_pre, o, proj, g_hgrn_norm, exchange_comm(gr_mix))
    parts_mix = [add_halves(g, l, core) for g, l in zip(gr_mix, landed_halves)]
    chips_summed = lambda parts, landed: [sum_chips(p, l, chip_idx) for p, l in zip(parts, landed)]
    (dq, dv, dlg, d_lb), landed_ff = hgrn_bwd(proj, do, lb_full, comm=scatter_comm(parts_ff))
    own_ff = chips_summed(parts_ff, landed_ff)
    dproj = jnp.concatenate([dq, dlg, dv, dog, dz, dga, dgb], axis=1)
    early = _pack([d_g_sgu, d_w_spatial, d_b_spatial[:, 0, :]])
    gr_in, (*landed_mix, got_early) = matmul(a1, dproj, mode="tn", out_dtype=BF16, tm=1024, tn=2816, tk=1024, name="dw_in", split=(4, 2),
                                             comm=_join(scatter_comm(parts_mix), gather8_comm(early)))
    own_mix = chips_summed(parts_mix, landed_mix)
    parts_in = halves_summed([gr_in], "exchange_in")
    da1, (landed_in, *other_rest) = matmul(dproj, w_in_f, mode="nt", out_dtype=F32, tm=1024, tn=1024, tk=2816, name="da1",
                                           comm=_join(scatter_comm(parts_in), share_comm(own_mix + own_ff)))
    own_in = chips_summed(parts_in, [landed_in])
    other_in = comm_call(share_comm(own_in), "share_w_in")
    own, other = own_in + own_mix + own_ff, list(other_in) + other_rest
    grad_x, d_sh1, d_sc1, d_g_pre_mix = mix_norm_bwd(da1, h0, dh1, g_pre_mix, sc1)
    out = {}

    mine = _pack([d_sh1, d_sc1, d_gt1, d_sh2, d_sc2, d_gt2, d_g_pre_mix, d_g_post_mix, d_g_pre_ffn, d_g_post_ffn, d_g_hgrn, d_lb,
                  loss_parts[0:1, 0:1]])
    got = all_gather_small(mine, "gather_small_grads")
    g_b_ada, g_g1, g_g2, g_g3, g_g4, g_hg, g_lb, sq_err = _unpack(
        sum_devices(got, "sum_small_grads"), [(1, 6 * D), (1, D), (1, D), (1, D), (1, D), (1, HEAD_DIM), (2, 1024), ()])
    loss = 0.5 * sq_err / D
    g_sg, g_ws, g_bs = _unpack(sum_devices(got_early, "sum_sgu_grads"), [(1, 1024), w_spatial.shape, b_spatial.shape])
    g_lbl = lax.dynamic_slice(lb_logits_grad(g_lb, lb_full), (0, 0, chip * n_lb), (2, 2, n_lb))
    names = ["b_ada", "g_pre_mix", "g_post_mix", "g_pre_ffn", "g_post_ffn", "g_hgrn_norm", "g_sgu_norm", "w_spatial", "b_spatial", "lb_logits"]
    ws = [b_ada, g_pre_mix, g_post_mix, g_pre_ffn, g_post_ffn, g_hgrn_norm, g_sgu_norm, w_spatial, b_spatial, lb_logits]
    gs = [g_b_ada, g_g1, g_g2, g_g3, g_g4, g_hg, g_sg, g_ws, g_bs, g_lbl]
    ms = [m_b_ada, m_g_pre_mix, m_g_post_mix, m_g_pre_ffn, m_g_post_ffn, m_g_hgrn_norm, m_g_sgu_norm, m_w_spatial, m_b_spatial, m_lb_logits]
    vs = [v_b_ada, v_g_pre_mix, v_g_post_mix, v_g_pre_ffn, v_g_post_ffn, v_g_hgrn_norm, v_g_sgu_norm, v_w_spatial, v_b_spatial, v_lb_logits]
    shapes = [w.shape for w in ws]
    upd = adamw(_pack(ws), _pack(gs), _pack(ms), _pack(vs), "adamw_small")
    upd = [_unpack(u, shapes) for u in upd]
    for i, nm in enumerate(names):
        out[nm] = (gs[i], upd[0][i], upd[1][i], upd[2][i])

    dmod_all = got[:, :6 * D // 128, :].reshape(8, 6 * D)
    dmod_chip = lax.dynamic_slice(dmod_all, (0, chip * n_ada), (8, n_ada))
    out["w_ada"] = tuple(a[None] for a in wada_update(c_all, dmod_chip, w_ada[0], m_w_ada[0], v_w_ada[0]))
    for (nm, w, _), a, b, m, v in zip(big, own, other, (m_w_in, m_w_a_out, m_w_b_out, m_w_o, m_w_ff1, m_w_ff2),
                                      (v_w_in, v_w_a_out, v_w_b_out, v_w_o, v_w_ff1, v_w_ff2)):
        out[nm] = tuple(t[None] for t in adamw_halves(w[0], a, b, m[0], v[0], core, "adamw_" + nm))

    order = ["w_ada", "b_ada", "g_pre_mix", "g_post_mix", "g_pre_ffn", "g_post_ffn", "w_in", "lb_logits", "g_hgrn_norm", "w_a_out",
             "g_sgu_norm", "w_spatial", "b_spatial", "w_b_out", "w_o", "w_ff1", "w_ff2"]
    return (loss, grad_x[None], *[out[nm][0] for nm in order], *[out[nm][1] for nm in order], *[out[nm][2] for nm in order],
            *[out[nm][3] for nm in order])
```

```python
import functools
import math

import jax
import jax.numpy as jnp
from jax import lax
from jax.experimental import pallas as pl
from jax.experimental.pallas import tpu as pltpu

F32, BF16 = jnp.float32, jnp.bfloat16
HI = lax.Precision.HIGHEST
MESH = pl.DeviceIdType.MESH
ANY = pl.BlockSpec(memory_space=pl.ANY)

EPS = 1e-6
D_MODEL = 2048
N_HEADS = 8
HEAD_DIM = 128
HGRN_CHUNK = 32
HGRN_BLOCK = 256
HGRN_BLOCK_FWD = 512
SGU_CHUNK = 128
SGU_GROUPS = 8
Q_SCALE = HEAD_DIM ** -0.5
COL_Q, COL_FFW, COL_FBW, COL_V, COL_OG, COL_U, COL_ZV, COL_GA, COL_GB = 0, 1, 2, 3, 4, 5, 6, 7, 9
N_PROJ = 11264
VMEM_BYTES_V7X = 64 * 1024 * 1024
VMEM_LIMIT = VMEM_BYTES_V7X - 8 * 1024 * 1024

ADAM_LR, ADAM_B1, ADAM_B2, ADAM_EPS, ADAM_WD, ADAM_STEP = 0.001, 0.9, 0.999, 1e-08, 0.01, 10
ADAM_C1 = 1.0 - ADAM_B1 ** ADAM_STEP
ADAM_C2 = 1.0 - ADAM_B2 ** ADAM_STEP


def _cp(*sem):
    return pltpu.CompilerParams(dimension_semantics=sem if sem else None, vmem_limit_bytes=VMEM_LIMIT)


def _vec(d):
    return pl.BlockSpec((1, d), lambda *_: (0, 0))


def _colsum(x):
    return jnp.sum(x, axis=0, keepdims=True)


def _nt(a, b):
    return lax.dot_general(a, b, (((1,), (1,)), ((), ())), preferred_element_type=F32)


def _tn(a, b):
    return lax.dot_general(a, b, (((0,), (0,)), ((), ())), preferred_element_type=F32)


def _nn(a, b):
    return jnp.dot(a, b, preferred_element_type=F32)


def _adamw(w, g, m, v):
    m2 = ADAM_B1 * m + (1.0 - ADAM_B1) * g
    v2 = ADAM_B2 * v + (1.0 - ADAM_B2) * (g * g)
    delta = -ADAM_LR * ((m2 / ADAM_C1) / (jnp.sqrt(v2 / ADAM_C2) + ADAM_EPS) + ADAM_WD * w)
    return delta, m2, v2


class _Comm:
    def __init__(self, operands, out_shape, aliases, n_sems, start, finish):
        self.operands, self.out_shape, self.aliases, self.n_sems = list(operands), list(out_shape), dict(aliases), n_sems
        self.start, self.finish = start, finish


def _pallas(body, *, name, grid, in_specs, out_specs, out_shape, scratch, semantics, operands, comm=None):
    if comm is None:
        res = pl.pallas_call(body, name=name, grid=grid, in_specs=in_specs, out_specs=out_specs, out_shape=out_shape,
                             scratch_shapes=scratch, compiler_params=_cp(*semantics))(*operands)
        return res, []
    n_in, n_out, n_scr = len(in_specs), len(out_specs), len(scratch)
    nci, nco = len(comm.operands), len(comm.out_shape)

    def with_comm(*refs):
        ins, rest = refs[:n_in], refs[n_in:]
        cin, rest = rest[:nci], rest[nci:]
        outs, rest = rest[:n_out], rest[n_out:]
        cout, rest = rest[:nco], rest[nco:]
        scr, (send, recv) = rest[:n_scr], rest[n_scr:]
        ids = [pl.program_id(a) for a in range(len(grid))]
        first = functools.reduce(jnp.logical_and, [i == 0 for i in ids])
        last = functools.reduce(jnp.logical_and, [i == g - 1 for i, g in zip(ids, grid)])

        @pl.when(first)
        def _():
            comm.start(cin, cout, send, recv)

        body(*ins, *outs, *scr)

        @pl.when(last)
        def _():
            comm.finish(cin, cout, send, recv)

    res = pl.pallas_call(
        with_comm, name=name, grid=grid, in_specs=list(in_specs) + [ANY] * nci, out_specs=list(out_specs) + [ANY] * nco,
        out_shape=list(out_shape) + comm.out_shape, input_output_aliases={n_in + i: n_out + o for i, o in comm.aliases.items()},
        scratch_shapes=list(scratch) + [pltpu.SemaphoreType.DMA((comm.n_sems,)), pltpu.SemaphoreType.DMA((comm.n_sems,))],
        compiler_params=_cp(*["arbitrary"] * len(grid)),
    )(*operands, *comm.operands)
    return res[:n_out], res[n_out:]


def matmul(a, b, *, mode, out_dtype, tm, tn, tk, name, split=None, comm=None, relu2=False):
    if mode == "tn":
        (K, M), (_, N) = a.shape, b.shape
    elif mode == "nt":
        (M, K), (N, _) = a.shape, b.shape
    else:
        (M, K), (_, N) = a.shape, b.shape
    tm, tn, tk = min(tm, M), min(tn, N), min(tk, K)
    nk = K // tk
    a_spec = pl.BlockSpec((tk, tm), lambda i, j, k: (k, i)) if mode == "tn" else pl.BlockSpec((tm, tk), lambda i, j, k: (i, k))
    b_spec = pl.BlockSpec((tn, tk), lambda i, j, k: (j, k)) if mode == "nt" else pl.BlockSpec((tk, tn), lambda i, j, k: (k, j))
    dot = {"nn": _nn, "nt": _nt, "tn": _tn}[mode]
    if split is None:
        out_shape = jax.ShapeDtypeStruct((M, N), out_dtype)
        out_spec = pl.BlockSpec((tm, tn), lambda i, j, k: (i, j))
    else:
        nj, nh = split
        rows, cols = M // nh, N // nj
        tm, tn = min(tm, rows), min(tn, cols)
        bi, bj = rows // tm, cols // tn
        out_shape = jax.ShapeDtypeStruct((nj, nh, rows, cols), out_dtype)
        out_spec = pl.BlockSpec((None, None, tm, tn), lambda i, j, k: (j // bj, i // bi, i % bi, j % bj))

    def finish(y, o_ref, sq_ref):
        o_ref[...] = y.astype(o_ref.dtype)
        if relu2:
            p = jnp.maximum(y, 0.0)
            sq_ref[0][...] = (p * p).astype(BF16)

    if nk == 1:
        def body(a_ref, b_ref, o_ref, *sq_ref):
            finish(dot(a_ref[...], b_ref[...]), o_ref, sq_ref)
        scratch = []
    else:
        def body(a_ref, b_ref, o_ref, *rest):
            acc_ref, k = rest[-1], pl.program_id(2)

            @pl.when(k == 0)
            def _():
                acc_ref[...] = jnp.zeros_like(acc_ref)

            acc_ref[...] += dot(a_ref[...], b_ref[...])

            @pl.when(k == nk - 1)
            def _():
                finish(acc_ref[...], o_ref, rest[:-1])
        scratch = [pltpu.VMEM((tm, tn), F32)]

    out_specs, out_shapes = [out_spec], [out_shape]
    if relu2:
        out_specs, out_shapes = out_specs + [out_spec], out_shapes + [jax.ShapeDtypeStruct(out_shape.shape, BF16)]
    outs, landed = _pallas(
        body, name=name, grid=(M // tm, N // tn, nk), in_specs=[a_spec, b_spec], out_specs=out_specs, out_shape=out_shapes,
        scratch=scratch, semantics=("parallel", "parallel", "arbitrary"), operands=(a, b), comm=comm)
    out = tuple(outs) if relu2 else outs[0]
    return out if comm is None else (out, landed)


def cast_into_full(w, kind, chip, name):
    r, cc = w.shape
    tr = min(r, 512)
    nb = r // tr

    def body(chip_ref, w_ref, o_ref):
        o_ref[...] = w_ref[...].astype(BF16)

    if kind == "col":
        full, out_map = (r, 4 * cc), lambda i, chip_ref: (i, chip_ref[0])
    else:
        full, out_map = (4 * r, cc), lambda i, chip_ref: (chip_ref[0] * nb + i, 0)
    return pl.pallas_call(
        body, name=name, out_shape=jax.ShapeDtypeStruct(full, BF16),
        grid_spec=pltpu.PrefetchScalarGridSpec(
            num_scalar_prefetch=1, grid=(nb,), in_specs=[pl.BlockSpec((tr, cc), lambda i, chip_ref: (i, 0))],
            out_specs=pl.BlockSpec((tr, cc), out_map)),
        compiler_params=_cp("parallel"),
    )(chip, w)


def mod_matmul(c_all, w_ada, b_ada):
    D, N = w_ada.shape
    tn = 1024

    def body(c_ref, w_ref, b_ref, o_ref):
        c = c_ref[...]
        sc = c * jax.nn.sigmoid(c)
        o_ref[...] = jnp.dot(sc, w_ref[...], precision=HI, preferred_element_type=F32) + b_ref[...]

    return pl.pallas_call(
        body, name="mod_matmul", out_shape=jax.ShapeDtypeStruct((8, N), F32), grid=(N // tn,),
        in_specs=[pl.BlockSpec((8, D), lambda j: (0, 0)), pl.BlockSpec((D, tn), lambda j: (0, j)),
                  pl.BlockSpec((1, tn), lambda j: (0, j))],
        out_specs=pl.BlockSpec((8, tn), lambda j: (0, j)), compiler_params=_cp("parallel"),
    )(c_all, w_ada, b_ada)


def prenorm(h, g, sc, sh):
    T, D = h.shape
    tm = min(256, T)

    def body(h_ref, g_ref, sc_ref, sh_ref, a_ref):
        x = h_ref[...]
        r = lax.rsqrt(jnp.mean(x * x, axis=-1, keepdims=True) + EPS)
        a_ref[...] = ((x * r) * g_ref[...] * (1.0 + sc_ref[...]) + sh_ref[...]).astype(BF16)

    row = pl.BlockSpec((tm, D), lambda i: (i, 0))
    return pl.pallas_call(
        body, name="prenorm", out_shape=jax.ShapeDtypeStruct((T, D), BF16), grid=(T // tm,),
        in_specs=[row, _vec(D), _vec(D), _vec(D)], out_specs=row, compiler_params=_cp("parallel"),
    )(h, g, sc, sh)


def in_proj_gathered(a, w_full, chip, dims, tail):
    T, D = a.shape
    rows, cc = dims
    tm, tn = min(512, T), cc // 2
    ni = T // tm
    half = rows // 2

    nt = len(tail.operands)

    def body(chip_ref, a_ref, w_in_ref, *rest):
        tail_in, (y_ref, w_ref), rest = rest[:nt], rest[nt:nt + 2], rest[nt + 2:]
        tail_out, (wbuf, wsem, send_sems, recv_sems, tail_send, tail_recv) = rest[:nt], rest[nt:]
        q, j, i = pl.program_id(0), pl.program_id(1), pl.program_id(2)
        mx, my, mc, _ = _place()
        me = chip_ref[0]

        def tile(block, jj):
            src = w_ref.at[:, pl.ds(pl.multiple_of(block * cc + jj * tn, 128), tn)]
            return pltpu.make_async_copy(src, wbuf.at[jj], wsem.at[jj])

        def rows_half(block, hh):
            return w_ref.at[pl.ds(pl.multiple_of(hh * half, 16), half), pl.ds(pl.multiple_of(block * cc, 128), cc)]

        def over_ici(s, block):
            peer = (1 - mx if s & 2 else mx, 1 - my if s & 1 else my, mc)
            reg = rows_half(block, mc)
            return pltpu.make_async_remote_copy(src_ref=reg, dst_ref=reg, send_sem=send_sems.at[s - 1], recv_sem=recv_sems.at[s - 1],
                                                device_id=peer, device_id_type=MESH)

        def over_d2d(s, block, hh):
            reg = rows_half(block, hh)
            return pltpu.make_async_remote_copy(src_ref=reg, dst_ref=reg, send_sem=send_sems.at[2 + s], recv_sem=recv_sems.at[2 + s],
                                                device_id=(mx, my, 1 - mc), device_id_type=MESH)

        def passed_on(s, block):
            k = s - 1
            reg = w_ref.at[pl.ds(pl.multiple_of(mc * half + k * (half // 2), 16), half // 2), pl.ds(pl.multiple_of(block * cc, 128), cc)]
            peer = (1 - mx, my, mc) if s == 1 else (mx, 1 - my, mc)
            return pltpu.make_async_remote_copy(src_ref=reg, dst_ref=reg, send_sem=send_sems.at[6 + k], recv_sem=recv_sems.at[6 + k],
                                                device_id=peer, device_id_type=MESH)

        @pl.when((q == 0) & (j == 0) & (i == 0))
        def _():
            for s in (1, 2):
                over_ici(s, me).start()
            tile(me, 0).start()

        @pl.when(i == 0)
        def _():
            tile(me ^ q, j).wait()

        @pl.when((i == 0) & (j == 0))
        def _():
            tile(me ^ q, 1).start()

        y_ref[...] = _nn(a_ref[...], wbuf[j])

        @pl.when((q == 0) & (j == 1) & (i == ni - 1))
        def _():
            for s in (1, 2):
                over_ici(s, me ^ s).wait_recv()
                passed_on(s, me ^ s).start()
                over_d2d(s, me ^ s, mc).start()
            tail.start(tail_in, tail_out, tail_send, tail_recv)
            over_d2d(1, me ^ 1, 1 - mc).wait_recv()
            tile(me ^ 1, 0).start()

        @pl.when((q == 1) & (j == 1) & (i == ni - 1))
        def _():
            over_d2d(2, me ^ 2, 1 - mc).wait_recv()
            tile(me ^ 2, 0).start()

        @pl.when((q == 2) & (j == 1) & (i == ni - 1))
        def _():
            for s in (1, 2):
                passed_on(s, me ^ 3).wait_recv()
            over_d2d(3, me ^ 3, mc).start()
            over_d2d(3, me ^ 3, 1 - mc).wait_recv()
            tile(me ^ 3, 0).start()

        @pl.when((q == 3) & (j == 1) & (i == ni - 1))
        def _():
            for s in (1, 2):
                over_ici(s, me).wait_send()
                passed_on(s, me ^ s).wait_send()
            for s in (1, 2, 3):
                over_d2d(s, me ^ s, mc).wait_send()
            tail.finish(tail_in, tail_out, tail_send, tail_recv)

    dma = pltpu.SemaphoreType.DMA
    y, w_out, *tail_res = pl.pallas_call(
        body, name="in_proj", out_shape=[jax.ShapeDtypeStruct((T, 4 * cc), F32), jax.ShapeDtypeStruct(w_full.shape, BF16)] + tail.out_shape,
        grid_spec=pltpu.PrefetchScalarGridSpec(
            num_scalar_prefetch=1, grid=(4, 2, ni),
            in_specs=[pl.BlockSpec((tm, D), lambda q, j, i, chip_ref: (i, 0)), ANY] + [ANY] * nt,
            out_specs=[pl.BlockSpec((tm, tn), lambda q, j, i, chip_ref: (i, (chip_ref[0] ^ q) * 2 + j)), ANY] + [ANY] * nt,
            scratch_shapes=[pltpu.VMEM((2, D, tn), BF16), dma((2,)), dma((8,)), dma((8,)), dma((tail.n_sems,)), dma((tail.n_sems,))]),
        input_output_aliases={2: 1, **{3 + i: 2 + o for i, o in tail.aliases.items()}},
        compiler_params=_cp("arbitrary", "arbitrary", "arbitrary"),
    )(chip, a, w_full, *tail.operands)
    return y, w_out, tail_res


def _hgrn_lower_bound(l_ref):
    l0, l1 = l_ref[0:1, :], l_ref[1:2, :]
    m = jnp.maximum(l0, l1)
    e0, e1 = jnp.exp(l0 - m), jnp.exp(l1 - m)
    return e0 / (e0 + e1)


def _hgrn_chunk_mask(d, blk):
    r = lax.broadcasted_iota(jnp.int32, (blk, blk), 0)
    c = lax.broadcasted_iota(jnp.int32, (blk, blk), 1)
    same = (r // HGRN_CHUNK) == (c // HGRN_CHUNK)
    fwd = d == 0
    return same & (((c <= r) & fwd) | ((c >= r) & jnp.logical_not(fwd)))


def _chunk_total(x):
    x3 = x.reshape(x.shape[0] // HGRN_CHUNK, HGRN_CHUNK, x.shape[1])
    return jnp.broadcast_to(jnp.sum(x3, axis=1, keepdims=True), x3.shape).reshape(x.shape)


def _chunk_cumsum(x, suffix):
    pos = lax.broadcasted_iota(jnp.int32, x.shape, 0) % HGRN_CHUNK
    p, s = x, 1
    while s < HGRN_CHUNK:
        p = p + jnp.where(pos >= s, pltpu.roll(p, s, 0), 0.0)
        s *= 2
    return jnp.where(suffix, _chunk_total(x) - p + x, p)


def _block_loop(T, blk, body, init):
    n = T // blk
    return lax.fori_loop(0, n, body, init, unroll=2 if n % 2 == 0 else 1)


def _hgrn_gate(f, lb):
    s = jax.nn.sigmoid(f)
    sn = jax.nn.sigmoid(-f)
    fg = lb + (1.0 - lb) * s
    return s, sn, fg, jnp.log(fg), (1.0 - lb) * sn


def _hgrn_specs(T):
    col = lambda base: pl.BlockSpec((T, HEAD_DIM), lambda h, d: (0, base * N_HEADS + h))
    f_spec = pl.BlockSpec((T, HEAD_DIM), lambda h, d: (0, COL_FFW * N_HEADS + N_HEADS * d + h))
    l_spec = pl.BlockSpec((None, 2, HEAD_DIM), lambda h, d: (d, 0, h))
    return col, f_spec, l_spec


def hgrn_fwd(proj, lb_logits, comm=None):
    T = proj.shape[0]
    blk = min(HGRN_BLOCK_FWD, T)
    NC, CPB = T // HGRN_CHUNK, blk // HGRN_CHUNK
    col, f_spec, l_spec = _hgrn_specs(T)

    def body(l_ref, q_ref, f_ref, v_ref, o_ref, st_ref, dec_ref, qd_ref):
        d = pl.program_id(1)
        lb = _hgrn_lower_bound(l_ref)
        mask = _hgrn_chunk_mask(d, blk)

        def block(i, carry):
            rows = pl.ds(pl.multiple_of(i * blk, blk), blk)
            _, _, _, lf, k = _hgrn_gate(f_ref[rows, :], lb)
            b = _chunk_cumsum(lf, d == 1)
            bl = _chunk_total(lf)
            qd = (q_ref[rows, :] * Q_SCALE * jnp.exp(b)).astype(BF16)
            kd = (k * jnp.exp(-b)).astype(BF16)
            ke = (k * jnp.exp(bl - b)).astype(BF16)
            vb = v_ref[rows, :].astype(BF16)
            att = jnp.where(mask, _nt(qd, kd), 0.0).astype(BF16)
            o_ref[rows, :] = jnp.where(d == 0, 0.0, o_ref[rows, :]) + _nn(att, vb)
            qd_ref[rows, :] = qd
            dec = jnp.exp(bl)
            for cc in range(CPB):
                sl = slice(cc * HGRN_CHUNK, (cc + 1) * HGRN_CHUNK)
                n = i * CPB + cc
                st_ref[n] = _tn(vb[sl], ke[sl])
                dec_ref[n] = dec[cc * HGRN_CHUNK:cc * HGRN_CHUNK + 8, :]
            return carry

        _block_loop(T, blk, block, 0)

        def scan(t, s):
            n = jnp.where(d == 0, t, NC - 1 - t)
            u = st_ref[n]
            st_ref[n] = s
            return dec_ref[n][0:1, :] * s + u

        lax.fori_loop(0, NC, scan, jnp.zeros((HEAD_DIM, HEAD_DIM), F32))

        def inter(i, carry):
            rows = pl.ds(pl.multiple_of(i * blk, blk), blk)
            qd = qd_ref[rows, :]
            o_ref[rows, :] += jnp.concatenate(
                [_nt(qd[cc * HGRN_CHUNK:(cc + 1) * HGRN_CHUNK], st_ref[i * CPB + cc].astype(BF16)) for cc in range(CPB)], axis=0)
            return carry

        _block_loop(T, blk, inter, 0)

    (o,), landed = _pallas(
        body, name="hgrn_fwd", grid=(N_HEADS, 2), in_specs=[l_spec, col(COL_Q), f_spec, col(COL_V)],
        out_specs=[pl.BlockSpec((T, HEAD_DIM), lambda h, d: (0, h))], out_shape=[jax.ShapeDtypeStruct((T, N_HEADS * HEAD_DIM), F32)],
        scratch=[pltpu.VMEM((NC, HEAD_DIM, HEAD_DIM), F32), pltpu.VMEM((NC, 8, HEAD_DIM), F32), pltpu.VMEM((T, HEAD_DIM), BF16)],
        semantics=("parallel", "arbitrary"), operands=(lb_logits, proj, proj, proj), comm=comm)
    return o if comm is None else (o, landed)


def hgrn_post_fwd(o, proj, g_norm):
    T, W = o.shape
    tm = min(256, T)

    def body(o_ref, og_ref, g_ref, y_ref):
        g = g_ref[...]
        for h in range(N_HEADS):
            sl = slice(h * HEAD_DIM, (h + 1) * HEAD_DIM)
            x = o_ref[:, sl]
            r = lax.rsqrt(jnp.mean(x * x, axis=-1, keepdims=True) + EPS)
            og = og_ref[:, sl]
            y_ref[:, sl] = ((x * r) * g * (og * jax.nn.sigmoid(og))).astype(BF16)

    return pl.pallas_call(
        body, name="hgrn_post_fwd", out_shape=jax.ShapeDtypeStruct((T, W), BF16), grid=(T // tm,),
        in_specs=[pl.BlockSpec((tm, W), lambda i: (i, 0)), pl.BlockSpec((tm, W), lambda i: (i, COL_OG)), _vec(HEAD_DIM)],
        out_specs=pl.BlockSpec((tm, W), lambda i: (i, 0)), compiler_params=_cp("parallel"),
    )(o, proj, g_norm)


def _gelu(x):
    return 0.5 * x * (1.0 + lax.erf(x * (1.0 / math.sqrt(2.0))))


def _gelu_grad(x):
    return 0.5 * (1.0 + lax.erf(x * (1.0 / math.sqrt(2.0)))) + x * jnp.exp(-0.5 * x * x) * (1.0 / math.sqrt(2.0 * math.pi))


def _sgu_mix(u_ref, v_ref, g_ref, ws_ref, bst_ref):
    W = u_ref.shape[1]
    zu, zv = _gelu(u_ref[...]), _gelu(v_ref[...])
    dv = zv - jnp.mean(zv, axis=-1, keepdims=True)
    rstd = lax.rsqrt(jnp.mean(dv * dv, axis=-1, keepdims=True) + EPS)
    dhat = dv * rstd
    vn = (dhat * g_ref[...]).astype(BF16)
    gw = W // SGU_GROUPS
    vm = [_nn(ws_ref[g].astype(BF16), vn[:, g * gw:(g + 1) * gw]) + bst_ref[:, g:g + 1] for g in range(SGU_GROUPS)]
    return zu, rstd, dhat, vn, jnp.concatenate(vm, axis=1)


def sgu_fwd(proj, g_norm, w_spatial, b_spatial_t):
    T = proj.shape[0]
    W = 1024
    n_chunks = T // SGU_CHUNK

    def body(u_ref, v_ref, g_ref, ws_ref, bst_ref, y_ref):
        zu, _, _, _, vm = _sgu_mix(u_ref, v_ref, g_ref, ws_ref, bst_ref)
        y_ref[...] = (zu * vm).astype(BF16)

    blk = lambda cb: pl.BlockSpec((SGU_CHUNK, W), lambda i: (i, cb))
    return pl.pallas_call(
        body, name="sgu_fwd", out_shape=jax.ShapeDtypeStruct((T, W), BF16), grid=(n_chunks,),
        in_specs=[blk(COL_U), blk(COL_ZV), _vec(W), pl.BlockSpec((SGU_GROUPS, SGU_CHUNK, SGU_CHUNK), lambda i: (0, 0, 0)),
                  pl.BlockSpec((SGU_CHUNK, SGU_GROUPS), lambda i: (0, 0))],
        out_specs=blk(0), compiler_params=_cp("parallel"),
    )(proj, proj, g_norm, w_spatial, b_spatial_t)


def merge_matmul(ya_pre, sgu, w_a, w_b, proj):
    T, K = ya_pre.shape
    N = w_a.shape[1]
    tm, tn = min(512, T), 512
    gpb = 1024 // tn

    def body(a_ref, b_ref, wa_ref, wb_ref, ga_ref, gb_ref, ya_ref, yb_ref, m_ref):
        ya = _nn(a_ref[...], wa_ref[...])
        yb = _nn(b_ref[...], wb_ref[...])
        ya_ref[...] = ya.astype(BF16)
        yb_ref[...] = yb.astype(BF16)
        m_ref[...] = (jax.nn.sigmoid(ga_ref[...]) * ya + jax.nn.sigmoid(gb_ref[...]) * yb).astype(BF16)

    lhs = pl.BlockSpec((tm, K), lambda i, j: (i, 0))
    rhs = pl.BlockSpec((K, tn), lambda i, j: (0, j))
    out = pl.BlockSpec((tm, tn), lambda i, j: (i, j))
    return pl.pallas_call(
        body, name="merge_matmul", grid=(T // tm, N // tn),
        out_shape=[jax.ShapeDtypeStruct((T, N), BF16)] * 3,
        in_specs=[lhs, lhs, rhs, rhs, pl.BlockSpec((tm, tn), lambda i, j: (i, COL_GA * gpb + j)),
                  pl.BlockSpec((tm, tn), lambda i, j: (i, COL_GB * gpb + j))],
        out_specs=[out, out, out], compiler_params=_cp("parallel", "parallel"),
    )(ya_pre, sgu, w_a, w_b, proj, proj)


def out_proj(merged, w_o, h0, gt1, g_post, g_pre2, sc2, sh2):
    T, D = h0.shape
    tm = min(256, T)

    def body(m_ref, w_ref, h_ref, gt_ref, gp_ref, g2_ref, sc_ref, sh_ref, mo_ref, h1_ref, a2_ref):
        mo = _nn(m_ref[...], w_ref[...])
        mo_ref[...] = mo
        r = lax.rsqrt(jnp.mean(mo * mo, axis=-1, keepdims=True) + EPS)
        h1 = h_ref[...] + gt_ref[...] * ((mo * r) * gp_ref[...])
        h1_ref[...] = h1
        r2 = lax.rsqrt(jnp.mean(h1 * h1, axis=-1, keepdims=True) + EPS)
        a2_ref[...] = ((h1 * r2) * g2_ref[...] * (1.0 + sc_ref[...]) + sh_ref[...]).astype(BF16)

    row = pl.BlockSpec((tm, D), lambda i: (i, 0))
    return pl.pallas_call(
        body, name="out_proj", grid=(T // tm,),
        out_shape=[jax.ShapeDtypeStruct((T, D), F32), jax.ShapeDtypeStruct((T, D), F32), jax.ShapeDtypeStruct((T, D), BF16)],
        in_specs=[row, pl.BlockSpec((D, D), lambda i: (0, 0)), row] + [_vec(D)] * 5,
        out_specs=[row, row, row], compiler_params=_cp("parallel"),
    )(merged, w_o, h0, gt1, g_post, g_pre2, sc2, sh2)


def loss_bwd(ff, h1, tgt, gt2, g_post):
    T, D = ff.shape
    tm = min(256, T)

    def body(f_ref, h_ref, t_ref, gt_ref, g_ref, dy_ref, dff_ref, loss_ref, dgt_ref, dg_ref):
        @pl.when(pl.program_id(0) == 0)
        def _():
            loss_ref[...] = jnp.zeros_like(loss_ref)
            dgt_ref[...] = jnp.zeros_like(dgt_ref)
            dg_ref[...] = jnp.zeros_like(dg_ref)

        ff = f_ref[...]
        gt, g = gt_ref[...], g_ref[...]
        r = lax.rsqrt(jnp.mean(ff * ff, axis=-1, keepdims=True) + EPS)
        fhat = ff * r
        nf = fhat * g
        err = (h_ref[...] + gt * nf) - t_ref[...]
        loss_ref[...] += jnp.sum(err * err)
        dy = err * (1.0 / D)
        dy_ref[...] = dy
        dgt_ref[...] += _colsum(dy * nf)
        dnf = dy * gt
        dg_ref[...] += _colsum(dnf * fhat)
        u = dnf * g
        dff_ref[...] = (r * (u - fhat * jnp.mean(u * fhat, axis=-1, keepdims=True))).astype(BF16)

    row = pl.BlockSpec((tm, D), lambda i: (i, 0))
    return pl.pallas_call(
        body, name="loss_bwd", grid=(T // tm,),
        out_shape=[jax.ShapeDtypeStruct((T, D), F32), jax.ShapeDtypeStruct((T, D), BF16), jax.ShapeDtypeStruct((8, 128), F32),
                   jax.ShapeDtypeStruct((1, D), F32), jax.ShapeDtypeStruct((1, D), F32)],
        in_specs=[row, row, row, _vec(D), _vec(D)],
        out_specs=[row, row, pl.BlockSpec((8, 128), lambda i: (0, 0)), _vec(D), _vec(D)],
        compiler_params=_cp("arbitrary"),
    )(ff, h1, tgt, gt2, g_post)


def ff2_bwd(dff, w_ff2, f1):
    T, D = dff.shape
    K = w_ff2.shape[0]
    tm, tn = min(1024, T), 2048

    def body(a_ref, w_ref, f_ref, o_ref):
        o_ref[...] = (_nt(a_ref[...], w_ref[...]) * (2.0 * jnp.maximum(f_ref[...].astype(F32), 0.0))).astype(BF16)

    return pl.pallas_call(
        body, name="ff2_bwd", out_shape=jax.ShapeDtypeStruct((T, K), BF16), grid=(K // tn, T // tm),
        in_specs=[pl.BlockSpec((tm, D), lambda j, i: (i, 0)), pl.BlockSpec((tn, D), lambda j, i: (j, 0)),
                  pl.BlockSpec((tm, tn), lambda j, i: (i, j))],
        out_specs=pl.BlockSpec((tm, tn), lambda j, i: (i, j)), compiler_params=_cp("parallel", "parallel"),
    )(dff, w_ff2, f1)


def ffn_norm_bwd(dy, da2, h1, mo, g_pre2, sc2, gt1, g_post, comm):
    T, D = dy.shape
    tm = min(256, T)

    def body(dy_ref, da_ref, h_ref, mo_ref, g2_ref, sc_ref, gt_ref, gp_ref, dh_ref, dmo_ref, s_sh, s_sc, s_g2, s_gt, s_gp):
        @pl.when(pl.program_id(0) == 0)
        def _():
            for s in (s_sh, s_sc, s_g2, s_gt, s_gp):
                s[...] = jnp.zeros_like(s)

        h1, da = h_ref[...], da_ref[...]
        g2, sc = g2_ref[...], sc_ref[...]
        r2 = lax.rsqrt(jnp.mean(h1 * h1, axis=-1, keepdims=True) + EPS)
        n2 = h1 * r2
        s_sh[...] += _colsum(da)
        s_sc[...] += _colsum(da * (n2 * g2))
        s_g2[...] += _colsum(da * (1.0 + sc) * n2)
        dn2 = da * g2 * (1.0 + sc)
        dh1 = dy_ref[...] + r2 * (dn2 - n2 * jnp.mean(dn2 * n2, axis=-1, keepdims=True))
        dh_ref[...] = dh1
        mo = mo_ref[...]
        gt, gp = gt_ref[...], gp_ref[...]
        r = lax.rsqrt(jnp.mean(mo * mo, axis=-1, keepdims=True) + EPS)
        mhat = mo * r
        s_gt[...] += _colsum(dh1 * (mhat * gp))
        dnm = dh1 * gt
        s_gp[...] += _colsum(dnm * mhat)
        u = dnm * gp
        dmo_ref[...] = (r * (u - mhat * jnp.mean(u * mhat, axis=-1, keepdims=True))).astype(BF16)

    row = pl.BlockSpec((tm, D), lambda i: (i, 0))
    vec_out = jax.ShapeDtypeStruct((1, D), F32)
    return _pallas(
        body, name="ffn_norm_bwd", grid=(T // tm,),
        out_shape=[jax.ShapeDtypeStruct((T, D), F32), jax.ShapeDtypeStruct((T, D), BF16)] + [vec_out] * 5,
        in_specs=[row, row, row, row] + [_vec(D)] * 4, out_specs=[row, row] + [_vec(D)] * 5,
        scratch=[], semantics=("arbitrary",), operands=(dy, da2, h1, mo, g_pre2, sc2, gt1, g_post), comm=comm)


def out_proj_bwd(dmo, w_o, y_a, y_b, proj):
    T, D = dmo.shape
    tm, tn = min(512, T), 512
    gpb = 1024 // tn

    def body(a_ref, w_ref, ya_ref, yb_ref, ga_ref, gb_ref, dya_ref, dyb_ref, dga_ref, dgb_ref):
        dm = _nt(a_ref[...], w_ref[...])
        sa, sb = jax.nn.sigmoid(ga_ref[...]), jax.nn.sigmoid(gb_ref[...])
        dya_ref[...] = (dm * sa).astype(BF16)
        dyb_ref[...] = (dm * sb).astype(BF16)
        dga_ref[...] = (dm * ya_ref[...].astype(F32) * sa * (1.0 - sa)).astype(BF16)
        dgb_ref[...] = (dm * yb_ref[...].astype(F32) * sb * (1.0 - sb)).astype(BF16)

    out = pl.BlockSpec((tm, tn), lambda i, j: (i, j))
    return pl.pallas_call(
        body, name="out_proj_bwd", grid=(T // tm, D // tn), out_shape=[jax.ShapeDtypeStruct((T, D), BF16)] * 4,
        in_specs=[pl.BlockSpec((tm, D), lambda i, j: (i, 0)), pl.BlockSpec((tn, D), lambda i, j: (j, 0)), out, out,
                  pl.BlockSpec((tm, tn), lambda i, j: (i, COL_GA * gpb + j)), pl.BlockSpec((tm, tn), lambda i, j: (i, COL_GB * gpb + j))],
        out_specs=[out] * 4, compiler_params=_cp("parallel", "parallel"),
    )(dmo, w_o, y_a, y_b, proj, proj)


def sgu_bwd(proj, dsgu, g_norm, w_spatial, b_spatial_t):
    T = proj.shape[0]
    W = 1024
    gw = W // SGU_GROUPS

    def body(u_ref, v_ref, ds_ref, g_ref, ws_ref, bst_ref, dz_ref, dw_ref, db_ref, dg_ref):
        @pl.when(pl.program_id(0) == 0)
        def _():
            dw_ref[...] = jnp.zeros_like(dw_ref)
            db_ref[...] = jnp.zeros_like(db_ref)
            dg_ref[...] = jnp.zeros_like(dg_ref)

        zu, rstd, dhat, vn, vm = _sgu_mix(u_ref, v_ref, g_ref, ws_ref, bst_ref)
        ds = ds_ref[...]
        du = ds * vm
        dvm = ds * zu
        dvm_b = dvm.astype(BF16)
        ones = jnp.ones((8, gw), F32)
        dvn = []
        for g in range(SGU_GROUPS):
            sl = slice(g * gw, (g + 1) * gw)
            dw_ref[g] += _nt(dvm_b[:, sl], vn[:, sl])
            db_ref[g] += lax.dot_general(ones, dvm[:, sl], (((1,), (1,)), ((), ())), precision=HI, preferred_element_type=F32)
            dvn.append(_tn(ws_ref[g].astype(BF16), dvm_b[:, sl]))
        dvn = jnp.concatenate(dvn, axis=1)
        dg_ref[...] += _colsum(dvn * dhat)
        ddh = dvn * g_ref[...]
        dzv = rstd * (ddh - jnp.mean(ddh, axis=-1, keepdims=True) - dhat * jnp.mean(ddh * dhat, axis=-1, keepdims=True))
        dz_ref[:, 0:W] = (du * _gelu_grad(u_ref[...])).astype(BF16)
        dz_ref[:, W:2 * W] = (dzv * _gelu_grad(v_ref[...])).astype(BF16)

    blk = lambda cb: pl.BlockSpec((SGU_CHUNK, W), lambda i: (i, cb))
    full3 = lambda a, b, c: pl.BlockSpec((a, b, c), lambda i: (0, 0, 0))
    return pl.pallas_call(
        body, name="sgu_bwd", grid=(T // SGU_CHUNK,),
        out_shape=[jax.ShapeDtypeStruct((T, 2 * W), BF16), jax.ShapeDtypeStruct((SGU_GROUPS, SGU_CHUNK, SGU_CHUNK), F32),
                   jax.ShapeDtypeStruct((SGU_GROUPS, 8, SGU_CHUNK), F32), jax.ShapeDtypeStruct((1, W), F32)],
        in_specs=[blk(COL_U), blk(COL_ZV), blk(0), _vec(W), full3(SGU_GROUPS, SGU_CHUNK, SGU_CHUNK),
                  pl.BlockSpec((SGU_CHUNK, SGU_GROUPS), lambda i: (0, 0))],
        out_specs=[pl.BlockSpec((SGU_CHUNK, 2 * W), lambda i: (i, 0)), full3(SGU_GROUPS, SGU_CHUNK, SGU_CHUNK),
                   full3(SGU_GROUPS, 8, SGU_CHUNK), _vec(W)],
        compiler_params=_cp("arbitrary"),
    )(proj, proj, dsgu, g_norm, w_spatial, b_spatial_t)


def hgrn_post_bwd(dya, o, proj, g_norm, comm):
    T, W = o.shape
    tm = min(256, T)

    def body(dy_ref, o_ref, og_ref, g_ref, do_ref, dog_ref, dg_ref):
        @pl.when(pl.program_id(0) == 0)
        def _():
            dg_ref[...] = jnp.zeros_like(dg_ref)

        g = g_ref[...]
        dg = jnp.zeros((1, HEAD_DIM), F32)
        for h in range(N_HEADS):
            sl = slice(h * HEAD_DIM, (h + 1) * HEAD_DIM)
            x, og, dy = o_ref[:, sl], og_ref[:, sl], dy_ref[:, sl]
            r = lax.rsqrt(jnp.mean(x * x, axis=-1, keepdims=True) + EPS)
            xhat = x * r
            s = jax.nn.sigmoid(og)
            don = dy * (og * s)
            dog_ref[:, sl] = (dy * (xhat * g) * (s * (1.0 + og * (1.0 - s)))).astype(BF16)
            dg += _colsum(don * xhat)
            u = don * g
            do_ref[:, sl] = r * (u - xhat * jnp.mean(u * xhat, axis=-1, keepdims=True))
        dg_ref[...] += dg

    row = pl.BlockSpec((tm, W), lambda i: (i, 0))
    return _pallas(
        body, name="hgrn_post_bwd", grid=(T // tm,),
        out_shape=[jax.ShapeDtypeStruct((T, W), F32), jax.ShapeDtypeStruct((T, W), BF16), jax.ShapeDtypeStruct((1, HEAD_DIM), F32)],
        in_specs=[row, row, pl.BlockSpec((tm, W), lambda i: (i, COL_OG)), _vec(HEAD_DIM)],
        out_specs=[row, row, _vec(HEAD_DIM)], scratch=[], semantics=("arbitrary",), operands=(dya, o, proj, g_norm), comm=comm)


def hgrn_bwd(proj, do, lb_logits, comm=None):
    T = proj.shape[0]
    NC, CPB = T // HGRN_CHUNK, HGRN_BLOCK // HGRN_CHUNK
    blk1 = min(HGRN_BLOCK_FWD, T)
    W = N_HEADS * HEAD_DIM
    col, f_spec, l_spec = _hgrn_specs(T)

    def body(l_ref, q_ref, f_ref, v_ref, do_ref, dq_ref, dv_ref, dlg_ref, dlb_ref, st_ref, dst_ref, dec_ref, ddec_ref, dqa_ref, dva_ref):
        d = pl.program_id(1)
        lb = _hgrn_lower_bound(l_ref)
        oml = 1.0 - lb
        mask = _hgrn_chunk_mask(d, HGRN_BLOCK)

        def values(rows):
            s, sn, fg, lf, k = _hgrn_gate(f_ref[rows, :], lb)
            b = _chunk_cumsum(lf, d == 1)
            bl = _chunk_total(lf)
            eb, enb, ee = jnp.exp(b), jnp.exp(-b), jnp.exp(bl - b)
            qd = q_ref[rows, :] * Q_SCALE * eb
            return s, sn, fg, k, bl, eb, enb, ee, qd, k * enb, k * ee

        def block1(i, carry):
            rows = pl.ds(pl.multiple_of(i * blk1, blk1), blk1)
            _, _, _, _, bl, _, _, _, qd, _, ke = values(rows)
            qd, ke = qd.astype(BF16), ke.astype(BF16)
            vb, dob = v_ref[rows, :].astype(BF16), do_ref[rows, :].astype(BF16)
            dec = jnp.exp(bl)
            for cc in range(blk1 // HGRN_CHUNK):
                sl = slice(cc * HGRN_CHUNK, (cc + 1) * HGRN_CHUNK)
                n = i * (blk1 // HGRN_CHUNK) + cc
                st_ref[n] = _tn(vb[sl], ke[sl])
                dst_ref[n] = _tn(dob[sl], qd[sl])
                dec_ref[n] = dec[cc * HGRN_CHUNK:cc * HGRN_CHUNK + 8, :]
            return carry

        _block_loop(T, blk1, block1, 0)

        def scan(t, s):
            n = jnp.where(d == 0, t, NC - 1 - t)
            u = st_ref[n]
            st_ref[n] = s
            return dec_ref[n][0:1, :] * s + u

        lax.fori_loop(0, NC, scan, jnp.zeros((HEAD_DIM, HEAD_DIM), F32))

        def rscan(t, ds):
            n = jnp.where(d == 0, NC - 1 - t, t)
            w = dst_ref[n]
            dst_ref[n] = ds
            ddec_ref[n] = jnp.broadcast_to(_colsum(ds * st_ref[n]), (8, HEAD_DIM))
            return dec_ref[n][0:1, :] * ds + w

        lax.fori_loop(0, NC, rscan, jnp.zeros((HEAD_DIM, HEAD_DIM), F32))

        def block3(i, dlb):
            rows = pl.ds(pl.multiple_of(i * HGRN_BLOCK, HGRN_BLOCK), HGRN_BLOCK)
            s, sn, fg, k, bl, eb, enb, ee, qd, kd, ke = values(rows)
            qdb, kdb, keb = qd.astype(BF16), kd.astype(BF16), ke.astype(BF16)
            vb, dob = v_ref[rows, :].astype(BF16), do_ref[rows, :].astype(BF16)
            att = jnp.where(mask, _nt(qdb, kdb), 0.0).astype(BF16)
            datt = jnp.where(mask, _nt(dob, vb), 0.0).astype(BF16)
            dv = _tn(att, dob)
            dqd = _nn(datt, kdb)
            dkd = _tn(datt, qdb)
            dv_i, dqd_i, dke, ddl = [], [], [], []
            for cc in range(CPB):
                sl = slice(cc * HGRN_CHUNK, (cc + 1) * HGRN_CHUNK)
                n = i * CPB + cc
                st_b, dst_b = st_ref[n].astype(BF16), dst_ref[n].astype(BF16)
                dv_i.append(_nt(keb[sl], dst_b))
                dqd_i.append(_nn(dob[sl], st_b))
                dke.append(_nn(vb[sl], dst_b))
                ddl.append(jnp.broadcast_to(ddec_ref[n][0:1, :] * dec_ref[n][0:1, :], (HGRN_CHUNK, HEAD_DIM)))
            dv = dv + jnp.concatenate(dv_i, axis=0)
            dqd = dqd + jnp.concatenate(dqd_i, axis=0)
            dke = jnp.concatenate(dke, axis=0)
            dq = dqd * eb * Q_SCALE
            dk = dkd * enb + dke * ee
            t_end = dke * ke
            db = dqd * qd - dkd * kd - t_end
            dlf = _chunk_cumsum(db, d == 0) + _chunk_total(t_end) + jnp.concatenate(ddl, axis=0)
            e = dlf / fg - dk
            dlg_ref[rows, :] = (oml * e * s * sn).astype(BF16)

            dq = jnp.where(d == 0, 0.0, dqa_ref[rows, :]) + dq
            dv = jnp.where(d == 0, 0.0, dva_ref[rows, :]) + dv
            dqa_ref[rows, :] = dq
            dva_ref[rows, :] = dv
            dq_ref[rows, :] = dq.astype(BF16)
            dv_ref[rows, :] = dv.astype(BF16)

            return dlb + _colsum(e * sn)

        dlb_ref[...] = _block_loop(T, HGRN_BLOCK, block3, jnp.zeros((1, HEAD_DIM), F32))

    head = pl.BlockSpec((T, HEAD_DIM), lambda h, d: (0, h))
    big = pltpu.VMEM((NC, HEAD_DIM, HEAD_DIM), F32)
    small = pltpu.VMEM((NC, 8, HEAD_DIM), F32)
    acc = pltpu.VMEM((T, HEAD_DIM), F32)
    outs, landed = _pallas(
        body, name="hgrn_bwd", grid=(N_HEADS, 2),
        out_shape=[jax.ShapeDtypeStruct((T, W), BF16), jax.ShapeDtypeStruct((T, W), BF16), jax.ShapeDtypeStruct((T, 2 * W), BF16),
                   jax.ShapeDtypeStruct((2, 1, W), F32)],
        in_specs=[l_spec, col(COL_Q), f_spec, col(COL_V), head],
        out_specs=[head, head, pl.BlockSpec((T, HEAD_DIM), lambda h, d: (0, N_HEADS * d + h)),
                   pl.BlockSpec((None, 1, HEAD_DIM), lambda h, d: (d, 0, h))],
        scratch=[big, big, small, small, acc, acc], semantics=("parallel", "arbitrary"), operands=(lb_logits, proj, proj, proj, do), comm=comm)
    return outs if comm is None else (outs, landed)


def mix_norm_bwd(da1, h0, dh1, g_pre, sc1):
    T, D = h0.shape
    tm = min(256, T)

    def body(da_ref, h_ref, dh_ref, g_ref, sc_ref, gx_ref, s_sh, s_sc, s_g):
        @pl.when(pl.program_id(0) == 0)
        def _():
            for s in (s_sh, s_sc, s_g):
                s[...] = jnp.zeros_like(s)

        h, da = h_ref[...], da_ref[...]
        g, sc = g_ref[...], sc_ref[...]
        r = lax.rsqrt(jnp.mean(h * h, axis=-1, keepdims=True) + EPS)
        n = h * r
        s_sh[...] += _colsum(da)
        s_sc[...] += _colsum(da * (n * g))
        s_g[...] += _colsum(da * (1.0 + sc) * n)
        dn = da * g * (1.0 + sc)
        gx_ref[...] = dh_ref[...] + r * (dn - n * jnp.mean(dn * n, axis=-1, keepdims=True))

    row = pl.BlockSpec((tm, D), lambda i: (i, 0))
    return pl.pallas_call(
        body, name="mix_norm_bwd", grid=(T // tm,),
        out_shape=[jax.ShapeDtypeStruct((T, D), F32)] + [jax.ShapeDtypeStruct((1, D), F32)] * 3,
        in_specs=[row, row, row, _vec(D), _vec(D)], out_specs=[row] + [_vec(D)] * 3, compiler_params=_cp("arbitrary"),
    )(da1, h0, dh1, g_pre, sc1)


def adamw(w, g, m, v, name):
    R, C = w.shape
    tr = R if R * C * 4 <= (1 << 21) else max(8, ((1 << 21) // (C * 4)) // 8 * 8)
    while R % tr:
        tr -= 8

    def body(w_ref, g_ref, m_ref, v_ref, d_ref, m2_ref, v2_ref):
        d_ref[...], m2_ref[...], v2_ref[...] = _adamw(w_ref[...], g_ref[...], m_ref[...], v_ref[...])

    row = pl.BlockSpec((tr, C), lambda i: (i, 0))
    return pl.pallas_call(
        body, name=name, grid=(R // tr,), out_shape=[jax.ShapeDtypeStruct((R, C), F32)] * 3,
        in_specs=[row] * 4, out_specs=[row] * 3, compiler_params=_cp("parallel"),
    )(w, g, m, v)


def wada_update(c_all, dmod, w, m, v):
    D, N = w.shape
    tm, tn = 512, 1024

    def body(c_ref, dm_ref, w_ref, m_ref, v_ref, g_ref, d_ref, m2_ref, v2_ref):
        c = c_ref[...]
        g = lax.dot_general(c * jax.nn.sigmoid(c), dm_ref[...], (((0,), (0,)), ((), ())), precision=HI, preferred_element_type=F32)
        g_ref[...] = g
        d_ref[...], m2_ref[...], v2_ref[...] = _adamw(w_ref[...], g, m_ref[...], v_ref[...])

    blk = pl.BlockSpec((tm, tn), lambda i, j: (i, j))
    return pl.pallas_call(
        body, name="wada_update", grid=(D // tm, N // tn), out_shape=[jax.ShapeDtypeStruct((D, N), F32)] * 4,
        in_specs=[pl.BlockSpec((8, tm), lambda i, j: (0, i)), pl.BlockSpec((8, tn), lambda i, j: (0, j)), blk, blk, blk],
        out_specs=[blk] * 4, compiler_params=_cp("parallel", "parallel"),
    )(c_all, dmod, w, m, v)


def sum_devices(gathered, name):
    n, R, C = gathered.shape

    def body(g_ref, o_ref):
        s = g_ref[0]
        for i in range(1, n):
            s = s + g_ref[i]
        o_ref[...] = s

    return pl.pallas_call(body, name=name, out_shape=jax.ShapeDtypeStruct((R, C), F32), compiler_params=_cp())(gathered)


def lb_logits_grad(dlb, lb_logits):
    def body(d_ref, l_ref, o_ref):
        for d in range(2):
            l0, l1 = l_ref[d, 0:1, :], l_ref[d, 1:2, :]
            m = jnp.maximum(l0, l1)
            e0, e1 = jnp.exp(l0 - m), jnp.exp(l1 - m)
            p0, p1 = e0 / (e0 + e1), e1 / (e0 + e1)
            g = d_ref[d:d + 1, :]
            o_ref[d, 0:1, :] = p0 * (g - p0 * g)
            o_ref[d, 1:2, :] = -p1 * (p0 * g)

    return pl.pallas_call(body, name="lb_logits_grad", out_shape=jax.ShapeDtypeStruct(lb_logits.shape, F32), compiler_params=_cp())(dlb, lb_logits)


def add_halves(g, landed, core):
    nj, _, r, cc = g.shape
    tr = min(256, r)

    def body(core_ref, g_ref, l_ref, o_ref):
        o_ref[...] = (g_ref[...].astype(F32) + l_ref[...].astype(F32)).astype(BF16)

    return pl.pallas_call(
        body, name="add_halves_%dx%d" % (r, cc), out_shape=jax.ShapeDtypeStruct((nj, r, cc), BF16),
        grid_spec=pltpu.PrefetchScalarGridSpec(
            num_scalar_prefetch=1, grid=(nj, r // tr),
            in_specs=[pl.BlockSpec((None, None, tr, cc), lambda j, i, core_ref: (j, core_ref[0], i, 0)),
                      pl.BlockSpec((None, None, tr, cc), lambda j, i, core_ref: (j, 0, i, 0))],
            out_specs=pl.BlockSpec((None, tr, cc), lambda j, i, core_ref: (j, i, 0))),
        compiler_params=_cp("parallel", "parallel"),
    )(core, g, landed)


def sum_chips(parts, landed, chip):
    nj, r, cc = parts.shape
    tr = min(256, r)

    def body(chip_ref, p_ref, l_ref, o_ref):
        mine = p_ref[...].astype(F32)
        s = None
        for j in range(nj):
            t = jnp.where(chip_ref[0] == j, mine, l_ref[j].astype(F32))
            s = t if s is None else s + t
        o_ref[...] = s

    return pl.pallas_call(
        body, name="sum_chips_%dx%d" % (r, cc), out_shape=jax.ShapeDtypeStruct((r, cc), F32),
        grid_spec=pltpu.PrefetchScalarGridSpec(
            num_scalar_prefetch=1, grid=(r // tr,),
            in_specs=[pl.BlockSpec((None, tr, cc), lambda i, chip_ref: (chip_ref[0], i, 0)),
                      pl.BlockSpec((nj, tr, cc), lambda i, chip_ref: (0, i, 0))],
            out_specs=pl.BlockSpec((tr, cc), lambda i, chip_ref: (i, 0))),
        compiler_params=_cp("parallel"),
    )(chip, parts, landed)


def adamw_halves(w, own, other, m, v, core, name):
    r, cc = own.shape
    tr = min(128, r)
    nb = r // tr

    def body(core_ref, w_ref, a_ref, b_ref, m_ref, v_ref, g_ref, d_ref, m2_ref, v2_ref):
        g = jnp.where(pl.program_id(0) == core_ref[0], a_ref[...], b_ref[...])
        g_ref[...] = g
        d_ref[...], m2_ref[...], v2_ref[...] = _adamw(w_ref[...], g, m_ref[...], v_ref[...])

    full = pl.BlockSpec((tr, cc), lambda h, i, core_ref: (h * nb + i, 0))
    mine = pl.BlockSpec((tr, cc), lambda h, i, core_ref: (jnp.where(h == core_ref[0], i, 0), 0))
    theirs = pl.BlockSpec((tr, cc), lambda h, i, core_ref: (jnp.where(h == core_ref[0], 0, i), 0))
    return pl.pallas_call(
        body, name=name, out_shape=[jax.ShapeDtypeStruct((2 * r, cc), F32)] * 4,
        grid_spec=pltpu.PrefetchScalarGridSpec(
            num_scalar_prefetch=1, grid=(2, nb), in_specs=[full, mine, theirs, full, full], out_specs=[full] * 4),
        compiler_params=_cp("arbitrary", "arbitrary"),
    )(core, w, own, other, m, v)


def _place():
    mx, my, mc = lax.axis_index("x"), lax.axis_index("y"), lax.axis_index("c")
    chips = [(1 - mx, my), (mx, 1 - my), (1 - mx, 1 - my)]
    return mx, my, mc, chips


def all_gather_small(x, name):
    R, C = x.shape

    def body(x_ref, out_ref, send_sems, recv_sems, local_sem):
        mx, my, mc, _ = _place()
        me = 4 * mx + 2 * my + mc
        mine = pltpu.make_async_copy(x_ref, out_ref.at[me], local_sem)
        mine.start()

        def peer(k):
            px = 1 - mx if k & 4 else mx
            py = 1 - my if k & 2 else my
            pc = 1 - mc if k & 1 else mc
            return px, py, pc

        def copy(k, src, slot):
            return pltpu.make_async_remote_copy(src_ref=src, dst_ref=out_ref.at[slot], send_sem=send_sems.at[k - 1],
                                                recv_sem=recv_sems.at[k - 1], device_id=peer(k), device_id_type=MESH)

        sends = [copy(k, x_ref, me) for k in range(1, 8)]
        for cp in sends:
            cp.start()
        for k in range(1, 8):
            px, py, pc = peer(k)
            slot = 4 * px + 2 * py + pc
            copy(k, out_ref.at[slot], slot).wait_recv()
        for cp in sends:
            cp.wait_send()
        mine.wait()

    return pl.pallas_call(
        body, name=name, out_shape=jax.ShapeDtypeStruct((8, R, C), F32),
        in_specs=[pl.BlockSpec(memory_space=pltpu.VMEM)], out_specs=pl.BlockSpec(memory_space=pltpu.VMEM),
        scratch_shapes=[pltpu.SemaphoreType.DMA((7,)), pltpu.SemaphoreType.DMA((7,)), pltpu.SemaphoreType.DMA],
        compiler_params=_cp(),
    )(x)


def gather8_comm(x):
    def copies(x_ref, out_ref, send_sems, recv_sems):
        mx, my, mc, _ = _place()
        me = 4 * mx + 2 * my + mc

        def peer(k):
            return (1 - mx if k & 4 else mx, 1 - my if k & 2 else my, 1 - mc if k & 1 else mc)

        def copy(k, src, slot):
            return pltpu.make_async_remote_copy(src_ref=src, dst_ref=out_ref.at[slot], send_sem=send_sems.at[k - 1],
                                                recv_sem=recv_sems.at[k - 1], device_id=peer(k), device_id_type=MESH)

        sends = [copy(k, x_ref, me) for k in range(1, 8)]
        arrivals = []
        for k in range(1, 8):
            px, py, pc = peer(k)
            slot = 4 * px + 2 * py + pc
            arrivals.append(copy(k, out_ref.at[slot], slot))
        return sends, arrivals, pltpu.make_async_copy(x_ref, out_ref.at[me], send_sems.at[7])

    def start(cin, cout, send_sems, recv_sems):
        sends, _, mine = copies(cin[0], cout[0], send_sems, recv_sems)
        mine.start()
        for cp in sends:
            cp.start()

    def finish(cin, cout, send_sems, recv_sems):
        sends, arrivals, mine = copies(cin[0], cout[0], send_sems, recv_sems)
        for cp in arrivals:
            cp.wait_recv()
        for cp in sends:
            cp.wait_send()
        mine.wait()

    return _Comm([x], [jax.ShapeDtypeStruct((8,) + x.shape, F32)], {}, 8, start, finish)


def _join(a, b):
    na_in, na_out = len(a.operands), len(a.out_shape)

    def split(fn_a, fn_b):
        def both(cin, cout, send_sems, recv_sems):
            fn_a(cin[:na_in], cout[:na_out], send_sems.at[pl.ds(0, a.n_sems)], recv_sems.at[pl.ds(0, a.n_sems)])
            fn_b(cin[na_in:], cout[na_out:], send_sems.at[pl.ds(a.n_sems, b.n_sems)], recv_sems.at[pl.ds(a.n_sems, b.n_sems)])
        return both

    aliases = dict(a.aliases)
    aliases.update({na_in + i: na_out + o for i, o in b.aliases.items()})
    return _Comm(a.operands + b.operands, a.out_shape + b.out_shape, aliases, a.n_sems + b.n_sems, split(a.start, b.start), split(a.finish, b.finish))


def _region(ref, kind, j, half, r, cc):
    nr = r if half is None else r // 2
    off = 0 if half is None else half * nr
    if kind == "col":
        return ref.at[pl.ds(off, nr), pl.ds(pl.multiple_of(j * cc, 128), cc)]
    return ref.at[pl.ds(pl.multiple_of(j * r + off, 16), nr), :]


def comm_call(comm, name):
    ni, no = len(comm.operands), len(comm.out_shape)

    def body(*refs):
        comm.start(refs[:ni], refs[ni:ni + no], *refs[ni + no:])
        comm.finish(refs[:ni], refs[ni:ni + no], *refs[ni + no:])

    return pl.pallas_call(
        body, name=name, out_shape=comm.out_shape, in_specs=[ANY] * ni, out_specs=[ANY] * no, input_output_aliases=comm.aliases,
        scratch_shapes=[pltpu.SemaphoreType.DMA((comm.n_sems,)), pltpu.SemaphoreType.DMA((comm.n_sems,))], compiler_params=_cp(),
    )(*comm.operands)


def gather_comm(fulls, kinds, dims):
    n = len(fulls)

    def copies(f_refs, send_sems, recv_sems):
        mx, my, mc, chips = _place()
        jme = 2 * mx + my

        def landed(w, k, half):
            px, py = chips[k]
            return _region(f_refs[w], kinds[w], 2 * px + py, half, *dims[w])

        def over_ici(w, k, reg):
            px, py = chips[k]
            return pltpu.make_async_remote_copy(src_ref=reg, dst_ref=reg, send_sem=send_sems.at[6 * w + k], recv_sem=recv_sems.at[6 * w + k],
                                                device_id=(px, py, mc), device_id_type=MESH)

        def over_d2d(w, k, half):
            reg = landed(w, k, half)
            return pltpu.make_async_remote_copy(src_ref=reg, dst_ref=reg, send_sem=send_sems.at[6 * w + 3 + k],
                                                recv_sem=recv_sems.at[6 * w + 3 + k], device_id=(mx, my, 1 - mc), device_id_type=MESH)

        sends = [over_ici(w, k, _region(f_refs[w], kinds[w], jme, mc, *dims[w])) for w in range(n) for k in range(3)]
        return mc, landed, over_ici, over_d2d, sends

    def start(cin, f_refs, send_sems, recv_sems):
        for cp in copies(f_refs, send_sems, recv_sems)[4]:
            cp.start()

    def finish(cin, f_refs, send_sems, recv_sems):
        mc, landed, over_ici, over_d2d, sends = copies(f_refs, send_sems, recv_sems)
        passed = []
        for w in range(n):
            for k in range(3):
                over_ici(w, k, landed(w, k, mc)).wait_recv()
                cp = over_d2d(w, k, mc)
                cp.start()
                passed.append(cp)
        for w in range(n):
            for k in range(3):
                over_d2d(w, k, 1 - mc).wait_recv()
        for cp in sends + passed:
            cp.wait_send()

    return _Comm(fulls, [jax.ShapeDtypeStruct(f.shape, BF16) for f in fulls], {w: w for w in range(n)}, 6 * n, start, finish)


def exchange_comm(grads):
    n = len(grads)

    def copies(g_refs, l_refs, send_sems, recv_sems):
        mx, my, mc, _ = _place()
        return [pltpu.make_async_remote_copy(src_ref=g_refs[w].at[:, pl.ds(1 - mc, 1)], dst_ref=l_refs[w], send_sem=send_sems.at[w],
                                             recv_sem=recv_sems.at[w], device_id=(mx, my, 1 - mc), device_id_type=MESH) for w in range(n)]

    def start(*refs):
        for cp in copies(*refs):
            cp.start()

    def finish(*refs):
        for cp in copies(*refs):
            cp.wait()

    return _Comm(grads, [jax.ShapeDtypeStruct((g.shape[0], 1) + g.shape[2:], BF16) for g in grads], {}, n, start, finish)


def exchange_halves(grads, name):
    return comm_call(exchange_comm(grads), name)


def scatter_comm(parts):
    n = len(parts)

    def sends(p_refs, l_refs, send_sems, recv_sems):
        mx, my, mc, chips = _place()
        return [pltpu.make_async_remote_copy(src_ref=p_refs[w].at[2 * px + py], dst_ref=l_refs[w].at[2 * mx + my],
                                             send_sem=send_sems.at[3 * w + k], recv_sem=recv_sems.at[3 * w + k],
                                             device_id=(px, py, mc), device_id_type=MESH) for w in range(n) for k, (px, py) in enumerate(chips)]

    def start(p_refs, l_refs, send_sems, recv_sems):
        for cp in sends(p_refs, l_refs, send_sems, recv_sems):
            cp.start()

    def finish(p_refs, l_refs, send_sems, recv_sems):
        mx, my, mc, chips = _place()
        for w in range(n):
            for k, (px, py) in enumerate(chips):
                slot = l_refs[w].at[2 * px + py]
                pltpu.make_async_remote_copy(src_ref=slot, dst_ref=slot, send_sem=send_sems.at[3 * w + k], recv_sem=recv_sems.at[3 * w + k],
                                             device_id=(px, py, mc), device_id_type=MESH).wait_recv()
        for cp in sends(p_refs, l_refs, send_sems, recv_sems):
            cp.wait_send()

    return _Comm(parts, [jax.ShapeDtypeStruct(p.shape, BF16) for p in parts], {}, 3 * n, start, finish)


def share_comm(sums):
    n = len(sums)

    def copies(q_refs, o_refs, send_sems, recv_sems):
        mx, my, mc, _ = _place()
        return [pltpu.make_async_remote_copy(src_ref=q_refs[w], dst_ref=o_refs[w], send_sem=send_sems.at[w], recv_sem=recv_sems.at[w],
                                             device_id=(mx, my, 1 - mc), device_id_type=MESH) for w in range(n)]

    def start(*refs):
        for cp in copies(*refs):
            cp.start()

    def finish(*refs):
        for cp in copies(*refs):
            cp.wait()

    return _Comm(sums, [jax.ShapeDtypeStruct(q.shape, F32) for q in sums], {}, n, start, finish)


def _pack(arrays):
    flat = jnp.concatenate([a.reshape(-1) for a in arrays])
    rows = -(-flat.shape[0] // 1024) * 8
    return jnp.pad(flat, (0, rows * 128 - flat.shape[0])).reshape(rows, 128)


def _unpack(packed, shapes):
    flat, out, off = packed.reshape(-1), [], 0
    for s in shapes:
        n = math.prod(s)
        out.append(flat[off:off + n].reshape(s))
        off += n
    return out


def kernel(x, c, w_ada, b_ada, g_pre_mix, g_post_mix, g_pre_ffn, g_post_ffn, w_in, lb_logits, g_hgrn_norm, w_a_out, g_sgu_norm, w_spatial, b_spatial, w_b_out, w_o, w_ff1, w_ff2, loss_target, m_w_ada, m_b_ada, m_g_pre_mix, m_g_post_mix, m_g_pre_ffn, m_g_post_ffn, m_w_in, m_lb_logits, m_g_hgrn_norm, m_w_a_out, m_g_sgu_norm, m_w_spatial, m_b_spatial, m_w_b_out, m_w_o, m_w_ff1, m_w_ff2, v_w_ada, v_b_ada, v_g_pre_mix, v_g_post_mix, v_g_pre_ffn, v_g_post_ffn, v_w_in, v_lb_logits, v_g_hgrn_norm, v_w_a_out, v_g_sgu_norm, v_w_spatial, v_b_spatial, v_w_b_out, v_w_o, v_w_ff1, v_w_ff2):
    mx, my, mc = lax.axis_index("x"), lax.axis_index("y"), lax.axis_index("c")
    chip, me = 2 * mx + my, 4 * mx + 2 * my + mc
    D = D_MODEL
    h0, tgt = x[0], loss_target[0]
    n_ada = w_ada.shape[2]
    n_lb = lb_logits.shape[2]

    got = all_gather_small(_pack([c, lb_logits]), "gather_inputs")
    c_all = got[:, :D // 128, :].reshape(8, D)
    lb_full = got[0::2, D // 128:D // 128 + 4 * n_lb // 128, :].reshape(4, 2, 2, n_lb).transpose(1, 2, 0, 3).reshape(2, 2, 4 * n_lb)
    b_ada_chip = lax.dynamic_slice(b_ada, (0, chip * n_ada), (1, n_ada))
    mod_cols = mod_matmul(c_all, w_ada[0], b_ada_chip)
    got = all_gather_small(mod_cols.reshape(-1, 128), "gather_mod").reshape(4, 2, 8, n_ada)
    mod = lax.dynamic_index_in_dim(got[:, 0], me, axis=1, keepdims=False).reshape(6, 1, D)
    sh1, sc1, gt1, sh2, sc2, gt2 = (mod[i] for i in range(6))

    big = [("w_in", w_in, "col"), ("w_a_out", w_a_out, "col"), ("w_b_out", w_b_out, "col"), ("w_o", w_o, "row"),
           ("w_ff1", w_ff1, "col"), ("w_ff2", w_ff2, "row")]
    kinds = [k for _, _, k in big]
    chip_idx, core = chip.reshape(1).astype(jnp.int32), mc.reshape(1).astype(jnp.int32)
    fulls = [cast_into_full(w[0], kind, chip_idx, "cast_" + nm) for nm, w, kind in big]
    dims = [w.shape[1:] for _, w, _ in big]
    later = lambda lo, hi: gather_comm(fulls[lo:hi], kinds[lo:hi], dims[lo:hi])
    halves_summed = lambda grads, name: [add_halves(g, l, core) for g, l in zip(grads, exchange_halves(grads, name))]

    bst = b_spatial[0].T
    a1 = prenorm(h0, g_pre_mix, sc1, sh1)
    proj, w_in_f, (w_a_f, w_b_f, w_o_f) = in_proj_gathered(a1, fulls[0], chip_idx, dims[0], later(1, 4))
    o, (w_ff1_f,) = hgrn_fwd(proj, lb_full, comm=later(4, 5))
    ya_pre = hgrn_post_fwd(o, proj, g_hgrn_norm)
    sgu = sgu_fwd(proj, g_sgu_norm, w_spatial[0], bst)
    y_a, y_b, merged = merge_matmul(ya_pre, sgu, w_a_f, w_b_f, proj)
    mo, h1, a2 = out_proj(merged, w_o_f, h0, gt1, g_post_mix, g_pre_ffn, sc2, sh2)
    (f1, hid), (w_ff2_f,) = matmul(a2, w_ff1_f, mode="nn", out_dtype=BF16, tm=1024, tn=2048, tk=2048, name="ff1", relu2=True,
                                   comm=later(5, 6))
    ff = matmul(hid, w_ff2_f, mode="nn", out_dtype=F32, tm=1024, tn=1024, tk=2048, name="ff2")
    dy, dff, loss_parts, d_gt2, d_g_post_ffn = loss_bwd(ff, h1, tgt, gt2, g_post_ffn)

    df1 = ff2_bwd(dff, w_ff2_f, f1)
    gr_ff2 = matmul(hid, dff, mode="tn", out_dtype=BF16, tm=1024, tn=1024, tk=2048, name="dw_ff2")
    gr_ff2 = gr_ff2.reshape(4, 2, -1, D)
    da2, (landed_ff2,) = matmul(df1, w_ff1_f, mode="nt", out_dtype=F32, tm=1024, tn=1024, tk=2048, name="da2", comm=exchange_comm([gr_ff2]))
    gr_ff1 = matmul(a2, df1, mode="tn", out_dtype=BF16, tm=1024, tn=2048, tk=2048, name="dw_ff1", split=(4, 2))
    (dh1, dmo, d_sh2, d_sc2, d_g_pre_ffn, d_gt1, d_g_post_mix), (landed_ff1,) = ffn_norm_bwd(
        dy, da2, h1, mo, g_pre_ffn, sc2, gt1, g_post_mix, exchange_comm([gr_ff1]))
    parts_ff = [add_halves(gr_ff1, landed_ff1, core), add_halves(gr_ff2, landed_ff2, core)]
    dya, dyb, dga, dgb = out_proj_bwd(dmo, w_o_f, y_a, y_b, proj)
    gr_o = matmul(merged, dmo, mode="tn", out_dtype=BF16, tm=1024, tn=1024, tk=2048, name="dw_o")
    dsgu = matmul(dyb, w_b_f, mode="nt", out_dtype=F32, tm=512, tn=1024, tk=2048, name="dsgu")
    gr_b = matmul(sgu, dyb, mode="tn", out_dtype=BF16, tm=512, tn=512, tk=4096, name="dw_b_out", split=(4, 2))
    dz, d_w_spatial, d_b_spatial, d_g_sgu = sgu_bwd(proj, dsgu, g_sgu_norm, w_spatial[0], bst)
    dya_pre = matmul(dya, w_a_f, mode="nt", out_dtype=F32, tm=512, tn=1024, tk=2048, name="dya_pre")
    gr_a = matmul(ya_pre, dya, mode="tn", out_dtype=BF16, tm=512, tn=512, tk=4096, name="dw_a_out", split=(4, 2))
    gr_mix = [gr_a, gr_b, gr_o.reshape(4, 2, -1, D)]
    (do, dog, d_g_hgrn), landed_halves = hgrn_post_bwd(dya_pre, o, proj, g_hgrn_norm, exchange_comm(gr_mix))
    parts_mix = [add_halves(g, l, core) for g, l in zip(gr_mix, landed_halves)]
    chips_summed = lambda parts, landed: [sum_chips(p, l, chip_idx) for p, l in zip(parts, landed)]
    (dq, dv, dlg, d_lb), landed_ff = hgrn_bwd(proj, do, lb_full, comm=scatter_comm(parts_ff))
    own_ff = chips_summed(parts_ff, landed_ff)
    dproj = jnp.concatenate([dq, dlg, dv, dog, dz, dga, dgb], axis=1)
    early = _pack([d_g_sgu, d_w_spatial, d_b_spatial[:, 0, :]])
    gr_in, (*landed_mix, got_early) = matmul(a1, dproj, mode="tn", out_dtype=BF16, tm=1024, tn=2816, tk=1024, name="dw_in", split=(4, 2),
                                             comm=_join(scatter_comm(parts_mix), gather8_comm(early)))
    own_mix = chips_summed(parts_mix, landed_mix)
    parts_in = halves_summed([gr_in], "exchange_in")
    da1, (landed_in, *other_rest) = matmul(dproj, w_in_f, mode="nt", out_dtype=F32, tm=1024, tn=1024, tk=2816, name="da1",
                                           comm=_join(scatter_comm(parts_in), share_comm(own_mix + own_ff)))
    own_in = chips_summed(parts_in, [landed_in])
    other_in = comm_call(share_comm(own_in), "share_w_in")
    own, other = own_in + own_mix + own_ff, list(other_in) + other_rest
    grad_x, d_sh1, d_sc1, d_g_pre_mix = mix_norm_bwd(da1, h0, dh1, g_pre_mix, sc1)
    out = {}

    mine = _pack([d_sh1, d_sc1, d_gt1, d_sh2, d_sc2, d_gt2, d_g_pre_mix, d_g_post_mix, d_g_pre_ffn, d_g_post_ffn, d_g_hgrn, d_lb,
                  loss_parts[0:1, 0:1]])
    got = all_gather_small(mine, "gather_small_grads")
    g_b_ada, g_g1, g_g2, g_g3, g_g4, g_hg, g_lb, sq_err = _unpack(
        sum_devices(got, "sum_small_grads"), [(1, 6 * D), (1, D), (1, D), (1, D), (1, D), (1, HEAD_DIM), (2, 1024), ()])
    loss = 0.5 * sq_err / D
    g_sg, g_ws, g_bs = _unpack(sum_devices(got_early, "sum_sgu_grads"), [(1, 1024), w_spatial.shape, b_spatial.shape])
    g_lbl = lax.dynamic_slice(lb_logits_grad(g_lb, lb_full), (0, 0, chip * n_lb), (2, 2, n_lb))
    names = ["b_ada", "g_pre_mix", "g_post_mix", "g_pre_ffn", "g_post_ffn", "g_hgrn_norm", "g_sgu_norm", "w_spatial", "b_spatial", "lb_logits"]
    ws = [b_ada, g_pre_mix, g_post_mix, g_pre_ffn, g_post_ffn, g_hgrn_norm, g_sgu_norm, w_spatial, b_spatial, lb_logits]
    gs = [g_b_ada, g_g1, g_g2, g_g3, g_g4, g_hg, g_sg, g_ws, g_bs, g_lbl]
    ms = [m_b_ada, m_g_pre_mix, m_g_post_mix, m_g_pre_ffn, m_g_post_ffn, m_g_hgrn_norm, m_g_sgu_norm, m_w_spatial, m_b_spatial, m_lb_logits]
    vs = [v_b_ada, v_g_pre_mix, v_g_post_mix, v_g_pre_ffn, v_g_post_ffn, v_g_hgrn_norm, v_g_sgu_norm, v_w_spatial, v_b_spatial, v_lb_logits]
    shapes = [w.shape for w in ws]
    upd = adamw(_pack(ws), _pack(gs), _pack(ms), _pack(vs), "adamw_small")
    upd = [_unpack(u, shapes) for u in upd]
    for i, nm in enumerate(names):
        out[nm] = (gs[i], upd[0][i], upd[1][i], upd[2][i])

    dmod_all = got[:, :6 * D // 128, :].reshape(8, 6 * D)
    dmod_chip = lax.dynamic_slice(dmod_all, (0, chip * n_ada), (8, n_ada))
    out["w_ada"] = tuple(a[None] for a in wada_update(c_all, dmod_chip, w_ada[0], m_w_ada[0], v_w_ada[0]))
    for (nm, w, _), a, b, m, v in zip(big, own, other, (m_w_in, m_w_a_out, m_w_b_out, m_w_o, m_w_ff1, m_w_ff2),
                                      (v_w_in, v_w_a_out, v_w_b_out, v_w_o, v_w_ff1, v_w_ff2)):
        out[nm] = tuple(t[None] for t in adamw_halves(w[0], a, b, m[0], v[0], core, "adamw_" + nm))

    order = ["w_ada", "b_ada", "g_pre_mix", "g_post_mix", "g_pre_ffn", "g_post_ffn", "w_in", "lb_logits", "g_hgrn_norm", "w_a_out",
             "g_sgu_norm", "w_spatial", "b_spatial", "w_b_out", "w_o", "w_ff1", "w_ff2"]
    return (loss, grad_x[None], *[out[nm][0] for nm in order], *[out[nm][1] for nm in order], *[out[nm][2] for nm in order],
            *[out[nm][3] for nm in order])
```

```python
import functools
import math

import jax
import jax.numpy as jnp
from jax import lax
from jax.experimental import pallas as pl
from jax.experimental.pallas import tpu as pltpu

F32, BF16 = jnp.float32, jnp.bfloat16
HI = lax.Precision.HIGHEST
MESH = pl.DeviceIdType.MESH
ANY = pl.BlockSpec(memory_space=pl.ANY)

EPS = 1e-6
D_MODEL = 2048
N_HEADS = 8
HEAD_DIM = 128
HGRN_CHUNK = 32
HGRN_BLOCK = 256
HGRN_BLOCK_FWD = 512
SGU_CHUNK = 128
SGU_GROUPS = 8
Q_SCALE = HEAD_DIM ** -0.5
COL_Q, COL_FFW, COL_FBW, COL_V, COL_OG, COL_U, COL_ZV, COL_GA, COL_GB = 0, 1, 2, 3, 4, 5, 6, 7, 9
N_PROJ = 11264
VMEM_BYTES_V7X = 64 * 1024 * 1024
VMEM_LIMIT = VMEM_BYTES_V7X - 8 * 1024 * 1024

ADAM_LR, ADAM_B1, ADAM_B2, ADAM_EPS, ADAM_WD, ADAM_STEP = 0.001, 0.9, 0.999, 1e-08, 0.01, 10
ADAM_C1 = 1.0 - ADAM_B1 ** ADAM_STEP
ADAM_C2 = 1.0 - ADAM_B2 ** ADAM_STEP


def _cp(*sem):
    return pltpu.CompilerParams(dimension_semantics=sem if sem else None, vmem_limit_bytes=VMEM_LIMIT)


def _vec(d):
    return pl.BlockSpec((1, d), lambda *_: (0, 0))


def _colsum(x):
    return jnp.sum(x, axis=0, keepdims=True)


def _nt(a, b):
    return lax.dot_general(a, b, (((1,), (1,)), ((), ())), preferred_element_type=F32)


def _tn(a, b):
    return lax.dot_general(a, b, (((0,), (0,)), ((), ())), preferred_element_type=F32)


def _nn(a, b):
    return jnp.dot(a, b, preferred_element_type=F32)


def _adamw(w, g, m, v):
    m2 = ADAM_B1 * m + (1.0 - ADAM_B1) * g
    v2 = ADAM_B2 * v + (1.0 - ADAM_B2) * (g * g)
    delta = -ADAM_LR * ((m2 / ADAM_C1) / (jnp.sqrt(v2 / ADAM_C2) + ADAM_EPS) + ADAM_WD * w)
    return delta, m2, v2


class _Comm:
    def __init__(self, operands, out_shape, aliases, n_sems, start, finish):
        self.operands, self.out_shape, self.aliases, self.n_sems = list(operands), list(out_shape), dict(aliases), n_sems
        self.start, self.finish = start, finish


def _pallas(body, *, name, grid, in_specs, out_specs, out_shape, scratch, semantics, operands, comm=None):
    if comm is None:
        res = pl.pallas_call(body, name=name, grid=grid, in_specs=in_specs, out_specs=out_specs, out_shape=out_shape,
                             scratch_shapes=scratch, compiler_params=_cp(*semantics))(*operands)
        return res, []
    n_in, n_out, n_scr = len(in_specs), len(out_specs), len(scratch)
    nci, nco = len(comm.operands), len(comm.out_shape)

    def with_comm(*refs):
        ins, rest = refs[:n_in], refs[n_in:]
        cin, rest = rest[:nci], rest[nci:]
        outs, rest = rest[:n_out], rest[n_out:]
        cout, rest = rest[:nco], rest[nco:]
        scr, (send, recv) = rest[:n_scr], rest[n_scr:]
        ids = [pl.program_id(a) for a in range(len(grid))]
        first = functools.reduce(jnp.logical_and, [i == 0 for i in ids])
        last = functools.reduce(jnp.logical_and, [i == g - 1 for i, g in zip(ids, grid)])

        @pl.when(first)
        def _():
            comm.start(cin, cout, send, recv)

        body(*ins, *outs, *scr)

        @pl.when(last)
        def _():
            comm.finish(cin, cout, send, recv)

    res = pl.pallas_call(
        with_comm, name=name, grid=grid, in_specs=list(in_specs) + [ANY] * nci, out_specs=list(out_specs) + [ANY] * nco,
        out_shape=list(out_shape) + comm.out_shape, input_output_aliases={n_in + i: n_out + o for i, o in comm.aliases.items()},
        scratch_shapes=list(scratch) + [pltpu.SemaphoreType.DMA((comm.n_sems,)), pltpu.SemaphoreType.DMA((comm.n_sems,))],
        compiler_params=_cp(*["arbitrary"] * len(grid)),
    )(*operands, *comm.operands)
    return res[:n_out], res[n_out:]


def matmul(a, b, *, mode, out_dtype, tm, tn, tk, name, split=None, comm=None, relu2=False):
    if mode == "tn":
        (K, M), (_, N) = a.shape, b.shape
    elif mode == "nt":
        (M, K), (N, _) = a.shape, b.shape
    else:
        (M, K), (_, N) = a.shape, b.shape
    tm, tn, tk = min(tm, M), min(tn, N), min(tk, K)
    nk = K // tk
    a_spec = pl.BlockSpec((tk, tm), lambda i, j, k: (k, i)) if mode == "tn" else pl.BlockSpec((tm, tk), lambda i, j, k: (i, k))
    b_spec = pl.BlockSpec((tn, tk), lambda i, j, k: (j, k)) if mode == "nt" else pl.BlockSpec((tk, tn), lambda i, j, k: (k, j))
    dot = {"nn": _nn, "nt": _nt, "tn": _tn}[mode]
    if split is None:
        out_shape = jax.ShapeDtypeStruct((M, N), out_dtype)
        out_spec = pl.BlockSpec((tm, tn), lambda i, j, k: (i, j))
    else:
        nj, nh = split
        rows, cols = M // nh, N // nj
        tm, tn = min(tm, rows), min(tn, cols)
        bi, bj = rows // tm, cols // tn
        out_shape = jax.ShapeDtypeStruct((nj, nh, rows, cols), out_dtype)
        out_spec = pl.BlockSpec((None, None, tm, tn), lambda i, j, k: (j // bj, i // bi, i % bi, j % bj))

    def finish(y, o_ref, sq_ref):
        o_ref[...] = y.astype(o_ref.dtype)
        if relu2:
            p = jnp.maximum(y, 0.0)
            sq_ref[0][...] = (p * p).astype(BF16)

    if nk == 1:
        def body(a_ref, b_ref, o_ref, *sq_ref):
            finish(dot(a_ref[...], b_ref[...]), o_ref, sq_ref)
        scratch = []
    else:
        def body(a_ref, b_ref, o_ref, *rest):
            acc_ref, k = rest[-1], pl.program_id(2)

            @pl.when(k == 0)
            def _():
                acc_ref[...] = jnp.zeros_like(acc_ref)

            acc_ref[...] += dot(a_ref[...], b_ref[...])

            @pl.when(k == nk - 1)
            def _():
                finish(acc_ref[...], o_ref, rest[:-1])
        scratch = [pltpu.VMEM((tm, tn), F32)]

    out_specs, out_shapes = [out_spec], [out_shape]
    if relu2:
        out_specs, out_shapes = out_specs + [out_spec], out_shapes + [jax.ShapeDtypeStruct(out_shape.shape, BF16)]
    outs, landed = _pallas(
        body, name=name, grid=(M // tm, N // tn, nk), in_specs=[a_spec, b_spec], out_specs=out_specs, out_shape=out_shapes,
        scratch=scratch, semantics=("parallel", "parallel", "arbitrary"), operands=(a, b), comm=comm)
    out = tuple(outs) if relu2 else outs[0]
    return out if comm is None else (out, landed)


def cast_into_full(w, kind, chip, name):
    r, cc = w.shape
    tr = min(r, 512)
    nb = r // tr

    def body(chip_ref, w_ref, o_ref):
        o_ref[...] = w_ref[...].astype(BF16)

    if kind == "col":
        full, out_map = (r, 4 * cc), lambda i, chip_ref: (i, chip_ref[0])
    else:
        full, out_map = (4 * r, cc), lambda i, chip_ref: (chip_ref[0] * nb + i, 0)
    return pl.pallas_call(
        body, name=name, out_shape=jax.ShapeDtypeStruct(full, BF16),
        grid_spec=pltpu.PrefetchScalarGridSpec(
            num_scalar_prefetch=1, grid=(nb,), in_specs=[pl.BlockSpec((tr, cc), lambda i, chip_ref: (i, 0))],
            out_specs=pl.BlockSpec((tr, cc), out_map)),
        compiler_params=_cp("parallel"),
    )(chip, w)


def mod_matmul(c_all, w_ada, b_ada):
    D, N = w_ada.shape
    tn = 1024

    def body(c_ref, w_ref, b_ref, o_ref):
        c = c_ref[...]
        sc = c * jax.nn.sigmoid(c)
        o_ref[...] = jnp.dot(sc, w_ref[...], precision=HI, preferred_element_type=F32) + b_ref[...]

    return pl.pallas_call(
        body, name="mod_matmul", out_shape=jax.ShapeDtypeStruct((8, N), F32), grid=(N // tn,),
        in_specs=[pl.BlockSpec((8, D), lambda j: (0, 0)), pl.BlockSpec((D, tn), lambda j: (0, j)),
                  pl.BlockSpec((1, tn), lambda j: (0, j))],
        out_specs=pl.BlockSpec((8, tn), lambda j: (0, j)), compiler_params=_cp("parallel"),
    )(c_all, w_ada, b_ada)


def prenorm(h, g, sc, sh):
    T, D = h.shape
    tm = min(256, T)

    def body(h_ref, g_ref, sc_ref, sh_ref, a_ref):
        x = h_ref[...]
        r = lax.rsqrt(jnp.mean(x * x, axis=-1, keepdims=True) + EPS)
        a_ref[...] = ((x * r) * g_ref[...] * (1.0 + sc_ref[...]) + sh_ref[...]).astype(BF16)

    row = pl.BlockSpec((tm, D), lambda i: (i, 0))
    return pl.pallas_call(
        body, name="prenorm", out_shape=jax.ShapeDtypeStruct((T, D), BF16), grid=(T // tm,),
        in_specs=[row, _vec(D), _vec(D), _vec(D)], out_specs=row, compiler_params=_cp("parallel"),
    )(h, g, sc, sh)


def in_proj_gathered(a, w_full, chip, dims, tail):
    T, D = a.shape
    rows, cc = dims
    tm, tn = min(512, T), cc // 2
    ni = T // tm
    half = rows // 2

    nt = len(tail.operands)

    def body(chip_ref, a_ref, w_in_ref, *rest):
        tail_in, (y_ref, w_ref), rest = rest[:nt], rest[nt:nt + 2], rest[nt + 2:]
        tail_out, (wbuf, wsem, send_sems, recv_sems, tail_send, tail_recv) = rest[:nt], rest[nt:]
        q, j, i = pl.program_id(0), pl.program_id(1), pl.program_id(2)
        mx, my, mc, _ = _place()
        me = chip_ref[0]

        def tile(block, jj):
            src = w_ref.at[:, pl.ds(pl.multiple_of(block * cc + jj * tn, 128), tn)]
            return pltpu.make_async_copy(src, wbuf.at[jj], wsem.at[jj])

        def rows_half(block, hh):
            return w_ref.at[pl.ds(pl.multiple_of(hh * half, 16), half), pl.ds(pl.multiple_of(block * cc, 128), cc)]

        def over_ici(s, block):
            peer = (1 - mx if s & 2 else mx, 1 - my if s & 1 else my, mc)
            reg = rows_half(block, mc)
            return pltpu.make_async_remote_copy(src_ref=reg, dst_ref=reg, send_sem=send_sems.at[s - 1], recv_sem=recv_sems.at[s - 1],
                                                device_id=peer, device_id_type=MESH)

        def over_d2d(s, block, hh):
            reg = rows_half(block, hh)
            return pltpu.make_async_remote_copy(src_ref=reg, dst_ref=reg, send_sem=send_sems.at[2 + s], recv_sem=recv_sems.at[2 + s],
                                                device_id=(mx, my, 1 - mc), device_id_type=MESH)

        def passed_on(s, block):
            k = s - 1
            reg = w_ref.at[pl.ds(pl.multiple_of(mc * half + k * (half // 2), 16), half // 2), pl.ds(pl.multiple_of(block * cc, 128), cc)]
            peer = (1 - mx, my, mc) if s == 1 else (mx, 1 - my, mc)
            return pltpu.make_async_remote_copy(src_ref=reg, dst_ref=reg, send_sem=send_sems.at[6 + k], recv_sem=recv_sems.at[6 + k],
                                                device_id=peer, device_id_type=MESH)

        @pl.when((q == 0) & (j == 0) & (i == 0))
        def _():
            for s in (1, 2):
                over_ici(s, me).start()
            tile(me, 0).start()

        @pl.when(i == 0)
        def _():
            tile(me ^ q, j).wait()

        @pl.when((i == 0) & (j == 0))
        def _():
            tile(me ^ q, 1).start()

        y_ref[...] = _nn(a_ref[...], wbuf[j])

        @pl.when((q == 0) & (j == 1) & (i == ni - 1))
        def _():
            for s in (1, 2):
                over_ici(s, me ^ s).wait_recv()
                passed_on(s, me ^ s).start()
                over_d2d(s, me ^ s, mc).start()
            tail.start(tail_in, tail_out, tail_send, tail_recv)
            over_d2d(1, me ^ 1, 1 - mc).wait_recv()
            tile(me ^ 1, 0).start()

        @pl.when((q == 1) & (j == 1) & (i == ni - 1))
        def _():
            over_d2d(2, me ^ 2, 1 - mc).wait_recv()
            tile(me ^ 2, 0).start()

        @pl.when((q == 2) & (j == 1) & (i == ni - 1))
        def _():
            for s in (1, 2):
                passed_on(s, me ^ 3).wait_recv()
            over_d2d(3, me ^ 3, mc).start()
            over_d2d(3, me ^ 3, 1 - mc).wait_recv()
            tile(me ^ 3, 0).start()

        @pl.when((q == 3) & (j == 1) & (i == ni - 1))
        def _():
            for s in (1, 2):
                over_ici(s, me).wait_send()
                passed_on(s, me ^ s).wait_send()
            for s in (1, 2, 3):
                over_d2d(s, me ^ s, mc).wait_send()
            tail.finish(tail_in, tail_out, tail_send, tail_recv)

    dma = pltpu.SemaphoreType.DMA
    y, w_out, *tail_res = pl.pallas_call(
        body, name="in_proj", out_shape=[jax.ShapeDtypeStruct((T, 4 * cc), F32), jax.ShapeDtypeStruct(w_full.shape, BF16)] + tail.out_shape,
        grid_spec=pltpu.PrefetchScalarGridSpec(
            num_scalar_prefetch=1, grid=(4, 2, ni),
            in_specs=[pl.BlockSpec((tm, D), lambda q, j, i, chip_ref: (i, 0)), ANY] + [ANY] * nt,
            out_specs=[pl.BlockSpec((tm, tn), lambda q, j, i, chip_ref: (i, (chip_ref[0] ^ q) * 2 + j)), ANY] + [ANY] * nt,
            scratch_shapes=[pltpu.VMEM((2, D, tn), BF16), dma((2,)), dma((8,)), dma((8,)), dma((tail.n_sems,)), dma((tail.n_sems,))]),
        input_output_aliases={2: 1, **{3 + i: 2 + o for i, o in tail.aliases.items()}},
        compiler_params=_cp("arbitrary", "arbitrary", "arbitrary"),
    )(chip, a, w_full, *tail.operands)
    return y, w_out, tail_res


def _hgrn_lower_bound(l_ref):
    l0, l1 = l_ref[0:1, :], l_ref[1:2, :]
    m = jnp.maximum(l0, l1)
    e0, e1 = jnp.exp(l0 - m), jnp.exp(l1 - m)
    return e0 / (e0 + e1)


def _hgrn_chunk_mask(d, blk):
    r = lax.broadcasted_iota(jnp.int32, (blk, blk), 0)
    c = lax.broadcasted_iota(jnp.int32, (blk, blk), 1)
    same = (r // HGRN_CHUNK) == (c // HGRN_CHUNK)
    fwd = d == 0
    return same & (((c <= r) & fwd) | ((c >= r) & jnp.logical_not(fwd)))


def _chunk_total(x):
    x3 = x.reshape(x.shape[0] // HGRN_CHUNK, HGRN_CHUNK, x.shape[1])
    return jnp.broadcast_to(jnp.sum(x3, axis=1, keepdims=True), x3.shape).reshape(x.shape)


def _chunk_cumsum(x, suffix):
    pos = lax.broadcasted_iota(jnp.int32, x.shape, 0) % HGRN_CHUNK
    p, s = x, 1
    while s < HGRN_CHUNK:
        p = p + jnp.where(pos >= s, pltpu.roll(p, s, 0), 0.0)
        s *= 2
    return jnp.where(suffix, _chunk_total(x) - p + x, p)


def _block_loop(T, blk, body, init):
    n = T // blk
    return lax.fori_loop(0, n, body, init, unroll=2 if n % 2 == 0 else 1)


def _hgrn_gate(f, lb):
    s = jax.nn.sigmoid(f)
    sn = jax.nn.sigmoid(-f)
    fg = lb + (1.0 - lb) * s
    return s, sn, fg, jnp.log(fg), (1.0 - lb) * sn


def _hgrn_specs(T):
    col = lambda base: pl.BlockSpec((T, HEAD_DIM), lambda h, d: (0, base * N_HEADS + h))
    f_spec = pl.BlockSpec((T, HEAD_DIM), lambda h, d: (0, COL_FFW * N_HEADS + N_HEADS * d + h))
    l_spec = pl.BlockSpec((None, 2, HEAD_DIM), lambda h, d: (d, 0, h))
    return col, f_spec, l_spec


def hgrn_fwd(proj, lb_logits, comm=None):
    T = proj.shape[0]
    blk = min(HGRN_BLOCK_FWD, T)
    NC, CPB = T // HGRN_CHUNK, blk // HGRN_CHUNK
    col, f_spec, l_spec = _hgrn_specs(T)

    def body(l_ref, q_ref, f_ref, v_ref, o_ref, st_ref, dec_ref, qd_ref):
        d = pl.program_id(1)
        lb = _hgrn_lower_bound(l_ref)
        mask = _hgrn_chunk_mask(d, blk)

        def block(i, carry):
            rows = pl.ds(pl.multiple_of(i * blk, blk), blk)
            _, _, _, lf, k = _hgrn_gate(f_ref[rows, :], lb)
            b = _chunk_cumsum(lf, d == 1)
            bl = _chunk_total(lf)
            qd = (q_ref[rows, :] * Q_SCALE * jnp.exp(b)).astype(BF16)
            kd = (k * jnp.exp(-b)).astype(BF16)
            ke = (k * jnp.exp(bl - b)).astype(BF16)
            vb = v_ref[rows, :].astype(BF16)
            att = jnp.where(mask, _nt(qd, kd), 0.0).astype(BF16)
            o_ref[rows, :] = jnp.where(d == 0, 0.0, o_ref[rows, :]) + _nn(att, vb)
            qd_ref[rows, :] = qd
            dec = jnp.exp(bl)
            for cc in range(CPB):
                sl = slice(cc * HGRN_CHUNK, (cc + 1) * HGRN_CHUNK)
                n = i * CPB + cc
                st_ref[n] = _tn(vb[sl], ke[sl])
                dec_ref[n] = dec[cc * HGRN_CHUNK:cc * HGRN_CHUNK + 8, :]
            return carry

        _block_loop(T, blk, block, 0)

        def scan(t, s):
            n = jnp.where(d == 0, t, NC - 1 - t)
            u = st_ref[n]
            st_ref[n] = s
            return dec_ref[n][0:1, :] * s + u

        lax.fori_loop(0, NC, scan, jnp.zeros((HEAD_DIM, HEAD_DIM), F32))

        def inter(i, carry):
            rows = pl.ds(pl.multiple_of(i * blk, blk), blk)
            qd = qd_ref[rows, :]
            o_ref[rows, :] += jnp.concatenate(
                [_nt(qd[cc * HGRN_CHUNK:(cc + 1) * HGRN_CHUNK], st_ref[i * CPB + cc].astype(BF16)) for cc in range(CPB)], axis=0)
            return carry

        _block_loop(T, blk, inter, 0)

    (o,), landed = _pallas(
        body, name="hgrn_fwd", grid=(N_HEADS, 2), in_specs=[l_spec, col(COL_Q), f_spec, col(COL_V)],
        out_specs=[pl.BlockSpec((T, HEAD_DIM), lambda h, d: (0, h))], out_shape=[jax.ShapeDtypeStruct((T, N_HEADS * HEAD_DIM), F32)],
        scratch=[pltpu.VMEM((NC, HEAD_DIM, HEAD_DIM), F32), pltpu.VMEM((NC, 8, HEAD_DIM), F32), pltpu.VMEM((T, HEAD_DIM), BF16)],
        semantics=("parallel", "arbitrary"), operands=(lb_logits, proj, proj, proj), comm=comm)
    return o if comm is None else (o, landed)


def hgrn_post_fwd(o, proj, g_norm):
    T, W = o.shape
    tm = min(256, T)

    def body(o_ref, og_ref, g_ref, y_ref):
        g = g_ref[...]
        for h in range(N_HEADS):
            sl = slice(h * HEAD_DIM, (h + 1) * HEAD_DIM)
            x = o_ref[:, sl]
            r = lax.rsqrt(jnp.mean(x * x, axis=-1, keepdims=True) + EPS)
            og = og_ref[:, sl]
            y_ref[:, sl] = ((x * r) * g * (og * jax.nn.sigmoid(og))).astype(BF16)

    return pl.pallas_call(
        body, name="hgrn_post_fwd", out_shape=jax.ShapeDtypeStruct((T, W), BF16), grid=(T // tm,),
        in_specs=[pl.BlockSpec((tm, W), lambda i: (i, 0)), pl.BlockSpec((tm, W), lambda i: (i, COL_OG)), _vec(HEAD_DIM)],
        out_specs=pl.BlockSpec((tm, W), lambda i: (i, 0)), compiler_params=_cp("parallel"),
    )(o, proj, g_norm)


def _gelu(x):
    return 0.5 * x * (1.0 + lax.erf(x * (1.0 / math.sqrt(2.0))))


def _gelu_grad(x):
    return 0.5 * (1.0 + lax.erf(x * (1.0 / math.sqrt(2.0)))) + x * jnp.exp(-0.5 * x * x) * (1.0 / math.sqrt(2.0 * math.pi))


def _sgu_mix(u_ref, v_ref, g_ref, ws_ref, bst_ref):
    W = u_ref.shape[1]
    zu, zv = _gelu(u_ref[...]), _gelu(v_ref[...])
    dv = zv - jnp.mean(zv, axis=-1, keepdims=True)
    rstd = lax.rsqrt(jnp.mean(dv * dv, axis=-1, keepdims=True) + EPS)
    dhat = dv * rstd
    vn = (dhat * g_ref[...]).astype(BF16)
    gw = W // SGU_GROUPS
    vm = [_nn(ws_ref[g].astype(BF16), vn[:, g * gw:(g + 1) * gw]) + bst_ref[:, g:g + 1] for g in range(SGU_GROUPS)]
    return zu, rstd, dhat, vn, jnp.concatenate(vm, axis=1)


def sgu_fwd(proj, g_norm, w_spatial, b_spatial_t):
    T = proj.shape[0]
    W = 1024
    n_chunks = T // SGU_CHUNK

    def body(u_ref, v_ref, g_ref, ws_ref, bst_ref, y_ref):
        zu, _, _, _, vm = _sgu_mix(u_ref, v_ref, g_ref, ws_ref, bst_ref)
        y_ref[...] = (zu * vm).astype(BF16)

    blk = lambda cb: pl.BlockSpec((SGU_CHUNK, W), lambda i: (i, cb))
    return pl.pallas_call(
        body, name="sgu_fwd", out_shape=jax.ShapeDtypeStruct((T, W), BF16), grid=(n_chunks,),
        in_specs=[blk(COL_U), blk(COL_ZV), _vec(W), pl.BlockSpec((SGU_GROUPS, SGU_CHUNK, SGU_CHUNK), lambda i: (0, 0, 0)),
                  pl.BlockSpec((SGU_CHUNK, SGU_GROUPS), lambda i: (0, 0))],
        out_specs=blk(0), compiler_params=_cp("parallel"),
    )(proj, proj, g_norm, w_spatial, b_spatial_t)


def merge_matmul(ya_pre, sgu, w_a, w_b, proj):
    T, K = ya_pre.shape
    N = w_a.shape[1]
    tm, tn = min(512, T), 512
    gpb = 1024 // tn

    def body(a_ref, b_ref, wa_ref, wb_ref, ga_ref, gb_ref, ya_ref, yb_ref, m_ref):
        ya = _nn(a_ref[...], wa_ref[...])
        yb = _nn(b_ref[...], wb_ref[...])
        ya_ref[...] = ya.astype(BF16)
        yb_ref[...] = yb.astype(BF16)
        m_ref[...] = (jax.nn.sigmoid(ga_ref[...]) * ya + jax.nn.sigmoid(gb_ref[...]) * yb).astype(BF16)

    lhs = pl.BlockSpec((tm, K), lambda i, j: (i, 0))
    rhs = pl.BlockSpec((K, tn), lambda i, j: (0, j))
    out = pl.BlockSpec((tm, tn), lambda i, j: (i, j))
    return pl.pallas_call(
        body, name="merge_matmul", grid=(T // tm, N // tn),
        out_shape=[jax.ShapeDtypeStruct((T, N), BF16)] * 3,
        in_specs=[lhs, lhs, rhs, rhs, pl.BlockSpec((tm, tn), lambda i, j: (i, COL_GA * gpb + j)),
                  pl.BlockSpec((tm, tn), lambda i, j: (i, COL_GB * gpb + j))],
        out_specs=[out, out, out], compiler_params=_cp("parallel", "parallel"),
    )(ya_pre, sgu, w_a, w_b, proj, proj)


def out_proj(merged, w_o, h0, gt1, g_post, g_pre2, sc2, sh2):
    T, D = h0.shape
    tm = min(256, T)

    def body(m_ref, w_ref, h_ref, gt_ref, gp_ref, g2_ref, sc_ref, sh_ref, mo_ref, h1_ref, a2_ref):
        mo = _nn(m_ref[...], w_ref[...])
        mo_ref[...] = mo
        r = lax.rsqrt(jnp.mean(mo * mo, axis=-1, keepdims=True) + EPS)
        h1 = h_ref[...] + gt_ref[...] * ((mo * r) * gp_ref[...])
        h1_ref[...] = h1
        r2 = lax.rsqrt(jnp.mean(h1 * h1, axis=-1, keepdims=True) + EPS)
        a2_ref[...] = ((h1 * r2) * g2_ref[...] * (1.0 + sc_ref[...]) + sh_ref[...]).astype(BF16)

    row = pl.BlockSpec((tm, D), lambda i: (i, 0))
    return pl.pallas_call(
        body, name="out_proj", grid=(T // tm,),
        out_shape=[jax.ShapeDtypeStruct((T, D), F32), jax.ShapeDtypeStruct((T, D), F32), jax.ShapeDtypeStruct((T, D), BF16)],
        in_specs=[row, pl.BlockSpec((D, D), lambda i: (0, 0)), row] + [_vec(D)] * 5,
        out_specs=[row, row, row], compiler_params=_cp("parallel"),
    )(merged, w_o, h0, gt1, g_post, g_pre2, sc2, sh2)


def loss_bwd(ff, h1, tgt, gt2, g_post):
    T, D = ff.shape
    tm = min(256, T)

    def body(f_ref, h_ref, t_ref, gt_ref, g_ref, dy_ref, dff_ref, loss_ref, dgt_ref, dg_ref):
        @pl.when(pl.program_id(0) == 0)
        def _():
            loss_ref[...] = jnp.zeros_like(loss_ref)
            dgt_ref[...] = jnp.zeros_like(dgt_ref)
            dg_ref[...] = jnp.zeros_like(dg_ref)

        ff = f_ref[...]
        gt, g = gt_ref[...], g_ref[...]
        r = lax.rsqrt(jnp.mean(ff * ff, axis=-1, keepdims=True) + EPS)
        fhat = ff * r
        nf = fhat * g
        err = (h_ref[...] + gt * nf) - t_ref[...]
        loss_ref[...] += jnp.sum(err * err)
        dy = err * (1.0 / D)
        dy_ref[...] = dy
        dgt_ref[...] += _colsum(dy * nf)
        dnf = dy * gt
        dg_ref[...] += _colsum(dnf * fhat)
        u = dnf * g
        dff_ref[...] = (r * (u - fhat * jnp.mean(u * fhat, axis=-1, keepdims=True))).astype(BF16)

    row = pl.BlockSpec((tm, D), lambda i: (i, 0))
    return pl.pallas_call(
        body, name="loss_bwd", grid=(T // tm,),
        out_shape=[jax.ShapeDtypeStruct((T, D), F32), jax.ShapeDtypeStruct((T, D), BF16), jax.ShapeDtypeStruct((8, 128), F32),
                   jax.ShapeDtypeStruct((1, D), F32), jax.ShapeDtypeStruct((1, D), F32)],
        in_specs=[row, row, row, _vec(D), _vec(D)],
        out_specs=[row, row, pl.BlockSpec((8, 128), lambda i: (0, 0)), _vec(D), _vec(D)],
        compiler_params=_cp("arbitrary"),
    )(ff, h1, tgt, gt2, g_post)


def ff2_bwd(dff, w_ff2, f1):
    T, D = dff.shape
    K = w_ff2.shape[0]
    tm, tn = min(1024, T), 2048

    def body(a_ref, w_ref, f_ref, o_ref):
        o_ref[...] = (_nt(a_ref[...], w_ref[...]) * (2.0 * jnp.maximum(f_ref[...].astype(F32), 0.0))).astype(BF16)

    return pl.pallas_call(
        body, name="ff2_bwd", out_shape=jax.ShapeDtypeStruct((T, K), BF16), grid=(K // tn, T // tm),
        in_specs=[pl.BlockSpec((tm, D), lambda j, i: (i, 0)), pl.BlockSpec((tn, D), lambda j, i: (j, 0)),
                  pl.BlockSpec((tm, tn), lambda j, i: (i, j))],
        out_specs=pl.BlockSpec((tm, tn), lambda j, i: (i, j)), compiler_params=_cp("parallel", "parallel"),
    )(dff, w_ff2, f1)


def ffn_norm_bwd(dy, da2, h1, mo, g_pre2, sc2, gt1, g_post, comm):
    T, D = dy.shape
    tm = min(256, T)

    def body(dy_ref, da_ref, h_ref, mo_ref, g2_ref, sc_ref, gt_ref, gp_ref, dh_ref, dmo_ref, s_sh, s_sc, s_g2, s_gt, s_gp):
        @pl.when(pl.program_id(0) == 0)
        def _():
            for s in (s_sh, s_sc, s_g2, s_gt, s_gp):
                s[...] = jnp.zeros_like(s)

        h1, da = h_ref[...], da_ref[...]
        g2, sc = g2_ref[...], sc_ref[...]
        r2 = lax.rsqrt(jnp.mean(h1 * h1, axis=-1, keepdims=True) + EPS)
        n2 = h1 * r2
        s_sh[...] += _colsum(da)
        s_sc[...] += _colsum(da * (n2 * g2))
        s_g2[...] += _colsum(da * (1.0 + sc) * n2)
        dn2 = da * g2 * (1.0 + sc)
        dh1 = dy_ref[...] + r2 * (dn2 - n2 * jnp.mean(dn2 * n2, axis=-1, keepdims=True))
        dh_ref[...] = dh1
        mo = mo_ref[...]
        gt, gp = gt_ref[...], gp_ref[...]
        r = lax.rsqrt(jnp.mean(mo * mo, axis=-1, keepdims=True) + EPS)
        mhat = mo * r
        s_gt[...] += _colsum(dh1 * (mhat * gp))
        dnm = dh1 * gt
        s_gp[...] += _colsum(dnm * mhat)
        u = dnm * gp
        dmo_ref[...] = (r * (u - mhat * jnp.mean(u * mhat, axis=-1, keepdims=True))).astype(BF16)

    row = pl.BlockSpec((tm, D), lambda i: (i, 0))
    vec_out = jax.ShapeDtypeStruct((1, D), F32)
    return _pallas(
        body, name="ffn_norm_bwd", grid=(T // tm,),
        out_shape=[jax.ShapeDtypeStruct((T, D), F32), jax.ShapeDtypeStruct((T, D), BF16)] + [vec_out] * 5,
        in_specs=[row, row, row, row] + [_vec(D)] * 4, out_specs=[row, row] + [_vec(D)] * 5,
        scratch=[], semantics=("arbitrary",), operands=(dy, da2, h1, mo, g_pre2, sc2, gt1, g_post), comm=comm)


def out_proj_bwd(dmo, w_o, y_a, y_b, proj, comm):
    T, D = dmo.shape
    tm, tn = min(512, T), 512
    gpb = 1024 // tn

    def body(a_ref, w_ref, ya_ref, yb_ref, ga_ref, gb_ref, dya_ref, dyb_ref, dga_ref, dgb_ref):
        dm = _nt(a_ref[...], w_ref[...])
        sa, sb = jax.nn.sigmoid(ga_ref[...]), jax.nn.sigmoid(gb_ref[...])
        dya_ref[...] = (dm * sa).astype(BF16)
        dyb_ref[...] = (dm * sb).astype(BF16)
        dga_ref[...] = (dm * ya_ref[...].astype(F32) * sa * (1.0 - sa)).astype(BF16)
        dgb_ref[...] = (dm * yb_ref[...].astype(F32) * sb * (1.0 - sb)).astype(BF16)

    out = pl.BlockSpec((tm, tn), lambda i, j: (i, j))
    return _pallas(
        body, name="out_proj_bwd", grid=(T // tm, D // tn), out_shape=[jax.ShapeDtypeStruct((T, D), BF16)] * 4,
        in_specs=[pl.BlockSpec((tm, D), lambda i, j: (i, 0)), pl.BlockSpec((tn, D), lambda i, j: (j, 0)), out, out,
                  pl.BlockSpec((tm, tn), lambda i, j: (i, COL_GA * gpb + j)), pl.BlockSpec((tm, tn), lambda i, j: (i, COL_GB * gpb + j))],
        out_specs=[out] * 4, scratch=[], semantics=("parallel", "parallel"), operands=(dmo, w_o, y_a, y_b, proj, proj), comm=comm)


def sgu_bwd(proj, dsgu, g_norm, w_spatial, b_spatial_t):
    T = proj.shape[0]
    W = 1024
    gw = W // SGU_GROUPS

    def body(u_ref, v_ref, ds_ref, g_ref, ws_ref, bst_ref, dz_ref, dw_ref, db_ref, dg_ref):
        @pl.when(pl.program_id(0) == 0)
        def _():
            dw_ref[...] = jnp.zeros_like(dw_ref)
            db_ref[...] = jnp.zeros_like(db_ref)
            dg_ref[...] = jnp.zeros_like(dg_ref)

        zu, rstd, dhat, vn, vm = _sgu_mix(u_ref, v_ref, g_ref, ws_ref, bst_ref)
        ds = ds_ref[...]
        du = ds * vm
        dvm = ds * zu
        dvm_b = dvm.astype(BF16)
        ones = jnp.ones((8, gw), F32)
        dvn = []
        for g in range(SGU_GROUPS):
            sl = slice(g * gw, (g + 1) * gw)
            dw_ref[g] += _nt(dvm_b[:, sl], vn[:, sl])
            db_ref[g] += lax.dot_general(ones, dvm[:, sl], (((1,), (1,)), ((), ())), precision=HI, preferred_element_type=F32)
            dvn.append(_tn(ws_ref[g].astype(BF16), dvm_b[:, sl]))
        dvn = jnp.concatenate(dvn, axis=1)
        dg_ref[...] += _colsum(dvn * dhat)
        ddh = dvn * g_ref[...]
        dzv = rstd * (ddh - jnp.mean(ddh, axis=-1, keepdims=True) - dhat * jnp.mean(ddh * dhat, axis=-1, keepdims=True))
        dz_ref[:, 0:W] = (du * _gelu_grad(u_ref[...])).astype(BF16)
        dz_ref[:, W:2 * W] = (dzv * _gelu_grad(v_ref[...])).astype(BF16)

    blk = lambda cb: pl.BlockSpec((SGU_CHUNK, W), lambda i: (i, cb))
    full3 = lambda a, b, c: pl.BlockSpec((a, b, c), lambda i: (0, 0, 0))
    return pl.pallas_call(
        body, name="sgu_bwd", grid=(T // SGU_CHUNK,),
        out_shape=[jax.ShapeDtypeStruct((T, 2 * W), BF16), jax.ShapeDtypeStruct((SGU_GROUPS, SGU_CHUNK, SGU_CHUNK), F32),
                   jax.ShapeDtypeStruct((SGU_GROUPS, 8, SGU_CHUNK), F32), jax.ShapeDtypeStruct((1, W), F32)],
        in_specs=[blk(COL_U), blk(COL_ZV), blk(0), _vec(W), full3(SGU_GROUPS, SGU_CHUNK, SGU_CHUNK),
                  pl.BlockSpec((SGU_CHUNK, SGU_GROUPS), lambda i: (0, 0))],
        out_specs=[pl.BlockSpec((SGU_CHUNK, 2 * W), lambda i: (i, 0)), full3(SGU_GROUPS, SGU_CHUNK, SGU_CHUNK),
                   full3(SGU_GROUPS, 8, SGU_CHUNK), _vec(W)],
        compiler_params=_cp("arbitrary"),
    )(proj, proj, dsgu, g_norm, w_spatial, b_spatial_t)


def hgrn_post_bwd(dya, o, proj, g_norm, comm):
    T, W = o.shape
    tm = min(256, T)

    def body(dy_ref, o_ref, og_ref, g_ref, do_ref, dog_ref, dg_ref):
        @pl.when(pl.program_id(0) == 0)
        def _():
            dg_ref[...] = jnp.zeros_like(dg_ref)

        g = g_ref[...]
        dg = jnp.zeros((1, HEAD_DIM), F32)
        for h in range(N_HEADS):
            sl = slice(h * HEAD_DIM, (h + 1) * HEAD_DIM)
            x, og, dy = o_ref[:, sl], og_ref[:, sl], dy_ref[:, sl]
            r = lax.rsqrt(jnp.mean(x * x, axis=-1, keepdims=True) + EPS)
            xhat = x * r
            s = jax.nn.sigmoid(og)
            don = dy * (og * s)
            dog_ref[:, sl] = (dy * (xhat * g) * (s * (1.0 + og * (1.0 - s)))).astype(BF16)
            dg += _colsum(don * xhat)
            u = don * g
            do_ref[:, sl] = r * (u - xhat * jnp.mean(u * xhat, axis=-1, keepdims=True))
        dg_ref[...] += dg

    row = pl.BlockSpec((tm, W), lambda i: (i, 0))
    return _pallas(
        body, name="hgrn_post_bwd", grid=(T // tm,),
        out_shape=[jax.ShapeDtypeStruct((T, W), F32), jax.ShapeDtypeStruct((T, W), BF16), jax.ShapeDtypeStruct((1, HEAD_DIM), F32)],
        in_specs=[row, row, pl.BlockSpec((tm, W), lambda i: (i, COL_OG)), _vec(HEAD_DIM)],
        out_specs=[row, row, _vec(HEAD_DIM)], scratch=[], semantics=("arbitrary",), operands=(dya, o, proj, g_norm), comm=comm)


def hgrn_bwd(proj, do, lb_logits, comm=None):
    T = proj.shape[0]
    NC, CPB = T // HGRN_CHUNK, HGRN_BLOCK // HGRN_CHUNK
    blk1 = min(HGRN_BLOCK_FWD, T)
    W = N_HEADS * HEAD_DIM
    col, f_spec, l_spec = _hgrn_specs(T)

    def body(l_ref, q_ref, f_ref, v_ref, do_ref, dq_ref, dv_ref, dlg_ref, dlb_ref, st_ref, dst_ref, dec_ref, ddec_ref, dqa_ref, dva_ref):
        d = pl.program_id(1)
        lb = _hgrn_lower_bound(l_ref)
        oml = 1.0 - lb
        mask = _hgrn_chunk_mask(d, HGRN_BLOCK)

        def values(rows):
            s, sn, fg, lf, k = _hgrn_gate(f_ref[rows, :], lb)
            b = _chunk_cumsum(lf, d == 1)
            bl = _chunk_total(lf)
            eb, enb, ee = jnp.exp(b), jnp.exp(-b), jnp.exp(bl - b)
            qd = q_ref[rows, :] * Q_SCALE * eb
            return s, sn, fg, k, bl, eb, enb, ee, qd, k * enb, k * ee

        def block1(i, carry):
            rows = pl.ds(pl.multiple_of(i * blk1, blk1), blk1)
            _, _, _, _, bl, _, _, _, qd, _, ke = values(rows)
            qd, ke = qd.astype(BF16), ke.astype(BF16)
            vb, dob = v_ref[rows, :].astype(BF16), do_ref[rows, :].astype(BF16)
            dec = jnp.exp(bl)
            for cc in range(blk1 // HGRN_CHUNK):
                sl = slice(cc * HGRN_CHUNK, (cc + 1) * HGRN_CHUNK)
                n = i * (blk1 // HGRN_CHUNK) + cc
                st_ref[n] = _tn(vb[sl], ke[sl])
                dst_ref[n] = _tn(dob[sl], qd[sl])
                dec_ref[n] = dec[cc * HGRN_CHUNK:cc * HGRN_CHUNK + 8, :]
            return carry

        _block_loop(T, blk1, block1, 0)

        def scan(t, s):
            n = jnp.where(d == 0, t, NC - 1 - t)
            u = st_ref[n]
            st_ref[n] = s
            return dec_ref[n][0:1, :] * s + u

        lax.fori_loop(0, NC, scan, jnp.zeros((HEAD_DIM, HEAD_DIM), F32))

        def rscan(t, ds):
            n = jnp.where(d == 0, NC - 1 - t, t)
            w = dst_ref[n]
            dst_ref[n] = ds
            ddec_ref[n] = jnp.broadcast_to(_colsum(ds * st_ref[n]), (8, HEAD_DIM))
            return dec_ref[n][0:1, :] * ds + w

        lax.fori_loop(0, NC, rscan, jnp.zeros((HEAD_DIM, HEAD_DIM), F32))

        def block3(i, dlb):
            rows = pl.ds(pl.multiple_of(i * HGRN_BLOCK, HGRN_BLOCK), HGRN_BLOCK)
            s, sn, fg, k, bl, eb, enb, ee, qd, kd, ke = values(rows)
            qdb, kdb, keb = qd.astype(BF16), kd.astype(BF16), ke.astype(BF16)
            vb, dob = v_ref[rows, :].astype(BF16), do_ref[rows, :].astype(BF16)
            att = jnp.where(mask, _nt(qdb, kdb), 0.0).astype(BF16)
            datt = jnp.where(mask, _nt(dob, vb), 0.0).astype(BF16)
            dv = _tn(att, dob)
            dqd = _nn(datt, kdb)
            dkd = _tn(datt, qdb)
            dv_i, dqd_i, dke, ddl = [], [], [], []
            for cc in range(CPB):
                sl = slice(cc * HGRN_CHUNK, (cc + 1) * HGRN_CHUNK)
                n = i * CPB + cc
                st_b, dst_b = st_ref[n].astype(BF16), dst_ref[n].astype(BF16)
                dv_i.append(_nt(keb[sl], dst_b))
                dqd_i.append(_nn(dob[sl], st_b))
                dke.append(_nn(vb[sl], dst_b))
                ddl.append(jnp.broadcast_to(ddec_ref[n][0:1, :] * dec_ref[n][0:1, :], (HGRN_CHUNK, HEAD_DIM)))
            dv = dv + jnp.concatenate(dv_i, axis=0)
            dqd = dqd + jnp.concatenate(dqd_i, axis=0)
            dke = jnp.concatenate(dke, axis=0)
            dq = dqd * eb * Q_SCALE
            dk = dkd * enb + dke * ee
            t_end = dke * ke
            db = dqd * qd - dkd * kd - t_end
            dlf = _chunk_cumsum(db, d == 0) + _chunk_total(t_end) + jnp.concatenate(ddl, axis=0)
            e = dlf / fg - dk
            dlg_ref[rows, :] = (oml * e * s * sn).astype(BF16)

            dq = jnp.where(d == 0, 0.0, dqa_ref[rows, :]) + dq
            dv = jnp.where(d == 0, 0.0, dva_ref[rows, :]) + dv
            dqa_ref[rows, :] = dq
            dva_ref[rows, :] = dv
            dq_ref[rows, :] = dq.astype(BF16)
            dv_ref[rows, :] = dv.astype(BF16)

            return dlb + _colsum(e * sn)

        dlb_ref[...] = _block_loop(T, HGRN_BLOCK, block3, jnp.zeros((1, HEAD_DIM), F32))

    head = pl.BlockSpec((T, HEAD_DIM), lambda h, d: (0, h))
    big = pltpu.VMEM((NC, HEAD_DIM, HEAD_DIM), F32)
    small = pltpu.VMEM((NC, 8, HEAD_DIM), F32)
    acc = pltpu.VMEM((T, HEAD_DIM), F32)
    outs, landed = _pallas(
        body, name="hgrn_bwd", grid=(N_HEADS, 2),
        out_shape=[jax.ShapeDtypeStruct((T, W), BF16), jax.ShapeDtypeStruct((T, W), BF16), jax.ShapeDtypeStruct((T, 2 * W), BF16),
                   jax.ShapeDtypeStruct((2, 1, W), F32)],
        in_specs=[l_spec, col(COL_Q), f_spec, col(COL_V), head],
        out_specs=[head, head, pl.BlockSpec((T, HEAD_DIM), lambda h, d: (0, N_HEADS * d + h)),
                   pl.BlockSpec((None, 1, HEAD_DIM), lambda h, d: (d, 0, h))],
        scratch=[big, big, small, small, acc, acc], semantics=("parallel", "arbitrary"), operands=(lb_logits, proj, proj, proj, do), comm=comm)
    return outs if comm is None else (outs, landed)


def mix_norm_bwd(da1, h0, dh1, g_pre, sc1):
    T, D = h0.shape
    tm = min(256, T)

    def body(da_ref, h_ref, dh_ref, g_ref, sc_ref, gx_ref, s_sh, s_sc, s_g):
        @pl.when(pl.program_id(0) == 0)
        def _():
            for s in (s_sh, s_sc, s_g):
                s[...] = jnp.zeros_like(s)

        h, da = h_ref[...], da_ref[...]
        g, sc = g_ref[...], sc_ref[...]
        r = lax.rsqrt(jnp.mean(h * h, axis=-1, keepdims=True) + EPS)
        n = h * r
        s_sh[...] += _colsum(da)
        s_sc[...] += _colsum(da * (n * g))
        s_g[...] += _colsum(da * (1.0 + sc) * n)
        dn = da * g * (1.0 + sc)
        gx_ref[...] = dh_ref[...] + r * (dn - n * jnp.mean(dn * n, axis=-1, keepdims=True))

    row = pl.BlockSpec((tm, D), lambda i: (i, 0))
    return pl.pallas_call(
        body, name="mix_norm_bwd", grid=(T // tm,),
        out_shape=[jax.ShapeDtypeStruct((T, D), F32)] + [jax.ShapeDtypeStruct((1, D), F32)] * 3,
        in_specs=[row, row, row, _vec(D), _vec(D)], out_specs=[row] + [_vec(D)] * 3, compiler_params=_cp("arbitrary"),
    )(da1, h0, dh1, g_pre, sc1)


def adamw(w, g, m, v, name):
    R, C = w.shape
    tr = R if R * C * 4 <= (1 << 21) else max(8, ((1 << 21) // (C * 4)) // 8 * 8)
    while R % tr:
        tr -= 8

    def body(w_ref, g_ref, m_ref, v_ref, d_ref, m2_ref, v2_ref):
        d_ref[...], m2_ref[...], v2_ref[...] = _adamw(w_ref[...], g_ref[...], m_ref[...], v_ref[...])

    row = pl.BlockSpec((tr, C), lambda i: (i, 0))
    return pl.pallas_call(
        body, name=name, grid=(R // tr,), out_shape=[jax.ShapeDtypeStruct((R, C), F32)] * 3,
        in_specs=[row] * 4, out_specs=[row] * 3, compiler_params=_cp("parallel"),
    )(w, g, m, v)


def wada_update(c_all, dmod, w, m, v):
    D, N = w.shape
    tm, tn = 512, 1024

    def body(c_ref, dm_ref, w_ref, m_ref, v_ref, g_ref, d_ref, m2_ref, v2_ref):
        c = c_ref[...]
        g = lax.dot_general(c * jax.nn.sigmoid(c), dm_ref[...], (((0,), (0,)), ((), ())), precision=HI, preferred_element_type=F32)
        g_ref[...] = g
        d_ref[...], m2_ref[...], v2_ref[...] = _adamw(w_ref[...], g, m_ref[...], v_ref[...])

    blk = pl.BlockSpec((tm, tn), lambda i, j: (i, j))
    return pl.pallas_call(
        body, name="wada_update", grid=(D // tm, N // tn), out_shape=[jax.ShapeDtypeStruct((D, N), F32)] * 4,
        in_specs=[pl.BlockSpec((8, tm), lambda i, j: (0, i)), pl.BlockSpec((8, tn), lambda i, j: (0, j)), blk, blk, blk],
        out_specs=[blk] * 4, compiler_params=_cp("parallel", "parallel"),
    )(c_all, dmod, w, m, v)


def sum_devices(gathered, name):
    n, R, C = gathered.shape

    def body(g_ref, o_ref):
        s = g_ref[0]
        for i in range(1, n):
            s = s + g_ref[i]
        o_ref[...] = s

    return pl.pallas_call(body, name=name, out_shape=jax.ShapeDtypeStruct((R, C), F32), compiler_params=_cp())(gathered)


def lb_logits_grad(dlb, lb_logits):
    def body(d_ref, l_ref, o_ref):
        for d in range(2):
            l0, l1 = l_ref[d, 0:1, :], l_ref[d, 1:2, :]
            m = jnp.maximum(l0, l1)
            e0, e1 = jnp.exp(l0 - m), jnp.exp(l1 - m)
            p0, p1 = e0 / (e0 + e1), e1 / (e0 + e1)
            g = d_ref[d:d + 1, :]
            o_ref[d, 0:1, :] = p0 * (g - p0 * g)
            o_ref[d, 1:2, :] = -p1 * (p0 * g)

    return pl.pallas_call(body, name="lb_logits_grad", out_shape=jax.ShapeDtypeStruct(lb_logits.shape, F32), compiler_params=_cp())(dlb, lb_logits)


def add_halves(g, landed, core):
    nj, _, r, cc = g.shape
    tr = min(256, r)

    def body(core_ref, g_ref, l_ref, o_ref):
        o_ref[...] = (g_ref[...].astype(F32) + l_ref[...].astype(F32)).astype(BF16)

    return pl.pallas_call(
        body, name="add_halves_%dx%d" % (r, cc), out_shape=jax.ShapeDtypeStruct((nj, r, cc), BF16),
        grid_spec=pltpu.PrefetchScalarGridSpec(
            num_scalar_prefetch=1, grid=(nj, r // tr),
            in_specs=[pl.BlockSpec((None, None, tr, cc), lambda j, i, core_ref: (j, core_ref[0], i, 0)),
                      pl.BlockSpec((None, None, tr, cc), lambda j, i, core_ref: (j, 0, i, 0))],
            out_specs=pl.BlockSpec((None, tr, cc), lambda j, i, core_ref: (j, i, 0))),
        compiler_params=_cp("parallel", "parallel"),
    )(core, g, landed)


def sum_chips(parts, landed, chip):
    nj, r, cc = parts.shape
    tr = min(256, r)

    def body(chip_ref, p_ref, l_ref, o_ref):
        mine = p_ref[...].astype(F32)
        s = None
        for j in range(nj):
            t = jnp.where(chip_ref[0] == j, mine, l_ref[j].astype(F32))
            s = t if s is None else s + t
        o_ref[...] = s

    return pl.pallas_call(
        body, name="sum_chips_%dx%d" % (r, cc), out_shape=jax.ShapeDtypeStruct((r, cc), F32),
        grid_spec=pltpu.PrefetchScalarGridSpec(
            num_scalar_prefetch=1, grid=(r // tr,),
            in_specs=[pl.BlockSpec((None, tr, cc), lambda i, chip_ref: (chip_ref[0], i, 0)),
                      pl.BlockSpec((nj, tr, cc), lambda i, chip_ref: (0, i, 0))],
            out_specs=pl.BlockSpec((tr, cc), lambda i, chip_ref: (i, 0))),
        compiler_params=_cp("parallel"),
    )(chip, parts, landed)


def adamw_halves(w, own, other, m, v, core, name):
    r, cc = own.shape
    tr = min(128, r)
    nb = r // tr

    def body(core_ref, w_ref, a_ref, b_ref, m_ref, v_ref, g_ref, d_ref, m2_ref, v2_ref):
        g = jnp.where(pl.program_id(0) == core_ref[0], a_ref[...], b_ref[...])
        g_ref[...] = g
        d_ref[...], m2_ref[...], v2_ref[...] = _adamw(w_ref[...], g, m_ref[...], v_ref[...])

    full = pl.BlockSpec((tr, cc), lambda h, i, core_ref: (h * nb + i, 0))
    mine = pl.BlockSpec((tr, cc), lambda h, i, core_ref: (jnp.where(h == core_ref[0], i, 0), 0))
    theirs = pl.BlockSpec((tr, cc), lambda h, i, core_ref: (jnp.where(h == core_ref[0], 0, i), 0))
    return pl.pallas_call(
        body, name=name, out_shape=[jax.ShapeDtypeStruct((2 * r, cc), F32)] * 4,
        grid_spec=pltpu.PrefetchScalarGridSpec(
            num_scalar_prefetch=1, grid=(2, nb), in_specs=[full, mine, theirs, full, full], out_specs=[full] * 4),
        compiler_params=_cp("arbitrary", "arbitrary"),
    )(core, w, own, other, m, v)


def _place():
    mx, my, mc = lax.axis_index("x"), lax.axis_index("y"), lax.axis_index("c")
    chips = [(1 - mx, my), (mx, 1 - my), (1 - mx, 1 - my)]
    return mx, my, mc, chips


def all_gather_small(x, name):
    R, C = x.shape

    def body(x_ref, out_ref, send_sems, recv_sems, local_sem):
        mx, my, mc, _ = _place()
        me = 4 * mx + 2 * my + mc
        mine = pltpu.make_async_copy(x_ref, out_ref.at[me], local_sem)
        mine.start()

        def peer(k):
            px = 1 - mx if k & 4 else mx
            py = 1 - my if k & 2 else my
            pc = 1 - mc if k & 1 else mc
            return px, py, pc

        def copy(k, src, slot):
            return pltpu.make_async_remote_copy(src_ref=src, dst_ref=out_ref.at[slot], send_sem=send_sems.at[k - 1],
                                                recv_sem=recv_sems.at[k - 1], device_id=peer(k), device_id_type=MESH)

        sends = [copy(k, x_ref, me) for k in range(1, 8)]
        for cp in sends:
            cp.start()
        for k in range(1, 8):
            px, py, pc = peer(k)
            slot = 4 * px + 2 * py + pc
            copy(k, out_ref.at[slot], slot).wait_recv()
        for cp in sends:
            cp.wait_send()
        mine.wait()

    return pl.pallas_call(
        body, name=name, out_shape=jax.ShapeDtypeStruct((8, R, C), F32),
        in_specs=[pl.BlockSpec(memory_space=pltpu.VMEM)], out_specs=pl.BlockSpec(memory_space=pltpu.VMEM),
        scratch_shapes=[pltpu.SemaphoreType.DMA((7,)), pltpu.SemaphoreType.DMA((7,)), pltpu.SemaphoreType.DMA],
        compiler_params=_cp(),
    )(x)


def gather8_comm(x):
    def copies(x_ref, out_ref, send_sems, recv_sems):
        mx, my, mc, _ = _place()
        me = 4 * mx + 2 * my + mc

        def peer(k):
            return (1 - mx if k & 4 else mx, 1 - my if k & 2 else my, 1 - mc if k & 1 else mc)

        def copy(k, src, slot):
            return pltpu.make_async_remote_copy(src_ref=src, dst_ref=out_ref.at[slot], send_sem=send_sems.at[k - 1],
                                                recv_sem=recv_sems.at[k - 1], device_id=peer(k), device_id_type=MESH)

        sends = [copy(k, x_ref, me) for k in range(1, 8)]
        arrivals = []
        for k in range(1, 8):
            px, py, pc = peer(k)
            slot = 4 * px + 2 * py + pc
            arrivals.append(copy(k, out_ref.at[slot], slot))
        return sends, arrivals, pltpu.make_async_copy(x_ref, out_ref.at[me], send_sems.at[7])

    def start(cin, cout, send_sems, recv_sems):
        sends, _, mine = copies(cin[0], cout[0], send_sems, recv_sems)
        mine.start()
        for cp in sends:
            cp.start()

    def finish(cin, cout, send_sems, recv_sems):
        sends, arrivals, mine = copies(cin[0], cout[0], send_sems, recv_sems)
        for cp in arrivals:
            cp.wait_recv()
        for cp in sends:
            cp.wait_send()
        mine.wait()

    return _Comm([x], [jax.ShapeDtypeStruct((8,) + x.shape, F32)], {}, 8, start, finish)


def _join(a, b):
    na_in, na_out = len(a.operands), len(a.out_shape)

    def split(fn_a, fn_b):
        def both(cin, cout, send_sems, recv_sems):
            fn_a(cin[:na_in], cout[:na_out], send_sems.at[pl.ds(0, a.n_sems)], recv_sems.at[pl.ds(0, a.n_sems)])
            fn_b(cin[na_in:], cout[na_out:], send_sems.at[pl.ds(a.n_sems, b.n_sems)], recv_sems.at[pl.ds(a.n_sems, b.n_sems)])
        return both

    aliases = dict(a.aliases)
    aliases.update({na_in + i: na_out + o for i, o in b.aliases.items()})
    return _Comm(a.operands + b.operands, a.out_shape + b.out_shape, aliases, a.n_sems + b.n_sems, split(a.start, b.start), split(a.finish, b.finish))


def _region(ref, kind, j, half, r, cc):
    nr = r if half is None else r // 2
    off = 0 if half is None else half * nr
    if kind == "col":
        return ref.at[pl.ds(off, nr), pl.ds(pl.multiple_of(j * cc, 128), cc)]
    return ref.at[pl.ds(pl.multiple_of(j * r + off, 16), nr), :]


def comm_call(comm, name):
    ni, no = len(comm.operands), len(comm.out_shape)

    def body(*refs):
        comm.start(refs[:ni], refs[ni:ni + no], *refs[ni + no:])
        comm.finish(refs[:ni], refs[ni:ni + no], *refs[ni + no:])

    return pl.pallas_call(
        body, name=name, out_shape=comm.out_shape, in_specs=[ANY] * ni, out_specs=[ANY] * no, input_output_aliases=comm.aliases,
        scratch_shapes=[pltpu.SemaphoreType.DMA((comm.n_sems,)), pltpu.SemaphoreType.DMA((comm.n_sems,))], compiler_params=_cp(),
    )(*comm.operands)


def gather_comm(fulls, kinds, dims):
    n = len(fulls)

    def copies(f_refs, send_sems, recv_sems):
        mx, my, mc, chips = _place()
        jme = 2 * mx + my

        def landed(w, k, half):
            px, py = chips[k]
            return _region(f_refs[w], kinds[w], 2 * px + py, half, *dims[w])

        def over_ici(w, k, reg):
            px, py = chips[k]
            return pltpu.make_async_remote_copy(src_ref=reg, dst_ref=reg, send_sem=send_sems.at[6 * w + k], recv_sem=recv_sems.at[6 * w + k],
                                                device_id=(px, py, mc), device_id_type=MESH)

        def over_d2d(w, k, half):
            reg = landed(w, k, half)
            return pltpu.make_async_remote_copy(src_ref=reg, dst_ref=reg, send_sem=send_sems.at[6 * w + 3 + k],
                                                recv_sem=recv_sems.at[6 * w + 3 + k], device_id=(mx, my, 1 - mc), device_id_type=MESH)

        sends = [over_ici(w, k, _region(f_refs[w], kinds[w], jme, mc, *dims[w])) for w in range(n) for k in range(3)]
        return mc, landed, over_ici, over_d2d, sends

    def start(cin, f_refs, send_sems, recv_sems):
        for cp in copies(f_refs, send_sems, recv_sems)[4]:
            cp.start()

    def finish(cin, f_refs, send_sems, recv_sems):
        mc, landed, over_ici, over_d2d, sends = copies(f_refs, send_sems, recv_sems)
        passed = []
        for w in range(n):
            for k in range(3):
                over_ici(w, k, landed(w, k, mc)).wait_recv()
                cp = over_d2d(w, k, mc)
                cp.start()
                passed.append(cp)
        for w in range(n):
            for k in range(3):
                over_d2d(w, k, 1 - mc).wait_recv()
        for cp in sends + passed:
            cp.wait_send()

    return _Comm(fulls, [jax.ShapeDtypeStruct(f.shape, BF16) for f in fulls], {w: w for w in range(n)}, 6 * n, start, finish)


def exchange_comm(grads):
    n = len(grads)

    def copies(g_refs, l_refs, send_sems, recv_sems):
        mx, my, mc, _ = _place()
        return [pltpu.make_async_remote_copy(src_ref=g_refs[w].at[:, pl.ds(1 - mc, 1)], dst_ref=l_refs[w], send_sem=send_sems.at[w],
                                             recv_sem=recv_sems.at[w], device_id=(mx, my, 1 - mc), device_id_type=MESH) for w in range(n)]

    def start(*refs):
        for cp in copies(*refs):
            cp.start()

    def finish(*refs):
        for cp in copies(*refs):
            cp.wait()

    return _Comm(grads, [jax.ShapeDtypeStruct((g.shape[0], 1) + g.shape[2:], BF16) for g in grads], {}, n, start, finish)


def exchange_halves(grads, name):
    return comm_call(exchange_comm(grads), name)


def scatter_comm(parts):
    n = len(parts)

    def sends(p_refs, l_refs, send_sems, recv_sems):
        mx, my, mc, chips = _place()
        return [pltpu.make_async_remote_copy(src_ref=p_refs[w].at[2 * px + py], dst_ref=l_refs[w].at[2 * mx + my],
                                             send_sem=send_sems.at[3 * w + k], recv_sem=recv_sems.at[3 * w + k],
                                             device_id=(px, py, mc), device_id_type=MESH) for w in range(n) for k, (px, py) in enumerate(chips)]

    def start(p_refs, l_refs, send_sems, recv_sems):
        for cp in sends(p_refs, l_refs, send_sems, recv_sems):
            cp.start()

    def finish(p_refs, l_refs, send_sems, recv_sems):
        mx, my, mc, chips = _place()
        for w in range(n):
            for k, (px, py) in enumerate(chips):
                slot = l_refs[w].at[2 * px + py]
                pltpu.make_async_remote_copy(src_ref=slot, dst_ref=slot, send_sem=send_sems.at[3 * w + k], recv_sem=recv_sems.at[3 * w + k],
                                             device_id=(px, py, mc), device_id_type=MESH).wait_recv()
        for cp in sends(p_refs, l_refs, send_sems, recv_sems):
            cp.wait_send()

    return _Comm(parts, [jax.ShapeDtypeStruct(p.shape, BF16) for p in parts], {}, 3 * n, start, finish)


def share_comm(sums):
    n = len(sums)

    def copies(q_refs, o_refs, send_sems, recv_sems):
        mx, my, mc, _ = _place()
        return [pltpu.make_async_remote_copy(src_ref=q_refs[w], dst_ref=o_refs[w], send_sem=send_sems.at[w], recv_sem=recv_sems.at[w],
                                             device_id=(mx, my, 1 - mc), device_id_type=MESH) for w in range(n)]

    def start(*refs):
        for cp in copies(*refs):
            cp.start()

    def finish(*refs):
        for cp in copies(*refs):
            cp.wait()

    return _Comm(sums, [jax.ShapeDtypeStruct(q.shape, F32) for q in sums], {}, n, start, finish)


def _pack(arrays):
    flat = jnp.concatenate([a.reshape(-1) for a in arrays])
    rows = -(-flat.shape[0] // 1024) * 8
    return jnp.pad(flat, (0, rows * 128 - flat.shape[0])).reshape(rows, 128)


def _unpack(packed, shapes):
    flat, out, off = packed.reshape(-1), [], 0
    for s in shapes:
        n = math.prod(s)
        out.append(flat[off:off + n].reshape(s))
        off += n
    return out


def kernel(x, c, w_ada, b_ada, g_pre_mix, g_post_mix, g_pre_ffn, g_post_ffn, w_in, lb_logits, g_hgrn_norm, w_a_out, g_sgu_norm, w_spatial, b_spatial, w_b_out, w_o, w_ff1, w_ff2, loss_target, m_w_ada, m_b_ada, m_g_pre_mix, m_g_post_mix, m_g_pre_ffn, m_g_post_ffn, m_w_in, m_lb_logits, m_g_hgrn_norm, m_w_a_out, m_g_sgu_norm, m_w_spatial, m_b_spatial, m_w_b_out, m_w_o, m_w_ff1, m_w_ff2, v_w_ada, v_b_ada, v_g_pre_mix, v_g_post_mix, v_g_pre_ffn, v_g_post_ffn, v_w_in, v_lb_logits, v_g_hgrn_norm, v_w_a_out, v_g_sgu_norm, v_w_spatial, v_b_spatial, v_w_b_out, v_w_o, v_w_ff1, v_w_ff2):
    mx, my, mc = lax.axis_index("x"), lax.axis_index("y"), lax.axis_index("c")
    chip, me = 2 * mx + my, 4 * mx + 2 * my + mc
    D = D_MODEL
    h0, tgt = x[0], loss_target[0]
    n_ada = w_ada.shape[2]
    n_lb = lb_logits.shape[2]

    got = all_gather_small(_pack([c, lb_logits]), "gather_inputs")
    c_all = got[:, :D // 128, :].reshape(8, D)
    lb_full = got[0::2, D // 128:D // 128 + 4 * n_lb // 128, :].reshape(4, 2, 2, n_lb).transpose(1, 2, 0, 3).reshape(2, 2, 4 * n_lb)
    b_ada_chip = lax.dynamic_slice(b_ada, (0, chip * n_ada), (1, n_ada))
    mod_cols = mod_matmul(c_all, w_ada[0], b_ada_chip)
    got = all_gather_small(mod_cols.reshape(-1, 128), "gather_mod").reshape(4, 2, 8, n_ada)
    mod = lax.dynamic_index_in_dim(got[:, 0], me, axis=1, keepdims=False).reshape(6, 1, D)
    sh1, sc1, gt1, sh2, sc2, gt2 = (mod[i] for i in range(6))

    big = [("w_in", w_in, "col"), ("w_a_out", w_a_out, "col"), ("w_b_out", w_b_out, "col"), ("w_o", w_o, "row"),
           ("w_ff1", w_ff1, "col"), ("w_ff2", w_ff2, "row")]
    kinds = [k for _, _, k in big]
    chip_idx, core = chip.reshape(1).astype(jnp.int32), mc.reshape(1).astype(jnp.int32)
    fulls = [cast_into_full(w[0], kind, chip_idx, "cast_" + nm) for nm, w, kind in big]
    dims = [w.shape[1:] for _, w, _ in big]
    later = lambda lo, hi: gather_comm(fulls[lo:hi], kinds[lo:hi], dims[lo:hi])
    halves_summed = lambda grads, name: [add_halves(g, l, core) for g, l in zip(grads, exchange_halves(grads, name))]

    bst = b_spatial[0].T
    a1 = prenorm(h0, g_pre_mix, sc1, sh1)
    proj, w_in_f, (w_a_f, w_b_f, w_o_f) = in_proj_gathered(a1, fulls[0], chip_idx, dims[0], later(1, 4))
    o, (w_ff1_f,) = hgrn_fwd(proj, lb_full, comm=later(4, 5))
    ya_pre = hgrn_post_fwd(o, proj, g_hgrn_norm)
    sgu = sgu_fwd(proj, g_sgu_norm, w_spatial[0], bst)
    y_a, y_b, merged = merge_matmul(ya_pre, sgu, w_a_f, w_b_f, proj)
    mo, h1, a2 = out_proj(merged, w_o_f, h0, gt1, g_post_mix, g_pre_ffn, sc2, sh2)
    (f1, hid), (w_ff2_f,) = matmul(a2, w_ff1_f, mode="nn", out_dtype=BF16, tm=1024, tn=1024, tk=2048, name="ff1", relu2=True,
                                   comm=later(5, 6))
    ff = matmul(hid, w_ff2_f, mode="nn", out_dtype=F32, tm=1024, tn=1024, tk=2048, name="ff2")
    dy, dff, loss_parts, d_gt2, d_g_post_ffn = loss_bwd(ff, h1, tgt, gt2, g_post_ffn)

    df1 = ff2_bwd(dff, w_ff2_f, f1)
    gr_ff2 = matmul(hid, dff, mode="tn", out_dtype=BF16, tm=1024, tn=1024, tk=2048, name="dw_ff2")
    gr_ff2 = gr_ff2.reshape(4, 2, -1, D)
    da2, (landed_ff2,) = matmul(df1, w_ff1_f, mode="nt", out_dtype=F32, tm=1024, tn=1024, tk=2048, name="da2", comm=exchange_comm([gr_ff2]))
    gr_ff1 = matmul(a2, df1, mode="tn", out_dtype=BF16, tm=1024, tn=2048, tk=2048, name="dw_ff1", split=(4, 2))
    (dh1, dmo, d_sh2, d_sc2, d_g_pre_ffn, d_gt1, d_g_post_mix), _ = ffn_norm_bwd(dy, da2, h1, mo, g_pre_ffn, sc2, gt1, g_post_mix, None)
    (dya, dyb, dga, dgb), (landed_ff1,) = out_proj_bwd(dmo, w_o_f, y_a, y_b, proj, exchange_comm([gr_ff1]))
    parts_ff = [add_halves(gr_ff1, landed_ff1, core), add_halves(gr_ff2, landed_ff2, core)]
    gr_o = matmul(merged, dmo, mode="tn", out_dtype=BF16, tm=1024, tn=1024, tk=2048, name="dw_o")
    dsgu = matmul(dyb, w_b_f, mode="nt", out_dtype=F32, tm=512, tn=1024, tk=2048, name="dsgu")
    gr_b = matmul(sgu, dyb, mode="tn", out_dtype=BF16, tm=512, tn=512, tk=4096, name="dw_b_out", split=(4, 2))
    dz, d_w_spatial, d_b_spatial, d_g_sgu = sgu_bwd(proj, dsgu, g_sgu_norm, w_spatial[0], bst)
    dya_pre = matmul(dya, w_a_f, mode="nt", out_dtype=F32, tm=512, tn=1024, tk=2048, name="dya_pre")
    gr_a = matmul(ya_pre, dya, mode="tn", out_dtype=BF16, tm=512, tn=512, tk=4096, name="dw_a_out", split=(4, 2))
    gr_mix = [gr_a, gr_b, gr_o.reshape(4, 2, -1, D)]
    (do, dog, d_g_hgrn), landed_halves = hgrn_post_bwd(dya_pre, o, proj, g_hgrn_norm, exchange_comm(gr_mix))
    parts_mix = [add_halves(g, l, core) for g, l in zip(gr_mix, landed_halves)]
    chips_summed = lambda parts, landed: [sum_chips(p, l, chip_idx) for p, l in zip(parts, landed)]
    (dq, dv, dlg, d_lb), landed_ff = hgrn_bwd(proj, do, lb_full, comm=scatter_comm(parts_ff))
    own_ff = chips_summed(parts_ff, landed_ff)
    dproj = jnp.concatenate([dq, dlg, dv, dog, dz, dga, dgb], axis=1)
    early = _pack([d_g_sgu, d_w_spatial, d_b_spatial[:, 0, :]])
    gr_in, (*landed_mix, got_early) = matmul(a1, dproj, mode="tn", out_dtype=BF16, tm=1024, tn=2816, tk=1024, name="dw_in", split=(4, 2),
                                             comm=_join(scatter_comm(parts_mix), gather8_comm(early)))
    own_mix = chips_summed(parts_mix, landed_mix)
    parts_in = halves_summed([gr_in], "exchange_in")
    da1, (landed_in, *other_rest) = matmul(dproj, w_in_f, mode="nt", out_dtype=F32, tm=1024, tn=1024, tk=2816, name="da1",
                                           comm=_join(scatter_comm(parts_in), share_comm(own_mix + own_ff)))
    own_in = chips_summed(parts_in, [landed_in])
    other_in = comm_call(share_comm(own_in), "share_w_in")
    own, other = own_in + own_mix + own_ff, list(other_in) + other_rest
    grad_x, d_sh1, d_sc1, d_g_pre_mix = mix_norm_bwd(da1, h0, dh1, g_pre_mix, sc1)
    out = {}

    mine = _pack([d_sh1, d_sc1, d_gt1, d_sh2, d_sc2, d_gt2, d_g_pre_mix, d_g_post_mix, d_g_pre_ffn, d_g_post_ffn, d_g_hgrn, d_lb,
                  loss_parts[0:1, 0:1]])
    got = all_gather_small(mine, "gather_small_grads")
    g_b_ada, g_g1, g_g2, g_g3, g_g4, g_hg, g_lb, sq_err = _unpack(
        sum_devices(got, "sum_small_grads"), [(1, 6 * D), (1, D), (1, D), (1, D), (1, D), (1, HEAD_DIM), (2, 1024), ()])
    loss = 0.5 * sq_err / D
    g_sg, g_ws, g_bs = _unpack(sum_devices(got_early, "sum_sgu_grads"), [(1, 1024), w_spatial.shape, b_spatial.shape])
    g_lbl = lax.dynamic_slice(lb_logits_grad(g_lb, lb_full), (0, 0, chip * n_lb), (2, 2, n_lb))
    names = ["b_ada", "g_pre_mix", "g_post_mix", "g_pre_ffn", "g_post_ffn", "g_hgrn_norm", "g_sgu_norm", "w_spatial", "b_spatial", "lb_logits"]
    ws = [b_ada, g_pre_mix, g_post_mix, g_pre_ffn, g_post_ffn, g_hgrn_norm, g_sgu_norm, w_spatial, b_spatial, lb_logits]
    gs = [g_b_ada, g_g1, g_g2, g_g3, g_g4, g_hg, g_sg, g_ws, g_bs, g_lbl]
    ms = [m_b_ada, m_g_pre_mix, m_g_post_mix, m_g_pre_ffn, m_g_post_ffn, m_g_hgrn_norm, m_g_sgu_norm, m_w_spatial, m_b_spatial, m_lb_logits]
    vs = [v_b_ada, v_g_pre_mix, v_g_post_mix, v_g_pre_ffn, v_g_post_ffn, v_g_hgrn_norm, v_g_sgu_norm, v_w_spatial, v_b_spatial, v_lb_logits]
    shapes = [w.shape for w in ws]
    upd = adamw(_pack(ws), _pack(gs), _pack(ms), _pack(vs), "adamw_small")
    upd = [_unpack(u, shapes) for u in upd]
    for i, nm in enumerate(names):
        out[nm] = (gs[i], upd[0][i], upd[1][i], upd[2][i])

    dmod_all = got[:, :6 * D // 128, :].reshape(8, 6 * D)
    dmod_chip = lax.dynamic_slice(dmod_all, (0, chip * n_ada), (8, n_ada))
    out["w_ada"] = tuple(a[None] for a in wada_update(c_all, dmod_chip, w_ada[0], m_w_ada[0], v_w_ada[0]))
    for (nm, w, _), a, b, m, v in zip(big, own, other, (m_w_in, m_w_a_out, m_w_b_out, m_w_o, m_w_ff1, m_w_ff2),
                                      (v_w_in, v_w_a_out, v_w_b_out, v_w_o, v_w_ff1, v_w_ff2)):
        out[nm] = tuple(t[None] for t in adamw_halves(w[0], a, b, m[0], v[0], core, "adamw_" + nm))

    order = ["w_ada", "b_ada", "g_pre_mix", "g_post_mix", "g_pre_ffn", "g_post_ffn", "w_in", "lb_logits", "g_hgrn_norm", "w_a_out",
             "g_sgu_norm", "w_spatial", "b_spatial", "w_b_out", "w_o", "w_ff1", "w_ff2"]
    return (loss, grad_x[None], *[out[nm][0] for nm in order], *[out[nm][1] for nm in order], *[out[nm][2] for nm in order],
            *[out[nm][3] for nm in order])
```

```python
import functools
import math

import jax
import jax.numpy as jnp
from jax import lax
from jax.experimental import pallas as pl
from jax.experimental.pallas import tpu as pltpu

F32, BF16 = jnp.float32, jnp.bfloat16
HI = lax.Precision.HIGHEST
MESH = pl.DeviceIdType.MESH
ANY = pl.BlockSpec(memory_space=pl.ANY)

EPS = 1e-6
D_MODEL = 2048
N_HEADS = 8
HEAD_DIM = 128
HGRN_CHUNK = 32
HGRN_BLOCK = 256
HGRN_BLOCK_FWD = 512
SGU_CHUNK = 128
SGU_GROUPS = 8
Q_SCALE = HEAD_DIM ** -0.5
COL_Q, COL_FFW, COL_FBW, COL_V, COL_OG, COL_U, COL_ZV, COL_GA, COL_GB = 0, 1, 2, 3, 4, 5, 6, 7, 9
N_PROJ = 11264
VMEM_BYTES_V7X = 64 * 1024 * 1024
VMEM_LIMIT = VMEM_BYTES_V7X - 8 * 1024 * 1024

ADAM_LR, ADAM_B1, ADAM_B2, ADAM_EPS, ADAM_WD, ADAM_STEP = 0.001, 0.9, 0.999, 1e-08, 0.01, 10
ADAM_C1 = 1.0 - ADAM_B1 ** ADAM_STEP
ADAM_C2 = 1.0 - ADAM_B2 ** ADAM_STEP


def _cp(*sem):
    return pltpu.CompilerParams(dimension_semantics=sem if sem else None, vmem_limit_bytes=VMEM_LIMIT)


def _vec(d):
    return pl.BlockSpec((1, d), lambda *_: (0, 0))


def _colsum(x):
    return jnp.sum(x, axis=0, keepdims=True)


def _nt(a, b):
    return lax.dot_general(a, b, (((1,), (1,)), ((), ())), preferred_element_type=F32)


def _tn(a, b):
    return lax.dot_general(a, b, (((0,), (0,)), ((), ())), preferred_element_type=F32)


def _nn(a, b):
    return jnp.dot(a, b, preferred_element_type=F32)


def _adamw(w, g, m, v):
    m2 = ADAM_B1 * m + (1.0 - ADAM_B1) * g
    v2 = ADAM_B2 * v + (1.0 - ADAM_B2) * (g * g)
    delta = -ADAM_LR * ((m2 / ADAM_C1) / (jnp.sqrt(v2 / ADAM_C2) + ADAM_EPS) + ADAM_WD * w)
    return delta, m2, v2


class _Comm:
    def __init__(self, operands, out_shape, aliases, n_sems, start, finish):
        self.operands, self.out_shape, self.aliases, self.n_sems = list(operands), list(out_shape), dict(aliases), n_sems
        self.start, self.finish = start, finish


def _pallas(body, *, name, grid, in_specs, out_specs, out_shape, scratch, semantics, operands, comm=None):
    if comm is None:
        res = pl.pallas_call(body, name=name, grid=grid, in_specs=in_specs, out_specs=out_specs, out_shape=out_shape,
                             scratch_shapes=scratch, compiler_params=_cp(*semantics))(*operands)
        return res, []
    n_in, n_out, n_scr = len(in_specs), len(out_specs), len(scratch)
    nci, nco = len(comm.operands), len(comm.out_shape)

    def with_comm(*refs):
        ins, rest = refs[:n_in], refs[n_in:]
        cin, rest = rest[:nci], rest[nci:]
        outs, rest = rest[:n_out], rest[n_out:]
        cout, rest = rest[:nco], rest[nco:]
        scr, (send, recv) = rest[:n_scr], rest[n_scr:]
        ids = [pl.program_id(a) for a in range(len(grid))]
        first = functools.reduce(jnp.logical_and, [i == 0 for i in ids])
        last = functools.reduce(jnp.logical_and, [i == g - 1 for i, g in zip(ids, grid)])

        @pl.when(first)
        def _():
            comm.start(cin, cout, send, recv)

        body(*ins, *outs, *scr)

        @pl.when(last)
        def _():
            comm.finish(cin, cout, send, recv)

    res = pl.pallas_call(
        with_comm, name=name, grid=grid, in_specs=list(in_specs) + [ANY] * nci, out_specs=list(out_specs) + [ANY] * nco,
        out_shape=list(out_shape) + comm.out_shape, input_output_aliases={n_in + i: n_out + o for i, o in comm.aliases.items()},
        scratch_shapes=list(scratch) + [pltpu.SemaphoreType.DMA((comm.n_sems,)), pltpu.SemaphoreType.DMA((comm.n_sems,))],
        compiler_params=_cp(*["arbitrary"] * len(grid)),
    )(*operands, *comm.operands)
    return res[:n_out], res[n_out:]


def matmul(a, b, *, mode, out_dtype, tm, tn, tk, name, split=None, comm=None, relu2=False):
    if mode == "tn":
        (K, M), (_, N) = a.shape, b.shape
    elif mode == "nt":
        (M, K), (N, _) = a.shape, b.shape
    else:
        (M, K), (_, N) = a.shape, b.shape
    tm, tn, tk = min(tm, M), min(tn, N), min(tk, K)
    nk = K // tk
    a_spec = pl.BlockSpec((tk, tm), lambda i, j, k: (k, i)) if mode == "tn" else pl.BlockSpec((tm, tk), lambda i, j, k: (i, k))
    b_spec = pl.BlockSpec((tn, tk), lambda i, j, k: (j, k)) if mode == "nt" else pl.BlockSpec((tk, tn), lambda i, j, k: (k, j))
    dot = {"nn": _nn, "nt": _nt, "tn": _tn}[mode]
    if split is None:
        out_shape = jax.ShapeDtypeStruct((M, N), out_dtype)
        out_spec = pl.BlockSpec((tm, tn), lambda i, j, k: (i, j))
    else:
        nj, nh = split
        rows, cols = M // nh, N // nj
        tm, tn = min(tm, rows), min(tn, cols)
        bi, bj = rows // tm, cols // tn
        out_shape = jax.ShapeDtypeStruct((nj, nh, rows, cols), out_dtype)
        out_spec = pl.BlockSpec((None, None, tm, tn), lambda i, j, k: (j // bj, i // bi, i % bi, j % bj))

    def finish(y, o_ref, sq_ref):
        o_ref[...] = y.astype(o_ref.dtype)
        if relu2:
            p = jnp.maximum(y, 0.0)
            sq_ref[0][...] = (p * p).astype(BF16)

    if nk == 1:
        def body(a_ref, b_ref, o_ref, *sq_ref):
            finish(dot(a_ref[...], b_ref[...]), o_ref, sq_ref)
        scratch = []
    else:
        def body(a_ref, b_ref, o_ref, *rest):
            acc_ref, k = rest[-1], pl.program_id(2)

            @pl.when(k == 0)
            def _():
                acc_ref[...] = jnp.zeros_like(acc_ref)

            acc_ref[...] += dot(a_ref[...], b_ref[...])

            @pl.when(k == nk - 1)
            def _():
                finish(acc_ref[...], o_ref, rest[:-1])
        scratch = [pltpu.VMEM((tm, tn), F32)]

    out_specs, out_shapes = [out_spec], [out_shape]
    if relu2:
        out_specs, out_shapes = out_specs + [out_spec], out_shapes + [jax.ShapeDtypeStruct(out_shape.shape, BF16)]
    outs, landed = _pallas(
        body, name=name, grid=(M // tm, N // tn, nk), in_specs=[a_spec, b_spec], out_specs=out_specs, out_shape=out_shapes,
        scratch=scratch, semantics=("parallel", "parallel", "arbitrary"), operands=(a, b), comm=comm)
    out = tuple(outs) if relu2 else outs[0]
    return out if comm is None else (out, landed)


def cast_into_full(w, kind, chip, name):
    r, cc = w.shape
    tr = min(r, 512)
    nb = r // tr

    def body(chip_ref, w_ref, o_ref):
        o_ref[...] = w_ref[...].astype(BF16)

    if kind == "col":
        full, out_map = (r, 4 * cc), lambda i, chip_ref: (i, chip_ref[0])
    else:
        full, out_map = (4 * r, cc), lambda i, chip_ref: (chip_ref[0] * nb + i, 0)
    return pl.pallas_call(
        body, name=name, out_shape=jax.ShapeDtypeStruct(full, BF16),
        grid_spec=pltpu.PrefetchScalarGridSpec(
            num_scalar_prefetch=1, grid=(nb,), in_specs=[pl.BlockSpec((tr, cc), lambda i, chip_ref: (i, 0))],
            out_specs=pl.BlockSpec((tr, cc), out_map)),
        compiler_params=_cp("parallel"),
    )(chip, w)


def mod_matmul(c_all, w_ada, b_ada):
    D, N = w_ada.shape
    tn = 1024

    def body(c_ref, w_ref, b_ref, o_ref):
        c = c_ref[...]
        sc = c * jax.nn.sigmoid(c)
        o_ref[...] = jnp.dot(sc.astype(BF16), w_ref[...].astype(BF16), preferred_element_type=F32) + b_ref[...]

    return pl.pallas_call(
        body, name="mod_matmul", out_shape=jax.ShapeDtypeStruct((8, N), F32), grid=(N // tn,),
        in_specs=[pl.BlockSpec((8, D), lambda j: (0, 0)), pl.BlockSpec((D, tn), lambda j: (0, j)),
                  pl.BlockSpec((1, tn), lambda j: (0, j))],
        out_specs=pl.BlockSpec((8, tn), lambda j: (0, j)), compiler_params=_cp("parallel"),
    )(c_all, w_ada, b_ada)


def prenorm(h, g, sc, sh):
    T, D = h.shape
    tm = min(256, T)

    def body(h_ref, g_ref, sc_ref, sh_ref, a_ref):
        x = h_ref[...]
        r = lax.rsqrt(jnp.mean(x * x, axis=-1, keepdims=True) + EPS)
        a_ref[...] = ((x * r) * g_ref[...] * (1.0 + sc_ref[...]) + sh_ref[...]).astype(BF16)

    row = pl.BlockSpec((tm, D), lambda i: (i, 0))
    return pl.pallas_call(
        body, name="prenorm", out_shape=jax.ShapeDtypeStruct((T, D), BF16), grid=(T // tm,),
        in_specs=[row, _vec(D), _vec(D), _vec(D)], out_specs=row, compiler_params=_cp("parallel"),
    )(h, g, sc, sh)


def in_proj_gathered(a, w_full, chip, dims, tail):
    T, D = a.shape
    rows, cc = dims
    tm, tn = min(512, T), cc // 2
    ni = T // tm
    half = rows // 2

    nt = len(tail.operands)

    def body(chip_ref, a_ref, w_in_ref, *rest):
        tail_in, (y_ref, w_ref), rest = rest[:nt], rest[nt:nt + 2], rest[nt + 2:]
        tail_out, (wbuf, wsem, send_sems, recv_sems, tail_send, tail_recv) = rest[:nt], rest[nt:]
        q, j, i = pl.program_id(0), pl.program_id(1), pl.program_id(2)
        mx, my, mc, _ = _place()
        me = chip_ref[0]

        def tile(block, jj):
            src = w_ref.at[:, pl.ds(pl.multiple_of(block * cc + jj * tn, 128), tn)]
            return pltpu.make_async_copy(src, wbuf.at[jj], wsem.at[jj])

        def rows_half(block, hh):
            return w_ref.at[pl.ds(pl.multiple_of(hh * half, 16), half), pl.ds(pl.multiple_of(block * cc, 128), cc)]

        def over_ici(s, block):
            peer = (1 - mx if s & 2 else mx, 1 - my if s & 1 else my, mc)
            reg = rows_half(block, mc)
            return pltpu.make_async_remote_copy(src_ref=reg, dst_ref=reg, send_sem=send_sems.at[s - 1], recv_sem=recv_sems.at[s - 1],
                                                device_id=peer, device_id_type=MESH)

        def over_d2d(s, block, hh):
            reg = rows_half(block, hh)
            return pltpu.make_async_remote_copy(src_ref=reg, dst_ref=reg, send_sem=send_sems.at[2 + s], recv_sem=recv_sems.at[2 + s],
                                                device_id=(mx, my, 1 - mc), device_id_type=MESH)

        def passed_on(s, block):
            k = s - 1
            reg = w_ref.at[pl.ds(pl.multiple_of(mc * half + k * (half // 2), 16), half // 2), pl.ds(pl.multiple_of(block * cc, 128), cc)]
            peer = (1 - mx, my, mc) if s == 1 else (mx, 1 - my, mc)
            return pltpu.make_async_remote_copy(src_ref=reg, dst_ref=reg, send_sem=send_sems.at[6 + k], recv_sem=recv_sems.at[6 + k],
                                                device_id=peer, device_id_type=MESH)

        @pl.when((q == 0) & (j == 0) & (i == 0))
        def _():
            for s in (1, 2):
                over_ici(s, me).start()
            tile(me, 0).start()

        @pl.when(i == 0)
        def _():
            tile(me ^ q, j).wait()

        @pl.when((i == 0) & (j == 0))
        def _():
            tile(me ^ q, 1).start()

        y_ref[...] = _nn(a_ref[...], wbuf[j])

        @pl.when((q == 0) & (j == 1) & (i == ni - 1))
        def _():
            for s in (1, 2):
                over_ici(s, me ^ s).wait_recv()
                passed_on(s, me ^ s).start()
                over_d2d(s, me ^ s, mc).start()
            tail.start(tail_in, tail_out, tail_send, tail_recv)
            over_d2d(1, me ^ 1, 1 - mc).wait_recv()
            tile(me ^ 1, 0).start()

        @pl.when((q == 1) & (j == 1) & (i == ni - 1))
        def _():
            over_d2d(2, me ^ 2, 1 - mc).wait_recv()
            tile(me ^ 2, 0).start()

        @pl.when((q == 2) & (j == 1) & (i == ni - 1))
        def _():
            for s in (1, 2):
                passed_on(s, me ^ 3).wait_recv()
            over_d2d(3, me ^ 3, mc).start()
            over_d2d(3, me ^ 3, 1 - mc).wait_recv()
            tile(me ^ 3, 0).start()

        @pl.when((q == 3) & (j == 1) & (i == ni - 1))
        def _():
            for s in (1, 2):
                over_ici(s, me).wait_send()
                passed_on(s, me ^ s).wait_send()
            for s in (1, 2, 3):
                over_d2d(s, me ^ s, mc).wait_send()
            tail.finish(tail_in, tail_out, tail_send, tail_recv)

    dma = pltpu.SemaphoreType.DMA
    y, w_out, *tail_res = pl.pallas_call(
        body, name="in_proj", out_shape=[jax.ShapeDtypeStruct((T, 4 * cc), F32), jax.ShapeDtypeStruct(w_full.shape, BF16)] + tail.out_shape,
        grid_spec=pltpu.PrefetchScalarGridSpec(
            num_scalar_prefetch=1, grid=(4, 2, ni),
            in_specs=[pl.BlockSpec((tm, D), lambda q, j, i, chip_ref: (i, 0)), ANY] + [ANY] * nt,
            out_specs=[pl.BlockSpec((tm, tn), lambda q, j, i, chip_ref: (i, (chip_ref[0] ^ q) * 2 + j)), ANY] + [ANY] * nt,
            scratch_shapes=[pltpu.VMEM((2, D, tn), BF16), dma((2,)), dma((8,)), dma((8,)), dma((tail.n_sems,)), dma((tail.n_sems,))]),
        input_output_aliases={2: 1, **{3 + i: 2 + o for i, o in tail.aliases.items()}},
        compiler_params=_cp("arbitrary", "arbitrary", "arbitrary"),
    )(chip, a, w_full, *tail.operands)
    return y, w_out, tail_res


def _hgrn_lower_bound(l_ref):
    l0, l1 = l_ref[0:1, :], l_ref[1:2, :]
    m = jnp.maximum(l0, l1)
    e0, e1 = jnp.exp(l0 - m), jnp.exp(l1 - m)
    return e0 / (e0 + e1)


def _hgrn_chunk_mask(d, blk):
    r = lax.broadcasted_iota(jnp.int32, (blk, blk), 0)
    c = lax.broadcasted_iota(jnp.int32, (blk, blk), 1)
    same = (r // HGRN_CHUNK) == (c // HGRN_CHUNK)
    fwd = d == 0
    return same & (((c <= r) & fwd) | ((c >= r) & jnp.logical_not(fwd)))


def _chunk_total(x):
    x3 = x.reshape(x.shape[0] // HGRN_CHUNK, HGRN_CHUNK, x.shape[1])
    return jnp.broadcast_to(jnp.sum(x3, axis=1, keepdims=True), x3.shape).reshape(x.shape)


def _chunk_cumsum(x, suffix):
    pos = lax.broadcasted_iota(jnp.int32, x.shape, 0) % HGRN_CHUNK
    p, s = x, 1
    while s < HGRN_CHUNK:
        p = p + jnp.where(pos >= s, pltpu.roll(p, s, 0), 0.0)
        s *= 2
    return jnp.where(suffix, _chunk_total(x) - p + x, p)


def _block_loop(T, blk, body, init):
    n = T // blk
    return lax.fori_loop(0, n, body, init, unroll=2 if n % 2 == 0 else 1)


def _hgrn_gate(f, lb):
    s = jax.nn.sigmoid(f)
    sn = jax.nn.sigmoid(-f)
    fg = lb + (1.0 - lb) * s
    return s, sn, fg, jnp.log(fg), (1.0 - lb) * sn


def _hgrn_specs(T):
    col = lambda base: pl.BlockSpec((T, HEAD_DIM), lambda h, d: (0, base * N_HEADS + h))
    f_spec = pl.BlockSpec((T, HEAD_DIM), lambda h, d: (0, COL_FFW * N_HEADS + N_HEADS * d + h))
    l_spec = pl.BlockSpec((None, 2, HEAD_DIM), lambda h, d: (d, 0, h))
    return col, f_spec, l_spec


def hgrn_fwd(proj, lb_logits, comm=None):
    T = proj.shape[0]
    blk = min(HGRN_BLOCK_FWD, T)
    NC, CPB = T // HGRN_CHUNK, blk // HGRN_CHUNK
    col, f_spec, l_spec = _hgrn_specs(T)

    def body(l_ref, q_ref, f_ref, v_ref, o_ref, st_ref, dec_ref, qd_ref):
        d = pl.program_id(1)
        lb = _hgrn_lower_bound(l_ref)
        mask = _hgrn_chunk_mask(d, blk)

        def block(i, carry):
            rows = pl.ds(pl.multiple_of(i * blk, blk), blk)
            _, _, _, lf, k = _hgrn_gate(f_ref[rows, :], lb)
            b = _chunk_cumsum(lf, d == 1)
            bl = _chunk_total(lf)
            qd = (q_ref[rows, :] * Q_SCALE * jnp.exp(b)).astype(BF16)
            kd = (k * jnp.exp(-b)).astype(BF16)
            ke = (k * jnp.exp(bl - b)).astype(BF16)
            vb = v_ref[rows, :].astype(BF16)
            att = jnp.where(mask, _nt(qd, kd), 0.0).astype(BF16)
            o_ref[rows, :] = jnp.where(d == 0, 0.0, o_ref[rows, :]) + _nn(att, vb)
            qd_ref[rows, :] = qd
            dec = jnp.exp(bl)
            for cc in range(CPB):
                sl = slice(cc * HGRN_CHUNK, (cc + 1) * HGRN_CHUNK)
                n = i * CPB + cc
                st_ref[n] = _tn(vb[sl], ke[sl])
                dec_ref[n] = dec[cc * HGRN_CHUNK:cc * HGRN_CHUNK + 8, :]
            return carry

        _block_loop(T, blk, block, 0)

        def scan(t, s):
            n = jnp.where(d == 0, t, NC - 1 - t)
            u = st_ref[n]
            st_ref[n] = s
            return dec_ref[n][0:1, :] * s + u

        lax.fori_loop(0, NC, scan, jnp.zeros((HEAD_DIM, HEAD_DIM), F32))

        def inter(i, carry):
            rows = pl.ds(pl.multiple_of(i * blk, blk), blk)
            qd = qd_ref[rows, :]
            o_ref[rows, :] += jnp.concatenate(
                [_nt(qd[cc * HGRN_CHUNK:(cc + 1) * HGRN_CHUNK], st_ref[i * CPB + cc].astype(BF16)) for cc in range(CPB)], axis=0)
            return carry

        _block_loop(T, blk, inter, 0)

    (o,), landed = _pallas(
        body, name="hgrn_fwd", grid=(N_HEADS, 2), in_specs=[l_spec, col(COL_Q), f_spec, col(COL_V)],
        out_specs=[pl.BlockSpec((T, HEAD_DIM), lambda h, d: (0, h))], out_shape=[jax.ShapeDtypeStruct((T, N_HEADS * HEAD_DIM), F32)],
        scratch=[pltpu.VMEM((NC, HEAD_DIM, HEAD_DIM), F32), pltpu.VMEM((NC, 8, HEAD_DIM), F32), pltpu.VMEM((T, HEAD_DIM), BF16)],
        semantics=("parallel", "arbitrary"), operands=(lb_logits, proj, proj, proj), comm=comm)
    return o if comm is None else (o, landed)


def hgrn_post_fwd(o, proj, g_norm):
    T, W = o.shape
    tm = min(256, T)

    def body(o_ref, og_ref, g_ref, y_ref):
        g = g_ref[...]
        for h in range(N_HEADS):
            sl = slice(h * HEAD_DIM, (h + 1) * HEAD_DIM)
            x = o_ref[:, sl]
            r = lax.rsqrt(jnp.mean(x * x, axis=-1, keepdims=True) + EPS)
            og = og_ref[:, sl]
            y_ref[:, sl] = ((x * r) * g * (og * jax.nn.sigmoid(og))).astype(BF16)

    return pl.pallas_call(
        body, name="hgrn_post_fwd", out_shape=jax.ShapeDtypeStruct((T, W), BF16), grid=(T // tm,),
        in_specs=[pl.BlockSpec((tm, W), lambda i: (i, 0)), pl.BlockSpec((tm, W), lambda i: (i, COL_OG)), _vec(HEAD_DIM)],
        out_specs=pl.BlockSpec((tm, W), lambda i: (i, 0)), compiler_params=_cp("parallel"),
    )(o, proj, g_norm)


def _gelu(x):
    return 0.5 * x * (1.0 + lax.erf(x * (1.0 / math.sqrt(2.0))))


def _gelu_grad(x):
    return 0.5 * (1.0 + lax.erf(x * (1.0 / math.sqrt(2.0)))) + x * jnp.exp(-0.5 * x * x) * (1.0 / math.sqrt(2.0 * math.pi))


def _sgu_mix(u_ref, v_ref, g_ref, ws_ref, bst_ref):
    W = u_ref.shape[1]
    zu, zv = _gelu(u_ref[...]), _gelu(v_ref[...])
    dv = zv - jnp.mean(zv, axis=-1, keepdims=True)
    rstd = lax.rsqrt(jnp.mean(dv * dv, axis=-1, keepdims=True) + EPS)
    dhat = dv * rstd
    vn = (dhat * g_ref[...]).astype(BF16)
    gw = W // SGU_GROUPS
    vm = [_nn(ws_ref[g].astype(BF16), vn[:, g * gw:(g + 1) * gw]) + bst_ref[:, g:g + 1] for g in range(SGU_GROUPS)]
    return zu, rstd, dhat, vn, jnp.concatenate(vm, axis=1)


def sgu_fwd(proj, g_norm, w_spatial, b_spatial_t):
    T = proj.shape[0]
    W = 1024
    n_chunks = T // SGU_CHUNK

    def body(u_ref, v_ref, g_ref, ws_ref, bst_ref, y_ref):
        zu, _, _, _, vm = _sgu_mix(u_ref, v_ref, g_ref, ws_ref, bst_ref)
        y_ref[...] = (zu * vm).astype(BF16)

    blk = lambda cb: pl.BlockSpec((SGU_CHUNK, W), lambda i: (i, cb))
    return pl.pallas_call(
        body, name="sgu_fwd", out_shape=jax.ShapeDtypeStruct((T, W), BF16), grid=(n_chunks,),
        in_specs=[blk(COL_U), blk(COL_ZV), _vec(W), pl.BlockSpec((SGU_GROUPS, SGU_CHUNK, SGU_CHUNK), lambda i: (0, 0, 0)),
                  pl.BlockSpec((SGU_CHUNK, SGU_GROUPS), lambda i: (0, 0))],
        out_specs=blk(0), compiler_params=_cp("parallel"),
    )(proj, proj, g_norm, w_spatial, b_spatial_t)


def merge_matmul(ya_pre, sgu, w_a, w_b, proj):
    T, K = ya_pre.shape
    N = w_a.shape[1]
    tm, tn = min(512, T), 512
    gpb = 1024 // tn

    def body(a_ref, b_ref, wa_ref, wb_ref, ga_ref, gb_ref, ya_ref, yb_ref, m_ref):
        ya = _nn(a_ref[...], wa_ref[...])
        yb = _nn(b_ref[...], wb_ref[...])
        ya_ref[...] = ya.astype(BF16)
        yb_ref[...] = yb.astype(BF16)
        m_ref[...] = (jax.nn.sigmoid(ga_ref[...]) * ya + jax.nn.sigmoid(gb_ref[...]) * yb).astype(BF16)

    lhs = pl.BlockSpec((tm, K), lambda i, j: (i, 0))
    rhs = pl.BlockSpec((K, tn), lambda i, j: (0, j))
    out = pl.BlockSpec((tm, tn), lambda i, j: (i, j))
    return pl.pallas_call(
        body, name="merge_matmul", grid=(T // tm, N // tn),
        out_shape=[jax.ShapeDtypeStruct((T, N), BF16)] * 3,
        in_specs=[lhs, lhs, rhs, rhs, pl.BlockSpec((tm, tn), lambda i, j: (i, COL_GA * gpb + j)),
                  pl.BlockSpec((tm, tn), lambda i, j: (i, COL_GB * gpb + j))],
        out_specs=[out, out, out], compiler_params=_cp("parallel", "parallel"),
    )(ya_pre, sgu, w_a, w_b, proj, proj)


def out_proj(merged, w_o, h0, gt1, g_post, g_pre2, sc2, sh2):
    T, D = h0.shape
    tm = min(256, T)

    def body(m_ref, w_ref, h_ref, gt_ref, gp_ref, g2_ref, sc_ref, sh_ref, mo_ref, h1_ref, a2_ref):
        mo = _nn(m_ref[...], w_ref[...])
        mo_ref[...] = mo
        r = lax.rsqrt(jnp.mean(mo * mo, axis=-1, keepdims=True) + EPS)
        h1 = h_ref[...] + gt_ref[...] * ((mo * r) * gp_ref[...])
        h1_ref[...] = h1
        r2 = lax.rsqrt(jnp.mean(h1 * h1, axis=-1, keepdims=True) + EPS)
        a2_ref[...] = ((h1 * r2) * g2_ref[...] * (1.0 + sc_ref[...]) + sh_ref[...]).astype(BF16)

    row = pl.BlockSpec((tm, D), lambda i: (i, 0))
    return pl.pallas_call(
        body, name="out_proj", grid=(T // tm,),
        out_shape=[jax.ShapeDtypeStruct((T, D), F32), jax.ShapeDtypeStruct((T, D), F32), jax.ShapeDtypeStruct((T, D), BF16)],
        in_specs=[row, pl.BlockSpec((D, D), lambda i: (0, 0)), row] + [_vec(D)] * 5,
        out_specs=[row, row, row], compiler_params=_cp("parallel"),
    )(merged, w_o, h0, gt1, g_post, g_pre2, sc2, sh2)


def loss_bwd(ff, h1, tgt, gt2, g_post):
    T, D = ff.shape
    tm = min(256, T)

    def body(f_ref, h_ref, t_ref, gt_ref, g_ref, dy_ref, dff_ref, loss_ref, dgt_ref, dg_ref):
        @pl.when(pl.program_id(0) == 0)
        def _():
            loss_ref[...] = jnp.zeros_like(loss_ref)
            dgt_ref[...] = jnp.zeros_like(dgt_ref)
            dg_ref[...] = jnp.zeros_like(dg_ref)

        ff = f_ref[...]
        gt, g = gt_ref[...], g_ref[...]
        r = lax.rsqrt(jnp.mean(ff * ff, axis=-1, keepdims=True) + EPS)
        fhat = ff * r
        nf = fhat * g
        err = (h_ref[...] + gt * nf) - t_ref[...]
        loss_ref[...] += jnp.sum(err * err)
        dy = err * (1.0 / D)
        dy_ref[...] = dy
        dgt_ref[...] += _colsum(dy * nf)
        dnf = dy * gt
        dg_ref[...] += _colsum(dnf * fhat)
        u = dnf * g
        dff_ref[...] = (r * (u - fhat * jnp.mean(u * fhat, axis=-1, keepdims=True))).astype(BF16)

    row = pl.BlockSpec((tm, D), lambda i: (i, 0))
    return pl.pallas_call(
        body, name="loss_bwd", grid=(T // tm,),
        out_shape=[jax.ShapeDtypeStruct((T, D), F32), jax.ShapeDtypeStruct((T, D), BF16), jax.ShapeDtypeStruct((8, 128), F32),
                   jax.ShapeDtypeStruct((1, D), F32), jax.ShapeDtypeStruct((1, D), F32)],
        in_specs=[row, row, row, _vec(D), _vec(D)],
        out_specs=[row, row, pl.BlockSpec((8, 128), lambda i: (0, 0)), _vec(D), _vec(D)],
        compiler_params=_cp("arbitrary"),
    )(ff, h1, tgt, gt2, g_post)


def ff2_bwd(dff, w_ff2, f1):
    T, D = dff.shape
    K = w_ff2.shape[0]
    tm, tn = min(1024, T), 2048

    def body(a_ref, w_ref, f_ref, o_ref):
        o_ref[...] = (_nt(a_ref[...], w_ref[...]) * (2.0 * jnp.maximum(f_ref[...].astype(F32), 0.0))).astype(BF16)

    return pl.pallas_call(
        body, name="ff2_bwd", out_shape=jax.ShapeDtypeStruct((T, K), BF16), grid=(K // tn, T // tm),
        in_specs=[pl.BlockSpec((tm, D), lambda j, i: (i, 0)), pl.BlockSpec((tn, D), lambda j, i: (j, 0)),
                  pl.BlockSpec((tm, tn), lambda j, i: (i, j))],
        out_specs=pl.BlockSpec((tm, tn), lambda j, i: (i, j)), compiler_params=_cp("parallel", "parallel"),
    )(dff, w_ff2, f1)


def ffn_norm_bwd(dy, da2, h1, mo, g_pre2, sc2, gt1, g_post, comm):
    T, D = dy.shape
    tm = min(256, T)

    def body(dy_ref, da_ref, h_ref, mo_ref, g2_ref, sc_ref, gt_ref, gp_ref, dh_ref, dmo_ref, s_sh, s_sc, s_g2, s_gt, s_gp):
        @pl.when(pl.program_id(0) == 0)
        def _():
            for s in (s_sh, s_sc, s_g2, s_gt, s_gp):
                s[...] = jnp.zeros_like(s)

        h1, da = h_ref[...], da_ref[...]
        g2, sc = g2_ref[...], sc_ref[...]
        r2 = lax.rsqrt(jnp.mean(h1 * h1, axis=-1, keepdims=True) + EPS)
        n2 = h1 * r2
        s_sh[...] += _colsum(da)
        s_sc[...] += _colsum(da * (n2 * g2))
        s_g2[...] += _colsum(da * (1.0 + sc) * n2)
        dn2 = da * g2 * (1.0 + sc)
        dh1 = dy_ref[...] + r2 * (dn2 - n2 * jnp.mean(dn2 * n2, axis=-1, keepdims=True))
        dh_ref[...] = dh1
        mo = mo_ref[...]
        gt, gp = gt_ref[...], gp_ref[...]
        r = lax.rsqrt(jnp.mean(mo * mo, axis=-1, keepdims=True) + EPS)
        mhat = mo * r
        s_gt[...] += _colsum(dh1 * (mhat * gp))
        dnm = dh1 * gt
        s_gp[...] += _colsum(dnm * mhat)
        u = dnm * gp
        dmo_ref[...] = (r * (u - mhat * jnp.mean(u * mhat, axis=-1, keepdims=True))).astype(BF16)

    row = pl.BlockSpec((tm, D), lambda i: (i, 0))
    vec_out = jax.ShapeDtypeStruct((1, D), F32)
    return _pallas(
        body, name="ffn_norm_bwd", grid=(T // tm,),
        out_shape=[jax.ShapeDtypeStruct((T, D), F32), jax.ShapeDtypeStruct((T, D), BF16)] + [vec_out] * 5,
        in_specs=[row, row, row, row] + [_vec(D)] * 4, out_specs=[row, row] + [_vec(D)] * 5,
        scratch=[], semantics=("arbitrary",), operands=(dy, da2, h1, mo, g_pre2, sc2, gt1, g_post), comm=comm)


def out_proj_bwd(dmo, w_o, y_a, y_b, proj):
    T, D = dmo.shape
    tm, tn = min(512, T), 512
    gpb = 1024 // tn

    def body(a_ref, w_ref, ya_ref, yb_ref, ga_ref, gb_ref, dya_ref, dyb_ref, dga_ref, dgb_ref):
        dm = _nt(a_ref[...], w_ref[...])
        sa, sb = jax.nn.sigmoid(ga_ref[...]), jax.nn.sigmoid(gb_ref[...])
        dya_ref[...] = (dm * sa).astype(BF16)
        dyb_ref[...] = (dm * sb).astype(BF16)
        dga_ref[...] = (dm * ya_ref[...].astype(F32) * sa * (1.0 - sa)).astype(BF16)
        dgb_ref[...] = (dm * yb_ref[...].astype(F32) * sb * (1.0 - sb)).astype(BF16)

    out = pl.BlockSpec((tm, tn), lambda i, j: (i, j))
    return pl.pallas_call(
        body, name="out_proj_bwd", grid=(T // tm, D // tn), out_shape=[jax.ShapeDtypeStruct((T, D), BF16)] * 4,
        in_specs=[pl.BlockSpec((tm, D), lambda i, j: (i, 0)), pl.BlockSpec((tn, D), lambda i, j: (j, 0)), out, out,
                  pl.BlockSpec((tm, tn), lambda i, j: (i, COL_GA * gpb + j)), pl.BlockSpec((tm, tn), lambda i, j: (i, COL_GB * gpb + j))],
        out_specs=[out] * 4, compiler_params=_cp("parallel", "parallel"),
    )(dmo, w_o, y_a, y_b, proj, proj)


def sgu_bwd(proj, dsgu, g_norm, w_spatial, b_spatial_t):
    T = proj.shape[0]
    W = 1024
    gw = W // SGU_GROUPS

    def body(u_ref, v_ref, ds_ref, g_ref, ws_ref, bst_ref, dz_ref, dw_ref, db_ref, dg_ref):
        @pl.when(pl.program_id(0) == 0)
        def _():
            dw_ref[...] = jnp.zeros_like(dw_ref)
            db_ref[...] = jnp.zeros_like(db_ref)
            dg_ref[...] = jnp.zeros_like(dg_ref)

        zu, rstd, dhat, vn, vm = _sgu_mix(u_ref, v_ref, g_ref, ws_ref, bst_ref)
        ds = ds_ref[...]
        du = ds * vm
        dvm = ds * zu
        dvm_b = dvm.astype(BF16)
        ones = jnp.ones((8, gw), F32)
        dvn = []
        for g in range(SGU_GROUPS):
            sl = slice(g * gw, (g + 1) * gw)
            dw_ref[g] += _nt(dvm_b[:, sl], vn[:, sl])
            db_ref[g] += lax.dot_general(ones, dvm[:, sl], (((1,), (1,)), ((), ())), precision=HI, preferred_element_type=F32)
            dvn.append(_tn(ws_ref[g].astype(BF16), dvm_b[:, sl]))
        dvn = jnp.concatenate(dvn, axis=1)
        dg_ref[...] += _colsum(dvn * dhat)
        ddh = dvn * g_ref[...]
        dzv = rstd * (ddh - jnp.mean(ddh, axis=-1, keepdims=True) - dhat * jnp.mean(ddh * dhat, axis=-1, keepdims=True))
        dz_ref[:, 0:W] = (du * _gelu_grad(u_ref[...])).astype(BF16)
        dz_ref[:, W:2 * W] = (dzv * _gelu_grad(v_ref[...])).astype(BF16)

    blk = lambda cb: pl.BlockSpec((SGU_CHUNK, W), lambda i: (i, cb))
    full3 = lambda a, b, c: pl.BlockSpec((a, b, c), lambda i: (0, 0, 0))
    return pl.pallas_call(
        body, name="sgu_bwd", grid=(T // SGU_CHUNK,),
        out_shape=[jax.ShapeDtypeStruct((T, 2 * W), BF16), jax.ShapeDtypeStruct((SGU_GROUPS, SGU_CHUNK, SGU_CHUNK), F32),
                   jax.ShapeDtypeStruct((SGU_GROUPS, 8, SGU_CHUNK), F32), jax.ShapeDtypeStruct((1, W), F32)],
        in_specs=[blk(COL_U), blk(COL_ZV), blk(0), _vec(W), full3(SGU_GROUPS, SGU_CHUNK, SGU_CHUNK),
                  pl.BlockSpec((SGU_CHUNK, SGU_GROUPS), lambda i: (0, 0))],
        out_specs=[pl.BlockSpec((SGU_CHUNK, 2 * W), lambda i: (i, 0)), full3(SGU_GROUPS, SGU_CHUNK, SGU_CHUNK),
                   full3(SGU_GROUPS, 8, SGU_CHUNK), _vec(W)],
        compiler_params=_cp("arbitrary"),
    )(proj, proj, dsgu, g_norm, w_spatial, b_spatial_t)


def hgrn_post_bwd(dya, o, proj, g_norm, comm):
    T, W = o.shape
    tm = min(256, T)

    def body(dy_ref, o_ref, og_ref, g_ref, do_ref, dog_ref, dg_ref):
        @pl.when(pl.program_id(0) == 0)
        def _():
            dg_ref[...] = jnp.zeros_like(dg_ref)

        g = g_ref[...]
        dg = jnp.zeros((1, HEAD_DIM), F32)
        for h in range(N_HEADS):
            sl = slice(h * HEAD_DIM, (h + 1) * HEAD_DIM)
            x, og, dy = o_ref[:, sl], og_ref[:, sl], dy_ref[:, sl]
            r = lax.rsqrt(jnp.mean(x * x, axis=-1, keepdims=True) + EPS)
            xhat = x * r
            s = jax.nn.sigmoid(og)
            don = dy * (og * s)
            dog_ref[:, sl] = (dy * (xhat * g) * (s * (1.0 + og * (1.0 - s)))).astype(BF16)
            dg += _colsum(don * xhat)
            u = don * g
            do_ref[:, sl] = r * (u - xhat * jnp.mean(u * xhat, axis=-1, keepdims=True))
        dg_ref[...] += dg

    row = pl.BlockSpec((tm, W), lambda i: (i, 0))
    return _pallas(
        body, name="hgrn_post_bwd", grid=(T // tm,),
        out_shape=[jax.ShapeDtypeStruct((T, W), F32), jax.ShapeDtypeStruct((T, W), BF16), jax.ShapeDtypeStruct((1, HEAD_DIM), F32)],
        in_specs=[row, row, pl.BlockSpec((tm, W), lambda i: (i, COL_OG)), _vec(HEAD_DIM)],
        out_specs=[row, row, _vec(HEAD_DIM)], scratch=[], semantics=("arbitrary",), operands=(dya, o, proj, g_norm), comm=comm)


def hgrn_bwd(proj, do, lb_logits, comm=None):
    T = proj.shape[0]
    NC, CPB = T // HGRN_CHUNK, HGRN_BLOCK // HGRN_CHUNK
    blk1 = min(HGRN_BLOCK_FWD, T)
    W = N_HEADS * HEAD_DIM
    col, f_spec, l_spec = _hgrn_specs(T)

    def body(l_ref, q_ref, f_ref, v_ref, do_ref, dq_ref, dv_ref, dlg_ref, dlb_ref, st_ref, dst_ref, dec_ref, ddec_ref, dqa_ref, dva_ref):
        d = pl.program_id(1)
        lb = _hgrn_lower_bound(l_ref)
        oml = 1.0 - lb
        mask = _hgrn_chunk_mask(d, HGRN_BLOCK)

        def values(rows):
            s, sn, fg, lf, k = _hgrn_gate(f_ref[rows, :], lb)
            b = _chunk_cumsum(lf, d == 1)
            bl = _chunk_total(lf)
            eb, enb, ee = jnp.exp(b), jnp.exp(-b), jnp.exp(bl - b)
            qd = q_ref[rows, :] * Q_SCALE * eb
            return s, sn, fg, k, bl, eb, enb, ee, qd, k * enb, k * ee

        def block1(i, carry):
            rows = pl.ds(pl.multiple_of(i * blk1, blk1), blk1)
            _, _, _, _, bl, _, _, _, qd, _, ke = values(rows)
            qd, ke = qd.astype(BF16), ke.astype(BF16)
            vb, dob = v_ref[rows, :].astype(BF16), do_ref[rows, :].astype(BF16)
            dec = jnp.exp(bl)
            for cc in range(blk1 // HGRN_CHUNK):
                sl = slice(cc * HGRN_CHUNK, (cc + 1) * HGRN_CHUNK)
                n = i * (blk1 // HGRN_CHUNK) + cc
                st_ref[n] = _tn(vb[sl], ke[sl])
                dst_ref[n] = _tn(dob[sl], qd[sl])
                dec_ref[n] = dec[cc * HGRN_CHUNK:cc * HGRN_CHUNK + 8, :]
            return carry

        _block_loop(T, blk1, block1, 0)

        def scan(t, s):
            n = jnp.where(d == 0, t, NC - 1 - t)
            u = st_ref[n]
            st_ref[n] = s
            return dec_ref[n][0:1, :] * s + u

        lax.fori_loop(0, NC, scan, jnp.zeros((HEAD_DIM, HEAD_DIM), F32))

        def rscan(t, ds):
            n = jnp.where(d == 0, NC - 1 - t, t)
            w = dst_ref[n]
            dst_ref[n] = ds
            ddec_ref[n] = jnp.broadcast_to(_colsum(ds * st_ref[n]), (8, HEAD_DIM))
            return dec_ref[n][0:1, :] * ds + w

        lax.fori_loop(0, NC, rscan, jnp.zeros((HEAD_DIM, HEAD_DIM), F32))

        def block3(i, dlb):
            rows = pl.ds(pl.multiple_of(i * HGRN_BLOCK, HGRN_BLOCK), HGRN_BLOCK)
            s, sn, fg, k, bl, eb, enb, ee, qd, kd, ke = values(rows)
            qdb, kdb, keb = qd.astype(BF16), kd.astype(BF16), ke.astype(BF16)
            vb, dob = v_ref[rows, :].astype(BF16), do_ref[rows, :].astype(BF16)
            att = jnp.where(mask, _nt(qdb, kdb), 0.0).astype(BF16)
            datt = jnp.where(mask, _nt(dob, vb), 0.0).astype(BF16)
            dv = _tn(att, dob)
            dqd = _nn(datt, kdb)
            dkd = _tn(datt, qdb)
            dv_i, dqd_i, dke, ddl = [], [], [], []
            for cc in range(CPB):
                sl = slice(cc * HGRN_CHUNK, (cc + 1) * HGRN_CHUNK)
                n = i * CPB + cc
                st_b, dst_b = st_ref[n].astype(BF16), dst_ref[n].astype(BF16)
                dv_i.append(_nt(keb[sl], dst_b))
                dqd_i.append(_nn(dob[sl], st_b))
                dke.append(_nn(vb[sl], dst_b))
                ddl.append(jnp.broadcast_to(ddec_ref[n][0:1, :] * dec_ref[n][0:1, :], (HGRN_CHUNK, HEAD_DIM)))
            dv = dv + jnp.concatenate(dv_i, axis=0)
            dqd = dqd + jnp.concatenate(dqd_i, axis=0)
            dke = jnp.concatenate(dke, axis=0)
            dq = dqd * eb * Q_SCALE
            dk = dkd * enb + dke * ee
            t_end = dke * ke
            db = dqd * qd - dkd * kd - t_end
            dlf = _chunk_cumsum(db, d == 0) + _chunk_total(t_end) + jnp.concatenate(ddl, axis=0)
            e = dlf / fg - dk
            dlg_ref[rows, :] = (oml * e * s * sn).astype(BF16)

            dq = jnp.where(d == 0, 0.0, dqa_ref[rows, :]) + dq
            dv = jnp.where(d == 0, 0.0, dva_ref[rows, :]) + dv
            dqa_ref[rows, :] = dq
            dva_ref[rows, :] = dv
            dq_ref[rows, :] = dq.astype(BF16)
            dv_ref[rows, :] = dv.astype(BF16)

            return dlb + _colsum(e * sn)

        dlb_ref[...] = _block_loop(T, HGRN_BLOCK, block3, jnp.zeros((1, HEAD_DIM), F32))

    head = pl.BlockSpec((T, HEAD_DIM), lambda h, d: (0, h))
    big = pltpu.VMEM((NC, HEAD_DIM, HEAD_DIM), F32)
    small = pltpu.VMEM((NC, 8, HEAD_DIM), F32)
    acc = pltpu.VMEM((T, HEAD_DIM), F32)
    outs, landed = _pallas(
        body, name="hgrn_bwd", grid=(N_HEADS, 2),
        out_shape=[jax.ShapeDtypeStruct((T, W), BF16), jax.ShapeDtypeStruct((T, W), BF16), jax.ShapeDtypeStruct((T, 2 * W), BF16),
                   jax.ShapeDtypeStruct((2, 1, W), F32)],
        in_specs=[l_spec, col(COL_Q), f_spec, col(COL_V), head],
        out_specs=[head, head, pl.BlockSpec((T, HEAD_DIM), lambda h, d: (0, N_HEADS * d + h)),
                   pl.BlockSpec((None, 1, HEAD_DIM), lambda h, d: (d, 0, h))],
        scratch=[big, big, small, small, acc, acc], semantics=("parallel", "arbitrary"), operands=(lb_logits, proj, proj, proj, do), comm=comm)
    return outs if comm is None else (outs, landed)


def mix_norm_bwd(da1, h0, dh1, g_pre, sc1):
    T, D = h0.shape
    tm = min(256, T)

    def body(da_ref, h_ref, dh_ref, g_ref, sc_ref, gx_ref, s_sh, s_sc, s_g):
        @pl.when(pl.program_id(0) == 0)
        def _():
            for s in (s_sh, s_sc, s_g):
                s[...] = jnp.zeros_like(s)

        h, da = h_ref[...], da_ref[...]
        g, sc = g_ref[...], sc_ref[...]
        r = lax.rsqrt(jnp.mean(h * h, axis=-1, keepdims=True) + EPS)
        n = h * r
        s_sh[...] += _colsum(da)
        s_sc[...] += _colsum(da * (n * g))
        s_g[...] += _colsum(da * (1.0 + sc) * n)
        dn = da * g * (1.0 + sc)
        gx_ref[...] = dh_ref[...] + r * (dn - n * jnp.mean(dn * n, axis=-1, keepdims=True))

    row = pl.BlockSpec((tm, D), lambda i: (i, 0))
    return pl.pallas_call(
        body, name="mix_norm_bwd", grid=(T // tm,),
        out_shape=[jax.ShapeDtypeStruct((T, D), F32)] + [jax.ShapeDtypeStruct((1, D), F32)] * 3,
        in_specs=[row, row, row, _vec(D), _vec(D)], out_specs=[row] + [_vec(D)] * 3, compiler_params=_cp("arbitrary"),
    )(da1, h0, dh1, g_pre, sc1)


def adamw(w, g, m, v, name):
    R, C = w.shape
    tr = R if R * C * 4 <= (1 << 21) else max(8, ((1 << 21) // (C * 4)) // 8 * 8)
    while R % tr:
        tr -= 8

    def body(w_ref, g_ref, m_ref, v_ref, d_ref, m2_ref, v2_ref):
        d_ref[...], m2_ref[...], v2_ref[...] = _adamw(w_ref[...], g_ref[...], m_ref[...], v_ref[...])

    row = pl.BlockSpec((tr, C), lambda i: (i, 0))
    return pl.pallas_call(
        body, name=name, grid=(R // tr,), out_shape=[jax.ShapeDtypeStruct((R, C), F32)] * 3,
        in_specs=[row] * 4, out_specs=[row] * 3, compiler_params=_cp("parallel"),
    )(w, g, m, v)


def wada_update(c_all, dmod, w, m, v):
    D, N = w.shape
    tm, tn = 512, 1024

    def body(c_ref, dm_ref, w_ref, m_ref, v_ref, g_ref, d_ref, m2_ref, v2_ref):
        c = c_ref[...]
        g = lax.dot_general(c * jax.nn.sigmoid(c), dm_ref[...], (((0,), (0,)), ((), ())), precision=HI, preferred_element_type=F32)
        g_ref[...] = g
        d_ref[...], m2_ref[...], v2_ref[...] = _adamw(w_ref[...], g, m_ref[...], v_ref[...])

    blk = pl.BlockSpec((tm, tn), lambda i, j: (i, j))
    return pl.pallas_call(
        body, name="wada_update", grid=(D // tm, N // tn), out_shape=[jax.ShapeDtypeStruct((D, N), F32)] * 4,
        in_specs=[pl.BlockSpec((8, tm), lambda i, j: (0, i)), pl.BlockSpec((8, tn), lambda i, j: (0, j)), blk, blk, blk],
        out_specs=[blk] * 4, compiler_params=_cp("parallel", "parallel"),
    )(c_all, dmod, w, m, v)


def sum_devices(gathered, name):
    n, R, C = gathered.shape

    def body(g_ref, o_ref):
        s = g_ref[0]
        for i in range(1, n):
            s = s + g_ref[i]
        o_ref[...] = s

    return pl.pallas_call(body, name=name, out_shape=jax.ShapeDtypeStruct((R, C), F32), compiler_params=_cp())(gathered)


def lb_logits_grad(dlb, lb_logits):
    def body(d_ref, l_ref, o_ref):
        for d in range(2):
            l0, l1 = l_ref[d, 0:1, :], l_ref[d, 1:2, :]
            m = jnp.maximum(l0, l1)
            e0, e1 = jnp.exp(l0 - m), jnp.exp(l1 - m)
            p0, p1 = e0 / (e0 + e1), e1 / (e0 + e1)
            g = d_ref[d:d + 1, :]
            o_ref[d, 0:1, :] = p0 * (g - p0 * g)
            o_ref[d, 1:2, :] = -p1 * (p0 * g)

    return pl.pallas_call(body, name="lb_logits_grad", out_shape=jax.ShapeDtypeStruct(lb_logits.shape, F32), compiler_params=_cp())(dlb, lb_logits)


def add_halves(g, landed, core):
    nj, _, r, cc = g.shape
    tr = min(256, r)

    def body(core_ref, g_ref, l_ref, o_ref):
        o_ref[...] = (g_ref[...].astype(F32) + l_ref[...].astype(F32)).astype(BF16)

    return pl.pallas_call(
        body, name="add_halves_%dx%d" % (r, cc), out_shape=jax.ShapeDtypeStruct((nj, r, cc), BF16),
        grid_spec=pltpu.PrefetchScalarGridSpec(
            num_scalar_prefetch=1, grid=(nj, r // tr),
            in_specs=[pl.BlockSpec((None, None, tr, cc), lambda j, i, core_ref: (j, core_ref[0], i, 0)),
                      pl.BlockSpec((None, None, tr, cc), lambda j, i, core_ref: (j, 0, i, 0))],
            out_specs=pl.BlockSpec((None, tr, cc), lambda j, i, core_ref: (j, i, 0))),
        compiler_params=_cp("parallel", "parallel"),
    )(core, g, landed)


def sum_chips(parts, landed, chip):
    nj, r, cc = parts.shape
    tr = min(256, r)

    def body(chip_ref, p_ref, l_ref, o_ref):
        mine = p_ref[...].astype(F32)
        s = None
        for j in range(nj):
            t = jnp.where(chip_ref[0] == j, mine, l_ref[j].astype(F32))
            s = t if s is None else s + t
        o_ref[...] = s

    return pl.pallas_call(
        body, name="sum_chips_%dx%d" % (r, cc), out_shape=jax.ShapeDtypeStruct((r, cc), F32),
        grid_spec=pltpu.PrefetchScalarGridSpec(
            num_scalar_prefetch=1, grid=(r // tr,),
            in_specs=[pl.BlockSpec((None, tr, cc), lambda i, chip_ref: (chip_ref[0], i, 0)),
                      pl.BlockSpec((nj, tr, cc), lambda i, chip_ref: (0, i, 0))],
            out_specs=pl.BlockSpec((tr, cc), lambda i, chip_ref: (i, 0))),
        compiler_params=_cp("parallel"),
    )(chip, parts, landed)


def adamw_halves(w, own, other, m, v, core, name):
    r, cc = own.shape
    tr = min(128, r)
    nb = r // tr

    def body(core_ref, w_ref, a_ref, b_ref, m_ref, v_ref, g_ref, d_ref, m2_ref, v2_ref):
        g = jnp.where(pl.program_id(0) == core_ref[0], a_ref[...], b_ref[...])
        g_ref[...] = g
        d_ref[...], m2_ref[...], v2_ref[...] = _adamw(w_ref[...], g, m_ref[...], v_ref[...])

    full = pl.BlockSpec((tr, cc), lambda h, i, core_ref: (h * nb + i, 0))
    mine = pl.BlockSpec((tr, cc), lambda h, i, core_ref: (jnp.where(h == core_ref[0], i, 0), 0))
    theirs = pl.BlockSpec((tr, cc), lambda h, i, core_ref: (jnp.where(h == core_ref[0], 0, i), 0))
    return pl.pallas_call(
        body, name=name, out_shape=[jax.ShapeDtypeStruct((2 * r, cc), F32)] * 4,
        grid_spec=pltpu.PrefetchScalarGridSpec(
            num_scalar_prefetch=1, grid=(2, nb), in_specs=[full, mine, theirs, full, full], out_specs=[full] * 4),
        compiler_params=_cp("arbitrary", "arbitrary"),
    )(core, w, own, other, m, v)


def _place():
    mx, my, mc = lax.axis_index("x"), lax.axis_index("y"), lax.axis_index("c")
    chips = [(1 - mx, my), (mx, 1 - my), (1 - mx, 1 - my)]
    return mx, my, mc, chips


def all_gather_small(x, name):
    R, C = x.shape

    def body(x_ref, out_ref, send_sems, recv_sems, local_sem):
        mx, my, mc, _ = _place()
        me = 4 * mx + 2 * my + mc
        mine = pltpu.make_async_copy(x_ref, out_ref.at[me], local_sem)
        mine.start()

        def peer(k):
            px = 1 - mx if k & 4 else mx
            py = 1 - my if k & 2 else my
            pc = 1 - mc if k & 1 else mc
            return px, py, pc

        def copy(k, src, slot):
            return pltpu.make_async_remote_copy(src_ref=src, dst_ref=out_ref.at[slot], send_sem=send_sems.at[k - 1],
                                                recv_sem=recv_sems.at[k - 1], device_id=peer(k), device_id_type=MESH)

        sends = [copy(k, x_ref, me) for k in range(1, 8)]
        for cp in sends:
            cp.start()
        for k in range(1, 8):
            px, py, pc = peer(k)
            slot = 4 * px + 2 * py + pc
            copy(k, out_ref.at[slot], slot).wait_recv()
        for cp in sends:
            cp.wait_send()
        mine.wait()

    return pl.pallas_call(
        body, name=name, out_shape=jax.ShapeDtypeStruct((8, R, C), F32),
        in_specs=[pl.BlockSpec(memory_space=pltpu.VMEM)], out_specs=pl.BlockSpec(memory_space=pltpu.VMEM),
        scratch_shapes=[pltpu.SemaphoreType.DMA((7,)), pltpu.SemaphoreType.DMA((7,)), pltpu.SemaphoreType.DMA],
        compiler_params=_cp(),
    )(x)


def gather8_comm(x):
    def copies(x_ref, out_ref, send_sems, recv_sems):
        mx, my, mc, _ = _place()
        me = 4 * mx + 2 * my + mc

        def peer(k):
            return (1 - mx if k & 4 else mx, 1 - my if k & 2 else my, 1 - mc if k & 1 else mc)

        def copy(k, src, slot):
            return pltpu.make_async_remote_copy(src_ref=src, dst_ref=out_ref.at[slot], send_sem=send_sems.at[k - 1],
                                                recv_sem=recv_sems.at[k - 1], device_id=peer(k), device_id_type=MESH)

        sends = [copy(k, x_ref, me) for k in range(1, 8)]
        arrivals = []
        for k in range(1, 8):
            px, py, pc = peer(k)
            slot = 4 * px + 2 * py + pc
            arrivals.append(copy(k, out_ref.at[slot], slot))
        return sends, arrivals, pltpu.make_async_copy(x_ref, out_ref.at[me], send_sems.at[7])

    def start(cin, cout, send_sems, recv_sems):
        sends, _, mine = copies(cin[0], cout[0], send_sems, recv_sems)
        mine.start()
        for cp in sends:
            cp.start()

    def finish(cin, cout, send_sems, recv_sems):
        sends, arrivals, mine = copies(cin[0], cout[0], send_sems, recv_sems)
        for cp in arrivals:
            cp.wait_recv()
        for cp in sends:
            cp.wait_send()
        mine.wait()

    return _Comm([x], [jax.ShapeDtypeStruct((8,) + x.shape, F32)], {}, 8, start, finish)


def _join(a, b):
    na_in, na_out = len(a.operands), len(a.out_shape)

    def split(fn_a, fn_b):
        def both(cin, cout, send_sems, recv_sems):
            fn_a(cin[:na_in], cout[:na_out], send_sems.at[pl.ds(0, a.n_sems)], recv_sems.at[pl.ds(0, a.n_sems)])
            fn_b(cin[na_in:], cout[na_out:], send_sems.at[pl.ds(a.n_sems, b.n_sems)], recv_sems.at[pl.ds(a.n_sems, b.n_sems)])
        return both

    aliases = dict(a.aliases)
    aliases.update({na_in + i: na_out + o for i, o in b.aliases.items()})
    return _Comm(a.operands + b.operands, a.out_shape + b.out_shape, aliases, a.n_sems + b.n_sems, split(a.start, b.start), split(a.finish, b.finish))


def _region(ref, kind, j, half, r, cc):
    nr = r if half is None else r // 2
    off = 0 if half is None else half * nr
    if kind == "col":
        return ref.at[pl.ds(off, nr), pl.ds(pl.multiple_of(j * cc, 128), cc)]
    return ref.at[pl.ds(pl.multiple_of(j * r + off, 16), nr), :]


def comm_call(comm, name):
    ni, no = len(comm.operands), len(comm.out_shape)

    def body(*refs):
        comm.start(refs[:ni], refs[ni:ni + no], *refs[ni + no:])
        comm.finish(refs[:ni], refs[ni:ni + no], *refs[ni + no:])

    return pl.pallas_call(
        body, name=name, out_shape=comm.out_shape, in_specs=[ANY] * ni, out_specs=[ANY] * no, input_output_aliases=comm.aliases,
        scratch_shapes=[pltpu.SemaphoreType.DMA((comm.n_sems,)), pltpu.SemaphoreType.DMA((comm.n_sems,))], compiler_params=_cp(),
    )(*comm.operands)


def gather_comm(fulls, kinds, dims):
    n = len(fulls)

    def copies(f_refs, send_sems, recv_sems):
        mx, my, mc, chips = _place()
        jme = 2 * mx + my

        def landed(w, k, half):
            px, py = chips[k]
            return _region(f_refs[w], kinds[w], 2 * px + py, half, *dims[w])

        def over_ici(w, k, reg):
            px, py = chips[k]
            return pltpu.make_async_remote_copy(src_ref=reg, dst_ref=reg, send_sem=send_sems.at[6 * w + k], recv_sem=recv_sems.at[6 * w + k],
                                                device_id=(px, py, mc), device_id_type=MESH)

        def over_d2d(w, k, half):
            reg = landed(w, k, half)
            return pltpu.make_async_remote_copy(src_ref=reg, dst_ref=reg, send_sem=send_sems.at[6 * w + 3 + k],
                                                recv_sem=recv_sems.at[6 * w + 3 + k], device_id=(mx, my, 1 - mc), device_id_type=MESH)

        sends = [over_ici(w, k, _region(f_refs[w], kinds[w], jme, mc, *dims[w])) for w in range(n) for k in range(3)]
        return mc, landed, over_ici, over_d2d, sends

    def start(cin, f_refs, send_sems, recv_sems):
        for cp in copies(f_refs, send_sems, recv_sems)[4]:
            cp.start()

    def finish(cin, f_refs, send_sems, recv_sems):
        mc, landed, over_ici, over_d2d, sends = copies(f_refs, send_sems, recv_sems)
        passed = []
        for w in range(n):
            for k in range(3):
                over_ici(w, k, landed(w, k, mc)).wait_recv()
                cp = over_d2d(w, k, mc)
                cp.start()
                passed.append(cp)
        for w in range(n):
            for k in range(3):
                over_d2d(w, k, 1 - mc).wait_recv()
        for cp in sends + passed:
            cp.wait_send()

    return _Comm(fulls, [jax.ShapeDtypeStruct(f.shape, BF16) for f in fulls], {w: w for w in range(n)}, 6 * n, start, finish)


def exchange_comm(grads):
    n = len(grads)

    def copies(g_refs, l_refs, send_sems, recv_sems):
        mx, my, mc, _ = _place()
        return [pltpu.make_async_remote_copy(src_ref=g_refs[w].at[:, pl.ds(1 - mc, 1)], dst_ref=l_refs[w], send_sem=send_sems.at[w],
                                             recv_sem=recv_sems.at[w], device_id=(mx, my, 1 - mc), device_id_type=MESH) for w in range(n)]

    def start(*refs):
        for cp in copies(*refs):
            cp.start()

    def finish(*refs):
        for cp in copies(*refs):
            cp.wait()

    return _Comm(grads, [jax.ShapeDtypeStruct((g.shape[0], 1) + g.shape[2:], BF16) for g in grads], {}, n, start, finish)


def exchange_halves(grads, name):
    return comm_call(exchange_comm(grads), name)


def scatter_comm(parts):
    n = len(parts)

    def sends(p_refs, l_refs, send_sems, recv_sems):
        mx, my, mc, chips = _place()
        return [pltpu.make_async_remote_copy(src_ref=p_refs[w].at[2 * px + py], dst_ref=l_refs[w].at[2 * mx + my],
                                             send_sem=send_sems.at[3 * w + k], recv_sem=recv_sems.at[3 * w + k],
                                             device_id=(px, py, mc), device_id_type=MESH) for w in range(n) for k, (px, py) in enumerate(chips)]

    def start(p_refs, l_refs, send_sems, recv_sems):
        for cp in sends(p_refs, l_refs, send_sems, recv_sems):
            cp.start()

    def finish(p_refs, l_refs, send_sems, recv_sems):
        mx, my, mc, chips = _place()
        for w in range(n):
            for k, (px, py) in enumerate(chips):
                slot = l_refs[w].at[2 * px + py]
                pltpu.make_async_remote_copy(src_ref=slot, dst_ref=slot, send_sem=send_sems.at[3 * w + k], recv_sem=recv_sems.at[3 * w + k],
                                             device_id=(px, py, mc), device_id_type=MESH).wait_recv()
        for cp in sends(p_refs, l_refs, send_sems, recv_sems):
            cp.wait_send()

    return _Comm(parts, [jax.ShapeDtypeStruct(p.shape, BF16) for p in parts], {}, 3 * n, start, finish)


def share_comm(sums):
    n = len(sums)

    def copies(q_refs, o_refs, send_sems, recv_sems):
        mx, my, mc, _ = _place()
        return [pltpu.make_async_remote_copy(src_ref=q_refs[w], dst_ref=o_refs[w], send_sem=send_sems.at[w], recv_sem=recv_sems.at[w],
                                             device_id=(mx, my, 1 - mc), device_id_type=MESH) for w in range(n)]

    def start(*refs):
        for cp in copies(*refs):
            cp.start()

    def finish(*refs):
        for cp in copies(*refs):
            cp.wait()

    return _Comm(sums, [jax.ShapeDtypeStruct(q.shape, F32) for q in sums], {}, n, start, finish)


def _pack(arrays):
    flat = jnp.concatenate([a.reshape(-1) for a in arrays])
    rows = -(-flat.shape[0] // 1024) * 8
    return jnp.pad(flat, (0, rows * 128 - flat.shape[0])).reshape(rows, 128)


def _unpack(packed, shapes):
    flat, out, off = packed.reshape(-1), [], 0
    for s in shapes:
        n = math.prod(s)
        out.append(flat[off:off + n].reshape(s))
        off += n
    return out


def kernel(x, c, w_ada, b_ada, g_pre_mix, g_post_mix, g_pre_ffn, g_post_ffn, w_in, lb_logits, g_hgrn_norm, w_a_out, g_sgu_norm, w_spatial, b_spatial, w_b_out, w_o, w_ff1, w_ff2, loss_target, m_w_ada, m_b_ada, m_g_pre_mix, m_g_post_mix, m_g_pre_ffn, m_g_post_ffn, m_w_in, m_lb_logits, m_g_hgrn_norm, m_w_a_out, m_g_sgu_norm, m_w_spatial, m_b_spatial, m_w_b_out, m_w_o, m_w_ff1, m_w_ff2, v_w_ada, v_b_ada, v_g_pre_mix, v_g_post_mix, v_g_pre_ffn, v_g_post_ffn, v_w_in, v_lb_logits, v_g_hgrn_norm, v_w_a_out, v_g_sgu_norm, v_w_spatial, v_b_spatial, v_w_b_out, v_w_o, v_w_ff1, v_w_ff2):
    mx, my, mc = lax.axis_index("x"), lax.axis_index("y"), lax.axis_index("c")
    chip, me = 2 * mx + my, 4 * mx + 2 * my + mc
    D = D_MODEL
    h0, tgt = x[0], loss_target[0]
    n_ada = w_ada.shape[2]
    n_lb = lb_logits.shape[2]

    got = all_gather_small(_pack([c, lb_logits]), "gather_inputs")
    c_all = got[:, :D // 128, :].reshape(8, D)
    lb_full = got[0::2, D // 128:D // 128 + 4 * n_lb // 128, :].reshape(4, 2, 2, n_lb).transpose(1, 2, 0, 3).reshape(2, 2, 4 * n_lb)
    b_ada_chip = lax.dynamic_slice(b_ada, (0, chip * n_ada), (1, n_ada))
    mod_cols = mod_matmul(c_all, w_ada[0], b_ada_chip)
    got = all_gather_small(mod_cols.reshape(-1, 128), "gather_mod").reshape(4, 2, 8, n_ada)
    mod = lax.dynamic_index_in_dim(got[:, 0], me, axis=1, keepdims=False).reshape(6, 1, D)
    sh1, sc1, gt1, sh2, sc2, gt2 = (mod[i] for i in range(6))

    big = [("w_in", w_in, "col"), ("w_a_out", w_a_out, "col"), ("w_b_out", w_b_out, "col"), ("w_o", w_o, "row"),
           ("w_ff1", w_ff1, "col"), ("w_ff2", w_ff2, "row")]
    kinds = [k for _, _, k in big]
    chip_idx, core = chip.reshape(1).astype(jnp.int32), mc.reshape(1).astype(jnp.int32)
    fulls = [cast_into_full(w[0], kind, chip_idx, "cast_" + nm) for nm, w, kind in big]
    dims = [w.shape[1:] for _, w, _ in big]
    later = lambda lo, hi: gather_comm(fulls[lo:hi], kinds[lo:hi], dims[lo:hi])
    halves_summed = lambda grads, name: [add_halves(g, l, core) for g, l in zip(grads, exchange_halves(grads, name))]

    bst = b_spatial[0].T
    a1 = prenorm(h0, g_pre_mix, sc1, sh1)
    proj, w_in_f, (w_a_f, w_b_f, w_o_f) = in_proj_gathered(a1, fulls[0], chip_idx, dims[0], later(1, 4))
    o, (w_ff1_f,) = hgrn_fwd(proj, lb_full, comm=later(4, 5))
    ya_pre = hgrn_post_fwd(o, proj, g_hgrn_norm)
    sgu = sgu_fwd(proj, g_sgu_norm, w_spatial[0], bst)
    y_a, y_b, merged = merge_matmul(ya_pre, sgu, w_a_f, w_b_f, proj)
    mo, h1, a2 = out_proj(merged, w_o_f, h0, gt1, g_post_mix, g_pre_ffn, sc2, sh2)
    (f1, hid), (w_ff2_f,) = matmul(a2, w_ff1_f, mode="nn", out_dtype=BF16, tm=1024, tn=1024, tk=2048, name="ff1", relu2=True,
                                   comm=later(5, 6))
    ff = matmul(hid, w_ff2_f, mode="nn", out_dtype=F32, tm=1024, tn=1024, tk=2048, name="ff2")
    dy, dff, loss_parts, d_gt2, d_g_post_ffn = loss_bwd(ff, h1, tgt, gt2, g_post_ffn)

    df1 = ff2_bwd(dff, w_ff2_f, f1)
    gr_ff2 = matmul(hid, dff, mode="tn", out_dtype=BF16, tm=1024, tn=1024, tk=2048, name="dw_ff2")
    gr_ff2 = gr_ff2.reshape(4, 2, -1, D)
    da2, (landed_ff2,) = matmul(df1, w_ff1_f, mode="nt", out_dtype=F32, tm=1024, tn=1024, tk=2048, name="da2", comm=exchange_comm([gr_ff2]))
    gr_ff1 = matmul(a2, df1, mode="tn", out_dtype=BF16, tm=1024, tn=2048, tk=2048, name="dw_ff1", split=(4, 2))
    (dh1, dmo, d_sh2, d_sc2, d_g_pre_ffn, d_gt1, d_g_post_mix), (landed_ff1,) = ffn_norm_bwd(
        dy, da2, h1, mo, g_pre_ffn, sc2, gt1, g_post_mix, exchange_comm([gr_ff1]))
    parts_ff = [add_halves(gr_ff1, landed_ff1, core), add_halves(gr_ff2, landed_ff2, core)]
    dya, dyb, dga, dgb = out_proj_bwd(dmo, w_o_f, y_a, y_b, proj)
    gr_o = matmul(merged, dmo, mode="tn", out_dtype=BF16, tm=1024, tn=1024, tk=2048, name="dw_o")
    dsgu = matmul(dyb, w_b_f, mode="nt", out_dtype=F32, tm=512, tn=1024, tk=2048, name="dsgu")
    gr_b = matmul(sgu, dyb, mode="tn", out_dtype=BF16, tm=512, tn=512, tk=4096, name="dw_b_out", split=(4, 2))
    dz, d_w_spatial, d_b_spatial, d_g_sgu = sgu_bwd(proj, dsgu, g_sgu_norm, w_spatial[0], bst)
    dya_pre = matmul(dya, w_a_f, mode="nt", out_dtype=F32, tm=512, tn=1024, tk=2048, name="dya_pre")
    gr_a = matmul(ya_pre, dya, mode="tn", out_dtype=BF16, tm=512, tn=512, tk=4096, name="dw_a_out", split=(4, 2))
    gr_mix = [gr_a, gr_b, gr_o.reshape(4, 2, -1, D)]
    (do, dog, d_g_hgrn), landed_halves = hgrn_post_bwd(dya_pre, o, proj, g_hgrn_norm, exchange_comm(gr_mix))
    parts_mix = [add_halves(g, l, core) for g, l in zip(gr_mix, landed_halves)]
    chips_summed = lambda parts, landed: [sum_chips(p, l, chip_idx) for p, l in zip(parts, landed)]
    (dq, dv, dlg, d_lb), landed_ff = hgrn_bwd(proj, do, lb_full, comm=scatter_comm(parts_ff))
    own_ff = chips_summed(parts_ff, landed_ff)
    dproj = jnp.concatenate([dq, dlg, dv, dog, dz, dga, dgb], axis=1)
    early = _pack([d_g_sgu, d_w_spatial, d_b_spatial[:, 0, :]])
    gr_in, (*landed_mix, got_early) = matmul(a1, dproj, mode="tn", out_dtype=BF16, tm=1024, tn=2816, tk=1024, name="dw_in", split=(4, 2),
                                             comm=_join(scatter_comm(parts_mix), gather8_comm(early)))
    own_mix = chips_summed(parts_mix, landed_mix)
    parts_in = halves_summed([gr_in], "exchange_in")
    da1, (landed_in, *other_rest) = matmul(dproj, w_in_f, mode="nt", out_dtype=F32, tm=1024, tn=1024, tk=2816, name="da1",
                                           comm=_join(scatter_comm(parts_in), share_comm(own_mix + own_ff)))
    own_in = chips_summed(parts_in, [landed_in])
    other_in = comm_call(share_comm(own_in), "share_w_in")
    own, other = own_in + own_mix + own_ff, list(other_in) + other_rest
    grad_x, d_sh1, d_sc1, d_g_pre_mix = mix_norm_bwd(da1, h0, dh1, g_pre_mix, sc1)
    out = {}

    mine = _pack([d_sh1, d_sc1, d_gt1, d_sh2, d_sc2, d_gt2, d_g_pre_mix, d_g_post_mix, d_g_pre_ffn, d_g_post_ffn, d_g_hgrn, d_lb,
                  loss_parts[0:1, 0:1]])
    got = all_gather_small(mine, "gather_small_grads")
    g_b_ada, g_g1, g_g2, g_g3, g_g4, g_hg, g_lb, sq_err = _unpack(
        sum_devices(got, "sum_small_grads"), [(1, 6 * D), (1, D), (1, D), (1, D), (1, D), (1, HEAD_DIM), (2, 1024), ()])
    loss = 0.5 * sq_err / D
    g_sg, g_ws, g_bs = _unpack(sum_devices(got_early, "sum_sgu_grads"), [(1, 1024), w_spatial.shape, b_spatial.shape])
    g_lbl = lax.dynamic_slice(lb_logits_grad(g_lb, lb_full), (0, 0, chip * n_lb), (2, 2, n_lb))
    names = ["b_ada", "g_pre_mix", "g_post_mix", "g_pre_ffn", "g_post_ffn", "g_hgrn_norm", "g_sgu_norm", "w_spatial", "b_spatial", "lb_logits"]
    ws = [b_ada, g_pre_mix, g_post_mix, g_pre_ffn, g_post_ffn, g_hgrn_norm, g_sgu_norm, w_spatial, b_spatial, lb_logits]
    gs = [g_b_ada, g_g1, g_g2, g_g3, g_g4, g_hg, g_sg, g_ws, g_bs, g_lbl]
    ms = [m_b_ada, m_g_pre_mix, m_g_post_mix, m_g_pre_ffn, m_g_post_ffn, m_g_hgrn_norm, m_g_sgu_norm, m_w_spatial, m_b_spatial, m_lb_logits]
    vs = [v_b_ada, v_g_pre_mix, v_g_post_mix, v_g_pre_ffn, v_g_post_ffn, v_g_hgrn_norm, v_g_sgu_norm, v_w_spatial, v_b_spatial, v_lb_logits]
    shapes = [w.shape for w in ws]
    upd = adamw(_pack(ws), _pack(gs), _pack(ms), _pack(vs), "adamw_small")
    upd = [_unpack(u, shapes) for u in upd]
    for i, nm in enumerate(names):
        out[nm] = (gs[i], upd[0][i], upd[1][i], upd[2][i])

    dmod_all = got[:, :6 * D // 128, :].reshape(8, 6 * D)
    dmod_chip = lax.dynamic_slice(dmod_all, (0, chip * n_ada), (8, n_ada))
    out["w_ada"] = tuple(a[None] for a in wada_update(c_all, dmod_chip, w_ada[0], m_w_ada[0], v_w_ada[0]))
    for (nm, w, _), a, b, m, v in zip(big, own, other, (m_w_in, m_w_a_out, m_w_b_out, m_w_o, m_w_ff1, m_w_ff2),
                                      (v_w_in, v_w_a_out, v_w_b_out, v_w_o, v_w_ff1, v_w_ff2)):
        out[nm] = tuple(t[None] for t in adamw_halves(w[0], a, b, m[0], v[0], core, "adamw_" + nm))

    order = ["w_ada", "b_ada", "g_pre_mix", "g_post_mix", "g_pre_ffn", "g_post_ffn", "w_in", "lb_logits", "g_hgrn_norm", "w_a_out",
             "g_sgu_norm", "w_spatial", "b_spatial", "w_b_out", "w_o", "w_ff1", "w_ff2"]
    return (loss, grad_x[None], *[out[nm][0] for nm in order], *[out[nm][1] for nm in order], *[out[nm][2] for nm in order],
            *[out[nm][3] for nm in order])
```

```python
import functools
import math

import jax
import jax.numpy as jnp
from jax import lax
from jax.experimental import pallas as pl
from jax.experimental.pallas import tpu as pltpu

F32, BF16 = jnp.float32, jnp.bfloat16
HI = lax.Precision.HIGHEST
MESH = pl.DeviceIdType.MESH
ANY = pl.BlockSpec(memory_space=pl.ANY)

EPS = 1e-6
D_MODEL = 2048
N_HEADS = 8
HEAD_DIM = 128
HGRN_CHUNK = 32
HGRN_BLOCK = 256
HGRN_BLOCK_FWD = 512
SGU_CHUNK = 128
SGU_GROUPS = 8
Q_SCALE = HEAD_DIM ** -0.5
COL_Q, COL_FFW, COL_FBW, COL_V, COL_OG, COL_U, COL_ZV, COL_GA, COL_GB = 0, 1, 2, 3, 4, 5, 6, 7, 9
N_PROJ = 11264
VMEM_BYTES_V7X = 64 * 1024 * 1024
VMEM_LIMIT = VMEM_BYTES_V7X - 8 * 1024 * 1024

ADAM_LR, ADAM_B1, ADAM_B2, ADAM_EPS, ADAM_WD, ADAM_STEP = 0.001, 0.9, 0.999, 1e-08, 0.01, 10
ADAM_C1 = 1.0 - ADAM_B1 ** ADAM_STEP
ADAM_C2 = 1.0 - ADAM_B2 ** ADAM_STEP


def _cp(*sem):
    return pltpu.CompilerParams(dimension_semantics=sem if sem else None, vmem_limit_bytes=VMEM_LIMIT)


def _vec(d):
    return pl.BlockSpec((1, d), lambda *_: (0, 0))


def _colsum(x):
    return jnp.sum(x, axis=0, keepdims=True)


def _nt(a, b):
    return lax.dot_general(a, b, (((1,), (1,)), ((), ())), preferred_element_type=F32)


def _tn(a, b):
    return lax.dot_general(a, b, (((0,), (0,)), ((), ())), preferred_element_type=F32)


def _nn(a, b):
    return jnp.dot(a, b, preferred_element_type=F32)


def _adamw(w, g, m, v):
    m2 = ADAM_B1 * m + (1.0 - ADAM_B1) * g
    v2 = ADAM_B2 * v + (1.0 - ADAM_B2) * (g * g)
    delta = -ADAM_LR * ((m2 / ADAM_C1) / (jnp.sqrt(v2 / ADAM_C2) + ADAM_EPS) + ADAM_WD * w)
    return delta, m2, v2


class _Comm:
    def __init__(self, operands, out_shape, aliases, n_sems, start, finish):
        self.operands, self.out_shape, self.aliases, self.n_sems = list(operands), list(out_shape), dict(aliases), n_sems
        self.start, self.finish = start, finish


def _pallas(body, *, name, grid, in_specs, out_specs, out_shape, scratch, semantics, operands, comm=None):
    if comm is None:
        res = pl.pallas_call(body, name=name, grid=grid, in_specs=in_specs, out_specs=out_specs, out_shape=out_shape,
                             scratch_shapes=scratch, compiler_params=_cp(*semantics))(*operands)
        return res, []
    n_in, n_out, n_scr = len(in_specs), len(out_specs), len(scratch)
    nci, nco = len(comm.operands), len(comm.out_shape)

    def with_comm(*refs):
        ins, rest = refs[:n_in], refs[n_in:]
        cin, rest = rest[:nci], rest[nci:]
        outs, rest = rest[:n_out], rest[n_out:]
        cout, rest = rest[:nco], rest[nco:]
        scr, (send, recv) = rest[:n_scr], rest[n_scr:]
        ids = [pl.program_id(a) for a in range(len(grid))]
        first = functools.reduce(jnp.logical_and, [i == 0 for i in ids])
        last = functools.reduce(jnp.logical_and, [i == g - 1 for i, g in zip(ids, grid)])

        @pl.when(first)
        def _():
            comm.start(cin, cout, send, recv)

        body(*ins, *outs, *scr)

        @pl.when(last)
        def _():
            comm.finish(cin, cout, send, recv)

    res = pl.pallas_call(
        with_comm, name=name, grid=grid, in_specs=list(in_specs) + [ANY] * nci, out_specs=list(out_specs) + [ANY] * nco,
        out_shape=list(out_shape) + comm.out_shape, input_output_aliases={n_in + i: n_out + o for i, o in comm.aliases.items()},
        scratch_shapes=list(scratch) + [pltpu.SemaphoreType.DMA((comm.n_sems,)), pltpu.SemaphoreType.DMA((comm.n_sems,))],
        compiler_params=_cp(*["arbitrary"] * len(grid)),
    )(*operands, *comm.operands)
    return res[:n_out], res[n_out:]


def matmul(a, b, *, mode, out_dtype, tm, tn, tk, name, split=None, comm=None, relu2=False):
    if mode == "tn":
        (K, M), (_, N) = a.shape, b.shape
    elif mode == "nt":
        (M, K), (N, _) = a.shape, b.shape
    else:
        (M, K), (_, N) = a.shape, b.shape
    tm, tn, tk = min(tm, M), min(tn, N), min(tk, K)
    nk = K // tk
    a_spec = pl.BlockSpec((tk, tm), lambda i, j, k: (k, i)) if mode == "tn" else pl.BlockSpec((tm, tk), lambda i, j, k: (i, k))
    b_spec = pl.BlockSpec((tn, tk), lambda i, j, k: (j, k)) if mode == "nt" else pl.BlockSpec((tk, tn), lambda i, j, k: (k, j))
    dot = {"nn": _nn, "nt": _nt, "tn": _tn}[mode]
    if split is None:
        out_shape = jax.ShapeDtypeStruct((M, N), out_dtype)
        out_spec = pl.BlockSpec((tm, tn), lambda i, j, k: (i, j))
    else:
        nj, nh = split
        rows, cols = M // nh, N // nj
        tm, tn = min(tm, rows), min(tn, cols)
        bi, bj = rows // tm, cols // tn
        out_shape = jax.ShapeDtypeStruct((nj, nh, rows, cols), out_dtype)
        out_spec = pl.BlockSpec((None, None, tm, tn), lambda i, j, k: (j // bj, i // bi, i % bi, j % bj))

    def finish(y, o_ref, sq_ref):
        o_ref[...] = y.astype(o_ref.dtype)
        if relu2:
            p = jnp.maximum(y, 0.0)
            sq_ref[0][...] = (p * p).astype(BF16)

    if nk == 1:
        def body(a_ref, b_ref, o_ref, *sq_ref):
            finish(dot(a_ref[...], b_ref[...]), o_ref, sq_ref)
        scratch = []
    else:
        def body(a_ref, b_ref, o_ref, *rest):
            acc_ref, k = rest[-1], pl.program_id(2)

            @pl.when(k == 0)
            def _():
                acc_ref[...] = jnp.zeros_like(acc_ref)

            acc_ref[...] += dot(a_ref[...], b_ref[...])

            @pl.when(k == nk - 1)
            def _():
                finish(acc_ref[...], o_ref, rest[:-1])
        scratch = [pltpu.VMEM((tm, tn), F32)]

    out_specs, out_shapes = [out_spec], [out_shape]
    if relu2:
        out_specs, out_shapes = out_specs + [out_spec], out_shapes + [jax.ShapeDtypeStruct(out_shape.shape, BF16)]
    outs, landed = _pallas(
        body, name=name, grid=(M // tm, N // tn, nk), in_specs=[a_spec, b_spec], out_specs=out_specs, out_shape=out_shapes,
        scratch=scratch, semantics=("parallel", "parallel", "arbitrary"), operands=(a, b), comm=comm)
    out = tuple(outs) if relu2 else outs[0]
    return out if comm is None else (out, landed)


def cast_into_full(w, kind, chip, name):
    r, cc = w.shape
    tr = min(r, 512)
    nb = r // tr

    def body(chip_ref, w_ref, o_ref):
        o_ref[...] = w_ref[...].astype(BF16)

    if kind == "col":
        full, out_map = (r, 4 * cc), lambda i, chip_ref: (i, chip_ref[0])
    else:
        full, out_map = (4 * r, cc), lambda i, chip_ref: (chip_ref[0] * nb + i, 0)
    return pl.pallas_call(
        body, name=name, out_shape=jax.ShapeDtypeStruct(full, BF16),
        grid_spec=pltpu.PrefetchScalarGridSpec(
            num_scalar_prefetch=1, grid=(nb,), in_specs=[pl.BlockSpec((tr, cc), lambda i, chip_ref: (i, 0))],
            out_specs=pl.BlockSpec((tr, cc), out_map)),
        compiler_params=_cp("parallel"),
    )(chip, w)


def mod_matmul(c_all, w_ada, b_ada):
    D, N = w_ada.shape
    tn = 1024

    def body(c_ref, w_ref, b_ref, o_ref):
        c = c_ref[...]
        sc = c * jax.nn.sigmoid(c)
        o_ref[...] = jnp.dot(sc.astype(BF16), w_ref[...].astype(BF16), preferred_element_type=F32) + b_ref[...]

    return pl.pallas_call(
        body, name="mod_matmul", out_shape=jax.ShapeDtypeStruct((8, N), F32), grid=(N // tn,),
        in_specs=[pl.BlockSpec((8, D), lambda j: (0, 0)), pl.BlockSpec((D, tn), lambda j: (0, j)),
                  pl.BlockSpec((1, tn), lambda j: (0, j))],
        out_specs=pl.BlockSpec((8, tn), lambda j: (0, j)), compiler_params=_cp("parallel"),
    )(c_all, w_ada, b_ada)


def prenorm(h, g, sc, sh):
    T, D = h.shape
    tm = min(256, T)

    def body(h_ref, g_ref, sc_ref, sh_ref, a_ref):
        x = h_ref[...]
        r = lax.rsqrt(jnp.mean(x * x, axis=-1, keepdims=True) + EPS)
        a_ref[...] = ((x * r) * g_ref[...] * (1.0 + sc_ref[...]) + sh_ref[...]).astype(BF16)

    row = pl.BlockSpec((tm, D), lambda i: (i, 0))
    return pl.pallas_call(
        body, name="prenorm", out_shape=jax.ShapeDtypeStruct((T, D), BF16), grid=(T // tm,),
        in_specs=[row, _vec(D), _vec(D), _vec(D)], out_specs=row, compiler_params=_cp("parallel"),
    )(h, g, sc, sh)


def in_proj_gathered(a, w_full, chip, dims, tail):
    T, D = a.shape
    rows, cc = dims
    tm, tn = min(512, T), cc // 2
    ni = T // tm
    half = rows // 2

    nt = len(tail.operands)

    def body(chip_ref, a_ref, w_in_ref, *rest):
        tail_in, (y_ref, w_ref), rest = rest[:nt], rest[nt:nt + 2], rest[nt + 2:]
        tail_out, (wbuf, wsem, send_sems, recv_sems, tail_send, tail_recv) = rest[:nt], rest[nt:]
        q, j, i = pl.program_id(0), pl.program_id(1), pl.program_id(2)
        mx, my, mc, _ = _place()
        me = chip_ref[0]

        def tile(block, jj):
            src = w_ref.at[:, pl.ds(pl.multiple_of(block * cc + jj * tn, 128), tn)]
            return pltpu.make_async_copy(src, wbuf.at[jj], wsem.at[jj])

        def rows_half(block, hh):
            return w_ref.at[pl.ds(pl.multiple_of(hh * half, 16), half), pl.ds(pl.multiple_of(block * cc, 128), cc)]

        def over_ici(s, block):
            peer = (1 - mx if s & 2 else mx, 1 - my if s & 1 else my, mc)
            reg = rows_half(block, mc)
            return pltpu.make_async_remote_copy(src_ref=reg, dst_ref=reg, send_sem=send_sems.at[s - 1], recv_sem=recv_sems.at[s - 1],
                                                device_id=peer, device_id_type=MESH)

        def over_d2d(s, block, hh):
            reg = rows_half(block, hh)
            return pltpu.make_async_remote_copy(src_ref=reg, dst_ref=reg, send_sem=send_sems.at[2 + s], recv_sem=recv_sems.at[2 + s],
                                                device_id=(mx, my, 1 - mc), device_id_type=MESH)

        def passed_on(s, block):
            k = s - 1
            reg = w_ref.at[pl.ds(pl.multiple_of(mc * half + k * (half // 2), 16), half // 2), pl.ds(pl.multiple_of(block * cc, 128), cc)]
            peer = (1 - mx, my, mc) if s == 1 else (mx, 1 - my, mc)
            return pltpu.make_async_remote_copy(src_ref=reg, dst_ref=reg, send_sem=send_sems.at[6 + k], recv_sem=recv_sems.at[6 + k],
                                                device_id=peer, device_id_type=MESH)

        @pl.when((q == 0) & (j == 0) & (i == 0))
        def _():
            for s in (1, 2):
                over_ici(s, me).start()
            tile(me, 0).start()

        @pl.when(i == 0)
        def _():
            tile(me ^ q, j).wait()

        @pl.when((i == 0) & (j == 0))
        def _():
            tile(me ^ q, 1).start()

        y_ref[...] = _nn(a_ref[...], wbuf[j])

        @pl.when((q == 0) & (j == 1) & (i == ni - 1))
        def _():
            for s in (1, 2):
                over_ici(s, me ^ s).wait_recv()
                passed_on(s, me ^ s).start()
                over_d2d(s, me ^ s, mc).start()
            tail.start(tail_in, tail_out, tail_send, tail_recv)
            over_d2d(1, me ^ 1, 1 - mc).wait_recv()
            tile(me ^ 1, 0).start()

        @pl.when((q == 1) & (j == 1) & (i == ni - 1))
        def _():
            over_d2d(2, me ^ 2, 1 - mc).wait_recv()
            tile(me ^ 2, 0).start()

        @pl.when((q == 2) & (j == 1) & (i == ni - 1))
        def _():
            for s in (1, 2):
                passed_on(s, me ^ 3).wait_recv()
            over_d2d(3, me ^ 3, mc).start()
            over_d2d(3, me ^ 3, 1 - mc).wait_recv()
            tile(me ^ 3, 0).start()

        @pl.when((q == 3) & (j == 1) & (i == ni - 1))
        def _():
            for s in (1, 2):
                over_ici(s, me).wait_send()
                passed_on(s, me ^ s).wait_send()
            for s in (1, 2, 3):
                over_d2d(s, me ^ s, mc).wait_send()
            tail.finish(tail_in, tail_out, tail_send, tail_recv)

    dma = pltpu.SemaphoreType.DMA
    y, w_out, *tail_res = pl.pallas_call(
        body, name="in_proj", out_shape=[jax.ShapeDtypeStruct((T, 4 * cc), F32), jax.ShapeDtypeStruct(w_full.shape, BF16)] + tail.out_shape,
        grid_spec=pltpu.PrefetchScalarGridSpec(
            num_scalar_prefetch=1, grid=(4, 2, ni),
            in_specs=[pl.BlockSpec((tm, D), lambda q, j, i, chip_ref: (i, 0)), ANY] + [ANY] * nt,
            out_specs=[pl.BlockSpec((tm, tn), lambda q, j, i, chip_ref: (i, (chip_ref[0] ^ q) * 2 + j)), ANY] + [ANY] * nt,
            scratch_shapes=[pltpu.VMEM((2, D, tn), BF16), dma((2,)), dma((8,)), dma((8,)), dma((tail.n_sems,)), dma((tail.n_sems,))]),
        input_output_aliases={2: 1, **{3 + i: 2 + o for i, o in tail.aliases.items()}},
        compiler_params=_cp("arbitrary", "arbitrary", "arbitrary"),
    )(chip, a, w_full, *tail.operands)
    return y, w_out, tail_res


def _hgrn_lower_bound(l_ref):
    l0, l1 = l_ref[0:1, :], l_ref[1:2, :]
    m = jnp.maximum(l0, l1)
    e0, e1 = jnp.exp(l0 - m), jnp.exp(l1 - m)
    return e0 / (e0 + e1)


def _hgrn_chunk_mask(d, blk):
    r = lax.broadcasted_iota(jnp.int32, (blk, blk), 0)
    c = lax.broadcasted_iota(jnp.int32, (blk, blk), 1)
    same = (r // HGRN_CHUNK) == (c // HGRN_CHUNK)
    fwd = d == 0
    return same & (((c <= r) & fwd) | ((c >= r) & jnp.logical_not(fwd)))


def _chunk_total(x):
    x3 = x.reshape(x.shape[0] // HGRN_CHUNK, HGRN_CHUNK, x.shape[1])
    return jnp.broadcast_to(jnp.sum(x3, axis=1, keepdims=True), x3.shape).reshape(x.shape)


def _chunk_cumsum(x, suffix):
    pos = lax.broadcasted_iota(jnp.int32, x.shape, 0) % HGRN_CHUNK
    p, s = x, 1
    while s < HGRN_CHUNK:
        p = p + jnp.where(pos >= s, pltpu.roll(p, s, 0), 0.0)
        s *= 2
    return jnp.where(suffix, _chunk_total(x) - p + x, p)


def _block_loop(T, blk, body, init):
    n = T // blk
    return lax.fori_loop(0, n, body, init, unroll=2 if n % 2 == 0 else 1)


def _hgrn_gate(f, lb):
    s = jax.nn.sigmoid(f)
    sn = jax.nn.sigmoid(-f)
    fg = lb + (1.0 - lb) * s
    return s, sn, fg, jnp.log(fg), (1.0 - lb) * sn


def _hgrn_specs(T):
    col = lambda base: pl.BlockSpec((T, HEAD_DIM), lambda h, d: (0, base * N_HEADS + h))
    f_spec = pl.BlockSpec((T, HEAD_DIM), lambda h, d: (0, COL_FFW * N_HEADS + N_HEADS * d + h))
    l_spec = pl.BlockSpec((None, 2, HEAD_DIM), lambda h, d: (d, 0, h))
    return col, f_spec, l_spec


def hgrn_fwd(proj, lb_logits, comm=None):
    T = proj.shape[0]
    blk = min(HGRN_BLOCK_FWD, T)
    NC, CPB = T // HGRN_CHUNK, blk // HGRN_CHUNK
    col, f_spec, l_spec = _hgrn_specs(T)

    def body(l_ref, q_ref, f_ref, v_ref, o_ref, st_ref, dec_ref, qd_ref):
        d = pl.program_id(1)
        lb = _hgrn_lower_bound(l_ref)
        mask = _hgrn_chunk_mask(d, blk)

        def block(i, carry):
            rows = pl.ds(pl.multiple_of(i * blk, blk), blk)
            _, _, _, lf, k = _hgrn_gate(f_ref[rows, :], lb)
            b = _chunk_cumsum(lf, d == 1)
            bl = _chunk_total(lf)
            qd = (q_ref[rows, :] * Q_SCALE * jnp.exp(b)).astype(BF16)
            kd = (k * jnp.exp(-b)).astype(BF16)
            ke = (k * jnp.exp(bl - b)).astype(BF16)
            vb = v_ref[rows, :].astype(BF16)
            att = jnp.where(mask, _nt(qd, kd), 0.0).astype(BF16)
            o_ref[rows, :] = jnp.where(d == 0, 0.0, o_ref[rows, :]) + _nn(att, vb)
            qd_ref[rows, :] = qd
            dec = jnp.exp(bl)
            for cc in range(CPB):
                sl = slice(cc * HGRN_CHUNK, (cc + 1) * HGRN_CHUNK)
                n = i * CPB + cc
                st_ref[n] = _tn(vb[sl], ke[sl])
                dec_ref[n] = dec[cc * HGRN_CHUNK:cc * HGRN_CHUNK + 8, :]
            return carry

        _block_loop(T, blk, block, 0)

        def scan(t, s):
            n = jnp.where(d == 0, t, NC - 1 - t)
            u = st_ref[n]
            st_ref[n] = s
            return dec_ref[n][0:1, :] * s + u

        lax.fori_loop(0, NC, scan, jnp.zeros((HEAD_DIM, HEAD_DIM), F32))

        def inter(i, carry):
            rows = pl.ds(pl.multiple_of(i * blk, blk), blk)
            qd = qd_ref[rows, :]
            o_ref[rows, :] += jnp.concatenate(
                [_nt(qd[cc * HGRN_CHUNK:(cc + 1) * HGRN_CHUNK], st_ref[i * CPB + cc].astype(BF16)) for cc in range(CPB)], axis=0)
            return carry

        _block_loop(T, blk, inter, 0)

    (o,), landed = _pallas(
        body, name="hgrn_fwd", grid=(N_HEADS, 2), in_specs=[l_spec, col(COL_Q), f_spec, col(COL_V)],
        out_specs=[pl.BlockSpec((T, HEAD_DIM), lambda h, d: (0, h))], out_shape=[jax.ShapeDtypeStruct((T, N_HEADS * HEAD_DIM), F32)],
        scratch=[pltpu.VMEM((NC, HEAD_DIM, HEAD_DIM), F32), pltpu.VMEM((NC, 8, HEAD_DIM), F32), pltpu.VMEM((T, HEAD_DIM), BF16)],
        semantics=("parallel", "arbitrary"), operands=(lb_logits, proj, proj, proj), comm=comm)
    return o if comm is None else (o, landed)


def hgrn_post_fwd(o, proj, g_norm):
    T, W = o.shape
    tm = min(256, T)

    def body(o_ref, og_ref, g_ref, y_ref):
        g = g_ref[...]
        for h in range(N_HEADS):
            sl = slice(h * HEAD_DIM, (h + 1) * HEAD_DIM)
            x = o_ref[:, sl]
            r = lax.rsqrt(jnp.mean(x * x, axis=-1, keepdims=True) + EPS)
            og = og_ref[:, sl]
            y_ref[:, sl] = ((x * r) * g * (og * jax.nn.sigmoid(og))).astype(BF16)

    return pl.pallas_call(
        body, name="hgrn_post_fwd", out_shape=jax.ShapeDtypeStruct((T, W), BF16), grid=(T // tm,),
        in_specs=[pl.BlockSpec((tm, W), lambda i: (i, 0)), pl.BlockSpec((tm, W), lambda i: (i, COL_OG)), _vec(HEAD_DIM)],
        out_specs=pl.BlockSpec((tm, W), lambda i: (i, 0)), compiler_params=_cp("parallel"),
    )(o, proj, g_norm)


def _gelu(x):
    return 0.5 * x * (1.0 + lax.erf(x * (1.0 / math.sqrt(2.0))))


def _gelu_grad(x):
    return 0.5 * (1.0 + lax.erf(x * (1.0 / math.sqrt(2.0)))) + x * jnp.exp(-0.5 * x * x) * (1.0 / math.sqrt(2.0 * math.pi))


def _sgu_mix(u_ref, v_ref, g_ref, ws_ref, bst_ref):
    W = u_ref.shape[1]
    zu, zv = _gelu(u_ref[...]), _gelu(v_ref[...])
    dv = zv - jnp.mean(zv, axis=-1, keepdims=True)
    rstd = lax.rsqrt(jnp.mean(dv * dv, axis=-1, keepdims=True) + EPS)
    dhat = dv * rstd
    vn = (dhat * g_ref[...]).astype(BF16)
    gw = W // SGU_GROUPS
    vm = [_nn(ws_ref[g].astype(BF16), vn[:, g * gw:(g + 1) * gw]) + bst_ref[:, g:g + 1] for g in range(SGU_GROUPS)]
    return zu, rstd, dhat, vn, jnp.concatenate(vm, axis=1)


def sgu_fwd(proj, g_norm, w_spatial, b_spatial_t):
    T = proj.shape[0]
    W = 1024
    n_chunks = T // SGU_CHUNK

    def body(u_ref, v_ref, g_ref, ws_ref, bst_ref, y_ref):
        zu, _, _, _, vm = _sgu_mix(u_ref, v_ref, g_ref, ws_ref, bst_ref)
        y_ref[...] = (zu * vm).astype(BF16)

    blk = lambda cb: pl.BlockSpec((SGU_CHUNK, W), lambda i: (i, cb))
    return pl.pallas_call(
        body, name="sgu_fwd", out_shape=jax.ShapeDtypeStruct((T, W), BF16), grid=(n_chunks,),
        in_specs=[blk(COL_U), blk(COL_ZV), _vec(W), pl.BlockSpec((SGU_GROUPS, SGU_CHUNK, SGU_CHUNK), lambda i: (0, 0, 0)),
                  pl.BlockSpec((SGU_CHUNK, SGU_GROUPS), lambda i: (0, 0))],
        out_specs=blk(0), compiler_params=_cp("parallel"),
    )(proj, proj, g_norm, w_spatial, b_spatial_t)


def merge_matmul(ya_pre, sgu, w_a, w_b, proj):
    T, K = ya_pre.shape
    N = w_a.shape[1]
    tm, tn = min(512, T), 512
    gpb = 1024 // tn

    def body(a_ref, b_ref, wa_ref, wb_ref, ga_ref, gb_ref, ya_ref, yb_ref, m_ref):
        ya = _nn(a_ref[...], wa_ref[...])
        yb = _nn(b_ref[...], wb_ref[...])
        ya_ref[...] = ya.astype(BF16)
        yb_ref[...] = yb.astype(BF16)
        m_ref[...] = (jax.nn.sigmoid(ga_ref[...]) * ya + jax.nn.sigmoid(gb_ref[...]) * yb).astype(BF16)

    lhs = pl.BlockSpec((tm, K), lambda i, j: (i, 0))
    rhs = pl.BlockSpec((K, tn), lambda i, j: (0, j))
    out = pl.BlockSpec((tm, tn), lambda i, j: (i, j))
    return pl.pallas_call(
        body, name="merge_matmul", grid=(T // tm, N // tn),
        out_shape=[jax.ShapeDtypeStruct((T, N), BF16)] * 3,
        in_specs=[lhs, lhs, rhs, rhs, pl.BlockSpec((tm, tn), lambda i, j: (i, COL_GA * gpb + j)),
                  pl.BlockSpec((tm, tn), lambda i, j: (i, COL_GB * gpb + j))],
        out_specs=[out, out, out], compiler_params=_cp("parallel", "parallel"),
    )(ya_pre, sgu, w_a, w_b, proj, proj)


def out_proj(merged, w_o, h0, gt1, g_post, g_pre2, sc2, sh2):
    T, D = h0.shape
    tm = min(256, T)

    def body(m_ref, w_ref, h_ref, gt_ref, gp_ref, g2_ref, sc_ref, sh_ref, mo_ref, h1_ref, a2_ref):
        mo = _nn(m_ref[...], w_ref[...])
        mo_ref[...] = mo
        r = lax.rsqrt(jnp.mean(mo * mo, axis=-1, keepdims=True) + EPS)
        h1 = h_ref[...] + gt_ref[...] * ((mo * r) * gp_ref[...])
        h1_ref[...] = h1
        r2 = lax.rsqrt(jnp.mean(h1 * h1, axis=-1, keepdims=True) + EPS)
        a2_ref[...] = ((h1 * r2) * g2_ref[...] * (1.0 + sc_ref[...]) + sh_ref[...]).astype(BF16)

    row = pl.BlockSpec((tm, D), lambda i: (i, 0))
    return pl.pallas_call(
        body, name="out_proj", grid=(T // tm,),
        out_shape=[jax.ShapeDtypeStruct((T, D), F32), jax.ShapeDtypeStruct((T, D), F32), jax.ShapeDtypeStruct((T, D), BF16)],
        in_specs=[row, pl.BlockSpec((D, D), lambda i: (0, 0)), row] + [_vec(D)] * 5,
        out_specs=[row, row, row], compiler_params=_cp("parallel"),
    )(merged, w_o, h0, gt1, g_post, g_pre2, sc2, sh2)


def loss_bwd(ff, h1, tgt, gt2, g_post):
    T, D = ff.shape
    tm = min(256, T)

    def body(f_ref, h_ref, t_ref, gt_ref, g_ref, dy_ref, dff_ref, loss_ref, dgt_ref, dg_ref):
        @pl.when(pl.program_id(0) == 0)
        def _():
            loss_ref[...] = jnp.zeros_like(loss_ref)
            dgt_ref[...] = jnp.zeros_like(dgt_ref)
            dg_ref[...] = jnp.zeros_like(dg_ref)

        ff = f_ref[...]
        gt, g = gt_ref[...], g_ref[...]
        r = lax.rsqrt(jnp.mean(ff * ff, axis=-1, keepdims=True) + EPS)
        fhat = ff * r
        nf = fhat * g
        err = (h_ref[...] + gt * nf) - t_ref[...]
        loss_ref[...] += jnp.sum(err * err)
        dy = err * (1.0 / D)
        dy_ref[...] = dy
        dgt_ref[...] += _colsum(dy * nf)
        dnf = dy * gt
        dg_ref[...] += _colsum(dnf * fhat)
        u = dnf * g
        dff_ref[...] = (r * (u - fhat * jnp.mean(u * fhat, axis=-1, keepdims=True))).astype(BF16)

    row = pl.BlockSpec((tm, D), lambda i: (i, 0))
    return pl.pallas_call(
        body, name="loss_bwd", grid=(T // tm,),
        out_shape=[jax.ShapeDtypeStruct((T, D), F32), jax.ShapeDtypeStruct((T, D), BF16), jax.ShapeDtypeStruct((8, 128), F32),
                   jax.ShapeDtypeStruct((1, D), F32), jax.ShapeDtypeStruct((1, D), F32)],
        in_specs=[row, row, row, _vec(D), _vec(D)],
        out_specs=[row, row, pl.BlockSpec((8, 128), lambda i: (0, 0)), _vec(D), _vec(D)],
        compiler_params=_cp("arbitrary"),
    )(ff, h1, tgt, gt2, g_post)


def ff2_bwd(dff, w_ff2, f1):
    T, D = dff.shape
    K = w_ff2.shape[0]
    tm, tn = min(1024, T), 2048

    def body(a_ref, w_ref, f_ref, o_ref):
        o_ref[...] = (_nt(a_ref[...], w_ref[...]) * (2.0 * jnp.maximum(f_ref[...].astype(F32), 0.0))).astype(BF16)

    return pl.pallas_call(
        body, name="ff2_bwd", out_shape=jax.ShapeDtypeStruct((T, K), BF16), grid=(K // tn, T // tm),
        in_specs=[pl.BlockSpec((tm, D), lambda j, i: (i, 0)), pl.BlockSpec((tn, D), lambda j, i: (j, 0)),
                  pl.BlockSpec((tm, tn), lambda j, i: (i, j))],
        out_specs=pl.BlockSpec((tm, tn), lambda j, i: (i, j)), compiler_params=_cp("parallel", "parallel"),
    )(dff, w_ff2, f1)


def ffn_norm_bwd(dy, da2, h1, mo, g_pre2, sc2, gt1, g_post, comm):
    T, D = dy.shape
    tm = min(256, T)

    def body(dy_ref, da_ref, h_ref, mo_ref, g2_ref, sc_ref, gt_ref, gp_ref, dh_ref, dmo_ref, s_sh, s_sc, s_g2, s_gt, s_gp):
        @pl.when(pl.program_id(0) == 0)
        def _():
            for s in (s_sh, s_sc, s_g2, s_gt, s_gp):
                s[...] = jnp.zeros_like(s)

        h1, da = h_ref[...], da_ref[...]
        g2, sc = g2_ref[...], sc_ref[...]
        r2 = lax.rsqrt(jnp.mean(h1 * h1, axis=-1, keepdims=True) + EPS)
        n2 = h1 * r2
        s_sh[...] += _colsum(da)
        s_sc[...] += _colsum(da * (n2 * g2))
        s_g2[...] += _colsum(da * (1.0 + sc) * n2)
        dn2 = da * g2 * (1.0 + sc)
        dh1 = dy_ref[...] + r2 * (dn2 - n2 * jnp.mean(dn2 * n2, axis=-1, keepdims=True))
        dh_ref[...] = dh1
        mo = mo_ref[...]
        gt, gp = gt_ref[...], gp_ref[...]
        r = lax.rsqrt(jnp.mean(mo * mo, axis=-1, keepdims=True) + EPS)
        mhat = mo * r
        s_gt[...] += _colsum(dh1 * (mhat * gp))
        dnm = dh1 * gt
        s_gp[...] += _colsum(dnm * mhat)
        u = dnm * gp
        dmo_ref[...] = (r * (u - mhat * jnp.mean(u * mhat, axis=-1, keepdims=True))).astype(BF16)

    row = pl.BlockSpec((tm, D), lambda i: (i, 0))
    vec_out = jax.ShapeDtypeStruct((1, D), F32)
    return _pallas(
        body, name="ffn_norm_bwd", grid=(T // tm,),
        out_shape=[jax.ShapeDtypeStruct((T, D), F32), jax.ShapeDtypeStruct((T, D), BF16)] + [vec_out] * 5,
        in_specs=[row, row, row, row] + [_vec(D)] * 4, out_specs=[row, row] + [_vec(D)] * 5,
        scratch=[], semantics=("arbitrary",), operands=(dy, da2, h1, mo, g_pre2, sc2, gt1, g_post), comm=comm)


def out_proj_bwd(dmo, w_o, y_a, y_b, proj):
    T, D = dmo.shape
    tm, tn = min(512, T), 512
    gpb = 1024 // tn

    def body(a_ref, w_ref, ya_ref, yb_ref, ga_ref, gb_ref, dya_ref, dyb_ref, dga_ref, dgb_ref):
        dm = _nt(a_ref[...], w_ref[...])
        sa, sb = jax.nn.sigmoid(ga_ref[...]), jax.nn.sigmoid(gb_ref[...])
        dya_ref[...] = (dm * sa).astype(BF16)
        dyb_ref[...] = (dm * sb).astype(BF16)
        dga_ref[...] = (dm * ya_ref[...].astype(F32) * sa * (1.0 - sa)).astype(BF16)
        dgb_ref[...] = (dm * yb_ref[...].astype(F32) * sb * (1.0 - sb)).astype(BF16)

    out = pl.BlockSpec((tm, tn), lambda i, j: (i, j))
    return pl.pallas_call(
        body, name="out_proj_bwd", grid=(T // tm, D // tn), out_shape=[jax.ShapeDtypeStruct((T, D), BF16)] * 4,
        in_specs=[pl.BlockSpec((tm, D), lambda i, j: (i, 0)), pl.BlockSpec((tn, D), lambda i, j: (j, 0)), out, out,
                  pl.BlockSpec((tm, tn), lambda i, j: (i, COL_GA * gpb + j)), pl.BlockSpec((tm, tn), lambda i, j: (i, COL_GB * gpb + j))],
        out_specs=[out] * 4, compiler_params=_cp("parallel", "parallel"),
    )(dmo, w_o, y_a, y_b, proj, proj)


def sgu_bwd(proj, dsgu, g_norm, w_spatial, b_spatial_t):
    T = proj.shape[0]
    W = 1024
    gw = W // SGU_GROUPS

    def body(u_ref, v_ref, ds_ref, g_ref, ws_ref, bst_ref, dz_ref, dw_ref, db_ref, dg_ref):
        @pl.when(pl.program_id(0) == 0)
        def _():
            dw_ref[...] = jnp.zeros_like(dw_ref)
            db_ref[...] = jnp.zeros_like(db_ref)
            dg_ref[...] = jnp.zeros_like(dg_ref)

        zu, rstd, dhat, vn, vm = _sgu_mix(u_ref, v_ref, g_ref, ws_ref, bst_ref)
        ds = ds_ref[...]
        du = ds * vm
        dvm = ds * zu
        dvm_b = dvm.astype(BF16)
        ones = jnp.ones((8, gw), F32)
        dvn = []
        for g in range(SGU_GROUPS):
            sl = slice(g * gw, (g + 1) * gw)
            dw_ref[g] += _nt(dvm_b[:, sl], vn[:, sl])
            db_ref[g] += lax.dot_general(ones, dvm[:, sl], (((1,), (1,)), ((), ())), precision=HI, preferred_element_type=F32)
            dvn.append(_tn(ws_ref[g].astype(BF16), dvm_b[:, sl]))
        dvn = jnp.concatenate(dvn, axis=1)
        dg_ref[...] += _colsum(dvn * dhat)
        ddh = dvn * g_ref[...]
        dzv = rstd * (ddh - jnp.mean(ddh, axis=-1, keepdims=True) - dhat * jnp.mean(ddh * dhat, axis=-1, keepdims=True))
        dz_ref[:, 0:W] = (du * _gelu_grad(u_ref[...])).astype(BF16)
        dz_ref[:, W:2 * W] = (dzv * _gelu_grad(v_ref[...])).astype(BF16)

    blk = lambda cb: pl.BlockSpec((SGU_CHUNK, W), lambda i: (i, cb))
    full3 = lambda a, b, c: pl.BlockSpec((a, b, c), lambda i: (0, 0, 0))
    return pl.pallas_call(
        body, name="sgu_bwd", grid=(T // SGU_CHUNK,),
        out_shape=[jax.ShapeDtypeStruct((T, 2 * W), BF16), jax.ShapeDtypeStruct((SGU_GROUPS, SGU_CHUNK, SGU_CHUNK), F32),
                   jax.ShapeDtypeStruct((SGU_GROUPS, 8, SGU_CHUNK), F32), jax.ShapeDtypeStruct((1, W), F32)],
        in_specs=[blk(COL_U), blk(COL_ZV), blk(0), _vec(W), full3(SGU_GROUPS, SGU_CHUNK, SGU_CHUNK),
                  pl.BlockSpec((SGU_CHUNK, SGU_GROUPS), lambda i: (0, 0))],
        out_specs=[pl.BlockSpec((SGU_CHUNK, 2 * W), lambda i: (i, 0)), full3(SGU_GROUPS, SGU_CHUNK, SGU_CHUNK),
                   full3(SGU_GROUPS, 8, SGU_CHUNK), _vec(W)],
        compiler_params=_cp("arbitrary"),
    )(proj, proj, dsgu, g_norm, w_spatial, b_spatial_t)


def hgrn_post_bwd(dya, o, proj, g_norm, comm):
    T, W = o.shape
    tm = min(256, T)

    def body(dy_ref, o_ref, og_ref, g_ref, do_ref, dog_ref, dg_ref):
        @pl.when(pl.program_id(0) == 0)
        def _():
            dg_ref[...] = jnp.zeros_like(dg_ref)

        g = g_ref[...]
        dg = jnp.zeros((1, HEAD_DIM), F32)
        for h in range(N_HEADS):
            sl = slice(h * HEAD_DIM, (h + 1) * HEAD_DIM)
            x, og, dy = o_ref[:, sl], og_ref[:, sl], dy_ref[:, sl]
            r = lax.rsqrt(jnp.mean(x * x, axis=-1, keepdims=True) + EPS)
            xhat = x * r
            s = jax.nn.sigmoid(og)
            don = dy * (og * s)
            dog_ref[:, sl] = (dy * (xhat * g) * (s * (1.0 + og * (1.0 - s)))).astype(BF16)
            dg += _colsum(don * xhat)
            u = don * g
            do_ref[:, sl] = r * (u - xhat * jnp.mean(u * xhat, axis=-1, keepdims=True))
        dg_ref[...] += dg

    row = pl.BlockSpec((tm, W), lambda i: (i, 0))
    return _pallas(
        body, name="hgrn_post_bwd", grid=(T // tm,),
        out_shape=[jax.ShapeDtypeStruct((T, W), F32), jax.ShapeDtypeStruct((T, W), BF16), jax.ShapeDtypeStruct((1, HEAD_DIM), F32)],
        in_specs=[row, row, pl.BlockSpec((tm, W), lambda i: (i, COL_OG)), _vec(HEAD_DIM)],
        out_specs=[row, row, _vec(HEAD_DIM)], scratch=[], semantics=("arbitrary",), operands=(dya, o, proj, g_norm), comm=comm)


def hgrn_bwd(proj, do, lb_logits, comm=None):
    T = proj.shape[0]
    NC, CPB = T // HGRN_CHUNK, HGRN_BLOCK // HGRN_CHUNK
    blk1 = min(HGRN_BLOCK_FWD, T)
    W = N_HEADS * HEAD_DIM
    col, f_spec, l_spec = _hgrn_specs(T)

    def body(l_ref, q_ref, f_ref, v_ref, do_ref, dq_ref, dv_ref, dlg_ref, dlb_ref, st_ref, dst_ref, dec_ref, ddec_ref, dqa_ref, dva_ref):
        d = pl.program_id(1)
        lb = _hgrn_lower_bound(l_ref)
        oml = 1.0 - lb
        mask = _hgrn_chunk_mask(d, HGRN_BLOCK)

        def values(rows):
            s, sn, fg, lf, k = _hgrn_gate(f_ref[rows, :], lb)
            b = _chunk_cumsum(lf, d == 1)
            bl = _chunk_total(lf)
            eb, enb, ee = jnp.exp(b), jnp.exp(-b), jnp.exp(bl - b)
            qd = q_ref[rows, :] * Q_SCALE * eb
            return s, sn, fg, k, bl, eb, enb, ee, qd, k * enb, k * ee

        def block1(i, carry):
            rows = pl.ds(pl.multiple_of(i * blk1, blk1), blk1)
            _, _, _, _, bl, _, _, _, qd, _, ke = values(rows)
            qd, ke = qd.astype(BF16), ke.astype(BF16)
            vb, dob = v_ref[rows, :].astype(BF16), do_ref[rows, :].astype(BF16)
            dec = jnp.exp(bl)
            for cc in range(blk1 // HGRN_CHUNK):
                sl = slice(cc * HGRN_CHUNK, (cc + 1) * HGRN_CHUNK)
                n = i * (blk1 // HGRN_CHUNK) + cc
                st_ref[n] = _tn(vb[sl], ke[sl])
                dst_ref[n] = _tn(dob[sl], qd[sl])
                dec_ref[n] = dec[cc * HGRN_CHUNK:cc * HGRN_CHUNK + 8, :]
            return carry

        _block_loop(T, blk1, block1, 0)

        def scan(t, s):
            n = jnp.where(d == 0, t, NC - 1 - t)
            u = st_ref[n]
            st_ref[n] = s
            return dec_ref[n][0:1, :] * s + u

        lax.fori_loop(0, NC, scan, jnp.zeros((HEAD_DIM, HEAD_DIM), F32))

        def rscan(t, ds):
            n = jnp.where(d == 0, NC - 1 - t, t)
            w = dst_ref[n]
            dst_ref[n] = ds
            ddec_ref[n] = jnp.broadcast_to(_colsum(ds * st_ref[n]), (8, HEAD_DIM))
            return dec_ref[n][0:1, :] * ds + w

        lax.fori_loop(0, NC, rscan, jnp.zeros((HEAD_DIM, HEAD_DIM), F32))

        def block3(i, dlb):
            rows = pl.ds(pl.multiple_of(i * HGRN_BLOCK, HGRN_BLOCK), HGRN_BLOCK)
            s, sn, fg, k, bl, eb, enb, ee, qd, kd, ke = values(rows)
            qdb, kdb, keb = qd.astype(BF16), kd.astype(BF16), ke.astype(BF16)
            vb, dob = v_ref[rows, :].astype(BF16), do_ref[rows, :].astype(BF16)
            att = jnp.where(mask, _nt(qdb, kdb), 0.0).astype(BF16)
            datt = jnp.where(mask, _nt(dob, vb), 0.0).astype(BF16)
            dv = _tn(att, dob)
            dqd = _nn(datt, kdb)
            dkd = _tn(datt, qdb)
            dv_i, dqd_i, dke, ddl = [], [], [], []
            for cc in range(CPB):
                sl = slice(cc * HGRN_CHUNK, (cc + 1) * HGRN_CHUNK)
                n = i * CPB + cc
                st_b, dst_b = st_ref[n].astype(BF16), dst_ref[n].astype(BF16)
                dv_i.append(_nt(keb[sl], dst_b))
                dqd_i.append(_nn(dob[sl], st_b))
                dke.append(_nn(vb[sl], dst_b))
                ddl.append(jnp.broadcast_to(ddec_ref[n][0:1, :] * dec_ref[n][0:1, :], (HGRN_CHUNK, HEAD_DIM)))
            dv = dv + jnp.concatenate(dv_i, axis=0)
            dqd = dqd + jnp.concatenate(dqd_i, axis=0)
            dke = jnp.concatenate(dke, axis=0)
            dq = dqd * eb * Q_SCALE
            dk = dkd * enb + dke * ee
            t_end = dke * ke
            db = dqd * qd - dkd * kd - t_end
            dlf = _chunk_cumsum(db, d == 0) + _chunk_total(t_end) + jnp.concatenate(ddl, axis=0)
            e = dlf / fg - dk
            dlg_ref[rows, :] = (oml * e * s * sn).astype(BF16)

            dq = jnp.where(d == 0, 0.0, dqa_ref[rows, :]) + dq
            dv = jnp.where(d == 0, 0.0, dva_ref[rows, :]) + dv
            dqa_ref[rows, :] = dq
            dva_ref[rows, :] = dv
            dq_ref[rows, :] = dq.astype(BF16)
            dv_ref[rows, :] = dv.astype(BF16)

            return dlb + _colsum(e * sn)

        dlb_ref[...] = _block_loop(T, HGRN_BLOCK, block3, jnp.zeros((1, HEAD_DIM), F32))

    head = pl.BlockSpec((T, HEAD_DIM), lambda h, d: (0, h))
    big = pltpu.VMEM((NC, HEAD_DIM, HEAD_DIM), F32)
    small = pltpu.VMEM((NC, 8, HEAD_DIM), F32)
    acc = pltpu.VMEM((T, HEAD_DIM), F32)
    outs, landed = _pallas(
        body, name="hgrn_bwd", grid=(N_HEADS, 2),
        out_shape=[jax.ShapeDtypeStruct((T, W), BF16), jax.ShapeDtypeStruct((T, W), BF16), jax.ShapeDtypeStruct((T, 2 * W), BF16),
                   jax.ShapeDtypeStruct((2, 1, W), F32)],
        in_specs=[l_spec, col(COL_Q), f_spec, col(COL_V), head],
        out_specs=[head, head, pl.BlockSpec((T, HEAD_DIM), lambda h, d: (0, N_HEADS * d + h)),
                   pl.BlockSpec((None, 1, HEAD_DIM), lambda h, d: (d, 0, h))],
        scratch=[big, big, small, small, acc, acc], semantics=("parallel", "arbitrary"), operands=(lb_logits, proj, proj, proj, do), comm=comm)
    return outs if comm is None else (outs, landed)


def mix_norm_bwd(da1, h0, dh1, g_pre, sc1):
    T, D = h0.shape
    tm = min(256, T)

    def body(da_ref, h_ref, dh_ref, g_ref, sc_ref, gx_ref, s_sh, s_sc, s_g):
        @pl.when(pl.program_id(0) == 0)
        def _():
            for s in (s_sh, s_sc, s_g):
                s[...] = jnp.zeros_like(s)

        h, da = h_ref[...], da_ref[...]
        g, sc = g_ref[...], sc_ref[...]
        r = lax.rsqrt(jnp.mean(h * h, axis=-1, keepdims=True) + EPS)
        n = h * r
        s_sh[...] += _colsum(da)
        s_sc[...] += _colsum(da * (n * g))
        s_g[...] += _colsum(da * (1.0 + sc) * n)
        dn = da * g * (1.0 + sc)
        gx_ref[...] = dh_ref[...] + r * (dn - n * jnp.mean(dn * n, axis=-1, keepdims=True))

    row = pl.BlockSpec((tm, D), lambda i: (i, 0))
    return pl.pallas_call(
        body, name="mix_norm_bwd", grid=(T // tm,),
        out_shape=[jax.ShapeDtypeStruct((T, D), F32)] + [jax.ShapeDtypeStruct((1, D), F32)] * 3,
        in_specs=[row, row, row, _vec(D), _vec(D)], out_specs=[row] + [_vec(D)] * 3, compiler_params=_cp("arbitrary"),
    )(da1, h0, dh1, g_pre, sc1)


def adamw(w, g, m, v, name):
    R, C = w.shape
    tr = R if R * C * 4 <= (1 << 21) else max(8, ((1 << 21) // (C * 4)) // 8 * 8)
    while R % tr:
        tr -= 8

    def body(w_ref, g_ref, m_ref, v_ref, d_ref, m2_ref, v2_ref):
        d_ref[...], m2_ref[...], v2_ref[...] = _adamw(w_ref[...], g_ref[...], m_ref[...], v_ref[...])

    row = pl.BlockSpec((tr, C), lambda i: (i, 0))
    return pl.pallas_call(
        body, name=name, grid=(R // tr,), out_shape=[jax.ShapeDtypeStruct((R, C), F32)] * 3,
        in_specs=[row] * 4, out_specs=[row] * 3, compiler_params=_cp("parallel"),
    )(w, g, m, v)


def wada_update(c_all, dmod, w, m, v):
    D, N = w.shape
    tm, tn = 512, 1024

    def body(c_ref, dm_ref, w_ref, m_ref, v_ref, g_ref, d_ref, m2_ref, v2_ref):
        c = c_ref[...]
        g = lax.dot_general(c * jax.nn.sigmoid(c), dm_ref[...], (((0,), (0,)), ((), ())), precision=HI, preferred_element_type=F32)
        g_ref[...] = g
        d_ref[...], m2_ref[...], v2_ref[...] = _adamw(w_ref[...], g, m_ref[...], v_ref[...])

    blk = pl.BlockSpec((tm, tn), lambda i, j: (i, j))
    return pl.pallas_call(
        body, name="wada_update", grid=(D // tm, N // tn), out_shape=[jax.ShapeDtypeStruct((D, N), F32)] * 4,
        in_specs=[pl.BlockSpec((8, tm), lambda i, j: (0, i)), pl.BlockSpec((8, tn), lambda i, j: (0, j)), blk, blk, blk],
        out_specs=[blk] * 4, compiler_params=_cp("parallel", "parallel"),
    )(c_all, dmod, w, m, v)


def sum_devices(gathered, name):
    n, R, C = gathered.shape

    def body(g_ref, o_ref):
        s = g_ref[0]
        for i in range(1, n):
            s = s + g_ref[i]
        o_ref[...] = s

    return pl.pallas_call(body, name=name, out_shape=jax.ShapeDtypeStruct((R, C), F32), compiler_params=_cp())(gathered)


def lb_logits_grad(dlb, lb_logits):
    def body(d_ref, l_ref, o_ref):
        for d in range(2):
            l0, l1 = l_ref[d, 0:1, :], l_ref[d, 1:2, :]
            m = jnp.maximum(l0, l1)
            e0, e1 = jnp.exp(l0 - m), jnp.exp(l1 - m)
            p0, p1 = e0 / (e0 + e1), e1 / (e0 + e1)
            g = d_ref[d:d + 1, :]
            o_ref[d, 0:1, :] = p0 * (g - p0 * g)
            o_ref[d, 1:2, :] = -p1 * (p0 * g)

    return pl.pallas_call(body, name="lb_logits_grad", out_shape=jax.ShapeDtypeStruct(lb_logits.shape, F32), compiler_params=_cp())(dlb, lb_logits)


def add_halves(g, landed, core):
    nj, _, r, cc = g.shape
    tr = min(256, r)

    def body(core_ref, g_ref, l_ref, o_ref):
        o_ref[...] = (g_ref[...].astype(F32) + l_ref[...].astype(F32)).astype(BF16)

    return pl.pallas_call(
        body, name="add_halves_%dx%d" % (r, cc), out_shape=jax.ShapeDtypeStruct((nj, r, cc), BF16),
        grid_spec=pltpu.PrefetchScalarGridSpec(
            num_scalar_prefetch=1, grid=(nj, r // tr),
            in_specs=[pl.BlockSpec((None, None, tr, cc), lambda j, i, core_ref: (j, core_ref[0], i, 0)),
                      pl.BlockSpec((None, None, tr, cc), lambda j, i, core_ref: (j, 0, i, 0))],
            out_specs=pl.BlockSpec((None, tr, cc), lambda j, i, core_ref: (j, i, 0))),
        compiler_params=_cp("parallel", "parallel"),
    )(core, g, landed)


def sum_chips(parts, landed, chip):
    nj, r, cc = parts.shape
    tr = min(256, r)

    def body(chip_ref, p_ref, l_ref, o_ref):
        mine = p_ref[...].astype(F32)
        s = None
        for j in range(nj):
            t = jnp.where(chip_ref[0] == j, mine, l_ref[j].astype(F32))
            s = t if s is None else s + t
        o_ref[...] = s

    return pl.pallas_call(
        body, name="sum_chips_%dx%d" % (r, cc), out_shape=jax.ShapeDtypeStruct((r, cc), F32),
        grid_spec=pltpu.PrefetchScalarGridSpec(
            num_scalar_prefetch=1, grid=(r // tr,),
            in_specs=[pl.BlockSpec((None, tr, cc), lambda i, chip_ref: (chip_ref[0], i, 0)),
                      pl.BlockSpec((nj, tr, cc), lambda i, chip_ref: (0, i, 0))],
            out_specs=pl.BlockSpec((tr, cc), lambda i, chip_ref: (i, 0))),
        compiler_params=_cp("parallel"),
    )(chip, parts, landed)


def adamw_halves(w, own, other, m, v, core, name):
    r, cc = own.shape
    tr = min(128, r)
    nb = r // tr

    def body(core_ref, w_ref, a_ref, b_ref, m_ref, v_ref, g_ref, d_ref, m2_ref, v2_ref):
        g = jnp.where(pl.program_id(0) == core_ref[0], a_ref[...], b_ref[...])
        g_ref[...] = g
        d_ref[...], m2_ref[...], v2_ref[...] = _adamw(w_ref[...], g, m_ref[...], v_ref[...])

    full = pl.BlockSpec((tr, cc), lambda h, i, core_ref: (h * nb + i, 0))
    mine = pl.BlockSpec((tr, cc), lambda h, i, core_ref: (jnp.where(h == core_ref[0], i, 0), 0))
    theirs = pl.BlockSpec((tr, cc), lambda h, i, core_ref: (jnp.where(h == core_ref[0], 0, i), 0))
    return pl.pallas_call(
        body, name=name, out_shape=[jax.ShapeDtypeStruct((2 * r, cc), F32)] * 4,
        grid_spec=pltpu.PrefetchScalarGridSpec(
            num_scalar_prefetch=1, grid=(2, nb), in_specs=[full, mine, theirs, full, full], out_specs=[full] * 4),
        compiler_params=_cp("arbitrary", "arbitrary"),
    )(core, w, own, other, m, v)


def _place():
    mx, my, mc = lax.axis_index("x"), lax.axis_index("y"), lax.axis_index("c")
    chips = [(1 - mx, my), (mx, 1 - my), (1 - mx, 1 - my)]
    return mx, my, mc, chips


def all_gather_small(x, name):
    R, C = x.shape

    def body(x_ref, out_ref, send_sems, recv_sems, local_sem):
        mx, my, mc, _ = _place()
        me = 4 * mx + 2 * my + mc
        mine = pltpu.make_async_copy(x_ref, out_ref.at[me], local_sem)
        mine.start()

        def peer(k):
            px = 1 - mx if k & 4 else mx
            py = 1 - my if k & 2 else my
            pc = 1 - mc if k & 1 else mc
            return px, py, pc

        def copy(k, src, slot):
            return pltpu.make_async_remote_copy(src_ref=src, dst_ref=out_ref.at[slot], send_sem=send_sems.at[k - 1],
                                                recv_sem=recv_sems.at[k - 1], device_id=peer(k), device_id_type=MESH)

        sends = [copy(k, x_ref, me) for k in range(1, 8)]
        for cp in sends:
            cp.start()
        for k in range(1, 8):
            px, py, pc = peer(k)
            slot = 4 * px + 2 * py + pc
            copy(k, out_ref.at[slot], slot).wait_recv()
        for cp in sends:
            cp.wait_send()
        mine.wait()

    return pl.pallas_call(
        body, name=name, out_shape=jax.ShapeDtypeStruct((8, R, C), F32),
        in_specs=[pl.BlockSpec(memory_space=pltpu.VMEM)], out_specs=pl.BlockSpec(memory_space=pltpu.VMEM),
        scratch_shapes=[pltpu.SemaphoreType.DMA((7,)), pltpu.SemaphoreType.DMA((7,)), pltpu.SemaphoreType.DMA],
        compiler_params=_cp(),
    )(x)


def gather8_comm(x):
    def copies(x_ref, out_ref, send_sems, recv_sems):
        mx, my, mc, _ = _place()
        me = 4 * mx + 2 * my + mc

        def peer(k):
            return (1 - mx if k & 4 else mx, 1 - my if k & 2 else my, 1 - mc if k & 1 else mc)

        def copy(k, src, slot):
            return pltpu.make_async_remote_copy(src_ref=src, dst_ref=out_ref.at[slot], send_sem=send_sems.at[k - 1],
                                                recv_sem=recv_sems.at[k - 1], device_id=peer(k), device_id_type=MESH)

        sends = [copy(k, x_ref, me) for k in range(1, 8)]
        arrivals = []
        for k in range(1, 8):
            px, py, pc = peer(k)
            slot = 4 * px + 2 * py + pc
            arrivals.append(copy(k, out_ref.at[slot], slot))
        return sends, arrivals, pltpu.make_async_copy(x_ref, out_ref.at[me], send_sems.at[7])

    def start(cin, cout, send_sems, recv_sems):
        sends, _, mine = copies(cin[0], cout[0], send_sems, recv_sems)
        mine.start()
        for cp in sends:
            cp.start()

    def finish(cin, cout, send_sems, recv_sems):
        sends, arrivals, mine = copies(cin[0], cout[0], send_sems, recv_sems)
        for cp in arrivals:
            cp.wait_recv()
        for cp in sends:
            cp.wait_send()
        mine.wait()

    return _Comm([x], [jax.ShapeDtypeStruct((8,) + x.shape, F32)], {}, 8, start, finish)


def _join(a, b):
    na_in, na_out = len(a.operands), len(a.out_shape)

    def split(fn_a, fn_b):
        def both(cin, cout, send_sems, recv_sems):
            fn_a(cin[:na_in], cout[:na_out], send_sems.at[pl.ds(0, a.n_sems)], recv_sems.at[pl.ds(0, a.n_sems)])
            fn_b(cin[na_in:], cout[na_out:], send_sems.at[pl.ds(a.n_sems, b.n_sems)], recv_sems.at[pl.ds(a.n_sems, b.n_sems)])
        return both

    aliases = dict(a.aliases)
    aliases.update({na_in + i: na_out + o for i, o in b.aliases.items()})
    return _Comm(a.operands + b.operands, a.out_shape + b.out_shape, aliases, a.n_sems + b.n_sems, split(a.start, b.start), split(a.finish, b.finish))


def _region(ref, kind, j, half, r, cc):
    nr = r if half is None else r // 2
    off = 0 if half is None else half * nr
    if kind == "col":
        return ref.at[pl.ds(off, nr), pl.ds(pl.multiple_of(j * cc, 128), cc)]
    return ref.at[pl.ds(pl.multiple_of(j * r + off, 16), nr), :]


def comm_call(comm, name):
    ni, no = len(comm.operands), len(comm.out_shape)

    def body(*refs):
        comm.start(refs[:ni], refs[ni:ni + no], *refs[ni + no:])
        comm.finish(refs[:ni], refs[ni:ni + no], *refs[ni + no:])

    return pl.pallas_call(
        body, name=name, out_shape=comm.out_shape, in_specs=[ANY] * ni, out_specs=[ANY] * no, input_output_aliases=comm.aliases,
        scratch_shapes=[pltpu.SemaphoreType.DMA((comm.n_sems,)), pltpu.SemaphoreType.DMA((comm.n_sems,))], compiler_params=_cp(),
    )(*comm.operands)


def gather_comm(fulls, kinds, dims):
    n = len(fulls)

    def copies(f_refs, send_sems, recv_sems):
        mx, my, mc, chips = _place()
        jme = 2 * mx + my

        def landed(w, k, half):
            px, py = chips[k]
            return _region(f_refs[w], kinds[w], 2 * px + py, half, *dims[w])

        def over_ici(w, k, reg):
            px, py = chips[k]
            return pltpu.make_async_remote_copy(src_ref=reg, dst_ref=reg, send_sem=send_sems.at[6 * w + k], recv_sem=recv_sems.at[6 * w + k],
                                                device_id=(px, py, mc), device_id_type=MESH)

        def over_d2d(w, k, half):
            reg = landed(w, k, half)
            return pltpu.make_async_remote_copy(src_ref=reg, dst_ref=reg, send_sem=send_sems.at[6 * w + 3 + k],
                                                recv_sem=recv_sems.at[6 * w + 3 + k], device_id=(mx, my, 1 - mc), device_id_type=MESH)

        sends = [over_ici(w, k, _region(f_refs[w], kinds[w], jme, mc, *dims[w])) for w in range(n) for k in range(3)]
        return mc, landed, over_ici, over_d2d, sends

    def start(cin, f_refs, send_sems, recv_sems):
        for cp in copies(f_refs, send_sems, recv_sems)[4]:
            cp.start()

    def finish(cin, f_refs, send_sems, recv_sems):
        mc, landed, over_ici, over_d2d, sends = copies(f_refs, send_sems, recv_sems)
        passed = []
        for w in range(n):
            for k in range(3):
                over_ici(w, k, landed(w, k, mc)).wait_recv()
                cp = over_d2d(w, k, mc)
                cp.start()
                passed.append(cp)
        for w in range(n):
            for k in range(3):
                over_d2d(w, k, 1 - mc).wait_recv()
        for cp in sends + passed:
            cp.wait_send()

    return _Comm(fulls, [jax.ShapeDtypeStruct(f.shape, BF16) for f in fulls], {w: w for w in range(n)}, 6 * n, start, finish)


def exchange_comm(grads):
    n = len(grads)

    def copies(g_refs, l_refs, send_sems, recv_sems):
        mx, my, mc, _ = _place()
        return [pltpu.make_async_remote_copy(src_ref=g_refs[w].at[:, pl.ds(1 - mc, 1)], dst_ref=l_refs[w], send_sem=send_sems.at[w],
                                             recv_sem=recv_sems.at[w], device_id=(mx, my, 1 - mc), device_id_type=MESH) for w in range(n)]

    def start(*refs):
        for cp in copies(*refs):
            cp.start()

    def finish(*refs):
        for cp in copies(*refs):
            cp.wait()

    return _Comm(grads, [jax.ShapeDtypeStruct((g.shape[0], 1) + g.shape[2:], BF16) for g in grads], {}, n, start, finish)


def exchange_halves(grads, name):
    return comm_call(exchange_comm(grads), name)


def scatter_comm(parts):
    n = len(parts)

    def sends(p_refs, l_refs, send_sems, recv_sems):
        mx, my, mc, chips = _place()
        return [pltpu.make_async_remote_copy(src_ref=p_refs[w].at[2 * px + py], dst_ref=l_refs[w].at[2 * mx + my],
                                             send_sem=send_sems.at[3 * w + k], recv_sem=recv_sems.at[3 * w + k],
                                             device_id=(px, py, mc), device_id_type=MESH) for w in range(n) for k, (px, py) in enumerate(chips)]

    def start(p_refs, l_refs, send_sems, recv_sems):
        for cp in sends(p_refs, l_refs, send_sems, recv_sems):
            cp.start()

    def finish(p_refs, l_refs, send_sems, recv_sems):
        mx, my, mc, chips = _place()
        for w in range(n):
            for k, (px, py) in enumerate(chips):
                slot = l_refs[w].at[2 * px + py]
                pltpu.make_async_remote_copy(src_ref=slot, dst_ref=slot, send_sem=send_sems.at[3 * w + k], recv_sem=recv_sems.at[3 * w + k],
                                             device_id=(px, py, mc), device_id_type=MESH).wait_recv()
        for cp in sends(p_refs, l_refs, send_sems, recv_sems):
            cp.wait_send()

    return _Comm(parts, [jax.ShapeDtypeStruct(p.shape, BF16) for p in parts], {}, 3 * n, start, finish)


def share_comm(sums):
    n = len(sums)

    def copies(q_refs, o_refs, send_sems, recv_sems):
        mx, my, mc, _ = _place()
        return [pltpu.make_async_remote_copy(src_ref=q_refs[w], dst_ref=o_refs[w], send_sem=send_sems.at[w], recv_sem=recv_sems.at[w],
                                             device_id=(mx, my, 1 - mc), device_id_type=MESH) for w in range(n)]

    def start(*refs):
        for cp in copies(*refs):
            cp.start()

    def finish(*refs):
        for cp in copies(*refs):
            cp.wait()

    return _Comm(sums, [jax.ShapeDtypeStruct(q.shape, F32) for q in sums], {}, n, start, finish)


def _pack(arrays):
    flat = jnp.concatenate([a.reshape(-1) for a in arrays])
    rows = -(-flat.shape[0] // 1024) * 8
    return jnp.pad(flat, (0, rows * 128 - flat.shape[0])).reshape(rows, 128)


def _unpack(packed, shapes):
    flat, out, off = packed.reshape(-1), [], 0
    for s in shapes:
        n = math.prod(s)
        out.append(flat[off:off + n].reshape(s))
        off += n
    return out


def kernel(x, c, w_ada, b_ada, g_pre_mix, g_post_mix, g_pre_ffn, g_post_ffn, w_in, lb_logits, g_hgrn_norm, w_a_out, g_sgu_norm, w_spatial, b_spatial, w_b_out, w_o, w_ff1, w_ff2, loss_target, m_w_ada, m_b_ada, m_g_pre_mix, m_g_post_mix, m_g_pre_ffn, m_g_post_ffn, m_w_in, m_lb_logits, m_g_hgrn_norm, m_w_a_out, m_g_sgu_norm, m_w_spatial, m_b_spatial, m_w_b_out, m_w_o, m_w_ff1, m_w_ff2, v_w_ada, v_b_ada, v_g_pre_mix, v_g_post_mix, v_g_pre_ffn, v_g_post_ffn, v_w_in, v_lb_logits, v_g_hgrn_norm, v_w_a_out, v_g_sgu_norm, v_w_spatial, v_b_spatial, v_w_b_out, v_w_o, v_w_ff1, v_w_ff2):
    mx, my, mc = lax.axis_index("x"), lax.axis_index("y"), lax.axis_index("c")
    chip, me = 2 * mx + my, 4 * mx + 2 * my + mc
    D = D_MODEL
    h0, tgt = x[0], loss_target[0]
    n_ada = w_ada.shape[2]
    n_lb = lb_logits.shape[2]

    got = all_gather_small(_pack([c, lb_logits]), "gather_inputs")
    c_all = got[:, :D // 128, :].reshape(8, D)
    lb_full = got[0::2, D // 128:D // 128 + 4 * n_lb // 128, :].reshape(4, 2, 2, n_lb).transpose(1, 2, 0, 3).reshape(2, 2, 4 * n_lb)
    b_ada_chip = lax.dynamic_slice(b_ada, (0, chip * n_ada), (1, n_ada))
    mod_cols = mod_matmul(c_all, w_ada[0], b_ada_chip)
    got = all_gather_small(mod_cols.reshape(-1, 128), "gather_mod").reshape(4, 2, 8, n_ada)
    mod = lax.dynamic_index_in_dim(got[:, 0], me, axis=1, keepdims=False).reshape(6, 1, D)
    sh1, sc1, gt1, sh2, sc2, gt2 = (mod[i] for i in range(6))

    big = [("w_in", w_in, "col"), ("w_a_out", w_a_out, "col"), ("w_b_out", w_b_out, "col"), ("w_o", w_o, "row"),
           ("w_ff1", w_ff1, "col"), ("w_ff2", w_ff2, "row")]
    kinds = [k for _, _, k in big]
    chip_idx, core = chip.reshape(1).astype(jnp.int32), mc.reshape(1).astype(jnp.int32)
    fulls = [cast_into_full(w[0], kind, chip_idx, "cast_" + nm) for nm, w, kind in big]
    dims = [w.shape[1:] for _, w, _ in big]
    later = lambda lo, hi: gather_comm(fulls[lo:hi], kinds[lo:hi], dims[lo:hi])
    halves_summed = lambda grads, name: [add_halves(g, l, core) for g, l in zip(grads, exchange_halves(grads, name))]

    bst = b_spatial[0].T
    a1 = prenorm(h0, g_pre_mix, sc1, sh1)
    proj, w_in_f, (w_a_f, w_b_f, w_o_f) = in_proj_gathered(a1, fulls[0], chip_idx, dims[0], later(1, 4))
    o, (w_ff1_f,) = hgrn_fwd(proj, lb_full, comm=later(4, 5))
    ya_pre = hgrn_post_fwd(o, proj, g_hgrn_norm)
    sgu = sgu_fwd(proj, g_sgu_norm, w_spatial[0], bst)
    y_a, y_b, merged = merge_matmul(ya_pre, sgu, w_a_f, w_b_f, proj)
    mo, h1, a2 = out_proj(merged, w_o_f, h0, gt1, g_post_mix, g_pre_ffn, sc2, sh2)
    (f1, hid), (w_ff2_f,) = matmul(a2, w_ff1_f, mode="nn", out_dtype=BF16, tm=1024, tn=2048, tk=2048, name="ff1", relu2=True,
                                   comm=later(5, 6))
    ff = matmul(hid, w_ff2_f, mode="nn", out_dtype=F32, tm=1024, tn=1024, tk=2048, name="ff2")
    dy, dff, loss_parts, d_gt2, d_g_post_ffn = loss_bwd(ff, h1, tgt, gt2, g_post_ffn)

    df1 = ff2_bwd(dff, w_ff2_f, f1)
    gr_ff2 = matmul(hid, dff, mode="tn", out_dtype=BF16, tm=1024, tn=1024, tk=2048, name="dw_ff2")
    gr_ff2 = gr_ff2.reshape(4, 2, -1, D)
    da2, (landed_ff2,) = matmul(df1, w_ff1_f, mode="nt", out_dtype=F32, tm=1024, tn=1024, tk=2048, name="da2", comm=exchange_comm([gr_ff2]))
    gr_ff1 = matmul(a2, df1, mode="tn", out_dtype=BF16, tm=1024, tn=2048, tk=2048, name="dw_ff1", split=(4, 2))
    (dh1, dmo, d_sh2, d_sc2, d_g_pre_ffn, d_gt1, d_g_post_mix), (landed_ff1,) = ffn_norm_bwd(
        dy, da2, h1, mo, g_pre_ffn, sc2, gt1, g_post_mix, exchange_comm([gr_ff1]))
    parts_ff = [add_halves(gr_ff1, landed_ff1, core), add_halves(gr_ff2, landed_ff2, core)]
    dya, dyb, dga, dgb = out_proj_bwd(dmo, w_o_f, y_a, y_b, proj)
    gr_o = matmul(merged, dmo, mode="tn", out_dtype=BF16, tm=1024, tn=1024, tk=2048, name="dw_o")
    dsgu = matmul(dyb, w_b_f, mode="nt", out_dtype=F32, tm=512, tn=1024, tk=2048, name="dsgu")
    gr_b = matmul(sgu, dyb, mode="tn", out_dtype=BF16, tm=512, tn=512, tk=4096, name="dw_b_out", split=(4, 2))
    dz, d_w_spatial, d_b_spatial, d_g_sgu = sgu_bwd(proj, dsgu, g_sgu_norm, w_spatial[0], bst)
    dya_pre = matmul(dya, w_a_f, mode="nt", out_dtype=F32, tm=512, tn=1024, tk=2048, name="dya_pre")
    gr_a = matmul(ya_pre, dya, mode="tn", out_dtype=BF16, tm=512, tn=512, tk=4096, name="dw_a_out", split=(4, 2))
    gr_mix = [gr_a, gr_b, gr_o.reshape(4, 2, -1, D)]
    (do, dog, d_g_hgrn), landed_halves = hgrn_post_bwd(dya_pre, o, proj, g_hgrn_norm, exchange_comm(gr_mix))
    parts_mix = [add_halves(g, l, core) for g, l in zip(gr_mix, landed_halves)]
    chips_summed = lambda parts, landed: [sum_chips(p, l, chip_idx) for p, l in zip(parts, landed)]
    (dq, dv, dlg, d_lb), landed_ff = hgrn_bwd(proj, do, lb_full, comm=scatter_comm(parts_ff))
    own_ff = chips_summed(parts_ff, landed_ff)
    dproj = jnp.concatenate([dq, dlg, dv, dog, dz, dga, dgb], axis=1)
    early = _pack([d_g_sgu, d_w_spatial, d_b_spatial[:, 0, :]])
    gr_in, (*landed_mix, got_early) = matmul(a1, dproj, mode="tn", out_dtype=BF16, tm=1024, tn=2816, tk=1024, name="dw_in", split=(4, 2),
                                             comm=_join(scatter_comm(parts_mix), gather8_comm(early)))
    own_mix = chips_summed(parts_mix, landed_mix)
    parts_in = halves_summed([gr_in], "exchange_in")
    da1, (landed_in, *other_rest) = matmul(dproj, w_in_f, mode="nt", out_dtype=F32, tm=1024, tn=1024, tk=2816, name="da1",
                                           comm=_join(scatter_comm(parts_in), share_comm(own_mix + own_ff)))
    own_in = chips_summed(parts_in, [landed_in])
    other_in = comm_call(share_comm(own_in), "share_w_in")
    own, other = own_in + own_mix + own_ff, list(other_in) + other_rest
    grad_x, d_sh1, d_sc1, d_g_pre_mix = mix_norm_bwd(da1, h0, dh1, g_pre_mix, sc1)
    out = {}

    mine = _pack([d_sh1, d_sc1, d_gt1, d_sh2, d_sc2, d_gt2, d_g_pre_mix, d_g_post_mix, d_g_pre_ffn, d_g_post_ffn, d_g_hgrn, d_lb,
                  loss_parts[0:1, 0:1]])
    got = all_gather_small(mine, "gather_small_grads")
    g_b_ada, g_g1, g_g2, g_g3, g_g4, g_hg, g_lb, sq_err = _unpack(
        sum_devices(got, "sum_small_grads"), [(1, 6 * D), (1, D), (1, D), (1, D), (1, D), (1, HEAD_DIM), (2, 1024), ()])
    loss = 0.5 * sq_err / D
    g_sg, g_ws, g_bs = _unpack(sum_devices(got_early, "sum_sgu_grads"), [(1, 1024), w_spatial.shape, b_spatial.shape])
    g_lbl = lax.dynamic_slice(lb_logits_grad(g_lb, lb_full), (0, 0, chip * n_lb), (2, 2, n_lb))
    names = ["b_ada", "g_pre_mix", "g_post_mix", "g_pre_ffn", "g_post_ffn", "g_hgrn_norm", "g_sgu_norm", "w_spatial", "b_spatial", "lb_logits"]
    ws = [b_ada, g_pre_mix, g_post_mix, g_pre_ffn, g_post_ffn, g_hgrn_norm, g_sgu_norm, w_spatial, b_spatial, lb_logits]
    gs = [g_b_ada, g_g1, g_g2, g_g3, g_g4, g_hg, g_sg, g_ws, g_bs, g_lbl]
    ms = [m_b_ada, m_g_pre_mix, m_g_post_mix, m_g_pre_ffn, m_g_post_ffn, m_g_hgrn_norm, m_g_sgu_norm, m_w_spatial, m_b_spatial, m_lb_logits]
    vs = [v_b_ada, v_g_pre_mix, v_g_post_mix, v_g_pre_ffn, v_g_post_ffn, v_g_hgrn_norm, v_g_sgu_norm, v_w_spatial, v_b_spatial, v_lb_logits]
    shapes = [w.shape for w in ws]
    upd = adamw(_pack(ws), _pack(gs), _pack(ms), _pack(vs), "adamw_small")
    upd = [_unpack(u, shapes) for u in upd]
    for i, nm in enumerate(names):
        out[nm] = (gs[i], upd[0][i], upd[1][i], upd[2][i])

    dmod_all = got[:, :6 * D // 128, :].reshape(8, 6 * D)
    dmod_chip = lax.dynamic_slice(dmod_all, (0, chip * n_ada), (8, n_ada))
    out["w_ada"] = tuple(a[None] for a in wada_update(c_all, dmod_chip, w_ada[0], m_w_ada[0], v_w_ada[0]))
    for (nm, w, _), a, b, m, v in zip(big, own, other, (m_w_in, m_w_a_out, m_w_b_out, m_w_o, m_w_ff1, m_w_ff2),
                                      (v_w_in, v_w_a_out, v_w_b_out, v_w_o, v_w_ff1, v_w_ff2)):
        out[nm] = tuple(t[None] for t in adamw_halves(w[0], a, b, m[0], v[0], core, "adamw_" + nm))

    order = ["w_ada", "b_ada", "g_pre_mix", "g_post_mix", "g_pre_ffn", "g_post_ffn", "w_in", "lb_logits", "g_hgrn_norm", "w_a_out",
             "g_sgu_norm", "w_spatial", "b_spatial", "w_b_out", "w_o", "w_ff1", "w_ff2"]
    return (loss, grad_x[None], *[out[nm][0] for nm in order], *[out[nm][1] for nm in order], *[out[nm][2] for nm in order],
            *[out[nm][3] for nm in order])
```
